```python
import math
import jax, jax.numpy as jnp
from jax import lax
import numpy as np

D_MODEL = 1024
BATCH = 8
SEQ = 2048
DEPTH = 2

N_MIXERS = 2
N_SSD_LAYERS = (DEPTH + 1) // 2
N_CONV_LAYERS = DEPTH // 2
NORM_EPS = 1e-5
ADA_MODS = 6

SSD_EXPAND = 2
SSD_D_INNER = SSD_EXPAND * D_MODEL
SSD_HEAD_DIM = 64
SSD_N_HEADS = SSD_D_INNER // SSD_HEAD_DIM
SSD_N_GROUPS = 4
SSD_HEADS_PER_GROUP = SSD_N_HEADS // SSD_N_GROUPS
SSD_D_STATE = 128
SSD_CONV_K = 4
SSD_CHUNK = 128
SSD_CONV_DIM = SSD_D_INNER + 2 * SSD_N_GROUPS * SSD_D_STATE
SSD_IN_DIM = SSD_D_INNER + SSD_CONV_DIM + SSD_N_HEADS
DT_MIN = 1e-3
DT_MAX = 1e-1

SC_WIDTH = D_MODEL
SC_CONV_K = 3

D_FF = 4 * D_MODEL

kernel_name = "hybrid_ssd_shortconv_adaln_trunk"


def rmsnorm(x, g, eps=NORM_EPS):
    xf = x.astype(jnp.float32)
    y = xf * lax.rsqrt(jnp.mean(xf * xf, axis=-1, keepdims=True) + eps)
    return (y * g.astype(jnp.float32)).astype(x.dtype)


def causal_dwconv(x, w, b=None):
    k, ch = w.shape
    out = lax.conv_general_dilated(
        x, w[:, None, :].astype(x.dtype), window_strides=(1,), padding=[(k - 1, 0)],
        dimension_numbers=("NWC", "WIO", "NWC"), feature_group_count=ch)
    if b is not None:
        out = out + b.astype(x.dtype)
    return out


def ssd_chunked(xs, dt, A, Bm, Cm):
    b, L, g, r, p = xs.shape
    n = Bm.shape[-1]
    nc = L // SSD_CHUNK
    xs = xs.astype(jnp.float32).reshape(b, nc, SSD_CHUNK, g, r, p)
    dt = dt.reshape(b, nc, SSD_CHUNK, g, r)
    Bc = Bm.astype(jnp.float32).reshape(b, nc, SSD_CHUNK, g, n)
    Cc = Cm.astype(jnp.float32).reshape(b, nc, SSD_CHUNK, g, n)
    X = xs * dt[..., None]
    Acs = jnp.cumsum(dt * A, axis=2)

    causal = jnp.tril(jnp.ones((SSD_CHUNK, SSD_CHUNK), dtype=bool))[:, :, None, None]
    seg = Acs[:, :, :, None] - Acs[:, :, None, :]
    Lmat = jnp.exp(jnp.where(causal, seg, -jnp.inf))
    scores = jnp.einsum("bclgn,bcsgn->bclsg", Cc, Bc)
    M = scores[..., None] * Lmat
    y_diag = jnp.einsum("bclsgr,bcsgrp->bclgrp", M, X)

    decay_states = jnp.exp(Acs[:, :, -1:] - Acs)
    states = jnp.einsum("bcsgn,bcsgrp->bcgrpn", Bc, X * decay_states[..., None])

    chunk_decay = jnp.exp(Acs[:, :, -1])

    def step(carry, inp):
        st, dec = inp
        return carry * dec[..., None, None] + st, carry

    init = jnp.zeros((b, g, r, p, n), dtype=states.dtype)
    _, prev = lax.scan(step, init, (jnp.moveaxis(states, 1, 0), jnp.moveaxis(chunk_decay, 1, 0)))
    prev = jnp.moveaxis(prev, 0, 1)

    y_off = jnp.einsum("bclgn,bcgrpn->bclgrp", Cc, prev) * jnp.exp(Acs)[..., None]
    return (y_diag + y_off).reshape(b, L, g, r, p)


def ssd_mixer(h, in_w, conv_w, conv_b, dt_bias, A_log, D_skip, norm_w, out_w):
    b, L, _ = h.shape
    G, R, P, N = SSD_N_GROUPS, SSD_HEADS_PER_GROUP, SSD_HEAD_DIM, SSD_D_STATE
    zxbcdt = h @ in_w
    z = zxbcdt[..., :SSD_D_INNER]
    xBC = zxbcdt[..., SSD_D_INNER:SSD_D_INNER + SSD_CONV_DIM]
    dt_raw = zxbcdt[..., SSD_D_INNER + SSD_CONV_DIM:]
    xBC = jax.nn.silu(causal_dwconv(xBC, conv_w, conv_b))
    xs = xBC[..., :SSD_D_INNER].reshape(b, L, G, R, P)
    Bm = xBC[..., SSD_D_INNER:SSD_D_INNER + G * N].reshape(b, L, G, N)
    Cm = xBC[..., SSD_D_INNER + G * N:].reshape(b, L, G, N)
    dt = jax.nn.softplus(dt_raw.astype(jnp.float32) + dt_bias.astype(jnp.float32)).reshape(b, L, G, R)
    A = -jnp.exp(A_log.astype(jnp.float32)).reshape(G, R)
    y = ssd_chunked(xs, dt, A, Bm, Cm)
    y = y + D_skip.astype(jnp.float32).reshape(G, R, 1) * xs.astype(jnp.float32)
    yg = y.reshape(b, L, SSD_D_INNER) * jax.nn.silu(z.astype(jnp.float32))
    yg = yg.reshape(b, L, G, SSD_D_INNER // G)
    yg = yg * lax.rsqrt(jnp.mean(yg * yg, axis=-1, keepdims=True) + NORM_EPS)
    yg = yg.reshape(b, L, SSD_D_INNER) * norm_w.astype(jnp.float32)
    return yg.astype(h.dtype) @ out_w


def short_conv_mixer(h, in_w, conv_w, out_w):
    proj = h @ in_w
    Bg, Cg, xv = jnp.split(proj, 3, axis=-1)
    y = Bg * causal_dwconv(Cg * xv, conv_w)
    return y @ out_w


def sqrelu_mlp(h, up_w, down_w):
    a = jax.nn.relu(h @ up_w)
    return (a * a) @ down_w


def _fwd_setup_inputs(seed: int = 0) -> dict:
    key = jax.random.key(seed)
    ks = jax.random.split(key, 24)
    f32 = jnp.float32
    D = D_MODEL
    nrm = lambda k, shape, s: jax.random.normal(k, shape, f32) * s
    x = jax.random.normal(ks[0], (BATCH, SEQ, D), f32)
    c = jax.random.normal(ks[1], (BATCH, D), f32)
    ada_w = nrm(ks[2], (DEPTH, D, ADA_MODS * D), 0.5 * D ** -0.5)
    ada_b = nrm(ks[3], (DEPTH, ADA_MODS * D), 0.02)
    mix_norm_w = 1.0 + nrm(ks[4], (DEPTH, D), 0.02)
    mlp_norm_w = 1.0 + nrm(ks[5], (DEPTH, D), 0.02)
    mlp_up = nrm(ks[6], (DEPTH, D, D_FF), D ** -0.5)
    mlp_down = nrm(ks[7], (DEPTH, D_FF, D), D_FF ** -0.5)
    ssd_in_w = nrm(ks[8], (N_SSD_LAYERS, D, SSD_IN_DIM), D ** -0.5)
    ssd_conv_w = nrm(ks[9], (N_SSD_LAYERS, SSD_CONV_K, SSD_CONV_DIM), SSD_CONV_K ** -0.5)
    ssd_conv_b = nrm(ks[10], (N_SSD_LAYERS, SSD_CONV_DIM), 0.02)
    u = jax.random.uniform(ks[11], (N_SSD_LAYERS, SSD_N_HEADS), f32)
    dt0 = jnp.exp(u * (math.log(DT_MAX) - math.log(DT_MIN)) + math.log(DT_MIN))
    ssd_dt_bias = dt0 + jnp.log(-jnp.expm1(-dt0))
    ssd_A_log = jnp.log(jax.random.uniform(ks[12], (N_SSD_LAYERS, SSD_N_HEADS), f32, 1.0, 16.0))
    ssd_D = 1.0 + nrm(ks[13], (N_SSD_LAYERS, SSD_N_HEADS), 0.02)
    ssd_norm_w = 1.0 + nrm(ks[14], (N_SSD_LAYERS, SSD_D_INNER), 0.02)
    ssd_out_w = nrm(ks[15], (N_SSD_LAYERS, SSD_D_INNER, D), SSD_D_INNER ** -0.5)
    sc_in_w = nrm(ks[16], (N_CONV_LAYERS, D, 3 * SC_WIDTH), D ** -0.5)
    sc_conv_w = nrm(ks[17], (N_CONV_LAYERS, SC_CONV_K, SC_WIDTH), SC_CONV_K ** -0.5)
    sc_out_w = nrm(ks[18], (N_CONV_LAYERS, SC_WIDTH, D), SC_WIDTH ** -0.5)
    final_norm_w = 1.0 + nrm(ks[19], (D,), 0.02)
    return {"x": x, "c": c, "ada_w": ada_w, "ada_b": ada_b,
            "mix_norm_w": mix_norm_w, "mlp_norm_w": mlp_norm_w,
            "mlp_up": mlp_up, "mlp_down": mlp_down,
            "ssd_in_w": ssd_in_w, "ssd_conv_w": ssd_conv_w, "ssd_conv_b": ssd_conv_b,
            "ssd_dt_bias": ssd_dt_bias, "ssd_A_log": ssd_A_log, "ssd_D": ssd_D,
            "ssd_norm_w": ssd_norm_w, "ssd_out_w": ssd_out_w,
            "sc_in_w": sc_in_w, "sc_conv_w": sc_conv_w, "sc_out_w": sc_out_w,
            "final_norm_w": final_norm_w}


def _fwd_reference(x, c, ada_w, ada_b, mix_norm_w, mlp_norm_w, mlp_up, mlp_down,
              ssd_in_w, ssd_conv_w, ssd_conv_b, ssd_dt_bias, ssd_A_log, ssd_D,
              ssd_norm_w, ssd_out_w, sc_in_w, sc_conv_w, sc_out_w, final_norm_w):
    cond = jax.nn.silu(c.astype(x.dtype))
    for i in range(DEPTH):
        mod = cond @ ada_w[i] + ada_b[i]
        sh_m, sc_m, g_m, sh_f, sc_f, g_f = [m[:, None, :] for m in jnp.split(mod, ADA_MODS, axis=-1)]
        h = rmsnorm(x, mix_norm_w[i]) * (1.0 + sc_m) + sh_m
        j = i // N_MIXERS
        if i % N_MIXERS == 0:
            y = ssd_mixer(h, ssd_in_w[j], ssd_conv_w[j], ssd_conv_b[j], ssd_dt_bias[j],
                          ssd_A_log[j], ssd_D[j], ssd_norm_w[j], ssd_out_w[j])
        else:
            y = short_conv_mixer(h, sc_in_w[j], sc_conv_w[j], sc_out_w[j])
        x = x + g_m * y
        h = rmsnorm(x, mlp_norm_w[i]) * (1.0 + sc_f) + sh_f
        x = x + g_f * sqrelu_mlp(h, mlp_up[i], mlp_down[i])
    return rmsnorm(x, final_norm_w)


import jax as _jax
import jax.numpy as _jnp

TWIN_FORMAT = 'train_step'
FWD_PARAMS = ['x', 'c', 'ada_w', 'ada_b', 'mix_norm_w', 'mlp_norm_w', 'mlp_up', 'mlp_down', 'ssd_in_w', 'ssd_conv_w', 'ssd_conv_b', 'ssd_dt_bias', 'ssd_A_log', 'ssd_D', 'ssd_norm_w', 'ssd_out_w', 'sc_in_w', 'sc_conv_w', 'sc_out_w', 'final_norm_w']
TWIN_WEIGHTS = ['ada_w', 'ada_b', 'mix_norm_w', 'mlp_norm_w', 'mlp_up', 'mlp_down', 'ssd_in_w', 'ssd_conv_w', 'ssd_conv_b', 'ssd_dt_bias', 'ssd_A_log', 'ssd_D', 'ssd_norm_w', 'ssd_out_w', 'sc_in_w', 'sc_conv_w', 'sc_out_w', 'final_norm_w']
TWIN_DIFF_INPUT = 'x'
TWIN_INPUTS = ['x', 'c', 'ada_w', 'ada_b', 'mix_norm_w', 'mlp_norm_w', 'mlp_up', 'mlp_down', 'ssd_in_w', 'ssd_conv_w', 'ssd_conv_b', 'ssd_dt_bias', 'ssd_A_log', 'ssd_D', 'ssd_norm_w', 'ssd_out_w', 'sc_in_w', 'sc_conv_w', 'sc_out_w', 'final_norm_w', 'loss_target', 'm_ada_w', 'm_ada_b', 'm_mix_norm_w', 'm_mlp_norm_w', 'm_mlp_up', 'm_mlp_down', 'm_ssd_in_w', 'm_ssd_conv_w', 'm_ssd_conv_b', 'm_ssd_dt_bias', 'm_ssd_A_log', 'm_ssd_D', 'm_ssd_norm_w', 'm_ssd_out_w', 'm_sc_in_w', 'm_sc_conv_w', 'm_sc_out_w', 'm_final_norm_w', 'v_ada_w', 'v_ada_b', 'v_mix_norm_w', 'v_mlp_norm_w', 'v_mlp_up', 'v_mlp_down', 'v_ssd_in_w', 'v_ssd_conv_w', 'v_ssd_conv_b', 'v_ssd_dt_bias', 'v_ssd_A_log', 'v_ssd_D', 'v_ssd_norm_w', 'v_ssd_out_w', 'v_sc_in_w', 'v_sc_conv_w', 'v_sc_out_w', 'v_final_norm_w']
TWIN_OUTPUTS = ['loss', 'grad_x', 'grad_ada_w', 'grad_ada_b', 'grad_mix_norm_w', 'grad_mlp_norm_w', 'grad_mlp_up', 'grad_mlp_down', 'grad_ssd_in_w', 'grad_ssd_conv_w', 'grad_ssd_conv_b', 'grad_ssd_dt_bias', 'grad_ssd_A_log', 'grad_ssd_D', 'grad_ssd_norm_w', 'grad_ssd_out_w', 'grad_sc_in_w', 'grad_sc_conv_w', 'grad_sc_out_w', 'grad_final_norm_w', 'delta_ada_w', 'delta_ada_b', 'delta_mix_norm_w', 'delta_mlp_norm_w', 'delta_mlp_up', 'delta_mlp_down', 'delta_ssd_in_w', 'delta_ssd_conv_w', 'delta_ssd_conv_b', 'delta_ssd_dt_bias', 'delta_ssd_A_log', 'delta_ssd_D', 'delta_ssd_norm_w', 'delta_ssd_out_w', 'delta_sc_in_w', 'delta_sc_conv_w', 'delta_sc_out_w', 'delta_final_norm_w', 'new_m_ada_w', 'new_m_ada_b', 'new_m_mix_norm_w', 'new_m_mlp_norm_w', 'new_m_mlp_up', 'new_m_mlp_down', 'new_m_ssd_in_w', 'new_m_ssd_conv_w', 'new_m_ssd_conv_b', 'new_m_ssd_dt_bias', 'new_m_ssd_A_log', 'new_m_ssd_D', 'new_m_ssd_norm_w', 'new_m_ssd_out_w', 'new_m_sc_in_w', 'new_m_sc_conv_w', 'new_m_sc_out_w', 'new_m_final_norm_w', 'new_v_ada_w', 'new_v_ada_b', 'new_v_mix_norm_w', 'new_v_mlp_norm_w', 'new_v_mlp_up', 'new_v_mlp_down', 'new_v_ssd_in_w', 'new_v_ssd_conv_w', 'new_v_ssd_conv_b', 'new_v_ssd_dt_bias', 'new_v_ssd_A_log', 'new_v_ssd_D', 'new_v_ssd_norm_w', 'new_v_ssd_out_w', 'new_v_sc_in_w', 'new_v_sc_conv_w', 'new_v_sc_out_w', 'new_v_final_norm_w']
TWIN_LEAF_KINDS = {'loss': 'loss', 'grad_x': 'grad_x', 'grad_ada_w': 'grad_w', 'grad_ada_b': 'grad_w', 'grad_mix_norm_w': 'grad_w', 'grad_mlp_norm_w': 'grad_w', 'grad_mlp_up': 'grad_w', 'grad_mlp_down': 'grad_w', 'grad_ssd_in_w': 'grad_w', 'grad_ssd_conv_w': 'grad_w', 'grad_ssd_conv_b': 'grad_w', 'grad_ssd_dt_bias': 'grad_w', 'grad_ssd_A_log': 'grad_w', 'grad_ssd_D': 'grad_w', 'grad_ssd_norm_w': 'grad_w', 'grad_ssd_out_w': 'grad_w', 'grad_sc_in_w': 'grad_w', 'grad_sc_conv_w': 'grad_w', 'grad_sc_out_w': 'grad_w', 'grad_final_norm_w': 'grad_w', 'delta_ada_w': 'delta_w', 'delta_ada_b': 'delta_w', 'delta_mix_norm_w': 'delta_w', 'delta_mlp_norm_w': 'delta_w', 'delta_mlp_up': 'delta_w', 'delta_mlp_down': 'delta_w', 'delta_ssd_in_w': 'delta_w', 'delta_ssd_conv_w': 'delta_w', 'delta_ssd_conv_b': 'delta_w', 'delta_ssd_dt_bias': 'delta_w', 'delta_ssd_A_log': 'delta_w', 'delta_ssd_D': 'delta_w', 'delta_ssd_norm_w': 'delta_w', 'delta_ssd_out_w': 'delta_w', 'delta_sc_in_w': 'delta_w', 'delta_sc_conv_w': 'delta_w', 'delta_sc_out_w': 'delta_w', 'delta_final_norm_w': 'delta_w', 'new_m_ada_w': 'new_m', 'new_m_ada_b': 'new_m', 'new_m_mix_norm_w': 'new_m', 'new_m_mlp_norm_w': 'new_m', 'new_m_mlp_up': 'new_m', 'new_m_mlp_down': 'new_m', 'new_m_ssd_in_w': 'new_m', 'new_m_ssd_conv_w': 'new_m', 'new_m_ssd_conv_b': 'new_m', 'new_m_ssd_dt_bias': 'new_m', 'new_m_ssd_A_log': 'new_m', 'new_m_ssd_D': 'new_m', 'new_m_ssd_norm_w': 'new_m', 'new_m_ssd_out_w': 'new_m', 'new_m_sc_in_w': 'new_m', 'new_m_sc_conv_w': 'new_m', 'new_m_sc_out_w': 'new_m', 'new_m_final_norm_w': 'new_m', 'new_v_ada_w': 'new_v', 'new_v_ada_b': 'new_v', 'new_v_mix_norm_w': 'new_v', 'new_v_mlp_norm_w': 'new_v', 'new_v_mlp_up': 'new_v', 'new_v_mlp_down': 'new_v', 'new_v_ssd_in_w': 'new_v', 'new_v_ssd_conv_w': 'new_v', 'new_v_ssd_conv_b': 'new_v', 'new_v_ssd_dt_bias': 'new_v', 'new_v_ssd_A_log': 'new_v', 'new_v_ssd_D': 'new_v', 'new_v_ssd_norm_w': 'new_v', 'new_v_ssd_out_w': 'new_v', 'new_v_sc_in_w': 'new_v', 'new_v_sc_conv_w': 'new_v', 'new_v_sc_out_w': 'new_v', 'new_v_final_norm_w': 'new_v'}


def _forward(args):
    return _fwd_reference(*[args[k] for k in FWD_PARAMS])


def _output_shape():
    out = _jax.eval_shape(lambda: _forward(_fwd_setup_inputs(0)))
    return out.shape, out.dtype

N_MICROBATCH = 1
ADAM_LR = 0.001
ADAM_B1 = 0.9
ADAM_B2 = 0.999
ADAM_EPS = 1e-08
ADAM_WD = 0.01
ADAM_STEP = 10
PER_EXAMPLE_BATCH_AXIS = {'x': 0, 'c': 0, 'loss_target': 0}
SHARED_INPUTS = []
_WEIGHT_DTYPES = {'ada_w': _jnp.float32, 'ada_b': _jnp.float32, 'mix_norm_w': _jnp.float32, 'mlp_norm_w': _jnp.float32, 'mlp_up': _jnp.float32, 'mlp_down': _jnp.float32, 'ssd_in_w': _jnp.float32, 'ssd_conv_w': _jnp.float32, 'ssd_conv_b': _jnp.float32, 'ssd_dt_bias': _jnp.float32, 'ssd_A_log': _jnp.float32, 'ssd_D': _jnp.float32, 'ssd_norm_w': _jnp.float32, 'ssd_out_w': _jnp.float32, 'sc_in_w': _jnp.float32, 'sc_conv_w': _jnp.float32, 'sc_out_w': _jnp.float32, 'final_norm_w': _jnp.float32}
MOMENT_SCALE = {'ada_w': 6.687647e-02, 'ada_b': 1.125716e-01, 'mix_norm_w': 6.926307e-02, 'mlp_norm_w': 5.456199e-02, 'mlp_up': 2.847945e-02, 'mlp_down': 5.035954e-02, 'ssd_in_w': 2.856388e-02, 'ssd_conv_w': 2.574947e-02, 'ssd_conv_b': 3.079816e-02, 'ssd_dt_bias': 6.850148e-02, 'ssd_A_log': 1.099661e-01, 'ssd_D': 1.735072e-01, 'ssd_norm_w': 3.017696e-02, 'ssd_out_w': 4.172926e-02, 'sc_in_w': 4.311908e-02, 'sc_conv_w': 4.318673e-02, 'sc_out_w': 4.313199e-02, 'final_norm_w': 1.622687e+01}


def _to_microbatches(a, axis):
    t = _jnp.moveaxis(a, axis, 0)
    t = t.reshape((N_MICROBATCH, t.shape[0] // N_MICROBATCH) + t.shape[1:])
    return _jnp.moveaxis(t, 1, axis + 1)


def setup_inputs(seed: int = 0) -> dict:
    inp = _fwd_setup_inputs(seed)
    key = _jax.random.fold_in(_jax.random.key(seed), 7919)
    shape, _ = _output_shape()
    out = dict(inp)
    out["loss_target"] = _jax.random.normal(_jax.random.fold_in(key, 0), shape, _jnp.float32)
    for i, name in enumerate(TWIN_WEIGHTS):
        w = inp[name].astype(_jnp.float32)
        if MOMENT_SCALE is None:
            s = _jnp.sqrt(_jnp.mean(_jnp.square(w)) + 1e-30)
        else:
            s = MOMENT_SCALE[name]
        km, kv = _jax.random.split(_jax.random.fold_in(key, i + 1))
        out[name] = w
        out["m_" + name] = s * _jax.random.normal(km, w.shape, _jnp.float32)
        out["v_" + name] = (s * s) * _jax.random.uniform(kv, w.shape, _jnp.float32, 0.5, 1.5)
    if N_MICROBATCH > 1:
        for name, axis in PER_EXAMPLE_BATCH_AXIS.items():
            out[name] = _to_microbatches(out[name], axis)
    return {'x': out['x'], 'c': out['c'], 'ada_w': out['ada_w'], 'ada_b': out['ada_b'], 'mix_norm_w': out['mix_norm_w'], 'mlp_norm_w': out['mlp_norm_w'], 'mlp_up': out['mlp_up'], 'mlp_down': out['mlp_down'], 'ssd_in_w': out['ssd_in_w'], 'ssd_conv_w': out['ssd_conv_w'], 'ssd_conv_b': out['ssd_conv_b'], 'ssd_dt_bias': out['ssd_dt_bias'], 'ssd_A_log': out['ssd_A_log'], 'ssd_D': out['ssd_D'], 'ssd_norm_w': out['ssd_norm_w'], 'ssd_out_w': out['ssd_out_w'], 'sc_in_w': out['sc_in_w'], 'sc_conv_w': out['sc_conv_w'], 'sc_out_w': out['sc_out_w'], 'final_norm_w': out['final_norm_w'], 'loss_target': out['loss_target'], 'm_ada_w': out['m_ada_w'], 'm_ada_b': out['m_ada_b'], 'm_mix_norm_w': out['m_mix_norm_w'], 'm_mlp_norm_w': out['m_mlp_norm_w'], 'm_mlp_up': out['m_mlp_up'], 'm_mlp_down': out['m_mlp_down'], 'm_ssd_in_w': out['m_ssd_in_w'], 'm_ssd_conv_w': out['m_ssd_conv_w'], 'm_ssd_conv_b': out['m_ssd_conv_b'], 'm_ssd_dt_bias': out['m_ssd_dt_bias'], 'm_ssd_A_log': out['m_ssd_A_log'], 'm_ssd_D': out['m_ssd_D'], 'm_ssd_norm_w': out['m_ssd_norm_w'], 'm_ssd_out_w': out['m_ssd_out_w'], 'm_sc_in_w': out['m_sc_in_w'], 'm_sc_conv_w': out['m_sc_conv_w'], 'm_sc_out_w': out['m_sc_out_w'], 'm_final_norm_w': out['m_final_norm_w'], 'v_ada_w': out['v_ada_w'], 'v_ada_b': out['v_ada_b'], 'v_mix_norm_w': out['v_mix_norm_w'], 'v_mlp_norm_w': out['v_mlp_norm_w'], 'v_mlp_up': out['v_mlp_up'], 'v_mlp_down': out['v_mlp_down'], 'v_ssd_in_w': out['v_ssd_in_w'], 'v_ssd_conv_w': out['v_ssd_conv_w'], 'v_ssd_conv_b': out['v_ssd_conv_b'], 'v_ssd_dt_bias': out['v_ssd_dt_bias'], 'v_ssd_A_log': out['v_ssd_A_log'], 'v_ssd_D': out['v_ssd_D'], 'v_ssd_norm_w': out['v_ssd_norm_w'], 'v_ssd_out_w': out['v_ssd_out_w'], 'v_sc_in_w': out['v_sc_in_w'], 'v_sc_conv_w': out['v_sc_conv_w'], 'v_sc_out_w': out['v_sc_out_w'], 'v_final_norm_w': out['v_final_norm_w']}


def _loss(weights, diff, rest, loss_target):
    with _jax.named_scope("forward"):
        args = {**rest, TWIN_DIFF_INPUT: diff, **{k: w.astype(_WEIGHT_DTYPES[k]) for k, w in weights.items()}}
        y = _forward(args)
    with _jax.named_scope("loss_head"):
        err = _jnp.square(y.astype(_jnp.float32) - loss_target)
        return 0.5 * _jnp.sum(_jnp.mean(err, axis=-1)) if err.ndim else 0.5 * err


def _adamw(w, g, m, v):
    m = ADAM_B1 * m + (1.0 - ADAM_B1) * g
    v = ADAM_B2 * v + (1.0 - ADAM_B2) * _jnp.square(g)
    m_hat = m / (1.0 - ADAM_B1 ** ADAM_STEP)
    v_hat = v / (1.0 - ADAM_B2 ** ADAM_STEP)
    delta = -ADAM_LR * (m_hat / (_jnp.sqrt(v_hat) + ADAM_EPS) + ADAM_WD * w)
    return delta, m, v


def reference(x, c, ada_w, ada_b, mix_norm_w, mlp_norm_w, mlp_up, mlp_down, ssd_in_w, ssd_conv_w, ssd_conv_b, ssd_dt_bias, ssd_A_log, ssd_D, ssd_norm_w, ssd_out_w, sc_in_w, sc_conv_w, sc_out_w, final_norm_w, loss_target, m_ada_w, m_ada_b, m_mix_norm_w, m_mlp_norm_w, m_mlp_up, m_mlp_down, m_ssd_in_w, m_ssd_conv_w, m_ssd_conv_b, m_ssd_dt_bias, m_ssd_A_log, m_ssd_D, m_ssd_norm_w, m_ssd_out_w, m_sc_in_w, m_sc_conv_w, m_sc_out_w, m_final_norm_w, v_ada_w, v_ada_b, v_mix_norm_w, v_mlp_norm_w, v_mlp_up, v_mlp_down, v_ssd_in_w, v_ssd_conv_w, v_ssd_conv_b, v_ssd_dt_bias, v_ssd_A_log, v_ssd_D, v_ssd_norm_w, v_ssd_out_w, v_sc_in_w, v_sc_conv_w, v_sc_out_w, v_final_norm_w):
    given = dict(x=x, c=c, ada_w=ada_w, ada_b=ada_b, mix_norm_w=mix_norm_w, mlp_norm_w=mlp_norm_w, mlp_up=mlp_up, mlp_down=mlp_down, ssd_in_w=ssd_in_w, ssd_conv_w=ssd_conv_w, ssd_conv_b=ssd_conv_b, ssd_dt_bias=ssd_dt_bias, ssd_A_log=ssd_A_log, ssd_D=ssd_D, ssd_norm_w=ssd_norm_w, ssd_out_w=ssd_out_w, sc_in_w=sc_in_w, sc_conv_w=sc_conv_w, sc_out_w=sc_out_w, final_norm_w=final_norm_w, loss_target=loss_target, m_ada_w=m_ada_w, m_ada_b=m_ada_b, m_mix_norm_w=m_mix_norm_w, m_mlp_norm_w=m_mlp_norm_w, m_mlp_up=m_mlp_up, m_mlp_down=m_mlp_down, m_ssd_in_w=m_ssd_in_w, m_ssd_conv_w=m_ssd_conv_w, m_ssd_conv_b=m_ssd_conv_b, m_ssd_dt_bias=m_ssd_dt_bias, m_ssd_A_log=m_ssd_A_log, m_ssd_D=m_ssd_D, m_ssd_norm_w=m_ssd_norm_w, m_ssd_out_w=m_ssd_out_w, m_sc_in_w=m_sc_in_w, m_sc_conv_w=m_sc_conv_w, m_sc_out_w=m_sc_out_w, m_final_norm_w=m_final_norm_w, v_ada_w=v_ada_w, v_ada_b=v_ada_b, v_mix_norm_w=v_mix_norm_w, v_mlp_norm_w=v_mlp_norm_w, v_mlp_up=v_mlp_up, v_mlp_down=v_mlp_down, v_ssd_in_w=v_ssd_in_w, v_ssd_conv_w=v_ssd_conv_w, v_ssd_conv_b=v_ssd_conv_b, v_ssd_dt_bias=v_ssd_dt_bias, v_ssd_A_log=v_ssd_A_log, v_ssd_D=v_ssd_D, v_ssd_norm_w=v_ssd_norm_w, v_ssd_out_w=v_ssd_out_w, v_sc_in_w=v_sc_in_w, v_sc_conv_w=v_sc_conv_w, v_sc_out_w=v_sc_out_w, v_final_norm_w=v_final_norm_w)
    weights = {n: given[n] for n in TWIN_WEIGHTS}
    shared = {n: given[n] for n in SHARED_INPUTS}
    per_example = {n: given[n] for n in ['x', 'c']}
    grad_fn = _jax.value_and_grad(_loss, argnums=(0, 1))

    def one_microbatch(ex, loss_target):
        ex = dict(ex)
        diff = ex.pop(TWIN_DIFF_INPUT)
        return grad_fn(weights, diff, {**shared, **ex}, loss_target)

    if N_MICROBATCH == 1:
        loss, (grad_w, grad_x) = one_microbatch(per_example, given["loss_target"])
    else:
        def body(carry, xs):
            loss_sum, grad_sum = carry
            l_k, (gw_k, gx_k) = one_microbatch(xs[0], xs[1])
            with _jax.named_scope("update"):
                return (loss_sum + l_k, _jax.tree.map(_jnp.add, grad_sum, gw_k)), gx_k

        init = (_jnp.zeros((), _jnp.float32), _jax.tree.map(_jnp.zeros_like, weights))
        (loss, grad_w), grad_x = _jax.lax.scan(body, init, (per_example, given["loss_target"]))
    with _jax.named_scope("update"):
        delta_w, new_m, new_v = {}, {}, {}
        for n in TWIN_WEIGHTS:
            delta_w[n], new_m[n], new_v[n] = _adamw(weights[n], grad_w[n], given["m_" + n], given["v_" + n])
    return (loss, grad_x, *[grad_w[n] for n in TWIN_WEIGHTS], *[delta_w[n] for n in TWIN_WEIGHTS],
            *[new_m[n] for n in TWIN_WEIGHTS], *[new_v[n] for n in TWIN_WEIGHTS])
```

```python
import functools

import jax
import jax.numpy as jnp
from jax import lax
from jax.experimental import pallas as pl
from jax.experimental.pallas import tpu as pltpu

F32 = jnp.float32
BF16 = jnp.bfloat16
N_DEV = 8
MESH_AXES = ("x", "y", "c")
MESH = pl.DeviceIdType.MESH

NORM_EPS = 1e-5
SSD_G = 4
SSD_P = 64
SSD_N = 128
SSD_CHUNK = 128
SSD_K = 4
SC_K = 3
LANES = 128

ADAM_LR = 0.001
ADAM_B1 = 0.9
ADAM_B2 = 0.999
ADAM_EPS = 1e-08
ADAM_WD = 0.01
ADAM_STEP = 10

VMEM_LIMIT = 56 * 1024 * 1024


def _pcall(body, **kw):
    return pl.pallas_call(body, **kw)


def _cparams(sem=None):
    if sem is None:
        return pltpu.CompilerParams(vmem_limit_bytes=VMEM_LIMIT)
    return pltpu.CompilerParams(dimension_semantics=sem, vmem_limit_bytes=VMEM_LIMIT)


def _my_index():
    return 4 * lax.axis_index("x") + 2 * lax.axis_index("y") + lax.axis_index("c")


_PEER_MASKS = [(0, 0, 1), (0, 1, 0), (0, 1, 1), (1, 0, 0), (1, 0, 1), (1, 1, 0), (1, 1, 1)]


def _peers():
    x, y, c = lax.axis_index("x"), lax.axis_index("y"), lax.axis_index("c")
    out = []
    for mx, my, mc in _PEER_MASKS:
        px = (1 - x) if mx else x
        py = (1 - y) if my else y
        pc = (1 - c) if mc else c
        out.append(((px, py, pc), 4 * px + 2 * py + pc))
    return out


def _exchange(arrs, name, gather):
    n = len(arrs)
    n_peer = N_DEV - 1

    def body(*refs):
        ins, outs = refs[:n], refs[n:2 * n]
        send_sems, recv_sems, local_sems = refs[2 * n:]
        me = _my_index()
        peers = _peers()
        started = []
        for a in range(n):
            src_own = ins[a] if gather else ins[a].at[me]
            own = pltpu.make_async_copy(src_own, outs[a].at[me], local_sems.at[a])
            own.start()
            started.append(own)
        sends = []
        for a in range(n):
            for k, (peer, pidx) in enumerate(peers):
                src = ins[a] if gather else ins[a].at[pidx]
                cp = pltpu.make_async_remote_copy(
                    src_ref=src, dst_ref=outs[a].at[me],
                    send_sem=send_sems.at[a * n_peer + k], recv_sem=recv_sems.at[a * n_peer + k],
                    device_id=peer, device_id_type=MESH)
                cp.start()
                sends.append(cp)
        for a in range(n):
            for k, (peer, pidx) in enumerate(peers):
                src = ins[a] if gather else ins[a].at[pidx]
                pltpu.make_async_remote_copy(
                    src_ref=src, dst_ref=outs[a].at[pidx],
                    send_sem=send_sems.at[a * n_peer + k], recv_sem=recv_sems.at[a * n_peer + k],
                    device_id=peer, device_id_type=MESH).wait_recv()
        for cp in sends:
            cp.wait_send()
        for own in started:
            own.wait()

    if gather:
        out_shape = [jax.ShapeDtypeStruct((N_DEV,) + a.shape, a.dtype) for a in arrs]
    else:
        out_shape = [jax.ShapeDtypeStruct(a.shape, a.dtype) for a in arrs]
    any_spec = pl.BlockSpec(memory_space=pl.ANY)
    outs = _pcall(
        body, name=name, out_shape=out_shape,
        in_specs=[any_spec] * n, out_specs=[any_spec] * n,
        scratch_shapes=[pltpu.SemaphoreType.DMA((n * n_peer,)), pltpu.SemaphoreType.DMA((n * n_peer,)),
                        pltpu.SemaphoreType.DMA((n,))],
        compiler_params=pltpu.CompilerParams(has_side_effects=True),
    )(*arrs)
    return list(outs)


_DIMS = {"nn": (((1,), (0,)), ((), ())), "nt": (((1,), (1,)), ((), ())), "tn": (((0,), (0,)), ((), ()))}


def _dot(a, b, mode="nn"):
    return lax.dot_general(a, b, _DIMS[mode], preferred_element_type=F32)


def _mm(a, b, *, mode, grid, a_spec, b_spec, out_shape, out_specs, acc_shape, epilogue, name,
        extra=(), extra_specs=()):
    nk = grid[2]
    n_extra = len(extra)

    def body(*refs):
        a_ref, b_ref = refs[0], refs[1]
        ex = refs[2:2 + n_extra]
        outs = refs[2 + n_extra:-1]
        acc = refs[-1]
        k = pl.program_id(2)

        @pl.when(k == 0)
        def _():
            acc[...] = jnp.zeros_like(acc)

        acc[...] += _dot(a_ref[...], b_ref[...], mode)

        @pl.when(k == nk - 1)
        def _():
            epilogue(acc[...], ex, outs)

    return _pcall(
        body, name=name, grid=grid, out_shape=out_shape,
        in_specs=[a_spec, b_spec] + list(extra_specs), out_specs=out_specs,
        scratch_shapes=[pltpu.VMEM(acc_shape, F32)],
        compiler_params=_cparams(("parallel", "parallel", "arbitrary")),
    )(a, b, *extra)


def _ep_store(dtype):
    def ep(acc, ex, outs):
        outs[0][...] = acc.astype(dtype)
    return ep


def _ep_relu2(acc, ex, outs):
    outs[0][...] = acc.astype(BF16)
    r = jnp.maximum(acc, 0.0)
    outs[1][...] = (r * r).astype(BF16)


def _ep_relu2_bwd(acc, ex, outs):
    u = ex[0][...].astype(F32)
    outs[0][...] = (acc * (2.0 * jnp.maximum(u, 0.0))).astype(BF16)


def _tile(n, want):
    t = min(n, want)
    while n % t:
        t //= 2
    return t


def _mm_nn(a, w, out_dtype, name, tm=512, tn=512, tk=1024, epilogue=None, n_out=1, out_dtypes=None):
    M, K = a.shape
    N = w.shape[1]
    tm, tn, tk = _tile(M, tm), _tile(N, tn), _tile(K, tk)
    out_dtypes = out_dtypes or [out_dtype]
    return _mm(a, w, mode="nn", grid=(M // tm, N // tn, K // tk),
               a_spec=pl.BlockSpec((tm, tk), lambda i, j, k: (i, k)),
               b_spec=pl.BlockSpec((tk, tn), lambda i, j, k: (k, j)),
               out_shape=[jax.ShapeDtypeStruct((M, N), d) for d in out_dtypes],
               out_specs=[pl.BlockSpec((tm, tn), lambda i, j, k: (i, j)) for _ in out_dtypes],
               acc_shape=(tm, tn), epilogue=epilogue or _ep_store(out_dtype), name=name)


def _mm_nn_blocked(a, wg, name, epilogue, out_dtypes, tm=512):
    M, K = a.shape
    n = wg.shape[2]
    tm = _tile(M, tm)
    return _mm(a, wg, mode="nn", grid=(M // tm, N_DEV, 1),
               a_spec=pl.BlockSpec((tm, K), lambda i, j, k: (i, 0)),
               b_spec=pl.BlockSpec((None, K, n), lambda i, j, k: (j, 0, 0)),
               out_shape=[jax.ShapeDtypeStruct((M, N_DEV * n), d) for d in out_dtypes],
               out_specs=[pl.BlockSpec((tm, n), lambda i, j, k: (i, j)) for _ in out_dtypes],
               acc_shape=(tm, n), epilogue=epilogue, name=name)


def _mm_nt(a, w, out_dtype, name, tm=512, tn=512, tk=1024, epilogue=None, extra=(), extra_specs=()):
    M, K = a.shape
    N = w.shape[0]
    tm, tn, tk = _tile(M, tm), _tile(N, tn), _tile(K, tk)
    if extra and not extra_specs:
        extra_specs = [pl.BlockSpec((tm, tn), lambda i, j, k: (i, j)) for _ in extra]
    return _mm(a, w, mode="nt", grid=(M // tm, N // tn, K // tk),
               a_spec=pl.BlockSpec((tm, tk), lambda i, j, k: (i, k)),
               b_spec=pl.BlockSpec((tn, tk), lambda i, j, k: (j, k)),
               out_shape=[jax.ShapeDtypeStruct((M, N), out_dtype)],
               out_specs=[pl.BlockSpec((tm, tn), lambda i, j, k: (i, j))],
               acc_shape=(tm, tn), epilogue=epilogue or _ep_store(out_dtype), name=name,
               extra=extra, extra_specs=extra_specs)[0]


def _mm_nt_blocked(a, wg, out_dtype, name, tm=512):
    M = a.shape[0]
    kout, n = wg.shape[1], wg.shape[2]
    tm = _tile(M, tm)
    return _mm(a, wg, mode="nt", grid=(M // tm, 1, N_DEV),
               a_spec=pl.BlockSpec((tm, n), lambda i, j, k: (i, k)),
               b_spec=pl.BlockSpec((None, kout, n), lambda i, j, k: (k, 0, 0)),
               out_shape=[jax.ShapeDtypeStruct((M, kout), out_dtype)],
               out_specs=[pl.BlockSpec((tm, kout), lambda i, j, k: (i, 0))],
               acc_shape=(tm, kout), epilogue=_ep_store(out_dtype), name=name)[0]


def _mm_tn(a, b, out_dtype, name, tm=512, tn=512, tk=1024):
    K, M = a.shape
    N = b.shape[1]
    tm, tn, tk = _tile(M, tm), _tile(N, tn), _tile(K, tk)
    return _mm(a, b, mode="tn", grid=(M // tm, N // tn, K // tk),
               a_spec=pl.BlockSpec((tk, tm), lambda i, j, k: (k, i)),
               b_spec=pl.BlockSpec((tk, tn), lambda i, j, k: (k, j)),
               out_shape=[jax.ShapeDtypeStruct((M, N), out_dtype)],
               out_specs=[pl.BlockSpec((tm, tn), lambda i, j, k: (i, j))],
               acc_shape=(tm, tn), epilogue=_ep_store(out_dtype), name=name)[0]


def _mm_tn_blocked(a, b, out_dtype, name, tm=512, tk=1024):
    K, M = a.shape
    n = b.shape[1] // N_DEV
    tm, tk = _tile(M, tm), _tile(K, tk)
    return _mm(a, b, mode="tn", grid=(M // tm, N_DEV, K // tk),
               a_spec=pl.BlockSpec((tk, tm), lambda i, j, k: (k, i)),
               b_spec=pl.BlockSpec((tk, n), lambda i, j, k: (k, j)),
               out_shape=[jax.ShapeDtypeStruct((N_DEV, M, n), out_dtype)],
               out_specs=[pl.BlockSpec((None, tm, n), lambda i, j, k: (j, i, 0))],
               acc_shape=(tm, n), epilogue=_ep_store(out_dtype), name=name)[0]


def _sigmoid(x):
    return 1.0 / (1.0 + jnp.exp(-x))


def _row_spec(tm, d):
    return pl.BlockSpec((tm, d), lambda i: (i, 0))


def _vec_spec(d):
    return pl.BlockSpec((1, d), lambda i: (0, 0))


def _norm_mod_fwd(x, y, gate, nw, scale, shift, name, tm=256):
    L, D = x.shape
    tm = _tile(L, tm)
    has_res = y is not None

    def body(*refs):
        if has_res:
            x_ref, y_ref, g_ref, nw_ref, sc_ref, sh_ref, xo_ref, h_ref = refs
            xn = x_ref[...] + g_ref[...] * y_ref[...]
            xo_ref[...] = xn
        else:
            x_ref, nw_ref, sc_ref, sh_ref, h_ref = refs
            xn = x_ref[...]
        rstd = lax.rsqrt(jnp.mean(xn * xn, axis=-1, keepdims=True) + NORM_EPS)
        h = xn * rstd * nw_ref[...] * (1.0 + sc_ref[...]) + sh_ref[...]
        h_ref[...] = h.astype(BF16)

    row, vec = _row_spec(tm, D), _vec_spec(D)
    if has_res:
        ins, in_specs = (x, y, gate, nw, scale, shift), [row, row, vec, vec, vec, vec]
        out_shape = [jax.ShapeDtypeStruct((L, D), F32), jax.ShapeDtypeStruct((L, D), BF16)]
        out_specs = [row, row]
    else:
        ins, in_specs = (x, nw, scale, shift), [row, vec, vec, vec]
        out_shape = [jax.ShapeDtypeStruct((L, D), BF16)]
        out_specs = [row]
    outs = _pcall(body, name=name, grid=(L // tm,), out_shape=out_shape, in_specs=in_specs,
                  out_specs=out_specs, compiler_params=_cparams(("parallel",)))(*ins)
    return outs if has_res else (x, outs[0])


def _norm_mod_bwd(dh, x, nw, scale, dres, name, tm=256):
    L, D = x.shape
    tm = _tile(L, tm)

    def body(dh_ref, x_ref, nw_ref, sc_ref, dres_ref, dx_ref, dsh_ref, dsc_ref, dnw_ref):
        @pl.when(pl.program_id(0) == 0)
        def _():
            dsh_ref[...] = jnp.zeros_like(dsh_ref)
            dsc_ref[...] = jnp.zeros_like(dsc_ref)
            dnw_ref[...] = jnp.zeros_like(dnw_ref)

        xv = x_ref[...]
        dh_v = dh_ref[...]
        nw_v = nw_ref[...]
        rstd = lax.rsqrt(jnp.mean(xv * xv, axis=-1, keepdims=True) + NORM_EPS)
        xhat = xv * rstd
        dsh_ref[...] += jnp.sum(dh_v, axis=0, keepdims=True)
        dsc_ref[...] += jnp.sum(dh_v * (xhat * nw_v), axis=0, keepdims=True)
        dr = dh_v * (1.0 + sc_ref[...])
        dnw_ref[...] += jnp.sum(dr * xhat, axis=0, keepdims=True)
        dxh = dr * nw_v
        dx = rstd * (dxh - xhat * jnp.mean(dxh * xhat, axis=-1, keepdims=True))
        dx_ref[...] = dx + dres_ref[...]

    row, vec = _row_spec(tm, D), _vec_spec(D)
    return _pcall(
        body, name=name, grid=(L // tm,),
        out_shape=[jax.ShapeDtypeStruct((L, D), F32)] + [jax.ShapeDtypeStruct((1, D), F32)] * 3,
        in_specs=[row, row, vec, vec, row], out_specs=[row, vec, vec, vec],
        compiler_params=_cparams(("arbitrary",)))(dh, x, nw, scale, dres)


def _gate_bwd(dx, y, gate, name, tm=256):
    L, D = dx.shape
    tm = _tile(L, tm)

    def body(dx_ref, y_ref, g_ref, dy_ref, dg_ref):
        @pl.when(pl.program_id(0) == 0)
        def _():
            dg_ref[...] = jnp.zeros_like(dg_ref)

        dxv = dx_ref[...]
        dy_ref[...] = (g_ref[...] * dxv).astype(BF16)
        dg_ref[...] += jnp.sum(dxv * y_ref[...], axis=0, keepdims=True)

    row, vec = _row_spec(tm, D), _vec_spec(D)
    return _pcall(
        body, name=name, grid=(L // tm,),
        out_shape=[jax.ShapeDtypeStruct((L, D), BF16), jax.ShapeDtypeStruct((1, D), F32)],
        in_specs=[row, row, vec], out_specs=[row, vec],
        compiler_params=_cparams(("arbitrary",)))(dx, y, gate)


def _final_loss(x, y, gate, fw, target, name, tm=256):
    L, D = x.shape
    tm = _tile(L, tm)

    def body(x_ref, y_ref, g_ref, fw_ref, t_ref, dx_ref, loss_ref, dfw_ref):
        @pl.when(pl.program_id(0) == 0)
        def _():
            loss_ref[...] = jnp.zeros_like(loss_ref)
            dfw_ref[...] = jnp.zeros_like(dfw_ref)

        xn = x_ref[...] + g_ref[...] * y_ref[...]
        fw_v = fw_ref[...]
        rstd = lax.rsqrt(jnp.mean(xn * xn, axis=-1, keepdims=True) + NORM_EPS)
        xhat = xn * rstd
        diff = xhat * fw_v - t_ref[...]
        loss_ref[...] += jnp.sum(diff * diff, axis=0, keepdims=True)
        dyf = diff * (1.0 / D)
        dfw_ref[...] += jnp.sum(dyf * xhat, axis=0, keepdims=True)
        dxh = dyf * fw_v
        dx_ref[...] = rstd * (dxh - xhat * jnp.mean(dxh * xhat, axis=-1, keepdims=True))

    row, vec = _row_spec(tm, D), _vec_spec(D)
    return _pcall(
        body, name=name, grid=(L // tm,),
        out_shape=[jax.ShapeDtypeStruct((L, D), F32), jax.ShapeDtypeStruct((1, D), F32),
                   jax.ShapeDtypeStruct((1, D), F32)],
        in_specs=[row, row, vec, vec, row], out_specs=[row, vec, vec],
        compiler_params=_cparams(("arbitrary",)))(x, y, gate, fw, target)


def _shift_down(v, s, row):
    if s == 0:
        return v
    return jnp.where(row >= s, pltpu.roll(v, s, 0), 0.0)


def _shift_up(v, s, row):
    if s == 0:
        return v
    n = v.shape[0]
    return jnp.where(row < n - s, pltpu.roll(v, n - s, 0), 0.0)


def _ssd_conv_fwd(zx, w, b, col0, width, name, cb=128):
    L = zx.shape[0]
    nb = width // cb
    off = col0 // cb

    def body(x_ref, w_ref, b_ref, o_ref):
        xv = x_ref[...]
        row = lax.broadcasted_iota(jnp.int32, xv.shape, 0)
        acc = b_ref[...] + w_ref[SSD_K - 1:SSD_K, :] * xv
        for s in range(1, SSD_K):
            acc = acc + w_ref[SSD_K - 1 - s:SSD_K - s, :] * _shift_down(xv, s, row)
        o_ref[...] = acc * _sigmoid(acc)

    return _pcall(
        body, name=name, grid=(nb,), out_shape=jax.ShapeDtypeStruct((L, width), F32),
        in_specs=[pl.BlockSpec((L, cb), lambda j: (0, off + j)),
                  pl.BlockSpec((SSD_K, cb), lambda j: (0, j)),
                  pl.BlockSpec((1, cb), lambda j: (0, j))],
        out_specs=pl.BlockSpec((L, cb), lambda j: (0, j)),
        compiler_params=_cparams(("parallel",)))(zx, w, b)


def _ssd_conv_bwd(zx, w, b, dxc, col0, name, cb=128):
    L = zx.shape[0]
    width = dxc.shape[1]
    nb = width // cb
    off = col0 // cb

    def body(x_ref, w_ref, b_ref, d_ref, dx_ref, dw_ref, db_ref):
        xv = x_ref[...]
        row = lax.broadcasted_iota(jnp.int32, xv.shape, 0)
        shifted = [_shift_down(xv, s, row) for s in range(SSD_K)]
        acc = b_ref[...] + w_ref[SSD_K - 1:SSD_K, :] * xv
        for s in range(1, SSD_K):
            acc = acc + w_ref[SSD_K - 1 - s:SSD_K - s, :] * shifted[s]
        sig = _sigmoid(acc)
        dpre = d_ref[...] * (sig * (1.0 + acc * (1.0 - sig)))
        db_ref[...] = jnp.sum(dpre, axis=0, keepdims=True)
        dx = w_ref[SSD_K - 1:SSD_K, :] * dpre
        for s in range(SSD_K):
            dw_ref[SSD_K - 1 - s:SSD_K - s, :] = jnp.sum(dpre * shifted[s], axis=0, keepdims=True)
            if s:
                dx = dx + w_ref[SSD_K - 1 - s:SSD_K - s, :] * _shift_up(dpre, s, row)
        dx_ref[...] = dx.astype(BF16)

    return _pcall(
        body, name=name, grid=(nb,),
        out_shape=[jax.ShapeDtypeStruct((L, width), BF16), jax.ShapeDtypeStruct((SSD_K, width), F32),
                   jax.ShapeDtypeStruct((1, width), F32)],
        in_specs=[pl.BlockSpec((L, cb), lambda j: (0, off + j)),
                  pl.BlockSpec((SSD_K, cb), lambda j: (0, j)),
                  pl.BlockSpec((1, cb), lambda j: (0, j)),
                  pl.BlockSpec((L, cb), lambda j: (0, j))],
        out_specs=[pl.BlockSpec((L, cb), lambda j: (0, j)),
                   pl.BlockSpec((SSD_K, cb), lambda j: (0, j)),
                   pl.BlockSpec((1, cb), lambda j: (0, j))],
        compiler_params=_cparams(("parallel",)))(zx, w, b, dxc)


def _sc_conv_fwd(proj, w, name, cb=128):
    L = proj.shape[0]
    width = proj.shape[1] // 3
    nb = width // cb

    def body(b_ref, c_ref, x_ref, w_ref, o_ref):
        q = c_ref[...] * x_ref[...]
        row = lax.broadcasted_iota(jnp.int32, q.shape, 0)
        acc = w_ref[SC_K - 1:SC_K, :] * q
        for s in range(1, SC_K):
            acc = acc + w_ref[SC_K - 1 - s:SC_K - s, :] * _shift_down(q, s, row)
        o_ref[...] = (b_ref[...] * acc).astype(BF16)

    return _pcall(
        body, name=name, grid=(nb,), out_shape=jax.ShapeDtypeStruct((L, width), BF16),
        in_specs=[pl.BlockSpec((L, cb), lambda j: (0, j)),
                  pl.BlockSpec((L, cb), lambda j: (0, nb + j)),
                  pl.BlockSpec((L, cb), lambda j: (0, 2 * nb + j)),
                  pl.BlockSpec((SC_K, cb), lambda j: (0, j))],
        out_specs=pl.BlockSpec((L, cb), lambda j: (0, j)),
        compiler_params=_cparams(("parallel",)))(proj, proj, proj, w)


def _sc_conv_bwd(proj, w, dy, name, cb=128):
    L = proj.shape[0]
    width = proj.shape[1] // 3
    nb = width // cb

    def body(b_ref, c_ref, x_ref, w_ref, dy_ref, db_ref, dc_ref, dxv_ref, dw_ref):
        cg, xv, dyv = c_ref[...], x_ref[...], dy_ref[...]
        q = cg * xv
        row = lax.broadcasted_iota(jnp.int32, q.shape, 0)
        shifted = [_shift_down(q, s, row) for s in range(SC_K)]
        conv = w_ref[SC_K - 1:SC_K, :] * q
        for s in range(1, SC_K):
            conv = conv + w_ref[SC_K - 1 - s:SC_K - s, :] * shifted[s]
        db_ref[...] = (dyv * conv).astype(BF16)
        dconv = dyv * b_ref[...]
        dq = w_ref[SC_K - 1:SC_K, :] * dconv
        for s in range(SC_K):
            dw_ref[SC_K - 1 - s:SC_K - s, :] = jnp.sum(dconv * shifted[s], axis=0, keepdims=True)
            if s:
                dq = dq + w_ref[SC_K - 1 - s:SC_K - s, :] * _shift_up(dconv, s, row)
        dc_ref[...] = (dq * xv).astype(BF16)
        dxv_ref[...] = (dq * cg).astype(BF16)

    blk = pl.BlockSpec((L, cb), lambda j: (0, j))
    wblk = pl.BlockSpec((SC_K, cb), lambda j: (0, j))
    return _pcall(
        body, name=name, grid=(nb,),
        out_shape=[jax.ShapeDtypeStruct((L, width), BF16)] * 3 + [jax.ShapeDtypeStruct((SC_K, width), F32)],
        in_specs=[blk, pl.BlockSpec((L, cb), lambda j: (0, nb + j)),
                  pl.BlockSpec((L, cb), lambda j: (0, 2 * nb + j)), wblk, blk],
        out_specs=[blk, blk, blk, wblk],
        compiler_params=_cparams(("parallel",)))(proj, proj, proj, w, dy)


def _split3(v):
    hi = v.astype(BF16)
    r1 = v - hi.astype(F32)
    mid = r1.astype(BF16)
    lo = (r1 - mid.astype(F32)).astype(BF16)
    return hi, mid, lo


def _dot_exact01(t01, v):
    hi, mid, lo = _split3(v)
    return _dot(t01, hi) + _dot(t01, mid) + _dot(t01, lo)


def _lane_col(v, lane, h):
    return jnp.sum(jnp.where(lane == h, v, 0.0), axis=1, keepdims=True)


def _sum_all(v):
    return jnp.sum(jnp.sum(v, axis=1, keepdims=True), axis=0, keepdims=True)


def _softplus(x):
    return jnp.maximum(x, 0.0) + jnp.log1p(jnp.exp(-jnp.abs(x)))


def _ssd_common(dt_ref, bias_ref, alog_ref, b_ref, c_ref, cst_ref, heads):
    c_sz = SSD_CHUNK
    lane = lax.broadcasted_iota(jnp.int32, (c_sz, LANES), 1)
    row = lax.broadcasted_iota(jnp.int32, (c_sz, LANES), 0)
    valid = lane < heads
    raw = dt_ref[...] + bias_ref[...]
    dt = _softplus(raw)
    a_row = -jnp.exp(alog_ref[...])
    a = jnp.where(valid, dt * a_row, 0.0)
    tri = (row >= lane).astype(BF16)
    cs = _dot_exact01(tri, a)
    cst_ref[...] = cs.T
    last_row = jnp.sum(a, axis=0, keepdims=True)
    bb = b_ref[...].astype(BF16)
    cb = c_ref[...].astype(BF16)
    scores = _dot(cb, bb, "nt")
    return dict(lane=lane, row=row, valid=valid, raw=raw, dt=dt, a_row=a_row, cs=cs,
                last_row=last_row, bb=bb, cb=cb, scores=scores, causal=row >= lane, lo=lane < SSD_P)


def _pair_terms(q, cst_ref, j):
    lane, lo = q["lane"], q["lo"]
    out = {}
    cols, dts, lasts, lms = [], [], [], []
    lane1 = lax.broadcasted_iota(jnp.int32, (1, LANES), 1)
    for h in (2 * j, 2 * j + 1):
        col = _lane_col(q["cs"], lane, h)
        rowv = cst_ref[h:h + 1, :]
        lms.append(jnp.exp(jnp.where(q["causal"], col - rowv, -1e30)))
        cols.append(col)
        dts.append(_lane_col(q["dt"], lane, h))
        lasts.append(jnp.sum(jnp.where(lane1 == h, q["last_row"], 0.0), axis=1, keepdims=True))
    out["lm"] = lms
    out["cols"] = cols
    out["lasts"] = lasts
    out["dt_b"] = jnp.where(lo, dts[0], dts[1])
    out["e_b"] = jnp.where(lo, jnp.exp(cols[0]), jnp.exp(cols[1]))
    out["dec_cols"] = [jnp.exp(lasts[0] - cols[0]), jnp.exp(lasts[1] - cols[1])]
    out["dec_b"] = jnp.where(lo, out["dec_cols"][0], out["dec_cols"][1])
    lo1 = lane1 < SSD_P
    out["explast"] = [jnp.exp(lasts[0]), jnp.exp(lasts[1])]
    out["explast_b"] = jnp.where(lo1, out["explast"][0], out["explast"][1])
    return out


def _ssd_fwd(zx, xc, bias_p, alog_p, d_lane, nw, d_inner, name):
    L = zx.shape[0]
    nc = L // SSD_CHUNK
    gw = d_inner // SSD_G
    heads = gw // SSD_P
    n_pair = heads // 2
    zb = gw // LANES
    bc0 = d_inner // LANES
    dt0 = (2 * d_inner + 2 * SSD_G * SSD_N) // LANES

    def body(z_ref, xs_ref, b_ref, c_ref, dt_ref, bias_ref, alog_ref, dl_ref, nw_ref,
             y_ref, yn_ref, prev_ref, s_ref, cst_ref):
        @pl.when(pl.program_id(1) == 0)
        def _():
            s_ref[...] = jnp.zeros_like(s_ref)

        q = _ssd_common(dt_ref, bias_ref, alog_ref, b_ref, c_ref, cst_ref, heads)
        prev_ref[...] = s_ref[...]
        lo = q["lo"]
        for j in range(n_pair):
            sl = slice(j * LANES, (j + 1) * LANES)
            p = _pair_terms(q, cst_ref, j)
            xs_p = xs_ref[:, sl]
            xp = xs_p * p["dt_b"]
            xb = xp.astype(BF16)
            m_a = (q["scores"] * p["lm"][0]).astype(BF16)
            m_b = (q["scores"] * p["lm"][1]).astype(BF16)
            yd = jnp.where(lo, _dot(m_a, xb), _dot(m_b, xb))
            s_p = s_ref[:, sl]
            yo = _dot(q["cb"], s_p.astype(BF16)) * p["e_b"]
            y_ref[:, sl] = yd + yo + dl_ref[:, sl] * xs_p
            st = _dot(q["bb"], (xp * p["dec_b"]).astype(BF16), "tn")
            s_ref[:, sl] = s_p * p["explast_b"] + st
        yv = y_ref[...]
        zv = z_ref[...]
        yg = yv * (zv * _sigmoid(zv))
        rstd = lax.rsqrt(jnp.mean(yg * yg, axis=-1, keepdims=True) + NORM_EPS)
        yn_ref[...] = (yg * rstd * nw_ref[...]).astype(BF16)

    grp = lambda width: pl.BlockSpec((None, 1, width), lambda g, c: (g, 0, 0))
    return _pcall(
        body, name=name, grid=(SSD_G, nc),
        out_shape=[jax.ShapeDtypeStruct((L, d_inner), F32), jax.ShapeDtypeStruct((L, d_inner), BF16),
                   jax.ShapeDtypeStruct((nc, SSD_G, SSD_N, gw), F32)],
        in_specs=[pl.BlockSpec((SSD_CHUNK, gw), lambda g, c: (c, g)),
                  pl.BlockSpec((SSD_CHUNK, gw), lambda g, c: (c, g)),
                  pl.BlockSpec((SSD_CHUNK, SSD_N), lambda g, c: (c, bc0 + g)),
                  pl.BlockSpec((SSD_CHUNK, SSD_N), lambda g, c: (c, bc0 + SSD_G + g)),
                  pl.BlockSpec((SSD_CHUNK, LANES), lambda g, c: (c, dt0 + g)),
                  grp(LANES), grp(LANES), grp(gw), grp(gw)],
        out_specs=[pl.BlockSpec((SSD_CHUNK, gw), lambda g, c: (c, g)),
                   pl.BlockSpec((SSD_CHUNK, gw), lambda g, c: (c, g)),
                   pl.BlockSpec((None, None, SSD_N, gw), lambda g, c: (c, g, 0, 0))],
        scratch_shapes=[pltpu.VMEM((SSD_N, gw), F32), pltpu.VMEM((SSD_CHUNK, LANES), F32)],
        compiler_params=_cparams(("parallel", "arbitrary")))(zx, xc, xc, xc, zx, bias_p, alog_p, d_lane, nw)


def _ssd_bwd(dyn, y, zx, xc, prev, bias_p, alog_p, d_lane, nw, d_inner, name):
    L = zx.shape[0]
    nc = L // SSD_CHUNK
    gw = d_inner // SSD_G
    heads = gw // SSD_P
    n_pair = heads // 2
    bc0 = d_inner // LANES
    dt0 = (2 * d_inner + 2 * SSD_G * SSD_N) // LANES

    def body(dyn_ref, y_ref, z_ref, xs_ref, b_ref, c_ref, dt_ref, prev_ref, bias_ref, alog_ref, dl_ref, nw_ref,
             dz_ref, dxs_ref, db_ref, dc_ref, ddt_ref, dbias_ref, dalog_ref, dd_ref, dnw_ref,
             ds_ref, cst_ref, racc_ref):
        @pl.when(pl.program_id(1) == 0)
        def _():
            ds_ref[...] = jnp.zeros_like(ds_ref)
            dbias_ref[...] = jnp.zeros_like(dbias_ref)
            dalog_ref[...] = jnp.zeros_like(dalog_ref)
            dd_ref[...] = jnp.zeros_like(dd_ref)
            dnw_ref[...] = jnp.zeros_like(dnw_ref)

        q = _ssd_common(dt_ref, bias_ref, alog_ref, b_ref, c_ref, cst_ref, heads)
        lane, row, lo = q["lane"], q["row"], q["lo"]
        lane1 = lax.broadcasted_iota(jnp.int32, (1, LANES), 1)

        yv, zv, dynv, nwv = y_ref[...], z_ref[...], dyn_ref[...], nw_ref[...]
        sig = _sigmoid(zv)
        sz = zv * sig
        yg = yv * sz
        rstd = lax.rsqrt(jnp.mean(yg * yg, axis=-1, keepdims=True) + NORM_EPS)
        yhat = yg * rstd
        dnw_ref[...] += jnp.sum(dynv * yhat, axis=0, keepdims=True)
        dyh = dynv * nwv
        dyg = rstd * (dyh - yhat * jnp.mean(dyh * yhat, axis=-1, keepdims=True))
        dz_ref[...] = (dyg * yv * (sig * (1.0 + zv * (1.0 - sig)))).astype(BF16)
        dy_all = dyg * sz

        dg = jnp.zeros((SSD_CHUNK, SSD_CHUNK), F32)
        dc_acc = jnp.zeros((SSD_CHUNK, SSD_N), F32)
        db_acc = jnp.zeros((SSD_CHUNK, SSD_N), F32)
        dcs_mat = jnp.zeros((SSD_CHUNK, LANES), F32)
        ddt_mat = jnp.zeros((SSD_CHUNK, LANES), F32)
        dd_row = jnp.zeros((1, LANES), F32)
        racc_ref[...] = jnp.zeros_like(racc_ref)
        is_last = row == SSD_CHUNK - 1

        for j in range(n_pair):
            sl = slice(j * LANES, (j + 1) * LANES)
            ha, hb = 2 * j, 2 * j + 1
            p = _pair_terms(q, cst_ref, j)
            xs_p = xs_ref[:, sl]
            dyp = dy_all[:, sl]
            xp = xs_p * p["dt_b"]
            xb = xp.astype(BF16)
            s_p = prev_ref[:, sl]
            s_pb = s_p.astype(BF16)
            dsn = ds_ref[:, sl]
            dsnb = dsn.astype(BF16)
            m_f = [q["scores"] * p["lm"][0], q["scores"] * p["lm"][1]]

            t0 = dyp * xs_p
            dd_row = dd_row + jnp.where(lane1 == ha, _sum_all(jnp.where(lo, t0, 0.0)), 0.0) \
                + jnp.where(lane1 == hb, _sum_all(jnp.where(lo, 0.0, t0)), 0.0)
            dxs_p = dl_ref[:, sl] * dyp

            yo = _dot(q["cb"], s_pb) * p["e_b"]
            dcs_b = (dyp * p["e_b"]).astype(BF16)
            dc_acc = dc_acc + _dot(dcs_b, s_pb, "nt")
            ds_yo = _dot(q["cb"], dcs_b, "tn")
            t1 = dyp * yo
            dcs_cols = [jnp.sum(jnp.where(lo, t1, 0.0), axis=1, keepdims=True),
                        jnp.sum(jnp.where(lo, 0.0, t1), axis=1, keepdims=True)]

            t2 = dsn * s_p
            dlast = [p["explast"][0] * _sum_all(jnp.where(lo, t2, 0.0)),
                     p["explast"][1] * _sum_all(jnp.where(lo, 0.0, t2))]
            ds_ref[:, sl] = dsn * p["explast_b"] + ds_yo
            w = _dot(q["bb"], dsnb)
            db_acc = db_acc + _dot((xp * p["dec_b"]).astype(BF16), dsnb, "nt")
            dxp = w * p["dec_b"]
            t3 = w * xp
            e = [jnp.sum(jnp.where(lo, t3, 0.0), axis=1, keepdims=True) * p["dec_cols"][0],
                 jnp.sum(jnp.where(lo, 0.0, t3), axis=1, keepdims=True) * p["dec_cols"][1]]
            for i in range(2):
                dlast[i] = dlast[i] + jnp.sum(e[i], axis=0, keepdims=True)
                dcs_cols[i] = dcs_cols[i] - e[i]

            dyb = dyp.astype(BF16)
            dy_h = [jnp.where(lo, dyp, 0.0).astype(BF16), jnp.where(lo, 0.0, dyp).astype(BF16)]
            dms = [_dot(dy_h[0], xb, "nt"), _dot(dy_h[1], xb, "nt")]
            dxp = dxp + jnp.where(lo, _dot(m_f[0].astype(BF16), dyb, "tn"), _dot(m_f[1].astype(BF16), dyb, "tn"))
            for i, h in enumerate((ha, hb)):
                dg = dg + dms[i] * p["lm"][i]
                qm = dms[i] * m_f[i]
                dcs_cols[i] = dcs_cols[i] + jnp.sum(qm, axis=1, keepdims=True)
                racc_ref[h:h + 1, :] = jnp.sum(qm, axis=0, keepdims=True)

            dxs_ref[:, sl] = dxs_p + dxp * p["dt_b"]
            t4 = dxp * xs_p
            ddt_cols = [jnp.sum(jnp.where(lo, t4, 0.0), axis=1, keepdims=True),
                        jnp.sum(jnp.where(lo, 0.0, t4), axis=1, keepdims=True)]
            for i, h in enumerate((ha, hb)):
                sel = lane == h
                dcs_mat = dcs_mat + jnp.where(sel, dcs_cols[i], 0.0) + jnp.where(sel & is_last, dlast[i], 0.0)
                ddt_mat = ddt_mat + jnp.where(sel, ddt_cols[i], 0.0)

        dcs_mat = dcs_mat - racc_ref[...].T
        tri_t = (row <= lane).astype(BF16)
        da = _dot_exact01(tri_t, dcs_mat)
        ddt = ddt_mat + da * q["a_row"]
        dalog_ref[...] += jnp.sum(jnp.where(q["valid"], da * q["dt"], 0.0), axis=0, keepdims=True) * q["a_row"]
        draw = jnp.where(q["valid"], ddt * _sigmoid(q["raw"]), 0.0)
        ddt_ref[...] = draw
        dbias_ref[...] += jnp.sum(draw, axis=0, keepdims=True)
        dd_ref[...] += dd_row
        dgb = dg.astype(BF16)
        dc_ref[...] = dc_acc + _dot(dgb, q["bb"])
        db_ref[...] = db_acc + _dot(dgb, q["cb"], "tn")

    rev = lambda c: nc - 1 - c
    grp = lambda width: pl.BlockSpec((None, 1, width), lambda g, c: (g, 0, 0))
    blk = lambda width, off: pl.BlockSpec((SSD_CHUNK, width), lambda g, c: (rev(c), off + g))
    return _pcall(
        body, name=name, grid=(SSD_G, nc),
        out_shape=[jax.ShapeDtypeStruct((L, d_inner), BF16), jax.ShapeDtypeStruct((L, d_inner), F32),
                   jax.ShapeDtypeStruct((L, SSD_G * SSD_N), F32), jax.ShapeDtypeStruct((L, SSD_G * SSD_N), F32),
                   jax.ShapeDtypeStruct((L, SSD_G * LANES), F32),
                   jax.ShapeDtypeStruct((SSD_G, 1, LANES), F32), jax.ShapeDtypeStruct((SSD_G, 1, LANES), F32),
                   jax.ShapeDtypeStruct((SSD_G, 1, LANES), F32), jax.ShapeDtypeStruct((SSD_G, 1, gw), F32)],
        in_specs=[blk(gw, 0), blk(gw, 0), blk(gw, 0), blk(gw, 0), blk(SSD_N, bc0), blk(SSD_N, bc0 + SSD_G),
                  blk(LANES, dt0),
                  pl.BlockSpec((None, None, SSD_N, gw), lambda g, c: (rev(c), g, 0, 0)),
                  grp(LANES), grp(LANES), grp(gw), grp(gw)],
        out_specs=[blk(gw, 0), blk(gw, 0), blk(SSD_N, 0), blk(SSD_N, 0), blk(LANES, 0),
                   grp(LANES), grp(LANES), grp(LANES), grp(gw)],
        scratch_shapes=[pltpu.VMEM((SSD_N, gw), F32), pltpu.VMEM((SSD_CHUNK, LANES), F32),
                        pltpu.VMEM((SSD_CHUNK, LANES), F32)],
        compiler_params=_cparams(("parallel", "arbitrary")))(
            dyn, y, zx, xc, xc, xc, zx, prev, bias_p, alog_p, d_lane, nw)


def _cond_mod(c_pad, ada_w, ada_b_loc, name):
    depth, D, n = ada_w.shape
    rows = c_pad.shape[0]

    def body(c_ref, w_ref, b_ref, mod_ref, cond_ref):
        cv = c_ref[...]
        cond = cv * _sigmoid(cv)
        cond_ref[...] = cond
        mod_ref[...] = _dot(cond.astype(BF16), w_ref[...].astype(BF16)) + b_ref[...]

    return _pcall(
        body, name=name, grid=(depth,),
        out_shape=[jax.ShapeDtypeStruct((depth, rows, n), F32), jax.ShapeDtypeStruct((rows, D), F32)],
        in_specs=[pl.BlockSpec((rows, D), lambda i: (0, 0)),
                  pl.BlockSpec((None, D, n), lambda i: (i, 0, 0)),
                  pl.BlockSpec((None, 1, n), lambda i: (i, 0, 0))],
        out_specs=[pl.BlockSpec((None, rows, n), lambda i: (i, 0, 0)),
                   pl.BlockSpec((rows, D), lambda i: (0, 0))],
        compiler_params=_cparams(("arbitrary",)))(c_pad, ada_w, ada_b_loc)


def _adamw_math(g, w, m, v):
    m_new = ADAM_B1 * m + (1.0 - ADAM_B1) * g
    v_new = ADAM_B2 * v + (1.0 - ADAM_B2) * (g * g)
    m_hat = m_new / (1.0 - ADAM_B1 ** ADAM_STEP)
    v_hat = v_new / (1.0 - ADAM_B2 ** ADAM_STEP)
    delta = -ADAM_LR * (m_hat / (jnp.sqrt(v_hat) + ADAM_EPS) + ADAM_WD * w)
    return delta, m_new, v_new


def _adamw_sum(parts, w, m, v, name, tr=256):
    R, C = w.shape
    tr = _tile(R, tr)

    def body(p_ref, w_ref, m_ref, v_ref, g_ref, d_ref, mo_ref, vo_ref):
        g = p_ref[0].astype(F32)
        for k in range(1, N_DEV):
            g = g + p_ref[k].astype(F32)
        d, mn, vn = _adamw_math(g, w_ref[...], m_ref[...], v_ref[...])
        g_ref[...] = g
        d_ref[...] = d
        mo_ref[...] = mn
        vo_ref[...] = vn

    blk = pl.BlockSpec((tr, C), lambda i: (i, 0))
    return _pcall(
        body, name=name, grid=(R // tr,),
        out_shape=[jax.ShapeDtypeStruct((R, C), F32)] * 4,
        in_specs=[pl.BlockSpec((N_DEV, tr, C), lambda i: (0, i, 0)), blk, blk, blk],
        out_specs=[blk] * 4, compiler_params=_cparams(("parallel",)))(parts, w, m, v)


def _ada_adamw(cond_pad, dmod_pad, w, m, v, name, tr=256):
    depth, D, n = w.shape
    rows = cond_pad.shape[0]
    tr = _tile(D, tr)

    def body(c_ref, dm_ref, w_ref, m_ref, v_ref, g_ref, d_ref, mo_ref, vo_ref):
        g = _dot(c_ref[...].astype(BF16), dm_ref[...].astype(BF16), "tn")
        d, mn, vn = _adamw_math(g, w_ref[...], m_ref[...], v_ref[...])
        g_ref[...] = g
        d_ref[...] = d
        mo_ref[...] = mn
        vo_ref[...] = vn

    blk = pl.BlockSpec((None, tr, n), lambda i, r: (i, r, 0))
    return _pcall(
        body, name=name, grid=(depth, D // tr),
        out_shape=[jax.ShapeDtypeStruct((depth, D, n), F32)] * 4,
        in_specs=[pl.BlockSpec((rows, tr), lambda i, r: (0, r)),
                  pl.BlockSpec((None, rows, n), lambda i, r: (i, 0, 0)), blk, blk, blk],
        out_specs=[blk] * 4, compiler_params=_cparams(("parallel", "parallel")))(cond_pad, dmod_pad, w, m, v)


def _pad_heads(v, heads_per_group):
    lead = v.shape[:-1]
    v = v.reshape(lead + (SSD_G, heads_per_group))
    v = jnp.pad(v, [(0, 0)] * len(lead) + [(0, 0), (0, LANES - heads_per_group)])
    return v.reshape(lead + (SSD_G * LANES,))


def _unpad_heads(v, heads_per_group):
    lead = v.shape[:-1]
    v = v.reshape(lead + (SSD_G, LANES))[..., :heads_per_group]
    return v.reshape(lead + (SSD_G * heads_per_group,))


def kernel(x, c, ada_w, ada_b, mix_norm_w, mlp_norm_w, mlp_up, mlp_down, ssd_in_w, ssd_conv_w, ssd_conv_b, ssd_dt_bias, ssd_A_log, ssd_D, ssd_norm_w, ssd_out_w, sc_in_w, sc_conv_w, sc_out_w, final_norm_w, loss_target, m_ada_w, m_ada_b, m_mix_norm_w, m_mlp_norm_w, m_mlp_up, m_mlp_down, m_ssd_in_w, m_ssd_conv_w, m_ssd_conv_b, m_ssd_dt_bias, m_ssd_A_log, m_ssd_D, m_ssd_norm_w, m_ssd_out_w, m_sc_in_w, m_sc_conv_w, m_sc_out_w, m_final_norm_w, v_ada_w, v_ada_b, v_mix_norm_w, v_mlp_norm_w, v_mlp_up, v_mlp_down, v_ssd_in_w, v_ssd_conv_w, v_ssd_conv_b, v_ssd_dt_bias, v_ssd_A_log, v_ssd_D, v_ssd_norm_w, v_ssd_out_w, v_sc_in_w, v_sc_conv_w, v_sc_out_w, v_final_norm_w):
    weights = dict(ada_w=ada_w, ada_b=ada_b, mix_norm_w=mix_norm_w, mlp_norm_w=mlp_norm_w, mlp_up=mlp_up,
                   mlp_down=mlp_down, ssd_in_w=ssd_in_w, ssd_conv_w=ssd_conv_w, ssd_conv_b=ssd_conv_b,
                   ssd_dt_bias=ssd_dt_bias, ssd_A_log=ssd_A_log, ssd_D=ssd_D, ssd_norm_w=ssd_norm_w,
                   ssd_out_w=ssd_out_w, sc_in_w=sc_in_w, sc_conv_w=sc_conv_w, sc_out_w=sc_out_w,
                   final_norm_w=final_norm_w)
    moms = dict(ada_w=m_ada_w, ada_b=m_ada_b, mix_norm_w=m_mix_norm_w, mlp_norm_w=m_mlp_norm_w, mlp_up=m_mlp_up,
                mlp_down=m_mlp_down, ssd_in_w=m_ssd_in_w, ssd_conv_w=m_ssd_conv_w, ssd_conv_b=m_ssd_conv_b,
                ssd_dt_bias=m_ssd_dt_bias, ssd_A_log=m_ssd_A_log, ssd_D=m_ssd_D, ssd_norm_w=m_ssd_norm_w,
                ssd_out_w=m_ssd_out_w, sc_in_w=m_sc_in_w, sc_conv_w=m_sc_conv_w, sc_out_w=m_sc_out_w,
                final_norm_w=m_final_norm_w)
    vars_ = dict(ada_w=v_ada_w, ada_b=v_ada_b, mix_norm_w=v_mix_norm_w, mlp_norm_w=v_mlp_norm_w, mlp_up=v_mlp_up,
                 mlp_down=v_mlp_down, ssd_in_w=v_ssd_in_w, ssd_conv_w=v_ssd_conv_w, ssd_conv_b=v_ssd_conv_b,
                 ssd_dt_bias=v_ssd_dt_bias, ssd_A_log=v_ssd_A_log, ssd_D=v_ssd_D, ssd_norm_w=v_ssd_norm_w,
                 ssd_out_w=v_ssd_out_w, sc_in_w=v_sc_in_w, sc_conv_w=v_sc_conv_w, sc_out_w=v_sc_out_w,
                 final_norm_w=v_final_norm_w)
    names = list(weights)

    L, D = x.shape[1], x.shape[2]
    d_inner = 2 * D
    n_heads = d_inner // SSD_P
    hpg = n_heads // SSD_G
    gw = d_inner // SSD_G
    conv_dim = d_inner + 2 * SSD_G * SSD_N
    zx_dim = d_inner + conv_dim
    me = _my_index()
    x0 = x[0]
    tgt = loss_target[0]

    to_gather = [c, mlp_up[0].astype(BF16), mlp_up[1].astype(BF16), mlp_down[0].astype(BF16),
                 mlp_down[1].astype(BF16), ssd_in_w[0].astype(BF16), ssd_out_w[0].astype(BF16),
                 sc_in_w[0].astype(BF16), sc_out_w[0].astype(BF16), ssd_conv_w[0], sc_conv_w[0]]
    c_all, up0, up1, down0, down1, ssd_in_g, ssd_out_g, sc_in_g, sc_out_g, cw_all, scw_all = _exchange(
        to_gather, "gather_weights", gather=True)
    ups, downs = [up0, up1], [down0.reshape(-1, D), down1.reshape(-1, D)]
    w_ssd_out = ssd_out_g.reshape(-1, D)
    w_sc_out = sc_out_g.reshape(-1, D)
    in_dim = ssd_in_g.shape[2] * N_DEV
    w_in_nat = jnp.transpose(ssd_in_g, (1, 0, 2)).reshape(D, in_dim)
    w_in_all = jnp.concatenate([w_in_nat[:, :zx_dim], _pad_heads(w_in_nat[:, zx_dim:], hpg)], axis=1)

    n_mod = ada_w.shape[2]
    c_pad = jnp.pad(c_all.reshape(N_DEV, D), ((0, 16 - N_DEV), (0, 0)))
    ada_b_loc = lax.dynamic_slice_in_dim(ada_b, me * n_mod, n_mod, axis=1).reshape(2, 1, n_mod)
    mod_blk, cond_pad = _cond_mod(c_pad, ada_w, ada_b_loc, "cond_mod")
    (mod_all,) = _exchange([mod_blk], "gather_mod", gather=True)
    mod_mine = lax.dynamic_index_in_dim(mod_all, me, axis=2, keepdims=False)
    mod_mine = jnp.transpose(mod_mine, (1, 0, 2)).reshape(2, 6, 1, D)
    sh_m, sc_m, g_m, sh_f, sc_f, g_f = [[mod_mine[i, k] for i in range(2)] for k in range(6)]

    vec = lambda a: a.reshape(1, -1)
    grads = {}
    small = {}

    _, h0 = _norm_mod_fwd(x0, None, None, vec(mix_norm_w[0]), sc_m[0], sh_m[0], "l0_mix_norm")
    (zx,) = _mm_nn(h0, w_in_all, F32, "ssd_in_proj", tn=512)
    conv_b0 = vec(ssd_conv_b[0])
    conv_w_full = jnp.transpose(cw_all, (1, 0, 2)).reshape(SSD_K, conv_dim)
    sc_conv_full = jnp.transpose(scw_all, (1, 0, 2)).reshape(SC_K, D)
    xc = _ssd_conv_fwd(zx, conv_w_full, conv_b0, d_inner, conv_dim, "ssd_conv")
    bias_p = _pad_heads(ssd_dt_bias[0], hpg).reshape(SSD_G, 1, LANES)
    alog_p = _pad_heads(ssd_A_log[0], hpg).reshape(SSD_G, 1, LANES)
    d_lane = jnp.repeat(ssd_D[0], SSD_P).reshape(SSD_G, 1, gw)
    nw_g = ssd_norm_w[0].reshape(SSD_G, 1, gw)
    y_ssd, yn, prev = _ssd_fwd(zx, xc, bias_p, alog_p, d_lane, nw_g, d_inner, "ssd_scan")
    (mix0,) = _mm_nn(yn, w_ssd_out, F32, "ssd_out_proj")
    x1, h1 = _norm_mod_fwd(x0, mix0, g_m[0], vec(mlp_norm_w[0]), sc_f[0], sh_f[0], "l0_mlp_norm")
    u0, s0 = _mm_nn_blocked(h1, ups[0], "l0_mlp_up", _ep_relu2, [BF16, BF16])
    (d0,) = _mm_nn(s0, downs[0], F32, "l0_mlp_down")
    x2, h2 = _norm_mod_fwd(x1, d0, g_f[0], vec(mix_norm_w[1]), sc_m[1], sh_m[1], "l1_mix_norm")
    (proj,) = _mm_nn_blocked(h2, sc_in_g, "sc_in_proj", _ep_store(F32), [F32])
    yc = _sc_conv_fwd(proj, sc_conv_full, "sc_conv")
    (mix1,) = _mm_nn(yc, w_sc_out, F32, "sc_out_proj")
    x3, h3 = _norm_mod_fwd(x2, mix1, g_m[1], vec(mlp_norm_w[1]), sc_f[1], sh_f[1], "l1_mlp_norm")
    u1, s1 = _mm_nn_blocked(h3, ups[1], "l1_mlp_up", _ep_relu2, [BF16, BF16])
    (d1,) = _mm_nn(s1, downs[1], F32, "l1_mlp_down")

    dx, loss_lane, dfw = _final_loss(x3, d1, g_f[1], vec(final_norm_w), tgt, "final_loss")
    loss = lax.psum(0.5 * jnp.sum(loss_lane) / D, MESH_AXES)
    small["final_norm_w"] = dfw

    dmod = [[None] * 6 for _ in range(2)]
    big = {}

    def mlp_backward(i, dx_out, d_out, x_mid, h_in, u, s):
        dd, dg = _gate_bwd(dx_out, d_out, g_f[i], f"l{i}_mlp_gate_bwd")
        dmod[i][5] = dg
        du = _mm_nt(dd, downs[i], BF16, f"l{i}_mlp_down_bwd", epilogue=_ep_relu2_bwd, extra=(u,))
        gdown = _mm_tn(s, dd, BF16, f"l{i}_mlp_down_wgrad")
        gup = _mm_tn_blocked(h_in, du, BF16, f"l{i}_mlp_up_wgrad")
        dh = _mm_nt_blocked(du, ups[i], F32, f"l{i}_mlp_up_bwd")
        dxm, dsh, dsc, dnw = _norm_mod_bwd(dh, x_mid, vec(mlp_norm_w[i]), sc_f[i], dx_out, f"l{i}_mlp_norm_bwd")
        dmod[i][3], dmod[i][4] = dsh, dsc
        return dxm, gdown.reshape(N_DEV, -1, D), gup, dnw

    dx3, gdown1, gup1, dnw_mlp1 = mlp_backward(1, dx, d1, x3, h3, u1, s1)
    dyc, dg = _gate_bwd(dx3, mix1, g_m[1], "l1_mix_gate_bwd")
    dmod[1][2] = dg
    big["sc_out_w"] = _mm_tn(yc, dyc, BF16, "sc_out_wgrad").reshape(N_DEV, -1, D)
    dconv_out = _mm_nt(dyc, w_sc_out, F32, "sc_out_bwd")
    dbg, dcg, dxv, dscw = _sc_conv_bwd(proj, sc_conv_full, dconv_out, "sc_conv_bwd")
    dproj = jnp.concatenate([dbg, dcg, dxv], axis=1)
    big["sc_in_w"] = _mm_tn_blocked(h2, dproj, BF16, "sc_in_wgrad")
    dh2 = _mm_nt_blocked(dproj, sc_in_g, F32, "sc_in_bwd")
    dx2, dsh, dsc, dnw_mix1 = _norm_mod_bwd(dh2, x2, vec(mix_norm_w[1]), sc_m[1], dx3, "l1_mix_norm_bwd")
    dmod[1][0], dmod[1][1] = dsh, dsc
    dx1, gdown0, gup0, dnw_mlp0 = mlp_backward(0, dx2, d0, x1, h1, u0, s0)
    dyo, dg = _gate_bwd(dx1, mix0, g_m[0], "l0_mix_gate_bwd")
    dmod[0][2] = dg
    big["ssd_out_w"] = _mm_tn(yn, dyo, BF16, "ssd_out_wgrad").reshape(N_DEV, -1, D)
    dyn = _mm_nt(dyo, w_ssd_out, F32, "ssd_out_bwd")
    dz, dxs, db_, dc_, ddt, dbias, dalog, dd_, dnw_ssd = _ssd_bwd(
        dyn, y_ssd, zx, xc, prev, bias_p, alog_p, d_lane, nw_g, d_inner, "ssd_scan_bwd")
    dxc = jnp.concatenate([dxs, db_, dc_], axis=1)
    dxbc, dcw, dcb = _ssd_conv_bwd(zx, conv_w_full, conv_b0, dxc, d_inner, "ssd_conv_bwd")
    dzx = jnp.concatenate([dz, dxbc, ddt.astype(BF16)], axis=1)
    g_in_all = _mm_tn(h0, dzx, BF16, "ssd_in_wgrad")
    g_in_nat = jnp.concatenate([g_in_all[:, :zx_dim], _unpad_heads(g_in_all[:, zx_dim:], hpg)], axis=1)
    big["ssd_in_w"] = jnp.transpose(g_in_nat.reshape(D, N_DEV, in_dim // N_DEV), (1, 0, 2))
    dh0 = _mm_nt(dzx, w_in_all, F32, "ssd_in_bwd")
    grad_x, dsh, dsc, dnw_mix0 = _norm_mod_bwd(dh0, x0, vec(mix_norm_w[0]), sc_m[0], dx1, "l0_mix_norm_bwd")
    dmod[0][0], dmod[0][1] = dsh, dsc

    small["mix_norm_w"] = jnp.concatenate([dnw_mix0, dnw_mix1], axis=0)
    small["mlp_norm_w"] = jnp.concatenate([dnw_mlp0, dnw_mlp1], axis=0)
    small["ssd_conv_w"] = dcw
    small["ssd_conv_b"] = dcb
    small["ssd_dt_bias"] = _unpad_heads(dbias.reshape(SSD_G * LANES), hpg)
    small["ssd_A_log"] = _unpad_heads(dalog.reshape(SSD_G * LANES), hpg)
    small["ssd_D"] = _unpad_heads(dd_.reshape(SSD_G * LANES), hpg)
    small["ssd_norm_w"] = dnw_ssd
    small["sc_conv_w"] = dscw
    small["dmod"] = jnp.concatenate([jnp.concatenate(dmod[i], axis=1) for i in range(2)], axis=0)

    big_names = ["mlp_up", "mlp_down", "ssd_in_w", "ssd_out_w", "sc_in_w", "sc_out_w"]
    send = [gup0, gup1, gdown0, gdown1, big["ssd_in_w"], big["ssd_out_w"], big["sc_in_w"], big["sc_out_w"]]
    r_up0, r_up1, r_down0, r_down1, r_ssd_in, r_ssd_out, r_sc_in, r_sc_out = _exchange(
        send, "exchange_grads", gather=False)

    small_order = ["dmod", "mix_norm_w", "mlp_norm_w", "ssd_conv_w", "ssd_conv_b", "ssd_dt_bias", "ssd_A_log",
                   "ssd_D", "ssd_norm_w", "sc_conv_w", "final_norm_w"]
    flat = jnp.concatenate([small[k].reshape(-1) for k in small_order])
    n_small = flat.shape[0]
    n_small_pad = -(-n_small // 1024) * 1024
    flat = jnp.pad(flat, (0, n_small_pad - n_small)).reshape(n_small_pad // LANES, LANES)
    (small_all,) = _exchange([flat], "gather_small_grads", gather=True)
    small_all = small_all.reshape(N_DEV, n_small_pad)
    offs, o = {}, 0
    for k in small_order:
        offs[k] = (o, small[k].size, small[k].shape)
        o += small[k].size

    def small_parts(k):
        o, n, shape = offs[k]
        return small_all[:, o:o + n].reshape((N_DEV,) + shape)

    out_g, out_d, out_m, out_v = {}, {}, {}, {}

    def big_update(name, parts_per_layer):
        w_, m_, v_ = weights[name], moms[name], vars_[name]
        res = []
        for i, parts in enumerate(parts_per_layer):
            res.append(_adamw_sum(parts, w_[i], m_[i], v_[i], f"adamw_{name}_{i}"))
        for k, dst in enumerate((out_g, out_d, out_m, out_v)):
            dst[name] = jnp.stack([r[k] for r in res], axis=0)

    big_update("mlp_up", [r_up0, r_up1])
    big_update("mlp_down", [r_down0, r_down1])
    big_update("ssd_in_w", [r_ssd_in])
    big_update("ssd_out_w", [r_ssd_out])
    big_update("sc_in_w", [r_sc_in])
    big_update("sc_out_w", [r_sc_out])

    dmod_all = small_parts("dmod")
    dmod_loc = lax.dynamic_slice_in_dim(dmod_all, me * n_mod, n_mod, axis=2)
    dmod_pad = jnp.pad(jnp.transpose(dmod_loc, (1, 0, 2)), ((0, 0), (0, 16 - N_DEV), (0, 0)))
    out_g["ada_w"], out_d["ada_w"], out_m["ada_w"], out_v["ada_w"] = _ada_adamw(
        cond_pad, dmod_pad, ada_w, m_ada_w, v_ada_w, "adamw_ada_w")

    pieces = []
    pieces.append(("ada_b", dmod_all.reshape(N_DEV, -1)))
    for k in ["mix_norm_w", "mlp_norm_w", "ssd_conv_b", "ssd_dt_bias", "ssd_A_log", "ssd_D", "ssd_norm_w",
              "final_norm_w"]:
        pieces.append((k, small_parts(k).reshape(N_DEV, -1)))
    n_cw = ssd_conv_w.shape[2]
    pieces.append(("ssd_conv_w", lax.dynamic_slice_in_dim(small_parts("ssd_conv_w"), me * n_cw, n_cw, axis=2)
                   .reshape(N_DEV, -1)))
    n_scw = sc_conv_w.shape[2]
    pieces.append(("sc_conv_w", lax.dynamic_slice_in_dim(small_parts("sc_conv_w"), me * n_scw, n_scw, axis=2)
                   .reshape(N_DEV, -1)))
    n_tot = sum(p.shape[1] for _, p in pieces)
    n_tot_pad = -(-n_tot // 1024) * 1024

    def pack(arrs, lead=()):
        f = jnp.concatenate(arrs, axis=-1)
        f = jnp.pad(f, [(0, 0)] * len(lead) + [(0, n_tot_pad - n_tot)])
        return f.reshape(lead + (n_tot_pad // LANES, LANES))

    parts_flat = pack([p for _, p in pieces], lead=(N_DEV,))
    w_flat = pack([weights[k].reshape(-1) for k, _ in pieces])
    m_flat = pack([moms[k].reshape(-1) for k, _ in pieces])
    v_flat = pack([vars_[k].reshape(-1) for k, _ in pieces])
    res = _adamw_sum(parts_flat, w_flat, m_flat, v_flat, "adamw_small", tr=n_tot_pad // LANES)
    o = 0
    for k, p in pieces:
        n = p.shape[1]
        for r, dst in zip(res, (out_g, out_d, out_m, out_v)):
            dst[k] = r.reshape(-1)[o:o + n].reshape(weights[k].shape)
        o += n

    return (loss, grad_x[None], *[out_g[k] for k in names], *[out_d[k] for k in names],
            *[out_m[k] for k in names], *[out_v[k] for k in names])
```

```python
import functools

import jax
import jax.numpy as jnp
from jax import lax
from jax.experimental import pallas as pl
from jax.experimental.pallas import tpu as pltpu

F32 = jnp.float32
BF16 = jnp.bfloat16
N_DEV = 8
MESH_AXES = ("x", "y", "c")
MESH = pl.DeviceIdType.MESH

NORM_EPS = 1e-5
SSD_G = 4
SSD_P = 64
SSD_N = 128
SSD_CHUNK = 128
SSD_K = 4
SC_K = 3
LANES = 128

ADAM_LR = 0.001
ADAM_B1 = 0.9
ADAM_B2 = 0.999
ADAM_EPS = 1e-08
ADAM_WD = 0.01
ADAM_STEP = 10

VMEM_LIMIT = 56 * 1024 * 1024


def _pcall(body, **kw):
    return pl.pallas_call(body, **kw)


def _cparams(sem=None):
    if sem is None:
        return pltpu.CompilerParams(vmem_limit_bytes=VMEM_LIMIT)
    return pltpu.CompilerParams(dimension_semantics=sem, vmem_limit_bytes=VMEM_LIMIT)


def _my_index():
    return 4 * lax.axis_index("x") + 2 * lax.axis_index("y") + lax.axis_index("c")


_PEER_MASKS = [(0, 0, 1), (0, 1, 0), (0, 1, 1), (1, 0, 0), (1, 0, 1), (1, 1, 0), (1, 1, 1)]


def _peers():
    x, y, c = lax.axis_index("x"), lax.axis_index("y"), lax.axis_index("c")
    out = []
    for mx, my, mc in _PEER_MASKS:
        px = (1 - x) if mx else x
        py = (1 - y) if my else y
        pc = (1 - c) if mc else c
        out.append(((px, py, pc), 4 * px + 2 * py + pc))
    return out


def _exchange(arrs, name, gather):
    n = len(arrs)
    n_peer = N_DEV - 1

    def body(*refs):
        ins, outs = refs[:n], refs[n:2 * n]
        send_sems, recv_sems, local_sems = refs[2 * n:]
        me = _my_index()
        peers = _peers()
        started = []
        for a in range(n):
            src_own = ins[a] if gather else ins[a].at[me]
            own = pltpu.make_async_copy(src_own, outs[a].at[me], local_sems.at[a])
            own.start()
            started.append(own)
        sends = []
        for a in range(n):
            for k, (peer, pidx) in enumerate(peers):
                src = ins[a] if gather else ins[a].at[pidx]
                cp = pltpu.make_async_remote_copy(
                    src_ref=src, dst_ref=outs[a].at[me],
                    send_sem=send_sems.at[a * n_peer + k], recv_sem=recv_sems.at[a * n_peer + k],
                    device_id=peer, device_id_type=MESH)
                cp.start()
                sends.append(cp)
        for a in range(n):
            for k, (peer, pidx) in enumerate(peers):
                src = ins[a] if gather else ins[a].at[pidx]
                pltpu.make_async_remote_copy(
                    src_ref=src, dst_ref=outs[a].at[pidx],
                    send_sem=send_sems.at[a * n_peer + k], recv_sem=recv_sems.at[a * n_peer + k],
                    device_id=peer, device_id_type=MESH).wait_recv()
        for cp in sends:
            cp.wait_send()
        for own in started:
            own.wait()

    if gather:
        out_shape = [jax.ShapeDtypeStruct((N_DEV,) + a.shape, a.dtype) for a in arrs]
    else:
        out_shape = [jax.ShapeDtypeStruct(a.shape, a.dtype) for a in arrs]
    any_spec = pl.BlockSpec(memory_space=pl.ANY)
    outs = _pcall(
        body, name=name, out_shape=out_shape,
        in_specs=[any_spec] * n, out_specs=[any_spec] * n,
        scratch_shapes=[pltpu.SemaphoreType.DMA((n * n_peer,)), pltpu.SemaphoreType.DMA((n * n_peer,)),
                        pltpu.SemaphoreType.DMA((n,))],
        compiler_params=pltpu.CompilerParams(has_side_effects=True),
    )(*arrs)
    return list(outs)


_HBM = pl.BlockSpec(memory_space=pltpu.HBM)
_SEM = pl.BlockSpec(memory_space=pltpu.SEMAPHORE)
_DATAFLOW = pltpu.SideEffectType.DATAFLOW_SIDE_EFFECTING


def _xfer_start(arrs, name, gather):
    n = len(arrs)
    n_peer = N_DEV - 1

    def body(*refs):
        ins, lands = refs[:n], refs[n:2 * n]
        sems = refs[2 * n:5 * n]
        token = refs[-1]
        me = _my_index()
        peers = _peers()
        for a in range(n):
            send_sems, recv_sems, loc_sem = sems[3 * a:3 * a + 3]
            src_own = ins[a] if gather else ins[a].at[me]
            pltpu.make_async_copy(src_own, lands[a].at[me], loc_sem).start()
            for k, (peer, pidx) in enumerate(peers):
                src = ins[a] if gather else ins[a].at[pidx]
                pltpu.make_async_remote_copy(
                    src_ref=src, dst_ref=lands[a].at[me], send_sem=send_sems.at[k], recv_sem=recv_sems.at[k],
                    device_id=peer, device_id_type=MESH).start()
        token[...] = jnp.zeros_like(token)

    land_shapes = [((N_DEV,) + a.shape) if gather else a.shape for a in arrs]
    out_shape, out_specs = [], []
    for _ in range(n):
        out_shape += [pltpu.SemaphoreType.DMA((n_peer,)), pltpu.SemaphoreType.DMA((n_peer,)),
                      pltpu.SemaphoreType.DMA(())]
        out_specs += [_SEM, _SEM, _SEM]
    out_shape += [pltpu.HBM(a.shape, a.dtype) for a in arrs]
    out_shape += [pltpu.HBM(s, a.dtype) for s, a in zip(land_shapes, arrs)]
    out_shape += [jax.ShapeDtypeStruct((8, LANES), F32)]
    out_specs += [_HBM] * (2 * n) + [pl.BlockSpec(memory_space=pltpu.VMEM)]
    aliases = {}
    for a in range(n):
        aliases[a] = 3 * n + a
        aliases[n + a] = 4 * n + a
    operands = [pltpu.with_memory_space_constraint(a, pltpu.HBM) for a in arrs]
    operands += [pltpu.with_memory_space_constraint(lax.empty(s, a.dtype), pltpu.HBM)
                 for s, a in zip(land_shapes, arrs)]
    outs = _pcall(
        body, name=name, out_shape=tuple(out_shape), in_specs=[_HBM] * (2 * n), out_specs=tuple(out_specs),
        input_output_aliases=aliases,
        compiler_params=pltpu.CompilerParams(has_side_effects=_DATAFLOW),
    )(*operands)
    handles = []
    for a in range(n):
        handles.append((outs[3 * n + a], outs[4 * n + a], outs[3 * a], outs[3 * a + 1], outs[3 * a + 2]))
    return handles, outs[-1]


def _xfer_wait(handle, after, name, gather):
    src_thru, land_thru, send_sems, recv_sems, loc_sem = handle

    def body(src_ref, land_ref, send_ref, recv_ref, loc_ref, after_ref, src_dead, got_ref):
        me = _my_index()
        src_own = src_ref if gather else src_ref.at[me]
        pltpu.make_async_copy(src_own, land_ref.at[me], loc_ref).wait()
        for k, (peer, pidx) in enumerate(_peers()):
            src = src_ref if gather else src_ref.at[pidx]
            cp = pltpu.make_async_remote_copy(
                src_ref=src, dst_ref=land_ref.at[pidx], send_sem=send_ref.at[k], recv_sem=recv_ref.at[k],
                device_id=peer, device_id_type=MESH)
            cp.wait_send()
            cp.wait_recv()

    return _pcall(
        body, name=name,
        out_shape=(pltpu.HBM(src_thru.shape, src_thru.dtype), pltpu.HBM(land_thru.shape, land_thru.dtype)),
        in_specs=[_HBM, _HBM, _SEM, _SEM, _SEM, pl.BlockSpec(memory_space=pl.ANY)], out_specs=(_HBM, _HBM),
        input_output_aliases={0: 0, 1: 1},
        compiler_params=pltpu.CompilerParams(has_side_effects=_DATAFLOW),
    )(src_thru, land_thru, send_sems, recv_sems, loc_sem, after)[1]


def _tie(v, token):
    return v + token[0:1, 0:1].reshape((1,) * v.ndim)


_DIMS = {"nn": (((1,), (0,)), ((), ())), "nt": (((1,), (1,)), ((), ())), "tn": (((0,), (0,)), ((), ()))}


def _dot(a, b, mode="nn"):
    return lax.dot_general(a, b, _DIMS[mode], preferred_element_type=F32)


def _mm(a, b, *, mode, grid, a_spec, b_spec, out_shape, out_specs, acc_shape, epilogue, name,
        extra=(), extra_specs=()):
    nk = grid[2]
    n_extra = len(extra)

    def body(*refs):
        a_ref, b_ref = refs[0], refs[1]
        ex = refs[2:2 + n_extra]
        outs = refs[2 + n_extra:-1]
        acc = refs[-1]
        k = pl.program_id(2)

        @pl.when(k == 0)
        def _():
            acc[...] = jnp.zeros_like(acc)

        acc[...] += _dot(a_ref[...], b_ref[...], mode)

        @pl.when(k == nk - 1)
        def _():
            epilogue(acc[...], ex, outs)

    return _pcall(
        body, name=name, grid=grid, out_shape=out_shape,
        in_specs=[a_spec, b_spec] + list(extra_specs), out_specs=out_specs,
        scratch_shapes=[pltpu.VMEM(acc_shape, F32)],
        compiler_params=_cparams(("parallel", "parallel", "arbitrary")),
    )(a, b, *extra)


def _ep_store(dtype):
    def ep(acc, ex, outs):
        outs[0][...] = acc.astype(dtype)
    return ep


def _ep_relu2(acc, ex, outs):
    outs[0][...] = acc.astype(BF16)
    r = jnp.maximum(acc, 0.0)
    outs[1][...] = (r * r).astype(BF16)


def _ep_relu2_bwd(acc, ex, outs):
    u = ex[0][...].astype(F32)
    outs[0][...] = (acc * (2.0 * jnp.maximum(u, 0.0))).astype(BF16)


def _tile(n, want):
    t = min(n, want)
    while n % t:
        t //= 2
    return t


def _mm_nn(a, w, out_dtype, name, tm=512, tn=512, tk=1024, epilogue=None, n_out=1, out_dtypes=None):
    M, K = a.shape
    N = w.shape[1]
    tm, tn, tk = _tile(M, tm), _tile(N, tn), _tile(K, tk)
    out_dtypes = out_dtypes or [out_dtype]
    return _mm(a, w, mode="nn", grid=(M // tm, N // tn, K // tk),
               a_spec=pl.BlockSpec((tm, tk), lambda i, j, k: (i, k)),
               b_spec=pl.BlockSpec((tk, tn), lambda i, j, k: (k, j)),
               out_shape=[jax.ShapeDtypeStruct((M, N), d) for d in out_dtypes],
               out_specs=[pl.BlockSpec((tm, tn), lambda i, j, k: (i, j)) for _ in out_dtypes],
               acc_shape=(tm, tn), epilogue=epilogue or _ep_store(out_dtype), name=name)


def _mm_nn_blocked(a, wg, name, epilogue, out_dtypes, tm=512):
    M, K = a.shape
    n = wg.shape[2]
    tm = _tile(M, tm)
    return _mm(a, wg, mode="nn", grid=(M // tm, N_DEV, 1),
               a_spec=pl.BlockSpec((tm, K), lambda i, j, k: (i, 0)),
               b_spec=pl.BlockSpec((None, K, n), lambda i, j, k: (j, 0, 0)),
               out_shape=[jax.ShapeDtypeStruct((M, N_DEV * n), d) for d in out_dtypes],
               out_specs=[pl.BlockSpec((tm, n), lambda i, j, k: (i, j)) for _ in out_dtypes],
               acc_shape=(tm, n), epilogue=epilogue, name=name)


def _mm_nt(a, w, out_dtype, name, tm=512, tn=512, tk=1024, epilogue=None, extra=(), extra_specs=()):
    M, K = a.shape
    N = w.shape[0]
    tm, tn, tk = _tile(M, tm), _tile(N, tn), _tile(K, tk)
    if extra and not extra_specs:
        extra_specs = [pl.BlockSpec((tm, tn), lambda i, j, k: (i, j)) for _ in extra]
    return _mm(a, w, mode="nt", grid=(M // tm, N // tn, K // tk),
               a_spec=pl.BlockSpec((tm, tk), lambda i, j, k: (i, k)),
               b_spec=pl.BlockSpec((tn, tk), lambda i, j, k: (j, k)),
               out_shape=[jax.ShapeDtypeStruct((M, N), out_dtype)],
               out_specs=[pl.BlockSpec((tm, tn), lambda i, j, k: (i, j))],
               acc_shape=(tm, tn), epilogue=epilogue or _ep_store(out_dtype), name=name,
               extra=extra, extra_specs=extra_specs)[0]


def _mm_nt_blocked(a, wg, out_dtype, name, tm=512):
    M = a.shape[0]
    kout, n = wg.shape[1], wg.shape[2]
    tm = _tile(M, tm)
    return _mm(a, wg, mode="nt", grid=(M // tm, 1, N_DEV),
               a_spec=pl.BlockSpec((tm, n), lambda i, j, k: (i, k)),
               b_spec=pl.BlockSpec((None, kout, n), lambda i, j, k: (k, 0, 0)),
               out_shape=[jax.ShapeDtypeStruct((M, kout), out_dtype)],
               out_specs=[pl.BlockSpec((tm, kout), lambda i, j, k: (i, 0))],
               acc_shape=(tm, kout), epilogue=_ep_store(out_dtype), name=name)[0]


def _mm_tn(a, b, out_dtype, name, tm=512, tn=512, tk=1024):
    K, M = a.shape
    N = b.shape[1]
    tm, tn, tk = _tile(M, tm), _tile(N, tn), _tile(K, tk)
    return _mm(a, b, mode="tn", grid=(M // tm, N // tn, K // tk),
               a_spec=pl.BlockSpec((tk, tm), lambda i, j, k: (k, i)),
               b_spec=pl.BlockSpec((tk, tn), lambda i, j, k: (k, j)),
               out_shape=[jax.ShapeDtypeStruct((M, N), out_dtype)],
               out_specs=[pl.BlockSpec((tm, tn), lambda i, j, k: (i, j))],
               acc_shape=(tm, tn), epilogue=_ep_store(out_dtype), name=name)[0]


def _mm_tn_blocked(a, b, out_dtype, name, tm=512, tk=1024):
    K, M = a.shape
    n = b.shape[1] // N_DEV
    tm, tk = _tile(M, tm), _tile(K, tk)
    return _mm(a, b, mode="tn", grid=(M // tm, N_DEV, K // tk),
               a_spec=pl.BlockSpec((tk, tm), lambda i, j, k: (k, i)),
               b_spec=pl.BlockSpec((tk, n), lambda i, j, k: (k, j)),
               out_shape=[jax.ShapeDtypeStruct((N_DEV, M, n), out_dtype)],
               out_specs=[pl.BlockSpec((None, tm, n), lambda i, j, k: (j, i, 0))],
               acc_shape=(tm, n), epilogue=_ep_store(out_dtype), name=name)[0]


def _sigmoid(x):
    return 1.0 / (1.0 + jnp.exp(-x))


def _row_spec(tm, d):
    return pl.BlockSpec((tm, d), lambda i: (i, 0))


def _vec_spec(d):
    return pl.BlockSpec((1, d), lambda i: (0, 0))


def _norm_mod_fwd(x, y, gate, nw, scale, shift, name, tm=256):
    L, D = x.shape
    tm = _tile(L, tm)
    has_res = y is not None

    def body(*refs):
        if has_res:
            x_ref, y_ref, g_ref, nw_ref, sc_ref, sh_ref, xo_ref, h_ref = refs
            xn = x_ref[...] + g_ref[...] * y_ref[...]
            xo_ref[...] = xn
        else:
            x_ref, nw_ref, sc_ref, sh_ref, h_ref = refs
            xn = x_ref[...]
        rstd = lax.rsqrt(jnp.mean(xn * xn, axis=-1, keepdims=True) + NORM_EPS)
        h = xn * rstd * nw_ref[...] * (1.0 + sc_ref[...]) + sh_ref[...]
        h_ref[...] = h.astype(BF16)

    row, vec = _row_spec(tm, D), _vec_spec(D)
    if has_res:
        ins, in_specs = (x, y, gate, nw, scale, shift), [row, row, vec, vec, vec, vec]
        out_shape = [jax.ShapeDtypeStruct((L, D), F32), jax.ShapeDtypeStruct((L, D), BF16)]
        out_specs = [row, row]
    else:
        ins, in_specs = (x, nw, scale, shift), [row, vec, vec, vec]
        out_shape = [jax.ShapeDtypeStruct((L, D), BF16)]
        out_specs = [row]
    outs = _pcall(body, name=name, grid=(L // tm,), out_shape=out_shape, in_specs=in_specs,
                  out_specs=out_specs, compiler_params=_cparams(("parallel",)))(*ins)
    return outs if has_res else (x, outs[0])


def _norm_mod_bwd(dh, x, nw, scale, dres, name, tm=256):
    L, D = x.shape
    tm = _tile(L, tm)

    def body(dh_ref, x_ref, nw_ref, sc_ref, dres_ref, dx_ref, dsh_ref, dsc_ref, dnw_ref):
        @pl.when(pl.program_id(0) == 0)
        def _():
            dsh_ref[...] = jnp.zeros_like(dsh_ref)
            dsc_ref[...] = jnp.zeros_like(dsc_ref)
            dnw_ref[...] = jnp.zeros_like(dnw_ref)

        xv = x_ref[...]
        dh_v = dh_ref[...]
        nw_v = nw_ref[...]
        rstd = lax.rsqrt(jnp.mean(xv * xv, axis=-1, keepdims=True) + NORM_EPS)
        xhat = xv * rstd
        dsh_ref[...] += jnp.sum(dh_v, axis=0, keepdims=True)
        dsc_ref[...] += jnp.sum(dh_v * (xhat * nw_v), axis=0, keepdims=True)
        dr = dh_v * (1.0 + sc_ref[...])
        dnw_ref[...] += jnp.sum(dr * xhat, axis=0, keepdims=True)
        dxh = dr * nw_v
        dx = rstd * (dxh - xhat * jnp.mean(dxh * xhat, axis=-1, keepdims=True))
        dx_ref[...] = dx + dres_ref[...]

    row, vec = _row_spec(tm, D), _vec_spec(D)
    return _pcall(
        body, name=name, grid=(L // tm,),
        out_shape=[jax.ShapeDtypeStruct((L, D), F32)] + [jax.ShapeDtypeStruct((1, D), F32)] * 3,
        in_specs=[row, row, vec, vec, row], out_specs=[row, vec, vec, vec],
        compiler_params=_cparams(("arbitrary",)))(dh, x, nw, scale, dres)


def _gate_bwd(dx, y, gate, name, tm=256):
    L, D = dx.shape
    tm = _tile(L, tm)

    def body(dx_ref, y_ref, g_ref, dy_ref, dg_ref):
        @pl.when(pl.program_id(0) == 0)
        def _():
            dg_ref[...] = jnp.zeros_like(dg_ref)

        dxv = dx_ref[...]
        dy_ref[...] = (g_ref[...] * dxv).astype(BF16)
        dg_ref[...] += jnp.sum(dxv * y_ref[...], axis=0, keepdims=True)

    row, vec = _row_spec(tm, D), _vec_spec(D)
    return _pcall(
        body, name=name, grid=(L // tm,),
        out_shape=[jax.ShapeDtypeStruct((L, D), BF16), jax.ShapeDtypeStruct((1, D), F32)],
        in_specs=[row, row, vec], out_specs=[row, vec],
        compiler_params=_cparams(("arbitrary",)))(dx, y, gate)


def _final_loss(x, y, gate, fw, target, name, tm=256):
    L, D = x.shape
    tm = _tile(L, tm)

    def body(x_ref, y_ref, g_ref, fw_ref, t_ref, dx_ref, loss_ref, dfw_ref):
        @pl.when(pl.program_id(0) == 0)
        def _():
            loss_ref[...] = jnp.zeros_like(loss_ref)
            dfw_ref[...] = jnp.zeros_like(dfw_ref)

        xn = x_ref[...] + g_ref[...] * y_ref[...]
        fw_v = fw_ref[...]
        rstd = lax.rsqrt(jnp.mean(xn * xn, axis=-1, keepdims=True) + NORM_EPS)
        xhat = xn * rstd
        diff = xhat * fw_v - t_ref[...]
        loss_ref[...] += jnp.sum(diff * diff, axis=0, keepdims=True)
        dyf = diff * (1.0 / D)
        dfw_ref[...] += jnp.sum(dyf * xhat, axis=0, keepdims=True)
        dxh = dyf * fw_v
        dx_ref[...] = rstd * (dxh - xhat * jnp.mean(dxh * xhat, axis=-1, keepdims=True))

    row, vec = _row_spec(tm, D), _vec_spec(D)
    return _pcall(
        body, name=name, grid=(L // tm,),
        out_shape=[jax.ShapeDtypeStruct((L, D), F32), jax.ShapeDtypeStruct((1, D), F32),
                   jax.ShapeDtypeStruct((1, D), F32)],
        in_specs=[row, row, vec, vec, row], out_specs=[row, vec, vec],
        compiler_params=_cparams(("arbitrary",)))(x, y, gate, fw, target)


def _shift_down(v, s, row):
    if s == 0:
        return v
    return jnp.where(row >= s, pltpu.roll(v, s, 0), 0.0)


def _shift_up(v, s, row):
    if s == 0:
        return v
    n = v.shape[0]
    return jnp.where(row < n - s, pltpu.roll(v, n - s, 0), 0.0)


def _ssd_conv_fwd(zx, w, b, col0, width, name, cb=128):
    L = zx.shape[0]
    nb = width // cb
    off = col0 // cb

    def body(x_ref, w_ref, b_ref, o_ref):
        xv = x_ref[...]
        row = lax.broadcasted_iota(jnp.int32, xv.shape, 0)
        acc = b_ref[...] + w_ref[SSD_K - 1:SSD_K, :] * xv
        for s in range(1, SSD_K):
            acc = acc + w_ref[SSD_K - 1 - s:SSD_K - s, :] * _shift_down(xv, s, row)
        o_ref[...] = acc * _sigmoid(acc)

    return _pcall(
        body, name=name, grid=(nb,), out_shape=jax.ShapeDtypeStruct((L, width), F32),
        in_specs=[pl.BlockSpec((L, cb), lambda j: (0, off + j)),
                  pl.BlockSpec((SSD_K, cb), lambda j: (0, j)),
                  pl.BlockSpec((1, cb), lambda j: (0, j))],
        out_specs=pl.BlockSpec((L, cb), lambda j: (0, j)),
        compiler_params=_cparams(("parallel",)))(zx, w, b)


def _ssd_conv_bwd(zx, w, b, dxc, col0, name, cb=128):
    L = zx.shape[0]
    width = dxc.shape[1]
    nb = width // cb
    off = col0 // cb

    def body(x_ref, w_ref, b_ref, d_ref, dx_ref, dw_ref, db_ref):
        xv = x_ref[...]
        row = lax.broadcasted_iota(jnp.int32, xv.shape, 0)
        shifted = [_shift_down(xv, s, row) for s in range(SSD_K)]
        acc = b_ref[...] + w_ref[SSD_K - 1:SSD_K, :] * xv
        for s in range(1, SSD_K):
            acc = acc + w_ref[SSD_K - 1 - s:SSD_K - s, :] * shifted[s]
        sig = _sigmoid(acc)
        dpre = d_ref[...] * (sig * (1.0 + acc * (1.0 - sig)))
        db_ref[...] = jnp.sum(dpre, axis=0, keepdims=True)
        dx = w_ref[SSD_K - 1:SSD_K, :] * dpre
        for s in range(SSD_K):
            dw_ref[SSD_K - 1 - s:SSD_K - s, :] = jnp.sum(dpre * shifted[s], axis=0, keepdims=True)
            if s:
                dx = dx + w_ref[SSD_K - 1 - s:SSD_K - s, :] * _shift_up(dpre, s, row)
        dx_ref[...] = dx.astype(BF16)

    return _pcall(
        body, name=name, grid=(nb,),
        out_shape=[jax.ShapeDtypeStruct((L, width), BF16), jax.ShapeDtypeStruct((SSD_K, width), F32),
                   jax.ShapeDtypeStruct((1, width), F32)],
        in_specs=[pl.BlockSpec((L, cb), lambda j: (0, off + j)),
                  pl.BlockSpec((SSD_K, cb), lambda j: (0, j)),
                  pl.BlockSpec((1, cb), lambda j: (0, j)),
                  pl.BlockSpec((L, cb), lambda j: (0, j))],
        out_specs=[pl.BlockSpec((L, cb), lambda j: (0, j)),
                   pl.BlockSpec((SSD_K, cb), lambda j: (0, j)),
                   pl.BlockSpec((1, cb), lambda j: (0, j))],
        compiler_params=_cparams(("parallel",)))(zx, w, b, dxc)


def _sc_conv_fwd(proj, w, name, cb=128):
    L = proj.shape[0]
    width = proj.shape[1] // 3
    nb = width // cb

    def body(b_ref, c_ref, x_ref, w_ref, o_ref):
        q = c_ref[...] * x_ref[...]
        row = lax.broadcasted_iota(jnp.int32, q.shape, 0)
        acc = w_ref[SC_K - 1:SC_K, :] * q
        for s in range(1, SC_K):
            acc = acc + w_ref[SC_K - 1 - s:SC_K - s, :] * _shift_down(q, s, row)
        o_ref[...] = (b_ref[...] * acc).astype(BF16)

    return _pcall(
        body, name=name, grid=(nb,), out_shape=jax.ShapeDtypeStruct((L, width), BF16),
        in_specs=[pl.BlockSpec((L, cb), lambda j: (0, j)),
                  pl.BlockSpec((L, cb), lambda j: (0, nb + j)),
                  pl.BlockSpec((L, cb), lambda j: (0, 2 * nb + j)),
                  pl.BlockSpec((SC_K, cb), lambda j: (0, j))],
        out_specs=pl.BlockSpec((L, cb), lambda j: (0, j)),
        compiler_params=_cparams(("parallel",)))(proj, proj, proj, w)


def _sc_conv_bwd(proj, w, dy, name, cb=128):
    L = proj.shape[0]
    width = proj.shape[1] // 3
    nb = width // cb

    def body(b_ref, c_ref, x_ref, w_ref, dy_ref, db_ref, dc_ref, dxv_ref, dw_ref):
        cg, xv, dyv = c_ref[...], x_ref[...], dy_ref[...]
        q = cg * xv
        row = lax.broadcasted_iota(jnp.int32, q.shape, 0)
        shifted = [_shift_down(q, s, row) for s in range(SC_K)]
        conv = w_ref[SC_K - 1:SC_K, :] * q
        for s in range(1, SC_K):
            conv = conv + w_ref[SC_K - 1 - s:SC_K - s, :] * shifted[s]
        db_ref[...] = (dyv * conv).astype(BF16)
        dconv = dyv * b_ref[...]
        dq = w_ref[SC_K - 1:SC_K, :] * dconv
        for s in range(SC_K):
            dw_ref[SC_K - 1 - s:SC_K - s, :] = jnp.sum(dconv * shifted[s], axis=0, keepdims=True)
            if s:
                dq = dq + w_ref[SC_K - 1 - s:SC_K - s, :] * _shift_up(dconv, s, row)
        dc_ref[...] = (dq * xv).astype(BF16)
        dxv_ref[...] = (dq * cg).astype(BF16)

    blk = pl.BlockSpec((L, cb), lambda j: (0, j))
    wblk = pl.BlockSpec((SC_K, cb), lambda j: (0, j))
    return _pcall(
        body, name=name, grid=(nb,),
        out_shape=[jax.ShapeDtypeStruct((L, width), BF16)] * 3 + [jax.ShapeDtypeStruct((SC_K, width), F32)],
        in_specs=[blk, pl.BlockSpec((L, cb), lambda j: (0, nb + j)),
                  pl.BlockSpec((L, cb), lambda j: (0, 2 * nb + j)), wblk, blk],
        out_specs=[blk, blk, blk, wblk],
        compiler_params=_cparams(("parallel",)))(proj, proj, proj, w, dy)


def _split3(v):
    hi = v.astype(BF16)
    r1 = v - hi.astype(F32)
    mid = r1.astype(BF16)
    lo = (r1 - mid.astype(F32)).astype(BF16)
    return hi, mid, lo


def _dot_exact01(t01, v):
    hi, mid, lo = _split3(v)
    return _dot(t01, hi) + _dot(t01, mid) + _dot(t01, lo)


def _lane_col(v, lane, h):
    return jnp.sum(jnp.where(lane == h, v, 0.0), axis=1, keepdims=True)


def _sum_all(v):
    return jnp.sum(jnp.sum(v, axis=1, keepdims=True), axis=0, keepdims=True)


def _softplus(x):
    return jnp.maximum(x, 0.0) + jnp.log1p(jnp.exp(-jnp.abs(x)))


def _ssd_common(dt_ref, bias_ref, alog_ref, b_ref, c_ref, cst_ref, heads):
    c_sz = SSD_CHUNK
    lane = lax.broadcasted_iota(jnp.int32, (c_sz, LANES), 1)
    row = lax.broadcasted_iota(jnp.int32, (c_sz, LANES), 0)
    valid = lane < heads
    raw = dt_ref[...] + bias_ref[...]
    dt = _softplus(raw)
    a_row = -jnp.exp(alog_ref[...])
    a = jnp.where(valid, dt * a_row, 0.0)
    tri = (row >= lane).astype(BF16)
    cs = _dot_exact01(tri, a)
    cst_ref[...] = cs.T
    last_row = jnp.sum(a, axis=0, keepdims=True)
    bb = b_ref[...].astype(BF16)
    cb = c_ref[...].astype(BF16)
    scores = _dot(cb, bb, "nt")
    return dict(lane=lane, row=row, valid=valid, raw=raw, dt=dt, a_row=a_row, cs=cs,
                last_row=last_row, bb=bb, cb=cb, scores=scores, causal=row >= lane, lo=lane < SSD_P)


def _pair_terms(q, cst_ref, j):
    lane, lo = q["lane"], q["lo"]
    out = {}
    cols, dts, lasts, lms = [], [], [], []
    lane1 = lax.broadcasted_iota(jnp.int32, (1, LANES), 1)
    for h in (2 * j, 2 * j + 1):
        col = _lane_col(q["cs"], lane, h)
        rowv = cst_ref[h:h + 1, :]
        lms.append(jnp.exp(jnp.where(q["causal"], col - rowv, -1e30)))
        cols.append(col)
        dts.append(_lane_col(q["dt"], lane, h))
        lasts.append(jnp.sum(jnp.where(lane1 == h, q["last_row"], 0.0), axis=1, keepdims=True))
    out["lm"] = lms
    out["cols"] = cols
    out["lasts"] = lasts
    out["dt_b"] = jnp.where(lo, dts[0], dts[1])
    out["e_b"] = jnp.where(lo, jnp.exp(cols[0]), jnp.exp(cols[1]))
    out["dec_cols"] = [jnp.exp(lasts[0] - cols[0]), jnp.exp(lasts[1] - cols[1])]
    out["dec_b"] = jnp.where(lo, out["dec_cols"][0], out["dec_cols"][1])
    lo1 = lane1 < SSD_P
    out["explast"] = [jnp.exp(lasts[0]), jnp.exp(lasts[1])]
    out["explast_b"] = jnp.where(lo1, out["explast"][0], out["explast"][1])
    return out


def _ssd_fwd(zx, xc, bias_p, alog_p, d_lane, nw, d_inner, name):
    L = zx.shape[0]
    nc = L // SSD_CHUNK
    gw = d_inner // SSD_G
    heads = gw // SSD_P
    n_pair = heads // 2
    zb = gw // LANES
    bc0 = d_inner // LANES
    dt0 = (2 * d_inner + 2 * SSD_G * SSD_N) // LANES

    def body(z_ref, xs_ref, b_ref, c_ref, dt_ref, bias_ref, alog_ref, dl_ref, nw_ref,
             y_ref, yn_ref, prev_ref, s_ref, cst_ref):
        @pl.when(pl.program_id(1) == 0)
        def _():
            s_ref[...] = jnp.zeros_like(s_ref)

        q = _ssd_common(dt_ref, bias_ref, alog_ref, b_ref, c_ref, cst_ref, heads)
        prev_ref[...] = s_ref[...]
        lo = q["lo"]
        for j in range(n_pair):
            sl = slice(j * LANES, (j + 1) * LANES)
            p = _pair_terms(q, cst_ref, j)
            xs_p = xs_ref[:, sl]
            xp = xs_p * p["dt_b"]
            xb = xp.astype(BF16)
            m_a = (q["scores"] * p["lm"][0]).astype(BF16)
            m_b = (q["scores"] * p["lm"][1]).astype(BF16)
            yd = jnp.where(lo, _dot(m_a, xb), _dot(m_b, xb))
            s_p = s_ref[:, sl]
            yo = _dot(q["cb"], s_p.astype(BF16)) * p["e_b"]
            y_ref[:, sl] = yd + yo + dl_ref[:, sl] * xs_p
            st = _dot(q["bb"], (xp * p["dec_b"]).astype(BF16), "tn")
            s_ref[:, sl] = s_p * p["explast_b"] + st
        yv = y_ref[...]
        zv = z_ref[...]
        yg = yv * (zv * _sigmoid(zv))
        rstd = lax.rsqrt(jnp.mean(yg * yg, axis=-1, keepdims=True) + NORM_EPS)
        yn_ref[...] = (yg * rstd * nw_ref[...]).astype(BF16)

    grp = lambda width: pl.BlockSpec((None, 1, width), lambda g, c: (g, 0, 0))
    return _pcall(
        body, name=name, grid=(SSD_G, nc),
        out_shape=[jax.ShapeDtypeStruct((L, d_inner), F32), jax.ShapeDtypeStruct((L, d_inner), BF16),
                   jax.ShapeDtypeStruct((nc, SSD_G, SSD_N, gw), F32)],
        in_specs=[pl.BlockSpec((SSD_CHUNK, gw), lambda g, c: (c, g)),
                  pl.BlockSpec((SSD_CHUNK, gw), lambda g, c: (c, g)),
                  pl.BlockSpec((SSD_CHUNK, SSD_N), lambda g, c: (c, bc0 + g)),
                  pl.BlockSpec((SSD_CHUNK, SSD_N), lambda g, c: (c, bc0 + SSD_G + g)),
                  pl.BlockSpec((SSD_CHUNK, LANES), lambda g, c: (c, dt0 + g)),
                  grp(LANES), grp(LANES), grp(gw), grp(gw)],
        out_specs=[pl.BlockSpec((SSD_CHUNK, gw), lambda g, c: (c, g)),
                   pl.BlockSpec((SSD_CHUNK, gw), lambda g, c: (c, g)),
                   pl.BlockSpec((None, None, SSD_N, gw), lambda g, c: (c, g, 0, 0))],
        scratch_shapes=[pltpu.VMEM((SSD_N, gw), F32), pltpu.VMEM((SSD_CHUNK, LANES), F32)],
        compiler_params=_cparams(("parallel", "arbitrary")))(zx, xc, xc, xc, zx, bias_p, alog_p, d_lane, nw)


def _ssd_bwd(dyn, y, zx, xc, prev, bias_p, alog_p, d_lane, nw, d_inner, name):
    L = zx.shape[0]
    nc = L // SSD_CHUNK
    gw = d_inner // SSD_G
    heads = gw // SSD_P
    n_pair = heads // 2
    bc0 = d_inner // LANES
    dt0 = (2 * d_inner + 2 * SSD_G * SSD_N) // LANES

    def body(dyn_ref, y_ref, z_ref, xs_ref, b_ref, c_ref, dt_ref, prev_ref, bias_ref, alog_ref, dl_ref, nw_ref,
             dz_ref, dxs_ref, db_ref, dc_ref, ddt_ref, dbias_ref, dalog_ref, dd_ref, dnw_ref,
             ds_ref, cst_ref, racc_ref):
        @pl.when(pl.program_id(1) == 0)
        def _():
            ds_ref[...] = jnp.zeros_like(ds_ref)
            dbias_ref[...] = jnp.zeros_like(dbias_ref)
            dalog_ref[...] = jnp.zeros_like(dalog_ref)
            dd_ref[...] = jnp.zeros_like(dd_ref)
            dnw_ref[...] = jnp.zeros_like(dnw_ref)

        q = _ssd_common(dt_ref, bias_ref, alog_ref, b_ref, c_ref, cst_ref, heads)
        lane, row, lo = q["lane"], q["row"], q["lo"]
        lane1 = lax.broadcasted_iota(jnp.int32, (1, LANES), 1)

        yv, zv, dynv, nwv = y_ref[...], z_ref[...], dyn_ref[...], nw_ref[...]
        sig = _sigmoid(zv)
        sz = zv * sig
        yg = yv * sz
        rstd = lax.rsqrt(jnp.mean(yg * yg, axis=-1, keepdims=True) + NORM_EPS)
        yhat = yg * rstd
        dnw_ref[...] += jnp.sum(dynv * yhat, axis=0, keepdims=True)
        dyh = dynv * nwv
        dyg = rstd * (dyh - yhat * jnp.mean(dyh * yhat, axis=-1, keepdims=True))
        dz_ref[...] = (dyg * yv * (sig * (1.0 + zv * (1.0 - sig)))).astype(BF16)
        dy_all = dyg * sz

        dg = jnp.zeros((SSD_CHUNK, SSD_CHUNK), F32)
        dc_acc = jnp.zeros((SSD_CHUNK, SSD_N), F32)
        db_acc = jnp.zeros((SSD_CHUNK, SSD_N), F32)
        dcs_mat = jnp.zeros((SSD_CHUNK, LANES), F32)
        ddt_mat = jnp.zeros((SSD_CHUNK, LANES), F32)
        dd_row = jnp.zeros((1, LANES), F32)
        racc_ref[...] = jnp.zeros_like(racc_ref)
        is_last = row == SSD_CHUNK - 1

        for j in range(n_pair):
            sl = slice(j * LANES, (j + 1) * LANES)
            ha, hb = 2 * j, 2 * j + 1
            p = _pair_terms(q, cst_ref, j)
            xs_p = xs_ref[:, sl]
            dyp = dy_all[:, sl]
            xp = xs_p * p["dt_b"]
            xb = xp.astype(BF16)
            s_p = prev_ref[:, sl]
            s_pb = s_p.astype(BF16)
            dsn = ds_ref[:, sl]
            dsnb = dsn.astype(BF16)
            m_f = [q["scores"] * p["lm"][0], q["scores"] * p["lm"][1]]

            t0 = dyp * xs_p
            dd_row = dd_row + jnp.where(lane1 == ha, _sum_all(jnp.where(lo, t0, 0.0)), 0.0) \
                + jnp.where(lane1 == hb, _sum_all(jnp.where(lo, 0.0, t0)), 0.0)
            dxs_p = dl_ref[:, sl] * dyp

            yo = _dot(q["cb"], s_pb) * p["e_b"]
            dcs_b = (dyp * p["e_b"]).astype(BF16)
            dc_acc = dc_acc + _dot(dcs_b, s_pb, "nt")
            ds_yo = _dot(q["cb"], dcs_b, "tn")
            t1 = dyp * yo
            dcs_cols = [jnp.sum(jnp.where(lo, t1, 0.0), axis=1, keepdims=True),
                        jnp.sum(jnp.where(lo, 0.0, t1), axis=1, keepdims=True)]

            t2 = dsn * s_p
            dlast = [p["explast"][0] * _sum_all(jnp.where(lo, t2, 0.0)),
                     p["explast"][1] * _sum_all(jnp.where(lo, 0.0, t2))]
            ds_ref[:, sl] = dsn * p["explast_b"] + ds_yo
            w = _dot(q["bb"], dsnb)
            db_acc = db_acc + _dot((xp * p["dec_b"]).astype(BF16), dsnb, "nt")
            dxp = w * p["dec_b"]
            t3 = w * xp
            e = [jnp.sum(jnp.where(lo, t3, 0.0), axis=1, keepdims=True) * p["dec_cols"][0],
                 jnp.sum(jnp.where(lo, 0.0, t3), axis=1, keepdims=True) * p["dec_cols"][1]]
            for i in range(2):
                dlast[i] = dlast[i] + jnp.sum(e[i], axis=0, keepdims=True)
                dcs_cols[i] = dcs_cols[i] - e[i]

            dyb = dyp.astype(BF16)
            dy_h = [jnp.where(lo, dyp, 0.0).astype(BF16), jnp.where(lo, 0.0, dyp).astype(BF16)]
            dms = [_dot(dy_h[0], xb, "nt"), _dot(dy_h[1], xb, "nt")]
            dxp = dxp + jnp.where(lo, _dot(m_f[0].astype(BF16), dyb, "tn"), _dot(m_f[1].astype(BF16), dyb, "tn"))
            for i, h in enumerate((ha, hb)):
                dg = dg + dms[i] * p["lm"][i]
                qm = dms[i] * m_f[i]
                dcs_cols[i] = dcs_cols[i] + jnp.sum(qm, axis=1, keepdims=True)
                racc_ref[h:h + 1, :] = jnp.sum(qm, axis=0, keepdims=True)

            dxs_ref[:, sl] = dxs_p + dxp * p["dt_b"]
            t4 = dxp * xs_p
            ddt_cols = [jnp.sum(jnp.where(lo, t4, 0.0), axis=1, keepdims=True),
                        jnp.sum(jnp.where(lo, 0.0, t4), axis=1, keepdims=True)]
            for i, h in enumerate((ha, hb)):
                sel = lane == h
                dcs_mat = dcs_mat + jnp.where(sel, dcs_cols[i], 0.0) + jnp.where(sel & is_last, dlast[i], 0.0)
                ddt_mat = ddt_mat + jnp.where(sel, ddt_cols[i], 0.0)

        dcs_mat = dcs_mat - racc_ref[...].T
        tri_t = (row <= lane).astype(BF16)
        da = _dot_exact01(tri_t, dcs_mat)
        ddt = ddt_mat + da * q["a_row"]
        dalog_ref[...] += jnp.sum(jnp.where(q["valid"], da * q["dt"], 0.0), axis=0, keepdims=True) * q["a_row"]
        draw = jnp.where(q["valid"], ddt * _sigmoid(q["raw"]), 0.0)
        ddt_ref[...] = draw
        dbias_ref[...] += jnp.sum(draw, axis=0, keepdims=True)
        dd_ref[...] += dd_row
        dgb = dg.astype(BF16)
        dc_ref[...] = dc_acc + _dot(dgb, q["bb"])
        db_ref[...] = db_acc + _dot(dgb, q["cb"], "tn")

    rev = lambda c: nc - 1 - c
    grp = lambda width: pl.BlockSpec((None, 1, width), lambda g, c: (g, 0, 0))
    blk = lambda width, off: pl.BlockSpec((SSD_CHUNK, width), lambda g, c: (rev(c), off + g))
    return _pcall(
        body, name=name, grid=(SSD_G, nc),
        out_shape=[jax.ShapeDtypeStruct((L, d_inner), BF16), jax.ShapeDtypeStruct((L, d_inner), F32),
                   jax.ShapeDtypeStruct((L, SSD_G * SSD_N), F32), jax.ShapeDtypeStruct((L, SSD_G * SSD_N), F32),
                   jax.ShapeDtypeStruct((L, SSD_G * LANES), F32),
                   jax.ShapeDtypeStruct((SSD_G, 1, LANES), F32), jax.ShapeDtypeStruct((SSD_G, 1, LANES), F32),
                   jax.ShapeDtypeStruct((SSD_G, 1, LANES), F32), jax.ShapeDtypeStruct((SSD_G, 1, gw), F32)],
        in_specs=[blk(gw, 0), blk(gw, 0), blk(gw, 0), blk(gw, 0), blk(SSD_N, bc0), blk(SSD_N, bc0 + SSD_G),
                  blk(LANES, dt0),
                  pl.BlockSpec((None, None, SSD_N, gw), lambda g, c: (rev(c), g, 0, 0)),
                  grp(LANES), grp(LANES), grp(gw), grp(gw)],
        out_specs=[blk(gw, 0), blk(gw, 0), blk(SSD_N, 0), blk(SSD_N, 0), blk(LANES, 0),
                   grp(LANES), grp(LANES), grp(LANES), grp(gw)],
        scratch_shapes=[pltpu.VMEM((SSD_N, gw), F32), pltpu.VMEM((SSD_CHUNK, LANES), F32),
                        pltpu.VMEM((SSD_CHUNK, LANES), F32)],
        compiler_params=_cparams(("parallel", "arbitrary")))(
            dyn, y, zx, xc, xc, xc, zx, prev, bias_p, alog_p, d_lane, nw)


def _cond_mod(c_pad, ada_w, ada_b_loc, name):
    depth, D, n = ada_w.shape
    rows = c_pad.shape[0]

    def body(c_ref, w_ref, b_ref, mod_ref, cond_ref):
        cv = c_ref[...]
        cond = cv * _sigmoid(cv)
        cond_ref[...] = cond
        mod_ref[...] = _dot(cond.astype(BF16), w_ref[...].astype(BF16)) + b_ref[...]

    return _pcall(
        body, name=name, grid=(depth,),
        out_shape=[jax.ShapeDtypeStruct((depth, rows, n), F32), jax.ShapeDtypeStruct((rows, D), F32)],
        in_specs=[pl.BlockSpec((rows, D), lambda i: (0, 0)),
                  pl.BlockSpec((None, D, n), lambda i: (i, 0, 0)),
                  pl.BlockSpec((None, 1, n), lambda i: (i, 0, 0))],
        out_specs=[pl.BlockSpec((None, rows, n), lambda i: (i, 0, 0)),
                   pl.BlockSpec((rows, D), lambda i: (0, 0))],
        compiler_params=_cparams(("arbitrary",)))(c_pad, ada_w, ada_b_loc)


def _adamw_math(g, w, m, v):
    m_new = ADAM_B1 * m + (1.0 - ADAM_B1) * g
    v_new = ADAM_B2 * v + (1.0 - ADAM_B2) * (g * g)
    m_hat = m_new / (1.0 - ADAM_B1 ** ADAM_STEP)
    v_hat = v_new / (1.0 - ADAM_B2 ** ADAM_STEP)
    delta = -ADAM_LR * (m_hat / (jnp.sqrt(v_hat) + ADAM_EPS) + ADAM_WD * w)
    return delta, m_new, v_new


def _adamw_sum(parts, w, m, v, name, tr=256):
    R, C = w.shape
    tr = _tile(R, tr)

    def body(p_ref, w_ref, m_ref, v_ref, g_ref, d_ref, mo_ref, vo_ref):
        g = p_ref[0].astype(F32)
        for k in range(1, N_DEV):
            g = g + p_ref[k].astype(F32)
        d, mn, vn = _adamw_math(g, w_ref[...], m_ref[...], v_ref[...])
        g_ref[...] = g
        d_ref[...] = d
        mo_ref[...] = mn
        vo_ref[...] = vn

    blk = pl.BlockSpec((tr, C), lambda i: (i, 0))
    return _pcall(
        body, name=name, grid=(R // tr,),
        out_shape=[jax.ShapeDtypeStruct((R, C), F32)] * 4,
        in_specs=[pl.BlockSpec((N_DEV, tr, C), lambda i: (0, i, 0)), blk, blk, blk],
        out_specs=[blk] * 4, compiler_params=_cparams(("parallel",)))(parts, w, m, v)


def _ada_adamw(cond_pad, dmod_pad, w, m, v, name, tr=256):
    depth, D, n = w.shape
    rows = cond_pad.shape[0]
    tr = _tile(D, tr)

    def body(c_ref, dm_ref, w_ref, m_ref, v_ref, g_ref, d_ref, mo_ref, vo_ref):
        g = _dot(c_ref[...].astype(BF16), dm_ref[...].astype(BF16), "tn")
        d, mn, vn = _adamw_math(g, w_ref[...], m_ref[...], v_ref[...])
        g_ref[...] = g
        d_ref[...] = d
        mo_ref[...] = mn
        vo_ref[...] = vn

    blk = pl.BlockSpec((None, tr, n), lambda i, r: (i, r, 0))
    return _pcall(
        body, name=name, grid=(depth, D // tr),
        out_shape=[jax.ShapeDtypeStruct((depth, D, n), F32)] * 4,
        in_specs=[pl.BlockSpec((rows, tr), lambda i, r: (0, r)),
                  pl.BlockSpec((None, rows, n), lambda i, r: (i, 0, 0)), blk, blk, blk],
        out_specs=[blk] * 4, compiler_params=_cparams(("parallel", "parallel")))(cond_pad, dmod_pad, w, m, v)


def _pad_heads(v, heads_per_group):
    lead = v.shape[:-1]
    v = v.reshape(lead + (SSD_G, heads_per_group))
    v = jnp.pad(v, [(0, 0)] * len(lead) + [(0, 0), (0, LANES - heads_per_group)])
    return v.reshape(lead + (SSD_G * LANES,))


def _unpad_heads(v, heads_per_group):
    lead = v.shape[:-1]
    v = v.reshape(lead + (SSD_G, LANES))[..., :heads_per_group]
    return v.reshape(lead + (SSD_G * heads_per_group,))


def kernel(x, c, ada_w, ada_b, mix_norm_w, mlp_norm_w, mlp_up, mlp_down, ssd_in_w, ssd_conv_w, ssd_conv_b, ssd_dt_bias, ssd_A_log, ssd_D, ssd_norm_w, ssd_out_w, sc_in_w, sc_conv_w, sc_out_w, final_norm_w, loss_target, m_ada_w, m_ada_b, m_mix_norm_w, m_mlp_norm_w, m_mlp_up, m_mlp_down, m_ssd_in_w, m_ssd_conv_w, m_ssd_conv_b, m_ssd_dt_bias, m_ssd_A_log, m_ssd_D, m_ssd_norm_w, m_ssd_out_w, m_sc_in_w, m_sc_conv_w, m_sc_out_w, m_final_norm_w, v_ada_w, v_ada_b, v_mix_norm_w, v_mlp_norm_w, v_mlp_up, v_mlp_down, v_ssd_in_w, v_ssd_conv_w, v_ssd_conv_b, v_ssd_dt_bias, v_ssd_A_log, v_ssd_D, v_ssd_norm_w, v_ssd_out_w, v_sc_in_w, v_sc_conv_w, v_sc_out_w, v_final_norm_w):
    weights = dict(ada_w=ada_w, ada_b=ada_b, mix_norm_w=mix_norm_w, mlp_norm_w=mlp_norm_w, mlp_up=mlp_up,
                   mlp_down=mlp_down, ssd_in_w=ssd_in_w, ssd_conv_w=ssd_conv_w, ssd_conv_b=ssd_conv_b,
                   ssd_dt_bias=ssd_dt_bias, ssd_A_log=ssd_A_log, ssd_D=ssd_D, ssd_norm_w=ssd_norm_w,
                   ssd_out_w=ssd_out_w, sc_in_w=sc_in_w, sc_conv_w=sc_conv_w, sc_out_w=sc_out_w,
                   final_norm_w=final_norm_w)
    moms = dict(ada_w=m_ada_w, ada_b=m_ada_b, mix_norm_w=m_mix_norm_w, mlp_norm_w=m_mlp_norm_w, mlp_up=m_mlp_up,
                mlp_down=m_mlp_down, ssd_in_w=m_ssd_in_w, ssd_conv_w=m_ssd_conv_w, ssd_conv_b=m_ssd_conv_b,
                ssd_dt_bias=m_ssd_dt_bias, ssd_A_log=m_ssd_A_log, ssd_D=m_ssd_D, ssd_norm_w=m_ssd_norm_w,
                ssd_out_w=m_ssd_out_w, sc_in_w=m_sc_in_w, sc_conv_w=m_sc_conv_w, sc_out_w=m_sc_out_w,
                final_norm_w=m_final_norm_w)
    vars_ = dict(ada_w=v_ada_w, ada_b=v_ada_b, mix_norm_w=v_mix_norm_w, mlp_norm_w=v_mlp_norm_w, mlp_up=v_mlp_up,
                 mlp_down=v_mlp_down, ssd_in_w=v_ssd_in_w, ssd_conv_w=v_ssd_conv_w, ssd_conv_b=v_ssd_conv_b,
                 ssd_dt_bias=v_ssd_dt_bias, ssd_A_log=v_ssd_A_log, ssd_D=v_ssd_D, ssd_norm_w=v_ssd_norm_w,
                 ssd_out_w=v_ssd_out_w, sc_in_w=v_sc_in_w, sc_conv_w=v_sc_conv_w, sc_out_w=v_sc_out_w,
                 final_norm_w=v_final_norm_w)
    names = list(weights)

    L, D = x.shape[1], x.shape[2]
    d_inner = 2 * D
    n_heads = d_inner // SSD_P
    hpg = n_heads // SSD_G
    gw = d_inner // SSD_G
    conv_dim = d_inner + 2 * SSD_G * SSD_N
    zx_dim = d_inner + conv_dim
    me = _my_index()
    x0 = x[0]
    tgt = loss_target[0]

    gather_order = ["c", "ssd_conv_w", "sc_conv_w", "ssd_in_w", "ssd_out_w", "up0", "down0", "sc_in_w", "sc_out_w",
                    "up1", "down1"]
    gather_src = dict(c=c, ssd_conv_w=ssd_conv_w[0], sc_conv_w=sc_conv_w[0], ssd_in_w=ssd_in_w[0].astype(BF16),
                      ssd_out_w=ssd_out_w[0].astype(BF16), up0=mlp_up[0].astype(BF16),
                      down0=mlp_down[0].astype(BF16), sc_in_w=sc_in_w[0].astype(BF16),
                      sc_out_w=sc_out_w[0].astype(BF16), up1=mlp_up[1].astype(BF16),
                      down1=mlp_down[1].astype(BF16))
    handles, gather_token = _xfer_start([gather_src[k] for k in gather_order], "gather_start", gather=True)
    gather_handle = dict(zip(gather_order, handles))

    def gathered(k, after):
        return _xfer_wait(gather_handle[k], after, f"gather_wait_{k}", gather=True)

    n_mod = ada_w.shape[2]
    c_all = gathered("c", gather_token)
    c_pad = jnp.pad(c_all.reshape(N_DEV, D), ((0, 16 - N_DEV), (0, 0)))
    ada_b_loc = lax.dynamic_slice_in_dim(ada_b, me * n_mod, n_mod, axis=1).reshape(2, 1, n_mod)
    mod_blk, cond_pad = _cond_mod(c_pad, ada_w, ada_b_loc, "cond_mod")
    (mod_all,) = _exchange([mod_blk], "gather_mod", gather=True)
    mod_mine = lax.dynamic_index_in_dim(mod_all, me, axis=2, keepdims=False)
    mod_mine = jnp.transpose(mod_mine, (1, 0, 2)).reshape(2, 6, 1, D)
    sh_m, sc_m, g_m, sh_f, sc_f, g_f = [[mod_mine[i, k] for i in range(2)] for k in range(6)]

    vec = lambda a: a.reshape(1, -1)
    grads = {}
    small = {}

    _, h0 = _norm_mod_fwd(x0, None, None, vec(mix_norm_w[0]), sc_m[0], sh_m[0], "l0_mix_norm")
    ssd_in_g = gathered("ssd_in_w", h0)
    in_dim = ssd_in_g.shape[2] * N_DEV
    w_in_nat = jnp.transpose(ssd_in_g, (1, 0, 2)).reshape(D, in_dim)
    w_in_all = jnp.concatenate([w_in_nat[:, :zx_dim], _pad_heads(w_in_nat[:, zx_dim:], hpg)], axis=1)
    (zx,) = _mm_nn(h0, w_in_all, F32, "ssd_in_proj", tn=512)
    conv_b0 = vec(ssd_conv_b[0])
    conv_w_full = jnp.transpose(gathered("ssd_conv_w", h0), (1, 0, 2)).reshape(SSD_K, conv_dim)
    sc_conv_full = jnp.transpose(gathered("sc_conv_w", h0), (1, 0, 2)).reshape(SC_K, D)
    xc = _ssd_conv_fwd(zx, conv_w_full, conv_b0, d_inner, conv_dim, "ssd_conv")
    bias_p = _pad_heads(ssd_dt_bias[0], hpg).reshape(SSD_G, 1, LANES)
    alog_p = _pad_heads(ssd_A_log[0], hpg).reshape(SSD_G, 1, LANES)
    d_lane = jnp.repeat(ssd_D[0], SSD_P).reshape(SSD_G, 1, gw)
    nw_g = ssd_norm_w[0].reshape(SSD_G, 1, gw)
    y_ssd, yn, prev = _ssd_fwd(zx, xc, bias_p, alog_p, d_lane, nw_g, d_inner, "ssd_scan")
    w_ssd_out = gathered("ssd_out_w", yn).reshape(-1, D)
    (mix0,) = _mm_nn(yn, w_ssd_out, F32, "ssd_out_proj")
    x1, h1 = _norm_mod_fwd(x0, mix0, g_m[0], vec(mlp_norm_w[0]), sc_f[0], sh_f[0], "l0_mlp_norm")
    ups, downs = [None, None], [None, None]
    ups[0] = gathered("up0", h1)
    u0, s0 = _mm_nn_blocked(h1, ups[0], "l0_mlp_up", _ep_relu2, [BF16, BF16])
    downs[0] = gathered("down0", s0).reshape(-1, D)
    (d0,) = _mm_nn(s0, downs[0], F32, "l0_mlp_down")
    x2, h2 = _norm_mod_fwd(x1, d0, g_f[0], vec(mix_norm_w[1]), sc_m[1], sh_m[1], "l1_mix_norm")
    sc_in_g = gathered("sc_in_w", h2)
    (proj,) = _mm_nn_blocked(h2, sc_in_g, "sc_in_proj", _ep_store(F32), [F32])
    yc = _sc_conv_fwd(proj, sc_conv_full, "sc_conv")
    w_sc_out = gathered("sc_out_w", yc).reshape(-1, D)
    (mix1,) = _mm_nn(yc, w_sc_out, F32, "sc_out_proj")
    x3, h3 = _norm_mod_fwd(x2, mix1, g_m[1], vec(mlp_norm_w[1]), sc_f[1], sh_f[1], "l1_mlp_norm")
    ups[1] = gathered("up1", h3)
    u1, s1 = _mm_nn_blocked(h3, ups[1], "l1_mlp_up", _ep_relu2, [BF16, BF16])
    downs[1] = gathered("down1", s1).reshape(-1, D)
    (d1,) = _mm_nn(s1, downs[1], F32, "l1_mlp_down")

    dx, loss_lane, dfw = _final_loss(x3, d1, g_f[1], vec(final_norm_w), tgt, "final_loss")
    loss = lax.psum(0.5 * jnp.sum(loss_lane) / D, MESH_AXES)
    small["final_norm_w"] = dfw

    dmod = [[None] * 6 for _ in range(2)]
    big = {}

    def mlp_backward(i, dx_out, d_out, x_mid, h_in, u, s):
        dd, dg = _gate_bwd(dx_out, d_out, g_f[i], f"l{i}_mlp_gate_bwd")
        dmod[i][5] = dg
        du = _mm_nt(dd, downs[i], BF16, f"l{i}_mlp_down_bwd", epilogue=_ep_relu2_bwd, extra=(u,))
        gdown = _mm_tn(s, dd, BF16, f"l{i}_mlp_down_wgrad").reshape(N_DEV, -1, D)
        gup = _mm_tn_blocked(h_in, du, BF16, f"l{i}_mlp_up_wgrad")
        (h_down, h_up), token = _xfer_start([gdown, gup], f"l{i}_mlp_grads_start", gather=False)
        grad_handle[f"mlp_down{i}"], grad_handle[f"mlp_up{i}"] = h_down, h_up
        dh = _mm_nt_blocked(du, ups[i], F32, f"l{i}_mlp_up_bwd")
        dxm, dsh, dsc, dnw = _norm_mod_bwd(dh, x_mid, _tie(vec(mlp_norm_w[i]), token), sc_f[i], dx_out,
                                           f"l{i}_mlp_norm_bwd")
        dmod[i][3], dmod[i][4] = dsh, dsc
        return dxm, dnw

    grad_handle = {}
    dx3, dnw_mlp1 = mlp_backward(1, dx, d1, x3, h3, u1, s1)
    dyc, dg = _gate_bwd(dx3, mix1, g_m[1], "l1_mix_gate_bwd")
    dmod[1][2] = dg
    g_sc_out = _mm_tn(yc, dyc, BF16, "sc_out_wgrad").reshape(N_DEV, -1, D)
    dconv_out = _mm_nt(dyc, w_sc_out, F32, "sc_out_bwd")
    dbg, dcg, dxv, dscw = _sc_conv_bwd(proj, sc_conv_full, dconv_out, "sc_conv_bwd")
    dproj = jnp.concatenate([dbg, dcg, dxv], axis=1)
    g_sc_in = _mm_tn_blocked(h2, dproj, BF16, "sc_in_wgrad")
    (grad_handle["sc_out_w0"], grad_handle["sc_in_w0"]), token = _xfer_start(
        [g_sc_out, g_sc_in], "sc_grads_start", gather=False)
    dh2 = _mm_nt_blocked(dproj, sc_in_g, F32, "sc_in_bwd")
    dx2, dsh, dsc, dnw_mix1 = _norm_mod_bwd(dh2, x2, _tie(vec(mix_norm_w[1]), token), sc_m[1], dx3,
                                            "l1_mix_norm_bwd")
    dmod[1][0], dmod[1][1] = dsh, dsc
    dx1, dnw_mlp0 = mlp_backward(0, dx2, d0, x1, h1, u0, s0)
    dyo, dg = _gate_bwd(dx1, mix0, g_m[0], "l0_mix_gate_bwd")
    dmod[0][2] = dg
    g_ssd_out = _mm_tn(yn, dyo, BF16, "ssd_out_wgrad").reshape(N_DEV, -1, D)
    dyn = _mm_nt(dyo, w_ssd_out, F32, "ssd_out_bwd")
    dz, dxs, db_, dc_, ddt, dbias, dalog, dd_, dnw_ssd = _ssd_bwd(
        dyn, y_ssd, zx, xc, prev, bias_p, alog_p, d_lane, nw_g, d_inner, "ssd_scan_bwd")
    dxc = jnp.concatenate([dxs, db_, dc_], axis=1)
    dxbc, dcw, dcb = _ssd_conv_bwd(zx, conv_w_full, conv_b0, dxc, d_inner, "ssd_conv_bwd")
    dzx = jnp.concatenate([dz, dxbc, ddt.astype(BF16)], axis=1)
    g_in_all = _mm_tn(h0, dzx, BF16, "ssd_in_wgrad")
    g_in_nat = jnp.concatenate([g_in_all[:, :zx_dim], _unpad_heads(g_in_all[:, zx_dim:], hpg)], axis=1)
    g_ssd_in = jnp.transpose(g_in_nat.reshape(D, N_DEV, in_dim // N_DEV), (1, 0, 2))
    (grad_handle["ssd_out_w0"], grad_handle["ssd_in_w0"]), token = _xfer_start(
        [g_ssd_out, g_ssd_in], "ssd_grads_start", gather=False)
    dh0 = _mm_nt(dzx, w_in_all, F32, "ssd_in_bwd")
    grad_x, dsh, dsc, dnw_mix0 = _norm_mod_bwd(dh0, x0, _tie(vec(mix_norm_w[0]), token), sc_m[0], dx1,
                                               "l0_mix_norm_bwd")
    dmod[0][0], dmod[0][1] = dsh, dsc

    small["mix_norm_w"] = jnp.concatenate([dnw_mix0, dnw_mix1], axis=0)
    small["mlp_norm_w"] = jnp.concatenate([dnw_mlp0, dnw_mlp1], axis=0)
    small["ssd_conv_w"] = dcw
    small["ssd_conv_b"] = dcb
    small["ssd_dt_bias"] = _unpad_heads(dbias.reshape(SSD_G * LANES), hpg)
    small["ssd_A_log"] = _unpad_heads(dalog.reshape(SSD_G * LANES), hpg)
    small["ssd_D"] = _unpad_heads(dd_.reshape(SSD_G * LANES), hpg)
    small["ssd_norm_w"] = dnw_ssd
    small["sc_conv_w"] = dscw
    small["dmod"] = jnp.concatenate([jnp.concatenate(dmod[i], axis=1) for i in range(2)], axis=0)

    small_order = ["dmod", "mix_norm_w", "mlp_norm_w", "ssd_conv_w", "ssd_conv_b", "ssd_dt_bias", "ssd_A_log",
                   "ssd_D", "ssd_norm_w", "sc_conv_w", "final_norm_w"]
    flat = jnp.concatenate([small[k].reshape(-1) for k in small_order])
    n_small = flat.shape[0]
    n_small_pad = -(-n_small // 1024) * 1024
    flat = jnp.pad(flat, (0, n_small_pad - n_small)).reshape(n_small_pad // LANES, LANES)
    (small_all,) = _exchange([flat], "gather_small_grads", gather=True)
    small_all = small_all.reshape(N_DEV, n_small_pad)
    offs, o = {}, 0
    for k in small_order:
        offs[k] = (o, small[k].size, small[k].shape)
        o += small[k].size

    def small_parts(k):
        o, n, shape = offs[k]
        return small_all[:, o:o + n].reshape((N_DEV,) + shape)

    out_g, out_d, out_m, out_v = {}, {}, {}, {}

    def big_update(name, n_layers):
        w_, m_, v_ = weights[name], moms[name], vars_[name]
        res = []
        for i in range(n_layers):
            parts = _xfer_wait(grad_handle[f"{name}{i}"], grad_x, f"grads_wait_{name}_{i}", gather=False)
            res.append(_adamw_sum(parts, w_[i], m_[i], v_[i], f"adamw_{name}_{i}"))
        for k, dst in enumerate((out_g, out_d, out_m, out_v)):
            dst[name] = jnp.stack([r[k] for r in res], axis=0)

    big_update("mlp_up", 2)
    big_update("mlp_down", 2)
    big_update("sc_out_w", 1)
    big_update("sc_in_w", 1)
    big_update("ssd_out_w", 1)
    big_update("ssd_in_w", 1)

    dmod_all = small_parts("dmod")
    dmod_loc = lax.dynamic_slice_in_dim(dmod_all, me * n_mod, n_mod, axis=2)
    dmod_pad = jnp.pad(jnp.transpose(dmod_loc, (1, 0, 2)), ((0, 0), (0, 16 - N_DEV), (0, 0)))
    out_g["ada_w"], out_d["ada_w"], out_m["ada_w"], out_v["ada_w"] = _ada_adamw(
        cond_pad, dmod_pad, ada_w, m_ada_w, v_ada_w, "adamw_ada_w")

    pieces = []
    pieces.append(("ada_b", dmod_all.reshape(N_DEV, -1)))
    for k in ["mix_norm_w", "mlp_norm_w", "ssd_conv_b", "ssd_dt_bias", "ssd_A_log", "ssd_D", "ssd_norm_w",
              "final_norm_w"]:
        pieces.append((k, small_parts(k).reshape(N_DEV, -1)))
    n_cw = ssd_conv_w.shape[2]
    pieces.append(("ssd_conv_w", lax.dynamic_slice_in_dim(small_parts("ssd_conv_w"), me * n_cw, n_cw, axis=2)
                   .reshape(N_DEV, -1)))
    n_scw = sc_conv_w.shape[2]
    pieces.append(("sc_conv_w", lax.dynamic_slice_in_dim(small_parts("sc_conv_w"), me * n_scw, n_scw, axis=2)
                   .reshape(N_DEV, -1)))
    n_tot = sum(p.shape[1] for _, p in pieces)
    n_tot_pad = -(-n_tot // 1024) * 1024

    def pack(arrs, lead=()):
        f = jnp.concatenate(arrs, axis=-1)
        f = jnp.pad(f, [(0, 0)] * len(lead) + [(0, n_tot_pad - n_tot)])
        return f.reshape(lead + (n_tot_pad // LANES, LANES))

    parts_flat = pack([p for _, p in pieces], lead=(N_DEV,))
    w_flat = pack([weights[k].reshape(-1) for k, _ in pieces])
    m_flat = pack([moms[k].reshape(-1) for k, _ in pieces])
    v_flat = pack([vars_[k].reshape(-1) for k, _ in pieces])
    res = _adamw_sum(parts_flat, w_flat, m_flat, v_flat, "adamw_small", tr=n_tot_pad // LANES)
    o = 0
    for k, p in pieces:
        n = p.shape[1]
        for r, dst in zip(res, (out_g, out_d, out_m, out_v)):
            dst[k] = r.reshape(-1)[o:o + n].reshape(weights[k].shape)
        o += n

    return (loss, grad_x[None], *[out_g[k] for k in names], *[out_d[k] for k in names],
            *[out_m[k] for k in names], *[out_v[k] for k in names])
```

```python
import functools

import jax
import jax.numpy as jnp
from jax import lax
from jax.experimental import pallas as pl
from jax.experimental.pallas import tpu as pltpu

F32 = jnp.float32
BF16 = jnp.bfloat16
N_DEV = 8
MESH_AXES = ("x", "y", "c")
MESH = pl.DeviceIdType.MESH

NORM_EPS = 1e-5
SSD_G = 4
SSD_P = 64
SSD_N = 128
SSD_CHUNK = 128
SSD_K = 4
SC_K = 3
LANES = 128

ADAM_LR = 0.001
ADAM_B1 = 0.9
ADAM_B2 = 0.999
ADAM_EPS = 1e-08
ADAM_WD = 0.01
ADAM_STEP = 10

VMEM_LIMIT = 56 * 1024 * 1024


def _pcall(body, **kw):
    return pl.pallas_call(body, **kw)


def _cparams(sem=None):
    if sem is None:
        return pltpu.CompilerParams(vmem_limit_bytes=VMEM_LIMIT)
    return pltpu.CompilerParams(dimension_semantics=sem, vmem_limit_bytes=VMEM_LIMIT)


def _my_index():
    return 4 * lax.axis_index("x") + 2 * lax.axis_index("y") + lax.axis_index("c")


_PEER_MASKS = [(0, 0, 1), (0, 1, 0), (0, 1, 1), (1, 0, 0), (1, 0, 1), (1, 1, 0), (1, 1, 1)]


def _peers():
    x, y, c = lax.axis_index("x"), lax.axis_index("y"), lax.axis_index("c")
    out = []
    for mx, my, mc in _PEER_MASKS:
        px = (1 - x) if mx else x
        py = (1 - y) if my else y
        pc = (1 - c) if mc else c
        out.append(((px, py, pc), 4 * px + 2 * py + pc))
    return out


def _exchange(arrs, name, gather):
    n = len(arrs)
    n_peer = N_DEV - 1

    def body(*refs):
        ins, outs = refs[:n], refs[n:2 * n]
        send_sems, recv_sems, local_sems = refs[2 * n:]
        me = _my_index()
        peers = _peers()
        started = []
        for a in range(n):
            src_own = ins[a] if gather else ins[a].at[me]
            own = pltpu.make_async_copy(src_own, outs[a].at[me], local_sems.at[a])
            own.start()
            started.append(own)
        sends = []
        for a in range(n):
            for k, (peer, pidx) in enumerate(peers):
                src = ins[a] if gather else ins[a].at[pidx]
                cp = pltpu.make_async_remote_copy(
                    src_ref=src, dst_ref=outs[a].at[me],
                    send_sem=send_sems.at[a * n_peer + k], recv_sem=recv_sems.at[a * n_peer + k],
                    device_id=peer, device_id_type=MESH)
                cp.start()
                sends.append(cp)
        for a in range(n):
            for k, (peer, pidx) in enumerate(peers):
                src = ins[a] if gather else ins[a].at[pidx]
                pltpu.make_async_remote_copy(
                    src_ref=src, dst_ref=outs[a].at[pidx],
                    send_sem=send_sems.at[a * n_peer + k], recv_sem=recv_sems.at[a * n_peer + k],
                    device_id=peer, device_id_type=MESH).wait_recv()
        for cp in sends:
            cp.wait_send()
        for own in started:
            own.wait()

    if gather:
        out_shape = [jax.ShapeDtypeStruct((N_DEV,) + a.shape, a.dtype) for a in arrs]
    else:
        out_shape = [jax.ShapeDtypeStruct(a.shape, a.dtype) for a in arrs]
    any_spec = pl.BlockSpec(memory_space=pl.ANY)
    outs = _pcall(
        body, name=name, out_shape=out_shape,
        in_specs=[any_spec] * n, out_specs=[any_spec] * n,
        scratch_shapes=[pltpu.SemaphoreType.DMA((n * n_peer,)), pltpu.SemaphoreType.DMA((n * n_peer,)),
                        pltpu.SemaphoreType.DMA((n,))],
        compiler_params=pltpu.CompilerParams(has_side_effects=True),
    )(*arrs)
    return list(outs)


_HBM = pl.BlockSpec(memory_space=pltpu.HBM)
_SEM = pl.BlockSpec(memory_space=pltpu.SEMAPHORE)
_DATAFLOW = pltpu.SideEffectType.DATAFLOW_SIDE_EFFECTING


def _xfer_start(arrs, name, gather):
    n = len(arrs)
    n_peer = N_DEV - 1

    def body(*refs):
        ins, lands = refs[:n], refs[n:2 * n]
        sems = refs[2 * n:5 * n]
        token = refs[-1]
        me = _my_index()
        peers = _peers()
        for a in range(n):
            send_sems, recv_sems, loc_sem = sems[3 * a:3 * a + 3]
            src_own = ins[a] if gather else ins[a].at[me]
            pltpu.make_async_copy(src_own, lands[a].at[me], loc_sem).start()
            for k, (peer, pidx) in enumerate(peers):
                src = ins[a] if gather else ins[a].at[pidx]
                pltpu.make_async_remote_copy(
                    src_ref=src, dst_ref=lands[a].at[me], send_sem=send_sems.at[k], recv_sem=recv_sems.at[k],
                    device_id=peer, device_id_type=MESH).start()
        token[...] = jnp.zeros_like(token)

    land_shapes = [((N_DEV,) + a.shape) if gather else a.shape for a in arrs]
    out_shape, out_specs = [], []
    for _ in range(n):
        out_shape += [pltpu.SemaphoreType.DMA((n_peer,)), pltpu.SemaphoreType.DMA((n_peer,)),
                      pltpu.SemaphoreType.DMA(())]
        out_specs += [_SEM, _SEM, _SEM]
    out_shape += [pltpu.HBM(a.shape, a.dtype) for a in arrs]
    out_shape += [pltpu.HBM(s, a.dtype) for s, a in zip(land_shapes, arrs)]
    out_shape += [jax.ShapeDtypeStruct((8, LANES), F32)]
    out_specs += [_HBM] * (2 * n) + [pl.BlockSpec(memory_space=pltpu.VMEM)]
    aliases = {}
    for a in range(n):
        aliases[a] = 3 * n + a
        aliases[n + a] = 4 * n + a
    operands = [pltpu.with_memory_space_constraint(a, pltpu.HBM) for a in arrs]
    operands += [pltpu.with_memory_space_constraint(lax.empty(s, a.dtype), pltpu.HBM)
                 for s, a in zip(land_shapes, arrs)]
    outs = _pcall(
        body, name=name, out_shape=tuple(out_shape), in_specs=[_HBM] * (2 * n), out_specs=tuple(out_specs),
        input_output_aliases=aliases,
        compiler_params=pltpu.CompilerParams(has_side_effects=_DATAFLOW),
    )(*operands)
    handles = []
    for a in range(n):
        handles.append((outs[3 * n + a], outs[4 * n + a], outs[3 * a], outs[3 * a + 1], outs[3 * a + 2]))
    return handles, outs[-1]


def _xfer_wait(handle, after, name, gather):
    src_thru, land_thru, send_sems, recv_sems, loc_sem = handle

    def body(src_ref, land_ref, send_ref, recv_ref, loc_ref, after_ref, src_dead, got_ref):
        me = _my_index()
        src_own = src_ref if gather else src_ref.at[me]
        pltpu.make_async_copy(src_own, land_ref.at[me], loc_ref).wait()
        for k, (peer, pidx) in enumerate(_peers()):
            src = src_ref if gather else src_ref.at[pidx]
            cp = pltpu.make_async_remote_copy(
                src_ref=src, dst_ref=land_ref.at[pidx], send_sem=send_ref.at[k], recv_sem=recv_ref.at[k],
                device_id=peer, device_id_type=MESH)
            cp.wait_send()
            cp.wait_recv()

    return _pcall(
        body, name=name,
        out_shape=(pltpu.HBM(src_thru.shape, src_thru.dtype), pltpu.HBM(land_thru.shape, land_thru.dtype)),
        in_specs=[_HBM, _HBM, _SEM, _SEM, _SEM, pl.BlockSpec(memory_space=pl.ANY)], out_specs=(_HBM, _HBM),
        input_output_aliases={0: 0, 1: 1},
        compiler_params=pltpu.CompilerParams(has_side_effects=_DATAFLOW),
    )(src_thru, land_thru, send_sems, recv_sems, loc_sem, after)[1]


def _tie(v, token):
    return v + token[0:1, 0:1].reshape((1,) * v.ndim)


_DIMS = {"nn": (((1,), (0,)), ((), ())), "nt": (((1,), (1,)), ((), ())), "tn": (((0,), (0,)), ((), ()))}


def _dot(a, b, mode="nn"):
    return lax.dot_general(a, b, _DIMS[mode], preferred_element_type=F32)


def _mm(a, b, *, mode, grid, a_spec, b_spec, out_shape, out_specs, acc_shape, epilogue, name,
        extra=(), extra_specs=()):
    nk = grid[2]
    n_extra = len(extra)

    def body_single(*refs):
        a_ref, b_ref = refs[0], refs[1]
        epilogue(_dot(a_ref[...], b_ref[...], mode), refs[2:2 + n_extra], refs[2 + n_extra:])

    def body_acc(*refs):
        a_ref, b_ref = refs[0], refs[1]
        ex = refs[2:2 + n_extra]
        outs = refs[2 + n_extra:-1]
        acc = refs[-1]
        k = pl.program_id(2)

        @pl.when(k == 0)
        def _():
            acc[...] = jnp.zeros_like(acc)

        acc[...] += _dot(a_ref[...], b_ref[...], mode)

        @pl.when(k == nk - 1)
        def _():
            epilogue(acc[...], ex, outs)

    return _pcall(
        body_single if nk == 1 else body_acc, name=name, grid=grid, out_shape=out_shape,
        in_specs=[a_spec, b_spec] + list(extra_specs), out_specs=out_specs,
        scratch_shapes=[] if nk == 1 else [pltpu.VMEM(acc_shape, F32)],
        compiler_params=_cparams(("parallel", "parallel", "arbitrary")),
    )(a, b, *extra)


def _ep_store(dtype):
    def ep(acc, ex, outs):
        outs[0][...] = acc.astype(dtype)
    return ep


def _ep_relu2(acc, ex, outs):
    outs[0][...] = acc.astype(BF16)
    r = jnp.maximum(acc, 0.0)
    outs[1][...] = (r * r).astype(BF16)


def _ep_relu2_bwd(acc, ex, outs):
    u = ex[0][...].astype(F32)
    outs[0][...] = (acc * (2.0 * jnp.maximum(u, 0.0))).astype(BF16)


def _tile(n, want):
    t = min(n, want)
    while n % t:
        t //= 2
    return t


def _mm_nn(a, w, out_dtype, name, tm=1024, tn=1024, tk=1024, epilogue=None, out_dtypes=None):
    M, K = a.shape
    N = w.shape[1]
    tm, tn, tk = _tile(M, tm), _tile(N, tn), _tile(K, tk)
    out_dtypes = out_dtypes or [out_dtype]
    return _mm(a, w, mode="nn", grid=(M // tm, N // tn, K // tk),
               a_spec=pl.BlockSpec((tm, tk), lambda i, j, k: (i, k)),
               b_spec=pl.BlockSpec((tk, tn), lambda i, j, k: (k, j)),
               out_shape=[jax.ShapeDtypeStruct((M, N), d) for d in out_dtypes],
               out_specs=[pl.BlockSpec((tm, tn), lambda i, j, k: (i, j)) for _ in out_dtypes],
               acc_shape=(tm, tn), epilogue=epilogue or _ep_store(out_dtype), name=name)


def _mm_nn_blocked(a, wg, name, epilogue, out_dtypes, tm=2048):
    M, K = a.shape
    n = wg.shape[2]
    tm = _tile(M, tm)
    return _mm(a, wg, mode="nn", grid=(M // tm, N_DEV, 1),
               a_spec=pl.BlockSpec((tm, K), lambda i, j, k: (i, 0)),
               b_spec=pl.BlockSpec((None, K, n), lambda i, j, k: (j, 0, 0)),
               out_shape=[jax.ShapeDtypeStruct((M, N_DEV * n), d) for d in out_dtypes],
               out_specs=[pl.BlockSpec((tm, n), lambda i, j, k: (i, j)) for _ in out_dtypes],
               acc_shape=(tm, n), epilogue=epilogue, name=name)


def _mm_nt(a, w, out_dtype, name, tm=1024, tn=1024, tk=1024, epilogue=None, extra=(), extra_specs=()):
    M, K = a.shape
    N = w.shape[0]
    tm, tn, tk = _tile(M, tm), _tile(N, tn), _tile(K, tk)
    if extra and not extra_specs:
        extra_specs = [pl.BlockSpec((tm, tn), lambda i, j, k: (i, j)) for _ in extra]
    return _mm(a, w, mode="nt", grid=(M // tm, N // tn, K // tk),
               a_spec=pl.BlockSpec((tm, tk), lambda i, j, k: (i, k)),
               b_spec=pl.BlockSpec((tn, tk), lambda i, j, k: (j, k)),
               out_shape=[jax.ShapeDtypeStruct((M, N), out_dtype)],
               out_specs=[pl.BlockSpec((tm, tn), lambda i, j, k: (i, j))],
               acc_shape=(tm, tn), epilogue=epilogue or _ep_store(out_dtype), name=name,
               extra=extra, extra_specs=extra_specs)[0]


def _mm_nt_blocked(a, wg, out_dtype, name, tm=1024):
    M = a.shape[0]
    kout, n = wg.shape[1], wg.shape[2]
    tm = _tile(M, tm)
    return _mm(a, wg, mode="nt", grid=(M // tm, 1, N_DEV),
               a_spec=pl.BlockSpec((tm, n), lambda i, j, k: (i, k)),
               b_spec=pl.BlockSpec((None, kout, n), lambda i, j, k: (k, 0, 0)),
               out_shape=[jax.ShapeDtypeStruct((M, kout), out_dtype)],
               out_specs=[pl.BlockSpec((tm, kout), lambda i, j, k: (i, 0))],
               acc_shape=(tm, kout), epilogue=_ep_store(out_dtype), name=name)[0]


def _mm_tn(a, b, out_dtype, name, tm=1024, tn=1024, tk=1024):
    K, M = a.shape
    N = b.shape[1]
    tm, tn, tk = _tile(M, tm), _tile(N, tn), _tile(K, tk)
    return _mm(a, b, mode="tn", grid=(M // tm, N // tn, K // tk),
               a_spec=pl.BlockSpec((tk, tm), lambda i, j, k: (k, i)),
               b_spec=pl.BlockSpec((tk, tn), lambda i, j, k: (k, j)),
               out_shape=[jax.ShapeDtypeStruct((M, N), out_dtype)],
               out_specs=[pl.BlockSpec((tm, tn), lambda i, j, k: (i, j))],
               acc_shape=(tm, tn), epilogue=_ep_store(out_dtype), name=name)[0]


def _mm_tn_blocked(a, b, out_dtype, name, tm=1024, tk=2048):
    K, M = a.shape
    n = b.shape[1] // N_DEV
    tm, tk = _tile(M, tm), _tile(K, tk)
    return _mm(a, b, mode="tn", grid=(M // tm, N_DEV, K // tk),
               a_spec=pl.BlockSpec((tk, tm), lambda i, j, k: (k, i)),
               b_spec=pl.BlockSpec((tk, n), lambda i, j, k: (k, j)),
               out_shape=[jax.ShapeDtypeStruct((N_DEV, M, n), out_dtype)],
               out_specs=[pl.BlockSpec((None, tm, n), lambda i, j, k: (j, i, 0))],
               acc_shape=(tm, n), epilogue=_ep_store(out_dtype), name=name)[0]


def _sigmoid(x):
    return 1.0 / (1.0 + jnp.exp(-x))


def _row_spec(tm, d):
    return pl.BlockSpec((tm, d), lambda i: (i, 0))


def _vec_spec(d):
    return pl.BlockSpec((1, d), lambda i: (0, 0))


def _norm_mod_fwd(x, y, gate, nw, scale, shift, name, tm=256):
    L, D = x.shape
    tm = _tile(L, tm)
    has_res = y is not None

    def body(*refs):
        if has_res:
            x_ref, y_ref, g_ref, nw_ref, sc_ref, sh_ref, xo_ref, h_ref = refs
            xn = x_ref[...] + g_ref[...] * y_ref[...]
            xo_ref[...] = xn
        else:
            x_ref, nw_ref, sc_ref, sh_ref, h_ref = refs
            xn = x_ref[...]
        rstd = lax.rsqrt(jnp.mean(xn * xn, axis=-1, keepdims=True) + NORM_EPS)
        h = xn * rstd * nw_ref[...] * (1.0 + sc_ref[...]) + sh_ref[...]
        h_ref[...] = h.astype(BF16)

    row, vec = _row_spec(tm, D), _vec_spec(D)
    if has_res:
        ins, in_specs = (x, y, gate, nw, scale, shift), [row, row, vec, vec, vec, vec]
        out_shape = [jax.ShapeDtypeStruct((L, D), F32), jax.ShapeDtypeStruct((L, D), BF16)]
        out_specs = [row, row]
    else:
        ins, in_specs = (x, nw, scale, shift), [row, vec, vec, vec]
        out_shape = [jax.ShapeDtypeStruct((L, D), BF16)]
        out_specs = [row]
    outs = _pcall(body, name=name, grid=(L // tm,), out_shape=out_shape, in_specs=in_specs,
                  out_specs=out_specs, compiler_params=_cparams(("parallel",)))(*ins)
    return outs if has_res else (x, outs[0])


def _norm_mod_bwd(dh, x, nw, scale, dres, name, tm=256):
    L, D = x.shape
    tm = _tile(L, tm)

    def body(dh_ref, x_ref, nw_ref, sc_ref, dres_ref, dx_ref, dsh_ref, dsc_ref, dnw_ref):
        @pl.when(pl.program_id(0) == 0)
        def _():
            dsh_ref[...] = jnp.zeros_like(dsh_ref)
            dsc_ref[...] = jnp.zeros_like(dsc_ref)
            dnw_ref[...] = jnp.zeros_like(dnw_ref)

        xv = x_ref[...]
        dh_v = dh_ref[...]
        nw_v = nw_ref[...]
        rstd = lax.rsqrt(jnp.mean(xv * xv, axis=-1, keepdims=True) + NORM_EPS)
        xhat = xv * rstd
        dsh_ref[...] += jnp.sum(dh_v, axis=0, keepdims=True)
        dsc_ref[...] += jnp.sum(dh_v * (xhat * nw_v), axis=0, keepdims=True)
        dr = dh_v * (1.0 + sc_ref[...])
        dnw_ref[...] += jnp.sum(dr * xhat, axis=0, keepdims=True)
        dxh = dr * nw_v
        dx = rstd * (dxh - xhat * jnp.mean(dxh * xhat, axis=-1, keepdims=True))
        dx_ref[...] = dx + dres_ref[...]

    row, vec = _row_spec(tm, D), _vec_spec(D)
    return _pcall(
        body, name=name, grid=(L // tm,),
        out_shape=[jax.ShapeDtypeStruct((L, D), F32)] + [jax.ShapeDtypeStruct((1, D), F32)] * 3,
        in_specs=[row, row, vec, vec, row], out_specs=[row, vec, vec, vec],
        compiler_params=_cparams(("arbitrary",)))(dh, x, nw, scale, dres)


def _gate_bwd(dx, y, gate, name, tm=256):
    L, D = dx.shape
    tm = _tile(L, tm)

    def body(dx_ref, y_ref, g_ref, dy_ref, dg_ref):
        @pl.when(pl.program_id(0) == 0)
        def _():
            dg_ref[...] = jnp.zeros_like(dg_ref)

        dxv = dx_ref[...]
        dy_ref[...] = (g_ref[...] * dxv).astype(BF16)
        dg_ref[...] += jnp.sum(dxv * y_ref[...], axis=0, keepdims=True)

    row, vec = _row_spec(tm, D), _vec_spec(D)
    return _pcall(
        body, name=name, grid=(L // tm,),
        out_shape=[jax.ShapeDtypeStruct((L, D), BF16), jax.ShapeDtypeStruct((1, D), F32)],
        in_specs=[row, row, vec], out_specs=[row, vec],
        compiler_params=_cparams(("arbitrary",)))(dx, y, gate)


def _final_loss(x, y, gate, fw, target, name, tm=256):
    L, D = x.shape
    tm = _tile(L, tm)

    def body(x_ref, y_ref, g_ref, fw_ref, t_ref, dx_ref, loss_ref, dfw_ref):
        @pl.when(pl.program_id(0) == 0)
        def _():
            loss_ref[...] = jnp.zeros_like(loss_ref)
            dfw_ref[...] = jnp.zeros_like(dfw_ref)

        xn = x_ref[...] + g_ref[...] * y_ref[...]
        fw_v = fw_ref[...]
        rstd = lax.rsqrt(jnp.mean(xn * xn, axis=-1, keepdims=True) + NORM_EPS)
        xhat = xn * rstd
        diff = xhat * fw_v - t_ref[...]
        loss_ref[...] += jnp.sum(diff * diff, axis=0, keepdims=True)
        dyf = diff * (1.0 / D)
        dfw_ref[...] += jnp.sum(dyf * xhat, axis=0, keepdims=True)
        dxh = dyf * fw_v
        dx_ref[...] = rstd * (dxh - xhat * jnp.mean(dxh * xhat, axis=-1, keepdims=True))

    row, vec = _row_spec(tm, D), _vec_spec(D)
    return _pcall(
        body, name=name, grid=(L // tm,),
        out_shape=[jax.ShapeDtypeStruct((L, D), F32), jax.ShapeDtypeStruct((1, D), F32),
                   jax.ShapeDtypeStruct((1, D), F32)],
        in_specs=[row, row, vec, vec, row], out_specs=[row, vec, vec],
        compiler_params=_cparams(("arbitrary",)))(x, y, gate, fw, target)


def _shift_down(v, s, row):
    if s == 0:
        return v
    return jnp.where(row >= s, pltpu.roll(v, s, 0), 0.0)


def _shift_up(v, s, row):
    if s == 0:
        return v
    n = v.shape[0]
    return jnp.where(row < n - s, pltpu.roll(v, n - s, 0), 0.0)


def _ssd_conv_fwd(zx, w, b, col0, width, name, cb=128):
    L = zx.shape[0]
    nb = width // cb
    off = col0 // cb

    def body(x_ref, w_ref, b_ref, o_ref):
        xv = x_ref[...]
        row = lax.broadcasted_iota(jnp.int32, xv.shape, 0)
        acc = b_ref[...] + w_ref[SSD_K - 1:SSD_K, :] * xv
        for s in range(1, SSD_K):
            acc = acc + w_ref[SSD_K - 1 - s:SSD_K - s, :] * _shift_down(xv, s, row)
        o_ref[...] = acc * _sigmoid(acc)

    return _pcall(
        body, name=name, grid=(nb,), out_shape=jax.ShapeDtypeStruct((L, width), F32),
        in_specs=[pl.BlockSpec((L, cb), lambda j: (0, off + j)),
                  pl.BlockSpec((SSD_K, cb), lambda j: (0, j)),
                  pl.BlockSpec((1, cb), lambda j: (0, j))],
        out_specs=pl.BlockSpec((L, cb), lambda j: (0, j)),
        compiler_params=_cparams(("parallel",)))(zx, w, b)


def _ssd_conv_bwd(zx, w, b, dxc, col0, name, cb=128):
    L = zx.shape[0]
    width = dxc.shape[1]
    nb = width // cb
    off = col0 // cb

    def body(x_ref, w_ref, b_ref, d_ref, dx_ref, dw_ref, db_ref):
        xv = x_ref[...]
        row = lax.broadcasted_iota(jnp.int32, xv.shape, 0)
        shifted = [_shift_down(xv, s, row) for s in range(SSD_K)]
        acc = b_ref[...] + w_ref[SSD_K - 1:SSD_K, :] * xv
        for s in range(1, SSD_K):
            acc = acc + w_ref[SSD_K - 1 - s:SSD_K - s, :] * shifted[s]
        sig = _sigmoid(acc)
        dpre = d_ref[...] * (sig * (1.0 + acc * (1.0 - sig)))
        db_ref[...] = jnp.sum(dpre, axis=0, keepdims=True)
        dx = w_ref[SSD_K - 1:SSD_K, :] * dpre
        for s in range(SSD_K):
            dw_ref[SSD_K - 1 - s:SSD_K - s, :] = jnp.sum(dpre * shifted[s], axis=0, keepdims=True)
            if s:
                dx = dx + w_ref[SSD_K - 1 - s:SSD_K - s, :] * _shift_up(dpre, s, row)
        dx_ref[...] = dx.astype(BF16)

    return _pcall(
        body, name=name, grid=(nb,),
        out_shape=[jax.ShapeDtypeStruct((L, width), BF16), jax.ShapeDtypeStruct((SSD_K, width), F32),
                   jax.ShapeDtypeStruct((1, width), F32)],
        in_specs=[pl.BlockSpec((L, cb), lambda j: (0, off + j)),
                  pl.BlockSpec((SSD_K, cb), lambda j: (0, j)),
                  pl.BlockSpec((1, cb), lambda j: (0, j)),
                  pl.BlockSpec((L, cb), lambda j: (0, j))],
        out_specs=[pl.BlockSpec((L, cb), lambda j: (0, j)),
                   pl.BlockSpec((SSD_K, cb), lambda j: (0, j)),
                   pl.BlockSpec((1, cb), lambda j: (0, j))],
        compiler_params=_cparams(("parallel",)))(zx, w, b, dxc)


def _sc_conv_fwd(proj, w, name, cb=128):
    L = proj.shape[0]
    width = proj.shape[1] // 3
    nb = width // cb

    def body(b_ref, c_ref, x_ref, w_ref, o_ref):
        q = c_ref[...] * x_ref[...]
        row = lax.broadcasted_iota(jnp.int32, q.shape, 0)
        acc = w_ref[SC_K - 1:SC_K, :] * q
        for s in range(1, SC_K):
            acc = acc + w_ref[SC_K - 1 - s:SC_K - s, :] * _shift_down(q, s, row)
        o_ref[...] = (b_ref[...] * acc).astype(BF16)

    return _pcall(
        body, name=name, grid=(nb,), out_shape=jax.ShapeDtypeStruct((L, width), BF16),
        in_specs=[pl.BlockSpec((L, cb), lambda j: (0, j)),
                  pl.BlockSpec((L, cb), lambda j: (0, nb + j)),
                  pl.BlockSpec((L, cb), lambda j: (0, 2 * nb + j)),
                  pl.BlockSpec((SC_K, cb), lambda j: (0, j))],
        out_specs=pl.BlockSpec((L, cb), lambda j: (0, j)),
        compiler_params=_cparams(("parallel",)))(proj, proj, proj, w)


def _sc_conv_bwd(proj, w, dy, name, cb=128):
    L = proj.shape[0]
    width = proj.shape[1] // 3
    nb = width // cb

    def body(b_ref, c_ref, x_ref, w_ref, dy_ref, db_ref, dc_ref, dxv_ref, dw_ref):
        cg, xv, dyv = c_ref[...], x_ref[...], dy_ref[...]
        q = cg * xv
        row = lax.broadcasted_iota(jnp.int32, q.shape, 0)
        shifted = [_shift_down(q, s, row) for s in range(SC_K)]
        conv = w_ref[SC_K - 1:SC_K, :] * q
        for s in range(1, SC_K):
            conv = conv + w_ref[SC_K - 1 - s:SC_K - s, :] * shifted[s]
        db_ref[...] = (dyv * conv).astype(BF16)
        dconv = dyv * b_ref[...]
        dq = w_ref[SC_K - 1:SC_K, :] * dconv
        for s in range(SC_K):
            dw_ref[SC_K - 1 - s:SC_K - s, :] = jnp.sum(dconv * shifted[s], axis=0, keepdims=True)
            if s:
                dq = dq + w_ref[SC_K - 1 - s:SC_K - s, :] * _shift_up(dconv, s, row)
        dc_ref[...] = (dq * xv).astype(BF16)
        dxv_ref[...] = (dq * cg).astype(BF16)

    blk = pl.BlockSpec((L, cb), lambda j: (0, j))
    wblk = pl.BlockSpec((SC_K, cb), lambda j: (0, j))
    return _pcall(
        body, name=name, grid=(nb,),
        out_shape=[jax.ShapeDtypeStruct((L, width), BF16)] * 3 + [jax.ShapeDtypeStruct((SC_K, width), F32)],
        in_specs=[blk, pl.BlockSpec((L, cb), lambda j: (0, nb + j)),
                  pl.BlockSpec((L, cb), lambda j: (0, 2 * nb + j)), wblk, blk],
        out_specs=[blk, blk, blk, wblk],
        compiler_params=_cparams(("parallel",)))(proj, proj, proj, w, dy)


def _split3(v):
    hi = v.astype(BF16)
    r1 = v - hi.astype(F32)
    mid = r1.astype(BF16)
    lo = (r1 - mid.astype(F32)).astype(BF16)
    return hi, mid, lo


def _dot_exact01(t01, v):
    hi, mid, lo = _split3(v)
    return _dot(t01, hi) + _dot(t01, mid) + _dot(t01, lo)


def _lane_col(v, lane, h):
    return jnp.sum(jnp.where(lane == h, v, 0.0), axis=1, keepdims=True)


def _sum_all(v):
    return jnp.sum(jnp.sum(v, axis=1, keepdims=True), axis=0, keepdims=True)


def _softplus(x):
    return jnp.maximum(x, 0.0) + jnp.log1p(jnp.exp(-jnp.abs(x)))


def _ssd_common(dt_ref, bias_ref, alog_ref, b_ref, c_ref, cst_ref, heads):
    c_sz = SSD_CHUNK
    lane = lax.broadcasted_iota(jnp.int32, (c_sz, LANES), 1)
    row = lax.broadcasted_iota(jnp.int32, (c_sz, LANES), 0)
    valid = lane < heads
    raw = dt_ref[...] + bias_ref[...]
    dt = _softplus(raw)
    a_row = -jnp.exp(alog_ref[...])
    a = jnp.where(valid, dt * a_row, 0.0)
    tri = (row >= lane).astype(BF16)
    cs = _dot_exact01(tri, a)
    cst_ref[...] = cs.T
    last_row = jnp.sum(a, axis=0, keepdims=True)
    bb = b_ref[...].astype(BF16)
    cb = c_ref[...].astype(BF16)
    scores = _dot(cb, bb, "nt")
    return dict(lane=lane, row=row, valid=valid, raw=raw, dt=dt, a_row=a_row, cs=cs,
                last_row=last_row, bb=bb, cb=cb, scores=scores, causal=row >= lane, lo=lane < SSD_P)


def _pair_terms(q, cst_ref, j):
    lane, lo = q["lane"], q["lo"]
    out = {}
    cols, dts, lasts, lms = [], [], [], []
    lane1 = lax.broadcasted_iota(jnp.int32, (1, LANES), 1)
    for h in (2 * j, 2 * j + 1):
        col = _lane_col(q["cs"], lane, h)
        rowv = cst_ref[h:h + 1, :]
        lms.append(jnp.exp(jnp.where(q["causal"], col - rowv, -1e30)))
        cols.append(col)
        dts.append(_lane_col(q["dt"], lane, h))
        lasts.append(jnp.sum(jnp.where(lane1 == h, q["last_row"], 0.0), axis=1, keepdims=True))
    out["lm"] = lms
    out["cols"] = cols
    out["lasts"] = lasts
    out["dt_b"] = jnp.where(lo, dts[0], dts[1])
    out["e_b"] = jnp.where(lo, jnp.exp(cols[0]), jnp.exp(cols[1]))
    out["dec_cols"] = [jnp.exp(lasts[0] - cols[0]), jnp.exp(lasts[1] - cols[1])]
    out["dec_b"] = jnp.where(lo, out["dec_cols"][0], out["dec_cols"][1])
    lo1 = lane1 < SSD_P
    out["explast"] = [jnp.exp(lasts[0]), jnp.exp(lasts[1])]
    out["explast_b"] = jnp.where(lo1, out["explast"][0], out["explast"][1])
    return out


def _ssd_fwd(zx, xc, bias_p, alog_p, d_lane, nw, d_inner, name):
    L = zx.shape[0]
    nc = L // SSD_CHUNK
    gw = d_inner // SSD_G
    heads = gw // SSD_P
    n_pair = heads // 2
    zb = gw // LANES
    bc0 = d_inner // LANES
    dt0 = (2 * d_inner + 2 * SSD_G * SSD_N) // LANES

    def body(z_ref, xs_ref, b_ref, c_ref, dt_ref, bias_ref, alog_ref, dl_ref, nw_ref,
             y_ref, yn_ref, prev_ref, s_ref, cst_ref):
        @pl.when(pl.program_id(1) == 0)
        def _():
            s_ref[...] = jnp.zeros_like(s_ref)

        q = _ssd_common(dt_ref, bias_ref, alog_ref, b_ref, c_ref, cst_ref, heads)
        prev_ref[...] = s_ref[...]
        lo = q["lo"]
        for j in range(n_pair):
            sl = slice(j * LANES, (j + 1) * LANES)
            p = _pair_terms(q, cst_ref, j)
            xs_p = xs_ref[:, sl]
            xp = xs_p * p["dt_b"]
            xb = xp.astype(BF16)
            m_a = (q["scores"] * p["lm"][0]).astype(BF16)
            m_b = (q["scores"] * p["lm"][1]).astype(BF16)
            yd = jnp.where(lo, _dot(m_a, xb), _dot(m_b, xb))
            s_p = s_ref[:, sl]
            yo = _dot(q["cb"], s_p.astype(BF16)) * p["e_b"]
            y_ref[:, sl] = yd + yo + dl_ref[:, sl] * xs_p
            st = _dot(q["bb"], (xp * p["dec_b"]).astype(BF16), "tn")
            s_ref[:, sl] = s_p * p["explast_b"] + st
        yv = y_ref[...]
        zv = z_ref[...]
        yg = yv * (zv * _sigmoid(zv))
        rstd = lax.rsqrt(jnp.mean(yg * yg, axis=-1, keepdims=True) + NORM_EPS)
        yn_ref[...] = (yg * rstd * nw_ref[...]).astype(BF16)

    grp = lambda width: pl.BlockSpec((None, 1, width), lambda g, c: (g, 0, 0))
    return _pcall(
        body, name=name, grid=(SSD_G, nc),
        out_shape=[jax.ShapeDtypeStruct((L, d_inner), F32), jax.ShapeDtypeStruct((L, d_inner), BF16),
                   jax.ShapeDtypeStruct((nc, SSD_G, SSD_N, gw), F32)],
        in_specs=[pl.BlockSpec((SSD_CHUNK, gw), lambda g, c: (c, g)),
                  pl.BlockSpec((SSD_CHUNK, gw), lambda g, c: (c, g)),
                  pl.BlockSpec((SSD_CHUNK, SSD_N), lambda g, c: (c, bc0 + g)),
                  pl.BlockSpec((SSD_CHUNK, SSD_N), lambda g, c: (c, bc0 + SSD_G + g)),
                  pl.BlockSpec((SSD_CHUNK, LANES), lambda g, c: (c, dt0 + g)),
                  grp(LANES), grp(LANES), grp(gw), grp(gw)],
        out_specs=[pl.BlockSpec((SSD_CHUNK, gw), lambda g, c: (c, g)),
                   pl.BlockSpec((SSD_CHUNK, gw), lambda g, c: (c, g)),
                   pl.BlockSpec((None, None, SSD_N, gw), lambda g, c: (c, g, 0, 0))],
        scratch_shapes=[pltpu.VMEM((SSD_N, gw), F32), pltpu.VMEM((SSD_CHUNK, LANES), F32)],
        compiler_params=_cparams(("parallel", "arbitrary")))(zx, xc, xc, xc, zx, bias_p, alog_p, d_lane, nw)


def _ssd_bwd(dyn, y, zx, xc, prev, bias_p, alog_p, d_lane, nw, d_inner, name):
    L = zx.shape[0]
    nc = L // SSD_CHUNK
    gw = d_inner // SSD_G
    heads = gw // SSD_P
    n_pair = heads // 2
    bc0 = d_inner // LANES
    dt0 = (2 * d_inner + 2 * SSD_G * SSD_N) // LANES

    def body(dyn_ref, y_ref, z_ref, xs_ref, b_ref, c_ref, dt_ref, prev_ref, bias_ref, alog_ref, dl_ref, nw_ref,
             dz_ref, dxs_ref, db_ref, dc_ref, ddt_ref, dbias_ref, dalog_ref, dd_ref, dnw_ref,
             ds_ref, cst_ref, racc_ref):
        @pl.when(pl.program_id(1) == 0)
        def _():
            ds_ref[...] = jnp.zeros_like(ds_ref)
            dbias_ref[...] = jnp.zeros_like(dbias_ref)
            dalog_ref[...] = jnp.zeros_like(dalog_ref)
            dd_ref[...] = jnp.zeros_like(dd_ref)
            dnw_ref[...] = jnp.zeros_like(dnw_ref)

        q = _ssd_common(dt_ref, bias_ref, alog_ref, b_ref, c_ref, cst_ref, heads)
        lane, row, lo = q["lane"], q["row"], q["lo"]
        lane1 = lax.broadcasted_iota(jnp.int32, (1, LANES), 1)

        yv, zv, dynv, nwv = y_ref[...], z_ref[...], dyn_ref[...], nw_ref[...]
        sig = _sigmoid(zv)
        sz = zv * sig
        yg = yv * sz
        rstd = lax.rsqrt(jnp.mean(yg * yg, axis=-1, keepdims=True) + NORM_EPS)
        yhat = yg * rstd
        dnw_ref[...] += jnp.sum(dynv * yhat, axis=0, keepdims=True)
        dyh = dynv * nwv
        dyg = rstd * (dyh - yhat * jnp.mean(dyh * yhat, axis=-1, keepdims=True))
        dz_ref[...] = (dyg * yv * (sig * (1.0 + zv * (1.0 - sig)))).astype(BF16)
        dy_all = dyg * sz

        dg = jnp.zeros((SSD_CHUNK, SSD_CHUNK), F32)
        dc_acc = jnp.zeros((SSD_CHUNK, SSD_N), F32)
        db_acc = jnp.zeros((SSD_CHUNK, SSD_N), F32)
        dcs_mat = jnp.zeros((SSD_CHUNK, LANES), F32)
        ddt_mat = jnp.zeros((SSD_CHUNK, LANES), F32)
        dd_row = jnp.zeros((1, LANES), F32)
        racc_ref[...] = jnp.zeros_like(racc_ref)
        is_last = row == SSD_CHUNK - 1

        for j in range(n_pair):
            sl = slice(j * LANES, (j + 1) * LANES)
            ha, hb = 2 * j, 2 * j + 1
            p = _pair_terms(q, cst_ref, j)
            xs_p = xs_ref[:, sl]
            dyp = dy_all[:, sl]
            xp = xs_p * p["dt_b"]
            xb = xp.astype(BF16)
            s_p = prev_ref[:, sl]
            s_pb = s_p.astype(BF16)
            dsn = ds_ref[:, sl]
            dsnb = dsn.astype(BF16)
            m_f = [q["scores"] * p["lm"][0], q["scores"] * p["lm"][1]]

            t0 = dyp * xs_p
            dd_row = dd_row + jnp.where(lane1 == ha, _sum_all(jnp.where(lo, t0, 0.0)), 0.0) \
                + jnp.where(lane1 == hb, _sum_all(jnp.where(lo, 0.0, t0)), 0.0)
            dxs_p = dl_ref[:, sl] * dyp

            yo = _dot(q["cb"], s_pb) * p["e_b"]
            dcs_b = (dyp * p["e_b"]).astype(BF16)
            dc_acc = dc_acc + _dot(dcs_b, s_pb, "nt")
            ds_yo = _dot(q["cb"], dcs_b, "tn")
            t1 = dyp * yo
            dcs_cols = [jnp.sum(jnp.where(lo, t1, 0.0), axis=1, keepdims=True),
                        jnp.sum(jnp.where(lo, 0.0, t1), axis=1, keepdims=True)]

            t2 = dsn * s_p
            dlast = [p["explast"][0] * _sum_all(jnp.where(lo, t2, 0.0)),
                     p["explast"][1] * _sum_all(jnp.where(lo, 0.0, t2))]
            ds_ref[:, sl] = dsn * p["explast_b"] + ds_yo
            w = _dot(q["bb"], dsnb)
            db_acc = db_acc + _dot((xp * p["dec_b"]).astype(BF16), dsnb, "nt")
            dxp = w * p["dec_b"]
            t3 = w * xp
            e = [jnp.sum(jnp.where(lo, t3, 0.0), axis=1, keepdims=True) * p["dec_cols"][0],
                 jnp.sum(jnp.where(lo, 0.0, t3), axis=1, keepdims=True) * p["dec_cols"][1]]
            for i in range(2):
                dlast[i] = dlast[i] + jnp.sum(e[i], axis=0, keepdims=True)
                dcs_cols[i] = dcs_cols[i] - e[i]

            dyb = dyp.astype(BF16)
            dy_h = [jnp.where(lo, dyp, 0.0).astype(BF16), jnp.where(lo, 0.0, dyp).astype(BF16)]
            dms = [_dot(dy_h[0], xb, "nt"), _dot(dy_h[1], xb, "nt")]
            dxp = dxp + jnp.where(lo, _dot(m_f[0].astype(BF16), dyb, "tn"), _dot(m_f[1].astype(BF16), dyb, "tn"))
            for i, h in enumerate((ha, hb)):
                dg = dg + dms[i] * p["lm"][i]
                qm = dms[i] * m_f[i]
                dcs_cols[i] = dcs_cols[i] + jnp.sum(qm, axis=1, keepdims=True)
                racc_ref[h:h + 1, :] = jnp.sum(qm, axis=0, keepdims=True)

            dxs_ref[:, sl] = dxs_p + dxp * p["dt_b"]
            t4 = dxp * xs_p
            ddt_cols = [jnp.sum(jnp.where(lo, t4, 0.0), axis=1, keepdims=True),
                        jnp.sum(jnp.where(lo, 0.0, t4), axis=1, keepdims=True)]
            for i, h in enumerate((ha, hb)):
                sel = lane == h
                dcs_mat = dcs_mat + jnp.where(sel, dcs_cols[i], 0.0) + jnp.where(sel & is_last, dlast[i], 0.0)
                ddt_mat = ddt_mat + jnp.where(sel, ddt_cols[i], 0.0)

        dcs_mat = dcs_mat - racc_ref[...].T
        tri_t = (row <= lane).astype(BF16)
        da = _dot_exact01(tri_t, dcs_mat)
        ddt = ddt_mat + da * q["a_row"]
        dalog_ref[...] += jnp.sum(jnp.where(q["valid"], da * q["dt"], 0.0), axis=0, keepdims=True) * q["a_row"]
        draw = jnp.where(q["valid"], ddt * _sigmoid(q["raw"]), 0.0)
        ddt_ref[...] = draw
        dbias_ref[...] += jnp.sum(draw, axis=0, keepdims=True)
        dd_ref[...] += dd_row
        dgb = dg.astype(BF16)
        dc_ref[...] = dc_acc + _dot(dgb, q["bb"])
        db_ref[...] = db_acc + _dot(dgb, q["cb"], "tn")

    rev = lambda c: nc - 1 - c
    grp = lambda width: pl.BlockSpec((None, 1, width), lambda g, c: (g, 0, 0))
    blk = lambda width, off: pl.BlockSpec((SSD_CHUNK, width), lambda g, c: (rev(c), off + g))
    return _pcall(
        body, name=name, grid=(SSD_G, nc),
        out_shape=[jax.ShapeDtypeStruct((L, d_inner), BF16), jax.ShapeDtypeStruct((L, d_inner), F32),
                   jax.ShapeDtypeStruct((L, SSD_G * SSD_N), F32), jax.ShapeDtypeStruct((L, SSD_G * SSD_N), F32),
                   jax.ShapeDtypeStruct((L, SSD_G * LANES), F32),
                   jax.ShapeDtypeStruct((SSD_G, 1, LANES), F32), jax.ShapeDtypeStruct((SSD_G, 1, LANES), F32),
                   jax.ShapeDtypeStruct((SSD_G, 1, LANES), F32), jax.ShapeDtypeStruct((SSD_G, 1, gw), F32)],
        in_specs=[blk(gw, 0), blk(gw, 0), blk(gw, 0), blk(gw, 0), blk(SSD_N, bc0), blk(SSD_N, bc0 + SSD_G),
                  blk(LANES, dt0),
                  pl.BlockSpec((None, None, SSD_N, gw), lambda g, c: (rev(c), g, 0, 0)),
                  grp(LANES), grp(LANES), grp(gw), grp(gw)],
        out_specs=[blk(gw, 0), blk(gw, 0), blk(SSD_N, 0), blk(SSD_N, 0), blk(LANES, 0),
                   grp(LANES), grp(LANES), grp(LANES), grp(gw)],
        scratch_shapes=[pltpu.VMEM((SSD_N, gw), F32), pltpu.VMEM((SSD_CHUNK, LANES), F32),
                        pltpu.VMEM((SSD_CHUNK, LANES), F32)],
        compiler_params=_cparams(("parallel", "arbitrary")))(
            dyn, y, zx, xc, xc, xc, zx, prev, bias_p, alog_p, d_lane, nw)


def _cond_mod(c_pad, ada_w, ada_b_loc, name):
    depth, D, n = ada_w.shape
    rows = c_pad.shape[0]

    def body(c_ref, w_ref, b_ref, mod_ref, cond_ref):
        cv = c_ref[...]
        cond = cv * _sigmoid(cv)
        cond_ref[...] = cond
        mod_ref[...] = _dot(cond.astype(BF16), w_ref[...].astype(BF16)) + b_ref[...]

    return _pcall(
        body, name=name, grid=(depth,),
        out_shape=[jax.ShapeDtypeStruct((depth, rows, n), F32), jax.ShapeDtypeStruct((rows, D), F32)],
        in_specs=[pl.BlockSpec((rows, D), lambda i: (0, 0)),
                  pl.BlockSpec((None, D, n), lambda i: (i, 0, 0)),
                  pl.BlockSpec((None, 1, n), lambda i: (i, 0, 0))],
        out_specs=[pl.BlockSpec((None, rows, n), lambda i: (i, 0, 0)),
                   pl.BlockSpec((rows, D), lambda i: (0, 0))],
        compiler_params=_cparams(("arbitrary",)))(c_pad, ada_w, ada_b_loc)


def _adamw_math(g, w, m, v):
    m_new = ADAM_B1 * m + (1.0 - ADAM_B1) * g
    v_new = ADAM_B2 * v + (1.0 - ADAM_B2) * (g * g)
    m_hat = m_new / (1.0 - ADAM_B1 ** ADAM_STEP)
    v_hat = v_new / (1.0 - ADAM_B2 ** ADAM_STEP)
    delta = -ADAM_LR * (m_hat / (jnp.sqrt(v_hat) + ADAM_EPS) + ADAM_WD * w)
    return delta, m_new, v_new


def _adamw_sum(parts, w, m, v, name, tr=256):
    R, C = w.shape
    tr = _tile(R, tr)

    def body(p_ref, w_ref, m_ref, v_ref, g_ref, d_ref, mo_ref, vo_ref):
        g = p_ref[0].astype(F32)
        for k in range(1, N_DEV):
            g = g + p_ref[k].astype(F32)
        d, mn, vn = _adamw_math(g, w_ref[...], m_ref[...], v_ref[...])
        g_ref[...] = g
        d_ref[...] = d
        mo_ref[...] = mn
        vo_ref[...] = vn

    blk = pl.BlockSpec((tr, C), lambda i: (i, 0))
    return _pcall(
        body, name=name, grid=(R // tr,),
        out_shape=[jax.ShapeDtypeStruct((R, C), F32)] * 4,
        in_specs=[pl.BlockSpec((N_DEV, tr, C), lambda i: (0, i, 0)), blk, blk, blk],
        out_specs=[blk] * 4, compiler_params=_cparams(("parallel",)))(parts, w, m, v)


def _ada_adamw(cond_pad, dmod_pad, w, m, v, name, tr=256):
    depth, D, n = w.shape
    rows = cond_pad.shape[0]
    tr = _tile(D, tr)

    def body(c_ref, dm_ref, w_ref, m_ref, v_ref, g_ref, d_ref, mo_ref, vo_ref):
        g = _dot(c_ref[...].astype(BF16), dm_ref[...].astype(BF16), "tn")
        d, mn, vn = _adamw_math(g, w_ref[...], m_ref[...], v_ref[...])
        g_ref[...] = g
        d_ref[...] = d
        mo_ref[...] = mn
        vo_ref[...] = vn

    blk = pl.BlockSpec((None, tr, n), lambda i, r: (i, r, 0))
    return _pcall(
        body, name=name, grid=(depth, D // tr),
        out_shape=[jax.ShapeDtypeStruct((depth, D, n), F32)] * 4,
        in_specs=[pl.BlockSpec((rows, tr), lambda i, r: (0, r)),
                  pl.BlockSpec((None, rows, n), lambda i, r: (i, 0, 0)), blk, blk, blk],
        out_specs=[blk] * 4, compiler_params=_cparams(("parallel", "parallel")))(cond_pad, dmod_pad, w, m, v)


def _pad_heads(v, heads_per_group):
    lead = v.shape[:-1]
    v = v.reshape(lead + (SSD_G, heads_per_group))
    v = jnp.pad(v, [(0, 0)] * len(lead) + [(0, 0), (0, LANES - heads_per_group)])
    return v.reshape(lead + (SSD_G * LANES,))


def _unpad_heads(v, heads_per_group):
    lead = v.shape[:-1]
    v = v.reshape(lead + (SSD_G, LANES))[..., :heads_per_group]
    return v.reshape(lead + (SSD_G * heads_per_group,))


def kernel(x, c, ada_w, ada_b, mix_norm_w, mlp_norm_w, mlp_up, mlp_down, ssd_in_w, ssd_conv_w, ssd_conv_b, ssd_dt_bias, ssd_A_log, ssd_D, ssd_norm_w, ssd_out_w, sc_in_w, sc_conv_w, sc_out_w, final_norm_w, loss_target, m_ada_w, m_ada_b, m_mix_norm_w, m_mlp_norm_w, m_mlp_up, m_mlp_down, m_ssd_in_w, m_ssd_conv_w, m_ssd_conv_b, m_ssd_dt_bias, m_ssd_A_log, m_ssd_D, m_ssd_norm_w, m_ssd_out_w, m_sc_in_w, m_sc_conv_w, m_sc_out_w, m_final_norm_w, v_ada_w, v_ada_b, v_mix_norm_w, v_mlp_norm_w, v_mlp_up, v_mlp_down, v_ssd_in_w, v_ssd_conv_w, v_ssd_conv_b, v_ssd_dt_bias, v_ssd_A_log, v_ssd_D, v_ssd_norm_w, v_ssd_out_w, v_sc_in_w, v_sc_conv_w, v_sc_out_w, v_final_norm_w):
    weights = dict(ada_w=ada_w, ada_b=ada_b, mix_norm_w=mix_norm_w, mlp_norm_w=mlp_norm_w, mlp_up=mlp_up,
                   mlp_down=mlp_down, ssd_in_w=ssd_in_w, ssd_conv_w=ssd_conv_w, ssd_conv_b=ssd_conv_b,
                   ssd_dt_bias=ssd_dt_bias, ssd_A_log=ssd_A_log, ssd_D=ssd_D, ssd_norm_w=ssd_norm_w,
                   ssd_out_w=ssd_out_w, sc_in_w=sc_in_w, sc_conv_w=sc_conv_w, sc_out_w=sc_out_w,
                   final_norm_w=final_norm_w)
    moms = dict(ada_w=m_ada_w, ada_b=m_ada_b, mix_norm_w=m_mix_norm_w, mlp_norm_w=m_mlp_norm_w, mlp_up=m_mlp_up,
                mlp_down=m_mlp_down, ssd_in_w=m_ssd_in_w, ssd_conv_w=m_ssd_conv_w, ssd_conv_b=m_ssd_conv_b,
                ssd_dt_bias=m_ssd_dt_bias, ssd_A_log=m_ssd_A_log, ssd_D=m_ssd_D, ssd_norm_w=m_ssd_norm_w,
                ssd_out_w=m_ssd_out_w, sc_in_w=m_sc_in_w, sc_conv_w=m_sc_conv_w, sc_out_w=m_sc_out_w,
                final_norm_w=m_final_norm_w)
    vars_ = dict(ada_w=v_ada_w, ada_b=v_ada_b, mix_norm_w=v_mix_norm_w, mlp_norm_w=v_mlp_norm_w, mlp_up=v_mlp_up,
                 mlp_down=v_mlp_down, ssd_in_w=v_ssd_in_w, ssd_conv_w=v_ssd_conv_w, ssd_conv_b=v_ssd_conv_b,
                 ssd_dt_bias=v_ssd_dt_bias, ssd_A_log=v_ssd_A_log, ssd_D=v_ssd_D, ssd_norm_w=v_ssd_norm_w,
                 ssd_out_w=v_ssd_out_w, sc_in_w=v_sc_in_w, sc_conv_w=v_sc_conv_w, sc_out_w=v_sc_out_w,
                 final_norm_w=v_final_norm_w)
    names = list(weights)

    L, D = x.shape[1], x.shape[2]
    d_inner = 2 * D
    n_heads = d_inner // SSD_P
    hpg = n_heads // SSD_G
    gw = d_inner // SSD_G
    conv_dim = d_inner + 2 * SSD_G * SSD_N
    zx_dim = d_inner + conv_dim
    me = _my_index()
    x0 = x[0]
    tgt = loss_target[0]

    n_mod = ada_w.shape[2]
    (c_all,) = _exchange([c], "gather_c", gather=True)
    c_pad = jnp.pad(c_all.reshape(N_DEV, D), ((0, 16 - N_DEV), (0, 0)))
    ada_b_loc = lax.dynamic_slice_in_dim(ada_b, me * n_mod, n_mod, axis=1).reshape(2, 1, n_mod)
    mod_blk, cond_pad = _cond_mod(c_pad, ada_w, ada_b_loc, "cond_mod")

    gather_order = ["mod", "ssd_conv_w", "sc_conv_w", "ssd_in_w", "ssd_out_w", "up0", "down0", "sc_in_w",
                    "sc_out_w", "up1", "down1"]
    gather_src = dict(mod=mod_blk, ssd_conv_w=ssd_conv_w[0], sc_conv_w=sc_conv_w[0],
                      ssd_in_w=ssd_in_w[0].astype(BF16), ssd_out_w=ssd_out_w[0].astype(BF16),
                      up0=mlp_up[0].astype(BF16), down0=mlp_down[0].astype(BF16),
                      sc_in_w=sc_in_w[0].astype(BF16), sc_out_w=sc_out_w[0].astype(BF16),
                      up1=mlp_up[1].astype(BF16), down1=mlp_down[1].astype(BF16))
    handles, gather_token = _xfer_start([gather_src[k] for k in gather_order], "gather_start", gather=True)
    gather_handle = dict(zip(gather_order, handles))

    def gathered(k, after):
        return _xfer_wait(gather_handle[k], after, f"gather_wait_{k}", gather=True)

    mod_all = gathered("mod", gather_token)
    mod_mine = lax.dynamic_index_in_dim(mod_all, me, axis=2, keepdims=False)
    mod_mine = jnp.transpose(mod_mine, (1, 0, 2)).reshape(2, 6, 1, D)
    sh_m, sc_m, g_m, sh_f, sc_f, g_f = [[mod_mine[i, k] for i in range(2)] for k in range(6)]

    vec = lambda a: a.reshape(1, -1)
    grads = {}
    small = {}

    _, h0 = _norm_mod_fwd(x0, None, None, vec(mix_norm_w[0]), sc_m[0], sh_m[0], "l0_mix_norm")
    ssd_in_g = gathered("ssd_in_w", h0)
    in_dim = ssd_in_g.shape[2] * N_DEV
    w_in_nat = jnp.transpose(ssd_in_g, (1, 0, 2)).reshape(D, in_dim)
    w_in_all = jnp.concatenate([w_in_nat[:, :zx_dim], _pad_heads(w_in_nat[:, zx_dim:], hpg)], axis=1)
    (zx,) = _mm_nn(h0, w_in_all, F32, "ssd_in_proj", tm=2048, tn=512)
    conv_b0 = vec(ssd_conv_b[0])
    conv_w_full = jnp.transpose(gathered("ssd_conv_w", h0), (1, 0, 2)).reshape(SSD_K, conv_dim)
    sc_conv_full = jnp.transpose(gathered("sc_conv_w", h0), (1, 0, 2)).reshape(SC_K, D)
    xc = _ssd_conv_fwd(zx, conv_w_full, conv_b0, d_inner, conv_dim, "ssd_conv")
    bias_p = _pad_heads(ssd_dt_bias[0], hpg).reshape(SSD_G, 1, LANES)
    alog_p = _pad_heads(ssd_A_log[0], hpg).reshape(SSD_G, 1, LANES)
    d_lane = jnp.repeat(ssd_D[0], SSD_P).reshape(SSD_G, 1, gw)
    nw_g = ssd_norm_w[0].reshape(SSD_G, 1, gw)
    y_ssd, yn, prev = _ssd_fwd(zx, xc, bias_p, alog_p, d_lane, nw_g, d_inner, "ssd_scan")
    w_ssd_out = gathered("ssd_out_w", yn).reshape(-1, D)
    (mix0,) = _mm_nn(yn, w_ssd_out, F32, "ssd_out_proj")
    x1, h1 = _norm_mod_fwd(x0, mix0, g_m[0], vec(mlp_norm_w[0]), sc_f[0], sh_f[0], "l0_mlp_norm")
    ups, downs = [None, None], [None, None]
    ups[0] = gathered("up0", h1)
    u0, s0 = _mm_nn_blocked(h1, ups[0], "l0_mlp_up", _ep_relu2, [BF16, BF16])
    downs[0] = gathered("down0", s0).reshape(-1, D)
    (d0,) = _mm_nn(s0, downs[0], F32, "l0_mlp_down")
    x2, h2 = _norm_mod_fwd(x1, d0, g_f[0], vec(mix_norm_w[1]), sc_m[1], sh_m[1], "l1_mix_norm")
    sc_in_g = gathered("sc_in_w", h2)
    (proj,) = _mm_nn_blocked(h2, sc_in_g, "sc_in_proj", _ep_store(F32), [F32])
    yc = _sc_conv_fwd(proj, sc_conv_full, "sc_conv")
    w_sc_out = gathered("sc_out_w", yc).reshape(-1, D)
    (mix1,) = _mm_nn(yc, w_sc_out, F32, "sc_out_proj")
    x3, h3 = _norm_mod_fwd(x2, mix1, g_m[1], vec(mlp_norm_w[1]), sc_f[1], sh_f[1], "l1_mlp_norm")
    ups[1] = gathered("up1", h3)
    u1, s1 = _mm_nn_blocked(h3, ups[1], "l1_mlp_up", _ep_relu2, [BF16, BF16])
    downs[1] = gathered("down1", s1).reshape(-1, D)
    (d1,) = _mm_nn(s1, downs[1], F32, "l1_mlp_down")

    dx, loss_lane, dfw = _final_loss(x3, d1, g_f[1], vec(final_norm_w), tgt, "final_loss")
    loss = lax.psum(0.5 * jnp.sum(loss_lane) / D, MESH_AXES)
    small["final_norm_w"] = dfw

    dmod = [[None] * 6 for _ in range(2)]
    big = {}

    def mlp_backward(i, dx_out, d_out, x_mid, h_in, u, s):
        dd, dg = _gate_bwd(dx_out, d_out, g_f[i], f"l{i}_mlp_gate_bwd")
        dmod[i][5] = dg
        du = _mm_nt(dd, downs[i], BF16, f"l{i}_mlp_down_bwd", epilogue=_ep_relu2_bwd, extra=(u,))
        gdown = _mm_tn(s, dd, BF16, f"l{i}_mlp_down_wgrad").reshape(N_DEV, -1, D)
        gup = _mm_tn_blocked(h_in, du, BF16, f"l{i}_mlp_up_wgrad")
        (h_down, h_up), token = _xfer_start([gdown, gup], f"l{i}_mlp_grads_start", gather=False)
        grad_handle[f"mlp_down{i}"], grad_handle[f"mlp_up{i}"] = h_down, h_up
        dh = _mm_nt_blocked(du, ups[i], F32, f"l{i}_mlp_up_bwd")
        dxm, dsh, dsc, dnw = _norm_mod_bwd(dh, x_mid, _tie(vec(mlp_norm_w[i]), token), sc_f[i], dx_out,
                                           f"l{i}_mlp_norm_bwd")
        dmod[i][3], dmod[i][4] = dsh, dsc
        return dxm, dnw

    grad_handle = {}
    dx3, dnw_mlp1 = mlp_backward(1, dx, d1, x3, h3, u1, s1)
    dyc, dg = _gate_bwd(dx3, mix1, g_m[1], "l1_mix_gate_bwd")
    dmod[1][2] = dg
    g_sc_out = _mm_tn(yc, dyc, BF16, "sc_out_wgrad").reshape(N_DEV, -1, D)
    dconv_out = _mm_nt(dyc, w_sc_out, F32, "sc_out_bwd")
    dbg, dcg, dxv, dscw = _sc_conv_bwd(proj, sc_conv_full, dconv_out, "sc_conv_bwd")
    dproj = jnp.concatenate([dbg, dcg, dxv], axis=1)
    g_sc_in = _mm_tn_blocked(h2, dproj, BF16, "sc_in_wgrad")
    (grad_handle["sc_out_w0"], grad_handle["sc_in_w0"]), token = _xfer_start(
        [g_sc_out, g_sc_in], "sc_grads_start", gather=False)
    dh2 = _mm_nt_blocked(dproj, sc_in_g, F32, "sc_in_bwd")
    dx2, dsh, dsc, dnw_mix1 = _norm_mod_bwd(dh2, x2, _tie(vec(mix_norm_w[1]), token), sc_m[1], dx3,
                                            "l1_mix_norm_bwd")
    dmod[1][0], dmod[1][1] = dsh, dsc
    dx1, dnw_mlp0 = mlp_backward(0, dx2, d0, x1, h1, u0, s0)
    dyo, dg = _gate_bwd(dx1, mix0, g_m[0], "l0_mix_gate_bwd")
    dmod[0][2] = dg
    g_ssd_out = _mm_tn(yn, dyo, BF16, "ssd_out_wgrad").reshape(N_DEV, -1, D)
    (grad_handle["ssd_out_w0"],), token = _xfer_start([g_ssd_out], "ssd_out_grad_start", gather=False)
    dyn = _mm_nt(dyo, w_ssd_out, F32, "ssd_out_bwd")
    dz, dxs, db_, dc_, ddt, dbias, dalog, dd_, dnw_ssd = _ssd_bwd(
        dyn, y_ssd, zx, xc, prev, bias_p, alog_p, d_lane, _tie(nw_g, token), d_inner, "ssd_scan_bwd")
    dxc = jnp.concatenate([dxs, db_, dc_], axis=1)
    dxbc, dcw, dcb = _ssd_conv_bwd(zx, conv_w_full, conv_b0, dxc, d_inner, "ssd_conv_bwd")
    dzx = jnp.concatenate([dz, dxbc, ddt.astype(BF16)], axis=1)
    g_in_all = _mm_tn(h0, dzx, BF16, "ssd_in_wgrad", tn=512, tk=2048)
    g_in_nat = jnp.concatenate([g_in_all[:, :zx_dim], _unpad_heads(g_in_all[:, zx_dim:], hpg)], axis=1)
    g_ssd_in = jnp.transpose(g_in_nat.reshape(D, N_DEV, in_dim // N_DEV), (1, 0, 2))
    (grad_handle["ssd_in_w0"],), token = _xfer_start([g_ssd_in], "ssd_in_grad_start", gather=False)
    dh0 = _mm_nt(dzx, w_in_all, F32, "ssd_in_bwd", tk=dzx.shape[1] // 2)
    grad_x, dsh, dsc, dnw_mix0 = _norm_mod_bwd(dh0, x0, _tie(vec(mix_norm_w[0]), token), sc_m[0], dx1,
                                               "l0_mix_norm_bwd")
    dmod[0][0], dmod[0][1] = dsh, dsc

    small["mix_norm_w"] = jnp.concatenate([dnw_mix0, dnw_mix1], axis=0)
    small["mlp_norm_w"] = jnp.concatenate([dnw_mlp0, dnw_mlp1], axis=0)
    small["ssd_conv_w"] = dcw
    small["ssd_conv_b"] = dcb
    small["ssd_dt_bias"] = _unpad_heads(dbias.reshape(SSD_G * LANES), hpg)
    small["ssd_A_log"] = _unpad_heads(dalog.reshape(SSD_G * LANES), hpg)
    small["ssd_D"] = _unpad_heads(dd_.reshape(SSD_G * LANES), hpg)
    small["ssd_norm_w"] = dnw_ssd
    small["sc_conv_w"] = dscw
    small["dmod"] = jnp.concatenate([jnp.concatenate(dmod[i], axis=1) for i in range(2)], axis=0)

    small_order = ["dmod", "mix_norm_w", "mlp_norm_w", "ssd_conv_w", "ssd_conv_b", "ssd_dt_bias", "ssd_A_log",
                   "ssd_D", "ssd_norm_w", "sc_conv_w", "final_norm_w"]
    flat = jnp.concatenate([small[k].reshape(-1) for k in small_order])
    n_small = flat.shape[0]
    n_small_pad = -(-n_small // 1024) * 1024
    flat = jnp.pad(flat, (0, n_small_pad - n_small)).reshape(n_small_pad // LANES, LANES)
    (small_handle,), _ = _xfer_start([flat], "small_grads_start", gather=True)
    offs, o = {}, 0
    for k in small_order:
        offs[k] = (o, small[k].size, small[k].shape)
        o += small[k].size

    def small_parts(k):
        o, n, shape = offs[k]
        return small_all[:, o:o + n].reshape((N_DEV,) + shape)

    out_g, out_d, out_m, out_v = {}, {}, {}, {}

    def big_update(name, n_layers, after):
        w_, m_, v_ = weights[name], moms[name], vars_[name]
        res = []
        for i in range(n_layers):
            parts = _xfer_wait(grad_handle[f"{name}{i}"], after, f"grads_wait_{name}_{i}", gather=False)
            res.append(_adamw_sum(parts, w_[i], m_[i], v_[i], f"adamw_{name}_{i}"))
        for k, dst in enumerate((out_g, out_d, out_m, out_v)):
            dst[name] = jnp.stack([r[k] for r in res], axis=0)

    big_update("mlp_up", 2, grad_x)
    big_update("mlp_down", 2, grad_x)
    big_update("sc_out_w", 1, grad_x)
    big_update("sc_in_w", 1, grad_x)
    big_update("ssd_out_w", 1, grad_x)
    small_all = _xfer_wait(small_handle, out_d["ssd_out_w"], "small_grads_wait", gather=True)
    small_all = small_all.reshape(N_DEV, n_small_pad)

    dmod_all = small_parts("dmod")
    dmod_loc = lax.dynamic_slice_in_dim(dmod_all, me * n_mod, n_mod, axis=2)
    dmod_pad = jnp.pad(jnp.transpose(dmod_loc, (1, 0, 2)), ((0, 0), (0, 16 - N_DEV), (0, 0)))
    out_g["ada_w"], out_d["ada_w"], out_m["ada_w"], out_v["ada_w"] = _ada_adamw(
        cond_pad, dmod_pad, ada_w, m_ada_w, v_ada_w, "adamw_ada_w")

    pieces = []
    pieces.append(("ada_b", dmod_all.reshape(N_DEV, -1)))
    for k in ["mix_norm_w", "mlp_norm_w", "ssd_conv_b", "ssd_dt_bias", "ssd_A_log", "ssd_D", "ssd_norm_w",
              "final_norm_w"]:
        pieces.append((k, small_parts(k).reshape(N_DEV, -1)))
    n_cw = ssd_conv_w.shape[2]
    pieces.append(("ssd_conv_w", lax.dynamic_slice_in_dim(small_parts("ssd_conv_w"), me * n_cw, n_cw, axis=2)
                   .reshape(N_DEV, -1)))
    n_scw = sc_conv_w.shape[2]
    pieces.append(("sc_conv_w", lax.dynamic_slice_in_dim(small_parts("sc_conv_w"), me * n_scw, n_scw, axis=2)
                   .reshape(N_DEV, -1)))
    n_tot = sum(p.shape[1] for _, p in pieces)
    n_tot_pad = -(-n_tot // 1024) * 1024

    def pack(arrs, lead=()):
        f = jnp.concatenate(arrs, axis=-1)
        f = jnp.pad(f, [(0, 0)] * len(lead) + [(0, n_tot_pad - n_tot)])
        return f.reshape(lead + (n_tot_pad // LANES, LANES))

    parts_flat = pack([p for _, p in pieces], lead=(N_DEV,))
    w_flat = pack([weights[k].reshape(-1) for k, _ in pieces])
    m_flat = pack([moms[k].reshape(-1) for k, _ in pieces])
    v_flat = pack([vars_[k].reshape(-1) for k, _ in pieces])
    res = _adamw_sum(parts_flat, w_flat, m_flat, v_flat, "adamw_small", tr=n_tot_pad // LANES)
    o = 0
    for k, p in pieces:
        n = p.shape[1]
        for r, dst in zip(res, (out_g, out_d, out_m, out_v)):
            dst[k] = r.reshape(-1)[o:o + n].reshape(weights[k].shape)
        o += n
    big_update("ssd_in_w", 1, res[1])

    return (loss, grad_x[None], *[out_g[k] for k in names], *[out_d[k] for k in names],
            *[out_m[k] for k in names], *[out_v[k] for k in names])
```

```python
import functools

import jax
import jax.numpy as jnp
from jax import lax
from jax.experimental import pallas as pl
from jax.experimental.pallas import tpu as pltpu

F32 = jnp.float32
BF16 = jnp.bfloat16
N_DEV = 8
MESH_AXES = ("x", "y", "c")
MESH = pl.DeviceIdType.MESH

NORM_EPS = 1e-5
SSD_G = 4
SSD_P = 64
SSD_N = 128
SSD_CHUNK = 128
SSD_K = 4
SC_K = 3
LANES = 128

ADAM_LR = 0.001
ADAM_B1 = 0.9
ADAM_B2 = 0.999
ADAM_EPS = 1e-08
ADAM_WD = 0.01
ADAM_STEP = 10

VMEM_LIMIT = 56 * 1024 * 1024


def _pcall(body, **kw):
    return pl.pallas_call(body, **kw)


def _cparams(sem=None):
    if sem is None:
        return pltpu.CompilerParams(vmem_limit_bytes=VMEM_LIMIT)
    return pltpu.CompilerParams(dimension_semantics=sem, vmem_limit_bytes=VMEM_LIMIT)


def _my_index():
    return 4 * lax.axis_index("x") + 2 * lax.axis_index("y") + lax.axis_index("c")


_PEER_MASKS = [(0, 0, 1), (0, 1, 0), (0, 1, 1), (1, 0, 0), (1, 0, 1), (1, 1, 0), (1, 1, 1)]


def _peers():
    x, y, c = lax.axis_index("x"), lax.axis_index("y"), lax.axis_index("c")
    out = []
    for mx, my, mc in _PEER_MASKS:
        px = (1 - x) if mx else x
        py = (1 - y) if my else y
        pc = (1 - c) if mc else c
        out.append(((px, py, pc), 4 * px + 2 * py + pc))
    return out


def _exchange(arrs, name, gather):
    n = len(arrs)
    n_peer = N_DEV - 1

    def body(*refs):
        ins, outs = refs[:n], refs[n:2 * n]
        send_sems, recv_sems, local_sems = refs[2 * n:]
        me = _my_index()
        peers = _peers()
        started = []
        for a in range(n):
            src_own = ins[a] if gather else ins[a].at[me]
            own = pltpu.make_async_copy(src_own, outs[a].at[me], local_sems.at[a])
            own.start()
            started.append(own)
        sends = []
        for a in range(n):
            for k, (peer, pidx) in enumerate(peers):
                src = ins[a] if gather else ins[a].at[pidx]
                cp = pltpu.make_async_remote_copy(
                    src_ref=src, dst_ref=outs[a].at[me],
                    send_sem=send_sems.at[a * n_peer + k], recv_sem=recv_sems.at[a * n_peer + k],
                    device_id=peer, device_id_type=MESH)
                cp.start()
                sends.append(cp)
        for a in range(n):
            for k, (peer, pidx) in enumerate(peers):
                src = ins[a] if gather else ins[a].at[pidx]
                pltpu.make_async_remote_copy(
                    src_ref=src, dst_ref=outs[a].at[pidx],
                    send_sem=send_sems.at[a * n_peer + k], recv_sem=recv_sems.at[a * n_peer + k],
                    device_id=peer, device_id_type=MESH).wait_recv()
        for cp in sends:
            cp.wait_send()
        for own in started:
            own.wait()

    if gather:
        out_shape = [jax.ShapeDtypeStruct((N_DEV,) + a.shape, a.dtype) for a in arrs]
    else:
        out_shape = [jax.ShapeDtypeStruct(a.shape, a.dtype) for a in arrs]
    any_spec = pl.BlockSpec(memory_space=pl.ANY)
    outs = _pcall(
        body, name=name, out_shape=out_shape,
        in_specs=[any_spec] * n, out_specs=[any_spec] * n,
        scratch_shapes=[pltpu.SemaphoreType.DMA((n * n_peer,)), pltpu.SemaphoreType.DMA((n * n_peer,)),
                        pltpu.SemaphoreType.DMA((n,))],
        compiler_params=pltpu.CompilerParams(has_side_effects=True),
    )(*arrs)
    return list(outs)


_HBM = pl.BlockSpec(memory_space=pltpu.HBM)
_SEM = pl.BlockSpec(memory_space=pltpu.SEMAPHORE)
_DATAFLOW = pltpu.SideEffectType.DATAFLOW_SIDE_EFFECTING


def _xfer_start(arrs, name, gather):
    n = len(arrs)
    n_peer = N_DEV - 1

    def body(*refs):
        ins, lands = refs[:n], refs[n:2 * n]
        sems = refs[2 * n:5 * n]
        token = refs[-1]
        me = _my_index()
        peers = _peers()
        for a in range(n):
            send_sems, recv_sems, loc_sem = sems[3 * a:3 * a + 3]
            src_own = ins[a] if gather else ins[a].at[me]
            pltpu.make_async_copy(src_own, lands[a].at[me], loc_sem).start()
            for k, (peer, pidx) in enumerate(peers):
                src = ins[a] if gather else ins[a].at[pidx]
                pltpu.make_async_remote_copy(
                    src_ref=src, dst_ref=lands[a].at[me], send_sem=send_sems.at[k], recv_sem=recv_sems.at[k],
                    device_id=peer, device_id_type=MESH).start()
        token[...] = jnp.zeros_like(token)

    land_shapes = [((N_DEV,) + a.shape) if gather else a.shape for a in arrs]
    out_shape, out_specs = [], []
    for _ in range(n):
        out_shape += [pltpu.SemaphoreType.DMA((n_peer,)), pltpu.SemaphoreType.DMA((n_peer,)),
                      pltpu.SemaphoreType.DMA(())]
        out_specs += [_SEM, _SEM, _SEM]
    out_shape += [pltpu.HBM(a.shape, a.dtype) for a in arrs]
    out_shape += [pltpu.HBM(s, a.dtype) for s, a in zip(land_shapes, arrs)]
    out_shape += [jax.ShapeDtypeStruct((8, LANES), F32)]
    out_specs += [_HBM] * (2 * n) + [pl.BlockSpec(memory_space=pltpu.VMEM)]
    aliases = {}
    for a in range(n):
        aliases[a] = 3 * n + a
        aliases[n + a] = 4 * n + a
    operands = [pltpu.with_memory_space_constraint(a, pltpu.HBM) for a in arrs]
    operands += [pltpu.with_memory_space_constraint(lax.empty(s, a.dtype), pltpu.HBM)
                 for s, a in zip(land_shapes, arrs)]
    outs = _pcall(
        body, name=name, out_shape=tuple(out_shape), in_specs=[_HBM] * (2 * n), out_specs=tuple(out_specs),
        input_output_aliases=aliases,
        compiler_params=pltpu.CompilerParams(has_side_effects=_DATAFLOW),
    )(*operands)
    handles = []
    for a in range(n):
        handles.append((outs[3 * n + a], outs[4 * n + a], outs[3 * a], outs[3 * a + 1], outs[3 * a + 2]))
    return handles, outs[-1]


def _xfer_wait(handle, after, name, gather):
    src_thru, land_thru, send_sems, recv_sems, loc_sem = handle

    def body(src_ref, land_ref, send_ref, recv_ref, loc_ref, after_ref, src_dead, got_ref):
        me = _my_index()
        src_own = src_ref if gather else src_ref.at[me]
        pltpu.make_async_copy(src_own, land_ref.at[me], loc_ref).wait()
        for k, (peer, pidx) in enumerate(_peers()):
            src = src_ref if gather else src_ref.at[pidx]
            cp = pltpu.make_async_remote_copy(
                src_ref=src, dst_ref=land_ref.at[pidx], send_sem=send_ref.at[k], recv_sem=recv_ref.at[k],
                device_id=peer, device_id_type=MESH)
            cp.wait_send()
            cp.wait_recv()

    return _pcall(
        body, name=name,
        out_shape=(pltpu.HBM(src_thru.shape, src_thru.dtype), pltpu.HBM(land_thru.shape, land_thru.dtype)),
        in_specs=[_HBM, _HBM, _SEM, _SEM, _SEM, pl.BlockSpec(memory_space=pl.ANY)], out_specs=(_HBM, _HBM),
        input_output_aliases={0: 0, 1: 1},
        compiler_params=pltpu.CompilerParams(has_side_effects=_DATAFLOW),
    )(src_thru, land_thru, send_sems, recv_sems, loc_sem, after)[1]


_DIMS = {"nn": (((1,), (0,)), ((), ())), "nt": (((1,), (1,)), ((), ())), "tn": (((0,), (0,)), ((), ()))}


def _dot(a, b, mode="nn"):
    return lax.dot_general(a, b, _DIMS[mode], preferred_element_type=F32)


def _mm(a, b, *, mode, grid, a_spec, b_spec, out_shape, out_specs, acc_shape, epilogue, name,
        extra=(), extra_specs=(), after=()):
    nk = grid[2]
    n_extra = len(extra)
    n_in = 2 + n_extra + len(after)

    def body_single(*refs):
        a_ref, b_ref = refs[0], refs[1]
        epilogue(_dot(a_ref[...], b_ref[...], mode), refs[2:2 + n_extra], refs[n_in:])

    def body_acc(*refs):
        a_ref, b_ref = refs[0], refs[1]
        ex = refs[2:2 + n_extra]
        outs = refs[n_in:-1]
        acc = refs[-1]
        k = pl.program_id(2)

        @pl.when(k == 0)
        def _():
            acc[...] = jnp.zeros_like(acc)

        acc[...] += _dot(a_ref[...], b_ref[...], mode)

        @pl.when(k == nk - 1)
        def _():
            epilogue(acc[...], ex, outs)

    return _pcall(
        body_single if nk == 1 else body_acc, name=name, grid=grid, out_shape=out_shape,
        in_specs=[a_spec, b_spec] + list(extra_specs) + [pl.BlockSpec(memory_space=pl.ANY)] * len(after),
        out_specs=out_specs,
        scratch_shapes=[] if nk == 1 else [pltpu.VMEM(acc_shape, F32)],
        compiler_params=_cparams(("parallel", "parallel", "arbitrary")),
    )(a, b, *extra, *after)


def _ep_store(dtype):
    def ep(acc, ex, outs):
        outs[0][...] = acc.astype(dtype)
    return ep


def _ep_relu2(acc, ex, outs):
    outs[0][...] = acc.astype(BF16)
    r = jnp.maximum(acc, 0.0)
    outs[1][...] = (r * r).astype(BF16)


def _ep_relu2_bwd(acc, ex, outs):
    u = ex[0][...].astype(F32)
    outs[0][...] = (acc * (2.0 * jnp.maximum(u, 0.0))).astype(BF16)


def _tile(n, want):
    t = min(n, want)
    while n % t:
        t //= 2
    return t


def _mm_nn(a, w, out_dtype, name, tm=1024, tn=1024, tk=1024, epilogue=None, out_dtypes=None):
    M, K = a.shape
    N = w.shape[1]
    tm, tn, tk = _tile(M, tm), _tile(N, tn), _tile(K, tk)
    out_dtypes = out_dtypes or [out_dtype]
    return _mm(a, w, mode="nn", grid=(M // tm, N // tn, K // tk),
               a_spec=pl.BlockSpec((tm, tk), lambda i, j, k: (i, k)),
               b_spec=pl.BlockSpec((tk, tn), lambda i, j, k: (k, j)),
               out_shape=[jax.ShapeDtypeStruct((M, N), d) for d in out_dtypes],
               out_specs=[pl.BlockSpec((tm, tn), lambda i, j, k: (i, j)) for _ in out_dtypes],
               acc_shape=(tm, tn), epilogue=epilogue or _ep_store(out_dtype), name=name)


def _mm_nn_blocked(a, wg, name, epilogue, out_dtypes, tm=2048):
    M, K = a.shape
    n = wg.shape[2]
    tm = _tile(M, tm)
    return _mm(a, wg, mode="nn", grid=(M // tm, N_DEV, 1),
               a_spec=pl.BlockSpec((tm, K), lambda i, j, k: (i, 0)),
               b_spec=pl.BlockSpec((None, K, n), lambda i, j, k: (j, 0, 0)),
               out_shape=[jax.ShapeDtypeStruct((M, N_DEV * n), d) for d in out_dtypes],
               out_specs=[pl.BlockSpec((tm, n), lambda i, j, k: (i, j)) for _ in out_dtypes],
               acc_shape=(tm, n), epilogue=epilogue, name=name)


def _mm_nt(a, w, out_dtype, name, tm=1024, tn=1024, tk=1024, epilogue=None, extra=(), extra_specs=(),
           after=()):
    M, K = a.shape
    N = w.shape[0]
    tm, tn, tk = _tile(M, tm), _tile(N, tn), _tile(K, tk)
    if extra and not extra_specs:
        extra_specs = [pl.BlockSpec((tm, tn), lambda i, j, k: (i, j)) for _ in extra]
    return _mm(a, w, mode="nt", grid=(M // tm, N // tn, K // tk),
               a_spec=pl.BlockSpec((tm, tk), lambda i, j, k: (i, k)),
               b_spec=pl.BlockSpec((tn, tk), lambda i, j, k: (j, k)),
               out_shape=[jax.ShapeDtypeStruct((M, N), out_dtype)],
               out_specs=[pl.BlockSpec((tm, tn), lambda i, j, k: (i, j))],
               acc_shape=(tm, tn), epilogue=epilogue or _ep_store(out_dtype), name=name,
               extra=extra, extra_specs=extra_specs, after=after)[0]


def _mm_nt_blocked(a, wg, out_dtype, name, tm=1024, after=()):
    M = a.shape[0]
    kout, n = wg.shape[1], wg.shape[2]
    tm = _tile(M, tm)
    return _mm(a, wg, mode="nt", grid=(M // tm, 1, N_DEV),
               a_spec=pl.BlockSpec((tm, n), lambda i, j, k: (i, k)),
               b_spec=pl.BlockSpec((None, kout, n), lambda i, j, k: (k, 0, 0)),
               out_shape=[jax.ShapeDtypeStruct((M, kout), out_dtype)],
               out_specs=[pl.BlockSpec((tm, kout), lambda i, j, k: (i, 0))],
               acc_shape=(tm, kout), epilogue=_ep_store(out_dtype), name=name, after=after)[0]


def _mm_tn(a, b, out_dtype, name, tm=1024, tn=1024, tk=1024):
    K, M = a.shape
    N = b.shape[1]
    tm, tn, tk = _tile(M, tm), _tile(N, tn), _tile(K, tk)
    return _mm(a, b, mode="tn", grid=(M // tm, N // tn, K // tk),
               a_spec=pl.BlockSpec((tk, tm), lambda i, j, k: (k, i)),
               b_spec=pl.BlockSpec((tk, tn), lambda i, j, k: (k, j)),
               out_shape=[jax.ShapeDtypeStruct((M, N), out_dtype)],
               out_specs=[pl.BlockSpec((tm, tn), lambda i, j, k: (i, j))],
               acc_shape=(tm, tn), epilogue=_ep_store(out_dtype), name=name)[0]


def _mm_tn_blocked(a, b, out_dtype, name, tm=1024, tk=2048):
    K, M = a.shape
    n = b.shape[1] // N_DEV
    tm, tk = _tile(M, tm), _tile(K, tk)
    return _mm(a, b, mode="tn", grid=(M // tm, N_DEV, K // tk),
               a_spec=pl.BlockSpec((tk, tm), lambda i, j, k: (k, i)),
               b_spec=pl.BlockSpec((tk, n), lambda i, j, k: (k, j)),
               out_shape=[jax.ShapeDtypeStruct((N_DEV, M, n), out_dtype)],
               out_specs=[pl.BlockSpec((None, tm, n), lambda i, j, k: (j, i, 0))],
               acc_shape=(tm, n), epilogue=_ep_store(out_dtype), name=name)[0]


def _sigmoid(x):
    return 1.0 / (1.0 + jnp.exp(-x))


def _row_spec(tm, d):
    return pl.BlockSpec((tm, d), lambda i: (i, 0))


def _vec_spec(d):
    return pl.BlockSpec((1, d), lambda i: (0, 0))


def _norm_mod_fwd(x, y, gate, nw, scale, shift, name, tm=256):
    L, D = x.shape
    tm = _tile(L, tm)
    has_res = y is not None

    def body(*refs):
        if has_res:
            x_ref, y_ref, g_ref, nw_ref, sc_ref, sh_ref, xo_ref, h_ref = refs
            xn = x_ref[...] + g_ref[...] * y_ref[...]
            xo_ref[...] = xn
        else:
            x_ref, nw_ref, sc_ref, sh_ref, h_ref = refs
            xn = x_ref[...]
        rstd = lax.rsqrt(jnp.mean(xn * xn, axis=-1, keepdims=True) + NORM_EPS)
        h = xn * rstd * nw_ref[...] * (1.0 + sc_ref[...]) + sh_ref[...]
        h_ref[...] = h.astype(BF16)

    row, vec = _row_spec(tm, D), _vec_spec(D)
    if has_res:
        ins, in_specs = (x, y, gate, nw, scale, shift), [row, row, vec, vec, vec, vec]
        out_shape = [jax.ShapeDtypeStruct((L, D), F32), jax.ShapeDtypeStruct((L, D), BF16)]
        out_specs = [row, row]
    else:
        ins, in_specs = (x, nw, scale, shift), [row, vec, vec, vec]
        out_shape = [jax.ShapeDtypeStruct((L, D), BF16)]
        out_specs = [row]
    outs = _pcall(body, name=name, grid=(L // tm,), out_shape=out_shape, in_specs=in_specs,
                  out_specs=out_specs, compiler_params=_cparams(("parallel",)))(*ins)
    return outs if has_res else (x, outs[0])


def _norm_mod_bwd(dh, x, nw, scale, dres, name, tm=256):
    L, D = x.shape
    tm = _tile(L, tm)

    def body(dh_ref, x_ref, nw_ref, sc_ref, dres_ref, dx_ref, dsh_ref, dsc_ref, dnw_ref):
        @pl.when(pl.program_id(0) == 0)
        def _():
            dsh_ref[...] = jnp.zeros_like(dsh_ref)
            dsc_ref[...] = jnp.zeros_like(dsc_ref)
            dnw_ref[...] = jnp.zeros_like(dnw_ref)

        xv = x_ref[...]
        dh_v = dh_ref[...]
        nw_v = nw_ref[...]
        rstd = lax.rsqrt(jnp.mean(xv * xv, axis=-1, keepdims=True) + NORM_EPS)
        xhat = xv * rstd
        dsh_ref[...] += jnp.sum(dh_v, axis=0, keepdims=True)
        dsc_ref[...] += jnp.sum(dh_v * (xhat * nw_v), axis=0, keepdims=True)
        dr = dh_v * (1.0 + sc_ref[...])
        dnw_ref[...] += jnp.sum(dr * xhat, axis=0, keepdims=True)
        dxh = dr * nw_v
        dx = rstd * (dxh - xhat * jnp.mean(dxh * xhat, axis=-1, keepdims=True))
        dx_ref[...] = dx + dres_ref[...]

    row, vec = _row_spec(tm, D), _vec_spec(D)
    return _pcall(
        body, name=name, grid=(L // tm,),
        out_shape=[jax.ShapeDtypeStruct((L, D), F32)] + [jax.ShapeDtypeStruct((1, D), F32)] * 3,
        in_specs=[row, row, vec, vec, row], out_specs=[row, vec, vec, vec],
        compiler_params=_cparams(("arbitrary",)))(dh, x, nw, scale, dres)


def _gate_bwd(dx, y, gate, name, tm=256):
    L, D = dx.shape
    tm = _tile(L, tm)

    def body(dx_ref, y_ref, g_ref, dy_ref, dg_ref):
        @pl.when(pl.program_id(0) == 0)
        def _():
            dg_ref[...] = jnp.zeros_like(dg_ref)

        dxv = dx_ref[...]
        dy_ref[...] = (g_ref[...] * dxv).astype(BF16)
        dg_ref[...] += jnp.sum(dxv * y_ref[...], axis=0, keepdims=True)

    row, vec = _row_spec(tm, D), _vec_spec(D)
    return _pcall(
        body, name=name, grid=(L // tm,),
        out_shape=[jax.ShapeDtypeStruct((L, D), BF16), jax.ShapeDtypeStruct((1, D), F32)],
        in_specs=[row, row, vec], out_specs=[row, vec],
        compiler_params=_cparams(("arbitrary",)))(dx, y, gate)


def _final_loss(x, y, gate, fw, target, name, tm=256):
    L, D = x.shape
    tm = _tile(L, tm)

    def body(x_ref, y_ref, g_ref, fw_ref, t_ref, dx_ref, loss_ref, dfw_ref):
        @pl.when(pl.program_id(0) == 0)
        def _():
            loss_ref[...] = jnp.zeros_like(loss_ref)
            dfw_ref[...] = jnp.zeros_like(dfw_ref)

        xn = x_ref[...] + g_ref[...] * y_ref[...]
        fw_v = fw_ref[...]
        rstd = lax.rsqrt(jnp.mean(xn * xn, axis=-1, keepdims=True) + NORM_EPS)
        xhat = xn * rstd
        diff = xhat * fw_v - t_ref[...]
        loss_ref[...] += jnp.sum(diff * diff, axis=0, keepdims=True)
        dyf = diff * (1.0 / D)
        dfw_ref[...] += jnp.sum(dyf * xhat, axis=0, keepdims=True)
        dxh = dyf * fw_v
        dx_ref[...] = rstd * (dxh - xhat * jnp.mean(dxh * xhat, axis=-1, keepdims=True))

    row, vec = _row_spec(tm, D), _vec_spec(D)
    return _pcall(
        body, name=name, grid=(L // tm,),
        out_shape=[jax.ShapeDtypeStruct((L, D), F32), jax.ShapeDtypeStruct((1, D), F32),
                   jax.ShapeDtypeStruct((1, D), F32)],
        in_specs=[row, row, vec, vec, row], out_specs=[row, vec, vec],
        compiler_params=_cparams(("arbitrary",)))(x, y, gate, fw, target)


def _shift_down(v, s, row):
    if s == 0:
        return v
    return jnp.where(row >= s, pltpu.roll(v, s, 0), 0.0)


def _shift_up(v, s, row):
    if s == 0:
        return v
    n = v.shape[0]
    return jnp.where(row < n - s, pltpu.roll(v, n - s, 0), 0.0)


def _ssd_conv_fwd(zx, w, b, col0, width, name, cb=128):
    L = zx.shape[0]
    nb = width // cb
    off = col0 // cb

    def body(x_ref, w_ref, b_ref, o_ref):
        xv = x_ref[...]
        row = lax.broadcasted_iota(jnp.int32, xv.shape, 0)
        acc = b_ref[...] + w_ref[SSD_K - 1:SSD_K, :] * xv
        for s in range(1, SSD_K):
            acc = acc + w_ref[SSD_K - 1 - s:SSD_K - s, :] * _shift_down(xv, s, row)
        o_ref[...] = acc * _sigmoid(acc)

    return _pcall(
        body, name=name, grid=(nb,), out_shape=jax.ShapeDtypeStruct((L, width), F32),
        in_specs=[pl.BlockSpec((L, cb), lambda j: (0, off + j)),
                  pl.BlockSpec((SSD_K, cb), lambda j: (0, j)),
                  pl.BlockSpec((1, cb), lambda j: (0, j))],
        out_specs=pl.BlockSpec((L, cb), lambda j: (0, j)),
        compiler_params=_cparams(("parallel",)))(zx, w, b)


def _ssd_conv_bwd(zx, w, b, dxc, col0, name, cb=128):
    L = zx.shape[0]
    width = dxc.shape[1]
    nb = width // cb
    off = col0 // cb

    def body(x_ref, w_ref, b_ref, d_ref, dx_ref, dw_ref, db_ref):
        xv = x_ref[...]
        row = lax.broadcasted_iota(jnp.int32, xv.shape, 0)
        shifted = [_shift_down(xv, s, row) for s in range(SSD_K)]
        acc = b_ref[...] + w_ref[SSD_K - 1:SSD_K, :] * xv
        for s in range(1, SSD_K):
            acc = acc + w_ref[SSD_K - 1 - s:SSD_K - s, :] * shifted[s]
        sig = _sigmoid(acc)
        dpre = d_ref[...] * (sig * (1.0 + acc * (1.0 - sig)))
        db_ref[...] = jnp.sum(dpre, axis=0, keepdims=True)
        dx = w_ref[SSD_K - 1:SSD_K, :] * dpre
        for s in range(SSD_K):
            dw_ref[SSD_K - 1 - s:SSD_K - s, :] = jnp.sum(dpre * shifted[s], axis=0, keepdims=True)
            if s:
                dx = dx + w_ref[SSD_K - 1 - s:SSD_K - s, :] * _shift_up(dpre, s, row)
        dx_ref[...] = dx.astype(BF16)

    return _pcall(
        body, name=name, grid=(nb,),
        out_shape=[jax.ShapeDtypeStruct((L, width), BF16), jax.ShapeDtypeStruct((SSD_K, width), F32),
                   jax.ShapeDtypeStruct((1, width), F32)],
        in_specs=[pl.BlockSpec((L, cb), lambda j: (0, off + j)),
                  pl.BlockSpec((SSD_K, cb), lambda j: (0, j)),
                  pl.BlockSpec((1, cb), lambda j: (0, j)),
                  pl.BlockSpec((L, cb), lambda j: (0, j))],
        out_specs=[pl.BlockSpec((L, cb), lambda j: (0, j)),
                   pl.BlockSpec((SSD_K, cb), lambda j: (0, j)),
                   pl.BlockSpec((1, cb), lambda j: (0, j))],
        compiler_params=_cparams(("parallel",)))(zx, w, b, dxc)


def _sc_conv_fwd(proj, w, name, cb=128):
    L = proj.shape[0]
    width = proj.shape[1] // 3
    nb = width // cb

    def body(b_ref, c_ref, x_ref, w_ref, o_ref):
        q = c_ref[...] * x_ref[...]
        row = lax.broadcasted_iota(jnp.int32, q.shape, 0)
        acc = w_ref[SC_K - 1:SC_K, :] * q
        for s in range(1, SC_K):
            acc = acc + w_ref[SC_K - 1 - s:SC_K - s, :] * _shift_down(q, s, row)
        o_ref[...] = (b_ref[...] * acc).astype(BF16)

    return _pcall(
        body, name=name, grid=(nb,), out_shape=jax.ShapeDtypeStruct((L, width), BF16),
        in_specs=[pl.BlockSpec((L, cb), lambda j: (0, j)),
                  pl.BlockSpec((L, cb), lambda j: (0, nb + j)),
                  pl.BlockSpec((L, cb), lambda j: (0, 2 * nb + j)),
                  pl.BlockSpec((SC_K, cb), lambda j: (0, j))],
        out_specs=pl.BlockSpec((L, cb), lambda j: (0, j)),
        compiler_params=_cparams(("parallel",)))(proj, proj, proj, w)


def _sc_conv_bwd(proj, w, dy, name, cb=128):
    L = proj.shape[0]
    width = proj.shape[1] // 3
    nb = width // cb

    def body(b_ref, c_ref, x_ref, w_ref, dy_ref, db_ref, dc_ref, dxv_ref, dw_ref):
        cg, xv, dyv = c_ref[...], x_ref[...], dy_ref[...]
        q = cg * xv
        row = lax.broadcasted_iota(jnp.int32, q.shape, 0)
        shifted = [_shift_down(q, s, row) for s in range(SC_K)]
        conv = w_ref[SC_K - 1:SC_K, :] * q
        for s in range(1, SC_K):
            conv = conv + w_ref[SC_K - 1 - s:SC_K - s, :] * shifted[s]
        db_ref[...] = (dyv * conv).astype(BF16)
        dconv = dyv * b_ref[...]
        dq = w_ref[SC_K - 1:SC_K, :] * dconv
        for s in range(SC_K):
            dw_ref[SC_K - 1 - s:SC_K - s, :] = jnp.sum(dconv * shifted[s], axis=0, keepdims=True)
            if s:
                dq = dq + w_ref[SC_K - 1 - s:SC_K - s, :] * _shift_up(dconv, s, row)
        dc_ref[...] = (dq * xv).astype(BF16)
        dxv_ref[...] = (dq * cg).astype(BF16)

    blk = pl.BlockSpec((L, cb), lambda j: (0, j))
    wblk = pl.BlockSpec((SC_K, cb), lambda j: (0, j))
    return _pcall(
        body, name=name, grid=(nb,),
        out_shape=[jax.ShapeDtypeStruct((L, width), BF16)] * 3 + [jax.ShapeDtypeStruct((SC_K, width), F32)],
        in_specs=[blk, pl.BlockSpec((L, cb), lambda j: (0, nb + j)),
                  pl.BlockSpec((L, cb), lambda j: (0, 2 * nb + j)), wblk, blk],
        out_specs=[blk, blk, blk, wblk],
        compiler_params=_cparams(("parallel",)))(proj, proj, proj, w, dy)


def _split3(v):
    hi = v.astype(BF16)
    r1 = v - hi.astype(F32)
    mid = r1.astype(BF16)
    lo = (r1 - mid.astype(F32)).astype(BF16)
    return hi, mid, lo


def _dot_exact01(t01, v):
    hi, mid, lo = _split3(v)
    return _dot(t01, hi) + _dot(t01, mid) + _dot(t01, lo)


def _lane_col(v, lane, h):
    return jnp.sum(jnp.where(lane == h, v, 0.0), axis=1, keepdims=True)


def _sum_all(v):
    return jnp.sum(jnp.sum(v, axis=1, keepdims=True), axis=0, keepdims=True)


def _softplus(x):
    return jnp.maximum(x, 0.0) + jnp.log1p(jnp.exp(-jnp.abs(x)))


def _ssd_common(dt_ref, bias_ref, alog_ref, b_ref, c_ref, cst_ref, heads):
    c_sz = SSD_CHUNK
    lane = lax.broadcasted_iota(jnp.int32, (c_sz, LANES), 1)
    row = lax.broadcasted_iota(jnp.int32, (c_sz, LANES), 0)
    valid = lane < heads
    raw = dt_ref[...] + bias_ref[...]
    dt = _softplus(raw)
    a_row = -jnp.exp(alog_ref[...])
    a = jnp.where(valid, dt * a_row, 0.0)
    tri = (row >= lane).astype(BF16)
    cs = _dot_exact01(tri, a)
    cst_ref[...] = cs.T
    last_row = jnp.sum(a, axis=0, keepdims=True)
    bb = b_ref[...].astype(BF16)
    cb = c_ref[...].astype(BF16)
    scores = _dot(cb, bb, "nt")
    return dict(lane=lane, row=row, valid=valid, raw=raw, dt=dt, a_row=a_row, cs=cs,
                last_row=last_row, bb=bb, cb=cb, scores=scores, causal=row >= lane, lo=lane < SSD_P)


def _pair_terms(q, cst_ref, j):
    lane, lo = q["lane"], q["lo"]
    out = {}
    cols, dts, lasts, lms = [], [], [], []
    lane1 = lax.broadcasted_iota(jnp.int32, (1, LANES), 1)
    for h in (2 * j, 2 * j + 1):
        col = _lane_col(q["cs"], lane, h)
        rowv = cst_ref[h:h + 1, :]
        lms.append(jnp.exp(jnp.where(q["causal"], col - rowv, -1e30)))
        cols.append(col)
        dts.append(_lane_col(q["dt"], lane, h))
        lasts.append(jnp.sum(jnp.where(lane1 == h, q["last_row"], 0.0), axis=1, keepdims=True))
    out["lm"] = lms
    out["cols"] = cols
    out["lasts"] = lasts
    out["dt_b"] = jnp.where(lo, dts[0], dts[1])
    out["e_b"] = jnp.where(lo, jnp.exp(cols[0]), jnp.exp(cols[1]))
    out["dec_cols"] = [jnp.exp(lasts[0] - cols[0]), jnp.exp(lasts[1] - cols[1])]
    out["dec_b"] = jnp.where(lo, out["dec_cols"][0], out["dec_cols"][1])
    lo1 = lane1 < SSD_P
    out["explast"] = [jnp.exp(lasts[0]), jnp.exp(lasts[1])]
    out["explast_b"] = jnp.where(lo1, out["explast"][0], out["explast"][1])
    return out


def _ssd_fwd(zx, xc, bias_p, alog_p, d_lane, nw, d_inner, name):
    L = zx.shape[0]
    nc = L // SSD_CHUNK
    gw = d_inner // SSD_G
    heads = gw // SSD_P
    n_pair = heads // 2
    zb = gw // LANES
    bc0 = d_inner // LANES
    dt0 = (2 * d_inner + 2 * SSD_G * SSD_N) // LANES

    def body(z_ref, xs_ref, b_ref, c_ref, dt_ref, bias_ref, alog_ref, dl_ref, nw_ref,
             y_ref, yn_ref, prev_ref, s_ref, cst_ref):
        @pl.when(pl.program_id(1) == 0)
        def _():
            s_ref[...] = jnp.zeros_like(s_ref)

        q = _ssd_common(dt_ref, bias_ref, alog_ref, b_ref, c_ref, cst_ref, heads)
        prev_ref[...] = s_ref[...]
        lo = q["lo"]
        for j in range(n_pair):
            sl = slice(j * LANES, (j + 1) * LANES)
            p = _pair_terms(q, cst_ref, j)
            xs_p = xs_ref[:, sl]
            xp = xs_p * p["dt_b"]
            xb = xp.astype(BF16)
            m_a = (q["scores"] * p["lm"][0]).astype(BF16)
            m_b = (q["scores"] * p["lm"][1]).astype(BF16)
            yd = jnp.where(lo, _dot(m_a, xb), _dot(m_b, xb))
            s_p = s_ref[:, sl]
            yo = _dot(q["cb"], s_p.astype(BF16)) * p["e_b"]
            y_ref[:, sl] = yd + yo + dl_ref[:, sl] * xs_p
            st = _dot(q["bb"], (xp * p["dec_b"]).astype(BF16), "tn")
            s_ref[:, sl] = s_p * p["explast_b"] + st
        yv = y_ref[...]
        zv = z_ref[...]
        yg = yv * (zv * _sigmoid(zv))
        rstd = lax.rsqrt(jnp.mean(yg * yg, axis=-1, keepdims=True) + NORM_EPS)
        yn_ref[...] = (yg * rstd * nw_ref[...]).astype(BF16)

    grp = lambda width: pl.BlockSpec((None, 1, width), lambda g, c: (g, 0, 0))
    return _pcall(
        body, name=name, grid=(SSD_G, nc),
        out_shape=[jax.ShapeDtypeStruct((L, d_inner), F32), jax.ShapeDtypeStruct((L, d_inner), BF16),
                   jax.ShapeDtypeStruct((nc, SSD_G, SSD_N, gw), F32)],
        in_specs=[pl.BlockSpec((SSD_CHUNK, gw), lambda g, c: (c, g)),
                  pl.BlockSpec((SSD_CHUNK, gw), lambda g, c: (c, g)),
                  pl.BlockSpec((SSD_CHUNK, SSD_N), lambda g, c: (c, bc0 + g)),
                  pl.BlockSpec((SSD_CHUNK, SSD_N), lambda g, c: (c, bc0 + SSD_G + g)),
                  pl.BlockSpec((SSD_CHUNK, LANES), lambda g, c: (c, dt0 + g)),
                  grp(LANES), grp(LANES), grp(gw), grp(gw)],
        out_specs=[pl.BlockSpec((SSD_CHUNK, gw), lambda g, c: (c, g)),
                   pl.BlockSpec((SSD_CHUNK, gw), lambda g, c: (c, g)),
                   pl.BlockSpec((None, None, SSD_N, gw), lambda g, c: (c, g, 0, 0))],
        scratch_shapes=[pltpu.VMEM((SSD_N, gw), F32), pltpu.VMEM((SSD_CHUNK, LANES), F32)],
        compiler_params=_cparams(("parallel", "arbitrary")))(zx, xc, xc, xc, zx, bias_p, alog_p, d_lane, nw)


def _ssd_bwd(dyn, y, zx, xc, prev, bias_p, alog_p, d_lane, nw, d_inner, name):
    L = zx.shape[0]
    nc = L // SSD_CHUNK
    gw = d_inner // SSD_G
    heads = gw // SSD_P
    n_pair = heads // 2
    bc0 = d_inner // LANES
    dt0 = (2 * d_inner + 2 * SSD_G * SSD_N) // LANES

    def body(dyn_ref, y_ref, z_ref, xs_ref, b_ref, c_ref, dt_ref, prev_ref, bias_ref, alog_ref, dl_ref, nw_ref,
             dz_ref, dxs_ref, db_ref, dc_ref, ddt_ref, dbias_ref, dalog_ref, dd_ref, dnw_ref,
             ds_ref, cst_ref, racc_ref):
        @pl.when(pl.program_id(1) == 0)
        def _():
            ds_ref[...] = jnp.zeros_like(ds_ref)
            dbias_ref[...] = jnp.zeros_like(dbias_ref)
            dalog_ref[...] = jnp.zeros_like(dalog_ref)
            dd_ref[...] = jnp.zeros_like(dd_ref)
            dnw_ref[...] = jnp.zeros_like(dnw_ref)

        q = _ssd_common(dt_ref, bias_ref, alog_ref, b_ref, c_ref, cst_ref, heads)
        lane, row, lo = q["lane"], q["row"], q["lo"]
        lane1 = lax.broadcasted_iota(jnp.int32, (1, LANES), 1)

        yv, zv, dynv, nwv = y_ref[...], z_ref[...], dyn_ref[...], nw_ref[...]
        sig = _sigmoid(zv)
        sz = zv * sig
        yg = yv * sz
        rstd = lax.rsqrt(jnp.mean(yg * yg, axis=-1, keepdims=True) + NORM_EPS)
        yhat = yg * rstd
        dnw_ref[...] += jnp.sum(dynv * yhat, axis=0, keepdims=True)
        dyh = dynv * nwv
        dyg = rstd * (dyh - yhat * jnp.mean(dyh * yhat, axis=-1, keepdims=True))
        dz_ref[...] = (dyg * yv * (sig * (1.0 + zv * (1.0 - sig)))).astype(BF16)
        dy_all = dyg * sz

        dg = jnp.zeros((SSD_CHUNK, SSD_CHUNK), F32)
        dc_acc = jnp.zeros((SSD_CHUNK, SSD_N), F32)
        db_acc = jnp.zeros((SSD_CHUNK, SSD_N), F32)
        dcs_mat = jnp.zeros((SSD_CHUNK, LANES), F32)
        ddt_mat = jnp.zeros((SSD_CHUNK, LANES), F32)
        dd_row = jnp.zeros((1, LANES), F32)
        racc_ref[...] = jnp.zeros_like(racc_ref)
        is_last = row == SSD_CHUNK - 1

        for j in range(n_pair):
            sl = slice(j * LANES, (j + 1) * LANES)
            ha, hb = 2 * j, 2 * j + 1
            p = _pair_terms(q, cst_ref, j)
            xs_p = xs_ref[:, sl]
            dyp = dy_all[:, sl]
            xp = xs_p * p["dt_b"]
            xb = xp.astype(BF16)
            s_p = prev_ref[:, sl]
            s_pb = s_p.astype(BF16)
            dsn = ds_ref[:, sl]
            dsnb = dsn.astype(BF16)
            m_f = [q["scores"] * p["lm"][0], q["scores"] * p["lm"][1]]

            t0 = dyp * xs_p
            dd_row = dd_row + jnp.where(lane1 == ha, _sum_all(jnp.where(lo, t0, 0.0)), 0.0) \
                + jnp.where(lane1 == hb, _sum_all(jnp.where(lo, 0.0, t0)), 0.0)
            dxs_p = dl_ref[:, sl] * dyp

            yo = _dot(q["cb"], s_pb) * p["e_b"]
            dcs_b = (dyp * p["e_b"]).astype(BF16)
            dc_acc = dc_acc + _dot(dcs_b, s_pb, "nt")
            ds_yo = _dot(q["cb"], dcs_b, "tn")
            t1 = dyp * yo
            dcs_cols = [jnp.sum(jnp.where(lo, t1, 0.0), axis=1, keepdims=True),
                        jnp.sum(jnp.where(lo, 0.0, t1), axis=1, keepdims=True)]

            t2 = dsn * s_p
            dlast = [p["explast"][0] * _sum_all(jnp.where(lo, t2, 0.0)),
                     p["explast"][1] * _sum_all(jnp.where(lo, 0.0, t2))]
            ds_ref[:, sl] = dsn * p["explast_b"] + ds_yo
            w = _dot(q["bb"], dsnb)
            db_acc = db_acc + _dot((xp * p["dec_b"]).astype(BF16), dsnb, "nt")
            dxp = w * p["dec_b"]
            t3 = w * xp
            e = [jnp.sum(jnp.where(lo, t3, 0.0), axis=1, keepdims=True) * p["dec_cols"][0],
                 jnp.sum(jnp.where(lo, 0.0, t3), axis=1, keepdims=True) * p["dec_cols"][1]]
            for i in range(2):
                dlast[i] = dlast[i] + jnp.sum(e[i], axis=0, keepdims=True)
                dcs_cols[i] = dcs_cols[i] - e[i]

            dyb = dyp.astype(BF16)
            dy_h = [jnp.where(lo, dyp, 0.0).astype(BF16), jnp.where(lo, 0.0, dyp).astype(BF16)]
            dms = [_dot(dy_h[0], xb, "nt"), _dot(dy_h[1], xb, "nt")]
            dxp = dxp + jnp.where(lo, _dot(m_f[0].astype(BF16), dyb, "tn"), _dot(m_f[1].astype(BF16), dyb, "tn"))
            for i, h in enumerate((ha, hb)):
                dg = dg + dms[i] * p["lm"][i]
                qm = dms[i] * m_f[i]
                dcs_cols[i] = dcs_cols[i] + jnp.sum(qm, axis=1, keepdims=True)
                racc_ref[h:h + 1, :] = jnp.sum(qm, axis=0, keepdims=True)

            dxs_ref[:, sl] = dxs_p + dxp * p["dt_b"]
            t4 = dxp * xs_p
            ddt_cols = [jnp.sum(jnp.where(lo, t4, 0.0), axis=1, keepdims=True),
                        jnp.sum(jnp.where(lo, 0.0, t4), axis=1, keepdims=True)]
            for i, h in enumerate((ha, hb)):
                sel = lane == h
                dcs_mat = dcs_mat + jnp.where(sel, dcs_cols[i], 0.0) + jnp.where(sel & is_last, dlast[i], 0.0)
                ddt_mat = ddt_mat + jnp.where(sel, ddt_cols[i], 0.0)

        dcs_mat = dcs_mat - racc_ref[...].T
        tri_t = (row <= lane).astype(BF16)
        da = _dot_exact01(tri_t, dcs_mat)
        ddt = ddt_mat + da * q["a_row"]
        dalog_ref[...] += jnp.sum(jnp.where(q["valid"], da * q["dt"], 0.0), axis=0, keepdims=True) * q["a_row"]
        draw = jnp.where(q["valid"], ddt * _sigmoid(q["raw"]), 0.0)
        ddt_ref[...] = draw
        dbias_ref[...] += jnp.sum(draw, axis=0, keepdims=True)
        dd_ref[...] += dd_row
        dgb = dg.astype(BF16)
        dc_ref[...] = dc_acc + _dot(dgb, q["bb"])
        db_ref[...] = db_acc + _dot(dgb, q["cb"], "tn")

    rev = lambda c: nc - 1 - c
    grp = lambda width: pl.BlockSpec((None, 1, width), lambda g, c: (g, 0, 0))
    blk = lambda width, off: pl.BlockSpec((SSD_CHUNK, width), lambda g, c: (rev(c), off + g))
    return _pcall(
        body, name=name, grid=(SSD_G, nc),
        out_shape=[jax.ShapeDtypeStruct((L, d_inner), BF16), jax.ShapeDtypeStruct((L, d_inner), F32),
                   jax.ShapeDtypeStruct((L, SSD_G * SSD_N), F32), jax.ShapeDtypeStruct((L, SSD_G * SSD_N), F32),
                   jax.ShapeDtypeStruct((L, SSD_G * LANES), F32),
                   jax.ShapeDtypeStruct((SSD_G, 1, LANES), F32), jax.ShapeDtypeStruct((SSD_G, 1, LANES), F32),
                   jax.ShapeDtypeStruct((SSD_G, 1, LANES), F32), jax.ShapeDtypeStruct((SSD_G, 1, gw), F32)],
        in_specs=[blk(gw, 0), blk(gw, 0), blk(gw, 0), blk(gw, 0), blk(SSD_N, bc0), blk(SSD_N, bc0 + SSD_G),
                  blk(LANES, dt0),
                  pl.BlockSpec((None, None, SSD_N, gw), lambda g, c: (rev(c), g, 0, 0)),
                  grp(LANES), grp(LANES), grp(gw), grp(gw)],
        out_specs=[blk(gw, 0), blk(gw, 0), blk(SSD_N, 0), blk(SSD_N, 0), blk(LANES, 0),
                   grp(LANES), grp(LANES), grp(LANES), grp(gw)],
        scratch_shapes=[pltpu.VMEM((SSD_N, gw), F32), pltpu.VMEM((SSD_CHUNK, LANES), F32),
                        pltpu.VMEM((SSD_CHUNK, LANES), F32)],
        compiler_params=_cparams(("parallel", "arbitrary")))(
            dyn, y, zx, xc, xc, xc, zx, prev, bias_p, alog_p, d_lane, nw)


def _cond_mod(c_pad, ada_w, ada_b_loc, name):
    depth, D, n = ada_w.shape
    rows = c_pad.shape[0]

    def body(c_ref, w_ref, b_ref, mod_ref, cond_ref):
        cv = c_ref[...]
        cond = cv * _sigmoid(cv)
        cond_ref[...] = cond
        mod_ref[...] = _dot(cond.astype(BF16), w_ref[...].astype(BF16)) + b_ref[...]

    return _pcall(
        body, name=name, grid=(depth,),
        out_shape=[jax.ShapeDtypeStruct((depth, rows, n), F32), jax.ShapeDtypeStruct((rows, D), F32)],
        in_specs=[pl.BlockSpec((rows, D), lambda i: (0, 0)),
                  pl.BlockSpec((None, D, n), lambda i: (i, 0, 0)),
                  pl.BlockSpec((None, 1, n), lambda i: (i, 0, 0))],
        out_specs=[pl.BlockSpec((None, rows, n), lambda i: (i, 0, 0)),
                   pl.BlockSpec((rows, D), lambda i: (0, 0))],
        compiler_params=_cparams(("arbitrary",)))(c_pad, ada_w, ada_b_loc)


def _adamw_math(g, w, m, v):
    m_new = ADAM_B1 * m + (1.0 - ADAM_B1) * g
    v_new = ADAM_B2 * v + (1.0 - ADAM_B2) * (g * g)
    m_hat = m_new / (1.0 - ADAM_B1 ** ADAM_STEP)
    v_hat = v_new / (1.0 - ADAM_B2 ** ADAM_STEP)
    delta = -ADAM_LR * (m_hat / (jnp.sqrt(v_hat) + ADAM_EPS) + ADAM_WD * w)
    return delta, m_new, v_new


def _adamw_sum(parts, w, m, v, layer, name, prev=None, tr=256):
    depth, R, C = w.shape
    tr = _tile(R, tr)

    def body(p_ref, w_ref, m_ref, v_ref, *rest):
        g_ref, d_ref, mo_ref, vo_ref = rest[-4:]
        g = p_ref[0].astype(F32)
        for k in range(1, N_DEV):
            g = g + p_ref[k].astype(F32)
        d, mn, vn = _adamw_math(g, w_ref[...], m_ref[...], v_ref[...])
        g_ref[...] = g
        d_ref[...] = d
        mo_ref[...] = mn
        vo_ref[...] = vn

    blk = pl.BlockSpec((None, tr, C), lambda i: (layer, i, 0))
    prev = list(prev) if prev is not None else []
    return _pcall(
        body, name=name, grid=(R // tr,),
        out_shape=[jax.ShapeDtypeStruct((depth, R, C), F32)] * 4,
        in_specs=[pl.BlockSpec((N_DEV, tr, C), lambda i: (0, i, 0)), blk, blk, blk]
        + [pl.BlockSpec(memory_space=pl.ANY)] * len(prev),
        out_specs=[blk] * 4, input_output_aliases={4 + k: k for k in range(len(prev))},
        compiler_params=_cparams(("parallel",)))(parts, w, m, v, *prev)


def _ada_adamw(cond_pad, dmod_pad, w, m, v, name, tr=256):
    depth, D, n = w.shape
    rows = cond_pad.shape[0]
    tr = _tile(D, tr)

    def body(c_ref, dm_ref, w_ref, m_ref, v_ref, g_ref, d_ref, mo_ref, vo_ref):
        g = _dot(c_ref[...].astype(BF16), dm_ref[...].astype(BF16), "tn")
        d, mn, vn = _adamw_math(g, w_ref[...], m_ref[...], v_ref[...])
        g_ref[...] = g
        d_ref[...] = d
        mo_ref[...] = mn
        vo_ref[...] = vn

    blk = pl.BlockSpec((None, tr, n), lambda i, r: (i, r, 0))
    return _pcall(
        body, name=name, grid=(depth, D // tr),
        out_shape=[jax.ShapeDtypeStruct((depth, D, n), F32)] * 4,
        in_specs=[pl.BlockSpec((rows, tr), lambda i, r: (0, r)),
                  pl.BlockSpec((None, rows, n), lambda i, r: (i, 0, 0)), blk, blk, blk],
        out_specs=[blk] * 4, compiler_params=_cparams(("parallel", "parallel")))(cond_pad, dmod_pad, w, m, v)


def _pad_heads(v, heads_per_group):
    lead = v.shape[:-1]
    v = v.reshape(lead + (SSD_G, heads_per_group))
    v = jnp.pad(v, [(0, 0)] * len(lead) + [(0, 0), (0, LANES - heads_per_group)])
    return v.reshape(lead + (SSD_G * LANES,))


def _unpad_heads(v, heads_per_group):
    lead = v.shape[:-1]
    v = v.reshape(lead + (SSD_G, LANES))[..., :heads_per_group]
    return v.reshape(lead + (SSD_G * heads_per_group,))


def kernel(x, c, ada_w, ada_b, mix_norm_w, mlp_norm_w, mlp_up, mlp_down, ssd_in_w, ssd_conv_w, ssd_conv_b, ssd_dt_bias, ssd_A_log, ssd_D, ssd_norm_w, ssd_out_w, sc_in_w, sc_conv_w, sc_out_w, final_norm_w, loss_target, m_ada_w, m_ada_b, m_mix_norm_w, m_mlp_norm_w, m_mlp_up, m_mlp_down, m_ssd_in_w, m_ssd_conv_w, m_ssd_conv_b, m_ssd_dt_bias, m_ssd_A_log, m_ssd_D, m_ssd_norm_w, m_ssd_out_w, m_sc_in_w, m_sc_conv_w, m_sc_out_w, m_final_norm_w, v_ada_w, v_ada_b, v_mix_norm_w, v_mlp_norm_w, v_mlp_up, v_mlp_down, v_ssd_in_w, v_ssd_conv_w, v_ssd_conv_b, v_ssd_dt_bias, v_ssd_A_log, v_ssd_D, v_ssd_norm_w, v_ssd_out_w, v_sc_in_w, v_sc_conv_w, v_sc_out_w, v_final_norm_w):
    weights = dict(ada_w=ada_w, ada_b=ada_b, mix_norm_w=mix_norm_w, mlp_norm_w=mlp_norm_w, mlp_up=mlp_up,
                   mlp_down=mlp_down, ssd_in_w=ssd_in_w, ssd_conv_w=ssd_conv_w, ssd_conv_b=ssd_conv_b,
                   ssd_dt_bias=ssd_dt_bias, ssd_A_log=ssd_A_log, ssd_D=ssd_D, ssd_norm_w=ssd_norm_w,
                   ssd_out_w=ssd_out_w, sc_in_w=sc_in_w, sc_conv_w=sc_conv_w, sc_out_w=sc_out_w,
                   final_norm_w=final_norm_w)
    moms = dict(ada_w=m_ada_w, ada_b=m_ada_b, mix_norm_w=m_mix_norm_w, mlp_norm_w=m_mlp_norm_w, mlp_up=m_mlp_up,
                mlp_down=m_mlp_down, ssd_in_w=m_ssd_in_w, ssd_conv_w=m_ssd_conv_w, ssd_conv_b=m_ssd_conv_b,
                ssd_dt_bias=m_ssd_dt_bias, ssd_A_log=m_ssd_A_log, ssd_D=m_ssd_D, ssd_norm_w=m_ssd_norm_w,
                ssd_out_w=m_ssd_out_w, sc_in_w=m_sc_in_w, sc_conv_w=m_sc_conv_w, sc_out_w=m_sc_out_w,
                final_norm_w=m_final_norm_w)
    vars_ = dict(ada_w=v_ada_w, ada_b=v_ada_b, mix_norm_w=v_mix_norm_w, mlp_norm_w=v_mlp_norm_w, mlp_up=v_mlp_up,
                 mlp_down=v_mlp_down, ssd_in_w=v_ssd_in_w, ssd_conv_w=v_ssd_conv_w, ssd_conv_b=v_ssd_conv_b,
                 ssd_dt_bias=v_ssd_dt_bias, ssd_A_log=v_ssd_A_log, ssd_D=v_ssd_D, ssd_norm_w=v_ssd_norm_w,
                 ssd_out_w=v_ssd_out_w, sc_in_w=v_sc_in_w, sc_conv_w=v_sc_conv_w, sc_out_w=v_sc_out_w,
                 final_norm_w=v_final_norm_w)
    names = list(weights)

    L, D = x.shape[1], x.shape[2]
    d_inner = 2 * D
    n_heads = d_inner // SSD_P
    hpg = n_heads // SSD_G
    gw = d_inner // SSD_G
    conv_dim = d_inner + 2 * SSD_G * SSD_N
    zx_dim = d_inner + conv_dim
    me = _my_index()
    x0 = x[0]
    tgt = loss_target[0]

    n_mod = ada_w.shape[2]
    (c_all,) = _exchange([c], "gather_c", gather=True)
    c_pad = jnp.pad(c_all.reshape(N_DEV, D), ((0, 16 - N_DEV), (0, 0)))
    ada_b_loc = lax.dynamic_slice_in_dim(ada_b, me * n_mod, n_mod, axis=1).reshape(2, 1, n_mod)
    mod_blk, cond_pad = _cond_mod(c_pad, ada_w, ada_b_loc, "cond_mod")

    gather_order = ["mod", "ssd_conv_w", "sc_conv_w", "ssd_in_w", "ssd_out_w", "up0", "down0", "sc_in_w",
                    "sc_out_w", "up1", "down1"]
    gather_src = dict(mod=mod_blk, ssd_conv_w=ssd_conv_w[0], sc_conv_w=sc_conv_w[0],
                      ssd_in_w=ssd_in_w[0].astype(BF16), ssd_out_w=ssd_out_w[0].astype(BF16),
                      up0=mlp_up[0].astype(BF16), down0=mlp_down[0].astype(BF16),
                      sc_in_w=sc_in_w[0].astype(BF16), sc_out_w=sc_out_w[0].astype(BF16),
                      up1=mlp_up[1].astype(BF16), down1=mlp_down[1].astype(BF16))
    handles, gather_token = _xfer_start([gather_src[k] for k in gather_order], "gather_start", gather=True)
    gather_handle = dict(zip(gather_order, handles))

    def gathered(k, after):
        return _xfer_wait(gather_handle[k], after, f"gather_wait_{k}", gather=True)

    mod_all = gathered("mod", gather_token)
    mod_mine = lax.dynamic_index_in_dim(mod_all, me, axis=2, keepdims=False)
    mod_mine = jnp.transpose(mod_mine, (1, 0, 2)).reshape(2, 6, 1, D)
    sh_m, sc_m, g_m, sh_f, sc_f, g_f = [[mod_mine[i, k] for i in range(2)] for k in range(6)]

    vec = lambda a: a.reshape(1, -1)
    grads = {}
    small = {}

    _, h0 = _norm_mod_fwd(x0, None, None, vec(mix_norm_w[0]), sc_m[0], sh_m[0], "l0_mix_norm")
    ssd_in_g = gathered("ssd_in_w", h0)
    in_dim = ssd_in_g.shape[2] * N_DEV
    w_in_nat = jnp.transpose(ssd_in_g, (1, 0, 2)).reshape(D, in_dim)
    w_in_all = jnp.concatenate([w_in_nat[:, :zx_dim], _pad_heads(w_in_nat[:, zx_dim:], hpg)], axis=1)
    (zx,) = _mm_nn(h0, w_in_all, F32, "ssd_in_proj", tm=2048, tn=512)
    conv_b0 = vec(ssd_conv_b[0])
    conv_w_full = jnp.transpose(gathered("ssd_conv_w", h0), (1, 0, 2)).reshape(SSD_K, conv_dim)
    sc_conv_full = jnp.transpose(gathered("sc_conv_w", h0), (1, 0, 2)).reshape(SC_K, D)
    xc = _ssd_conv_fwd(zx, conv_w_full, conv_b0, d_inner, conv_dim, "ssd_conv")
    bias_p = _pad_heads(ssd_dt_bias[0], hpg).reshape(SSD_G, 1, LANES)
    alog_p = _pad_heads(ssd_A_log[0], hpg).reshape(SSD_G, 1, LANES)
    d_lane = jnp.repeat(ssd_D[0], SSD_P).reshape(SSD_G, 1, gw)
    nw_g = ssd_norm_w[0].reshape(SSD_G, 1, gw)
    y_ssd, yn, prev = _ssd_fwd(zx, xc, bias_p, alog_p, d_lane, nw_g, d_inner, "ssd_scan")
    w_ssd_out = gathered("ssd_out_w", yn).reshape(-1, D)
    (mix0,) = _mm_nn(yn, w_ssd_out, F32, "ssd_out_proj")
    x1, h1 = _norm_mod_fwd(x0, mix0, g_m[0], vec(mlp_norm_w[0]), sc_f[0], sh_f[0], "l0_mlp_norm")
    ups, downs = [None, None], [None, None]
    ups[0] = gathered("up0", h1)
    u0, s0 = _mm_nn_blocked(h1, ups[0], "l0_mlp_up", _ep_relu2, [BF16, BF16])
    downs[0] = gathered("down0", s0).reshape(-1, D)
    (d0,) = _mm_nn(s0, downs[0], F32, "l0_mlp_down")
    x2, h2 = _norm_mod_fwd(x1, d0, g_f[0], vec(mix_norm_w[1]), sc_m[1], sh_m[1], "l1_mix_norm")
    sc_in_g = gathered("sc_in_w", h2)
    (proj,) = _mm_nn_blocked(h2, sc_in_g, "sc_in_proj", _ep_store(F32), [F32])
    yc = _sc_conv_fwd(proj, sc_conv_full, "sc_conv")
    w_sc_out = gathered("sc_out_w", yc).reshape(-1, D)
    (mix1,) = _mm_nn(yc, w_sc_out, F32, "sc_out_proj")
    x3, h3 = _norm_mod_fwd(x2, mix1, g_m[1], vec(mlp_norm_w[1]), sc_f[1], sh_f[1], "l1_mlp_norm")
    ups[1] = gathered("up1", h3)
    u1, s1 = _mm_nn_blocked(h3, ups[1], "l1_mlp_up", _ep_relu2, [BF16, BF16])
    downs[1] = gathered("down1", s1).reshape(-1, D)
    (d1,) = _mm_nn(s1, downs[1], F32, "l1_mlp_down")

    dx, loss_lane, dfw = _final_loss(x3, d1, g_f[1], vec(final_norm_w), tgt, "final_loss")
    loss = lax.psum(0.5 * jnp.sum(loss_lane) / D, MESH_AXES)
    small["final_norm_w"] = dfw

    dmod = [[None] * 6 for _ in range(2)]
    big = {}

    def mlp_backward(i, dx_out, d_out, x_mid, h_in, u, s):
        dd, dg = _gate_bwd(dx_out, d_out, g_f[i], f"l{i}_mlp_gate_bwd")
        dmod[i][5] = dg
        du = _mm_nt(dd, downs[i], BF16, f"l{i}_mlp_down_bwd", epilogue=_ep_relu2_bwd, extra=(u,))
        gdown = _mm_tn(s, dd, BF16, f"l{i}_mlp_down_wgrad").reshape(N_DEV, -1, D)
        gup = _mm_tn_blocked(h_in, du, BF16, f"l{i}_mlp_up_wgrad")
        (h_down, h_up), token = _xfer_start([gdown, gup], f"l{i}_mlp_grads_start", gather=False)
        grad_handle[f"mlp_down{i}"], grad_handle[f"mlp_up{i}"] = h_down, h_up
        dh = _mm_nt_blocked(du, ups[i], F32, f"l{i}_mlp_up_bwd", after=(token,))
        dxm, dsh, dsc, dnw = _norm_mod_bwd(dh, x_mid, vec(mlp_norm_w[i]), sc_f[i], dx_out, f"l{i}_mlp_norm_bwd")
        dmod[i][3], dmod[i][4] = dsh, dsc
        return dxm, dnw

    grad_handle = {}
    dx3, dnw_mlp1 = mlp_backward(1, dx, d1, x3, h3, u1, s1)
    dyc, dg = _gate_bwd(dx3, mix1, g_m[1], "l1_mix_gate_bwd")
    dmod[1][2] = dg
    g_sc_out = _mm_tn(yc, dyc, BF16, "sc_out_wgrad").reshape(N_DEV, -1, D)
    dconv_out = _mm_nt(dyc, w_sc_out, F32, "sc_out_bwd")
    dbg, dcg, dxv, dscw = _sc_conv_bwd(proj, sc_conv_full, dconv_out, "sc_conv_bwd")
    dproj = jnp.concatenate([dbg, dcg, dxv], axis=1)
    g_sc_in = _mm_tn_blocked(h2, dproj, BF16, "sc_in_wgrad")
    (grad_handle["sc_out_w0"], grad_handle["sc_in_w0"]), token = _xfer_start(
        [g_sc_out, g_sc_in], "sc_grads_start", gather=False)
    dh2 = _mm_nt_blocked(dproj, sc_in_g, F32, "sc_in_bwd", after=(token,))
    dx2, dsh, dsc, dnw_mix1 = _norm_mod_bwd(dh2, x2, vec(mix_norm_w[1]), sc_m[1], dx3, "l1_mix_norm_bwd")
    dmod[1][0], dmod[1][1] = dsh, dsc
    dx1, dnw_mlp0 = mlp_backward(0, dx2, d0, x1, h1, u0, s0)
    dyo, dg = _gate_bwd(dx1, mix0, g_m[0], "l0_mix_gate_bwd")
    dmod[0][2] = dg
    g_ssd_out = _mm_tn(yn, dyo, BF16, "ssd_out_wgrad").reshape(N_DEV, -1, D)
    (grad_handle["ssd_out_w0"],), token = _xfer_start([g_ssd_out], "ssd_out_grad_start", gather=False)
    dyn = _mm_nt(dyo, w_ssd_out, F32, "ssd_out_bwd", after=(token,))
    dz, dxs, db_, dc_, ddt, dbias, dalog, dd_, dnw_ssd = _ssd_bwd(
        dyn, y_ssd, zx, xc, prev, bias_p, alog_p, d_lane, nw_g, d_inner, "ssd_scan_bwd")
    dxc = jnp.concatenate([dxs, db_, dc_], axis=1)
    dxbc, dcw, dcb = _ssd_conv_bwd(zx, conv_w_full, conv_b0, dxc, d_inner, "ssd_conv_bwd")
    dzx = jnp.concatenate([dz, dxbc, ddt.astype(BF16)], axis=1)
    g_in_all = _mm_tn(h0, dzx, BF16, "ssd_in_wgrad", tn=512, tk=2048)
    g_in_nat = jnp.concatenate([g_in_all[:, :zx_dim], _unpad_heads(g_in_all[:, zx_dim:], hpg)], axis=1)
    g_ssd_in = jnp.transpose(g_in_nat.reshape(D, N_DEV, in_dim // N_DEV), (1, 0, 2))
    (grad_handle["ssd_in_w0"],), token = _xfer_start([g_ssd_in], "ssd_in_grad_start", gather=False)
    dh0 = _mm_nt(dzx, w_in_all, F32, "ssd_in_bwd", tk=dzx.shape[1] // 2, after=(token,))
    grad_x, dsh, dsc, dnw_mix0 = _norm_mod_bwd(dh0, x0, vec(mix_norm_w[0]), sc_m[0], dx1, "l0_mix_norm_bwd")
    dmod[0][0], dmod[0][1] = dsh, dsc

    small["mix_norm_w"] = jnp.concatenate([dnw_mix0, dnw_mix1], axis=0)
    small["mlp_norm_w"] = jnp.concatenate([dnw_mlp0, dnw_mlp1], axis=0)
    small["ssd_conv_w"] = dcw
    small["ssd_conv_b"] = dcb
    small["ssd_dt_bias"] = _unpad_heads(dbias.reshape(SSD_G * LANES), hpg)
    small["ssd_A_log"] = _unpad_heads(dalog.reshape(SSD_G * LANES), hpg)
    small["ssd_D"] = _unpad_heads(dd_.reshape(SSD_G * LANES), hpg)
    small["ssd_norm_w"] = dnw_ssd
    small["sc_conv_w"] = dscw
    small["dmod"] = jnp.concatenate([jnp.concatenate(dmod[i], axis=1) for i in range(2)], axis=0)

    small_order = ["dmod", "mix_norm_w", "mlp_norm_w", "ssd_conv_w", "ssd_conv_b", "ssd_dt_bias", "ssd_A_log",
                   "ssd_D", "ssd_norm_w", "sc_conv_w", "final_norm_w"]
    flat = jnp.concatenate([small[k].reshape(-1) for k in small_order])
    n_small = flat.shape[0]
    n_small_pad = -(-n_small // 1024) * 1024
    flat = jnp.pad(flat, (0, n_small_pad - n_small)).reshape(n_small_pad // LANES, LANES)
    (small_handle,), small_token = _xfer_start([flat], "small_grads_start", gather=True)
    offs, o = {}, 0
    for k in small_order:
        offs[k] = (o, small[k].size, small[k].shape)
        o += small[k].size

    def small_parts(k):
        o, n, shape = offs[k]
        return small_all[:, o:o + n].reshape((N_DEV,) + shape)

    out_g, out_d, out_m, out_v = {}, {}, {}, {}

    layer_res = {}

    def big_update(name, i, after):
        parts = _xfer_wait(grad_handle[f"{name}{i}"], after, f"grads_wait_{name}_{i}", gather=False)
        res = _adamw_sum(parts, weights[name], moms[name], vars_[name], i, f"adamw_{name}_{i}",
                         prev=layer_res.get(name))
        layer_res[name] = res
        return res[1]

    chain = small_token
    for name, i in [("mlp_down", 1), ("mlp_up", 1), ("sc_out_w", 0), ("sc_in_w", 0), ("mlp_down", 0),
                    ("mlp_up", 0), ("ssd_out_w", 0)]:
        chain = big_update(name, i, chain)
    small_all = _xfer_wait(small_handle, chain, "small_grads_wait", gather=True)
    small_all = small_all.reshape(N_DEV, n_small_pad)

    dmod_all = small_parts("dmod")
    dmod_loc = lax.dynamic_slice_in_dim(dmod_all, me * n_mod, n_mod, axis=2)
    dmod_pad = jnp.pad(jnp.transpose(dmod_loc, (1, 0, 2)), ((0, 0), (0, 16 - N_DEV), (0, 0)))
    out_g["ada_w"], out_d["ada_w"], out_m["ada_w"], out_v["ada_w"] = _ada_adamw(
        cond_pad, dmod_pad, ada_w, m_ada_w, v_ada_w, "adamw_ada_w")

    pieces = []
    pieces.append(("ada_b", dmod_all.reshape(N_DEV, -1)))
    for k in ["mix_norm_w", "mlp_norm_w", "ssd_conv_b", "ssd_dt_bias", "ssd_A_log", "ssd_D", "ssd_norm_w",
              "final_norm_w"]:
        pieces.append((k, small_parts(k).reshape(N_DEV, -1)))
    n_cw = ssd_conv_w.shape[2]
    pieces.append(("ssd_conv_w", lax.dynamic_slice_in_dim(small_parts("ssd_conv_w"), me * n_cw, n_cw, axis=2)
                   .reshape(N_DEV, -1)))
    n_scw = sc_conv_w.shape[2]
    pieces.append(("sc_conv_w", lax.dynamic_slice_in_dim(small_parts("sc_conv_w"), me * n_scw, n_scw, axis=2)
                   .reshape(N_DEV, -1)))
    n_tot = sum(p.shape[1] for _, p in pieces)
    n_tot_pad = -(-n_tot // 1024) * 1024

    def pack(arrs, lead=()):
        f = jnp.concatenate(arrs, axis=-1)
        f = jnp.pad(f, [(0, 0)] * len(lead) + [(0, n_tot_pad - n_tot)])
        return f.reshape(lead + (n_tot_pad // LANES, LANES))

    parts_flat = pack([p for _, p in pieces], lead=(N_DEV,))
    w_flat = pack([weights[k].reshape(-1) for k, _ in pieces])
    m_flat = pack([moms[k].reshape(-1) for k, _ in pieces])
    v_flat = pack([vars_[k].reshape(-1) for k, _ in pieces])
    res = _adamw_sum(parts_flat, w_flat[None], m_flat[None], v_flat[None], 0, "adamw_small",
                     tr=n_tot_pad // LANES)
    o = 0
    for k, p in pieces:
        n = p.shape[1]
        for r, dst in zip(res, (out_g, out_d, out_m, out_v)):
            dst[k] = r.reshape(-1)[o:o + n].reshape(weights[k].shape)
        o += n
    big_update("ssd_in_w", 0, res[1])
    for name, res4 in layer_res.items():
        for r, dst in zip(res4, (out_g, out_d, out_m, out_v)):
            dst[name] = r

    return (loss, grad_x[None], *[out_g[k] for k in names], *[out_d[k] for k in names],
            *[out_m[k] for k in names], *[out_v[k] for k in names])
```

```python
import functools

import jax
import jax.numpy as jnp
from jax import lax
from jax.experimental import pallas as pl
from jax.experimental.pallas import tpu as pltpu

F32 = jnp.float32
BF16 = jnp.bfloat16
N_DEV = 8
MESH_AXES = ("x", "y", "c")
MESH = pl.DeviceIdType.MESH

NORM_EPS = 1e-5
SSD_G = 4
SSD_P = 64
SSD_N = 128
SSD_CHUNK = 128
SSD_K = 4
SC_K = 3
LANES = 128

ADAM_LR = 0.001
ADAM_B1 = 0.9
ADAM_B2 = 0.999
ADAM_EPS = 1e-08
ADAM_WD = 0.01
ADAM_STEP = 10

VMEM_LIMIT = 56 * 1024 * 1024


def _pcall(body, **kw):
    return pl.pallas_call(body, **kw)


def _cparams(sem=None):
    if sem is None:
        return pltpu.CompilerParams(vmem_limit_bytes=VMEM_LIMIT)
    return pltpu.CompilerParams(dimension_semantics=sem, vmem_limit_bytes=VMEM_LIMIT)


def _my_index():
    return 4 * lax.axis_index("x") + 2 * lax.axis_index("y") + lax.axis_index("c")


_PEER_MASKS = [(0, 0, 1), (0, 1, 0), (0, 1, 1), (1, 0, 0), (1, 0, 1), (1, 1, 0), (1, 1, 1)]


def _peers():
    x, y, c = lax.axis_index("x"), lax.axis_index("y"), lax.axis_index("c")
    out = []
    for mx, my, mc in _PEER_MASKS:
        px = (1 - x) if mx else x
        py = (1 - y) if my else y
        pc = (1 - c) if mc else c
        out.append(((px, py, pc), 4 * px + 2 * py + pc))
    return out


def _exchange(arrs, name, gather):
    n = len(arrs)
    n_peer = N_DEV - 1

    def body(*refs):
        ins, outs = refs[:n], refs[n:2 * n]
        send_sems, recv_sems, local_sems = refs[2 * n:]
        me = _my_index()
        peers = _peers()
        started = []
        for a in range(n):
            src_own = ins[a] if gather else ins[a].at[me]
            own = pltpu.make_async_copy(src_own, outs[a].at[me], local_sems.at[a])
            own.start()
            started.append(own)
        sends = []
        for a in range(n):
            for k, (peer, pidx) in enumerate(peers):
                src = ins[a] if gather else ins[a].at[pidx]
                cp = pltpu.make_async_remote_copy(
                    src_ref=src, dst_ref=outs[a].at[me],
                    send_sem=send_sems.at[a * n_peer + k], recv_sem=recv_sems.at[a * n_peer + k],
                    device_id=peer, device_id_type=MESH)
                cp.start()
                sends.append(cp)
        for a in range(n):
            for k, (peer, pidx) in enumerate(peers):
                src = ins[a] if gather else ins[a].at[pidx]
                pltpu.make_async_remote_copy(
                    src_ref=src, dst_ref=outs[a].at[pidx],
                    send_sem=send_sems.at[a * n_peer + k], recv_sem=recv_sems.at[a * n_peer + k],
                    device_id=peer, device_id_type=MESH).wait_recv()
        for cp in sends:
            cp.wait_send()
        for own in started:
            own.wait()

    if gather:
        out_shape = [jax.ShapeDtypeStruct((N_DEV,) + a.shape, a.dtype) for a in arrs]
    else:
        out_shape = [jax.ShapeDtypeStruct(a.shape, a.dtype) for a in arrs]
    any_spec = pl.BlockSpec(memory_space=pl.ANY)
    outs = _pcall(
        body, name=name, out_shape=out_shape,
        in_specs=[any_spec] * n, out_specs=[any_spec] * n,
        scratch_shapes=[pltpu.SemaphoreType.DMA((n * n_peer,)), pltpu.SemaphoreType.DMA((n * n_peer,)),
                        pltpu.SemaphoreType.DMA((n,))],
        compiler_params=pltpu.CompilerParams(has_side_effects=True),
    )(*arrs)
    return list(outs)


_HBM = pl.BlockSpec(memory_space=pltpu.HBM)
_SEM = pl.BlockSpec(memory_space=pltpu.SEMAPHORE)
_DATAFLOW = pltpu.SideEffectType.DATAFLOW_SIDE_EFFECTING


def _xfer_start(arrs, name, gather):
    n = len(arrs)
    n_peer = N_DEV - 1

    def body(*refs):
        ins, lands = refs[:n], refs[n:2 * n]
        sems = refs[2 * n:5 * n]
        token = refs[-1]
        me = _my_index()
        peers = _peers()
        for a in range(n):
            send_sems, recv_sems, loc_sem = sems[3 * a:3 * a + 3]
            src_own = ins[a] if gather else ins[a].at[me]
            pltpu.make_async_copy(src_own, lands[a].at[me], loc_sem).start()
            for k, (peer, pidx) in enumerate(peers):
                src = ins[a] if gather else ins[a].at[pidx]
                pltpu.make_async_remote_copy(
                    src_ref=src, dst_ref=lands[a].at[me], send_sem=send_sems.at[k], recv_sem=recv_sems.at[k],
                    device_id=peer, device_id_type=MESH).start()
        token[...] = jnp.zeros_like(token)

    land_shapes = [((N_DEV,) + a.shape) if gather else a.shape for a in arrs]
    out_shape, out_specs = [], []
    for _ in range(n):
        out_shape += [pltpu.SemaphoreType.DMA((n_peer,)), pltpu.SemaphoreType.DMA((n_peer,)),
                      pltpu.SemaphoreType.DMA(())]
        out_specs += [_SEM, _SEM, _SEM]
    out_shape += [pltpu.HBM(a.shape, a.dtype) for a in arrs]
    out_shape += [pltpu.HBM(s, a.dtype) for s, a in zip(land_shapes, arrs)]
    out_shape += [jax.ShapeDtypeStruct((8, LANES), F32)]
    out_specs += [_HBM] * (2 * n) + [pl.BlockSpec(memory_space=pltpu.VMEM)]
    aliases = {}
    for a in range(n):
        aliases[a] = 3 * n + a
        aliases[n + a] = 4 * n + a
    operands = [pltpu.with_memory_space_constraint(a, pltpu.HBM) for a in arrs]
    operands += [pltpu.with_memory_space_constraint(lax.empty(s, a.dtype), pltpu.HBM)
                 for s, a in zip(land_shapes, arrs)]
    outs = _pcall(
        body, name=name, out_shape=tuple(out_shape), in_specs=[_HBM] * (2 * n), out_specs=tuple(out_specs),
        input_output_aliases=aliases,
        compiler_params=pltpu.CompilerParams(has_side_effects=_DATAFLOW),
    )(*operands)
    handles = []
    for a in range(n):
        handles.append((outs[3 * n + a], outs[4 * n + a], outs[3 * a], outs[3 * a + 1], outs[3 * a + 2]))
    return handles, outs[-1]


def _xfer_wait(handle, after, name, gather):
    src_thru, land_thru, send_sems, recv_sems, loc_sem = handle

    def body(src_ref, land_ref, send_ref, recv_ref, loc_ref, after_ref, src_dead, got_ref):
        me = _my_index()
        src_own = src_ref if gather else src_ref.at[me]
        pltpu.make_async_copy(src_own, land_ref.at[me], loc_ref).wait()
        for k, (peer, pidx) in enumerate(_peers()):
            src = src_ref if gather else src_ref.at[pidx]
            cp = pltpu.make_async_remote_copy(
                src_ref=src, dst_ref=land_ref.at[pidx], send_sem=send_ref.at[k], recv_sem=recv_ref.at[k],
                device_id=peer, device_id_type=MESH)
            cp.wait_send()
            cp.wait_recv()

    return _pcall(
        body, name=name,
        out_shape=(pltpu.HBM(src_thru.shape, src_thru.dtype), pltpu.HBM(land_thru.shape, land_thru.dtype)),
        in_specs=[_HBM, _HBM, _SEM, _SEM, _SEM, pl.BlockSpec(memory_space=pl.ANY)], out_specs=(_HBM, _HBM),
        input_output_aliases={0: 0, 1: 1},
        compiler_params=pltpu.CompilerParams(has_side_effects=_DATAFLOW),
    )(src_thru, land_thru, send_sems, recv_sems, loc_sem, after)[1]


_DIMS = {"nn": (((1,), (0,)), ((), ())), "nt": (((1,), (1,)), ((), ())), "tn": (((0,), (0,)), ((), ()))}


def _dot(a, b, mode="nn"):
    return lax.dot_general(a, b, _DIMS[mode], preferred_element_type=F32)


def _mm(a, b, *, mode, grid, a_spec, b_spec, out_shape, out_specs, acc_shape, epilogue, name,
        extra=(), extra_specs=(), after=()):
    nk = grid[2]
    n_extra = len(extra)
    n_in = 2 + n_extra + len(after)

    def body_single(*refs):
        a_ref, b_ref = refs[0], refs[1]
        epilogue(_dot(a_ref[...], b_ref[...], mode), refs[2:2 + n_extra], refs[n_in:])

    def body_acc(*refs):
        a_ref, b_ref = refs[0], refs[1]
        ex = refs[2:2 + n_extra]
        outs = refs[n_in:-1]
        acc = refs[-1]
        k = pl.program_id(2)

        @pl.when(k == 0)
        def _():
            acc[...] = jnp.zeros_like(acc)

        acc[...] += _dot(a_ref[...], b_ref[...], mode)

        @pl.when(k == nk - 1)
        def _():
            epilogue(acc[...], ex, outs)

    return _pcall(
        body_single if nk == 1 else body_acc, name=name, grid=grid, out_shape=out_shape,
        in_specs=[a_spec, b_spec] + list(extra_specs) + [pl.BlockSpec(memory_space=pl.ANY)] * len(after),
        out_specs=out_specs,
        scratch_shapes=[] if nk == 1 else [pltpu.VMEM(acc_shape, F32)],
        compiler_params=_cparams(("parallel", "parallel", "arbitrary")),
    )(a, b, *extra, *after)


def _ep_store(dtype):
    def ep(acc, ex, outs):
        outs[0][...] = acc.astype(dtype)
    return ep


def _ep_relu2(acc, ex, outs):
    outs[0][...] = acc.astype(BF16)
    r = jnp.maximum(acc, 0.0)
    outs[1][...] = (r * r).astype(BF16)


def _ep_relu2_bwd(acc, ex, outs):
    u = ex[0][...].astype(F32)
    outs[0][...] = (acc * (2.0 * jnp.maximum(u, 0.0))).astype(BF16)


def _tile(n, want):
    t = min(n, want)
    while n % t:
        t //= 2
    return t


def _mm_nn(a, w, out_dtype, name, tm=1024, tn=1024, tk=1024, epilogue=None, out_dtypes=None):
    M, K = a.shape
    N = w.shape[1]
    tm, tn, tk = _tile(M, tm), _tile(N, tn), _tile(K, tk)
    out_dtypes = out_dtypes or [out_dtype]
    return _mm(a, w, mode="nn", grid=(M // tm, N // tn, K // tk),
               a_spec=pl.BlockSpec((tm, tk), lambda i, j, k: (i, k)),
               b_spec=pl.BlockSpec((tk, tn), lambda i, j, k: (k, j)),
               out_shape=[jax.ShapeDtypeStruct((M, N), d) for d in out_dtypes],
               out_specs=[pl.BlockSpec((tm, tn), lambda i, j, k: (i, j)) for _ in out_dtypes],
               acc_shape=(tm, tn), epilogue=epilogue or _ep_store(out_dtype), name=name)


def _mm_nn_blocked(a, wg, name, epilogue, out_dtypes, tm=2048):
    M, K = a.shape
    n = wg.shape[2]
    tm = _tile(M, tm)
    return _mm(a, wg, mode="nn", grid=(M // tm, N_DEV, 1),
               a_spec=pl.BlockSpec((tm, K), lambda i, j, k: (i, 0)),
               b_spec=pl.BlockSpec((None, K, n), lambda i, j, k: (j, 0, 0)),
               out_shape=[jax.ShapeDtypeStruct((M, N_DEV * n), d) for d in out_dtypes],
               out_specs=[pl.BlockSpec((tm, n), lambda i, j, k: (i, j)) for _ in out_dtypes],
               acc_shape=(tm, n), epilogue=epilogue, name=name)


def _mm_nt(a, w, out_dtype, name, tm=1024, tn=1024, tk=1024, epilogue=None, extra=(), extra_specs=(),
           after=()):
    M, K = a.shape
    N = w.shape[0]
    tm, tn, tk = _tile(M, tm), _tile(N, tn), _tile(K, tk)
    if extra and not extra_specs:
        extra_specs = [pl.BlockSpec((tm, tn), lambda i, j, k: (i, j)) for _ in extra]
    return _mm(a, w, mode="nt", grid=(M // tm, N // tn, K // tk),
               a_spec=pl.BlockSpec((tm, tk), lambda i, j, k: (i, k)),
               b_spec=pl.BlockSpec((tn, tk), lambda i, j, k: (j, k)),
               out_shape=[jax.ShapeDtypeStruct((M, N), out_dtype)],
               out_specs=[pl.BlockSpec((tm, tn), lambda i, j, k: (i, j))],
               acc_shape=(tm, tn), epilogue=epilogue or _ep_store(out_dtype), name=name,
               extra=extra, extra_specs=extra_specs, after=after)[0]


def _mm_nt_blocked(a, wg, out_dtype, name, tm=1024, after=()):
    M = a.shape[0]
    kout, n = wg.shape[1], wg.shape[2]
    tm = _tile(M, tm)
    return _mm(a, wg, mode="nt", grid=(M // tm, 1, N_DEV),
               a_spec=pl.BlockSpec((tm, n), lambda i, j, k: (i, k)),
               b_spec=pl.BlockSpec((None, kout, n), lambda i, j, k: (k, 0, 0)),
               out_shape=[jax.ShapeDtypeStruct((M, kout), out_dtype)],
               out_specs=[pl.BlockSpec((tm, kout), lambda i, j, k: (i, 0))],
               acc_shape=(tm, kout), epilogue=_ep_store(out_dtype), name=name, after=after)[0]


def _mm_tn(a, b, out_dtype, name, tm=1024, tn=1024, tk=1024):
    K, M = a.shape
    N = b.shape[1]
    tm, tn, tk = _tile(M, tm), _tile(N, tn), _tile(K, tk)
    return _mm(a, b, mode="tn", grid=(M // tm, N // tn, K // tk),
               a_spec=pl.BlockSpec((tk, tm), lambda i, j, k: (k, i)),
               b_spec=pl.BlockSpec((tk, tn), lambda i, j, k: (k, j)),
               out_shape=[jax.ShapeDtypeStruct((M, N), out_dtype)],
               out_specs=[pl.BlockSpec((tm, tn), lambda i, j, k: (i, j))],
               acc_shape=(tm, tn), epilogue=_ep_store(out_dtype), name=name)[0]


def _mm_tn_blocked(a, b, out_dtype, name, tm=1024, tk=2048):
    K, M = a.shape
    n = b.shape[1] // N_DEV
    tm, tk = _tile(M, tm), _tile(K, tk)
    return _mm(a, b, mode="tn", grid=(M // tm, N_DEV, K // tk),
               a_spec=pl.BlockSpec((tk, tm), lambda i, j, k: (k, i)),
               b_spec=pl.BlockSpec((tk, n), lambda i, j, k: (k, j)),
               out_shape=[jax.ShapeDtypeStruct((N_DEV, M, n), out_dtype)],
               out_specs=[pl.BlockSpec((None, tm, n), lambda i, j, k: (j, i, 0))],
               acc_shape=(tm, n), epilogue=_ep_store(out_dtype), name=name)[0]


def _window_geometry(ws):
    base = [(ws * k // LANES) * LANES for k in range(N_DEV)]
    off = [ws * k - base[k] for k in range(N_DEV)]
    win = -(-(max(off) + ws) // LANES) * LANES
    return base, off, win


def _to_window(shard, off, win, me):
    pad = jnp.zeros(shard.shape[:-1] + (win,), shard.dtype)
    start = (0,) * (shard.ndim - 1) + (jnp.asarray(off, jnp.int32)[me],)
    return lax.dynamic_update_slice(pad, shard, start)


def _from_window(window, off, ws, me):
    start = (0,) * (window.ndim - 1) + (jnp.asarray(off, jnp.int32)[me],)
    return lax.dynamic_slice(window, start, window.shape[:-1] + (ws,))


def _windows_to_columns(xg, base, win, n_out, name, tr=256):
    R = xg.shape[1]
    tr = _tile(R, tr)
    nb_win = win // LANES

    def body(x_ref, o_ref):
        for b in range(n_out // LANES):
            acc = None
            for k in range(N_DEV):
                i = b - base[k] // LANES
                if 0 <= i < nb_win:
                    blk = x_ref[k, :, i * LANES:(i + 1) * LANES].astype(F32)
                    acc = blk if acc is None else acc + blk
            if acc is None:
                acc = jnp.zeros((tr, LANES), F32)
            o_ref[:, b * LANES:(b + 1) * LANES] = acc.astype(o_ref.dtype)

    return _pcall(
        body, name=name, grid=(R // tr,), out_shape=jax.ShapeDtypeStruct((R, n_out), xg.dtype),
        in_specs=[pl.BlockSpec((N_DEV, tr, win), lambda i: (0, i, 0))],
        out_specs=pl.BlockSpec((tr, n_out), lambda i: (i, 0)),
        compiler_params=_cparams(("parallel",)))(xg)


def _sigmoid(x):
    return 1.0 / (1.0 + jnp.exp(-x))


def _row_spec(tm, d):
    return pl.BlockSpec((tm, d), lambda i: (i, 0))


def _vec_spec(d):
    return pl.BlockSpec((1, d), lambda i: (0, 0))


def _norm_mod_fwd(x, y, gate, nw, scale, shift, name, tm=256):
    L, D = x.shape
    tm = _tile(L, tm)
    has_res = y is not None

    def body(*refs):
        if has_res:
            x_ref, y_ref, g_ref, nw_ref, sc_ref, sh_ref, xo_ref, h_ref = refs
            xn = x_ref[...] + g_ref[...] * y_ref[...]
            xo_ref[...] = xn
        else:
            x_ref, nw_ref, sc_ref, sh_ref, h_ref = refs
            xn = x_ref[...]
        rstd = lax.rsqrt(jnp.mean(xn * xn, axis=-1, keepdims=True) + NORM_EPS)
        h = xn * rstd * nw_ref[...] * (1.0 + sc_ref[...]) + sh_ref[...]
        h_ref[...] = h.astype(BF16)

    row, vec = _row_spec(tm, D), _vec_spec(D)
    if has_res:
        ins, in_specs = (x, y, gate, nw, scale, shift), [row, row, vec, vec, vec, vec]
        out_shape = [jax.ShapeDtypeStruct((L, D), F32), jax.ShapeDtypeStruct((L, D), BF16)]
        out_specs = [row, row]
    else:
        ins, in_specs = (x, nw, scale, shift), [row, vec, vec, vec]
        out_shape = [jax.ShapeDtypeStruct((L, D), BF16)]
        out_specs = [row]
    outs = _pcall(body, name=name, grid=(L // tm,), out_shape=out_shape, in_specs=in_specs,
                  out_specs=out_specs, compiler_params=_cparams(("parallel",)))(*ins)
    return outs if has_res else (x, outs[0])


def _norm_mod_bwd(dh, x, nw, scale, dres, name, tm=256):
    L, D = x.shape
    tm = _tile(L, tm)

    def body(dh_ref, x_ref, nw_ref, sc_ref, dres_ref, dx_ref, dsh_ref, dsc_ref, dnw_ref):
        @pl.when(pl.program_id(0) == 0)
        def _():
            dsh_ref[...] = jnp.zeros_like(dsh_ref)
            dsc_ref[...] = jnp.zeros_like(dsc_ref)
            dnw_ref[...] = jnp.zeros_like(dnw_ref)

        xv = x_ref[...]
        dh_v = dh_ref[...]
        nw_v = nw_ref[...]
        rstd = lax.rsqrt(jnp.mean(xv * xv, axis=-1, keepdims=True) + NORM_EPS)
        xhat = xv * rstd
        dsh_ref[...] += jnp.sum(dh_v, axis=0, keepdims=True)
        dsc_ref[...] += jnp.sum(dh_v * (xhat * nw_v), axis=0, keepdims=True)
        dr = dh_v * (1.0 + sc_ref[...])
        dnw_ref[...] += jnp.sum(dr * xhat, axis=0, keepdims=True)
        dxh = dr * nw_v
        dx = rstd * (dxh - xhat * jnp.mean(dxh * xhat, axis=-1, keepdims=True))
        dx_ref[...] = dx + dres_ref[...]

    row, vec = _row_spec(tm, D), _vec_spec(D)
    return _pcall(
        body, name=name, grid=(L // tm,),
        out_shape=[jax.ShapeDtypeStruct((L, D), F32)] + [jax.ShapeDtypeStruct((1, D), F32)] * 3,
        in_specs=[row, row, vec, vec, row], out_specs=[row, vec, vec, vec],
        compiler_params=_cparams(("arbitrary",)))(dh, x, nw, scale, dres)


def _gate_bwd(dx, y, gate, name, tm=256):
    L, D = dx.shape
    tm = _tile(L, tm)

    def body(dx_ref, y_ref, g_ref, dy_ref, dg_ref):
        @pl.when(pl.program_id(0) == 0)
        def _():
            dg_ref[...] = jnp.zeros_like(dg_ref)

        dxv = dx_ref[...]
        dy_ref[...] = (g_ref[...] * dxv).astype(BF16)
        dg_ref[...] += jnp.sum(dxv * y_ref[...], axis=0, keepdims=True)

    row, vec = _row_spec(tm, D), _vec_spec(D)
    return _pcall(
        body, name=name, grid=(L // tm,),
        out_shape=[jax.ShapeDtypeStruct((L, D), BF16), jax.ShapeDtypeStruct((1, D), F32)],
        in_specs=[row, row, vec], out_specs=[row, vec],
        compiler_params=_cparams(("arbitrary",)))(dx, y, gate)


def _final_loss(x, y, gate, fw, target, name, tm=256):
    L, D = x.shape
    tm = _tile(L, tm)

    def body(x_ref, y_ref, g_ref, fw_ref, t_ref, dx_ref, loss_ref, dfw_ref):
        @pl.when(pl.program_id(0) == 0)
        def _():
            loss_ref[...] = jnp.zeros_like(loss_ref)
            dfw_ref[...] = jnp.zeros_like(dfw_ref)

        xn = x_ref[...] + g_ref[...] * y_ref[...]
        fw_v = fw_ref[...]
        rstd = lax.rsqrt(jnp.mean(xn * xn, axis=-1, keepdims=True) + NORM_EPS)
        xhat = xn * rstd
        diff = xhat * fw_v - t_ref[...]
        loss_ref[...] += jnp.sum(diff * diff, axis=0, keepdims=True)
        dyf = diff * (1.0 / D)
        dfw_ref[...] += jnp.sum(dyf * xhat, axis=0, keepdims=True)
        dxh = dyf * fw_v
        dx_ref[...] = rstd * (dxh - xhat * jnp.mean(dxh * xhat, axis=-1, keepdims=True))

    row, vec = _row_spec(tm, D), _vec_spec(D)
    return _pcall(
        body, name=name, grid=(L // tm,),
        out_shape=[jax.ShapeDtypeStruct((L, D), F32), jax.ShapeDtypeStruct((1, D), F32),
                   jax.ShapeDtypeStruct((1, D), F32)],
        in_specs=[row, row, vec, vec, row], out_specs=[row, vec, vec],
        compiler_params=_cparams(("arbitrary",)))(x, y, gate, fw, target)


def _shift_down(v, s, row):
    if s == 0:
        return v
    return jnp.where(row >= s, pltpu.roll(v, s, 0), 0.0)


def _shift_up(v, s, row):
    if s == 0:
        return v
    n = v.shape[0]
    return jnp.where(row < n - s, pltpu.roll(v, n - s, 0), 0.0)


def _ssd_conv_fwd(zx, w, b, col0, width, name, cb=128):
    L = zx.shape[0]
    nb = width // cb
    off = col0 // cb

    def body(x_ref, w_ref, b_ref, o_ref):
        xv = x_ref[...]
        row = lax.broadcasted_iota(jnp.int32, xv.shape, 0)
        acc = b_ref[...] + w_ref[SSD_K - 1:SSD_K, :] * xv
        for s in range(1, SSD_K):
            acc = acc + w_ref[SSD_K - 1 - s:SSD_K - s, :] * _shift_down(xv, s, row)
        o_ref[...] = acc * _sigmoid(acc)

    return _pcall(
        body, name=name, grid=(nb,), out_shape=jax.ShapeDtypeStruct((L, width), F32),
        in_specs=[pl.BlockSpec((L, cb), lambda j: (0, off + j)),
                  pl.BlockSpec((SSD_K, cb), lambda j: (0, j)),
                  pl.BlockSpec((1, cb), lambda j: (0, j))],
        out_specs=pl.BlockSpec((L, cb), lambda j: (0, j)),
        compiler_params=_cparams(("parallel",)))(zx, w, b)


def _ssd_conv_bwd(zx, w, b, dxc, col0, name, cb=128):
    L = zx.shape[0]
    width = dxc.shape[1]
    nb = width // cb
    off = col0 // cb

    def body(x_ref, w_ref, b_ref, d_ref, dx_ref, dw_ref, db_ref):
        xv = x_ref[...]
        row = lax.broadcasted_iota(jnp.int32, xv.shape, 0)
        shifted = [_shift_down(xv, s, row) for s in range(SSD_K)]
        acc = b_ref[...] + w_ref[SSD_K - 1:SSD_K, :] * xv
        for s in range(1, SSD_K):
            acc = acc + w_ref[SSD_K - 1 - s:SSD_K - s, :] * shifted[s]
        sig = _sigmoid(acc)
        dpre = d_ref[...] * (sig * (1.0 + acc * (1.0 - sig)))
        db_ref[...] = jnp.sum(dpre, axis=0, keepdims=True)
        dx = w_ref[SSD_K - 1:SSD_K, :] * dpre
        for s in range(SSD_K):
            dw_ref[SSD_K - 1 - s:SSD_K - s, :] = jnp.sum(dpre * shifted[s], axis=0, keepdims=True)
            if s:
                dx = dx + w_ref[SSD_K - 1 - s:SSD_K - s, :] * _shift_up(dpre, s, row)
        dx_ref[...] = dx.astype(BF16)

    return _pcall(
        body, name=name, grid=(nb,),
        out_shape=[jax.ShapeDtypeStruct((L, width), BF16), jax.ShapeDtypeStruct((SSD_K, width), F32),
                   jax.ShapeDtypeStruct((1, width), F32)],
        in_specs=[pl.BlockSpec((L, cb), lambda j: (0, off + j)),
                  pl.BlockSpec((SSD_K, cb), lambda j: (0, j)),
                  pl.BlockSpec((1, cb), lambda j: (0, j)),
                  pl.BlockSpec((L, cb), lambda j: (0, j))],
        out_specs=[pl.BlockSpec((L, cb), lambda j: (0, j)),
                   pl.BlockSpec((SSD_K, cb), lambda j: (0, j)),
                   pl.BlockSpec((1, cb), lambda j: (0, j))],
        compiler_params=_cparams(("parallel",)))(zx, w, b, dxc)


def _sc_conv_fwd(proj, w, name, cb=128):
    L = proj.shape[0]
    width = proj.shape[1] // 3
    nb = width // cb

    def body(b_ref, c_ref, x_ref, w_ref, o_ref):
        q = c_ref[...] * x_ref[...]
        row = lax.broadcasted_iota(jnp.int32, q.shape, 0)
        acc = w_ref[SC_K - 1:SC_K, :] * q
        for s in range(1, SC_K):
            acc = acc + w_ref[SC_K - 1 - s:SC_K - s, :] * _shift_down(q, s, row)
        o_ref[...] = (b_ref[...] * acc).astype(BF16)

    return _pcall(
        body, name=name, grid=(nb,), out_shape=jax.ShapeDtypeStruct((L, width), BF16),
        in_specs=[pl.BlockSpec((L, cb), lambda j: (0, j)),
                  pl.BlockSpec((L, cb), lambda j: (0, nb + j)),
                  pl.BlockSpec((L, cb), lambda j: (0, 2 * nb + j)),
                  pl.BlockSpec((SC_K, cb), lambda j: (0, j))],
        out_specs=pl.BlockSpec((L, cb), lambda j: (0, j)),
        compiler_params=_cparams(("parallel",)))(proj, proj, proj, w)


def _sc_conv_bwd(proj, w, dy, name, cb=128):
    L = proj.shape[0]
    width = proj.shape[1] // 3
    nb = width // cb

    def body(b_ref, c_ref, x_ref, w_ref, dy_ref, db_ref, dc_ref, dxv_ref, dw_ref):
        cg, xv, dyv = c_ref[...], x_ref[...], dy_ref[...]
        q = cg * xv
        row = lax.broadcasted_iota(jnp.int32, q.shape, 0)
        shifted = [_shift_down(q, s, row) for s in range(SC_K)]
        conv = w_ref[SC_K - 1:SC_K, :] * q
        for s in range(1, SC_K):
            conv = conv + w_ref[SC_K - 1 - s:SC_K - s, :] * shifted[s]
        db_ref[...] = (dyv * conv).astype(BF16)
        dconv = dyv * b_ref[...]
        dq = w_ref[SC_K - 1:SC_K, :] * dconv
        for s in range(SC_K):
            dw_ref[SC_K - 1 - s:SC_K - s, :] = jnp.sum(dconv * shifted[s], axis=0, keepdims=True)
            if s:
                dq = dq + w_ref[SC_K - 1 - s:SC_K - s, :] * _shift_up(dconv, s, row)
        dc_ref[...] = (dq * xv).astype(BF16)
        dxv_ref[...] = (dq * cg).astype(BF16)

    blk = pl.BlockSpec((L, cb), lambda j: (0, j))
    wblk = pl.BlockSpec((SC_K, cb), lambda j: (0, j))
    return _pcall(
        body, name=name, grid=(nb,),
        out_shape=[jax.ShapeDtypeStruct((L, width), BF16)] * 3 + [jax.ShapeDtypeStruct((SC_K, width), F32)],
        in_specs=[blk, pl.BlockSpec((L, cb), lambda j: (0, nb + j)),
                  pl.BlockSpec((L, cb), lambda j: (0, 2 * nb + j)), wblk, blk],
        out_specs=[blk, blk, blk, wblk],
        compiler_params=_cparams(("parallel",)))(proj, proj, proj, w, dy)


def _split3(v):
    hi = v.astype(BF16)
    r1 = v - hi.astype(F32)
    mid = r1.astype(BF16)
    lo = (r1 - mid.astype(F32)).astype(BF16)
    return hi, mid, lo


def _dot_exact01(t01, v):
    hi, mid, lo = _split3(v)
    return _dot(t01, hi) + _dot(t01, mid) + _dot(t01, lo)


def _lane_col(v, lane, h):
    return jnp.sum(jnp.where(lane == h, v, 0.0), axis=1, keepdims=True)


def _sum_all(v):
    return jnp.sum(jnp.sum(v, axis=1, keepdims=True), axis=0, keepdims=True)


def _softplus(x):
    return jnp.maximum(x, 0.0) + jnp.log1p(jnp.exp(-jnp.abs(x)))


def _ssd_common(dt_ref, bias_ref, alog_ref, b_ref, c_ref, cst_ref, heads):
    c_sz = SSD_CHUNK
    lane = lax.broadcasted_iota(jnp.int32, (c_sz, LANES), 1)
    row = lax.broadcasted_iota(jnp.int32, (c_sz, LANES), 0)
    valid = lane < heads
    raw = dt_ref[...] + bias_ref[...]
    dt = _softplus(raw)
    a_row = -jnp.exp(alog_ref[...])
    a = jnp.where(valid, dt * a_row, 0.0)
    tri = (row >= lane).astype(BF16)
    cs = _dot_exact01(tri, a)
    cst_ref[...] = cs.T
    last_row = jnp.sum(a, axis=0, keepdims=True)
    bb = b_ref[...].astype(BF16)
    cb = c_ref[...].astype(BF16)
    scores = _dot(cb, bb, "nt")
    return dict(lane=lane, row=row, valid=valid, raw=raw, dt=dt, a_row=a_row, cs=cs,
                last_row=last_row, bb=bb, cb=cb, scores=scores, causal=row >= lane, lo=lane < SSD_P)


def _pair_terms(q, cst_ref, h0):
    lane, lo = q["lane"], q["lo"]
    out = {}
    cols, dts, lasts, lms = [], [], [], []
    lane1 = lax.broadcasted_iota(jnp.int32, (1, LANES), 1)
    for h in (h0, h0 + 1):
        col = _lane_col(q["cs"], lane, h)
        rowv = cst_ref[pl.ds(h, 1), :]
        lms.append(jnp.exp(jnp.where(q["causal"], col - rowv, -1e30)))
        cols.append(col)
        dts.append(_lane_col(q["dt"], lane, h))
        lasts.append(jnp.sum(jnp.where(lane1 == h, q["last_row"], 0.0), axis=1, keepdims=True))
    out["lm"] = lms
    out["cols"] = cols
    out["lasts"] = lasts
    out["dt_b"] = jnp.where(lo, dts[0], dts[1])
    out["e_b"] = jnp.where(lo, jnp.exp(cols[0]), jnp.exp(cols[1]))
    out["dec_cols"] = [jnp.exp(lasts[0] - cols[0]), jnp.exp(lasts[1] - cols[1])]
    out["dec_b"] = jnp.where(lo, out["dec_cols"][0], out["dec_cols"][1])
    lo1 = lane1 < SSD_P
    out["explast"] = [jnp.exp(lasts[0]), jnp.exp(lasts[1])]
    out["explast_b"] = jnp.where(lo1, out["explast"][0], out["explast"][1])
    return out


def _ssd_fwd(zx, xc, bias_p, alog_p, d_lane, nw, d_inner, name):
    L = zx.shape[0]
    nc = L // SSD_CHUNK
    gw = d_inner // SSD_G
    heads = gw // SSD_P
    n_pair = heads // 2
    bc0 = d_inner // LANES
    dt0 = (2 * d_inner + 2 * SSD_G * SSD_N) // LANES

    def body(z_ref, xs_ref, b_ref, c_ref, dt_ref, bias_ref, alog_ref, dl_ref, nw_ref,
             y_ref, yn_ref, prev_ref, s_ref, cst_ref):
        @pl.when(pl.program_id(1) == 0)
        def _():
            s_ref[...] = jnp.zeros_like(s_ref)

        q = _ssd_common(dt_ref, bias_ref, alog_ref, b_ref, c_ref, cst_ref, SSD_G * heads)
        prev_ref[...] = s_ref[...]
        lo = q["lo"]
        for j in range(n_pair):
            sl = slice(j * LANES, (j + 1) * LANES)
            p = _pair_terms(q, cst_ref, pl.program_id(0) * heads + 2 * j)
            xs_p = xs_ref[:, sl]
            xp = xs_p * p["dt_b"]
            xb = xp.astype(BF16)
            m_a = (q["scores"] * p["lm"][0]).astype(BF16)
            m_b = (q["scores"] * p["lm"][1]).astype(BF16)
            yd = jnp.where(lo, _dot(m_a, xb), _dot(m_b, xb))
            s_p = s_ref[:, sl]
            yo = _dot(q["cb"], s_p.astype(BF16)) * p["e_b"]
            y_ref[:, sl] = yd + yo + dl_ref[:, sl] * xs_p
            st = _dot(q["bb"], (xp * p["dec_b"]).astype(BF16), "tn")
            s_ref[:, sl] = s_p * p["explast_b"] + st
        yv = y_ref[...]
        zv = z_ref[...]
        yg = yv * (zv * _sigmoid(zv))
        rstd = lax.rsqrt(jnp.mean(yg * yg, axis=-1, keepdims=True) + NORM_EPS)
        yn_ref[...] = (yg * rstd * nw_ref[...]).astype(BF16)

    grp = lambda width: pl.BlockSpec((None, 1, width), lambda g, c: (g, 0, 0))
    head_vec = pl.BlockSpec((1, LANES), lambda g, c: (0, 0))
    return _pcall(
        body, name=name, grid=(SSD_G, nc),
        out_shape=[jax.ShapeDtypeStruct((L, d_inner), F32), jax.ShapeDtypeStruct((L, d_inner), BF16),
                   jax.ShapeDtypeStruct((nc, SSD_G, SSD_N, gw), F32)],
        in_specs=[pl.BlockSpec((SSD_CHUNK, gw), lambda g, c: (c, g)),
                  pl.BlockSpec((SSD_CHUNK, gw), lambda g, c: (c, g)),
                  pl.BlockSpec((SSD_CHUNK, SSD_N), lambda g, c: (c, bc0 + g)),
                  pl.BlockSpec((SSD_CHUNK, SSD_N), lambda g, c: (c, bc0 + SSD_G + g)),
                  pl.BlockSpec((SSD_CHUNK, LANES), lambda g, c: (c, dt0)),
                  head_vec, head_vec, grp(gw), grp(gw)],
        out_specs=[pl.BlockSpec((SSD_CHUNK, gw), lambda g, c: (c, g)),
                   pl.BlockSpec((SSD_CHUNK, gw), lambda g, c: (c, g)),
                   pl.BlockSpec((None, None, SSD_N, gw), lambda g, c: (c, g, 0, 0))],
        scratch_shapes=[pltpu.VMEM((SSD_N, gw), F32), pltpu.VMEM((SSD_CHUNK, LANES), F32)],
        compiler_params=_cparams(("parallel", "arbitrary")))(zx, xc, xc, xc, zx, bias_p, alog_p, d_lane, nw)


def _ssd_bwd(dyn, y, zx, xc, prev, bias_p, alog_p, d_lane, nw, d_inner, name):
    L = zx.shape[0]
    nc = L // SSD_CHUNK
    gw = d_inner // SSD_G
    heads = gw // SSD_P
    n_pair = heads // 2
    bc0 = d_inner // LANES
    dt0 = (2 * d_inner + 2 * SSD_G * SSD_N) // LANES

    def body(dyn_ref, y_ref, z_ref, xs_ref, b_ref, c_ref, dt_ref, prev_ref, bias_ref, alog_ref, dl_ref, nw_ref,
             dz_ref, dxs_ref, db_ref, dc_ref, ddt_ref, dbias_ref, dalog_ref, dd_ref, dnw_ref,
             ds_ref, cst_ref, racc_ref):
        @pl.when(pl.program_id(1) == 0)
        def _():
            ds_ref[...] = jnp.zeros_like(ds_ref)
            dbias_ref[...] = jnp.zeros_like(dbias_ref)
            dalog_ref[...] = jnp.zeros_like(dalog_ref)
            dd_ref[...] = jnp.zeros_like(dd_ref)
            dnw_ref[...] = jnp.zeros_like(dnw_ref)

        q = _ssd_common(dt_ref, bias_ref, alog_ref, b_ref, c_ref, cst_ref, SSD_G * heads)
        lane, row, lo = q["lane"], q["row"], q["lo"]
        lane1 = lax.broadcasted_iota(jnp.int32, (1, LANES), 1)
        head0 = pl.program_id(0) * heads
        mine = (lane >= head0) & (lane < head0 + heads)

        yv, zv, dynv, nwv = y_ref[...], z_ref[...], dyn_ref[...], nw_ref[...]
        sig = _sigmoid(zv)
        sz = zv * sig
        yg = yv * sz
        rstd = lax.rsqrt(jnp.mean(yg * yg, axis=-1, keepdims=True) + NORM_EPS)
        yhat = yg * rstd
        dnw_ref[...] += jnp.sum(dynv * yhat, axis=0, keepdims=True)
        dyh = dynv * nwv
        dyg = rstd * (dyh - yhat * jnp.mean(dyh * yhat, axis=-1, keepdims=True))
        dz_ref[...] = (dyg * yv * (sig * (1.0 + zv * (1.0 - sig)))).astype(BF16)
        dy_all = dyg * sz

        dg = jnp.zeros((SSD_CHUNK, SSD_CHUNK), F32)
        dc_acc = jnp.zeros((SSD_CHUNK, SSD_N), F32)
        db_acc = jnp.zeros((SSD_CHUNK, SSD_N), F32)
        dcs_mat = jnp.zeros((SSD_CHUNK, LANES), F32)
        ddt_mat = jnp.zeros((SSD_CHUNK, LANES), F32)
        dd_row = jnp.zeros((1, LANES), F32)
        racc_ref[...] = jnp.zeros_like(racc_ref)
        is_last = row == SSD_CHUNK - 1

        for j in range(n_pair):
            sl = slice(j * LANES, (j + 1) * LANES)
            ha, hb = head0 + 2 * j, head0 + 2 * j + 1
            p = _pair_terms(q, cst_ref, ha)
            xs_p = xs_ref[:, sl]
            dyp = dy_all[:, sl]
            xp = xs_p * p["dt_b"]
            xb = xp.astype(BF16)
            s_p = prev_ref[:, sl]
            s_pb = s_p.astype(BF16)
            dsn = ds_ref[:, sl]
            dsnb = dsn.astype(BF16)
            m_f = [q["scores"] * p["lm"][0], q["scores"] * p["lm"][1]]

            t0 = dyp * xs_p
            dd_row = dd_row + jnp.where(lane1 == ha, _sum_all(jnp.where(lo, t0, 0.0)), 0.0) \
                + jnp.where(lane1 == hb, _sum_all(jnp.where(lo, 0.0, t0)), 0.0)
            dxs_p = dl_ref[:, sl] * dyp

            yo = _dot(q["cb"], s_pb) * p["e_b"]
            dcs_b = (dyp * p["e_b"]).astype(BF16)
            dc_acc = dc_acc + _dot(dcs_b, s_pb, "nt")
            ds_yo = _dot(q["cb"], dcs_b, "tn")
            t1 = dyp * yo
            dcs_cols = [jnp.sum(jnp.where(lo, t1, 0.0), axis=1, keepdims=True),
                        jnp.sum(jnp.where(lo, 0.0, t1), axis=1, keepdims=True)]

            t2 = dsn * s_p
            dlast = [p["explast"][0] * _sum_all(jnp.where(lo, t2, 0.0)),
                     p["explast"][1] * _sum_all(jnp.where(lo, 0.0, t2))]
            ds_ref[:, sl] = dsn * p["explast_b"] + ds_yo
            w = _dot(q["bb"], dsnb)
            db_acc = db_acc + _dot((xp * p["dec_b"]).astype(BF16), dsnb, "nt")
            dxp = w * p["dec_b"]
            t3 = w * xp
            e = [jnp.sum(jnp.where(lo, t3, 0.0), axis=1, keepdims=True) * p["dec_cols"][0],
                 jnp.sum(jnp.where(lo, 0.0, t3), axis=1, keepdims=True) * p["dec_cols"][1]]
            for i in range(2):
                dlast[i] = dlast[i] + jnp.sum(e[i], axis=0, keepdims=True)
                dcs_cols[i] = dcs_cols[i] - e[i]

            dyb = dyp.astype(BF16)
            dy_h = [jnp.where(lo, dyp, 0.0).astype(BF16), jnp.where(lo, 0.0, dyp).astype(BF16)]
            dms = [_dot(dy_h[0], xb, "nt"), _dot(dy_h[1], xb, "nt")]
            dxp = dxp + jnp.where(lo, _dot(m_f[0].astype(BF16), dyb, "tn"), _dot(m_f[1].astype(BF16), dyb, "tn"))
            for i, h in enumerate((ha, hb)):
                dg = dg + dms[i] * p["lm"][i]
                qm = dms[i] * m_f[i]
                dcs_cols[i] = dcs_cols[i] + jnp.sum(qm, axis=1, keepdims=True)
                racc_ref[pl.ds(h, 1), :] = jnp.sum(qm, axis=0, keepdims=True)

            dxs_ref[:, sl] = dxs_p + dxp * p["dt_b"]
            t4 = dxp * xs_p
            ddt_cols = [jnp.sum(jnp.where(lo, t4, 0.0), axis=1, keepdims=True),
                        jnp.sum(jnp.where(lo, 0.0, t4), axis=1, keepdims=True)]
            for i, h in enumerate((ha, hb)):
                sel = lane == h
                dcs_mat = dcs_mat + jnp.where(sel, dcs_cols[i], 0.0) + jnp.where(sel & is_last, dlast[i], 0.0)
                ddt_mat = ddt_mat + jnp.where(sel, ddt_cols[i], 0.0)

        dcs_mat = dcs_mat - racc_ref[...].T
        tri_t = (row <= lane).astype(BF16)
        da = _dot_exact01(tri_t, dcs_mat)
        ddt = ddt_mat + da * q["a_row"]
        dalog_ref[...] += jnp.sum(jnp.where(mine, da * q["dt"], 0.0), axis=0, keepdims=True) * q["a_row"]
        draw = jnp.where(mine, ddt * _sigmoid(q["raw"]), 0.0)
        ddt_ref[...] = draw
        dbias_ref[...] += jnp.sum(draw, axis=0, keepdims=True)
        dd_ref[...] += dd_row
        dgb = dg.astype(BF16)
        dc_ref[...] = dc_acc + _dot(dgb, q["bb"])
        db_ref[...] = db_acc + _dot(dgb, q["cb"], "tn")

    rev = lambda c: nc - 1 - c
    grp = lambda width: pl.BlockSpec((None, 1, width), lambda g, c: (g, 0, 0))
    blk = lambda width, off: pl.BlockSpec((SSD_CHUNK, width), lambda g, c: (rev(c), off + g))
    head_vec = pl.BlockSpec((1, LANES), lambda g, c: (0, 0))
    return _pcall(
        body, name=name, grid=(SSD_G, nc),
        out_shape=[jax.ShapeDtypeStruct((L, d_inner), BF16), jax.ShapeDtypeStruct((L, d_inner), F32),
                   jax.ShapeDtypeStruct((L, SSD_G * SSD_N), F32), jax.ShapeDtypeStruct((L, SSD_G * SSD_N), F32),
                   jax.ShapeDtypeStruct((SSD_G, L, LANES), F32),
                   jax.ShapeDtypeStruct((SSD_G, 1, LANES), F32), jax.ShapeDtypeStruct((SSD_G, 1, LANES), F32),
                   jax.ShapeDtypeStruct((SSD_G, 1, LANES), F32), jax.ShapeDtypeStruct((SSD_G, 1, gw), F32)],
        in_specs=[blk(gw, 0), blk(gw, 0), blk(gw, 0), blk(gw, 0), blk(SSD_N, bc0), blk(SSD_N, bc0 + SSD_G),
                  pl.BlockSpec((SSD_CHUNK, LANES), lambda g, c: (rev(c), dt0)),
                  pl.BlockSpec((None, None, SSD_N, gw), lambda g, c: (rev(c), g, 0, 0)),
                  head_vec, head_vec, grp(gw), grp(gw)],
        out_specs=[blk(gw, 0), blk(gw, 0), blk(SSD_N, 0), blk(SSD_N, 0),
                   pl.BlockSpec((None, SSD_CHUNK, LANES), lambda g, c: (g, rev(c), 0)),
                   grp(LANES), grp(LANES), grp(LANES), grp(gw)],
        scratch_shapes=[pltpu.VMEM((SSD_N, gw), F32), pltpu.VMEM((SSD_CHUNK, LANES), F32),
                        pltpu.VMEM((SSD_CHUNK, LANES), F32)],
        compiler_params=_cparams(("parallel", "arbitrary")))(
            dyn, y, zx, xc, xc, xc, zx, prev, bias_p, alog_p, d_lane, nw)


def _cond_mod(c_pad, ada_w, ada_b_loc, name):
    depth, D, n = ada_w.shape
    rows = c_pad.shape[0]

    def body(c_ref, w_ref, b_ref, mod_ref, cond_ref):
        cv = c_ref[...]
        cond = cv * _sigmoid(cv)
        cond_ref[...] = cond
        mod_ref[...] = _dot(cond.astype(BF16), w_ref[...].astype(BF16)) + b_ref[...]

    return _pcall(
        body, name=name, grid=(depth,),
        out_shape=[jax.ShapeDtypeStruct((depth, rows, n), F32), jax.ShapeDtypeStruct((rows, D), F32)],
        in_specs=[pl.BlockSpec((rows, D), lambda i: (0, 0)),
                  pl.BlockSpec((None, D, n), lambda i: (i, 0, 0)),
                  pl.BlockSpec((None, 1, n), lambda i: (i, 0, 0))],
        out_specs=[pl.BlockSpec((None, rows, n), lambda i: (i, 0, 0)),
                   pl.BlockSpec((rows, D), lambda i: (0, 0))],
        compiler_params=_cparams(("arbitrary",)))(c_pad, ada_w, ada_b_loc)


def _adamw_math(g, w, m, v):
    m_new = ADAM_B1 * m + (1.0 - ADAM_B1) * g
    v_new = ADAM_B2 * v + (1.0 - ADAM_B2) * (g * g)
    m_hat = m_new / (1.0 - ADAM_B1 ** ADAM_STEP)
    v_hat = v_new / (1.0 - ADAM_B2 ** ADAM_STEP)
    delta = -ADAM_LR * (m_hat / (jnp.sqrt(v_hat) + ADAM_EPS) + ADAM_WD * w)
    return delta, m_new, v_new


def _adamw_sum(parts, w, m, v, layer, name, prev=None, tr=256):
    depth, R, C = w.shape
    tr = _tile(R, tr)

    def body(p_ref, w_ref, m_ref, v_ref, *rest):
        g_ref, d_ref, mo_ref, vo_ref = rest[-4:]
        g = p_ref[0].astype(F32)
        for k in range(1, N_DEV):
            g = g + p_ref[k].astype(F32)
        d, mn, vn = _adamw_math(g, w_ref[...], m_ref[...], v_ref[...])
        g_ref[...] = g
        d_ref[...] = d
        mo_ref[...] = mn
        vo_ref[...] = vn

    blk = pl.BlockSpec((None, tr, C), lambda i: (layer, i, 0))
    prev = list(prev) if prev is not None else []
    return _pcall(
        body, name=name, grid=(R // tr,),
        out_shape=[jax.ShapeDtypeStruct((depth, R, C), F32)] * 4,
        in_specs=[pl.BlockSpec((N_DEV, tr, C), lambda i: (0, i, 0)), blk, blk, blk]
        + [pl.BlockSpec(memory_space=pl.ANY)] * len(prev),
        out_specs=[blk] * 4, input_output_aliases={4 + k: k for k in range(len(prev))},
        compiler_params=_cparams(("parallel",)))(parts, w, m, v, *prev)


def _ada_adamw(cond_pad, dmod_pad, w, m, v, name, tr=256):
    depth, D, n = w.shape
    rows = cond_pad.shape[0]
    tr = _tile(D, tr)

    def body(c_ref, dm_ref, w_ref, m_ref, v_ref, g_ref, d_ref, mo_ref, vo_ref):
        g = _dot(c_ref[...].astype(BF16), dm_ref[...].astype(BF16), "tn")
        d, mn, vn = _adamw_math(g, w_ref[...], m_ref[...], v_ref[...])
        g_ref[...] = g
        d_ref[...] = d
        mo_ref[...] = mn
        vo_ref[...] = vn

    blk = pl.BlockSpec((None, tr, n), lambda i, r: (i, r, 0))
    return _pcall(
        body, name=name, grid=(depth, D // tr),
        out_shape=[jax.ShapeDtypeStruct((depth, D, n), F32)] * 4,
        in_specs=[pl.BlockSpec((rows, tr), lambda i, r: (0, r)),
                  pl.BlockSpec((None, rows, n), lambda i, r: (i, 0, 0)), blk, blk, blk],
        out_specs=[blk] * 4, compiler_params=_cparams(("parallel", "parallel")))(cond_pad, dmod_pad, w, m, v)


def kernel(x, c, ada_w, ada_b, mix_norm_w, mlp_norm_w, mlp_up, mlp_down, ssd_in_w, ssd_conv_w, ssd_conv_b, ssd_dt_bias, ssd_A_log, ssd_D, ssd_norm_w, ssd_out_w, sc_in_w, sc_conv_w, sc_out_w, final_norm_w, loss_target, m_ada_w, m_ada_b, m_mix_norm_w, m_mlp_norm_w, m_mlp_up, m_mlp_down, m_ssd_in_w, m_ssd_conv_w, m_ssd_conv_b, m_ssd_dt_bias, m_ssd_A_log, m_ssd_D, m_ssd_norm_w, m_ssd_out_w, m_sc_in_w, m_sc_conv_w, m_sc_out_w, m_final_norm_w, v_ada_w, v_ada_b, v_mix_norm_w, v_mlp_norm_w, v_mlp_up, v_mlp_down, v_ssd_in_w, v_ssd_conv_w, v_ssd_conv_b, v_ssd_dt_bias, v_ssd_A_log, v_ssd_D, v_ssd_norm_w, v_ssd_out_w, v_sc_in_w, v_sc_conv_w, v_sc_out_w, v_final_norm_w):
    weights = dict(ada_w=ada_w, ada_b=ada_b, mix_norm_w=mix_norm_w, mlp_norm_w=mlp_norm_w, mlp_up=mlp_up,
                   mlp_down=mlp_down, ssd_in_w=ssd_in_w, ssd_conv_w=ssd_conv_w, ssd_conv_b=ssd_conv_b,
                   ssd_dt_bias=ssd_dt_bias, ssd_A_log=ssd_A_log, ssd_D=ssd_D, ssd_norm_w=ssd_norm_w,
                   ssd_out_w=ssd_out_w, sc_in_w=sc_in_w, sc_conv_w=sc_conv_w, sc_out_w=sc_out_w,
                   final_norm_w=final_norm_w)
    moms = dict(ada_w=m_ada_w, ada_b=m_ada_b, mix_norm_w=m_mix_norm_w, mlp_norm_w=m_mlp_norm_w, mlp_up=m_mlp_up,
                mlp_down=m_mlp_down, ssd_in_w=m_ssd_in_w, ssd_conv_w=m_ssd_conv_w, ssd_conv_b=m_ssd_conv_b,
                ssd_dt_bias=m_ssd_dt_bias, ssd_A_log=m_ssd_A_log, ssd_D=m_ssd_D, ssd_norm_w=m_ssd_norm_w,
                ssd_out_w=m_ssd_out_w, sc_in_w=m_sc_in_w, sc_conv_w=m_sc_conv_w, sc_out_w=m_sc_out_w,
                final_norm_w=m_final_norm_w)
    vars_ = dict(ada_w=v_ada_w, ada_b=v_ada_b, mix_norm_w=v_mix_norm_w, mlp_norm_w=v_mlp_norm_w, mlp_up=v_mlp_up,
                 mlp_down=v_mlp_down, ssd_in_w=v_ssd_in_w, ssd_conv_w=v_ssd_conv_w, ssd_conv_b=v_ssd_conv_b,
                 ssd_dt_bias=v_ssd_dt_bias, ssd_A_log=v_ssd_A_log, ssd_D=v_ssd_D, ssd_norm_w=v_ssd_norm_w,
                 ssd_out_w=v_ssd_out_w, sc_in_w=v_sc_in_w, sc_conv_w=v_sc_conv_w, sc_out_w=v_sc_out_w,
                 final_norm_w=v_final_norm_w)
    names = list(weights)

    L, D = x.shape[1], x.shape[2]
    d_inner = 2 * D
    n_heads = d_inner // SSD_P
    hpg = n_heads // SSD_G
    gw = d_inner // SSD_G
    conv_dim = d_inner + 2 * SSD_G * SSD_N
    zx_dim = d_inner + conv_dim
    zx_pad = -(-(zx_dim + LANES) // 512) * 512
    in_ws = ssd_in_w.shape[2]
    in_base, in_off, in_win = _window_geometry(in_ws)
    me = _my_index()
    x0 = x[0]
    tgt = loss_target[0]

    n_mod = ada_w.shape[2]
    (c_all,) = _exchange([c], "gather_c", gather=True)
    c_pad = jnp.pad(c_all.reshape(N_DEV, D), ((0, 16 - N_DEV), (0, 0)))
    ada_b_loc = lax.dynamic_slice_in_dim(ada_b, me * n_mod, n_mod, axis=1).reshape(2, 1, n_mod)
    mod_blk, cond_pad = _cond_mod(c_pad, ada_w, ada_b_loc, "cond_mod")

    gather_order = ["mod", "ssd_conv_w", "sc_conv_w", "ssd_in_w", "ssd_out_w", "up0", "down0", "sc_in_w",
                    "sc_out_w", "up1", "down1"]
    gather_src = dict(mod=mod_blk, ssd_conv_w=ssd_conv_w[0], sc_conv_w=sc_conv_w[0],
                      ssd_in_w=_to_window(ssd_in_w[0].astype(BF16), in_off, in_win, me),
                      ssd_out_w=ssd_out_w[0].astype(BF16),
                      up0=mlp_up[0].astype(BF16), down0=mlp_down[0].astype(BF16),
                      sc_in_w=sc_in_w[0].astype(BF16), sc_out_w=sc_out_w[0].astype(BF16),
                      up1=mlp_up[1].astype(BF16), down1=mlp_down[1].astype(BF16))
    handles, gather_token = _xfer_start([gather_src[k] for k in gather_order], "gather_start", gather=True)
    gather_handle = dict(zip(gather_order, handles))

    def gathered(k, after):
        return _xfer_wait(gather_handle[k], after, f"gather_wait_{k}", gather=True)

    mod_all = gathered("mod", gather_token)
    mod_mine = lax.dynamic_index_in_dim(mod_all, me, axis=2, keepdims=False)
    mod_mine = jnp.transpose(mod_mine, (1, 0, 2)).reshape(2, 6, 1, D)
    sh_m, sc_m, g_m, sh_f, sc_f, g_f = [[mod_mine[i, k] for i in range(2)] for k in range(6)]

    vec = lambda a: a.reshape(1, -1)
    grads = {}
    small = {}

    _, h0 = _norm_mod_fwd(x0, None, None, vec(mix_norm_w[0]), sc_m[0], sh_m[0], "l0_mix_norm")
    w_in_all = _windows_to_columns(gathered("ssd_in_w", h0), in_base, in_win, zx_pad, "ssd_in_w_columns")
    (zx,) = _mm_nn(h0, w_in_all, F32, "ssd_in_proj", tm=2048, tn=512)
    conv_b0 = vec(ssd_conv_b[0])
    conv_w_full = jnp.transpose(gathered("ssd_conv_w", h0), (1, 0, 2)).reshape(SSD_K, conv_dim)
    sc_conv_full = jnp.transpose(gathered("sc_conv_w", h0), (1, 0, 2)).reshape(SC_K, D)
    xc = _ssd_conv_fwd(zx, conv_w_full, conv_b0, d_inner, conv_dim, "ssd_conv")
    bias_p = jnp.pad(ssd_dt_bias[0], (0, LANES - n_heads)).reshape(1, LANES)
    alog_p = jnp.pad(ssd_A_log[0], (0, LANES - n_heads)).reshape(1, LANES)
    d_lane = jnp.repeat(ssd_D[0], SSD_P).reshape(SSD_G, 1, gw)
    nw_g = ssd_norm_w[0].reshape(SSD_G, 1, gw)
    y_ssd, yn, prev = _ssd_fwd(zx, xc, bias_p, alog_p, d_lane, nw_g, d_inner, "ssd_scan")
    w_ssd_out = gathered("ssd_out_w", yn).reshape(-1, D)
    (mix0,) = _mm_nn(yn, w_ssd_out, F32, "ssd_out_proj")
    x1, h1 = _norm_mod_fwd(x0, mix0, g_m[0], vec(mlp_norm_w[0]), sc_f[0], sh_f[0], "l0_mlp_norm")
    ups, downs = [None, None], [None, None]
    ups[0] = gathered("up0", h1)
    u0, s0 = _mm_nn_blocked(h1, ups[0], "l0_mlp_up", _ep_relu2, [BF16, BF16])
    downs[0] = gathered("down0", s0).reshape(-1, D)
    (d0,) = _mm_nn(s0, downs[0], F32, "l0_mlp_down")
    x2, h2 = _norm_mod_fwd(x1, d0, g_f[0], vec(mix_norm_w[1]), sc_m[1], sh_m[1], "l1_mix_norm")
    sc_in_g = gathered("sc_in_w", h2)
    (proj,) = _mm_nn_blocked(h2, sc_in_g, "sc_in_proj", _ep_store(F32), [F32])
    yc = _sc_conv_fwd(proj, sc_conv_full, "sc_conv")
    w_sc_out = gathered("sc_out_w", yc).reshape(-1, D)
    (mix1,) = _mm_nn(yc, w_sc_out, F32, "sc_out_proj")
    x3, h3 = _norm_mod_fwd(x2, mix1, g_m[1], vec(mlp_norm_w[1]), sc_f[1], sh_f[1], "l1_mlp_norm")
    ups[1] = gathered("up1", h3)
    u1, s1 = _mm_nn_blocked(h3, ups[1], "l1_mlp_up", _ep_relu2, [BF16, BF16])
    downs[1] = gathered("down1", s1).reshape(-1, D)
    (d1,) = _mm_nn(s1, downs[1], F32, "l1_mlp_down")

    dx, loss_lane, dfw = _final_loss(x3, d1, g_f[1], vec(final_norm_w), tgt, "final_loss")
    loss = lax.psum(0.5 * jnp.sum(loss_lane) / D, MESH_AXES)
    small["final_norm_w"] = dfw

    dmod = [[None] * 6 for _ in range(2)]
    big = {}

    def mlp_backward(i, dx_out, d_out, x_mid, h_in, u, s):
        dd, dg = _gate_bwd(dx_out, d_out, g_f[i], f"l{i}_mlp_gate_bwd")
        dmod[i][5] = dg
        du = _mm_nt(dd, downs[i], BF16, f"l{i}_mlp_down_bwd", epilogue=_ep_relu2_bwd, extra=(u,))
        gdown = _mm_tn(s, dd, BF16, f"l{i}_mlp_down_wgrad").reshape(N_DEV, -1, D)
        gup = _mm_tn_blocked(h_in, du, BF16, f"l{i}_mlp_up_wgrad")
        (h_down, h_up), token = _xfer_start([gdown, gup], f"l{i}_mlp_grads_start", gather=False)
        grad_handle[f"mlp_down{i}"], grad_handle[f"mlp_up{i}"] = h_down, h_up
        dh = _mm_nt_blocked(du, ups[i], F32, f"l{i}_mlp_up_bwd", after=(token,))
        dxm, dsh, dsc, dnw = _norm_mod_bwd(dh, x_mid, vec(mlp_norm_w[i]), sc_f[i], dx_out, f"l{i}_mlp_norm_bwd")
        dmod[i][3], dmod[i][4] = dsh, dsc
        return dxm, dnw

    grad_handle = {}
    dx3, dnw_mlp1 = mlp_backward(1, dx, d1, x3, h3, u1, s1)
    dyc, dg = _gate_bwd(dx3, mix1, g_m[1], "l1_mix_gate_bwd")
    dmod[1][2] = dg
    g_sc_out = _mm_tn(yc, dyc, BF16, "sc_out_wgrad").reshape(N_DEV, -1, D)
    dconv_out = _mm_nt(dyc, w_sc_out, F32, "sc_out_bwd")
    dbg, dcg, dxv, dscw = _sc_conv_bwd(proj, sc_conv_full, dconv_out, "sc_conv_bwd")
    dproj = jnp.concatenate([dbg, dcg, dxv], axis=1)
    g_sc_in = _mm_tn_blocked(h2, dproj, BF16, "sc_in_wgrad")
    (grad_handle["sc_out_w0"], grad_handle["sc_in_w0"]), token = _xfer_start(
        [g_sc_out, g_sc_in], "sc_grads_start", gather=False)
    dh2 = _mm_nt_blocked(dproj, sc_in_g, F32, "sc_in_bwd", after=(token,))
    dx2, dsh, dsc, dnw_mix1 = _norm_mod_bwd(dh2, x2, vec(mix_norm_w[1]), sc_m[1], dx3, "l1_mix_norm_bwd")
    dmod[1][0], dmod[1][1] = dsh, dsc
    dx1, dnw_mlp0 = mlp_backward(0, dx2, d0, x1, h1, u0, s0)
    dyo, dg = _gate_bwd(dx1, mix0, g_m[0], "l0_mix_gate_bwd")
    dmod[0][2] = dg
    g_ssd_out = _mm_tn(yn, dyo, BF16, "ssd_out_wgrad").reshape(N_DEV, -1, D)
    (grad_handle["ssd_out_w0"],), token = _xfer_start([g_ssd_out], "ssd_out_grad_start", gather=False)
    dyn = _mm_nt(dyo, w_ssd_out, F32, "ssd_out_bwd", after=(token,))
    dz, dxs, db_, dc_, ddt, dbias, dalog, dd_, dnw_ssd = _ssd_bwd(
        dyn, y_ssd, zx, xc, prev, bias_p, alog_p, d_lane, nw_g, d_inner, "ssd_scan_bwd")
    dxc = jnp.concatenate([dxs, db_, dc_], axis=1)
    dxbc, dcw, dcb = _ssd_conv_bwd(zx, conv_w_full, conv_b0, dxc, d_inner, "ssd_conv_bwd")
    dzx = jnp.concatenate([dz, dxbc, jnp.sum(ddt, axis=0).astype(BF16),
                           jnp.zeros((L, zx_pad - zx_dim - LANES), BF16)], axis=1)
    g_in_all = _mm_tn(h0, dzx, BF16, "ssd_in_wgrad", tn=512, tk=2048)
    g_ssd_in = jnp.stack([g_in_all[:, b:b + in_win] for b in in_base], axis=0)
    (grad_handle["ssd_in_w0"],), token = _xfer_start([g_ssd_in], "ssd_in_grad_start", gather=False)
    dh0 = _mm_nt(dzx, w_in_all, F32, "ssd_in_bwd", tk=dzx.shape[1] // 2, after=(token,))
    grad_x, dsh, dsc, dnw_mix0 = _norm_mod_bwd(dh0, x0, vec(mix_norm_w[0]), sc_m[0], dx1, "l0_mix_norm_bwd")
    dmod[0][0], dmod[0][1] = dsh, dsc

    small["mix_norm_w"] = jnp.concatenate([dnw_mix0, dnw_mix1], axis=0)
    small["mlp_norm_w"] = jnp.concatenate([dnw_mlp0, dnw_mlp1], axis=0)
    small["ssd_conv_w"] = dcw
    small["ssd_conv_b"] = dcb
    small["ssd_dt_bias"] = jnp.sum(dbias, axis=0)[:, :n_heads]
    small["ssd_A_log"] = jnp.sum(dalog, axis=0)[:, :n_heads]
    small["ssd_D"] = jnp.sum(dd_, axis=0)[:, :n_heads]
    small["ssd_norm_w"] = dnw_ssd
    small["sc_conv_w"] = dscw
    small["dmod"] = jnp.concatenate([jnp.concatenate(dmod[i], axis=1) for i in range(2)], axis=0)

    small_order = ["dmod", "mix_norm_w", "mlp_norm_w", "ssd_conv_w", "ssd_conv_b", "ssd_dt_bias", "ssd_A_log",
                   "ssd_D", "ssd_norm_w", "sc_conv_w", "final_norm_w"]
    flat = jnp.concatenate([small[k].reshape(-1) for k in small_order])
    n_small = flat.shape[0]
    n_small_pad = -(-n_small // 1024) * 1024
    flat = jnp.pad(flat, (0, n_small_pad - n_small)).reshape(n_small_pad // LANES, LANES)
    (small_handle,), small_token = _xfer_start([flat], "small_grads_start", gather=True)
    offs, o = {}, 0
    for k in small_order:
        offs[k] = (o, small[k].size, small[k].shape)
        o += small[k].size

    def small_parts(k):
        o, n, shape = offs[k]
        return small_all[:, o:o + n].reshape((N_DEV,) + shape)

    out_g, out_d, out_m, out_v = {}, {}, {}, {}

    layer_res = {}

    def big_update(name, i, after):
        parts = _xfer_wait(grad_handle[f"{name}{i}"], after, f"grads_wait_{name}_{i}", gather=False)
        wmv = [d[name] for d in (weights, moms, vars_)]
        if name == "ssd_in_w":
            wmv = [_to_window(a, in_off, in_win, me) for a in wmv]
        res = _adamw_sum(parts, *wmv, i, f"adamw_{name}_{i}", prev=layer_res.get(name))
        layer_res[name] = res
        return res[1]

    chain = small_token
    for name, i in [("mlp_down", 1), ("mlp_up", 1), ("sc_out_w", 0), ("sc_in_w", 0), ("mlp_down", 0),
                    ("mlp_up", 0), ("ssd_out_w", 0)]:
        chain = big_update(name, i, chain)
    small_all = _xfer_wait(small_handle, chain, "small_grads_wait", gather=True)
    small_all = small_all.reshape(N_DEV, n_small_pad)

    dmod_all = small_parts("dmod")
    dmod_loc = lax.dynamic_slice_in_dim(dmod_all, me * n_mod, n_mod, axis=2)
    dmod_pad = jnp.pad(jnp.transpose(dmod_loc, (1, 0, 2)), ((0, 0), (0, 16 - N_DEV), (0, 0)))
    out_g["ada_w"], out_d["ada_w"], out_m["ada_w"], out_v["ada_w"] = _ada_adamw(
        cond_pad, dmod_pad, ada_w, m_ada_w, v_ada_w, "adamw_ada_w")

    pieces = []
    pieces.append(("ada_b", dmod_all.reshape(N_DEV, -1)))
    for k in ["mix_norm_w", "mlp_norm_w", "ssd_conv_b", "ssd_dt_bias", "ssd_A_log", "ssd_D", "ssd_norm_w",
              "final_norm_w"]:
        pieces.append((k, small_parts(k).reshape(N_DEV, -1)))
    n_cw = ssd_conv_w.shape[2]
    pieces.append(("ssd_conv_w", lax.dynamic_slice_in_dim(small_parts("ssd_conv_w"), me * n_cw, n_cw, axis=2)
                   .reshape(N_DEV, -1)))
    n_scw = sc_conv_w.shape[2]
    pieces.append(("sc_conv_w", lax.dynamic_slice_in_dim(small_parts("sc_conv_w"), me * n_scw, n_scw, axis=2)
                   .reshape(N_DEV, -1)))
    n_tot = sum(p.shape[1] for _, p in pieces)
    n_tot_pad = -(-n_tot // 1024) * 1024

    def pack(arrs, lead=()):
        f = jnp.concatenate(arrs, axis=-1)
        f = jnp.pad(f, [(0, 0)] * len(lead) + [(0, n_tot_pad - n_tot)])
        return f.reshape(lead + (n_tot_pad // LANES, LANES))

    parts_flat = pack([p for _, p in pieces], lead=(N_DEV,))
    w_flat = pack([weights[k].reshape(-1) for k, _ in pieces])
    m_flat = pack([moms[k].reshape(-1) for k, _ in pieces])
    v_flat = pack([vars_[k].reshape(-1) for k, _ in pieces])
    res = _adamw_sum(parts_flat, w_flat[None], m_flat[None], v_flat[None], 0, "adamw_small",
                     tr=n_tot_pad // LANES)
    o = 0
    for k, p in pieces:
        n = p.shape[1]
        for r, dst in zip(res, (out_g, out_d, out_m, out_v)):
            dst[k] = r.reshape(-1)[o:o + n].reshape(weights[k].shape)
        o += n
    big_update("ssd_in_w", 0, res[1])
    for name, res4 in layer_res.items():
        for r, dst in zip(res4, (out_g, out_d, out_m, out_v)):
            dst[name] = _from_window(r, in_off, in_ws, me) if name == "ssd_in_w" else r

    return (loss, grad_x[None], *[out_g[k] for k in names], *[out_d[k] for k in names],
            *[out_m[k] for k in names], *[out_v[k] for k in names])
```

```python
import functools

import jax
import jax.numpy as jnp
from jax import lax
from jax.experimental import pallas as pl
from jax.experimental.pallas import tpu as pltpu

F32 = jnp.float32
BF16 = jnp.bfloat16
N_DEV = 8
MESH_AXES = ("x", "y", "c")
MESH = pl.DeviceIdType.MESH

NORM_EPS = 1e-5
SSD_G = 4
SSD_P = 64
SSD_N = 128
SSD_CHUNK = 128
SSD_K = 4
SC_K = 3
LANES = 128

ADAM_LR = 0.001
ADAM_B1 = 0.9
ADAM_B2 = 0.999
ADAM_EPS = 1e-08
ADAM_WD = 0.01
ADAM_STEP = 10

VMEM_LIMIT = 56 * 1024 * 1024


def _pcall(body, **kw):
    return pl.pallas_call(body, **kw)


def _cparams(sem=None):
    if sem is None:
        return pltpu.CompilerParams(vmem_limit_bytes=VMEM_LIMIT)
    return pltpu.CompilerParams(dimension_semantics=sem, vmem_limit_bytes=VMEM_LIMIT)


def _my_index():
    return 4 * lax.axis_index("x") + 2 * lax.axis_index("y") + lax.axis_index("c")


_PEER_MASKS = [(0, 0, 1), (0, 1, 0), (0, 1, 1), (1, 0, 0), (1, 0, 1), (1, 1, 0), (1, 1, 1)]


def _peers():
    x, y, c = lax.axis_index("x"), lax.axis_index("y"), lax.axis_index("c")
    out = []
    for mx, my, mc in _PEER_MASKS:
        px = (1 - x) if mx else x
        py = (1 - y) if my else y
        pc = (1 - c) if mc else c
        out.append(((px, py, pc), 4 * px + 2 * py + pc))
    return out


def _exchange(arrs, name, gather):
    n = len(arrs)
    n_peer = N_DEV - 1

    def body(*refs):
        ins, outs = refs[:n], refs[n:2 * n]
        send_sems, recv_sems, local_sems = refs[2 * n:]
        me = _my_index()
        peers = _peers()
        started = []
        for a in range(n):
            src_own = ins[a] if gather else ins[a].at[me]
            own = pltpu.make_async_copy(src_own, outs[a].at[me], local_sems.at[a])
            own.start()
            started.append(own)
        sends = []
        for a in range(n):
            for k, (peer, pidx) in enumerate(peers):
                src = ins[a] if gather else ins[a].at[pidx]
                cp = pltpu.make_async_remote_copy(
                    src_ref=src, dst_ref=outs[a].at[me],
                    send_sem=send_sems.at[a * n_peer + k], recv_sem=recv_sems.at[a * n_peer + k],
                    device_id=peer, device_id_type=MESH)
                cp.start()
                sends.append(cp)
        for a in range(n):
            for k, (peer, pidx) in enumerate(peers):
                src = ins[a] if gather else ins[a].at[pidx]
                pltpu.make_async_remote_copy(
                    src_ref=src, dst_ref=outs[a].at[pidx],
                    send_sem=send_sems.at[a * n_peer + k], recv_sem=recv_sems.at[a * n_peer + k],
                    device_id=peer, device_id_type=MESH).wait_recv()
        for cp in sends:
            cp.wait_send()
        for own in started:
            own.wait()

    if gather:
        out_shape = [jax.ShapeDtypeStruct((N_DEV,) + a.shape, a.dtype) for a in arrs]
    else:
        out_shape = [jax.ShapeDtypeStruct(a.shape, a.dtype) for a in arrs]
    any_spec = pl.BlockSpec(memory_space=pl.ANY)
    outs = _pcall(
        body, name=name, out_shape=out_shape,
        in_specs=[any_spec] * n, out_specs=[any_spec] * n,
        scratch_shapes=[pltpu.SemaphoreType.DMA((n * n_peer,)), pltpu.SemaphoreType.DMA((n * n_peer,)),
                        pltpu.SemaphoreType.DMA((n,))],
        compiler_params=pltpu.CompilerParams(has_side_effects=True),
    )(*arrs)
    return list(outs)


_HBM = pl.BlockSpec(memory_space=pltpu.HBM)
_SEM = pl.BlockSpec(memory_space=pltpu.SEMAPHORE)
_DATAFLOW = pltpu.SideEffectType.DATAFLOW_SIDE_EFFECTING


_ALL_PEERS = tuple(range(N_DEV - 1))
_SAME_CORE_PEERS = (0, 1, 3, 5)
_OTHER_CHIPS = (1, 3, 5)


def _xfer_start(arrs, name, gather, via_sibling=()):
    n = len(arrs)
    n_peer = N_DEV - 1
    peer_ks = [_SAME_CORE_PEERS if a in via_sibling else _ALL_PEERS for a in range(n)]

    def body(*refs):
        ins, lands = refs[:n], refs[n:2 * n]
        sems = refs[2 * n:5 * n]
        token = refs[-1]
        me = _my_index()
        peers = _peers()
        for a in range(n):
            send_sems, recv_sems, loc_sem = sems[3 * a:3 * a + 3]
            src_own = ins[a] if gather else ins[a].at[me]
            pltpu.make_async_copy(src_own, lands[a].at[me], loc_sem).start()
            for k in peer_ks[a]:
                peer, pidx = peers[k]
                src = ins[a] if gather else ins[a].at[pidx]
                pltpu.make_async_remote_copy(
                    src_ref=src, dst_ref=lands[a].at[me], send_sem=send_sems.at[k], recv_sem=recv_sems.at[k],
                    device_id=peer, device_id_type=MESH).start()
        token[...] = jnp.zeros_like(token)

    land_shapes = [((N_DEV,) + a.shape) if gather else a.shape for a in arrs]
    out_shape, out_specs = [], []
    for _ in range(n):
        out_shape += [pltpu.SemaphoreType.DMA((n_peer,)), pltpu.SemaphoreType.DMA((n_peer,)),
                      pltpu.SemaphoreType.DMA(())]
        out_specs += [_SEM, _SEM, _SEM]
    out_shape += [pltpu.HBM(a.shape, a.dtype) for a in arrs]
    out_shape += [pltpu.HBM(s, a.dtype) for s, a in zip(land_shapes, arrs)]
    out_shape += [jax.ShapeDtypeStruct((8, LANES), F32)]
    out_specs += [_HBM] * (2 * n) + [pl.BlockSpec(memory_space=pltpu.VMEM)]
    aliases = {}
    for a in range(n):
        aliases[a] = 3 * n + a
        aliases[n + a] = 4 * n + a
    operands = [pltpu.with_memory_space_constraint(a, pltpu.HBM) for a in arrs]
    operands += [pltpu.with_memory_space_constraint(lax.empty(s, a.dtype), pltpu.HBM)
                 for s, a in zip(land_shapes, arrs)]
    outs = _pcall(
        body, name=name, out_shape=tuple(out_shape), in_specs=[_HBM] * (2 * n), out_specs=tuple(out_specs),
        input_output_aliases=aliases,
        compiler_params=pltpu.CompilerParams(has_side_effects=_DATAFLOW),
    )(*operands)
    handles = []
    for a in range(n):
        handles.append((outs[3 * n + a], outs[4 * n + a], outs[3 * a], outs[3 * a + 1], outs[3 * a + 2],
                        peer_ks[a]))
    return handles, outs[-1]


def _xfer_wait(handles, after, name, gather):
    n = len(handles)
    peer_ks = [h[5] for h in handles]

    def body(*refs):
        me = _my_index()
        peers = _peers()
        for a in range(n):
            src_ref, land_ref, send_ref, recv_ref, loc_ref = refs[5 * a:5 * a + 5]
            src_own = src_ref if gather else src_ref.at[me]
            pltpu.make_async_copy(src_own, land_ref.at[me], loc_ref).wait()
            for k in peer_ks[a]:
                peer, pidx = peers[k]
                src = src_ref if gather else src_ref.at[pidx]
                cp = pltpu.make_async_remote_copy(
                    src_ref=src, dst_ref=land_ref.at[pidx], send_sem=send_ref.at[k], recv_sem=recv_ref.at[k],
                    device_id=peer, device_id_type=MESH)
                cp.wait_send()
                cp.wait_recv()

    operands, in_specs, out_shape, aliases = [], [], [], {}
    for a, h in enumerate(handles):
        operands += list(h[:5])
        in_specs += [_HBM, _HBM, _SEM, _SEM, _SEM]
        out_shape += [pltpu.HBM(h[0].shape, h[0].dtype), pltpu.HBM(h[1].shape, h[1].dtype)]
        aliases[5 * a] = 2 * a
        aliases[5 * a + 1] = 2 * a + 1
    outs = _pcall(
        body, name=name, out_shape=tuple(out_shape),
        in_specs=in_specs + [pl.BlockSpec(memory_space=pl.ANY)], out_specs=tuple([_HBM] * (2 * n)),
        input_output_aliases=aliases,
        compiler_params=pltpu.CompilerParams(has_side_effects=_DATAFLOW),
    )(*operands, after)
    return [outs[2 * a + 1] for a in range(n)]


def _sibling_forward(lands, name):
    n = len(lands)
    n_fwd = len(_OTHER_CHIPS)

    def body(*refs):
        ins, bufs = refs[:n], refs[n:2 * n]
        send_sems, recv_sems = refs[2 * n:]
        x, y, c = lax.axis_index("x"), lax.axis_index("y"), lax.axis_index("c")
        sibling = (x, y, 1 - c)
        peers = _peers()
        sends = []
        for a in range(n):
            for j, k in enumerate(_OTHER_CHIPS):
                slot = peers[k][1]
                cp = pltpu.make_async_remote_copy(
                    src_ref=ins[a].at[slot], dst_ref=bufs[a].at[slot],
                    send_sem=send_sems.at[a * n_fwd + j], recv_sem=recv_sems.at[a * n_fwd + j],
                    device_id=sibling, device_id_type=MESH)
                cp.start()
                sends.append(cp)
        for a in range(n):
            for j, k in enumerate(_OTHER_CHIPS):
                (px, py, pc), slot = peers[k]
                theirs = 4 * px + 2 * py + (1 - pc)
                pltpu.make_async_remote_copy(
                    src_ref=ins[a].at[slot], dst_ref=bufs[a].at[theirs],
                    send_sem=send_sems.at[a * n_fwd + j], recv_sem=recv_sems.at[a * n_fwd + j],
                    device_id=sibling, device_id_type=MESH).wait_recv()
        for cp in sends:
            cp.wait_send()

    any_spec = pl.BlockSpec(memory_space=pl.ANY)
    outs = _pcall(
        body, name=name, out_shape=[jax.ShapeDtypeStruct(a.shape, a.dtype) for a in lands],
        in_specs=[any_spec] * n, out_specs=[any_spec] * n,
        input_output_aliases={a: a for a in range(n)},
        scratch_shapes=[pltpu.SemaphoreType.DMA((n * n_fwd,)), pltpu.SemaphoreType.DMA((n * n_fwd,))],
        compiler_params=pltpu.CompilerParams(has_side_effects=True),
    )(*lands)
    return list(outs)


_DIMS = {"nn": (((1,), (0,)), ((), ())), "nt": (((1,), (1,)), ((), ())), "tn": (((0,), (0,)), ((), ()))}


def _dot(a, b, mode="nn"):
    return lax.dot_general(a, b, _DIMS[mode], preferred_element_type=F32)


def _mm(a, b, *, mode, grid, a_spec, b_spec, out_shape, out_specs, acc_shape, epilogue, name,
        extra=(), extra_specs=(), after=()):
    nk = grid[2]
    n_extra = len(extra)
    n_in = 2 + n_extra + len(after)

    def body_single(*refs):
        a_ref, b_ref = refs[0], refs[1]
        epilogue(_dot(a_ref[...], b_ref[...], mode), refs[2:2 + n_extra], refs[n_in:])

    def body_acc(*refs):
        a_ref, b_ref = refs[0], refs[1]
        ex = refs[2:2 + n_extra]
        outs = refs[n_in:-1]
        acc = refs[-1]
        k = pl.program_id(2)

        @pl.when(k == 0)
        def _():
            acc[...] = jnp.zeros_like(acc)

        acc[...] += _dot(a_ref[...], b_ref[...], mode)

        @pl.when(k == nk - 1)
        def _():
            epilogue(acc[...], ex, outs)

    return _pcall(
        body_single if nk == 1 else body_acc, name=name, grid=grid, out_shape=out_shape,
        in_specs=[a_spec, b_spec] + list(extra_specs) + [pl.BlockSpec(memory_space=pl.ANY)] * len(after),
        out_specs=out_specs,
        scratch_shapes=[] if nk == 1 else [pltpu.VMEM(acc_shape, F32)],
        compiler_params=_cparams(("parallel", "parallel", "arbitrary")),
    )(a, b, *extra, *after)


def _ep_store(dtype):
    def ep(acc, ex, outs):
        outs[0][...] = acc.astype(dtype)
    return ep


def _ep_relu2(acc, ex, outs):
    outs[0][...] = acc.astype(BF16)
    r = jnp.maximum(acc, 0.0)
    outs[1][...] = (r * r).astype(BF16)


def _ep_relu2_bwd(acc, ex, outs):
    u = ex[0][...].astype(F32)
    outs[0][...] = (acc * (2.0 * jnp.maximum(u, 0.0))).astype(BF16)


def _tile(n, want):
    t = min(n, want)
    while n % t:
        t //= 2
    return t


def _mm_nn(a, w, out_dtype, name, tm=1024, tn=1024, tk=1024, epilogue=None, out_dtypes=None):
    M, K = a.shape
    N = w.shape[1]
    tm, tn, tk = _tile(M, tm), _tile(N, tn), _tile(K, tk)
    out_dtypes = out_dtypes or [out_dtype]
    return _mm(a, w, mode="nn", grid=(M // tm, N // tn, K // tk),
               a_spec=pl.BlockSpec((tm, tk), lambda i, j, k: (i, k)),
               b_spec=pl.BlockSpec((tk, tn), lambda i, j, k: (k, j)),
               out_shape=[jax.ShapeDtypeStruct((M, N), d) for d in out_dtypes],
               out_specs=[pl.BlockSpec((tm, tn), lambda i, j, k: (i, j)) for _ in out_dtypes],
               acc_shape=(tm, tn), epilogue=epilogue or _ep_store(out_dtype), name=name)


def _mm_nn_blocked(a, wg, name, epilogue, out_dtypes, tm=2048):
    M, K = a.shape
    n = wg.shape[2]
    tm = _tile(M, tm)
    return _mm(a, wg, mode="nn", grid=(M // tm, N_DEV, 1),
               a_spec=pl.BlockSpec((tm, K), lambda i, j, k: (i, 0)),
               b_spec=pl.BlockSpec((None, K, n), lambda i, j, k: (j, 0, 0)),
               out_shape=[jax.ShapeDtypeStruct((M, N_DEV * n), d) for d in out_dtypes],
               out_specs=[pl.BlockSpec((tm, n), lambda i, j, k: (i, j)) for _ in out_dtypes],
               acc_shape=(tm, n), epilogue=epilogue, name=name)


def _mm_nt(a, w, out_dtype, name, tm=1024, tn=1024, tk=1024, epilogue=None, extra=(), extra_specs=(),
           after=()):
    M, K = a.shape
    N = w.shape[0]
    tm, tn, tk = _tile(M, tm), _tile(N, tn), _tile(K, tk)
    if extra and not extra_specs:
        extra_specs = [pl.BlockSpec((tm, tn), lambda i, j, k: (i, j)) for _ in extra]
    return _mm(a, w, mode="nt", grid=(M // tm, N // tn, K // tk),
               a_spec=pl.BlockSpec((tm, tk), lambda i, j, k: (i, k)),
               b_spec=pl.BlockSpec((tn, tk), lambda i, j, k: (j, k)),
               out_shape=[jax.ShapeDtypeStruct((M, N), out_dtype)],
               out_specs=[pl.BlockSpec((tm, tn), lambda i, j, k: (i, j))],
               acc_shape=(tm, tn), epilogue=epilogue or _ep_store(out_dtype), name=name,
               extra=extra, extra_specs=extra_specs, after=after)[0]


def _mm_nt_blocked(a, wg, out_dtype, name, tm=1024, after=()):
    M = a.shape[0]
    kout, n = wg.shape[1], wg.shape[2]
    tm = _tile(M, tm)
    return _mm(a, wg, mode="nt", grid=(M // tm, 1, N_DEV),
               a_spec=pl.BlockSpec((tm, n), lambda i, j, k: (i, k)),
               b_spec=pl.BlockSpec((None, kout, n), lambda i, j, k: (k, 0, 0)),
               out_shape=[jax.ShapeDtypeStruct((M, kout), out_dtype)],
               out_specs=[pl.BlockSpec((tm, kout), lambda i, j, k: (i, 0))],
               acc_shape=(tm, kout), epilogue=_ep_store(out_dtype), name=name, after=after)[0]


def _mm_tn(a, b, out_dtype, name, tm=1024, tn=1024, tk=1024):
    K, M = a.shape
    N = b.shape[1]
    tm, tn, tk = _tile(M, tm), _tile(N, tn), _tile(K, tk)
    return _mm(a, b, mode="tn", grid=(M // tm, N // tn, K // tk),
               a_spec=pl.BlockSpec((tk, tm), lambda i, j, k: (k, i)),
               b_spec=pl.BlockSpec((tk, tn), lambda i, j, k: (k, j)),
               out_shape=[jax.ShapeDtypeStruct((M, N), out_dtype)],
               out_specs=[pl.BlockSpec((tm, tn), lambda i, j, k: (i, j))],
               acc_shape=(tm, tn), epilogue=_ep_store(out_dtype), name=name)[0]


def _mm_tn_blocked(a, b, out_dtype, name, tm=1024, tk=2048):
    K, M = a.shape
    n = b.shape[1] // N_DEV
    tm, tk = _tile(M, tm), _tile(K, tk)
    return _mm(a, b, mode="tn", grid=(M // tm, N_DEV, K // tk),
               a_spec=pl.BlockSpec((tk, tm), lambda i, j, k: (k, i)),
               b_spec=pl.BlockSpec((tk, n), lambda i, j, k: (k, j)),
               out_shape=[jax.ShapeDtypeStruct((N_DEV, M, n), out_dtype)],
               out_specs=[pl.BlockSpec((None, tm, n), lambda i, j, k: (j, i, 0))],
               acc_shape=(tm, n), epilogue=_ep_store(out_dtype), name=name)[0]


def _window_geometry(ws):
    base = [(ws * k // LANES) * LANES for k in range(N_DEV)]
    off = [ws * k - base[k] for k in range(N_DEV)]
    win = -(-(max(off) + ws) // LANES) * LANES
    return base, off, win


def _to_window(shard, off, win, me):
    pad = jnp.zeros(shard.shape[:-1] + (win,), shard.dtype)
    start = (0,) * (shard.ndim - 1) + (jnp.asarray(off, jnp.int32)[me],)
    return lax.dynamic_update_slice(pad, shard, start)


def _windows_to_columns(xg, base, win, n_out, name, tr=256):
    R = xg.shape[1]
    tr = _tile(R, tr)
    nb_win = win // LANES

    def body(x_ref, o_ref):
        for b in range(n_out // LANES):
            acc = None
            for k in range(N_DEV):
                i = b - base[k] // LANES
                if 0 <= i < nb_win:
                    blk = x_ref[k, :, i * LANES:(i + 1) * LANES].astype(F32)
                    acc = blk if acc is None else acc + blk
            if acc is None:
                acc = jnp.zeros((tr, LANES), F32)
            o_ref[:, b * LANES:(b + 1) * LANES] = acc.astype(o_ref.dtype)

    return _pcall(
        body, name=name, grid=(R // tr,), out_shape=jax.ShapeDtypeStruct((R, n_out), xg.dtype),
        in_specs=[pl.BlockSpec((N_DEV, tr, win), lambda i: (0, i, 0))],
        out_specs=pl.BlockSpec((tr, n_out), lambda i: (i, 0)),
        compiler_params=_cparams(("parallel",)))(xg)


def _sigmoid(x):
    return 1.0 / (1.0 + jnp.exp(-x))


def _row_spec(tm, d):
    return pl.BlockSpec((tm, d), lambda i: (i, 0))


def _vec_spec(d):
    return pl.BlockSpec((1, d), lambda i: (0, 0))


def _norm_mod_fwd(x, y, gate, nw, scale, shift, name, tm=256):
    L, D = x.shape
    tm = _tile(L, tm)
    has_res = y is not None

    def body(*refs):
        if has_res:
            x_ref, y_ref, g_ref, nw_ref, sc_ref, sh_ref, xo_ref, h_ref = refs
            xn = x_ref[...] + g_ref[...] * y_ref[...]
            xo_ref[...] = xn
        else:
            x_ref, nw_ref, sc_ref, sh_ref, h_ref = refs
            xn = x_ref[...]
        rstd = lax.rsqrt(jnp.mean(xn * xn, axis=-1, keepdims=True) + NORM_EPS)
        h = xn * rstd * nw_ref[...] * (1.0 + sc_ref[...]) + sh_ref[...]
        h_ref[...] = h.astype(BF16)

    row, vec = _row_spec(tm, D), _vec_spec(D)
    if has_res:
        ins, in_specs = (x, y, gate, nw, scale, shift), [row, row, vec, vec, vec, vec]
        out_shape = [jax.ShapeDtypeStruct((L, D), F32), jax.ShapeDtypeStruct((L, D), BF16)]
        out_specs = [row, row]
    else:
        ins, in_specs = (x, nw, scale, shift), [row, vec, vec, vec]
        out_shape = [jax.ShapeDtypeStruct((L, D), BF16)]
        out_specs = [row]
    outs = _pcall(body, name=name, grid=(L // tm,), out_shape=out_shape, in_specs=in_specs,
                  out_specs=out_specs, compiler_params=_cparams(("parallel",)))(*ins)
    return outs if has_res else (x, outs[0])


def _norm_mod_bwd(dh, x, nw, scale, dres, name, tm=256):
    L, D = x.shape
    tm = _tile(L, tm)

    def body(dh_ref, x_ref, nw_ref, sc_ref, dres_ref, dx_ref, dsh_ref, dsc_ref, dnw_ref):
        @pl.when(pl.program_id(0) == 0)
        def _():
            dsh_ref[...] = jnp.zeros_like(dsh_ref)
            dsc_ref[...] = jnp.zeros_like(dsc_ref)
            dnw_ref[...] = jnp.zeros_like(dnw_ref)

        xv = x_ref[...]
        dh_v = dh_ref[...]
        nw_v = nw_ref[...]
        rstd = lax.rsqrt(jnp.mean(xv * xv, axis=-1, keepdims=True) + NORM_EPS)
        xhat = xv * rstd
        dsh_ref[...] += jnp.sum(dh_v, axis=0, keepdims=True)
        dsc_ref[...] += jnp.sum(dh_v * (xhat * nw_v), axis=0, keepdims=True)
        dr = dh_v * (1.0 + sc_ref[...])
        dnw_ref[...] += jnp.sum(dr * xhat, axis=0, keepdims=True)
        dxh = dr * nw_v
        dx = rstd * (dxh - xhat * jnp.mean(dxh * xhat, axis=-1, keepdims=True))
        dx_ref[...] = dx + dres_ref[...]

    row, vec = _row_spec(tm, D), _vec_spec(D)
    return _pcall(
        body, name=name, grid=(L // tm,),
        out_shape=[jax.ShapeDtypeStruct((L, D), F32)] + [jax.ShapeDtypeStruct((1, D), F32)] * 3,
        in_specs=[row, row, vec, vec, row], out_specs=[row, vec, vec, vec],
        compiler_params=_cparams(("arbitrary",)))(dh, x, nw, scale, dres)


def _gate_bwd(dx, y, gate, name, tm=256):
    L, D = dx.shape
    tm = _tile(L, tm)

    def body(dx_ref, y_ref, g_ref, dy_ref, dg_ref):
        @pl.when(pl.program_id(0) == 0)
        def _():
            dg_ref[...] = jnp.zeros_like(dg_ref)

        dxv = dx_ref[...]
        dy_ref[...] = (g_ref[...] * dxv).astype(BF16)
        dg_ref[...] += jnp.sum(dxv * y_ref[...], axis=0, keepdims=True)

    row, vec = _row_spec(tm, D), _vec_spec(D)
    return _pcall(
        body, name=name, grid=(L // tm,),
        out_shape=[jax.ShapeDtypeStruct((L, D), BF16), jax.ShapeDtypeStruct((1, D), F32)],
        in_specs=[row, row, vec], out_specs=[row, vec],
        compiler_params=_cparams(("arbitrary",)))(dx, y, gate)


def _final_loss(x, y, gate, fw, target, name, tm=256):
    L, D = x.shape
    tm = _tile(L, tm)

    def body(x_ref, y_ref, g_ref, fw_ref, t_ref, dx_ref, loss_ref, dfw_ref):
        @pl.when(pl.program_id(0) == 0)
        def _():
            loss_ref[...] = jnp.zeros_like(loss_ref)
            dfw_ref[...] = jnp.zeros_like(dfw_ref)

        xn = x_ref[...] + g_ref[...] * y_ref[...]
        fw_v = fw_ref[...]
        rstd = lax.rsqrt(jnp.mean(xn * xn, axis=-1, keepdims=True) + NORM_EPS)
        xhat = xn * rstd
        diff = xhat * fw_v - t_ref[...]
        loss_ref[...] += jnp.sum(diff * diff, axis=0, keepdims=True)
        dyf = diff * (1.0 / D)
        dfw_ref[...] += jnp.sum(dyf * xhat, axis=0, keepdims=True)
        dxh = dyf * fw_v
        dx_ref[...] = rstd * (dxh - xhat * jnp.mean(dxh * xhat, axis=-1, keepdims=True))

    row, vec = _row_spec(tm, D), _vec_spec(D)
    return _pcall(
        body, name=name, grid=(L // tm,),
        out_shape=[jax.ShapeDtypeStruct((L, D), F32), jax.ShapeDtypeStruct((1, D), F32),
                   jax.ShapeDtypeStruct((1, D), F32)],
        in_specs=[row, row, vec, vec, row], out_specs=[row, vec, vec],
        compiler_params=_cparams(("arbitrary",)))(x, y, gate, fw, target)


def _shift_down(v, s, row):
    if s == 0:
        return v
    return jnp.where(row >= s, pltpu.roll(v, s, 0), 0.0)


def _shift_up(v, s, row):
    if s == 0:
        return v
    n = v.shape[0]
    return jnp.where(row < n - s, pltpu.roll(v, n - s, 0), 0.0)


def _ssd_conv_fwd(zx, w, b, col0, width, name, cb=128):
    L = zx.shape[0]
    nb = width // cb
    off = col0 // cb

    def body(x_ref, w_ref, b_ref, o_ref):
        xv = x_ref[...]
        row = lax.broadcasted_iota(jnp.int32, xv.shape, 0)
        acc = b_ref[...] + w_ref[SSD_K - 1:SSD_K, :] * xv
        for s in range(1, SSD_K):
            acc = acc + w_ref[SSD_K - 1 - s:SSD_K - s, :] * _shift_down(xv, s, row)
        o_ref[...] = acc * _sigmoid(acc)

    return _pcall(
        body, name=name, grid=(nb,), out_shape=jax.ShapeDtypeStruct((L, width), F32),
        in_specs=[pl.BlockSpec((L, cb), lambda j: (0, off + j)),
                  pl.BlockSpec((SSD_K, cb), lambda j: (0, j)),
                  pl.BlockSpec((1, cb), lambda j: (0, j))],
        out_specs=pl.BlockSpec((L, cb), lambda j: (0, j)),
        compiler_params=_cparams(("parallel",)))(zx, w, b)


def _ssd_conv_bwd(zx, w, b, dxc, col0, name, cb=128):
    L = zx.shape[0]
    width = dxc.shape[1]
    nb = width // cb
    off = col0 // cb

    def body(x_ref, w_ref, b_ref, d_ref, dx_ref, dw_ref, db_ref):
        xv = x_ref[...]
        row = lax.broadcasted_iota(jnp.int32, xv.shape, 0)
        shifted = [_shift_down(xv, s, row) for s in range(SSD_K)]
        acc = b_ref[...] + w_ref[SSD_K - 1:SSD_K, :] * xv
        for s in range(1, SSD_K):
            acc = acc + w_ref[SSD_K - 1 - s:SSD_K - s, :] * shifted[s]
        sig = _sigmoid(acc)
        dpre = d_ref[...] * (sig * (1.0 + acc * (1.0 - sig)))
        db_ref[...] = jnp.sum(dpre, axis=0, keepdims=True)
        dx = w_ref[SSD_K - 1:SSD_K, :] * dpre
        for s in range(SSD_K):
            dw_ref[SSD_K - 1 - s:SSD_K - s, :] = jnp.sum(dpre * shifted[s], axis=0, keepdims=True)
            if s:
                dx = dx + w_ref[SSD_K - 1 - s:SSD_K - s, :] * _shift_up(dpre, s, row)
        dx_ref[...] = dx.astype(BF16)

    return _pcall(
        body, name=name, grid=(nb,),
        out_shape=[jax.ShapeDtypeStruct((L, width), BF16), jax.ShapeDtypeStruct((SSD_K, width), F32),
                   jax.ShapeDtypeStruct((1, width), F32)],
        in_specs=[pl.BlockSpec((L, cb), lambda j: (0, off + j)),
                  pl.BlockSpec((SSD_K, cb), lambda j: (0, j)),
                  pl.BlockSpec((1, cb), lambda j: (0, j)),
                  pl.BlockSpec((L, cb), lambda j: (0, j))],
        out_specs=[pl.BlockSpec((L, cb), lambda j: (0, j)),
                   pl.BlockSpec((SSD_K, cb), lambda j: (0, j)),
                   pl.BlockSpec((1, cb), lambda j: (0, j))],
        compiler_params=_cparams(("parallel",)))(zx, w, b, dxc)


def _sc_conv_fwd(proj, w, name, cb=128):
    L = proj.shape[0]
    width = proj.shape[1] // 3
    nb = width // cb

    def body(b_ref, c_ref, x_ref, w_ref, o_ref):
        q = c_ref[...] * x_ref[...]
        row = lax.broadcasted_iota(jnp.int32, q.shape, 0)
        acc = w_ref[SC_K - 1:SC_K, :] * q
        for s in range(1, SC_K):
            acc = acc + w_ref[SC_K - 1 - s:SC_K - s, :] * _shift_down(q, s, row)
        o_ref[...] = (b_ref[...] * acc).astype(BF16)

    return _pcall(
        body, name=name, grid=(nb,), out_shape=jax.ShapeDtypeStruct((L, width), BF16),
        in_specs=[pl.BlockSpec((L, cb), lambda j: (0, j)),
                  pl.BlockSpec((L, cb), lambda j: (0, nb + j)),
                  pl.BlockSpec((L, cb), lambda j: (0, 2 * nb + j)),
                  pl.BlockSpec((SC_K, cb), lambda j: (0, j))],
        out_specs=pl.BlockSpec((L, cb), lambda j: (0, j)),
        compiler_params=_cparams(("parallel",)))(proj, proj, proj, w)


def _sc_conv_bwd(proj, w, dy, name, cb=128):
    L = proj.shape[0]
    width = proj.shape[1] // 3
    nb = width // cb

    def body(b_ref, c_ref, x_ref, w_ref, dy_ref, db_ref, dc_ref, dxv_ref, dw_ref):
        cg, xv, dyv = c_ref[...], x_ref[...], dy_ref[...]
        q = cg * xv
        row = lax.broadcasted_iota(jnp.int32, q.shape, 0)
        shifted = [_shift_down(q, s, row) for s in range(SC_K)]
        conv = w_ref[SC_K - 1:SC_K, :] * q
        for s in range(1, SC_K):
            conv = conv + w_ref[SC_K - 1 - s:SC_K - s, :] * shifted[s]
        db_ref[...] = (dyv * conv).astype(BF16)
        dconv = dyv * b_ref[...]
        dq = w_ref[SC_K - 1:SC_K, :] * dconv
        for s in range(SC_K):
            dw_ref[SC_K - 1 - s:SC_K - s, :] = jnp.sum(dconv * shifted[s], axis=0, keepdims=True)
            if s:
                dq = dq + w_ref[SC_K - 1 - s:SC_K - s, :] * _shift_up(dconv, s, row)
        dc_ref[...] = (dq * xv).astype(BF16)
        dxv_ref[...] = (dq * cg).astype(BF16)

    blk = pl.BlockSpec((L, cb), lambda j: (0, j))
    wblk = pl.BlockSpec((SC_K, cb), lambda j: (0, j))
    return _pcall(
        body, name=name, grid=(nb,),
        out_shape=[jax.ShapeDtypeStruct((L, width), BF16)] * 3 + [jax.ShapeDtypeStruct((SC_K, width), F32)],
        in_specs=[blk, pl.BlockSpec((L, cb), lambda j: (0, nb + j)),
                  pl.BlockSpec((L, cb), lambda j: (0, 2 * nb + j)), wblk, blk],
        out_specs=[blk, blk, blk, wblk],
        compiler_params=_cparams(("parallel",)))(proj, proj, proj, w, dy)


def _split3(v):
    hi = v.astype(BF16)
    r1 = v - hi.astype(F32)
    mid = r1.astype(BF16)
    lo = (r1 - mid.astype(F32)).astype(BF16)
    return hi, mid, lo


def _dot_exact01(t01, v):
    hi, mid, lo = _split3(v)
    return _dot(t01, hi) + _dot(t01, mid) + _dot(t01, lo)


def _lane_col(v, lane, h):
    return jnp.sum(jnp.where(lane == h, v, 0.0), axis=1, keepdims=True)


def _sum_all(v):
    return jnp.sum(jnp.sum(v, axis=1, keepdims=True), axis=0, keepdims=True)


def _softplus(x):
    return jnp.maximum(x, 0.0) + jnp.log1p(jnp.exp(-jnp.abs(x)))


def _ssd_common(dt_ref, bias_ref, alog_ref, b_ref, c_ref, cst_ref, heads):
    c_sz = SSD_CHUNK
    lane = lax.broadcasted_iota(jnp.int32, (c_sz, LANES), 1)
    row = lax.broadcasted_iota(jnp.int32, (c_sz, LANES), 0)
    valid = lane < heads
    raw = dt_ref[...] + bias_ref[...]
    dt = _softplus(raw)
    a_row = -jnp.exp(alog_ref[...])
    a = jnp.where(valid, dt * a_row, 0.0)
    tri = (row >= lane).astype(BF16)
    cs = _dot_exact01(tri, a)
    cst_ref[...] = cs.T
    last_row = jnp.sum(a, axis=0, keepdims=True)
    bb = b_ref[...].astype(BF16)
    cb = c_ref[...].astype(BF16)
    scores = _dot(cb, bb, "nt")
    return dict(lane=lane, row=row, valid=valid, raw=raw, dt=dt, a_row=a_row, cs=cs,
                last_row=last_row, bb=bb, cb=cb, scores=scores, causal=row >= lane, lo=lane < SSD_P)


def _pair_terms(q, cst_ref, h0):
    lane, lo = q["lane"], q["lo"]
    out = {}
    cols, dts, lasts, lms = [], [], [], []
    lane1 = lax.broadcasted_iota(jnp.int32, (1, LANES), 1)
    for h in (h0, h0 + 1):
        col = _lane_col(q["cs"], lane, h)
        rowv = cst_ref[pl.ds(h, 1), :]
        lms.append(jnp.exp(jnp.where(q["causal"], col - rowv, -1e30)))
        cols.append(col)
        dts.append(_lane_col(q["dt"], lane, h))
        lasts.append(jnp.sum(jnp.where(lane1 == h, q["last_row"], 0.0), axis=1, keepdims=True))
    out["lm"] = lms
    out["cols"] = cols
    out["lasts"] = lasts
    out["dt_b"] = jnp.where(lo, dts[0], dts[1])
    out["e_b"] = jnp.where(lo, jnp.exp(cols[0]), jnp.exp(cols[1]))
    out["dec_cols"] = [jnp.exp(lasts[0] - cols[0]), jnp.exp(lasts[1] - cols[1])]
    out["dec_b"] = jnp.where(lo, out["dec_cols"][0], out["dec_cols"][1])
    lo1 = lane1 < SSD_P
    out["explast"] = [jnp.exp(lasts[0]), jnp.exp(lasts[1])]
    out["explast_b"] = jnp.where(lo1, out["explast"][0], out["explast"][1])
    return out


def _ssd_fwd(zx, xc, bias_p, alog_p, d_lane, nw, d_inner, name):
    L = zx.shape[0]
    nc = L // SSD_CHUNK
    gw = d_inner // SSD_G
    heads = gw // SSD_P
    n_pair = heads // 2
    bc0 = d_inner // LANES
    dt0 = (2 * d_inner + 2 * SSD_G * SSD_N) // LANES

    def body(z_ref, xs_ref, b_ref, c_ref, dt_ref, bias_ref, alog_ref, dl_ref, nw_ref,
             y_ref, yn_ref, prev_ref, s_ref, cst_ref):
        @pl.when(pl.program_id(1) == 0)
        def _():
            s_ref[...] = jnp.zeros_like(s_ref)

        q = _ssd_common(dt_ref, bias_ref, alog_ref, b_ref, c_ref, cst_ref, SSD_G * heads)
        prev_ref[...] = s_ref[...]
        lo = q["lo"]
        for j in range(n_pair):
            sl = slice(j * LANES, (j + 1) * LANES)
            p = _pair_terms(q, cst_ref, pl.program_id(0) * heads + 2 * j)
            xs_p = xs_ref[:, sl]
            xp = xs_p * p["dt_b"]
            xb = xp.astype(BF16)
            m_a = (q["scores"] * p["lm"][0]).astype(BF16)
            m_b = (q["scores"] * p["lm"][1]).astype(BF16)
            yd = jnp.where(lo, _dot(m_a, xb), _dot(m_b, xb))
            s_p = s_ref[:, sl]
            yo = _dot(q["cb"], s_p.astype(BF16)) * p["e_b"]
            y_ref[:, sl] = yd + yo + dl_ref[:, sl] * xs_p
            st = _dot(q["bb"], (xp * p["dec_b"]).astype(BF16), "tn")
            s_ref[:, sl] = s_p * p["explast_b"] + st
        yv = y_ref[...]
        zv = z_ref[...]
        yg = yv * (zv * _sigmoid(zv))
        rstd = lax.rsqrt(jnp.mean(yg * yg, axis=-1, keepdims=True) + NORM_EPS)
        yn_ref[...] = (yg * rstd * nw_ref[...]).astype(BF16)

    grp = lambda width: pl.BlockSpec((None, 1, width), lambda g, c: (g, 0, 0))
    head_vec = pl.BlockSpec((1, LANES), lambda g, c: (0, 0))
    return _pcall(
        body, name=name, grid=(SSD_G, nc),
        out_shape=[jax.ShapeDtypeStruct((L, d_inner), F32), jax.ShapeDtypeStruct((L, d_inner), BF16),
                   jax.ShapeDtypeStruct((nc, SSD_G, SSD_N, gw), F32)],
        in_specs=[pl.BlockSpec((SSD_CHUNK, gw), lambda g, c: (c, g)),
                  pl.BlockSpec((SSD_CHUNK, gw), lambda g, c: (c, g)),
                  pl.BlockSpec((SSD_CHUNK, SSD_N), lambda g, c: (c, bc0 + g)),
                  pl.BlockSpec((SSD_CHUNK, SSD_N), lambda g, c: (c, bc0 + SSD_G + g)),
                  pl.BlockSpec((SSD_CHUNK, LANES), lambda g, c: (c, dt0)),
                  head_vec, head_vec, grp(gw), grp(gw)],
        out_specs=[pl.BlockSpec((SSD_CHUNK, gw), lambda g, c: (c, g)),
                   pl.BlockSpec((SSD_CHUNK, gw), lambda g, c: (c, g)),
                   pl.BlockSpec((None, None, SSD_N, gw), lambda g, c: (c, g, 0, 0))],
        scratch_shapes=[pltpu.VMEM((SSD_N, gw), F32), pltpu.VMEM((SSD_CHUNK, LANES), F32)],
        compiler_params=_cparams(("parallel", "arbitrary")))(zx, xc, xc, xc, zx, bias_p, alog_p, d_lane, nw)


def _ssd_bwd(dyn, y, zx, xc, prev, bias_p, alog_p, d_lane, nw, d_inner, name):
    L = zx.shape[0]
    nc = L // SSD_CHUNK
    gw = d_inner // SSD_G
    heads = gw // SSD_P
    n_pair = heads // 2
    bc0 = d_inner // LANES
    dt0 = (2 * d_inner + 2 * SSD_G * SSD_N) // LANES

    def body(dyn_ref, y_ref, z_ref, xs_ref, b_ref, c_ref, dt_ref, prev_ref, bias_ref, alog_ref, dl_ref, nw_ref,
             dz_ref, dxs_ref, db_ref, dc_ref, ddt_ref, dbias_ref, dalog_ref, dd_ref, dnw_ref,
             ds_ref, cst_ref, racc_ref):
        @pl.when(pl.program_id(1) == 0)
        def _():
            ds_ref[...] = jnp.zeros_like(ds_ref)
            dbias_ref[...] = jnp.zeros_like(dbias_ref)
            dalog_ref[...] = jnp.zeros_like(dalog_ref)
            dd_ref[...] = jnp.zeros_like(dd_ref)
            dnw_ref[...] = jnp.zeros_like(dnw_ref)

        q = _ssd_common(dt_ref, bias_ref, alog_ref, b_ref, c_ref, cst_ref, SSD_G * heads)
        lane, row, lo = q["lane"], q["row"], q["lo"]
        lane1 = lax.broadcasted_iota(jnp.int32, (1, LANES), 1)
        head0 = pl.program_id(0) * heads
        mine = (lane >= head0) & (lane < head0 + heads)

        yv, zv, dynv, nwv = y_ref[...], z_ref[...], dyn_ref[...], nw_ref[...]
        sig = _sigmoid(zv)
        sz = zv * sig
        yg = yv * sz
        rstd = lax.rsqrt(jnp.mean(yg * yg, axis=-1, keepdims=True) + NORM_EPS)
        yhat = yg * rstd
        dnw_ref[...] += jnp.sum(dynv * yhat, axis=0, keepdims=True)
        dyh = dynv * nwv
        dyg = rstd * (dyh - yhat * jnp.mean(dyh * yhat, axis=-1, keepdims=True))
        dz_ref[...] = (dyg * yv * (sig * (1.0 + zv * (1.0 - sig)))).astype(BF16)
        dy_all = dyg * sz

        dg = jnp.zeros((SSD_CHUNK, SSD_CHUNK), F32)
        dc_acc = jnp.zeros((SSD_CHUNK, SSD_N), F32)
        db_acc = jnp.zeros((SSD_CHUNK, SSD_N), F32)
        dcs_mat = jnp.zeros((SSD_CHUNK, LANES), F32)
        ddt_mat = jnp.zeros((SSD_CHUNK, LANES), F32)
        dd_row = jnp.zeros((1, LANES), F32)
        racc_ref[...] = jnp.zeros_like(racc_ref)
        is_last = row == SSD_CHUNK - 1

        for j in range(n_pair):
            sl = slice(j * LANES, (j + 1) * LANES)
            ha, hb = head0 + 2 * j, head0 + 2 * j + 1
            p = _pair_terms(q, cst_ref, ha)
            xs_p = xs_ref[:, sl]
            dyp = dy_all[:, sl]
            xp = xs_p * p["dt_b"]
            xb = xp.astype(BF16)
            s_p = prev_ref[:, sl]
            s_pb = s_p.astype(BF16)
            dsn = ds_ref[:, sl]
            dsnb = dsn.astype(BF16)
            m_f = [q["scores"] * p["lm"][0], q["scores"] * p["lm"][1]]

            t0 = dyp * xs_p
            dd_row = dd_row + jnp.where(lane1 == ha, _sum_all(jnp.where(lo, t0, 0.0)), 0.0) \
                + jnp.where(lane1 == hb, _sum_all(jnp.where(lo, 0.0, t0)), 0.0)
            dxs_p = dl_ref[:, sl] * dyp

            yo = _dot(q["cb"], s_pb) * p["e_b"]
            dcs_b = (dyp * p["e_b"]).astype(BF16)
            dc_acc = dc_acc + _dot(dcs_b, s_pb, "nt")
            ds_yo = _dot(q["cb"], dcs_b, "tn")
            t1 = dyp * yo
            dcs_cols = [jnp.sum(jnp.where(lo, t1, 0.0), axis=1, keepdims=True),
                        jnp.sum(jnp.where(lo, 0.0, t1), axis=1, keepdims=True)]

            t2 = dsn * s_p
            dlast = [p["explast"][0] * _sum_all(jnp.where(lo, t2, 0.0)),
                     p["explast"][1] * _sum_all(jnp.where(lo, 0.0, t2))]
            ds_ref[:, sl] = dsn * p["explast_b"] + ds_yo
            w = _dot(q["bb"], dsnb)
            db_acc = db_acc + _dot((xp * p["dec_b"]).astype(BF16), dsnb, "nt")
            dxp = w * p["dec_b"]
            t3 = w * xp
            e = [jnp.sum(jnp.where(lo, t3, 0.0), axis=1, keepdims=True) * p["dec_cols"][0],
                 jnp.sum(jnp.where(lo, 0.0, t3), axis=1, keepdims=True) * p["dec_cols"][1]]
            for i in range(2):
                dlast[i] = dlast[i] + jnp.sum(e[i], axis=0, keepdims=True)
                dcs_cols[i] = dcs_cols[i] - e[i]

            dyb = dyp.astype(BF16)
            dy_h = [jnp.where(lo, dyp, 0.0).astype(BF16), jnp.where(lo, 0.0, dyp).astype(BF16)]
            dms = [_dot(dy_h[0], xb, "nt"), _dot(dy_h[1], xb, "nt")]
            dxp = dxp + jnp.where(lo, _dot(m_f[0].astype(BF16), dyb, "tn"), _dot(m_f[1].astype(BF16), dyb, "tn"))
            for i, h in enumerate((ha, hb)):
                dg = dg + dms[i] * p["lm"][i]
                qm = dms[i] * m_f[i]
                dcs_cols[i] = dcs_cols[i] + jnp.sum(qm, axis=1, keepdims=True)
                racc_ref[pl.ds(h, 1), :] = jnp.sum(qm, axis=0, keepdims=True)

            dxs_ref[:, sl] = dxs_p + dxp * p["dt_b"]
            t4 = dxp * xs_p
            ddt_cols = [jnp.sum(jnp.where(lo, t4, 0.0), axis=1, keepdims=True),
                        jnp.sum(jnp.where(lo, 0.0, t4), axis=1, keepdims=True)]
            for i, h in enumerate((ha, hb)):
                sel = lane == h
                dcs_mat = dcs_mat + jnp.where(sel, dcs_cols[i], 0.0) + jnp.where(sel & is_last, dlast[i], 0.0)
                ddt_mat = ddt_mat + jnp.where(sel, ddt_cols[i], 0.0)

        dcs_mat = dcs_mat - racc_ref[...].T
        tri_t = (row <= lane).astype(BF16)
        da = _dot_exact01(tri_t, dcs_mat)
        ddt = ddt_mat + da * q["a_row"]
        dalog_ref[...] += jnp.sum(jnp.where(mine, da * q["dt"], 0.0), axis=0, keepdims=True) * q["a_row"]
        draw = jnp.where(mine, ddt * _sigmoid(q["raw"]), 0.0)
        ddt_ref[...] = draw
        dbias_ref[...] += jnp.sum(draw, axis=0, keepdims=True)
        dd_ref[...] += dd_row
        dgb = dg.astype(BF16)
        dc_ref[...] = dc_acc + _dot(dgb, q["bb"])
        db_ref[...] = db_acc + _dot(dgb, q["cb"], "tn")

    rev = lambda c: nc - 1 - c
    grp = lambda width: pl.BlockSpec((None, 1, width), lambda g, c: (g, 0, 0))
    blk = lambda width, off: pl.BlockSpec((SSD_CHUNK, width), lambda g, c: (rev(c), off + g))
    head_vec = pl.BlockSpec((1, LANES), lambda g, c: (0, 0))
    return _pcall(
        body, name=name, grid=(SSD_G, nc),
        out_shape=[jax.ShapeDtypeStruct((L, d_inner), BF16), jax.ShapeDtypeStruct((L, d_inner), F32),
                   jax.ShapeDtypeStruct((L, SSD_G * SSD_N), F32), jax.ShapeDtypeStruct((L, SSD_G * SSD_N), F32),
                   jax.ShapeDtypeStruct((SSD_G, L, LANES), F32),
                   jax.ShapeDtypeStruct((SSD_G, 1, LANES), F32), jax.ShapeDtypeStruct((SSD_G, 1, LANES), F32),
                   jax.ShapeDtypeStruct((SSD_G, 1, LANES), F32), jax.ShapeDtypeStruct((SSD_G, 1, gw), F32)],
        in_specs=[blk(gw, 0), blk(gw, 0), blk(gw, 0), blk(gw, 0), blk(SSD_N, bc0), blk(SSD_N, bc0 + SSD_G),
                  pl.BlockSpec((SSD_CHUNK, LANES), lambda g, c: (rev(c), dt0)),
                  pl.BlockSpec((None, None, SSD_N, gw), lambda g, c: (rev(c), g, 0, 0)),
                  head_vec, head_vec, grp(gw), grp(gw)],
        out_specs=[blk(gw, 0), blk(gw, 0), blk(SSD_N, 0), blk(SSD_N, 0),
                   pl.BlockSpec((None, SSD_CHUNK, LANES), lambda g, c: (g, rev(c), 0)),
                   grp(LANES), grp(LANES), grp(LANES), grp(gw)],
        scratch_shapes=[pltpu.VMEM((SSD_N, gw), F32), pltpu.VMEM((SSD_CHUNK, LANES), F32),
                        pltpu.VMEM((SSD_CHUNK, LANES), F32)],
        compiler_params=_cparams(("parallel", "arbitrary")))(
            dyn, y, zx, xc, xc, xc, zx, prev, bias_p, alog_p, d_lane, nw)


def _cond_mod(c_pad, ada_w, ada_b_loc, name):
    depth, D, n = ada_w.shape
    rows = c_pad.shape[0]

    def body(c_ref, w_ref, b_ref, mod_ref, cond_ref):
        cv = c_ref[...]
        cond = cv * _sigmoid(cv)
        cond_ref[...] = cond
        mod_ref[...] = _dot(cond.astype(BF16), w_ref[...].astype(BF16)) + b_ref[...]

    return _pcall(
        body, name=name, grid=(depth,),
        out_shape=[jax.ShapeDtypeStruct((depth, rows, n), F32), jax.ShapeDtypeStruct((rows, D), F32)],
        in_specs=[pl.BlockSpec((rows, D), lambda i: (0, 0)),
                  pl.BlockSpec((None, D, n), lambda i: (i, 0, 0)),
                  pl.BlockSpec((None, 1, n), lambda i: (i, 0, 0))],
        out_specs=[pl.BlockSpec((None, rows, n), lambda i: (i, 0, 0)),
                   pl.BlockSpec((rows, D), lambda i: (0, 0))],
        compiler_params=_cparams(("arbitrary",)))(c_pad, ada_w, ada_b_loc)


def _adamw_math(g, w, m, v):
    m_new = ADAM_B1 * m + (1.0 - ADAM_B1) * g
    v_new = ADAM_B2 * v + (1.0 - ADAM_B2) * (g * g)
    m_hat = m_new / (1.0 - ADAM_B1 ** ADAM_STEP)
    v_hat = v_new / (1.0 - ADAM_B2 ** ADAM_STEP)
    delta = -ADAM_LR * (m_hat / (jnp.sqrt(v_hat) + ADAM_EPS) + ADAM_WD * w)
    return delta, m_new, v_new


def _adamw_sum(parts, w, m, v, layer, name, prev=None, tr=256, window_off=None):
    depth, R, C = w.shape
    tr = _tile(R, tr)
    win = parts.shape[2]
    scratch = [] if window_off is None else [pltpu.VMEM((tr, win), F32)]

    def body(p_ref, w_ref, m_ref, v_ref, *rest):
        g_ref, d_ref, mo_ref, vo_ref = rest[-4 - len(scratch):len(rest) - len(scratch)]
        g = p_ref[0].astype(F32)
        for k in range(1, N_DEV):
            g = g + p_ref[k].astype(F32)
        if window_off is not None:
            me = _my_index()
            off = 0
            for k in range(N_DEV):
                off = jnp.where(me == k, window_off[k], off)
            src = lax.broadcasted_iota(jnp.int32, (win, win), 0)
            dst = lax.broadcasted_iota(jnp.int32, (win, win), 1)
            shift = ((src == dst + off) & (dst < C)).astype(BF16)
            hi, mid, lo = _split3(g)
            rest[-1][...] = _dot(hi, shift) + _dot(mid, shift) + _dot(lo, shift)
            g = rest[-1][:, 0:C]
        d, mn, vn = _adamw_math(g, w_ref[...], m_ref[...], v_ref[...])
        g_ref[...] = g
        d_ref[...] = d
        mo_ref[...] = mn
        vo_ref[...] = vn

    blk = pl.BlockSpec((None, tr, C), lambda i: (layer, i, 0))
    prev = list(prev) if prev is not None else []
    return _pcall(
        body, name=name, grid=(R // tr,),
        out_shape=[jax.ShapeDtypeStruct((depth, R, C), F32)] * 4,
        in_specs=[pl.BlockSpec((N_DEV, tr, win), lambda i: (0, i, 0)), blk, blk, blk]
        + [pl.BlockSpec(memory_space=pl.ANY)] * len(prev),
        out_specs=[blk] * 4, input_output_aliases={4 + k: k for k in range(len(prev))},
        scratch_shapes=scratch,
        compiler_params=_cparams(("parallel",)))(parts, w, m, v, *prev)


def _ada_adamw(cond_pad, dmod_pad, w, m, v, name, tr=256):
    depth, D, n = w.shape
    rows = cond_pad.shape[0]
    tr = _tile(D, tr)

    def body(c_ref, dm_ref, w_ref, m_ref, v_ref, g_ref, d_ref, mo_ref, vo_ref):
        g = _dot(c_ref[...].astype(BF16), dm_ref[...].astype(BF16), "tn")
        d, mn, vn = _adamw_math(g, w_ref[...], m_ref[...], v_ref[...])
        g_ref[...] = g
        d_ref[...] = d
        mo_ref[...] = mn
        vo_ref[...] = vn

    blk = pl.BlockSpec((None, tr, n), lambda i, r: (i, r, 0))
    return _pcall(
        body, name=name, grid=(depth, D // tr),
        out_shape=[jax.ShapeDtypeStruct((depth, D, n), F32)] * 4,
        in_specs=[pl.BlockSpec((rows, tr), lambda i, r: (0, r)),
                  pl.BlockSpec((None, rows, n), lambda i, r: (i, 0, 0)), blk, blk, blk],
        out_specs=[blk] * 4, compiler_params=_cparams(("parallel", "parallel")))(cond_pad, dmod_pad, w, m, v)


def kernel(x, c, ada_w, ada_b, mix_norm_w, mlp_norm_w, mlp_up, mlp_down, ssd_in_w, ssd_conv_w, ssd_conv_b, ssd_dt_bias, ssd_A_log, ssd_D, ssd_norm_w, ssd_out_w, sc_in_w, sc_conv_w, sc_out_w, final_norm_w, loss_target, m_ada_w, m_ada_b, m_mix_norm_w, m_mlp_norm_w, m_mlp_up, m_mlp_down, m_ssd_in_w, m_ssd_conv_w, m_ssd_conv_b, m_ssd_dt_bias, m_ssd_A_log, m_ssd_D, m_ssd_norm_w, m_ssd_out_w, m_sc_in_w, m_sc_conv_w, m_sc_out_w, m_final_norm_w, v_ada_w, v_ada_b, v_mix_norm_w, v_mlp_norm_w, v_mlp_up, v_mlp_down, v_ssd_in_w, v_ssd_conv_w, v_ssd_conv_b, v_ssd_dt_bias, v_ssd_A_log, v_ssd_D, v_ssd_norm_w, v_ssd_out_w, v_sc_in_w, v_sc_conv_w, v_sc_out_w, v_final_norm_w):
    weights = dict(ada_w=ada_w, ada_b=ada_b, mix_norm_w=mix_norm_w, mlp_norm_w=mlp_norm_w, mlp_up=mlp_up,
                   mlp_down=mlp_down, ssd_in_w=ssd_in_w, ssd_conv_w=ssd_conv_w, ssd_conv_b=ssd_conv_b,
                   ssd_dt_bias=ssd_dt_bias, ssd_A_log=ssd_A_log, ssd_D=ssd_D, ssd_norm_w=ssd_norm_w,
                   ssd_out_w=ssd_out_w, sc_in_w=sc_in_w, sc_conv_w=sc_conv_w, sc_out_w=sc_out_w,
                   final_norm_w=final_norm_w)
    moms = dict(ada_w=m_ada_w, ada_b=m_ada_b, mix_norm_w=m_mix_norm_w, mlp_norm_w=m_mlp_norm_w, mlp_up=m_mlp_up,
                mlp_down=m_mlp_down, ssd_in_w=m_ssd_in_w, ssd_conv_w=m_ssd_conv_w, ssd_conv_b=m_ssd_conv_b,
                ssd_dt_bias=m_ssd_dt_bias, ssd_A_log=m_ssd_A_log, ssd_D=m_ssd_D, ssd_norm_w=m_ssd_norm_w,
                ssd_out_w=m_ssd_out_w, sc_in_w=m_sc_in_w, sc_conv_w=m_sc_conv_w, sc_out_w=m_sc_out_w,
                final_norm_w=m_final_norm_w)
    vars_ = dict(ada_w=v_ada_w, ada_b=v_ada_b, mix_norm_w=v_mix_norm_w, mlp_norm_w=v_mlp_norm_w, mlp_up=v_mlp_up,
                 mlp_down=v_mlp_down, ssd_in_w=v_ssd_in_w, ssd_conv_w=v_ssd_conv_w, ssd_conv_b=v_ssd_conv_b,
                 ssd_dt_bias=v_ssd_dt_bias, ssd_A_log=v_ssd_A_log, ssd_D=v_ssd_D, ssd_norm_w=v_ssd_norm_w,
                 ssd_out_w=v_ssd_out_w, sc_in_w=v_sc_in_w, sc_conv_w=v_sc_conv_w, sc_out_w=v_sc_out_w,
                 final_norm_w=v_final_norm_w)
    names = list(weights)

    L, D = x.shape[1], x.shape[2]
    d_inner = 2 * D
    n_heads = d_inner // SSD_P
    hpg = n_heads // SSD_G
    gw = d_inner // SSD_G
    conv_dim = d_inner + 2 * SSD_G * SSD_N
    zx_dim = d_inner + conv_dim
    zx_pad = -(-(zx_dim + LANES) // 512) * 512
    in_ws = ssd_in_w.shape[2]
    in_base, in_off, in_win = _window_geometry(in_ws)
    me = _my_index()
    x0 = x[0]
    tgt = loss_target[0]

    n_mod = ada_w.shape[2]
    (c_all,) = _exchange([c], "gather_c", gather=True)
    c_pad = jnp.pad(c_all.reshape(N_DEV, D), ((0, 16 - N_DEV), (0, 0)))
    ada_b_loc = lax.dynamic_slice_in_dim(ada_b, me * n_mod, n_mod, axis=1).reshape(2, 1, n_mod)
    mod_blk, cond_pad = _cond_mod(c_pad, ada_w, ada_b_loc, "cond_mod")

    gather_order = ["mod", "ssd_conv_w", "sc_conv_w", "ssd_in_w", "ssd_out_w", "up0", "down0", "sc_in_w",
                    "sc_out_w", "up1", "down1"]
    gather_src = dict(mod=mod_blk, ssd_conv_w=ssd_conv_w[0], sc_conv_w=sc_conv_w[0],
                      ssd_in_w=_to_window(ssd_in_w[0].astype(BF16), in_off, in_win, me),
                      ssd_out_w=ssd_out_w[0].astype(BF16),
                      up0=mlp_up[0].astype(BF16), down0=mlp_down[0].astype(BF16),
                      sc_in_w=sc_in_w[0].astype(BF16), sc_out_w=sc_out_w[0].astype(BF16),
                      up1=mlp_up[1].astype(BF16), down1=mlp_down[1].astype(BF16))
    handles, gather_token = _xfer_start([gather_src[k] for k in gather_order], "gather_start", gather=True,
                                        via_sibling=tuple(range(3, len(gather_order))))
    gather_handle = dict(zip(gather_order, handles))

    def gathered(keys, after, forward):
        tag = "_".join(keys)
        lands = _xfer_wait([gather_handle[k] for k in keys], after, f"gather_wait_{tag}", gather=True)
        return _sibling_forward(lands, f"gather_forward_{tag}") if forward else lands

    (mod_all,) = gathered(["mod"], gather_token, False)
    mod_mine = lax.dynamic_index_in_dim(mod_all, me, axis=2, keepdims=False)
    mod_mine = jnp.transpose(mod_mine, (1, 0, 2)).reshape(2, 6, 1, D)
    sh_m, sc_m, g_m, sh_f, sc_f, g_f = [[mod_mine[i, k] for i in range(2)] for k in range(6)]

    vec = lambda a: a.reshape(1, -1)
    grads = {}
    small = {}

    _, h0 = _norm_mod_fwd(x0, None, None, vec(mix_norm_w[0]), sc_m[0], sh_m[0], "l0_mix_norm")
    cw_all, scw_all = gathered(["ssd_conv_w", "sc_conv_w"], h0, False)
    (ssd_in_g,) = gathered(["ssd_in_w"], h0, True)
    w_in_all = _windows_to_columns(ssd_in_g, in_base, in_win, zx_pad, "ssd_in_w_columns")
    (zx,) = _mm_nn(h0, w_in_all, F32, "ssd_in_proj", tm=2048, tn=512)
    conv_b0 = vec(ssd_conv_b[0])
    conv_w_full = jnp.transpose(cw_all, (1, 0, 2)).reshape(SSD_K, conv_dim)
    sc_conv_full = jnp.transpose(scw_all, (1, 0, 2)).reshape(SC_K, D)
    xc = _ssd_conv_fwd(zx, conv_w_full, conv_b0, d_inner, conv_dim, "ssd_conv")
    bias_p = jnp.pad(ssd_dt_bias[0], (0, LANES - n_heads)).reshape(1, LANES)
    alog_p = jnp.pad(ssd_A_log[0], (0, LANES - n_heads)).reshape(1, LANES)
    d_lane = jnp.repeat(ssd_D[0], SSD_P).reshape(SSD_G, 1, gw)
    nw_g = ssd_norm_w[0].reshape(SSD_G, 1, gw)
    y_ssd, yn, prev = _ssd_fwd(zx, xc, bias_p, alog_p, d_lane, nw_g, d_inner, "ssd_scan")
    ups, downs = [None, None], [None, None]
    ssd_out_g, ups[0], down0_g = gathered(["ssd_out_w", "up0", "down0"], yn, True)
    w_ssd_out, downs[0] = ssd_out_g.reshape(-1, D), down0_g.reshape(-1, D)
    (mix0,) = _mm_nn(yn, w_ssd_out, F32, "ssd_out_proj")
    x1, h1 = _norm_mod_fwd(x0, mix0, g_m[0], vec(mlp_norm_w[0]), sc_f[0], sh_f[0], "l0_mlp_norm")
    u0, s0 = _mm_nn_blocked(h1, ups[0], "l0_mlp_up", _ep_relu2, [BF16, BF16])
    (d0,) = _mm_nn(s0, downs[0], F32, "l0_mlp_down")
    x2, h2 = _norm_mod_fwd(x1, d0, g_f[0], vec(mix_norm_w[1]), sc_m[1], sh_m[1], "l1_mix_norm")
    sc_in_g, sc_out_g, ups[1], down1_g = gathered(["sc_in_w", "sc_out_w", "up1", "down1"], h2, True)
    w_sc_out, downs[1] = sc_out_g.reshape(-1, D), down1_g.reshape(-1, D)
    (proj,) = _mm_nn_blocked(h2, sc_in_g, "sc_in_proj", _ep_store(F32), [F32])
    yc = _sc_conv_fwd(proj, sc_conv_full, "sc_conv")
    (mix1,) = _mm_nn(yc, w_sc_out, F32, "sc_out_proj")
    x3, h3 = _norm_mod_fwd(x2, mix1, g_m[1], vec(mlp_norm_w[1]), sc_f[1], sh_f[1], "l1_mlp_norm")
    u1, s1 = _mm_nn_blocked(h3, ups[1], "l1_mlp_up", _ep_relu2, [BF16, BF16])
    (d1,) = _mm_nn(s1, downs[1], F32, "l1_mlp_down")

    dx, loss_lane, dfw = _final_loss(x3, d1, g_f[1], vec(final_norm_w), tgt, "final_loss")
    loss = lax.psum(0.5 * jnp.sum(loss_lane) / D, MESH_AXES)
    small["final_norm_w"] = dfw

    dmod = [[None] * 6 for _ in range(2)]
    big = {}

    def mlp_backward(i, dx_out, d_out, x_mid, h_in, u, s):
        dd, dg = _gate_bwd(dx_out, d_out, g_f[i], f"l{i}_mlp_gate_bwd")
        dmod[i][5] = dg
        du = _mm_nt(dd, downs[i], BF16, f"l{i}_mlp_down_bwd", epilogue=_ep_relu2_bwd, extra=(u,))
        gdown = _mm_tn(s, dd, BF16, f"l{i}_mlp_down_wgrad").reshape(N_DEV, -1, D)
        gup = _mm_tn_blocked(h_in, du, BF16, f"l{i}_mlp_up_wgrad")
        (h_down, h_up), token = _xfer_start([gdown, gup], f"l{i}_mlp_grads_start", gather=False)
        grad_handle[f"mlp_down{i}"], grad_handle[f"mlp_up{i}"] = h_down, h_up
        dh = _mm_nt_blocked(du, ups[i], F32, f"l{i}_mlp_up_bwd", after=(token,))
        dxm, dsh, dsc, dnw = _norm_mod_bwd(dh, x_mid, vec(mlp_norm_w[i]), sc_f[i], dx_out, f"l{i}_mlp_norm_bwd")
        dmod[i][3], dmod[i][4] = dsh, dsc
        return dxm, dnw

    grad_handle = {}
    dx3, dnw_mlp1 = mlp_backward(1, dx, d1, x3, h3, u1, s1)
    dyc, dg = _gate_bwd(dx3, mix1, g_m[1], "l1_mix_gate_bwd")
    dmod[1][2] = dg
    g_sc_out = _mm_tn(yc, dyc, BF16, "sc_out_wgrad").reshape(N_DEV, -1, D)
    dconv_out = _mm_nt(dyc, w_sc_out, F32, "sc_out_bwd")
    dbg, dcg, dxv, dscw = _sc_conv_bwd(proj, sc_conv_full, dconv_out, "sc_conv_bwd")
    dproj = jnp.concatenate([dbg, dcg, dxv], axis=1)
    g_sc_in = _mm_tn_blocked(h2, dproj, BF16, "sc_in_wgrad")
    (grad_handle["sc_out_w0"], grad_handle["sc_in_w0"]), token = _xfer_start(
        [g_sc_out, g_sc_in], "sc_grads_start", gather=False)
    dh2 = _mm_nt_blocked(dproj, sc_in_g, F32, "sc_in_bwd", after=(token,))
    dx2, dsh, dsc, dnw_mix1 = _norm_mod_bwd(dh2, x2, vec(mix_norm_w[1]), sc_m[1], dx3, "l1_mix_norm_bwd")
    dmod[1][0], dmod[1][1] = dsh, dsc
    dx1, dnw_mlp0 = mlp_backward(0, dx2, d0, x1, h1, u0, s0)
    dyo, dg = _gate_bwd(dx1, mix0, g_m[0], "l0_mix_gate_bwd")
    dmod[0][2] = dg
    g_ssd_out = _mm_tn(yn, dyo, BF16, "ssd_out_wgrad").reshape(N_DEV, -1, D)
    (grad_handle["ssd_out_w0"],), token = _xfer_start([g_ssd_out], "ssd_out_grad_start", gather=False)
    dyn = _mm_nt(dyo, w_ssd_out, F32, "ssd_out_bwd", after=(token,))
    dz, dxs, db_, dc_, ddt, dbias, dalog, dd_, dnw_ssd = _ssd_bwd(
        dyn, y_ssd, zx, xc, prev, bias_p, alog_p, d_lane, nw_g, d_inner, "ssd_scan_bwd")
    dxc = jnp.concatenate([dxs, db_, dc_], axis=1)
    dxbc, dcw, dcb = _ssd_conv_bwd(zx, conv_w_full, conv_b0, dxc, d_inner, "ssd_conv_bwd")
    dzx = jnp.concatenate([dz, dxbc, jnp.sum(ddt, axis=0).astype(BF16),
                           jnp.zeros((L, zx_pad - zx_dim - LANES), BF16)], axis=1)
    g_in_all = _mm_tn(h0, dzx, BF16, "ssd_in_wgrad", tn=512, tk=2048)
    g_ssd_in = jnp.stack([g_in_all[:, b:b + in_win] for b in in_base], axis=0)
    (grad_handle["ssd_in_w0"],), token = _xfer_start([g_ssd_in], "ssd_in_grad_start", gather=False)
    dh0 = _mm_nt(dzx, w_in_all, F32, "ssd_in_bwd", tk=dzx.shape[1] // 2, after=(token,))
    grad_x, dsh, dsc, dnw_mix0 = _norm_mod_bwd(dh0, x0, vec(mix_norm_w[0]), sc_m[0], dx1, "l0_mix_norm_bwd")
    dmod[0][0], dmod[0][1] = dsh, dsc

    small["mix_norm_w"] = jnp.concatenate([dnw_mix0, dnw_mix1], axis=0)
    small["mlp_norm_w"] = jnp.concatenate([dnw_mlp0, dnw_mlp1], axis=0)
    small["ssd_conv_w"] = dcw
    small["ssd_conv_b"] = dcb
    small["ssd_dt_bias"] = jnp.sum(dbias, axis=0)[:, :n_heads]
    small["ssd_A_log"] = jnp.sum(dalog, axis=0)[:, :n_heads]
    small["ssd_D"] = jnp.sum(dd_, axis=0)[:, :n_heads]
    small["ssd_norm_w"] = dnw_ssd
    small["sc_conv_w"] = dscw
    small["dmod"] = jnp.concatenate([jnp.concatenate(dmod[i], axis=1) for i in range(2)], axis=0)

    small_order = ["dmod", "mix_norm_w", "mlp_norm_w", "ssd_conv_w", "ssd_conv_b", "ssd_dt_bias", "ssd_A_log",
                   "ssd_D", "ssd_norm_w", "sc_conv_w", "final_norm_w"]
    flat = jnp.concatenate([small[k].reshape(-1) for k in small_order])
    n_small = flat.shape[0]
    n_small_pad = -(-n_small // 1024) * 1024
    flat = jnp.pad(flat, (0, n_small_pad - n_small)).reshape(n_small_pad // LANES, LANES)
    (small_handle,), small_token = _xfer_start([flat], "small_grads_start", gather=True)
    offs, o = {}, 0
    for k in small_order:
        offs[k] = (o, small[k].size, small[k].shape)
        o += small[k].size

    def small_parts(k):
        o, n, shape = offs[k]
        return small_all[:, o:o + n].reshape((N_DEV,) + shape)

    out_g, out_d, out_m, out_v = {}, {}, {}, {}

    layer_res = {}

    def big_update(name, i, after):
        (parts,) = _xfer_wait([grad_handle[f"{name}{i}"]], after, f"grads_wait_{name}_{i}", gather=False)
        res = _adamw_sum(parts, weights[name], moms[name], vars_[name], i, f"adamw_{name}_{i}",
                         prev=layer_res.get(name), window_off=in_off if name == "ssd_in_w" else None)
        layer_res[name] = res
        return res[1]

    chain = small_token
    for name, i in [("mlp_down", 1), ("mlp_up", 1), ("sc_out_w", 0), ("sc_in_w", 0), ("mlp_down", 0),
                    ("mlp_up", 0), ("ssd_out_w", 0)]:
        chain = big_update(name, i, chain)
    (small_all,) = _xfer_wait([small_handle], chain, "small_grads_wait", gather=True)
    small_all = small_all.reshape(N_DEV, n_small_pad)

    dmod_all = small_parts("dmod")
    dmod_loc = lax.dynamic_slice_in_dim(dmod_all, me * n_mod, n_mod, axis=2)
    dmod_pad = jnp.pad(jnp.transpose(dmod_loc, (1, 0, 2)), ((0, 0), (0, 16 - N_DEV), (0, 0)))
    out_g["ada_w"], out_d["ada_w"], out_m["ada_w"], out_v["ada_w"] = _ada_adamw(
        cond_pad, dmod_pad, ada_w, m_ada_w, v_ada_w, "adamw_ada_w")

    pieces = []
    pieces.append(("ada_b", dmod_all.reshape(N_DEV, -1)))
    for k in ["mix_norm_w", "mlp_norm_w", "ssd_conv_b", "ssd_dt_bias", "ssd_A_log", "ssd_D", "ssd_norm_w",
              "final_norm_w"]:
        pieces.append((k, small_parts(k).reshape(N_DEV, -1)))
    n_cw = ssd_conv_w.shape[2]
    pieces.append(("ssd_conv_w", lax.dynamic_slice_in_dim(small_parts("ssd_conv_w"), me * n_cw, n_cw, axis=2)
                   .reshape(N_DEV, -1)))
    n_scw = sc_conv_w.shape[2]
    pieces.append(("sc_conv_w", lax.dynamic_slice_in_dim(small_parts("sc_conv_w"), me * n_scw, n_scw, axis=2)
                   .reshape(N_DEV, -1)))
    n_tot = sum(p.shape[1] for _, p in pieces)
    n_tot_pad = -(-n_tot // 1024) * 1024

    def pack(arrs, lead=()):
        f = jnp.concatenate(arrs, axis=-1)
        f = jnp.pad(f, [(0, 0)] * len(lead) + [(0, n_tot_pad - n_tot)])
        return f.reshape(lead + (n_tot_pad // LANES, LANES))

    parts_flat = pack([p for _, p in pieces], lead=(N_DEV,))
    w_flat = pack([weights[k].reshape(-1) for k, _ in pieces])
    m_flat = pack([moms[k].reshape(-1) for k, _ in pieces])
    v_flat = pack([vars_[k].reshape(-1) for k, _ in pieces])
    res = _adamw_sum(parts_flat, w_flat[None], m_flat[None], v_flat[None], 0, "adamw_small",
                     tr=n_tot_pad // LANES)
    o = 0
    for k, p in pieces:
        n = p.shape[1]
        for r, dst in zip(res, (out_g, out_d, out_m, out_v)):
            dst[k] = r.reshape(-1)[o:o + n].reshape(weights[k].shape)
        o += n
    big_update("ssd_in_w", 0, res[1])
    for name, res4 in layer_res.items():
        for r, dst in zip(res4, (out_g, out_d, out_m, out_v)):
            dst[name] = r

    return (loss, grad_x[None], *[out_g[k] for k in names], *[out_d[k] for k in names],
            *[out_m[k] for k in names], *[out_v[k] for k in names])
```

```python
import functools

import jax
import jax.numpy as jnp
from jax import lax
from jax.experimental import pallas as pl
from jax.experimental.pallas import tpu as pltpu

F32 = jnp.float32
BF16 = jnp.bfloat16
N_DEV = 8
MESH_AXES = ("x", "y", "c")
MESH = pl.DeviceIdType.MESH

NORM_EPS = 1e-5
SSD_G = 4
SSD_P = 64
SSD_N = 128
SSD_CHUNK = 128
SSD_K = 4
SC_K = 3
LANES = 128

ADAM_LR = 0.001
ADAM_B1 = 0.9
ADAM_B2 = 0.999
ADAM_EPS = 1e-08
ADAM_WD = 0.01
ADAM_STEP = 10

VMEM_LIMIT = 56 * 1024 * 1024


def _pcall(body, **kw):
    return pl.pallas_call(body, **kw)


def _cparams(sem=None):
    if sem is None:
        return pltpu.CompilerParams(vmem_limit_bytes=VMEM_LIMIT)
    return pltpu.CompilerParams(dimension_semantics=sem, vmem_limit_bytes=VMEM_LIMIT)


def _my_index():
    return 4 * lax.axis_index("x") + 2 * lax.axis_index("y") + lax.axis_index("c")


_PEER_MASKS = [(0, 0, 1), (0, 1, 0), (0, 1, 1), (1, 0, 0), (1, 0, 1), (1, 1, 0), (1, 1, 1)]


def _peers():
    x, y, c = lax.axis_index("x"), lax.axis_index("y"), lax.axis_index("c")
    out = []
    for mx, my, mc in _PEER_MASKS:
        px = (1 - x) if mx else x
        py = (1 - y) if my else y
        pc = (1 - c) if mc else c
        out.append(((px, py, pc), 4 * px + 2 * py + pc))
    return out


def _exchange(arrs, name, gather):
    n = len(arrs)
    n_peer = N_DEV - 1

    def body(*refs):
        ins, outs = refs[:n], refs[n:2 * n]
        send_sems, recv_sems, local_sems = refs[2 * n:]
        me = _my_index()
        peers = _peers()
        started = []
        for a in range(n):
            src_own = ins[a] if gather else ins[a].at[me]
            own = pltpu.make_async_copy(src_own, outs[a].at[me], local_sems.at[a])
            own.start()
            started.append(own)
        sends = []
        for a in range(n):
            for k, (peer, pidx) in enumerate(peers):
                src = ins[a] if gather else ins[a].at[pidx]
                cp = pltpu.make_async_remote_copy(
                    src_ref=src, dst_ref=outs[a].at[me],
                    send_sem=send_sems.at[a * n_peer + k], recv_sem=recv_sems.at[a * n_peer + k],
                    device_id=peer, device_id_type=MESH)
                cp.start()
                sends.append(cp)
        for a in range(n):
            for k, (peer, pidx) in enumerate(peers):
                src = ins[a] if gather else ins[a].at[pidx]
                pltpu.make_async_remote_copy(
                    src_ref=src, dst_ref=outs[a].at[pidx],
                    send_sem=send_sems.at[a * n_peer + k], recv_sem=recv_sems.at[a * n_peer + k],
                    device_id=peer, device_id_type=MESH).wait_recv()
        for cp in sends:
            cp.wait_send()
        for own in started:
            own.wait()

    if gather:
        out_shape = [jax.ShapeDtypeStruct((N_DEV,) + a.shape, a.dtype) for a in arrs]
    else:
        out_shape = [jax.ShapeDtypeStruct(a.shape, a.dtype) for a in arrs]
    any_spec = pl.BlockSpec(memory_space=pl.ANY)
    outs = _pcall(
        body, name=name, out_shape=out_shape,
        in_specs=[any_spec] * n, out_specs=[any_spec] * n,
        scratch_shapes=[pltpu.SemaphoreType.DMA((n * n_peer,)), pltpu.SemaphoreType.DMA((n * n_peer,)),
                        pltpu.SemaphoreType.DMA((n,))],
        compiler_params=pltpu.CompilerParams(has_side_effects=True),
    )(*arrs)
    return list(outs)


_HBM = pl.BlockSpec(memory_space=pltpu.HBM)
_SEM = pl.BlockSpec(memory_space=pltpu.SEMAPHORE)
_DATAFLOW = pltpu.SideEffectType.DATAFLOW_SIDE_EFFECTING


_ALL_PEERS = tuple(range(N_DEV - 1))
_SAME_CORE_PEERS = (0, 1, 3, 5)
_OTHER_CHIPS = (1, 3, 5)


def _xfer_start(arrs, name, gather, via_sibling=()):
    n = len(arrs)
    n_peer = N_DEV - 1
    peer_ks = [_SAME_CORE_PEERS if a in via_sibling else _ALL_PEERS for a in range(n)]

    def body(*refs):
        ins, lands = refs[:n], refs[n:2 * n]
        sems = refs[2 * n:5 * n]
        token = refs[-1]
        me = _my_index()
        peers = _peers()
        for a in range(n):
            send_sems, recv_sems, loc_sem = sems[3 * a:3 * a + 3]
            src_own = ins[a] if gather else ins[a].at[me]
            pltpu.make_async_copy(src_own, lands[a].at[me], loc_sem).start()
            for k in peer_ks[a]:
                peer, pidx = peers[k]
                src = ins[a] if gather else ins[a].at[pidx]
                pltpu.make_async_remote_copy(
                    src_ref=src, dst_ref=lands[a].at[me], send_sem=send_sems.at[k], recv_sem=recv_sems.at[k],
                    device_id=peer, device_id_type=MESH).start()
        token[...] = jnp.zeros_like(token)

    land_shapes = [((N_DEV,) + a.shape) if gather else a.shape for a in arrs]
    out_shape, out_specs = [], []
    for _ in range(n):
        out_shape += [pltpu.SemaphoreType.DMA((n_peer,)), pltpu.SemaphoreType.DMA((n_peer,)),
                      pltpu.SemaphoreType.DMA(())]
        out_specs += [_SEM, _SEM, _SEM]
    out_shape += [pltpu.HBM(a.shape, a.dtype) for a in arrs]
    out_shape += [pltpu.HBM(s, a.dtype) for s, a in zip(land_shapes, arrs)]
    out_shape += [jax.ShapeDtypeStruct((8, LANES), F32)]
    out_specs += [_HBM] * (2 * n) + [pl.BlockSpec(memory_space=pltpu.VMEM)]
    aliases = {}
    for a in range(n):
        aliases[a] = 3 * n + a
        aliases[n + a] = 4 * n + a
    operands = [pltpu.with_memory_space_constraint(a, pltpu.HBM) for a in arrs]
    operands += [pltpu.with_memory_space_constraint(lax.empty(s, a.dtype), pltpu.HBM)
                 for s, a in zip(land_shapes, arrs)]
    outs = _pcall(
        body, name=name, out_shape=tuple(out_shape), in_specs=[_HBM] * (2 * n), out_specs=tuple(out_specs),
        input_output_aliases=aliases,
        compiler_params=pltpu.CompilerParams(has_side_effects=_DATAFLOW),
    )(*operands)
    handles = []
    for a in range(n):
        handles.append((outs[3 * n + a], outs[4 * n + a], outs[3 * a], outs[3 * a + 1], outs[3 * a + 2],
                        peer_ks[a]))
    return handles, outs[-1]


def _xfer_wait(handles, after, name, gather):
    n = len(handles)
    peer_ks = [h[5] for h in handles]

    def body(*refs):
        me = _my_index()
        peers = _peers()
        for a in range(n):
            src_ref, land_ref, send_ref, recv_ref, loc_ref = refs[5 * a:5 * a + 5]
            src_own = src_ref if gather else src_ref.at[me]
            pltpu.make_async_copy(src_own, land_ref.at[me], loc_ref).wait()
            for k in peer_ks[a]:
                peer, pidx = peers[k]
                src = src_ref if gather else src_ref.at[pidx]
                cp = pltpu.make_async_remote_copy(
                    src_ref=src, dst_ref=land_ref.at[pidx], send_sem=send_ref.at[k], recv_sem=recv_ref.at[k],
                    device_id=peer, device_id_type=MESH)
                cp.wait_send()
                cp.wait_recv()

    operands, in_specs, out_shape, aliases = [], [], [], {}
    for a, h in enumerate(handles):
        operands += list(h[:5])
        in_specs += [_HBM, _HBM, _SEM, _SEM, _SEM]
        out_shape += [pltpu.HBM(h[0].shape, h[0].dtype), pltpu.HBM(h[1].shape, h[1].dtype)]
        aliases[5 * a] = 2 * a
        aliases[5 * a + 1] = 2 * a + 1
    outs = _pcall(
        body, name=name, out_shape=tuple(out_shape),
        in_specs=in_specs + [pl.BlockSpec(memory_space=pl.ANY)], out_specs=tuple([_HBM] * (2 * n)),
        input_output_aliases=aliases,
        compiler_params=pltpu.CompilerParams(has_side_effects=_DATAFLOW),
    )(*operands, after)
    return [outs[2 * a + 1] for a in range(n)]


def _sibling_forward(lands, name):
    n = len(lands)
    n_fwd = len(_OTHER_CHIPS)

    def body(*refs):
        ins, bufs = refs[:n], refs[n:2 * n]
        send_sems, recv_sems = refs[2 * n:]
        x, y, c = lax.axis_index("x"), lax.axis_index("y"), lax.axis_index("c")
        sibling = (x, y, 1 - c)
        peers = _peers()
        sends = []
        for a in range(n):
            for j, k in enumerate(_OTHER_CHIPS):
                slot = peers[k][1]
                cp = pltpu.make_async_remote_copy(
                    src_ref=ins[a].at[slot], dst_ref=bufs[a].at[slot],
                    send_sem=send_sems.at[a * n_fwd + j], recv_sem=recv_sems.at[a * n_fwd + j],
                    device_id=sibling, device_id_type=MESH)
                cp.start()
                sends.append(cp)
        for a in range(n):
            for j, k in enumerate(_OTHER_CHIPS):
                (px, py, pc), slot = peers[k]
                theirs = 4 * px + 2 * py + (1 - pc)
                pltpu.make_async_remote_copy(
                    src_ref=ins[a].at[slot], dst_ref=bufs[a].at[theirs],
                    send_sem=send_sems.at[a * n_fwd + j], recv_sem=recv_sems.at[a * n_fwd + j],
                    device_id=sibling, device_id_type=MESH).wait_recv()
        for cp in sends:
            cp.wait_send()

    any_spec = pl.BlockSpec(memory_space=pl.ANY)
    outs = _pcall(
        body, name=name, out_shape=[jax.ShapeDtypeStruct(a.shape, a.dtype) for a in lands],
        in_specs=[any_spec] * n, out_specs=[any_spec] * n,
        input_output_aliases={a: a for a in range(n)},
        scratch_shapes=[pltpu.SemaphoreType.DMA((n * n_fwd,)), pltpu.SemaphoreType.DMA((n * n_fwd,))],
        compiler_params=pltpu.CompilerParams(has_side_effects=True),
    )(*lands)
    return list(outs)


_DIMS = {"nn": (((1,), (0,)), ((), ())), "nt": (((1,), (1,)), ((), ())), "tn": (((0,), (0,)), ((), ()))}


def _dot(a, b, mode="nn"):
    return lax.dot_general(a, b, _DIMS[mode], preferred_element_type=F32)


def _mm(a, b, *, mode, grid, a_spec, b_spec, out_shape, out_specs, acc_shape, epilogue, name,
        extra=(), extra_specs=(), after=()):
    nk = grid[2]
    n_extra = len(extra)
    n_in = 2 + n_extra + len(after)

    def body_single(*refs):
        a_ref, b_ref = refs[0], refs[1]
        epilogue(_dot(a_ref[...], b_ref[...], mode), refs[2:2 + n_extra], refs[n_in:])

    def body_acc(*refs):
        a_ref, b_ref = refs[0], refs[1]
        ex = refs[2:2 + n_extra]
        outs = refs[n_in:-1]
        acc = refs[-1]
        k = pl.program_id(2)

        @pl.when(k == 0)
        def _():
            acc[...] = jnp.zeros_like(acc)

        acc[...] += _dot(a_ref[...], b_ref[...], mode)

        @pl.when(k == nk - 1)
        def _():
            epilogue(acc[...], ex, outs)

    return _pcall(
        body_single if nk == 1 else body_acc, name=name, grid=grid, out_shape=out_shape,
        in_specs=[a_spec, b_spec] + list(extra_specs) + [pl.BlockSpec(memory_space=pl.ANY)] * len(after),
        out_specs=out_specs,
        scratch_shapes=[] if nk == 1 else [pltpu.VMEM(acc_shape, F32)],
        compiler_params=_cparams(("parallel", "parallel", "arbitrary")),
    )(a, b, *extra, *after)


def _ep_store(dtype):
    def ep(acc, ex, outs):
        outs[0][...] = acc.astype(dtype)
    return ep


def _ep_relu2(acc, ex, outs):
    outs[0][...] = acc.astype(BF16)
    r = jnp.maximum(acc, 0.0)
    outs[1][...] = (r * r).astype(BF16)


def _ep_relu2_bwd(acc, ex, outs):
    u = ex[0][...].astype(F32)
    outs[0][...] = (acc * (2.0 * jnp.maximum(u, 0.0))).astype(BF16)


def _tile(n, want):
    t = min(n, want)
    while n % t:
        t //= 2
    return t


def _mm_nn(a, w, out_dtype, name, tm=1024, tn=1024, tk=1024, epilogue=None, out_dtypes=None):
    M, K = a.shape
    N = w.shape[1]
    tm, tn, tk = _tile(M, tm), _tile(N, tn), _tile(K, tk)
    out_dtypes = out_dtypes or [out_dtype]
    return _mm(a, w, mode="nn", grid=(M // tm, N // tn, K // tk),
               a_spec=pl.BlockSpec((tm, tk), lambda i, j, k: (i, k)),
               b_spec=pl.BlockSpec((tk, tn), lambda i, j, k: (k, j)),
               out_shape=[jax.ShapeDtypeStruct((M, N), d) for d in out_dtypes],
               out_specs=[pl.BlockSpec((tm, tn), lambda i, j, k: (i, j)) for _ in out_dtypes],
               acc_shape=(tm, tn), epilogue=epilogue or _ep_store(out_dtype), name=name)


def _mm_nn_blocked(a, wg, name, epilogue, out_dtypes, tm=2048):
    M, K = a.shape
    n = wg.shape[2]
    tm = _tile(M, tm)
    return _mm(a, wg, mode="nn", grid=(M // tm, N_DEV, 1),
               a_spec=pl.BlockSpec((tm, K), lambda i, j, k: (i, 0)),
               b_spec=pl.BlockSpec((None, K, n), lambda i, j, k: (j, 0, 0)),
               out_shape=[jax.ShapeDtypeStruct((M, N_DEV * n), d) for d in out_dtypes],
               out_specs=[pl.BlockSpec((tm, n), lambda i, j, k: (i, j)) for _ in out_dtypes],
               acc_shape=(tm, n), epilogue=epilogue, name=name)


def _mm_nt(a, w, out_dtype, name, tm=1024, tn=1024, tk=1024, epilogue=None, extra=(), extra_specs=(),
           after=()):
    M, K = a.shape
    N = w.shape[0]
    tm, tn, tk = _tile(M, tm), _tile(N, tn), _tile(K, tk)
    if extra and not extra_specs:
        extra_specs = [pl.BlockSpec((tm, tn), lambda i, j, k: (i, j)) for _ in extra]
    return _mm(a, w, mode="nt", grid=(M // tm, N // tn, K // tk),
               a_spec=pl.BlockSpec((tm, tk), lambda i, j, k: (i, k)),
               b_spec=pl.BlockSpec((tn, tk), lambda i, j, k: (j, k)),
               out_shape=[jax.ShapeDtypeStruct((M, N), out_dtype)],
               out_specs=[pl.BlockSpec((tm, tn), lambda i, j, k: (i, j))],
               acc_shape=(tm, tn), epilogue=epilogue or _ep_store(out_dtype), name=name,
               extra=extra, extra_specs=extra_specs, after=after)[0]


def _mm_nt_blocked(a, wg, out_dtype, name, tm=1024, after=()):
    M = a.shape[0]
    kout, n = wg.shape[1], wg.shape[2]
    tm = _tile(M, tm)
    return _mm(a, wg, mode="nt", grid=(M // tm, 1, N_DEV),
               a_spec=pl.BlockSpec((tm, n), lambda i, j, k: (i, k)),
               b_spec=pl.BlockSpec((None, kout, n), lambda i, j, k: (k, 0, 0)),
               out_shape=[jax.ShapeDtypeStruct((M, kout), out_dtype)],
               out_specs=[pl.BlockSpec((tm, kout), lambda i, j, k: (i, 0))],
               acc_shape=(tm, kout), epilogue=_ep_store(out_dtype), name=name, after=after)[0]


def _mm_tn(a, b, out_dtype, name, tm=1024, tn=1024, tk=1024):
    K, M = a.shape
    N = b.shape[1]
    tm, tn, tk = _tile(M, tm), _tile(N, tn), _tile(K, tk)
    return _mm(a, b, mode="tn", grid=(M // tm, N // tn, K // tk),
               a_spec=pl.BlockSpec((tk, tm), lambda i, j, k: (k, i)),
               b_spec=pl.BlockSpec((tk, tn), lambda i, j, k: (k, j)),
               out_shape=[jax.ShapeDtypeStruct((M, N), out_dtype)],
               out_specs=[pl.BlockSpec((tm, tn), lambda i, j, k: (i, j))],
               acc_shape=(tm, tn), epilogue=_ep_store(out_dtype), name=name)[0]


def _mm_tn_blocked(a, b, out_dtype, name, tm=1024, tk=2048):
    K, M = a.shape
    n = b.shape[1] // N_DEV
    tm, tk = _tile(M, tm), _tile(K, tk)
    return _mm(a, b, mode="tn", grid=(M // tm, N_DEV, K // tk),
               a_spec=pl.BlockSpec((tk, tm), lambda i, j, k: (k, i)),
               b_spec=pl.BlockSpec((tk, n), lambda i, j, k: (k, j)),
               out_shape=[jax.ShapeDtypeStruct((N_DEV, M, n), out_dtype)],
               out_specs=[pl.BlockSpec((None, tm, n), lambda i, j, k: (j, i, 0))],
               acc_shape=(tm, n), epilogue=_ep_store(out_dtype), name=name)[0]


def _window_geometry(ws):
    base = [(ws * k // LANES) * LANES for k in range(N_DEV)]
    off = [ws * k - base[k] for k in range(N_DEV)]
    win = -(-(max(off) + ws) // LANES) * LANES
    return base, off, win


def _shards_to_columns(xg, base, off, win, n_out, name, tr=256):
    R, ws = xg.shape[1], xg.shape[2]
    tr = _tile(R, tr)
    nb_win = win // LANES

    def body(x_ref, o_ref, frame_ref):
        written = set()
        frame_ref[...] = jnp.zeros_like(frame_ref)
        for k in range(N_DEV):
            frame_ref[:, 0:ws] = x_ref[k].astype(F32)
            window = frame_ref[...]
            if off[k]:
                window = pltpu.roll(window, off[k], 1)
            for i in range(nb_win):
                b = base[k] // LANES + i
                if b * LANES >= n_out:
                    continue
                cols = slice(b * LANES, (b + 1) * LANES)
                blk = window[:, i * LANES:(i + 1) * LANES]
                if b in written:
                    blk = blk + o_ref[:, cols].astype(F32)
                o_ref[:, cols] = blk.astype(o_ref.dtype)
                written.add(b)
        for b in range(n_out // LANES):
            if b not in written:
                o_ref[:, b * LANES:(b + 1) * LANES] = jnp.zeros((tr, LANES), o_ref.dtype)

    return _pcall(
        body, name=name, grid=(R // tr,), out_shape=jax.ShapeDtypeStruct((R, n_out), xg.dtype),
        in_specs=[pl.BlockSpec((N_DEV, tr, ws), lambda i: (0, i, 0))],
        out_specs=pl.BlockSpec((tr, n_out), lambda i: (i, 0)),
        scratch_shapes=[pltpu.VMEM((tr, win), F32)],
        compiler_params=_cparams(("parallel",)))(xg)


def _sigmoid(x):
    return 1.0 / (1.0 + jnp.exp(-x))


def _row_spec(tm, d):
    return pl.BlockSpec((tm, d), lambda i: (i, 0))


def _vec_spec(d):
    return pl.BlockSpec((1, d), lambda i: (0, 0))


def _norm_mod_fwd(x, y, gate, nw, scale, shift, name, tm=256):
    L, D = x.shape
    tm = _tile(L, tm)
    has_res = y is not None

    def body(*refs):
        if has_res:
            x_ref, y_ref, g_ref, nw_ref, sc_ref, sh_ref, xo_ref, h_ref = refs
            xn = x_ref[...] + g_ref[...] * y_ref[...]
            xo_ref[...] = xn
        else:
            x_ref, nw_ref, sc_ref, sh_ref, h_ref = refs
            xn = x_ref[...]
        rstd = lax.rsqrt(jnp.mean(xn * xn, axis=-1, keepdims=True) + NORM_EPS)
        h = xn * rstd * nw_ref[...] * (1.0 + sc_ref[...]) + sh_ref[...]
        h_ref[...] = h.astype(BF16)

    row, vec = _row_spec(tm, D), _vec_spec(D)
    if has_res:
        ins, in_specs = (x, y, gate, nw, scale, shift), [row, row, vec, vec, vec, vec]
        out_shape = [jax.ShapeDtypeStruct((L, D), F32), jax.ShapeDtypeStruct((L, D), BF16)]
        out_specs = [row, row]
    else:
        ins, in_specs = (x, nw, scale, shift), [row, vec, vec, vec]
        out_shape = [jax.ShapeDtypeStruct((L, D), BF16)]
        out_specs = [row]
    outs = _pcall(body, name=name, grid=(L // tm,), out_shape=out_shape, in_specs=in_specs,
                  out_specs=out_specs, compiler_params=_cparams(("parallel",)))(*ins)
    return outs if has_res else (x, outs[0])


def _norm_mod_bwd(dh, x, nw, scale, dres, name, tm=256):
    L, D = x.shape
    tm = _tile(L, tm)

    def body(dh_ref, x_ref, nw_ref, sc_ref, dres_ref, dx_ref, dsh_ref, dsc_ref, dnw_ref):
        @pl.when(pl.program_id(0) == 0)
        def _():
            dsh_ref[...] = jnp.zeros_like(dsh_ref)
            dsc_ref[...] = jnp.zeros_like(dsc_ref)
            dnw_ref[...] = jnp.zeros_like(dnw_ref)

        xv = x_ref[...]
        dh_v = dh_ref[...]
        nw_v = nw_ref[...]
        rstd = lax.rsqrt(jnp.mean(xv * xv, axis=-1, keepdims=True) + NORM_EPS)
        xhat = xv * rstd
        dsh_ref[...] += jnp.sum(dh_v, axis=0, keepdims=True)
        dsc_ref[...] += jnp.sum(dh_v * (xhat * nw_v), axis=0, keepdims=True)
        dr = dh_v * (1.0 + sc_ref[...])
        dnw_ref[...] += jnp.sum(dr * xhat, axis=0, keepdims=True)
        dxh = dr * nw_v
        dx = rstd * (dxh - xhat * jnp.mean(dxh * xhat, axis=-1, keepdims=True))
        dx_ref[...] = dx + dres_ref[...]

    row, vec = _row_spec(tm, D), _vec_spec(D)
    return _pcall(
        body, name=name, grid=(L // tm,),
        out_shape=[jax.ShapeDtypeStruct((L, D), F32)] + [jax.ShapeDtypeStruct((1, D), F32)] * 3,
        in_specs=[row, row, vec, vec, row], out_specs=[row, vec, vec, vec],
        compiler_params=_cparams(("arbitrary",)))(dh, x, nw, scale, dres)


def _gate_bwd(dx, y, gate, name, tm=256):
    L, D = dx.shape
    tm = _tile(L, tm)

    def body(dx_ref, y_ref, g_ref, dy_ref, dg_ref):
        @pl.when(pl.program_id(0) == 0)
        def _():
            dg_ref[...] = jnp.zeros_like(dg_ref)

        dxv = dx_ref[...]
        dy_ref[...] = (g_ref[...] * dxv).astype(BF16)
        dg_ref[...] += jnp.sum(dxv * y_ref[...], axis=0, keepdims=True)

    row, vec = _row_spec(tm, D), _vec_spec(D)
    return _pcall(
        body, name=name, grid=(L // tm,),
        out_shape=[jax.ShapeDtypeStruct((L, D), BF16), jax.ShapeDtypeStruct((1, D), F32)],
        in_specs=[row, row, vec], out_specs=[row, vec],
        compiler_params=_cparams(("arbitrary",)))(dx, y, gate)


def _final_loss(x, y, gate, fw, target, name, tm=256):
    L, D = x.shape
    tm = _tile(L, tm)

    def body(x_ref, y_ref, g_ref, fw_ref, t_ref, dx_ref, loss_ref, dfw_ref):
        @pl.when(pl.program_id(0) == 0)
        def _():
            loss_ref[...] = jnp.zeros_like(loss_ref)
            dfw_ref[...] = jnp.zeros_like(dfw_ref)

        xn = x_ref[...] + g_ref[...] * y_ref[...]
        fw_v = fw_ref[...]
        rstd = lax.rsqrt(jnp.mean(xn * xn, axis=-1, keepdims=True) + NORM_EPS)
        xhat = xn * rstd
        diff = xhat * fw_v - t_ref[...]
        loss_ref[...] += jnp.sum(diff * diff, axis=0, keepdims=True)
        dyf = diff * (1.0 / D)
        dfw_ref[...] += jnp.sum(dyf * xhat, axis=0, keepdims=True)
        dxh = dyf * fw_v
        dx_ref[...] = rstd * (dxh - xhat * jnp.mean(dxh * xhat, axis=-1, keepdims=True))

    row, vec = _row_spec(tm, D), _vec_spec(D)
    return _pcall(
        body, name=name, grid=(L // tm,),
        out_shape=[jax.ShapeDtypeStruct((L, D), F32), jax.ShapeDtypeStruct((1, D), F32),
                   jax.ShapeDtypeStruct((1, D), F32)],
        in_specs=[row, row, vec, vec, row], out_specs=[row, vec, vec],
        compiler_params=_cparams(("arbitrary",)))(x, y, gate, fw, target)


def _shift_down(v, s, row):
    if s == 0:
        return v
    return jnp.where(row >= s, pltpu.roll(v, s, 0), 0.0)


def _shift_up(v, s, row):
    if s == 0:
        return v
    n = v.shape[0]
    return jnp.where(row < n - s, pltpu.roll(v, n - s, 0), 0.0)


def _ssd_conv_fwd(zx, w, b, col0, width, name, cb=128):
    L = zx.shape[0]
    nb = width // cb
    off = col0 // cb

    def body(x_ref, w_ref, b_ref, o_ref):
        xv = x_ref[...]
        row = lax.broadcasted_iota(jnp.int32, xv.shape, 0)
        acc = b_ref[...] + w_ref[SSD_K - 1:SSD_K, :] * xv
        for s in range(1, SSD_K):
            acc = acc + w_ref[SSD_K - 1 - s:SSD_K - s, :] * _shift_down(xv, s, row)
        o_ref[...] = acc * _sigmoid(acc)

    return _pcall(
        body, name=name, grid=(nb,), out_shape=jax.ShapeDtypeStruct((L, width), F32),
        in_specs=[pl.BlockSpec((L, cb), lambda j: (0, off + j)),
                  pl.BlockSpec((SSD_K, cb), lambda j: (0, j)),
                  pl.BlockSpec((1, cb), lambda j: (0, j))],
        out_specs=pl.BlockSpec((L, cb), lambda j: (0, j)),
        compiler_params=_cparams(("parallel",)))(zx, w, b)


def _ssd_conv_bwd(zx, w, b, d_parts, dzx, col0, name, cb=128):
    L = zx.shape[0]
    widths = [p.shape[1] for p in d_parts]
    width = sum(widths)
    nb = width // cb
    off = col0 // cb
    starts = [sum(widths[:i]) // cb for i in range(len(d_parts))]
    counts = [wd // cb for wd in widths]

    def body(x_ref, w_ref, b_ref, *rest):
        d_refs = rest[:len(d_parts)]
        dx_ref, dw_ref, db_ref = rest[len(d_parts) + 1:]
        j = pl.program_id(0)
        d_val = d_refs[-1][...]
        for i in range(len(d_parts) - 2, -1, -1):
            d_val = jnp.where(j < starts[i + 1], d_refs[i][...], d_val)
        xv = x_ref[...]
        row = lax.broadcasted_iota(jnp.int32, xv.shape, 0)
        shifted = [_shift_down(xv, s, row) for s in range(SSD_K)]
        acc = b_ref[...] + w_ref[SSD_K - 1:SSD_K, :] * xv
        for s in range(1, SSD_K):
            acc = acc + w_ref[SSD_K - 1 - s:SSD_K - s, :] * shifted[s]
        sig = _sigmoid(acc)
        dpre = d_val * (sig * (1.0 + acc * (1.0 - sig)))
        db_ref[...] = jnp.sum(dpre, axis=0, keepdims=True)
        dx = w_ref[SSD_K - 1:SSD_K, :] * dpre
        for s in range(SSD_K):
            dw_ref[SSD_K - 1 - s:SSD_K - s, :] = jnp.sum(dpre * shifted[s], axis=0, keepdims=True)
            if s:
                dx = dx + w_ref[SSD_K - 1 - s:SSD_K - s, :] * _shift_up(dpre, s, row)
        dx_ref[...] = dx.astype(BF16)

    def part_spec(i):
        return pl.BlockSpec((L, cb), lambda j: (0, jnp.clip(j - starts[i], 0, counts[i] - 1)))

    return _pcall(
        body, name=name, grid=(nb,),
        out_shape=[jax.ShapeDtypeStruct(dzx.shape, BF16), jax.ShapeDtypeStruct((SSD_K, width), F32),
                   jax.ShapeDtypeStruct((1, width), F32)],
        in_specs=[pl.BlockSpec((L, cb), lambda j: (0, off + j)),
                  pl.BlockSpec((SSD_K, cb), lambda j: (0, j)),
                  pl.BlockSpec((1, cb), lambda j: (0, j))]
        + [part_spec(i) for i in range(len(d_parts))] + [pl.BlockSpec(memory_space=pl.ANY)],
        out_specs=[pl.BlockSpec((L, cb), lambda j: (0, off + j)),
                   pl.BlockSpec((SSD_K, cb), lambda j: (0, j)),
                   pl.BlockSpec((1, cb), lambda j: (0, j))],
        input_output_aliases={3 + len(d_parts): 0},
        compiler_params=_cparams(("parallel",)))(zx, w, b, *d_parts, dzx)


def _dzx_finish(dzx, ddt, col0, name, tl=512):
    G, L, _ = ddt.shape
    tail = dzx.shape[1] - col0
    tl = _tile(L, tl)

    def body(ddt_ref, dzx_ref, o_ref):
        s = ddt_ref[0]
        for g in range(1, G):
            s = s + ddt_ref[g]
        o_ref[:, 0:LANES] = s.astype(o_ref.dtype)
        if tail > LANES:
            o_ref[:, LANES:] = jnp.zeros((tl, tail - LANES), o_ref.dtype)

    return _pcall(
        body, name=name, grid=(L // tl,), out_shape=jax.ShapeDtypeStruct(dzx.shape, dzx.dtype),
        in_specs=[pl.BlockSpec((G, tl, LANES), lambda i: (0, i, 0)), pl.BlockSpec(memory_space=pl.ANY)],
        out_specs=pl.BlockSpec((tl, tail), lambda i: (i, col0 // tail)),
        input_output_aliases={1: 0},
        compiler_params=_cparams(("parallel",)))(ddt, dzx)


def _sc_conv_fwd(proj, w, name, cb=128):
    L = proj.shape[0]
    width = proj.shape[1] // 3
    nb = width // cb

    def body(b_ref, c_ref, x_ref, w_ref, o_ref):
        q = c_ref[...] * x_ref[...]
        row = lax.broadcasted_iota(jnp.int32, q.shape, 0)
        acc = w_ref[SC_K - 1:SC_K, :] * q
        for s in range(1, SC_K):
            acc = acc + w_ref[SC_K - 1 - s:SC_K - s, :] * _shift_down(q, s, row)
        o_ref[...] = (b_ref[...] * acc).astype(BF16)

    return _pcall(
        body, name=name, grid=(nb,), out_shape=jax.ShapeDtypeStruct((L, width), BF16),
        in_specs=[pl.BlockSpec((L, cb), lambda j: (0, j)),
                  pl.BlockSpec((L, cb), lambda j: (0, nb + j)),
                  pl.BlockSpec((L, cb), lambda j: (0, 2 * nb + j)),
                  pl.BlockSpec((SC_K, cb), lambda j: (0, j))],
        out_specs=pl.BlockSpec((L, cb), lambda j: (0, j)),
        compiler_params=_cparams(("parallel",)))(proj, proj, proj, w)


def _sc_conv_bwd(proj, w, dy, name, cb=128):
    L = proj.shape[0]
    width = proj.shape[1] // 3
    nb = width // cb

    def body(b_ref, c_ref, x_ref, w_ref, dy_ref, db_ref, dc_ref, dxv_ref, dw_ref):
        cg, xv, dyv = c_ref[...], x_ref[...], dy_ref[...]
        q = cg * xv
        row = lax.broadcasted_iota(jnp.int32, q.shape, 0)
        shifted = [_shift_down(q, s, row) for s in range(SC_K)]
        conv = w_ref[SC_K - 1:SC_K, :] * q
        for s in range(1, SC_K):
            conv = conv + w_ref[SC_K - 1 - s:SC_K - s, :] * shifted[s]
        db_ref[...] = (dyv * conv).astype(BF16)
        dconv = dyv * b_ref[...]
        dq = w_ref[SC_K - 1:SC_K, :] * dconv
        for s in range(SC_K):
            dw_ref[SC_K - 1 - s:SC_K - s, :] = jnp.sum(dconv * shifted[s], axis=0, keepdims=True)
            if s:
                dq = dq + w_ref[SC_K - 1 - s:SC_K - s, :] * _shift_up(dconv, s, row)
        dc_ref[...] = (dq * xv).astype(BF16)
        dxv_ref[...] = (dq * cg).astype(BF16)

    blk = pl.BlockSpec((L, cb), lambda j: (0, j))
    wblk = pl.BlockSpec((SC_K, cb), lambda j: (0, j))
    return _pcall(
        body, name=name, grid=(nb,),
        out_shape=[jax.ShapeDtypeStruct((L, width), BF16)] * 3 + [jax.ShapeDtypeStruct((SC_K, width), F32)],
        in_specs=[blk, pl.BlockSpec((L, cb), lambda j: (0, nb + j)),
                  pl.BlockSpec((L, cb), lambda j: (0, 2 * nb + j)), wblk, blk],
        out_specs=[blk, blk, blk, wblk],
        compiler_params=_cparams(("parallel",)))(proj, proj, proj, w, dy)


def _split3(v):
    hi = v.astype(BF16)
    r1 = v - hi.astype(F32)
    mid = r1.astype(BF16)
    lo = (r1 - mid.astype(F32)).astype(BF16)
    return hi, mid, lo


def _dot_exact01(t01, v):
    hi, mid, lo = _split3(v)
    return _dot(t01, hi) + _dot(t01, mid) + _dot(t01, lo)


def _lane_col(v, lane, h):
    return jnp.sum(jnp.where(lane == h, v, 0.0), axis=1, keepdims=True)


def _sum_all(v):
    return jnp.sum(jnp.sum(v, axis=1, keepdims=True), axis=0, keepdims=True)


def _softplus(x):
    return jnp.maximum(x, 0.0) + jnp.log1p(jnp.exp(-jnp.abs(x)))


def _ssd_common(dt_ref, bias_ref, alog_ref, b_ref, c_ref, cst_ref, heads):
    c_sz = SSD_CHUNK
    lane = lax.broadcasted_iota(jnp.int32, (c_sz, LANES), 1)
    row = lax.broadcasted_iota(jnp.int32, (c_sz, LANES), 0)
    valid = lane < heads
    raw = dt_ref[...] + bias_ref[...]
    dt = _softplus(raw)
    a_row = -jnp.exp(alog_ref[...])
    a = jnp.where(valid, dt * a_row, 0.0)
    tri = (row >= lane).astype(BF16)
    cs = _dot_exact01(tri, a)
    cst_ref[...] = cs.T
    last_row = jnp.sum(a, axis=0, keepdims=True)
    bb = b_ref[...].astype(BF16)
    cb = c_ref[...].astype(BF16)
    scores = _dot(cb, bb, "nt")
    return dict(lane=lane, row=row, valid=valid, raw=raw, dt=dt, a_row=a_row, cs=cs,
                last_row=last_row, bb=bb, cb=cb, scores=scores, causal=row >= lane, lo=lane < SSD_P)


def _pair_terms(q, cst_ref, h0):
    lane, lo = q["lane"], q["lo"]
    out = {}
    cols, dts, lasts, lms = [], [], [], []
    lane1 = lax.broadcasted_iota(jnp.int32, (1, LANES), 1)
    for h in (h0, h0 + 1):
        col = _lane_col(q["cs"], lane, h)
        rowv = cst_ref[pl.ds(h, 1), :]
        lms.append(jnp.exp(jnp.where(q["causal"], col - rowv, -1e30)))
        cols.append(col)
        dts.append(_lane_col(q["dt"], lane, h))
        lasts.append(jnp.sum(jnp.where(lane1 == h, q["last_row"], 0.0), axis=1, keepdims=True))
    out["lm"] = lms
    out["cols"] = cols
    out["lasts"] = lasts
    out["dt_b"] = jnp.where(lo, dts[0], dts[1])
    out["e_b"] = jnp.where(lo, jnp.exp(cols[0]), jnp.exp(cols[1]))
    out["dec_cols"] = [jnp.exp(lasts[0] - cols[0]), jnp.exp(lasts[1] - cols[1])]
    out["dec_b"] = jnp.where(lo, out["dec_cols"][0], out["dec_cols"][1])
    lo1 = lane1 < SSD_P
    out["explast"] = [jnp.exp(lasts[0]), jnp.exp(lasts[1])]
    out["explast_b"] = jnp.where(lo1, out["explast"][0], out["explast"][1])
    return out


def _ssd_fwd(zx, xc, bias_p, alog_p, d_lane, nw, d_inner, name):
    L = zx.shape[0]
    nc = L // SSD_CHUNK
    gw = d_inner // SSD_G
    heads = gw // SSD_P
    n_pair = heads // 2
    bc0 = d_inner // LANES
    dt0 = (2 * d_inner + 2 * SSD_G * SSD_N) // LANES

    def body(z_ref, xs_ref, b_ref, c_ref, dt_ref, bias_ref, alog_ref, dl_ref, nw_ref,
             y_ref, yn_ref, prev_ref, s_ref, cst_ref):
        @pl.when(pl.program_id(1) == 0)
        def _():
            s_ref[...] = jnp.zeros_like(s_ref)

        q = _ssd_common(dt_ref, bias_ref, alog_ref, b_ref, c_ref, cst_ref, SSD_G * heads)
        prev_ref[...] = s_ref[...]
        lo = q["lo"]
        for j in range(n_pair):
            sl = slice(j * LANES, (j + 1) * LANES)
            p = _pair_terms(q, cst_ref, pl.program_id(0) * heads + 2 * j)
            xs_p = xs_ref[:, sl]
            xp = xs_p * p["dt_b"]
            xb = xp.astype(BF16)
            m_a = (q["scores"] * p["lm"][0]).astype(BF16)
            m_b = (q["scores"] * p["lm"][1]).astype(BF16)
            yd = jnp.where(lo, _dot(m_a, xb), _dot(m_b, xb))
            s_p = s_ref[:, sl]
            yo = _dot(q["cb"], s_p.astype(BF16)) * p["e_b"]
            y_ref[:, sl] = yd + yo + dl_ref[:, sl] * xs_p
            st = _dot(q["bb"], (xp * p["dec_b"]).astype(BF16), "tn")
            s_ref[:, sl] = s_p * p["explast_b"] + st
        yv = y_ref[...]
        zv = z_ref[...]
        yg = yv * (zv * _sigmoid(zv))
        rstd = lax.rsqrt(jnp.mean(yg * yg, axis=-1, keepdims=True) + NORM_EPS)
        yn_ref[...] = (yg * rstd * nw_ref[...]).astype(BF16)

    grp = lambda width: pl.BlockSpec((None, 1, width), lambda g, c: (g, 0, 0))
    head_vec = pl.BlockSpec((1, LANES), lambda g, c: (0, 0))
    return _pcall(
        body, name=name, grid=(SSD_G, nc),
        out_shape=[jax.ShapeDtypeStruct((L, d_inner), F32), jax.ShapeDtypeStruct((L, d_inner), BF16),
                   jax.ShapeDtypeStruct((nc, SSD_G, SSD_N, gw), F32)],
        in_specs=[pl.BlockSpec((SSD_CHUNK, gw), lambda g, c: (c, g)),
                  pl.BlockSpec((SSD_CHUNK, gw), lambda g, c: (c, g)),
                  pl.BlockSpec((SSD_CHUNK, SSD_N), lambda g, c: (c, bc0 + g)),
                  pl.BlockSpec((SSD_CHUNK, SSD_N), lambda g, c: (c, bc0 + SSD_G + g)),
                  pl.BlockSpec((SSD_CHUNK, LANES), lambda g, c: (c, dt0)),
                  head_vec, head_vec, grp(gw), grp(gw)],
        out_specs=[pl.BlockSpec((SSD_CHUNK, gw), lambda g, c: (c, g)),
                   pl.BlockSpec((SSD_CHUNK, gw), lambda g, c: (c, g)),
                   pl.BlockSpec((None, None, SSD_N, gw), lambda g, c: (c, g, 0, 0))],
        scratch_shapes=[pltpu.VMEM((SSD_N, gw), F32), pltpu.VMEM((SSD_CHUNK, LANES), F32)],
        compiler_params=_cparams(("parallel", "arbitrary")))(zx, xc, xc, xc, zx, bias_p, alog_p, d_lane, nw)


def _ssd_bwd(dyn, y, zx, xc, prev, bias_p, alog_p, d_lane, nw, d_inner, name):
    L = zx.shape[0]
    nc = L // SSD_CHUNK
    gw = d_inner // SSD_G
    heads = gw // SSD_P
    n_pair = heads // 2
    bc0 = d_inner // LANES
    dt0 = (2 * d_inner + 2 * SSD_G * SSD_N) // LANES

    def body(dyn_ref, y_ref, z_ref, xs_ref, b_ref, c_ref, dt_ref, prev_ref, bias_ref, alog_ref, dl_ref, nw_ref,
             dz_ref, dxs_ref, db_ref, dc_ref, ddt_ref, dbias_ref, dalog_ref, dd_ref, dnw_ref,
             ds_ref, cst_ref, racc_ref):
        @pl.when(pl.program_id(1) == 0)
        def _():
            ds_ref[...] = jnp.zeros_like(ds_ref)
            dbias_ref[...] = jnp.zeros_like(dbias_ref)
            dalog_ref[...] = jnp.zeros_like(dalog_ref)
            dd_ref[...] = jnp.zeros_like(dd_ref)
            dnw_ref[...] = jnp.zeros_like(dnw_ref)

        q = _ssd_common(dt_ref, bias_ref, alog_ref, b_ref, c_ref, cst_ref, SSD_G * heads)
        lane, row, lo = q["lane"], q["row"], q["lo"]
        lane1 = lax.broadcasted_iota(jnp.int32, (1, LANES), 1)
        head0 = pl.program_id(0) * heads
        mine = (lane >= head0) & (lane < head0 + heads)

        yv, zv, dynv, nwv = y_ref[...], z_ref[...], dyn_ref[...], nw_ref[...]
        sig = _sigmoid(zv)
        sz = zv * sig
        yg = yv * sz
        rstd = lax.rsqrt(jnp.mean(yg * yg, axis=-1, keepdims=True) + NORM_EPS)
        yhat = yg * rstd
        dnw_ref[...] += jnp.sum(dynv * yhat, axis=0, keepdims=True)
        dyh = dynv * nwv
        dyg = rstd * (dyh - yhat * jnp.mean(dyh * yhat, axis=-1, keepdims=True))
        dz_ref[...] = (dyg * yv * (sig * (1.0 + zv * (1.0 - sig)))).astype(BF16)
        dy_all = dyg * sz

        dg = jnp.zeros((SSD_CHUNK, SSD_CHUNK), F32)
        dc_acc = jnp.zeros((SSD_CHUNK, SSD_N), F32)
        db_acc = jnp.zeros((SSD_CHUNK, SSD_N), F32)
        dcs_mat = jnp.zeros((SSD_CHUNK, LANES), F32)
        ddt_mat = jnp.zeros((SSD_CHUNK, LANES), F32)
        dd_row = jnp.zeros((1, LANES), F32)
        racc_ref[...] = jnp.zeros_like(racc_ref)
        is_last = row == SSD_CHUNK - 1

        for j in range(n_pair):
            sl = slice(j * LANES, (j + 1) * LANES)
            ha, hb = head0 + 2 * j, head0 + 2 * j + 1
            p = _pair_terms(q, cst_ref, ha)
            xs_p = xs_ref[:, sl]
            dyp = dy_all[:, sl]
            xp = xs_p * p["dt_b"]
            xb = xp.astype(BF16)
            s_p = prev_ref[:, sl]
            s_pb = s_p.astype(BF16)
            dsn = ds_ref[:, sl]
            dsnb = dsn.astype(BF16)
            m_f = [q["scores"] * p["lm"][0], q["scores"] * p["lm"][1]]

            t0 = dyp * xs_p
            dd_row = dd_row + jnp.where(lane1 == ha, _sum_all(jnp.where(lo, t0, 0.0)), 0.0) \
                + jnp.where(lane1 == hb, _sum_all(jnp.where(lo, 0.0, t0)), 0.0)
            dxs_p = dl_ref[:, sl] * dyp

            yo = _dot(q["cb"], s_pb) * p["e_b"]
            dcs_b = (dyp * p["e_b"]).astype(BF16)
            dc_acc = dc_acc + _dot(dcs_b, s_pb, "nt")
            ds_yo = _dot(q["cb"], dcs_b, "tn")
            t1 = dyp * yo
            dcs_cols = [jnp.sum(jnp.where(lo, t1, 0.0), axis=1, keepdims=True),
                        jnp.sum(jnp.where(lo, 0.0, t1), axis=1, keepdims=True)]

            t2 = dsn * s_p
            dlast = [p["explast"][0] * _sum_all(jnp.where(lo, t2, 0.0)),
                     p["explast"][1] * _sum_all(jnp.where(lo, 0.0, t2))]
            ds_ref[:, sl] = dsn * p["explast_b"] + ds_yo
            w = _dot(q["bb"], dsnb)
            db_acc = db_acc + _dot((xp * p["dec_b"]).astype(BF16), dsnb, "nt")
            dxp = w * p["dec_b"]
            t3 = w * xp
            e = [jnp.sum(jnp.where(lo, t3, 0.0), axis=1, keepdims=True) * p["dec_cols"][0],
                 jnp.sum(jnp.where(lo, 0.0, t3), axis=1, keepdims=True) * p["dec_cols"][1]]
            for i in range(2):
                dlast[i] = dlast[i] + jnp.sum(e[i], axis=0, keepdims=True)
                dcs_cols[i] = dcs_cols[i] - e[i]

            dyb = dyp.astype(BF16)
            dy_h = [jnp.where(lo, dyp, 0.0).astype(BF16), jnp.where(lo, 0.0, dyp).astype(BF16)]
            dms = [_dot(dy_h[0], xb, "nt"), _dot(dy_h[1], xb, "nt")]
            dxp = dxp + jnp.where(lo, _dot(m_f[0].astype(BF16), dyb, "tn"), _dot(m_f[1].astype(BF16), dyb, "tn"))
            for i, h in enumerate((ha, hb)):
                dg = dg + dms[i] * p["lm"][i]
                qm = dms[i] * m_f[i]
                dcs_cols[i] = dcs_cols[i] + jnp.sum(qm, axis=1, keepdims=True)
                racc_ref[pl.ds(h, 1), :] = jnp.sum(qm, axis=0, keepdims=True)

            dxs_ref[:, sl] = dxs_p + dxp * p["dt_b"]
            t4 = dxp * xs_p
            ddt_cols = [jnp.sum(jnp.where(lo, t4, 0.0), axis=1, keepdims=True),
                        jnp.sum(jnp.where(lo, 0.0, t4), axis=1, keepdims=True)]
            for i, h in enumerate((ha, hb)):
                sel = lane == h
                dcs_mat = dcs_mat + jnp.where(sel, dcs_cols[i], 0.0) + jnp.where(sel & is_last, dlast[i], 0.0)
                ddt_mat = ddt_mat + jnp.where(sel, ddt_cols[i], 0.0)

        dcs_mat = dcs_mat - racc_ref[...].T
        tri_t = (row <= lane).astype(BF16)
        da = _dot_exact01(tri_t, dcs_mat)
        ddt = ddt_mat + da * q["a_row"]
        dalog_ref[...] += jnp.sum(jnp.where(mine, da * q["dt"], 0.0), axis=0, keepdims=True) * q["a_row"]
        draw = jnp.where(mine, ddt * _sigmoid(q["raw"]), 0.0)
        ddt_ref[...] = draw
        dbias_ref[...] += jnp.sum(draw, axis=0, keepdims=True)
        dd_ref[...] += dd_row
        dgb = dg.astype(BF16)
        dc_ref[...] = dc_acc + _dot(dgb, q["bb"])
        db_ref[...] = db_acc + _dot(dgb, q["cb"], "tn")

    rev = lambda c: nc - 1 - c
    grp = lambda width: pl.BlockSpec((None, 1, width), lambda g, c: (g, 0, 0))
    blk = lambda width, off: pl.BlockSpec((SSD_CHUNK, width), lambda g, c: (rev(c), off + g))
    head_vec = pl.BlockSpec((1, LANES), lambda g, c: (0, 0))
    return _pcall(
        body, name=name, grid=(SSD_G, nc),
        out_shape=[jax.ShapeDtypeStruct(zx.shape, BF16), jax.ShapeDtypeStruct((L, d_inner), F32),
                   jax.ShapeDtypeStruct((L, SSD_G * SSD_N), F32), jax.ShapeDtypeStruct((L, SSD_G * SSD_N), F32),
                   jax.ShapeDtypeStruct((SSD_G, L, LANES), F32),
                   jax.ShapeDtypeStruct((SSD_G, 1, LANES), F32), jax.ShapeDtypeStruct((SSD_G, 1, LANES), F32),
                   jax.ShapeDtypeStruct((SSD_G, 1, LANES), F32), jax.ShapeDtypeStruct((SSD_G, 1, gw), F32)],
        in_specs=[blk(gw, 0), blk(gw, 0), blk(gw, 0), blk(gw, 0), blk(SSD_N, bc0), blk(SSD_N, bc0 + SSD_G),
                  pl.BlockSpec((SSD_CHUNK, LANES), lambda g, c: (rev(c), dt0)),
                  pl.BlockSpec((None, None, SSD_N, gw), lambda g, c: (rev(c), g, 0, 0)),
                  head_vec, head_vec, grp(gw), grp(gw)],
        out_specs=[blk(gw, 0), blk(gw, 0), blk(SSD_N, 0), blk(SSD_N, 0),
                   pl.BlockSpec((None, SSD_CHUNK, LANES), lambda g, c: (g, rev(c), 0)),
                   grp(LANES), grp(LANES), grp(LANES), grp(gw)],
        scratch_shapes=[pltpu.VMEM((SSD_N, gw), F32), pltpu.VMEM((SSD_CHUNK, LANES), F32),
                        pltpu.VMEM((SSD_CHUNK, LANES), F32)],
        compiler_params=_cparams(("parallel", "arbitrary")))(
            dyn, y, zx, xc, xc, xc, zx, prev, bias_p, alog_p, d_lane, nw)


def _cond_mod(c_pad, ada_w, ada_b_loc, name):
    depth, D, n = ada_w.shape
    rows = c_pad.shape[0]

    def body(c_ref, w_ref, b_ref, mod_ref, cond_ref):
        cv = c_ref[...]
        cond = cv * _sigmoid(cv)
        cond_ref[...] = cond
        mod_ref[...] = _dot(cond.astype(BF16), w_ref[...].astype(BF16)) + b_ref[...]

    return _pcall(
        body, name=name, grid=(depth,),
        out_shape=[jax.ShapeDtypeStruct((depth, rows, n), F32), jax.ShapeDtypeStruct((rows, D), F32)],
        in_specs=[pl.BlockSpec((rows, D), lambda i: (0, 0)),
                  pl.BlockSpec((None, D, n), lambda i: (i, 0, 0)),
                  pl.BlockSpec((None, 1, n), lambda i: (i, 0, 0))],
        out_specs=[pl.BlockSpec((None, rows, n), lambda i: (i, 0, 0)),
                   pl.BlockSpec((rows, D), lambda i: (0, 0))],
        compiler_params=_cparams(("arbitrary",)))(c_pad, ada_w, ada_b_loc)


def _adamw_math(g, w, m, v):
    m_new = ADAM_B1 * m + (1.0 - ADAM_B1) * g
    v_new = ADAM_B2 * v + (1.0 - ADAM_B2) * (g * g)
    m_hat = m_new / (1.0 - ADAM_B1 ** ADAM_STEP)
    v_hat = v_new / (1.0 - ADAM_B2 ** ADAM_STEP)
    delta = -ADAM_LR * (m_hat / (jnp.sqrt(v_hat) + ADAM_EPS) + ADAM_WD * w)
    return delta, m_new, v_new


def _adamw_sum(parts, w, m, v, layer, name, prev=None, tr=256, window_off=None):
    depth, R, C = w.shape
    tr = _tile(R, tr)
    win = parts.shape[2]
    scratch = [] if window_off is None else [pltpu.VMEM((tr, win), F32)]

    def body(p_ref, w_ref, m_ref, v_ref, *rest):
        g_ref, d_ref, mo_ref, vo_ref = rest[-4 - len(scratch):len(rest) - len(scratch)]
        g = p_ref[0].astype(F32)
        for k in range(1, N_DEV):
            g = g + p_ref[k].astype(F32)
        if window_off is not None:
            me = _my_index()
            off = 0
            for k in range(N_DEV):
                off = jnp.where(me == k, window_off[k], off)
            src = lax.broadcasted_iota(jnp.int32, (win, win), 0)
            dst = lax.broadcasted_iota(jnp.int32, (win, win), 1)
            shift = ((src == dst + off) & (dst < C)).astype(BF16)
            hi, mid, lo = _split3(g)
            rest[-1][...] = _dot(hi, shift) + _dot(mid, shift) + _dot(lo, shift)
            g = rest[-1][:, 0:C]
        d, mn, vn = _adamw_math(g, w_ref[...], m_ref[...], v_ref[...])
        g_ref[...] = g
        d_ref[...] = d
        mo_ref[...] = mn
        vo_ref[...] = vn

    blk = pl.BlockSpec((None, tr, C), lambda i: (layer, i, 0))
    prev = list(prev) if prev is not None else []
    return _pcall(
        body, name=name, grid=(R // tr,),
        out_shape=[jax.ShapeDtypeStruct((depth, R, C), F32)] * 4,
        in_specs=[pl.BlockSpec((N_DEV, tr, win), lambda i: (0, i, 0)), blk, blk, blk]
        + [pl.BlockSpec(memory_space=pl.ANY)] * len(prev),
        out_specs=[blk] * 4, input_output_aliases={4 + k: k for k in range(len(prev))},
        scratch_shapes=scratch,
        compiler_params=_cparams(("parallel",)))(parts, w, m, v, *prev)


def _ada_adamw(cond_pad, dmod_pad, w, m, v, name, tr=256):
    depth, D, n = w.shape
    rows = cond_pad.shape[0]
    tr = _tile(D, tr)

    def body(c_ref, dm_ref, w_ref, m_ref, v_ref, g_ref, d_ref, mo_ref, vo_ref):
        g = _dot(c_ref[...].astype(BF16), dm_ref[...].astype(BF16), "tn")
        d, mn, vn = _adamw_math(g, w_ref[...], m_ref[...], v_ref[...])
        g_ref[...] = g
        d_ref[...] = d
        mo_ref[...] = mn
        vo_ref[...] = vn

    blk = pl.BlockSpec((None, tr, n), lambda i, r: (i, r, 0))
    return _pcall(
        body, name=name, grid=(depth, D // tr),
        out_shape=[jax.ShapeDtypeStruct((depth, D, n), F32)] * 4,
        in_specs=[pl.BlockSpec((rows, tr), lambda i, r: (0, r)),
                  pl.BlockSpec((None, rows, n), lambda i, r: (i, 0, 0)), blk, blk, blk],
        out_specs=[blk] * 4, compiler_params=_cparams(("parallel", "parallel")))(cond_pad, dmod_pad, w, m, v)


def kernel(x, c, ada_w, ada_b, mix_norm_w, mlp_norm_w, mlp_up, mlp_down, ssd_in_w, ssd_conv_w, ssd_conv_b, ssd_dt_bias, ssd_A_log, ssd_D, ssd_norm_w, ssd_out_w, sc_in_w, sc_conv_w, sc_out_w, final_norm_w, loss_target, m_ada_w, m_ada_b, m_mix_norm_w, m_mlp_norm_w, m_mlp_up, m_mlp_down, m_ssd_in_w, m_ssd_conv_w, m_ssd_conv_b, m_ssd_dt_bias, m_ssd_A_log, m_ssd_D, m_ssd_norm_w, m_ssd_out_w, m_sc_in_w, m_sc_conv_w, m_sc_out_w, m_final_norm_w, v_ada_w, v_ada_b, v_mix_norm_w, v_mlp_norm_w, v_mlp_up, v_mlp_down, v_ssd_in_w, v_ssd_conv_w, v_ssd_conv_b, v_ssd_dt_bias, v_ssd_A_log, v_ssd_D, v_ssd_norm_w, v_ssd_out_w, v_sc_in_w, v_sc_conv_w, v_sc_out_w, v_final_norm_w):
    weights = dict(ada_w=ada_w, ada_b=ada_b, mix_norm_w=mix_norm_w, mlp_norm_w=mlp_norm_w, mlp_up=mlp_up,
                   mlp_down=mlp_down, ssd_in_w=ssd_in_w, ssd_conv_w=ssd_conv_w, ssd_conv_b=ssd_conv_b,
                   ssd_dt_bias=ssd_dt_bias, ssd_A_log=ssd_A_log, ssd_D=ssd_D, ssd_norm_w=ssd_norm_w,
                   ssd_out_w=ssd_out_w, sc_in_w=sc_in_w, sc_conv_w=sc_conv_w, sc_out_w=sc_out_w,
                   final_norm_w=final_norm_w)
    moms = dict(ada_w=m_ada_w, ada_b=m_ada_b, mix_norm_w=m_mix_norm_w, mlp_norm_w=m_mlp_norm_w, mlp_up=m_mlp_up,
                mlp_down=m_mlp_down, ssd_in_w=m_ssd_in_w, ssd_conv_w=m_ssd_conv_w, ssd_conv_b=m_ssd_conv_b,
                ssd_dt_bias=m_ssd_dt_bias, ssd_A_log=m_ssd_A_log, ssd_D=m_ssd_D, ssd_norm_w=m_ssd_norm_w,
                ssd_out_w=m_ssd_out_w, sc_in_w=m_sc_in_w, sc_conv_w=m_sc_conv_w, sc_out_w=m_sc_out_w,
                final_norm_w=m_final_norm_w)
    vars_ = dict(ada_w=v_ada_w, ada_b=v_ada_b, mix_norm_w=v_mix_norm_w, mlp_norm_w=v_mlp_norm_w, mlp_up=v_mlp_up,
                 mlp_down=v_mlp_down, ssd_in_w=v_ssd_in_w, ssd_conv_w=v_ssd_conv_w, ssd_conv_b=v_ssd_conv_b,
                 ssd_dt_bias=v_ssd_dt_bias, ssd_A_log=v_ssd_A_log, ssd_D=v_ssd_D, ssd_norm_w=v_ssd_norm_w,
                 ssd_out_w=v_ssd_out_w, sc_in_w=v_sc_in_w, sc_conv_w=v_sc_conv_w, sc_out_w=v_sc_out_w,
                 final_norm_w=v_final_norm_w)
    names = list(weights)

    L, D = x.shape[1], x.shape[2]
    d_inner = 2 * D
    n_heads = d_inner // SSD_P
    hpg = n_heads // SSD_G
    gw = d_inner // SSD_G
    conv_dim = d_inner + 2 * SSD_G * SSD_N
    zx_dim = d_inner + conv_dim
    zx_pad = -(-(zx_dim + LANES) // 512) * 512
    in_ws = ssd_in_w.shape[2]
    in_base, in_off, in_win = _window_geometry(in_ws)
    me = _my_index()
    x0 = x[0]
    tgt = loss_target[0]

    n_mod = ada_w.shape[2]
    (c_all,) = _exchange([c], "gather_c", gather=True)
    c_pad = jnp.pad(c_all.reshape(N_DEV, D), ((0, 16 - N_DEV), (0, 0)))
    ada_b_loc = lax.dynamic_slice_in_dim(ada_b, me * n_mod, n_mod, axis=1).reshape(2, 1, n_mod)
    mod_blk, cond_pad = _cond_mod(c_pad, ada_w, ada_b_loc, "cond_mod")

    gather_order = ["mod", "ssd_conv_w", "sc_conv_w", "ssd_in_w", "ssd_out_w", "up0", "down0", "sc_in_w",
                    "sc_out_w", "up1", "down1"]
    gather_src = dict(mod=mod_blk, ssd_conv_w=ssd_conv_w[0], sc_conv_w=sc_conv_w[0],
                      ssd_in_w=ssd_in_w[0].astype(BF16),
                      ssd_out_w=ssd_out_w[0].astype(BF16),
                      up0=mlp_up[0].astype(BF16), down0=mlp_down[0].astype(BF16),
                      sc_in_w=sc_in_w[0].astype(BF16), sc_out_w=sc_out_w[0].astype(BF16),
                      up1=mlp_up[1].astype(BF16), down1=mlp_down[1].astype(BF16))
    handles, gather_token = _xfer_start([gather_src[k] for k in gather_order], "gather_start", gather=True,
                                        via_sibling=tuple(range(3, len(gather_order))))
    gather_handle = dict(zip(gather_order, handles))

    def gathered(keys, after, forward):
        tag = "_".join(keys)
        lands = _xfer_wait([gather_handle[k] for k in keys], after, f"gather_wait_{tag}", gather=True)
        return _sibling_forward(lands, f"gather_forward_{tag}") if forward else lands

    (mod_all,) = gathered(["mod"], gather_token, False)
    mod_mine = lax.dynamic_index_in_dim(mod_all, me, axis=2, keepdims=False)
    mod_mine = jnp.transpose(mod_mine, (1, 0, 2)).reshape(2, 6, 1, D)
    sh_m, sc_m, g_m, sh_f, sc_f, g_f = [[mod_mine[i, k] for i in range(2)] for k in range(6)]

    vec = lambda a: a.reshape(1, -1)
    grads = {}
    small = {}

    _, h0 = _norm_mod_fwd(x0, None, None, vec(mix_norm_w[0]), sc_m[0], sh_m[0], "l0_mix_norm")
    cw_all, scw_all = gathered(["ssd_conv_w", "sc_conv_w"], h0, False)
    (ssd_in_g,) = gathered(["ssd_in_w"], h0, True)
    w_in_all = _shards_to_columns(ssd_in_g, in_base, in_off, in_win, zx_pad, "ssd_in_w_columns")
    (zx,) = _mm_nn(h0, w_in_all, F32, "ssd_in_proj", tm=2048, tn=512)
    conv_b0 = vec(ssd_conv_b[0])
    conv_w_full = jnp.transpose(cw_all, (1, 0, 2)).reshape(SSD_K, conv_dim)
    sc_conv_full = jnp.transpose(scw_all, (1, 0, 2)).reshape(SC_K, D)
    xc = _ssd_conv_fwd(zx, conv_w_full, conv_b0, d_inner, conv_dim, "ssd_conv")
    bias_p = jnp.pad(ssd_dt_bias[0], (0, LANES - n_heads)).reshape(1, LANES)
    alog_p = jnp.pad(ssd_A_log[0], (0, LANES - n_heads)).reshape(1, LANES)
    d_lane = jnp.repeat(ssd_D[0], SSD_P).reshape(SSD_G, 1, gw)
    nw_g = ssd_norm_w[0].reshape(SSD_G, 1, gw)
    y_ssd, yn, prev = _ssd_fwd(zx, xc, bias_p, alog_p, d_lane, nw_g, d_inner, "ssd_scan")
    ups, downs = [None, None], [None, None]
    ssd_out_g, ups[0], down0_g = gathered(["ssd_out_w", "up0", "down0"], yn, True)
    w_ssd_out, downs[0] = ssd_out_g.reshape(-1, D), down0_g.reshape(-1, D)
    (mix0,) = _mm_nn(yn, w_ssd_out, F32, "ssd_out_proj")
    x1, h1 = _norm_mod_fwd(x0, mix0, g_m[0], vec(mlp_norm_w[0]), sc_f[0], sh_f[0], "l0_mlp_norm")
    u0, s0 = _mm_nn_blocked(h1, ups[0], "l0_mlp_up", _ep_relu2, [BF16, BF16])
    (d0,) = _mm_nn(s0, downs[0], F32, "l0_mlp_down")
    x2, h2 = _norm_mod_fwd(x1, d0, g_f[0], vec(mix_norm_w[1]), sc_m[1], sh_m[1], "l1_mix_norm")
    sc_in_g, sc_out_g, ups[1], down1_g = gathered(["sc_in_w", "sc_out_w", "up1", "down1"], h2, True)
    w_sc_out, downs[1] = sc_out_g.reshape(-1, D), down1_g.reshape(-1, D)
    (proj,) = _mm_nn_blocked(h2, sc_in_g, "sc_in_proj", _ep_store(F32), [F32])
    yc = _sc_conv_fwd(proj, sc_conv_full, "sc_conv")
    (mix1,) = _mm_nn(yc, w_sc_out, F32, "sc_out_proj")
    x3, h3 = _norm_mod_fwd(x2, mix1, g_m[1], vec(mlp_norm_w[1]), sc_f[1], sh_f[1], "l1_mlp_norm")
    u1, s1 = _mm_nn_blocked(h3, ups[1], "l1_mlp_up", _ep_relu2, [BF16, BF16])
    (d1,) = _mm_nn(s1, downs[1], F32, "l1_mlp_down")

    dx, loss_lane, dfw = _final_loss(x3, d1, g_f[1], vec(final_norm_w), tgt, "final_loss")
    loss = lax.psum(0.5 * jnp.sum(loss_lane) / D, MESH_AXES)
    small["final_norm_w"] = dfw

    dmod = [[None] * 6 for _ in range(2)]
    big = {}

    def mlp_backward(i, dx_out, d_out, x_mid, h_in, u, s):
        dd, dg = _gate_bwd(dx_out, d_out, g_f[i], f"l{i}_mlp_gate_bwd")
        dmod[i][5] = dg
        du = _mm_nt(dd, downs[i], BF16, f"l{i}_mlp_down_bwd", epilogue=_ep_relu2_bwd, extra=(u,))
        gdown = _mm_tn(s, dd, BF16, f"l{i}_mlp_down_wgrad").reshape(N_DEV, -1, D)
        gup = _mm_tn_blocked(h_in, du, BF16, f"l{i}_mlp_up_wgrad")
        (h_down, h_up), token = _xfer_start([gdown, gup], f"l{i}_mlp_grads_start", gather=False)
        grad_handle[f"mlp_down{i}"], grad_handle[f"mlp_up{i}"] = h_down, h_up
        dh = _mm_nt_blocked(du, ups[i], F32, f"l{i}_mlp_up_bwd", after=(token,))
        dxm, dsh, dsc, dnw = _norm_mod_bwd(dh, x_mid, vec(mlp_norm_w[i]), sc_f[i], dx_out, f"l{i}_mlp_norm_bwd")
        dmod[i][3], dmod[i][4] = dsh, dsc
        return dxm, dnw

    grad_handle = {}
    dx3, dnw_mlp1 = mlp_backward(1, dx, d1, x3, h3, u1, s1)
    dyc, dg = _gate_bwd(dx3, mix1, g_m[1], "l1_mix_gate_bwd")
    dmod[1][2] = dg
    g_sc_out = _mm_tn(yc, dyc, BF16, "sc_out_wgrad").reshape(N_DEV, -1, D)
    dconv_out = _mm_nt(dyc, w_sc_out, F32, "sc_out_bwd")
    dbg, dcg, dxv, dscw = _sc_conv_bwd(proj, sc_conv_full, dconv_out, "sc_conv_bwd")
    dproj = jnp.concatenate([dbg, dcg, dxv], axis=1)
    g_sc_in = _mm_tn_blocked(h2, dproj, BF16, "sc_in_wgrad")
    (grad_handle["sc_out_w0"], grad_handle["sc_in_w0"]), token = _xfer_start(
        [g_sc_out, g_sc_in], "sc_grads_start", gather=False)
    dh2 = _mm_nt_blocked(dproj, sc_in_g, F32, "sc_in_bwd", after=(token,))
    dx2, dsh, dsc, dnw_mix1 = _norm_mod_bwd(dh2, x2, vec(mix_norm_w[1]), sc_m[1], dx3, "l1_mix_norm_bwd")
    dmod[1][0], dmod[1][1] = dsh, dsc
    dx1, dnw_mlp0 = mlp_backward(0, dx2, d0, x1, h1, u0, s0)
    dyo, dg = _gate_bwd(dx1, mix0, g_m[0], "l0_mix_gate_bwd")
    dmod[0][2] = dg
    g_ssd_out = _mm_tn(yn, dyo, BF16, "ssd_out_wgrad").reshape(N_DEV, -1, D)
    (grad_handle["ssd_out_w0"],), token = _xfer_start([g_ssd_out], "ssd_out_grad_start", gather=False)
    dyn = _mm_nt(dyo, w_ssd_out, F32, "ssd_out_bwd", after=(token,))
    dz, dxs, db_, dc_, ddt, dbias, dalog, dd_, dnw_ssd = _ssd_bwd(
        dyn, y_ssd, zx, xc, prev, bias_p, alog_p, d_lane, nw_g, d_inner, "ssd_scan_bwd")
    dzx, dcw, dcb = _ssd_conv_bwd(zx, conv_w_full, conv_b0, [dxs, db_, dc_], dz, d_inner, "ssd_conv_bwd")
    dzx = _dzx_finish(dzx, ddt, zx_dim, "ssd_dzx_finish")
    g_in_all = _mm_tn(h0, dzx, BF16, "ssd_in_wgrad", tn=512, tk=2048)
    g_ssd_in = jnp.stack([g_in_all[:, b:b + in_win] for b in in_base], axis=0)
    (grad_handle["ssd_in_w0"],), token = _xfer_start([g_ssd_in], "ssd_in_grad_start", gather=False)
    dh0 = _mm_nt(dzx, w_in_all, F32, "ssd_in_bwd", tk=dzx.shape[1] // 2, after=(token,))
    grad_x, dsh, dsc, dnw_mix0 = _norm_mod_bwd(dh0, x0, vec(mix_norm_w[0]), sc_m[0], dx1, "l0_mix_norm_bwd")
    dmod[0][0], dmod[0][1] = dsh, dsc

    small["mix_norm_w"] = jnp.concatenate([dnw_mix0, dnw_mix1], axis=0)
    small["mlp_norm_w"] = jnp.concatenate([dnw_mlp0, dnw_mlp1], axis=0)
    small["ssd_conv_w"] = dcw
    small["ssd_conv_b"] = dcb
    small["ssd_dt_bias"] = jnp.sum(dbias, axis=0)[:, :n_heads]
    small["ssd_A_log"] = jnp.sum(dalog, axis=0)[:, :n_heads]
    small["ssd_D"] = jnp.sum(dd_, axis=0)[:, :n_heads]
    small["ssd_norm_w"] = dnw_ssd
    small["sc_conv_w"] = dscw
    small["dmod"] = jnp.concatenate([jnp.concatenate(dmod[i], axis=1) for i in range(2)], axis=0)

    small_order = ["dmod", "mix_norm_w", "mlp_norm_w", "ssd_conv_w", "ssd_conv_b", "ssd_dt_bias", "ssd_A_log",
                   "ssd_D", "ssd_norm_w", "sc_conv_w", "final_norm_w"]
    flat = jnp.concatenate([small[k].reshape(-1) for k in small_order])
    n_small = flat.shape[0]
    n_small_pad = -(-n_small // 1024) * 1024
    flat = jnp.pad(flat, (0, n_small_pad - n_small)).reshape(n_small_pad // LANES, LANES)
    (small_handle,), small_token = _xfer_start([flat], "small_grads_start", gather=True)
    offs, o = {}, 0
    for k in small_order:
        offs[k] = (o, small[k].size, small[k].shape)
        o += small[k].size

    def small_parts(k):
        o, n, shape = offs[k]
        return small_all[:, o:o + n].reshape((N_DEV,) + shape)

    out_g, out_d, out_m, out_v = {}, {}, {}, {}

    layer_res = {}

    def big_update(name, i, after):
        (parts,) = _xfer_wait([grad_handle[f"{name}{i}"]], after, f"grads_wait_{name}_{i}", gather=False)
        res = _adamw_sum(parts, weights[name], moms[name], vars_[name], i, f"adamw_{name}_{i}",
                         prev=layer_res.get(name), window_off=in_off if name == "ssd_in_w" else None)
        layer_res[name] = res
        return res[1]

    chain = small_token
    for name, i in [("mlp_down", 1), ("mlp_up", 1), ("sc_out_w", 0), ("sc_in_w", 0), ("mlp_down", 0),
                    ("mlp_up", 0), ("ssd_out_w", 0), ("ssd_in_w", 0)]:
        chain = big_update(name, i, chain)
    (small_all,) = _xfer_wait([small_handle], chain, "small_grads_wait", gather=True)
    small_all = small_all.reshape(N_DEV, n_small_pad)

    dmod_all = small_parts("dmod")
    dmod_loc = lax.dynamic_slice_in_dim(dmod_all, me * n_mod, n_mod, axis=2)
    dmod_pad = jnp.pad(jnp.transpose(dmod_loc, (1, 0, 2)), ((0, 0), (0, 16 - N_DEV), (0, 0)))
    out_g["ada_w"], out_d["ada_w"], out_m["ada_w"], out_v["ada_w"] = _ada_adamw(
        cond_pad, dmod_pad, ada_w, m_ada_w, v_ada_w, "adamw_ada_w")

    pieces = []
    pieces.append(("ada_b", dmod_all.reshape(N_DEV, -1)))
    for k in ["mix_norm_w", "mlp_norm_w", "ssd_conv_b", "ssd_dt_bias", "ssd_A_log", "ssd_D", "ssd_norm_w",
              "final_norm_w"]:
        pieces.append((k, small_parts(k).reshape(N_DEV, -1)))
    n_cw = ssd_conv_w.shape[2]
    pieces.append(("ssd_conv_w", lax.dynamic_slice_in_dim(small_parts("ssd_conv_w"), me * n_cw, n_cw, axis=2)
                   .reshape(N_DEV, -1)))
    n_scw = sc_conv_w.shape[2]
    pieces.append(("sc_conv_w", lax.dynamic_slice_in_dim(small_parts("sc_conv_w"), me * n_scw, n_scw, axis=2)
                   .reshape(N_DEV, -1)))
    n_tot = sum(p.shape[1] for _, p in pieces)
    n_tot_pad = -(-n_tot // 1024) * 1024

    def pack(arrs, lead=()):
        f = jnp.concatenate(arrs, axis=-1)
        f = jnp.pad(f, [(0, 0)] * len(lead) + [(0, n_tot_pad - n_tot)])
        return f.reshape(lead + (n_tot_pad // LANES, LANES))

    parts_flat = pack([p for _, p in pieces], lead=(N_DEV,))
    w_flat = pack([weights[k].reshape(-1) for k, _ in pieces])
    m_flat = pack([moms[k].reshape(-1) for k, _ in pieces])
    v_flat = pack([vars_[k].reshape(-1) for k, _ in pieces])
    res = _adamw_sum(parts_flat, w_flat[None], m_flat[None], v_flat[None], 0, "adamw_small",
                     tr=n_tot_pad // LANES)
    o = 0
    for k, p in pieces:
        n = p.shape[1]
        for r, dst in zip(res, (out_g, out_d, out_m, out_v)):
            dst[k] = r.reshape(-1)[o:o + n].reshape(weights[k].shape)
        o += n
    for name, res4 in layer_res.items():
        for r, dst in zip(res4, (out_g, out_d, out_m, out_v)):
            dst[name] = r

    return (loss, grad_x[None], *[out_g[k] for k in names], *[out_d[k] for k in names],
            *[out_m[k] for k in names], *[out_v[k] for k in names])
```

```python
import functools

import jax
import jax.numpy as jnp
from jax import lax
from jax.experimental import pallas as pl
from jax.experimental.pallas import tpu as pltpu

F32 = jnp.float32
BF16 = jnp.bfloat16
N_DEV = 8
MESH_AXES = ("x", "y", "c")
MESH = pl.DeviceIdType.MESH

NORM_EPS = 1e-5
SSD_G = 4
SSD_P = 64
SSD_N = 128
SSD_CHUNK = 128
SSD_K = 4
SC_K = 3
LANES = 128

ADAM_LR = 0.001
ADAM_B1 = 0.9
ADAM_B2 = 0.999
ADAM_EPS = 1e-08
ADAM_WD = 0.01
ADAM_STEP = 10

VMEM_LIMIT = 56 * 1024 * 1024


def _pcall(body, **kw):
    return pl.pallas_call(body, **kw)


def _cparams(sem=None):
    if sem is None:
        return pltpu.CompilerParams(vmem_limit_bytes=VMEM_LIMIT)
    return pltpu.CompilerParams(dimension_semantics=sem, vmem_limit_bytes=VMEM_LIMIT)


def _my_index():
    return 4 * lax.axis_index("x") + 2 * lax.axis_index("y") + lax.axis_index("c")


_PEER_MASKS = [(0, 0, 1), (0, 1, 0), (0, 1, 1), (1, 0, 0), (1, 0, 1), (1, 1, 0), (1, 1, 1)]


def _peers():
    x, y, c = lax.axis_index("x"), lax.axis_index("y"), lax.axis_index("c")
    out = []
    for mx, my, mc in _PEER_MASKS:
        px = (1 - x) if mx else x
        py = (1 - y) if my else y
        pc = (1 - c) if mc else c
        out.append(((px, py, pc), 4 * px + 2 * py + pc))
    return out


def _exchange(arrs, name, gather):
    n = len(arrs)
    n_peer = N_DEV - 1

    def body(*refs):
        ins, outs = refs[:n], refs[n:2 * n]
        send_sems, recv_sems, local_sems = refs[2 * n:]
        me = _my_index()
        peers = _peers()
        started = []
        for a in range(n):
            src_own = ins[a] if gather else ins[a].at[me]
            own = pltpu.make_async_copy(src_own, outs[a].at[me], local_sems.at[a])
            own.start()
            started.append(own)
        sends = []
        for a in range(n):
            for k, (peer, pidx) in enumerate(peers):
                src = ins[a] if gather else ins[a].at[pidx]
                cp = pltpu.make_async_remote_copy(
                    src_ref=src, dst_ref=outs[a].at[me],
                    send_sem=send_sems.at[a * n_peer + k], recv_sem=recv_sems.at[a * n_peer + k],
                    device_id=peer, device_id_type=MESH)
                cp.start()
                sends.append(cp)
        for a in range(n):
            for k, (peer, pidx) in enumerate(peers):
                src = ins[a] if gather else ins[a].at[pidx]
                pltpu.make_async_remote_copy(
                    src_ref=src, dst_ref=outs[a].at[pidx],
                    send_sem=send_sems.at[a * n_peer + k], recv_sem=recv_sems.at[a * n_peer + k],
                    device_id=peer, device_id_type=MESH).wait_recv()
        for cp in sends:
            cp.wait_send()
        for own in started:
            own.wait()

    if gather:
        out_shape = [jax.ShapeDtypeStruct((N_DEV,) + a.shape, a.dtype) for a in arrs]
    else:
        out_shape = [jax.ShapeDtypeStruct(a.shape, a.dtype) for a in arrs]
    any_spec = pl.BlockSpec(memory_space=pl.ANY)
    outs = _pcall(
        body, name=name, out_shape=out_shape,
        in_specs=[any_spec] * n, out_specs=[any_spec] * n,
        scratch_shapes=[pltpu.SemaphoreType.DMA((n * n_peer,)), pltpu.SemaphoreType.DMA((n * n_peer,)),
                        pltpu.SemaphoreType.DMA((n,))],
        compiler_params=pltpu.CompilerParams(has_side_effects=True),
    )(*arrs)
    return list(outs)


_HBM = pl.BlockSpec(memory_space=pltpu.HBM)
_SEM = pl.BlockSpec(memory_space=pltpu.SEMAPHORE)
_DATAFLOW = pltpu.SideEffectType.DATAFLOW_SIDE_EFFECTING


_ALL_PEERS = tuple(range(N_DEV - 1))
_SAME_CORE_PEERS = (0, 1, 3, 5)
_OTHER_CHIPS = (1, 3, 5)


def _xfer_start(arrs, name, gather, via_sibling=()):
    n = len(arrs)
    n_peer = N_DEV - 1
    peer_ks = [_SAME_CORE_PEERS if a in via_sibling else _ALL_PEERS for a in range(n)]

    def body(*refs):
        ins, lands = refs[:n], refs[n:2 * n]
        sems = refs[2 * n:5 * n]
        token = refs[-1]
        me = _my_index()
        peers = _peers()
        for a in range(n):
            send_sems, recv_sems, loc_sem = sems[3 * a:3 * a + 3]
            src_own = ins[a] if gather else ins[a].at[me]
            pltpu.make_async_copy(src_own, lands[a].at[me], loc_sem).start()
            for k in peer_ks[a]:
                peer, pidx = peers[k]
                src = ins[a] if gather else ins[a].at[pidx]
                pltpu.make_async_remote_copy(
                    src_ref=src, dst_ref=lands[a].at[me], send_sem=send_sems.at[k], recv_sem=recv_sems.at[k],
                    device_id=peer, device_id_type=MESH).start()
        token[...] = jnp.zeros_like(token)

    land_shapes = [((N_DEV,) + a.shape) if gather else a.shape for a in arrs]
    out_shape, out_specs = [], []
    for _ in range(n):
        out_shape += [pltpu.SemaphoreType.DMA((n_peer,)), pltpu.SemaphoreType.DMA((n_peer,)),
                      pltpu.SemaphoreType.DMA(())]
        out_specs += [_SEM, _SEM, _SEM]
    out_shape += [pltpu.HBM(a.shape, a.dtype) for a in arrs]
    out_shape += [pltpu.HBM(s, a.dtype) for s, a in zip(land_shapes, arrs)]
    out_shape += [jax.ShapeDtypeStruct((8, LANES), F32)]
    out_specs += [_HBM] * (2 * n) + [pl.BlockSpec(memory_space=pltpu.VMEM)]
    aliases = {}
    for a in range(n):
        aliases[a] = 3 * n + a
        aliases[n + a] = 4 * n + a
    operands = [pltpu.with_memory_space_constraint(a, pltpu.HBM) for a in arrs]
    operands += [pltpu.with_memory_space_constraint(lax.empty(s, a.dtype), pltpu.HBM)
                 for s, a in zip(land_shapes, arrs)]
    outs = _pcall(
        body, name=name, out_shape=tuple(out_shape), in_specs=[_HBM] * (2 * n), out_specs=tuple(out_specs),
        input_output_aliases=aliases,
        compiler_params=pltpu.CompilerParams(has_side_effects=_DATAFLOW),
    )(*operands)
    handles = []
    for a in range(n):
        handles.append((outs[3 * n + a], outs[4 * n + a], outs[3 * a], outs[3 * a + 1], outs[3 * a + 2],
                        peer_ks[a]))
    return handles, outs[-1]


def _xfer_wait(handles, after, name, gather):
    n = len(handles)
    peer_ks = [h[5] for h in handles]

    def body(*refs):
        me = _my_index()
        peers = _peers()
        for a in range(n):
            src_ref, land_ref, send_ref, recv_ref, loc_ref = refs[5 * a:5 * a + 5]
            src_own = src_ref if gather else src_ref.at[me]
            pltpu.make_async_copy(src_own, land_ref.at[me], loc_ref).wait()
            for k in peer_ks[a]:
                peer, pidx = peers[k]
                src = src_ref if gather else src_ref.at[pidx]
                cp = pltpu.make_async_remote_copy(
                    src_ref=src, dst_ref=land_ref.at[pidx], send_sem=send_ref.at[k], recv_sem=recv_ref.at[k],
                    device_id=peer, device_id_type=MESH)
                cp.wait_send()
                cp.wait_recv()

    operands, in_specs, out_shape, aliases = [], [], [], {}
    for a, h in enumerate(handles):
        operands += list(h[:5])
        in_specs += [_HBM, _HBM, _SEM, _SEM, _SEM]
        out_shape += [pltpu.HBM(h[0].shape, h[0].dtype), pltpu.HBM(h[1].shape, h[1].dtype)]
        aliases[5 * a] = 2 * a
        aliases[5 * a + 1] = 2 * a + 1
    outs = _pcall(
        body, name=name, out_shape=tuple(out_shape),
        in_specs=in_specs + [pl.BlockSpec(memory_space=pl.ANY)], out_specs=tuple([_HBM] * (2 * n)),
        input_output_aliases=aliases,
        compiler_params=pltpu.CompilerParams(has_side_effects=_DATAFLOW),
    )(*operands, after)
    return [outs[2 * a + 1] for a in range(n)]


def _sibling_forward(lands, name):
    n = len(lands)
    n_fwd = len(_OTHER_CHIPS)

    def body(*refs):
        ins, bufs = refs[:n], refs[n:2 * n]
        send_sems, recv_sems = refs[2 * n:]
        x, y, c = lax.axis_index("x"), lax.axis_index("y"), lax.axis_index("c")
        sibling = (x, y, 1 - c)
        peers = _peers()
        sends = []
        for a in range(n):
            for j, k in enumerate(_OTHER_CHIPS):
                slot = peers[k][1]
                cp = pltpu.make_async_remote_copy(
                    src_ref=ins[a].at[slot], dst_ref=bufs[a].at[slot],
                    send_sem=send_sems.at[a * n_fwd + j], recv_sem=recv_sems.at[a * n_fwd + j],
                    device_id=sibling, device_id_type=MESH)
                cp.start()
                sends.append(cp)
        for a in range(n):
            for j, k in enumerate(_OTHER_CHIPS):
                (px, py, pc), slot = peers[k]
                theirs = 4 * px + 2 * py + (1 - pc)
                pltpu.make_async_remote_copy(
                    src_ref=ins[a].at[slot], dst_ref=bufs[a].at[theirs],
                    send_sem=send_sems.at[a * n_fwd + j], recv_sem=recv_sems.at[a * n_fwd + j],
                    device_id=sibling, device_id_type=MESH).wait_recv()
        for cp in sends:
            cp.wait_send()

    any_spec = pl.BlockSpec(memory_space=pl.ANY)
    outs = _pcall(
        body, name=name, out_shape=[jax.ShapeDtypeStruct(a.shape, a.dtype) for a in lands],
        in_specs=[any_spec] * n, out_specs=[any_spec] * n,
        input_output_aliases={a: a for a in range(n)},
        scratch_shapes=[pltpu.SemaphoreType.DMA((n * n_fwd,)), pltpu.SemaphoreType.DMA((n * n_fwd,))],
        compiler_params=pltpu.CompilerParams(has_side_effects=True),
    )(*lands)
    return list(outs)


_DIMS = {"nn": (((1,), (0,)), ((), ())), "nt": (((1,), (1,)), ((), ())), "tn": (((0,), (0,)), ((), ()))}


def _dot(a, b, mode="nn"):
    return lax.dot_general(a, b, _DIMS[mode], preferred_element_type=F32)


def _mm(a, b, *, mode, grid, a_spec, b_spec, out_shape, out_specs, acc_shape, epilogue, name,
        extra=(), extra_specs=(), after=()):
    nk = grid[2]
    n_extra = len(extra)
    n_in = 2 + n_extra + len(after)

    def body_single(*refs):
        a_ref, b_ref = refs[0], refs[1]
        epilogue(_dot(a_ref[...], b_ref[...], mode), refs[2:2 + n_extra], refs[n_in:])

    def body_acc(*refs):
        a_ref, b_ref = refs[0], refs[1]
        ex = refs[2:2 + n_extra]
        outs = refs[n_in:-1]
        acc = refs[-1]
        k = pl.program_id(2)

        @pl.when(k == 0)
        def _():
            acc[...] = jnp.zeros_like(acc)

        acc[...] += _dot(a_ref[...], b_ref[...], mode)

        @pl.when(k == nk - 1)
        def _():
            epilogue(acc[...], ex, outs)

    return _pcall(
        body_single if nk == 1 else body_acc, name=name, grid=grid, out_shape=out_shape,
        in_specs=[a_spec, b_spec] + list(extra_specs) + [pl.BlockSpec(memory_space=pl.ANY)] * len(after),
        out_specs=out_specs,
        scratch_shapes=[] if nk == 1 else [pltpu.VMEM(acc_shape, F32)],
        compiler_params=_cparams(("parallel", "parallel", "arbitrary")),
    )(a, b, *extra, *after)


def _ep_store(dtype):
    def ep(acc, ex, outs):
        outs[0][...] = acc.astype(dtype)
    return ep


def _ep_relu2(acc, ex, outs):
    outs[0][...] = acc.astype(BF16)
    r = jnp.maximum(acc, 0.0)
    outs[1][...] = (r * r).astype(BF16)


def _ep_relu2_bwd(acc, ex, outs):
    u = ex[0][...].astype(F32)
    outs[0][...] = (acc * (2.0 * jnp.maximum(u, 0.0))).astype(BF16)


def _tile(n, want):
    t = min(n, want)
    while n % t:
        t //= 2
    return t


def _mm_nn(a, w, out_dtype, name, tm=1024, tn=1024, tk=1024, epilogue=None, out_dtypes=None):
    M, K = a.shape
    N = w.shape[1]
    tm, tn, tk = _tile(M, tm), _tile(N, tn), _tile(K, tk)
    out_dtypes = out_dtypes or [out_dtype]
    return _mm(a, w, mode="nn", grid=(M // tm, N // tn, K // tk),
               a_spec=pl.BlockSpec((tm, tk), lambda i, j, k: (i, k)),
               b_spec=pl.BlockSpec((tk, tn), lambda i, j, k: (k, j)),
               out_shape=[jax.ShapeDtypeStruct((M, N), d) for d in out_dtypes],
               out_specs=[pl.BlockSpec((tm, tn), lambda i, j, k: (i, j)) for _ in out_dtypes],
               acc_shape=(tm, tn), epilogue=epilogue or _ep_store(out_dtype), name=name)


def _mm_nn_blocked(a, wg, name, epilogue, out_dtypes, tm=2048):
    M, K = a.shape
    n = wg.shape[2]
    tm = _tile(M, tm)
    return _mm(a, wg, mode="nn", grid=(M // tm, N_DEV, 1),
               a_spec=pl.BlockSpec((tm, K), lambda i, j, k: (i, 0)),
               b_spec=pl.BlockSpec((None, K, n), lambda i, j, k: (j, 0, 0)),
               out_shape=[jax.ShapeDtypeStruct((M, N_DEV * n), d) for d in out_dtypes],
               out_specs=[pl.BlockSpec((tm, n), lambda i, j, k: (i, j)) for _ in out_dtypes],
               acc_shape=(tm, n), epilogue=epilogue, name=name)


def _mm_nt(a, w, out_dtype, name, tm=1024, tn=1024, tk=1024, epilogue=None, extra=(), extra_specs=(),
           after=()):
    M, K = a.shape
    N = w.shape[0]
    tm, tn, tk = _tile(M, tm), _tile(N, tn), _tile(K, tk)
    if extra and not extra_specs:
        extra_specs = [pl.BlockSpec((tm, tn), lambda i, j, k: (i, j)) for _ in extra]
    return _mm(a, w, mode="nt", grid=(M // tm, N // tn, K // tk),
               a_spec=pl.BlockSpec((tm, tk), lambda i, j, k: (i, k)),
               b_spec=pl.BlockSpec((tn, tk), lambda i, j, k: (j, k)),
               out_shape=[jax.ShapeDtypeStruct((M, N), out_dtype)],
               out_specs=[pl.BlockSpec((tm, tn), lambda i, j, k: (i, j))],
               acc_shape=(tm, tn), epilogue=epilogue or _ep_store(out_dtype), name=name,
               extra=extra, extra_specs=extra_specs, after=after)[0]


def _mm_nt_blocked(a, wg, out_dtype, name, tm=1024, after=()):
    M = a.shape[0]
    kout, n = wg.shape[1], wg.shape[2]
    tm = _tile(M, tm)
    return _mm(a, wg, mode="nt", grid=(M // tm, 1, N_DEV),
               a_spec=pl.BlockSpec((tm, n), lambda i, j, k: (i, k)),
               b_spec=pl.BlockSpec((None, kout, n), lambda i, j, k: (k, 0, 0)),
               out_shape=[jax.ShapeDtypeStruct((M, kout), out_dtype)],
               out_specs=[pl.BlockSpec((tm, kout), lambda i, j, k: (i, 0))],
               acc_shape=(tm, kout), epilogue=_ep_store(out_dtype), name=name, after=after)[0]


def _mm_tn(a, b, out_dtype, name, tm=1024, tn=1024, tk=1024):
    K, M = a.shape
    N = b.shape[1]
    tm, tn, tk = _tile(M, tm), _tile(N, tn), _tile(K, tk)
    return _mm(a, b, mode="tn", grid=(M // tm, N // tn, K // tk),
               a_spec=pl.BlockSpec((tk, tm), lambda i, j, k: (k, i)),
               b_spec=pl.BlockSpec((tk, tn), lambda i, j, k: (k, j)),
               out_shape=[jax.ShapeDtypeStruct((M, N), out_dtype)],
               out_specs=[pl.BlockSpec((tm, tn), lambda i, j, k: (i, j))],
               acc_shape=(tm, tn), epilogue=_ep_store(out_dtype), name=name)[0]


def _mm_tn_blocked(a, b, out_dtype, name, tm=1024, tk=2048):
    K, M = a.shape
    n = b.shape[1] // N_DEV
    tm, tk = _tile(M, tm), _tile(K, tk)
    return _mm(a, b, mode="tn", grid=(M // tm, N_DEV, K // tk),
               a_spec=pl.BlockSpec((tk, tm), lambda i, j, k: (k, i)),
               b_spec=pl.BlockSpec((tk, n), lambda i, j, k: (k, j)),
               out_shape=[jax.ShapeDtypeStruct((N_DEV, M, n), out_dtype)],
               out_specs=[pl.BlockSpec((None, tm, n), lambda i, j, k: (j, i, 0))],
               acc_shape=(tm, n), epilogue=_ep_store(out_dtype), name=name)[0]


def _window_geometry(ws):
    base = [(ws * k // LANES) * LANES for k in range(N_DEV)]
    off = [ws * k - base[k] for k in range(N_DEV)]
    win = -(-(max(off) + ws) // LANES) * LANES
    return base, off, win


def _shards_to_columns(xg, base, off, win, n_out, name, tr=256):
    R, ws = xg.shape[1], xg.shape[2]
    tr = _tile(R, tr)
    nb_win = win // LANES

    def body(x_ref, o_ref, frame_ref):
        written = set()
        frame_ref[...] = jnp.zeros_like(frame_ref)
        for k in range(N_DEV):
            frame_ref[:, 0:ws] = x_ref[k].astype(F32)
            window = frame_ref[...]
            if off[k]:
                window = pltpu.roll(window, off[k], 1)
            for i in range(nb_win):
                b = base[k] // LANES + i
                if b * LANES >= n_out:
                    continue
                cols = slice(b * LANES, (b + 1) * LANES)
                blk = window[:, i * LANES:(i + 1) * LANES]
                if b in written:
                    blk = blk + o_ref[:, cols].astype(F32)
                o_ref[:, cols] = blk.astype(o_ref.dtype)
                written.add(b)
        for b in range(n_out // LANES):
            if b not in written:
                o_ref[:, b * LANES:(b + 1) * LANES] = jnp.zeros((tr, LANES), o_ref.dtype)

    return _pcall(
        body, name=name, grid=(R // tr,), out_shape=jax.ShapeDtypeStruct((R, n_out), xg.dtype),
        in_specs=[pl.BlockSpec((N_DEV, tr, ws), lambda i: (0, i, 0))],
        out_specs=pl.BlockSpec((tr, n_out), lambda i: (i, 0)),
        scratch_shapes=[pltpu.VMEM((tr, win), F32)],
        compiler_params=_cparams(("parallel",)))(xg)


def _sigmoid(x):
    return 1.0 / (1.0 + jnp.exp(-x))


def _row_spec(tm, d):
    return pl.BlockSpec((tm, d), lambda i: (i, 0))


def _vec_spec(d):
    return pl.BlockSpec((1, d), lambda i: (0, 0))


def _norm_mod_fwd(x, y, gate, nw, scale, shift, name, tm=256):
    L, D = x.shape
    tm = _tile(L, tm)
    has_res = y is not None

    def body(*refs):
        if has_res:
            x_ref, y_ref, g_ref, nw_ref, sc_ref, sh_ref, xo_ref, h_ref = refs
            xn = x_ref[...] + g_ref[...] * y_ref[...]
            xo_ref[...] = xn
        else:
            x_ref, nw_ref, sc_ref, sh_ref, h_ref = refs
            xn = x_ref[...]
        rstd = lax.rsqrt(jnp.mean(xn * xn, axis=-1, keepdims=True) + NORM_EPS)
        h = xn * rstd * nw_ref[...] * (1.0 + sc_ref[...]) + sh_ref[...]
        h_ref[...] = h.astype(BF16)

    row, vec = _row_spec(tm, D), _vec_spec(D)
    if has_res:
        ins, in_specs = (x, y, gate, nw, scale, shift), [row, row, vec, vec, vec, vec]
        out_shape = [jax.ShapeDtypeStruct((L, D), F32), jax.ShapeDtypeStruct((L, D), BF16)]
        out_specs = [row, row]
    else:
        ins, in_specs = (x, nw, scale, shift), [row, vec, vec, vec]
        out_shape = [jax.ShapeDtypeStruct((L, D), BF16)]
        out_specs = [row]
    outs = _pcall(body, name=name, grid=(L // tm,), out_shape=out_shape, in_specs=in_specs,
                  out_specs=out_specs, compiler_params=_cparams(("parallel",)))(*ins)
    return outs if has_res else (x, outs[0])


def _gated_branch_bwd(dx, branch, y_ref, g_ref, dy_ref, dg_ref):
    if branch is None:
        return
    dy_ref[...] = (g_ref[...] * dx).astype(BF16)
    dg_ref[...] += jnp.sum(dx * y_ref[...], axis=0, keepdims=True)


def _norm_mod_bwd(dh, x, nw, scale, dres, name, branch=None, tm=256):
    L, D = x.shape
    tm = _tile(L, tm)
    nb = 0 if branch is None else 2

    def body(dh_ref, x_ref, nw_ref, sc_ref, dres_ref, *rest):
        y_ref, g_ref = rest[:nb] if nb else (None, None)
        dx_ref, dsh_ref, dsc_ref, dnw_ref = rest[nb:nb + 4]
        dy_ref, dg_ref = rest[nb + 4:] if nb else (None, None)

        @pl.when(pl.program_id(0) == 0)
        def _():
            dsh_ref[...] = jnp.zeros_like(dsh_ref)
            dsc_ref[...] = jnp.zeros_like(dsc_ref)
            dnw_ref[...] = jnp.zeros_like(dnw_ref)
            if nb:
                dg_ref[...] = jnp.zeros_like(dg_ref)

        xv = x_ref[...]
        dh_v = dh_ref[...]
        nw_v = nw_ref[...]
        rstd = lax.rsqrt(jnp.mean(xv * xv, axis=-1, keepdims=True) + NORM_EPS)
        xhat = xv * rstd
        dsh_ref[...] += jnp.sum(dh_v, axis=0, keepdims=True)
        dsc_ref[...] += jnp.sum(dh_v * (xhat * nw_v), axis=0, keepdims=True)
        dr = dh_v * (1.0 + sc_ref[...])
        dnw_ref[...] += jnp.sum(dr * xhat, axis=0, keepdims=True)
        dxh = dr * nw_v
        dx = rstd * (dxh - xhat * jnp.mean(dxh * xhat, axis=-1, keepdims=True)) + dres_ref[...]
        dx_ref[...] = dx
        _gated_branch_bwd(dx, branch, y_ref, g_ref, dy_ref, dg_ref)

    row, vec = _row_spec(tm, D), _vec_spec(D)
    extra_in = [] if branch is None else list(branch)
    return _pcall(
        body, name=name, grid=(L // tm,),
        out_shape=[jax.ShapeDtypeStruct((L, D), F32)] + [jax.ShapeDtypeStruct((1, D), F32)] * 3
        + ([jax.ShapeDtypeStruct((L, D), BF16), jax.ShapeDtypeStruct((1, D), F32)] if nb else []),
        in_specs=[row, row, vec, vec, row] + ([row, vec] if nb else []),
        out_specs=[row, vec, vec, vec] + ([row, vec] if nb else []),
        compiler_params=_cparams(("arbitrary",)))(dh, x, nw, scale, dres, *extra_in)


def _final_loss(x, y, gate, fw, target, name, tm=256):
    L, D = x.shape
    tm = _tile(L, tm)

    def body(x_ref, y_ref, g_ref, fw_ref, t_ref, dx_ref, loss_ref, dfw_ref, dy_ref, dg_ref):
        @pl.when(pl.program_id(0) == 0)
        def _():
            loss_ref[...] = jnp.zeros_like(loss_ref)
            dfw_ref[...] = jnp.zeros_like(dfw_ref)
            dg_ref[...] = jnp.zeros_like(dg_ref)

        xn = x_ref[...] + g_ref[...] * y_ref[...]
        fw_v = fw_ref[...]
        rstd = lax.rsqrt(jnp.mean(xn * xn, axis=-1, keepdims=True) + NORM_EPS)
        xhat = xn * rstd
        diff = xhat * fw_v - t_ref[...]
        loss_ref[...] += jnp.sum(diff * diff, axis=0, keepdims=True)
        dyf = diff * (1.0 / D)
        dfw_ref[...] += jnp.sum(dyf * xhat, axis=0, keepdims=True)
        dxh = dyf * fw_v
        dx = rstd * (dxh - xhat * jnp.mean(dxh * xhat, axis=-1, keepdims=True))
        dx_ref[...] = dx
        _gated_branch_bwd(dx, True, y_ref, g_ref, dy_ref, dg_ref)

    row, vec = _row_spec(tm, D), _vec_spec(D)
    return _pcall(
        body, name=name, grid=(L // tm,),
        out_shape=[jax.ShapeDtypeStruct((L, D), F32), jax.ShapeDtypeStruct((1, D), F32),
                   jax.ShapeDtypeStruct((1, D), F32), jax.ShapeDtypeStruct((L, D), BF16),
                   jax.ShapeDtypeStruct((1, D), F32)],
        in_specs=[row, row, vec, vec, row], out_specs=[row, vec, vec, row, vec],
        compiler_params=_cparams(("arbitrary",)))(x, y, gate, fw, target)


def _shift_down(v, s, row):
    if s == 0:
        return v
    return jnp.where(row >= s, pltpu.roll(v, s, 0), 0.0)


def _shift_up(v, s, row):
    if s == 0:
        return v
    n = v.shape[0]
    return jnp.where(row < n - s, pltpu.roll(v, n - s, 0), 0.0)


def _ssd_conv_fwd(zx, w, b, col0, width, name, cb=128):
    L = zx.shape[0]
    nb = width // cb
    off = col0 // cb

    def body(x_ref, w_ref, b_ref, o_ref):
        xv = x_ref[...]
        row = lax.broadcasted_iota(jnp.int32, xv.shape, 0)
        acc = b_ref[...] + w_ref[SSD_K - 1:SSD_K, :] * xv
        for s in range(1, SSD_K):
            acc = acc + w_ref[SSD_K - 1 - s:SSD_K - s, :] * _shift_down(xv, s, row)
        o_ref[...] = acc * _sigmoid(acc)

    return _pcall(
        body, name=name, grid=(nb,), out_shape=jax.ShapeDtypeStruct((L, width), F32),
        in_specs=[pl.BlockSpec((L, cb), lambda j: (0, off + j)),
                  pl.BlockSpec((SSD_K, cb), lambda j: (0, j)),
                  pl.BlockSpec((1, cb), lambda j: (0, j))],
        out_specs=pl.BlockSpec((L, cb), lambda j: (0, j)),
        compiler_params=_cparams(("parallel",)))(zx, w, b)


def _ssd_conv_bwd(zx, w, b, d_parts, dzx, col0, name, cb=128):
    L = zx.shape[0]
    widths = [p.shape[1] for p in d_parts]
    width = sum(widths)
    nb = width // cb
    off = col0 // cb
    starts = [sum(widths[:i]) // cb for i in range(len(d_parts))]
    counts = [wd // cb for wd in widths]

    def body(x_ref, w_ref, b_ref, *rest):
        d_refs = rest[:len(d_parts)]
        dx_ref, dw_ref, db_ref = rest[len(d_parts) + 1:]
        j = pl.program_id(0)
        d_val = d_refs[-1][...]
        for i in range(len(d_parts) - 2, -1, -1):
            d_val = jnp.where(j < starts[i + 1], d_refs[i][...], d_val)
        xv = x_ref[...]
        row = lax.broadcasted_iota(jnp.int32, xv.shape, 0)
        shifted = [_shift_down(xv, s, row) for s in range(SSD_K)]
        acc = b_ref[...] + w_ref[SSD_K - 1:SSD_K, :] * xv
        for s in range(1, SSD_K):
            acc = acc + w_ref[SSD_K - 1 - s:SSD_K - s, :] * shifted[s]
        sig = _sigmoid(acc)
        dpre = d_val * (sig * (1.0 + acc * (1.0 - sig)))
        db_ref[...] = jnp.sum(dpre, axis=0, keepdims=True)
        dx = w_ref[SSD_K - 1:SSD_K, :] * dpre
        for s in range(SSD_K):
            dw_ref[SSD_K - 1 - s:SSD_K - s, :] = jnp.sum(dpre * shifted[s], axis=0, keepdims=True)
            if s:
                dx = dx + w_ref[SSD_K - 1 - s:SSD_K - s, :] * _shift_up(dpre, s, row)
        dx_ref[...] = dx.astype(BF16)

    def part_spec(i):
        return pl.BlockSpec((L, cb), lambda j: (0, jnp.clip(j - starts[i], 0, counts[i] - 1)))

    return _pcall(
        body, name=name, grid=(nb,),
        out_shape=[jax.ShapeDtypeStruct(dzx.shape, BF16), jax.ShapeDtypeStruct((SSD_K, width), F32),
                   jax.ShapeDtypeStruct((1, width), F32)],
        in_specs=[pl.BlockSpec((L, cb), lambda j: (0, off + j)),
                  pl.BlockSpec((SSD_K, cb), lambda j: (0, j)),
                  pl.BlockSpec((1, cb), lambda j: (0, j))]
        + [part_spec(i) for i in range(len(d_parts))] + [pl.BlockSpec(memory_space=pl.ANY)],
        out_specs=[pl.BlockSpec((L, cb), lambda j: (0, off + j)),
                   pl.BlockSpec((SSD_K, cb), lambda j: (0, j)),
                   pl.BlockSpec((1, cb), lambda j: (0, j))],
        input_output_aliases={3 + len(d_parts): 0},
        compiler_params=_cparams(("parallel",)))(zx, w, b, *d_parts, dzx)


def _dzx_finish(dzx, ddt, col0, name, tl=512):
    G, L, _ = ddt.shape
    tail = dzx.shape[1] - col0
    tl = _tile(L, tl)

    def body(ddt_ref, dzx_ref, o_ref):
        s = ddt_ref[0]
        for g in range(1, G):
            s = s + ddt_ref[g]
        o_ref[:, 0:LANES] = s.astype(o_ref.dtype)
        if tail > LANES:
            o_ref[:, LANES:] = jnp.zeros((tl, tail - LANES), o_ref.dtype)

    return _pcall(
        body, name=name, grid=(L // tl,), out_shape=jax.ShapeDtypeStruct(dzx.shape, dzx.dtype),
        in_specs=[pl.BlockSpec((G, tl, LANES), lambda i: (0, i, 0)), pl.BlockSpec(memory_space=pl.ANY)],
        out_specs=pl.BlockSpec((tl, tail), lambda i: (i, col0 // tail)),
        input_output_aliases={1: 0},
        compiler_params=_cparams(("parallel",)))(ddt, dzx)


def _sc_conv_fwd(proj, w, name, cb=128):
    L = proj.shape[0]
    width = proj.shape[1] // 3
    nb = width // cb

    def body(b_ref, c_ref, x_ref, w_ref, o_ref):
        q = c_ref[...] * x_ref[...]
        row = lax.broadcasted_iota(jnp.int32, q.shape, 0)
        acc = w_ref[SC_K - 1:SC_K, :] * q
        for s in range(1, SC_K):
            acc = acc + w_ref[SC_K - 1 - s:SC_K - s, :] * _shift_down(q, s, row)
        o_ref[...] = (b_ref[...] * acc).astype(BF16)

    return _pcall(
        body, name=name, grid=(nb,), out_shape=jax.ShapeDtypeStruct((L, width), BF16),
        in_specs=[pl.BlockSpec((L, cb), lambda j: (0, j)),
                  pl.BlockSpec((L, cb), lambda j: (0, nb + j)),
                  pl.BlockSpec((L, cb), lambda j: (0, 2 * nb + j)),
                  pl.BlockSpec((SC_K, cb), lambda j: (0, j))],
        out_specs=pl.BlockSpec((L, cb), lambda j: (0, j)),
        compiler_params=_cparams(("parallel",)))(proj, proj, proj, w)


def _sc_conv_bwd(proj, w, dy, name, cb=128):
    L = proj.shape[0]
    width = proj.shape[1] // 3
    nb = width // cb

    def body(b_ref, c_ref, x_ref, w_ref, dy_ref, db_ref, dc_ref, dxv_ref, dw_ref):
        cg, xv, dyv = c_ref[...], x_ref[...], dy_ref[...]
        q = cg * xv
        row = lax.broadcasted_iota(jnp.int32, q.shape, 0)
        shifted = [_shift_down(q, s, row) for s in range(SC_K)]
        conv = w_ref[SC_K - 1:SC_K, :] * q
        for s in range(1, SC_K):
            conv = conv + w_ref[SC_K - 1 - s:SC_K - s, :] * shifted[s]
        db_ref[...] = (dyv * conv).astype(BF16)
        dconv = dyv * b_ref[...]
        dq = w_ref[SC_K - 1:SC_K, :] * dconv
        for s in range(SC_K):
            dw_ref[SC_K - 1 - s:SC_K - s, :] = jnp.sum(dconv * shifted[s], axis=0, keepdims=True)
            if s:
                dq = dq + w_ref[SC_K - 1 - s:SC_K - s, :] * _shift_up(dconv, s, row)
        dc_ref[...] = (dq * xv).astype(BF16)
        dxv_ref[...] = (dq * cg).astype(BF16)

    blk = pl.BlockSpec((L, cb), lambda j: (0, j))
    wblk = pl.BlockSpec((SC_K, cb), lambda j: (0, j))
    return _pcall(
        body, name=name, grid=(nb,),
        out_shape=[jax.ShapeDtypeStruct((L, width), BF16)] * 3 + [jax.ShapeDtypeStruct((SC_K, width), F32)],
        in_specs=[blk, pl.BlockSpec((L, cb), lambda j: (0, nb + j)),
                  pl.BlockSpec((L, cb), lambda j: (0, 2 * nb + j)), wblk, blk],
        out_specs=[blk, blk, blk, wblk],
        compiler_params=_cparams(("parallel",)))(proj, proj, proj, w, dy)


def _split3(v):
    hi = v.astype(BF16)
    r1 = v - hi.astype(F32)
    mid = r1.astype(BF16)
    lo = (r1 - mid.astype(F32)).astype(BF16)
    return hi, mid, lo


def _dot_exact01(t01, v):
    hi, mid, lo = _split3(v)
    return _dot(t01, hi) + _dot(t01, mid) + _dot(t01, lo)


def _lane_col(v, lane, h):
    return jnp.sum(jnp.where(lane == h, v, 0.0), axis=1, keepdims=True)


def _sum_all(v):
    return jnp.sum(jnp.sum(v, axis=1, keepdims=True), axis=0, keepdims=True)


def _softplus(x):
    return jnp.maximum(x, 0.0) + jnp.log1p(jnp.exp(-jnp.abs(x)))


def _ssd_common(dt_ref, bias_ref, alog_ref, b_ref, c_ref, cst_ref, heads):
    c_sz = SSD_CHUNK
    lane = lax.broadcasted_iota(jnp.int32, (c_sz, LANES), 1)
    row = lax.broadcasted_iota(jnp.int32, (c_sz, LANES), 0)
    valid = lane < heads
    raw = dt_ref[...] + bias_ref[...]
    dt = _softplus(raw)
    a_row = -jnp.exp(alog_ref[...])
    a = jnp.where(valid, dt * a_row, 0.0)
    tri = (row >= lane).astype(BF16)
    cs = _dot_exact01(tri, a)
    cst_ref[...] = cs.T
    last_row = jnp.sum(a, axis=0, keepdims=True)
    bb = b_ref[...].astype(BF16)
    cb = c_ref[...].astype(BF16)
    scores = _dot(cb, bb, "nt")
    return dict(lane=lane, row=row, valid=valid, raw=raw, dt=dt, a_row=a_row, cs=cs,
                last_row=last_row, bb=bb, cb=cb, scores=scores, causal=row >= lane, lo=lane < SSD_P)


def _pair_terms(q, cst_ref, h0):
    lane, lo = q["lane"], q["lo"]
    out = {}
    cols, dts, lasts, lms = [], [], [], []
    lane1 = lax.broadcasted_iota(jnp.int32, (1, LANES), 1)
    for h in (h0, h0 + 1):
        col = _lane_col(q["cs"], lane, h)
        rowv = cst_ref[pl.ds(h, 1), :]
        lms.append(jnp.exp(jnp.where(q["causal"], col - rowv, -1e30)))
        cols.append(col)
        dts.append(_lane_col(q["dt"], lane, h))
        lasts.append(jnp.sum(jnp.where(lane1 == h, q["last_row"], 0.0), axis=1, keepdims=True))
    out["lm"] = lms
    out["cols"] = cols
    out["lasts"] = lasts
    out["dt_b"] = jnp.where(lo, dts[0], dts[1])
    out["e_b"] = jnp.where(lo, jnp.exp(cols[0]), jnp.exp(cols[1]))
    out["dec_cols"] = [jnp.exp(lasts[0] - cols[0]), jnp.exp(lasts[1] - cols[1])]
    out["dec_b"] = jnp.where(lo, out["dec_cols"][0], out["dec_cols"][1])
    lo1 = lane1 < SSD_P
    out["explast"] = [jnp.exp(lasts[0]), jnp.exp(lasts[1])]
    out["explast_b"] = jnp.where(lo1, out["explast"][0], out["explast"][1])
    return out


def _ssd_fwd(zx, xc, bias_p, alog_p, d_lane, nw, d_inner, name):
    L = zx.shape[0]
    nc = L // SSD_CHUNK
    gw = d_inner // SSD_G
    heads = gw // SSD_P
    n_pair = heads // 2
    bc0 = d_inner // LANES
    dt0 = (2 * d_inner + 2 * SSD_G * SSD_N) // LANES

    def body(z_ref, xs_ref, b_ref, c_ref, dt_ref, bias_ref, alog_ref, dl_ref, nw_ref,
             y_ref, yn_ref, prev_ref, s_ref, cst_ref):
        @pl.when(pl.program_id(1) == 0)
        def _():
            s_ref[...] = jnp.zeros_like(s_ref)

        q = _ssd_common(dt_ref, bias_ref, alog_ref, b_ref, c_ref, cst_ref, SSD_G * heads)
        prev_ref[...] = s_ref[...]
        lo = q["lo"]
        for j in range(n_pair):
            sl = slice(j * LANES, (j + 1) * LANES)
            p = _pair_terms(q, cst_ref, pl.program_id(0) * heads + 2 * j)
            xs_p = xs_ref[:, sl]
            xp = xs_p * p["dt_b"]
            xb = xp.astype(BF16)
            m_a = (q["scores"] * p["lm"][0]).astype(BF16)
            m_b = (q["scores"] * p["lm"][1]).astype(BF16)
            yd = jnp.where(lo, _dot(m_a, xb), _dot(m_b, xb))
            s_p = s_ref[:, sl]
            yo = _dot(q["cb"], s_p.astype(BF16)) * p["e_b"]
            y_ref[:, sl] = yd + yo + dl_ref[:, sl] * xs_p
            st = _dot(q["bb"], (xp * p["dec_b"]).astype(BF16), "tn")
            s_ref[:, sl] = s_p * p["explast_b"] + st
        yv = y_ref[...]
        zv = z_ref[...]
        yg = yv * (zv * _sigmoid(zv))
        rstd = lax.rsqrt(jnp.mean(yg * yg, axis=-1, keepdims=True) + NORM_EPS)
        yn_ref[...] = (yg * rstd * nw_ref[...]).astype(BF16)

    grp = lambda width: pl.BlockSpec((None, 1, width), lambda g, c: (g, 0, 0))
    head_vec = pl.BlockSpec((1, LANES), lambda g, c: (0, 0))
    return _pcall(
        body, name=name, grid=(SSD_G, nc),
        out_shape=[jax.ShapeDtypeStruct((L, d_inner), F32), jax.ShapeDtypeStruct((L, d_inner), BF16),
                   jax.ShapeDtypeStruct((nc, SSD_G, SSD_N, gw), F32)],
        in_specs=[pl.BlockSpec((SSD_CHUNK, gw), lambda g, c: (c, g)),
                  pl.BlockSpec((SSD_CHUNK, gw), lambda g, c: (c, g)),
                  pl.BlockSpec((SSD_CHUNK, SSD_N), lambda g, c: (c, bc0 + g)),
                  pl.BlockSpec((SSD_CHUNK, SSD_N), lambda g, c: (c, bc0 + SSD_G + g)),
                  pl.BlockSpec((SSD_CHUNK, LANES), lambda g, c: (c, dt0)),
                  head_vec, head_vec, grp(gw), grp(gw)],
        out_specs=[pl.BlockSpec((SSD_CHUNK, gw), lambda g, c: (c, g)),
                   pl.BlockSpec((SSD_CHUNK, gw), lambda g, c: (c, g)),
                   pl.BlockSpec((None, None, SSD_N, gw), lambda g, c: (c, g, 0, 0))],
        scratch_shapes=[pltpu.VMEM((SSD_N, gw), F32), pltpu.VMEM((SSD_CHUNK, LANES), F32)],
        compiler_params=_cparams(("parallel", "arbitrary")))(zx, xc, xc, xc, zx, bias_p, alog_p, d_lane, nw)


def _ssd_bwd(dyn, y, zx, xc, prev, bias_p, alog_p, d_lane, nw, d_inner, name):
    L = zx.shape[0]
    nc = L // SSD_CHUNK
    gw = d_inner // SSD_G
    heads = gw // SSD_P
    n_pair = heads // 2
    bc0 = d_inner // LANES
    dt0 = (2 * d_inner + 2 * SSD_G * SSD_N) // LANES

    def body(dyn_ref, y_ref, z_ref, xs_ref, b_ref, c_ref, dt_ref, prev_ref, bias_ref, alog_ref, dl_ref, nw_ref,
             dz_ref, dxs_ref, db_ref, dc_ref, ddt_ref, dbias_ref, dalog_ref, dd_ref, dnw_ref,
             ds_ref, cst_ref, racc_ref):
        @pl.when(pl.program_id(1) == 0)
        def _():
            ds_ref[...] = jnp.zeros_like(ds_ref)
            dbias_ref[...] = jnp.zeros_like(dbias_ref)
            dalog_ref[...] = jnp.zeros_like(dalog_ref)
            dd_ref[...] = jnp.zeros_like(dd_ref)
            dnw_ref[...] = jnp.zeros_like(dnw_ref)

        q = _ssd_common(dt_ref, bias_ref, alog_ref, b_ref, c_ref, cst_ref, SSD_G * heads)
        lane, row, lo = q["lane"], q["row"], q["lo"]
        lane1 = lax.broadcasted_iota(jnp.int32, (1, LANES), 1)
        head0 = pl.program_id(0) * heads
        mine = (lane >= head0) & (lane < head0 + heads)

        yv, zv, dynv, nwv = y_ref[...], z_ref[...], dyn_ref[...], nw_ref[...]
        sig = _sigmoid(zv)
        sz = zv * sig
        yg = yv * sz
        rstd = lax.rsqrt(jnp.mean(yg * yg, axis=-1, keepdims=True) + NORM_EPS)
        yhat = yg * rstd
        dnw_ref[...] += jnp.sum(dynv * yhat, axis=0, keepdims=True)
        dyh = dynv * nwv
        dyg = rstd * (dyh - yhat * jnp.mean(dyh * yhat, axis=-1, keepdims=True))
        dz_ref[...] = (dyg * yv * (sig * (1.0 + zv * (1.0 - sig)))).astype(BF16)
        dy_all = dyg * sz

        dg = jnp.zeros((SSD_CHUNK, SSD_CHUNK), F32)
        dc_acc = jnp.zeros((SSD_CHUNK, SSD_N), F32)
        db_acc = jnp.zeros((SSD_CHUNK, SSD_N), F32)
        dcs_mat = jnp.zeros((SSD_CHUNK, LANES), F32)
        ddt_mat = jnp.zeros((SSD_CHUNK, LANES), F32)
        dd_row = jnp.zeros((1, LANES), F32)
        racc_ref[...] = jnp.zeros_like(racc_ref)
        is_last = row == SSD_CHUNK - 1

        for j in range(n_pair):
            sl = slice(j * LANES, (j + 1) * LANES)
            ha, hb = head0 + 2 * j, head0 + 2 * j + 1
            p = _pair_terms(q, cst_ref, ha)
            xs_p = xs_ref[:, sl]
            dyp = dy_all[:, sl]
            xp = xs_p * p["dt_b"]
            xb = xp.astype(BF16)
            s_p = prev_ref[:, sl]
            s_pb = s_p.astype(BF16)
            dsn = ds_ref[:, sl]
            dsnb = dsn.astype(BF16)
            m_f = [q["scores"] * p["lm"][0], q["scores"] * p["lm"][1]]

            t0 = dyp * xs_p
            dd_row = dd_row + jnp.where(lane1 == ha, _sum_all(jnp.where(lo, t0, 0.0)), 0.0) \
                + jnp.where(lane1 == hb, _sum_all(jnp.where(lo, 0.0, t0)), 0.0)
            dxs_p = dl_ref[:, sl] * dyp

            yo = _dot(q["cb"], s_pb) * p["e_b"]
            dcs_b = (dyp * p["e_b"]).astype(BF16)
            dc_acc = dc_acc + _dot(dcs_b, s_pb, "nt")
            ds_yo = _dot(q["cb"], dcs_b, "tn")
            t1 = dyp * yo
            dcs_cols = [jnp.sum(jnp.where(lo, t1, 0.0), axis=1, keepdims=True),
                        jnp.sum(jnp.where(lo, 0.0, t1), axis=1, keepdims=True)]

            t2 = dsn * s_p
            dlast = [p["explast"][0] * _sum_all(jnp.where(lo, t2, 0.0)),
                     p["explast"][1] * _sum_all(jnp.where(lo, 0.0, t2))]
            ds_ref[:, sl] = dsn * p["explast_b"] + ds_yo
            w = _dot(q["bb"], dsnb)
            db_acc = db_acc + _dot((xp * p["dec_b"]).astype(BF16), dsnb, "nt")
            dxp = w * p["dec_b"]
            t3 = w * xp
            e = [jnp.sum(jnp.where(lo, t3, 0.0), axis=1, keepdims=True) * p["dec_cols"][0],
                 jnp.sum(jnp.where(lo, 0.0, t3), axis=1, keepdims=True) * p["dec_cols"][1]]
            for i in range(2):
                dlast[i] = dlast[i] + jnp.sum(e[i], axis=0, keepdims=True)
                dcs_cols[i] = dcs_cols[i] - e[i]

            dyb = dyp.astype(BF16)
            dy_h = [jnp.where(lo, dyp, 0.0).astype(BF16), jnp.where(lo, 0.0, dyp).astype(BF16)]
            dms = [_dot(dy_h[0], xb, "nt"), _dot(dy_h[1], xb, "nt")]
            dxp = dxp + jnp.where(lo, _dot(m_f[0].astype(BF16), dyb, "tn"), _dot(m_f[1].astype(BF16), dyb, "tn"))
            for i, h in enumerate((ha, hb)):
                dg = dg + dms[i] * p["lm"][i]
                qm = dms[i] * m_f[i]
                dcs_cols[i] = dcs_cols[i] + jnp.sum(qm, axis=1, keepdims=True)
                racc_ref[pl.ds(h, 1), :] = jnp.sum(qm, axis=0, keepdims=True)

            dxs_ref[:, sl] = dxs_p + dxp * p["dt_b"]
            t4 = dxp * xs_p
            ddt_cols = [jnp.sum(jnp.where(lo, t4, 0.0), axis=1, keepdims=True),
                        jnp.sum(jnp.where(lo, 0.0, t4), axis=1, keepdims=True)]
            for i, h in enumerate((ha, hb)):
                sel = lane == h
                dcs_mat = dcs_mat + jnp.where(sel, dcs_cols[i], 0.0) + jnp.where(sel & is_last, dlast[i], 0.0)
                ddt_mat = ddt_mat + jnp.where(sel, ddt_cols[i], 0.0)

        dcs_mat = dcs_mat - racc_ref[...].T
        tri_t = (row <= lane).astype(BF16)
        da = _dot_exact01(tri_t, dcs_mat)
        ddt = ddt_mat + da * q["a_row"]
        dalog_ref[...] += jnp.sum(jnp.where(mine, da * q["dt"], 0.0), axis=0, keepdims=True) * q["a_row"]
        draw = jnp.where(mine, ddt * _sigmoid(q["raw"]), 0.0)
        ddt_ref[...] = draw
        dbias_ref[...] += jnp.sum(draw, axis=0, keepdims=True)
        dd_ref[...] += dd_row
        dgb = dg.astype(BF16)
        dc_ref[...] = dc_acc + _dot(dgb, q["bb"])
        db_ref[...] = db_acc + _dot(dgb, q["cb"], "tn")

    rev = lambda c: nc - 1 - c
    grp = lambda width: pl.BlockSpec((None, 1, width), lambda g, c: (g, 0, 0))
    blk = lambda width, off: pl.BlockSpec((SSD_CHUNK, width), lambda g, c: (rev(c), off + g))
    head_vec = pl.BlockSpec((1, LANES), lambda g, c: (0, 0))
    return _pcall(
        body, name=name, grid=(SSD_G, nc),
        out_shape=[jax.ShapeDtypeStruct(zx.shape, BF16), jax.ShapeDtypeStruct((L, d_inner), F32),
                   jax.ShapeDtypeStruct((L, SSD_G * SSD_N), F32), jax.ShapeDtypeStruct((L, SSD_G * SSD_N), F32),
                   jax.ShapeDtypeStruct((SSD_G, L, LANES), F32),
                   jax.ShapeDtypeStruct((SSD_G, 1, LANES), F32), jax.ShapeDtypeStruct((SSD_G, 1, LANES), F32),
                   jax.ShapeDtypeStruct((SSD_G, 1, LANES), F32), jax.ShapeDtypeStruct((SSD_G, 1, gw), F32)],
        in_specs=[blk(gw, 0), blk(gw, 0), blk(gw, 0), blk(gw, 0), blk(SSD_N, bc0), blk(SSD_N, bc0 + SSD_G),
                  pl.BlockSpec((SSD_CHUNK, LANES), lambda g, c: (rev(c), dt0)),
                  pl.BlockSpec((None, None, SSD_N, gw), lambda g, c: (rev(c), g, 0, 0)),
                  head_vec, head_vec, grp(gw), grp(gw)],
        out_specs=[blk(gw, 0), blk(gw, 0), blk(SSD_N, 0), blk(SSD_N, 0),
                   pl.BlockSpec((None, SSD_CHUNK, LANES), lambda g, c: (g, rev(c), 0)),
                   grp(LANES), grp(LANES), grp(LANES), grp(gw)],
        scratch_shapes=[pltpu.VMEM((SSD_N, gw), F32), pltpu.VMEM((SSD_CHUNK, LANES), F32),
                        pltpu.VMEM((SSD_CHUNK, LANES), F32)],
        compiler_params=_cparams(("parallel", "arbitrary")))(
            dyn, y, zx, xc, xc, xc, zx, prev, bias_p, alog_p, d_lane, nw)


def _cond_mod(c_pad, ada_w, ada_b_loc, name):
    depth, D, n = ada_w.shape
    rows = c_pad.shape[0]

    def body(c_ref, w_ref, b_ref, mod_ref, cond_ref):
        cv = c_ref[...]
        cond = cv * _sigmoid(cv)
        cond_ref[...] = cond
        mod_ref[...] = _dot(cond.astype(BF16), w_ref[...].astype(BF16)) + b_ref[...]

    return _pcall(
        body, name=name, grid=(depth,),
        out_shape=[jax.ShapeDtypeStruct((depth, rows, n), F32), jax.ShapeDtypeStruct((rows, D), F32)],
        in_specs=[pl.BlockSpec((rows, D), lambda i: (0, 0)),
                  pl.BlockSpec((None, D, n), lambda i: (i, 0, 0)),
                  pl.BlockSpec((None, 1, n), lambda i: (i, 0, 0))],
        out_specs=[pl.BlockSpec((None, rows, n), lambda i: (i, 0, 0)),
                   pl.BlockSpec((rows, D), lambda i: (0, 0))],
        compiler_params=_cparams(("arbitrary",)))(c_pad, ada_w, ada_b_loc)


def _adamw_math(g, w, m, v):
    m_new = ADAM_B1 * m + (1.0 - ADAM_B1) * g
    v_new = ADAM_B2 * v + (1.0 - ADAM_B2) * (g * g)
    m_hat = m_new / (1.0 - ADAM_B1 ** ADAM_STEP)
    v_hat = v_new / (1.0 - ADAM_B2 ** ADAM_STEP)
    delta = -ADAM_LR * (m_hat / (jnp.sqrt(v_hat) + ADAM_EPS) + ADAM_WD * w)
    return delta, m_new, v_new


def _adamw_sum(parts, w, m, v, layer, name, prev=None, tr=256, window_off=None):
    depth, R, C = w.shape
    tr = _tile(R, tr)
    win = parts.shape[2]
    scratch = [] if window_off is None else [pltpu.VMEM((tr, win), F32)]

    def body(p_ref, w_ref, m_ref, v_ref, *rest):
        g_ref, d_ref, mo_ref, vo_ref = rest[-4 - len(scratch):len(rest) - len(scratch)]
        g = p_ref[0].astype(F32)
        for k in range(1, N_DEV):
            g = g + p_ref[k].astype(F32)
        if window_off is not None:
            me = _my_index()
            off = 0
            for k in range(N_DEV):
                off = jnp.where(me == k, window_off[k], off)
            src = lax.broadcasted_iota(jnp.int32, (win, win), 0)
            dst = lax.broadcasted_iota(jnp.int32, (win, win), 1)
            shift = ((src == dst + off) & (dst < C)).astype(BF16)
            hi, mid, lo = _split3(g)
            rest[-1][...] = _dot(hi, shift) + _dot(mid, shift) + _dot(lo, shift)
            g = rest[-1][:, 0:C]
        d, mn, vn = _adamw_math(g, w_ref[...], m_ref[...], v_ref[...])
        g_ref[...] = g
        d_ref[...] = d
        mo_ref[...] = mn
        vo_ref[...] = vn

    blk = pl.BlockSpec((None, tr, C), lambda i: (layer, i, 0))
    prev = list(prev) if prev is not None else []
    return _pcall(
        body, name=name, grid=(R // tr,),
        out_shape=[jax.ShapeDtypeStruct((depth, R, C), F32)] * 4,
        in_specs=[pl.BlockSpec((N_DEV, tr, win), lambda i: (0, i, 0)), blk, blk, blk]
        + [pl.BlockSpec(memory_space=pl.ANY)] * len(prev),
        out_specs=[blk] * 4, input_output_aliases={4 + k: k for k in range(len(prev))},
        scratch_shapes=scratch,
        compiler_params=_cparams(("parallel",)))(parts, w, m, v, *prev)


def _adamw_small(parts, wmv, head_parts, head_wmv, loss_parts, name):
    n, nh = len(parts), len(head_parts)
    n_heads = head_wmv[0][0].shape[1] if nh else 0
    groups = head_parts[0].shape[1] if nh else 0
    d_model = loss_parts.shape[2]

    def body(*refs):
        p_refs, refs = refs[:n], refs[n:]
        wmv_refs, refs = refs[:3 * n], refs[3 * n:]
        hp_refs, refs = refs[:nh], refs[nh:]
        hwmv_refs, refs = refs[:3 * nh], refs[3 * nh:]
        loss_ref, refs = refs[0], refs[1:]
        outs, loss_out, head_scr = refs[:4 * (n + nh)], refs[4 * (n + nh)], refs[4 * (n + nh) + 1]

        def update(i, g, w_ref, m_ref, v_ref):
            res = (g,) + _adamw_math(g, w_ref[...], m_ref[...], v_ref[...])
            for o_ref, r in zip(outs[4 * i:4 * i + 4], res):
                o_ref[...] = r

        for i in range(n):
            g = p_refs[i][0]
            for k in range(1, N_DEV):
                g = g + p_refs[i][k]
            update(i, g, *wmv_refs[3 * i:3 * i + 3])
        for i in range(nh):
            g = None
            for k in range(N_DEV):
                for grp in range(groups):
                    g = hp_refs[i][k, grp] if g is None else g + hp_refs[i][k, grp]
            head_scr[...] = g
            update(n + i, head_scr[:, 0:n_heads], *hwmv_refs[3 * i:3 * i + 3])
        tot = loss_ref[0]
        for k in range(1, N_DEV):
            tot = tot + loss_ref[k]
        loss_out[...] = jnp.broadcast_to(_sum_all(tot) * (0.5 / d_model), loss_out.shape)

    operands = list(parts) + [a for t in wmv for a in t] + list(head_parts) + [a for t in head_wmv for a in t]
    operands.append(loss_parts)
    out_shape = [jax.ShapeDtypeStruct(t[0].shape, F32) for t in list(wmv) + list(head_wmv) for _ in range(4)]
    out_shape.append(jax.ShapeDtypeStruct((1, LANES), F32))
    vmem = pl.BlockSpec(memory_space=pltpu.VMEM)
    outs = _pcall(body, name=name, out_shape=out_shape, in_specs=[vmem] * len(operands),
                  out_specs=[vmem] * len(out_shape), scratch_shapes=[pltpu.VMEM((1, LANES), F32)],
                  compiler_params=_cparams())(*operands)
    return [outs[4 * i:4 * i + 4] for i in range(n + nh)], outs[-1]


def _ada_adamw(cond_pad, dmod_pad, w, m, v, name, tr=256):
    depth, D, n = w.shape
    rows = cond_pad.shape[0]
    tr = _tile(D, tr)

    def body(c_ref, dm_ref, w_ref, m_ref, v_ref, g_ref, d_ref, mo_ref, vo_ref):
        g = _dot(c_ref[...].astype(BF16), dm_ref[...].astype(BF16), "tn")
        d, mn, vn = _adamw_math(g, w_ref[...], m_ref[...], v_ref[...])
        g_ref[...] = g
        d_ref[...] = d
        mo_ref[...] = mn
        vo_ref[...] = vn

    blk = pl.BlockSpec((None, tr, n), lambda i, r: (i, r, 0))
    return _pcall(
        body, name=name, grid=(depth, D // tr),
        out_shape=[jax.ShapeDtypeStruct((depth, D, n), F32)] * 4,
        in_specs=[pl.BlockSpec((rows, tr), lambda i, r: (0, r)),
                  pl.BlockSpec((None, rows, n), lambda i, r: (i, 0, 0)), blk, blk, blk],
        out_specs=[blk] * 4, compiler_params=_cparams(("parallel", "parallel")))(cond_pad, dmod_pad, w, m, v)


def kernel(x, c, ada_w, ada_b, mix_norm_w, mlp_norm_w, mlp_up, mlp_down, ssd_in_w, ssd_conv_w, ssd_conv_b, ssd_dt_bias, ssd_A_log, ssd_D, ssd_norm_w, ssd_out_w, sc_in_w, sc_conv_w, sc_out_w, final_norm_w, loss_target, m_ada_w, m_ada_b, m_mix_norm_w, m_mlp_norm_w, m_mlp_up, m_mlp_down, m_ssd_in_w, m_ssd_conv_w, m_ssd_conv_b, m_ssd_dt_bias, m_ssd_A_log, m_ssd_D, m_ssd_norm_w, m_ssd_out_w, m_sc_in_w, m_sc_conv_w, m_sc_out_w, m_final_norm_w, v_ada_w, v_ada_b, v_mix_norm_w, v_mlp_norm_w, v_mlp_up, v_mlp_down, v_ssd_in_w, v_ssd_conv_w, v_ssd_conv_b, v_ssd_dt_bias, v_ssd_A_log, v_ssd_D, v_ssd_norm_w, v_ssd_out_w, v_sc_in_w, v_sc_conv_w, v_sc_out_w, v_final_norm_w):
    weights = dict(ada_w=ada_w, ada_b=ada_b, mix_norm_w=mix_norm_w, mlp_norm_w=mlp_norm_w, mlp_up=mlp_up,
                   mlp_down=mlp_down, ssd_in_w=ssd_in_w, ssd_conv_w=ssd_conv_w, ssd_conv_b=ssd_conv_b,
                   ssd_dt_bias=ssd_dt_bias, ssd_A_log=ssd_A_log, ssd_D=ssd_D, ssd_norm_w=ssd_norm_w,
                   ssd_out_w=ssd_out_w, sc_in_w=sc_in_w, sc_conv_w=sc_conv_w, sc_out_w=sc_out_w,
                   final_norm_w=final_norm_w)
    moms = dict(ada_w=m_ada_w, ada_b=m_ada_b, mix_norm_w=m_mix_norm_w, mlp_norm_w=m_mlp_norm_w, mlp_up=m_mlp_up,
                mlp_down=m_mlp_down, ssd_in_w=m_ssd_in_w, ssd_conv_w=m_ssd_conv_w, ssd_conv_b=m_ssd_conv_b,
                ssd_dt_bias=m_ssd_dt_bias, ssd_A_log=m_ssd_A_log, ssd_D=m_ssd_D, ssd_norm_w=m_ssd_norm_w,
                ssd_out_w=m_ssd_out_w, sc_in_w=m_sc_in_w, sc_conv_w=m_sc_conv_w, sc_out_w=m_sc_out_w,
                final_norm_w=m_final_norm_w)
    vars_ = dict(ada_w=v_ada_w, ada_b=v_ada_b, mix_norm_w=v_mix_norm_w, mlp_norm_w=v_mlp_norm_w, mlp_up=v_mlp_up,
                 mlp_down=v_mlp_down, ssd_in_w=v_ssd_in_w, ssd_conv_w=v_ssd_conv_w, ssd_conv_b=v_ssd_conv_b,
                 ssd_dt_bias=v_ssd_dt_bias, ssd_A_log=v_ssd_A_log, ssd_D=v_ssd_D, ssd_norm_w=v_ssd_norm_w,
                 ssd_out_w=v_ssd_out_w, sc_in_w=v_sc_in_w, sc_conv_w=v_sc_conv_w, sc_out_w=v_sc_out_w,
                 final_norm_w=v_final_norm_w)
    names = list(weights)

    L, D = x.shape[1], x.shape[2]
    d_inner = 2 * D
    n_heads = d_inner // SSD_P
    hpg = n_heads // SSD_G
    gw = d_inner // SSD_G
    conv_dim = d_inner + 2 * SSD_G * SSD_N
    zx_dim = d_inner + conv_dim
    zx_pad = -(-(zx_dim + LANES) // 512) * 512
    in_ws = ssd_in_w.shape[2]
    in_base, in_off, in_win = _window_geometry(in_ws)
    me = _my_index()
    x0 = x[0]
    tgt = loss_target[0]

    n_mod = ada_w.shape[2]
    (c_all,) = _exchange([c], "gather_c", gather=True)
    c_pad = jnp.pad(c_all.reshape(N_DEV, D), ((0, 16 - N_DEV), (0, 0)))
    ada_b_loc = lax.dynamic_slice_in_dim(ada_b, me * n_mod, n_mod, axis=1).reshape(2, 1, n_mod)
    mod_blk, cond_pad = _cond_mod(c_pad, ada_w, ada_b_loc, "cond_mod")

    gather_order = ["mod", "ssd_conv_w", "sc_conv_w", "ssd_in_w", "ssd_out_w", "up0", "down0", "sc_in_w",
                    "sc_out_w", "up1", "down1"]
    gather_src = dict(mod=mod_blk, ssd_conv_w=ssd_conv_w[0], sc_conv_w=sc_conv_w[0],
                      ssd_in_w=ssd_in_w[0].astype(BF16),
                      ssd_out_w=ssd_out_w[0].astype(BF16),
                      up0=mlp_up[0].astype(BF16), down0=mlp_down[0].astype(BF16),
                      sc_in_w=sc_in_w[0].astype(BF16), sc_out_w=sc_out_w[0].astype(BF16),
                      up1=mlp_up[1].astype(BF16), down1=mlp_down[1].astype(BF16))
    handles, gather_token = _xfer_start([gather_src[k] for k in gather_order], "gather_start", gather=True,
                                        via_sibling=tuple(range(3, len(gather_order))))
    gather_handle = dict(zip(gather_order, handles))

    def gathered(keys, after, forward):
        tag = "_".join(keys)
        lands = _xfer_wait([gather_handle[k] for k in keys], after, f"gather_wait_{tag}", gather=True)
        return _sibling_forward(lands, f"gather_forward_{tag}") if forward else lands

    (mod_all,) = gathered(["mod"], gather_token, False)
    mod_mine = lax.dynamic_index_in_dim(mod_all, me, axis=2, keepdims=False)
    mod_mine = jnp.transpose(mod_mine, (1, 0, 2)).reshape(2, 6, 1, D)
    sh_m, sc_m, g_m, sh_f, sc_f, g_f = [[mod_mine[i, k] for i in range(2)] for k in range(6)]

    vec = lambda a: a.reshape(1, -1)
    grads = {}
    small = {}

    _, h0 = _norm_mod_fwd(x0, None, None, vec(mix_norm_w[0]), sc_m[0], sh_m[0], "l0_mix_norm")
    cw_all, scw_all = gathered(["ssd_conv_w", "sc_conv_w"], h0, False)
    (ssd_in_g,) = gathered(["ssd_in_w"], h0, True)
    w_in_all = _shards_to_columns(ssd_in_g, in_base, in_off, in_win, zx_pad, "ssd_in_w_columns")
    (zx,) = _mm_nn(h0, w_in_all, F32, "ssd_in_proj", tm=2048, tn=512)
    conv_b0 = vec(ssd_conv_b[0])
    conv_w_full = jnp.transpose(cw_all, (1, 0, 2)).reshape(SSD_K, conv_dim)
    sc_conv_full = jnp.transpose(scw_all, (1, 0, 2)).reshape(SC_K, D)
    xc = _ssd_conv_fwd(zx, conv_w_full, conv_b0, d_inner, conv_dim, "ssd_conv")
    bias_p = jnp.pad(ssd_dt_bias[0], (0, LANES - n_heads)).reshape(1, LANES)
    alog_p = jnp.pad(ssd_A_log[0], (0, LANES - n_heads)).reshape(1, LANES)
    d_lane = jnp.repeat(ssd_D[0], SSD_P).reshape(SSD_G, 1, gw)
    nw_g = ssd_norm_w[0].reshape(SSD_G, 1, gw)
    y_ssd, yn, prev = _ssd_fwd(zx, xc, bias_p, alog_p, d_lane, nw_g, d_inner, "ssd_scan")
    ups, downs = [None, None], [None, None]
    ssd_out_g, ups[0], down0_g = gathered(["ssd_out_w", "up0", "down0"], yn, True)
    w_ssd_out, downs[0] = ssd_out_g.reshape(-1, D), down0_g.reshape(-1, D)
    (mix0,) = _mm_nn(yn, w_ssd_out, F32, "ssd_out_proj")
    x1, h1 = _norm_mod_fwd(x0, mix0, g_m[0], vec(mlp_norm_w[0]), sc_f[0], sh_f[0], "l0_mlp_norm")
    u0, s0 = _mm_nn_blocked(h1, ups[0], "l0_mlp_up", _ep_relu2, [BF16, BF16])
    (d0,) = _mm_nn(s0, downs[0], F32, "l0_mlp_down")
    x2, h2 = _norm_mod_fwd(x1, d0, g_f[0], vec(mix_norm_w[1]), sc_m[1], sh_m[1], "l1_mix_norm")
    sc_in_g, sc_out_g, ups[1], down1_g = gathered(["sc_in_w", "sc_out_w", "up1", "down1"], h2, True)
    w_sc_out, downs[1] = sc_out_g.reshape(-1, D), down1_g.reshape(-1, D)
    (proj,) = _mm_nn_blocked(h2, sc_in_g, "sc_in_proj", _ep_store(F32), [F32])
    yc = _sc_conv_fwd(proj, sc_conv_full, "sc_conv")
    (mix1,) = _mm_nn(yc, w_sc_out, F32, "sc_out_proj")
    x3, h3 = _norm_mod_fwd(x2, mix1, g_m[1], vec(mlp_norm_w[1]), sc_f[1], sh_f[1], "l1_mlp_norm")
    u1, s1 = _mm_nn_blocked(h3, ups[1], "l1_mlp_up", _ep_relu2, [BF16, BF16])
    (d1,) = _mm_nn(s1, downs[1], F32, "l1_mlp_down")

    dx, loss_lane, dfw, dd1, dg = _final_loss(x3, d1, g_f[1], vec(final_norm_w), tgt, "final_loss")
    small["final_norm_w"] = dfw

    dmod = [[None] * 6 for _ in range(2)]
    dmod[1][5] = dg

    def mlp_backward(i, dx_out, dd, x_mid, h_in, u, s, mix, gate):
        du = _mm_nt(dd, downs[i], BF16, f"l{i}_mlp_down_bwd", epilogue=_ep_relu2_bwd, extra=(u,))
        gdown = _mm_tn(s, dd, BF16, f"l{i}_mlp_down_wgrad").reshape(N_DEV, -1, D)
        gup = _mm_tn_blocked(h_in, du, BF16, f"l{i}_mlp_up_wgrad")
        (h_down, h_up), token = _xfer_start([gdown, gup], f"l{i}_mlp_grads_start", gather=False)
        grad_handle[f"mlp_down{i}"], grad_handle[f"mlp_up{i}"] = h_down, h_up
        dh = _mm_nt_blocked(du, ups[i], F32, f"l{i}_mlp_up_bwd", after=(token,))
        dxm, dsh, dsc, dnw, dmix, dgate = _norm_mod_bwd(dh, x_mid, vec(mlp_norm_w[i]), sc_f[i], dx_out,
                                                        f"l{i}_mlp_norm_bwd", branch=(mix, gate))
        dmod[i][3], dmod[i][4], dmod[i][2] = dsh, dsc, dgate
        return dxm, dmix, dnw

    grad_handle = {}
    dx3, dyc, dnw_mlp1 = mlp_backward(1, dx, dd1, x3, h3, u1, s1, mix1, g_m[1])
    g_sc_out = _mm_tn(yc, dyc, BF16, "sc_out_wgrad").reshape(N_DEV, -1, D)
    dconv_out = _mm_nt(dyc, w_sc_out, F32, "sc_out_bwd")
    dbg, dcg, dxv, dscw = _sc_conv_bwd(proj, sc_conv_full, dconv_out, "sc_conv_bwd")
    dproj = jnp.concatenate([dbg, dcg, dxv], axis=1)
    g_sc_in = _mm_tn_blocked(h2, dproj, BF16, "sc_in_wgrad")
    (grad_handle["sc_out_w0"], grad_handle["sc_in_w0"]), token = _xfer_start(
        [g_sc_out, g_sc_in], "sc_grads_start", gather=False)
    dh2 = _mm_nt_blocked(dproj, sc_in_g, F32, "sc_in_bwd", after=(token,))
    dx2, dsh, dsc, dnw_mix1, dd0, dg = _norm_mod_bwd(dh2, x2, vec(mix_norm_w[1]), sc_m[1], dx3, "l1_mix_norm_bwd",
                                                     branch=(d0, g_f[0]))
    dmod[1][0], dmod[1][1], dmod[0][5] = dsh, dsc, dg
    dx1, dyo, dnw_mlp0 = mlp_backward(0, dx2, dd0, x1, h1, u0, s0, mix0, g_m[0])
    g_ssd_out = _mm_tn(yn, dyo, BF16, "ssd_out_wgrad").reshape(N_DEV, -1, D)
    (grad_handle["ssd_out_w0"],), token = _xfer_start([g_ssd_out], "ssd_out_grad_start", gather=False)
    dyn = _mm_nt(dyo, w_ssd_out, F32, "ssd_out_bwd", after=(token,))
    dz, dxs, db_, dc_, ddt, dbias, dalog, dd_, dnw_ssd = _ssd_bwd(
        dyn, y_ssd, zx, xc, prev, bias_p, alog_p, d_lane, nw_g, d_inner, "ssd_scan_bwd")
    dzx, dcw, dcb = _ssd_conv_bwd(zx, conv_w_full, conv_b0, [dxs, db_, dc_], dz, d_inner, "ssd_conv_bwd")
    dzx = _dzx_finish(dzx, ddt, zx_dim, "ssd_dzx_finish")
    g_in_all = _mm_tn(h0, dzx, BF16, "ssd_in_wgrad", tn=512, tk=2048)
    g_ssd_in = jnp.stack([g_in_all[:, b:b + in_win] for b in in_base], axis=0)
    (grad_handle["ssd_in_w0"],), token = _xfer_start([g_ssd_in], "ssd_in_grad_start", gather=False)
    dh0 = _mm_nt(dzx, w_in_all, F32, "ssd_in_bwd", tk=dzx.shape[1] // 2, after=(token,))
    grad_x, dsh, dsc, dnw_mix0 = _norm_mod_bwd(dh0, x0, vec(mix_norm_w[0]), sc_m[0], dx1, "l0_mix_norm_bwd")
    dmod[0][0], dmod[0][1] = dsh, dsc

    small["ada_b"] = jnp.concatenate([jnp.concatenate(dmod[i], axis=1) for i in range(2)], axis=0)
    small["mix_norm_w"] = jnp.concatenate([dnw_mix0, dnw_mix1], axis=0)
    small["mlp_norm_w"] = jnp.concatenate([dnw_mlp0, dnw_mlp1], axis=0)
    small["ssd_conv_w"] = dcw
    small["ssd_conv_b"] = dcb
    small["ssd_norm_w"] = dnw_ssd.reshape(1, d_inner)
    small["sc_conv_w"] = dscw
    small["loss"] = loss_lane
    small_names = list(small)
    head_names = ["ssd_dt_bias", "ssd_A_log", "ssd_D"]
    handles, small_token = _xfer_start([small[k] for k in small_names] + [dbias, dalog, dd_],
                                       "small_grads_start", gather=True)

    out_g, out_d, out_m, out_v = {}, {}, {}, {}

    layer_res = {}

    def big_update(name, i, after):
        (parts,) = _xfer_wait([grad_handle[f"{name}{i}"]], after, f"grads_wait_{name}_{i}", gather=False)
        res = _adamw_sum(parts, weights[name], moms[name], vars_[name], i, f"adamw_{name}_{i}",
                         prev=layer_res.get(name), window_off=in_off if name == "ssd_in_w" else None)
        layer_res[name] = res
        return res[1]

    chain = small_token
    for name, i in [("mlp_down", 1), ("mlp_up", 1), ("sc_out_w", 0), ("sc_in_w", 0), ("mlp_down", 0),
                    ("mlp_up", 0), ("ssd_out_w", 0), ("ssd_in_w", 0)]:
        chain = big_update(name, i, chain)
    gathered_small = _xfer_wait(handles, chain, "small_grads_wait", gather=True)
    small_all = dict(zip(small_names + head_names, gathered_small))

    dmod_loc = lax.dynamic_slice_in_dim(small_all["ada_b"], me * n_mod, n_mod, axis=2)
    dmod_pad = jnp.pad(jnp.transpose(dmod_loc, (1, 0, 2)), ((0, 0), (0, 16 - N_DEV), (0, 0)))
    out_g["ada_w"], out_d["ada_w"], out_m["ada_w"], out_v["ada_w"] = _ada_adamw(
        cond_pad, dmod_pad, ada_w, m_ada_w, v_ada_w, "adamw_ada_w")

    for k in ("ssd_conv_w", "sc_conv_w"):
        n_loc = weights[k].shape[2]
        small_all[k] = lax.dynamic_slice_in_dim(small_all[k], me * n_loc, n_loc, axis=2)
    plain = [k for k in small_names if k != "loss"]
    as2d = lambda a: a.reshape(-1, a.shape[-1])
    res, loss_row = _adamw_small(
        [small_all[k] for k in plain], [tuple(as2d(d[k]) for d in (weights, moms, vars_)) for k in plain],
        [small_all[k] for k in head_names], [tuple(as2d(d[k]) for d in (weights, moms, vars_)) for k in head_names],
        small_all["loss"], "adamw_small")
    loss = loss_row[0, 0]
    for k, res4 in zip(plain + head_names, res):
        for r, dst in zip(res4, (out_g, out_d, out_m, out_v)):
            dst[k] = r.reshape(weights[k].shape)
    for name, res4 in layer_res.items():
        for r, dst in zip(res4, (out_g, out_d, out_m, out_v)):
            dst[name] = r

    return (loss, grad_x[None], *[out_g[k] for k in names], *[out_d[k] for k in names],
            *[out_m[k] for k in names], *[out_v[k] for k in names])
```

```python
import functools

import jax
import jax.numpy as jnp
from jax import lax
from jax.experimental import pallas as pl
from jax.experimental.pallas import tpu as pltpu

F32 = jnp.float32
BF16 = jnp.bfloat16
N_DEV = 8
MESH_AXES = ("x", "y", "c")
MESH = pl.DeviceIdType.MESH

NORM_EPS = 1e-5
SSD_G = 4
SSD_P = 64
SSD_N = 128
SSD_CHUNK = 128
SSD_K = 4
SC_K = 3
LANES = 128

ADAM_LR = 0.001
ADAM_B1 = 0.9
ADAM_B2 = 0.999
ADAM_EPS = 1e-08
ADAM_WD = 0.01
ADAM_STEP = 10

VMEM_LIMIT = 56 * 1024 * 1024


def _pcall(body, **kw):
    return pl.pallas_call(body, **kw)


def _cparams(sem=None):
    if sem is None:
        return pltpu.CompilerParams(vmem_limit_bytes=VMEM_LIMIT)
    return pltpu.CompilerParams(dimension_semantics=sem, vmem_limit_bytes=VMEM_LIMIT)


def _my_index():
    return 4 * lax.axis_index("x") + 2 * lax.axis_index("y") + lax.axis_index("c")


_PEER_MASKS = [(0, 0, 1), (0, 1, 0), (0, 1, 1), (1, 0, 0), (1, 0, 1), (1, 1, 0), (1, 1, 1)]


def _peers():
    x, y, c = lax.axis_index("x"), lax.axis_index("y"), lax.axis_index("c")
    out = []
    for mx, my, mc in _PEER_MASKS:
        px = (1 - x) if mx else x
        py = (1 - y) if my else y
        pc = (1 - c) if mc else c
        out.append(((px, py, pc), 4 * px + 2 * py + pc))
    return out


def _exchange(arrs, name, gather):
    n = len(arrs)
    n_peer = N_DEV - 1

    def body(*refs):
        ins, outs = refs[:n], refs[n:2 * n]
        send_sems, recv_sems, local_sems = refs[2 * n:]
        me = _my_index()
        peers = _peers()
        started = []
        for a in range(n):
            src_own = ins[a] if gather else ins[a].at[me]
            own = pltpu.make_async_copy(src_own, outs[a].at[me], local_sems.at[a])
            own.start()
            started.append(own)
        sends = []
        for a in range(n):
            for k, (peer, pidx) in enumerate(peers):
                src = ins[a] if gather else ins[a].at[pidx]
                cp = pltpu.make_async_remote_copy(
                    src_ref=src, dst_ref=outs[a].at[me],
                    send_sem=send_sems.at[a * n_peer + k], recv_sem=recv_sems.at[a * n_peer + k],
                    device_id=peer, device_id_type=MESH)
                cp.start()
                sends.append(cp)
        for a in range(n):
            for k, (peer, pidx) in enumerate(peers):
                src = ins[a] if gather else ins[a].at[pidx]
                pltpu.make_async_remote_copy(
                    src_ref=src, dst_ref=outs[a].at[pidx],
                    send_sem=send_sems.at[a * n_peer + k], recv_sem=recv_sems.at[a * n_peer + k],
                    device_id=peer, device_id_type=MESH).wait_recv()
        for cp in sends:
            cp.wait_send()
        for own in started:
            own.wait()

    if gather:
        out_shape = [jax.ShapeDtypeStruct((N_DEV,) + a.shape, a.dtype) for a in arrs]
    else:
        out_shape = [jax.ShapeDtypeStruct(a.shape, a.dtype) for a in arrs]
    any_spec = pl.BlockSpec(memory_space=pl.ANY)
    outs = _pcall(
        body, name=name, out_shape=out_shape,
        in_specs=[any_spec] * n, out_specs=[any_spec] * n,
        scratch_shapes=[pltpu.SemaphoreType.DMA((n * n_peer,)), pltpu.SemaphoreType.DMA((n * n_peer,)),
                        pltpu.SemaphoreType.DMA((n,))],
        compiler_params=pltpu.CompilerParams(has_side_effects=True),
    )(*arrs)
    return list(outs)


_HBM = pl.BlockSpec(memory_space=pltpu.HBM)
_SEM = pl.BlockSpec(memory_space=pltpu.SEMAPHORE)
_DATAFLOW = pltpu.SideEffectType.DATAFLOW_SIDE_EFFECTING


_ALL_PEERS = tuple(range(N_DEV - 1))
_SAME_CORE_PEERS = (0, 1, 3, 5)
_OTHER_CHIPS = (1, 3, 5)


def _xfer_start(arrs, name, gather, via_sibling=(), after=()):
    n = len(arrs)
    n_peer = N_DEV - 1
    n_after = len(after)
    peer_ks = [_SAME_CORE_PEERS if a in via_sibling else _ALL_PEERS for a in range(n)]

    def body(*refs):
        ins, lands = refs[:n], refs[n:2 * n]
        sems = refs[2 * n + n_after:5 * n + n_after]
        token = refs[-1]
        me = _my_index()
        peers = _peers()
        for a in range(n):
            send_sems, recv_sems, loc_sem = sems[3 * a:3 * a + 3]
            src_own = ins[a] if gather else ins[a].at[me]
            pltpu.make_async_copy(src_own, lands[a].at[me], loc_sem).start()
            for k in peer_ks[a]:
                peer, pidx = peers[k]
                src = ins[a] if gather else ins[a].at[pidx]
                pltpu.make_async_remote_copy(
                    src_ref=src, dst_ref=lands[a].at[me], send_sem=send_sems.at[k], recv_sem=recv_sems.at[k],
                    device_id=peer, device_id_type=MESH).start()
        token[...] = jnp.zeros_like(token)

    land_shapes = [((N_DEV,) + a.shape) if gather else a.shape for a in arrs]
    out_shape, out_specs = [], []
    for _ in range(n):
        out_shape += [pltpu.SemaphoreType.DMA((n_peer,)), pltpu.SemaphoreType.DMA((n_peer,)),
                      pltpu.SemaphoreType.DMA(())]
        out_specs += [_SEM, _SEM, _SEM]
    out_shape += [pltpu.HBM(a.shape, a.dtype) for a in arrs]
    out_shape += [pltpu.HBM(s, a.dtype) for s, a in zip(land_shapes, arrs)]
    out_shape += [jax.ShapeDtypeStruct((8, LANES), F32)]
    out_specs += [_HBM] * (2 * n) + [pl.BlockSpec(memory_space=pltpu.VMEM)]
    aliases = {}
    for a in range(n):
        aliases[a] = 3 * n + a
        aliases[n + a] = 4 * n + a
    operands = [pltpu.with_memory_space_constraint(a, pltpu.HBM) for a in arrs]
    operands += [pltpu.with_memory_space_constraint(lax.empty(s, a.dtype), pltpu.HBM)
                 for s, a in zip(land_shapes, arrs)]
    outs = _pcall(
        body, name=name, out_shape=tuple(out_shape),
        in_specs=[_HBM] * (2 * n) + [pl.BlockSpec(memory_space=pl.ANY)] * n_after, out_specs=tuple(out_specs),
        input_output_aliases=aliases,
        compiler_params=pltpu.CompilerParams(has_side_effects=_DATAFLOW),
    )(*operands, *after)
    handles = []
    for a in range(n):
        handles.append((outs[3 * n + a], outs[4 * n + a], outs[3 * a], outs[3 * a + 1], outs[3 * a + 2],
                        peer_ks[a]))
    return handles, outs[-1]


def _xfer_wait(handles, after, name, gather):
    n = len(handles)
    after = tuple(after) if isinstance(after, (tuple, list)) else (after,)
    peer_ks = [h[5] for h in handles]

    def body(*refs):
        me = _my_index()
        peers = _peers()
        for a in range(n):
            src_ref, land_ref, send_ref, recv_ref, loc_ref = refs[5 * a:5 * a + 5]
            src_own = src_ref if gather else src_ref.at[me]
            pltpu.make_async_copy(src_own, land_ref.at[me], loc_ref).wait()
            for k in peer_ks[a]:
                peer, pidx = peers[k]
                src = src_ref if gather else src_ref.at[pidx]
                cp = pltpu.make_async_remote_copy(
                    src_ref=src, dst_ref=land_ref.at[pidx], send_sem=send_ref.at[k], recv_sem=recv_ref.at[k],
                    device_id=peer, device_id_type=MESH)
                cp.wait_send()
                cp.wait_recv()

    operands, in_specs, out_shape, aliases = [], [], [], {}
    for a, h in enumerate(handles):
        operands += list(h[:5])
        in_specs += [_HBM, _HBM, _SEM, _SEM, _SEM]
        out_shape += [pltpu.HBM(h[0].shape, h[0].dtype), pltpu.HBM(h[1].shape, h[1].dtype)]
        aliases[5 * a] = 2 * a
        aliases[5 * a + 1] = 2 * a + 1
    outs = _pcall(
        body, name=name, out_shape=tuple(out_shape),
        in_specs=in_specs + [pl.BlockSpec(memory_space=pl.ANY)] * len(after),
        out_specs=tuple([_HBM] * (2 * n)), input_output_aliases=aliases,
        compiler_params=pltpu.CompilerParams(has_side_effects=_DATAFLOW),
    )(*operands, *after)
    return [outs[2 * a + 1] for a in range(n)]


def _sibling_forward(lands, name):
    n = len(lands)
    n_fwd = len(_OTHER_CHIPS)

    def body(*refs):
        ins, bufs = refs[:n], refs[n:2 * n]
        send_sems, recv_sems = refs[2 * n:]
        x, y, c = lax.axis_index("x"), lax.axis_index("y"), lax.axis_index("c")
        sibling = (x, y, 1 - c)
        peers = _peers()
        sends = []
        for a in range(n):
            for j, k in enumerate(_OTHER_CHIPS):
                slot = peers[k][1]
                cp = pltpu.make_async_remote_copy(
                    src_ref=ins[a].at[slot], dst_ref=bufs[a].at[slot],
                    send_sem=send_sems.at[a * n_fwd + j], recv_sem=recv_sems.at[a * n_fwd + j],
                    device_id=sibling, device_id_type=MESH)
                cp.start()
                sends.append(cp)
        for a in range(n):
            for j, k in enumerate(_OTHER_CHIPS):
                (px, py, pc), slot = peers[k]
                theirs = 4 * px + 2 * py + (1 - pc)
                pltpu.make_async_remote_copy(
                    src_ref=ins[a].at[slot], dst_ref=bufs[a].at[theirs],
                    send_sem=send_sems.at[a * n_fwd + j], recv_sem=recv_sems.at[a * n_fwd + j],
                    device_id=sibling, device_id_type=MESH).wait_recv()
        for cp in sends:
            cp.wait_send()

    any_spec = pl.BlockSpec(memory_space=pl.ANY)
    outs = _pcall(
        body, name=name, out_shape=[jax.ShapeDtypeStruct(a.shape, a.dtype) for a in lands],
        in_specs=[any_spec] * n, out_specs=[any_spec] * n,
        input_output_aliases={a: a for a in range(n)},
        scratch_shapes=[pltpu.SemaphoreType.DMA((n * n_fwd,)), pltpu.SemaphoreType.DMA((n * n_fwd,))],
        compiler_params=pltpu.CompilerParams(has_side_effects=True),
    )(*lands)
    return list(outs)


_DIMS = {"nn": (((1,), (0,)), ((), ())), "nt": (((1,), (1,)), ((), ())), "tn": (((0,), (0,)), ((), ()))}


def _dot(a, b, mode="nn"):
    return lax.dot_general(a, b, _DIMS[mode], preferred_element_type=F32)


def _mm(a, b, *, mode, grid, a_spec, b_spec, out_shape, out_specs, acc_shape, epilogue, name,
        extra=(), extra_specs=(), after=()):
    nk = grid[2]
    n_extra = len(extra)
    n_in = 2 + n_extra + len(after)

    def body_single(*refs):
        a_ref, b_ref = refs[0], refs[1]
        epilogue(_dot(a_ref[...], b_ref[...], mode), refs[2:2 + n_extra], refs[n_in:])

    def body_acc(*refs):
        a_ref, b_ref = refs[0], refs[1]
        ex = refs[2:2 + n_extra]
        outs = refs[n_in:-1]
        acc = refs[-1]
        k = pl.program_id(2)

        @pl.when(k == 0)
        def _():
            acc[...] = jnp.zeros_like(acc)

        acc[...] += _dot(a_ref[...], b_ref[...], mode)

        @pl.when(k == nk - 1)
        def _():
            epilogue(acc[...], ex, outs)

    return _pcall(
        body_single if nk == 1 else body_acc, name=name, grid=grid, out_shape=out_shape,
        in_specs=[a_spec, b_spec] + list(extra_specs) + [pl.BlockSpec(memory_space=pl.ANY)] * len(after),
        out_specs=out_specs,
        scratch_shapes=[] if nk == 1 else [pltpu.VMEM(acc_shape, F32)],
        compiler_params=_cparams(("parallel", "parallel", "arbitrary")),
    )(a, b, *extra, *after)


def _ep_store(dtype):
    def ep(acc, ex, outs):
        outs[0][...] = acc.astype(dtype)
    return ep


def _ep_relu2(acc, ex, outs):
    outs[0][...] = acc.astype(BF16)
    r = jnp.maximum(acc, 0.0)
    outs[1][...] = (r * r).astype(BF16)


def _ep_relu2_bwd(acc, ex, outs):
    u = ex[0][...].astype(F32)
    outs[0][...] = (acc * (2.0 * jnp.maximum(u, 0.0))).astype(BF16)


def _tile(n, want):
    t = min(n, want)
    while n % t:
        t //= 2
    return t


def _mm_nn(a, w, out_dtype, name, tm=1024, tn=1024, tk=1024, epilogue=None, out_dtypes=None):
    M, K = a.shape
    N = w.shape[1]
    tm, tn, tk = _tile(M, tm), _tile(N, tn), _tile(K, tk)
    out_dtypes = out_dtypes or [out_dtype]
    return _mm(a, w, mode="nn", grid=(M // tm, N // tn, K // tk),
               a_spec=pl.BlockSpec((tm, tk), lambda i, j, k: (i, k)),
               b_spec=pl.BlockSpec((tk, tn), lambda i, j, k: (k, j)),
               out_shape=[jax.ShapeDtypeStruct((M, N), d) for d in out_dtypes],
               out_specs=[pl.BlockSpec((tm, tn), lambda i, j, k: (i, j)) for _ in out_dtypes],
               acc_shape=(tm, tn), epilogue=epilogue or _ep_store(out_dtype), name=name)


def _mm_nn_blocked(a, wg, name, epilogue, out_dtypes, tm=2048):
    M, K = a.shape
    n = wg.shape[2]
    tm = _tile(M, tm)
    return _mm(a, wg, mode="nn", grid=(M // tm, N_DEV, 1),
               a_spec=pl.BlockSpec((tm, K), lambda i, j, k: (i, 0)),
               b_spec=pl.BlockSpec((None, K, n), lambda i, j, k: (j, 0, 0)),
               out_shape=[jax.ShapeDtypeStruct((M, N_DEV * n), d) for d in out_dtypes],
               out_specs=[pl.BlockSpec((tm, n), lambda i, j, k: (i, j)) for _ in out_dtypes],
               acc_shape=(tm, n), epilogue=epilogue, name=name)


def _mm_nt(a, w, out_dtype, name, tm=1024, tn=1024, tk=1024, epilogue=None, extra=(), extra_specs=(),
           after=()):
    M, K = a.shape
    N = w.shape[0]
    tm, tn, tk = _tile(M, tm), _tile(N, tn), _tile(K, tk)
    if extra and not extra_specs:
        extra_specs = [pl.BlockSpec((tm, tn), lambda i, j, k: (i, j)) for _ in extra]
    return _mm(a, w, mode="nt", grid=(M // tm, N // tn, K // tk),
               a_spec=pl.BlockSpec((tm, tk), lambda i, j, k: (i, k)),
               b_spec=pl.BlockSpec((tn, tk), lambda i, j, k: (j, k)),
               out_shape=[jax.ShapeDtypeStruct((M, N), out_dtype)],
               out_specs=[pl.BlockSpec((tm, tn), lambda i, j, k: (i, j))],
               acc_shape=(tm, tn), epilogue=epilogue or _ep_store(out_dtype), name=name,
               extra=extra, extra_specs=extra_specs, after=after)[0]


def _mm_nt_blocked(a, wg, out_dtype, name, tm=1024, after=()):
    M = a.shape[0]
    kout, n = wg.shape[1], wg.shape[2]
    tm = _tile(M, tm)
    return _mm(a, wg, mode="nt", grid=(M // tm, 1, N_DEV),
               a_spec=pl.BlockSpec((tm, n), lambda i, j, k: (i, k)),
               b_spec=pl.BlockSpec((None, kout, n), lambda i, j, k: (k, 0, 0)),
               out_shape=[jax.ShapeDtypeStruct((M, kout), out_dtype)],
               out_specs=[pl.BlockSpec((tm, kout), lambda i, j, k: (i, 0))],
               acc_shape=(tm, kout), epilogue=_ep_store(out_dtype), name=name, after=after)[0]


def _mm_tn(a, b, out_dtype, name, tm=1024, tn=1024, tk=1024):
    K, M = a.shape
    N = b.shape[1]
    tm, tn, tk = _tile(M, tm), _tile(N, tn), _tile(K, tk)
    return _mm(a, b, mode="tn", grid=(M // tm, N // tn, K // tk),
               a_spec=pl.BlockSpec((tk, tm), lambda i, j, k: (k, i)),
               b_spec=pl.BlockSpec((tk, tn), lambda i, j, k: (k, j)),
               out_shape=[jax.ShapeDtypeStruct((M, N), out_dtype)],
               out_specs=[pl.BlockSpec((tm, tn), lambda i, j, k: (i, j))],
               acc_shape=(tm, tn), epilogue=_ep_store(out_dtype), name=name)[0]


def _mm_tn_blocked(a, b, out_dtype, name, tm=1024, tk=2048):
    K, M = a.shape
    n = b.shape[1] // N_DEV
    tm, tk = _tile(M, tm), _tile(K, tk)
    return _mm(a, b, mode="tn", grid=(M // tm, N_DEV, K // tk),
               a_spec=pl.BlockSpec((tk, tm), lambda i, j, k: (k, i)),
               b_spec=pl.BlockSpec((tk, n), lambda i, j, k: (k, j)),
               out_shape=[jax.ShapeDtypeStruct((N_DEV, M, n), out_dtype)],
               out_specs=[pl.BlockSpec((None, tm, n), lambda i, j, k: (j, i, 0))],
               acc_shape=(tm, n), epilogue=_ep_store(out_dtype), name=name)[0]


def _window_geometry(ws):
    base = [(ws * k // LANES) * LANES for k in range(N_DEV)]
    off = [ws * k - base[k] for k in range(N_DEV)]
    win = -(-(max(off) + ws) // LANES) * LANES
    return base, off, win


def _shards_to_columns(xg, base, off, win, n_out, name, tr=256):
    R, ws = xg.shape[1], xg.shape[2]
    tr = _tile(R, tr)
    nb_win = win // LANES

    def body(x_ref, o_ref, frame_ref):
        written = set()
        frame_ref[...] = jnp.zeros_like(frame_ref)
        for k in range(N_DEV):
            frame_ref[:, 0:ws] = x_ref[k].astype(F32)
            window = frame_ref[...]
            if off[k]:
                window = pltpu.roll(window, off[k], 1)
            for i in range(nb_win):
                b = base[k] // LANES + i
                if b * LANES >= n_out:
                    continue
                cols = slice(b * LANES, (b + 1) * LANES)
                blk = window[:, i * LANES:(i + 1) * LANES]
                if b in written:
                    blk = blk + o_ref[:, cols].astype(F32)
                o_ref[:, cols] = blk.astype(o_ref.dtype)
                written.add(b)
        for b in range(n_out // LANES):
            if b not in written:
                o_ref[:, b * LANES:(b + 1) * LANES] = jnp.zeros((tr, LANES), o_ref.dtype)

    return _pcall(
        body, name=name, grid=(R // tr,), out_shape=jax.ShapeDtypeStruct((R, n_out), xg.dtype),
        in_specs=[pl.BlockSpec((N_DEV, tr, ws), lambda i: (0, i, 0))],
        out_specs=pl.BlockSpec((tr, n_out), lambda i: (i, 0)),
        scratch_shapes=[pltpu.VMEM((tr, win), F32)],
        compiler_params=_cparams(("parallel",)))(xg)


def _sigmoid(x):
    return 1.0 / (1.0 + jnp.exp(-x))


def _row_spec(tm, d):
    return pl.BlockSpec((tm, d), lambda i: (i, 0))


def _vec_spec(d):
    return pl.BlockSpec((1, d), lambda i: (0, 0))


def _norm_mod_fwd(x, y, gate, nw, scale, shift, name, tm=256):
    L, D = x.shape
    tm = _tile(L, tm)
    has_res = y is not None

    def body(*refs):
        if has_res:
            x_ref, y_ref, g_ref, nw_ref, sc_ref, sh_ref, xo_ref, h_ref = refs
            xn = x_ref[...] + g_ref[...] * y_ref[...]
            xo_ref[...] = xn
        else:
            x_ref, nw_ref, sc_ref, sh_ref, h_ref = refs
            xn = x_ref[...]
        rstd = lax.rsqrt(jnp.mean(xn * xn, axis=-1, keepdims=True) + NORM_EPS)
        h = xn * rstd * nw_ref[...] * (1.0 + sc_ref[...]) + sh_ref[...]
        h_ref[...] = h.astype(BF16)

    row, vec = _row_spec(tm, D), _vec_spec(D)
    if has_res:
        ins, in_specs = (x, y, gate, nw, scale, shift), [row, row, vec, vec, vec, vec]
        out_shape = [jax.ShapeDtypeStruct((L, D), F32), jax.ShapeDtypeStruct((L, D), BF16)]
        out_specs = [row, row]
    else:
        ins, in_specs = (x, nw, scale, shift), [row, vec, vec, vec]
        out_shape = [jax.ShapeDtypeStruct((L, D), BF16)]
        out_specs = [row]
    outs = _pcall(body, name=name, grid=(L // tm,), out_shape=out_shape, in_specs=in_specs,
                  out_specs=out_specs, compiler_params=_cparams(("parallel",)))(*ins)
    return outs if has_res else (x, outs[0])


def _gated_branch_bwd(dx, branch, y_ref, g_ref, dy_ref, dg_ref):
    if branch is None:
        return
    dy_ref[...] = (g_ref[...] * dx).astype(BF16)
    dg_ref[...] += jnp.sum(dx * y_ref[...], axis=0, keepdims=True)


def _norm_mod_bwd(dh, x, nw, scale, dres, name, branch=None, tm=256):
    L, D = x.shape
    tm = _tile(L, tm)
    nb = 0 if branch is None else 2

    def body(dh_ref, x_ref, nw_ref, sc_ref, dres_ref, *rest):
        y_ref, g_ref = rest[:nb] if nb else (None, None)
        dx_ref, dsh_ref, dsc_ref, dnw_ref = rest[nb:nb + 4]
        dy_ref, dg_ref = rest[nb + 4:] if nb else (None, None)

        @pl.when(pl.program_id(0) == 0)
        def _():
            dsh_ref[...] = jnp.zeros_like(dsh_ref)
            dsc_ref[...] = jnp.zeros_like(dsc_ref)
            dnw_ref[...] = jnp.zeros_like(dnw_ref)
            if nb:
                dg_ref[...] = jnp.zeros_like(dg_ref)

        xv = x_ref[...]
        dh_v = dh_ref[...]
        nw_v = nw_ref[...]
        rstd = lax.rsqrt(jnp.mean(xv * xv, axis=-1, keepdims=True) + NORM_EPS)
        xhat = xv * rstd
        dsh_ref[...] += jnp.sum(dh_v, axis=0, keepdims=True)
        dsc_ref[...] += jnp.sum(dh_v * (xhat * nw_v), axis=0, keepdims=True)
        dr = dh_v * (1.0 + sc_ref[...])
        dnw_ref[...] += jnp.sum(dr * xhat, axis=0, keepdims=True)
        dxh = dr * nw_v
        dx = rstd * (dxh - xhat * jnp.mean(dxh * xhat, axis=-1, keepdims=True)) + dres_ref[...]
        dx_ref[...] = dx
        _gated_branch_bwd(dx, branch, y_ref, g_ref, dy_ref, dg_ref)

    row, vec = _row_spec(tm, D), _vec_spec(D)
    extra_in = [] if branch is None else list(branch)
    return _pcall(
        body, name=name, grid=(L // tm,),
        out_shape=[jax.ShapeDtypeStruct((L, D), F32)] + [jax.ShapeDtypeStruct((1, D), F32)] * 3
        + ([jax.ShapeDtypeStruct((L, D), BF16), jax.ShapeDtypeStruct((1, D), F32)] if nb else []),
        in_specs=[row, row, vec, vec, row] + ([row, vec] if nb else []),
        out_specs=[row, vec, vec, vec] + ([row, vec] if nb else []),
        compiler_params=_cparams(("arbitrary",)))(dh, x, nw, scale, dres, *extra_in)


def _final_loss(x, y, gate, fw, target, name, tm=256):
    L, D = x.shape
    tm = _tile(L, tm)

    def body(x_ref, y_ref, g_ref, fw_ref, t_ref, dx_ref, loss_ref, dfw_ref, dy_ref, dg_ref):
        @pl.when(pl.program_id(0) == 0)
        def _():
            loss_ref[...] = jnp.zeros_like(loss_ref)
            dfw_ref[...] = jnp.zeros_like(dfw_ref)
            dg_ref[...] = jnp.zeros_like(dg_ref)

        xn = x_ref[...] + g_ref[...] * y_ref[...]
        fw_v = fw_ref[...]
        rstd = lax.rsqrt(jnp.mean(xn * xn, axis=-1, keepdims=True) + NORM_EPS)
        xhat = xn * rstd
        diff = xhat * fw_v - t_ref[...]
        loss_ref[...] += jnp.sum(diff * diff, axis=0, keepdims=True)
        dyf = diff * (1.0 / D)
        dfw_ref[...] += jnp.sum(dyf * xhat, axis=0, keepdims=True)
        dxh = dyf * fw_v
        dx = rstd * (dxh - xhat * jnp.mean(dxh * xhat, axis=-1, keepdims=True))
        dx_ref[...] = dx
        _gated_branch_bwd(dx, True, y_ref, g_ref, dy_ref, dg_ref)

    row, vec = _row_spec(tm, D), _vec_spec(D)
    return _pcall(
        body, name=name, grid=(L // tm,),
        out_shape=[jax.ShapeDtypeStruct((L, D), F32), jax.ShapeDtypeStruct((1, D), F32),
                   jax.ShapeDtypeStruct((1, D), F32), jax.ShapeDtypeStruct((L, D), BF16),
                   jax.ShapeDtypeStruct((1, D), F32)],
        in_specs=[row, row, vec, vec, row], out_specs=[row, vec, vec, row, vec],
        compiler_params=_cparams(("arbitrary",)))(x, y, gate, fw, target)


def _shift_down(v, s, row):
    if s == 0:
        return v
    return jnp.where(row >= s, pltpu.roll(v, s, 0), 0.0)


def _shift_up(v, s, row):
    if s == 0:
        return v
    n = v.shape[0]
    return jnp.where(row < n - s, pltpu.roll(v, n - s, 0), 0.0)


def _ssd_conv_fwd(zx, w, b, col0, width, name, cb=128):
    L = zx.shape[0]
    nb = width // cb
    off = col0 // cb

    def body(x_ref, w_ref, b_ref, o_ref):
        xv = x_ref[...]
        row = lax.broadcasted_iota(jnp.int32, xv.shape, 0)
        acc = b_ref[...] + w_ref[SSD_K - 1:SSD_K, :] * xv
        for s in range(1, SSD_K):
            acc = acc + w_ref[SSD_K - 1 - s:SSD_K - s, :] * _shift_down(xv, s, row)
        o_ref[...] = acc * _sigmoid(acc)

    return _pcall(
        body, name=name, grid=(nb,), out_shape=jax.ShapeDtypeStruct((L, width), F32),
        in_specs=[pl.BlockSpec((L, cb), lambda j: (0, off + j)),
                  pl.BlockSpec((SSD_K, cb), lambda j: (0, j)),
                  pl.BlockSpec((1, cb), lambda j: (0, j))],
        out_specs=pl.BlockSpec((L, cb), lambda j: (0, j)),
        compiler_params=_cparams(("parallel",)))(zx, w, b)


def _ssd_conv_bwd(zx, w, b, d_parts, dzx, col0, name, cb=128):
    L = zx.shape[0]
    widths = [p.shape[1] for p in d_parts]
    width = sum(widths)
    nb = width // cb
    off = col0 // cb
    starts = [sum(widths[:i]) // cb for i in range(len(d_parts))]
    counts = [wd // cb for wd in widths]

    def body(x_ref, w_ref, b_ref, *rest):
        d_refs = rest[:len(d_parts)]
        dx_ref, dw_ref, db_ref = rest[len(d_parts) + 1:]
        j = pl.program_id(0)
        d_val = d_refs[-1][...]
        for i in range(len(d_parts) - 2, -1, -1):
            d_val = jnp.where(j < starts[i + 1], d_refs[i][...], d_val)
        xv = x_ref[...]
        row = lax.broadcasted_iota(jnp.int32, xv.shape, 0)
        shifted = [_shift_down(xv, s, row) for s in range(SSD_K)]
        acc = b_ref[...] + w_ref[SSD_K - 1:SSD_K, :] * xv
        for s in range(1, SSD_K):
            acc = acc + w_ref[SSD_K - 1 - s:SSD_K - s, :] * shifted[s]
        sig = _sigmoid(acc)
        dpre = d_val * (sig * (1.0 + acc * (1.0 - sig)))
        db_ref[...] = jnp.sum(dpre, axis=0, keepdims=True)
        dx = w_ref[SSD_K - 1:SSD_K, :] * dpre
        for s in range(SSD_K):
            dw_ref[SSD_K - 1 - s:SSD_K - s, :] = jnp.sum(dpre * shifted[s], axis=0, keepdims=True)
            if s:
                dx = dx + w_ref[SSD_K - 1 - s:SSD_K - s, :] * _shift_up(dpre, s, row)
        dx_ref[...] = dx.astype(BF16)

    def part_spec(i):
        return pl.BlockSpec((L, cb), lambda j: (0, jnp.clip(j - starts[i], 0, counts[i] - 1)))

    return _pcall(
        body, name=name, grid=(nb,),
        out_shape=[jax.ShapeDtypeStruct(dzx.shape, BF16), jax.ShapeDtypeStruct((SSD_K, width), F32),
                   jax.ShapeDtypeStruct((1, width), F32)],
        in_specs=[pl.BlockSpec((L, cb), lambda j: (0, off + j)),
                  pl.BlockSpec((SSD_K, cb), lambda j: (0, j)),
                  pl.BlockSpec((1, cb), lambda j: (0, j))]
        + [part_spec(i) for i in range(len(d_parts))] + [pl.BlockSpec(memory_space=pl.ANY)],
        out_specs=[pl.BlockSpec((L, cb), lambda j: (0, off + j)),
                   pl.BlockSpec((SSD_K, cb), lambda j: (0, j)),
                   pl.BlockSpec((1, cb), lambda j: (0, j))],
        input_output_aliases={3 + len(d_parts): 0},
        compiler_params=_cparams(("parallel",)))(zx, w, b, *d_parts, dzx)


def _dzx_finish(dzx, ddt, col0, name, tl=512):
    G, L, _ = ddt.shape
    tail = dzx.shape[1] - col0
    tl = _tile(L, tl)

    def body(ddt_ref, dzx_ref, o_ref):
        s = ddt_ref[0]
        for g in range(1, G):
            s = s + ddt_ref[g]
        o_ref[:, 0:LANES] = s.astype(o_ref.dtype)
        if tail > LANES:
            o_ref[:, LANES:] = jnp.zeros((tl, tail - LANES), o_ref.dtype)

    return _pcall(
        body, name=name, grid=(L // tl,), out_shape=jax.ShapeDtypeStruct(dzx.shape, dzx.dtype),
        in_specs=[pl.BlockSpec((G, tl, LANES), lambda i: (0, i, 0)), pl.BlockSpec(memory_space=pl.ANY)],
        out_specs=pl.BlockSpec((tl, tail), lambda i: (i, col0 // tail)),
        input_output_aliases={1: 0},
        compiler_params=_cparams(("parallel",)))(ddt, dzx)


def _sc_conv_fwd(proj, w, name, cb=128):
    L = proj.shape[0]
    width = proj.shape[1] // 3
    nb = width // cb

    def body(b_ref, c_ref, x_ref, w_ref, o_ref):
        q = c_ref[...] * x_ref[...]
        row = lax.broadcasted_iota(jnp.int32, q.shape, 0)
        acc = w_ref[SC_K - 1:SC_K, :] * q
        for s in range(1, SC_K):
            acc = acc + w_ref[SC_K - 1 - s:SC_K - s, :] * _shift_down(q, s, row)
        o_ref[...] = (b_ref[...] * acc).astype(BF16)

    return _pcall(
        body, name=name, grid=(nb,), out_shape=jax.ShapeDtypeStruct((L, width), BF16),
        in_specs=[pl.BlockSpec((L, cb), lambda j: (0, j)),
                  pl.BlockSpec((L, cb), lambda j: (0, nb + j)),
                  pl.BlockSpec((L, cb), lambda j: (0, 2 * nb + j)),
                  pl.BlockSpec((SC_K, cb), lambda j: (0, j))],
        out_specs=pl.BlockSpec((L, cb), lambda j: (0, j)),
        compiler_params=_cparams(("parallel",)))(proj, proj, proj, w)


def _sc_conv_bwd(proj, w, dy, name, cb=128):
    L = proj.shape[0]
    width = proj.shape[1] // 3
    nb = width // cb

    def body(b_ref, c_ref, x_ref, w_ref, dy_ref, db_ref, dc_ref, dxv_ref, dw_ref):
        cg, xv, dyv = c_ref[...], x_ref[...], dy_ref[...]
        q = cg * xv
        row = lax.broadcasted_iota(jnp.int32, q.shape, 0)
        shifted = [_shift_down(q, s, row) for s in range(SC_K)]
        conv = w_ref[SC_K - 1:SC_K, :] * q
        for s in range(1, SC_K):
            conv = conv + w_ref[SC_K - 1 - s:SC_K - s, :] * shifted[s]
        db_ref[...] = (dyv * conv).astype(BF16)
        dconv = dyv * b_ref[...]
        dq = w_ref[SC_K - 1:SC_K, :] * dconv
        for s in range(SC_K):
            dw_ref[SC_K - 1 - s:SC_K - s, :] = jnp.sum(dconv * shifted[s], axis=0, keepdims=True)
            if s:
                dq = dq + w_ref[SC_K - 1 - s:SC_K - s, :] * _shift_up(dconv, s, row)
        dc_ref[...] = (dq * xv).astype(BF16)
        dxv_ref[...] = (dq * cg).astype(BF16)

    blk = pl.BlockSpec((L, cb), lambda j: (0, j))
    wblk = pl.BlockSpec((SC_K, cb), lambda j: (0, j))
    return _pcall(
        body, name=name, grid=(nb,),
        out_shape=[jax.ShapeDtypeStruct((L, width), BF16)] * 3 + [jax.ShapeDtypeStruct((SC_K, width), F32)],
        in_specs=[blk, pl.BlockSpec((L, cb), lambda j: (0, nb + j)),
                  pl.BlockSpec((L, cb), lambda j: (0, 2 * nb + j)), wblk, blk],
        out_specs=[blk, blk, blk, wblk],
        compiler_params=_cparams(("parallel",)))(proj, proj, proj, w, dy)


def _split3(v):
    hi = v.astype(BF16)
    r1 = v - hi.astype(F32)
    mid = r1.astype(BF16)
    lo = (r1 - mid.astype(F32)).astype(BF16)
    return hi, mid, lo


def _dot_exact01(t01, v):
    hi, mid, lo = _split3(v)
    return _dot(t01, hi) + _dot(t01, mid) + _dot(t01, lo)


def _lane_col(v, lane, h):
    return jnp.sum(jnp.where(lane == h, v, 0.0), axis=1, keepdims=True)


def _sum_all(v):
    return jnp.sum(jnp.sum(v, axis=1, keepdims=True), axis=0, keepdims=True)


def _softplus(x):
    return jnp.maximum(x, 0.0) + jnp.log1p(jnp.exp(-jnp.abs(x)))


def _ssd_common(dt_ref, bias_ref, alog_ref, b_ref, c_ref, cst_ref, heads):
    c_sz = SSD_CHUNK
    lane = lax.broadcasted_iota(jnp.int32, (c_sz, LANES), 1)
    row = lax.broadcasted_iota(jnp.int32, (c_sz, LANES), 0)
    valid = lane < heads
    raw = dt_ref[...] + bias_ref[...]
    dt = _softplus(raw)
    a_row = -jnp.exp(alog_ref[...])
    a = jnp.where(valid, dt * a_row, 0.0)
    tri = (row >= lane).astype(BF16)
    cs = _dot_exact01(tri, a)
    cst_ref[...] = cs.T
    last_row = jnp.sum(a, axis=0, keepdims=True)
    bb = b_ref[...].astype(BF16)
    cb = c_ref[...].astype(BF16)
    scores = _dot(cb, bb, "nt")
    return dict(lane=lane, row=row, valid=valid, raw=raw, dt=dt, a_row=a_row, cs=cs,
                last_row=last_row, bb=bb, cb=cb, scores=scores, causal=row >= lane, lo=lane < SSD_P)


def _pair_terms(q, cst_ref, h0):
    lane, lo = q["lane"], q["lo"]
    out = {}
    cols, dts, lasts, lms = [], [], [], []
    lane1 = lax.broadcasted_iota(jnp.int32, (1, LANES), 1)
    for h in (h0, h0 + 1):
        col = _lane_col(q["cs"], lane, h)
        rowv = cst_ref[pl.ds(h, 1), :]
        lms.append(jnp.exp(jnp.where(q["causal"], col - rowv, -1e30)))
        cols.append(col)
        dts.append(_lane_col(q["dt"], lane, h))
        lasts.append(jnp.sum(jnp.where(lane1 == h, q["last_row"], 0.0), axis=1, keepdims=True))
    out["lm"] = lms
    out["cols"] = cols
    out["lasts"] = lasts
    out["dt_b"] = jnp.where(lo, dts[0], dts[1])
    out["e_b"] = jnp.where(lo, jnp.exp(cols[0]), jnp.exp(cols[1]))
    out["dec_cols"] = [jnp.exp(lasts[0] - cols[0]), jnp.exp(lasts[1] - cols[1])]
    out["dec_b"] = jnp.where(lo, out["dec_cols"][0], out["dec_cols"][1])
    lo1 = lane1 < SSD_P
    out["explast"] = [jnp.exp(lasts[0]), jnp.exp(lasts[1])]
    out["explast_b"] = jnp.where(lo1, out["explast"][0], out["explast"][1])
    return out


def _ssd_fwd(zx, xc, bias_p, alog_p, d_lane, nw, d_inner, name):
    L = zx.shape[0]
    nc = L // SSD_CHUNK
    gw = d_inner // SSD_G
    heads = gw // SSD_P
    n_pair = heads // 2
    bc0 = d_inner // LANES
    dt0 = (2 * d_inner + 2 * SSD_G * SSD_N) // LANES

    def body(z_ref, xs_ref, b_ref, c_ref, dt_ref, bias_ref, alog_ref, dl_ref, nw_ref,
             y_ref, yn_ref, prev_ref, s_ref, cst_ref):
        @pl.when(pl.program_id(1) == 0)
        def _():
            s_ref[...] = jnp.zeros_like(s_ref)

        q = _ssd_common(dt_ref, bias_ref, alog_ref, b_ref, c_ref, cst_ref, SSD_G * heads)
        prev_ref[...] = s_ref[...]
        lo = q["lo"]
        for j in range(n_pair):
            sl = slice(j * LANES, (j + 1) * LANES)
            p = _pair_terms(q, cst_ref, pl.program_id(0) * heads + 2 * j)
            xs_p = xs_ref[:, sl]
            xp = xs_p * p["dt_b"]
            xb = xp.astype(BF16)
            m_a = (q["scores"] * p["lm"][0]).astype(BF16)
            m_b = (q["scores"] * p["lm"][1]).astype(BF16)
            yd = jnp.where(lo, _dot(m_a, xb), _dot(m_b, xb))
            s_p = s_ref[:, sl]
            yo = _dot(q["cb"], s_p.astype(BF16)) * p["e_b"]
            y_ref[:, sl] = yd + yo + dl_ref[:, sl] * xs_p
            st = _dot(q["bb"], (xp * p["dec_b"]).astype(BF16), "tn")
            s_ref[:, sl] = s_p * p["explast_b"] + st
        yv = y_ref[...]
        zv = z_ref[...]
        yg = yv * (zv * _sigmoid(zv))
        rstd = lax.rsqrt(jnp.mean(yg * yg, axis=-1, keepdims=True) + NORM_EPS)
        yn_ref[...] = (yg * rstd * nw_ref[...]).astype(BF16)

    grp = lambda width: pl.BlockSpec((None, 1, width), lambda g, c: (g, 0, 0))
    head_vec = pl.BlockSpec((1, LANES), lambda g, c: (0, 0))
    return _pcall(
        body, name=name, grid=(SSD_G, nc),
        out_shape=[jax.ShapeDtypeStruct((L, d_inner), F32), jax.ShapeDtypeStruct((L, d_inner), BF16),
                   jax.ShapeDtypeStruct((nc, SSD_G, SSD_N, gw), F32)],
        in_specs=[pl.BlockSpec((SSD_CHUNK, gw), lambda g, c: (c, g)),
                  pl.BlockSpec((SSD_CHUNK, gw), lambda g, c: (c, g)),
                  pl.BlockSpec((SSD_CHUNK, SSD_N), lambda g, c: (c, bc0 + g)),
                  pl.BlockSpec((SSD_CHUNK, SSD_N), lambda g, c: (c, bc0 + SSD_G + g)),
                  pl.BlockSpec((SSD_CHUNK, LANES), lambda g, c: (c, dt0)),
                  head_vec, head_vec, grp(gw), grp(gw)],
        out_specs=[pl.BlockSpec((SSD_CHUNK, gw), lambda g, c: (c, g)),
                   pl.BlockSpec((SSD_CHUNK, gw), lambda g, c: (c, g)),
                   pl.BlockSpec((None, None, SSD_N, gw), lambda g, c: (c, g, 0, 0))],
        scratch_shapes=[pltpu.VMEM((SSD_N, gw), F32), pltpu.VMEM((SSD_CHUNK, LANES), F32)],
        compiler_params=_cparams(("parallel", "arbitrary")))(zx, xc, xc, xc, zx, bias_p, alog_p, d_lane, nw)


def _ssd_bwd(dyn, y, zx, xc, prev, bias_p, alog_p, d_lane, nw, d_inner, name):
    L = zx.shape[0]
    nc = L // SSD_CHUNK
    gw = d_inner // SSD_G
    heads = gw // SSD_P
    n_pair = heads // 2
    bc0 = d_inner // LANES
    dt0 = (2 * d_inner + 2 * SSD_G * SSD_N) // LANES

    def body(dyn_ref, y_ref, z_ref, xs_ref, b_ref, c_ref, dt_ref, prev_ref, bias_ref, alog_ref, dl_ref, nw_ref,
             dz_ref, dxs_ref, db_ref, dc_ref, ddt_ref, dbias_ref, dalog_ref, dd_ref, dnw_ref,
             ds_ref, cst_ref, racc_ref):
        @pl.when(pl.program_id(1) == 0)
        def _():
            ds_ref[...] = jnp.zeros_like(ds_ref)
            dbias_ref[...] = jnp.zeros_like(dbias_ref)
            dalog_ref[...] = jnp.zeros_like(dalog_ref)
            dd_ref[...] = jnp.zeros_like(dd_ref)
            dnw_ref[...] = jnp.zeros_like(dnw_ref)

        q = _ssd_common(dt_ref, bias_ref, alog_ref, b_ref, c_ref, cst_ref, SSD_G * heads)
        lane, row, lo = q["lane"], q["row"], q["lo"]
        lane1 = lax.broadcasted_iota(jnp.int32, (1, LANES), 1)
        head0 = pl.program_id(0) * heads
        mine = (lane >= head0) & (lane < head0 + heads)

        yv, zv, dynv, nwv = y_ref[...], z_ref[...], dyn_ref[...], nw_ref[...]
        sig = _sigmoid(zv)
        sz = zv * sig
        yg = yv * sz
        rstd = lax.rsqrt(jnp.mean(yg * yg, axis=-1, keepdims=True) + NORM_EPS)
        yhat = yg * rstd
        dnw_ref[...] += jnp.sum(dynv * yhat, axis=0, keepdims=True)
        dyh = dynv * nwv
        dyg = rstd * (dyh - yhat * jnp.mean(dyh * yhat, axis=-1, keepdims=True))
        dz_ref[...] = (dyg * yv * (sig * (1.0 + zv * (1.0 - sig)))).astype(BF16)
        dy_all = dyg * sz

        dg = jnp.zeros((SSD_CHUNK, SSD_CHUNK), F32)
        dc_acc = jnp.zeros((SSD_CHUNK, SSD_N), F32)
        db_acc = jnp.zeros((SSD_CHUNK, SSD_N), F32)
        dcs_mat = jnp.zeros((SSD_CHUNK, LANES), F32)
        ddt_mat = jnp.zeros((SSD_CHUNK, LANES), F32)
        dd_row = jnp.zeros((1, LANES), F32)
        racc_ref[...] = jnp.zeros_like(racc_ref)
        is_last = row == SSD_CHUNK - 1

        for j in range(n_pair):
            sl = slice(j * LANES, (j + 1) * LANES)
            ha, hb = head0 + 2 * j, head0 + 2 * j + 1
            p = _pair_terms(q, cst_ref, ha)
            xs_p = xs_ref[:, sl]
            dyp = dy_all[:, sl]
            xp = xs_p * p["dt_b"]
            xb = xp.astype(BF16)
            s_p = prev_ref[:, sl]
            s_pb = s_p.astype(BF16)
            dsn = ds_ref[:, sl]
            dsnb = dsn.astype(BF16)
            m_f = [q["scores"] * p["lm"][0], q["scores"] * p["lm"][1]]

            t0 = dyp * xs_p
            dd_row = dd_row + jnp.where(lane1 == ha, _sum_all(jnp.where(lo, t0, 0.0)), 0.0) \
                + jnp.where(lane1 == hb, _sum_all(jnp.where(lo, 0.0, t0)), 0.0)
            dxs_p = dl_ref[:, sl] * dyp

            yo = _dot(q["cb"], s_pb) * p["e_b"]
            dcs_b = (dyp * p["e_b"]).astype(BF16)
            dc_acc = dc_acc + _dot(dcs_b, s_pb, "nt")
            ds_yo = _dot(q["cb"], dcs_b, "tn")
            t1 = dyp * yo
            dcs_cols = [jnp.sum(jnp.where(lo, t1, 0.0), axis=1, keepdims=True),
                        jnp.sum(jnp.where(lo, 0.0, t1), axis=1, keepdims=True)]

            t2 = dsn * s_p
            dlast = [p["explast"][0] * _sum_all(jnp.where(lo, t2, 0.0)),
                     p["explast"][1] * _sum_all(jnp.where(lo, 0.0, t2))]
            ds_ref[:, sl] = dsn * p["explast_b"] + ds_yo
            w = _dot(q["bb"], dsnb)
            db_acc = db_acc + _dot((xp * p["dec_b"]).astype(BF16), dsnb, "nt")
            dxp = w * p["dec_b"]
            t3 = w * xp
            e = [jnp.sum(jnp.where(lo, t3, 0.0), axis=1, keepdims=True) * p["dec_cols"][0],
                 jnp.sum(jnp.where(lo, 0.0, t3), axis=1, keepdims=True) * p["dec_cols"][1]]
            for i in range(2):
                dlast[i] = dlast[i] + jnp.sum(e[i], axis=0, keepdims=True)
                dcs_cols[i] = dcs_cols[i] - e[i]

            dyb = dyp.astype(BF16)
            dy_h = [jnp.where(lo, dyp, 0.0).astype(BF16), jnp.where(lo, 0.0, dyp).astype(BF16)]
            dms = [_dot(dy_h[0], xb, "nt"), _dot(dy_h[1], xb, "nt")]
            dxp = dxp + jnp.where(lo, _dot(m_f[0].astype(BF16), dyb, "tn"), _dot(m_f[1].astype(BF16), dyb, "tn"))
            for i, h in enumerate((ha, hb)):
                dg = dg + dms[i] * p["lm"][i]
                qm = dms[i] * m_f[i]
                dcs_cols[i] = dcs_cols[i] + jnp.sum(qm, axis=1, keepdims=True)
                racc_ref[pl.ds(h, 1), :] = jnp.sum(qm, axis=0, keepdims=True)

            dxs_ref[:, sl] = dxs_p + dxp * p["dt_b"]
            t4 = dxp * xs_p
            ddt_cols = [jnp.sum(jnp.where(lo, t4, 0.0), axis=1, keepdims=True),
                        jnp.sum(jnp.where(lo, 0.0, t4), axis=1, keepdims=True)]
            for i, h in enumerate((ha, hb)):
                sel = lane == h
                dcs_mat = dcs_mat + jnp.where(sel, dcs_cols[i], 0.0) + jnp.where(sel & is_last, dlast[i], 0.0)
                ddt_mat = ddt_mat + jnp.where(sel, ddt_cols[i], 0.0)

        dcs_mat = dcs_mat - racc_ref[...].T
        tri_t = (row <= lane).astype(BF16)
        da = _dot_exact01(tri_t, dcs_mat)
        ddt = ddt_mat + da * q["a_row"]
        dalog_ref[...] += jnp.sum(jnp.where(mine, da * q["dt"], 0.0), axis=0, keepdims=True) * q["a_row"]
        draw = jnp.where(mine, ddt * _sigmoid(q["raw"]), 0.0)
        ddt_ref[...] = draw
        dbias_ref[...] += jnp.sum(draw, axis=0, keepdims=True)
        dd_ref[...] += dd_row
        dgb = dg.astype(BF16)
        dc_ref[...] = dc_acc + _dot(dgb, q["bb"])
        db_ref[...] = db_acc + _dot(dgb, q["cb"], "tn")

    rev = lambda c: nc - 1 - c
    grp = lambda width: pl.BlockSpec((None, 1, width), lambda g, c: (g, 0, 0))
    blk = lambda width, off: pl.BlockSpec((SSD_CHUNK, width), lambda g, c: (rev(c), off + g))
    head_vec = pl.BlockSpec((1, LANES), lambda g, c: (0, 0))
    return _pcall(
        body, name=name, grid=(SSD_G, nc),
        out_shape=[jax.ShapeDtypeStruct(zx.shape, BF16), jax.ShapeDtypeStruct((L, d_inner), F32),
                   jax.ShapeDtypeStruct((L, SSD_G * SSD_N), F32), jax.ShapeDtypeStruct((L, SSD_G * SSD_N), F32),
                   jax.ShapeDtypeStruct((SSD_G, L, LANES), F32),
                   jax.ShapeDtypeStruct((SSD_G, 1, LANES), F32), jax.ShapeDtypeStruct((SSD_G, 1, LANES), F32),
                   jax.ShapeDtypeStruct((SSD_G, 1, LANES), F32), jax.ShapeDtypeStruct((SSD_G, 1, gw), F32)],
        in_specs=[blk(gw, 0), blk(gw, 0), blk(gw, 0), blk(gw, 0), blk(SSD_N, bc0), blk(SSD_N, bc0 + SSD_G),
                  pl.BlockSpec((SSD_CHUNK, LANES), lambda g, c: (rev(c), dt0)),
                  pl.BlockSpec((None, None, SSD_N, gw), lambda g, c: (rev(c), g, 0, 0)),
                  head_vec, head_vec, grp(gw), grp(gw)],
        out_specs=[blk(gw, 0), blk(gw, 0), blk(SSD_N, 0), blk(SSD_N, 0),
                   pl.BlockSpec((None, SSD_CHUNK, LANES), lambda g, c: (g, rev(c), 0)),
                   grp(LANES), grp(LANES), grp(LANES), grp(gw)],
        scratch_shapes=[pltpu.VMEM((SSD_N, gw), F32), pltpu.VMEM((SSD_CHUNK, LANES), F32),
                        pltpu.VMEM((SSD_CHUNK, LANES), F32)],
        compiler_params=_cparams(("parallel", "arbitrary")))(
            dyn, y, zx, xc, xc, xc, zx, prev, bias_p, alog_p, d_lane, nw)


def _cond_mod(c_pad, ada_w, ada_b_loc, after, name):
    depth, D, n = ada_w.shape
    rows = c_pad.shape[0]

    def body(c_ref, w_ref, b_ref, after_ref, mod_ref, cond_ref):
        cv = c_ref[...]
        cond = cv * _sigmoid(cv)
        cond_ref[...] = cond
        mod_ref[...] = _dot(cond.astype(BF16), w_ref[...].astype(BF16)) + b_ref[...]

    return _pcall(
        body, name=name, grid=(depth,),
        out_shape=[jax.ShapeDtypeStruct((depth, rows, n), F32), jax.ShapeDtypeStruct((rows, D), F32)],
        in_specs=[pl.BlockSpec((rows, D), lambda i: (0, 0)),
                  pl.BlockSpec((None, D, n), lambda i: (i, 0, 0)),
                  pl.BlockSpec((None, 1, n), lambda i: (i, 0, 0)),
                  pl.BlockSpec(memory_space=pl.ANY)],
        out_specs=[pl.BlockSpec((None, rows, n), lambda i: (i, 0, 0)),
                   pl.BlockSpec((rows, D), lambda i: (0, 0))],
        compiler_params=_cparams(("arbitrary",)))(c_pad, ada_w, ada_b_loc, after)


def _adamw_math(g, w, m, v):
    m_new = ADAM_B1 * m + (1.0 - ADAM_B1) * g
    v_new = ADAM_B2 * v + (1.0 - ADAM_B2) * (g * g)
    m_hat = m_new / (1.0 - ADAM_B1 ** ADAM_STEP)
    v_hat = v_new / (1.0 - ADAM_B2 ** ADAM_STEP)
    delta = -ADAM_LR * (m_hat / (jnp.sqrt(v_hat) + ADAM_EPS) + ADAM_WD * w)
    return delta, m_new, v_new


def _adamw_sum(parts, w, m, v, layer, name, prev=None, tr=256, window_off=None):
    depth, R, C = w.shape
    tr = _tile(R, tr)
    win = parts.shape[2]
    scratch = [] if window_off is None else [pltpu.VMEM((tr, win), F32)]

    def body(p_ref, w_ref, m_ref, v_ref, *rest):
        g_ref, d_ref, mo_ref, vo_ref = rest[-4 - len(scratch):len(rest) - len(scratch)]
        g = p_ref[0].astype(F32)
        for k in range(1, N_DEV):
            g = g + p_ref[k].astype(F32)
        if window_off is not None:
            me = _my_index()
            off = 0
            for k in range(N_DEV):
                off = jnp.where(me == k, window_off[k], off)
            src = lax.broadcasted_iota(jnp.int32, (win, win), 0)
            dst = lax.broadcasted_iota(jnp.int32, (win, win), 1)
            shift = ((src == dst + off) & (dst < C)).astype(BF16)
            hi, mid, lo = _split3(g)
            rest[-1][...] = _dot(hi, shift) + _dot(mid, shift) + _dot(lo, shift)
            g = rest[-1][:, 0:C]
        d, mn, vn = _adamw_math(g, w_ref[...], m_ref[...], v_ref[...])
        g_ref[...] = g
        d_ref[...] = d
        mo_ref[...] = mn
        vo_ref[...] = vn

    blk = pl.BlockSpec((None, tr, C), lambda i: (layer, i, 0))
    prev = list(prev) if prev is not None else []
    return _pcall(
        body, name=name, grid=(R // tr,),
        out_shape=[jax.ShapeDtypeStruct((depth, R, C), F32)] * 4,
        in_specs=[pl.BlockSpec((N_DEV, tr, win), lambda i: (0, i, 0)), blk, blk, blk]
        + [pl.BlockSpec(memory_space=pl.ANY)] * len(prev),
        out_specs=[blk] * 4, input_output_aliases={4 + k: k for k in range(len(prev))},
        scratch_shapes=scratch,
        compiler_params=_cparams(("parallel",)))(parts, w, m, v, *prev)


def _adamw_small(parts, wmv, head_parts, head_wmv, loss_parts, name):
    n, nh = len(parts), len(head_parts)
    n_heads = head_wmv[0][0].shape[1] if nh else 0
    groups = head_parts[0].shape[1] if nh else 0
    d_model = loss_parts.shape[2]

    def body(*refs):
        p_refs, refs = refs[:n], refs[n:]
        wmv_refs, refs = refs[:3 * n], refs[3 * n:]
        hp_refs, refs = refs[:nh], refs[nh:]
        hwmv_refs, refs = refs[:3 * nh], refs[3 * nh:]
        loss_ref, refs = refs[0], refs[1:]
        outs, loss_out, head_scr = refs[:4 * (n + nh)], refs[4 * (n + nh)], refs[4 * (n + nh) + 1]

        def update(i, g, w_ref, m_ref, v_ref):
            res = (g,) + _adamw_math(g, w_ref[...], m_ref[...], v_ref[...])
            for o_ref, r in zip(outs[4 * i:4 * i + 4], res):
                o_ref[...] = r

        for i in range(n):
            g = p_refs[i][0]
            for k in range(1, N_DEV):
                g = g + p_refs[i][k]
            update(i, g, *wmv_refs[3 * i:3 * i + 3])
        for i in range(nh):
            g = None
            for k in range(N_DEV):
                for grp in range(groups):
                    g = hp_refs[i][k, grp] if g is None else g + hp_refs[i][k, grp]
            head_scr[...] = g
            update(n + i, head_scr[:, 0:n_heads], *hwmv_refs[3 * i:3 * i + 3])
        tot = loss_ref[0]
        for k in range(1, N_DEV):
            tot = tot + loss_ref[k]
        loss_out[...] = jnp.broadcast_to(_sum_all(tot) * (0.5 / d_model), loss_out.shape)

    operands = list(parts) + [a for t in wmv for a in t] + list(head_parts) + [a for t in head_wmv for a in t]
    operands.append(loss_parts)
    out_shape = [jax.ShapeDtypeStruct(t[0].shape, F32) for t in list(wmv) + list(head_wmv) for _ in range(4)]
    out_shape.append(jax.ShapeDtypeStruct((1, LANES), F32))
    vmem = pl.BlockSpec(memory_space=pltpu.VMEM)
    outs = _pcall(body, name=name, out_shape=out_shape, in_specs=[vmem] * len(operands),
                  out_specs=[vmem] * len(out_shape), scratch_shapes=[pltpu.VMEM((1, LANES), F32)],
                  compiler_params=_cparams())(*operands)
    return [outs[4 * i:4 * i + 4] for i in range(n + nh)], outs[-1]


def _ada_adamw(cond_pad, dmod_pad, w, m, v, name, tr=256):
    depth, D, n = w.shape
    rows = cond_pad.shape[0]
    tr = _tile(D, tr)

    def body(c_ref, dm_ref, w_ref, m_ref, v_ref, g_ref, d_ref, mo_ref, vo_ref):
        g = _dot(c_ref[...].astype(BF16), dm_ref[...].astype(BF16), "tn")
        d, mn, vn = _adamw_math(g, w_ref[...], m_ref[...], v_ref[...])
        g_ref[...] = g
        d_ref[...] = d
        mo_ref[...] = mn
        vo_ref[...] = vn

    blk = pl.BlockSpec((None, tr, n), lambda i, r: (i, r, 0))
    return _pcall(
        body, name=name, grid=(depth, D // tr),
        out_shape=[jax.ShapeDtypeStruct((depth, D, n), F32)] * 4,
        in_specs=[pl.BlockSpec((rows, tr), lambda i, r: (0, r)),
                  pl.BlockSpec((None, rows, n), lambda i, r: (i, 0, 0)), blk, blk, blk],
        out_specs=[blk] * 4, compiler_params=_cparams(("parallel", "parallel")))(cond_pad, dmod_pad, w, m, v)


def kernel(x, c, ada_w, ada_b, mix_norm_w, mlp_norm_w, mlp_up, mlp_down, ssd_in_w, ssd_conv_w, ssd_conv_b, ssd_dt_bias, ssd_A_log, ssd_D, ssd_norm_w, ssd_out_w, sc_in_w, sc_conv_w, sc_out_w, final_norm_w, loss_target, m_ada_w, m_ada_b, m_mix_norm_w, m_mlp_norm_w, m_mlp_up, m_mlp_down, m_ssd_in_w, m_ssd_conv_w, m_ssd_conv_b, m_ssd_dt_bias, m_ssd_A_log, m_ssd_D, m_ssd_norm_w, m_ssd_out_w, m_sc_in_w, m_sc_conv_w, m_sc_out_w, m_final_norm_w, v_ada_w, v_ada_b, v_mix_norm_w, v_mlp_norm_w, v_mlp_up, v_mlp_down, v_ssd_in_w, v_ssd_conv_w, v_ssd_conv_b, v_ssd_dt_bias, v_ssd_A_log, v_ssd_D, v_ssd_norm_w, v_ssd_out_w, v_sc_in_w, v_sc_conv_w, v_sc_out_w, v_final_norm_w):
    weights = dict(ada_w=ada_w, ada_b=ada_b, mix_norm_w=mix_norm_w, mlp_norm_w=mlp_norm_w, mlp_up=mlp_up,
                   mlp_down=mlp_down, ssd_in_w=ssd_in_w, ssd_conv_w=ssd_conv_w, ssd_conv_b=ssd_conv_b,
                   ssd_dt_bias=ssd_dt_bias, ssd_A_log=ssd_A_log, ssd_D=ssd_D, ssd_norm_w=ssd_norm_w,
                   ssd_out_w=ssd_out_w, sc_in_w=sc_in_w, sc_conv_w=sc_conv_w, sc_out_w=sc_out_w,
                   final_norm_w=final_norm_w)
    moms = dict(ada_w=m_ada_w, ada_b=m_ada_b, mix_norm_w=m_mix_norm_w, mlp_norm_w=m_mlp_norm_w, mlp_up=m_mlp_up,
                mlp_down=m_mlp_down, ssd_in_w=m_ssd_in_w, ssd_conv_w=m_ssd_conv_w, ssd_conv_b=m_ssd_conv_b,
                ssd_dt_bias=m_ssd_dt_bias, ssd_A_log=m_ssd_A_log, ssd_D=m_ssd_D, ssd_norm_w=m_ssd_norm_w,
                ssd_out_w=m_ssd_out_w, sc_in_w=m_sc_in_w, sc_conv_w=m_sc_conv_w, sc_out_w=m_sc_out_w,
                final_norm_w=m_final_norm_w)
    vars_ = dict(ada_w=v_ada_w, ada_b=v_ada_b, mix_norm_w=v_mix_norm_w, mlp_norm_w=v_mlp_norm_w, mlp_up=v_mlp_up,
                 mlp_down=v_mlp_down, ssd_in_w=v_ssd_in_w, ssd_conv_w=v_ssd_conv_w, ssd_conv_b=v_ssd_conv_b,
                 ssd_dt_bias=v_ssd_dt_bias, ssd_A_log=v_ssd_A_log, ssd_D=v_ssd_D, ssd_norm_w=v_ssd_norm_w,
                 ssd_out_w=v_ssd_out_w, sc_in_w=v_sc_in_w, sc_conv_w=v_sc_conv_w, sc_out_w=v_sc_out_w,
                 final_norm_w=v_final_norm_w)
    names = list(weights)

    L, D = x.shape[1], x.shape[2]
    d_inner = 2 * D
    n_heads = d_inner // SSD_P
    hpg = n_heads // SSD_G
    gw = d_inner // SSD_G
    conv_dim = d_inner + 2 * SSD_G * SSD_N
    zx_dim = d_inner + conv_dim
    zx_pad = -(-(zx_dim + LANES) // 512) * 512
    in_ws = ssd_in_w.shape[2]
    in_base, in_off, in_win = _window_geometry(in_ws)
    me = _my_index()
    x0 = x[0]
    tgt = loss_target[0]

    n_mod = ada_w.shape[2]
    (c_all,) = _exchange([c], "gather_c", gather=True)
    gather_handle = {}
    (gather_handle["ssd_in_w"],), token_in = _xfer_start(
        [ssd_in_w[0].astype(BF16)], "gather_start_ssd_in_w", gather=True, via_sibling=(0,), after=(c_all,))
    c_pad = jnp.pad(c_all.reshape(N_DEV, D), ((0, 16 - N_DEV), (0, 0)))
    ada_b_loc = lax.dynamic_slice_in_dim(ada_b, me * n_mod, n_mod, axis=1).reshape(2, 1, n_mod)
    mod_blk, cond_pad = _cond_mod(c_pad, ada_w, ada_b_loc, token_in, "cond_mod")
    gather_order = ["mod", "ssd_conv_w", "sc_conv_w", "ssd_out_w", "up0", "down0", "sc_in_w", "sc_out_w", "up1",
                    "down1"]
    gather_src = dict(mod=mod_blk, ssd_conv_w=ssd_conv_w[0], sc_conv_w=sc_conv_w[0],
                      ssd_out_w=ssd_out_w[0].astype(BF16),
                      up0=mlp_up[0].astype(BF16), down0=mlp_down[0].astype(BF16),
                      sc_in_w=sc_in_w[0].astype(BF16), sc_out_w=sc_out_w[0].astype(BF16),
                      up1=mlp_up[1].astype(BF16), down1=mlp_down[1].astype(BF16))
    handles, gather_token = _xfer_start([gather_src[k] for k in gather_order], "gather_start", gather=True,
                                        via_sibling=tuple(range(3, len(gather_order))))
    gather_handle.update(zip(gather_order, handles))

    def gathered(keys, after, forward):
        tag = "_".join(keys)
        lands = _xfer_wait([gather_handle[k] for k in keys], after, f"gather_wait_{tag}", gather=True)
        return _sibling_forward(lands, f"gather_forward_{tag}") if forward else lands

    (ssd_in_g,) = gathered(["ssd_in_w"], (gather_token, m_ssd_in_w, v_ssd_in_w), True)
    w_in_all = _shards_to_columns(ssd_in_g, in_base, in_off, in_win, zx_pad, "ssd_in_w_columns")
    (mod_all,) = gathered(["mod"], w_in_all, False)
    mod_mine = lax.dynamic_index_in_dim(mod_all, me, axis=2, keepdims=False)
    mod_mine = jnp.transpose(mod_mine, (1, 0, 2)).reshape(2, 6, 1, D)
    sh_m, sc_m, g_m, sh_f, sc_f, g_f = [[mod_mine[i, k] for i in range(2)] for k in range(6)]

    vec = lambda a: a.reshape(1, -1)
    grads = {}
    small = {}

    _, h0 = _norm_mod_fwd(x0, None, None, vec(mix_norm_w[0]), sc_m[0], sh_m[0], "l0_mix_norm")
    cw_all, scw_all = gathered(["ssd_conv_w", "sc_conv_w"], h0, False)
    (zx,) = _mm_nn(h0, w_in_all, F32, "ssd_in_proj", tm=2048, tn=512)
    conv_b0 = vec(ssd_conv_b[0])
    conv_w_full = jnp.transpose(cw_all, (1, 0, 2)).reshape(SSD_K, conv_dim)
    sc_conv_full = jnp.transpose(scw_all, (1, 0, 2)).reshape(SC_K, D)
    xc = _ssd_conv_fwd(zx, conv_w_full, conv_b0, d_inner, conv_dim, "ssd_conv")
    bias_p = jnp.pad(ssd_dt_bias[0], (0, LANES - n_heads)).reshape(1, LANES)
    alog_p = jnp.pad(ssd_A_log[0], (0, LANES - n_heads)).reshape(1, LANES)
    d_lane = jnp.repeat(ssd_D[0], SSD_P).reshape(SSD_G, 1, gw)
    nw_g = ssd_norm_w[0].reshape(SSD_G, 1, gw)
    y_ssd, yn, prev = _ssd_fwd(zx, xc, bias_p, alog_p, d_lane, nw_g, d_inner, "ssd_scan")
    ups, downs = [None, None], [None, None]
    ssd_out_g, ups[0], down0_g = gathered(["ssd_out_w", "up0", "down0"], yn, True)
    w_ssd_out, downs[0] = ssd_out_g.reshape(-1, D), down0_g.reshape(-1, D)
    (mix0,) = _mm_nn(yn, w_ssd_out, F32, "ssd_out_proj")
    x1, h1 = _norm_mod_fwd(x0, mix0, g_m[0], vec(mlp_norm_w[0]), sc_f[0], sh_f[0], "l0_mlp_norm")
    u0, s0 = _mm_nn_blocked(h1, ups[0], "l0_mlp_up", _ep_relu2, [BF16, BF16])
    (d0,) = _mm_nn(s0, downs[0], F32, "l0_mlp_down")
    x2, h2 = _norm_mod_fwd(x1, d0, g_f[0], vec(mix_norm_w[1]), sc_m[1], sh_m[1], "l1_mix_norm")
    sc_in_g, sc_out_g, ups[1], down1_g = gathered(["sc_in_w", "sc_out_w", "up1", "down1"], h2, True)
    w_sc_out, downs[1] = sc_out_g.reshape(-1, D), down1_g.reshape(-1, D)
    (proj,) = _mm_nn_blocked(h2, sc_in_g, "sc_in_proj", _ep_store(F32), [F32])
    yc = _sc_conv_fwd(proj, sc_conv_full, "sc_conv")
    (mix1,) = _mm_nn(yc, w_sc_out, F32, "sc_out_proj")
    x3, h3 = _norm_mod_fwd(x2, mix1, g_m[1], vec(mlp_norm_w[1]), sc_f[1], sh_f[1], "l1_mlp_norm")
    u1, s1 = _mm_nn_blocked(h3, ups[1], "l1_mlp_up", _ep_relu2, [BF16, BF16])
    (d1,) = _mm_nn(s1, downs[1], F32, "l1_mlp_down")

    dx, loss_lane, dfw, dd1, dg = _final_loss(x3, d1, g_f[1], vec(final_norm_w), tgt, "final_loss")
    small["final_norm_w"] = dfw

    dmod = [[None] * 6 for _ in range(2)]
    dmod[1][5] = dg

    def mlp_backward(i, dx_out, dd, x_mid, h_in, u, s, mix, gate):
        du = _mm_nt(dd, downs[i], BF16, f"l{i}_mlp_down_bwd", epilogue=_ep_relu2_bwd, extra=(u,))
        gdown = _mm_tn(s, dd, BF16, f"l{i}_mlp_down_wgrad").reshape(N_DEV, -1, D)
        gup = _mm_tn_blocked(h_in, du, BF16, f"l{i}_mlp_up_wgrad")
        (h_down, h_up), token = _xfer_start([gdown, gup], f"l{i}_mlp_grads_start", gather=False)
        grad_handle[f"mlp_down{i}"], grad_handle[f"mlp_up{i}"] = h_down, h_up
        dh = _mm_nt_blocked(du, ups[i], F32, f"l{i}_mlp_up_bwd", after=(token,))
        dxm, dsh, dsc, dnw, dmix, dgate = _norm_mod_bwd(dh, x_mid, vec(mlp_norm_w[i]), sc_f[i], dx_out,
                                                        f"l{i}_mlp_norm_bwd", branch=(mix, gate))
        dmod[i][3], dmod[i][4], dmod[i][2] = dsh, dsc, dgate
        return dxm, dmix, dnw

    grad_handle = {}
    dx3, dyc, dnw_mlp1 = mlp_backward(1, dx, dd1, x3, h3, u1, s1, mix1, g_m[1])
    g_sc_out = _mm_tn(yc, dyc, BF16, "sc_out_wgrad").reshape(N_DEV, -1, D)
    dconv_out = _mm_nt(dyc, w_sc_out, F32, "sc_out_bwd")
    dbg, dcg, dxv, dscw = _sc_conv_bwd(proj, sc_conv_full, dconv_out, "sc_conv_bwd")
    dproj = jnp.concatenate([dbg, dcg, dxv], axis=1)
    g_sc_in = _mm_tn_blocked(h2, dproj, BF16, "sc_in_wgrad")
    (grad_handle["sc_out_w0"], grad_handle["sc_in_w0"]), token = _xfer_start(
        [g_sc_out, g_sc_in], "sc_grads_start", gather=False)
    dh2 = _mm_nt_blocked(dproj, sc_in_g, F32, "sc_in_bwd", after=(token,))
    dx2, dsh, dsc, dnw_mix1, dd0, dg = _norm_mod_bwd(dh2, x2, vec(mix_norm_w[1]), sc_m[1], dx3, "l1_mix_norm_bwd",
                                                     branch=(d0, g_f[0]))
    dmod[1][0], dmod[1][1], dmod[0][5] = dsh, dsc, dg
    dx1, dyo, dnw_mlp0 = mlp_backward(0, dx2, dd0, x1, h1, u0, s0, mix0, g_m[0])
    g_ssd_out = _mm_tn(yn, dyo, BF16, "ssd_out_wgrad").reshape(N_DEV, -1, D)
    (grad_handle["ssd_out_w0"],), token = _xfer_start([g_ssd_out], "ssd_out_grad_start", gather=False)
    dyn = _mm_nt(dyo, w_ssd_out, F32, "ssd_out_bwd", after=(token,))
    dz, dxs, db_, dc_, ddt, dbias, dalog, dd_, dnw_ssd = _ssd_bwd(
        dyn, y_ssd, zx, xc, prev, bias_p, alog_p, d_lane, nw_g, d_inner, "ssd_scan_bwd")
    dzx, dcw, dcb = _ssd_conv_bwd(zx, conv_w_full, conv_b0, [dxs, db_, dc_], dz, d_inner, "ssd_conv_bwd")
    dzx = _dzx_finish(dzx, ddt, zx_dim, "ssd_dzx_finish")
    g_in_all = _mm_tn(h0, dzx, BF16, "ssd_in_wgrad", tn=512, tk=2048)
    g_ssd_in = jnp.stack([g_in_all[:, b:b + in_win] for b in in_base], axis=0)
    (grad_handle["ssd_in_w0"],), token = _xfer_start([g_ssd_in], "ssd_in_grad_start", gather=False)
    dh0 = _mm_nt(dzx, w_in_all, F32, "ssd_in_bwd", tk=dzx.shape[1] // 2, after=(token,))
    grad_x, dsh, dsc, dnw_mix0 = _norm_mod_bwd(dh0, x0, vec(mix_norm_w[0]), sc_m[0], dx1, "l0_mix_norm_bwd")
    dmod[0][0], dmod[0][1] = dsh, dsc

    small["ada_b"] = jnp.concatenate([jnp.concatenate(dmod[i], axis=1) for i in range(2)], axis=0)
    small["mix_norm_w"] = jnp.concatenate([dnw_mix0, dnw_mix1], axis=0)
    small["mlp_norm_w"] = jnp.concatenate([dnw_mlp0, dnw_mlp1], axis=0)
    small["ssd_conv_w"] = dcw
    small["ssd_conv_b"] = dcb
    small["ssd_norm_w"] = dnw_ssd.reshape(1, d_inner)
    small["sc_conv_w"] = dscw
    small["loss"] = loss_lane
    small_names = list(small)
    head_names = ["ssd_dt_bias", "ssd_A_log", "ssd_D"]
    handles, small_token = _xfer_start([small[k] for k in small_names] + [dbias, dalog, dd_],
                                       "small_grads_start", gather=True)

    out_g, out_d, out_m, out_v = {}, {}, {}, {}

    layer_res = {}

    def big_update(name, i, after):
        (parts,) = _xfer_wait([grad_handle[f"{name}{i}"]], after, f"grads_wait_{name}_{i}", gather=False)
        res = _adamw_sum(parts, weights[name], moms[name], vars_[name], i, f"adamw_{name}_{i}",
                         prev=layer_res.get(name), window_off=in_off if name == "ssd_in_w" else None)
        layer_res[name] = res
        return res[1]

    chain = small_token
    for name, i in [("mlp_down", 1), ("mlp_up", 1), ("sc_out_w", 0), ("sc_in_w", 0), ("mlp_down", 0),
                    ("mlp_up", 0), ("ssd_out_w", 0), ("ssd_in_w", 0)]:
        chain = big_update(name, i, chain)
    gathered_small = _xfer_wait(handles, chain, "small_grads_wait", gather=True)
    small_all = dict(zip(small_names + head_names, gathered_small))

    dmod_loc = lax.dynamic_slice_in_dim(small_all["ada_b"], me * n_mod, n_mod, axis=2)
    dmod_pad = jnp.pad(jnp.transpose(dmod_loc, (1, 0, 2)), ((0, 0), (0, 16 - N_DEV), (0, 0)))
    out_g["ada_w"], out_d["ada_w"], out_m["ada_w"], out_v["ada_w"] = _ada_adamw(
        cond_pad, dmod_pad, ada_w, m_ada_w, v_ada_w, "adamw_ada_w")

    for k in ("ssd_conv_w", "sc_conv_w"):
        n_loc = weights[k].shape[2]
        small_all[k] = lax.dynamic_slice_in_dim(small_all[k], me * n_loc, n_loc, axis=2)
    plain = [k for k in small_names if k != "loss"]
    as2d = lambda a: a.reshape(-1, a.shape[-1])
    res, loss_row = _adamw_small(
        [small_all[k] for k in plain], [tuple(as2d(d[k]) for d in (weights, moms, vars_)) for k in plain],
        [small_all[k] for k in head_names], [tuple(as2d(d[k]) for d in (weights, moms, vars_)) for k in head_names],
        small_all["loss"], "adamw_small")
    loss = loss_row[0, 0]
    for k, res4 in zip(plain + head_names, res):
        for r, dst in zip(res4, (out_g, out_d, out_m, out_v)):
            dst[k] = r.reshape(weights[k].shape)
    for name, res4 in layer_res.items():
        for r, dst in zip(res4, (out_g, out_d, out_m, out_v)):
            dst[name] = r

    return (loss, grad_x[None], *[out_g[k] for k in names], *[out_d[k] for k in names],
            *[out_m[k] for k in names], *[out_v[k] for k in names])
```

```python
import functools

import jax
import jax.numpy as jnp
from jax import lax
from jax.experimental import pallas as pl
from jax.experimental.pallas import tpu as pltpu

F32 = jnp.float32
BF16 = jnp.bfloat16
N_DEV = 8
MESH_AXES = ("x", "y", "c")
MESH = pl.DeviceIdType.MESH

NORM_EPS = 1e-5
SSD_G = 4
SSD_P = 64
SSD_N = 128
SSD_CHUNK = 128
SSD_K = 4
SC_K = 3
LANES = 128

ADAM_LR = 0.001
ADAM_B1 = 0.9
ADAM_B2 = 0.999
ADAM_EPS = 1e-08
ADAM_WD = 0.01
ADAM_STEP = 10

VMEM_LIMIT = 56 * 1024 * 1024


def _pcall(body, **kw):
    return pl.pallas_call(body, **kw)


def _cparams(sem=None):
    if sem is None:
        return pltpu.CompilerParams(vmem_limit_bytes=VMEM_LIMIT)
    return pltpu.CompilerParams(dimension_semantics=sem, vmem_limit_bytes=VMEM_LIMIT)


def _my_index():
    return 4 * lax.axis_index("x") + 2 * lax.axis_index("y") + lax.axis_index("c")


_PEER_MASKS = [(0, 0, 1), (0, 1, 0), (0, 1, 1), (1, 0, 0), (1, 0, 1), (1, 1, 0), (1, 1, 1)]


def _peers():
    x, y, c = lax.axis_index("x"), lax.axis_index("y"), lax.axis_index("c")
    out = []
    for mx, my, mc in _PEER_MASKS:
        px = (1 - x) if mx else x
        py = (1 - y) if my else y
        pc = (1 - c) if mc else c
        out.append(((px, py, pc), 4 * px + 2 * py + pc))
    return out


def _exchange(arrs, name, gather):
    n = len(arrs)
    n_peer = N_DEV - 1

    def body(*refs):
        ins, outs = refs[:n], refs[n:2 * n]
        send_sems, recv_sems, local_sems = refs[2 * n:]
        me = _my_index()
        peers = _peers()
        started = []
        for a in range(n):
            src_own = ins[a] if gather else ins[a].at[me]
            own = pltpu.make_async_copy(src_own, outs[a].at[me], local_sems.at[a])
            own.start()
            started.append(own)
        sends = []
        for a in range(n):
            for k, (peer, pidx) in enumerate(peers):
                src = ins[a] if gather else ins[a].at[pidx]
                cp = pltpu.make_async_remote_copy(
                    src_ref=src, dst_ref=outs[a].at[me],
                    send_sem=send_sems.at[a * n_peer + k], recv_sem=recv_sems.at[a * n_peer + k],
                    device_id=peer, device_id_type=MESH)
                cp.start()
                sends.append(cp)
        for a in range(n):
            for k, (peer, pidx) in enumerate(peers):
                src = ins[a] if gather else ins[a].at[pidx]
                pltpu.make_async_remote_copy(
                    src_ref=src, dst_ref=outs[a].at[pidx],
                    send_sem=send_sems.at[a * n_peer + k], recv_sem=recv_sems.at[a * n_peer + k],
                    device_id=peer, device_id_type=MESH).wait_recv()
        for cp in sends:
            cp.wait_send()
        for own in started:
            own.wait()

    if gather:
        out_shape = [jax.ShapeDtypeStruct((N_DEV,) + a.shape, a.dtype) for a in arrs]
    else:
        out_shape = [jax.ShapeDtypeStruct(a.shape, a.dtype) for a in arrs]
    any_spec = pl.BlockSpec(memory_space=pl.ANY)
    outs = _pcall(
        body, name=name, out_shape=out_shape,
        in_specs=[any_spec] * n, out_specs=[any_spec] * n,
        scratch_shapes=[pltpu.SemaphoreType.DMA((n * n_peer,)), pltpu.SemaphoreType.DMA((n * n_peer,)),
                        pltpu.SemaphoreType.DMA((n,))],
        compiler_params=pltpu.CompilerParams(has_side_effects=True),
    )(*arrs)
    return list(outs)


def _sibling_forward_start(lands, name):
    n = len(lands)
    n_fwd = len(_OTHER_CHIPS)

    def body(*refs):
        ins, bufs = refs[:n], refs[3 * n:4 * n]
        token = refs[-1]
        sibling = (lax.axis_index("x"), lax.axis_index("y"), 1 - lax.axis_index("c"))
        peers = _peers()
        for a in range(n):
            send_sems, recv_sems = refs[n + 2 * a], refs[n + 2 * a + 1]
            for j, k in enumerate(_OTHER_CHIPS):
                slot = peers[k][1]
                pltpu.make_async_remote_copy(
                    src_ref=ins[a].at[slot], dst_ref=bufs[a].at[slot], send_sem=send_sems.at[j],
                    recv_sem=recv_sems.at[j], device_id=sibling, device_id_type=MESH).start()
        token[...] = jnp.zeros_like(token)

    out_shape, out_specs = [], []
    for _ in range(n):
        out_shape += [pltpu.SemaphoreType.DMA((n_fwd,)), pltpu.SemaphoreType.DMA((n_fwd,))]
        out_specs += [_SEM, _SEM]
    out_shape += [pltpu.HBM(a.shape, a.dtype) for a in lands] + [jax.ShapeDtypeStruct((8, LANES), F32)]
    out_specs += [_HBM] * n + [pl.BlockSpec(memory_space=pltpu.VMEM)]
    outs = _pcall(
        body, name=name, out_shape=tuple(out_shape), in_specs=[_HBM] * n, out_specs=tuple(out_specs),
        input_output_aliases={a: 2 * n + a for a in range(n)},
        compiler_params=pltpu.CompilerParams(has_side_effects=_DATAFLOW),
    )(*[pltpu.with_memory_space_constraint(a, pltpu.HBM) for a in lands])
    return [(outs[2 * n + a], outs[2 * a], outs[2 * a + 1]) for a in range(n)], outs[-1]


def _sibling_forward_wait(handles, after, name):
    n = len(handles)

    def body(*refs):
        sibling = (lax.axis_index("x"), lax.axis_index("y"), 1 - lax.axis_index("c"))
        peers = _peers()
        for a in range(n):
            buf, send_sems, recv_sems = refs[3 * a:3 * a + 3]
            for j, k in enumerate(_OTHER_CHIPS):
                (px, py, pc), slot = peers[k]
                theirs = 4 * px + 2 * py + (1 - pc)
                cp = pltpu.make_async_remote_copy(
                    src_ref=buf.at[slot], dst_ref=buf.at[theirs], send_sem=send_sems.at[j],
                    recv_sem=recv_sems.at[j], device_id=sibling, device_id_type=MESH)
                cp.wait_send()
                cp.wait_recv()

    operands, in_specs = [], []
    for h in handles:
        operands += list(h)
        in_specs += [_HBM, _SEM, _SEM]
    outs = _pcall(
        body, name=name, out_shape=tuple(pltpu.HBM(h[0].shape, h[0].dtype) for h in handles),
        in_specs=in_specs + [pl.BlockSpec(memory_space=pl.ANY)], out_specs=tuple([_HBM] * n),
        input_output_aliases={3 * a: a for a in range(n)},
        compiler_params=pltpu.CompilerParams(has_side_effects=_DATAFLOW),
    )(*operands, after)
    return list(outs)


_HBM = pl.BlockSpec(memory_space=pltpu.HBM)
_SEM = pl.BlockSpec(memory_space=pltpu.SEMAPHORE)
_DATAFLOW = pltpu.SideEffectType.DATAFLOW_SIDE_EFFECTING


_ALL_PEERS = tuple(range(N_DEV - 1))
_SAME_CORE_PEERS = (0, 1, 3, 5)
_OTHER_CHIPS = (1, 3, 5)


def _xfer_start(arrs, name, gather, via_sibling=(), after=()):
    n = len(arrs)
    n_peer = N_DEV - 1
    n_after = len(after)
    peer_ks = [_SAME_CORE_PEERS if a in via_sibling else _ALL_PEERS for a in range(n)]

    def body(*refs):
        ins, lands = refs[:n], refs[n:2 * n]
        sems = refs[2 * n + n_after:5 * n + n_after]
        token = refs[-1]
        me = _my_index()
        peers = _peers()
        for a in range(n):
            send_sems, recv_sems, loc_sem = sems[3 * a:3 * a + 3]
            src_own = ins[a] if gather else ins[a].at[me]
            pltpu.make_async_copy(src_own, lands[a].at[me], loc_sem).start()
            for k in peer_ks[a]:
                peer, pidx = peers[k]
                src = ins[a] if gather else ins[a].at[pidx]
                pltpu.make_async_remote_copy(
                    src_ref=src, dst_ref=lands[a].at[me], send_sem=send_sems.at[k], recv_sem=recv_sems.at[k],
                    device_id=peer, device_id_type=MESH).start()
        token[...] = jnp.zeros_like(token)

    land_shapes = [((N_DEV,) + a.shape) if gather else a.shape for a in arrs]
    out_shape, out_specs = [], []
    for _ in range(n):
        out_shape += [pltpu.SemaphoreType.DMA((n_peer,)), pltpu.SemaphoreType.DMA((n_peer,)),
                      pltpu.SemaphoreType.DMA(())]
        out_specs += [_SEM, _SEM, _SEM]
    out_shape += [pltpu.HBM(a.shape, a.dtype) for a in arrs]
    out_shape += [pltpu.HBM(s, a.dtype) for s, a in zip(land_shapes, arrs)]
    out_shape += [jax.ShapeDtypeStruct((8, LANES), F32)]
    out_specs += [_HBM] * (2 * n) + [pl.BlockSpec(memory_space=pltpu.VMEM)]
    aliases = {}
    for a in range(n):
        aliases[a] = 3 * n + a
        aliases[n + a] = 4 * n + a
    operands = [pltpu.with_memory_space_constraint(a, pltpu.HBM) for a in arrs]
    operands += [pltpu.with_memory_space_constraint(lax.empty(s, a.dtype), pltpu.HBM)
                 for s, a in zip(land_shapes, arrs)]
    outs = _pcall(
        body, name=name, out_shape=tuple(out_shape),
        in_specs=[_HBM] * (2 * n) + [pl.BlockSpec(memory_space=pl.ANY)] * n_after, out_specs=tuple(out_specs),
        input_output_aliases=aliases,
        compiler_params=pltpu.CompilerParams(has_side_effects=_DATAFLOW),
    )(*operands, *after)
    handles = []
    for a in range(n):
        handles.append((outs[3 * n + a], outs[4 * n + a], outs[3 * a], outs[3 * a + 1], outs[3 * a + 2],
                        peer_ks[a]))
    return handles, outs[-1]


def _xfer_wait(handles, after, name, gather):
    n = len(handles)
    after = tuple(after) if isinstance(after, (tuple, list)) else (after,)
    peer_ks = [h[5] for h in handles]

    def body(*refs):
        me = _my_index()
        peers = _peers()
        for a in range(n):
            src_ref, land_ref, send_ref, recv_ref, loc_ref = refs[5 * a:5 * a + 5]
            src_own = src_ref if gather else src_ref.at[me]
            pltpu.make_async_copy(src_own, land_ref.at[me], loc_ref).wait()
            for k in peer_ks[a]:
                peer, pidx = peers[k]
                src = src_ref if gather else src_ref.at[pidx]
                cp = pltpu.make_async_remote_copy(
                    src_ref=src, dst_ref=land_ref.at[pidx], send_sem=send_ref.at[k], recv_sem=recv_ref.at[k],
                    device_id=peer, device_id_type=MESH)
                cp.wait_send()
                cp.wait_recv()

    operands, in_specs, out_shape, aliases = [], [], [], {}
    for a, h in enumerate(handles):
        operands += list(h[:5])
        in_specs += [_HBM, _HBM, _SEM, _SEM, _SEM]
        out_shape += [pltpu.HBM(h[0].shape, h[0].dtype), pltpu.HBM(h[1].shape, h[1].dtype)]
        aliases[5 * a] = 2 * a
        aliases[5 * a + 1] = 2 * a + 1
    outs = _pcall(
        body, name=name, out_shape=tuple(out_shape),
        in_specs=in_specs + [pl.BlockSpec(memory_space=pl.ANY)] * len(after),
        out_specs=tuple([_HBM] * (2 * n)), input_output_aliases=aliases,
        compiler_params=pltpu.CompilerParams(has_side_effects=_DATAFLOW),
    )(*operands, *after)
    return [outs[2 * a + 1] for a in range(n)]


def _sibling_forward(lands, name):
    n = len(lands)
    n_fwd = len(_OTHER_CHIPS)

    def body(*refs):
        ins, bufs = refs[:n], refs[n:2 * n]
        send_sems, recv_sems = refs[2 * n:]
        x, y, c = lax.axis_index("x"), lax.axis_index("y"), lax.axis_index("c")
        sibling = (x, y, 1 - c)
        peers = _peers()
        sends = []
        for a in range(n):
            for j, k in enumerate(_OTHER_CHIPS):
                slot = peers[k][1]
                cp = pltpu.make_async_remote_copy(
                    src_ref=ins[a].at[slot], dst_ref=bufs[a].at[slot],
                    send_sem=send_sems.at[a * n_fwd + j], recv_sem=recv_sems.at[a * n_fwd + j],
                    device_id=sibling, device_id_type=MESH)
                cp.start()
                sends.append(cp)
        for a in range(n):
            for j, k in enumerate(_OTHER_CHIPS):
                (px, py, pc), slot = peers[k]
                theirs = 4 * px + 2 * py + (1 - pc)
                pltpu.make_async_remote_copy(
                    src_ref=ins[a].at[slot], dst_ref=bufs[a].at[theirs],
                    send_sem=send_sems.at[a * n_fwd + j], recv_sem=recv_sems.at[a * n_fwd + j],
                    device_id=sibling, device_id_type=MESH).wait_recv()
        for cp in sends:
            cp.wait_send()

    any_spec = pl.BlockSpec(memory_space=pl.ANY)
    outs = _pcall(
        body, name=name, out_shape=[jax.ShapeDtypeStruct(a.shape, a.dtype) for a in lands],
        in_specs=[any_spec] * n, out_specs=[any_spec] * n,
        input_output_aliases={a: a for a in range(n)},
        scratch_shapes=[pltpu.SemaphoreType.DMA((n * n_fwd,)), pltpu.SemaphoreType.DMA((n * n_fwd,))],
        compiler_params=pltpu.CompilerParams(has_side_effects=True),
    )(*lands)
    return list(outs)


_DIMS = {"nn": (((1,), (0,)), ((), ())), "nt": (((1,), (1,)), ((), ())), "tn": (((0,), (0,)), ((), ()))}


def _dot(a, b, mode="nn"):
    return lax.dot_general(a, b, _DIMS[mode], preferred_element_type=F32)


def _mm(a, b, *, mode, grid, a_spec, b_spec, out_shape, out_specs, acc_shape, epilogue, name,
        extra=(), extra_specs=(), after=()):
    nk = grid[2]
    n_extra = len(extra)
    n_in = 2 + n_extra + len(after)

    def body_single(*refs):
        a_ref, b_ref = refs[0], refs[1]
        epilogue(_dot(a_ref[...], b_ref[...], mode), refs[2:2 + n_extra], refs[n_in:])

    def body_acc(*refs):
        a_ref, b_ref = refs[0], refs[1]
        ex = refs[2:2 + n_extra]
        outs = refs[n_in:-1]
        acc = refs[-1]
        k = pl.program_id(2)

        @pl.when(k == 0)
        def _():
            acc[...] = jnp.zeros_like(acc)

        acc[...] += _dot(a_ref[...], b_ref[...], mode)

        @pl.when(k == nk - 1)
        def _():
            epilogue(acc[...], ex, outs)

    return _pcall(
        body_single if nk == 1 else body_acc, name=name, grid=grid, out_shape=out_shape,
        in_specs=[a_spec, b_spec] + list(extra_specs) + [pl.BlockSpec(memory_space=pl.ANY)] * len(after),
        out_specs=out_specs,
        scratch_shapes=[] if nk == 1 else [pltpu.VMEM(acc_shape, F32)],
        compiler_params=_cparams(("parallel", "parallel", "arbitrary")),
    )(a, b, *extra, *after)


def _ep_store(dtype):
    def ep(acc, ex, outs):
        outs[0][...] = acc.astype(dtype)
    return ep


def _ep_relu2(acc, ex, outs):
    outs[0][...] = acc.astype(BF16)
    r = jnp.maximum(acc, 0.0)
    outs[1][...] = (r * r).astype(BF16)


def _ep_relu2_bwd(acc, ex, outs):
    u = ex[0][...].astype(F32)
    outs[0][...] = (acc * (2.0 * jnp.maximum(u, 0.0))).astype(BF16)


def _tile(n, want):
    t = min(n, want)
    while n % t:
        t //= 2
    return t


def _mm_nn(a, w, out_dtype, name, tm=1024, tn=1024, tk=1024, epilogue=None, out_dtypes=None, after=()):
    M, K = a.shape
    N = w.shape[1]
    tm, tn, tk = _tile(M, tm), _tile(N, tn), _tile(K, tk)
    out_dtypes = out_dtypes or [out_dtype]
    return _mm(a, w, mode="nn", grid=(M // tm, N // tn, K // tk),
               a_spec=pl.BlockSpec((tm, tk), lambda i, j, k: (i, k)),
               b_spec=pl.BlockSpec((tk, tn), lambda i, j, k: (k, j)),
               out_shape=[jax.ShapeDtypeStruct((M, N), d) for d in out_dtypes],
               out_specs=[pl.BlockSpec((tm, tn), lambda i, j, k: (i, j)) for _ in out_dtypes],
               acc_shape=(tm, tn), epilogue=epilogue or _ep_store(out_dtype), name=name, after=after)


def _mm_nn_blocked(a, wg, name, epilogue, out_dtypes, tm=2048):
    M, K = a.shape
    n = wg.shape[2]
    tm = _tile(M, tm)
    return _mm(a, wg, mode="nn", grid=(M // tm, N_DEV, 1),
               a_spec=pl.BlockSpec((tm, K), lambda i, j, k: (i, 0)),
               b_spec=pl.BlockSpec((None, K, n), lambda i, j, k: (j, 0, 0)),
               out_shape=[jax.ShapeDtypeStruct((M, N_DEV * n), d) for d in out_dtypes],
               out_specs=[pl.BlockSpec((tm, n), lambda i, j, k: (i, j)) for _ in out_dtypes],
               acc_shape=(tm, n), epilogue=epilogue, name=name)


def _mm_nt(a, w, out_dtype, name, tm=1024, tn=1024, tk=1024, epilogue=None, extra=(), extra_specs=(),
           after=()):
    M, K = a.shape
    N = w.shape[0]
    tm, tn, tk = _tile(M, tm), _tile(N, tn), _tile(K, tk)
    if extra and not extra_specs:
        extra_specs = [pl.BlockSpec((tm, tn), lambda i, j, k: (i, j)) for _ in extra]
    return _mm(a, w, mode="nt", grid=(M // tm, N // tn, K // tk),
               a_spec=pl.BlockSpec((tm, tk), lambda i, j, k: (i, k)),
               b_spec=pl.BlockSpec((tn, tk), lambda i, j, k: (j, k)),
               out_shape=[jax.ShapeDtypeStruct((M, N), out_dtype)],
               out_specs=[pl.BlockSpec((tm, tn), lambda i, j, k: (i, j))],
               acc_shape=(tm, tn), epilogue=epilogue or _ep_store(out_dtype), name=name,
               extra=extra, extra_specs=extra_specs, after=after)[0]


def _mm_nt_blocked(a, wg, out_dtype, name, tm=1024, after=()):
    M = a.shape[0]
    kout, n = wg.shape[1], wg.shape[2]
    tm = _tile(M, tm)
    return _mm(a, wg, mode="nt", grid=(M // tm, 1, N_DEV),
               a_spec=pl.BlockSpec((tm, n), lambda i, j, k: (i, k)),
               b_spec=pl.BlockSpec((None, kout, n), lambda i, j, k: (k, 0, 0)),
               out_shape=[jax.ShapeDtypeStruct((M, kout), out_dtype)],
               out_specs=[pl.BlockSpec((tm, kout), lambda i, j, k: (i, 0))],
               acc_shape=(tm, kout), epilogue=_ep_store(out_dtype), name=name, after=after)[0]


def _mm_tn(a, b, out_dtype, name, tm=1024, tn=1024, tk=1024):
    K, M = a.shape
    N = b.shape[1]
    tm, tn, tk = _tile(M, tm), _tile(N, tn), _tile(K, tk)
    return _mm(a, b, mode="tn", grid=(M // tm, N // tn, K // tk),
               a_spec=pl.BlockSpec((tk, tm), lambda i, j, k: (k, i)),
               b_spec=pl.BlockSpec((tk, tn), lambda i, j, k: (k, j)),
               out_shape=[jax.ShapeDtypeStruct((M, N), out_dtype)],
               out_specs=[pl.BlockSpec((tm, tn), lambda i, j, k: (i, j))],
               acc_shape=(tm, tn), epilogue=_ep_store(out_dtype), name=name)[0]


def _mm_tn_blocked(a, b, out_dtype, name, tm=1024, tk=2048):
    K, M = a.shape
    n = b.shape[1] // N_DEV
    tm, tk = _tile(M, tm), _tile(K, tk)
    return _mm(a, b, mode="tn", grid=(M // tm, N_DEV, K // tk),
               a_spec=pl.BlockSpec((tk, tm), lambda i, j, k: (k, i)),
               b_spec=pl.BlockSpec((tk, n), lambda i, j, k: (k, j)),
               out_shape=[jax.ShapeDtypeStruct((N_DEV, M, n), out_dtype)],
               out_specs=[pl.BlockSpec((None, tm, n), lambda i, j, k: (j, i, 0))],
               acc_shape=(tm, n), epilogue=_ep_store(out_dtype), name=name)[0]


def _window_geometry(ws):
    base = [(ws * k // LANES) * LANES for k in range(N_DEV)]
    off = [ws * k - base[k] for k in range(N_DEV)]
    win = -(-(max(off) + ws) // LANES) * LANES
    return base, off, win


def _shards_to_columns(xg, base, off, win, n_out, name, tr=256):
    R, ws = xg.shape[1], xg.shape[2]
    tr = _tile(R, tr)
    nb_win = win // LANES

    def body(x_ref, o_ref, frame_ref):
        written = set()
        frame_ref[...] = jnp.zeros_like(frame_ref)
        for k in range(N_DEV):
            frame_ref[:, 0:ws] = x_ref[k].astype(F32)
            window = frame_ref[...]
            if off[k]:
                window = pltpu.roll(window, off[k], 1)
            for i in range(nb_win):
                b = base[k] // LANES + i
                if b * LANES >= n_out:
                    continue
                cols = slice(b * LANES, (b + 1) * LANES)
                blk = window[:, i * LANES:(i + 1) * LANES]
                if b in written:
                    blk = blk + o_ref[:, cols].astype(F32)
                o_ref[:, cols] = blk.astype(o_ref.dtype)
                written.add(b)
        for b in range(n_out // LANES):
            if b not in written:
                o_ref[:, b * LANES:(b + 1) * LANES] = jnp.zeros((tr, LANES), o_ref.dtype)

    return _pcall(
        body, name=name, grid=(R // tr,), out_shape=jax.ShapeDtypeStruct((R, n_out), xg.dtype),
        in_specs=[pl.BlockSpec((N_DEV, tr, ws), lambda i: (0, i, 0))],
        out_specs=pl.BlockSpec((tr, n_out), lambda i: (i, 0)),
        scratch_shapes=[pltpu.VMEM((tr, win), F32)],
        compiler_params=_cparams(("parallel",)))(xg)


def _sigmoid(x):
    return 1.0 / (1.0 + jnp.exp(-x))


def _row_spec(tm, d):
    return pl.BlockSpec((tm, d), lambda i: (i, 0))


def _vec_spec(d):
    return pl.BlockSpec((1, d), lambda i: (0, 0))


def _norm_mod_fwd(x, y, gate, nw, scale, shift, name, tm=512):
    L, D = x.shape
    tm = _tile(L, tm)
    has_res = y is not None

    def body(*refs):
        if has_res:
            x_ref, y_ref, g_ref, nw_ref, sc_ref, sh_ref, xo_ref, h_ref = refs
            xn = x_ref[...] + g_ref[...] * y_ref[...]
            xo_ref[...] = xn
        else:
            x_ref, nw_ref, sc_ref, sh_ref, h_ref = refs
            xn = x_ref[...]
        rstd = lax.rsqrt(jnp.mean(xn * xn, axis=-1, keepdims=True) + NORM_EPS)
        h = xn * rstd * nw_ref[...] * (1.0 + sc_ref[...]) + sh_ref[...]
        h_ref[...] = h.astype(BF16)

    row, vec = _row_spec(tm, D), _vec_spec(D)
    if has_res:
        ins, in_specs = (x, y, gate, nw, scale, shift), [row, row, vec, vec, vec, vec]
        out_shape = [jax.ShapeDtypeStruct((L, D), F32), jax.ShapeDtypeStruct((L, D), BF16)]
        out_specs = [row, row]
    else:
        ins, in_specs = (x, nw, scale, shift), [row, vec, vec, vec]
        out_shape = [jax.ShapeDtypeStruct((L, D), BF16)]
        out_specs = [row]
    outs = _pcall(body, name=name, grid=(L // tm,), out_shape=out_shape, in_specs=in_specs,
                  out_specs=out_specs, compiler_params=_cparams(("parallel",)))(*ins)
    return outs if has_res else (x, outs[0])


def _gated_branch_bwd(dx, branch, y_ref, g_ref, dy_ref, dg_ref):
    if branch is None:
        return
    dy_ref[...] = (g_ref[...] * dx).astype(BF16)
    dg_ref[...] += jnp.sum(dx * y_ref[...], axis=0, keepdims=True)


def _norm_mod_bwd(dh, x, nw, scale, dres, name, branch=None, tm=512):
    L, D = x.shape
    tm = _tile(L, tm)
    nb = 0 if branch is None else 2

    def body(dh_ref, x_ref, nw_ref, sc_ref, dres_ref, *rest):
        y_ref, g_ref = rest[:nb] if nb else (None, None)
        dx_ref, dsh_ref, dsc_ref, dnw_ref = rest[nb:nb + 4]
        dy_ref, dg_ref = rest[nb + 4:] if nb else (None, None)

        @pl.when(pl.program_id(0) == 0)
        def _():
            dsh_ref[...] = jnp.zeros_like(dsh_ref)
            dsc_ref[...] = jnp.zeros_like(dsc_ref)
            dnw_ref[...] = jnp.zeros_like(dnw_ref)
            if nb:
                dg_ref[...] = jnp.zeros_like(dg_ref)

        xv = x_ref[...]
        dh_v = dh_ref[...]
        nw_v = nw_ref[...]
        rstd = lax.rsqrt(jnp.mean(xv * xv, axis=-1, keepdims=True) + NORM_EPS)
        xhat = xv * rstd
        dsh_ref[...] += jnp.sum(dh_v, axis=0, keepdims=True)
        dsc_ref[...] += jnp.sum(dh_v * (xhat * nw_v), axis=0, keepdims=True)
        dr = dh_v * (1.0 + sc_ref[...])
        dnw_ref[...] += jnp.sum(dr * xhat, axis=0, keepdims=True)
        dxh = dr * nw_v
        dx = rstd * (dxh - xhat * jnp.mean(dxh * xhat, axis=-1, keepdims=True)) + dres_ref[...]
        dx_ref[...] = dx
        _gated_branch_bwd(dx, branch, y_ref, g_ref, dy_ref, dg_ref)

    row, vec = _row_spec(tm, D), _vec_spec(D)
    extra_in = [] if branch is None else list(branch)
    return _pcall(
        body, name=name, grid=(L // tm,),
        out_shape=[jax.ShapeDtypeStruct((L, D), F32)] + [jax.ShapeDtypeStruct((1, D), F32)] * 3
        + ([jax.ShapeDtypeStruct((L, D), BF16), jax.ShapeDtypeStruct((1, D), F32)] if nb else []),
        in_specs=[row, row, vec, vec, row] + ([row, vec] if nb else []),
        out_specs=[row, vec, vec, vec] + ([row, vec] if nb else []),
        compiler_params=_cparams(("arbitrary",)))(dh, x, nw, scale, dres, *extra_in)


def _final_loss(x, y, gate, fw, target, name, tm=512):
    L, D = x.shape
    tm = _tile(L, tm)

    def body(x_ref, y_ref, g_ref, fw_ref, t_ref, dx_ref, loss_ref, dfw_ref, dy_ref, dg_ref):
        @pl.when(pl.program_id(0) == 0)
        def _():
            loss_ref[...] = jnp.zeros_like(loss_ref)
            dfw_ref[...] = jnp.zeros_like(dfw_ref)
            dg_ref[...] = jnp.zeros_like(dg_ref)

        xn = x_ref[...] + g_ref[...] * y_ref[...]
        fw_v = fw_ref[...]
        rstd = lax.rsqrt(jnp.mean(xn * xn, axis=-1, keepdims=True) + NORM_EPS)
        xhat = xn * rstd
        diff = xhat * fw_v - t_ref[...]
        loss_ref[...] += jnp.sum(diff * diff, axis=0, keepdims=True)
        dyf = diff * (1.0 / D)
        dfw_ref[...] += jnp.sum(dyf * xhat, axis=0, keepdims=True)
        dxh = dyf * fw_v
        dx = rstd * (dxh - xhat * jnp.mean(dxh * xhat, axis=-1, keepdims=True))
        dx_ref[...] = dx
        _gated_branch_bwd(dx, True, y_ref, g_ref, dy_ref, dg_ref)

    row, vec = _row_spec(tm, D), _vec_spec(D)
    return _pcall(
        body, name=name, grid=(L // tm,),
        out_shape=[jax.ShapeDtypeStruct((L, D), F32), jax.ShapeDtypeStruct((1, D), F32),
                   jax.ShapeDtypeStruct((1, D), F32), jax.ShapeDtypeStruct((L, D), BF16),
                   jax.ShapeDtypeStruct((1, D), F32)],
        in_specs=[row, row, vec, vec, row], out_specs=[row, vec, vec, row, vec],
        compiler_params=_cparams(("arbitrary",)))(x, y, gate, fw, target)


def _shift_down(v, s, row):
    if s == 0:
        return v
    return jnp.where(row >= s, pltpu.roll(v, s, 0), 0.0)


def _shift_up(v, s, row):
    if s == 0:
        return v
    n = v.shape[0]
    return jnp.where(row < n - s, pltpu.roll(v, n - s, 0), 0.0)


def _ssd_conv_fwd(zx, w, b, col0, width, name, cb=128):
    L = zx.shape[0]
    nb = width // cb
    off = col0 // cb

    def body(x_ref, w_ref, b_ref, o_ref):
        xv = x_ref[...]
        row = lax.broadcasted_iota(jnp.int32, xv.shape, 0)
        acc = b_ref[...] + w_ref[SSD_K - 1:SSD_K, :] * xv
        for s in range(1, SSD_K):
            acc = acc + w_ref[SSD_K - 1 - s:SSD_K - s, :] * _shift_down(xv, s, row)
        o_ref[...] = acc * _sigmoid(acc)

    return _pcall(
        body, name=name, grid=(nb,), out_shape=jax.ShapeDtypeStruct((L, width), F32),
        in_specs=[pl.BlockSpec((L, cb), lambda j: (0, off + j)),
                  pl.BlockSpec((SSD_K, cb), lambda j: (0, j)),
                  pl.BlockSpec((1, cb), lambda j: (0, j))],
        out_specs=pl.BlockSpec((L, cb), lambda j: (0, j)),
        compiler_params=_cparams(("parallel",)))(zx, w, b)


def _ssd_conv_bwd(zx, w, b, d_parts, dzx, col0, name, cb=128):
    L = zx.shape[0]
    widths = [p.shape[1] for p in d_parts]
    width = sum(widths)
    nb = width // cb
    off = col0 // cb
    starts = [sum(widths[:i]) // cb for i in range(len(d_parts))]
    counts = [wd // cb for wd in widths]

    def body(x_ref, w_ref, b_ref, *rest):
        d_refs = rest[:len(d_parts)]
        dx_ref, dw_ref, db_ref = rest[len(d_parts) + 1:]
        j = pl.program_id(0)
        d_val = d_refs[-1][...]
        for i in range(len(d_parts) - 2, -1, -1):
            d_val = jnp.where(j < starts[i + 1], d_refs[i][...], d_val)
        xv = x_ref[...]
        row = lax.broadcasted_iota(jnp.int32, xv.shape, 0)
        shifted = [_shift_down(xv, s, row) for s in range(SSD_K)]
        acc = b_ref[...] + w_ref[SSD_K - 1:SSD_K, :] * xv
        for s in range(1, SSD_K):
            acc = acc + w_ref[SSD_K - 1 - s:SSD_K - s, :] * shifted[s]
        sig = _sigmoid(acc)
        dpre = d_val * (sig * (1.0 + acc * (1.0 - sig)))
        db_ref[...] = jnp.sum(dpre, axis=0, keepdims=True)
        dx = w_ref[SSD_K - 1:SSD_K, :] * dpre
        for s in range(SSD_K):
            dw_ref[SSD_K - 1 - s:SSD_K - s, :] = jnp.sum(dpre * shifted[s], axis=0, keepdims=True)
            if s:
                dx = dx + w_ref[SSD_K - 1 - s:SSD_K - s, :] * _shift_up(dpre, s, row)
        dx_ref[...] = dx.astype(BF16)

    def part_spec(i):
        return pl.BlockSpec((L, cb), lambda j: (0, jnp.clip(j - starts[i], 0, counts[i] - 1)))

    return _pcall(
        body, name=name, grid=(nb,),
        out_shape=[jax.ShapeDtypeStruct(dzx.shape, BF16), jax.ShapeDtypeStruct((SSD_K, width), F32),
                   jax.ShapeDtypeStruct((1, width), F32)],
        in_specs=[pl.BlockSpec((L, cb), lambda j: (0, off + j)),
                  pl.BlockSpec((SSD_K, cb), lambda j: (0, j)),
                  pl.BlockSpec((1, cb), lambda j: (0, j))]
        + [part_spec(i) for i in range(len(d_parts))] + [pl.BlockSpec(memory_space=pl.ANY)],
        out_specs=[pl.BlockSpec((L, cb), lambda j: (0, off + j)),
                   pl.BlockSpec((SSD_K, cb), lambda j: (0, j)),
                   pl.BlockSpec((1, cb), lambda j: (0, j))],
        input_output_aliases={3 + len(d_parts): 0},
        compiler_params=_cparams(("parallel",)))(zx, w, b, *d_parts, dzx)


def _dzx_finish(dzx, ddt, col0, name, tl=512):
    G, L, _ = ddt.shape
    tail = dzx.shape[1] - col0
    tl = _tile(L, tl)

    def body(ddt_ref, dzx_ref, o_ref):
        s = ddt_ref[0]
        for g in range(1, G):
            s = s + ddt_ref[g]
        o_ref[:, 0:LANES] = s.astype(o_ref.dtype)
        if tail > LANES:
            o_ref[:, LANES:] = jnp.zeros((tl, tail - LANES), o_ref.dtype)

    return _pcall(
        body, name=name, grid=(L // tl,), out_shape=jax.ShapeDtypeStruct(dzx.shape, dzx.dtype),
        in_specs=[pl.BlockSpec((G, tl, LANES), lambda i: (0, i, 0)), pl.BlockSpec(memory_space=pl.ANY)],
        out_specs=pl.BlockSpec((tl, tail), lambda i: (i, col0 // tail)),
        input_output_aliases={1: 0},
        compiler_params=_cparams(("parallel",)))(ddt, dzx)


def _sc_conv_fwd(proj, w, name, cb=128):
    L = proj.shape[0]
    width = proj.shape[1] // 3
    nb = width // cb

    def body(b_ref, c_ref, x_ref, w_ref, o_ref):
        q = c_ref[...] * x_ref[...]
        row = lax.broadcasted_iota(jnp.int32, q.shape, 0)
        acc = w_ref[SC_K - 1:SC_K, :] * q
        for s in range(1, SC_K):
            acc = acc + w_ref[SC_K - 1 - s:SC_K - s, :] * _shift_down(q, s, row)
        o_ref[...] = (b_ref[...] * acc).astype(BF16)

    return _pcall(
        body, name=name, grid=(nb,), out_shape=jax.ShapeDtypeStruct((L, width), BF16),
        in_specs=[pl.BlockSpec((L, cb), lambda j: (0, j)),
                  pl.BlockSpec((L, cb), lambda j: (0, nb + j)),
                  pl.BlockSpec((L, cb), lambda j: (0, 2 * nb + j)),
                  pl.BlockSpec((SC_K, cb), lambda j: (0, j))],
        out_specs=pl.BlockSpec((L, cb), lambda j: (0, j)),
        compiler_params=_cparams(("parallel",)))(proj, proj, proj, w)


def _sc_conv_bwd(proj, w, dy, name, cb=128):
    L = proj.shape[0]
    width = proj.shape[1] // 3
    nb = width // cb

    def body(b_ref, c_ref, x_ref, w_ref, dy_ref, db_ref, dc_ref, dxv_ref, dw_ref):
        cg, xv, dyv = c_ref[...], x_ref[...], dy_ref[...]
        q = cg * xv
        row = lax.broadcasted_iota(jnp.int32, q.shape, 0)
        shifted = [_shift_down(q, s, row) for s in range(SC_K)]
        conv = w_ref[SC_K - 1:SC_K, :] * q
        for s in range(1, SC_K):
            conv = conv + w_ref[SC_K - 1 - s:SC_K - s, :] * shifted[s]
        db_ref[...] = (dyv * conv).astype(BF16)
        dconv = dyv * b_ref[...]
        dq = w_ref[SC_K - 1:SC_K, :] * dconv
        for s in range(SC_K):
            dw_ref[SC_K - 1 - s:SC_K - s, :] = jnp.sum(dconv * shifted[s], axis=0, keepdims=True)
            if s:
                dq = dq + w_ref[SC_K - 1 - s:SC_K - s, :] * _shift_up(dconv, s, row)
        dc_ref[...] = (dq * xv).astype(BF16)
        dxv_ref[...] = (dq * cg).astype(BF16)

    blk = pl.BlockSpec((L, cb), lambda j: (0, j))
    wblk = pl.BlockSpec((SC_K, cb), lambda j: (0, j))
    return _pcall(
        body, name=name, grid=(nb,),
        out_shape=[jax.ShapeDtypeStruct((L, width), BF16)] * 3 + [jax.ShapeDtypeStruct((SC_K, width), F32)],
        in_specs=[blk, pl.BlockSpec((L, cb), lambda j: (0, nb + j)),
                  pl.BlockSpec((L, cb), lambda j: (0, 2 * nb + j)), wblk, blk],
        out_specs=[blk, blk, blk, wblk],
        compiler_params=_cparams(("parallel",)))(proj, proj, proj, w, dy)


def _split3(v):
    hi = v.astype(BF16)
    r1 = v - hi.astype(F32)
    mid = r1.astype(BF16)
    lo = (r1 - mid.astype(F32)).astype(BF16)
    return hi, mid, lo


def _dot_exact01(t01, v):
    hi, mid, lo = _split3(v)
    return _dot(t01, hi) + _dot(t01, mid) + _dot(t01, lo)


def _lane_col(v, lane, h):
    return jnp.sum(jnp.where(lane == h, v, 0.0), axis=1, keepdims=True)


def _sum_all(v):
    return jnp.sum(jnp.sum(v, axis=1, keepdims=True), axis=0, keepdims=True)


def _softplus(x):
    return jnp.maximum(x, 0.0) + jnp.log1p(jnp.exp(-jnp.abs(x)))


def _ssd_common(dt_ref, bias_ref, alog_ref, b_ref, c_ref, cst_ref, heads):
    c_sz = SSD_CHUNK
    lane = lax.broadcasted_iota(jnp.int32, (c_sz, LANES), 1)
    row = lax.broadcasted_iota(jnp.int32, (c_sz, LANES), 0)
    valid = lane < heads
    raw = dt_ref[...] + bias_ref[...]
    dt = _softplus(raw)
    a_row = -jnp.exp(alog_ref[...])
    a = jnp.where(valid, dt * a_row, 0.0)
    tri = (row >= lane).astype(BF16)
    cs = _dot_exact01(tri, a)
    cst_ref[...] = cs.T
    last_row = jnp.sum(a, axis=0, keepdims=True)
    bb = b_ref[...].astype(BF16)
    cb = c_ref[...].astype(BF16)
    scores = _dot(cb, bb, "nt")
    return dict(lane=lane, row=row, valid=valid, raw=raw, dt=dt, a_row=a_row, cs=cs,
                last_row=last_row, bb=bb, cb=cb, scores=scores, causal=row >= lane, lo=lane < SSD_P)


def _pair_terms(q, cst_ref, h0):
    lane, lo = q["lane"], q["lo"]
    out = {}
    cols, dts, lasts, lms = [], [], [], []
    lane1 = lax.broadcasted_iota(jnp.int32, (1, LANES), 1)
    for h in (h0, h0 + 1):
        col = _lane_col(q["cs"], lane, h)
        rowv = cst_ref[pl.ds(h, 1), :]
        lms.append(jnp.exp(jnp.where(q["causal"], col - rowv, -1e30)))
        cols.append(col)
        dts.append(_lane_col(q["dt"], lane, h))
        lasts.append(jnp.sum(jnp.where(lane1 == h, q["last_row"], 0.0), axis=1, keepdims=True))
    out["lm"] = lms
    out["cols"] = cols
    out["lasts"] = lasts
    out["dt_b"] = jnp.where(lo, dts[0], dts[1])
    out["e_b"] = jnp.where(lo, jnp.exp(cols[0]), jnp.exp(cols[1]))
    out["dec_cols"] = [jnp.exp(lasts[0] - cols[0]), jnp.exp(lasts[1] - cols[1])]
    out["dec_b"] = jnp.where(lo, out["dec_cols"][0], out["dec_cols"][1])
    lo1 = lane1 < SSD_P
    out["explast"] = [jnp.exp(lasts[0]), jnp.exp(lasts[1])]
    out["explast_b"] = jnp.where(lo1, out["explast"][0], out["explast"][1])
    return out


def _ssd_fwd(zx, xc, bias_p, alog_p, d_lane, nw, d_inner, after, name):
    L = zx.shape[0]
    nc = L // SSD_CHUNK
    gw = d_inner // SSD_G
    heads = gw // SSD_P
    n_pair = heads // 2
    bc0 = d_inner // LANES
    dt0 = (2 * d_inner + 2 * SSD_G * SSD_N) // LANES

    def body(z_ref, xs_ref, b_ref, c_ref, dt_ref, bias_ref, alog_ref, dl_ref, nw_ref, after_ref,
             y_ref, yn_ref, prev_ref, s_ref, cst_ref):
        @pl.when(pl.program_id(1) == 0)
        def _():
            s_ref[...] = jnp.zeros_like(s_ref)

        q = _ssd_common(dt_ref, bias_ref, alog_ref, b_ref, c_ref, cst_ref, SSD_G * heads)
        prev_ref[...] = s_ref[...]
        lo = q["lo"]
        for j in range(n_pair):
            sl = slice(j * LANES, (j + 1) * LANES)
            p = _pair_terms(q, cst_ref, pl.program_id(0) * heads + 2 * j)
            xs_p = xs_ref[:, sl]
            xp = xs_p * p["dt_b"]
            xb = xp.astype(BF16)
            m_a = (q["scores"] * p["lm"][0]).astype(BF16)
            m_b = (q["scores"] * p["lm"][1]).astype(BF16)
            yd = jnp.where(lo, _dot(m_a, xb), _dot(m_b, xb))
            s_p = s_ref[:, sl]
            yo = _dot(q["cb"], s_p.astype(BF16)) * p["e_b"]
            y_ref[:, sl] = yd + yo + dl_ref[:, sl] * xs_p
            st = _dot(q["bb"], (xp * p["dec_b"]).astype(BF16), "tn")
            s_ref[:, sl] = s_p * p["explast_b"] + st
        yv = y_ref[...]
        zv = z_ref[...]
        yg = yv * (zv * _sigmoid(zv))
        rstd = lax.rsqrt(jnp.mean(yg * yg, axis=-1, keepdims=True) + NORM_EPS)
        yn_ref[...] = (yg * rstd * nw_ref[...]).astype(BF16)

    grp = lambda width: pl.BlockSpec((None, 1, width), lambda g, c: (g, 0, 0))
    head_vec = pl.BlockSpec((1, LANES), lambda g, c: (0, 0))
    return _pcall(
        body, name=name, grid=(SSD_G, nc),
        out_shape=[jax.ShapeDtypeStruct((L, d_inner), F32), jax.ShapeDtypeStruct((L, d_inner), BF16),
                   jax.ShapeDtypeStruct((nc, SSD_G, SSD_N, gw), F32)],
        in_specs=[pl.BlockSpec((SSD_CHUNK, gw), lambda g, c: (c, g)),
                  pl.BlockSpec((SSD_CHUNK, gw), lambda g, c: (c, g)),
                  pl.BlockSpec((SSD_CHUNK, SSD_N), lambda g, c: (c, bc0 + g)),
                  pl.BlockSpec((SSD_CHUNK, SSD_N), lambda g, c: (c, bc0 + SSD_G + g)),
                  pl.BlockSpec((SSD_CHUNK, LANES), lambda g, c: (c, dt0)),
                  head_vec, head_vec, grp(gw), grp(gw), pl.BlockSpec(memory_space=pl.ANY)],
        out_specs=[pl.BlockSpec((SSD_CHUNK, gw), lambda g, c: (c, g)),
                   pl.BlockSpec((SSD_CHUNK, gw), lambda g, c: (c, g)),
                   pl.BlockSpec((None, None, SSD_N, gw), lambda g, c: (c, g, 0, 0))],
        scratch_shapes=[pltpu.VMEM((SSD_N, gw), F32), pltpu.VMEM((SSD_CHUNK, LANES), F32)],
        compiler_params=_cparams(("parallel", "arbitrary")))(
            zx, xc, xc, xc, zx, bias_p, alog_p, d_lane, nw, after)


def _ssd_bwd(dyn, y, zx, xc, prev, bias_p, alog_p, d_lane, nw, d_inner, name):
    L = zx.shape[0]
    nc = L // SSD_CHUNK
    gw = d_inner // SSD_G
    heads = gw // SSD_P
    n_pair = heads // 2
    bc0 = d_inner // LANES
    dt0 = (2 * d_inner + 2 * SSD_G * SSD_N) // LANES

    def body(dyn_ref, y_ref, z_ref, xs_ref, b_ref, c_ref, dt_ref, prev_ref, bias_ref, alog_ref, dl_ref, nw_ref,
             dz_ref, dxs_ref, db_ref, dc_ref, ddt_ref, dbias_ref, dalog_ref, dd_ref, dnw_ref,
             ds_ref, cst_ref, racc_ref):
        @pl.when(pl.program_id(1) == 0)
        def _():
            ds_ref[...] = jnp.zeros_like(ds_ref)
            dbias_ref[...] = jnp.zeros_like(dbias_ref)
            dalog_ref[...] = jnp.zeros_like(dalog_ref)
            dd_ref[...] = jnp.zeros_like(dd_ref)
            dnw_ref[...] = jnp.zeros_like(dnw_ref)

        q = _ssd_common(dt_ref, bias_ref, alog_ref, b_ref, c_ref, cst_ref, SSD_G * heads)
        lane, row, lo = q["lane"], q["row"], q["lo"]
        lane1 = lax.broadcasted_iota(jnp.int32, (1, LANES), 1)
        head0 = pl.program_id(0) * heads
        mine = (lane >= head0) & (lane < head0 + heads)

        yv, zv, dynv, nwv = y_ref[...], z_ref[...], dyn_ref[...], nw_ref[...]
        sig = _sigmoid(zv)
        sz = zv * sig
        yg = yv * sz
        rstd = lax.rsqrt(jnp.mean(yg * yg, axis=-1, keepdims=True) + NORM_EPS)
        yhat = yg * rstd
        dnw_ref[...] += jnp.sum(dynv * yhat, axis=0, keepdims=True)
        dyh = dynv * nwv
        dyg = rstd * (dyh - yhat * jnp.mean(dyh * yhat, axis=-1, keepdims=True))
        dz_ref[...] = (dyg * yv * (sig * (1.0 + zv * (1.0 - sig)))).astype(BF16)
        dy_all = dyg * sz

        dg = jnp.zeros((SSD_CHUNK, SSD_CHUNK), F32)
        dc_acc = jnp.zeros((SSD_CHUNK, SSD_N), F32)
        db_acc = jnp.zeros((SSD_CHUNK, SSD_N), F32)
        dcs_mat = jnp.zeros((SSD_CHUNK, LANES), F32)
        ddt_mat = jnp.zeros((SSD_CHUNK, LANES), F32)
        dd_row = jnp.zeros((1, LANES), F32)
        racc_ref[...] = jnp.zeros_like(racc_ref)
        is_last = row == SSD_CHUNK - 1

        for j in range(n_pair):
            sl = slice(j * LANES, (j + 1) * LANES)
            ha, hb = head0 + 2 * j, head0 + 2 * j + 1
            p = _pair_terms(q, cst_ref, ha)
            xs_p = xs_ref[:, sl]
            dyp = dy_all[:, sl]
            xp = xs_p * p["dt_b"]
            xb = xp.astype(BF16)
            s_p = prev_ref[:, sl]
            s_pb = s_p.astype(BF16)
            dsn = ds_ref[:, sl]
            dsnb = dsn.astype(BF16)
            m_f = [q["scores"] * p["lm"][0], q["scores"] * p["lm"][1]]

            t0 = dyp * xs_p
            dd_row = dd_row + jnp.where(lane1 == ha, _sum_all(jnp.where(lo, t0, 0.0)), 0.0) \
                + jnp.where(lane1 == hb, _sum_all(jnp.where(lo, 0.0, t0)), 0.0)
            dxs_p = dl_ref[:, sl] * dyp

            yo = _dot(q["cb"], s_pb) * p["e_b"]
            dcs_b = (dyp * p["e_b"]).astype(BF16)
            dc_acc = dc_acc + _dot(dcs_b, s_pb, "nt")
            ds_yo = _dot(q["cb"], dcs_b, "tn")
            t1 = dyp * yo
            dcs_cols = [jnp.sum(jnp.where(lo, t1, 0.0), axis=1, keepdims=True),
                        jnp.sum(jnp.where(lo, 0.0, t1), axis=1, keepdims=True)]

            t2 = dsn * s_p
            dlast = [p["explast"][0] * _sum_all(jnp.where(lo, t2, 0.0)),
                     p["explast"][1] * _sum_all(jnp.where(lo, 0.0, t2))]
            ds_ref[:, sl] = dsn * p["explast_b"] + ds_yo
            w = _dot(q["bb"], dsnb)
            db_acc = db_acc + _dot((xp * p["dec_b"]).astype(BF16), dsnb, "nt")
            dxp = w * p["dec_b"]
            t3 = w * xp
            e = [jnp.sum(jnp.where(lo, t3, 0.0), axis=1, keepdims=True) * p["dec_cols"][0],
                 jnp.sum(jnp.where(lo, 0.0, t3), axis=1, keepdims=True) * p["dec_cols"][1]]
            for i in range(2):
                dlast[i] = dlast[i] + jnp.sum(e[i], axis=0, keepdims=True)
                dcs_cols[i] = dcs_cols[i] - e[i]

            dyb = dyp.astype(BF16)
            dy_h = [jnp.where(lo, dyp, 0.0).astype(BF16), jnp.where(lo, 0.0, dyp).astype(BF16)]
            dms = [_dot(dy_h[0], xb, "nt"), _dot(dy_h[1], xb, "nt")]
            dxp = dxp + jnp.where(lo, _dot(m_f[0].astype(BF16), dyb, "tn"), _dot(m_f[1].astype(BF16), dyb, "tn"))
            for i, h in enumerate((ha, hb)):
                dg = dg + dms[i] * p["lm"][i]
                qm = dms[i] * m_f[i]
                dcs_cols[i] = dcs_cols[i] + jnp.sum(qm, axis=1, keepdims=True)
                racc_ref[pl.ds(h, 1), :] = jnp.sum(qm, axis=0, keepdims=True)

            dxs_ref[:, sl] = dxs_p + dxp * p["dt_b"]
            t4 = dxp * xs_p
            ddt_cols = [jnp.sum(jnp.where(lo, t4, 0.0), axis=1, keepdims=True),
                        jnp.sum(jnp.where(lo, 0.0, t4), axis=1, keepdims=True)]
            for i, h in enumerate((ha, hb)):
                sel = lane == h
                dcs_mat = dcs_mat + jnp.where(sel, dcs_cols[i], 0.0) + jnp.where(sel & is_last, dlast[i], 0.0)
                ddt_mat = ddt_mat + jnp.where(sel, ddt_cols[i], 0.0)

        dcs_mat = dcs_mat - racc_ref[...].T
        tri_t = (row <= lane).astype(BF16)
        da = _dot_exact01(tri_t, dcs_mat)
        ddt = ddt_mat + da * q["a_row"]
        dalog_ref[...] += jnp.sum(jnp.where(mine, da * q["dt"], 0.0), axis=0, keepdims=True) * q["a_row"]
        draw = jnp.where(mine, ddt * _sigmoid(q["raw"]), 0.0)
        ddt_ref[...] = draw
        dbias_ref[...] += jnp.sum(draw, axis=0, keepdims=True)
        dd_ref[...] += dd_row
        dgb = dg.astype(BF16)
        dc_ref[...] = dc_acc + _dot(dgb, q["bb"])
        db_ref[...] = db_acc + _dot(dgb, q["cb"], "tn")

    rev = lambda c: nc - 1 - c
    grp = lambda width: pl.BlockSpec((None, 1, width), lambda g, c: (g, 0, 0))
    blk = lambda width, off: pl.BlockSpec((SSD_CHUNK, width), lambda g, c: (rev(c), off + g))
    head_vec = pl.BlockSpec((1, LANES), lambda g, c: (0, 0))
    return _pcall(
        body, name=name, grid=(SSD_G, nc),
        out_shape=[jax.ShapeDtypeStruct(zx.shape, BF16), jax.ShapeDtypeStruct((L, d_inner), F32),
                   jax.ShapeDtypeStruct((L, SSD_G * SSD_N), F32), jax.ShapeDtypeStruct((L, SSD_G * SSD_N), F32),
                   jax.ShapeDtypeStruct((SSD_G, L, LANES), F32),
                   jax.ShapeDtypeStruct((SSD_G, 1, LANES), F32), jax.ShapeDtypeStruct((SSD_G, 1, LANES), F32),
                   jax.ShapeDtypeStruct((SSD_G, 1, LANES), F32), jax.ShapeDtypeStruct((SSD_G, 1, gw), F32)],
        in_specs=[blk(gw, 0), blk(gw, 0), blk(gw, 0), blk(gw, 0), blk(SSD_N, bc0), blk(SSD_N, bc0 + SSD_G),
                  pl.BlockSpec((SSD_CHUNK, LANES), lambda g, c: (rev(c), dt0)),
                  pl.BlockSpec((None, None, SSD_N, gw), lambda g, c: (rev(c), g, 0, 0)),
                  head_vec, head_vec, grp(gw), grp(gw)],
        out_specs=[blk(gw, 0), blk(gw, 0), blk(SSD_N, 0), blk(SSD_N, 0),
                   pl.BlockSpec((None, SSD_CHUNK, LANES), lambda g, c: (g, rev(c), 0)),
                   grp(LANES), grp(LANES), grp(LANES), grp(gw)],
        scratch_shapes=[pltpu.VMEM((SSD_N, gw), F32), pltpu.VMEM((SSD_CHUNK, LANES), F32),
                        pltpu.VMEM((SSD_CHUNK, LANES), F32)],
        compiler_params=_cparams(("parallel", "arbitrary")))(
            dyn, y, zx, xc, xc, xc, zx, prev, bias_p, alog_p, d_lane, nw)


def _cond_mod(c_pad, ada_w, ada_b_loc, after, name):
    depth, D, n = ada_w.shape
    rows = c_pad.shape[0]

    def body(c_ref, w_ref, b_ref, after_ref, mod_ref, cond_ref):
        cv = c_ref[...]
        cond = cv * _sigmoid(cv)
        cond_ref[...] = cond
        mod_ref[...] = _dot(cond.astype(BF16), w_ref[...].astype(BF16)) + b_ref[...]

    return _pcall(
        body, name=name, grid=(depth,),
        out_shape=[jax.ShapeDtypeStruct((depth, rows, n), F32), jax.ShapeDtypeStruct((rows, D), F32)],
        in_specs=[pl.BlockSpec((rows, D), lambda i: (0, 0)),
                  pl.BlockSpec((None, D, n), lambda i: (i, 0, 0)),
                  pl.BlockSpec((None, 1, n), lambda i: (i, 0, 0)),
                  pl.BlockSpec(memory_space=pl.ANY)],
        out_specs=[pl.BlockSpec((None, rows, n), lambda i: (i, 0, 0)),
                   pl.BlockSpec((rows, D), lambda i: (0, 0))],
        compiler_params=_cparams(("arbitrary",)))(c_pad, ada_w, ada_b_loc, after)


def _adamw_math(g, w, m, v):
    m_new = ADAM_B1 * m + (1.0 - ADAM_B1) * g
    v_new = ADAM_B2 * v + (1.0 - ADAM_B2) * (g * g)
    m_hat = m_new / (1.0 - ADAM_B1 ** ADAM_STEP)
    v_hat = v_new / (1.0 - ADAM_B2 ** ADAM_STEP)
    delta = -ADAM_LR * (m_hat / (jnp.sqrt(v_hat) + ADAM_EPS) + ADAM_WD * w)
    return delta, m_new, v_new


def _adamw_sum(parts, w, m, v, layer, name, prev=None, tr=None, window_off=None):
    depth, R, C = w.shape
    tr = _tile(R, tr if tr is not None else (512 if C <= 512 else 256))
    win = parts.shape[2]
    scratch = [] if window_off is None else [pltpu.VMEM((tr, win), F32)]

    def body(p_ref, w_ref, m_ref, v_ref, *rest):
        g_ref, d_ref, mo_ref, vo_ref = rest[-4 - len(scratch):len(rest) - len(scratch)]
        g = p_ref[0].astype(F32)
        for k in range(1, N_DEV):
            g = g + p_ref[k].astype(F32)
        if window_off is not None:
            me = _my_index()
            off = 0
            for k in range(N_DEV):
                off = jnp.where(me == k, window_off[k], off)
            src = lax.broadcasted_iota(jnp.int32, (win, win), 0)
            dst = lax.broadcasted_iota(jnp.int32, (win, win), 1)
            shift = ((src == dst + off) & (dst < C)).astype(BF16)
            hi, mid, lo = _split3(g)
            rest[-1][...] = _dot(hi, shift) + _dot(mid, shift) + _dot(lo, shift)
            g = rest[-1][:, 0:C]
        d, mn, vn = _adamw_math(g, w_ref[...], m_ref[...], v_ref[...])
        g_ref[...] = g
        d_ref[...] = d
        mo_ref[...] = mn
        vo_ref[...] = vn

    blk = pl.BlockSpec((None, tr, C), lambda i: (layer, i, 0))
    prev = list(prev) if prev is not None else []
    return _pcall(
        body, name=name, grid=(R // tr,),
        out_shape=[jax.ShapeDtypeStruct((depth, R, C), F32)] * 4,
        in_specs=[pl.BlockSpec((N_DEV, tr, win), lambda i: (0, i, 0)), blk, blk, blk]
        + [pl.BlockSpec(memory_space=pl.ANY)] * len(prev),
        out_specs=[blk] * 4, input_output_aliases={4 + k: k for k in range(len(prev))},
        scratch_shapes=scratch,
        compiler_params=_cparams(("parallel",)))(parts, w, m, v, *prev)


def _adamw_small(parts, wmv, head_parts, head_wmv, loss_parts, name):
    n, nh = len(parts), len(head_parts)
    n_heads = head_wmv[0][0].shape[1] if nh else 0
    groups = head_parts[0].shape[1] if nh else 0
    d_model = loss_parts.shape[2]

    def body(*refs):
        p_refs, refs = refs[:n], refs[n:]
        wmv_refs, refs = refs[:3 * n], refs[3 * n:]
        hp_refs, refs = refs[:nh], refs[nh:]
        hwmv_refs, refs = refs[:3 * nh], refs[3 * nh:]
        loss_ref, refs = refs[0], refs[1:]
        outs, loss_out, head_scr = refs[:4 * (n + nh)], refs[4 * (n + nh)], refs[4 * (n + nh) + 1]

        def update(i, g, w_ref, m_ref, v_ref):
            res = (g,) + _adamw_math(g, w_ref[...], m_ref[...], v_ref[...])
            for o_ref, r in zip(outs[4 * i:4 * i + 4], res):
                o_ref[...] = r

        for i in range(n):
            g = p_refs[i][0]
            for k in range(1, N_DEV):
                g = g + p_refs[i][k]
            update(i, g, *wmv_refs[3 * i:3 * i + 3])
        for i in range(nh):
            g = None
            for k in range(N_DEV):
                for grp in range(groups):
                    g = hp_refs[i][k, grp] if g is None else g + hp_refs[i][k, grp]
            head_scr[...] = g
            update(n + i, head_scr[:, 0:n_heads], *hwmv_refs[3 * i:3 * i + 3])
        tot = loss_ref[0]
        for k in range(1, N_DEV):
            tot = tot + loss_ref[k]
        loss_out[...] = jnp.broadcast_to(_sum_all(tot) * (0.5 / d_model), loss_out.shape)

    operands = list(parts) + [a for t in wmv for a in t] + list(head_parts) + [a for t in head_wmv for a in t]
    operands.append(loss_parts)
    out_shape = [jax.ShapeDtypeStruct(t[0].shape, F32) for t in list(wmv) + list(head_wmv) for _ in range(4)]
    out_shape.append(jax.ShapeDtypeStruct((1, LANES), F32))
    vmem = pl.BlockSpec(memory_space=pltpu.VMEM)
    outs = _pcall(body, name=name, out_shape=out_shape, in_specs=[vmem] * len(operands),
                  out_specs=[vmem] * len(out_shape), scratch_shapes=[pltpu.VMEM((1, LANES), F32)],
                  compiler_params=_cparams())(*operands)
    return [outs[4 * i:4 * i + 4] for i in range(n + nh)], outs[-1]


def _ada_adamw(cond_pad, dmod_pad, w, m, v, name, tr=512):
    depth, D, n = w.shape
    rows = cond_pad.shape[0]
    tr = _tile(D, tr)

    def body(c_ref, dm_ref, w_ref, m_ref, v_ref, g_ref, d_ref, mo_ref, vo_ref):
        g = _dot(c_ref[...].astype(BF16), dm_ref[...].astype(BF16), "tn")
        d, mn, vn = _adamw_math(g, w_ref[...], m_ref[...], v_ref[...])
        g_ref[...] = g
        d_ref[...] = d
        mo_ref[...] = mn
        vo_ref[...] = vn

    blk = pl.BlockSpec((None, tr, n), lambda i, r: (i, r, 0))
    return _pcall(
        body, name=name, grid=(depth, D // tr),
        out_shape=[jax.ShapeDtypeStruct((depth, D, n), F32)] * 4,
        in_specs=[pl.BlockSpec((rows, tr), lambda i, r: (0, r)),
                  pl.BlockSpec((None, rows, n), lambda i, r: (i, 0, 0)), blk, blk, blk],
        out_specs=[blk] * 4, compiler_params=_cparams(("parallel", "parallel")))(cond_pad, dmod_pad, w, m, v)


def kernel(x, c, ada_w, ada_b, mix_norm_w, mlp_norm_w, mlp_up, mlp_down, ssd_in_w, ssd_conv_w, ssd_conv_b, ssd_dt_bias, ssd_A_log, ssd_D, ssd_norm_w, ssd_out_w, sc_in_w, sc_conv_w, sc_out_w, final_norm_w, loss_target, m_ada_w, m_ada_b, m_mix_norm_w, m_mlp_norm_w, m_mlp_up, m_mlp_down, m_ssd_in_w, m_ssd_conv_w, m_ssd_conv_b, m_ssd_dt_bias, m_ssd_A_log, m_ssd_D, m_ssd_norm_w, m_ssd_out_w, m_sc_in_w, m_sc_conv_w, m_sc_out_w, m_final_norm_w, v_ada_w, v_ada_b, v_mix_norm_w, v_mlp_norm_w, v_mlp_up, v_mlp_down, v_ssd_in_w, v_ssd_conv_w, v_ssd_conv_b, v_ssd_dt_bias, v_ssd_A_log, v_ssd_D, v_ssd_norm_w, v_ssd_out_w, v_sc_in_w, v_sc_conv_w, v_sc_out_w, v_final_norm_w):
    weights = dict(ada_w=ada_w, ada_b=ada_b, mix_norm_w=mix_norm_w, mlp_norm_w=mlp_norm_w, mlp_up=mlp_up,
                   mlp_down=mlp_down, ssd_in_w=ssd_in_w, ssd_conv_w=ssd_conv_w, ssd_conv_b=ssd_conv_b,
                   ssd_dt_bias=ssd_dt_bias, ssd_A_log=ssd_A_log, ssd_D=ssd_D, ssd_norm_w=ssd_norm_w,
                   ssd_out_w=ssd_out_w, sc_in_w=sc_in_w, sc_conv_w=sc_conv_w, sc_out_w=sc_out_w,
                   final_norm_w=final_norm_w)
    moms = dict(ada_w=m_ada_w, ada_b=m_ada_b, mix_norm_w=m_mix_norm_w, mlp_norm_w=m_mlp_norm_w, mlp_up=m_mlp_up,
                mlp_down=m_mlp_down, ssd_in_w=m_ssd_in_w, ssd_conv_w=m_ssd_conv_w, ssd_conv_b=m_ssd_conv_b,
                ssd_dt_bias=m_ssd_dt_bias, ssd_A_log=m_ssd_A_log, ssd_D=m_ssd_D, ssd_norm_w=m_ssd_norm_w,
                ssd_out_w=m_ssd_out_w, sc_in_w=m_sc_in_w, sc_conv_w=m_sc_conv_w, sc_out_w=m_sc_out_w,
                final_norm_w=m_final_norm_w)
    vars_ = dict(ada_w=v_ada_w, ada_b=v_ada_b, mix_norm_w=v_mix_norm_w, mlp_norm_w=v_mlp_norm_w, mlp_up=v_mlp_up,
                 mlp_down=v_mlp_down, ssd_in_w=v_ssd_in_w, ssd_conv_w=v_ssd_conv_w, ssd_conv_b=v_ssd_conv_b,
                 ssd_dt_bias=v_ssd_dt_bias, ssd_A_log=v_ssd_A_log, ssd_D=v_ssd_D, ssd_norm_w=v_ssd_norm_w,
                 ssd_out_w=v_ssd_out_w, sc_in_w=v_sc_in_w, sc_conv_w=v_sc_conv_w, sc_out_w=v_sc_out_w,
                 final_norm_w=v_final_norm_w)
    names = list(weights)

    L, D = x.shape[1], x.shape[2]
    d_inner = 2 * D
    n_heads = d_inner // SSD_P
    hpg = n_heads // SSD_G
    gw = d_inner // SSD_G
    conv_dim = d_inner + 2 * SSD_G * SSD_N
    zx_dim = d_inner + conv_dim
    zx_pad = -(-(zx_dim + LANES) // 512) * 512
    in_ws = ssd_in_w.shape[2]
    in_base, in_off, in_win = _window_geometry(in_ws)
    me = _my_index()
    x0 = x[0]
    tgt = loss_target[0]

    n_mod = ada_w.shape[2]
    (c_all,) = _exchange([c], "gather_c", gather=True)
    gather_handle = {}
    (gather_handle["ssd_in_w"],), token_in = _xfer_start(
        [ssd_in_w[0].astype(BF16)], "gather_start_ssd_in_w", gather=True, via_sibling=(0,), after=(c_all,))
    c_pad = jnp.pad(c_all.reshape(N_DEV, D), ((0, 16 - N_DEV), (0, 0)))
    ada_b_loc = lax.dynamic_slice_in_dim(ada_b, me * n_mod, n_mod, axis=1).reshape(2, 1, n_mod)
    mod_blk, cond_pad = _cond_mod(c_pad, ada_w, ada_b_loc, token_in, "cond_mod")
    gather_order = ["mod", "ssd_conv_w", "sc_conv_w", "ssd_out_w", "up0", "down0", "sc_in_w", "sc_out_w", "up1",
                    "down1"]
    gather_src = dict(mod=mod_blk, ssd_conv_w=ssd_conv_w[0], sc_conv_w=sc_conv_w[0],
                      ssd_out_w=ssd_out_w[0].astype(BF16),
                      up0=mlp_up[0].astype(BF16), down0=mlp_down[0].astype(BF16),
                      sc_in_w=sc_in_w[0].astype(BF16), sc_out_w=sc_out_w[0].astype(BF16),
                      up1=mlp_up[1].astype(BF16), down1=mlp_down[1].astype(BF16))
    handles, gather_token = _xfer_start([gather_src[k] for k in gather_order], "gather_start", gather=True,
                                        via_sibling=tuple(range(3, len(gather_order))))
    gather_handle.update(zip(gather_order, handles))

    def gathered(keys, after, forward):
        tag = "_".join(keys)
        lands = _xfer_wait([gather_handle[k] for k in keys], after, f"gather_wait_{tag}", gather=True)
        return _sibling_forward(lands, f"gather_forward_{tag}") if forward else lands

    def forward_behind(keys, after):
        tag = "_".join(keys)
        lands = _xfer_wait([gather_handle[k] for k in keys], after, f"gather_wait_{tag}", gather=True)
        fwd_handles, token = _sibling_forward_start(lands, f"gather_forward_start_{tag}")
        return (lambda done: _sibling_forward_wait(fwd_handles, done, f"gather_forward_wait_{tag}")), token

    (ssd_in_g,) = gathered(["ssd_in_w"], (gather_token, m_ssd_in_w, v_ssd_in_w), True)
    w_in_all = _shards_to_columns(ssd_in_g, in_base, in_off, in_win, zx_pad, "ssd_in_w_columns")
    (mod_all,) = gathered(["mod"], w_in_all, False)
    mod_mine = lax.dynamic_index_in_dim(mod_all, me, axis=2, keepdims=False)
    mod_mine = jnp.transpose(mod_mine, (1, 0, 2)).reshape(2, 6, 1, D)
    sh_m, sc_m, g_m, sh_f, sc_f, g_f = [[mod_mine[i, k] for i in range(2)] for k in range(6)]

    vec = lambda a: a.reshape(1, -1)
    grads = {}
    small = {}

    _, h0 = _norm_mod_fwd(x0, None, None, vec(mix_norm_w[0]), sc_m[0], sh_m[0], "l0_mix_norm")
    cw_all, scw_all = gathered(["ssd_conv_w", "sc_conv_w"], h0, False)
    (zx,) = _mm_nn(h0, w_in_all, F32, "ssd_in_proj", tm=2048, tn=512)
    conv_b0 = vec(ssd_conv_b[0])
    conv_w_full = jnp.transpose(cw_all, (1, 0, 2)).reshape(SSD_K, conv_dim)
    sc_conv_full = jnp.transpose(scw_all, (1, 0, 2)).reshape(SC_K, D)
    xc = _ssd_conv_fwd(zx, conv_w_full, conv_b0, d_inner, conv_dim, "ssd_conv")
    bias_p = jnp.pad(ssd_dt_bias[0], (0, LANES - n_heads)).reshape(1, LANES)
    alog_p = jnp.pad(ssd_A_log[0], (0, LANES - n_heads)).reshape(1, LANES)
    d_lane = jnp.repeat(ssd_D[0], SSD_P).reshape(SSD_G, 1, gw)
    nw_g = ssd_norm_w[0].reshape(SSD_G, 1, gw)
    finish, token = forward_behind(["ssd_out_w", "up0", "down0"], xc)
    y_ssd, yn, prev = _ssd_fwd(zx, xc, bias_p, alog_p, d_lane, nw_g, d_inner, token, "ssd_scan")
    ups, downs = [None, None], [None, None]
    ssd_out_g, ups[0], down0_g = finish(yn)
    w_ssd_out, downs[0] = ssd_out_g.reshape(-1, D), down0_g.reshape(-1, D)
    (mix0,) = _mm_nn(yn, w_ssd_out, F32, "ssd_out_proj")
    x1, h1 = _norm_mod_fwd(x0, mix0, g_m[0], vec(mlp_norm_w[0]), sc_f[0], sh_f[0], "l0_mlp_norm")
    u0, s0 = _mm_nn_blocked(h1, ups[0], "l0_mlp_up", _ep_relu2, [BF16, BF16])
    finish, token = forward_behind(["sc_in_w", "sc_out_w", "up1", "down1"], s0)
    (d0,) = _mm_nn(s0, downs[0], F32, "l0_mlp_down", after=(token,))
    x2, h2 = _norm_mod_fwd(x1, d0, g_f[0], vec(mix_norm_w[1]), sc_m[1], sh_m[1], "l1_mix_norm")
    sc_in_g, sc_out_g, ups[1], down1_g = finish(h2)
    w_sc_out, downs[1] = sc_out_g.reshape(-1, D), down1_g.reshape(-1, D)
    (proj,) = _mm_nn_blocked(h2, sc_in_g, "sc_in_proj", _ep_store(F32), [F32])
    yc = _sc_conv_fwd(proj, sc_conv_full, "sc_conv")
    (mix1,) = _mm_nn(yc, w_sc_out, F32, "sc_out_proj")
    x3, h3 = _norm_mod_fwd(x2, mix1, g_m[1], vec(mlp_norm_w[1]), sc_f[1], sh_f[1], "l1_mlp_norm")
    u1, s1 = _mm_nn_blocked(h3, ups[1], "l1_mlp_up", _ep_relu2, [BF16, BF16])
    (d1,) = _mm_nn(s1, downs[1], F32, "l1_mlp_down")

    dx, loss_lane, dfw, dd1, dg = _final_loss(x3, d1, g_f[1], vec(final_norm_w), tgt, "final_loss")
    small["final_norm_w"] = dfw

    dmod = [[None] * 6 for _ in range(2)]
    dmod[1][5] = dg

    def mlp_backward(i, dx_out, dd, x_mid, h_in, u, s, mix, gate):
        du = _mm_nt(dd, downs[i], BF16, f"l{i}_mlp_down_bwd", epilogue=_ep_relu2_bwd, extra=(u,))
        gdown = _mm_tn(s, dd, BF16, f"l{i}_mlp_down_wgrad").reshape(N_DEV, -1, D)
        gup = _mm_tn_blocked(h_in, du, BF16, f"l{i}_mlp_up_wgrad")
        (h_down, h_up), token = _xfer_start([gdown, gup], f"l{i}_mlp_grads_start", gather=False)
        grad_handle[f"mlp_down{i}"], grad_handle[f"mlp_up{i}"] = h_down, h_up
        dh = _mm_nt_blocked(du, ups[i], F32, f"l{i}_mlp_up_bwd", after=(token,))
        dxm, dsh, dsc, dnw, dmix, dgate = _norm_mod_bwd(dh, x_mid, vec(mlp_norm_w[i]), sc_f[i], dx_out,
                                                        f"l{i}_mlp_norm_bwd", branch=(mix, gate))
        dmod[i][3], dmod[i][4], dmod[i][2] = dsh, dsc, dgate
        return dxm, dmix, dnw

    grad_handle = {}
    dx3, dyc, dnw_mlp1 = mlp_backward(1, dx, dd1, x3, h3, u1, s1, mix1, g_m[1])
    g_sc_out = _mm_tn(yc, dyc, BF16, "sc_out_wgrad").reshape(N_DEV, -1, D)
    dconv_out = _mm_nt(dyc, w_sc_out, F32, "sc_out_bwd")
    dbg, dcg, dxv, dscw = _sc_conv_bwd(proj, sc_conv_full, dconv_out, "sc_conv_bwd")
    dproj = jnp.concatenate([dbg, dcg, dxv], axis=1)
    g_sc_in = _mm_tn_blocked(h2, dproj, BF16, "sc_in_wgrad")
    (grad_handle["sc_out_w0"], grad_handle["sc_in_w0"]), token = _xfer_start(
        [g_sc_out, g_sc_in], "sc_grads_start", gather=False)
    dh2 = _mm_nt_blocked(dproj, sc_in_g, F32, "sc_in_bwd", after=(token,))
    dx2, dsh, dsc, dnw_mix1, dd0, dg = _norm_mod_bwd(dh2, x2, vec(mix_norm_w[1]), sc_m[1], dx3, "l1_mix_norm_bwd",
                                                     branch=(d0, g_f[0]))
    dmod[1][0], dmod[1][1], dmod[0][5] = dsh, dsc, dg
    dx1, dyo, dnw_mlp0 = mlp_backward(0, dx2, dd0, x1, h1, u0, s0, mix0, g_m[0])
    g_ssd_out = _mm_tn(yn, dyo, BF16, "ssd_out_wgrad").reshape(N_DEV, -1, D)
    (grad_handle["ssd_out_w0"],), token = _xfer_start([g_ssd_out], "ssd_out_grad_start", gather=False)
    dyn = _mm_nt(dyo, w_ssd_out, F32, "ssd_out_bwd", after=(token,))
    dz, dxs, db_, dc_, ddt, dbias, dalog, dd_, dnw_ssd = _ssd_bwd(
        dyn, y_ssd, zx, xc, prev, bias_p, alog_p, d_lane, nw_g, d_inner, "ssd_scan_bwd")
    dzx, dcw, dcb = _ssd_conv_bwd(zx, conv_w_full, conv_b0, [dxs, db_, dc_], dz, d_inner, "ssd_conv_bwd")
    dzx = _dzx_finish(dzx, ddt, zx_dim, "ssd_dzx_finish")
    g_in_all = _mm_tn(h0, dzx, BF16, "ssd_in_wgrad", tn=512, tk=2048)
    g_ssd_in = jnp.stack([g_in_all[:, b:b + in_win] for b in in_base], axis=0)
    (grad_handle["ssd_in_w0"],), token = _xfer_start([g_ssd_in], "ssd_in_grad_start", gather=False)
    dh0 = _mm_nt(dzx, w_in_all, F32, "ssd_in_bwd", tk=dzx.shape[1] // 2, after=(token,))
    grad_x, dsh, dsc, dnw_mix0 = _norm_mod_bwd(dh0, x0, vec(mix_norm_w[0]), sc_m[0], dx1, "l0_mix_norm_bwd")
    dmod[0][0], dmod[0][1] = dsh, dsc

    small["ada_b"] = jnp.concatenate([jnp.concatenate(dmod[i], axis=1) for i in range(2)], axis=0)
    small["mix_norm_w"] = jnp.concatenate([dnw_mix0, dnw_mix1], axis=0)
    small["mlp_norm_w"] = jnp.concatenate([dnw_mlp0, dnw_mlp1], axis=0)
    small["ssd_conv_w"] = dcw
    small["ssd_conv_b"] = dcb
    small["ssd_norm_w"] = dnw_ssd.reshape(1, d_inner)
    small["sc_conv_w"] = dscw
    small["loss"] = loss_lane
    small_names = list(small)
    head_names = ["ssd_dt_bias", "ssd_A_log", "ssd_D"]
    handles, small_token = _xfer_start([small[k] for k in small_names] + [dbias, dalog, dd_],
                                       "small_grads_start", gather=True)

    out_g, out_d, out_m, out_v = {}, {}, {}, {}

    layer_res = {}

    def big_update(name, i, after):
        (parts,) = _xfer_wait([grad_handle[f"{name}{i}"]], after, f"grads_wait_{name}_{i}", gather=False)
        res = _adamw_sum(parts, weights[name], moms[name], vars_[name], i, f"adamw_{name}_{i}",
                         prev=layer_res.get(name), window_off=in_off if name == "ssd_in_w" else None)
        layer_res[name] = res
        return res[1]

    chain = small_token
    for name, i in [("mlp_down", 1), ("mlp_up", 1), ("sc_out_w", 0), ("sc_in_w", 0), ("mlp_down", 0),
                    ("mlp_up", 0), ("ssd_out_w", 0), ("ssd_in_w", 0)]:
        chain = big_update(name, i, chain)
    gathered_small = _xfer_wait(handles, chain, "small_grads_wait", gather=True)
    small_all = dict(zip(small_names + head_names, gathered_small))

    dmod_loc = lax.dynamic_slice_in_dim(small_all["ada_b"], me * n_mod, n_mod, axis=2)
    dmod_pad = jnp.pad(jnp.transpose(dmod_loc, (1, 0, 2)), ((0, 0), (0, 16 - N_DEV), (0, 0)))
    out_g["ada_w"], out_d["ada_w"], out_m["ada_w"], out_v["ada_w"] = _ada_adamw(
        cond_pad, dmod_pad, ada_w, m_ada_w, v_ada_w, "adamw_ada_w")

    for k in ("ssd_conv_w", "sc_conv_w"):
        n_loc = weights[k].shape[2]
        small_all[k] = lax.dynamic_slice_in_dim(small_all[k], me * n_loc, n_loc, axis=2)
    plain = [k for k in small_names if k != "loss"]
    as2d = lambda a: a.reshape(-1, a.shape[-1])
    res, loss_row = _adamw_small(
        [small_all[k] for k in plain], [tuple(as2d(d[k]) for d in (weights, moms, vars_)) for k in plain],
        [small_all[k] for k in head_names], [tuple(as2d(d[k]) for d in (weights, moms, vars_)) for k in head_names],
        small_all["loss"], "adamw_small")
    loss = loss_row[0, 0]
    for k, res4 in zip(plain + head_names, res):
        for r, dst in zip(res4, (out_g, out_d, out_m, out_v)):
            dst[k] = r.reshape(weights[k].shape)
    for name, res4 in layer_res.items():
        for r, dst in zip(res4, (out_g, out_d, out_m, out_v)):
            dst[name] = r

    return (loss, grad_x[None], *[out_g[k] for k in names], *[out_d[k] for k in names],
            *[out_m[k] for k in names], *[out_v[k] for k in names])
```

```python
import functools

import jax
import jax.numpy as jnp
from jax import lax
from jax.experimental import pallas as pl
from jax.experimental.pallas import tpu as pltpu

F32 = jnp.float32
BF16 = jnp.bfloat16
N_DEV = 8
MESH_AXES = ("x", "y", "c")
MESH = pl.DeviceIdType.MESH

NORM_EPS = 1e-5
SSD_G = 4
SSD_P = 64
SSD_N = 128
SSD_CHUNK = 128
SSD_K = 4
SC_K = 3
LANES = 128

ADAM_LR = 0.001
ADAM_B1 = 0.9
ADAM_B2 = 0.999
ADAM_EPS = 1e-08
ADAM_WD = 0.01
ADAM_STEP = 10

VMEM_LIMIT = 56 * 1024 * 1024


def _pcall(body, **kw):
    return pl.pallas_call(body, **kw)


def _cparams(sem=None):
    if sem is None:
        return pltpu.CompilerParams(vmem_limit_bytes=VMEM_LIMIT)
    return pltpu.CompilerParams(dimension_semantics=sem, vmem_limit_bytes=VMEM_LIMIT)


def _my_index():
    return 4 * lax.axis_index("x") + 2 * lax.axis_index("y") + lax.axis_index("c")


_PEER_MASKS = [(0, 0, 1), (0, 1, 0), (0, 1, 1), (1, 0, 0), (1, 0, 1), (1, 1, 0), (1, 1, 1)]


def _peers():
    x, y, c = lax.axis_index("x"), lax.axis_index("y"), lax.axis_index("c")
    out = []
    for mx, my, mc in _PEER_MASKS:
        px = (1 - x) if mx else x
        py = (1 - y) if my else y
        pc = (1 - c) if mc else c
        out.append(((px, py, pc), 4 * px + 2 * py + pc))
    return out


def _exchange(arrs, name, gather):
    n = len(arrs)
    n_peer = N_DEV - 1

    def body(*refs):
        ins, outs = refs[:n], refs[n:2 * n]
        send_sems, recv_sems, local_sems = refs[2 * n:]
        me = _my_index()
        peers = _peers()
        started = []
        for a in range(n):
            src_own = ins[a] if gather else ins[a].at[me]
            own = pltpu.make_async_copy(src_own, outs[a].at[me], local_sems.at[a])
            own.start()
            started.append(own)
        sends = []
        for a in range(n):
            for k, (peer, pidx) in enumerate(peers):
                src = ins[a] if gather else ins[a].at[pidx]
                cp = pltpu.make_async_remote_copy(
                    src_ref=src, dst_ref=outs[a].at[me],
                    send_sem=send_sems.at[a * n_peer + k], recv_sem=recv_sems.at[a * n_peer + k],
                    device_id=peer, device_id_type=MESH)
                cp.start()
                sends.append(cp)
        for a in range(n):
            for k, (peer, pidx) in enumerate(peers):
                src = ins[a] if gather else ins[a].at[pidx]
                pltpu.make_async_remote_copy(
                    src_ref=src, dst_ref=outs[a].at[pidx],
                    send_sem=send_sems.at[a * n_peer + k], recv_sem=recv_sems.at[a * n_peer + k],
                    device_id=peer, device_id_type=MESH).wait_recv()
        for cp in sends:
            cp.wait_send()
        for own in started:
            own.wait()

    if gather:
        out_shape = [jax.ShapeDtypeStruct((N_DEV,) + a.shape, a.dtype) for a in arrs]
    else:
        out_shape = [jax.ShapeDtypeStruct(a.shape, a.dtype) for a in arrs]
    any_spec = pl.BlockSpec(memory_space=pl.ANY)
    outs = _pcall(
        body, name=name, out_shape=out_shape,
        in_specs=[any_spec] * n, out_specs=[any_spec] * n,
        scratch_shapes=[pltpu.SemaphoreType.DMA((n * n_peer,)), pltpu.SemaphoreType.DMA((n * n_peer,)),
                        pltpu.SemaphoreType.DMA((n,))],
        compiler_params=pltpu.CompilerParams(has_side_effects=True),
    )(*arrs)
    return list(outs)


def _sibling_forward_start(lands, name):
    n = len(lands)
    n_fwd = len(_OTHER_CHIPS)

    def body(*refs):
        ins, bufs = refs[:n], refs[3 * n:4 * n]
        token = refs[-1]
        sibling = (lax.axis_index("x"), lax.axis_index("y"), 1 - lax.axis_index("c"))
        peers = _peers()
        for a in range(n):
            send_sems, recv_sems = refs[n + 2 * a], refs[n + 2 * a + 1]
            for j, k in enumerate(_OTHER_CHIPS):
                slot = peers[k][1]
                pltpu.make_async_remote_copy(
                    src_ref=ins[a].at[slot], dst_ref=bufs[a].at[slot], send_sem=send_sems.at[j],
                    recv_sem=recv_sems.at[j], device_id=sibling, device_id_type=MESH).start()
        token[...] = jnp.zeros_like(token)

    out_shape, out_specs = [], []
    for _ in range(n):
        out_shape += [pltpu.SemaphoreType.DMA((n_fwd,)), pltpu.SemaphoreType.DMA((n_fwd,))]
        out_specs += [_SEM, _SEM]
    out_shape += [pltpu.HBM(a.shape, a.dtype) for a in lands] + [jax.ShapeDtypeStruct((8, LANES), F32)]
    out_specs += [_HBM] * n + [pl.BlockSpec(memory_space=pltpu.VMEM)]
    outs = _pcall(
        body, name=name, out_shape=tuple(out_shape), in_specs=[_HBM] * n, out_specs=tuple(out_specs),
        input_output_aliases={a: 2 * n + a for a in range(n)},
        compiler_params=pltpu.CompilerParams(has_side_effects=_DATAFLOW),
    )(*[pltpu.with_memory_space_constraint(a, pltpu.HBM) for a in lands])
    return [(outs[2 * n + a], outs[2 * a], outs[2 * a + 1]) for a in range(n)], outs[-1]


def _sibling_forward_wait(handles, after, name):
    n = len(handles)

    def body(*refs):
        sibling = (lax.axis_index("x"), lax.axis_index("y"), 1 - lax.axis_index("c"))
        peers = _peers()
        for a in range(n):
            buf, send_sems, recv_sems = refs[3 * a:3 * a + 3]
            for j, k in enumerate(_OTHER_CHIPS):
                (px, py, pc), slot = peers[k]
                theirs = 4 * px + 2 * py + (1 - pc)
                cp = pltpu.make_async_remote_copy(
                    src_ref=buf.at[slot], dst_ref=buf.at[theirs], send_sem=send_sems.at[j],
                    recv_sem=recv_sems.at[j], device_id=sibling, device_id_type=MESH)
                cp.wait_send()
                cp.wait_recv()

    operands, in_specs = [], []
    for h in handles:
        operands += list(h)
        in_specs += [_HBM, _SEM, _SEM]
    outs = _pcall(
        body, name=name, out_shape=tuple(pltpu.HBM(h[0].shape, h[0].dtype) for h in handles),
        in_specs=in_specs + [pl.BlockSpec(memory_space=pl.ANY)], out_specs=tuple([_HBM] * n),
        input_output_aliases={3 * a: a for a in range(n)},
        compiler_params=pltpu.CompilerParams(has_side_effects=_DATAFLOW),
    )(*operands, after)
    return list(outs)


_HBM = pl.BlockSpec(memory_space=pltpu.HBM)
_SEM = pl.BlockSpec(memory_space=pltpu.SEMAPHORE)
_DATAFLOW = pltpu.SideEffectType.DATAFLOW_SIDE_EFFECTING


_ALL_PEERS = tuple(range(N_DEV - 1))
_SAME_CORE_PEERS = (0, 1, 3, 5)
_OTHER_CHIPS = (1, 3, 5)


def _xfer_start(arrs, name, gather, via_sibling=(), after=()):
    n = len(arrs)
    n_peer = N_DEV - 1
    n_after = len(after)
    peer_ks = [_SAME_CORE_PEERS if a in via_sibling else _ALL_PEERS for a in range(n)]

    def body(*refs):
        ins, lands = refs[:n], refs[n:2 * n]
        sems = refs[2 * n + n_after:5 * n + n_after]
        token = refs[-1]
        me = _my_index()
        peers = _peers()
        for a in range(n):
            send_sems, recv_sems, loc_sem = sems[3 * a:3 * a + 3]
            src_own = ins[a] if gather else ins[a].at[me]
            pltpu.make_async_copy(src_own, lands[a].at[me], loc_sem).start()
            for k in peer_ks[a]:
                peer, pidx = peers[k]
                src = ins[a] if gather else ins[a].at[pidx]
                pltpu.make_async_remote_copy(
                    src_ref=src, dst_ref=lands[a].at[me], send_sem=send_sems.at[k], recv_sem=recv_sems.at[k],
                    device_id=peer, device_id_type=MESH).start()
        token[...] = jnp.zeros_like(token)

    land_shapes = [((N_DEV,) + a.shape) if gather else a.shape for a in arrs]
    out_shape, out_specs = [], []
    for _ in range(n):
        out_shape += [pltpu.SemaphoreType.DMA((n_peer,)), pltpu.SemaphoreType.DMA((n_peer,)),
                      pltpu.SemaphoreType.DMA(())]
        out_specs += [_SEM, _SEM, _SEM]
    out_shape += [pltpu.HBM(a.shape, a.dtype) for a in arrs]
    out_shape += [pltpu.HBM(s, a.dtype) for s, a in zip(land_shapes, arrs)]
    out_shape += [jax.ShapeDtypeStruct((8, LANES), F32)]
    out_specs += [_HBM] * (2 * n) + [pl.BlockSpec(memory_space=pltpu.VMEM)]
    aliases = {}
    for a in range(n):
        aliases[a] = 3 * n + a
        aliases[n + a] = 4 * n + a
    operands = [pltpu.with_memory_space_constraint(a, pltpu.HBM) for a in arrs]
    operands += [pltpu.with_memory_space_constraint(lax.empty(s, a.dtype), pltpu.HBM)
                 for s, a in zip(land_shapes, arrs)]
    outs = _pcall(
        body, name=name, out_shape=tuple(out_shape),
        in_specs=[_HBM] * (2 * n) + [pl.BlockSpec(memory_space=pl.ANY)] * n_after, out_specs=tuple(out_specs),
        input_output_aliases=aliases,
        compiler_params=pltpu.CompilerParams(has_side_effects=_DATAFLOW),
    )(*operands, *after)
    handles = []
    for a in range(n):
        handles.append((outs[3 * n + a], outs[4 * n + a], outs[3 * a], outs[3 * a + 1], outs[3 * a + 2],
                        peer_ks[a]))
    return handles, outs[-1]


def _xfer_wait(handles, after, name, gather):
    n = len(handles)
    after = tuple(after) if isinstance(after, (tuple, list)) else (after,)
    peer_ks = [h[5] for h in handles]

    def body(*refs):
        me = _my_index()
        peers = _peers()
        for a in range(n):
            src_ref, land_ref, send_ref, recv_ref, loc_ref = refs[5 * a:5 * a + 5]
            src_own = src_ref if gather else src_ref.at[me]
            pltpu.make_async_copy(src_own, land_ref.at[me], loc_ref).wait()
            for k in peer_ks[a]:
                peer, pidx = peers[k]
                src = src_ref if gather else src_ref.at[pidx]
                cp = pltpu.make_async_remote_copy(
                    src_ref=src, dst_ref=land_ref.at[pidx], send_sem=send_ref.at[k], recv_sem=recv_ref.at[k],
                    device_id=peer, device_id_type=MESH)
                cp.wait_send()
                cp.wait_recv()

    operands, in_specs, out_shape, aliases = [], [], [], {}
    for a, h in enumerate(handles):
        operands += list(h[:5])
        in_specs += [_HBM, _HBM, _SEM, _SEM, _SEM]
        out_shape += [pltpu.HBM(h[0].shape, h[0].dtype), pltpu.HBM(h[1].shape, h[1].dtype)]
        aliases[5 * a] = 2 * a
        aliases[5 * a + 1] = 2 * a + 1
    outs = _pcall(
        body, name=name, out_shape=tuple(out_shape),
        in_specs=in_specs + [pl.BlockSpec(memory_space=pl.ANY)] * len(after),
        out_specs=tuple([_HBM] * (2 * n)), input_output_aliases=aliases,
        compiler_params=pltpu.CompilerParams(has_side_effects=_DATAFLOW),
    )(*operands, *after)
    return [outs[2 * a + 1] for a in range(n)]


def _sibling_forward(lands, name):
    n = len(lands)
    n_fwd = len(_OTHER_CHIPS)

    def body(*refs):
        ins, bufs = refs[:n], refs[n:2 * n]
        send_sems, recv_sems = refs[2 * n:]
        x, y, c = lax.axis_index("x"), lax.axis_index("y"), lax.axis_index("c")
        sibling = (x, y, 1 - c)
        peers = _peers()
        sends = []
        for a in range(n):
            for j, k in enumerate(_OTHER_CHIPS):
                slot = peers[k][1]
                cp = pltpu.make_async_remote_copy(
                    src_ref=ins[a].at[slot], dst_ref=bufs[a].at[slot],
                    send_sem=send_sems.at[a * n_fwd + j], recv_sem=recv_sems.at[a * n_fwd + j],
                    device_id=sibling, device_id_type=MESH)
                cp.start()
                sends.append(cp)
        for a in range(n):
            for j, k in enumerate(_OTHER_CHIPS):
                (px, py, pc), slot = peers[k]
                theirs = 4 * px + 2 * py + (1 - pc)
                pltpu.make_async_remote_copy(
                    src_ref=ins[a].at[slot], dst_ref=bufs[a].at[theirs],
                    send_sem=send_sems.at[a * n_fwd + j], recv_sem=recv_sems.at[a * n_fwd + j],
                    device_id=sibling, device_id_type=MESH).wait_recv()
        for cp in sends:
            cp.wait_send()

    any_spec = pl.BlockSpec(memory_space=pl.ANY)
    outs = _pcall(
        body, name=name, out_shape=[jax.ShapeDtypeStruct(a.shape, a.dtype) for a in lands],
        in_specs=[any_spec] * n, out_specs=[any_spec] * n,
        input_output_aliases={a: a for a in range(n)},
        scratch_shapes=[pltpu.SemaphoreType.DMA((n * n_fwd,)), pltpu.SemaphoreType.DMA((n * n_fwd,))],
        compiler_params=pltpu.CompilerParams(has_side_effects=True),
    )(*lands)
    return list(outs)


_DIMS = {"nn": (((1,), (0,)), ((), ())), "nt": (((1,), (1,)), ((), ())), "tn": (((0,), (0,)), ((), ()))}


def _dot(a, b, mode="nn"):
    return lax.dot_general(a, b, _DIMS[mode], preferred_element_type=F32)


def _mm(a, b, *, mode, grid, a_spec, b_spec, out_shape, out_specs, acc_shape, epilogue, name,
        extra=(), extra_specs=(), after=(), semantics=("parallel", "parallel", "arbitrary")):
    nk = grid[2]
    n_extra = len(extra)
    n_in = 2 + n_extra + len(after)

    def body_single(*refs):
        a_ref, b_ref = refs[0], refs[1]
        epilogue(_dot(a_ref[...], b_ref[...], mode), refs[2:2 + n_extra], refs[n_in:])

    def body_acc(*refs):
        a_ref, b_ref = refs[0], refs[1]
        ex = refs[2:2 + n_extra]
        outs = refs[n_in:-1]
        acc = refs[-1]
        k = pl.program_id(2)

        @pl.when(k == 0)
        def _():
            acc[...] = jnp.zeros_like(acc)

        acc[...] += _dot(a_ref[...], b_ref[...], mode)

        @pl.when(k == nk - 1)
        def _():
            epilogue(acc[...], ex, outs)

    return _pcall(
        body_single if nk == 1 else body_acc, name=name, grid=grid, out_shape=out_shape,
        in_specs=[a_spec, b_spec] + list(extra_specs) + [pl.BlockSpec(memory_space=pl.ANY)] * len(after),
        out_specs=out_specs,
        scratch_shapes=[] if nk == 1 else [pltpu.VMEM(acc_shape, F32)],
        compiler_params=_cparams(semantics),
    )(a, b, *extra, *after)


def _ep_store(dtype):
    def ep(acc, ex, outs):
        outs[0][...] = acc.astype(dtype)
    return ep


def _ep_relu2(acc, ex, outs):
    outs[0][...] = acc.astype(BF16)
    r = jnp.maximum(acc, 0.0)
    outs[1][...] = (r * r).astype(BF16)


def _ep_relu2_bwd(acc, ex, outs):
    u = ex[0][...].astype(F32)
    outs[0][...] = (acc * (2.0 * jnp.maximum(u, 0.0))).astype(BF16)


def _tile(n, want):
    t = min(n, want)
    while n % t:
        t //= 2
    return t


def _mm_nn(a, w, out_dtype, name, tm=1024, tn=1024, tk=1024, epilogue=None, out_dtypes=None, after=()):
    M, K = a.shape
    N = w.shape[1]
    tm, tn, tk = _tile(M, tm), _tile(N, tn), _tile(K, tk)
    out_dtypes = out_dtypes or [out_dtype]
    return _mm(a, w, mode="nn", grid=(M // tm, N // tn, K // tk),
               a_spec=pl.BlockSpec((tm, tk), lambda i, j, k: (i, k)),
               b_spec=pl.BlockSpec((tk, tn), lambda i, j, k: (k, j)),
               out_shape=[jax.ShapeDtypeStruct((M, N), d) for d in out_dtypes],
               out_specs=[pl.BlockSpec((tm, tn), lambda i, j, k: (i, j)) for _ in out_dtypes],
               acc_shape=(tm, tn), epilogue=epilogue or _ep_store(out_dtype), name=name, after=after)


def _mm_nn_blocked(a, wg, name, epilogue, out_dtypes, tm=2048):
    M, K = a.shape
    n = wg.shape[2]
    tm = _tile(M, tm)
    return _mm(a, wg, mode="nn", grid=(M // tm, N_DEV, 1),
               a_spec=pl.BlockSpec((tm, K), lambda i, j, k: (i, 0)),
               b_spec=pl.BlockSpec((None, K, n), lambda i, j, k: (j, 0, 0)),
               out_shape=[jax.ShapeDtypeStruct((M, N_DEV * n), d) for d in out_dtypes],
               out_specs=[pl.BlockSpec((tm, n), lambda i, j, k: (i, j)) for _ in out_dtypes],
               acc_shape=(tm, n), epilogue=epilogue, name=name)


def _mm_nt(a, w, out_dtype, name, tm=1024, tn=1024, tk=1024, epilogue=None, extra=(), extra_specs=(),
           after=()):
    M, K = a.shape
    N = w.shape[0]
    tm, tn, tk = _tile(M, tm), _tile(N, tn), _tile(K, tk)
    if extra and not extra_specs:
        extra_specs = [pl.BlockSpec((tm, tn), lambda i, j, k: (i, j)) for _ in extra]
    return _mm(a, w, mode="nt", grid=(M // tm, N // tn, K // tk),
               a_spec=pl.BlockSpec((tm, tk), lambda i, j, k: (i, k)),
               b_spec=pl.BlockSpec((tn, tk), lambda i, j, k: (j, k)),
               out_shape=[jax.ShapeDtypeStruct((M, N), out_dtype)],
               out_specs=[pl.BlockSpec((tm, tn), lambda i, j, k: (i, j))],
               acc_shape=(tm, tn), epilogue=epilogue or _ep_store(out_dtype), name=name,
               extra=extra, extra_specs=extra_specs, after=after)[0]


def _mm_nt_blocked(a, wg, out_dtype, name, tm=1024, after=()):
    M = a.shape[0]
    kout, n = wg.shape[1], wg.shape[2]
    tm = _tile(M, tm)
    return _mm(a, wg, mode="nt", grid=(M // tm, 1, N_DEV),
               a_spec=pl.BlockSpec((tm, n), lambda i, j, k: (i, k)),
               b_spec=pl.BlockSpec((None, kout, n), lambda i, j, k: (k, 0, 0)),
               out_shape=[jax.ShapeDtypeStruct((M, kout), out_dtype)],
               out_specs=[pl.BlockSpec((tm, kout), lambda i, j, k: (i, 0))],
               acc_shape=(tm, kout), epilogue=_ep_store(out_dtype), name=name, after=after)[0]


def _mm_tn(a, b, out_dtype, name, tm=1024, tn=1024, tk=1024):
    K, M = a.shape
    N = b.shape[1]
    tm, tn, tk = _tile(M, tm), _tile(N, tn), _tile(K, tk)
    return _mm(a, b, mode="tn", grid=(M // tm, N // tn, K // tk),
               a_spec=pl.BlockSpec((tk, tm), lambda i, j, k: (k, i)),
               b_spec=pl.BlockSpec((tk, tn), lambda i, j, k: (k, j)),
               out_shape=[jax.ShapeDtypeStruct((M, N), out_dtype)],
               out_specs=[pl.BlockSpec((tm, tn), lambda i, j, k: (i, j))],
               acc_shape=(tm, tn), epilogue=_ep_store(out_dtype), name=name)[0]


def _mm_tn_blocked(a, b, out_dtype, name, tm=1024, tk=2048):
    K, M = a.shape
    n = b.shape[1] // N_DEV
    tm, tk = _tile(M, tm), _tile(K, tk)
    return _mm(a, b, mode="tn", grid=(M // tm, N_DEV, K // tk),
               a_spec=pl.BlockSpec((tk, tm), lambda i, j, k: (k, i)),
               b_spec=pl.BlockSpec((tk, n), lambda i, j, k: (k, j)),
               out_shape=[jax.ShapeDtypeStruct((N_DEV, M, n), out_dtype)],
               out_specs=[pl.BlockSpec((None, tm, n), lambda i, j, k: (j, i, 0))],
               acc_shape=(tm, n), epilogue=_ep_store(out_dtype), name=name)[0]


def _window_geometry(ws):
    base = [(ws * k // LANES) * LANES for k in range(N_DEV)]
    off = [ws * k - base[k] for k in range(N_DEV)]
    win = -(-(max(off) + ws) // LANES) * LANES
    return base, off, win


def _shards_to_columns(xg, base, off, win, n_out, name, tr=256):
    R, ws = xg.shape[1], xg.shape[2]
    tr = _tile(R, tr)
    nb_win = win // LANES

    def body(x_ref, o_ref, frame_ref):
        written = set()
        frame_ref[...] = jnp.zeros_like(frame_ref)
        for k in range(N_DEV):
            frame_ref[:, 0:ws] = x_ref[k].astype(F32)
            window = frame_ref[...]
            if off[k]:
                window = pltpu.roll(window, off[k], 1)
            for i in range(nb_win):
                b = base[k] // LANES + i
                if b * LANES >= n_out:
                    continue
                cols = slice(b * LANES, (b + 1) * LANES)
                blk = window[:, i * LANES:(i + 1) * LANES]
                if b in written:
                    blk = blk + o_ref[:, cols].astype(F32)
                o_ref[:, cols] = blk.astype(o_ref.dtype)
                written.add(b)
        for b in range(n_out // LANES):
            if b not in written:
                o_ref[:, b * LANES:(b + 1) * LANES] = jnp.zeros((tr, LANES), o_ref.dtype)

    return _pcall(
        body, name=name, grid=(R // tr,), out_shape=jax.ShapeDtypeStruct((R, n_out), xg.dtype),
        in_specs=[pl.BlockSpec((N_DEV, tr, ws), lambda i: (0, i, 0))],
        out_specs=pl.BlockSpec((tr, n_out), lambda i: (i, 0)),
        scratch_shapes=[pltpu.VMEM((tr, win), F32)],
        compiler_params=_cparams(("parallel",)))(xg)


def _sigmoid(x):
    return 1.0 / (1.0 + jnp.exp(-x))


def _row_spec(tm, d):
    return pl.BlockSpec((tm, d), lambda i: (i, 0))


def _vec_spec(d):
    return pl.BlockSpec((1, d), lambda i: (0, 0))


def _norm_mod_fwd(x, y, gate, nw, scale, shift, name, tm=512):
    L, D = x.shape
    tm = _tile(L, tm)
    has_res = y is not None

    def body(*refs):
        if has_res:
            x_ref, y_ref, g_ref, nw_ref, sc_ref, sh_ref, xo_ref, h_ref = refs
            xn = x_ref[...] + g_ref[...] * y_ref[...]
            xo_ref[...] = xn
        else:
            x_ref, nw_ref, sc_ref, sh_ref, h_ref = refs
            xn = x_ref[...]
        rstd = lax.rsqrt(jnp.mean(xn * xn, axis=-1, keepdims=True) + NORM_EPS)
        h = xn * rstd * nw_ref[...] * (1.0 + sc_ref[...]) + sh_ref[...]
        h_ref[...] = h.astype(BF16)

    row, vec = _row_spec(tm, D), _vec_spec(D)
    if has_res:
        ins, in_specs = (x, y, gate, nw, scale, shift), [row, row, vec, vec, vec, vec]
        out_shape = [jax.ShapeDtypeStruct((L, D), F32), jax.ShapeDtypeStruct((L, D), BF16)]
        out_specs = [row, row]
    else:
        ins, in_specs = (x, nw, scale, shift), [row, vec, vec, vec]
        out_shape = [jax.ShapeDtypeStruct((L, D), BF16)]
        out_specs = [row]
    outs = _pcall(body, name=name, grid=(L // tm,), out_shape=out_shape, in_specs=in_specs,
                  out_specs=out_specs, compiler_params=_cparams(("parallel",)))(*ins)
    return outs if has_res else (x, outs[0])


def _gated_branch_bwd(dx, branch, y_ref, g_ref, dy_ref, dg_ref):
    if branch is None:
        return
    dy_ref[...] = (g_ref[...] * dx).astype(BF16)
    dg_ref[...] += jnp.sum(dx * y_ref[...], axis=0, keepdims=True)


def _norm_mod_bwd(dh, x, nw, scale, dres, name, branch=None, tm=512):
    L, D = x.shape
    tm = _tile(L, tm)
    nb = 0 if branch is None else 2

    def body(dh_ref, x_ref, nw_ref, sc_ref, dres_ref, *rest):
        y_ref, g_ref = rest[:nb] if nb else (None, None)
        dx_ref, dsh_ref, dsc_ref, dnw_ref = rest[nb:nb + 4]
        dy_ref, dg_ref = rest[nb + 4:] if nb else (None, None)

        @pl.when(pl.program_id(0) == 0)
        def _():
            dsh_ref[...] = jnp.zeros_like(dsh_ref)
            dsc_ref[...] = jnp.zeros_like(dsc_ref)
            dnw_ref[...] = jnp.zeros_like(dnw_ref)
            if nb:
                dg_ref[...] = jnp.zeros_like(dg_ref)

        xv = x_ref[...]
        dh_v = dh_ref[...]
        nw_v = nw_ref[...]
        rstd = lax.rsqrt(jnp.mean(xv * xv, axis=-1, keepdims=True) + NORM_EPS)
        xhat = xv * rstd
        dsh_ref[...] += jnp.sum(dh_v, axis=0, keepdims=True)
        dsc_ref[...] += jnp.sum(dh_v * (xhat * nw_v), axis=0, keepdims=True)
        dr = dh_v * (1.0 + sc_ref[...])
        dnw_ref[...] += jnp.sum(dr * xhat, axis=0, keepdims=True)
        dxh = dr * nw_v
        dx = rstd * (dxh - xhat * jnp.mean(dxh * xhat, axis=-1, keepdims=True)) + dres_ref[...]
        dx_ref[...] = dx
        _gated_branch_bwd(dx, branch, y_ref, g_ref, dy_ref, dg_ref)

    row, vec = _row_spec(tm, D), _vec_spec(D)
    extra_in = [] if branch is None else list(branch)
    return _pcall(
        body, name=name, grid=(L // tm,),
        out_shape=[jax.ShapeDtypeStruct((L, D), F32)] + [jax.ShapeDtypeStruct((1, D), F32)] * 3
        + ([jax.ShapeDtypeStruct((L, D), BF16), jax.ShapeDtypeStruct((1, D), F32)] if nb else []),
        in_specs=[row, row, vec, vec, row] + ([row, vec] if nb else []),
        out_specs=[row, vec, vec, vec] + ([row, vec] if nb else []),
        compiler_params=_cparams(("arbitrary",)))(dh, x, nw, scale, dres, *extra_in)


def _final_loss(x, y, gate, fw, target, name, tm=512):
    L, D = x.shape
    tm = _tile(L, tm)

    def body(x_ref, y_ref, g_ref, fw_ref, t_ref, dx_ref, loss_ref, dfw_ref, dy_ref, dg_ref):
        @pl.when(pl.program_id(0) == 0)
        def _():
            loss_ref[...] = jnp.zeros_like(loss_ref)
            dfw_ref[...] = jnp.zeros_like(dfw_ref)
            dg_ref[...] = jnp.zeros_like(dg_ref)

        xn = x_ref[...] + g_ref[...] * y_ref[...]
        fw_v = fw_ref[...]
        rstd = lax.rsqrt(jnp.mean(xn * xn, axis=-1, keepdims=True) + NORM_EPS)
        xhat = xn * rstd
        diff = xhat * fw_v - t_ref[...]
        loss_ref[...] += jnp.sum(diff * diff, axis=0, keepdims=True)
        dyf = diff * (1.0 / D)
        dfw_ref[...] += jnp.sum(dyf * xhat, axis=0, keepdims=True)
        dxh = dyf * fw_v
        dx = rstd * (dxh - xhat * jnp.mean(dxh * xhat, axis=-1, keepdims=True))
        dx_ref[...] = dx
        _gated_branch_bwd(dx, True, y_ref, g_ref, dy_ref, dg_ref)

    row, vec = _row_spec(tm, D), _vec_spec(D)
    return _pcall(
        body, name=name, grid=(L // tm,),
        out_shape=[jax.ShapeDtypeStruct((L, D), F32), jax.ShapeDtypeStruct((1, D), F32),
                   jax.ShapeDtypeStruct((1, D), F32), jax.ShapeDtypeStruct((L, D), BF16),
                   jax.ShapeDtypeStruct((1, D), F32)],
        in_specs=[row, row, vec, vec, row], out_specs=[row, vec, vec, row, vec],
        compiler_params=_cparams(("arbitrary",)))(x, y, gate, fw, target)


def _mm_nt_norm_bwd(a, w, x, nw, scale, dres, name, branch=None, blocked=False, tm=512, tk=1024, after=()):
    M = a.shape[0]
    D = x.shape[1]
    tm = _tile(M, tm)
    nb = 0 if branch is None else 2

    def epilogue(dh_v, ex, outs):
        x_ref, nw_ref, sc_ref, dres_ref = ex[:4]
        y_ref, g_ref = ex[4:] if nb else (None, None)
        dx_ref, dsh_ref, dsc_ref, dnw_ref = outs[:4]
        dy_ref, dg_ref = outs[4:] if nb else (None, None)

        @pl.when(pl.program_id(0) == 0)
        def _():
            dsh_ref[...] = jnp.zeros_like(dsh_ref)
            dsc_ref[...] = jnp.zeros_like(dsc_ref)
            dnw_ref[...] = jnp.zeros_like(dnw_ref)
            if nb:
                dg_ref[...] = jnp.zeros_like(dg_ref)

        xv = x_ref[...]
        nw_v = nw_ref[...]
        rstd = lax.rsqrt(jnp.mean(xv * xv, axis=-1, keepdims=True) + NORM_EPS)
        xhat = xv * rstd
        dsh_ref[...] += jnp.sum(dh_v, axis=0, keepdims=True)
        dsc_ref[...] += jnp.sum(dh_v * (xhat * nw_v), axis=0, keepdims=True)
        dr = dh_v * (1.0 + sc_ref[...])
        dnw_ref[...] += jnp.sum(dr * xhat, axis=0, keepdims=True)
        dxh = dr * nw_v
        dx = rstd * (dxh - xhat * jnp.mean(dxh * xhat, axis=-1, keepdims=True)) + dres_ref[...]
        dx_ref[...] = dx
        _gated_branch_bwd(dx, branch, y_ref, g_ref, dy_ref, dg_ref)

    row = pl.BlockSpec((tm, D), lambda i, j, k: (i, 0))
    vec = pl.BlockSpec((1, D), lambda i, j, k: (0, 0))
    if blocked:
        n = w.shape[2]
        grid = (M // tm, 1, N_DEV)
        a_spec = pl.BlockSpec((tm, n), lambda i, j, k: (i, k))
        b_spec = pl.BlockSpec((None, D, n), lambda i, j, k: (k, 0, 0))
    else:
        K = a.shape[1]
        tk = _tile(K, tk)
        grid = (M // tm, 1, K // tk)
        a_spec = pl.BlockSpec((tm, tk), lambda i, j, k: (i, k))
        b_spec = pl.BlockSpec((D, tk), lambda i, j, k: (0, k))
    return _mm(a, w, mode="nt", grid=grid, a_spec=a_spec, b_spec=b_spec,
               out_shape=[jax.ShapeDtypeStruct((M, D), F32)] + [jax.ShapeDtypeStruct((1, D), F32)] * 3
               + ([jax.ShapeDtypeStruct((M, D), BF16), jax.ShapeDtypeStruct((1, D), F32)] if nb else []),
               out_specs=[row, vec, vec, vec] + ([row, vec] if nb else []),
               acc_shape=(tm, D), epilogue=epilogue, name=name,
               extra=(x, nw, scale, dres) + (tuple(branch) if nb else ()),
               extra_specs=[row, vec, vec, row] + ([row, vec] if nb else []), after=after,
               semantics=("arbitrary", "arbitrary", "arbitrary"))


def _shift_down(v, s, row):
    if s == 0:
        return v
    return jnp.where(row >= s, pltpu.roll(v, s, 0), 0.0)


def _shift_up(v, s, row):
    if s == 0:
        return v
    n = v.shape[0]
    return jnp.where(row < n - s, pltpu.roll(v, n - s, 0), 0.0)


def _ssd_conv_fwd(zx, w, b, col0, width, name, cb=128):
    L = zx.shape[0]
    nb = width // cb
    off = col0 // cb

    def body(x_ref, w_ref, b_ref, o_ref):
        xv = x_ref[...]
        row = lax.broadcasted_iota(jnp.int32, xv.shape, 0)
        acc = b_ref[...] + w_ref[SSD_K - 1:SSD_K, :] * xv
        for s in range(1, SSD_K):
            acc = acc + w_ref[SSD_K - 1 - s:SSD_K - s, :] * _shift_down(xv, s, row)
        o_ref[...] = acc * _sigmoid(acc)

    return _pcall(
        body, name=name, grid=(nb,), out_shape=jax.ShapeDtypeStruct((L, width), F32),
        in_specs=[pl.BlockSpec((L, cb), lambda j: (0, off + j)),
                  pl.BlockSpec((SSD_K, cb), lambda j: (0, j)),
                  pl.BlockSpec((1, cb), lambda j: (0, j))],
        out_specs=pl.BlockSpec((L, cb), lambda j: (0, j)),
        compiler_params=_cparams(("parallel",)))(zx, w, b)


def _ssd_conv_bwd(zx, w, b, d_parts, dzx, col0, name, cb=128):
    L = zx.shape[0]
    widths = [p.shape[1] for p in d_parts]
    width = sum(widths)
    nb = width // cb
    off = col0 // cb
    starts = [sum(widths[:i]) // cb for i in range(len(d_parts))]
    counts = [wd // cb for wd in widths]

    def body(x_ref, w_ref, b_ref, *rest):
        d_refs = rest[:len(d_parts)]
        dx_ref, dw_ref, db_ref = rest[len(d_parts) + 1:]
        j = pl.program_id(0)
        d_val = d_refs[-1][...]
        for i in range(len(d_parts) - 2, -1, -1):
            d_val = jnp.where(j < starts[i + 1], d_refs[i][...], d_val)
        xv = x_ref[...]
        row = lax.broadcasted_iota(jnp.int32, xv.shape, 0)
        shifted = [_shift_down(xv, s, row) for s in range(SSD_K)]
        acc = b_ref[...] + w_ref[SSD_K - 1:SSD_K, :] * xv
        for s in range(1, SSD_K):
            acc = acc + w_ref[SSD_K - 1 - s:SSD_K - s, :] * shifted[s]
        sig = _sigmoid(acc)
        dpre = d_val * (sig * (1.0 + acc * (1.0 - sig)))
        db_ref[...] = jnp.sum(dpre, axis=0, keepdims=True)
        dx = w_ref[SSD_K - 1:SSD_K, :] * dpre
        for s in range(SSD_K):
            dw_ref[SSD_K - 1 - s:SSD_K - s, :] = jnp.sum(dpre * shifted[s], axis=0, keepdims=True)
            if s:
                dx = dx + w_ref[SSD_K - 1 - s:SSD_K - s, :] * _shift_up(dpre, s, row)
        dx_ref[...] = dx.astype(BF16)

    def part_spec(i):
        return pl.BlockSpec((L, cb), lambda j: (0, jnp.clip(j - starts[i], 0, counts[i] - 1)))

    return _pcall(
        body, name=name, grid=(nb,),
        out_shape=[jax.ShapeDtypeStruct(dzx.shape, BF16), jax.ShapeDtypeStruct((SSD_K, width), F32),
                   jax.ShapeDtypeStruct((1, width), F32)],
        in_specs=[pl.BlockSpec((L, cb), lambda j: (0, off + j)),
                  pl.BlockSpec((SSD_K, cb), lambda j: (0, j)),
                  pl.BlockSpec((1, cb), lambda j: (0, j))]
        + [part_spec(i) for i in range(len(d_parts))] + [pl.BlockSpec(memory_space=pl.ANY)],
        out_specs=[pl.BlockSpec((L, cb), lambda j: (0, off + j)),
                   pl.BlockSpec((SSD_K, cb), lambda j: (0, j)),
                   pl.BlockSpec((1, cb), lambda j: (0, j))],
        input_output_aliases={3 + len(d_parts): 0},
        compiler_params=_cparams(("parallel",)))(zx, w, b, *d_parts, dzx)


def _dzx_finish(dzx, ddt, col0, name, tl=512):
    G, L, _ = ddt.shape
    tail = dzx.shape[1] - col0
    tl = _tile(L, tl)

    def body(ddt_ref, dzx_ref, o_ref):
        s = ddt_ref[0]
        for g in range(1, G):
            s = s + ddt_ref[g]
        o_ref[:, 0:LANES] = s.astype(o_ref.dtype)
        if tail > LANES:
            o_ref[:, LANES:] = jnp.zeros((tl, tail - LANES), o_ref.dtype)

    return _pcall(
        body, name=name, grid=(L // tl,), out_shape=jax.ShapeDtypeStruct(dzx.shape, dzx.dtype),
        in_specs=[pl.BlockSpec((G, tl, LANES), lambda i: (0, i, 0)), pl.BlockSpec(memory_space=pl.ANY)],
        out_specs=pl.BlockSpec((tl, tail), lambda i: (i, col0 // tail)),
        input_output_aliases={1: 0},
        compiler_params=_cparams(("parallel",)))(ddt, dzx)


def _sc_conv_fwd(proj, w, name, cb=128):
    L = proj.shape[0]
    width = proj.shape[1] // 3
    nb = width // cb

    def body(b_ref, c_ref, x_ref, w_ref, o_ref):
        q = c_ref[...] * x_ref[...]
        row = lax.broadcasted_iota(jnp.int32, q.shape, 0)
        acc = w_ref[SC_K - 1:SC_K, :] * q
        for s in range(1, SC_K):
            acc = acc + w_ref[SC_K - 1 - s:SC_K - s, :] * _shift_down(q, s, row)
        o_ref[...] = (b_ref[...] * acc).astype(BF16)

    return _pcall(
        body, name=name, grid=(nb,), out_shape=jax.ShapeDtypeStruct((L, width), BF16),
        in_specs=[pl.BlockSpec((L, cb), lambda j: (0, j)),
                  pl.BlockSpec((L, cb), lambda j: (0, nb + j)),
                  pl.BlockSpec((L, cb), lambda j: (0, 2 * nb + j)),
                  pl.BlockSpec((SC_K, cb), lambda j: (0, j))],
        out_specs=pl.BlockSpec((L, cb), lambda j: (0, j)),
        compiler_params=_cparams(("parallel",)))(proj, proj, proj, w)


def _sc_conv_bwd(proj, w, dy, name, cb=128):
    L = proj.shape[0]
    width = proj.shape[1] // 3
    nb = width // cb

    def body(b_ref, c_ref, x_ref, w_ref, dy_ref, db_ref, dc_ref, dxv_ref, dw_ref):
        cg, xv, dyv = c_ref[...], x_ref[...], dy_ref[...]
        q = cg * xv
        row = lax.broadcasted_iota(jnp.int32, q.shape, 0)
        shifted = [_shift_down(q, s, row) for s in range(SC_K)]
        conv = w_ref[SC_K - 1:SC_K, :] * q
        for s in range(1, SC_K):
            conv = conv + w_ref[SC_K - 1 - s:SC_K - s, :] * shifted[s]
        db_ref[...] = (dyv * conv).astype(BF16)
        dconv = dyv * b_ref[...]
        dq = w_ref[SC_K - 1:SC_K, :] * dconv
        for s in range(SC_K):
            dw_ref[SC_K - 1 - s:SC_K - s, :] = jnp.sum(dconv * shifted[s], axis=0, keepdims=True)
            if s:
                dq = dq + w_ref[SC_K - 1 - s:SC_K - s, :] * _shift_up(dconv, s, row)
        dc_ref[...] = (dq * xv).astype(BF16)
        dxv_ref[...] = (dq * cg).astype(BF16)

    blk = pl.BlockSpec((L, cb), lambda j: (0, j))
    wblk = pl.BlockSpec((SC_K, cb), lambda j: (0, j))
    return _pcall(
        body, name=name, grid=(nb,),
        out_shape=[jax.ShapeDtypeStruct((L, width), BF16)] * 3 + [jax.ShapeDtypeStruct((SC_K, width), F32)],
        in_specs=[blk, pl.BlockSpec((L, cb), lambda j: (0, nb + j)),
                  pl.BlockSpec((L, cb), lambda j: (0, 2 * nb + j)), wblk, blk],
        out_specs=[blk, blk, blk, wblk],
        compiler_params=_cparams(("parallel",)))(proj, proj, proj, w, dy)


def _split3(v):
    hi = v.astype(BF16)
    r1 = v - hi.astype(F32)
    mid = r1.astype(BF16)
    lo = (r1 - mid.astype(F32)).astype(BF16)
    return hi, mid, lo


def _dot_exact01(t01, v):
    hi, mid, lo = _split3(v)
    return _dot(t01, hi) + _dot(t01, mid) + _dot(t01, lo)


def _lane_col(v, lane, h):
    return jnp.sum(jnp.where(lane == h, v, 0.0), axis=1, keepdims=True)


def _sum_all(v):
    return jnp.sum(jnp.sum(v, axis=1, keepdims=True), axis=0, keepdims=True)


def _softplus(x):
    return jnp.maximum(x, 0.0) + jnp.log1p(jnp.exp(-jnp.abs(x)))


def _ssd_common(dt_ref, bias_ref, alog_ref, b_ref, c_ref, cst_ref, heads):
    c_sz = SSD_CHUNK
    lane = lax.broadcasted_iota(jnp.int32, (c_sz, LANES), 1)
    row = lax.broadcasted_iota(jnp.int32, (c_sz, LANES), 0)
    valid = lane < heads
    raw = dt_ref[...] + bias_ref[...]
    dt = _softplus(raw)
    a_row = -jnp.exp(alog_ref[...])
    a = jnp.where(valid, dt * a_row, 0.0)
    tri = (row >= lane).astype(BF16)
    cs = _dot_exact01(tri, a)
    cst_ref[...] = cs.T
    last_row = jnp.sum(a, axis=0, keepdims=True)
    bb = b_ref[...].astype(BF16)
    cb = c_ref[...].astype(BF16)
    scores = _dot(cb, bb, "nt")
    return dict(lane=lane, row=row, valid=valid, raw=raw, dt=dt, a_row=a_row, cs=cs,
                last_row=last_row, bb=bb, cb=cb, scores=scores, causal=row >= lane, lo=lane < SSD_P)


def _pair_terms(q, cst_ref, h0):
    lane, lo = q["lane"], q["lo"]
    out = {}
    cols, dts, lasts, lms = [], [], [], []
    lane1 = lax.broadcasted_iota(jnp.int32, (1, LANES), 1)
    for h in (h0, h0 + 1):
        col = _lane_col(q["cs"], lane, h)
        rowv = cst_ref[pl.ds(h, 1), :]
        lms.append(jnp.exp(jnp.where(q["causal"], col - rowv, -1e30)))
        cols.append(col)
        dts.append(_lane_col(q["dt"], lane, h))
        lasts.append(jnp.sum(jnp.where(lane1 == h, q["last_row"], 0.0), axis=1, keepdims=True))
    out["lm"] = lms
    out["cols"] = cols
    out["lasts"] = lasts
    out["dt_b"] = jnp.where(lo, dts[0], dts[1])
    out["e_b"] = jnp.where(lo, jnp.exp(cols[0]), jnp.exp(cols[1]))
    out["dec_cols"] = [jnp.exp(lasts[0] - cols[0]), jnp.exp(lasts[1] - cols[1])]
    out["dec_b"] = jnp.where(lo, out["dec_cols"][0], out["dec_cols"][1])
    lo1 = lane1 < SSD_P
    out["explast"] = [jnp.exp(lasts[0]), jnp.exp(lasts[1])]
    out["explast_b"] = jnp.where(lo1, out["explast"][0], out["explast"][1])
    return out


def _ssd_fwd(zx, xc, bias_p, alog_p, d_lane, nw, d_inner, after, name):
    L = zx.shape[0]
    nc = L // SSD_CHUNK
    gw = d_inner // SSD_G
    heads = gw // SSD_P
    n_pair = heads // 2
    bc0 = d_inner // LANES
    dt0 = (2 * d_inner + 2 * SSD_G * SSD_N) // LANES

    def body(z_ref, xs_ref, b_ref, c_ref, dt_ref, bias_ref, alog_ref, dl_ref, nw_ref, after_ref,
             y_ref, yn_ref, prev_ref, s_ref, cst_ref):
        @pl.when(pl.program_id(1) == 0)
        def _():
            s_ref[...] = jnp.zeros_like(s_ref)

        q = _ssd_common(dt_ref, bias_ref, alog_ref, b_ref, c_ref, cst_ref, SSD_G * heads)
        prev_ref[...] = s_ref[...]
        lo = q["lo"]
        for j in range(n_pair):
            sl = slice(j * LANES, (j + 1) * LANES)
            p = _pair_terms(q, cst_ref, pl.program_id(0) * heads + 2 * j)
            xs_p = xs_ref[:, sl]
            xp = xs_p * p["dt_b"]
            xb = xp.astype(BF16)
            m_a = (q["scores"] * p["lm"][0]).astype(BF16)
            m_b = (q["scores"] * p["lm"][1]).astype(BF16)
            yd = jnp.where(lo, _dot(m_a, xb), _dot(m_b, xb))
            s_p = s_ref[:, sl]
            yo = _dot(q["cb"], s_p.astype(BF16)) * p["e_b"]
            y_ref[:, sl] = yd + yo + dl_ref[:, sl] * xs_p
            st = _dot(q["bb"], (xp * p["dec_b"]).astype(BF16), "tn")
            s_ref[:, sl] = s_p * p["explast_b"] + st
        yv = y_ref[...]
        zv = z_ref[...]
        yg = yv * (zv * _sigmoid(zv))
        rstd = lax.rsqrt(jnp.mean(yg * yg, axis=-1, keepdims=True) + NORM_EPS)
        yn_ref[...] = (yg * rstd * nw_ref[...]).astype(BF16)

    grp = lambda width: pl.BlockSpec((None, 1, width), lambda g, c: (g, 0, 0))
    head_vec = pl.BlockSpec((1, LANES), lambda g, c: (0, 0))
    return _pcall(
        body, name=name, grid=(SSD_G, nc),
        out_shape=[jax.ShapeDtypeStruct((L, d_inner), F32), jax.ShapeDtypeStruct((L, d_inner), BF16),
                   jax.ShapeDtypeStruct((nc, SSD_G, SSD_N, gw), F32)],
        in_specs=[pl.BlockSpec((SSD_CHUNK, gw), lambda g, c: (c, g)),
                  pl.BlockSpec((SSD_CHUNK, gw), lambda g, c: (c, g)),
                  pl.BlockSpec((SSD_CHUNK, SSD_N), lambda g, c: (c, bc0 + g)),
                  pl.BlockSpec((SSD_CHUNK, SSD_N), lambda g, c: (c, bc0 + SSD_G + g)),
                  pl.BlockSpec((SSD_CHUNK, LANES), lambda g, c: (c, dt0)),
                  head_vec, head_vec, grp(gw), grp(gw), pl.BlockSpec(memory_space=pl.ANY)],
        out_specs=[pl.BlockSpec((SSD_CHUNK, gw), lambda g, c: (c, g)),
                   pl.BlockSpec((SSD_CHUNK, gw), lambda g, c: (c, g)),
                   pl.BlockSpec((None, None, SSD_N, gw), lambda g, c: (c, g, 0, 0))],
        scratch_shapes=[pltpu.VMEM((SSD_N, gw), F32), pltpu.VMEM((SSD_CHUNK, LANES), F32)],
        compiler_params=_cparams(("parallel", "arbitrary")))(
            zx, xc, xc, xc, zx, bias_p, alog_p, d_lane, nw, after)


def _ssd_bwd(dyn, y, zx, xc, prev, bias_p, alog_p, d_lane, nw, d_inner, name):
    L = zx.shape[0]
    nc = L // SSD_CHUNK
    gw = d_inner // SSD_G
    heads = gw // SSD_P
    n_pair = heads // 2
    bc0 = d_inner // LANES
    dt0 = (2 * d_inner + 2 * SSD_G * SSD_N) // LANES

    def body(dyn_ref, y_ref, z_ref, xs_ref, b_ref, c_ref, dt_ref, prev_ref, bias_ref, alog_ref, dl_ref, nw_ref,
             dz_ref, dxs_ref, db_ref, dc_ref, ddt_ref, dbias_ref, dalog_ref, dd_ref, dnw_ref,
             ds_ref, cst_ref, racc_ref):
        @pl.when(pl.program_id(1) == 0)
        def _():
            ds_ref[...] = jnp.zeros_like(ds_ref)
            dbias_ref[...] = jnp.zeros_like(dbias_ref)
            dalog_ref[...] = jnp.zeros_like(dalog_ref)
            dd_ref[...] = jnp.zeros_like(dd_ref)
            dnw_ref[...] = jnp.zeros_like(dnw_ref)

        q = _ssd_common(dt_ref, bias_ref, alog_ref, b_ref, c_ref, cst_ref, SSD_G * heads)
        lane, row, lo = q["lane"], q["row"], q["lo"]
        lane1 = lax.broadcasted_iota(jnp.int32, (1, LANES), 1)
        head0 = pl.program_id(0) * heads
        mine = (lane >= head0) & (lane < head0 + heads)

        yv, zv, dynv, nwv = y_ref[...], z_ref[...], dyn_ref[...], nw_ref[...]
        sig = _sigmoid(zv)
        sz = zv * sig
        yg = yv * sz
        rstd = lax.rsqrt(jnp.mean(yg * yg, axis=-1, keepdims=True) + NORM_EPS)
        yhat = yg * rstd
        dnw_ref[...] += jnp.sum(dynv * yhat, axis=0, keepdims=True)
        dyh = dynv * nwv
        dyg = rstd * (dyh - yhat * jnp.mean(dyh * yhat, axis=-1, keepdims=True))
        dz_ref[...] = (dyg * yv * (sig * (1.0 + zv * (1.0 - sig)))).astype(BF16)
        dy_all = dyg * sz

        dg = jnp.zeros((SSD_CHUNK, SSD_CHUNK), F32)
        dc_acc = jnp.zeros((SSD_CHUNK, SSD_N), F32)
        db_acc = jnp.zeros((SSD_CHUNK, SSD_N), F32)
        dcs_mat = jnp.zeros((SSD_CHUNK, LANES), F32)
        ddt_mat = jnp.zeros((SSD_CHUNK, LANES), F32)
        dd_row = jnp.zeros((1, LANES), F32)
        racc_ref[...] = jnp.zeros_like(racc_ref)
        is_last = row == SSD_CHUNK - 1

        for j in range(n_pair):
            sl = slice(j * LANES, (j + 1) * LANES)
            ha, hb = head0 + 2 * j, head0 + 2 * j + 1
            p = _pair_terms(q, cst_ref, ha)
            xs_p = xs_ref[:, sl]
            dyp = dy_all[:, sl]
            xp = xs_p * p["dt_b"]
            xb = xp.astype(BF16)
            s_p = prev_ref[:, sl]
            s_pb = s_p.astype(BF16)
            dsn = ds_ref[:, sl]
            dsnb = dsn.astype(BF16)
            m_f = [q["scores"] * p["lm"][0], q["scores"] * p["lm"][1]]

            t0 = dyp * xs_p
            dd_row = dd_row + jnp.where(lane1 == ha, _sum_all(jnp.where(lo, t0, 0.0)), 0.0) \
                + jnp.where(lane1 == hb, _sum_all(jnp.where(lo, 0.0, t0)), 0.0)
            dxs_p = dl_ref[:, sl] * dyp

            yo = _dot(q["cb"], s_pb) * p["e_b"]
            dcs_b = (dyp * p["e_b"]).astype(BF16)
            dc_acc = dc_acc + _dot(dcs_b, s_pb, "nt")
            ds_yo = _dot(q["cb"], dcs_b, "tn")
            t1 = dyp * yo
            dcs_cols = [jnp.sum(jnp.where(lo, t1, 0.0), axis=1, keepdims=True),
                        jnp.sum(jnp.where(lo, 0.0, t1), axis=1, keepdims=True)]

            t2 = dsn * s_p
            dlast = [p["explast"][0] * _sum_all(jnp.where(lo, t2, 0.0)),
                     p["explast"][1] * _sum_all(jnp.where(lo, 0.0, t2))]
            ds_ref[:, sl] = dsn * p["explast_b"] + ds_yo
            w = _dot(q["bb"], dsnb)
            db_acc = db_acc + _dot((xp * p["dec_b"]).astype(BF16), dsnb, "nt")
            dxp = w * p["dec_b"]
            t3 = w * xp
            e = [jnp.sum(jnp.where(lo, t3, 0.0), axis=1, keepdims=True) * p["dec_cols"][0],
                 jnp.sum(jnp.where(lo, 0.0, t3), axis=1, keepdims=True) * p["dec_cols"][1]]
            for i in range(2):
                dlast[i] = dlast[i] + jnp.sum(e[i], axis=0, keepdims=True)
                dcs_cols[i] = dcs_cols[i] - e[i]

            dyb = dyp.astype(BF16)
            dy_h = [jnp.where(lo, dyp, 0.0).astype(BF16), jnp.where(lo, 0.0, dyp).astype(BF16)]
            dms = [_dot(dy_h[0], xb, "nt"), _dot(dy_h[1], xb, "nt")]
            dxp = dxp + jnp.where(lo, _dot(m_f[0].astype(BF16), dyb, "tn"), _dot(m_f[1].astype(BF16), dyb, "tn"))
            for i, h in enumerate((ha, hb)):
                dg = dg + dms[i] * p["lm"][i]
                qm = dms[i] * m_f[i]
                dcs_cols[i] = dcs_cols[i] + jnp.sum(qm, axis=1, keepdims=True)
                racc_ref[pl.ds(h, 1), :] = jnp.sum(qm, axis=0, keepdims=True)

            dxs_ref[:, sl] = dxs_p + dxp * p["dt_b"]
            t4 = dxp * xs_p
            ddt_cols = [jnp.sum(jnp.where(lo, t4, 0.0), axis=1, keepdims=True),
                        jnp.sum(jnp.where(lo, 0.0, t4), axis=1, keepdims=True)]
            for i, h in enumerate((ha, hb)):
                sel = lane == h
                dcs_mat = dcs_mat + jnp.where(sel, dcs_cols[i], 0.0) + jnp.where(sel & is_last, dlast[i], 0.0)
                ddt_mat = ddt_mat + jnp.where(sel, ddt_cols[i], 0.0)

        dcs_mat = dcs_mat - racc_ref[...].T
        tri_t = (row <= lane).astype(BF16)
        da = _dot_exact01(tri_t, dcs_mat)
        ddt = ddt_mat + da * q["a_row"]
        dalog_ref[...] += jnp.sum(jnp.where(mine, da * q["dt"], 0.0), axis=0, keepdims=True) * q["a_row"]
        draw = jnp.where(mine, ddt * _sigmoid(q["raw"]), 0.0)
        ddt_ref[...] = draw
        dbias_ref[...] += jnp.sum(draw, axis=0, keepdims=True)
        dd_ref[...] += dd_row
        dgb = dg.astype(BF16)
        dc_ref[...] = dc_acc + _dot(dgb, q["bb"])
        db_ref[...] = db_acc + _dot(dgb, q["cb"], "tn")

    rev = lambda c: nc - 1 - c
    grp = lambda width: pl.BlockSpec((None, 1, width), lambda g, c: (g, 0, 0))
    blk = lambda width, off: pl.BlockSpec((SSD_CHUNK, width), lambda g, c: (rev(c), off + g))
    head_vec = pl.BlockSpec((1, LANES), lambda g, c: (0, 0))
    return _pcall(
        body, name=name, grid=(SSD_G, nc),
        out_shape=[jax.ShapeDtypeStruct(zx.shape, BF16), jax.ShapeDtypeStruct((L, d_inner), F32),
                   jax.ShapeDtypeStruct((L, SSD_G * SSD_N), F32), jax.ShapeDtypeStruct((L, SSD_G * SSD_N), F32),
                   jax.ShapeDtypeStruct((SSD_G, L, LANES), F32),
                   jax.ShapeDtypeStruct((SSD_G, 1, LANES), F32), jax.ShapeDtypeStruct((SSD_G, 1, LANES), F32),
                   jax.ShapeDtypeStruct((SSD_G, 1, LANES), F32), jax.ShapeDtypeStruct((SSD_G, 1, gw), F32)],
        in_specs=[blk(gw, 0), blk(gw, 0), blk(gw, 0), blk(gw, 0), blk(SSD_N, bc0), blk(SSD_N, bc0 + SSD_G),
                  pl.BlockSpec((SSD_CHUNK, LANES), lambda g, c: (rev(c), dt0)),
                  pl.BlockSpec((None, None, SSD_N, gw), lambda g, c: (rev(c), g, 0, 0)),
                  head_vec, head_vec, grp(gw), grp(gw)],
        out_specs=[blk(gw, 0), blk(gw, 0), blk(SSD_N, 0), blk(SSD_N, 0),
                   pl.BlockSpec((None, SSD_CHUNK, LANES), lambda g, c: (g, rev(c), 0)),
                   grp(LANES), grp(LANES), grp(LANES), grp(gw)],
        scratch_shapes=[pltpu.VMEM((SSD_N, gw), F32), pltpu.VMEM((SSD_CHUNK, LANES), F32),
                        pltpu.VMEM((SSD_CHUNK, LANES), F32)],
        compiler_params=_cparams(("parallel", "arbitrary")))(
            dyn, y, zx, xc, xc, xc, zx, prev, bias_p, alog_p, d_lane, nw)


def _cond_mod(c_pad, ada_w, ada_b_loc, after, name):
    depth, D, n = ada_w.shape
    rows = c_pad.shape[0]

    def body(c_ref, w_ref, b_ref, after_ref, mod_ref, cond_ref):
        cv = c_ref[...]
        cond = cv * _sigmoid(cv)
        cond_ref[...] = cond
        mod_ref[...] = _dot(cond.astype(BF16), w_ref[...].astype(BF16)) + b_ref[...]

    return _pcall(
        body, name=name, grid=(depth,),
        out_shape=[jax.ShapeDtypeStruct((depth, rows, n), F32), jax.ShapeDtypeStruct((rows, D), F32)],
        in_specs=[pl.BlockSpec((rows, D), lambda i: (0, 0)),
                  pl.BlockSpec((None, D, n), lambda i: (i, 0, 0)),
                  pl.BlockSpec((None, 1, n), lambda i: (i, 0, 0)),
                  pl.BlockSpec(memory_space=pl.ANY)],
        out_specs=[pl.BlockSpec((None, rows, n), lambda i: (i, 0, 0)),
                   pl.BlockSpec((rows, D), lambda i: (0, 0))],
        compiler_params=_cparams(("arbitrary",)))(c_pad, ada_w, ada_b_loc, after)


def _adamw_math(g, w, m, v):
    m_new = ADAM_B1 * m + (1.0 - ADAM_B1) * g
    v_new = ADAM_B2 * v + (1.0 - ADAM_B2) * (g * g)
    m_hat = m_new / (1.0 - ADAM_B1 ** ADAM_STEP)
    v_hat = v_new / (1.0 - ADAM_B2 ** ADAM_STEP)
    delta = -ADAM_LR * (m_hat / (jnp.sqrt(v_hat) + ADAM_EPS) + ADAM_WD * w)
    return delta, m_new, v_new


def _adamw_sum(parts, w, m, v, layer, name, prev=None, tr=None, window_off=None):
    depth, R, C = w.shape
    tr = _tile(R, tr if tr is not None else (512 if C <= 512 else 256))
    win = parts.shape[2]
    scratch = [] if window_off is None else [pltpu.VMEM((tr, win), F32)]

    def body(p_ref, w_ref, m_ref, v_ref, *rest):
        g_ref, d_ref, mo_ref, vo_ref = rest[-4 - len(scratch):len(rest) - len(scratch)]
        g = p_ref[0].astype(F32)
        for k in range(1, N_DEV):
            g = g + p_ref[k].astype(F32)
        if window_off is not None:
            me = _my_index()
            off = 0
            for k in range(N_DEV):
                off = jnp.where(me == k, window_off[k], off)
            src = lax.broadcasted_iota(jnp.int32, (win, win), 0)
            dst = lax.broadcasted_iota(jnp.int32, (win, win), 1)
            shift = ((src == dst + off) & (dst < C)).astype(BF16)
            hi, mid, lo = _split3(g)
            rest[-1][...] = _dot(hi, shift) + _dot(mid, shift) + _dot(lo, shift)
            g = rest[-1][:, 0:C]
        d, mn, vn = _adamw_math(g, w_ref[...], m_ref[...], v_ref[...])
        g_ref[...] = g
        d_ref[...] = d
        mo_ref[...] = mn
        vo_ref[...] = vn

    blk = pl.BlockSpec((None, tr, C), lambda i: (layer, i, 0))
    prev = list(prev) if prev is not None else []
    return _pcall(
        body, name=name, grid=(R // tr,),
        out_shape=[jax.ShapeDtypeStruct((depth, R, C), F32)] * 4,
        in_specs=[pl.BlockSpec((N_DEV, tr, win), lambda i: (0, i, 0)), blk, blk, blk]
        + [pl.BlockSpec(memory_space=pl.ANY)] * len(prev),
        out_specs=[blk] * 4, input_output_aliases={4 + k: k for k in range(len(prev))},
        scratch_shapes=scratch,
        compiler_params=_cparams(("parallel",)))(parts, w, m, v, *prev)


def _adamw_small(parts, wmv, head_parts, head_wmv, loss_parts, name):
    n, nh = len(parts), len(head_parts)
    n_heads = head_wmv[0][0].shape[1] if nh else 0
    groups = head_parts[0].shape[1] if nh else 0
    d_model = loss_parts.shape[2]

    def body(*refs):
        p_refs, refs = refs[:n], refs[n:]
        wmv_refs, refs = refs[:3 * n], refs[3 * n:]
        hp_refs, refs = refs[:nh], refs[nh:]
        hwmv_refs, refs = refs[:3 * nh], refs[3 * nh:]
        loss_ref, refs = refs[0], refs[1:]
        outs, loss_out, head_scr = refs[:4 * (n + nh)], refs[4 * (n + nh)], refs[4 * (n + nh) + 1]

        def update(i, g, w_ref, m_ref, v_ref):
            res = (g,) + _adamw_math(g, w_ref[...], m_ref[...], v_ref[...])
            for o_ref, r in zip(outs[4 * i:4 * i + 4], res):
                o_ref[...] = r

        for i in range(n):
            g = p_refs[i][0]
            for k in range(1, N_DEV):
                g = g + p_refs[i][k]
            update(i, g, *wmv_refs[3 * i:3 * i + 3])
        for i in range(nh):
            g = None
            for k in range(N_DEV):
                for grp in range(groups):
                    g = hp_refs[i][k, grp] if g is None else g + hp_refs[i][k, grp]
            head_scr[...] = g
            update(n + i, head_scr[:, 0:n_heads], *hwmv_refs[3 * i:3 * i + 3])
        tot = loss_ref[0]
        for k in range(1, N_DEV):
            tot = tot + loss_ref[k]
        loss_out[...] = jnp.broadcast_to(_sum_all(tot) * (0.5 / d_model), loss_out.shape)

    operands = list(parts) + [a for t in wmv for a in t] + list(head_parts) + [a for t in head_wmv for a in t]
    operands.append(loss_parts)
    out_shape = [jax.ShapeDtypeStruct(t[0].shape, F32) for t in list(wmv) + list(head_wmv) for _ in range(4)]
    out_shape.append(jax.ShapeDtypeStruct((1, LANES), F32))
    vmem = pl.BlockSpec(memory_space=pltpu.VMEM)
    outs = _pcall(body, name=name, out_shape=out_shape, in_specs=[vmem] * len(operands),
                  out_specs=[vmem] * len(out_shape), scratch_shapes=[pltpu.VMEM((1, LANES), F32)],
                  compiler_params=_cparams())(*operands)
    return [outs[4 * i:4 * i + 4] for i in range(n + nh)], outs[-1]


def _ada_adamw(cond_pad, dmod_pad, w, m, v, name, tr=512):
    depth, D, n = w.shape
    rows = cond_pad.shape[0]
    tr = _tile(D, tr)

    def body(c_ref, dm_ref, w_ref, m_ref, v_ref, g_ref, d_ref, mo_ref, vo_ref):
        g = _dot(c_ref[...].astype(BF16), dm_ref[...].astype(BF16), "tn")
        d, mn, vn = _adamw_math(g, w_ref[...], m_ref[...], v_ref[...])
        g_ref[...] = g
        d_ref[...] = d
        mo_ref[...] = mn
        vo_ref[...] = vn

    blk = pl.BlockSpec((None, tr, n), lambda i, r: (i, r, 0))
    return _pcall(
        body, name=name, grid=(depth, D // tr),
        out_shape=[jax.ShapeDtypeStruct((depth, D, n), F32)] * 4,
        in_specs=[pl.BlockSpec((rows, tr), lambda i, r: (0, r)),
                  pl.BlockSpec((None, rows, n), lambda i, r: (i, 0, 0)), blk, blk, blk],
        out_specs=[blk] * 4, compiler_params=_cparams(("parallel", "parallel")))(cond_pad, dmod_pad, w, m, v)


def kernel(x, c, ada_w, ada_b, mix_norm_w, mlp_norm_w, mlp_up, mlp_down, ssd_in_w, ssd_conv_w, ssd_conv_b, ssd_dt_bias, ssd_A_log, ssd_D, ssd_norm_w, ssd_out_w, sc_in_w, sc_conv_w, sc_out_w, final_norm_w, loss_target, m_ada_w, m_ada_b, m_mix_norm_w, m_mlp_norm_w, m_mlp_up, m_mlp_down, m_ssd_in_w, m_ssd_conv_w, m_ssd_conv_b, m_ssd_dt_bias, m_ssd_A_log, m_ssd_D, m_ssd_norm_w, m_ssd_out_w, m_sc_in_w, m_sc_conv_w, m_sc_out_w, m_final_norm_w, v_ada_w, v_ada_b, v_mix_norm_w, v_mlp_norm_w, v_mlp_up, v_mlp_down, v_ssd_in_w, v_ssd_conv_w, v_ssd_conv_b, v_ssd_dt_bias, v_ssd_A_log, v_ssd_D, v_ssd_norm_w, v_ssd_out_w, v_sc_in_w, v_sc_conv_w, v_sc_out_w, v_final_norm_w):
    weights = dict(ada_w=ada_w, ada_b=ada_b, mix_norm_w=mix_norm_w, mlp_norm_w=mlp_norm_w, mlp_up=mlp_up,
                   mlp_down=mlp_down, ssd_in_w=ssd_in_w, ssd_conv_w=ssd_conv_w, ssd_conv_b=ssd_conv_b,
                   ssd_dt_bias=ssd_dt_bias, ssd_A_log=ssd_A_log, ssd_D=ssd_D, ssd_norm_w=ssd_norm_w,
                   ssd_out_w=ssd_out_w, sc_in_w=sc_in_w, sc_conv_w=sc_conv_w, sc_out_w=sc_out_w,
                   final_norm_w=final_norm_w)
    moms = dict(ada_w=m_ada_w, ada_b=m_ada_b, mix_norm_w=m_mix_norm_w, mlp_norm_w=m_mlp_norm_w, mlp_up=m_mlp_up,
                mlp_down=m_mlp_down, ssd_in_w=m_ssd_in_w, ssd_conv_w=m_ssd_conv_w, ssd_conv_b=m_ssd_conv_b,
                ssd_dt_bias=m_ssd_dt_bias, ssd_A_log=m_ssd_A_log, ssd_D=m_ssd_D, ssd_norm_w=m_ssd_norm_w,
                ssd_out_w=m_ssd_out_w, sc_in_w=m_sc_in_w, sc_conv_w=m_sc_conv_w, sc_out_w=m_sc_out_w,
                final_norm_w=m_final_norm_w)
    vars_ = dict(ada_w=v_ada_w, ada_b=v_ada_b, mix_norm_w=v_mix_norm_w, mlp_norm_w=v_mlp_norm_w, mlp_up=v_mlp_up,
                 mlp_down=v_mlp_down, ssd_in_w=v_ssd_in_w, ssd_conv_w=v_ssd_conv_w, ssd_conv_b=v_ssd_conv_b,
                 ssd_dt_bias=v_ssd_dt_bias, ssd_A_log=v_ssd_A_log, ssd_D=v_ssd_D, ssd_norm_w=v_ssd_norm_w,
                 ssd_out_w=v_ssd_out_w, sc_in_w=v_sc_in_w, sc_conv_w=v_sc_conv_w, sc_out_w=v_sc_out_w,
                 final_norm_w=v_final_norm_w)
    names = list(weights)

    L, D = x.shape[1], x.shape[2]
    d_inner = 2 * D
    n_heads = d_inner // SSD_P
    hpg = n_heads // SSD_G
    gw = d_inner // SSD_G
    conv_dim = d_inner + 2 * SSD_G * SSD_N
    zx_dim = d_inner + conv_dim
    zx_pad = -(-(zx_dim + LANES) // 512) * 512
    in_ws = ssd_in_w.shape[2]
    in_base, in_off, in_win = _window_geometry(in_ws)
    me = _my_index()
    x0 = x[0]
    tgt = loss_target[0]

    n_mod = ada_w.shape[2]
    (c_all,) = _exchange([c], "gather_c", gather=True)
    gather_handle = {}
    (gather_handle["ssd_in_w"],), token_in = _xfer_start(
        [ssd_in_w[0].astype(BF16)], "gather_start_ssd_in_w", gather=True, via_sibling=(0,), after=(c_all,))
    c_pad = jnp.pad(c_all.reshape(N_DEV, D), ((0, 16 - N_DEV), (0, 0)))
    ada_b_loc = lax.dynamic_slice_in_dim(ada_b, me * n_mod, n_mod, axis=1).reshape(2, 1, n_mod)
    mod_blk, cond_pad = _cond_mod(c_pad, ada_w, ada_b_loc, token_in, "cond_mod")
    gather_order = ["mod", "ssd_conv_w", "sc_conv_w", "ssd_out_w", "up0", "down0", "sc_in_w", "sc_out_w", "up1",
                    "down1"]
    gather_src = dict(mod=mod_blk, ssd_conv_w=ssd_conv_w[0], sc_conv_w=sc_conv_w[0],
                      ssd_out_w=ssd_out_w[0].astype(BF16),
                      up0=mlp_up[0].astype(BF16), down0=mlp_down[0].astype(BF16),
                      sc_in_w=sc_in_w[0].astype(BF16), sc_out_w=sc_out_w[0].astype(BF16),
                      up1=mlp_up[1].astype(BF16), down1=mlp_down[1].astype(BF16))
    handles, gather_token = _xfer_start([gather_src[k] for k in gather_order], "gather_start", gather=True,
                                        via_sibling=tuple(range(3, len(gather_order))))
    gather_handle.update(zip(gather_order, handles))

    def gathered(keys, after, forward):
        tag = "_".join(keys)
        lands = _xfer_wait([gather_handle[k] for k in keys], after, f"gather_wait_{tag}", gather=True)
        return _sibling_forward(lands, f"gather_forward_{tag}") if forward else lands

    def forward_behind(keys, after):
        tag = "_".join(keys)
        lands = _xfer_wait([gather_handle[k] for k in keys], after, f"gather_wait_{tag}", gather=True)
        fwd_handles, token = _sibling_forward_start(lands, f"gather_forward_start_{tag}")
        return (lambda done: _sibling_forward_wait(fwd_handles, done, f"gather_forward_wait_{tag}")), token

    (ssd_in_g,) = gathered(["ssd_in_w"], (gather_token, m_ssd_in_w, v_ssd_in_w), True)
    w_in_all = _shards_to_columns(ssd_in_g, in_base, in_off, in_win, zx_pad, "ssd_in_w_columns")
    (mod_all,) = gathered(["mod"], w_in_all, False)
    mod_mine = lax.dynamic_index_in_dim(mod_all, me, axis=2, keepdims=False)
    mod_mine = jnp.transpose(mod_mine, (1, 0, 2)).reshape(2, 6, 1, D)
    sh_m, sc_m, g_m, sh_f, sc_f, g_f = [[mod_mine[i, k] for i in range(2)] for k in range(6)]

    vec = lambda a: a.reshape(1, -1)
    grads = {}
    small = {}

    _, h0 = _norm_mod_fwd(x0, None, None, vec(mix_norm_w[0]), sc_m[0], sh_m[0], "l0_mix_norm")
    cw_all, scw_all = gathered(["ssd_conv_w", "sc_conv_w"], h0, False)
    (zx,) = _mm_nn(h0, w_in_all, F32, "ssd_in_proj", tm=2048, tn=512)
    conv_b0 = vec(ssd_conv_b[0])
    conv_w_full = jnp.transpose(cw_all, (1, 0, 2)).reshape(SSD_K, conv_dim)
    sc_conv_full = jnp.transpose(scw_all, (1, 0, 2)).reshape(SC_K, D)
    xc = _ssd_conv_fwd(zx, conv_w_full, conv_b0, d_inner, conv_dim, "ssd_conv")
    bias_p = jnp.pad(ssd_dt_bias[0], (0, LANES - n_heads)).reshape(1, LANES)
    alog_p = jnp.pad(ssd_A_log[0], (0, LANES - n_heads)).reshape(1, LANES)
    d_lane = jnp.repeat(ssd_D[0], SSD_P).reshape(SSD_G, 1, gw)
    nw_g = ssd_norm_w[0].reshape(SSD_G, 1, gw)
    finish, token = forward_behind(["ssd_out_w"], xc)
    y_ssd, yn, prev = _ssd_fwd(zx, xc, bias_p, alog_p, d_lane, nw_g, d_inner, token, "ssd_scan")
    ups, downs = [None, None], [None, None]
    (ssd_out_g,) = finish(yn)
    w_ssd_out = ssd_out_g.reshape(-1, D)
    finish, token = forward_behind(["up0", "down0"], ssd_out_g)
    (mix0,) = _mm_nn(yn, w_ssd_out, F32, "ssd_out_proj", after=(token,))
    x1, h1 = _norm_mod_fwd(x0, mix0, g_m[0], vec(mlp_norm_w[0]), sc_f[0], sh_f[0], "l0_mlp_norm")
    ups[0], down0_g = finish(h1)
    downs[0] = down0_g.reshape(-1, D)
    u0, s0 = _mm_nn_blocked(h1, ups[0], "l0_mlp_up", _ep_relu2, [BF16, BF16])
    finish, token = forward_behind(["sc_in_w", "sc_out_w", "up1", "down1"], s0)
    (d0,) = _mm_nn(s0, downs[0], F32, "l0_mlp_down", after=(token,))
    x2, h2 = _norm_mod_fwd(x1, d0, g_f[0], vec(mix_norm_w[1]), sc_m[1], sh_m[1], "l1_mix_norm")
    sc_in_g, sc_out_g, ups[1], down1_g = finish(h2)
    w_sc_out, downs[1] = sc_out_g.reshape(-1, D), down1_g.reshape(-1, D)
    (proj,) = _mm_nn_blocked(h2, sc_in_g, "sc_in_proj", _ep_store(F32), [F32])
    yc = _sc_conv_fwd(proj, sc_conv_full, "sc_conv")
    (mix1,) = _mm_nn(yc, w_sc_out, F32, "sc_out_proj")
    x3, h3 = _norm_mod_fwd(x2, mix1, g_m[1], vec(mlp_norm_w[1]), sc_f[1], sh_f[1], "l1_mlp_norm")
    u1, s1 = _mm_nn_blocked(h3, ups[1], "l1_mlp_up", _ep_relu2, [BF16, BF16])
    (d1,) = _mm_nn(s1, downs[1], F32, "l1_mlp_down")

    dx, loss_lane, dfw, dd1, dg = _final_loss(x3, d1, g_f[1], vec(final_norm_w), tgt, "final_loss")
    small["final_norm_w"] = dfw

    dmod = [[None] * 6 for _ in range(2)]
    dmod[1][5] = dg

    def mlp_backward(i, dx_out, dd, x_mid, h_in, u, s, mix, gate):
        du = _mm_nt(dd, downs[i], BF16, f"l{i}_mlp_down_bwd", epilogue=_ep_relu2_bwd, extra=(u,))
        gdown = _mm_tn(s, dd, BF16, f"l{i}_mlp_down_wgrad").reshape(N_DEV, -1, D)
        gup = _mm_tn_blocked(h_in, du, BF16, f"l{i}_mlp_up_wgrad")
        (h_down, h_up), token = _xfer_start([gdown, gup], f"l{i}_mlp_grads_start", gather=False)
        grad_handle[f"mlp_down{i}"], grad_handle[f"mlp_up{i}"] = h_down, h_up
        dxm, dsh, dsc, dnw, dmix, dgate = _mm_nt_norm_bwd(
            du, ups[i], x_mid, vec(mlp_norm_w[i]), sc_f[i], dx_out, f"l{i}_mlp_up_norm_bwd", branch=(mix, gate),
            blocked=True, after=(token,))
        dmod[i][3], dmod[i][4], dmod[i][2] = dsh, dsc, dgate
        return dxm, dmix, dnw

    grad_handle = {}
    dx3, dyc, dnw_mlp1 = mlp_backward(1, dx, dd1, x3, h3, u1, s1, mix1, g_m[1])
    g_sc_out = _mm_tn(yc, dyc, BF16, "sc_out_wgrad").reshape(N_DEV, -1, D)
    dconv_out = _mm_nt(dyc, w_sc_out, F32, "sc_out_bwd")
    dbg, dcg, dxv, dscw = _sc_conv_bwd(proj, sc_conv_full, dconv_out, "sc_conv_bwd")
    dproj = jnp.concatenate([dbg, dcg, dxv], axis=1)
    g_sc_in = _mm_tn_blocked(h2, dproj, BF16, "sc_in_wgrad")
    (grad_handle["sc_out_w0"], grad_handle["sc_in_w0"]), token = _xfer_start(
        [g_sc_out, g_sc_in], "sc_grads_start", gather=False)
    dx2, dsh, dsc, dnw_mix1, dd0, dg = _mm_nt_norm_bwd(
        dproj, sc_in_g, x2, vec(mix_norm_w[1]), sc_m[1], dx3, "sc_in_norm_bwd", branch=(d0, g_f[0]),
        blocked=True, after=(token,))
    dmod[1][0], dmod[1][1], dmod[0][5] = dsh, dsc, dg
    dx1, dyo, dnw_mlp0 = mlp_backward(0, dx2, dd0, x1, h1, u0, s0, mix0, g_m[0])
    g_ssd_out = _mm_tn(yn, dyo, BF16, "ssd_out_wgrad").reshape(N_DEV, -1, D)
    (grad_handle["ssd_out_w0"],), token = _xfer_start([g_ssd_out], "ssd_out_grad_start", gather=False)
    dyn = _mm_nt(dyo, w_ssd_out, F32, "ssd_out_bwd", after=(token,))
    dz, dxs, db_, dc_, ddt, dbias, dalog, dd_, dnw_ssd = _ssd_bwd(
        dyn, y_ssd, zx, xc, prev, bias_p, alog_p, d_lane, nw_g, d_inner, "ssd_scan_bwd")
    dzx, dcw, dcb = _ssd_conv_bwd(zx, conv_w_full, conv_b0, [dxs, db_, dc_], dz, d_inner, "ssd_conv_bwd")
    dzx = _dzx_finish(dzx, ddt, zx_dim, "ssd_dzx_finish")
    g_in_all = _mm_tn(h0, dzx, BF16, "ssd_in_wgrad", tn=512, tk=2048)
    g_ssd_in = jnp.stack([g_in_all[:, b:b + in_win] for b in in_base], axis=0)
    (grad_handle["ssd_in_w0"],), token = _xfer_start([g_ssd_in], "ssd_in_grad_start", gather=False)
    grad_x, dsh, dsc, dnw_mix0 = _mm_nt_norm_bwd(
        dzx, w_in_all, x0, vec(mix_norm_w[0]), sc_m[0], dx1, "ssd_in_norm_bwd", tk=dzx.shape[1] // 4,
        after=(token,))
    dmod[0][0], dmod[0][1] = dsh, dsc

    small["ada_b"] = jnp.concatenate([jnp.concatenate(dmod[i], axis=1) for i in range(2)], axis=0)
    small["mix_norm_w"] = jnp.concatenate([dnw_mix0, dnw_mix1], axis=0)
    small["mlp_norm_w"] = jnp.concatenate([dnw_mlp0, dnw_mlp1], axis=0)
    small["ssd_conv_w"] = dcw
    small["ssd_conv_b"] = dcb
    small["ssd_norm_w"] = dnw_ssd.reshape(1, d_inner)
    small["sc_conv_w"] = dscw
    small["loss"] = loss_lane
    small_names = list(small)
    head_names = ["ssd_dt_bias", "ssd_A_log", "ssd_D"]
    handles, small_token = _xfer_start([small[k] for k in small_names] + [dbias, dalog, dd_],
                                       "small_grads_start", gather=True)

    out_g, out_d, out_m, out_v = {}, {}, {}, {}

    layer_res = {}

    def big_update(name, i, after):
        (parts,) = _xfer_wait([grad_handle[f"{name}{i}"]], after, f"grads_wait_{name}_{i}", gather=False)
        res = _adamw_sum(parts, weights[name], moms[name], vars_[name], i, f"adamw_{name}_{i}",
                         prev=layer_res.get(name), window_off=in_off if name == "ssd_in_w" else None)
        layer_res[name] = res
        return res[1]

    chain = small_token
    for name, i in [("mlp_down", 1), ("mlp_up", 1), ("sc_out_w", 0), ("sc_in_w", 0), ("mlp_down", 0),
                    ("mlp_up", 0), ("ssd_out_w", 0), ("ssd_in_w", 0)]:
        chain = big_update(name, i, chain)
    gathered_small = _xfer_wait(handles, chain, "small_grads_wait", gather=True)
    small_all = dict(zip(small_names + head_names, gathered_small))

    dmod_loc = lax.dynamic_slice_in_dim(small_all["ada_b"], me * n_mod, n_mod, axis=2)
    dmod_pad = jnp.pad(jnp.transpose(dmod_loc, (1, 0, 2)), ((0, 0), (0, 16 - N_DEV), (0, 0)))
    out_g["ada_w"], out_d["ada_w"], out_m["ada_w"], out_v["ada_w"] = _ada_adamw(
        cond_pad, dmod_pad, ada_w, m_ada_w, v_ada_w, "adamw_ada_w")

    for k in ("ssd_conv_w", "sc_conv_w"):
        n_loc = weights[k].shape[2]
        small_all[k] = lax.dynamic_slice_in_dim(small_all[k], me * n_loc, n_loc, axis=2)
    plain = [k for k in small_names if k != "loss"]
    as2d = lambda a: a.reshape(-1, a.shape[-1])
    res, loss_row = _adamw_small(
        [small_all[k] for k in plain], [tuple(as2d(d[k]) for d in (weights, moms, vars_)) for k in plain],
        [small_all[k] for k in head_names], [tuple(as2d(d[k]) for d in (weights, moms, vars_)) for k in head_names],
        small_all["loss"], "adamw_small")
    loss = loss_row[0, 0]
    for k, res4 in zip(plain + head_names, res):
        for r, dst in zip(res4, (out_g, out_d, out_m, out_v)):
            dst[k] = r.reshape(weights[k].shape)
    for name, res4 in layer_res.items():
        for r, dst in zip(res4, (out_g, out_d, out_m, out_v)):
            dst[name] = r

    return (loss, grad_x[None], *[out_g[k] for k in names], *[out_d[k] for k in names],
            *[out_m[k] for k in names], *[out_v[k] for k in names])
```

```python
import functools

import jax
import jax.numpy as jnp
from jax import lax
from jax.experimental import pallas as pl
from jax.experimental.pallas import tpu as pltpu

F32 = jnp.float32
BF16 = jnp.bfloat16
N_DEV = 8
MESH_AXES = ("x", "y", "c")
MESH = pl.DeviceIdType.MESH

NORM_EPS = 1e-5
SSD_G = 4
SSD_P = 64
SSD_N = 128
SSD_CHUNK = 128
SSD_K = 4
SC_K = 3
LANES = 128

ADAM_LR = 0.001
ADAM_B1 = 0.9
ADAM_B2 = 0.999
ADAM_EPS = 1e-08
ADAM_WD = 0.01
ADAM_STEP = 10

VMEM_LIMIT = 56 * 1024 * 1024


def _pcall(body, **kw):
    return pl.pallas_call(body, **kw)


def _cparams(sem=None):
    if sem is None:
        return pltpu.CompilerParams(vmem_limit_bytes=VMEM_LIMIT)
    return pltpu.CompilerParams(dimension_semantics=sem, vmem_limit_bytes=VMEM_LIMIT)


def _my_index():
    return 4 * lax.axis_index("x") + 2 * lax.axis_index("y") + lax.axis_index("c")


_PEER_MASKS = [(0, 0, 1), (0, 1, 0), (0, 1, 1), (1, 0, 0), (1, 0, 1), (1, 1, 0), (1, 1, 1)]


def _peers():
    x, y, c = lax.axis_index("x"), lax.axis_index("y"), lax.axis_index("c")
    out = []
    for mx, my, mc in _PEER_MASKS:
        px = (1 - x) if mx else x
        py = (1 - y) if my else y
        pc = (1 - c) if mc else c
        out.append(((px, py, pc), 4 * px + 2 * py + pc))
    return out


def _exchange(arrs, name, gather):
    n = len(arrs)
    n_peer = N_DEV - 1

    def body(*refs):
        ins, outs = refs[:n], refs[n:2 * n]
        send_sems, recv_sems, local_sems = refs[2 * n:]
        me = _my_index()
        peers = _peers()
        started = []
        for a in range(n):
            src_own = ins[a] if gather else ins[a].at[me]
            own = pltpu.make_async_copy(src_own, outs[a].at[me], local_sems.at[a])
            own.start()
            started.append(own)
        sends = []
        for a in range(n):
            for k, (peer, pidx) in enumerate(peers):
                src = ins[a] if gather else ins[a].at[pidx]
                cp = pltpu.make_async_remote_copy(
                    src_ref=src, dst_ref=outs[a].at[me],
                    send_sem=send_sems.at[a * n_peer + k], recv_sem=recv_sems.at[a * n_peer + k],
                    device_id=peer, device_id_type=MESH)
                cp.start()
                sends.append(cp)
        for a in range(n):
            for k, (peer, pidx) in enumerate(peers):
                src = ins[a] if gather else ins[a].at[pidx]
                pltpu.make_async_remote_copy(
                    src_ref=src, dst_ref=outs[a].at[pidx],
                    send_sem=send_sems.at[a * n_peer + k], recv_sem=recv_sems.at[a * n_peer + k],
                    device_id=peer, device_id_type=MESH).wait_recv()
        for cp in sends:
            cp.wait_send()
        for own in started:
            own.wait()

    if gather:
        out_shape = [jax.ShapeDtypeStruct((N_DEV,) + a.shape, a.dtype) for a in arrs]
    else:
        out_shape = [jax.ShapeDtypeStruct(a.shape, a.dtype) for a in arrs]
    any_spec = pl.BlockSpec(memory_space=pl.ANY)
    outs = _pcall(
        body, name=name, out_shape=out_shape,
        in_specs=[any_spec] * n, out_specs=[any_spec] * n,
        scratch_shapes=[pltpu.SemaphoreType.DMA((n * n_peer,)), pltpu.SemaphoreType.DMA((n * n_peer,)),
                        pltpu.SemaphoreType.DMA((n,))],
        compiler_params=pltpu.CompilerParams(has_side_effects=True),
    )(*arrs)
    return list(outs)


def _sibling_forward_start(lands, name):
    n = len(lands)
    n_fwd = len(_OTHER_CHIPS)

    def body(*refs):
        ins, bufs = refs[:n], refs[3 * n:4 * n]
        token = refs[-1]
        sibling = (lax.axis_index("x"), lax.axis_index("y"), 1 - lax.axis_index("c"))
        peers = _peers()
        for a in range(n):
            send_sems, recv_sems = refs[n + 2 * a], refs[n + 2 * a + 1]
            for j, k in enumerate(_OTHER_CHIPS):
                slot = peers[k][1]
                pltpu.make_async_remote_copy(
                    src_ref=ins[a].at[slot], dst_ref=bufs[a].at[slot], send_sem=send_sems.at[j],
                    recv_sem=recv_sems.at[j], device_id=sibling, device_id_type=MESH).start()
        token[...] = jnp.zeros_like(token)

    out_shape, out_specs = [], []
    for _ in range(n):
        out_shape += [pltpu.SemaphoreType.DMA((n_fwd,)), pltpu.SemaphoreType.DMA((n_fwd,))]
        out_specs += [_SEM, _SEM]
    out_shape += [pltpu.HBM(a.shape, a.dtype) for a in lands] + [jax.ShapeDtypeStruct((8, LANES), F32)]
    out_specs += [_HBM] * n + [pl.BlockSpec(memory_space=pltpu.VMEM)]
    outs = _pcall(
        body, name=name, out_shape=tuple(out_shape), in_specs=[_HBM] * n, out_specs=tuple(out_specs),
        input_output_aliases={a: 2 * n + a for a in range(n)},
        compiler_params=pltpu.CompilerParams(has_side_effects=_DATAFLOW),
    )(*[pltpu.with_memory_space_constraint(a, pltpu.HBM) for a in lands])
    return [(outs[2 * n + a], outs[2 * a], outs[2 * a + 1]) for a in range(n)], outs[-1]


def _sibling_forward_wait(handles, after, name):
    n = len(handles)

    def body(*refs):
        sibling = (lax.axis_index("x"), lax.axis_index("y"), 1 - lax.axis_index("c"))
        peers = _peers()
        for a in range(n):
            buf, send_sems, recv_sems = refs[3 * a:3 * a + 3]
            for j, k in enumerate(_OTHER_CHIPS):
                (px, py, pc), slot = peers[k]
                theirs = 4 * px + 2 * py + (1 - pc)
                cp = pltpu.make_async_remote_copy(
                    src_ref=buf.at[slot], dst_ref=buf.at[theirs], send_sem=send_sems.at[j],
                    recv_sem=recv_sems.at[j], device_id=sibling, device_id_type=MESH)
                cp.wait_send()
                cp.wait_recv()

    operands, in_specs = [], []
    for h in handles:
        operands += list(h)
        in_specs += [_HBM, _SEM, _SEM]
    outs = _pcall(
        body, name=name, out_shape=tuple(pltpu.HBM(h[0].shape, h[0].dtype) for h in handles),
        in_specs=in_specs + [pl.BlockSpec(memory_space=pl.ANY)], out_specs=tuple([_HBM] * n),
        input_output_aliases={3 * a: a for a in range(n)},
        compiler_params=pltpu.CompilerParams(has_side_effects=_DATAFLOW),
    )(*operands, after)
    return list(outs)


_HBM = pl.BlockSpec(memory_space=pltpu.HBM)
_SEM = pl.BlockSpec(memory_space=pltpu.SEMAPHORE)
_DATAFLOW = pltpu.SideEffectType.DATAFLOW_SIDE_EFFECTING


_ALL_PEERS = tuple(range(N_DEV - 1))
_SAME_CORE_PEERS = (0, 1, 3, 5)
_OTHER_CHIPS = (1, 3, 5)


def _xfer_start(arrs, name, gather, via_sibling=(), after=()):
    n = len(arrs)
    n_peer = N_DEV - 1
    n_after = len(after)
    peer_ks = [_SAME_CORE_PEERS if a in via_sibling else _ALL_PEERS for a in range(n)]

    def body(*refs):
        ins, lands = refs[:n], refs[n:2 * n]
        sems = refs[2 * n + n_after:5 * n + n_after]
        token = refs[-1]
        me = _my_index()
        peers = _peers()
        for a in range(n):
            send_sems, recv_sems, loc_sem = sems[3 * a:3 * a + 3]
            src_own = ins[a] if gather else ins[a].at[me]
            pltpu.make_async_copy(src_own, lands[a].at[me], loc_sem).start()
            for k in peer_ks[a]:
                peer, pidx = peers[k]
                src = ins[a] if gather else ins[a].at[pidx]
                pltpu.make_async_remote_copy(
                    src_ref=src, dst_ref=lands[a].at[me], send_sem=send_sems.at[k], recv_sem=recv_sems.at[k],
                    device_id=peer, device_id_type=MESH).start()
        token[...] = jnp.zeros_like(token)

    land_shapes = [((N_DEV,) + a.shape) if gather else a.shape for a in arrs]
    out_shape, out_specs = [], []
    for _ in range(n):
        out_shape += [pltpu.SemaphoreType.DMA((n_peer,)), pltpu.SemaphoreType.DMA((n_peer,)),
                      pltpu.SemaphoreType.DMA(())]
        out_specs += [_SEM, _SEM, _SEM]
    out_shape += [pltpu.HBM(a.shape, a.dtype) for a in arrs]
    out_shape += [pltpu.HBM(s, a.dtype) for s, a in zip(land_shapes, arrs)]
    out_shape += [jax.ShapeDtypeStruct((8, LANES), F32)]
    out_specs += [_HBM] * (2 * n) + [pl.BlockSpec(memory_space=pltpu.VMEM)]
    aliases = {}
    for a in range(n):
        aliases[a] = 3 * n + a
        aliases[n + a] = 4 * n + a
    operands = [pltpu.with_memory_space_constraint(a, pltpu.HBM) for a in arrs]
    operands += [pltpu.with_memory_space_constraint(lax.empty(s, a.dtype), pltpu.HBM)
                 for s, a in zip(land_shapes, arrs)]
    outs = _pcall(
        body, name=name, out_shape=tuple(out_shape),
        in_specs=[_HBM] * (2 * n) + [pl.BlockSpec(memory_space=pl.ANY)] * n_after, out_specs=tuple(out_specs),
        input_output_aliases=aliases,
        compiler_params=pltpu.CompilerParams(has_side_effects=_DATAFLOW),
    )(*operands, *after)
    handles = []
    for a in range(n):
        handles.append((outs[3 * n + a], outs[4 * n + a], outs[3 * a], outs[3 * a + 1], outs[3 * a + 2],
                        peer_ks[a]))
    return handles, outs[-1]


def _xfer_wait(handles, after, name, gather):
    n = len(handles)
    after = tuple(after) if isinstance(after, (tuple, list)) else (after,)
    peer_ks = [h[5] for h in handles]

    def body(*refs):
        me = _my_index()
        peers = _peers()
        for a in range(n):
            src_ref, land_ref, send_ref, recv_ref, loc_ref = refs[5 * a:5 * a + 5]
            src_own = src_ref if gather else src_ref.at[me]
            pltpu.make_async_copy(src_own, land_ref.at[me], loc_ref).wait()
            for k in peer_ks[a]:
                peer, pidx = peers[k]
                src = src_ref if gather else src_ref.at[pidx]
                cp = pltpu.make_async_remote_copy(
                    src_ref=src, dst_ref=land_ref.at[pidx], send_sem=send_ref.at[k], recv_sem=recv_ref.at[k],
                    device_id=peer, device_id_type=MESH)
                cp.wait_send()
                cp.wait_recv()

    operands, in_specs, out_shape, aliases = [], [], [], {}
    for a, h in enumerate(handles):
        operands += list(h[:5])
        in_specs += [_HBM, _HBM, _SEM, _SEM, _SEM]
        out_shape += [pltpu.HBM(h[0].shape, h[0].dtype), pltpu.HBM(h[1].shape, h[1].dtype)]
        aliases[5 * a] = 2 * a
        aliases[5 * a + 1] = 2 * a + 1
    outs = _pcall(
        body, name=name, out_shape=tuple(out_shape),
        in_specs=in_specs + [pl.BlockSpec(memory_space=pl.ANY)] * len(after),
        out_specs=tuple([_HBM] * (2 * n)), input_output_aliases=aliases,
        compiler_params=pltpu.CompilerParams(has_side_effects=_DATAFLOW),
    )(*operands, *after)
    return [outs[2 * a + 1] for a in range(n)]


def _sibling_forward(lands, name):
    n = len(lands)
    n_fwd = len(_OTHER_CHIPS)

    def body(*refs):
        ins, bufs = refs[:n], refs[n:2 * n]
        send_sems, recv_sems = refs[2 * n:]
        x, y, c = lax.axis_index("x"), lax.axis_index("y"), lax.axis_index("c")
        sibling = (x, y, 1 - c)
        peers = _peers()
        sends = []
        for a in range(n):
            for j, k in enumerate(_OTHER_CHIPS):
                slot = peers[k][1]
                cp = pltpu.make_async_remote_copy(
                    src_ref=ins[a].at[slot], dst_ref=bufs[a].at[slot],
                    send_sem=send_sems.at[a * n_fwd + j], recv_sem=recv_sems.at[a * n_fwd + j],
                    device_id=sibling, device_id_type=MESH)
                cp.start()
                sends.append(cp)
        for a in range(n):
            for j, k in enumerate(_OTHER_CHIPS):
                (px, py, pc), slot = peers[k]
                theirs = 4 * px + 2 * py + (1 - pc)
                pltpu.make_async_remote_copy(
                    src_ref=ins[a].at[slot], dst_ref=bufs[a].at[theirs],
                    send_sem=send_sems.at[a * n_fwd + j], recv_sem=recv_sems.at[a * n_fwd + j],
                    device_id=sibling, device_id_type=MESH).wait_recv()
        for cp in sends:
            cp.wait_send()

    any_spec = pl.BlockSpec(memory_space=pl.ANY)
    outs = _pcall(
        body, name=name, out_shape=[jax.ShapeDtypeStruct(a.shape, a.dtype) for a in lands],
        in_specs=[any_spec] * n, out_specs=[any_spec] * n,
        input_output_aliases={a: a for a in range(n)},
        scratch_shapes=[pltpu.SemaphoreType.DMA((n * n_fwd,)), pltpu.SemaphoreType.DMA((n * n_fwd,))],
        compiler_params=pltpu.CompilerParams(has_side_effects=True),
    )(*lands)
    return list(outs)


_DIMS = {"nn": (((1,), (0,)), ((), ())), "nt": (((1,), (1,)), ((), ())), "tn": (((0,), (0,)), ((), ()))}


def _dot(a, b, mode="nn"):
    return lax.dot_general(a, b, _DIMS[mode], preferred_element_type=F32)


def _mm(a, b, *, mode, grid, a_spec, b_spec, out_shape, out_specs, acc_shape, epilogue, name,
        extra=(), extra_specs=(), after=(), semantics=("parallel", "parallel", "arbitrary")):
    nk = grid[2]
    n_extra = len(extra)
    n_in = 2 + n_extra + len(after)

    def body_single(*refs):
        a_ref, b_ref = refs[0], refs[1]
        epilogue(_dot(a_ref[...], b_ref[...], mode), refs[2:2 + n_extra], refs[n_in:])

    def body_acc(*refs):
        a_ref, b_ref = refs[0], refs[1]
        ex = refs[2:2 + n_extra]
        outs = refs[n_in:-1]
        acc = refs[-1]
        k = pl.program_id(2)

        @pl.when(k == 0)
        def _():
            acc[...] = jnp.zeros_like(acc)

        acc[...] += _dot(a_ref[...], b_ref[...], mode)

        @pl.when(k == nk - 1)
        def _():
            epilogue(acc[...], ex, outs)

    return _pcall(
        body_single if nk == 1 else body_acc, name=name, grid=grid, out_shape=out_shape,
        in_specs=[a_spec, b_spec] + list(extra_specs) + [pl.BlockSpec(memory_space=pl.ANY)] * len(after),
        out_specs=out_specs,
        scratch_shapes=[] if nk == 1 else [pltpu.VMEM(acc_shape, F32)],
        compiler_params=_cparams(semantics),
    )(a, b, *extra, *after)


def _ep_store(dtype):
    def ep(acc, ex, outs):
        outs[0][...] = acc.astype(dtype)
    return ep


def _ep_relu2(acc, ex, outs):
    outs[0][...] = acc.astype(BF16)
    r = jnp.maximum(acc, 0.0)
    outs[1][...] = (r * r).astype(BF16)


def _ep_relu2_bwd(acc, ex, outs):
    u = ex[0][...].astype(F32)
    outs[0][...] = (acc * (2.0 * jnp.maximum(u, 0.0))).astype(BF16)


def _tile(n, want):
    t = min(n, want)
    while n % t:
        t //= 2
    return t


def _mm_nn(a, w, out_dtype, name, tm=1024, tn=1024, tk=1024, epilogue=None, out_dtypes=None, after=()):
    M, K = a.shape
    N = w.shape[1]
    tm, tn, tk = _tile(M, tm), _tile(N, tn), _tile(K, tk)
    out_dtypes = out_dtypes or [out_dtype]
    return _mm(a, w, mode="nn", grid=(M // tm, N // tn, K // tk),
               a_spec=pl.BlockSpec((tm, tk), lambda i, j, k: (i, k)),
               b_spec=pl.BlockSpec((tk, tn), lambda i, j, k: (k, j)),
               out_shape=[jax.ShapeDtypeStruct((M, N), d) for d in out_dtypes],
               out_specs=[pl.BlockSpec((tm, tn), lambda i, j, k: (i, j)) for _ in out_dtypes],
               acc_shape=(tm, tn), epilogue=epilogue or _ep_store(out_dtype), name=name, after=after)


def _mm_nn_blocked(a, wg, name, epilogue, out_dtypes, tm=2048):
    M, K = a.shape
    n = wg.shape[2]
    tm = _tile(M, tm)
    return _mm(a, wg, mode="nn", grid=(M // tm, N_DEV, 1),
               a_spec=pl.BlockSpec((tm, K), lambda i, j, k: (i, 0)),
               b_spec=pl.BlockSpec((None, K, n), lambda i, j, k: (j, 0, 0)),
               out_shape=[jax.ShapeDtypeStruct((M, N_DEV * n), d) for d in out_dtypes],
               out_specs=[pl.BlockSpec((tm, n), lambda i, j, k: (i, j)) for _ in out_dtypes],
               acc_shape=(tm, n), epilogue=epilogue, name=name)


def _mm_nt(a, w, out_dtype, name, tm=1024, tn=1024, tk=1024, epilogue=None, extra=(), extra_specs=(),
           after=()):
    M, K = a.shape
    N = w.shape[0]
    tm, tn, tk = _tile(M, tm), _tile(N, tn), _tile(K, tk)
    if extra and not extra_specs:
        extra_specs = [pl.BlockSpec((tm, tn), lambda i, j, k: (i, j)) for _ in extra]
    return _mm(a, w, mode="nt", grid=(M // tm, N // tn, K // tk),
               a_spec=pl.BlockSpec((tm, tk), lambda i, j, k: (i, k)),
               b_spec=pl.BlockSpec((tn, tk), lambda i, j, k: (j, k)),
               out_shape=[jax.ShapeDtypeStruct((M, N), out_dtype)],
               out_specs=[pl.BlockSpec((tm, tn), lambda i, j, k: (i, j))],
               acc_shape=(tm, tn), epilogue=epilogue or _ep_store(out_dtype), name=name,
               extra=extra, extra_specs=extra_specs, after=after)[0]


def _mm_nt_blocked(a, wg, out_dtype, name, tm=1024, after=()):
    M = a.shape[0]
    kout, n = wg.shape[1], wg.shape[2]
    tm = _tile(M, tm)
    return _mm(a, wg, mode="nt", grid=(M // tm, 1, N_DEV),
               a_spec=pl.BlockSpec((tm, n), lambda i, j, k: (i, k)),
               b_spec=pl.BlockSpec((None, kout, n), lambda i, j, k: (k, 0, 0)),
               out_shape=[jax.ShapeDtypeStruct((M, kout), out_dtype)],
               out_specs=[pl.BlockSpec((tm, kout), lambda i, j, k: (i, 0))],
               acc_shape=(tm, kout), epilogue=_ep_store(out_dtype), name=name, after=after)[0]


def _mm_tn(a, b, out_dtype, name, tm=1024, tn=1024, tk=1024):
    K, M = a.shape
    N = b.shape[1]
    tm, tn, tk = _tile(M, tm), _tile(N, tn), _tile(K, tk)
    return _mm(a, b, mode="tn", grid=(M // tm, N // tn, K // tk),
               a_spec=pl.BlockSpec((tk, tm), lambda i, j, k: (k, i)),
               b_spec=pl.BlockSpec((tk, tn), lambda i, j, k: (k, j)),
               out_shape=[jax.ShapeDtypeStruct((M, N), out_dtype)],
               out_specs=[pl.BlockSpec((tm, tn), lambda i, j, k: (i, j))],
               acc_shape=(tm, tn), epilogue=_ep_store(out_dtype), name=name)[0]


def _mm_tn_blocked(a, b, out_dtype, name, tm=1024, tk=2048):
    K, M = a.shape
    n = b.shape[1] // N_DEV
    tm, tk = _tile(M, tm), _tile(K, tk)
    return _mm(a, b, mode="tn", grid=(M // tm, N_DEV, K // tk),
               a_spec=pl.BlockSpec((tk, tm), lambda i, j, k: (k, i)),
               b_spec=pl.BlockSpec((tk, n), lambda i, j, k: (k, j)),
               out_shape=[jax.ShapeDtypeStruct((N_DEV, M, n), out_dtype)],
               out_specs=[pl.BlockSpec((None, tm, n), lambda i, j, k: (j, i, 0))],
               acc_shape=(tm, n), epilogue=_ep_store(out_dtype), name=name)[0]


def _window_geometry(ws):
    base = [(ws * k // LANES) * LANES for k in range(N_DEV)]
    off = [ws * k - base[k] for k in range(N_DEV)]
    win = -(-(max(off) + ws) // LANES) * LANES
    return base, off, win


def _shards_to_columns(xg, base, off, win, n_out, name, tr=256):
    R, ws = xg.shape[1], xg.shape[2]
    tr = _tile(R, tr)
    nb_win = win // LANES

    def body(x_ref, o_ref, frame_ref):
        written = set()
        frame_ref[...] = jnp.zeros_like(frame_ref)
        for k in range(N_DEV):
            frame_ref[:, 0:ws] = x_ref[k].astype(F32)
            window = frame_ref[...]
            if off[k]:
                window = pltpu.roll(window, off[k], 1)
            for i in range(nb_win):
                b = base[k] // LANES + i
                if b * LANES >= n_out:
                    continue
                cols = slice(b * LANES, (b + 1) * LANES)
                blk = window[:, i * LANES:(i + 1) * LANES]
                if b in written:
                    blk = blk + o_ref[:, cols].astype(F32)
                o_ref[:, cols] = blk.astype(o_ref.dtype)
                written.add(b)
        for b in range(n_out // LANES):
            if b not in written:
                o_ref[:, b * LANES:(b + 1) * LANES] = jnp.zeros((tr, LANES), o_ref.dtype)

    return _pcall(
        body, name=name, grid=(R // tr,), out_shape=jax.ShapeDtypeStruct((R, n_out), xg.dtype),
        in_specs=[pl.BlockSpec((N_DEV, tr, ws), lambda i: (0, i, 0))],
        out_specs=pl.BlockSpec((tr, n_out), lambda i: (i, 0)),
        scratch_shapes=[pltpu.VMEM((tr, win), F32)],
        compiler_params=_cparams(("parallel",)))(xg)


def _sigmoid(x):
    return 1.0 / (1.0 + jnp.exp(-x))


def _row_spec(tm, d):
    return pl.BlockSpec((tm, d), lambda i: (i, 0))


def _vec_spec(d):
    return pl.BlockSpec((1, d), lambda i: (0, 0))


def _norm_mod_fwd(x, y, gate, nw, scale, shift, name, tm=512):
    L, D = x.shape
    tm = _tile(L, tm)
    has_res = y is not None

    def body(*refs):
        if has_res:
            x_ref, y_ref, g_ref, nw_ref, sc_ref, sh_ref, xo_ref, h_ref = refs
            xn = x_ref[...] + g_ref[...] * y_ref[...]
            xo_ref[...] = xn
        else:
            x_ref, nw_ref, sc_ref, sh_ref, h_ref = refs
            xn = x_ref[...]
        rstd = lax.rsqrt(jnp.mean(xn * xn, axis=-1, keepdims=True) + NORM_EPS)
        h = xn * rstd * nw_ref[...] * (1.0 + sc_ref[...]) + sh_ref[...]
        h_ref[...] = h.astype(BF16)

    row, vec = _row_spec(tm, D), _vec_spec(D)
    if has_res:
        ins, in_specs = (x, y, gate, nw, scale, shift), [row, row, vec, vec, vec, vec]
        out_shape = [jax.ShapeDtypeStruct((L, D), F32), jax.ShapeDtypeStruct((L, D), BF16)]
        out_specs = [row, row]
    else:
        ins, in_specs = (x, nw, scale, shift), [row, vec, vec, vec]
        out_shape = [jax.ShapeDtypeStruct((L, D), BF16)]
        out_specs = [row]
    outs = _pcall(body, name=name, grid=(L // tm,), out_shape=out_shape, in_specs=in_specs,
                  out_specs=out_specs, compiler_params=_cparams(("parallel",)))(*ins)
    return outs if has_res else (x, outs[0])


def _gated_branch_bwd(dx, branch, y_ref, g_ref, dy_ref, dg_ref):
    if branch is None:
        return
    dy_ref[...] = (g_ref[...] * dx).astype(BF16)
    dg_ref[...] += jnp.sum(dx * y_ref[...], axis=0, keepdims=True)


def _norm_mod_bwd(dh, x, nw, scale, dres, name, branch=None, tm=512):
    L, D = x.shape
    tm = _tile(L, tm)
    nb = 0 if branch is None else 2

    def body(dh_ref, x_ref, nw_ref, sc_ref, dres_ref, *rest):
        y_ref, g_ref = rest[:nb] if nb else (None, None)
        dx_ref, dsh_ref, dsc_ref, dnw_ref = rest[nb:nb + 4]
        dy_ref, dg_ref = rest[nb + 4:] if nb else (None, None)

        @pl.when(pl.program_id(0) == 0)
        def _():
            dsh_ref[...] = jnp.zeros_like(dsh_ref)
            dsc_ref[...] = jnp.zeros_like(dsc_ref)
            dnw_ref[...] = jnp.zeros_like(dnw_ref)
            if nb:
                dg_ref[...] = jnp.zeros_like(dg_ref)

        xv = x_ref[...]
        dh_v = dh_ref[...]
        nw_v = nw_ref[...]
        rstd = lax.rsqrt(jnp.mean(xv * xv, axis=-1, keepdims=True) + NORM_EPS)
        xhat = xv * rstd
        dsh_ref[...] += jnp.sum(dh_v, axis=0, keepdims=True)
        dsc_ref[...] += jnp.sum(dh_v * (xhat * nw_v), axis=0, keepdims=True)
        dr = dh_v * (1.0 + sc_ref[...])
        dnw_ref[...] += jnp.sum(dr * xhat, axis=0, keepdims=True)
        dxh = dr * nw_v
        dx = rstd * (dxh - xhat * jnp.mean(dxh * xhat, axis=-1, keepdims=True)) + dres_ref[...]
        dx_ref[...] = dx
        _gated_branch_bwd(dx, branch, y_ref, g_ref, dy_ref, dg_ref)

    row, vec = _row_spec(tm, D), _vec_spec(D)
    extra_in = [] if branch is None else list(branch)
    return _pcall(
        body, name=name, grid=(L // tm,),
        out_shape=[jax.ShapeDtypeStruct((L, D), F32)] + [jax.ShapeDtypeStruct((1, D), F32)] * 3
        + ([jax.ShapeDtypeStruct((L, D), BF16), jax.ShapeDtypeStruct((1, D), F32)] if nb else []),
        in_specs=[row, row, vec, vec, row] + ([row, vec] if nb else []),
        out_specs=[row, vec, vec, vec] + ([row, vec] if nb else []),
        compiler_params=_cparams(("arbitrary",)))(dh, x, nw, scale, dres, *extra_in)


def _final_loss(x, y, gate, fw, target, name, tm=512):
    L, D = x.shape
    tm = _tile(L, tm)

    def body(x_ref, y_ref, g_ref, fw_ref, t_ref, dx_ref, loss_ref, dfw_ref, dy_ref, dg_ref):
        @pl.when(pl.program_id(0) == 0)
        def _():
            loss_ref[...] = jnp.zeros_like(loss_ref)
            dfw_ref[...] = jnp.zeros_like(dfw_ref)
            dg_ref[...] = jnp.zeros_like(dg_ref)

        xn = x_ref[...] + g_ref[...] * y_ref[...]
        fw_v = fw_ref[...]
        rstd = lax.rsqrt(jnp.mean(xn * xn, axis=-1, keepdims=True) + NORM_EPS)
        xhat = xn * rstd
        diff = xhat * fw_v - t_ref[...]
        loss_ref[...] += jnp.sum(diff * diff, axis=0, keepdims=True)
        dyf = diff * (1.0 / D)
        dfw_ref[...] += jnp.sum(dyf * xhat, axis=0, keepdims=True)
        dxh = dyf * fw_v
        dx = rstd * (dxh - xhat * jnp.mean(dxh * xhat, axis=-1, keepdims=True))
        dx_ref[...] = dx
        _gated_branch_bwd(dx, True, y_ref, g_ref, dy_ref, dg_ref)

    row, vec = _row_spec(tm, D), _vec_spec(D)
    return _pcall(
        body, name=name, grid=(L // tm,),
        out_shape=[jax.ShapeDtypeStruct((L, D), F32), jax.ShapeDtypeStruct((1, D), F32),
                   jax.ShapeDtypeStruct((1, D), F32), jax.ShapeDtypeStruct((L, D), BF16),
                   jax.ShapeDtypeStruct((1, D), F32)],
        in_specs=[row, row, vec, vec, row], out_specs=[row, vec, vec, row, vec],
        compiler_params=_cparams(("arbitrary",)))(x, y, gate, fw, target)


def _mm_nt_norm_bwd(a, w, x, nw, scale, dres, name, branch=None, blocked=False, tm=512, tk=1024, after=()):
    M = a.shape[0]
    D = x.shape[1]
    tm = _tile(M, tm)
    nb = 0 if branch is None else 2

    def epilogue(dh_v, ex, outs):
        x_ref, nw_ref, sc_ref, dres_ref = ex[:4]
        y_ref, g_ref = ex[4:] if nb else (None, None)
        dx_ref, dsh_ref, dsc_ref, dnw_ref = outs[:4]
        dy_ref, dg_ref = outs[4:] if nb else (None, None)

        @pl.when(pl.program_id(0) == 0)
        def _():
            dsh_ref[...] = jnp.zeros_like(dsh_ref)
            dsc_ref[...] = jnp.zeros_like(dsc_ref)
            dnw_ref[...] = jnp.zeros_like(dnw_ref)
            if nb:
                dg_ref[...] = jnp.zeros_like(dg_ref)

        xv = x_ref[...]
        nw_v = nw_ref[...]
        rstd = lax.rsqrt(jnp.mean(xv * xv, axis=-1, keepdims=True) + NORM_EPS)
        xhat = xv * rstd
        dsh_ref[...] += jnp.sum(dh_v, axis=0, keepdims=True)
        dsc_ref[...] += jnp.sum(dh_v * (xhat * nw_v), axis=0, keepdims=True)
        dr = dh_v * (1.0 + sc_ref[...])
        dnw_ref[...] += jnp.sum(dr * xhat, axis=0, keepdims=True)
        dxh = dr * nw_v
        dx = rstd * (dxh - xhat * jnp.mean(dxh * xhat, axis=-1, keepdims=True)) + dres_ref[...]
        dx_ref[...] = dx
        _gated_branch_bwd(dx, branch, y_ref, g_ref, dy_ref, dg_ref)

    row = pl.BlockSpec((tm, D), lambda i, j, k: (i, 0))
    vec = pl.BlockSpec((1, D), lambda i, j, k: (0, 0))
    if blocked:
        n = w.shape[2]
        grid = (M // tm, 1, N_DEV)
        a_spec = pl.BlockSpec((tm, n), lambda i, j, k: (i, k))
        b_spec = pl.BlockSpec((None, D, n), lambda i, j, k: (k, 0, 0))
    else:
        K = a.shape[1]
        tk = _tile(K, tk)
        grid = (M // tm, 1, K // tk)
        a_spec = pl.BlockSpec((tm, tk), lambda i, j, k: (i, k))
        b_spec = pl.BlockSpec((D, tk), lambda i, j, k: (0, k))
    return _mm(a, w, mode="nt", grid=grid, a_spec=a_spec, b_spec=b_spec,
               out_shape=[jax.ShapeDtypeStruct((M, D), F32)] + [jax.ShapeDtypeStruct((1, D), F32)] * 3
               + ([jax.ShapeDtypeStruct((M, D), BF16), jax.ShapeDtypeStruct((1, D), F32)] if nb else []),
               out_specs=[row, vec, vec, vec] + ([row, vec] if nb else []),
               acc_shape=(tm, D), epilogue=epilogue, name=name,
               extra=(x, nw, scale, dres) + (tuple(branch) if nb else ()),
               extra_specs=[row, vec, vec, row] + ([row, vec] if nb else []), after=after,
               semantics=("arbitrary", "arbitrary", "arbitrary"))


def _shift_down(v, s, row):
    if s == 0:
        return v
    return jnp.where(row >= s, pltpu.roll(v, s, 0), 0.0)


def _shift_up(v, s, row):
    if s == 0:
        return v
    n = v.shape[0]
    return jnp.where(row < n - s, pltpu.roll(v, n - s, 0), 0.0)


def _ssd_conv_fwd(zx, w, b, col0, width, name, cb=128):
    L = zx.shape[0]
    nb = width // cb
    off = col0 // cb

    def body(x_ref, w_ref, b_ref, o_ref):
        xv = x_ref[...]
        row = lax.broadcasted_iota(jnp.int32, xv.shape, 0)
        acc = b_ref[...] + w_ref[SSD_K - 1:SSD_K, :] * xv
        for s in range(1, SSD_K):
            acc = acc + w_ref[SSD_K - 1 - s:SSD_K - s, :] * _shift_down(xv, s, row)
        o_ref[...] = acc * _sigmoid(acc)

    return _pcall(
        body, name=name, grid=(nb,), out_shape=jax.ShapeDtypeStruct((L, width), F32),
        in_specs=[pl.BlockSpec((L, cb), lambda j: (0, off + j)),
                  pl.BlockSpec((SSD_K, cb), lambda j: (0, j)),
                  pl.BlockSpec((1, cb), lambda j: (0, j))],
        out_specs=pl.BlockSpec((L, cb), lambda j: (0, j)),
        compiler_params=_cparams(("parallel",)))(zx, w, b)


def _ssd_conv_bwd(zx, w, b, d_parts, dzx, col0, name, cb=128):
    L = zx.shape[0]
    widths = [p.shape[1] for p in d_parts]
    width = sum(widths)
    nb = width // cb
    off = col0 // cb
    starts = [sum(widths[:i]) // cb for i in range(len(d_parts))]
    counts = [wd // cb for wd in widths]

    def body(x_ref, w_ref, b_ref, *rest):
        d_refs = rest[:len(d_parts)]
        dx_ref, dw_ref, db_ref = rest[len(d_parts) + 1:]
        j = pl.program_id(0)
        d_val = d_refs[-1][...]
        for i in range(len(d_parts) - 2, -1, -1):
            d_val = jnp.where(j < starts[i + 1], d_refs[i][...], d_val)
        xv = x_ref[...]
        row = lax.broadcasted_iota(jnp.int32, xv.shape, 0)
        shifted = [_shift_down(xv, s, row) for s in range(SSD_K)]
        acc = b_ref[...] + w_ref[SSD_K - 1:SSD_K, :] * xv
        for s in range(1, SSD_K):
            acc = acc + w_ref[SSD_K - 1 - s:SSD_K - s, :] * shifted[s]
        sig = _sigmoid(acc)
        dpre = d_val * (sig * (1.0 + acc * (1.0 - sig)))
        db_ref[...] = jnp.sum(dpre, axis=0, keepdims=True)
        dx = w_ref[SSD_K - 1:SSD_K, :] * dpre
        for s in range(SSD_K):
            dw_ref[SSD_K - 1 - s:SSD_K - s, :] = jnp.sum(dpre * shifted[s], axis=0, keepdims=True)
            if s:
                dx = dx + w_ref[SSD_K - 1 - s:SSD_K - s, :] * _shift_up(dpre, s, row)
        dx_ref[...] = dx.astype(BF16)

    def part_spec(i):
        return pl.BlockSpec((L, cb), lambda j: (0, jnp.clip(j - starts[i], 0, counts[i] - 1)))

    return _pcall(
        body, name=name, grid=(nb,),
        out_shape=[jax.ShapeDtypeStruct(dzx.shape, BF16), jax.ShapeDtypeStruct((SSD_K, width), F32),
                   jax.ShapeDtypeStruct((1, width), F32)],
        in_specs=[pl.BlockSpec((L, cb), lambda j: (0, off + j)),
                  pl.BlockSpec((SSD_K, cb), lambda j: (0, j)),
                  pl.BlockSpec((1, cb), lambda j: (0, j))]
        + [part_spec(i) for i in range(len(d_parts))] + [pl.BlockSpec(memory_space=pl.ANY)],
        out_specs=[pl.BlockSpec((L, cb), lambda j: (0, off + j)),
                   pl.BlockSpec((SSD_K, cb), lambda j: (0, j)),
                   pl.BlockSpec((1, cb), lambda j: (0, j))],
        input_output_aliases={3 + len(d_parts): 0},
        compiler_params=_cparams(("parallel",)))(zx, w, b, *d_parts, dzx)


def _dzx_finish(dzx, ddt, col0, name, tl=512):
    G, L, _ = ddt.shape
    tail = dzx.shape[1] - col0
    tl = _tile(L, tl)

    def body(ddt_ref, dzx_ref, o_ref):
        s = ddt_ref[0]
        for g in range(1, G):
            s = s + ddt_ref[g]
        o_ref[:, 0:LANES] = s.astype(o_ref.dtype)
        if tail > LANES:
            o_ref[:, LANES:] = jnp.zeros((tl, tail - LANES), o_ref.dtype)

    return _pcall(
        body, name=name, grid=(L // tl,), out_shape=jax.ShapeDtypeStruct(dzx.shape, dzx.dtype),
        in_specs=[pl.BlockSpec((G, tl, LANES), lambda i: (0, i, 0)), pl.BlockSpec(memory_space=pl.ANY)],
        out_specs=pl.BlockSpec((tl, tail), lambda i: (i, col0 // tail)),
        input_output_aliases={1: 0},
        compiler_params=_cparams(("parallel",)))(ddt, dzx)


def _sc_conv_fwd(proj, w, name, cb=128):
    L = proj.shape[0]
    width = proj.shape[1] // 3
    nb = width // cb

    def body(b_ref, c_ref, x_ref, w_ref, o_ref):
        q = c_ref[...] * x_ref[...]
        row = lax.broadcasted_iota(jnp.int32, q.shape, 0)
        acc = w_ref[SC_K - 1:SC_K, :] * q
        for s in range(1, SC_K):
            acc = acc + w_ref[SC_K - 1 - s:SC_K - s, :] * _shift_down(q, s, row)
        o_ref[...] = (b_ref[...] * acc).astype(BF16)

    return _pcall(
        body, name=name, grid=(nb,), out_shape=jax.ShapeDtypeStruct((L, width), BF16),
        in_specs=[pl.BlockSpec((L, cb), lambda j: (0, j)),
                  pl.BlockSpec((L, cb), lambda j: (0, nb + j)),
                  pl.BlockSpec((L, cb), lambda j: (0, 2 * nb + j)),
                  pl.BlockSpec((SC_K, cb), lambda j: (0, j))],
        out_specs=pl.BlockSpec((L, cb), lambda j: (0, j)),
        compiler_params=_cparams(("parallel",)))(proj, proj, proj, w)


def _sc_conv_bwd(proj, w, dy, name, cb=128):
    L = proj.shape[0]
    width = proj.shape[1] // 3
    nb = width // cb

    def body(b_ref, c_ref, x_ref, w_ref, dy_ref, db_ref, dc_ref, dxv_ref, dw_ref):
        cg, xv, dyv = c_ref[...], x_ref[...], dy_ref[...]
        q = cg * xv
        row = lax.broadcasted_iota(jnp.int32, q.shape, 0)
        shifted = [_shift_down(q, s, row) for s in range(SC_K)]
        conv = w_ref[SC_K - 1:SC_K, :] * q
        for s in range(1, SC_K):
            conv = conv + w_ref[SC_K - 1 - s:SC_K - s, :] * shifted[s]
        db_ref[...] = (dyv * conv).astype(BF16)
        dconv = dyv * b_ref[...]
        dq = w_ref[SC_K - 1:SC_K, :] * dconv
        for s in range(SC_K):
            dw_ref[SC_K - 1 - s:SC_K - s, :] = jnp.sum(dconv * shifted[s], axis=0, keepdims=True)
            if s:
                dq = dq + w_ref[SC_K - 1 - s:SC_K - s, :] * _shift_up(dconv, s, row)
        dc_ref[...] = (dq * xv).astype(BF16)
        dxv_ref[...] = (dq * cg).astype(BF16)

    blk = pl.BlockSpec((L, cb), lambda j: (0, j))
    wblk = pl.BlockSpec((SC_K, cb), lambda j: (0, j))
    return _pcall(
        body, name=name, grid=(nb,),
        out_shape=[jax.ShapeDtypeStruct((L, width), BF16)] * 3 + [jax.ShapeDtypeStruct((SC_K, width), F32)],
        in_specs=[blk, pl.BlockSpec((L, cb), lambda j: (0, nb + j)),
                  pl.BlockSpec((L, cb), lambda j: (0, 2 * nb + j)), wblk, blk],
        out_specs=[blk, blk, blk, wblk],
        compiler_params=_cparams(("parallel",)))(proj, proj, proj, w, dy)


def _split3(v):
    hi = v.astype(BF16)
    r1 = v - hi.astype(F32)
    mid = r1.astype(BF16)
    lo = (r1 - mid.astype(F32)).astype(BF16)
    return hi, mid, lo


def _dot_exact01(t01, v):
    hi, mid, lo = _split3(v)
    return _dot(t01, hi) + _dot(t01, mid) + _dot(t01, lo)


def _lane_col(v, lane, h):
    return jnp.sum(jnp.where(lane == h, v, 0.0), axis=1, keepdims=True)


def _sum_all(v):
    return jnp.sum(jnp.sum(v, axis=1, keepdims=True), axis=0, keepdims=True)


def _softplus(x):
    return jnp.maximum(x, 0.0) + jnp.log1p(jnp.exp(-jnp.abs(x)))


def _ssd_decay(zx, bias_p, alog_p, n_heads, dt_block, name):
    L = zx.shape[0]
    nc = L // SSD_CHUNK

    def body(raw_ref, bias_ref, alog_ref, dt_ref, sg_ref, cs_ref, cst_ref, last_ref):
        lane = lax.broadcasted_iota(jnp.int32, (SSD_CHUNK, LANES), 1)
        row = lax.broadcasted_iota(jnp.int32, (SSD_CHUNK, LANES), 0)
        valid = lane < n_heads
        raw = raw_ref[...] + bias_ref[...]
        dt = jnp.where(valid, _softplus(raw), 0.0)
        a = dt * (-jnp.exp(alog_ref[...]))
        cs = _dot_exact01((row >= lane).astype(BF16), a)
        dt_ref[...] = dt
        sg_ref[...] = _sigmoid(raw)
        cs_ref[...] = cs
        cst_ref[...] = cs.T
        last_ref[...] = jnp.sum(a, axis=0, keepdims=True)

    blk = pl.BlockSpec((SSD_CHUNK, LANES), lambda c: (c, 0))
    head_vec = pl.BlockSpec((1, LANES), lambda c: (0, 0))
    return _pcall(
        body, name=name, grid=(nc,),
        out_shape=[jax.ShapeDtypeStruct((L, LANES), F32)] * 3
        + [jax.ShapeDtypeStruct((nc, SSD_CHUNK, LANES), F32), jax.ShapeDtypeStruct((nc, 1, LANES), F32)],
        in_specs=[pl.BlockSpec((SSD_CHUNK, LANES), lambda c: (c, dt_block)), head_vec, head_vec],
        out_specs=[blk, blk, blk, pl.BlockSpec((None, SSD_CHUNK, LANES), lambda c: (c, 0, 0)),
                   pl.BlockSpec((None, 1, LANES), lambda c: (c, 0, 0))],
        compiler_params=_cparams(("parallel",)))(zx, bias_p, alog_p)


def _ssd_common(dt_ref, cs_ref, last_ref, b_ref, c_ref):
    c_sz = SSD_CHUNK
    lane = lax.broadcasted_iota(jnp.int32, (c_sz, LANES), 1)
    row = lax.broadcasted_iota(jnp.int32, (c_sz, LANES), 0)
    bb = b_ref[...].astype(BF16)
    cb = c_ref[...].astype(BF16)
    scores = _dot(cb, bb, "nt")
    return dict(lane=lane, row=row, dt=dt_ref[...], cs=cs_ref[...], last_row=last_ref[...], bb=bb, cb=cb,
                scores=scores, causal=row >= lane, lo=lane < SSD_P)


def _pair_terms(q, cst_ref, h0):
    lane, lo = q["lane"], q["lo"]
    out = {}
    cols, dts, lasts, lms = [], [], [], []
    lane1 = lax.broadcasted_iota(jnp.int32, (1, LANES), 1)
    for h in (h0, h0 + 1):
        col = _lane_col(q["cs"], lane, h)
        rowv = cst_ref[pl.ds(h, 1), :]
        lms.append(jnp.exp(jnp.where(q["causal"], col - rowv, -1e30)))
        cols.append(col)
        dts.append(_lane_col(q["dt"], lane, h))
        lasts.append(jnp.sum(jnp.where(lane1 == h, q["last_row"], 0.0), axis=1, keepdims=True))
    out["lm"] = lms
    out["cols"] = cols
    out["lasts"] = lasts
    out["dt_b"] = jnp.where(lo, dts[0], dts[1])
    out["e_b"] = jnp.where(lo, jnp.exp(cols[0]), jnp.exp(cols[1]))
    out["dec_cols"] = [jnp.exp(lasts[0] - cols[0]), jnp.exp(lasts[1] - cols[1])]
    out["dec_b"] = jnp.where(lo, out["dec_cols"][0], out["dec_cols"][1])
    lo1 = lane1 < SSD_P
    out["explast"] = [jnp.exp(lasts[0]), jnp.exp(lasts[1])]
    out["explast_b"] = jnp.where(lo1, out["explast"][0], out["explast"][1])
    return out


def _ssd_fwd(zx, xc, decay, d_lane, nw, d_inner, after, name):
    L = zx.shape[0]
    nc = L // SSD_CHUNK
    gw = d_inner // SSD_G
    heads = gw // SSD_P
    n_pair = heads // 2
    bc0 = d_inner // LANES

    def body(z_ref, xs_ref, b_ref, c_ref, dt_ref, cs_ref, cst_ref, last_ref, dl_ref, nw_ref, after_ref,
             y_ref, yn_ref, prev_ref, s_ref):
        @pl.when(pl.program_id(1) == 0)
        def _():
            s_ref[...] = jnp.zeros_like(s_ref)

        q = _ssd_common(dt_ref, cs_ref, last_ref, b_ref, c_ref)
        prev_ref[...] = s_ref[...]
        lo = q["lo"]
        for j in range(n_pair):
            sl = slice(j * LANES, (j + 1) * LANES)
            p = _pair_terms(q, cst_ref, pl.program_id(0) * heads + 2 * j)
            xs_p = xs_ref[:, sl]
            xp = xs_p * p["dt_b"]
            xb = xp.astype(BF16)
            m_a = (q["scores"] * p["lm"][0]).astype(BF16)
            m_b = (q["scores"] * p["lm"][1]).astype(BF16)
            yd = jnp.where(lo, _dot(m_a, xb), _dot(m_b, xb))
            s_p = s_ref[:, sl]
            yo = _dot(q["cb"], s_p.astype(BF16)) * p["e_b"]
            y_ref[:, sl] = yd + yo + dl_ref[:, sl] * xs_p
            st = _dot(q["bb"], (xp * p["dec_b"]).astype(BF16), "tn")
            s_ref[:, sl] = s_p * p["explast_b"] + st
        yv = y_ref[...]
        zv = z_ref[...]
        yg = yv * (zv * _sigmoid(zv))
        rstd = lax.rsqrt(jnp.mean(yg * yg, axis=-1, keepdims=True) + NORM_EPS)
        yn_ref[...] = (yg * rstd * nw_ref[...]).astype(BF16)

    grp = lambda width: pl.BlockSpec((None, 1, width), lambda g, c: (g, 0, 0))
    dt_, _, cs_, cst_, last_ = decay
    return _pcall(
        body, name=name, grid=(SSD_G, nc),
        out_shape=[jax.ShapeDtypeStruct((L, d_inner), F32), jax.ShapeDtypeStruct((L, d_inner), BF16),
                   jax.ShapeDtypeStruct((nc, SSD_G, SSD_N, gw), F32)],
        in_specs=[pl.BlockSpec((SSD_CHUNK, gw), lambda g, c: (c, g)),
                  pl.BlockSpec((SSD_CHUNK, gw), lambda g, c: (c, g)),
                  pl.BlockSpec((SSD_CHUNK, SSD_N), lambda g, c: (c, bc0 + g)),
                  pl.BlockSpec((SSD_CHUNK, SSD_N), lambda g, c: (c, bc0 + SSD_G + g)),
                  pl.BlockSpec((SSD_CHUNK, LANES), lambda g, c: (c, 0)),
                  pl.BlockSpec((SSD_CHUNK, LANES), lambda g, c: (c, 0)),
                  pl.BlockSpec((None, SSD_CHUNK, LANES), lambda g, c: (c, 0, 0)),
                  pl.BlockSpec((None, 1, LANES), lambda g, c: (c, 0, 0)),
                  grp(gw), grp(gw), pl.BlockSpec(memory_space=pl.ANY)],
        out_specs=[pl.BlockSpec((SSD_CHUNK, gw), lambda g, c: (c, g)),
                   pl.BlockSpec((SSD_CHUNK, gw), lambda g, c: (c, g)),
                   pl.BlockSpec((None, None, SSD_N, gw), lambda g, c: (c, g, 0, 0))],
        scratch_shapes=[pltpu.VMEM((SSD_N, gw), F32)],
        compiler_params=_cparams(("parallel", "arbitrary")))(
            zx, xc, xc, xc, dt_, cs_, cst_, last_, d_lane, nw, after)


def _ssd_bwd(dyn, y, zx, xc, prev, decay, alog_p, d_lane, nw, d_inner, name):
    L = zx.shape[0]
    nc = L // SSD_CHUNK
    gw = d_inner // SSD_G
    heads = gw // SSD_P
    n_pair = heads // 2
    bc0 = d_inner // LANES

    def body(dyn_ref, y_ref, z_ref, xs_ref, b_ref, c_ref, prev_ref, dt_ref, sg_ref, cs_ref, cst_ref, last_ref,
             alog_ref, dl_ref, nw_ref,
             dz_ref, dxs_ref, db_ref, dc_ref, ddt_ref, dbias_ref, dalog_ref, dd_ref, dnw_ref,
             ds_ref, racc_ref):
        @pl.when(pl.program_id(1) == 0)
        def _():
            ds_ref[...] = jnp.zeros_like(ds_ref)
            dbias_ref[...] = jnp.zeros_like(dbias_ref)
            dalog_ref[...] = jnp.zeros_like(dalog_ref)
            dd_ref[...] = jnp.zeros_like(dd_ref)
            dnw_ref[...] = jnp.zeros_like(dnw_ref)

        q = _ssd_common(dt_ref, cs_ref, last_ref, b_ref, c_ref)
        a_row = -jnp.exp(alog_ref[...])
        lane, row, lo = q["lane"], q["row"], q["lo"]
        lane1 = lax.broadcasted_iota(jnp.int32, (1, LANES), 1)
        head0 = pl.program_id(0) * heads
        mine = (lane >= head0) & (lane < head0 + heads)

        yv, zv, dynv, nwv = y_ref[...], z_ref[...], dyn_ref[...], nw_ref[...]
        sig = _sigmoid(zv)
        sz = zv * sig
        yg = yv * sz
        rstd = lax.rsqrt(jnp.mean(yg * yg, axis=-1, keepdims=True) + NORM_EPS)
        yhat = yg * rstd
        dnw_ref[...] += jnp.sum(dynv * yhat, axis=0, keepdims=True)
        dyh = dynv * nwv
        dyg = rstd * (dyh - yhat * jnp.mean(dyh * yhat, axis=-1, keepdims=True))
        dz_ref[...] = (dyg * yv * (sig * (1.0 + zv * (1.0 - sig)))).astype(BF16)
        dy_all = dyg * sz

        dg = jnp.zeros((SSD_CHUNK, SSD_CHUNK), F32)
        dc_acc = jnp.zeros((SSD_CHUNK, SSD_N), F32)
        db_acc = jnp.zeros((SSD_CHUNK, SSD_N), F32)
        dcs_mat = jnp.zeros((SSD_CHUNK, LANES), F32)
        ddt_mat = jnp.zeros((SSD_CHUNK, LANES), F32)
        dd_row = jnp.zeros((1, LANES), F32)
        racc_ref[...] = jnp.zeros_like(racc_ref)
        is_last = row == SSD_CHUNK - 1

        for j in range(n_pair):
            sl = slice(j * LANES, (j + 1) * LANES)
            ha, hb = head0 + 2 * j, head0 + 2 * j + 1
            p = _pair_terms(q, cst_ref, ha)
            xs_p = xs_ref[:, sl]
            dyp = dy_all[:, sl]
            xp = xs_p * p["dt_b"]
            xb = xp.astype(BF16)
            s_p = prev_ref[:, sl]
            s_pb = s_p.astype(BF16)
            dsn = ds_ref[:, sl]
            dsnb = dsn.astype(BF16)
            m_f = [q["scores"] * p["lm"][0], q["scores"] * p["lm"][1]]

            t0 = dyp * xs_p
            dd_row = dd_row + jnp.where(lane1 == ha, _sum_all(jnp.where(lo, t0, 0.0)), 0.0) \
                + jnp.where(lane1 == hb, _sum_all(jnp.where(lo, 0.0, t0)), 0.0)
            dxs_p = dl_ref[:, sl] * dyp

            yo = _dot(q["cb"], s_pb) * p["e_b"]
            dcs_b = (dyp * p["e_b"]).astype(BF16)
            dc_acc = dc_acc + _dot(dcs_b, s_pb, "nt")
            ds_yo = _dot(q["cb"], dcs_b, "tn")
            t1 = dyp * yo
            dcs_cols = [jnp.sum(jnp.where(lo, t1, 0.0), axis=1, keepdims=True),
                        jnp.sum(jnp.where(lo, 0.0, t1), axis=1, keepdims=True)]

            t2 = dsn * s_p
            dlast = [p["explast"][0] * _sum_all(jnp.where(lo, t2, 0.0)),
                     p["explast"][1] * _sum_all(jnp.where(lo, 0.0, t2))]
            ds_ref[:, sl] = dsn * p["explast_b"] + ds_yo
            w = _dot(q["bb"], dsnb)
            db_acc = db_acc + _dot((xp * p["dec_b"]).astype(BF16), dsnb, "nt")
            dxp = w * p["dec_b"]
            t3 = w * xp
            e = [jnp.sum(jnp.where(lo, t3, 0.0), axis=1, keepdims=True) * p["dec_cols"][0],
                 jnp.sum(jnp.where(lo, 0.0, t3), axis=1, keepdims=True) * p["dec_cols"][1]]
            for i in range(2):
                dlast[i] = dlast[i] + jnp.sum(e[i], axis=0, keepdims=True)
                dcs_cols[i] = dcs_cols[i] - e[i]

            dyb = dyp.astype(BF16)
            dy_h = [jnp.where(lo, dyp, 0.0).astype(BF16), jnp.where(lo, 0.0, dyp).astype(BF16)]
            dms = [_dot(dy_h[0], xb, "nt"), _dot(dy_h[1], xb, "nt")]
            dxp = dxp + jnp.where(lo, _dot(m_f[0].astype(BF16), dyb, "tn"), _dot(m_f[1].astype(BF16), dyb, "tn"))
            for i, h in enumerate((ha, hb)):
                dg = dg + dms[i] * p["lm"][i]
                qm = dms[i] * m_f[i]
                dcs_cols[i] = dcs_cols[i] + jnp.sum(qm, axis=1, keepdims=True)
                racc_ref[pl.ds(h, 1), :] = jnp.sum(qm, axis=0, keepdims=True)

            dxs_ref[:, sl] = dxs_p + dxp * p["dt_b"]
            t4 = dxp * xs_p
            ddt_cols = [jnp.sum(jnp.where(lo, t4, 0.0), axis=1, keepdims=True),
                        jnp.sum(jnp.where(lo, 0.0, t4), axis=1, keepdims=True)]
            for i, h in enumerate((ha, hb)):
                sel = lane == h
                dcs_mat = dcs_mat + jnp.where(sel, dcs_cols[i], 0.0) + jnp.where(sel & is_last, dlast[i], 0.0)
                ddt_mat = ddt_mat + jnp.where(sel, ddt_cols[i], 0.0)

        dcs_mat = dcs_mat - racc_ref[...].T
        tri_t = (row <= lane).astype(BF16)
        da = _dot_exact01(tri_t, dcs_mat)
        ddt = ddt_mat + da * a_row
        dalog_ref[...] += jnp.sum(jnp.where(mine, da * q["dt"], 0.0), axis=0, keepdims=True) * a_row
        draw = jnp.where(mine, ddt * sg_ref[...], 0.0)
        ddt_ref[...] = draw
        dbias_ref[...] += jnp.sum(draw, axis=0, keepdims=True)
        dd_ref[...] += dd_row
        dgb = dg.astype(BF16)
        dc_ref[...] = dc_acc + _dot(dgb, q["bb"])
        db_ref[...] = db_acc + _dot(dgb, q["cb"], "tn")

    rev = lambda c: nc - 1 - c
    grp = lambda width: pl.BlockSpec((None, 1, width), lambda g, c: (g, 0, 0))
    blk = lambda width, off: pl.BlockSpec((SSD_CHUNK, width), lambda g, c: (rev(c), off + g))
    head_vec = pl.BlockSpec((1, LANES), lambda g, c: (0, 0))
    chunk_rows = pl.BlockSpec((SSD_CHUNK, LANES), lambda g, c: (rev(c), 0))
    dt_, sg_, cs_, cst_, last_ = decay
    return _pcall(
        body, name=name, grid=(SSD_G, nc),
        out_shape=[jax.ShapeDtypeStruct(zx.shape, BF16), jax.ShapeDtypeStruct((L, d_inner), F32),
                   jax.ShapeDtypeStruct((L, SSD_G * SSD_N), F32), jax.ShapeDtypeStruct((L, SSD_G * SSD_N), F32),
                   jax.ShapeDtypeStruct((SSD_G, L, LANES), F32),
                   jax.ShapeDtypeStruct((SSD_G, 1, LANES), F32), jax.ShapeDtypeStruct((SSD_G, 1, LANES), F32),
                   jax.ShapeDtypeStruct((SSD_G, 1, LANES), F32), jax.ShapeDtypeStruct((SSD_G, 1, gw), F32)],
        in_specs=[blk(gw, 0), blk(gw, 0), blk(gw, 0), blk(gw, 0), blk(SSD_N, bc0), blk(SSD_N, bc0 + SSD_G),
                  pl.BlockSpec((None, None, SSD_N, gw), lambda g, c: (rev(c), g, 0, 0)),
                  chunk_rows, chunk_rows, chunk_rows,
                  pl.BlockSpec((None, SSD_CHUNK, LANES), lambda g, c: (rev(c), 0, 0)),
                  pl.BlockSpec((None, 1, LANES), lambda g, c: (rev(c), 0, 0)),
                  head_vec, grp(gw), grp(gw)],
        out_specs=[blk(gw, 0), blk(gw, 0), blk(SSD_N, 0), blk(SSD_N, 0),
                   pl.BlockSpec((None, SSD_CHUNK, LANES), lambda g, c: (g, rev(c), 0)),
                   grp(LANES), grp(LANES), grp(LANES), grp(gw)],
        scratch_shapes=[pltpu.VMEM((SSD_N, gw), F32), pltpu.VMEM((SSD_CHUNK, LANES), F32)],
        compiler_params=_cparams(("parallel", "arbitrary")))(
            dyn, y, zx, xc, xc, xc, prev, dt_, sg_, cs_, cst_, last_, alog_p, d_lane, nw)


def _cond_mod(c_pad, ada_w, ada_b_loc, after, name):
    depth, D, n = ada_w.shape
    rows = c_pad.shape[0]

    def body(c_ref, w_ref, b_ref, after_ref, mod_ref, cond_ref):
        cv = c_ref[...]
        cond = cv * _sigmoid(cv)
        cond_ref[...] = cond
        mod_ref[...] = _dot(cond.astype(BF16), w_ref[...].astype(BF16)) + b_ref[...]

    return _pcall(
        body, name=name, grid=(depth,),
        out_shape=[jax.ShapeDtypeStruct((depth, rows, n), F32), jax.ShapeDtypeStruct((rows, D), F32)],
        in_specs=[pl.BlockSpec((rows, D), lambda i: (0, 0)),
                  pl.BlockSpec((None, D, n), lambda i: (i, 0, 0)),
                  pl.BlockSpec((None, 1, n), lambda i: (i, 0, 0)),
                  pl.BlockSpec(memory_space=pl.ANY)],
        out_specs=[pl.BlockSpec((None, rows, n), lambda i: (i, 0, 0)),
                   pl.BlockSpec((rows, D), lambda i: (0, 0))],
        compiler_params=_cparams(("arbitrary",)))(c_pad, ada_w, ada_b_loc, after)


def _adamw_math(g, w, m, v):
    m_new = ADAM_B1 * m + (1.0 - ADAM_B1) * g
    v_new = ADAM_B2 * v + (1.0 - ADAM_B2) * (g * g)
    m_hat = m_new / (1.0 - ADAM_B1 ** ADAM_STEP)
    v_hat = v_new / (1.0 - ADAM_B2 ** ADAM_STEP)
    delta = -ADAM_LR * (m_hat / (jnp.sqrt(v_hat) + ADAM_EPS) + ADAM_WD * w)
    return delta, m_new, v_new


def _adamw_sum(parts, w, m, v, layer, name, prev=None, tr=None, window_off=None):
    depth, R, C = w.shape
    tr = _tile(R, tr if tr is not None else (512 if C <= 512 else 256))
    win = parts.shape[2]
    scratch = [] if window_off is None else [pltpu.VMEM((tr, win), F32)]

    def body(p_ref, w_ref, m_ref, v_ref, *rest):
        g_ref, d_ref, mo_ref, vo_ref = rest[-4 - len(scratch):len(rest) - len(scratch)]
        g = p_ref[0].astype(F32)
        for k in range(1, N_DEV):
            g = g + p_ref[k].astype(F32)
        if window_off is not None:
            me = _my_index()
            off = 0
            for k in range(N_DEV):
                off = jnp.where(me == k, window_off[k], off)
            src = lax.broadcasted_iota(jnp.int32, (win, win), 0)
            dst = lax.broadcasted_iota(jnp.int32, (win, win), 1)
            shift = ((src == dst + off) & (dst < C)).astype(BF16)
            hi, mid, lo = _split3(g)
            rest[-1][...] = _dot(hi, shift) + _dot(mid, shift) + _dot(lo, shift)
            g = rest[-1][:, 0:C]
        d, mn, vn = _adamw_math(g, w_ref[...], m_ref[...], v_ref[...])
        g_ref[...] = g
        d_ref[...] = d
        mo_ref[...] = mn
        vo_ref[...] = vn

    blk = pl.BlockSpec((None, tr, C), lambda i: (layer, i, 0))
    prev = list(prev) if prev is not None else []
    return _pcall(
        body, name=name, grid=(R // tr,),
        out_shape=[jax.ShapeDtypeStruct((depth, R, C), F32)] * 4,
        in_specs=[pl.BlockSpec((N_DEV, tr, win), lambda i: (0, i, 0)), blk, blk, blk]
        + [pl.BlockSpec(memory_space=pl.ANY)] * len(prev),
        out_specs=[blk] * 4, input_output_aliases={4 + k: k for k in range(len(prev))},
        scratch_shapes=scratch,
        compiler_params=_cparams(("parallel",)))(parts, w, m, v, *prev)


def _adamw_small(parts, wmv, head_parts, head_wmv, loss_parts, name):
    n, nh = len(parts), len(head_parts)
    n_heads = head_wmv[0][0].shape[1] if nh else 0
    groups = head_parts[0].shape[1] if nh else 0
    d_model = loss_parts.shape[2]

    def body(*refs):
        p_refs, refs = refs[:n], refs[n:]
        wmv_refs, refs = refs[:3 * n], refs[3 * n:]
        hp_refs, refs = refs[:nh], refs[nh:]
        hwmv_refs, refs = refs[:3 * nh], refs[3 * nh:]
        loss_ref, refs = refs[0], refs[1:]
        outs, loss_out, head_scr = refs[:4 * (n + nh)], refs[4 * (n + nh)], refs[4 * (n + nh) + 1]

        def update(i, g, w_ref, m_ref, v_ref):
            res = (g,) + _adamw_math(g, w_ref[...], m_ref[...], v_ref[...])
            for o_ref, r in zip(outs[4 * i:4 * i + 4], res):
                o_ref[...] = r

        for i in range(n):
            g = p_refs[i][0]
            for k in range(1, N_DEV):
                g = g + p_refs[i][k]
            update(i, g, *wmv_refs[3 * i:3 * i + 3])
        for i in range(nh):
            g = None
            for k in range(N_DEV):
                for grp in range(groups):
                    g = hp_refs[i][k, grp] if g is None else g + hp_refs[i][k, grp]
            head_scr[...] = g
            update(n + i, head_scr[:, 0:n_heads], *hwmv_refs[3 * i:3 * i + 3])
        tot = loss_ref[0]
        for k in range(1, N_DEV):
            tot = tot + loss_ref[k]
        loss_out[...] = jnp.broadcast_to(_sum_all(tot) * (0.5 / d_model), loss_out.shape)

    operands = list(parts) + [a for t in wmv for a in t] + list(head_parts) + [a for t in head_wmv for a in t]
    operands.append(loss_parts)
    out_shape = [jax.ShapeDtypeStruct(t[0].shape, F32) for t in list(wmv) + list(head_wmv) for _ in range(4)]
    out_shape.append(jax.ShapeDtypeStruct((1, LANES), F32))
    vmem = pl.BlockSpec(memory_space=pltpu.VMEM)
    outs = _pcall(body, name=name, out_shape=out_shape, in_specs=[vmem] * len(operands),
                  out_specs=[vmem] * len(out_shape), scratch_shapes=[pltpu.VMEM((1, LANES), F32)],
                  compiler_params=_cparams())(*operands)
    return [outs[4 * i:4 * i + 4] for i in range(n + nh)], outs[-1]


def _ada_adamw(cond_pad, dmod_pad, w, m, v, name, tr=512):
    depth, D, n = w.shape
    rows = cond_pad.shape[0]
    tr = _tile(D, tr)

    def body(c_ref, dm_ref, w_ref, m_ref, v_ref, g_ref, d_ref, mo_ref, vo_ref):
        g = _dot(c_ref[...].astype(BF16), dm_ref[...].astype(BF16), "tn")
        d, mn, vn = _adamw_math(g, w_ref[...], m_ref[...], v_ref[...])
        g_ref[...] = g
        d_ref[...] = d
        mo_ref[...] = mn
        vo_ref[...] = vn

    blk = pl.BlockSpec((None, tr, n), lambda i, r: (i, r, 0))
    return _pcall(
        body, name=name, grid=(depth, D // tr),
        out_shape=[jax.ShapeDtypeStruct((depth, D, n), F32)] * 4,
        in_specs=[pl.BlockSpec((rows, tr), lambda i, r: (0, r)),
                  pl.BlockSpec((None, rows, n), lambda i, r: (i, 0, 0)), blk, blk, blk],
        out_specs=[blk] * 4, compiler_params=_cparams(("parallel", "parallel")))(cond_pad, dmod_pad, w, m, v)


def kernel(x, c, ada_w, ada_b, mix_norm_w, mlp_norm_w, mlp_up, mlp_down, ssd_in_w, ssd_conv_w, ssd_conv_b, ssd_dt_bias, ssd_A_log, ssd_D, ssd_norm_w, ssd_out_w, sc_in_w, sc_conv_w, sc_out_w, final_norm_w, loss_target, m_ada_w, m_ada_b, m_mix_norm_w, m_mlp_norm_w, m_mlp_up, m_mlp_down, m_ssd_in_w, m_ssd_conv_w, m_ssd_conv_b, m_ssd_dt_bias, m_ssd_A_log, m_ssd_D, m_ssd_norm_w, m_ssd_out_w, m_sc_in_w, m_sc_conv_w, m_sc_out_w, m_final_norm_w, v_ada_w, v_ada_b, v_mix_norm_w, v_mlp_norm_w, v_mlp_up, v_mlp_down, v_ssd_in_w, v_ssd_conv_w, v_ssd_conv_b, v_ssd_dt_bias, v_ssd_A_log, v_ssd_D, v_ssd_norm_w, v_ssd_out_w, v_sc_in_w, v_sc_conv_w, v_sc_out_w, v_final_norm_w):
    weights = dict(ada_w=ada_w, ada_b=ada_b, mix_norm_w=mix_norm_w, mlp_norm_w=mlp_norm_w, mlp_up=mlp_up,
                   mlp_down=mlp_down, ssd_in_w=ssd_in_w, ssd_conv_w=ssd_conv_w, ssd_conv_b=ssd_conv_b,
                   ssd_dt_bias=ssd_dt_bias, ssd_A_log=ssd_A_log, ssd_D=ssd_D, ssd_norm_w=ssd_norm_w,
                   ssd_out_w=ssd_out_w, sc_in_w=sc_in_w, sc_conv_w=sc_conv_w, sc_out_w=sc_out_w,
                   final_norm_w=final_norm_w)
    moms = dict(ada_w=m_ada_w, ada_b=m_ada_b, mix_norm_w=m_mix_norm_w, mlp_norm_w=m_mlp_norm_w, mlp_up=m_mlp_up,
                mlp_down=m_mlp_down, ssd_in_w=m_ssd_in_w, ssd_conv_w=m_ssd_conv_w, ssd_conv_b=m_ssd_conv_b,
                ssd_dt_bias=m_ssd_dt_bias, ssd_A_log=m_ssd_A_log, ssd_D=m_ssd_D, ssd_norm_w=m_ssd_norm_w,
                ssd_out_w=m_ssd_out_w, sc_in_w=m_sc_in_w, sc_conv_w=m_sc_conv_w, sc_out_w=m_sc_out_w,
                final_norm_w=m_final_norm_w)
    vars_ = dict(ada_w=v_ada_w, ada_b=v_ada_b, mix_norm_w=v_mix_norm_w, mlp_norm_w=v_mlp_norm_w, mlp_up=v_mlp_up,
                 mlp_down=v_mlp_down, ssd_in_w=v_ssd_in_w, ssd_conv_w=v_ssd_conv_w, ssd_conv_b=v_ssd_conv_b,
                 ssd_dt_bias=v_ssd_dt_bias, ssd_A_log=v_ssd_A_log, ssd_D=v_ssd_D, ssd_norm_w=v_ssd_norm_w,
                 ssd_out_w=v_ssd_out_w, sc_in_w=v_sc_in_w, sc_conv_w=v_sc_conv_w, sc_out_w=v_sc_out_w,
                 final_norm_w=v_final_norm_w)
    names = list(weights)

    L, D = x.shape[1], x.shape[2]
    d_inner = 2 * D
    n_heads = d_inner // SSD_P
    hpg = n_heads // SSD_G
    gw = d_inner // SSD_G
    conv_dim = d_inner + 2 * SSD_G * SSD_N
    zx_dim = d_inner + conv_dim
    zx_pad = -(-(zx_dim + LANES) // 512) * 512
    in_ws = ssd_in_w.shape[2]
    in_base, in_off, in_win = _window_geometry(in_ws)
    me = _my_index()
    x0 = x[0]
    tgt = loss_target[0]

    n_mod = ada_w.shape[2]
    (c_all,) = _exchange([c], "gather_c", gather=True)
    gather_handle = {}
    (gather_handle["ssd_in_w"],), token_in = _xfer_start(
        [ssd_in_w[0].astype(BF16)], "gather_start_ssd_in_w", gather=True, via_sibling=(0,), after=(c_all,))
    c_pad = jnp.pad(c_all.reshape(N_DEV, D), ((0, 16 - N_DEV), (0, 0)))
    ada_b_loc = lax.dynamic_slice_in_dim(ada_b, me * n_mod, n_mod, axis=1).reshape(2, 1, n_mod)
    mod_blk, cond_pad = _cond_mod(c_pad, ada_w, ada_b_loc, token_in, "cond_mod")
    gather_order = ["mod", "ssd_conv_w", "sc_conv_w", "ssd_out_w", "up0", "down0", "sc_in_w", "sc_out_w", "up1",
                    "down1"]
    gather_src = dict(mod=mod_blk, ssd_conv_w=ssd_conv_w[0], sc_conv_w=sc_conv_w[0],
                      ssd_out_w=ssd_out_w[0].astype(BF16),
                      up0=mlp_up[0].astype(BF16), down0=mlp_down[0].astype(BF16),
                      sc_in_w=sc_in_w[0].astype(BF16), sc_out_w=sc_out_w[0].astype(BF16),
                      up1=mlp_up[1].astype(BF16), down1=mlp_down[1].astype(BF16))
    handles, gather_token = _xfer_start([gather_src[k] for k in gather_order], "gather_start", gather=True,
                                        via_sibling=tuple(range(3, len(gather_order))))
    gather_handle.update(zip(gather_order, handles))

    def gathered(keys, after, forward):
        tag = "_".join(keys)
        lands = _xfer_wait([gather_handle[k] for k in keys], after, f"gather_wait_{tag}", gather=True)
        return _sibling_forward(lands, f"gather_forward_{tag}") if forward else lands

    def forward_behind(keys, after):
        tag = "_".join(keys)
        lands = _xfer_wait([gather_handle[k] for k in keys], after, f"gather_wait_{tag}", gather=True)
        fwd_handles, token = _sibling_forward_start(lands, f"gather_forward_start_{tag}")
        return (lambda done: _sibling_forward_wait(fwd_handles, done, f"gather_forward_wait_{tag}")), token

    (ssd_in_g,) = gathered(["ssd_in_w"], (gather_token, m_ssd_in_w, v_ssd_in_w), True)
    w_in_all = _shards_to_columns(ssd_in_g, in_base, in_off, in_win, zx_pad, "ssd_in_w_columns")
    (mod_all,) = gathered(["mod"], w_in_all, False)
    mod_mine = lax.dynamic_index_in_dim(mod_all, me, axis=2, keepdims=False)
    mod_mine = jnp.transpose(mod_mine, (1, 0, 2)).reshape(2, 6, 1, D)
    sh_m, sc_m, g_m, sh_f, sc_f, g_f = [[mod_mine[i, k] for i in range(2)] for k in range(6)]

    vec = lambda a: a.reshape(1, -1)
    grads = {}
    small = {}

    _, h0 = _norm_mod_fwd(x0, None, None, vec(mix_norm_w[0]), sc_m[0], sh_m[0], "l0_mix_norm")
    cw_all, scw_all = gathered(["ssd_conv_w", "sc_conv_w"], h0, False)
    (zx,) = _mm_nn(h0, w_in_all, F32, "ssd_in_proj", tm=2048, tn=512)
    conv_b0 = vec(ssd_conv_b[0])
    conv_w_full = jnp.transpose(cw_all, (1, 0, 2)).reshape(SSD_K, conv_dim)
    sc_conv_full = jnp.transpose(scw_all, (1, 0, 2)).reshape(SC_K, D)
    xc = _ssd_conv_fwd(zx, conv_w_full, conv_b0, d_inner, conv_dim, "ssd_conv")
    bias_p = jnp.pad(ssd_dt_bias[0], (0, LANES - n_heads)).reshape(1, LANES)
    alog_p = jnp.pad(ssd_A_log[0], (0, LANES - n_heads)).reshape(1, LANES)
    d_lane = jnp.repeat(ssd_D[0], SSD_P).reshape(SSD_G, 1, gw)
    nw_g = ssd_norm_w[0].reshape(SSD_G, 1, gw)
    finish, token = forward_behind(["ssd_out_w"], xc)
    decay = _ssd_decay(zx, bias_p, alog_p, n_heads, zx_dim // LANES, "ssd_decay")
    y_ssd, yn, prev = _ssd_fwd(zx, xc, decay, d_lane, nw_g, d_inner, token, "ssd_scan")
    ups, downs = [None, None], [None, None]
    (ssd_out_g,) = finish(yn)
    w_ssd_out = ssd_out_g.reshape(-1, D)
    finish, token = forward_behind(["up0", "down0"], ssd_out_g)
    (mix0,) = _mm_nn(yn, w_ssd_out, F32, "ssd_out_proj", after=(token,))
    x1, h1 = _norm_mod_fwd(x0, mix0, g_m[0], vec(mlp_norm_w[0]), sc_f[0], sh_f[0], "l0_mlp_norm")
    ups[0], down0_g = finish(h1)
    downs[0] = down0_g.reshape(-1, D)
    u0, s0 = _mm_nn_blocked(h1, ups[0], "l0_mlp_up", _ep_relu2, [BF16, BF16])
    finish, token = forward_behind(["sc_in_w", "sc_out_w", "up1", "down1"], s0)
    (d0,) = _mm_nn(s0, downs[0], F32, "l0_mlp_down", after=(token,))
    x2, h2 = _norm_mod_fwd(x1, d0, g_f[0], vec(mix_norm_w[1]), sc_m[1], sh_m[1], "l1_mix_norm")
    sc_in_g, sc_out_g, ups[1], down1_g = finish(h2)
    w_sc_out, downs[1] = sc_out_g.reshape(-1, D), down1_g.reshape(-1, D)
    (proj,) = _mm_nn_blocked(h2, sc_in_g, "sc_in_proj", _ep_store(F32), [F32])
    yc = _sc_conv_fwd(proj, sc_conv_full, "sc_conv")
    (mix1,) = _mm_nn(yc, w_sc_out, F32, "sc_out_proj")
    x3, h3 = _norm_mod_fwd(x2, mix1, g_m[1], vec(mlp_norm_w[1]), sc_f[1], sh_f[1], "l1_mlp_norm")
    u1, s1 = _mm_nn_blocked(h3, ups[1], "l1_mlp_up", _ep_relu2, [BF16, BF16])
    (d1,) = _mm_nn(s1, downs[1], F32, "l1_mlp_down")

    dx, loss_lane, dfw, dd1, dg = _final_loss(x3, d1, g_f[1], vec(final_norm_w), tgt, "final_loss")
    small["final_norm_w"] = dfw

    dmod = [[None] * 6 for _ in range(2)]
    dmod[1][5] = dg

    def mlp_backward(i, dx_out, dd, x_mid, h_in, u, s, mix, gate):
        du = _mm_nt(dd, downs[i], BF16, f"l{i}_mlp_down_bwd", epilogue=_ep_relu2_bwd, extra=(u,))
        gdown = _mm_tn(s, dd, BF16, f"l{i}_mlp_down_wgrad").reshape(N_DEV, -1, D)
        gup = _mm_tn_blocked(h_in, du, BF16, f"l{i}_mlp_up_wgrad")
        (h_down, h_up), token = _xfer_start([gdown, gup], f"l{i}_mlp_grads_start", gather=False)
        grad_handle[f"mlp_down{i}"], grad_handle[f"mlp_up{i}"] = h_down, h_up
        dh = _mm_nt_blocked(du, ups[i], F32, f"l{i}_mlp_up_bwd", after=(token,))
        dxm, dsh, dsc, dnw, dmix, dgate = _norm_mod_bwd(dh, x_mid, vec(mlp_norm_w[i]), sc_f[i], dx_out,
                                                        f"l{i}_mlp_norm_bwd", branch=(mix, gate))
        dmod[i][3], dmod[i][4], dmod[i][2] = dsh, dsc, dgate
        return dxm, dmix, dnw

    grad_handle = {}
    dx3, dyc, dnw_mlp1 = mlp_backward(1, dx, dd1, x3, h3, u1, s1, mix1, g_m[1])
    g_sc_out = _mm_tn(yc, dyc, BF16, "sc_out_wgrad").reshape(N_DEV, -1, D)
    dconv_out = _mm_nt(dyc, w_sc_out, F32, "sc_out_bwd")
    dbg, dcg, dxv, dscw = _sc_conv_bwd(proj, sc_conv_full, dconv_out, "sc_conv_bwd")
    dproj = jnp.concatenate([dbg, dcg, dxv], axis=1)
    g_sc_in = _mm_tn_blocked(h2, dproj, BF16, "sc_in_wgrad")
    (grad_handle["sc_out_w0"], grad_handle["sc_in_w0"]), token = _xfer_start(
        [g_sc_out, g_sc_in], "sc_grads_start", gather=False)
    dh2 = _mm_nt_blocked(dproj, sc_in_g, F32, "sc_in_bwd", after=(token,))
    dx2, dsh, dsc, dnw_mix1, dd0, dg = _norm_mod_bwd(dh2, x2, vec(mix_norm_w[1]), sc_m[1], dx3, "l1_mix_norm_bwd",
                                                     branch=(d0, g_f[0]))
    dmod[1][0], dmod[1][1], dmod[0][5] = dsh, dsc, dg
    dx1, dyo, dnw_mlp0 = mlp_backward(0, dx2, dd0, x1, h1, u0, s0, mix0, g_m[0])
    g_ssd_out = _mm_tn(yn, dyo, BF16, "ssd_out_wgrad").reshape(N_DEV, -1, D)
    (grad_handle["ssd_out_w0"],), token = _xfer_start([g_ssd_out], "ssd_out_grad_start", gather=False)
    dyn = _mm_nt(dyo, w_ssd_out, F32, "ssd_out_bwd", after=(token,))
    dz, dxs, db_, dc_, ddt, dbias, dalog, dd_, dnw_ssd = _ssd_bwd(
        dyn, y_ssd, zx, xc, prev, decay, alog_p, d_lane, nw_g, d_inner, "ssd_scan_bwd")
    dzx, dcw, dcb = _ssd_conv_bwd(zx, conv_w_full, conv_b0, [dxs, db_, dc_], dz, d_inner, "ssd_conv_bwd")
    dzx = _dzx_finish(dzx, ddt, zx_dim, "ssd_dzx_finish")
    g_in_all = _mm_tn(h0, dzx, BF16, "ssd_in_wgrad", tn=512, tk=2048)
    g_ssd_in = jnp.stack([g_in_all[:, b:b + in_win] for b in in_base], axis=0)
    (grad_handle["ssd_in_w0"],), token = _xfer_start([g_ssd_in], "ssd_in_grad_start", gather=False)
    dh0 = _mm_nt(dzx, w_in_all, F32, "ssd_in_bwd", tk=dzx.shape[1] // 2, after=(token,))
    grad_x, dsh, dsc, dnw_mix0 = _norm_mod_bwd(dh0, x0, vec(mix_norm_w[0]), sc_m[0], dx1, "l0_mix_norm_bwd")
    dmod[0][0], dmod[0][1] = dsh, dsc

    small["ada_b"] = jnp.concatenate([jnp.concatenate(dmod[i], axis=1) for i in range(2)], axis=0)
    small["mix_norm_w"] = jnp.concatenate([dnw_mix0, dnw_mix1], axis=0)
    small["mlp_norm_w"] = jnp.concatenate([dnw_mlp0, dnw_mlp1], axis=0)
    small["ssd_conv_w"] = dcw
    small["ssd_conv_b"] = dcb
    small["ssd_norm_w"] = dnw_ssd.reshape(1, d_inner)
    small["sc_conv_w"] = dscw
    small["loss"] = loss_lane
    small_names = list(small)
    head_names = ["ssd_dt_bias", "ssd_A_log", "ssd_D"]
    handles, small_token = _xfer_start([small[k] for k in small_names] + [dbias, dalog, dd_],
                                       "small_grads_start", gather=True)

    out_g, out_d, out_m, out_v = {}, {}, {}, {}

    layer_res = {}

    def big_update(name, i, after):
        (parts,) = _xfer_wait([grad_handle[f"{name}{i}"]], after, f"grads_wait_{name}_{i}", gather=False)
        res = _adamw_sum(parts, weights[name], moms[name], vars_[name], i, f"adamw_{name}_{i}",
                         prev=layer_res.get(name), window_off=in_off if name == "ssd_in_w" else None)
        layer_res[name] = res
        return res[1]

    chain = small_token
    for name, i in [("mlp_down", 1), ("mlp_up", 1), ("sc_out_w", 0), ("sc_in_w", 0), ("mlp_down", 0),
                    ("mlp_up", 0), ("ssd_out_w", 0), ("ssd_in_w", 0)]:
        chain = big_update(name, i, chain)
    gathered_small = _xfer_wait(handles, chain, "small_grads_wait", gather=True)
    small_all = dict(zip(small_names + head_names, gathered_small))

    dmod_loc = lax.dynamic_slice_in_dim(small_all["ada_b"], me * n_mod, n_mod, axis=2)
    dmod_pad = jnp.pad(jnp.transpose(dmod_loc, (1, 0, 2)), ((0, 0), (0, 16 - N_DEV), (0, 0)))
    out_g["ada_w"], out_d["ada_w"], out_m["ada_w"], out_v["ada_w"] = _ada_adamw(
        cond_pad, dmod_pad, ada_w, m_ada_w, v_ada_w, "adamw_ada_w")

    for k in ("ssd_conv_w", "sc_conv_w"):
        n_loc = weights[k].shape[2]
        small_all[k] = lax.dynamic_slice_in_dim(small_all[k], me * n_loc, n_loc, axis=2)
    plain = [k for k in small_names if k != "loss"]
    as2d = lambda a: a.reshape(-1, a.shape[-1])
    res, loss_row = _adamw_small(
        [small_all[k] for k in plain], [tuple(as2d(d[k]) for d in (weights, moms, vars_)) for k in plain],
        [small_all[k] for k in head_names], [tuple(as2d(d[k]) for d in (weights, moms, vars_)) for k in head_names],
        small_all["loss"], "adamw_small")
    loss = loss_row[0, 0]
    for k, res4 in zip(plain + head_names, res):
        for r, dst in zip(res4, (out_g, out_d, out_m, out_v)):
            dst[k] = r.reshape(weights[k].shape)
    for name, res4 in layer_res.items():
        for r, dst in zip(res4, (out_g, out_d, out_m, out_v)):
            dst[name] = r

    return (loss, grad_x[None], *[out_g[k] for k in names], *[out_d[k] for k in names],
            *[out_m[k] for k in names], *[out_v[k] for k in names])
```

```python
import functools

import jax
import jax.numpy as jnp
from jax import lax
from jax.experimental import pallas as pl
from jax.experimental.pallas import tpu as pltpu

F32 = jnp.float32
BF16 = jnp.bfloat16
N_DEV = 8
MESH_AXES = ("x", "y", "c")
MESH = pl.DeviceIdType.MESH

NORM_EPS = 1e-5
SSD_G = 4
SSD_P = 64
SSD_N = 128
SSD_CHUNK = 128
SSD_K = 4
SC_K = 3
LANES = 128

ADAM_LR = 0.001
ADAM_B1 = 0.9
ADAM_B2 = 0.999
ADAM_EPS = 1e-08
ADAM_WD = 0.01
ADAM_STEP = 10

VMEM_LIMIT = 56 * 1024 * 1024


def _pcall(body, **kw):
    return pl.pallas_call(body, **kw)


def _cparams(sem=None):
    if sem is None:
        return pltpu.CompilerParams(vmem_limit_bytes=VMEM_LIMIT)
    return pltpu.CompilerParams(dimension_semantics=sem, vmem_limit_bytes=VMEM_LIMIT)


def _my_index():
    return 4 * lax.axis_index("x") + 2 * lax.axis_index("y") + lax.axis_index("c")


_PEER_MASKS = [(0, 0, 1), (0, 1, 0), (0, 1, 1), (1, 0, 0), (1, 0, 1), (1, 1, 0), (1, 1, 1)]


def _peers():
    x, y, c = lax.axis_index("x"), lax.axis_index("y"), lax.axis_index("c")
    out = []
    for mx, my, mc in _PEER_MASKS:
        px = (1 - x) if mx else x
        py = (1 - y) if my else y
        pc = (1 - c) if mc else c
        out.append(((px, py, pc), 4 * px + 2 * py + pc))
    return out


def _exchange(arrs, name, gather):
    n = len(arrs)
    n_peer = N_DEV - 1

    def body(*refs):
        ins, outs = refs[:n], refs[n:2 * n]
        send_sems, recv_sems, local_sems = refs[2 * n:]
        me = _my_index()
        peers = _peers()
        started = []
        for a in range(n):
            src_own = ins[a] if gather else ins[a].at[me]
            own = pltpu.make_async_copy(src_own, outs[a].at[me], local_sems.at[a])
            own.start()
            started.append(own)
        sends = []
        for a in range(n):
            for k, (peer, pidx) in enumerate(peers):
                src = ins[a] if gather else ins[a].at[pidx]
                cp = pltpu.make_async_remote_copy(
                    src_ref=src, dst_ref=outs[a].at[me],
                    send_sem=send_sems.at[a * n_peer + k], recv_sem=recv_sems.at[a * n_peer + k],
                    device_id=peer, device_id_type=MESH)
                cp.start()
                sends.append(cp)
        for a in range(n):
            for k, (peer, pidx) in enumerate(peers):
                src = ins[a] if gather else ins[a].at[pidx]
                pltpu.make_async_remote_copy(
                    src_ref=src, dst_ref=outs[a].at[pidx],
                    send_sem=send_sems.at[a * n_peer + k], recv_sem=recv_sems.at[a * n_peer + k],
                    device_id=peer, device_id_type=MESH).wait_recv()
        for cp in sends:
            cp.wait_send()
        for own in started:
            own.wait()

    if gather:
        out_shape = [jax.ShapeDtypeStruct((N_DEV,) + a.shape, a.dtype) for a in arrs]
    else:
        out_shape = [jax.ShapeDtypeStruct(a.shape, a.dtype) for a in arrs]
    any_spec = pl.BlockSpec(memory_space=pl.ANY)
    outs = _pcall(
        body, name=name, out_shape=out_shape,
        in_specs=[any_spec] * n, out_specs=[any_spec] * n,
        scratch_shapes=[pltpu.SemaphoreType.DMA((n * n_peer,)), pltpu.SemaphoreType.DMA((n * n_peer,)),
                        pltpu.SemaphoreType.DMA((n,))],
        compiler_params=pltpu.CompilerParams(has_side_effects=True),
    )(*arrs)
    return list(outs)


def _sibling_forward_start(lands, name):
    n = len(lands)
    n_fwd = len(_OTHER_CHIPS)

    def body(*refs):
        ins, bufs = refs[:n], refs[3 * n:4 * n]
        token = refs[-1]
        sibling = (lax.axis_index("x"), lax.axis_index("y"), 1 - lax.axis_index("c"))
        peers = _peers()
        for a in range(n):
            send_sems, recv_sems = refs[n + 2 * a], refs[n + 2 * a + 1]
            for j, k in enumerate(_OTHER_CHIPS):
                slot = peers[k][1]
                pltpu.make_async_remote_copy(
                    src_ref=ins[a].at[slot], dst_ref=bufs[a].at[slot], send_sem=send_sems.at[j],
                    recv_sem=recv_sems.at[j], device_id=sibling, device_id_type=MESH).start()
        token[...] = jnp.zeros_like(token)

    out_shape, out_specs = [], []
    for _ in range(n):
        out_shape += [pltpu.SemaphoreType.DMA((n_fwd,)), pltpu.SemaphoreType.DMA((n_fwd,))]
        out_specs += [_SEM, _SEM]
    out_shape += [pltpu.HBM(a.shape, a.dtype) for a in lands] + [jax.ShapeDtypeStruct((8, LANES), F32)]
    out_specs += [_HBM] * n + [pl.BlockSpec(memory_space=pltpu.VMEM)]
    outs = _pcall(
        body, name=name, out_shape=tuple(out_shape), in_specs=[_HBM] * n, out_specs=tuple(out_specs),
        input_output_aliases={a: 2 * n + a for a in range(n)},
        compiler_params=pltpu.CompilerParams(has_side_effects=_DATAFLOW),
    )(*[pltpu.with_memory_space_constraint(a, pltpu.HBM) for a in lands])
    return [(outs[2 * n + a], outs[2 * a], outs[2 * a + 1]) for a in range(n)], outs[-1]


def _sibling_forward_wait(handles, after, name):
    n = len(handles)

    def body(*refs):
        sibling = (lax.axis_index("x"), lax.axis_index("y"), 1 - lax.axis_index("c"))
        peers = _peers()
        for a in range(n):
            buf, send_sems, recv_sems = refs[3 * a:3 * a + 3]
            for j, k in enumerate(_OTHER_CHIPS):
                (px, py, pc), slot = peers[k]
                theirs = 4 * px + 2 * py + (1 - pc)
                cp = pltpu.make_async_remote_copy(
                    src_ref=buf.at[slot], dst_ref=buf.at[theirs], send_sem=send_sems.at[j],
                    recv_sem=recv_sems.at[j], device_id=sibling, device_id_type=MESH)
                cp.wait_send()
                cp.wait_recv()

    operands, in_specs = [], []
    for h in handles:
        operands += list(h)
        in_specs += [_HBM, _SEM, _SEM]
    outs = _pcall(
        body, name=name, out_shape=tuple(pltpu.HBM(h[0].shape, h[0].dtype) for h in handles),
        in_specs=in_specs + [pl.BlockSpec(memory_space=pl.ANY)], out_specs=tuple([_HBM] * n),
        input_output_aliases={3 * a: a for a in range(n)},
        compiler_params=pltpu.CompilerParams(has_side_effects=_DATAFLOW),
    )(*operands, after)
    return list(outs)


_HBM = pl.BlockSpec(memory_space=pltpu.HBM)
_SEM = pl.BlockSpec(memory_space=pltpu.SEMAPHORE)
_DATAFLOW = pltpu.SideEffectType.DATAFLOW_SIDE_EFFECTING


_ALL_PEERS = tuple(range(N_DEV - 1))
_SAME_CORE_PEERS = (0, 1, 3, 5)
_OTHER_CHIPS = (1, 3, 5)


def _xfer_start(arrs, name, gather, via_sibling=(), after=()):
    n = len(arrs)
    n_peer = N_DEV - 1
    n_after = len(after)
    peer_ks = [_SAME_CORE_PEERS if a in via_sibling else _ALL_PEERS for a in range(n)]

    def body(*refs):
        ins, lands = refs[:n], refs[n:2 * n]
        sems = refs[2 * n + n_after:5 * n + n_after]
        token = refs[-1]
        me = _my_index()
        peers = _peers()
        for a in range(n):
            send_sems, recv_sems, loc_sem = sems[3 * a:3 * a + 3]
            src_own = ins[a] if gather else ins[a].at[me]
            pltpu.make_async_copy(src_own, lands[a].at[me], loc_sem).start()
            for k in peer_ks[a]:
                peer, pidx = peers[k]
                src = ins[a] if gather else ins[a].at[pidx]
                pltpu.make_async_remote_copy(
                    src_ref=src, dst_ref=lands[a].at[me], send_sem=send_sems.at[k], recv_sem=recv_sems.at[k],
                    device_id=peer, device_id_type=MESH).start()
        token[...] = jnp.zeros_like(token)

    land_shapes = [((N_DEV,) + a.shape) if gather else a.shape for a in arrs]
    out_shape, out_specs = [], []
    for _ in range(n):
        out_shape += [pltpu.SemaphoreType.DMA((n_peer,)), pltpu.SemaphoreType.DMA((n_peer,)),
                      pltpu.SemaphoreType.DMA(())]
        out_specs += [_SEM, _SEM, _SEM]
    out_shape += [pltpu.HBM(a.shape, a.dtype) for a in arrs]
    out_shape += [pltpu.HBM(s, a.dtype) for s, a in zip(land_shapes, arrs)]
    out_shape += [jax.ShapeDtypeStruct((8, LANES), F32)]
    out_specs += [_HBM] * (2 * n) + [pl.BlockSpec(memory_space=pltpu.VMEM)]
    aliases = {}
    for a in range(n):
        aliases[a] = 3 * n + a
        aliases[n + a] = 4 * n + a
    operands = [pltpu.with_memory_space_constraint(a, pltpu.HBM) for a in arrs]
    operands += [pltpu.with_memory_space_constraint(lax.empty(s, a.dtype), pltpu.HBM)
                 for s, a in zip(land_shapes, arrs)]
    outs = _pcall(
        body, name=name, out_shape=tuple(out_shape),
        in_specs=[_HBM] * (2 * n) + [pl.BlockSpec(memory_space=pl.ANY)] * n_after, out_specs=tuple(out_specs),
        input_output_aliases=aliases,
        compiler_params=pltpu.CompilerParams(has_side_effects=_DATAFLOW),
    )(*operands, *after)
    handles = []
    for a in range(n):
        handles.append((outs[3 * n + a], outs[4 * n + a], outs[3 * a], outs[3 * a + 1], outs[3 * a + 2],
                        peer_ks[a]))
    return handles, outs[-1]


def _xfer_wait(handles, after, name, gather):
    n = len(handles)
    after = tuple(after) if isinstance(after, (tuple, list)) else (after,)
    peer_ks = [h[5] for h in handles]

    def body(*refs):
        me = _my_index()
        peers = _peers()
        for a in range(n):
            src_ref, land_ref, send_ref, recv_ref, loc_ref = refs[5 * a:5 * a + 5]
            src_own = src_ref if gather else src_ref.at[me]
            pltpu.make_async_copy(src_own, land_ref.at[me], loc_ref).wait()
            for k in peer_ks[a]:
                peer, pidx = peers[k]
                src = src_ref if gather else src_ref.at[pidx]
                cp = pltpu.make_async_remote_copy(
                    src_ref=src, dst_ref=land_ref.at[pidx], send_sem=send_ref.at[k], recv_sem=recv_ref.at[k],
                    device_id=peer, device_id_type=MESH)
                cp.wait_send()
                cp.wait_recv()

    operands, in_specs, out_shape, aliases = [], [], [], {}
    for a, h in enumerate(handles):
        operands += list(h[:5])
        in_specs += [_HBM, _HBM, _SEM, _SEM, _SEM]
        out_shape += [pltpu.HBM(h[0].shape, h[0].dtype), pltpu.HBM(h[1].shape, h[1].dtype)]
        aliases[5 * a] = 2 * a
        aliases[5 * a + 1] = 2 * a + 1
    outs = _pcall(
        body, name=name, out_shape=tuple(out_shape),
        in_specs=in_specs + [pl.BlockSpec(memory_space=pl.ANY)] * len(after),
        out_specs=tuple([_HBM] * (2 * n)), input_output_aliases=aliases,
        compiler_params=pltpu.CompilerParams(has_side_effects=_DATAFLOW),
    )(*operands, *after)
    return [outs[2 * a + 1] for a in range(n)]


def _sibling_forward(lands, name):
    n = len(lands)
    n_fwd = len(_OTHER_CHIPS)

    def body(*refs):
        ins, bufs = refs[:n], refs[n:2 * n]
        send_sems, recv_sems = refs[2 * n:]
        x, y, c = lax.axis_index("x"), lax.axis_index("y"), lax.axis_index("c")
        sibling = (x, y, 1 - c)
        peers = _peers()
        sends = []
        for a in range(n):
            for j, k in enumerate(_OTHER_CHIPS):
                slot = peers[k][1]
                cp = pltpu.make_async_remote_copy(
                    src_ref=ins[a].at[slot], dst_ref=bufs[a].at[slot],
                    send_sem=send_sems.at[a * n_fwd + j], recv_sem=recv_sems.at[a * n_fwd + j],
                    device_id=sibling, device_id_type=MESH)
                cp.start()
                sends.append(cp)
        for a in range(n):
            for j, k in enumerate(_OTHER_CHIPS):
                (px, py, pc), slot = peers[k]
                theirs = 4 * px + 2 * py + (1 - pc)
                pltpu.make_async_remote_copy(
                    src_ref=ins[a].at[slot], dst_ref=bufs[a].at[theirs],
                    send_sem=send_sems.at[a * n_fwd + j], recv_sem=recv_sems.at[a * n_fwd + j],
                    device_id=sibling, device_id_type=MESH).wait_recv()
        for cp in sends:
            cp.wait_send()

    any_spec = pl.BlockSpec(memory_space=pl.ANY)
    outs = _pcall(
        body, name=name, out_shape=[jax.ShapeDtypeStruct(a.shape, a.dtype) for a in lands],
        in_specs=[any_spec] * n, out_specs=[any_spec] * n,
        input_output_aliases={a: a for a in range(n)},
        scratch_shapes=[pltpu.SemaphoreType.DMA((n * n_fwd,)), pltpu.SemaphoreType.DMA((n * n_fwd,))],
        compiler_params=pltpu.CompilerParams(has_side_effects=True),
    )(*lands)
    return list(outs)


_DIMS = {"nn": (((1,), (0,)), ((), ())), "nt": (((1,), (1,)), ((), ())), "tn": (((0,), (0,)), ((), ()))}


def _dot(a, b, mode="nn"):
    return lax.dot_general(a, b, _DIMS[mode], preferred_element_type=F32)


def _mm(a, b, *, mode, grid, a_spec, b_spec, out_shape, out_specs, acc_shape, epilogue, name,
        extra=(), extra_specs=(), after=(), semantics=("parallel", "parallel", "arbitrary")):
    nk = grid[2]
    n_extra = len(extra)
    n_in = 2 + n_extra + len(after)

    def body_single(*refs):
        a_ref, b_ref = refs[0], refs[1]
        epilogue(_dot(a_ref[...], b_ref[...], mode), refs[2:2 + n_extra], refs[n_in:])

    def body_acc(*refs):
        a_ref, b_ref = refs[0], refs[1]
        ex = refs[2:2 + n_extra]
        outs = refs[n_in:-1]
        acc = refs[-1]
        k = pl.program_id(2)

        @pl.when(k == 0)
        def _():
            acc[...] = jnp.zeros_like(acc)

        acc[...] += _dot(a_ref[...], b_ref[...], mode)

        @pl.when(k == nk - 1)
        def _():
            epilogue(acc[...], ex, outs)

    return _pcall(
        body_single if nk == 1 else body_acc, name=name, grid=grid, out_shape=out_shape,
        in_specs=[a_spec, b_spec] + list(extra_specs) + [pl.BlockSpec(memory_space=pl.ANY)] * len(after),
        out_specs=out_specs,
        scratch_shapes=[] if nk == 1 else [pltpu.VMEM(acc_shape, F32)],
        compiler_params=_cparams(semantics),
    )(a, b, *extra, *after)


def _ep_store(dtype):
    def ep(acc, ex, outs):
        outs[0][...] = acc.astype(dtype)
    return ep


def _ep_relu2(acc, ex, outs):
    outs[0][...] = acc.astype(BF16)
    r = jnp.maximum(acc, 0.0)
    outs[1][...] = (r * r).astype(BF16)


def _ep_relu2_bwd(acc, ex, outs):
    u = ex[0][...].astype(F32)
    outs[0][...] = (acc * (2.0 * jnp.maximum(u, 0.0))).astype(BF16)


def _tile(n, want):
    t = min(n, want)
    while n % t:
        t //= 2
    return t


def _mm_nn(a, w, out_dtype, name, tm=2048, tn=1024, tk=1024, epilogue=None, out_dtypes=None, after=()):
    M, K = a.shape
    N = w.shape[1]
    tm, tn, tk = _tile(M, tm), _tile(N, tn), _tile(K, tk)
    out_dtypes = out_dtypes or [out_dtype]
    return _mm(a, w, mode="nn", grid=(M // tm, N // tn, K // tk),
               a_spec=pl.BlockSpec((tm, tk), lambda i, j, k: (i, k)),
               b_spec=pl.BlockSpec((tk, tn), lambda i, j, k: (k, j)),
               out_shape=[jax.ShapeDtypeStruct((M, N), d) for d in out_dtypes],
               out_specs=[pl.BlockSpec((tm, tn), lambda i, j, k: (i, j)) for _ in out_dtypes],
               acc_shape=(tm, tn), epilogue=epilogue or _ep_store(out_dtype), name=name, after=after)


def _mm_nn_blocked(a, wg, name, epilogue, out_dtypes, tm=2048):
    M, K = a.shape
    n = wg.shape[2]
    tm = _tile(M, tm)
    return _mm(a, wg, mode="nn", grid=(M // tm, N_DEV, 1),
               a_spec=pl.BlockSpec((tm, K), lambda i, j, k: (i, 0)),
               b_spec=pl.BlockSpec((None, K, n), lambda i, j, k: (j, 0, 0)),
               out_shape=[jax.ShapeDtypeStruct((M, N_DEV * n), d) for d in out_dtypes],
               out_specs=[pl.BlockSpec((tm, n), lambda i, j, k: (i, j)) for _ in out_dtypes],
               acc_shape=(tm, n), epilogue=epilogue, name=name)


def _mm_nt(a, w, out_dtype, name, tm=2048, tn=1024, tk=1024, epilogue=None, extra=(), extra_specs=(),
           after=()):
    M, K = a.shape
    N = w.shape[0]
    tm, tn, tk = _tile(M, tm), _tile(N, tn), _tile(K, tk)
    if extra and not extra_specs:
        extra_specs = [pl.BlockSpec((tm, tn), lambda i, j, k: (i, j)) for _ in extra]
    return _mm(a, w, mode="nt", grid=(M // tm, N // tn, K // tk),
               a_spec=pl.BlockSpec((tm, tk), lambda i, j, k: (i, k)),
               b_spec=pl.BlockSpec((tn, tk), lambda i, j, k: (j, k)),
               out_shape=[jax.ShapeDtypeStruct((M, N), out_dtype)],
               out_specs=[pl.BlockSpec((tm, tn), lambda i, j, k: (i, j))],
               acc_shape=(tm, tn), epilogue=epilogue or _ep_store(out_dtype), name=name,
               extra=extra, extra_specs=extra_specs, after=after)[0]


def _mm_nt_blocked(a, wg, out_dtype, name, tm=1024, after=()):
    M = a.shape[0]
    kout, n = wg.shape[1], wg.shape[2]
    tm = _tile(M, tm)
    return _mm(a, wg, mode="nt", grid=(M // tm, 1, N_DEV),
               a_spec=pl.BlockSpec((tm, n), lambda i, j, k: (i, k)),
               b_spec=pl.BlockSpec((None, kout, n), lambda i, j, k: (k, 0, 0)),
               out_shape=[jax.ShapeDtypeStruct((M, kout), out_dtype)],
               out_specs=[pl.BlockSpec((tm, kout), lambda i, j, k: (i, 0))],
               acc_shape=(tm, kout), epilogue=_ep_store(out_dtype), name=name, after=after)[0]


def _mm_tn(a, b, out_dtype, name, tm=1024, tn=1024, tk=2048):
    K, M = a.shape
    N = b.shape[1]
    tm, tn, tk = _tile(M, tm), _tile(N, tn), _tile(K, tk)
    return _mm(a, b, mode="tn", grid=(M // tm, N // tn, K // tk),
               a_spec=pl.BlockSpec((tk, tm), lambda i, j, k: (k, i)),
               b_spec=pl.BlockSpec((tk, tn), lambda i, j, k: (k, j)),
               out_shape=[jax.ShapeDtypeStruct((M, N), out_dtype)],
               out_specs=[pl.BlockSpec((tm, tn), lambda i, j, k: (i, j))],
               acc_shape=(tm, tn), epilogue=_ep_store(out_dtype), name=name)[0]


def _mm_tn_blocked(a, b, out_dtype, name, tm=1024, tk=2048):
    K, M = a.shape
    n = b.shape[1] // N_DEV
    tm, tk = _tile(M, tm), _tile(K, tk)
    return _mm(a, b, mode="tn", grid=(M // tm, N_DEV, K // tk),
               a_spec=pl.BlockSpec((tk, tm), lambda i, j, k: (k, i)),
               b_spec=pl.BlockSpec((tk, n), lambda i, j, k: (k, j)),
               out_shape=[jax.ShapeDtypeStruct((N_DEV, M, n), out_dtype)],
               out_specs=[pl.BlockSpec((None, tm, n), lambda i, j, k: (j, i, 0))],
               acc_shape=(tm, n), epilogue=_ep_store(out_dtype), name=name)[0]


def _window_geometry(ws):
    base = [(ws * k // LANES) * LANES for k in range(N_DEV)]
    off = [ws * k - base[k] for k in range(N_DEV)]
    win = -(-(max(off) + ws) // LANES) * LANES
    return base, off, win


def _shards_to_columns(xg, base, off, win, n_out, name, tr=256):
    R, ws = xg.shape[1], xg.shape[2]
    tr = _tile(R, tr)
    nb_win = win // LANES

    def body(x_ref, o_ref, frame_ref):
        written = set()
        frame_ref[...] = jnp.zeros_like(frame_ref)
        for k in range(N_DEV):
            frame_ref[:, 0:ws] = x_ref[k].astype(F32)
            window = frame_ref[...]
            if off[k]:
                window = pltpu.roll(window, off[k], 1)
            for i in range(nb_win):
                b = base[k] // LANES + i
                if b * LANES >= n_out:
                    continue
                cols = slice(b * LANES, (b + 1) * LANES)
                blk = window[:, i * LANES:(i + 1) * LANES]
                if b in written:
                    blk = blk + o_ref[:, cols].astype(F32)
                o_ref[:, cols] = blk.astype(o_ref.dtype)
                written.add(b)
        for b in range(n_out // LANES):
            if b not in written:
                o_ref[:, b * LANES:(b + 1) * LANES] = jnp.zeros((tr, LANES), o_ref.dtype)

    return _pcall(
        body, name=name, grid=(R // tr,), out_shape=jax.ShapeDtypeStruct((R, n_out), xg.dtype),
        in_specs=[pl.BlockSpec((N_DEV, tr, ws), lambda i: (0, i, 0))],
        out_specs=pl.BlockSpec((tr, n_out), lambda i: (i, 0)),
        scratch_shapes=[pltpu.VMEM((tr, win), F32)],
        compiler_params=_cparams(("parallel",)))(xg)


def _sigmoid(x):
    return 1.0 / (1.0 + jnp.exp(-x))


def _row_spec(tm, d):
    return pl.BlockSpec((tm, d), lambda i: (i, 0))


def _vec_spec(d):
    return pl.BlockSpec((1, d), lambda i: (0, 0))


def _norm_mod_fwd(x, y, gate, nw, scale, shift, name, tm=512):
    L, D = x.shape
    tm = _tile(L, tm)
    has_res = y is not None

    def body(*refs):
        if has_res:
            x_ref, y_ref, g_ref, nw_ref, sc_ref, sh_ref, xo_ref, h_ref = refs
            xn = x_ref[...] + g_ref[...] * y_ref[...]
            xo_ref[...] = xn
        else:
            x_ref, nw_ref, sc_ref, sh_ref, h_ref = refs
            xn = x_ref[...]
        rstd = lax.rsqrt(jnp.mean(xn * xn, axis=-1, keepdims=True) + NORM_EPS)
        h = xn * rstd * nw_ref[...] * (1.0 + sc_ref[...]) + sh_ref[...]
        h_ref[...] = h.astype(BF16)

    row, vec = _row_spec(tm, D), _vec_spec(D)
    if has_res:
        ins, in_specs = (x, y, gate, nw, scale, shift), [row, row, vec, vec, vec, vec]
        out_shape = [jax.ShapeDtypeStruct((L, D), F32), jax.ShapeDtypeStruct((L, D), BF16)]
        out_specs = [row, row]
    else:
        ins, in_specs = (x, nw, scale, shift), [row, vec, vec, vec]
        out_shape = [jax.ShapeDtypeStruct((L, D), BF16)]
        out_specs = [row]
    outs = _pcall(body, name=name, grid=(L // tm,), out_shape=out_shape, in_specs=in_specs,
                  out_specs=out_specs, compiler_params=_cparams(("parallel",)))(*ins)
    return outs if has_res else (x, outs[0])


def _gated_branch_bwd(dx, branch, y_ref, g_ref, dy_ref, dg_ref):
    if branch is None:
        return
    dy_ref[...] = (g_ref[...] * dx).astype(BF16)
    dg_ref[...] += jnp.sum(dx * y_ref[...], axis=0, keepdims=True)


def _norm_mod_bwd(dh, x, nw, scale, dres, name, branch=None, tm=512):
    L, D = x.shape
    tm = _tile(L, tm)
    nb = 0 if branch is None else 2

    def body(dh_ref, x_ref, nw_ref, sc_ref, dres_ref, *rest):
        y_ref, g_ref = rest[:nb] if nb else (None, None)
        dx_ref, dsh_ref, dsc_ref, dnw_ref = rest[nb:nb + 4]
        dy_ref, dg_ref = rest[nb + 4:] if nb else (None, None)

        @pl.when(pl.program_id(0) == 0)
        def _():
            dsh_ref[...] = jnp.zeros_like(dsh_ref)
            dsc_ref[...] = jnp.zeros_like(dsc_ref)
            dnw_ref[...] = jnp.zeros_like(dnw_ref)
            if nb:
                dg_ref[...] = jnp.zeros_like(dg_ref)

        xv = x_ref[...]
        dh_v = dh_ref[...]
        nw_v = nw_ref[...]
        rstd = lax.rsqrt(jnp.mean(xv * xv, axis=-1, keepdims=True) + NORM_EPS)
        xhat = xv * rstd
        dsh_ref[...] += jnp.sum(dh_v, axis=0, keepdims=True)
        dsc_ref[...] += jnp.sum(dh_v * (xhat * nw_v), axis=0, keepdims=True)
        dr = dh_v * (1.0 + sc_ref[...])
        dnw_ref[...] += jnp.sum(dr * xhat, axis=0, keepdims=True)
        dxh = dr * nw_v
        dx = rstd * (dxh - xhat * jnp.mean(dxh * xhat, axis=-1, keepdims=True)) + dres_ref[...]
        dx_ref[...] = dx
        _gated_branch_bwd(dx, branch, y_ref, g_ref, dy_ref, dg_ref)

    row, vec = _row_spec(tm, D), _vec_spec(D)
    extra_in = [] if branch is None else list(branch)
    return _pcall(
        body, name=name, grid=(L // tm,),
        out_shape=[jax.ShapeDtypeStruct((L, D), F32)] + [jax.ShapeDtypeStruct((1, D), F32)] * 3
        + ([jax.ShapeDtypeStruct((L, D), BF16), jax.ShapeDtypeStruct((1, D), F32)] if nb else []),
        in_specs=[row, row, vec, vec, row] + ([row, vec] if nb else []),
        out_specs=[row, vec, vec, vec] + ([row, vec] if nb else []),
        compiler_params=_cparams(("arbitrary",)))(dh, x, nw, scale, dres, *extra_in)


def _final_loss(x, y, gate, fw, target, name, tm=512):
    L, D = x.shape
    tm = _tile(L, tm)

    def body(x_ref, y_ref, g_ref, fw_ref, t_ref, dx_ref, loss_ref, dfw_ref, dy_ref, dg_ref):
        @pl.when(pl.program_id(0) == 0)
        def _():
            loss_ref[...] = jnp.zeros_like(loss_ref)
            dfw_ref[...] = jnp.zeros_like(dfw_ref)
            dg_ref[...] = jnp.zeros_like(dg_ref)

        xn = x_ref[...] + g_ref[...] * y_ref[...]
        fw_v = fw_ref[...]
        rstd = lax.rsqrt(jnp.mean(xn * xn, axis=-1, keepdims=True) + NORM_EPS)
        xhat = xn * rstd
        diff = xhat * fw_v - t_ref[...]
        loss_ref[...] += jnp.sum(diff * diff, axis=0, keepdims=True)
        dyf = diff * (1.0 / D)
        dfw_ref[...] += jnp.sum(dyf * xhat, axis=0, keepdims=True)
        dxh = dyf * fw_v
        dx = rstd * (dxh - xhat * jnp.mean(dxh * xhat, axis=-1, keepdims=True))
        dx_ref[...] = dx
        _gated_branch_bwd(dx, True, y_ref, g_ref, dy_ref, dg_ref)

    row, vec = _row_spec(tm, D), _vec_spec(D)
    return _pcall(
        body, name=name, grid=(L // tm,),
        out_shape=[jax.ShapeDtypeStruct((L, D), F32), jax.ShapeDtypeStruct((1, D), F32),
                   jax.ShapeDtypeStruct((1, D), F32), jax.ShapeDtypeStruct((L, D), BF16),
                   jax.ShapeDtypeStruct((1, D), F32)],
        in_specs=[row, row, vec, vec, row], out_specs=[row, vec, vec, row, vec],
        compiler_params=_cparams(("arbitrary",)))(x, y, gate, fw, target)


def _mm_nt_norm_bwd(a, w, x, nw, scale, dres, name, branch=None, blocked=False, tm=512, tk=1024, after=()):
    M = a.shape[0]
    D = x.shape[1]
    tm = _tile(M, tm)
    nb = 0 if branch is None else 2

    def epilogue(dh_v, ex, outs):
        x_ref, nw_ref, sc_ref, dres_ref = ex[:4]
        y_ref, g_ref = ex[4:] if nb else (None, None)
        dx_ref, dsh_ref, dsc_ref, dnw_ref = outs[:4]
        dy_ref, dg_ref = outs[4:] if nb else (None, None)

        @pl.when(pl.program_id(0) == 0)
        def _():
            dsh_ref[...] = jnp.zeros_like(dsh_ref)
            dsc_ref[...] = jnp.zeros_like(dsc_ref)
            dnw_ref[...] = jnp.zeros_like(dnw_ref)
            if nb:
                dg_ref[...] = jnp.zeros_like(dg_ref)

        xv = x_ref[...]
        nw_v = nw_ref[...]
        rstd = lax.rsqrt(jnp.mean(xv * xv, axis=-1, keepdims=True) + NORM_EPS)
        xhat = xv * rstd
        dsh_ref[...] += jnp.sum(dh_v, axis=0, keepdims=True)
        dsc_ref[...] += jnp.sum(dh_v * (xhat * nw_v), axis=0, keepdims=True)
        dr = dh_v * (1.0 + sc_ref[...])
        dnw_ref[...] += jnp.sum(dr * xhat, axis=0, keepdims=True)
        dxh = dr * nw_v
        dx = rstd * (dxh - xhat * jnp.mean(dxh * xhat, axis=-1, keepdims=True)) + dres_ref[...]
        dx_ref[...] = dx
        _gated_branch_bwd(dx, branch, y_ref, g_ref, dy_ref, dg_ref)

    row = pl.BlockSpec((tm, D), lambda i, j, k: (i, 0))
    vec = pl.BlockSpec((1, D), lambda i, j, k: (0, 0))
    if blocked:
        n = w.shape[2]
        grid = (M // tm, 1, N_DEV)
        a_spec = pl.BlockSpec((tm, n), lambda i, j, k: (i, k))
        b_spec = pl.BlockSpec((None, D, n), lambda i, j, k: (k, 0, 0))
    else:
        K = a.shape[1]
        tk = _tile(K, tk)
        grid = (M // tm, 1, K // tk)
        a_spec = pl.BlockSpec((tm, tk), lambda i, j, k: (i, k))
        b_spec = pl.BlockSpec((D, tk), lambda i, j, k: (0, k))
    return _mm(a, w, mode="nt", grid=grid, a_spec=a_spec, b_spec=b_spec,
               out_shape=[jax.ShapeDtypeStruct((M, D), F32)] + [jax.ShapeDtypeStruct((1, D), F32)] * 3
               + ([jax.ShapeDtypeStruct((M, D), BF16), jax.ShapeDtypeStruct((1, D), F32)] if nb else []),
               out_specs=[row, vec, vec, vec] + ([row, vec] if nb else []),
               acc_shape=(tm, D), epilogue=epilogue, name=name,
               extra=(x, nw, scale, dres) + (tuple(branch) if nb else ()),
               extra_specs=[row, vec, vec, row] + ([row, vec] if nb else []), after=after,
               semantics=("arbitrary", "arbitrary", "arbitrary"))


def _shift_down(v, s, row):
    if s == 0:
        return v
    return jnp.where(row >= s, pltpu.roll(v, s, 0), 0.0)


def _shift_up(v, s, row):
    if s == 0:
        return v
    n = v.shape[0]
    return jnp.where(row < n - s, pltpu.roll(v, n - s, 0), 0.0)


def _ssd_conv_fwd(zx, w, b, col0, width, name, cb=128):
    L = zx.shape[0]
    nb = width // cb
    off = col0 // cb

    def body(x_ref, w_ref, b_ref, o_ref):
        xv = x_ref[...]
        row = lax.broadcasted_iota(jnp.int32, xv.shape, 0)
        acc = b_ref[...] + w_ref[SSD_K - 1:SSD_K, :] * xv
        for s in range(1, SSD_K):
            acc = acc + w_ref[SSD_K - 1 - s:SSD_K - s, :] * _shift_down(xv, s, row)
        o_ref[...] = acc * _sigmoid(acc)

    return _pcall(
        body, name=name, grid=(nb,), out_shape=jax.ShapeDtypeStruct((L, width), F32),
        in_specs=[pl.BlockSpec((L, cb), lambda j: (0, off + j)),
                  pl.BlockSpec((SSD_K, cb), lambda j: (0, j)),
                  pl.BlockSpec((1, cb), lambda j: (0, j))],
        out_specs=pl.BlockSpec((L, cb), lambda j: (0, j)),
        compiler_params=_cparams(("parallel",)))(zx, w, b)


def _ssd_conv_bwd(zx, w, b, d_parts, dzx, col0, name, cb=128):
    L = zx.shape[0]
    widths = [p.shape[1] for p in d_parts]
    width = sum(widths)
    nb = width // cb
    off = col0 // cb
    starts = [sum(widths[:i]) // cb for i in range(len(d_parts))]
    counts = [wd // cb for wd in widths]

    def body(x_ref, w_ref, b_ref, *rest):
        d_refs = rest[:len(d_parts)]
        dx_ref, dw_ref, db_ref = rest[len(d_parts) + 1:]
        j = pl.program_id(0)
        d_val = d_refs[-1][...]
        for i in range(len(d_parts) - 2, -1, -1):
            d_val = jnp.where(j < starts[i + 1], d_refs[i][...], d_val)
        xv = x_ref[...]
        row = lax.broadcasted_iota(jnp.int32, xv.shape, 0)
        shifted = [_shift_down(xv, s, row) for s in range(SSD_K)]
        acc = b_ref[...] + w_ref[SSD_K - 1:SSD_K, :] * xv
        for s in range(1, SSD_K):
            acc = acc + w_ref[SSD_K - 1 - s:SSD_K - s, :] * shifted[s]
        sig = _sigmoid(acc)
        dpre = d_val * (sig * (1.0 + acc * (1.0 - sig)))
        db_ref[...] = jnp.sum(dpre, axis=0, keepdims=True)
        dx = w_ref[SSD_K - 1:SSD_K, :] * dpre
        for s in range(SSD_K):
            dw_ref[SSD_K - 1 - s:SSD_K - s, :] = jnp.sum(dpre * shifted[s], axis=0, keepdims=True)
            if s:
                dx = dx + w_ref[SSD_K - 1 - s:SSD_K - s, :] * _shift_up(dpre, s, row)
        dx_ref[...] = dx.astype(BF16)

    def part_spec(i):
        return pl.BlockSpec((L, cb), lambda j: (0, jnp.clip(j - starts[i], 0, counts[i] - 1)))

    return _pcall(
        body, name=name, grid=(nb,),
        out_shape=[jax.ShapeDtypeStruct(dzx.shape, BF16), jax.ShapeDtypeStruct((SSD_K, width), F32),
                   jax.ShapeDtypeStruct((1, width), F32)],
        in_specs=[pl.BlockSpec((L, cb), lambda j: (0, off + j)),
                  pl.BlockSpec((SSD_K, cb), lambda j: (0, j)),
                  pl.BlockSpec((1, cb), lambda j: (0, j))]
        + [part_spec(i) for i in range(len(d_parts))] + [pl.BlockSpec(memory_space=pl.ANY)],
        out_specs=[pl.BlockSpec((L, cb), lambda j: (0, off + j)),
                   pl.BlockSpec((SSD_K, cb), lambda j: (0, j)),
                   pl.BlockSpec((1, cb), lambda j: (0, j))],
        input_output_aliases={3 + len(d_parts): 0},
        compiler_params=_cparams(("parallel",)))(zx, w, b, *d_parts, dzx)


def _dzx_finish(dzx, ddt, col0, name, tl=512):
    G, L, _ = ddt.shape
    tail = dzx.shape[1] - col0
    tl = _tile(L, tl)

    def body(ddt_ref, dzx_ref, o_ref):
        s = ddt_ref[0]
        for g in range(1, G):
            s = s + ddt_ref[g]
        o_ref[:, 0:LANES] = s.astype(o_ref.dtype)
        if tail > LANES:
            o_ref[:, LANES:] = jnp.zeros((tl, tail - LANES), o_ref.dtype)

    return _pcall(
        body, name=name, grid=(L // tl,), out_shape=jax.ShapeDtypeStruct(dzx.shape, dzx.dtype),
        in_specs=[pl.BlockSpec((G, tl, LANES), lambda i: (0, i, 0)), pl.BlockSpec(memory_space=pl.ANY)],
        out_specs=pl.BlockSpec((tl, tail), lambda i: (i, col0 // tail)),
        input_output_aliases={1: 0},
        compiler_params=_cparams(("parallel",)))(ddt, dzx)


def _sc_conv_fwd(proj, w, name, cb=128):
    L = proj.shape[0]
    width = proj.shape[1] // 3
    nb = width // cb

    def body(b_ref, c_ref, x_ref, w_ref, o_ref):
        q = c_ref[...] * x_ref[...]
        row = lax.broadcasted_iota(jnp.int32, q.shape, 0)
        acc = w_ref[SC_K - 1:SC_K, :] * q
        for s in range(1, SC_K):
            acc = acc + w_ref[SC_K - 1 - s:SC_K - s, :] * _shift_down(q, s, row)
        o_ref[...] = (b_ref[...] * acc).astype(BF16)

    return _pcall(
        body, name=name, grid=(nb,), out_shape=jax.ShapeDtypeStruct((L, width), BF16),
        in_specs=[pl.BlockSpec((L, cb), lambda j: (0, j)),
                  pl.BlockSpec((L, cb), lambda j: (0, nb + j)),
                  pl.BlockSpec((L, cb), lambda j: (0, 2 * nb + j)),
                  pl.BlockSpec((SC_K, cb), lambda j: (0, j))],
        out_specs=pl.BlockSpec((L, cb), lambda j: (0, j)),
        compiler_params=_cparams(("parallel",)))(proj, proj, proj, w)


def _sc_conv_bwd(proj, w, dy, name, cb=128):
    L = proj.shape[0]
    width = proj.shape[1] // 3
    nb = width // cb

    def body(b_ref, c_ref, x_ref, w_ref, dy_ref, db_ref, dc_ref, dxv_ref, dw_ref):
        cg, xv, dyv = c_ref[...], x_ref[...], dy_ref[...]
        q = cg * xv
        row = lax.broadcasted_iota(jnp.int32, q.shape, 0)
        shifted = [_shift_down(q, s, row) for s in range(SC_K)]
        conv = w_ref[SC_K - 1:SC_K, :] * q
        for s in range(1, SC_K):
            conv = conv + w_ref[SC_K - 1 - s:SC_K - s, :] * shifted[s]
        db_ref[...] = (dyv * conv).astype(BF16)
        dconv = dyv * b_ref[...]
        dq = w_ref[SC_K - 1:SC_K, :] * dconv
        for s in range(SC_K):
            dw_ref[SC_K - 1 - s:SC_K - s, :] = jnp.sum(dconv * shifted[s], axis=0, keepdims=True)
            if s:
                dq = dq + w_ref[SC_K - 1 - s:SC_K - s, :] * _shift_up(dconv, s, row)
        dc_ref[...] = (dq * xv).astype(BF16)
        dxv_ref[...] = (dq * cg).astype(BF16)

    blk = pl.BlockSpec((L, cb), lambda j: (0, j))
    wblk = pl.BlockSpec((SC_K, cb), lambda j: (0, j))
    return _pcall(
        body, name=name, grid=(nb,),
        out_shape=[jax.ShapeDtypeStruct((L, width), BF16)] * 3 + [jax.ShapeDtypeStruct((SC_K, width), F32)],
        in_specs=[blk, pl.BlockSpec((L, cb), lambda j: (0, nb + j)),
                  pl.BlockSpec((L, cb), lambda j: (0, 2 * nb + j)), wblk, blk],
        out_specs=[blk, blk, blk, wblk],
        compiler_params=_cparams(("parallel",)))(proj, proj, proj, w, dy)


def _split3(v):
    hi = v.astype(BF16)
    r1 = v - hi.astype(F32)
    mid = r1.astype(BF16)
    lo = (r1 - mid.astype(F32)).astype(BF16)
    return hi, mid, lo


def _dot_exact01(t01, v):
    hi, mid, lo = _split3(v)
    return _dot(t01, hi) + _dot(t01, mid) + _dot(t01, lo)


def _lane_col(v, lane, h):
    return jnp.sum(jnp.where(lane == h, v, 0.0), axis=1, keepdims=True)


def _sum_all(v):
    return jnp.sum(jnp.sum(v, axis=1, keepdims=True), axis=0, keepdims=True)


def _softplus(x):
    return jnp.maximum(x, 0.0) + jnp.log1p(jnp.exp(-jnp.abs(x)))


def _ssd_decay(zx, bias_p, alog_p, n_heads, dt_block, name):
    L = zx.shape[0]
    nc = L // SSD_CHUNK

    def body(raw_ref, bias_ref, alog_ref, dt_ref, sg_ref, cs_ref, cst_ref, last_ref):
        lane = lax.broadcasted_iota(jnp.int32, (SSD_CHUNK, LANES), 1)
        row = lax.broadcasted_iota(jnp.int32, (SSD_CHUNK, LANES), 0)
        valid = lane < n_heads
        raw = raw_ref[...] + bias_ref[...]
        dt = jnp.where(valid, _softplus(raw), 0.0)
        a = dt * (-jnp.exp(alog_ref[...]))
        cs = _dot_exact01((row >= lane).astype(BF16), a)
        dt_ref[...] = dt
        sg_ref[...] = _sigmoid(raw)
        cs_ref[...] = cs
        cst_ref[...] = cs.T
        last_ref[...] = jnp.sum(a, axis=0, keepdims=True)

    blk = pl.BlockSpec((SSD_CHUNK, LANES), lambda c: (c, 0))
    head_vec = pl.BlockSpec((1, LANES), lambda c: (0, 0))
    return _pcall(
        body, name=name, grid=(nc,),
        out_shape=[jax.ShapeDtypeStruct((L, LANES), F32)] * 3
        + [jax.ShapeDtypeStruct((nc, SSD_CHUNK, LANES), F32), jax.ShapeDtypeStruct((nc, 1, LANES), F32)],
        in_specs=[pl.BlockSpec((SSD_CHUNK, LANES), lambda c: (c, dt_block)), head_vec, head_vec],
        out_specs=[blk, blk, blk, pl.BlockSpec((None, SSD_CHUNK, LANES), lambda c: (c, 0, 0)),
                   pl.BlockSpec((None, 1, LANES), lambda c: (c, 0, 0))],
        compiler_params=_cparams(("parallel",)))(zx, bias_p, alog_p)


def _ssd_common(dt_ref, cs_ref, last_ref, b_ref, c_ref):
    c_sz = SSD_CHUNK
    lane = lax.broadcasted_iota(jnp.int32, (c_sz, LANES), 1)
    row = lax.broadcasted_iota(jnp.int32, (c_sz, LANES), 0)
    bb = b_ref[...].astype(BF16)
    cb = c_ref[...].astype(BF16)
    scores = _dot(cb, bb, "nt")
    return dict(lane=lane, row=row, dt=dt_ref[...], cs=cs_ref[...], last_row=last_ref[...], bb=bb, cb=cb,
                scores=scores, causal=row >= lane, lo=lane < SSD_P)


def _pair_terms(q, cst_ref, h0):
    lane, lo = q["lane"], q["lo"]
    out = {}
    cols, dts, lasts, lms = [], [], [], []
    lane1 = lax.broadcasted_iota(jnp.int32, (1, LANES), 1)
    for h in (h0, h0 + 1):
        col = _lane_col(q["cs"], lane, h)
        rowv = cst_ref[pl.ds(h, 1), :]
        lms.append(jnp.exp(jnp.where(q["causal"], col - rowv, -1e30)))
        cols.append(col)
        dts.append(_lane_col(q["dt"], lane, h))
        lasts.append(jnp.sum(jnp.where(lane1 == h, q["last_row"], 0.0), axis=1, keepdims=True))
    out["lm"] = lms
    out["cols"] = cols
    out["lasts"] = lasts
    out["dt_b"] = jnp.where(lo, dts[0], dts[1])
    out["e_b"] = jnp.where(lo, jnp.exp(cols[0]), jnp.exp(cols[1]))
    out["dec_cols"] = [jnp.exp(lasts[0] - cols[0]), jnp.exp(lasts[1] - cols[1])]
    out["dec_b"] = jnp.where(lo, out["dec_cols"][0], out["dec_cols"][1])
    lo1 = lane1 < SSD_P
    out["explast"] = [jnp.exp(lasts[0]), jnp.exp(lasts[1])]
    out["explast_b"] = jnp.where(lo1, out["explast"][0], out["explast"][1])
    return out


def _ssd_fwd(zx, xc, decay, d_lane, nw, d_inner, after, name):
    L = zx.shape[0]
    nc = L // SSD_CHUNK
    gw = d_inner // SSD_G
    heads = gw // SSD_P
    n_pair = heads // 2
    bc0 = d_inner // LANES

    def body(z_ref, xs_ref, b_ref, c_ref, dt_ref, cs_ref, cst_ref, last_ref, dl_ref, nw_ref, after_ref,
             y_ref, yn_ref, prev_ref, s_ref):
        @pl.when(pl.program_id(1) == 0)
        def _():
            s_ref[...] = jnp.zeros_like(s_ref)

        q = _ssd_common(dt_ref, cs_ref, last_ref, b_ref, c_ref)
        prev_ref[...] = s_ref[...]
        lo = q["lo"]
        for j in range(n_pair):
            sl = slice(j * LANES, (j + 1) * LANES)
            p = _pair_terms(q, cst_ref, pl.program_id(0) * heads + 2 * j)
            xs_p = xs_ref[:, sl]
            xp = xs_p * p["dt_b"]
            xb = xp.astype(BF16)
            m_a = (q["scores"] * p["lm"][0]).astype(BF16)
            m_b = (q["scores"] * p["lm"][1]).astype(BF16)
            yd = jnp.where(lo, _dot(m_a, xb), _dot(m_b, xb))
            s_p = s_ref[:, sl]
            yo = _dot(q["cb"], s_p.astype(BF16)) * p["e_b"]
            y_ref[:, sl] = yd + yo + dl_ref[:, sl] * xs_p
            st = _dot(q["bb"], (xp * p["dec_b"]).astype(BF16), "tn")
            s_ref[:, sl] = s_p * p["explast_b"] + st
        yv = y_ref[...]
        zv = z_ref[...]
        yg = yv * (zv * _sigmoid(zv))
        rstd = lax.rsqrt(jnp.mean(yg * yg, axis=-1, keepdims=True) + NORM_EPS)
        yn_ref[...] = (yg * rstd * nw_ref[...]).astype(BF16)

    grp = lambda width: pl.BlockSpec((None, 1, width), lambda g, c: (g, 0, 0))
    dt_, _, cs_, cst_, last_ = decay
    return _pcall(
        body, name=name, grid=(SSD_G, nc),
        out_shape=[jax.ShapeDtypeStruct((L, d_inner), F32), jax.ShapeDtypeStruct((L, d_inner), BF16),
                   jax.ShapeDtypeStruct((nc, SSD_G, SSD_N, gw), F32)],
        in_specs=[pl.BlockSpec((SSD_CHUNK, gw), lambda g, c: (c, g)),
                  pl.BlockSpec((SSD_CHUNK, gw), lambda g, c: (c, g)),
                  pl.BlockSpec((SSD_CHUNK, SSD_N), lambda g, c: (c, bc0 + g)),
                  pl.BlockSpec((SSD_CHUNK, SSD_N), lambda g, c: (c, bc0 + SSD_G + g)),
                  pl.BlockSpec((SSD_CHUNK, LANES), lambda g, c: (c, 0)),
                  pl.BlockSpec((SSD_CHUNK, LANES), lambda g, c: (c, 0)),
                  pl.BlockSpec((None, SSD_CHUNK, LANES), lambda g, c: (c, 0, 0)),
                  pl.BlockSpec((None, 1, LANES), lambda g, c: (c, 0, 0)),
                  grp(gw), grp(gw), pl.BlockSpec(memory_space=pl.ANY)],
        out_specs=[pl.BlockSpec((SSD_CHUNK, gw), lambda g, c: (c, g)),
                   pl.BlockSpec((SSD_CHUNK, gw), lambda g, c: (c, g)),
                   pl.BlockSpec((None, None, SSD_N, gw), lambda g, c: (c, g, 0, 0))],
        scratch_shapes=[pltpu.VMEM((SSD_N, gw), F32)],
        compiler_params=_cparams(("parallel", "arbitrary")))(
            zx, xc, xc, xc, dt_, cs_, cst_, last_, d_lane, nw, after)


def _ssd_bwd(dyn, y, zx, xc, prev, decay, alog_p, d_lane, nw, d_inner, name):
    L = zx.shape[0]
    nc = L // SSD_CHUNK
    gw = d_inner // SSD_G
    heads = gw // SSD_P
    n_pair = heads // 2
    bc0 = d_inner // LANES

    def body(dyn_ref, y_ref, z_ref, xs_ref, b_ref, c_ref, prev_ref, dt_ref, sg_ref, cs_ref, cst_ref, last_ref,
             alog_ref, dl_ref, nw_ref,
             dz_ref, dxs_ref, db_ref, dc_ref, ddt_ref, dbias_ref, dalog_ref, dd_ref, dnw_ref,
             ds_ref, racc_ref):
        @pl.when(pl.program_id(1) == 0)
        def _():
            ds_ref[...] = jnp.zeros_like(ds_ref)
            dbias_ref[...] = jnp.zeros_like(dbias_ref)
            dalog_ref[...] = jnp.zeros_like(dalog_ref)
            dd_ref[...] = jnp.zeros_like(dd_ref)
            dnw_ref[...] = jnp.zeros_like(dnw_ref)

        q = _ssd_common(dt_ref, cs_ref, last_ref, b_ref, c_ref)
        a_row = -jnp.exp(alog_ref[...])
        lane, row, lo = q["lane"], q["row"], q["lo"]
        lane1 = lax.broadcasted_iota(jnp.int32, (1, LANES), 1)
        head0 = pl.program_id(0) * heads
        mine = (lane >= head0) & (lane < head0 + heads)

        yv, zv, dynv, nwv = y_ref[...], z_ref[...], dyn_ref[...], nw_ref[...]
        sig = _sigmoid(zv)
        sz = zv * sig
        yg = yv * sz
        rstd = lax.rsqrt(jnp.mean(yg * yg, axis=-1, keepdims=True) + NORM_EPS)
        yhat = yg * rstd
        dnw_ref[...] += jnp.sum(dynv * yhat, axis=0, keepdims=True)
        dyh = dynv * nwv
        dyg = rstd * (dyh - yhat * jnp.mean(dyh * yhat, axis=-1, keepdims=True))
        dz_ref[...] = (dyg * yv * (sig * (1.0 + zv * (1.0 - sig)))).astype(BF16)
        dy_all = dyg * sz

        dg = jnp.zeros((SSD_CHUNK, SSD_CHUNK), F32)
        dc_acc = jnp.zeros((SSD_CHUNK, SSD_N), F32)
        db_acc = jnp.zeros((SSD_CHUNK, SSD_N), F32)
        dcs_mat = jnp.zeros((SSD_CHUNK, LANES), F32)
        ddt_mat = jnp.zeros((SSD_CHUNK, LANES), F32)
        dd_row = jnp.zeros((1, LANES), F32)
        racc_ref[...] = jnp.zeros_like(racc_ref)
        is_last = row == SSD_CHUNK - 1

        for j in range(n_pair):
            sl = slice(j * LANES, (j + 1) * LANES)
            ha, hb = head0 + 2 * j, head0 + 2 * j + 1
            p = _pair_terms(q, cst_ref, ha)
            xs_p = xs_ref[:, sl]
            dyp = dy_all[:, sl]
            xp = xs_p * p["dt_b"]
            xb = xp.astype(BF16)
            s_p = prev_ref[:, sl]
            s_pb = s_p.astype(BF16)
            dsn = ds_ref[:, sl]
            dsnb = dsn.astype(BF16)
            m_f = [q["scores"] * p["lm"][0], q["scores"] * p["lm"][1]]

            t0 = dyp * xs_p
            dd_row = dd_row + jnp.where(lane1 == ha, _sum_all(jnp.where(lo, t0, 0.0)), 0.0) \
                + jnp.where(lane1 == hb, _sum_all(jnp.where(lo, 0.0, t0)), 0.0)
            dxs_p = dl_ref[:, sl] * dyp

            yo = _dot(q["cb"], s_pb) * p["e_b"]
            dcs_b = (dyp * p["e_b"]).astype(BF16)
            dc_acc = dc_acc + _dot(dcs_b, s_pb, "nt")
            ds_yo = _dot(q["cb"], dcs_b, "tn")
            t1 = dyp * yo
            dcs_cols = [jnp.sum(jnp.where(lo, t1, 0.0), axis=1, keepdims=True),
                        jnp.sum(jnp.where(lo, 0.0, t1), axis=1, keepdims=True)]

            t2 = dsn * s_p
            dlast = [p["explast"][0] * _sum_all(jnp.where(lo, t2, 0.0)),
                     p["explast"][1] * _sum_all(jnp.where(lo, 0.0, t2))]
            ds_ref[:, sl] = dsn * p["explast_b"] + ds_yo
            w = _dot(q["bb"], dsnb)
            db_acc = db_acc + _dot((xp * p["dec_b"]).astype(BF16), dsnb, "nt")
            dxp = w * p["dec_b"]
            t3 = w * xp
            e = [jnp.sum(jnp.where(lo, t3, 0.0), axis=1, keepdims=True) * p["dec_cols"][0],
                 jnp.sum(jnp.where(lo, 0.0, t3), axis=1, keepdims=True) * p["dec_cols"][1]]
            for i in range(2):
                dlast[i] = dlast[i] + jnp.sum(e[i], axis=0, keepdims=True)
                dcs_cols[i] = dcs_cols[i] - e[i]

            dyb = dyp.astype(BF16)
            dy_h = [jnp.where(lo, dyp, 0.0).astype(BF16), jnp.where(lo, 0.0, dyp).astype(BF16)]
            dms = [_dot(dy_h[0], xb, "nt"), _dot(dy_h[1], xb, "nt")]
            dxp = dxp + jnp.where(lo, _dot(m_f[0].astype(BF16), dyb, "tn"), _dot(m_f[1].astype(BF16), dyb, "tn"))
            for i, h in enumerate((ha, hb)):
                dg = dg + dms[i] * p["lm"][i]
                qm = dms[i] * m_f[i]
                dcs_cols[i] = dcs_cols[i] + jnp.sum(qm, axis=1, keepdims=True)
                racc_ref[pl.ds(h, 1), :] = jnp.sum(qm, axis=0, keepdims=True)

            dxs_ref[:, sl] = dxs_p + dxp * p["dt_b"]
            t4 = dxp * xs_p
            ddt_cols = [jnp.sum(jnp.where(lo, t4, 0.0), axis=1, keepdims=True),
                        jnp.sum(jnp.where(lo, 0.0, t4), axis=1, keepdims=True)]
            for i, h in enumerate((ha, hb)):
                sel = lane == h
                dcs_mat = dcs_mat + jnp.where(sel, dcs_cols[i], 0.0) + jnp.where(sel & is_last, dlast[i], 0.0)
                ddt_mat = ddt_mat + jnp.where(sel, ddt_cols[i], 0.0)

        dcs_mat = dcs_mat - racc_ref[...].T
        tri_t = (row <= lane).astype(BF16)
        da = _dot_exact01(tri_t, dcs_mat)
        ddt = ddt_mat + da * a_row
        dalog_ref[...] += jnp.sum(jnp.where(mine, da * q["dt"], 0.0), axis=0, keepdims=True) * a_row
        draw = jnp.where(mine, ddt * sg_ref[...], 0.0)
        ddt_ref[...] = draw
        dbias_ref[...] += jnp.sum(draw, axis=0, keepdims=True)
        dd_ref[...] += dd_row
        dgb = dg.astype(BF16)
        dc_ref[...] = dc_acc + _dot(dgb, q["bb"])
        db_ref[...] = db_acc + _dot(dgb, q["cb"], "tn")

    rev = lambda c: nc - 1 - c
    grp = lambda width: pl.BlockSpec((None, 1, width), lambda g, c: (g, 0, 0))
    blk = lambda width, off: pl.BlockSpec((SSD_CHUNK, width), lambda g, c: (rev(c), off + g))
    head_vec = pl.BlockSpec((1, LANES), lambda g, c: (0, 0))
    chunk_rows = pl.BlockSpec((SSD_CHUNK, LANES), lambda g, c: (rev(c), 0))
    dt_, sg_, cs_, cst_, last_ = decay
    return _pcall(
        body, name=name, grid=(SSD_G, nc),
        out_shape=[jax.ShapeDtypeStruct(zx.shape, BF16), jax.ShapeDtypeStruct((L, d_inner), F32),
                   jax.ShapeDtypeStruct((L, SSD_G * SSD_N), F32), jax.ShapeDtypeStruct((L, SSD_G * SSD_N), F32),
                   jax.ShapeDtypeStruct((SSD_G, L, LANES), F32),
                   jax.ShapeDtypeStruct((SSD_G, 1, LANES), F32), jax.ShapeDtypeStruct((SSD_G, 1, LANES), F32),
                   jax.ShapeDtypeStruct((SSD_G, 1, LANES), F32), jax.ShapeDtypeStruct((SSD_G, 1, gw), F32)],
        in_specs=[blk(gw, 0), blk(gw, 0), blk(gw, 0), blk(gw, 0), blk(SSD_N, bc0), blk(SSD_N, bc0 + SSD_G),
                  pl.BlockSpec((None, None, SSD_N, gw), lambda g, c: (rev(c), g, 0, 0)),
                  chunk_rows, chunk_rows, chunk_rows,
                  pl.BlockSpec((None, SSD_CHUNK, LANES), lambda g, c: (rev(c), 0, 0)),
                  pl.BlockSpec((None, 1, LANES), lambda g, c: (rev(c), 0, 0)),
                  head_vec, grp(gw), grp(gw)],
        out_specs=[blk(gw, 0), blk(gw, 0), blk(SSD_N, 0), blk(SSD_N, 0),
                   pl.BlockSpec((None, SSD_CHUNK, LANES), lambda g, c: (g, rev(c), 0)),
                   grp(LANES), grp(LANES), grp(LANES), grp(gw)],
        scratch_shapes=[pltpu.VMEM((SSD_N, gw), F32), pltpu.VMEM((SSD_CHUNK, LANES), F32)],
        compiler_params=_cparams(("parallel", "arbitrary")))(
            dyn, y, zx, xc, xc, xc, prev, dt_, sg_, cs_, cst_, last_, alog_p, d_lane, nw)


def _cond_mod(c_pad, ada_w, ada_b_loc, after, name):
    depth, D, n = ada_w.shape
    rows = c_pad.shape[0]

    def body(c_ref, w_ref, b_ref, after_ref, mod_ref, cond_ref):
        cv = c_ref[...]
        cond = cv * _sigmoid(cv)
        cond_ref[...] = cond
        mod_ref[...] = _dot(cond.astype(BF16), w_ref[...].astype(BF16)) + b_ref[...]

    return _pcall(
        body, name=name, grid=(depth,),
        out_shape=[jax.ShapeDtypeStruct((depth, rows, n), F32), jax.ShapeDtypeStruct((rows, D), F32)],
        in_specs=[pl.BlockSpec((rows, D), lambda i: (0, 0)),
                  pl.BlockSpec((None, D, n), lambda i: (i, 0, 0)),
                  pl.BlockSpec((None, 1, n), lambda i: (i, 0, 0)),
                  pl.BlockSpec(memory_space=pl.ANY)],
        out_specs=[pl.BlockSpec((None, rows, n), lambda i: (i, 0, 0)),
                   pl.BlockSpec((rows, D), lambda i: (0, 0))],
        compiler_params=_cparams(("arbitrary",)))(c_pad, ada_w, ada_b_loc, after)


def _adamw_math(g, w, m, v):
    m_new = ADAM_B1 * m + (1.0 - ADAM_B1) * g
    v_new = ADAM_B2 * v + (1.0 - ADAM_B2) * (g * g)
    m_hat = m_new / (1.0 - ADAM_B1 ** ADAM_STEP)
    v_hat = v_new / (1.0 - ADAM_B2 ** ADAM_STEP)
    delta = -ADAM_LR * (m_hat / (jnp.sqrt(v_hat) + ADAM_EPS) + ADAM_WD * w)
    return delta, m_new, v_new


def _adamw_sum(parts, w, m, v, layer, name, prev=None, tr=None, window_off=None):
    depth, R, C = w.shape
    tr = _tile(R, tr if tr is not None else (512 if C <= 512 else 256))
    win = parts.shape[2]
    scratch = [] if window_off is None else [pltpu.VMEM((tr, win), F32)]

    def body(p_ref, w_ref, m_ref, v_ref, *rest):
        g_ref, d_ref, mo_ref, vo_ref = rest[-4 - len(scratch):len(rest) - len(scratch)]
        g = p_ref[0].astype(F32)
        for k in range(1, N_DEV):
            g = g + p_ref[k].astype(F32)
        if window_off is not None:
            me = _my_index()
            off = 0
            for k in range(N_DEV):
                off = jnp.where(me == k, window_off[k], off)
            src = lax.broadcasted_iota(jnp.int32, (win, win), 0)
            dst = lax.broadcasted_iota(jnp.int32, (win, win), 1)
            shift = ((src == dst + off) & (dst < C)).astype(BF16)
            hi, mid, lo = _split3(g)
            rest[-1][...] = _dot(hi, shift) + _dot(mid, shift) + _dot(lo, shift)
            g = rest[-1][:, 0:C]
        d, mn, vn = _adamw_math(g, w_ref[...], m_ref[...], v_ref[...])
        g_ref[...] = g
        d_ref[...] = d
        mo_ref[...] = mn
        vo_ref[...] = vn

    blk = pl.BlockSpec((None, tr, C), lambda i: (layer, i, 0))
    prev = list(prev) if prev is not None else []
    return _pcall(
        body, name=name, grid=(R // tr,),
        out_shape=[jax.ShapeDtypeStruct((depth, R, C), F32)] * 4,
        in_specs=[pl.BlockSpec((N_DEV, tr, win), lambda i: (0, i, 0)), blk, blk, blk]
        + [pl.BlockSpec(memory_space=pl.ANY)] * len(prev),
        out_specs=[blk] * 4, input_output_aliases={4 + k: k for k in range(len(prev))},
        scratch_shapes=scratch,
        compiler_params=_cparams(("parallel",)))(parts, w, m, v, *prev)


def _adamw_small(parts, wmv, head_parts, head_wmv, loss_parts, name):
    n, nh = len(parts), len(head_parts)
    n_heads = head_wmv[0][0].shape[1] if nh else 0
    groups = head_parts[0].shape[1] if nh else 0
    d_model = loss_parts.shape[2]

    def body(*refs):
        p_refs, refs = refs[:n], refs[n:]
        wmv_refs, refs = refs[:3 * n], refs[3 * n:]
        hp_refs, refs = refs[:nh], refs[nh:]
        hwmv_refs, refs = refs[:3 * nh], refs[3 * nh:]
        loss_ref, refs = refs[0], refs[1:]
        outs, loss_out, head_scr = refs[:4 * (n + nh)], refs[4 * (n + nh)], refs[4 * (n + nh) + 1]

        def update(i, g, w_ref, m_ref, v_ref):
            res = (g,) + _adamw_math(g, w_ref[...], m_ref[...], v_ref[...])
            for o_ref, r in zip(outs[4 * i:4 * i + 4], res):
                o_ref[...] = r

        for i in range(n):
            g = p_refs[i][0]
            for k in range(1, N_DEV):
                g = g + p_refs[i][k]
            update(i, g, *wmv_refs[3 * i:3 * i + 3])
        for i in range(nh):
            g = None
            for k in range(N_DEV):
                for grp in range(groups):
                    g = hp_refs[i][k, grp] if g is None else g + hp_refs[i][k, grp]
            head_scr[...] = g
            update(n + i, head_scr[:, 0:n_heads], *hwmv_refs[3 * i:3 * i + 3])
        tot = loss_ref[0]
        for k in range(1, N_DEV):
            tot = tot + loss_ref[k]
        loss_out[...] = jnp.broadcast_to(_sum_all(tot) * (0.5 / d_model), loss_out.shape)

    operands = list(parts) + [a for t in wmv for a in t] + list(head_parts) + [a for t in head_wmv for a in t]
    operands.append(loss_parts)
    out_shape = [jax.ShapeDtypeStruct(t[0].shape, F32) for t in list(wmv) + list(head_wmv) for _ in range(4)]
    out_shape.append(jax.ShapeDtypeStruct((1, LANES), F32))
    vmem = pl.BlockSpec(memory_space=pltpu.VMEM)
    outs = _pcall(body, name=name, out_shape=out_shape, in_specs=[vmem] * len(operands),
                  out_specs=[vmem] * len(out_shape), scratch_shapes=[pltpu.VMEM((1, LANES), F32)],
                  compiler_params=_cparams())(*operands)
    return [outs[4 * i:4 * i + 4] for i in range(n + nh)], outs[-1]


def _ada_adamw(cond_pad, dmod_pad, w, m, v, name, tr=512):
    depth, D, n = w.shape
    rows = cond_pad.shape[0]
    tr = _tile(D, tr)

    def body(c_ref, dm_ref, w_ref, m_ref, v_ref, g_ref, d_ref, mo_ref, vo_ref):
        g = _dot(c_ref[...].astype(BF16), dm_ref[...].astype(BF16), "tn")
        d, mn, vn = _adamw_math(g, w_ref[...], m_ref[...], v_ref[...])
        g_ref[...] = g
        d_ref[...] = d
        mo_ref[...] = mn
        vo_ref[...] = vn

    blk = pl.BlockSpec((None, tr, n), lambda i, r: (i, r, 0))
    return _pcall(
        body, name=name, grid=(depth, D // tr),
        out_shape=[jax.ShapeDtypeStruct((depth, D, n), F32)] * 4,
        in_specs=[pl.BlockSpec((rows, tr), lambda i, r: (0, r)),
                  pl.BlockSpec((None, rows, n), lambda i, r: (i, 0, 0)), blk, blk, blk],
        out_specs=[blk] * 4, compiler_params=_cparams(("parallel", "parallel")))(cond_pad, dmod_pad, w, m, v)


def kernel(x, c, ada_w, ada_b, mix_norm_w, mlp_norm_w, mlp_up, mlp_down, ssd_in_w, ssd_conv_w, ssd_conv_b, ssd_dt_bias, ssd_A_log, ssd_D, ssd_norm_w, ssd_out_w, sc_in_w, sc_conv_w, sc_out_w, final_norm_w, loss_target, m_ada_w, m_ada_b, m_mix_norm_w, m_mlp_norm_w, m_mlp_up, m_mlp_down, m_ssd_in_w, m_ssd_conv_w, m_ssd_conv_b, m_ssd_dt_bias, m_ssd_A_log, m_ssd_D, m_ssd_norm_w, m_ssd_out_w, m_sc_in_w, m_sc_conv_w, m_sc_out_w, m_final_norm_w, v_ada_w, v_ada_b, v_mix_norm_w, v_mlp_norm_w, v_mlp_up, v_mlp_down, v_ssd_in_w, v_ssd_conv_w, v_ssd_conv_b, v_ssd_dt_bias, v_ssd_A_log, v_ssd_D, v_ssd_norm_w, v_ssd_out_w, v_sc_in_w, v_sc_conv_w, v_sc_out_w, v_final_norm_w):
    weights = dict(ada_w=ada_w, ada_b=ada_b, mix_norm_w=mix_norm_w, mlp_norm_w=mlp_norm_w, mlp_up=mlp_up,
                   mlp_down=mlp_down, ssd_in_w=ssd_in_w, ssd_conv_w=ssd_conv_w, ssd_conv_b=ssd_conv_b,
                   ssd_dt_bias=ssd_dt_bias, ssd_A_log=ssd_A_log, ssd_D=ssd_D, ssd_norm_w=ssd_norm_w,
                   ssd_out_w=ssd_out_w, sc_in_w=sc_in_w, sc_conv_w=sc_conv_w, sc_out_w=sc_out_w,
                   final_norm_w=final_norm_w)
    moms = dict(ada_w=m_ada_w, ada_b=m_ada_b, mix_norm_w=m_mix_norm_w, mlp_norm_w=m_mlp_norm_w, mlp_up=m_mlp_up,
                mlp_down=m_mlp_down, ssd_in_w=m_ssd_in_w, ssd_conv_w=m_ssd_conv_w, ssd_conv_b=m_ssd_conv_b,
                ssd_dt_bias=m_ssd_dt_bias, ssd_A_log=m_ssd_A_log, ssd_D=m_ssd_D, ssd_norm_w=m_ssd_norm_w,
                ssd_out_w=m_ssd_out_w, sc_in_w=m_sc_in_w, sc_conv_w=m_sc_conv_w, sc_out_w=m_sc_out_w,
                final_norm_w=m_final_norm_w)
    vars_ = dict(ada_w=v_ada_w, ada_b=v_ada_b, mix_norm_w=v_mix_norm_w, mlp_norm_w=v_mlp_norm_w, mlp_up=v_mlp_up,
                 mlp_down=v_mlp_down, ssd_in_w=v_ssd_in_w, ssd_conv_w=v_ssd_conv_w, ssd_conv_b=v_ssd_conv_b,
                 ssd_dt_bias=v_ssd_dt_bias, ssd_A_log=v_ssd_A_log, ssd_D=v_ssd_D, ssd_norm_w=v_ssd_norm_w,
                 ssd_out_w=v_ssd_out_w, sc_in_w=v_sc_in_w, sc_conv_w=v_sc_conv_w, sc_out_w=v_sc_out_w,
                 final_norm_w=v_final_norm_w)
    names = list(weights)

    L, D = x.shape[1], x.shape[2]
    d_inner = 2 * D
    n_heads = d_inner // SSD_P
    hpg = n_heads // SSD_G
    gw = d_inner // SSD_G
    conv_dim = d_inner + 2 * SSD_G * SSD_N
    zx_dim = d_inner + conv_dim
    zx_pad = -(-(zx_dim + LANES) // 512) * 512
    in_ws = ssd_in_w.shape[2]
    in_base, in_off, in_win = _window_geometry(in_ws)
    me = _my_index()
    x0 = x[0]
    tgt = loss_target[0]

    n_mod = ada_w.shape[2]
    (c_all,) = _exchange([c], "gather_c", gather=True)
    gather_handle = {}
    (gather_handle["ssd_in_w"],), token_in = _xfer_start(
        [ssd_in_w[0].astype(BF16)], "gather_start_ssd_in_w", gather=True, via_sibling=(0,), after=(c_all,))
    c_pad = jnp.pad(c_all.reshape(N_DEV, D), ((0, 16 - N_DEV), (0, 0)))
    ada_b_loc = lax.dynamic_slice_in_dim(ada_b, me * n_mod, n_mod, axis=1).reshape(2, 1, n_mod)
    mod_blk, cond_pad = _cond_mod(c_pad, ada_w, ada_b_loc, token_in, "cond_mod")
    gather_order = ["mod", "ssd_conv_w", "sc_conv_w", "ssd_out_w", "up0", "down0", "sc_in_w", "sc_out_w", "up1",
                    "down1"]
    gather_src = dict(mod=mod_blk, ssd_conv_w=ssd_conv_w[0], sc_conv_w=sc_conv_w[0],
                      ssd_out_w=ssd_out_w[0].astype(BF16),
                      up0=mlp_up[0].astype(BF16), down0=mlp_down[0].astype(BF16),
                      sc_in_w=sc_in_w[0].astype(BF16), sc_out_w=sc_out_w[0].astype(BF16),
                      up1=mlp_up[1].astype(BF16), down1=mlp_down[1].astype(BF16))
    handles, gather_token = _xfer_start([gather_src[k] for k in gather_order], "gather_start", gather=True,
                                        via_sibling=tuple(range(3, len(gather_order))))
    gather_handle.update(zip(gather_order, handles))

    def gathered(keys, after, forward):
        tag = "_".join(keys)
        lands = _xfer_wait([gather_handle[k] for k in keys], after, f"gather_wait_{tag}", gather=True)
        return _sibling_forward(lands, f"gather_forward_{tag}") if forward else lands

    def forward_behind(keys, after):
        tag = "_".join(keys)
        lands = _xfer_wait([gather_handle[k] for k in keys], after, f"gather_wait_{tag}", gather=True)
        fwd_handles, token = _sibling_forward_start(lands, f"gather_forward_start_{tag}")
        return (lambda done: _sibling_forward_wait(fwd_handles, done, f"gather_forward_wait_{tag}")), token

    (ssd_in_g,) = gathered(["ssd_in_w"], (gather_token, m_ssd_in_w, v_ssd_in_w), True)
    w_in_all = _shards_to_columns(ssd_in_g, in_base, in_off, in_win, zx_pad, "ssd_in_w_columns")
    (mod_all,) = gathered(["mod"], w_in_all, False)
    mod_mine = lax.dynamic_index_in_dim(mod_all, me, axis=2, keepdims=False)
    mod_mine = jnp.transpose(mod_mine, (1, 0, 2)).reshape(2, 6, 1, D)
    sh_m, sc_m, g_m, sh_f, sc_f, g_f = [[mod_mine[i, k] for i in range(2)] for k in range(6)]

    vec = lambda a: a.reshape(1, -1)
    grads = {}
    small = {}

    _, h0 = _norm_mod_fwd(x0, None, None, vec(mix_norm_w[0]), sc_m[0], sh_m[0], "l0_mix_norm")
    cw_all, scw_all = gathered(["ssd_conv_w", "sc_conv_w"], h0, False)
    (zx,) = _mm_nn(h0, w_in_all, F32, "ssd_in_proj", tm=2048, tn=512)
    conv_b0 = vec(ssd_conv_b[0])
    conv_w_full = jnp.transpose(cw_all, (1, 0, 2)).reshape(SSD_K, conv_dim)
    sc_conv_full = jnp.transpose(scw_all, (1, 0, 2)).reshape(SC_K, D)
    xc = _ssd_conv_fwd(zx, conv_w_full, conv_b0, d_inner, conv_dim, "ssd_conv")
    bias_p = jnp.pad(ssd_dt_bias[0], (0, LANES - n_heads)).reshape(1, LANES)
    alog_p = jnp.pad(ssd_A_log[0], (0, LANES - n_heads)).reshape(1, LANES)
    d_lane = jnp.repeat(ssd_D[0], SSD_P).reshape(SSD_G, 1, gw)
    nw_g = ssd_norm_w[0].reshape(SSD_G, 1, gw)
    finish, token = forward_behind(["ssd_out_w"], xc)
    decay = _ssd_decay(zx, bias_p, alog_p, n_heads, zx_dim // LANES, "ssd_decay")
    y_ssd, yn, prev = _ssd_fwd(zx, xc, decay, d_lane, nw_g, d_inner, token, "ssd_scan")
    ups, downs = [None, None], [None, None]
    (ssd_out_g,) = finish(yn)
    w_ssd_out = ssd_out_g.reshape(-1, D)
    finish, token = forward_behind(["up0", "down0"], ssd_out_g)
    (mix0,) = _mm_nn(yn, w_ssd_out, F32, "ssd_out_proj", after=(token,))
    x1, h1 = _norm_mod_fwd(x0, mix0, g_m[0], vec(mlp_norm_w[0]), sc_f[0], sh_f[0], "l0_mlp_norm")
    ups[0], down0_g = finish(h1)
    downs[0] = down0_g.reshape(-1, D)
    u0, s0 = _mm_nn_blocked(h1, ups[0], "l0_mlp_up", _ep_relu2, [BF16, BF16])
    finish, token = forward_behind(["sc_in_w", "sc_out_w", "up1", "down1"], s0)
    (d0,) = _mm_nn(s0, downs[0], F32, "l0_mlp_down", after=(token,))
    x2, h2 = _norm_mod_fwd(x1, d0, g_f[0], vec(mix_norm_w[1]), sc_m[1], sh_m[1], "l1_mix_norm")
    sc_in_g, sc_out_g, ups[1], down1_g = finish(h2)
    w_sc_out, downs[1] = sc_out_g.reshape(-1, D), down1_g.reshape(-1, D)
    (proj,) = _mm_nn_blocked(h2, sc_in_g, "sc_in_proj", _ep_store(F32), [F32])
    yc = _sc_conv_fwd(proj, sc_conv_full, "sc_conv")
    (mix1,) = _mm_nn(yc, w_sc_out, F32, "sc_out_proj")
    x3, h3 = _norm_mod_fwd(x2, mix1, g_m[1], vec(mlp_norm_w[1]), sc_f[1], sh_f[1], "l1_mlp_norm")
    u1, s1 = _mm_nn_blocked(h3, ups[1], "l1_mlp_up", _ep_relu2, [BF16, BF16])
    (d1,) = _mm_nn(s1, downs[1], F32, "l1_mlp_down")

    dx, loss_lane, dfw, dd1, dg = _final_loss(x3, d1, g_f[1], vec(final_norm_w), tgt, "final_loss")
    small["final_norm_w"] = dfw

    dmod = [[None] * 6 for _ in range(2)]
    dmod[1][5] = dg

    def mlp_backward(i, dx_out, dd, x_mid, h_in, u, s, mix, gate):
        du = _mm_nt(dd, downs[i], BF16, f"l{i}_mlp_down_bwd", epilogue=_ep_relu2_bwd, extra=(u,))
        gdown = _mm_tn(s, dd, BF16, f"l{i}_mlp_down_wgrad").reshape(N_DEV, -1, D)
        gup = _mm_tn_blocked(h_in, du, BF16, f"l{i}_mlp_up_wgrad")
        (h_down, h_up), token = _xfer_start([gdown, gup], f"l{i}_mlp_grads_start", gather=False)
        grad_handle[f"mlp_down{i}"], grad_handle[f"mlp_up{i}"] = h_down, h_up
        dh = _mm_nt_blocked(du, ups[i], F32, f"l{i}_mlp_up_bwd", after=(token,))
        dxm, dsh, dsc, dnw, dmix, dgate = _norm_mod_bwd(dh, x_mid, vec(mlp_norm_w[i]), sc_f[i], dx_out,
                                                        f"l{i}_mlp_norm_bwd", branch=(mix, gate))
        dmod[i][3], dmod[i][4], dmod[i][2] = dsh, dsc, dgate
        return dxm, dmix, dnw

    grad_handle = {}
    dx3, dyc, dnw_mlp1 = mlp_backward(1, dx, dd1, x3, h3, u1, s1, mix1, g_m[1])
    g_sc_out = _mm_tn(yc, dyc, BF16, "sc_out_wgrad").reshape(N_DEV, -1, D)
    dconv_out = _mm_nt(dyc, w_sc_out, F32, "sc_out_bwd")
    dbg, dcg, dxv, dscw = _sc_conv_bwd(proj, sc_conv_full, dconv_out, "sc_conv_bwd")
    dproj = jnp.concatenate([dbg, dcg, dxv], axis=1)
    g_sc_in = _mm_tn_blocked(h2, dproj, BF16, "sc_in_wgrad")
    (grad_handle["sc_out_w0"], grad_handle["sc_in_w0"]), token = _xfer_start(
        [g_sc_out, g_sc_in], "sc_grads_start", gather=False)
    dh2 = _mm_nt_blocked(dproj, sc_in_g, F32, "sc_in_bwd", after=(token,))
    dx2, dsh, dsc, dnw_mix1, dd0, dg = _norm_mod_bwd(dh2, x2, vec(mix_norm_w[1]), sc_m[1], dx3, "l1_mix_norm_bwd",
                                                     branch=(d0, g_f[0]))
    dmod[1][0], dmod[1][1], dmod[0][5] = dsh, dsc, dg
    dx1, dyo, dnw_mlp0 = mlp_backward(0, dx2, dd0, x1, h1, u0, s0, mix0, g_m[0])
    g_ssd_out = _mm_tn(yn, dyo, BF16, "ssd_out_wgrad").reshape(N_DEV, -1, D)
    (grad_handle["ssd_out_w0"],), token = _xfer_start([g_ssd_out], "ssd_out_grad_start", gather=False)
    dyn = _mm_nt(dyo, w_ssd_out, F32, "ssd_out_bwd", after=(token,))
    dz, dxs, db_, dc_, ddt, dbias, dalog, dd_, dnw_ssd = _ssd_bwd(
        dyn, y_ssd, zx, xc, prev, decay, alog_p, d_lane, nw_g, d_inner, "ssd_scan_bwd")
    dzx, dcw, dcb = _ssd_conv_bwd(zx, conv_w_full, conv_b0, [dxs, db_, dc_], dz, d_inner, "ssd_conv_bwd")
    dzx = _dzx_finish(dzx, ddt, zx_dim, "ssd_dzx_finish")
    g_in_all = _mm_tn(h0, dzx, BF16, "ssd_in_wgrad", tn=512, tk=2048)
    g_ssd_in = jnp.stack([g_in_all[:, b:b + in_win] for b in in_base], axis=0)
    (grad_handle["ssd_in_w0"],), token = _xfer_start([g_ssd_in], "ssd_in_grad_start", gather=False)
    dh0 = _mm_nt(dzx, w_in_all, F32, "ssd_in_bwd", tm=1024, tk=dzx.shape[1] // 2, after=(token,))
    grad_x, dsh, dsc, dnw_mix0 = _norm_mod_bwd(dh0, x0, vec(mix_norm_w[0]), sc_m[0], dx1, "l0_mix_norm_bwd")
    dmod[0][0], dmod[0][1] = dsh, dsc

    small["ada_b"] = jnp.concatenate([jnp.concatenate(dmod[i], axis=1) for i in range(2)], axis=0)
    small["mix_norm_w"] = jnp.concatenate([dnw_mix0, dnw_mix1], axis=0)
    small["mlp_norm_w"] = jnp.concatenate([dnw_mlp0, dnw_mlp1], axis=0)
    small["ssd_conv_w"] = dcw
    small["ssd_conv_b"] = dcb
    small["ssd_norm_w"] = dnw_ssd.reshape(1, d_inner)
    small["sc_conv_w"] = dscw
    small["loss"] = loss_lane
    small_names = list(small)
    head_names = ["ssd_dt_bias", "ssd_A_log", "ssd_D"]
    handles, small_token = _xfer_start([small[k] for k in small_names] + [dbias, dalog, dd_],
                                       "small_grads_start", gather=True)

    out_g, out_d, out_m, out_v = {}, {}, {}, {}

    layer_res = {}

    def big_update(name, i, after):
        (parts,) = _xfer_wait([grad_handle[f"{name}{i}"]], after, f"grads_wait_{name}_{i}", gather=False)
        res = _adamw_sum(parts, weights[name], moms[name], vars_[name], i, f"adamw_{name}_{i}",
                         prev=layer_res.get(name), window_off=in_off if name == "ssd_in_w" else None)
        layer_res[name] = res
        return res[1]

    chain = small_token
    for name, i in [("mlp_down", 1), ("mlp_up", 1), ("sc_out_w", 0), ("sc_in_w", 0), ("mlp_down", 0),
                    ("mlp_up", 0), ("ssd_out_w", 0), ("ssd_in_w", 0)]:
        chain = big_update(name, i, chain)
    gathered_small = _xfer_wait(handles, chain, "small_grads_wait", gather=True)
    small_all = dict(zip(small_names + head_names, gathered_small))

    dmod_loc = lax.dynamic_slice_in_dim(small_all["ada_b"], me * n_mod, n_mod, axis=2)
    dmod_pad = jnp.pad(jnp.transpose(dmod_loc, (1, 0, 2)), ((0, 0), (0, 16 - N_DEV), (0, 0)))
    out_g["ada_w"], out_d["ada_w"], out_m["ada_w"], out_v["ada_w"] = _ada_adamw(
        cond_pad, dmod_pad, ada_w, m_ada_w, v_ada_w, "adamw_ada_w")

    for k in ("ssd_conv_w", "sc_conv_w"):
        n_loc = weights[k].shape[2]
        small_all[k] = lax.dynamic_slice_in_dim(small_all[k], me * n_loc, n_loc, axis=2)
    plain = [k for k in small_names if k != "loss"]
    as2d = lambda a: a.reshape(-1, a.shape[-1])
    res, loss_row = _adamw_small(
        [small_all[k] for k in plain], [tuple(as2d(d[k]) for d in (weights, moms, vars_)) for k in plain],
        [small_all[k] for k in head_names], [tuple(as2d(d[k]) for d in (weights, moms, vars_)) for k in head_names],
        small_all["loss"], "adamw_small")
    loss = loss_row[0, 0]
    for k, res4 in zip(plain + head_names, res):
        for r, dst in zip(res4, (out_g, out_d, out_m, out_v)):
            dst[k] = r.reshape(weights[k].shape)
    for name, res4 in layer_res.items():
        for r, dst in zip(res4, (out_g, out_d, out_m, out_v)):
            dst[name] = r

    return (loss, grad_x[None], *[out_g[k] for k in names], *[out_d[k] for k in names],
            *[out_m[k] for k in names], *[out_v[k] for k in names])
```

```python
import functools

import jax
import jax.numpy as jnp
from jax import lax
from jax.experimental import pallas as pl
from jax.experimental.pallas import tpu as pltpu

F32 = jnp.float32
BF16 = jnp.bfloat16
N_DEV = 8
MESH_AXES = ("x", "y", "c")
MESH = pl.DeviceIdType.MESH

NORM_EPS = 1e-5
SSD_G = 4
SSD_P = 64
SSD_N = 128
SSD_CHUNK = 128
SSD_K = 4
SC_K = 3
LANES = 128

ADAM_LR = 0.001
ADAM_B1 = 0.9
ADAM_B2 = 0.999
ADAM_EPS = 1e-08
ADAM_WD = 0.01
ADAM_STEP = 10

VMEM_LIMIT = 56 * 1024 * 1024


def _pcall(body, **kw):
    return pl.pallas_call(body, **kw)


def _cparams(sem=None):
    if sem is None:
        return pltpu.CompilerParams(vmem_limit_bytes=VMEM_LIMIT)
    return pltpu.CompilerParams(dimension_semantics=sem, vmem_limit_bytes=VMEM_LIMIT)


def _my_index():
    return 4 * lax.axis_index("x") + 2 * lax.axis_index("y") + lax.axis_index("c")


_PEER_MASKS = [(0, 0, 1), (0, 1, 0), (0, 1, 1), (1, 0, 0), (1, 0, 1), (1, 1, 0), (1, 1, 1)]


def _peers():
    x, y, c = lax.axis_index("x"), lax.axis_index("y"), lax.axis_index("c")
    out = []
    for mx, my, mc in _PEER_MASKS:
        px = (1 - x) if mx else x
        py = (1 - y) if my else y
        pc = (1 - c) if mc else c
        out.append(((px, py, pc), 4 * px + 2 * py + pc))
    return out


def _exchange(arrs, name, gather):
    n = len(arrs)
    n_peer = N_DEV - 1

    def body(*refs):
        ins, outs = refs[:n], refs[n:2 * n]
        send_sems, recv_sems, local_sems = refs[2 * n:]
        me = _my_index()
        peers = _peers()
        started = []
        for a in range(n):
            src_own = ins[a] if gather else ins[a].at[me]
            own = pltpu.make_async_copy(src_own, outs[a].at[me], local_sems.at[a])
            own.start()
            started.append(own)
        sends = []
        for a in range(n):
            for k, (peer, pidx) in enumerate(peers):
                src = ins[a] if gather else ins[a].at[pidx]
                cp = pltpu.make_async_remote_copy(
                    src_ref=src, dst_ref=outs[a].at[me],
                    send_sem=send_sems.at[a * n_peer + k], recv_sem=recv_sems.at[a * n_peer + k],
                    device_id=peer, device_id_type=MESH)
                cp.start()
                sends.append(cp)
        for a in range(n):
            for k, (peer, pidx) in enumerate(peers):
                src = ins[a] if gather else ins[a].at[pidx]
                pltpu.make_async_remote_copy(
                    src_ref=src, dst_ref=outs[a].at[pidx],
                    send_sem=send_sems.at[a * n_peer + k], recv_sem=recv_sems.at[a * n_peer + k],
                    device_id=peer, device_id_type=MESH).wait_recv()
        for cp in sends:
            cp.wait_send()
        for own in started:
            own.wait()

    if gather:
        out_shape = [jax.ShapeDtypeStruct((N_DEV,) + a.shape, a.dtype) for a in arrs]
    else:
        out_shape = [jax.ShapeDtypeStruct(a.shape, a.dtype) for a in arrs]
    any_spec = pl.BlockSpec(memory_space=pl.ANY)
    outs = _pcall(
        body, name=name, out_shape=out_shape,
        in_specs=[any_spec] * n, out_specs=[any_spec] * n,
        scratch_shapes=[pltpu.SemaphoreType.DMA((n * n_peer,)), pltpu.SemaphoreType.DMA((n * n_peer,)),
                        pltpu.SemaphoreType.DMA((n,))],
        compiler_params=pltpu.CompilerParams(has_side_effects=True),
    )(*arrs)
    return list(outs)


def _sibling_forward_start(lands, name):
    n = len(lands)
    n_fwd = len(_OTHER_CHIPS)

    def body(*refs):
        ins, bufs = refs[:n], refs[3 * n:4 * n]
        token = refs[-1]
        sibling = (lax.axis_index("x"), lax.axis_index("y"), 1 - lax.axis_index("c"))
        peers = _peers()
        for a in range(n):
            send_sems, recv_sems = refs[n + 2 * a], refs[n + 2 * a + 1]
            for j, k in enumerate(_OTHER_CHIPS):
                slot = peers[k][1]
                pltpu.make_async_remote_copy(
                    src_ref=ins[a].at[slot], dst_ref=bufs[a].at[slot], send_sem=send_sems.at[j],
                    recv_sem=recv_sems.at[j], device_id=sibling, device_id_type=MESH).start()
        token[...] = jnp.zeros_like(token)

    out_shape, out_specs = [], []
    for _ in range(n):
        out_shape += [pltpu.SemaphoreType.DMA((n_fwd,)), pltpu.SemaphoreType.DMA((n_fwd,))]
        out_specs += [_SEM, _SEM]
    out_shape += [pltpu.HBM(a.shape, a.dtype) for a in lands] + [jax.ShapeDtypeStruct((8, LANES), F32)]
    out_specs += [_HBM] * n + [pl.BlockSpec(memory_space=pltpu.VMEM)]
    outs = _pcall(
        body, name=name, out_shape=tuple(out_shape), in_specs=[_HBM] * n, out_specs=tuple(out_specs),
        input_output_aliases={a: 2 * n + a for a in range(n)},
        compiler_params=pltpu.CompilerParams(has_side_effects=_DATAFLOW),
    )(*[pltpu.with_memory_space_constraint(a, pltpu.HBM) for a in lands])
    return [(outs[2 * n + a], outs[2 * a], outs[2 * a + 1]) for a in range(n)], outs[-1]


def _sibling_forward_wait(handles, after, name):
    n = len(handles)

    def body(*refs):
        sibling = (lax.axis_index("x"), lax.axis_index("y"), 1 - lax.axis_index("c"))
        peers = _peers()
        for a in range(n):
            buf, send_sems, recv_sems = refs[3 * a:3 * a + 3]
            for j, k in enumerate(_OTHER_CHIPS):
                (px, py, pc), slot = peers[k]
                theirs = 4 * px + 2 * py + (1 - pc)
                cp = pltpu.make_async_remote_copy(
                    src_ref=buf.at[slot], dst_ref=buf.at[theirs], send_sem=send_sems.at[j],
                    recv_sem=recv_sems.at[j], device_id=sibling, device_id_type=MESH)
                cp.wait_send()
                cp.wait_recv()

    operands, in_specs = [], []
    for h in handles:
        operands += list(h)
        in_specs += [_HBM, _SEM, _SEM]
    outs = _pcall(
        body, name=name, out_shape=tuple(pltpu.HBM(h[0].shape, h[0].dtype) for h in handles),
        in_specs=in_specs + [pl.BlockSpec(memory_space=pl.ANY)], out_specs=tuple([_HBM] * n),
        input_output_aliases={3 * a: a for a in range(n)},
        compiler_params=pltpu.CompilerParams(has_side_effects=_DATAFLOW),
    )(*operands, after)
    return list(outs)


_HBM = pl.BlockSpec(memory_space=pltpu.HBM)
_SEM = pl.BlockSpec(memory_space=pltpu.SEMAPHORE)
_DATAFLOW = pltpu.SideEffectType.DATAFLOW_SIDE_EFFECTING


_ALL_PEERS = tuple(range(N_DEV - 1))
_SAME_CORE_PEERS = (0, 1, 3, 5)
_OTHER_CHIPS = (1, 3, 5)


def _xfer_start(arrs, name, gather, via_sibling=(), after=()):
    n = len(arrs)
    n_peer = N_DEV - 1
    n_after = len(after)
    peer_ks = [_SAME_CORE_PEERS if a in via_sibling else _ALL_PEERS for a in range(n)]

    def body(*refs):
        ins, lands = refs[:n], refs[n:2 * n]
        sems = refs[2 * n + n_after:5 * n + n_after]
        token = refs[-1]
        me = _my_index()
        peers = _peers()
        for a in range(n):
            send_sems, recv_sems, loc_sem = sems[3 * a:3 * a + 3]
            src_own = ins[a] if gather else ins[a].at[me]
            pltpu.make_async_copy(src_own, lands[a].at[me], loc_sem).start()
            for k in peer_ks[a]:
                peer, pidx = peers[k]
                src = ins[a] if gather else ins[a].at[pidx]
                pltpu.make_async_remote_copy(
                    src_ref=src, dst_ref=lands[a].at[me], send_sem=send_sems.at[k], recv_sem=recv_sems.at[k],
                    device_id=peer, device_id_type=MESH).start()
        token[...] = jnp.zeros_like(token)

    land_shapes = [((N_DEV,) + a.shape) if gather else a.shape for a in arrs]
    out_shape, out_specs = [], []
    for _ in range(n):
        out_shape += [pltpu.SemaphoreType.DMA((n_peer,)), pltpu.SemaphoreType.DMA((n_peer,)),
                      pltpu.SemaphoreType.DMA(())]
        out_specs += [_SEM, _SEM, _SEM]
    out_shape += [pltpu.HBM(a.shape, a.dtype) for a in arrs]
    out_shape += [pltpu.HBM(s, a.dtype) for s, a in zip(land_shapes, arrs)]
    out_shape += [jax.ShapeDtypeStruct((8, LANES), F32)]
    out_specs += [_HBM] * (2 * n) + [pl.BlockSpec(memory_space=pltpu.VMEM)]
    aliases = {}
    for a in range(n):
        aliases[a] = 3 * n + a
        aliases[n + a] = 4 * n + a
    operands = [pltpu.with_memory_space_constraint(a, pltpu.HBM) for a in arrs]
    operands += [pltpu.with_memory_space_constraint(lax.empty(s, a.dtype), pltpu.HBM)
                 for s, a in zip(land_shapes, arrs)]
    outs = _pcall(
        body, name=name, out_shape=tuple(out_shape),
        in_specs=[_HBM] * (2 * n) + [pl.BlockSpec(memory_space=pl.ANY)] * n_after, out_specs=tuple(out_specs),
        input_output_aliases=aliases,
        compiler_params=pltpu.CompilerParams(has_side_effects=_DATAFLOW),
    )(*operands, *after)
    handles = []
    for a in range(n):
        handles.append((outs[3 * n + a], outs[4 * n + a], outs[3 * a], outs[3 * a + 1], outs[3 * a + 2],
                        peer_ks[a]))
    return handles, outs[-1]


def _xfer_wait(handles, after, name, gather):
    n = len(handles)
    after = tuple(after) if isinstance(after, (tuple, list)) else (after,)
    peer_ks = [h[5] for h in handles]

    def body(*refs):
        me = _my_index()
        peers = _peers()
        for a in range(n):
            src_ref, land_ref, send_ref, recv_ref, loc_ref = refs[5 * a:5 * a + 5]
            src_own = src_ref if gather else src_ref.at[me]
            pltpu.make_async_copy(src_own, land_ref.at[me], loc_ref).wait()
            for k in peer_ks[a]:
                peer, pidx = peers[k]
                src = src_ref if gather else src_ref.at[pidx]
                cp = pltpu.make_async_remote_copy(
                    src_ref=src, dst_ref=land_ref.at[pidx], send_sem=send_ref.at[k], recv_sem=recv_ref.at[k],
                    device_id=peer, device_id_type=MESH)
                cp.wait_send()
                cp.wait_recv()

    operands, in_specs, out_shape, aliases = [], [], [], {}
    for a, h in enumerate(handles):
        operands += list(h[:5])
        in_specs += [_HBM, _HBM, _SEM, _SEM, _SEM]
        out_shape += [pltpu.HBM(h[0].shape, h[0].dtype), pltpu.HBM(h[1].shape, h[1].dtype)]
        aliases[5 * a] = 2 * a
        aliases[5 * a + 1] = 2 * a + 1
    outs = _pcall(
        body, name=name, out_shape=tuple(out_shape),
        in_specs=in_specs + [pl.BlockSpec(memory_space=pl.ANY)] * len(after),
        out_specs=tuple([_HBM] * (2 * n)), input_output_aliases=aliases,
        compiler_params=pltpu.CompilerParams(has_side_effects=_DATAFLOW),
    )(*operands, *after)
    return [outs[2 * a + 1] for a in range(n)]


def _sibling_forward(lands, name):
    n = len(lands)
    n_fwd = len(_OTHER_CHIPS)

    def body(*refs):
        ins, bufs = refs[:n], refs[n:2 * n]
        send_sems, recv_sems = refs[2 * n:]
        x, y, c = lax.axis_index("x"), lax.axis_index("y"), lax.axis_index("c")
        sibling = (x, y, 1 - c)
        peers = _peers()
        sends = []
        for a in range(n):
            for j, k in enumerate(_OTHER_CHIPS):
                slot = peers[k][1]
                cp = pltpu.make_async_remote_copy(
                    src_ref=ins[a].at[slot], dst_ref=bufs[a].at[slot],
                    send_sem=send_sems.at[a * n_fwd + j], recv_sem=recv_sems.at[a * n_fwd + j],
                    device_id=sibling, device_id_type=MESH)
                cp.start()
                sends.append(cp)
        for a in range(n):
            for j, k in enumerate(_OTHER_CHIPS):
                (px, py, pc), slot = peers[k]
                theirs = 4 * px + 2 * py + (1 - pc)
                pltpu.make_async_remote_copy(
                    src_ref=ins[a].at[slot], dst_ref=bufs[a].at[theirs],
                    send_sem=send_sems.at[a * n_fwd + j], recv_sem=recv_sems.at[a * n_fwd + j],
                    device_id=sibling, device_id_type=MESH).wait_recv()
        for cp in sends:
            cp.wait_send()

    any_spec = pl.BlockSpec(memory_space=pl.ANY)
    outs = _pcall(
        body, name=name, out_shape=[jax.ShapeDtypeStruct(a.shape, a.dtype) for a in lands],
        in_specs=[any_spec] * n, out_specs=[any_spec] * n,
        input_output_aliases={a: a for a in range(n)},
        scratch_shapes=[pltpu.SemaphoreType.DMA((n * n_fwd,)), pltpu.SemaphoreType.DMA((n * n_fwd,))],
        compiler_params=pltpu.CompilerParams(has_side_effects=True),
    )(*lands)
    return list(outs)


_DIMS = {"nn": (((1,), (0,)), ((), ())), "nt": (((1,), (1,)), ((), ())), "tn": (((0,), (0,)), ((), ()))}


def _dot(a, b, mode="nn"):
    return lax.dot_general(a, b, _DIMS[mode], preferred_element_type=F32)


def _mm(a, b, *, mode, grid, a_spec, b_spec, out_shape, out_specs, acc_shape, epilogue, name,
        extra=(), extra_specs=(), after=(), semantics=("parallel", "parallel", "arbitrary")):
    nk = grid[2]
    n_extra = len(extra)
    n_in = 2 + n_extra + len(after)

    def body_single(*refs):
        a_ref, b_ref = refs[0], refs[1]
        epilogue(_dot(a_ref[...], b_ref[...], mode), refs[2:2 + n_extra], refs[n_in:])

    def body_acc(*refs):
        a_ref, b_ref = refs[0], refs[1]
        ex = refs[2:2 + n_extra]
        outs = refs[n_in:-1]
        acc = refs[-1]
        k = pl.program_id(2)

        @pl.when(k == 0)
        def _():
            acc[...] = jnp.zeros_like(acc)

        acc[...] += _dot(a_ref[...], b_ref[...], mode)

        @pl.when(k == nk - 1)
        def _():
            epilogue(acc[...], ex, outs)

    return _pcall(
        body_single if nk == 1 else body_acc, name=name, grid=grid, out_shape=out_shape,
        in_specs=[a_spec, b_spec] + list(extra_specs) + [pl.BlockSpec(memory_space=pl.ANY)] * len(after),
        out_specs=out_specs,
        scratch_shapes=[] if nk == 1 else [pltpu.VMEM(acc_shape, F32)],
        compiler_params=_cparams(semantics),
    )(a, b, *extra, *after)


def _ep_store(dtype):
    def ep(acc, ex, outs):
        outs[0][...] = acc.astype(dtype)
    return ep


def _ep_relu2(acc, ex, outs):
    outs[0][...] = acc.astype(BF16)
    r = jnp.maximum(acc, 0.0)
    outs[1][...] = (r * r).astype(BF16)


def _ep_relu2_bwd(acc, ex, outs):
    u = ex[0][...].astype(F32)
    outs[0][...] = (acc * (2.0 * jnp.maximum(u, 0.0))).astype(BF16)


def _tile(n, want):
    t = min(n, want)
    while n % t:
        t //= 2
    return t


def _mm_nn(a, w, out_dtype, name, tm=2048, tn=1024, tk=1024, epilogue=None, out_dtypes=None, after=()):
    M, K = a.shape
    N = w.shape[1]
    tm, tn, tk = _tile(M, tm), _tile(N, tn), _tile(K, tk)
    out_dtypes = out_dtypes or [out_dtype]
    return _mm(a, w, mode="nn", grid=(M // tm, N // tn, K // tk),
               a_spec=pl.BlockSpec((tm, tk), lambda i, j, k: (i, k)),
               b_spec=pl.BlockSpec((tk, tn), lambda i, j, k: (k, j)),
               out_shape=[jax.ShapeDtypeStruct((M, N), d) for d in out_dtypes],
               out_specs=[pl.BlockSpec((tm, tn), lambda i, j, k: (i, j)) for _ in out_dtypes],
               acc_shape=(tm, tn), epilogue=epilogue or _ep_store(out_dtype), name=name, after=after)


def _mm_nn_blocked(a, wg, name, epilogue, out_dtypes, tm=2048):
    M, K = a.shape
    n = wg.shape[2]
    tm = _tile(M, tm)
    return _mm(a, wg, mode="nn", grid=(M // tm, N_DEV, 1),
               a_spec=pl.BlockSpec((tm, K), lambda i, j, k: (i, 0)),
               b_spec=pl.BlockSpec((None, K, n), lambda i, j, k: (j, 0, 0)),
               out_shape=[jax.ShapeDtypeStruct((M, N_DEV * n), d) for d in out_dtypes],
               out_specs=[pl.BlockSpec((tm, n), lambda i, j, k: (i, j)) for _ in out_dtypes],
               acc_shape=(tm, n), epilogue=epilogue, name=name)


def _mm_nt(a, w, out_dtype, name, tm=2048, tn=1024, tk=1024, epilogue=None, extra=(), extra_specs=(),
           after=()):
    M, K = a.shape
    N = w.shape[0]
    tm, tn, tk = _tile(M, tm), _tile(N, tn), _tile(K, tk)
    if extra and not extra_specs:
        extra_specs = [pl.BlockSpec((tm, tn), lambda i, j, k: (i, j)) for _ in extra]
    return _mm(a, w, mode="nt", grid=(M // tm, N // tn, K // tk),
               a_spec=pl.BlockSpec((tm, tk), lambda i, j, k: (i, k)),
               b_spec=pl.BlockSpec((tn, tk), lambda i, j, k: (j, k)),
               out_shape=[jax.ShapeDtypeStruct((M, N), out_dtype)],
               out_specs=[pl.BlockSpec((tm, tn), lambda i, j, k: (i, j))],
               acc_shape=(tm, tn), epilogue=epilogue or _ep_store(out_dtype), name=name,
               extra=extra, extra_specs=extra_specs, after=after)[0]


def _mm_nt_blocked(a, wg, out_dtype, name, tm=1024, after=()):
    M = a.shape[0]
    kout, n = wg.shape[1], wg.shape[2]
    tm = _tile(M, tm)
    return _mm(a, wg, mode="nt", grid=(M // tm, 1, N_DEV),
               a_spec=pl.BlockSpec((tm, n), lambda i, j, k: (i, k)),
               b_spec=pl.BlockSpec((None, kout, n), lambda i, j, k: (k, 0, 0)),
               out_shape=[jax.ShapeDtypeStruct((M, kout), out_dtype)],
               out_specs=[pl.BlockSpec((tm, kout), lambda i, j, k: (i, 0))],
               acc_shape=(tm, kout), epilogue=_ep_store(out_dtype), name=name, after=after)[0]


def _mm_tn(a, b, out_dtype, name, tm=1024, tn=1024, tk=2048):
    K, M = a.shape
    N = b.shape[1]
    tm, tn, tk = _tile(M, tm), _tile(N, tn), _tile(K, tk)
    return _mm(a, b, mode="tn", grid=(M // tm, N // tn, K // tk),
               a_spec=pl.BlockSpec((tk, tm), lambda i, j, k: (k, i)),
               b_spec=pl.BlockSpec((tk, tn), lambda i, j, k: (k, j)),
               out_shape=[jax.ShapeDtypeStruct((M, N), out_dtype)],
               out_specs=[pl.BlockSpec((tm, tn), lambda i, j, k: (i, j))],
               acc_shape=(tm, tn), epilogue=_ep_store(out_dtype), name=name)[0]


def _mm_tn_blocked(a, b, out_dtype, name, tm=1024, tk=2048):
    K, M = a.shape
    n = b.shape[1] // N_DEV
    tm, tk = _tile(M, tm), _tile(K, tk)
    return _mm(a, b, mode="tn", grid=(M // tm, N_DEV, K // tk),
               a_spec=pl.BlockSpec((tk, tm), lambda i, j, k: (k, i)),
               b_spec=pl.BlockSpec((tk, n), lambda i, j, k: (k, j)),
               out_shape=[jax.ShapeDtypeStruct((N_DEV, M, n), out_dtype)],
               out_specs=[pl.BlockSpec((None, tm, n), lambda i, j, k: (j, i, 0))],
               acc_shape=(tm, n), epilogue=_ep_store(out_dtype), name=name)[0]


def _window_geometry(ws):
    base = [(ws * k // LANES) * LANES for k in range(N_DEV)]
    off = [ws * k - base[k] for k in range(N_DEV)]
    win = -(-(max(off) + ws) // LANES) * LANES
    return base, off, win


def _shards_to_columns(xg, base, off, win, n_out, name, tr=256):
    R, ws = xg.shape[1], xg.shape[2]
    tr = _tile(R, tr)
    nb_win = win // LANES

    def body(x_ref, o_ref, frame_ref):
        written = set()
        frame_ref[...] = jnp.zeros_like(frame_ref)
        for k in range(N_DEV):
            frame_ref[:, 0:ws] = x_ref[k].astype(F32)
            window = frame_ref[...]
            if off[k]:
                window = pltpu.roll(window, off[k], 1)
            for i in range(nb_win):
                b = base[k] // LANES + i
                if b * LANES >= n_out:
                    continue
                cols = slice(b * LANES, (b + 1) * LANES)
                blk = window[:, i * LANES:(i + 1) * LANES]
                if b in written:
                    blk = blk + o_ref[:, cols].astype(F32)
                o_ref[:, cols] = blk.astype(o_ref.dtype)
                written.add(b)
        for b in range(n_out // LANES):
            if b not in written:
                o_ref[:, b * LANES:(b + 1) * LANES] = jnp.zeros((tr, LANES), o_ref.dtype)

    return _pcall(
        body, name=name, grid=(R // tr,), out_shape=jax.ShapeDtypeStruct((R, n_out), xg.dtype),
        in_specs=[pl.BlockSpec((N_DEV, tr, ws), lambda i: (0, i, 0))],
        out_specs=pl.BlockSpec((tr, n_out), lambda i: (i, 0)),
        scratch_shapes=[pltpu.VMEM((tr, win), F32)],
        compiler_params=_cparams(("parallel",)))(xg)


def _sigmoid(x):
    return 1.0 / (1.0 + jnp.exp(-x))


def _row_spec(tm, d):
    return pl.BlockSpec((tm, d), lambda i: (i, 0))


def _vec_spec(d):
    return pl.BlockSpec((1, d), lambda i: (0, 0))


def _norm_mod_fwd(x, y, gate, nw, scale, shift, name, tm=512):
    L, D = x.shape
    tm = _tile(L, tm)
    has_res = y is not None

    def body(*refs):
        if has_res:
            x_ref, y_ref, g_ref, nw_ref, sc_ref, sh_ref, xo_ref, h_ref = refs
            xn = x_ref[...] + g_ref[...] * y_ref[...]
            xo_ref[...] = xn
        else:
            x_ref, nw_ref, sc_ref, sh_ref, h_ref = refs
            xn = x_ref[...]
        rstd = lax.rsqrt(jnp.mean(xn * xn, axis=-1, keepdims=True) + NORM_EPS)
        h = xn * rstd * nw_ref[...] * (1.0 + sc_ref[...]) + sh_ref[...]
        h_ref[...] = h.astype(BF16)

    row, vec = _row_spec(tm, D), _vec_spec(D)
    if has_res:
        ins, in_specs = (x, y, gate, nw, scale, shift), [row, row, vec, vec, vec, vec]
        out_shape = [jax.ShapeDtypeStruct((L, D), F32), jax.ShapeDtypeStruct((L, D), BF16)]
        out_specs = [row, row]
    else:
        ins, in_specs = (x, nw, scale, shift), [row, vec, vec, vec]
        out_shape = [jax.ShapeDtypeStruct((L, D), BF16)]
        out_specs = [row]
    outs = _pcall(body, name=name, grid=(L // tm,), out_shape=out_shape, in_specs=in_specs,
                  out_specs=out_specs, compiler_params=_cparams(("parallel",)))(*ins)
    return outs if has_res else (x, outs[0])


def _gated_branch_bwd(dx, branch, y_ref, g_ref, dy_ref, dg_ref):
    if branch is None:
        return
    dy_ref[...] = (g_ref[...] * dx).astype(BF16)
    dg_ref[...] += jnp.sum(dx * y_ref[...], axis=0, keepdims=True)


def _norm_mod_bwd(dh, x, nw, scale, dres, name, branch=None, tm=512):
    L, D = x.shape
    tm = _tile(L, tm)
    nb = 0 if branch is None else 2

    def body(dh_ref, x_ref, nw_ref, sc_ref, dres_ref, *rest):
        y_ref, g_ref = rest[:nb] if nb else (None, None)
        dx_ref, dsh_ref, dsc_ref, dnw_ref = rest[nb:nb + 4]
        dy_ref, dg_ref = rest[nb + 4:] if nb else (None, None)

        @pl.when(pl.program_id(0) == 0)
        def _():
            dsh_ref[...] = jnp.zeros_like(dsh_ref)
            dsc_ref[...] = jnp.zeros_like(dsc_ref)
            dnw_ref[...] = jnp.zeros_like(dnw_ref)
            if nb:
                dg_ref[...] = jnp.zeros_like(dg_ref)

        xv = x_ref[...]
        dh_v = dh_ref[...]
        nw_v = nw_ref[...]
        rstd = lax.rsqrt(jnp.mean(xv * xv, axis=-1, keepdims=True) + NORM_EPS)
        xhat = xv * rstd
        dsh_ref[...] += jnp.sum(dh_v, axis=0, keepdims=True)
        dsc_ref[...] += jnp.sum(dh_v * (xhat * nw_v), axis=0, keepdims=True)
        dr = dh_v * (1.0 + sc_ref[...])
        dnw_ref[...] += jnp.sum(dr * xhat, axis=0, keepdims=True)
        dxh = dr * nw_v
        dx = rstd * (dxh - xhat * jnp.mean(dxh * xhat, axis=-1, keepdims=True)) + dres_ref[...]
        dx_ref[...] = dx
        _gated_branch_bwd(dx, branch, y_ref, g_ref, dy_ref, dg_ref)

    row, vec = _row_spec(tm, D), _vec_spec(D)
    extra_in = [] if branch is None else list(branch)
    return _pcall(
        body, name=name, grid=(L // tm,),
        out_shape=[jax.ShapeDtypeStruct((L, D), F32)] + [jax.ShapeDtypeStruct((1, D), F32)] * 3
        + ([jax.ShapeDtypeStruct((L, D), BF16), jax.ShapeDtypeStruct((1, D), F32)] if nb else []),
        in_specs=[row, row, vec, vec, row] + ([row, vec] if nb else []),
        out_specs=[row, vec, vec, vec] + ([row, vec] if nb else []),
        compiler_params=_cparams(("arbitrary",)))(dh, x, nw, scale, dres, *extra_in)


def _final_loss(x, y, gate, fw, target, name, tm=512):
    L, D = x.shape
    tm = _tile(L, tm)

    def body(x_ref, y_ref, g_ref, fw_ref, t_ref, dx_ref, loss_ref, dfw_ref, dy_ref, dg_ref):
        @pl.when(pl.program_id(0) == 0)
        def _():
            loss_ref[...] = jnp.zeros_like(loss_ref)
            dfw_ref[...] = jnp.zeros_like(dfw_ref)
            dg_ref[...] = jnp.zeros_like(dg_ref)

        xn = x_ref[...] + g_ref[...] * y_ref[...]
        fw_v = fw_ref[...]
        rstd = lax.rsqrt(jnp.mean(xn * xn, axis=-1, keepdims=True) + NORM_EPS)
        xhat = xn * rstd
        diff = xhat * fw_v - t_ref[...]
        loss_ref[...] += jnp.sum(diff * diff, axis=0, keepdims=True)
        dyf = diff * (1.0 / D)
        dfw_ref[...] += jnp.sum(dyf * xhat, axis=0, keepdims=True)
        dxh = dyf * fw_v
        dx = rstd * (dxh - xhat * jnp.mean(dxh * xhat, axis=-1, keepdims=True))
        dx_ref[...] = dx
        _gated_branch_bwd(dx, True, y_ref, g_ref, dy_ref, dg_ref)

    row, vec = _row_spec(tm, D), _vec_spec(D)
    return _pcall(
        body, name=name, grid=(L // tm,),
        out_shape=[jax.ShapeDtypeStruct((L, D), F32), jax.ShapeDtypeStruct((1, D), F32),
                   jax.ShapeDtypeStruct((1, D), F32), jax.ShapeDtypeStruct((L, D), BF16),
                   jax.ShapeDtypeStruct((1, D), F32)],
        in_specs=[row, row, vec, vec, row], out_specs=[row, vec, vec, row, vec],
        compiler_params=_cparams(("arbitrary",)))(x, y, gate, fw, target)


def _mm_nt_norm_bwd(a, w, x, nw, scale, dres, name, branch=None, blocked=False, tm=512, tk=1024, after=()):
    M = a.shape[0]
    D = x.shape[1]
    tm = _tile(M, tm)
    nb = 0 if branch is None else 2

    def epilogue(dh_v, ex, outs):
        x_ref, nw_ref, sc_ref, dres_ref = ex[:4]
        y_ref, g_ref = ex[4:] if nb else (None, None)
        dx_ref, dsh_ref, dsc_ref, dnw_ref = outs[:4]
        dy_ref, dg_ref = outs[4:] if nb else (None, None)

        @pl.when(pl.program_id(0) == 0)
        def _():
            dsh_ref[...] = jnp.zeros_like(dsh_ref)
            dsc_ref[...] = jnp.zeros_like(dsc_ref)
            dnw_ref[...] = jnp.zeros_like(dnw_ref)
            if nb:
                dg_ref[...] = jnp.zeros_like(dg_ref)

        xv = x_ref[...]
        nw_v = nw_ref[...]
        rstd = lax.rsqrt(jnp.mean(xv * xv, axis=-1, keepdims=True) + NORM_EPS)
        xhat = xv * rstd
        dsh_ref[...] += jnp.sum(dh_v, axis=0, keepdims=True)
        dsc_ref[...] += jnp.sum(dh_v * (xhat * nw_v), axis=0, keepdims=True)
        dr = dh_v * (1.0 + sc_ref[...])
        dnw_ref[...] += jnp.sum(dr * xhat, axis=0, keepdims=True)
        dxh = dr * nw_v
        dx = rstd * (dxh - xhat * jnp.mean(dxh * xhat, axis=-1, keepdims=True)) + dres_ref[...]
        dx_ref[...] = dx
        _gated_branch_bwd(dx, branch, y_ref, g_ref, dy_ref, dg_ref)

    row = pl.BlockSpec((tm, D), lambda i, j, k: (i, 0))
    vec = pl.BlockSpec((1, D), lambda i, j, k: (0, 0))
    if blocked:
        n = w.shape[2]
        grid = (M // tm, 1, N_DEV)
        a_spec = pl.BlockSpec((tm, n), lambda i, j, k: (i, k))
        b_spec = pl.BlockSpec((None, D, n), lambda i, j, k: (k, 0, 0))
    else:
        K = a.shape[1]
        tk = _tile(K, tk)
        grid = (M // tm, 1, K // tk)
        a_spec = pl.BlockSpec((tm, tk), lambda i, j, k: (i, k))
        b_spec = pl.BlockSpec((D, tk), lambda i, j, k: (0, k))
    return _mm(a, w, mode="nt", grid=grid, a_spec=a_spec, b_spec=b_spec,
               out_shape=[jax.ShapeDtypeStruct((M, D), F32)] + [jax.ShapeDtypeStruct((1, D), F32)] * 3
               + ([jax.ShapeDtypeStruct((M, D), BF16), jax.ShapeDtypeStruct((1, D), F32)] if nb else []),
               out_specs=[row, vec, vec, vec] + ([row, vec] if nb else []),
               acc_shape=(tm, D), epilogue=epilogue, name=name,
               extra=(x, nw, scale, dres) + (tuple(branch) if nb else ()),
               extra_specs=[row, vec, vec, row] + ([row, vec] if nb else []), after=after,
               semantics=("arbitrary", "arbitrary", "arbitrary"))


def _shift_down(v, s, row):
    if s == 0:
        return v
    return jnp.where(row >= s, pltpu.roll(v, s, 0), 0.0)


def _shift_up(v, s, row):
    if s == 0:
        return v
    n = v.shape[0]
    return jnp.where(row < n - s, pltpu.roll(v, n - s, 0), 0.0)


CONV_ROWS = 32


def _shifted_rows(x_ref, r0, n):
    cur = x_ref[r0:r0 + CONV_ROWS, :]
    if r0 >= n - 1:
        return [cur] + [x_ref[r0 - s:r0 - s + CONV_ROWS, :] for s in range(1, n)]
    row = lax.broadcasted_iota(jnp.int32, cur.shape, 0)
    return [_shift_down(cur, s, row) for s in range(n)]


def _ssd_conv_fwd(zx, w, b, col0, width, name, cb=128):
    L = zx.shape[0]
    nb = width // cb
    off = col0 // cb

    def body(x_ref, w_ref, b_ref, o_ref):
        taps = [w_ref[k:k + 1, :] for k in range(SSD_K)]
        bias = b_ref[...]
        for r0 in range(0, L, CONV_ROWS):
            taps_in = _shifted_rows(x_ref, r0, SSD_K)
            acc = bias + taps[SSD_K - 1] * taps_in[0]
            for s in range(1, SSD_K):
                acc = acc + taps[SSD_K - 1 - s] * taps_in[s]
            o_ref[r0:r0 + CONV_ROWS, :] = acc * _sigmoid(acc)

    return _pcall(
        body, name=name, grid=(nb,), out_shape=jax.ShapeDtypeStruct((L, width), F32),
        in_specs=[pl.BlockSpec((L, cb), lambda j: (0, off + j)),
                  pl.BlockSpec((SSD_K, cb), lambda j: (0, j)),
                  pl.BlockSpec((1, cb), lambda j: (0, j))],
        out_specs=pl.BlockSpec((L, cb), lambda j: (0, j)),
        compiler_params=_cparams(("parallel",)))(zx, w, b)


def _ssd_conv_bwd(zx, w, b, d_parts, dzx, col0, name, cb=128):
    L = zx.shape[0]
    widths = [p.shape[1] for p in d_parts]
    width = sum(widths)
    nb = width // cb
    off = col0 // cb
    starts = [sum(widths[:i]) // cb for i in range(len(d_parts))]
    counts = [wd // cb for wd in widths]

    def body(x_ref, w_ref, b_ref, *rest):
        d_refs = rest[:len(d_parts)]
        dx_ref, dw_ref, db_ref, dpre_ref = rest[len(d_parts) + 1:]
        j = pl.program_id(0)
        taps = [w_ref[k:k + 1, :] for k in range(SSD_K)]
        bias = b_ref[...]
        fold = lambda v: sum(v[r:r + 8, :] for r in range(0, CONV_ROWS, 8))
        db8 = jnp.zeros((8, cb), F32)
        dw8 = [jnp.zeros((8, cb), F32) for _ in range(SSD_K)]
        for r0 in range(0, L, CONV_ROWS):
            rows = slice(r0, r0 + CONV_ROWS)
            d_val = d_refs[-1][rows, :]
            for i in range(len(d_parts) - 2, -1, -1):
                d_val = jnp.where(j < starts[i + 1], d_refs[i][rows, :], d_val)
            taps_in = _shifted_rows(x_ref, r0, SSD_K)
            acc = bias + taps[SSD_K - 1] * taps_in[0]
            for s in range(1, SSD_K):
                acc = acc + taps[SSD_K - 1 - s] * taps_in[s]
            sig = _sigmoid(acc)
            dpre = d_val * (sig * (1.0 + acc * (1.0 - sig)))
            dpre_ref[rows, :] = dpre
            db8 = db8 + fold(dpre)
            for s in range(SSD_K):
                dw8[s] = dw8[s] + fold(dpre * taps_in[s])
        dpre_ref[L:L + 8, :] = jnp.zeros((8, cb), F32)
        db_ref[...] = jnp.sum(db8, axis=0, keepdims=True)
        for s in range(SSD_K):
            dw_ref[SSD_K - 1 - s:SSD_K - s, :] = jnp.sum(dw8[s], axis=0, keepdims=True)
        for r0 in range(0, L, CONV_ROWS):
            dx = taps[SSD_K - 1] * dpre_ref[r0:r0 + CONV_ROWS, :]
            for s in range(1, SSD_K):
                dx = dx + taps[SSD_K - 1 - s] * dpre_ref[r0 + s:r0 + s + CONV_ROWS, :]
            dx_ref[r0:r0 + CONV_ROWS, :] = dx.astype(BF16)

    def part_spec(i):
        return pl.BlockSpec((L, cb), lambda j: (0, jnp.clip(j - starts[i], 0, counts[i] - 1)))

    return _pcall(
        body, name=name, grid=(nb,),
        out_shape=[jax.ShapeDtypeStruct(dzx.shape, BF16), jax.ShapeDtypeStruct((SSD_K, width), F32),
                   jax.ShapeDtypeStruct((1, width), F32)],
        in_specs=[pl.BlockSpec((L, cb), lambda j: (0, off + j)),
                  pl.BlockSpec((SSD_K, cb), lambda j: (0, j)),
                  pl.BlockSpec((1, cb), lambda j: (0, j))]
        + [part_spec(i) for i in range(len(d_parts))] + [pl.BlockSpec(memory_space=pl.ANY)],
        out_specs=[pl.BlockSpec((L, cb), lambda j: (0, off + j)),
                   pl.BlockSpec((SSD_K, cb), lambda j: (0, j)),
                   pl.BlockSpec((1, cb), lambda j: (0, j))],
        input_output_aliases={3 + len(d_parts): 0},
        scratch_shapes=[pltpu.VMEM((L + 8, cb), F32)],
        compiler_params=_cparams(("parallel",)))(zx, w, b, *d_parts, dzx)


def _dzx_finish(dzx, ddt, col0, name, tl=512):
    G, L, _ = ddt.shape
    tail = dzx.shape[1] - col0
    tl = _tile(L, tl)

    def body(ddt_ref, dzx_ref, o_ref):
        s = ddt_ref[0]
        for g in range(1, G):
            s = s + ddt_ref[g]
        o_ref[:, 0:LANES] = s.astype(o_ref.dtype)
        if tail > LANES:
            o_ref[:, LANES:] = jnp.zeros((tl, tail - LANES), o_ref.dtype)

    return _pcall(
        body, name=name, grid=(L // tl,), out_shape=jax.ShapeDtypeStruct(dzx.shape, dzx.dtype),
        in_specs=[pl.BlockSpec((G, tl, LANES), lambda i: (0, i, 0)), pl.BlockSpec(memory_space=pl.ANY)],
        out_specs=pl.BlockSpec((tl, tail), lambda i: (i, col0 // tail)),
        input_output_aliases={1: 0},
        compiler_params=_cparams(("parallel",)))(ddt, dzx)


def _sc_conv_fwd(proj, w, name, cb=128):
    L = proj.shape[0]
    width = proj.shape[1] // 3
    nb = width // cb

    def body(b_ref, c_ref, x_ref, w_ref, o_ref):
        taps = [w_ref[k:k + 1, :] for k in range(SC_K)]
        for r0 in range(0, L, CONV_ROWS):
            rows = slice(r0, r0 + CONV_ROWS)
            q = [c * x for c, x in zip(_shifted_rows(c_ref, r0, SC_K), _shifted_rows(x_ref, r0, SC_K))]
            acc = taps[SC_K - 1] * q[0]
            for s in range(1, SC_K):
                acc = acc + taps[SC_K - 1 - s] * q[s]
            o_ref[rows, :] = (b_ref[rows, :] * acc).astype(BF16)

    return _pcall(
        body, name=name, grid=(nb,), out_shape=jax.ShapeDtypeStruct((L, width), BF16),
        in_specs=[pl.BlockSpec((L, cb), lambda j: (0, j)),
                  pl.BlockSpec((L, cb), lambda j: (0, nb + j)),
                  pl.BlockSpec((L, cb), lambda j: (0, 2 * nb + j)),
                  pl.BlockSpec((SC_K, cb), lambda j: (0, j))],
        out_specs=pl.BlockSpec((L, cb), lambda j: (0, j)),
        compiler_params=_cparams(("parallel",)))(proj, proj, proj, w)


def _sc_conv_bwd(proj, w, dy, name, cb=128):
    L = proj.shape[0]
    width = proj.shape[1] // 3
    nb = width // cb

    def body(b_ref, c_ref, x_ref, w_ref, dy_ref, db_ref, dc_ref, dxv_ref, dw_ref, dconv_ref):
        taps = [w_ref[k:k + 1, :] for k in range(SC_K)]
        fold = lambda v: sum(v[r:r + 8, :] for r in range(0, CONV_ROWS, 8))
        dw8 = [jnp.zeros((8, cb), F32) for _ in range(SC_K)]
        for r0 in range(0, L, CONV_ROWS):
            rows = slice(r0, r0 + CONV_ROWS)
            q = [c * x for c, x in zip(_shifted_rows(c_ref, r0, SC_K), _shifted_rows(x_ref, r0, SC_K))]
            conv = taps[SC_K - 1] * q[0]
            for s in range(1, SC_K):
                conv = conv + taps[SC_K - 1 - s] * q[s]
            dyv = dy_ref[rows, :]
            db_ref[rows, :] = (dyv * conv).astype(BF16)
            dconv = dyv * b_ref[rows, :]
            dconv_ref[rows, :] = dconv
            for s in range(SC_K):
                dw8[s] = dw8[s] + fold(dconv * q[s])
        dconv_ref[L:L + 8, :] = jnp.zeros((8, cb), F32)
        for s in range(SC_K):
            dw_ref[SC_K - 1 - s:SC_K - s, :] = jnp.sum(dw8[s], axis=0, keepdims=True)
        for r0 in range(0, L, CONV_ROWS):
            rows = slice(r0, r0 + CONV_ROWS)
            dq = taps[SC_K - 1] * dconv_ref[rows, :]
            for s in range(1, SC_K):
                dq = dq + taps[SC_K - 1 - s] * dconv_ref[r0 + s:r0 + s + CONV_ROWS, :]
            dc_ref[rows, :] = (dq * x_ref[rows, :]).astype(BF16)
            dxv_ref[rows, :] = (dq * c_ref[rows, :]).astype(BF16)

    blk = pl.BlockSpec((L, cb), lambda j: (0, j))
    wblk = pl.BlockSpec((SC_K, cb), lambda j: (0, j))
    return _pcall(
        body, name=name, grid=(nb,),
        out_shape=[jax.ShapeDtypeStruct((L, width), BF16)] * 3 + [jax.ShapeDtypeStruct((SC_K, width), F32)],
        in_specs=[blk, pl.BlockSpec((L, cb), lambda j: (0, nb + j)),
                  pl.BlockSpec((L, cb), lambda j: (0, 2 * nb + j)), wblk, blk],
        out_specs=[blk, blk, blk, wblk], scratch_shapes=[pltpu.VMEM((L + 8, cb), F32)],
        compiler_params=_cparams(("parallel",)))(proj, proj, proj, w, dy)


def _split3(v):
    hi = v.astype(BF16)
    r1 = v - hi.astype(F32)
    mid = r1.astype(BF16)
    lo = (r1 - mid.astype(F32)).astype(BF16)
    return hi, mid, lo


def _dot_exact01(t01, v):
    hi, mid, lo = _split3(v)
    return _dot(t01, hi) + _dot(t01, mid) + _dot(t01, lo)


def _lane_col(v, lane, h):
    return jnp.sum(jnp.where(lane == h, v, 0.0), axis=1, keepdims=True)


def _sum_all(v):
    return jnp.sum(jnp.sum(v, axis=1, keepdims=True), axis=0, keepdims=True)


def _softplus(x):
    return jnp.maximum(x, 0.0) + jnp.log1p(jnp.exp(-jnp.abs(x)))


def _ssd_decay(zx, bias_p, alog_p, n_heads, dt_block, name):
    L = zx.shape[0]
    nc = L // SSD_CHUNK

    def body(raw_ref, bias_ref, alog_ref, dt_ref, sg_ref, cs_ref, cst_ref, last_ref):
        lane = lax.broadcasted_iota(jnp.int32, (SSD_CHUNK, LANES), 1)
        row = lax.broadcasted_iota(jnp.int32, (SSD_CHUNK, LANES), 0)
        valid = lane < n_heads
        raw = raw_ref[...] + bias_ref[...]
        dt = jnp.where(valid, _softplus(raw), 0.0)
        a = dt * (-jnp.exp(alog_ref[...]))
        cs = _dot_exact01((row >= lane).astype(BF16), a)
        dt_ref[...] = dt
        sg_ref[...] = _sigmoid(raw)
        cs_ref[...] = cs
        cst_ref[...] = cs.T
        last_ref[...] = jnp.sum(a, axis=0, keepdims=True)

    blk = pl.BlockSpec((SSD_CHUNK, LANES), lambda c: (c, 0))
    head_vec = pl.BlockSpec((1, LANES), lambda c: (0, 0))
    return _pcall(
        body, name=name, grid=(nc,),
        out_shape=[jax.ShapeDtypeStruct((L, LANES), F32)] * 3
        + [jax.ShapeDtypeStruct((nc, SSD_CHUNK, LANES), F32), jax.ShapeDtypeStruct((nc, 1, LANES), F32)],
        in_specs=[pl.BlockSpec((SSD_CHUNK, LANES), lambda c: (c, dt_block)), head_vec, head_vec],
        out_specs=[blk, blk, blk, pl.BlockSpec((None, SSD_CHUNK, LANES), lambda c: (c, 0, 0)),
                   pl.BlockSpec((None, 1, LANES), lambda c: (c, 0, 0))],
        compiler_params=_cparams(("parallel",)))(zx, bias_p, alog_p)


def _ssd_common(dt_ref, cs_ref, last_ref, b_ref, c_ref):
    c_sz = SSD_CHUNK
    lane = lax.broadcasted_iota(jnp.int32, (c_sz, LANES), 1)
    row = lax.broadcasted_iota(jnp.int32, (c_sz, LANES), 0)
    bb = b_ref[...].astype(BF16)
    cb = c_ref[...].astype(BF16)
    scores = _dot(cb, bb, "nt")
    return dict(lane=lane, row=row, dt=dt_ref[...], cs=cs_ref[...], last_row=last_ref[...], bb=bb, cb=cb,
                scores=scores, causal=row >= lane, lo=lane < SSD_P)


def _pair_terms(q, cst_ref, h0):
    lane, lo = q["lane"], q["lo"]
    out = {}
    cols, dts, lasts, lms = [], [], [], []
    lane1 = lax.broadcasted_iota(jnp.int32, (1, LANES), 1)
    for h in (h0, h0 + 1):
        col = _lane_col(q["cs"], lane, h)
        rowv = cst_ref[pl.ds(h, 1), :]
        lms.append(jnp.exp(jnp.where(q["causal"], col - rowv, -1e30)))
        cols.append(col)
        dts.append(_lane_col(q["dt"], lane, h))
        lasts.append(jnp.sum(jnp.where(lane1 == h, q["last_row"], 0.0), axis=1, keepdims=True))
    out["lm"] = lms
    out["cols"] = cols
    out["lasts"] = lasts
    out["dt_b"] = jnp.where(lo, dts[0], dts[1])
    out["e_b"] = jnp.where(lo, jnp.exp(cols[0]), jnp.exp(cols[1]))
    out["dec_cols"] = [jnp.exp(lasts[0] - cols[0]), jnp.exp(lasts[1] - cols[1])]
    out["dec_b"] = jnp.where(lo, out["dec_cols"][0], out["dec_cols"][1])
    lo1 = lane1 < SSD_P
    out["explast"] = [jnp.exp(lasts[0]), jnp.exp(lasts[1])]
    out["explast_b"] = jnp.where(lo1, out["explast"][0], out["explast"][1])
    return out


def _ssd_fwd(zx, xc, decay, d_lane, nw, d_inner, after, name):
    L = zx.shape[0]
    nc = L // SSD_CHUNK
    gw = d_inner // SSD_G
    heads = gw // SSD_P
    n_pair = heads // 2
    bc0 = d_inner // LANES

    def body(z_ref, xs_ref, b_ref, c_ref, dt_ref, cs_ref, cst_ref, last_ref, dl_ref, nw_ref, after_ref,
             y_ref, yn_ref, prev_ref, s_ref):
        @pl.when(pl.program_id(1) == 0)
        def _():
            s_ref[...] = jnp.zeros_like(s_ref)

        q = _ssd_common(dt_ref, cs_ref, last_ref, b_ref, c_ref)
        prev_ref[...] = s_ref[...]
        lo = q["lo"]
        for j in range(n_pair):
            sl = slice(j * LANES, (j + 1) * LANES)
            p = _pair_terms(q, cst_ref, pl.program_id(0) * heads + 2 * j)
            xs_p = xs_ref[:, sl]
            xp = xs_p * p["dt_b"]
            xb = xp.astype(BF16)
            m_a = (q["scores"] * p["lm"][0]).astype(BF16)
            m_b = (q["scores"] * p["lm"][1]).astype(BF16)
            yd = jnp.where(lo, _dot(m_a, xb), _dot(m_b, xb))
            s_p = s_ref[:, sl]
            yo = _dot(q["cb"], s_p.astype(BF16)) * p["e_b"]
            y_ref[:, sl] = yd + yo + dl_ref[:, sl] * xs_p
            st = _dot(q["bb"], (xp * p["dec_b"]).astype(BF16), "tn")
            s_ref[:, sl] = s_p * p["explast_b"] + st
        yv = y_ref[...]
        zv = z_ref[...]
        yg = yv * (zv * _sigmoid(zv))
        rstd = lax.rsqrt(jnp.mean(yg * yg, axis=-1, keepdims=True) + NORM_EPS)
        yn_ref[...] = (yg * rstd * nw_ref[...]).astype(BF16)

    grp = lambda width: pl.BlockSpec((None, 1, width), lambda g, c: (g, 0, 0))
    dt_, _, cs_, cst_, last_ = decay
    return _pcall(
        body, name=name, grid=(SSD_G, nc),
        out_shape=[jax.ShapeDtypeStruct((L, d_inner), F32), jax.ShapeDtypeStruct((L, d_inner), BF16),
                   jax.ShapeDtypeStruct((nc, SSD_G, SSD_N, gw), F32)],
        in_specs=[pl.BlockSpec((SSD_CHUNK, gw), lambda g, c: (c, g)),
                  pl.BlockSpec((SSD_CHUNK, gw), lambda g, c: (c, g)),
                  pl.BlockSpec((SSD_CHUNK, SSD_N), lambda g, c: (c, bc0 + g)),
                  pl.BlockSpec((SSD_CHUNK, SSD_N), lambda g, c: (c, bc0 + SSD_G + g)),
                  pl.BlockSpec((SSD_CHUNK, LANES), lambda g, c: (c, 0)),
                  pl.BlockSpec((SSD_CHUNK, LANES), lambda g, c: (c, 0)),
                  pl.BlockSpec((None, SSD_CHUNK, LANES), lambda g, c: (c, 0, 0)),
                  pl.BlockSpec((None, 1, LANES), lambda g, c: (c, 0, 0)),
                  grp(gw), grp(gw), pl.BlockSpec(memory_space=pl.ANY)],
        out_specs=[pl.BlockSpec((SSD_CHUNK, gw), lambda g, c: (c, g)),
                   pl.BlockSpec((SSD_CHUNK, gw), lambda g, c: (c, g)),
                   pl.BlockSpec((None, None, SSD_N, gw), lambda g, c: (c, g, 0, 0))],
        scratch_shapes=[pltpu.VMEM((SSD_N, gw), F32)],
        compiler_params=_cparams(("parallel", "arbitrary")))(
            zx, xc, xc, xc, dt_, cs_, cst_, last_, d_lane, nw, after)


def _ssd_bwd(dyn, y, zx, xc, prev, decay, alog_p, d_lane, nw, d_inner, name):
    L = zx.shape[0]
    nc = L // SSD_CHUNK
    gw = d_inner // SSD_G
    heads = gw // SSD_P
    n_pair = heads // 2
    bc0 = d_inner // LANES

    def body(dyn_ref, y_ref, z_ref, xs_ref, b_ref, c_ref, prev_ref, dt_ref, sg_ref, cs_ref, cst_ref, last_ref,
             alog_ref, dl_ref, nw_ref,
             dz_ref, dxs_ref, db_ref, dc_ref, ddt_ref, dbias_ref, dalog_ref, dd_ref, dnw_ref,
             ds_ref, racc_ref):
        @pl.when(pl.program_id(1) == 0)
        def _():
            ds_ref[...] = jnp.zeros_like(ds_ref)
            dbias_ref[...] = jnp.zeros_like(dbias_ref)
            dalog_ref[...] = jnp.zeros_like(dalog_ref)
            dd_ref[...] = jnp.zeros_like(dd_ref)
            dnw_ref[...] = jnp.zeros_like(dnw_ref)

        q = _ssd_common(dt_ref, cs_ref, last_ref, b_ref, c_ref)
        a_row = -jnp.exp(alog_ref[...])
        lane, row, lo = q["lane"], q["row"], q["lo"]
        lane1 = lax.broadcasted_iota(jnp.int32, (1, LANES), 1)
        head0 = pl.program_id(0) * heads
        mine = (lane >= head0) & (lane < head0 + heads)

        yv, zv, dynv, nwv = y_ref[...], z_ref[...], dyn_ref[...], nw_ref[...]
        sig = _sigmoid(zv)
        sz = zv * sig
        yg = yv * sz
        rstd = lax.rsqrt(jnp.mean(yg * yg, axis=-1, keepdims=True) + NORM_EPS)
        yhat = yg * rstd
        dnw_ref[...] += jnp.sum(dynv * yhat, axis=0, keepdims=True)
        dyh = dynv * nwv
        dyg = rstd * (dyh - yhat * jnp.mean(dyh * yhat, axis=-1, keepdims=True))
        dz_ref[...] = (dyg * yv * (sig * (1.0 + zv * (1.0 - sig)))).astype(BF16)
        dy_all = dyg * sz

        dg = jnp.zeros((SSD_CHUNK, SSD_CHUNK), F32)
        dc_acc = jnp.zeros((SSD_CHUNK, SSD_N), F32)
        db_acc = jnp.zeros((SSD_CHUNK, SSD_N), F32)
        dcs_mat = jnp.zeros((SSD_CHUNK, LANES), F32)
        ddt_mat = jnp.zeros((SSD_CHUNK, LANES), F32)
        dd_row = jnp.zeros((1, LANES), F32)
        racc_ref[...] = jnp.zeros_like(racc_ref)
        is_last = row == SSD_CHUNK - 1

        for j in range(n_pair):
            sl = slice(j * LANES, (j + 1) * LANES)
            ha, hb = head0 + 2 * j, head0 + 2 * j + 1
            p = _pair_terms(q, cst_ref, ha)
            xs_p = xs_ref[:, sl]
            dyp = dy_all[:, sl]
            xp = xs_p * p["dt_b"]
            xb = xp.astype(BF16)
            s_p = prev_ref[:, sl]
            s_pb = s_p.astype(BF16)
            dsn = ds_ref[:, sl]
            dsnb = dsn.astype(BF16)
            m_f = [q["scores"] * p["lm"][0], q["scores"] * p["lm"][1]]

            t0 = dyp * xs_p
            dd_row = dd_row + jnp.where(lane1 == ha, _sum_all(jnp.where(lo, t0, 0.0)), 0.0) \
                + jnp.where(lane1 == hb, _sum_all(jnp.where(lo, 0.0, t0)), 0.0)
            dxs_p = dl_ref[:, sl] * dyp

            yo = _dot(q["cb"], s_pb) * p["e_b"]
            dcs_b = (dyp * p["e_b"]).astype(BF16)
            dc_acc = dc_acc + _dot(dcs_b, s_pb, "nt")
            ds_yo = _dot(q["cb"], dcs_b, "tn")
            t1 = dyp * yo
            dcs_cols = [jnp.sum(jnp.where(lo, t1, 0.0), axis=1, keepdims=True),
                        jnp.sum(jnp.where(lo, 0.0, t1), axis=1, keepdims=True)]

            t2 = dsn * s_p
            dlast = [p["explast"][0] * _sum_all(jnp.where(lo, t2, 0.0)),
                     p["explast"][1] * _sum_all(jnp.where(lo, 0.0, t2))]
            ds_ref[:, sl] = dsn * p["explast_b"] + ds_yo
            w = _dot(q["bb"], dsnb)
            db_acc = db_acc + _dot((xp * p["dec_b"]).astype(BF16), dsnb, "nt")
            dxp = w * p["dec_b"]
            t3 = w * xp
            e = [jnp.sum(jnp.where(lo, t3, 0.0), axis=1, keepdims=True) * p["dec_cols"][0],
                 jnp.sum(jnp.where(lo, 0.0, t3), axis=1, keepdims=True) * p["dec_cols"][1]]
            for i in range(2):
                dlast[i] = dlast[i] + jnp.sum(e[i], axis=0, keepdims=True)
                dcs_cols[i] = dcs_cols[i] - e[i]

            dyb = dyp.astype(BF16)
            dy_h = [jnp.where(lo, dyp, 0.0).astype(BF16), jnp.where(lo, 0.0, dyp).astype(BF16)]
            dms = [_dot(dy_h[0], xb, "nt"), _dot(dy_h[1], xb, "nt")]
            dxp = dxp + jnp.where(lo, _dot(m_f[0].astype(BF16), dyb, "tn"), _dot(m_f[1].astype(BF16), dyb, "tn"))
            for i, h in enumerate((ha, hb)):
                dg = dg + dms[i] * p["lm"][i]
                qm = dms[i] * m_f[i]
                dcs_cols[i] = dcs_cols[i] + jnp.sum(qm, axis=1, keepdims=True)
                racc_ref[pl.ds(h, 1), :] = jnp.sum(qm, axis=0, keepdims=True)

            dxs_ref[:, sl] = dxs_p + dxp * p["dt_b"]
            t4 = dxp * xs_p
            ddt_cols = [jnp.sum(jnp.where(lo, t4, 0.0), axis=1, keepdims=True),
                        jnp.sum(jnp.where(lo, 0.0, t4), axis=1, keepdims=True)]
            for i, h in enumerate((ha, hb)):
                sel = lane == h
                dcs_mat = dcs_mat + jnp.where(sel, dcs_cols[i], 0.0) + jnp.where(sel & is_last, dlast[i], 0.0)
                ddt_mat = ddt_mat + jnp.where(sel, ddt_cols[i], 0.0)

        dcs_mat = dcs_mat - racc_ref[...].T
        tri_t = (row <= lane).astype(BF16)
        da = _dot_exact01(tri_t, dcs_mat)
        ddt = ddt_mat + da * a_row
        dalog_ref[...] += jnp.sum(jnp.where(mine, da * q["dt"], 0.0), axis=0, keepdims=True) * a_row
        draw = jnp.where(mine, ddt * sg_ref[...], 0.0)
        ddt_ref[...] = draw
        dbias_ref[...] += jnp.sum(draw, axis=0, keepdims=True)
        dd_ref[...] += dd_row
        dgb = dg.astype(BF16)
        dc_ref[...] = dc_acc + _dot(dgb, q["bb"])
        db_ref[...] = db_acc + _dot(dgb, q["cb"], "tn")

    rev = lambda c: nc - 1 - c
    grp = lambda width: pl.BlockSpec((None, 1, width), lambda g, c: (g, 0, 0))
    blk = lambda width, off: pl.BlockSpec((SSD_CHUNK, width), lambda g, c: (rev(c), off + g))
    head_vec = pl.BlockSpec((1, LANES), lambda g, c: (0, 0))
    chunk_rows = pl.BlockSpec((SSD_CHUNK, LANES), lambda g, c: (rev(c), 0))
    dt_, sg_, cs_, cst_, last_ = decay
    return _pcall(
        body, name=name, grid=(SSD_G, nc),
        out_shape=[jax.ShapeDtypeStruct(zx.shape, BF16), jax.ShapeDtypeStruct((L, d_inner), F32),
                   jax.ShapeDtypeStruct((L, SSD_G * SSD_N), F32), jax.ShapeDtypeStruct((L, SSD_G * SSD_N), F32),
                   jax.ShapeDtypeStruct((SSD_G, L, LANES), F32),
                   jax.ShapeDtypeStruct((SSD_G, 1, LANES), F32), jax.ShapeDtypeStruct((SSD_G, 1, LANES), F32),
                   jax.ShapeDtypeStruct((SSD_G, 1, LANES), F32), jax.ShapeDtypeStruct((SSD_G, 1, gw), F32)],
        in_specs=[blk(gw, 0), blk(gw, 0), blk(gw, 0), blk(gw, 0), blk(SSD_N, bc0), blk(SSD_N, bc0 + SSD_G),
                  pl.BlockSpec((None, None, SSD_N, gw), lambda g, c: (rev(c), g, 0, 0)),
                  chunk_rows, chunk_rows, chunk_rows,
                  pl.BlockSpec((None, SSD_CHUNK, LANES), lambda g, c: (rev(c), 0, 0)),
                  pl.BlockSpec((None, 1, LANES), lambda g, c: (rev(c), 0, 0)),
                  head_vec, grp(gw), grp(gw)],
        out_specs=[blk(gw, 0), blk(gw, 0), blk(SSD_N, 0), blk(SSD_N, 0),
                   pl.BlockSpec((None, SSD_CHUNK, LANES), lambda g, c: (g, rev(c), 0)),
                   grp(LANES), grp(LANES), grp(LANES), grp(gw)],
        scratch_shapes=[pltpu.VMEM((SSD_N, gw), F32), pltpu.VMEM((SSD_CHUNK, LANES), F32)],
        compiler_params=_cparams(("parallel", "arbitrary")))(
            dyn, y, zx, xc, xc, xc, prev, dt_, sg_, cs_, cst_, last_, alog_p, d_lane, nw)


def _cond_mod(c_pad, ada_w, ada_b_loc, after, name):
    depth, D, n = ada_w.shape
    rows = c_pad.shape[0]

    def body(c_ref, w_ref, b_ref, after_ref, mod_ref, cond_ref):
        cv = c_ref[...]
        cond = cv * _sigmoid(cv)
        cond_ref[...] = cond
        mod_ref[...] = _dot(cond.astype(BF16), w_ref[...].astype(BF16)) + b_ref[...]

    return _pcall(
        body, name=name, grid=(depth,),
        out_shape=[jax.ShapeDtypeStruct((depth, rows, n), F32), jax.ShapeDtypeStruct((rows, D), F32)],
        in_specs=[pl.BlockSpec((rows, D), lambda i: (0, 0)),
                  pl.BlockSpec((None, D, n), lambda i: (i, 0, 0)),
                  pl.BlockSpec((None, 1, n), lambda i: (i, 0, 0)),
                  pl.BlockSpec(memory_space=pl.ANY)],
        out_specs=[pl.BlockSpec((None, rows, n), lambda i: (i, 0, 0)),
                   pl.BlockSpec((rows, D), lambda i: (0, 0))],
        compiler_params=_cparams(("arbitrary",)))(c_pad, ada_w, ada_b_loc, after)


def _adamw_math(g, w, m, v):
    m_new = ADAM_B1 * m + (1.0 - ADAM_B1) * g
    v_new = ADAM_B2 * v + (1.0 - ADAM_B2) * (g * g)
    m_hat = m_new / (1.0 - ADAM_B1 ** ADAM_STEP)
    v_hat = v_new / (1.0 - ADAM_B2 ** ADAM_STEP)
    delta = -ADAM_LR * (m_hat / (jnp.sqrt(v_hat) + ADAM_EPS) + ADAM_WD * w)
    return delta, m_new, v_new


def _adamw_sum(parts, w, m, v, layer, name, prev=None, tr=None, window_off=None):
    depth, R, C = w.shape
    tr = _tile(R, tr if tr is not None else (512 if C <= 512 else 256))
    win = parts.shape[2]
    scratch = [] if window_off is None else [pltpu.VMEM((tr, win), F32)]

    def body(p_ref, w_ref, m_ref, v_ref, *rest):
        g_ref, d_ref, mo_ref, vo_ref = rest[-4 - len(scratch):len(rest) - len(scratch)]
        g = p_ref[0].astype(F32)
        for k in range(1, N_DEV):
            g = g + p_ref[k].astype(F32)
        if window_off is not None:
            me = _my_index()
            off = 0
            for k in range(N_DEV):
                off = jnp.where(me == k, window_off[k], off)
            src = lax.broadcasted_iota(jnp.int32, (win, win), 0)
            dst = lax.broadcasted_iota(jnp.int32, (win, win), 1)
            shift = ((src == dst + off) & (dst < C)).astype(BF16)
            hi, mid, lo = _split3(g)
            rest[-1][...] = _dot(hi, shift) + _dot(mid, shift) + _dot(lo, shift)
            g = rest[-1][:, 0:C]
        d, mn, vn = _adamw_math(g, w_ref[...], m_ref[...], v_ref[...])
        g_ref[...] = g
        d_ref[...] = d
        mo_ref[...] = mn
        vo_ref[...] = vn

    blk = pl.BlockSpec((None, tr, C), lambda i: (layer, i, 0))
    prev = list(prev) if prev is not None else []
    return _pcall(
        body, name=name, grid=(R // tr,),
        out_shape=[jax.ShapeDtypeStruct((depth, R, C), F32)] * 4,
        in_specs=[pl.BlockSpec((N_DEV, tr, win), lambda i: (0, i, 0)), blk, blk, blk]
        + [pl.BlockSpec(memory_space=pl.ANY)] * len(prev),
        out_specs=[blk] * 4, input_output_aliases={4 + k: k for k in range(len(prev))},
        scratch_shapes=scratch,
        compiler_params=_cparams(("parallel",)))(parts, w, m, v, *prev)


def _adamw_small(parts, wmv, head_parts, head_wmv, loss_parts, name):
    n, nh = len(parts), len(head_parts)
    n_heads = head_wmv[0][0].shape[1] if nh else 0
    groups = head_parts[0].shape[1] if nh else 0
    d_model = loss_parts.shape[2]

    def body(*refs):
        p_refs, refs = refs[:n], refs[n:]
        wmv_refs, refs = refs[:3 * n], refs[3 * n:]
        hp_refs, refs = refs[:nh], refs[nh:]
        hwmv_refs, refs = refs[:3 * nh], refs[3 * nh:]
        loss_ref, refs = refs[0], refs[1:]
        outs, loss_out, head_scr = refs[:4 * (n + nh)], refs[4 * (n + nh)], refs[4 * (n + nh) + 1]

        def update(i, g, w_ref, m_ref, v_ref):
            res = (g,) + _adamw_math(g, w_ref[...], m_ref[...], v_ref[...])
            for o_ref, r in zip(outs[4 * i:4 * i + 4], res):
                o_ref[...] = r

        for i in range(n):
            g = p_refs[i][0]
            for k in range(1, N_DEV):
                g = g + p_refs[i][k]
            update(i, g, *wmv_refs[3 * i:3 * i + 3])
        for i in range(nh):
            g = None
            for k in range(N_DEV):
                for grp in range(groups):
                    g = hp_refs[i][k, grp] if g is None else g + hp_refs[i][k, grp]
            head_scr[...] = g
            update(n + i, head_scr[:, 0:n_heads], *hwmv_refs[3 * i:3 * i + 3])
        tot = loss_ref[0]
        for k in range(1, N_DEV):
            tot = tot + loss_ref[k]
        loss_out[...] = jnp.broadcast_to(_sum_all(tot) * (0.5 / d_model), loss_out.shape)

    operands = list(parts) + [a for t in wmv for a in t] + list(head_parts) + [a for t in head_wmv for a in t]
    operands.append(loss_parts)
    out_shape = [jax.ShapeDtypeStruct(t[0].shape, F32) for t in list(wmv) + list(head_wmv) for _ in range(4)]
    out_shape.append(jax.ShapeDtypeStruct((1, LANES), F32))
    vmem = pl.BlockSpec(memory_space=pltpu.VMEM)
    outs = _pcall(body, name=name, out_shape=out_shape, in_specs=[vmem] * len(operands),
                  out_specs=[vmem] * len(out_shape), scratch_shapes=[pltpu.VMEM((1, LANES), F32)],
                  compiler_params=_cparams())(*operands)
    return [outs[4 * i:4 * i + 4] for i in range(n + nh)], outs[-1]


def _ada_adamw(cond_pad, dmod_pad, w, m, v, name, tr=512):
    depth, D, n = w.shape
    rows = cond_pad.shape[0]
    tr = _tile(D, tr)

    def body(c_ref, dm_ref, w_ref, m_ref, v_ref, g_ref, d_ref, mo_ref, vo_ref):
        g = _dot(c_ref[...].astype(BF16), dm_ref[...].astype(BF16), "tn")
        d, mn, vn = _adamw_math(g, w_ref[...], m_ref[...], v_ref[...])
        g_ref[...] = g
        d_ref[...] = d
        mo_ref[...] = mn
        vo_ref[...] = vn

    blk = pl.BlockSpec((None, tr, n), lambda i, r: (i, r, 0))
    return _pcall(
        body, name=name, grid=(depth, D // tr),
        out_shape=[jax.ShapeDtypeStruct((depth, D, n), F32)] * 4,
        in_specs=[pl.BlockSpec((rows, tr), lambda i, r: (0, r)),
                  pl.BlockSpec((None, rows, n), lambda i, r: (i, 0, 0)), blk, blk, blk],
        out_specs=[blk] * 4, compiler_params=_cparams(("parallel", "parallel")))(cond_pad, dmod_pad, w, m, v)


def kernel(x, c, ada_w, ada_b, mix_norm_w, mlp_norm_w, mlp_up, mlp_down, ssd_in_w, ssd_conv_w, ssd_conv_b, ssd_dt_bias, ssd_A_log, ssd_D, ssd_norm_w, ssd_out_w, sc_in_w, sc_conv_w, sc_out_w, final_norm_w, loss_target, m_ada_w, m_ada_b, m_mix_norm_w, m_mlp_norm_w, m_mlp_up, m_mlp_down, m_ssd_in_w, m_ssd_conv_w, m_ssd_conv_b, m_ssd_dt_bias, m_ssd_A_log, m_ssd_D, m_ssd_norm_w, m_ssd_out_w, m_sc_in_w, m_sc_conv_w, m_sc_out_w, m_final_norm_w, v_ada_w, v_ada_b, v_mix_norm_w, v_mlp_norm_w, v_mlp_up, v_mlp_down, v_ssd_in_w, v_ssd_conv_w, v_ssd_conv_b, v_ssd_dt_bias, v_ssd_A_log, v_ssd_D, v_ssd_norm_w, v_ssd_out_w, v_sc_in_w, v_sc_conv_w, v_sc_out_w, v_final_norm_w):
    weights = dict(ada_w=ada_w, ada_b=ada_b, mix_norm_w=mix_norm_w, mlp_norm_w=mlp_norm_w, mlp_up=mlp_up,
                   mlp_down=mlp_down, ssd_in_w=ssd_in_w, ssd_conv_w=ssd_conv_w, ssd_conv_b=ssd_conv_b,
                   ssd_dt_bias=ssd_dt_bias, ssd_A_log=ssd_A_log, ssd_D=ssd_D, ssd_norm_w=ssd_norm_w,
                   ssd_out_w=ssd_out_w, sc_in_w=sc_in_w, sc_conv_w=sc_conv_w, sc_out_w=sc_out_w,
                   final_norm_w=final_norm_w)
    moms = dict(ada_w=m_ada_w, ada_b=m_ada_b, mix_norm_w=m_mix_norm_w, mlp_norm_w=m_mlp_norm_w, mlp_up=m_mlp_up,
                mlp_down=m_mlp_down, ssd_in_w=m_ssd_in_w, ssd_conv_w=m_ssd_conv_w, ssd_conv_b=m_ssd_conv_b,
                ssd_dt_bias=m_ssd_dt_bias, ssd_A_log=m_ssd_A_log, ssd_D=m_ssd_D, ssd_norm_w=m_ssd_norm_w,
                ssd_out_w=m_ssd_out_w, sc_in_w=m_sc_in_w, sc_conv_w=m_sc_conv_w, sc_out_w=m_sc_out_w,
                final_norm_w=m_final_norm_w)
    vars_ = dict(ada_w=v_ada_w, ada_b=v_ada_b, mix_norm_w=v_mix_norm_w, mlp_norm_w=v_mlp_norm_w, mlp_up=v_mlp_up,
                 mlp_down=v_mlp_down, ssd_in_w=v_ssd_in_w, ssd_conv_w=v_ssd_conv_w, ssd_conv_b=v_ssd_conv_b,
                 ssd_dt_bias=v_ssd_dt_bias, ssd_A_log=v_ssd_A_log, ssd_D=v_ssd_D, ssd_norm_w=v_ssd_norm_w,
                 ssd_out_w=v_ssd_out_w, sc_in_w=v_sc_in_w, sc_conv_w=v_sc_conv_w, sc_out_w=v_sc_out_w,
                 final_norm_w=v_final_norm_w)
    names = list(weights)

    L, D = x.shape[1], x.shape[2]
    d_inner = 2 * D
    n_heads = d_inner // SSD_P
    hpg = n_heads // SSD_G
    gw = d_inner // SSD_G
    conv_dim = d_inner + 2 * SSD_G * SSD_N
    zx_dim = d_inner + conv_dim
    zx_pad = -(-(zx_dim + LANES) // 512) * 512
    in_ws = ssd_in_w.shape[2]
    in_base, in_off, in_win = _window_geometry(in_ws)
    me = _my_index()
    x0 = x[0]
    tgt = loss_target[0]

    n_mod = ada_w.shape[2]
    (c_all,) = _exchange([c], "gather_c", gather=True)
    gather_handle = {}
    (gather_handle["ssd_in_w"],), token_in = _xfer_start(
        [ssd_in_w[0].astype(BF16)], "gather_start_ssd_in_w", gather=True, via_sibling=(0,), after=(c_all,))
    c_pad = jnp.pad(c_all.reshape(N_DEV, D), ((0, 16 - N_DEV), (0, 0)))
    ada_b_loc = lax.dynamic_slice_in_dim(ada_b, me * n_mod, n_mod, axis=1).reshape(2, 1, n_mod)
    mod_blk, cond_pad = _cond_mod(c_pad, ada_w, ada_b_loc, token_in, "cond_mod")
    gather_order = ["mod", "ssd_conv_w", "sc_conv_w", "ssd_out_w", "up0", "down0", "sc_in_w", "sc_out_w", "up1",
                    "down1"]
    gather_src = dict(mod=mod_blk, ssd_conv_w=ssd_conv_w[0], sc_conv_w=sc_conv_w[0],
                      ssd_out_w=ssd_out_w[0].astype(BF16),
                      up0=mlp_up[0].astype(BF16), down0=mlp_down[0].astype(BF16),
                      sc_in_w=sc_in_w[0].astype(BF16), sc_out_w=sc_out_w[0].astype(BF16),
                      up1=mlp_up[1].astype(BF16), down1=mlp_down[1].astype(BF16))
    handles, gather_token = _xfer_start([gather_src[k] for k in gather_order], "gather_start", gather=True,
                                        via_sibling=tuple(range(3, len(gather_order))))
    gather_handle.update(zip(gather_order, handles))

    def gathered(keys, after, forward):
        tag = "_".join(keys)
        lands = _xfer_wait([gather_handle[k] for k in keys], after, f"gather_wait_{tag}", gather=True)
        return _sibling_forward(lands, f"gather_forward_{tag}") if forward else lands

    def forward_behind(keys, after):
        tag = "_".join(keys)
        lands = _xfer_wait([gather_handle[k] for k in keys], after, f"gather_wait_{tag}", gather=True)
        fwd_handles, token = _sibling_forward_start(lands, f"gather_forward_start_{tag}")
        return (lambda done: _sibling_forward_wait(fwd_handles, done, f"gather_forward_wait_{tag}")), token

    (ssd_in_g,) = gathered(["ssd_in_w"], (gather_token, m_ssd_in_w, v_ssd_in_w), True)
    w_in_all = _shards_to_columns(ssd_in_g, in_base, in_off, in_win, zx_pad, "ssd_in_w_columns")
    (mod_all,) = gathered(["mod"], w_in_all, False)
    mod_mine = lax.dynamic_index_in_dim(mod_all, me, axis=2, keepdims=False)
    mod_mine = jnp.transpose(mod_mine, (1, 0, 2)).reshape(2, 6, 1, D)
    sh_m, sc_m, g_m, sh_f, sc_f, g_f = [[mod_mine[i, k] for i in range(2)] for k in range(6)]

    vec = lambda a: a.reshape(1, -1)
    grads = {}
    small = {}

    _, h0 = _norm_mod_fwd(x0, None, None, vec(mix_norm_w[0]), sc_m[0], sh_m[0], "l0_mix_norm")
    cw_all, scw_all = gathered(["ssd_conv_w", "sc_conv_w"], h0, False)
    (zx,) = _mm_nn(h0, w_in_all, F32, "ssd_in_proj", tm=2048, tn=512)
    conv_b0 = vec(ssd_conv_b[0])
    conv_w_full = jnp.transpose(cw_all, (1, 0, 2)).reshape(SSD_K, conv_dim)
    sc_conv_full = jnp.transpose(scw_all, (1, 0, 2)).reshape(SC_K, D)
    xc = _ssd_conv_fwd(zx, conv_w_full, conv_b0, d_inner, conv_dim, "ssd_conv")
    bias_p = jnp.pad(ssd_dt_bias[0], (0, LANES - n_heads)).reshape(1, LANES)
    alog_p = jnp.pad(ssd_A_log[0], (0, LANES - n_heads)).reshape(1, LANES)
    d_lane = jnp.repeat(ssd_D[0], SSD_P).reshape(SSD_G, 1, gw)
    nw_g = ssd_norm_w[0].reshape(SSD_G, 1, gw)
    finish, token = forward_behind(["ssd_out_w"], xc)
    decay = _ssd_decay(zx, bias_p, alog_p, n_heads, zx_dim // LANES, "ssd_decay")
    y_ssd, yn, prev = _ssd_fwd(zx, xc, decay, d_lane, nw_g, d_inner, token, "ssd_scan")
    ups, downs = [None, None], [None, None]
    (ssd_out_g,) = finish(yn)
    w_ssd_out = ssd_out_g.reshape(-1, D)
    finish, token = forward_behind(["up0", "down0"], ssd_out_g)
    (mix0,) = _mm_nn(yn, w_ssd_out, F32, "ssd_out_proj", after=(token,))
    x1, h1 = _norm_mod_fwd(x0, mix0, g_m[0], vec(mlp_norm_w[0]), sc_f[0], sh_f[0], "l0_mlp_norm")
    ups[0], down0_g = finish(h1)
    downs[0] = down0_g.reshape(-1, D)
    u0, s0 = _mm_nn_blocked(h1, ups[0], "l0_mlp_up", _ep_relu2, [BF16, BF16])
    finish, token = forward_behind(["sc_in_w", "sc_out_w", "up1", "down1"], s0)
    (d0,) = _mm_nn(s0, downs[0], F32, "l0_mlp_down", after=(token,))
    x2, h2 = _norm_mod_fwd(x1, d0, g_f[0], vec(mix_norm_w[1]), sc_m[1], sh_m[1], "l1_mix_norm")
    sc_in_g, sc_out_g, ups[1], down1_g = finish(h2)
    w_sc_out, downs[1] = sc_out_g.reshape(-1, D), down1_g.reshape(-1, D)
    (proj,) = _mm_nn_blocked(h2, sc_in_g, "sc_in_proj", _ep_store(F32), [F32])
    yc = _sc_conv_fwd(proj, sc_conv_full, "sc_conv")
    (mix1,) = _mm_nn(yc, w_sc_out, F32, "sc_out_proj")
    x3, h3 = _norm_mod_fwd(x2, mix1, g_m[1], vec(mlp_norm_w[1]), sc_f[1], sh_f[1], "l1_mlp_norm")
    u1, s1 = _mm_nn_blocked(h3, ups[1], "l1_mlp_up", _ep_relu2, [BF16, BF16])
    (d1,) = _mm_nn(s1, downs[1], F32, "l1_mlp_down")

    dx, loss_lane, dfw, dd1, dg = _final_loss(x3, d1, g_f[1], vec(final_norm_w), tgt, "final_loss")
    small["final_norm_w"] = dfw

    dmod = [[None] * 6 for _ in range(2)]
    dmod[1][5] = dg

    def mlp_backward(i, dx_out, dd, x_mid, h_in, u, s, mix, gate):
        du = _mm_nt(dd, downs[i], BF16, f"l{i}_mlp_down_bwd", epilogue=_ep_relu2_bwd, extra=(u,))
        gdown = _mm_tn(s, dd, BF16, f"l{i}_mlp_down_wgrad").reshape(N_DEV, -1, D)
        gup = _mm_tn_blocked(h_in, du, BF16, f"l{i}_mlp_up_wgrad")
        (h_down, h_up), token = _xfer_start([gdown, gup], f"l{i}_mlp_grads_start", gather=False)
        grad_handle[f"mlp_down{i}"], grad_handle[f"mlp_up{i}"] = h_down, h_up
        dh = _mm_nt_blocked(du, ups[i], F32, f"l{i}_mlp_up_bwd", after=(token,))
        dxm, dsh, dsc, dnw, dmix, dgate = _norm_mod_bwd(dh, x_mid, vec(mlp_norm_w[i]), sc_f[i], dx_out,
                                                        f"l{i}_mlp_norm_bwd", branch=(mix, gate))
        dmod[i][3], dmod[i][4], dmod[i][2] = dsh, dsc, dgate
        return dxm, dmix, dnw

    grad_handle = {}
    dx3, dyc, dnw_mlp1 = mlp_backward(1, dx, dd1, x3, h3, u1, s1, mix1, g_m[1])
    g_sc_out = _mm_tn(yc, dyc, BF16, "sc_out_wgrad").reshape(N_DEV, -1, D)
    dconv_out = _mm_nt(dyc, w_sc_out, F32, "sc_out_bwd")
    dbg, dcg, dxv, dscw = _sc_conv_bwd(proj, sc_conv_full, dconv_out, "sc_conv_bwd")
    dproj = jnp.concatenate([dbg, dcg, dxv], axis=1)
    g_sc_in = _mm_tn_blocked(h2, dproj, BF16, "sc_in_wgrad")
    (grad_handle["sc_out_w0"], grad_handle["sc_in_w0"]), token = _xfer_start(
        [g_sc_out, g_sc_in], "sc_grads_start", gather=False)
    dh2 = _mm_nt_blocked(dproj, sc_in_g, F32, "sc_in_bwd", after=(token,))
    dx2, dsh, dsc, dnw_mix1, dd0, dg = _norm_mod_bwd(dh2, x2, vec(mix_norm_w[1]), sc_m[1], dx3, "l1_mix_norm_bwd",
                                                     branch=(d0, g_f[0]))
    dmod[1][0], dmod[1][1], dmod[0][5] = dsh, dsc, dg
    dx1, dyo, dnw_mlp0 = mlp_backward(0, dx2, dd0, x1, h1, u0, s0, mix0, g_m[0])
    g_ssd_out = _mm_tn(yn, dyo, BF16, "ssd_out_wgrad").reshape(N_DEV, -1, D)
    (grad_handle["ssd_out_w0"],), token = _xfer_start([g_ssd_out], "ssd_out_grad_start", gather=False)
    dyn = _mm_nt(dyo, w_ssd_out, F32, "ssd_out_bwd", after=(token,))
    dz, dxs, db_, dc_, ddt, dbias, dalog, dd_, dnw_ssd = _ssd_bwd(
        dyn, y_ssd, zx, xc, prev, decay, alog_p, d_lane, nw_g, d_inner, "ssd_scan_bwd")
    dzx, dcw, dcb = _ssd_conv_bwd(zx, conv_w_full, conv_b0, [dxs, db_, dc_], dz, d_inner, "ssd_conv_bwd")
    dzx = _dzx_finish(dzx, ddt, zx_dim, "ssd_dzx_finish")
    g_in_all = _mm_tn(h0, dzx, BF16, "ssd_in_wgrad", tn=512, tk=2048)
    g_ssd_in = jnp.stack([g_in_all[:, b:b + in_win] for b in in_base], axis=0)
    (grad_handle["ssd_in_w0"],), token = _xfer_start([g_ssd_in], "ssd_in_grad_start", gather=False)
    dh0 = _mm_nt(dzx, w_in_all, F32, "ssd_in_bwd", tm=1024, tk=dzx.shape[1] // 2, after=(token,))
    grad_x, dsh, dsc, dnw_mix0 = _norm_mod_bwd(dh0, x0, vec(mix_norm_w[0]), sc_m[0], dx1, "l0_mix_norm_bwd")
    dmod[0][0], dmod[0][1] = dsh, dsc

    small["ada_b"] = jnp.concatenate([jnp.concatenate(dmod[i], axis=1) for i in range(2)], axis=0)
    small["mix_norm_w"] = jnp.concatenate([dnw_mix0, dnw_mix1], axis=0)
    small["mlp_norm_w"] = jnp.concatenate([dnw_mlp0, dnw_mlp1], axis=0)
    small["ssd_conv_w"] = dcw
    small["ssd_conv_b"] = dcb
    small["ssd_norm_w"] = dnw_ssd.reshape(1, d_inner)
    small["sc_conv_w"] = dscw
    small["loss"] = loss_lane
    small_names = list(small)
    head_names = ["ssd_dt_bias", "ssd_A_log", "ssd_D"]
    handles, small_token = _xfer_start([small[k] for k in small_names] + [dbias, dalog, dd_],
                                       "small_grads_start", gather=True)

    out_g, out_d, out_m, out_v = {}, {}, {}, {}

    layer_res = {}

    def big_update(name, i, after):
        (parts,) = _xfer_wait([grad_handle[f"{name}{i}"]], after, f"grads_wait_{name}_{i}", gather=False)
        res = _adamw_sum(parts, weights[name], moms[name], vars_[name], i, f"adamw_{name}_{i}",
                         prev=layer_res.get(name), window_off=in_off if name == "ssd_in_w" else None)
        layer_res[name] = res
        return res[1]

    chain = small_token
    for name, i in [("mlp_down", 1), ("mlp_up", 1), ("sc_out_w", 0), ("sc_in_w", 0), ("mlp_down", 0),
                    ("mlp_up", 0), ("ssd_out_w", 0), ("ssd_in_w", 0)]:
        chain = big_update(name, i, chain)
    gathered_small = _xfer_wait(handles, chain, "small_grads_wait", gather=True)
    small_all = dict(zip(small_names + head_names, gathered_small))

    dmod_loc = lax.dynamic_slice_in_dim(small_all["ada_b"], me * n_mod, n_mod, axis=2)
    dmod_pad = jnp.pad(jnp.transpose(dmod_loc, (1, 0, 2)), ((0, 0), (0, 16 - N_DEV), (0, 0)))
    out_g["ada_w"], out_d["ada_w"], out_m["ada_w"], out_v["ada_w"] = _ada_adamw(
        cond_pad, dmod_pad, ada_w, m_ada_w, v_ada_w, "adamw_ada_w")

    for k in ("ssd_conv_w", "sc_conv_w"):
        n_loc = weights[k].shape[2]
        small_all[k] = lax.dynamic_slice_in_dim(small_all[k], me * n_loc, n_loc, axis=2)
    plain = [k for k in small_names if k != "loss"]
    as2d = lambda a: a.reshape(-1, a.shape[-1])
    res, loss_row = _adamw_small(
        [small_all[k] for k in plain], [tuple(as2d(d[k]) for d in (weights, moms, vars_)) for k in plain],
        [small_all[k] for k in head_names], [tuple(as2d(d[k]) for d in (weights, moms, vars_)) for k in head_names],
        small_all["loss"], "adamw_small")
    loss = loss_row[0, 0]
    for k, res4 in zip(plain + head_names, res):
        for r, dst in zip(res4, (out_g, out_d, out_m, out_v)):
            dst[k] = r.reshape(weights[k].shape)
    for name, res4 in layer_res.items():
        for r, dst in zip(res4, (out_g, out_d, out_m, out_v)):
            dst[name] = r

    return (loss, grad_x[None], *[out_g[k] for k in names], *[out_d[k] for k in names],
            *[out_m[k] for k in names], *[out_v[k] for k in names])
```

```python
import functools

import jax
import jax.numpy as jnp
from jax import lax
from jax.experimental import pallas as pl
from jax.experimental.pallas import tpu as pltpu

F32 = jnp.float32
BF16 = jnp.bfloat16
N_DEV = 8
MESH_AXES = ("x", "y", "c")
MESH = pl.DeviceIdType.MESH

NORM_EPS = 1e-5
SSD_G = 4
SSD_P = 64
SSD_N = 128
SSD_CHUNK = 128
SSD_K = 4
SC_K = 3
LANES = 128

ADAM_LR = 0.001
ADAM_B1 = 0.9
ADAM_B2 = 0.999
ADAM_EPS = 1e-08
ADAM_WD = 0.01
ADAM_STEP = 10

VMEM_LIMIT = 56 * 1024 * 1024


def _pcall(body, **kw):
    return pl.pallas_call(body, **kw)


def _cparams(sem=None):
    if sem is None:
        return pltpu.CompilerParams(vmem_limit_bytes=VMEM_LIMIT)
    return pltpu.CompilerParams(dimension_semantics=sem, vmem_limit_bytes=VMEM_LIMIT)


def _my_index():
    return 4 * lax.axis_index("x") + 2 * lax.axis_index("y") + lax.axis_index("c")


_PEER_MASKS = [(0, 0, 1), (0, 1, 0), (0, 1, 1), (1, 0, 0), (1, 0, 1), (1, 1, 0), (1, 1, 1)]


def _peers():
    x, y, c = lax.axis_index("x"), lax.axis_index("y"), lax.axis_index("c")
    out = []
    for mx, my, mc in _PEER_MASKS:
        px = (1 - x) if mx else x
        py = (1 - y) if my else y
        pc = (1 - c) if mc else c
        out.append(((px, py, pc), 4 * px + 2 * py + pc))
    return out


def _exchange(arrs, name, gather):
    n = len(arrs)
    n_peer = N_DEV - 1

    def body(*refs):
        ins, outs = refs[:n], refs[n:2 * n]
        send_sems, recv_sems, local_sems = refs[2 * n:]
        me = _my_index()
        peers = _peers()
        started = []
        for a in range(n):
            src_own = ins[a] if gather else ins[a].at[me]
            own = pltpu.make_async_copy(src_own, outs[a].at[me], local_sems.at[a])
            own.start()
            started.append(own)
        sends = []
        for a in range(n):
            for k, (peer, pidx) in enumerate(peers):
                src = ins[a] if gather else ins[a].at[pidx]
                cp = pltpu.make_async_remote_copy(
                    src_ref=src, dst_ref=outs[a].at[me],
                    send_sem=send_sems.at[a * n_peer + k], recv_sem=recv_sems.at[a * n_peer + k],
                    device_id=peer, device_id_type=MESH)
                cp.start()
                sends.append(cp)
        for a in range(n):
            for k, (peer, pidx) in enumerate(peers):
                src = ins[a] if gather else ins[a].at[pidx]
                pltpu.make_async_remote_copy(
                    src_ref=src, dst_ref=outs[a].at[pidx],
                    send_sem=send_sems.at[a * n_peer + k], recv_sem=recv_sems.at[a * n_peer + k],
                    device_id=peer, device_id_type=MESH).wait_recv()
        for cp in sends:
            cp.wait_send()
        for own in started:
            own.wait()

    if gather:
        out_shape = [jax.ShapeDtypeStruct((N_DEV,) + a.shape, a.dtype) for a in arrs]
    else:
        out_shape = [jax.ShapeDtypeStruct(a.shape, a.dtype) for a in arrs]
    any_spec = pl.BlockSpec(memory_space=pl.ANY)
    outs = _pcall(
        body, name=name, out_shape=out_shape,
        in_specs=[any_spec] * n, out_specs=[any_spec] * n,
        scratch_shapes=[pltpu.SemaphoreType.DMA((n * n_peer,)), pltpu.SemaphoreType.DMA((n * n_peer,)),
                        pltpu.SemaphoreType.DMA((n,))],
        compiler_params=pltpu.CompilerParams(has_side_effects=True),
    )(*arrs)
    return list(outs)


def _sibling_forward_start(lands, name):
    n = len(lands)
    n_fwd = len(_OTHER_CHIPS)

    def body(*refs):
        ins, bufs = refs[:n], refs[3 * n:4 * n]
        token = refs[-1]
        sibling = (lax.axis_index("x"), lax.axis_index("y"), 1 - lax.axis_index("c"))
        peers = _peers()
        for a in range(n):
            send_sems, recv_sems = refs[n + 2 * a], refs[n + 2 * a + 1]
            for j, k in enumerate(_OTHER_CHIPS):
                slot = peers[k][1]
                pltpu.make_async_remote_copy(
                    src_ref=ins[a].at[slot], dst_ref=bufs[a].at[slot], send_sem=send_sems.at[j],
                    recv_sem=recv_sems.at[j], device_id=sibling, device_id_type=MESH).start()
        token[...] = jnp.zeros_like(token)

    out_shape, out_specs = [], []
    for _ in range(n):
        out_shape += [pltpu.SemaphoreType.DMA((n_fwd,)), pltpu.SemaphoreType.DMA((n_fwd,))]
        out_specs += [_SEM, _SEM]
    out_shape += [pltpu.HBM(a.shape, a.dtype) for a in lands] + [jax.ShapeDtypeStruct((8, LANES), F32)]
    out_specs += [_HBM] * n + [pl.BlockSpec(memory_space=pltpu.VMEM)]
    outs = _pcall(
        body, name=name, out_shape=tuple(out_shape), in_specs=[_HBM] * n, out_specs=tuple(out_specs),
        input_output_aliases={a: 2 * n + a for a in range(n)},
        compiler_params=pltpu.CompilerParams(has_side_effects=_DATAFLOW),
    )(*[pltpu.with_memory_space_constraint(a, pltpu.HBM) for a in lands])
    return [(outs[2 * n + a], outs[2 * a], outs[2 * a + 1]) for a in range(n)], outs[-1]


def _sibling_forward_wait(handles, after, name):
    n = len(handles)

    def body(*refs):
        sibling = (lax.axis_index("x"), lax.axis_index("y"), 1 - lax.axis_index("c"))
        peers = _peers()
        for a in range(n):
            buf, send_sems, recv_sems = refs[3 * a:3 * a + 3]
            for j, k in enumerate(_OTHER_CHIPS):
                (px, py, pc), slot = peers[k]
                theirs = 4 * px + 2 * py + (1 - pc)
                cp = pltpu.make_async_remote_copy(
                    src_ref=buf.at[slot], dst_ref=buf.at[theirs], send_sem=send_sems.at[j],
                    recv_sem=recv_sems.at[j], device_id=sibling, device_id_type=MESH)
                cp.wait_send()
                cp.wait_recv()

    operands, in_specs = [], []
    for h in handles:
        operands += list(h)
        in_specs += [_HBM, _SEM, _SEM]
    outs = _pcall(
        body, name=name, out_shape=tuple(pltpu.HBM(h[0].shape, h[0].dtype) for h in handles),
        in_specs=in_specs + [pl.BlockSpec(memory_space=pl.ANY)], out_specs=tuple([_HBM] * n),
        input_output_aliases={3 * a: a for a in range(n)},
        compiler_params=pltpu.CompilerParams(has_side_effects=_DATAFLOW),
    )(*operands, after)
    return list(outs)


_HBM = pl.BlockSpec(memory_space=pltpu.HBM)
_SEM = pl.BlockSpec(memory_space=pltpu.SEMAPHORE)
_DATAFLOW = pltpu.SideEffectType.DATAFLOW_SIDE_EFFECTING


_ALL_PEERS = tuple(range(N_DEV - 1))
_SAME_CORE_PEERS = (0, 1, 3, 5)
_OTHER_CHIPS = (1, 3, 5)


def _xfer_start(arrs, name, gather, via_sibling=(), after=()):
    n = len(arrs)
    n_peer = N_DEV - 1
    n_after = len(after)
    peer_ks = [_SAME_CORE_PEERS if a in via_sibling else _ALL_PEERS for a in range(n)]

    def body(*refs):
        ins, lands = refs[:n], refs[n:2 * n]
        sems = refs[2 * n + n_after:5 * n + n_after]
        token = refs[-1]
        me = _my_index()
        peers = _peers()
        for a in range(n):
            send_sems, recv_sems, loc_sem = sems[3 * a:3 * a + 3]
            src_own = ins[a] if gather else ins[a].at[me]
            pltpu.make_async_copy(src_own, lands[a].at[me], loc_sem).start()
            for k in peer_ks[a]:
                peer, pidx = peers[k]
                src = ins[a] if gather else ins[a].at[pidx]
                pltpu.make_async_remote_copy(
                    src_ref=src, dst_ref=lands[a].at[me], send_sem=send_sems.at[k], recv_sem=recv_sems.at[k],
                    device_id=peer, device_id_type=MESH).start()
        token[...] = jnp.zeros_like(token)

    land_shapes = [((N_DEV,) + a.shape) if gather else a.shape for a in arrs]
    out_shape, out_specs = [], []
    for _ in range(n):
        out_shape += [pltpu.SemaphoreType.DMA((n_peer,)), pltpu.SemaphoreType.DMA((n_peer,)),
                      pltpu.SemaphoreType.DMA(())]
        out_specs += [_SEM, _SEM, _SEM]
    out_shape += [pltpu.HBM(a.shape, a.dtype) for a in arrs]
    out_shape += [pltpu.HBM(s, a.dtype) for s, a in zip(land_shapes, arrs)]
    out_shape += [jax.ShapeDtypeStruct((8, LANES), F32)]
    out_specs += [_HBM] * (2 * n) + [pl.BlockSpec(memory_space=pltpu.VMEM)]
    aliases = {}
    for a in range(n):
        aliases[a] = 3 * n + a
        aliases[n + a] = 4 * n + a
    operands = [pltpu.with_memory_space_constraint(a, pltpu.HBM) for a in arrs]
    operands += [pltpu.with_memory_space_constraint(lax.empty(s, a.dtype), pltpu.HBM)
                 for s, a in zip(land_shapes, arrs)]
    outs = _pcall(
        body, name=name, out_shape=tuple(out_shape),
        in_specs=[_HBM] * (2 * n) + [pl.BlockSpec(memory_space=pl.ANY)] * n_after, out_specs=tuple(out_specs),
        input_output_aliases=aliases,
        compiler_params=pltpu.CompilerParams(has_side_effects=_DATAFLOW),
    )(*operands, *after)
    handles = []
    for a in range(n):
        handles.append((outs[3 * n + a], outs[4 * n + a], outs[3 * a], outs[3 * a + 1], outs[3 * a + 2],
                        peer_ks[a]))
    return handles, outs[-1]


def _xfer_wait(handles, after, name, gather):
    n = len(handles)
    after = tuple(after) if isinstance(after, (tuple, list)) else (after,)
    peer_ks = [h[5] for h in handles]

    def body(*refs):
        me = _my_index()
        peers = _peers()
        for a in range(n):
            src_ref, land_ref, send_ref, recv_ref, loc_ref = refs[5 * a:5 * a + 5]
            src_own = src_ref if gather else src_ref.at[me]
            pltpu.make_async_copy(src_own, land_ref.at[me], loc_ref).wait()
            for k in peer_ks[a]:
                peer, pidx = peers[k]
                src = src_ref if gather else src_ref.at[pidx]
                cp = pltpu.make_async_remote_copy(
                    src_ref=src, dst_ref=land_ref.at[pidx], send_sem=send_ref.at[k], recv_sem=recv_ref.at[k],
                    device_id=peer, device_id_type=MESH)
                cp.wait_send()
                cp.wait_recv()

    operands, in_specs, out_shape, aliases = [], [], [], {}
    for a, h in enumerate(handles):
        operands += list(h[:5])
        in_specs += [_HBM, _HBM, _SEM, _SEM, _SEM]
        out_shape += [pltpu.HBM(h[0].shape, h[0].dtype), pltpu.HBM(h[1].shape, h[1].dtype)]
        aliases[5 * a] = 2 * a
        aliases[5 * a + 1] = 2 * a + 1
    outs = _pcall(
        body, name=name, out_shape=tuple(out_shape),
        in_specs=in_specs + [pl.BlockSpec(memory_space=pl.ANY)] * len(after),
        out_specs=tuple([_HBM] * (2 * n)), input_output_aliases=aliases,
        compiler_params=pltpu.CompilerParams(has_side_effects=_DATAFLOW),
    )(*operands, *after)
    return [outs[2 * a + 1] for a in range(n)]


def _sibling_forward(lands, name):
    n = len(lands)
    n_fwd = len(_OTHER_CHIPS)

    def body(*refs):
        ins, bufs = refs[:n], refs[n:2 * n]
        send_sems, recv_sems = refs[2 * n:]
        x, y, c = lax.axis_index("x"), lax.axis_index("y"), lax.axis_index("c")
        sibling = (x, y, 1 - c)
        peers = _peers()
        sends = []
        for a in range(n):
            for j, k in enumerate(_OTHER_CHIPS):
                slot = peers[k][1]
                cp = pltpu.make_async_remote_copy(
                    src_ref=ins[a].at[slot], dst_ref=bufs[a].at[slot],
                    send_sem=send_sems.at[a * n_fwd + j], recv_sem=recv_sems.at[a * n_fwd + j],
                    device_id=sibling, device_id_type=MESH)
                cp.start()
                sends.append(cp)
        for a in range(n):
            for j, k in enumerate(_OTHER_CHIPS):
                (px, py, pc), slot = peers[k]
                theirs = 4 * px + 2 * py + (1 - pc)
                pltpu.make_async_remote_copy(
                    src_ref=ins[a].at[slot], dst_ref=bufs[a].at[theirs],
                    send_sem=send_sems.at[a * n_fwd + j], recv_sem=recv_sems.at[a * n_fwd + j],
                    device_id=sibling, device_id_type=MESH).wait_recv()
        for cp in sends:
            cp.wait_send()

    any_spec = pl.BlockSpec(memory_space=pl.ANY)
    outs = _pcall(
        body, name=name, out_shape=[jax.ShapeDtypeStruct(a.shape, a.dtype) for a in lands],
        in_specs=[any_spec] * n, out_specs=[any_spec] * n,
        input_output_aliases={a: a for a in range(n)},
        scratch_shapes=[pltpu.SemaphoreType.DMA((n * n_fwd,)), pltpu.SemaphoreType.DMA((n * n_fwd,))],
        compiler_params=pltpu.CompilerParams(has_side_effects=True),
    )(*lands)
    return list(outs)


_DIMS = {"nn": (((1,), (0,)), ((), ())), "nt": (((1,), (1,)), ((), ())), "tn": (((0,), (0,)), ((), ()))}


def _dot(a, b, mode="nn"):
    return lax.dot_general(a, b, _DIMS[mode], preferred_element_type=F32)


def _mm(a, b, *, mode, grid, a_spec, b_spec, out_shape, out_specs, acc_shape, epilogue, name,
        extra=(), extra_specs=(), after=(), semantics=("parallel", "parallel", "arbitrary")):
    nk = grid[2]
    n_extra = len(extra)
    n_in = 2 + n_extra + len(after)

    def body_single(*refs):
        a_ref, b_ref = refs[0], refs[1]
        epilogue(_dot(a_ref[...], b_ref[...], mode), refs[2:2 + n_extra], refs[n_in:])

    def body_acc(*refs):
        a_ref, b_ref = refs[0], refs[1]
        ex = refs[2:2 + n_extra]
        outs = refs[n_in:-1]
        acc = refs[-1]
        k = pl.program_id(2)

        @pl.when(k == 0)
        def _():
            acc[...] = jnp.zeros_like(acc)

        acc[...] += _dot(a_ref[...], b_ref[...], mode)

        @pl.when(k == nk - 1)
        def _():
            epilogue(acc[...], ex, outs)

    return _pcall(
        body_single if nk == 1 else body_acc, name=name, grid=grid, out_shape=out_shape,
        in_specs=[a_spec, b_spec] + list(extra_specs) + [pl.BlockSpec(memory_space=pl.ANY)] * len(after),
        out_specs=out_specs,
        scratch_shapes=[] if nk == 1 else [pltpu.VMEM(acc_shape, F32)],
        compiler_params=_cparams(semantics),
    )(a, b, *extra, *after)


def _ep_store(dtype):
    def ep(acc, ex, outs):
        outs[0][...] = acc.astype(dtype)
    return ep


def _ep_relu2(acc, ex, outs):
    outs[0][...] = acc.astype(BF16)
    r = jnp.maximum(acc, 0.0)
    outs[1][...] = (r * r).astype(BF16)


def _ep_relu2_bwd(acc, ex, outs):
    u = ex[0][...].astype(F32)
    outs[0][...] = (acc * (2.0 * jnp.maximum(u, 0.0))).astype(BF16)


def _tile(n, want):
    t = min(n, want)
    while n % t:
        t //= 2
    return t


def _mm_nn(a, w, out_dtype, name, tm=2048, tn=1024, tk=1024, epilogue=None, out_dtypes=None, after=()):
    M, K = a.shape
    N = w.shape[1]
    tm, tn, tk = _tile(M, tm), _tile(N, tn), _tile(K, tk)
    out_dtypes = out_dtypes or [out_dtype]
    return _mm(a, w, mode="nn", grid=(M // tm, N // tn, K // tk),
               a_spec=pl.BlockSpec((tm, tk), lambda i, j, k: (i, k)),
               b_spec=pl.BlockSpec((tk, tn), lambda i, j, k: (k, j)),
               out_shape=[jax.ShapeDtypeStruct((M, N), d) for d in out_dtypes],
               out_specs=[pl.BlockSpec((tm, tn), lambda i, j, k: (i, j)) for _ in out_dtypes],
               acc_shape=(tm, tn), epilogue=epilogue or _ep_store(out_dtype), name=name, after=after)


def _mm_nn_blocked(a, wg, name, epilogue, out_dtypes, tm=2048):
    M, K = a.shape
    n = wg.shape[2]
    tm = _tile(M, tm)
    return _mm(a, wg, mode="nn", grid=(M // tm, N_DEV, 1),
               a_spec=pl.BlockSpec((tm, K), lambda i, j, k: (i, 0)),
               b_spec=pl.BlockSpec((None, K, n), lambda i, j, k: (j, 0, 0)),
               out_shape=[jax.ShapeDtypeStruct((M, N_DEV * n), d) for d in out_dtypes],
               out_specs=[pl.BlockSpec((tm, n), lambda i, j, k: (i, j)) for _ in out_dtypes],
               acc_shape=(tm, n), epilogue=epilogue, name=name)


def _mm_nt(a, w, out_dtype, name, tm=2048, tn=1024, tk=1024, epilogue=None, extra=(), extra_specs=(),
           after=()):
    M, K = a.shape
    N = w.shape[0]
    tm, tn, tk = _tile(M, tm), _tile(N, tn), _tile(K, tk)
    if extra and not extra_specs:
        extra_specs = [pl.BlockSpec((tm, tn), lambda i, j, k: (i, j)) for _ in extra]
    return _mm(a, w, mode="nt", grid=(M // tm, N // tn, K // tk),
               a_spec=pl.BlockSpec((tm, tk), lambda i, j, k: (i, k)),
               b_spec=pl.BlockSpec((tn, tk), lambda i, j, k: (j, k)),
               out_shape=[jax.ShapeDtypeStruct((M, N), out_dtype)],
               out_specs=[pl.BlockSpec((tm, tn), lambda i, j, k: (i, j))],
               acc_shape=(tm, tn), epilogue=epilogue or _ep_store(out_dtype), name=name,
               extra=extra, extra_specs=extra_specs, after=after)[0]


def _mm_nt_blocked(a, wg, out_dtype, name, tm=1024, after=()):
    M = a.shape[0]
    kout, n = wg.shape[1], wg.shape[2]
    tm = _tile(M, tm)
    return _mm(a, wg, mode="nt", grid=(M // tm, 1, N_DEV),
               a_spec=pl.BlockSpec((tm, n), lambda i, j, k: (i, k)),
               b_spec=pl.BlockSpec((None, kout, n), lambda i, j, k: (k, 0, 0)),
               out_shape=[jax.ShapeDtypeStruct((M, kout), out_dtype)],
               out_specs=[pl.BlockSpec((tm, kout), lambda i, j, k: (i, 0))],
               acc_shape=(tm, kout), epilogue=_ep_store(out_dtype), name=name, after=after)[0]


def _mm_tn(a, b, out_dtype, name, tm=1024, tn=1024, tk=2048):
    K, M = a.shape
    N = b.shape[1]
    tm, tn, tk = _tile(M, tm), _tile(N, tn), _tile(K, tk)
    return _mm(a, b, mode="tn", grid=(M // tm, N // tn, K // tk),
               a_spec=pl.BlockSpec((tk, tm), lambda i, j, k: (k, i)),
               b_spec=pl.BlockSpec((tk, tn), lambda i, j, k: (k, j)),
               out_shape=[jax.ShapeDtypeStruct((M, N), out_dtype)],
               out_specs=[pl.BlockSpec((tm, tn), lambda i, j, k: (i, j))],
               acc_shape=(tm, tn), epilogue=_ep_store(out_dtype), name=name)[0]


def _mm_tn_blocked(a, b, out_dtype, name, tm=1024, tk=2048):
    K, M = a.shape
    n = b.shape[1] // N_DEV
    tm, tk = _tile(M, tm), _tile(K, tk)
    return _mm(a, b, mode="tn", grid=(M // tm, N_DEV, K // tk),
               a_spec=pl.BlockSpec((tk, tm), lambda i, j, k: (k, i)),
               b_spec=pl.BlockSpec((tk, n), lambda i, j, k: (k, j)),
               out_shape=[jax.ShapeDtypeStruct((N_DEV, M, n), out_dtype)],
               out_specs=[pl.BlockSpec((None, tm, n), lambda i, j, k: (j, i, 0))],
               acc_shape=(tm, n), epilogue=_ep_store(out_dtype), name=name)[0]


def _window_geometry(ws):
    base = [(ws * k // LANES) * LANES for k in range(N_DEV)]
    off = [ws * k - base[k] for k in range(N_DEV)]
    win = -(-(max(off) + ws) // LANES) * LANES
    return base, off, win


def _shards_to_columns(xg, base, off, win, n_out, name, tr=256):
    R, ws = xg.shape[1], xg.shape[2]
    tr = _tile(R, tr)
    nb_win = win // LANES

    def body(x_ref, o_ref, frame_ref):
        written = set()
        frame_ref[...] = jnp.zeros_like(frame_ref)
        for k in range(N_DEV):
            frame_ref[:, 0:ws] = x_ref[k].astype(F32)
            window = frame_ref[...]
            if off[k]:
                window = pltpu.roll(window, off[k], 1)
            for i in range(nb_win):
                b = base[k] // LANES + i
                if b * LANES >= n_out:
                    continue
                cols = slice(b * LANES, (b + 1) * LANES)
                blk = window[:, i * LANES:(i + 1) * LANES]
                if b in written:
                    blk = blk + o_ref[:, cols].astype(F32)
                o_ref[:, cols] = blk.astype(o_ref.dtype)
                written.add(b)
        for b in range(n_out // LANES):
            if b not in written:
                o_ref[:, b * LANES:(b + 1) * LANES] = jnp.zeros((tr, LANES), o_ref.dtype)

    return _pcall(
        body, name=name, grid=(R // tr,), out_shape=jax.ShapeDtypeStruct((R, n_out), xg.dtype),
        in_specs=[pl.BlockSpec((N_DEV, tr, ws), lambda i: (0, i, 0))],
        out_specs=pl.BlockSpec((tr, n_out), lambda i: (i, 0)),
        scratch_shapes=[pltpu.VMEM((tr, win), F32)],
        compiler_params=_cparams(("parallel",)))(xg)


def _sigmoid(x):
    return 1.0 / (1.0 + jnp.exp(-x))


def _row_spec(tm, d):
    return pl.BlockSpec((tm, d), lambda i: (i, 0))


def _vec_spec(d):
    return pl.BlockSpec((1, d), lambda i: (0, 0))


def _norm_mod_fwd(x, y, gate, nw, scale, shift, name, tm=512):
    L, D = x.shape
    tm = _tile(L, tm)
    has_res = y is not None

    def body(*refs):
        if has_res:
            x_ref, y_ref, g_ref, nw_ref, sc_ref, sh_ref, xo_ref, h_ref = refs
            xn = x_ref[...] + g_ref[...] * y_ref[...]
            xo_ref[...] = xn
        else:
            x_ref, nw_ref, sc_ref, sh_ref, h_ref = refs
            xn = x_ref[...]
        rstd = lax.rsqrt(jnp.mean(xn * xn, axis=-1, keepdims=True) + NORM_EPS)
        h = xn * rstd * nw_ref[...] * (1.0 + sc_ref[...]) + sh_ref[...]
        h_ref[...] = h.astype(BF16)

    row, vec = _row_spec(tm, D), _vec_spec(D)
    if has_res:
        ins, in_specs = (x, y, gate, nw, scale, shift), [row, row, vec, vec, vec, vec]
        out_shape = [jax.ShapeDtypeStruct((L, D), F32), jax.ShapeDtypeStruct((L, D), BF16)]
        out_specs = [row, row]
    else:
        ins, in_specs = (x, nw, scale, shift), [row, vec, vec, vec]
        out_shape = [jax.ShapeDtypeStruct((L, D), BF16)]
        out_specs = [row]
    outs = _pcall(body, name=name, grid=(L // tm,), out_shape=out_shape, in_specs=in_specs,
                  out_specs=out_specs, compiler_params=_cparams(("parallel",)))(*ins)
    return outs if has_res else (x, outs[0])


def _gated_branch_bwd(dx, branch, y_ref, g_ref, dy_ref, dg_ref):
    if branch is None:
        return
    dy_ref[...] = (g_ref[...] * dx).astype(BF16)
    dg_ref[...] += jnp.sum(dx * y_ref[...], axis=0, keepdims=True)


def _norm_mod_bwd(dh, x, nw, scale, dres, name, branch=None, tm=512):
    L, D = x.shape
    tm = _tile(L, tm)
    nb = 0 if branch is None else 2

    def body(dh_ref, x_ref, nw_ref, sc_ref, dres_ref, *rest):
        y_ref, g_ref = rest[:nb] if nb else (None, None)
        dx_ref, dsh_ref, dsc_ref, dnw_ref = rest[nb:nb + 4]
        dy_ref, dg_ref = rest[nb + 4:] if nb else (None, None)

        @pl.when(pl.program_id(0) == 0)
        def _():
            dsh_ref[...] = jnp.zeros_like(dsh_ref)
            dsc_ref[...] = jnp.zeros_like(dsc_ref)
            dnw_ref[...] = jnp.zeros_like(dnw_ref)
            if nb:
                dg_ref[...] = jnp.zeros_like(dg_ref)

        xv = x_ref[...]
        dh_v = dh_ref[...]
        nw_v = nw_ref[...]
        rstd = lax.rsqrt(jnp.mean(xv * xv, axis=-1, keepdims=True) + NORM_EPS)
        xhat = xv * rstd
        dsh_ref[...] += jnp.sum(dh_v, axis=0, keepdims=True)
        dsc_ref[...] += jnp.sum(dh_v * (xhat * nw_v), axis=0, keepdims=True)
        dr = dh_v * (1.0 + sc_ref[...])
        dnw_ref[...] += jnp.sum(dr * xhat, axis=0, keepdims=True)
        dxh = dr * nw_v
        dx = rstd * (dxh - xhat * jnp.mean(dxh * xhat, axis=-1, keepdims=True)) + dres_ref[...]
        dx_ref[...] = dx
        _gated_branch_bwd(dx, branch, y_ref, g_ref, dy_ref, dg_ref)

    row, vec = _row_spec(tm, D), _vec_spec(D)
    extra_in = [] if branch is None else list(branch)
    return _pcall(
        body, name=name, grid=(L // tm,),
        out_shape=[jax.ShapeDtypeStruct((L, D), F32)] + [jax.ShapeDtypeStruct((1, D), F32)] * 3
        + ([jax.ShapeDtypeStruct((L, D), BF16), jax.ShapeDtypeStruct((1, D), F32)] if nb else []),
        in_specs=[row, row, vec, vec, row] + ([row, vec] if nb else []),
        out_specs=[row, vec, vec, vec] + ([row, vec] if nb else []),
        compiler_params=_cparams(("arbitrary",)))(dh, x, nw, scale, dres, *extra_in)


def _final_loss(x, y, gate, fw, target, name, tm=512):
    L, D = x.shape
    tm = _tile(L, tm)

    def body(x_ref, y_ref, g_ref, fw_ref, t_ref, dx_ref, loss_ref, dfw_ref, dy_ref, dg_ref):
        @pl.when(pl.program_id(0) == 0)
        def _():
            loss_ref[...] = jnp.zeros_like(loss_ref)
            dfw_ref[...] = jnp.zeros_like(dfw_ref)
            dg_ref[...] = jnp.zeros_like(dg_ref)

        xn = x_ref[...] + g_ref[...] * y_ref[...]
        fw_v = fw_ref[...]
        rstd = lax.rsqrt(jnp.mean(xn * xn, axis=-1, keepdims=True) + NORM_EPS)
        xhat = xn * rstd
        diff = xhat * fw_v - t_ref[...]
        loss_ref[...] += jnp.sum(diff * diff, axis=0, keepdims=True)
        dyf = diff * (1.0 / D)
        dfw_ref[...] += jnp.sum(dyf * xhat, axis=0, keepdims=True)
        dxh = dyf * fw_v
        dx = rstd * (dxh - xhat * jnp.mean(dxh * xhat, axis=-1, keepdims=True))
        dx_ref[...] = dx
        _gated_branch_bwd(dx, True, y_ref, g_ref, dy_ref, dg_ref)

    row, vec = _row_spec(tm, D), _vec_spec(D)
    return _pcall(
        body, name=name, grid=(L // tm,),
        out_shape=[jax.ShapeDtypeStruct((L, D), F32), jax.ShapeDtypeStruct((1, D), F32),
                   jax.ShapeDtypeStruct((1, D), F32), jax.ShapeDtypeStruct((L, D), BF16),
                   jax.ShapeDtypeStruct((1, D), F32)],
        in_specs=[row, row, vec, vec, row], out_specs=[row, vec, vec, row, vec],
        compiler_params=_cparams(("arbitrary",)))(x, y, gate, fw, target)


def _mm_nt_norm_bwd(a, w, x, nw, scale, dres, name, branch=None, blocked=False, tm=512, tk=1024, after=()):
    M = a.shape[0]
    D = x.shape[1]
    tm = _tile(M, tm)
    nb = 0 if branch is None else 2

    def epilogue(dh_v, ex, outs):
        x_ref, nw_ref, sc_ref, dres_ref = ex[:4]
        y_ref, g_ref = ex[4:] if nb else (None, None)
        dx_ref, dsh_ref, dsc_ref, dnw_ref = outs[:4]
        dy_ref, dg_ref = outs[4:] if nb else (None, None)

        @pl.when(pl.program_id(0) == 0)
        def _():
            dsh_ref[...] = jnp.zeros_like(dsh_ref)
            dsc_ref[...] = jnp.zeros_like(dsc_ref)
            dnw_ref[...] = jnp.zeros_like(dnw_ref)
            if nb:
                dg_ref[...] = jnp.zeros_like(dg_ref)

        xv = x_ref[...]
        nw_v = nw_ref[...]
        rstd = lax.rsqrt(jnp.mean(xv * xv, axis=-1, keepdims=True) + NORM_EPS)
        xhat = xv * rstd
        dsh_ref[...] += jnp.sum(dh_v, axis=0, keepdims=True)
        dsc_ref[...] += jnp.sum(dh_v * (xhat * nw_v), axis=0, keepdims=True)
        dr = dh_v * (1.0 + sc_ref[...])
        dnw_ref[...] += jnp.sum(dr * xhat, axis=0, keepdims=True)
        dxh = dr * nw_v
        dx = rstd * (dxh - xhat * jnp.mean(dxh * xhat, axis=-1, keepdims=True)) + dres_ref[...]
        dx_ref[...] = dx
        _gated_branch_bwd(dx, branch, y_ref, g_ref, dy_ref, dg_ref)

    row = pl.BlockSpec((tm, D), lambda i, j, k: (i, 0))
    vec = pl.BlockSpec((1, D), lambda i, j, k: (0, 0))
    if blocked:
        n = w.shape[2]
        grid = (M // tm, 1, N_DEV)
        a_spec = pl.BlockSpec((tm, n), lambda i, j, k: (i, k))
        b_spec = pl.BlockSpec((None, D, n), lambda i, j, k: (k, 0, 0))
    else:
        K = a.shape[1]
        tk = _tile(K, tk)
        grid = (M // tm, 1, K // tk)
        a_spec = pl.BlockSpec((tm, tk), lambda i, j, k: (i, k))
        b_spec = pl.BlockSpec((D, tk), lambda i, j, k: (0, k))
    return _mm(a, w, mode="nt", grid=grid, a_spec=a_spec, b_spec=b_spec,
               out_shape=[jax.ShapeDtypeStruct((M, D), F32)] + [jax.ShapeDtypeStruct((1, D), F32)] * 3
               + ([jax.ShapeDtypeStruct((M, D), BF16), jax.ShapeDtypeStruct((1, D), F32)] if nb else []),
               out_specs=[row, vec, vec, vec] + ([row, vec] if nb else []),
               acc_shape=(tm, D), epilogue=epilogue, name=name,
               extra=(x, nw, scale, dres) + (tuple(branch) if nb else ()),
               extra_specs=[row, vec, vec, row] + ([row, vec] if nb else []), after=after,
               semantics=("arbitrary", "arbitrary", "arbitrary"))


def _shift_down(v, s, row):
    if s == 0:
        return v
    return jnp.where(row >= s, pltpu.roll(v, s, 0), 0.0)


def _shift_up(v, s, row):
    if s == 0:
        return v
    n = v.shape[0]
    return jnp.where(row < n - s, pltpu.roll(v, n - s, 0), 0.0)


CONV_ROWS = 32


def _shifted_rows(x_ref, r0, n, lanes=slice(None)):
    cur = x_ref[r0:r0 + CONV_ROWS, lanes]
    if r0 >= n - 1:
        return [cur] + [x_ref[r0 - s:r0 - s + CONV_ROWS, lanes] for s in range(1, n)]
    row = lax.broadcasted_iota(jnp.int32, cur.shape, 0)
    return [_shift_down(cur, s, row) for s in range(n)]


def _ssd_conv_fwd(zx, w, b, col0, width, name, cb=512):
    L = zx.shape[0]
    nb = width // cb
    off = col0 // cb

    def body(x_ref, w_ref, b_ref, o_ref):
        for l0 in range(0, cb, LANES):
            lanes = slice(l0, l0 + LANES)
            taps = [w_ref[k:k + 1, lanes] for k in range(SSD_K)]
            bias = b_ref[:, lanes]
            for r0 in range(0, L, CONV_ROWS):
                taps_in = _shifted_rows(x_ref, r0, SSD_K, lanes)
                acc = bias + taps[SSD_K - 1] * taps_in[0]
                for s in range(1, SSD_K):
                    acc = acc + taps[SSD_K - 1 - s] * taps_in[s]
                o_ref[r0:r0 + CONV_ROWS, lanes] = acc * _sigmoid(acc)

    return _pcall(
        body, name=name, grid=(nb,), out_shape=jax.ShapeDtypeStruct((L, width), F32),
        in_specs=[pl.BlockSpec((L, cb), lambda j: (0, off + j)),
                  pl.BlockSpec((SSD_K, cb), lambda j: (0, j)),
                  pl.BlockSpec((1, cb), lambda j: (0, j))],
        out_specs=pl.BlockSpec((L, cb), lambda j: (0, j)),
        compiler_params=_cparams(("parallel",)))(zx, w, b)


def _ssd_conv_bwd(zx, w, b, d_parts, dzx, col0, name, cb=128):
    L = zx.shape[0]
    widths = [p.shape[1] for p in d_parts]
    width = sum(widths)
    nb = width // cb
    off = col0 // cb
    starts = [sum(widths[:i]) // cb for i in range(len(d_parts))]
    counts = [wd // cb for wd in widths]

    def body(x_ref, w_ref, b_ref, *rest):
        d_refs = rest[:len(d_parts)]
        dx_ref, dw_ref, db_ref, dpre_ref = rest[len(d_parts) + 1:]
        j = pl.program_id(0)
        taps = [w_ref[k:k + 1, :] for k in range(SSD_K)]
        bias = b_ref[...]
        fold = lambda v: sum(v[r:r + 8, :] for r in range(0, CONV_ROWS, 8))
        db8 = jnp.zeros((8, cb), F32)
        dw8 = [jnp.zeros((8, cb), F32) for _ in range(SSD_K)]
        for r0 in range(0, L, CONV_ROWS):
            rows = slice(r0, r0 + CONV_ROWS)
            d_val = d_refs[-1][rows, :]
            for i in range(len(d_parts) - 2, -1, -1):
                d_val = jnp.where(j < starts[i + 1], d_refs[i][rows, :], d_val)
            taps_in = _shifted_rows(x_ref, r0, SSD_K)
            acc = bias + taps[SSD_K - 1] * taps_in[0]
            for s in range(1, SSD_K):
                acc = acc + taps[SSD_K - 1 - s] * taps_in[s]
            sig = _sigmoid(acc)
            dpre = d_val * (sig * (1.0 + acc * (1.0 - sig)))
            dpre_ref[rows, :] = dpre
            db8 = db8 + fold(dpre)
            for s in range(SSD_K):
                dw8[s] = dw8[s] + fold(dpre * taps_in[s])
        dpre_ref[L:L + 8, :] = jnp.zeros((8, cb), F32)
        db_ref[...] = jnp.sum(db8, axis=0, keepdims=True)
        for s in range(SSD_K):
            dw_ref[SSD_K - 1 - s:SSD_K - s, :] = jnp.sum(dw8[s], axis=0, keepdims=True)
        for r0 in range(0, L, CONV_ROWS):
            dx = taps[SSD_K - 1] * dpre_ref[r0:r0 + CONV_ROWS, :]
            for s in range(1, SSD_K):
                dx = dx + taps[SSD_K - 1 - s] * dpre_ref[r0 + s:r0 + s + CONV_ROWS, :]
            dx_ref[r0:r0 + CONV_ROWS, :] = dx.astype(BF16)

    def part_spec(i):
        return pl.BlockSpec((L, cb), lambda j: (0, jnp.clip(j - starts[i], 0, counts[i] - 1)))

    return _pcall(
        body, name=name, grid=(nb,),
        out_shape=[jax.ShapeDtypeStruct(dzx.shape, BF16), jax.ShapeDtypeStruct((SSD_K, width), F32),
                   jax.ShapeDtypeStruct((1, width), F32)],
        in_specs=[pl.BlockSpec((L, cb), lambda j: (0, off + j)),
                  pl.BlockSpec((SSD_K, cb), lambda j: (0, j)),
                  pl.BlockSpec((1, cb), lambda j: (0, j))]
        + [part_spec(i) for i in range(len(d_parts))] + [pl.BlockSpec(memory_space=pl.ANY)],
        out_specs=[pl.BlockSpec((L, cb), lambda j: (0, off + j)),
                   pl.BlockSpec((SSD_K, cb), lambda j: (0, j)),
                   pl.BlockSpec((1, cb), lambda j: (0, j))],
        input_output_aliases={3 + len(d_parts): 0},
        scratch_shapes=[pltpu.VMEM((L + 8, cb), F32)],
        compiler_params=_cparams(("parallel",)))(zx, w, b, *d_parts, dzx)


def _dzx_finish(dzx, ddt, col0, name, tl=512):
    G, L, _ = ddt.shape
    tail = dzx.shape[1] - col0
    tl = _tile(L, tl)

    def body(ddt_ref, dzx_ref, o_ref):
        s = ddt_ref[0]
        for g in range(1, G):
            s = s + ddt_ref[g]
        o_ref[:, 0:LANES] = s.astype(o_ref.dtype)
        if tail > LANES:
            o_ref[:, LANES:] = jnp.zeros((tl, tail - LANES), o_ref.dtype)

    return _pcall(
        body, name=name, grid=(L // tl,), out_shape=jax.ShapeDtypeStruct(dzx.shape, dzx.dtype),
        in_specs=[pl.BlockSpec((G, tl, LANES), lambda i: (0, i, 0)), pl.BlockSpec(memory_space=pl.ANY)],
        out_specs=pl.BlockSpec((tl, tail), lambda i: (i, col0 // tail)),
        input_output_aliases={1: 0},
        compiler_params=_cparams(("parallel",)))(ddt, dzx)


def _sc_conv_fwd(proj, w, name, cb=512):
    L = proj.shape[0]
    width = proj.shape[1] // 3
    nb = width // cb

    def body(b_ref, c_ref, x_ref, w_ref, o_ref):
        for l0 in range(0, cb, LANES):
            lanes = slice(l0, l0 + LANES)
            taps = [w_ref[k:k + 1, lanes] for k in range(SC_K)]
            for r0 in range(0, L, CONV_ROWS):
                rows = slice(r0, r0 + CONV_ROWS)
                q = [c * x for c, x in zip(_shifted_rows(c_ref, r0, SC_K, lanes),
                                           _shifted_rows(x_ref, r0, SC_K, lanes))]
                acc = taps[SC_K - 1] * q[0]
                for s in range(1, SC_K):
                    acc = acc + taps[SC_K - 1 - s] * q[s]
                o_ref[rows, lanes] = (b_ref[rows, lanes] * acc).astype(BF16)

    return _pcall(
        body, name=name, grid=(nb,), out_shape=jax.ShapeDtypeStruct((L, width), BF16),
        in_specs=[pl.BlockSpec((L, cb), lambda j: (0, j)),
                  pl.BlockSpec((L, cb), lambda j: (0, nb + j)),
                  pl.BlockSpec((L, cb), lambda j: (0, 2 * nb + j)),
                  pl.BlockSpec((SC_K, cb), lambda j: (0, j))],
        out_specs=pl.BlockSpec((L, cb), lambda j: (0, j)),
        compiler_params=_cparams(("parallel",)))(proj, proj, proj, w)


def _sc_conv_bwd(proj, w, dy, name, cb=128):
    L = proj.shape[0]
    width = proj.shape[1] // 3
    nb = width // cb

    def body(b_ref, c_ref, x_ref, w_ref, dy_ref, db_ref, dc_ref, dxv_ref, dw_ref, dconv_ref):
        taps = [w_ref[k:k + 1, :] for k in range(SC_K)]
        fold = lambda v: sum(v[r:r + 8, :] for r in range(0, CONV_ROWS, 8))
        dw8 = [jnp.zeros((8, cb), F32) for _ in range(SC_K)]
        for r0 in range(0, L, CONV_ROWS):
            rows = slice(r0, r0 + CONV_ROWS)
            q = [c * x for c, x in zip(_shifted_rows(c_ref, r0, SC_K), _shifted_rows(x_ref, r0, SC_K))]
            conv = taps[SC_K - 1] * q[0]
            for s in range(1, SC_K):
                conv = conv + taps[SC_K - 1 - s] * q[s]
            dyv = dy_ref[rows, :]
            db_ref[rows, :] = (dyv * conv).astype(BF16)
            dconv = dyv * b_ref[rows, :]
            dconv_ref[rows, :] = dconv
            for s in range(SC_K):
                dw8[s] = dw8[s] + fold(dconv * q[s])
        dconv_ref[L:L + 8, :] = jnp.zeros((8, cb), F32)
        for s in range(SC_K):
            dw_ref[SC_K - 1 - s:SC_K - s, :] = jnp.sum(dw8[s], axis=0, keepdims=True)
        for r0 in range(0, L, CONV_ROWS):
            rows = slice(r0, r0 + CONV_ROWS)
            dq = taps[SC_K - 1] * dconv_ref[rows, :]
            for s in range(1, SC_K):
                dq = dq + taps[SC_K - 1 - s] * dconv_ref[r0 + s:r0 + s + CONV_ROWS, :]
            dc_ref[rows, :] = (dq * x_ref[rows, :]).astype(BF16)
            dxv_ref[rows, :] = (dq * c_ref[rows, :]).astype(BF16)

    blk = pl.BlockSpec((L, cb), lambda j: (0, j))
    wblk = pl.BlockSpec((SC_K, cb), lambda j: (0, j))
    return _pcall(
        body, name=name, grid=(nb,),
        out_shape=[jax.ShapeDtypeStruct((L, width), BF16)] * 3 + [jax.ShapeDtypeStruct((SC_K, width), F32)],
        in_specs=[blk, pl.BlockSpec((L, cb), lambda j: (0, nb + j)),
                  pl.BlockSpec((L, cb), lambda j: (0, 2 * nb + j)), wblk, blk],
        out_specs=[blk, blk, blk, wblk], scratch_shapes=[pltpu.VMEM((L + 8, cb), F32)],
        compiler_params=_cparams(("parallel",)))(proj, proj, proj, w, dy)


def _split3(v):
    hi = v.astype(BF16)
    r1 = v - hi.astype(F32)
    mid = r1.astype(BF16)
    lo = (r1 - mid.astype(F32)).astype(BF16)
    return hi, mid, lo


def _dot_exact01(t01, v):
    hi, mid, lo = _split3(v)
    return _dot(t01, hi) + _dot(t01, mid) + _dot(t01, lo)


def _lane_col(v, lane, h):
    return jnp.sum(jnp.where(lane == h, v, 0.0), axis=1, keepdims=True)


def _sum_all(v):
    return jnp.sum(jnp.sum(v, axis=1, keepdims=True), axis=0, keepdims=True)


def _softplus(x):
    return jnp.maximum(x, 0.0) + jnp.log1p(jnp.exp(-jnp.abs(x)))


def _ssd_decay(zx, bias_p, alog_p, n_heads, dt_block, name):
    L = zx.shape[0]
    nc = L // SSD_CHUNK

    def body(raw_ref, bias_ref, alog_ref, dt_ref, sg_ref, cs_ref, cst_ref, last_ref):
        lane = lax.broadcasted_iota(jnp.int32, (SSD_CHUNK, LANES), 1)
        row = lax.broadcasted_iota(jnp.int32, (SSD_CHUNK, LANES), 0)
        valid = lane < n_heads
        raw = raw_ref[...] + bias_ref[...]
        dt = jnp.where(valid, _softplus(raw), 0.0)
        a = dt * (-jnp.exp(alog_ref[...]))
        cs = _dot_exact01((row >= lane).astype(BF16), a)
        dt_ref[...] = dt
        sg_ref[...] = _sigmoid(raw)
        cs_ref[...] = cs
        cst_ref[...] = cs.T
        last_ref[...] = jnp.sum(a, axis=0, keepdims=True)

    blk = pl.BlockSpec((SSD_CHUNK, LANES), lambda c: (c, 0))
    head_vec = pl.BlockSpec((1, LANES), lambda c: (0, 0))
    return _pcall(
        body, name=name, grid=(nc,),
        out_shape=[jax.ShapeDtypeStruct((L, LANES), F32)] * 3
        + [jax.ShapeDtypeStruct((nc, SSD_CHUNK, LANES), F32), jax.ShapeDtypeStruct((nc, 1, LANES), F32)],
        in_specs=[pl.BlockSpec((SSD_CHUNK, LANES), lambda c: (c, dt_block)), head_vec, head_vec],
        out_specs=[blk, blk, blk, pl.BlockSpec((None, SSD_CHUNK, LANES), lambda c: (c, 0, 0)),
                   pl.BlockSpec((None, 1, LANES), lambda c: (c, 0, 0))],
        compiler_params=_cparams(("parallel",)))(zx, bias_p, alog_p)


def _ssd_common(dt_ref, cs_ref, last_ref, b_ref, c_ref):
    c_sz = SSD_CHUNK
    lane = lax.broadcasted_iota(jnp.int32, (c_sz, LANES), 1)
    row = lax.broadcasted_iota(jnp.int32, (c_sz, LANES), 0)
    bb = b_ref[...].astype(BF16)
    cb = c_ref[...].astype(BF16)
    scores = _dot(cb, bb, "nt")
    return dict(lane=lane, row=row, dt=dt_ref[...], cs=cs_ref[...], last_row=last_ref[...], bb=bb, cb=cb,
                scores=scores, causal=row >= lane, lo=lane < SSD_P)


def _pair_terms(q, cst_ref, h0):
    lane, lo = q["lane"], q["lo"]
    out = {}
    cols, dts, lasts, lms = [], [], [], []
    lane1 = lax.broadcasted_iota(jnp.int32, (1, LANES), 1)
    for h in (h0, h0 + 1):
        col = _lane_col(q["cs"], lane, h)
        rowv = cst_ref[pl.ds(h, 1), :]
        lms.append(jnp.exp(jnp.where(q["causal"], col - rowv, -1e30)))
        cols.append(col)
        dts.append(_lane_col(q["dt"], lane, h))
        lasts.append(jnp.sum(jnp.where(lane1 == h, q["last_row"], 0.0), axis=1, keepdims=True))
    out["lm"] = lms
    out["cols"] = cols
    out["lasts"] = lasts
    out["dt_b"] = jnp.where(lo, dts[0], dts[1])
    out["e_b"] = jnp.where(lo, jnp.exp(cols[0]), jnp.exp(cols[1]))
    out["dec_cols"] = [jnp.exp(lasts[0] - cols[0]), jnp.exp(lasts[1] - cols[1])]
    out["dec_b"] = jnp.where(lo, out["dec_cols"][0], out["dec_cols"][1])
    lo1 = lane1 < SSD_P
    out["explast"] = [jnp.exp(lasts[0]), jnp.exp(lasts[1])]
    out["explast_b"] = jnp.where(lo1, out["explast"][0], out["explast"][1])
    return out


def _ssd_fwd(zx, xc, decay, d_lane, nw, d_inner, after, name):
    L = zx.shape[0]
    nc = L // SSD_CHUNK
    gw = d_inner // SSD_G
    heads = gw // SSD_P
    n_pair = heads // 2
    bc0 = d_inner // LANES

    def body(z_ref, xs_ref, b_ref, c_ref, dt_ref, cs_ref, cst_ref, last_ref, dl_ref, nw_ref, after_ref,
             y_ref, yn_ref, prev_ref, s_ref):
        @pl.when(pl.program_id(1) == 0)
        def _():
            s_ref[...] = jnp.zeros_like(s_ref)

        q = _ssd_common(dt_ref, cs_ref, last_ref, b_ref, c_ref)
        prev_ref[...] = s_ref[...]
        lo = q["lo"]
        for j in range(n_pair):
            sl = slice(j * LANES, (j + 1) * LANES)
            p = _pair_terms(q, cst_ref, pl.program_id(0) * heads + 2 * j)
            xs_p = xs_ref[:, sl]
            xp = xs_p * p["dt_b"]
            xb = xp.astype(BF16)
            m_a = (q["scores"] * p["lm"][0]).astype(BF16)
            m_b = (q["scores"] * p["lm"][1]).astype(BF16)
            yd = jnp.where(lo, _dot(m_a, xb), _dot(m_b, xb))
            s_p = s_ref[:, sl]
            yo = _dot(q["cb"], s_p.astype(BF16)) * p["e_b"]
            y_ref[:, sl] = yd + yo + dl_ref[:, sl] * xs_p
            st = _dot(q["bb"], (xp * p["dec_b"]).astype(BF16), "tn")
            s_ref[:, sl] = s_p * p["explast_b"] + st
        yv = y_ref[...]
        zv = z_ref[...]
        yg = yv * (zv * _sigmoid(zv))
        rstd = lax.rsqrt(jnp.mean(yg * yg, axis=-1, keepdims=True) + NORM_EPS)
        yn_ref[...] = (yg * rstd * nw_ref[...]).astype(BF16)

    grp = lambda width: pl.BlockSpec((None, 1, width), lambda g, c: (g, 0, 0))
    dt_, _, cs_, cst_, last_ = decay
    return _pcall(
        body, name=name, grid=(SSD_G, nc),
        out_shape=[jax.ShapeDtypeStruct((L, d_inner), F32), jax.ShapeDtypeStruct((L, d_inner), BF16),
                   jax.ShapeDtypeStruct((nc, SSD_G, SSD_N, gw), F32)],
        in_specs=[pl.BlockSpec((SSD_CHUNK, gw), lambda g, c: (c, g)),
                  pl.BlockSpec((SSD_CHUNK, gw), lambda g, c: (c, g)),
                  pl.BlockSpec((SSD_CHUNK, SSD_N), lambda g, c: (c, bc0 + g)),
                  pl.BlockSpec((SSD_CHUNK, SSD_N), lambda g, c: (c, bc0 + SSD_G + g)),
                  pl.BlockSpec((SSD_CHUNK, LANES), lambda g, c: (c, 0)),
                  pl.BlockSpec((SSD_CHUNK, LANES), lambda g, c: (c, 0)),
                  pl.BlockSpec((None, SSD_CHUNK, LANES), lambda g, c: (c, 0, 0)),
                  pl.BlockSpec((None, 1, LANES), lambda g, c: (c, 0, 0)),
                  grp(gw), grp(gw), pl.BlockSpec(memory_space=pl.ANY)],
        out_specs=[pl.BlockSpec((SSD_CHUNK, gw), lambda g, c: (c, g)),
                   pl.BlockSpec((SSD_CHUNK, gw), lambda g, c: (c, g)),
                   pl.BlockSpec((None, None, SSD_N, gw), lambda g, c: (c, g, 0, 0))],
        scratch_shapes=[pltpu.VMEM((SSD_N, gw), F32)],
        compiler_params=_cparams(("parallel", "arbitrary")))(
            zx, xc, xc, xc, dt_, cs_, cst_, last_, d_lane, nw, after)


def _ssd_bwd(dyn, y, zx, xc, prev, decay, alog_p, d_lane, nw, d_inner, name):
    L = zx.shape[0]
    nc = L // SSD_CHUNK
    gw = d_inner // SSD_G
    heads = gw // SSD_P
    n_pair = heads // 2
    bc0 = d_inner // LANES

    def body(dyn_ref, y_ref, z_ref, xs_ref, b_ref, c_ref, prev_ref, dt_ref, sg_ref, cs_ref, cst_ref, last_ref,
             alog_ref, dl_ref, nw_ref,
             dz_ref, dxs_ref, db_ref, dc_ref, ddt_ref, dbias_ref, dalog_ref, dd_ref, dnw_ref,
             ds_ref, racc_ref):
        @pl.when(pl.program_id(1) == 0)
        def _():
            ds_ref[...] = jnp.zeros_like(ds_ref)
            dbias_ref[...] = jnp.zeros_like(dbias_ref)
            dalog_ref[...] = jnp.zeros_like(dalog_ref)
            dd_ref[...] = jnp.zeros_like(dd_ref)
            dnw_ref[...] = jnp.zeros_like(dnw_ref)

        q = _ssd_common(dt_ref, cs_ref, last_ref, b_ref, c_ref)
        a_row = -jnp.exp(alog_ref[...])
        lane, row, lo = q["lane"], q["row"], q["lo"]
        lane1 = lax.broadcasted_iota(jnp.int32, (1, LANES), 1)
        head0 = pl.program_id(0) * heads
        mine = (lane >= head0) & (lane < head0 + heads)

        yv, zv, dynv, nwv = y_ref[...], z_ref[...], dyn_ref[...], nw_ref[...]
        sig = _sigmoid(zv)
        sz = zv * sig
        yg = yv * sz
        rstd = lax.rsqrt(jnp.mean(yg * yg, axis=-1, keepdims=True) + NORM_EPS)
        yhat = yg * rstd
        dnw_ref[...] += jnp.sum(dynv * yhat, axis=0, keepdims=True)
        dyh = dynv * nwv
        dyg = rstd * (dyh - yhat * jnp.mean(dyh * yhat, axis=-1, keepdims=True))
        dz_ref[...] = (dyg * yv * (sig * (1.0 + zv * (1.0 - sig)))).astype(BF16)
        dy_all = dyg * sz

        dg = jnp.zeros((SSD_CHUNK, SSD_CHUNK), F32)
        dc_acc = jnp.zeros((SSD_CHUNK, SSD_N), F32)
        db_acc = jnp.zeros((SSD_CHUNK, SSD_N), F32)
        dcs_mat = jnp.zeros((SSD_CHUNK, LANES), F32)
        ddt_mat = jnp.zeros((SSD_CHUNK, LANES), F32)
        dd_row = jnp.zeros((1, LANES), F32)
        racc_ref[...] = jnp.zeros_like(racc_ref)
        is_last = row == SSD_CHUNK - 1

        for j in range(n_pair):
            sl = slice(j * LANES, (j + 1) * LANES)
            ha, hb = head0 + 2 * j, head0 + 2 * j + 1
            p = _pair_terms(q, cst_ref, ha)
            xs_p = xs_ref[:, sl]
            dyp = dy_all[:, sl]
            xp = xs_p * p["dt_b"]
            xb = xp.astype(BF16)
            s_p = prev_ref[:, sl]
            s_pb = s_p.astype(BF16)
            dsn = ds_ref[:, sl]
            dsnb = dsn.astype(BF16)
            m_f = [q["scores"] * p["lm"][0], q["scores"] * p["lm"][1]]

            t0 = dyp * xs_p
            dd_row = dd_row + jnp.where(lane1 == ha, _sum_all(jnp.where(lo, t0, 0.0)), 0.0) \
                + jnp.where(lane1 == hb, _sum_all(jnp.where(lo, 0.0, t0)), 0.0)
            dxs_p = dl_ref[:, sl] * dyp

            yo = _dot(q["cb"], s_pb) * p["e_b"]
            dcs_b = (dyp * p["e_b"]).astype(BF16)
            dc_acc = dc_acc + _dot(dcs_b, s_pb, "nt")
            ds_yo = _dot(q["cb"], dcs_b, "tn")
            t1 = dyp * yo
            dcs_cols = [jnp.sum(jnp.where(lo, t1, 0.0), axis=1, keepdims=True),
                        jnp.sum(jnp.where(lo, 0.0, t1), axis=1, keepdims=True)]

            t2 = dsn * s_p
            dlast = [p["explast"][0] * _sum_all(jnp.where(lo, t2, 0.0)),
                     p["explast"][1] * _sum_all(jnp.where(lo, 0.0, t2))]
            ds_ref[:, sl] = dsn * p["explast_b"] + ds_yo
            w = _dot(q["bb"], dsnb)
            db_acc = db_acc + _dot((xp * p["dec_b"]).astype(BF16), dsnb, "nt")
            dxp = w * p["dec_b"]
            t3 = w * xp
            e = [jnp.sum(jnp.where(lo, t3, 0.0), axis=1, keepdims=True) * p["dec_cols"][0],
                 jnp.sum(jnp.where(lo, 0.0, t3), axis=1, keepdims=True) * p["dec_cols"][1]]
            for i in range(2):
                dlast[i] = dlast[i] + jnp.sum(e[i], axis=0, keepdims=True)
                dcs_cols[i] = dcs_cols[i] - e[i]

            dyb = dyp.astype(BF16)
            dy_h = [jnp.where(lo, dyp, 0.0).astype(BF16), jnp.where(lo, 0.0, dyp).astype(BF16)]
            dms = [_dot(dy_h[0], xb, "nt"), _dot(dy_h[1], xb, "nt")]
            dxp = dxp + jnp.where(lo, _dot(m_f[0].astype(BF16), dyb, "tn"), _dot(m_f[1].astype(BF16), dyb, "tn"))
            for i, h in enumerate((ha, hb)):
                dg = dg + dms[i] * p["lm"][i]
                qm = dms[i] * m_f[i]
                dcs_cols[i] = dcs_cols[i] + jnp.sum(qm, axis=1, keepdims=True)
                racc_ref[pl.ds(h, 1), :] = jnp.sum(qm, axis=0, keepdims=True)

            dxs_ref[:, sl] = dxs_p + dxp * p["dt_b"]
            t4 = dxp * xs_p
            ddt_cols = [jnp.sum(jnp.where(lo, t4, 0.0), axis=1, keepdims=True),
                        jnp.sum(jnp.where(lo, 0.0, t4), axis=1, keepdims=True)]
            for i, h in enumerate((ha, hb)):
                sel = lane == h
                dcs_mat = dcs_mat + jnp.where(sel, dcs_cols[i], 0.0) + jnp.where(sel & is_last, dlast[i], 0.0)
                ddt_mat = ddt_mat + jnp.where(sel, ddt_cols[i], 0.0)

        dcs_mat = dcs_mat - racc_ref[...].T
        tri_t = (row <= lane).astype(BF16)
        da = _dot_exact01(tri_t, dcs_mat)
        ddt = ddt_mat + da * a_row
        dalog_ref[...] += jnp.sum(jnp.where(mine, da * q["dt"], 0.0), axis=0, keepdims=True) * a_row
        draw = jnp.where(mine, ddt * sg_ref[...], 0.0)
        ddt_ref[...] = draw
        dbias_ref[...] += jnp.sum(draw, axis=0, keepdims=True)
        dd_ref[...] += dd_row
        dgb = dg.astype(BF16)
        dc_ref[...] = dc_acc + _dot(dgb, q["bb"])
        db_ref[...] = db_acc + _dot(dgb, q["cb"], "tn")

    rev = lambda c: nc - 1 - c
    grp = lambda width: pl.BlockSpec((None, 1, width), lambda g, c: (g, 0, 0))
    blk = lambda width, off: pl.BlockSpec((SSD_CHUNK, width), lambda g, c: (rev(c), off + g))
    head_vec = pl.BlockSpec((1, LANES), lambda g, c: (0, 0))
    chunk_rows = pl.BlockSpec((SSD_CHUNK, LANES), lambda g, c: (rev(c), 0))
    dt_, sg_, cs_, cst_, last_ = decay
    return _pcall(
        body, name=name, grid=(SSD_G, nc),
        out_shape=[jax.ShapeDtypeStruct(zx.shape, BF16), jax.ShapeDtypeStruct((L, d_inner), F32),
                   jax.ShapeDtypeStruct((L, SSD_G * SSD_N), F32), jax.ShapeDtypeStruct((L, SSD_G * SSD_N), F32),
                   jax.ShapeDtypeStruct((SSD_G, L, LANES), F32),
                   jax.ShapeDtypeStruct((SSD_G, 1, LANES), F32), jax.ShapeDtypeStruct((SSD_G, 1, LANES), F32),
                   jax.ShapeDtypeStruct((SSD_G, 1, LANES), F32), jax.ShapeDtypeStruct((SSD_G, 1, gw), F32)],
        in_specs=[blk(gw, 0), blk(gw, 0), blk(gw, 0), blk(gw, 0), blk(SSD_N, bc0), blk(SSD_N, bc0 + SSD_G),
                  pl.BlockSpec((None, None, SSD_N, gw), lambda g, c: (rev(c), g, 0, 0)),
                  chunk_rows, chunk_rows, chunk_rows,
                  pl.BlockSpec((None, SSD_CHUNK, LANES), lambda g, c: (rev(c), 0, 0)),
                  pl.BlockSpec((None, 1, LANES), lambda g, c: (rev(c), 0, 0)),
                  head_vec, grp(gw), grp(gw)],
        out_specs=[blk(gw, 0), blk(gw, 0), blk(SSD_N, 0), blk(SSD_N, 0),
                   pl.BlockSpec((None, SSD_CHUNK, LANES), lambda g, c: (g, rev(c), 0)),
                   grp(LANES), grp(LANES), grp(LANES), grp(gw)],
        scratch_shapes=[pltpu.VMEM((SSD_N, gw), F32), pltpu.VMEM((SSD_CHUNK, LANES), F32)],
        compiler_params=_cparams(("parallel", "arbitrary")))(
            dyn, y, zx, xc, xc, xc, prev, dt_, sg_, cs_, cst_, last_, alog_p, d_lane, nw)


def _cond_mod(c_pad, ada_w, ada_b_loc, after, name):
    depth, D, n = ada_w.shape
    rows = c_pad.shape[0]

    def body(c_ref, w_ref, b_ref, after_ref, mod_ref, cond_ref):
        cv = c_ref[...]
        cond = cv * _sigmoid(cv)
        cond_ref[...] = cond
        mod_ref[...] = _dot(cond.astype(BF16), w_ref[...].astype(BF16)) + b_ref[...]

    return _pcall(
        body, name=name, grid=(depth,),
        out_shape=[jax.ShapeDtypeStruct((depth, rows, n), F32), jax.ShapeDtypeStruct((rows, D), F32)],
        in_specs=[pl.BlockSpec((rows, D), lambda i: (0, 0)),
                  pl.BlockSpec((None, D, n), lambda i: (i, 0, 0)),
                  pl.BlockSpec((None, 1, n), lambda i: (i, 0, 0)),
                  pl.BlockSpec(memory_space=pl.ANY)],
        out_specs=[pl.BlockSpec((None, rows, n), lambda i: (i, 0, 0)),
                   pl.BlockSpec((rows, D), lambda i: (0, 0))],
        compiler_params=_cparams(("arbitrary",)))(c_pad, ada_w, ada_b_loc, after)


def _adamw_math(g, w, m, v):
    m_new = ADAM_B1 * m + (1.0 - ADAM_B1) * g
    v_new = ADAM_B2 * v + (1.0 - ADAM_B2) * (g * g)
    m_hat = m_new / (1.0 - ADAM_B1 ** ADAM_STEP)
    v_hat = v_new / (1.0 - ADAM_B2 ** ADAM_STEP)
    delta = -ADAM_LR * (m_hat / (jnp.sqrt(v_hat) + ADAM_EPS) + ADAM_WD * w)
    return delta, m_new, v_new


def _adamw_sum(parts, w, m, v, layer, name, prev=None, tr=None, window_off=None):
    depth, R, C = w.shape
    tr = _tile(R, tr if tr is not None else (512 if C <= 512 else 256))
    win = parts.shape[2]
    scratch = [] if window_off is None else [pltpu.VMEM((tr, win), F32)]

    def body(p_ref, w_ref, m_ref, v_ref, *rest):
        g_ref, d_ref, mo_ref, vo_ref = rest[-4 - len(scratch):len(rest) - len(scratch)]
        g = p_ref[0].astype(F32)
        for k in range(1, N_DEV):
            g = g + p_ref[k].astype(F32)
        if window_off is not None:
            me = _my_index()
            off = 0
            for k in range(N_DEV):
                off = jnp.where(me == k, window_off[k], off)
            src = lax.broadcasted_iota(jnp.int32, (win, win), 0)
            dst = lax.broadcasted_iota(jnp.int32, (win, win), 1)
            shift = ((src == dst + off) & (dst < C)).astype(BF16)
            hi, mid, lo = _split3(g)
            rest[-1][...] = _dot(hi, shift) + _dot(mid, shift) + _dot(lo, shift)
            g = rest[-1][:, 0:C]
        d, mn, vn = _adamw_math(g, w_ref[...], m_ref[...], v_ref[...])
        g_ref[...] = g
        d_ref[...] = d
        mo_ref[...] = mn
        vo_ref[...] = vn

    blk = pl.BlockSpec((None, tr, C), lambda i: (layer, i, 0))
    prev = list(prev) if prev is not None else []
    return _pcall(
        body, name=name, grid=(R // tr,),
        out_shape=[jax.ShapeDtypeStruct((depth, R, C), F32)] * 4,
        in_specs=[pl.BlockSpec((N_DEV, tr, win), lambda i: (0, i, 0)), blk, blk, blk]
        + [pl.BlockSpec(memory_space=pl.ANY)] * len(prev),
        out_specs=[blk] * 4, input_output_aliases={4 + k: k for k in range(len(prev))},
        scratch_shapes=scratch,
        compiler_params=_cparams(("parallel",)))(parts, w, m, v, *prev)


def _adamw_small(parts, wmv, head_parts, head_wmv, loss_parts, name):
    n, nh = len(parts), len(head_parts)
    n_heads = head_wmv[0][0].shape[1] if nh else 0
    groups = head_parts[0].shape[1] if nh else 0
    d_model = loss_parts.shape[2]

    def body(*refs):
        p_refs, refs = refs[:n], refs[n:]
        wmv_refs, refs = refs[:3 * n], refs[3 * n:]
        hp_refs, refs = refs[:nh], refs[nh:]
        hwmv_refs, refs = refs[:3 * nh], refs[3 * nh:]
        loss_ref, refs = refs[0], refs[1:]
        outs, loss_out, head_scr = refs[:4 * (n + nh)], refs[4 * (n + nh)], refs[4 * (n + nh) + 1]

        def update(i, g, w_ref, m_ref, v_ref):
            res = (g,) + _adamw_math(g, w_ref[...], m_ref[...], v_ref[...])
            for o_ref, r in zip(outs[4 * i:4 * i + 4], res):
                o_ref[...] = r

        for i in range(n):
            g = p_refs[i][0]
            for k in range(1, N_DEV):
                g = g + p_refs[i][k]
            update(i, g, *wmv_refs[3 * i:3 * i + 3])
        for i in range(nh):
            g = None
            for k in range(N_DEV):
                for grp in range(groups):
                    g = hp_refs[i][k, grp] if g is None else g + hp_refs[i][k, grp]
            head_scr[...] = g
            update(n + i, head_scr[:, 0:n_heads], *hwmv_refs[3 * i:3 * i + 3])
        tot = loss_ref[0]
        for k in range(1, N_DEV):
            tot = tot + loss_ref[k]
        loss_out[...] = jnp.broadcast_to(_sum_all(tot) * (0.5 / d_model), loss_out.shape)

    operands = list(parts) + [a for t in wmv for a in t] + list(head_parts) + [a for t in head_wmv for a in t]
    operands.append(loss_parts)
    out_shape = [jax.ShapeDtypeStruct(t[0].shape, F32) for t in list(wmv) + list(head_wmv) for _ in range(4)]
    out_shape.append(jax.ShapeDtypeStruct((1, LANES), F32))
    vmem = pl.BlockSpec(memory_space=pltpu.VMEM)
    outs = _pcall(body, name=name, out_shape=out_shape, in_specs=[vmem] * len(operands),
                  out_specs=[vmem] * len(out_shape), scratch_shapes=[pltpu.VMEM((1, LANES), F32)],
                  compiler_params=_cparams())(*operands)
    return [outs[4 * i:4 * i + 4] for i in range(n + nh)], outs[-1]


def _ada_adamw(cond_pad, dmod_pad, w, m, v, name, tr=512):
    depth, D, n = w.shape
    rows = cond_pad.shape[0]
    tr = _tile(D, tr)

    def body(c_ref, dm_ref, w_ref, m_ref, v_ref, g_ref, d_ref, mo_ref, vo_ref):
        g = _dot(c_ref[...].astype(BF16), dm_ref[...].astype(BF16), "tn")
        d, mn, vn = _adamw_math(g, w_ref[...], m_ref[...], v_ref[...])
        g_ref[...] = g
        d_ref[...] = d
        mo_ref[...] = mn
        vo_ref[...] = vn

    blk = pl.BlockSpec((None, tr, n), lambda i, r: (i, r, 0))
    return _pcall(
        body, name=name, grid=(depth, D // tr),
        out_shape=[jax.ShapeDtypeStruct((depth, D, n), F32)] * 4,
        in_specs=[pl.BlockSpec((rows, tr), lambda i, r: (0, r)),
                  pl.BlockSpec((None, rows, n), lambda i, r: (i, 0, 0)), blk, blk, blk],
        out_specs=[blk] * 4, compiler_params=_cparams(("parallel", "parallel")))(cond_pad, dmod_pad, w, m, v)


def kernel(x, c, ada_w, ada_b, mix_norm_w, mlp_norm_w, mlp_up, mlp_down, ssd_in_w, ssd_conv_w, ssd_conv_b, ssd_dt_bias, ssd_A_log, ssd_D, ssd_norm_w, ssd_out_w, sc_in_w, sc_conv_w, sc_out_w, final_norm_w, loss_target, m_ada_w, m_ada_b, m_mix_norm_w, m_mlp_norm_w, m_mlp_up, m_mlp_down, m_ssd_in_w, m_ssd_conv_w, m_ssd_conv_b, m_ssd_dt_bias, m_ssd_A_log, m_ssd_D, m_ssd_norm_w, m_ssd_out_w, m_sc_in_w, m_sc_conv_w, m_sc_out_w, m_final_norm_w, v_ada_w, v_ada_b, v_mix_norm_w, v_mlp_norm_w, v_mlp_up, v_mlp_down, v_ssd_in_w, v_ssd_conv_w, v_ssd_conv_b, v_ssd_dt_bias, v_ssd_A_log, v_ssd_D, v_ssd_norm_w, v_ssd_out_w, v_sc_in_w, v_sc_conv_w, v_sc_out_w, v_final_norm_w):
    weights = dict(ada_w=ada_w, ada_b=ada_b, mix_norm_w=mix_norm_w, mlp_norm_w=mlp_norm_w, mlp_up=mlp_up,
                   mlp_down=mlp_down, ssd_in_w=ssd_in_w, ssd_conv_w=ssd_conv_w, ssd_conv_b=ssd_conv_b,
                   ssd_dt_bias=ssd_dt_bias, ssd_A_log=ssd_A_log, ssd_D=ssd_D, ssd_norm_w=ssd_norm_w,
                   ssd_out_w=ssd_out_w, sc_in_w=sc_in_w, sc_conv_w=sc_conv_w, sc_out_w=sc_out_w,
                   final_norm_w=final_norm_w)
    moms = dict(ada_w=m_ada_w, ada_b=m_ada_b, mix_norm_w=m_mix_norm_w, mlp_norm_w=m_mlp_norm_w, mlp_up=m_mlp_up,
                mlp_down=m_mlp_down, ssd_in_w=m_ssd_in_w, ssd_conv_w=m_ssd_conv_w, ssd_conv_b=m_ssd_conv_b,
                ssd_dt_bias=m_ssd_dt_bias, ssd_A_log=m_ssd_A_log, ssd_D=m_ssd_D, ssd_norm_w=m_ssd_norm_w,
                ssd_out_w=m_ssd_out_w, sc_in_w=m_sc_in_w, sc_conv_w=m_sc_conv_w, sc_out_w=m_sc_out_w,
                final_norm_w=m_final_norm_w)
    vars_ = dict(ada_w=v_ada_w, ada_b=v_ada_b, mix_norm_w=v_mix_norm_w, mlp_norm_w=v_mlp_norm_w, mlp_up=v_mlp_up,
                 mlp_down=v_mlp_down, ssd_in_w=v_ssd_in_w, ssd_conv_w=v_ssd_conv_w, ssd_conv_b=v_ssd_conv_b,
                 ssd_dt_bias=v_ssd_dt_bias, ssd_A_log=v_ssd_A_log, ssd_D=v_ssd_D, ssd_norm_w=v_ssd_norm_w,
                 ssd_out_w=v_ssd_out_w, sc_in_w=v_sc_in_w, sc_conv_w=v_sc_conv_w, sc_out_w=v_sc_out_w,
                 final_norm_w=v_final_norm_w)
    names = list(weights)

    L, D = x.shape[1], x.shape[2]
    d_inner = 2 * D
    n_heads = d_inner // SSD_P
    hpg = n_heads // SSD_G
    gw = d_inner // SSD_G
    conv_dim = d_inner + 2 * SSD_G * SSD_N
    zx_dim = d_inner + conv_dim
    zx_pad = -(-(zx_dim + LANES) // 512) * 512
    in_ws = ssd_in_w.shape[2]
    in_base, in_off, in_win = _window_geometry(in_ws)
    me = _my_index()
    x0 = x[0]
    tgt = loss_target[0]

    n_mod = ada_w.shape[2]
    (c_all,) = _exchange([c], "gather_c", gather=True)
    gather_handle = {}
    (gather_handle["ssd_in_w"],), token_in = _xfer_start(
        [ssd_in_w[0].astype(BF16)], "gather_start_ssd_in_w", gather=True, via_sibling=(0,), after=(c_all,))
    c_pad = jnp.pad(c_all.reshape(N_DEV, D), ((0, 16 - N_DEV), (0, 0)))
    ada_b_loc = lax.dynamic_slice_in_dim(ada_b, me * n_mod, n_mod, axis=1).reshape(2, 1, n_mod)
    mod_blk, cond_pad = _cond_mod(c_pad, ada_w, ada_b_loc, token_in, "cond_mod")
    gather_order = ["mod", "ssd_conv_w", "sc_conv_w", "ssd_out_w", "up0", "down0", "sc_in_w", "sc_out_w", "up1",
                    "down1"]
    gather_src = dict(mod=mod_blk, ssd_conv_w=ssd_conv_w[0], sc_conv_w=sc_conv_w[0],
                      ssd_out_w=ssd_out_w[0].astype(BF16),
                      up0=mlp_up[0].astype(BF16), down0=mlp_down[0].astype(BF16),
                      sc_in_w=sc_in_w[0].astype(BF16), sc_out_w=sc_out_w[0].astype(BF16),
                      up1=mlp_up[1].astype(BF16), down1=mlp_down[1].astype(BF16))
    handles, gather_token = _xfer_start([gather_src[k] for k in gather_order], "gather_start", gather=True,
                                        via_sibling=tuple(range(3, len(gather_order))))
    gather_handle.update(zip(gather_order, handles))

    def gathered(keys, after, forward):
        tag = "_".join(keys)
        lands = _xfer_wait([gather_handle[k] for k in keys], after, f"gather_wait_{tag}", gather=True)
        return _sibling_forward(lands, f"gather_forward_{tag}") if forward else lands

    def forward_behind(keys, after):
        tag = "_".join(keys)
        lands = _xfer_wait([gather_handle[k] for k in keys], after, f"gather_wait_{tag}", gather=True)
        fwd_handles, token = _sibling_forward_start(lands, f"gather_forward_start_{tag}")
        return (lambda done: _sibling_forward_wait(fwd_handles, done, f"gather_forward_wait_{tag}")), token

    (ssd_in_g,) = gathered(["ssd_in_w"], (gather_token, m_ssd_in_w, v_ssd_in_w), True)
    w_in_all = _shards_to_columns(ssd_in_g, in_base, in_off, in_win, zx_pad, "ssd_in_w_columns")
    (mod_all,) = gathered(["mod"], w_in_all, False)
    mod_mine = lax.dynamic_index_in_dim(mod_all, me, axis=2, keepdims=False)
    mod_mine = jnp.transpose(mod_mine, (1, 0, 2)).reshape(2, 6, 1, D)
    sh_m, sc_m, g_m, sh_f, sc_f, g_f = [[mod_mine[i, k] for i in range(2)] for k in range(6)]

    vec = lambda a: a.reshape(1, -1)
    grads = {}
    small = {}

    _, h0 = _norm_mod_fwd(x0, None, None, vec(mix_norm_w[0]), sc_m[0], sh_m[0], "l0_mix_norm")
    cw_all, scw_all = gathered(["ssd_conv_w", "sc_conv_w"], h0, False)
    (zx,) = _mm_nn(h0, w_in_all, F32, "ssd_in_proj", tm=2048, tn=512)
    conv_b0 = vec(ssd_conv_b[0])
    conv_w_full = jnp.transpose(cw_all, (1, 0, 2)).reshape(SSD_K, conv_dim)
    sc_conv_full = jnp.transpose(scw_all, (1, 0, 2)).reshape(SC_K, D)
    xc = _ssd_conv_fwd(zx, conv_w_full, conv_b0, d_inner, conv_dim, "ssd_conv")
    bias_p = jnp.pad(ssd_dt_bias[0], (0, LANES - n_heads)).reshape(1, LANES)
    alog_p = jnp.pad(ssd_A_log[0], (0, LANES - n_heads)).reshape(1, LANES)
    d_lane = jnp.repeat(ssd_D[0], SSD_P).reshape(SSD_G, 1, gw)
    nw_g = ssd_norm_w[0].reshape(SSD_G, 1, gw)
    finish, token = forward_behind(["ssd_out_w"], xc)
    decay = _ssd_decay(zx, bias_p, alog_p, n_heads, zx_dim // LANES, "ssd_decay")
    y_ssd, yn, prev = _ssd_fwd(zx, xc, decay, d_lane, nw_g, d_inner, token, "ssd_scan")
    ups, downs = [None, None], [None, None]
    (ssd_out_g,) = finish(yn)
    w_ssd_out = ssd_out_g.reshape(-1, D)
    finish, token = forward_behind(["up0", "down0"], ssd_out_g)
    (mix0,) = _mm_nn(yn, w_ssd_out, F32, "ssd_out_proj", after=(token,))
    x1, h1 = _norm_mod_fwd(x0, mix0, g_m[0], vec(mlp_norm_w[0]), sc_f[0], sh_f[0], "l0_mlp_norm")
    ups[0], down0_g = finish(h1)
    downs[0] = down0_g.reshape(-1, D)
    u0, s0 = _mm_nn_blocked(h1, ups[0], "l0_mlp_up", _ep_relu2, [BF16, BF16])
    finish, token = forward_behind(["sc_in_w", "sc_out_w", "up1", "down1"], s0)
    (d0,) = _mm_nn(s0, downs[0], F32, "l0_mlp_down", after=(token,))
    x2, h2 = _norm_mod_fwd(x1, d0, g_f[0], vec(mix_norm_w[1]), sc_m[1], sh_m[1], "l1_mix_norm")
    sc_in_g, sc_out_g, ups[1], down1_g = finish(h2)
    w_sc_out, downs[1] = sc_out_g.reshape(-1, D), down1_g.reshape(-1, D)
    (proj,) = _mm_nn_blocked(h2, sc_in_g, "sc_in_proj", _ep_store(F32), [F32])
    yc = _sc_conv_fwd(proj, sc_conv_full, "sc_conv")
    (mix1,) = _mm_nn(yc, w_sc_out, F32, "sc_out_proj")
    x3, h3 = _norm_mod_fwd(x2, mix1, g_m[1], vec(mlp_norm_w[1]), sc_f[1], sh_f[1], "l1_mlp_norm")
    u1, s1 = _mm_nn_blocked(h3, ups[1], "l1_mlp_up", _ep_relu2, [BF16, BF16])
    (d1,) = _mm_nn(s1, downs[1], F32, "l1_mlp_down")

    dx, loss_lane, dfw, dd1, dg = _final_loss(x3, d1, g_f[1], vec(final_norm_w), tgt, "final_loss")
    small["final_norm_w"] = dfw

    dmod = [[None] * 6 for _ in range(2)]
    dmod[1][5] = dg

    def mlp_backward(i, dx_out, dd, x_mid, h_in, u, s, mix, gate):
        du = _mm_nt(dd, downs[i], BF16, f"l{i}_mlp_down_bwd", epilogue=_ep_relu2_bwd, extra=(u,))
        gdown = _mm_tn(s, dd, BF16, f"l{i}_mlp_down_wgrad").reshape(N_DEV, -1, D)
        gup = _mm_tn_blocked(h_in, du, BF16, f"l{i}_mlp_up_wgrad")
        (h_down, h_up), token = _xfer_start([gdown, gup], f"l{i}_mlp_grads_start", gather=False)
        grad_handle[f"mlp_down{i}"], grad_handle[f"mlp_up{i}"] = h_down, h_up
        dh = _mm_nt_blocked(du, ups[i], F32, f"l{i}_mlp_up_bwd", after=(token,))
        dxm, dsh, dsc, dnw, dmix, dgate = _norm_mod_bwd(dh, x_mid, vec(mlp_norm_w[i]), sc_f[i], dx_out,
                                                        f"l{i}_mlp_norm_bwd", branch=(mix, gate))
        dmod[i][3], dmod[i][4], dmod[i][2] = dsh, dsc, dgate
        return dxm, dmix, dnw

    grad_handle = {}
    dx3, dyc, dnw_mlp1 = mlp_backward(1, dx, dd1, x3, h3, u1, s1, mix1, g_m[1])
    g_sc_out = _mm_tn(yc, dyc, BF16, "sc_out_wgrad").reshape(N_DEV, -1, D)
    dconv_out = _mm_nt(dyc, w_sc_out, F32, "sc_out_bwd")
    dbg, dcg, dxv, dscw = _sc_conv_bwd(proj, sc_conv_full, dconv_out, "sc_conv_bwd")
    dproj = jnp.concatenate([dbg, dcg, dxv], axis=1)
    g_sc_in = _mm_tn_blocked(h2, dproj, BF16, "sc_in_wgrad")
    (grad_handle["sc_out_w0"], grad_handle["sc_in_w0"]), token = _xfer_start(
        [g_sc_out, g_sc_in], "sc_grads_start", gather=False)
    dh2 = _mm_nt_blocked(dproj, sc_in_g, F32, "sc_in_bwd", after=(token,))
    dx2, dsh, dsc, dnw_mix1, dd0, dg = _norm_mod_bwd(dh2, x2, vec(mix_norm_w[1]), sc_m[1], dx3, "l1_mix_norm_bwd",
                                                     branch=(d0, g_f[0]))
    dmod[1][0], dmod[1][1], dmod[0][5] = dsh, dsc, dg
    dx1, dyo, dnw_mlp0 = mlp_backward(0, dx2, dd0, x1, h1, u0, s0, mix0, g_m[0])
    g_ssd_out = _mm_tn(yn, dyo, BF16, "ssd_out_wgrad").reshape(N_DEV, -1, D)
    (grad_handle["ssd_out_w0"],), token = _xfer_start([g_ssd_out], "ssd_out_grad_start", gather=False)
    dyn = _mm_nt(dyo, w_ssd_out, F32, "ssd_out_bwd", after=(token,))
    dz, dxs, db_, dc_, ddt, dbias, dalog, dd_, dnw_ssd = _ssd_bwd(
        dyn, y_ssd, zx, xc, prev, decay, alog_p, d_lane, nw_g, d_inner, "ssd_scan_bwd")
    dzx, dcw, dcb = _ssd_conv_bwd(zx, conv_w_full, conv_b0, [dxs, db_, dc_], dz, d_inner, "ssd_conv_bwd")
    dzx = _dzx_finish(dzx, ddt, zx_dim, "ssd_dzx_finish")
    g_in_all = _mm_tn(h0, dzx, BF16, "ssd_in_wgrad", tn=512, tk=2048)
    g_ssd_in = jnp.stack([g_in_all[:, b:b + in_win] for b in in_base], axis=0)
    (grad_handle["ssd_in_w0"],), token = _xfer_start([g_ssd_in], "ssd_in_grad_start", gather=False)
    dh0 = _mm_nt(dzx, w_in_all, F32, "ssd_in_bwd", tm=1024, tk=dzx.shape[1] // 2, after=(token,))
    grad_x, dsh, dsc, dnw_mix0 = _norm_mod_bwd(dh0, x0, vec(mix_norm_w[0]), sc_m[0], dx1, "l0_mix_norm_bwd")
    dmod[0][0], dmod[0][1] = dsh, dsc

    small["ada_b"] = jnp.concatenate([jnp.concatenate(dmod[i], axis=1) for i in range(2)], axis=0)
    small["mix_norm_w"] = jnp.concatenate([dnw_mix0, dnw_mix1], axis=0)
    small["mlp_norm_w"] = jnp.concatenate([dnw_mlp0, dnw_mlp1], axis=0)
    small["ssd_conv_w"] = dcw
    small["ssd_conv_b"] = dcb
    small["ssd_norm_w"] = dnw_ssd.reshape(1, d_inner)
    small["sc_conv_w"] = dscw
    small["loss"] = loss_lane
    small_names = list(small)
    head_names = ["ssd_dt_bias", "ssd_A_log", "ssd_D"]
    handles, small_token = _xfer_start([small[k] for k in small_names] + [dbias, dalog, dd_],
                                       "small_grads_start", gather=True)

    out_g, out_d, out_m, out_v = {}, {}, {}, {}

    layer_res = {}

    def big_update(name, i, after):
        (parts,) = _xfer_wait([grad_handle[f"{name}{i}"]], after, f"grads_wait_{name}_{i}", gather=False)
        res = _adamw_sum(parts, weights[name], moms[name], vars_[name], i, f"adamw_{name}_{i}",
                         prev=layer_res.get(name), window_off=in_off if name == "ssd_in_w" else None)
        layer_res[name] = res
        return res[1]

    chain = small_token
    for name, i in [("mlp_down", 1), ("mlp_up", 1), ("sc_out_w", 0), ("sc_in_w", 0), ("mlp_down", 0),
                    ("mlp_up", 0), ("ssd_out_w", 0), ("ssd_in_w", 0)]:
        chain = big_update(name, i, chain)
    gathered_small = _xfer_wait(handles, chain, "small_grads_wait", gather=True)
    small_all = dict(zip(small_names + head_names, gathered_small))

    dmod_loc = lax.dynamic_slice_in_dim(small_all["ada_b"], me * n_mod, n_mod, axis=2)
    dmod_pad = jnp.pad(jnp.transpose(dmod_loc, (1, 0, 2)), ((0, 0), (0, 16 - N_DEV), (0, 0)))
    out_g["ada_w"], out_d["ada_w"], out_m["ada_w"], out_v["ada_w"] = _ada_adamw(
        cond_pad, dmod_pad, ada_w, m_ada_w, v_ada_w, "adamw_ada_w")

    for k in ("ssd_conv_w", "sc_conv_w"):
        n_loc = weights[k].shape[2]
        small_all[k] = lax.dynamic_slice_in_dim(small_all[k], me * n_loc, n_loc, axis=2)
    plain = [k for k in small_names if k != "loss"]
    as2d = lambda a: a.reshape(-1, a.shape[-1])
    res, loss_row = _adamw_small(
        [small_all[k] for k in plain], [tuple(as2d(d[k]) for d in (weights, moms, vars_)) for k in plain],
        [small_all[k] for k in head_names], [tuple(as2d(d[k]) for d in (weights, moms, vars_)) for k in head_names],
        small_all["loss"], "adamw_small")
    loss = loss_row[0, 0]
    for k, res4 in zip(plain + head_names, res):
        for r, dst in zip(res4, (out_g, out_d, out_m, out_v)):
            dst[k] = r.reshape(weights[k].shape)
    for name, res4 in layer_res.items():
        for r, dst in zip(res4, (out_g, out_d, out_m, out_v)):
            dst[name] = r

    return (loss, grad_x[None], *[out_g[k] for k in names], *[out_d[k] for k in names],
            *[out_m[k] for k in names], *[out_v[k] for k in names])
```

```python
import jax
import jax.numpy as jnp
from jax import lax
from jax.experimental import pallas as pl
from jax.experimental.pallas import tpu as pltpu

F32 = jnp.float32
BF16 = jnp.bfloat16
N_DEV = 8
MESH = pl.DeviceIdType.MESH

NORM_EPS = 1e-5
SSD_G = 4
SSD_P = 64
SSD_N = 128
SSD_CHUNK = 128
SSD_K = 4
SC_K = 3
LANES = 128

ADAM_LR = 0.001
ADAM_B1 = 0.9
ADAM_B2 = 0.999
ADAM_EPS = 1e-08
ADAM_WD = 0.01
ADAM_STEP = 10

VMEM_LIMIT = 56 * 1024 * 1024


def _pcall(body, **kw):
    return pl.pallas_call(body, **kw)


def _cparams(sem=None):
    if sem is None:
        return pltpu.CompilerParams(vmem_limit_bytes=VMEM_LIMIT)
    return pltpu.CompilerParams(dimension_semantics=sem, vmem_limit_bytes=VMEM_LIMIT)


def _my_index():
    return 4 * lax.axis_index("x") + 2 * lax.axis_index("y") + lax.axis_index("c")


_PEER_MASKS = [(0, 0, 1), (0, 1, 0), (0, 1, 1), (1, 0, 0), (1, 0, 1), (1, 1, 0), (1, 1, 1)]


def _peers():
    x, y, c = lax.axis_index("x"), lax.axis_index("y"), lax.axis_index("c")
    out = []
    for mx, my, mc in _PEER_MASKS:
        px = (1 - x) if mx else x
        py = (1 - y) if my else y
        pc = (1 - c) if mc else c
        out.append(((px, py, pc), 4 * px + 2 * py + pc))
    return out


def _exchange(arrs, name, gather):
    n = len(arrs)
    n_peer = N_DEV - 1

    def body(*refs):
        ins, outs = refs[:n], refs[n:2 * n]
        send_sems, recv_sems, local_sems = refs[2 * n:]
        me = _my_index()
        peers = _peers()
        started = []
        for a in range(n):
            src_own = ins[a] if gather else ins[a].at[me]
            own = pltpu.make_async_copy(src_own, outs[a].at[me], local_sems.at[a])
            own.start()
            started.append(own)
        sends = []
        for a in range(n):
            for k, (peer, pidx) in enumerate(peers):
                src = ins[a] if gather else ins[a].at[pidx]
                cp = pltpu.make_async_remote_copy(
                    src_ref=src, dst_ref=outs[a].at[me],
                    send_sem=send_sems.at[a * n_peer + k], recv_sem=recv_sems.at[a * n_peer + k],
                    device_id=peer, device_id_type=MESH)
                cp.start()
                sends.append(cp)
        for a in range(n):
            for k, (peer, pidx) in enumerate(peers):
                src = ins[a] if gather else ins[a].at[pidx]
                pltpu.make_async_remote_copy(
                    src_ref=src, dst_ref=outs[a].at[pidx],
                    send_sem=send_sems.at[a * n_peer + k], recv_sem=recv_sems.at[a * n_peer + k],
                    device_id=peer, device_id_type=MESH).wait_recv()
        for cp in sends:
            cp.wait_send()
        for own in started:
            own.wait()

    if gather:
        out_shape = [jax.ShapeDtypeStruct((N_DEV,) + a.shape, a.dtype) for a in arrs]
    else:
        out_shape = [jax.ShapeDtypeStruct(a.shape, a.dtype) for a in arrs]
    any_spec = pl.BlockSpec(memory_space=pl.ANY)
    outs = _pcall(
        body, name=name, out_shape=out_shape,
        in_specs=[any_spec] * n, out_specs=[any_spec] * n,
        scratch_shapes=[pltpu.SemaphoreType.DMA((n * n_peer,)), pltpu.SemaphoreType.DMA((n * n_peer,)),
                        pltpu.SemaphoreType.DMA((n,))],
        compiler_params=pltpu.CompilerParams(has_side_effects=True),
    )(*arrs)
    return list(outs)


def _sibling_forward_start(lands, name):
    n = len(lands)
    n_fwd = len(_OTHER_CHIPS)

    def body(*refs):
        ins, bufs = refs[:n], refs[3 * n:4 * n]
        token = refs[-1]
        sibling = (lax.axis_index("x"), lax.axis_index("y"), 1 - lax.axis_index("c"))
        peers = _peers()
        for a in range(n):
            send_sems, recv_sems = refs[n + 2 * a], refs[n + 2 * a + 1]
            for j, k in enumerate(_OTHER_CHIPS):
                slot = peers[k][1]
                pltpu.make_async_remote_copy(
                    src_ref=ins[a].at[slot], dst_ref=bufs[a].at[slot], send_sem=send_sems.at[j],
                    recv_sem=recv_sems.at[j], device_id=sibling, device_id_type=MESH).start()
        token[...] = jnp.zeros_like(token)

    out_shape, out_specs = [], []
    for _ in range(n):
        out_shape += [pltpu.SemaphoreType.DMA((n_fwd,)), pltpu.SemaphoreType.DMA((n_fwd,))]
        out_specs += [_SEM, _SEM]
    out_shape += [pltpu.HBM(a.shape, a.dtype) for a in lands] + [jax.ShapeDtypeStruct((8, LANES), F32)]
    out_specs += [_HBM] * n + [pl.BlockSpec(memory_space=pltpu.VMEM)]
    outs = _pcall(
        body, name=name, out_shape=tuple(out_shape), in_specs=[_HBM] * n, out_specs=tuple(out_specs),
        input_output_aliases={a: 2 * n + a for a in range(n)},
        compiler_params=pltpu.CompilerParams(has_side_effects=_DATAFLOW),
    )(*[pltpu.with_memory_space_constraint(a, pltpu.HBM) for a in lands])
    return [(outs[2 * n + a], outs[2 * a], outs[2 * a + 1]) for a in range(n)], outs[-1]


def _sibling_forward_wait(handles, after, name):
    n = len(handles)

    def body(*refs):
        sibling = (lax.axis_index("x"), lax.axis_index("y"), 1 - lax.axis_index("c"))
        peers = _peers()
        for a in range(n):
            buf, send_sems, recv_sems = refs[3 * a:3 * a + 3]
            for j, k in enumerate(_OTHER_CHIPS):
                (px, py, pc), slot = peers[k]
                theirs = 4 * px + 2 * py + (1 - pc)
                cp = pltpu.make_async_remote_copy(
                    src_ref=buf.at[slot], dst_ref=buf.at[theirs], send_sem=send_sems.at[j],
                    recv_sem=recv_sems.at[j], device_id=sibling, device_id_type=MESH)
                cp.wait_send()
                cp.wait_recv()

    operands, in_specs = [], []
    for h in handles:
        operands += list(h)
        in_specs += [_HBM, _SEM, _SEM]
    outs = _pcall(
        body, name=name, out_shape=tuple(pltpu.HBM(h[0].shape, h[0].dtype) for h in handles),
        in_specs=in_specs + [pl.BlockSpec(memory_space=pl.ANY)], out_specs=tuple([_HBM] * n),
        input_output_aliases={3 * a: a for a in range(n)},
        compiler_params=pltpu.CompilerParams(has_side_effects=_DATAFLOW),
    )(*operands, after)
    return list(outs)


_HBM = pl.BlockSpec(memory_space=pltpu.HBM)
_SEM = pl.BlockSpec(memory_space=pltpu.SEMAPHORE)
_DATAFLOW = pltpu.SideEffectType.DATAFLOW_SIDE_EFFECTING


_ALL_PEERS = tuple(range(N_DEV - 1))
_SAME_CORE_PEERS = (0, 1, 3, 5)
_OTHER_CHIPS = (1, 3, 5)


def _xfer_start(arrs, name, gather, via_sibling=(), after=()):
    n = len(arrs)
    n_peer = N_DEV - 1
    n_after = len(after)
    peer_ks = [_SAME_CORE_PEERS if a in via_sibling else _ALL_PEERS for a in range(n)]

    def body(*refs):
        ins, lands = refs[:n], refs[n:2 * n]
        sems = refs[2 * n + n_after:5 * n + n_after]
        token = refs[-1]
        me = _my_index()
        peers = _peers()
        for a in range(n):
            send_sems, recv_sems, loc_sem = sems[3 * a:3 * a + 3]
            src_own = ins[a] if gather else ins[a].at[me]
            pltpu.make_async_copy(src_own, lands[a].at[me], loc_sem).start()
            for k in peer_ks[a]:
                peer, pidx = peers[k]
                src = ins[a] if gather else ins[a].at[pidx]
                pltpu.make_async_remote_copy(
                    src_ref=src, dst_ref=lands[a].at[me], send_sem=send_sems.at[k], recv_sem=recv_sems.at[k],
                    device_id=peer, device_id_type=MESH).start()
        token[...] = jnp.zeros_like(token)

    land_shapes = [((N_DEV,) + a.shape) if gather else a.shape for a in arrs]
    out_shape, out_specs = [], []
    for _ in range(n):
        out_shape += [pltpu.SemaphoreType.DMA((n_peer,)), pltpu.SemaphoreType.DMA((n_peer,)),
                      pltpu.SemaphoreType.DMA(())]
        out_specs += [_SEM, _SEM, _SEM]
    out_shape += [pltpu.HBM(a.shape, a.dtype) for a in arrs]
    out_shape += [pltpu.HBM(s, a.dtype) for s, a in zip(land_shapes, arrs)]
    out_shape += [jax.ShapeDtypeStruct((8, LANES), F32)]
    out_specs += [_HBM] * (2 * n) + [pl.BlockSpec(memory_space=pltpu.VMEM)]
    aliases = {}
    for a in range(n):
        aliases[a] = 3 * n + a
        aliases[n + a] = 4 * n + a
    operands = [pltpu.with_memory_space_constraint(a, pltpu.HBM) for a in arrs]
    operands += [pltpu.with_memory_space_constraint(lax.empty(s, a.dtype), pltpu.HBM)
                 for s, a in zip(land_shapes, arrs)]
    outs = _pcall(
        body, name=name, out_shape=tuple(out_shape),
        in_specs=[_HBM] * (2 * n) + [pl.BlockSpec(memory_space=pl.ANY)] * n_after, out_specs=tuple(out_specs),
        input_output_aliases=aliases,
        compiler_params=pltpu.CompilerParams(has_side_effects=_DATAFLOW),
    )(*operands, *after)
    handles = []
    for a in range(n):
        handles.append((outs[3 * n + a], outs[4 * n + a], outs[3 * a], outs[3 * a + 1], outs[3 * a + 2],
                        peer_ks[a]))
    return handles, outs[-1]


def _xfer_wait(handles, after, name, gather):
    n = len(handles)
    after = tuple(after) if isinstance(after, (tuple, list)) else (after,)
    peer_ks = [h[5] for h in handles]

    def body(*refs):
        me = _my_index()
        peers = _peers()
        for a in range(n):
            src_ref, land_ref, send_ref, recv_ref, loc_ref = refs[5 * a:5 * a + 5]
            src_own = src_ref if gather else src_ref.at[me]
            pltpu.make_async_copy(src_own, land_ref.at[me], loc_ref).wait()
            for k in peer_ks[a]:
                peer, pidx = peers[k]
                src = src_ref if gather else src_ref.at[pidx]
                cp = pltpu.make_async_remote_copy(
                    src_ref=src, dst_ref=land_ref.at[pidx], send_sem=send_ref.at[k], recv_sem=recv_ref.at[k],
                    device_id=peer, device_id_type=MESH)
                cp.wait_send()
                cp.wait_recv()

    operands, in_specs, out_shape, aliases = [], [], [], {}
    for a, h in enumerate(handles):
        operands += list(h[:5])
        in_specs += [_HBM, _HBM, _SEM, _SEM, _SEM]
        out_shape += [pltpu.HBM(h[0].shape, h[0].dtype), pltpu.HBM(h[1].shape, h[1].dtype)]
        aliases[5 * a] = 2 * a
        aliases[5 * a + 1] = 2 * a + 1
    outs = _pcall(
        body, name=name, out_shape=tuple(out_shape),
        in_specs=in_specs + [pl.BlockSpec(memory_space=pl.ANY)] * len(after),
        out_specs=tuple([_HBM] * (2 * n)), input_output_aliases=aliases,
        compiler_params=pltpu.CompilerParams(has_side_effects=_DATAFLOW),
    )(*operands, *after)
    return [outs[2 * a + 1] for a in range(n)]


def _sibling_forward(lands, name):
    n = len(lands)
    n_fwd = len(_OTHER_CHIPS)

    def body(*refs):
        ins, bufs = refs[:n], refs[n:2 * n]
        send_sems, recv_sems = refs[2 * n:]
        x, y, c = lax.axis_index("x"), lax.axis_index("y"), lax.axis_index("c")
        sibling = (x, y, 1 - c)
        peers = _peers()
        sends = []
        for a in range(n):
            for j, k in enumerate(_OTHER_CHIPS):
                slot = peers[k][1]
                cp = pltpu.make_async_remote_copy(
                    src_ref=ins[a].at[slot], dst_ref=bufs[a].at[slot],
                    send_sem=send_sems.at[a * n_fwd + j], recv_sem=recv_sems.at[a * n_fwd + j],
                    device_id=sibling, device_id_type=MESH)
                cp.start()
                sends.append(cp)
        for a in range(n):
            for j, k in enumerate(_OTHER_CHIPS):
                (px, py, pc), slot = peers[k]
                theirs = 4 * px + 2 * py + (1 - pc)
                pltpu.make_async_remote_copy(
                    src_ref=ins[a].at[slot], dst_ref=bufs[a].at[theirs],
                    send_sem=send_sems.at[a * n_fwd + j], recv_sem=recv_sems.at[a * n_fwd + j],
                    device_id=sibling, device_id_type=MESH).wait_recv()
        for cp in sends:
            cp.wait_send()

    any_spec = pl.BlockSpec(memory_space=pl.ANY)
    outs = _pcall(
        body, name=name, out_shape=[jax.ShapeDtypeStruct(a.shape, a.dtype) for a in lands],
        in_specs=[any_spec] * n, out_specs=[any_spec] * n,
        input_output_aliases={a: a for a in range(n)},
        scratch_shapes=[pltpu.SemaphoreType.DMA((n * n_fwd,)), pltpu.SemaphoreType.DMA((n * n_fwd,))],
        compiler_params=pltpu.CompilerParams(has_side_effects=True),
    )(*lands)
    return list(outs)


_DIMS = {"nn": (((1,), (0,)), ((), ())), "nt": (((1,), (1,)), ((), ())), "tn": (((0,), (0,)), ((), ()))}


def _dot(a, b, mode="nn"):
    return lax.dot_general(a, b, _DIMS[mode], preferred_element_type=F32)


def _mm(a, b, *, mode, grid, a_spec, b_spec, out_shape, out_specs, acc_shape, epilogue, name,
        extra=(), extra_specs=(), after=(), semantics=("parallel", "parallel", "arbitrary")):
    nk = grid[2]
    n_extra = len(extra)
    n_in = 2 + n_extra + len(after)

    def body_single(*refs):
        a_ref, b_ref = refs[0], refs[1]
        epilogue(_dot(a_ref[...], b_ref[...], mode), refs[2:2 + n_extra], refs[n_in:])

    def body_acc(*refs):
        a_ref, b_ref = refs[0], refs[1]
        ex = refs[2:2 + n_extra]
        outs = refs[n_in:-1]
        acc = refs[-1]
        k = pl.program_id(2)

        @pl.when(k == 0)
        def _():
            acc[...] = jnp.zeros_like(acc)

        acc[...] += _dot(a_ref[...], b_ref[...], mode)

        @pl.when(k == nk - 1)
        def _():
            epilogue(acc[...], ex, outs)

    return _pcall(
        body_single if nk == 1 else body_acc, name=name, grid=grid, out_shape=out_shape,
        in_specs=[a_spec, b_spec] + list(extra_specs) + [pl.BlockSpec(memory_space=pl.ANY)] * len(after),
        out_specs=out_specs,
        scratch_shapes=[] if nk == 1 else [pltpu.VMEM(acc_shape, F32)],
        compiler_params=_cparams(semantics),
    )(a, b, *extra, *after)


def _ep_store(dtype):
    def ep(acc, ex, outs):
        outs[0][...] = acc.astype(dtype)
    return ep


def _ep_relu2(acc, ex, outs):
    outs[0][...] = acc.astype(BF16)
    r = jnp.maximum(acc, 0.0)
    outs[1][...] = (r * r).astype(BF16)


def _ep_relu2_bwd(acc, ex, outs):
    u = ex[0][...].astype(F32)
    outs[0][...] = (acc * (2.0 * jnp.maximum(u, 0.0))).astype(BF16)


def _tile(n, want):
    t = min(n, want)
    while n % t:
        t //= 2
    return t


def _mm_nn(a, w, out_dtype, name, tm=2048, tn=1024, tk=1024, epilogue=None, out_dtypes=None, after=()):
    M, K = a.shape
    N = w.shape[1]
    tm, tn, tk = _tile(M, tm), _tile(N, tn), _tile(K, tk)
    out_dtypes = out_dtypes or [out_dtype]
    return _mm(a, w, mode="nn", grid=(M // tm, N // tn, K // tk),
               a_spec=pl.BlockSpec((tm, tk), lambda i, j, k: (i, k)),
               b_spec=pl.BlockSpec((tk, tn), lambda i, j, k: (k, j)),
               out_shape=[jax.ShapeDtypeStruct((M, N), d) for d in out_dtypes],
               out_specs=[pl.BlockSpec((tm, tn), lambda i, j, k: (i, j)) for _ in out_dtypes],
               acc_shape=(tm, tn), epilogue=epilogue or _ep_store(out_dtype), name=name, after=after)


def _mm_nn_blocked(a, wg, name, epilogue, out_dtypes, tm=2048):
    M, K = a.shape
    n = wg.shape[2]
    tm = _tile(M, tm)
    return _mm(a, wg, mode="nn", grid=(M // tm, N_DEV, 1),
               a_spec=pl.BlockSpec((tm, K), lambda i, j, k: (i, 0)),
               b_spec=pl.BlockSpec((None, K, n), lambda i, j, k: (j, 0, 0)),
               out_shape=[jax.ShapeDtypeStruct((M, N_DEV * n), d) for d in out_dtypes],
               out_specs=[pl.BlockSpec((tm, n), lambda i, j, k: (i, j)) for _ in out_dtypes],
               acc_shape=(tm, n), epilogue=epilogue, name=name)


def _mm_nt(a, w, out_dtype, name, tm=2048, tn=1024, tk=1024, epilogue=None, extra=(), extra_specs=(),
           after=()):
    M, K = a.shape
    N = w.shape[0]
    tm, tn, tk = _tile(M, tm), _tile(N, tn), _tile(K, tk)
    if extra and not extra_specs:
        extra_specs = [pl.BlockSpec((tm, tn), lambda i, j, k: (i, j)) for _ in extra]
    return _mm(a, w, mode="nt", grid=(M // tm, N // tn, K // tk),
               a_spec=pl.BlockSpec((tm, tk), lambda i, j, k: (i, k)),
               b_spec=pl.BlockSpec((tn, tk), lambda i, j, k: (j, k)),
               out_shape=[jax.ShapeDtypeStruct((M, N), out_dtype)],
               out_specs=[pl.BlockSpec((tm, tn), lambda i, j, k: (i, j))],
               acc_shape=(tm, tn), epilogue=epilogue or _ep_store(out_dtype), name=name,
               extra=extra, extra_specs=extra_specs, after=after)[0]


def _mm_nt_blocked(a, wg, out_dtype, name, tm=1024, after=()):
    M = a.shape[0]
    kout, n = wg.shape[1], wg.shape[2]
    tm = _tile(M, tm)
    return _mm(a, wg, mode="nt", grid=(M // tm, 1, N_DEV),
               a_spec=pl.BlockSpec((tm, n), lambda i, j, k: (i, k)),
               b_spec=pl.BlockSpec((None, kout, n), lambda i, j, k: (k, 0, 0)),
               out_shape=[jax.ShapeDtypeStruct((M, kout), out_dtype)],
               out_specs=[pl.BlockSpec((tm, kout), lambda i, j, k: (i, 0))],
               acc_shape=(tm, kout), epilogue=_ep_store(out_dtype), name=name, after=after)[0]


def _mm_tn(a, b, out_dtype, name, tm=1024, tn=1024, tk=2048):
    K, M = a.shape
    N = b.shape[1]
    tm, tn, tk = _tile(M, tm), _tile(N, tn), _tile(K, tk)
    return _mm(a, b, mode="tn", grid=(M // tm, N // tn, K // tk),
               a_spec=pl.BlockSpec((tk, tm), lambda i, j, k: (k, i)),
               b_spec=pl.BlockSpec((tk, tn), lambda i, j, k: (k, j)),
               out_shape=[jax.ShapeDtypeStruct((M, N), out_dtype)],
               out_specs=[pl.BlockSpec((tm, tn), lambda i, j, k: (i, j))],
               acc_shape=(tm, tn), epilogue=_ep_store(out_dtype), name=name)[0]


def _mm_tn_blocked(a, b, out_dtype, name, tm=1024, tk=2048):
    K, M = a.shape
    n = b.shape[1] // N_DEV
    tm, tk = _tile(M, tm), _tile(K, tk)
    return _mm(a, b, mode="tn", grid=(M // tm, N_DEV, K // tk),
               a_spec=pl.BlockSpec((tk, tm), lambda i, j, k: (k, i)),
               b_spec=pl.BlockSpec((tk, n), lambda i, j, k: (k, j)),
               out_shape=[jax.ShapeDtypeStruct((N_DEV, M, n), out_dtype)],
               out_specs=[pl.BlockSpec((None, tm, n), lambda i, j, k: (j, i, 0))],
               acc_shape=(tm, n), epilogue=_ep_store(out_dtype), name=name)[0]


def _window_geometry(ws):
    base = [(ws * k // LANES) * LANES for k in range(N_DEV)]
    off = [ws * k - base[k] for k in range(N_DEV)]
    win = -(-(max(off) + ws) // LANES) * LANES
    return base, off, win


def _shards_to_columns(xg, base, off, win, n_out, name, tr=256):
    R, ws = xg.shape[1], xg.shape[2]
    tr = _tile(R, tr)
    nb_win = win // LANES

    def body(x_ref, o_ref, frame_ref):
        written = set()
        frame_ref[...] = jnp.zeros_like(frame_ref)
        for k in range(N_DEV):
            frame_ref[:, 0:ws] = x_ref[k].astype(F32)
            window = frame_ref[...]
            if off[k]:
                window = pltpu.roll(window, off[k], 1)
            for i in range(nb_win):
                b = base[k] // LANES + i
                if b * LANES >= n_out:
                    continue
                cols = slice(b * LANES, (b + 1) * LANES)
                blk = window[:, i * LANES:(i + 1) * LANES]
                if b in written:
                    blk = blk + o_ref[:, cols].astype(F32)
                o_ref[:, cols] = blk.astype(o_ref.dtype)
                written.add(b)
        for b in range(n_out // LANES):
            if b not in written:
                o_ref[:, b * LANES:(b + 1) * LANES] = jnp.zeros((tr, LANES), o_ref.dtype)

    return _pcall(
        body, name=name, grid=(R // tr,), out_shape=jax.ShapeDtypeStruct((R, n_out), xg.dtype),
        in_specs=[pl.BlockSpec((N_DEV, tr, ws), lambda i: (0, i, 0))],
        out_specs=pl.BlockSpec((tr, n_out), lambda i: (i, 0)),
        scratch_shapes=[pltpu.VMEM((tr, win), F32)],
        compiler_params=_cparams(("parallel",)))(xg)


def _sigmoid(x):
    return 1.0 / (1.0 + jnp.exp(-x))


def _row_spec(tm, d):
    return pl.BlockSpec((tm, d), lambda i: (i, 0))


def _vec_spec(d):
    return pl.BlockSpec((1, d), lambda i: (0, 0))


def _norm_mod_fwd(x, y, gate, nw, scale, shift, name, tm=512):
    L, D = x.shape
    tm = _tile(L, tm)
    has_res = y is not None

    def body(*refs):
        if has_res:
            x_ref, y_ref, g_ref, nw_ref, sc_ref, sh_ref, xo_ref, h_ref = refs
            xn = x_ref[...] + g_ref[...] * y_ref[...]
            xo_ref[...] = xn
        else:
            x_ref, nw_ref, sc_ref, sh_ref, h_ref = refs
            xn = x_ref[...]
        rstd = lax.rsqrt(jnp.mean(xn * xn, axis=-1, keepdims=True) + NORM_EPS)
        h = xn * rstd * nw_ref[...] * (1.0 + sc_ref[...]) + sh_ref[...]
        h_ref[...] = h.astype(BF16)

    row, vec = _row_spec(tm, D), _vec_spec(D)
    if has_res:
        ins, in_specs = (x, y, gate, nw, scale, shift), [row, row, vec, vec, vec, vec]
        out_shape = [jax.ShapeDtypeStruct((L, D), F32), jax.ShapeDtypeStruct((L, D), BF16)]
        out_specs = [row, row]
    else:
        ins, in_specs = (x, nw, scale, shift), [row, vec, vec, vec]
        out_shape = [jax.ShapeDtypeStruct((L, D), BF16)]
        out_specs = [row]
    outs = _pcall(body, name=name, grid=(L // tm,), out_shape=out_shape, in_specs=in_specs,
                  out_specs=out_specs, compiler_params=_cparams(("parallel",)))(*ins)
    return outs if has_res else (x, outs[0])


def _gated_branch_bwd(dx, branch, y_ref, g_ref, dy_ref, dg_ref):
    if branch is None:
        return
    dy_ref[...] = (g_ref[...] * dx).astype(BF16)
    dg_ref[...] += jnp.sum(dx * y_ref[...], axis=0, keepdims=True)


def _norm_mod_bwd(dh, x, nw, scale, dres, name, branch=None, tm=512):
    L, D = x.shape
    tm = _tile(L, tm)
    nb = 0 if branch is None else 2

    def body(dh_ref, x_ref, nw_ref, sc_ref, dres_ref, *rest):
        y_ref, g_ref = rest[:nb] if nb else (None, None)
        dx_ref, dsh_ref, dsc_ref, dnw_ref = rest[nb:nb + 4]
        dy_ref, dg_ref = rest[nb + 4:] if nb else (None, None)

        @pl.when(pl.program_id(0) == 0)
        def _():
            dsh_ref[...] = jnp.zeros_like(dsh_ref)
            dsc_ref[...] = jnp.zeros_like(dsc_ref)
            dnw_ref[...] = jnp.zeros_like(dnw_ref)
            if nb:
                dg_ref[...] = jnp.zeros_like(dg_ref)

        xv = x_ref[...]
        dh_v = dh_ref[...]
        nw_v = nw_ref[...]
        rstd = lax.rsqrt(jnp.mean(xv * xv, axis=-1, keepdims=True) + NORM_EPS)
        xhat = xv * rstd
        dsh_ref[...] += jnp.sum(dh_v, axis=0, keepdims=True)
        dsc_ref[...] += jnp.sum(dh_v * (xhat * nw_v), axis=0, keepdims=True)
        dr = dh_v * (1.0 + sc_ref[...])
        dnw_ref[...] += jnp.sum(dr * xhat, axis=0, keepdims=True)
        dxh = dr * nw_v
        dx = rstd * (dxh - xhat * jnp.mean(dxh * xhat, axis=-1, keepdims=True)) + dres_ref[...]
        dx_ref[...] = dx
        _gated_branch_bwd(dx, branch, y_ref, g_ref, dy_ref, dg_ref)

    row, vec = _row_spec(tm, D), _vec_spec(D)
    extra_in = [] if branch is None else list(branch)
    return _pcall(
        body, name=name, grid=(L // tm,),
        out_shape=[jax.ShapeDtypeStruct((L, D), F32)] + [jax.ShapeDtypeStruct((1, D), F32)] * 3
        + ([jax.ShapeDtypeStruct((L, D), BF16), jax.ShapeDtypeStruct((1, D), F32)] if nb else []),
        in_specs=[row, row, vec, vec, row] + ([row, vec] if nb else []),
        out_specs=[row, vec, vec, vec] + ([row, vec] if nb else []),
        compiler_params=_cparams(("arbitrary",)))(dh, x, nw, scale, dres, *extra_in)


def _final_loss(x, y, gate, fw, target, name, tm=512):
    L, D = x.shape
    tm = _tile(L, tm)

    def body(x_ref, y_ref, g_ref, fw_ref, t_ref, dx_ref, loss_ref, dfw_ref, dy_ref, dg_ref):
        @pl.when(pl.program_id(0) == 0)
        def _():
            loss_ref[...] = jnp.zeros_like(loss_ref)
            dfw_ref[...] = jnp.zeros_like(dfw_ref)
            dg_ref[...] = jnp.zeros_like(dg_ref)

        xn = x_ref[...] + g_ref[...] * y_ref[...]
        fw_v = fw_ref[...]
        rstd = lax.rsqrt(jnp.mean(xn * xn, axis=-1, keepdims=True) + NORM_EPS)
        xhat = xn * rstd
        diff = xhat * fw_v - t_ref[...]
        loss_ref[...] += jnp.sum(diff * diff, axis=0, keepdims=True)
        dyf = diff * (1.0 / D)
        dfw_ref[...] += jnp.sum(dyf * xhat, axis=0, keepdims=True)
        dxh = dyf * fw_v
        dx = rstd * (dxh - xhat * jnp.mean(dxh * xhat, axis=-1, keepdims=True))
        dx_ref[...] = dx
        _gated_branch_bwd(dx, True, y_ref, g_ref, dy_ref, dg_ref)

    row, vec = _row_spec(tm, D), _vec_spec(D)
    return _pcall(
        body, name=name, grid=(L // tm,),
        out_shape=[jax.ShapeDtypeStruct((L, D), F32), jax.ShapeDtypeStruct((1, D), F32),
                   jax.ShapeDtypeStruct((1, D), F32), jax.ShapeDtypeStruct((L, D), BF16),
                   jax.ShapeDtypeStruct((1, D), F32)],
        in_specs=[row, row, vec, vec, row], out_specs=[row, vec, vec, row, vec],
        compiler_params=_cparams(("arbitrary",)))(x, y, gate, fw, target)


def _shift_down(v, s, row):
    if s == 0:
        return v
    return jnp.where(row >= s, pltpu.roll(v, s, 0), 0.0)


CONV_ROWS = 32


def _shifted_rows(x_ref, r0, n, lanes=slice(None)):
    cur = x_ref[r0:r0 + CONV_ROWS, lanes]
    if r0 >= n - 1:
        return [cur] + [x_ref[r0 - s:r0 - s + CONV_ROWS, lanes] for s in range(1, n)]
    row = lax.broadcasted_iota(jnp.int32, cur.shape, 0)
    return [_shift_down(cur, s, row) for s in range(n)]


def _ssd_conv_fwd(zx, w, b, col0, width, name, cb=512):
    L = zx.shape[0]
    nb = width // cb
    off = col0 // cb

    def body(x_ref, w_ref, b_ref, o_ref):
        for l0 in range(0, cb, LANES):
            lanes = slice(l0, l0 + LANES)
            taps = [w_ref[k:k + 1, lanes] for k in range(SSD_K)]
            bias = b_ref[:, lanes]
            for r0 in range(0, L, CONV_ROWS):
                taps_in = _shifted_rows(x_ref, r0, SSD_K, lanes)
                acc = bias + taps[SSD_K - 1] * taps_in[0]
                for s in range(1, SSD_K):
                    acc = acc + taps[SSD_K - 1 - s] * taps_in[s]
                o_ref[r0:r0 + CONV_ROWS, lanes] = acc * _sigmoid(acc)

    return _pcall(
        body, name=name, grid=(nb,), out_shape=jax.ShapeDtypeStruct((L, width), F32),
        in_specs=[pl.BlockSpec((L, cb), lambda j: (0, off + j)),
                  pl.BlockSpec((SSD_K, cb), lambda j: (0, j)),
                  pl.BlockSpec((1, cb), lambda j: (0, j))],
        out_specs=pl.BlockSpec((L, cb), lambda j: (0, j)),
        compiler_params=_cparams(("parallel",)))(zx, w, b)


def _ssd_conv_bwd(zx, w, b, d_parts, dzx, col0, name, cb=128):
    L = zx.shape[0]
    widths = [p.shape[1] for p in d_parts]
    width = sum(widths)
    nb = width // cb
    off = col0 // cb
    starts = [sum(widths[:i]) // cb for i in range(len(d_parts))]
    counts = [wd // cb for wd in widths]

    def body(x_ref, w_ref, b_ref, *rest):
        d_refs = rest[:len(d_parts)]
        dx_ref, dw_ref, db_ref, dpre_ref = rest[len(d_parts) + 1:]
        j = pl.program_id(0)
        taps = [w_ref[k:k + 1, :] for k in range(SSD_K)]
        bias = b_ref[...]
        fold = lambda v: sum(v[r:r + 8, :] for r in range(0, CONV_ROWS, 8))
        db8 = jnp.zeros((8, cb), F32)
        dw8 = [jnp.zeros((8, cb), F32) for _ in range(SSD_K)]
        for r0 in range(0, L, CONV_ROWS):
            rows = slice(r0, r0 + CONV_ROWS)
            d_val = d_refs[-1][rows, :]
            for i in range(len(d_parts) - 2, -1, -1):
                d_val = jnp.where(j < starts[i + 1], d_refs[i][rows, :], d_val)
            taps_in = _shifted_rows(x_ref, r0, SSD_K)
            acc = bias + taps[SSD_K - 1] * taps_in[0]
            for s in range(1, SSD_K):
                acc = acc + taps[SSD_K - 1 - s] * taps_in[s]
            sig = _sigmoid(acc)
            dpre = d_val * (sig * (1.0 + acc * (1.0 - sig)))
            dpre_ref[rows, :] = dpre
            db8 = db8 + fold(dpre)
            for s in range(SSD_K):
                dw8[s] = dw8[s] + fold(dpre * taps_in[s])
        dpre_ref[L:L + 8, :] = jnp.zeros((8, cb), F32)
        db_ref[...] = jnp.sum(db8, axis=0, keepdims=True)
        for s in range(SSD_K):
            dw_ref[SSD_K - 1 - s:SSD_K - s, :] = jnp.sum(dw8[s], axis=0, keepdims=True)
        for r0 in range(0, L, CONV_ROWS):
            dx = taps[SSD_K - 1] * dpre_ref[r0:r0 + CONV_ROWS, :]
            for s in range(1, SSD_K):
                dx = dx + taps[SSD_K - 1 - s] * dpre_ref[r0 + s:r0 + s + CONV_ROWS, :]
            dx_ref[r0:r0 + CONV_ROWS, :] = dx.astype(BF16)

    def part_spec(i):
        return pl.BlockSpec((L, cb), lambda j: (0, jnp.clip(j - starts[i], 0, counts[i] - 1)))

    return _pcall(
        body, name=name, grid=(nb,),
        out_shape=[jax.ShapeDtypeStruct(dzx.shape, BF16), jax.ShapeDtypeStruct((SSD_K, width), F32),
                   jax.ShapeDtypeStruct((1, width), F32)],
        in_specs=[pl.BlockSpec((L, cb), lambda j: (0, off + j)),
                  pl.BlockSpec((SSD_K, cb), lambda j: (0, j)),
                  pl.BlockSpec((1, cb), lambda j: (0, j))]
        + [part_spec(i) for i in range(len(d_parts))] + [pl.BlockSpec(memory_space=pl.ANY)],
        out_specs=[pl.BlockSpec((L, cb), lambda j: (0, off + j)),
                   pl.BlockSpec((SSD_K, cb), lambda j: (0, j)),
                   pl.BlockSpec((1, cb), lambda j: (0, j))],
        input_output_aliases={3 + len(d_parts): 0},
        scratch_shapes=[pltpu.VMEM((L + 8, cb), F32)],
        compiler_params=_cparams(("parallel",)))(zx, w, b, *d_parts, dzx)


def _dzx_finish(dzx, ddt, col0, name, tl=512):
    G, L, _ = ddt.shape
    tail = dzx.shape[1] - col0
    tl = _tile(L, tl)

    def body(ddt_ref, dzx_ref, o_ref):
        s = ddt_ref[0]
        for g in range(1, G):
            s = s + ddt_ref[g]
        o_ref[:, 0:LANES] = s.astype(o_ref.dtype)
        if tail > LANES:
            o_ref[:, LANES:] = jnp.zeros((tl, tail - LANES), o_ref.dtype)

    return _pcall(
        body, name=name, grid=(L // tl,), out_shape=jax.ShapeDtypeStruct(dzx.shape, dzx.dtype),
        in_specs=[pl.BlockSpec((G, tl, LANES), lambda i: (0, i, 0)), pl.BlockSpec(memory_space=pl.ANY)],
        out_specs=pl.BlockSpec((tl, tail), lambda i: (i, col0 // tail)),
        input_output_aliases={1: 0},
        compiler_params=_cparams(("parallel",)))(ddt, dzx)


def _sc_conv_fwd(proj, w, name, cb=512):
    L = proj.shape[0]
    width = proj.shape[1] // 3
    nb = width // cb

    def body(b_ref, c_ref, x_ref, w_ref, o_ref):
        for l0 in range(0, cb, LANES):
            lanes = slice(l0, l0 + LANES)
            taps = [w_ref[k:k + 1, lanes] for k in range(SC_K)]
            for r0 in range(0, L, CONV_ROWS):
                rows = slice(r0, r0 + CONV_ROWS)
                q = [c * x for c, x in zip(_shifted_rows(c_ref, r0, SC_K, lanes),
                                           _shifted_rows(x_ref, r0, SC_K, lanes))]
                acc = taps[SC_K - 1] * q[0]
                for s in range(1, SC_K):
                    acc = acc + taps[SC_K - 1 - s] * q[s]
                o_ref[rows, lanes] = (b_ref[rows, lanes] * acc).astype(BF16)

    return _pcall(
        body, name=name, grid=(nb,), out_shape=jax.ShapeDtypeStruct((L, width), BF16),
        in_specs=[pl.BlockSpec((L, cb), lambda j: (0, j)),
                  pl.BlockSpec((L, cb), lambda j: (0, nb + j)),
                  pl.BlockSpec((L, cb), lambda j: (0, 2 * nb + j)),
                  pl.BlockSpec((SC_K, cb), lambda j: (0, j))],
        out_specs=pl.BlockSpec((L, cb), lambda j: (0, j)),
        compiler_params=_cparams(("parallel",)))(proj, proj, proj, w)


def _sc_conv_bwd(proj, w, dy, name, cb=128):
    L = proj.shape[0]
    width = proj.shape[1] // 3
    nb = width // cb

    def body(b_ref, c_ref, x_ref, w_ref, dy_ref, db_ref, dc_ref, dxv_ref, dw_ref, dconv_ref):
        taps = [w_ref[k:k + 1, :] for k in range(SC_K)]
        fold = lambda v: sum(v[r:r + 8, :] for r in range(0, CONV_ROWS, 8))
        dw8 = [jnp.zeros((8, cb), F32) for _ in range(SC_K)]
        for r0 in range(0, L, CONV_ROWS):
            rows = slice(r0, r0 + CONV_ROWS)
            q = [c * x for c, x in zip(_shifted_rows(c_ref, r0, SC_K), _shifted_rows(x_ref, r0, SC_K))]
            conv = taps[SC_K - 1] * q[0]
            for s in range(1, SC_K):
                conv = conv + taps[SC_K - 1 - s] * q[s]
            dyv = dy_ref[rows, :]
            db_ref[rows, :] = (dyv * conv).astype(BF16)
            dconv = dyv * b_ref[rows, :]
            dconv_ref[rows, :] = dconv
            for s in range(SC_K):
                dw8[s] = dw8[s] + fold(dconv * q[s])
        dconv_ref[L:L + 8, :] = jnp.zeros((8, cb), F32)
        for s in range(SC_K):
            dw_ref[SC_K - 1 - s:SC_K - s, :] = jnp.sum(dw8[s], axis=0, keepdims=True)
        for r0 in range(0, L, CONV_ROWS):
            rows = slice(r0, r0 + CONV_ROWS)
            dq = taps[SC_K - 1] * dconv_ref[rows, :]
            for s in range(1, SC_K):
                dq = dq + taps[SC_K - 1 - s] * dconv_ref[r0 + s:r0 + s + CONV_ROWS, :]
            dc_ref[rows, :] = (dq * x_ref[rows, :]).astype(BF16)
            dxv_ref[rows, :] = (dq * c_ref[rows, :]).astype(BF16)

    blk = pl.BlockSpec((L, cb), lambda j: (0, j))
    wblk = pl.BlockSpec((SC_K, cb), lambda j: (0, j))
    return _pcall(
        body, name=name, grid=(nb,),
        out_shape=[jax.ShapeDtypeStruct((L, width), BF16)] * 3 + [jax.ShapeDtypeStruct((SC_K, width), F32)],
        in_specs=[blk, pl.BlockSpec((L, cb), lambda j: (0, nb + j)),
                  pl.BlockSpec((L, cb), lambda j: (0, 2 * nb + j)), wblk, blk],
        out_specs=[blk, blk, blk, wblk], scratch_shapes=[pltpu.VMEM((L + 8, cb), F32)],
        compiler_params=_cparams(("parallel",)))(proj, proj, proj, w, dy)


def _split3(v):
    hi = v.astype(BF16)
    r1 = v - hi.astype(F32)
    mid = r1.astype(BF16)
    lo = (r1 - mid.astype(F32)).astype(BF16)
    return hi, mid, lo


def _dot_exact01(t01, v):
    hi, mid, lo = _split3(v)
    return _dot(t01, hi) + _dot(t01, mid) + _dot(t01, lo)


def _lane_col(v, lane, h):
    return jnp.sum(jnp.where(lane == h, v, 0.0), axis=1, keepdims=True)


def _sum_all(v):
    return jnp.sum(jnp.sum(v, axis=1, keepdims=True), axis=0, keepdims=True)


def _softplus(x):
    return jnp.maximum(x, 0.0) + jnp.log1p(jnp.exp(-jnp.abs(x)))


def _ssd_decay(zx, bias_p, alog_p, n_heads, dt_block, name):
    L = zx.shape[0]
    nc = L // SSD_CHUNK
    per_step = 4 if nc % 4 == 0 else 1
    rows_step = per_step * SSD_CHUNK

    def body(raw_ref, bias_ref, alog_ref, dt_ref, sg_ref, cs_ref, cst_ref, last_ref):
        lane = lax.broadcasted_iota(jnp.int32, (SSD_CHUNK, LANES), 1)
        row = lax.broadcasted_iota(jnp.int32, (SSD_CHUNK, LANES), 0)
        valid = lane < n_heads
        tri = (row >= lane).astype(BF16)
        a_row = -jnp.exp(alog_ref[...])
        for i in range(per_step):
            rows = slice(i * SSD_CHUNK, (i + 1) * SSD_CHUNK)
            raw = raw_ref[rows, :] + bias_ref[...]
            dt = jnp.where(valid, _softplus(raw), 0.0)
            a = dt * a_row
            cs = _dot_exact01(tri, a)
            dt_ref[rows, :] = dt
            sg_ref[rows, :] = _sigmoid(raw)
            cs_ref[rows, :] = cs
            cst_ref[i] = cs.T
            last_ref[i] = jnp.sum(a, axis=0, keepdims=True)

    blk = pl.BlockSpec((rows_step, LANES), lambda c: (c, 0))
    head_vec = pl.BlockSpec((1, LANES), lambda c: (0, 0))
    return _pcall(
        body, name=name, grid=(nc // per_step,),
        out_shape=[jax.ShapeDtypeStruct((L, LANES), F32)] * 3
        + [jax.ShapeDtypeStruct((nc, SSD_CHUNK, LANES), F32), jax.ShapeDtypeStruct((nc, 1, LANES), F32)],
        in_specs=[pl.BlockSpec((rows_step, LANES), lambda c: (c, dt_block)), head_vec, head_vec],
        out_specs=[blk, blk, blk, pl.BlockSpec((per_step, SSD_CHUNK, LANES), lambda c: (c, 0, 0)),
                   pl.BlockSpec((per_step, 1, LANES), lambda c: (c, 0, 0))],
        compiler_params=_cparams(("parallel",)))(zx, bias_p, alog_p)


def _ssd_common(dt_ref, cs_ref, last_ref, b_ref, c_ref):
    c_sz = SSD_CHUNK
    lane = lax.broadcasted_iota(jnp.int32, (c_sz, LANES), 1)
    row = lax.broadcasted_iota(jnp.int32, (c_sz, LANES), 0)
    bb = b_ref[...].astype(BF16)
    cb = c_ref[...].astype(BF16)
    scores = _dot(cb, bb, "nt")
    return dict(lane=lane, row=row, dt=dt_ref[...], cs=cs_ref[...], last_row=last_ref[...], bb=bb, cb=cb,
                scores=scores, causal=row >= lane, lo=lane < SSD_P)


def _pair_terms(q, cst_ref, h0):
    lane, lo = q["lane"], q["lo"]
    out = {}
    cols, dts, lasts, lms = [], [], [], []
    lane1 = lax.broadcasted_iota(jnp.int32, (1, LANES), 1)
    for h in (h0, h0 + 1):
        col = _lane_col(q["cs"], lane, h)
        rowv = cst_ref[pl.ds(h, 1), :]
        lms.append(jnp.exp(jnp.where(q["causal"], col - rowv, -1e30)))
        cols.append(col)
        dts.append(_lane_col(q["dt"], lane, h))
        lasts.append(jnp.sum(jnp.where(lane1 == h, q["last_row"], 0.0), axis=1, keepdims=True))
    out["lm"] = lms
    out["cols"] = cols
    out["lasts"] = lasts
    out["dt_b"] = jnp.where(lo, dts[0], dts[1])
    out["e_b"] = jnp.where(lo, jnp.exp(cols[0]), jnp.exp(cols[1]))
    out["dec_cols"] = [jnp.exp(lasts[0] - cols[0]), jnp.exp(lasts[1] - cols[1])]
    out["dec_b"] = jnp.where(lo, out["dec_cols"][0], out["dec_cols"][1])
    lo1 = lane1 < SSD_P
    out["explast"] = [jnp.exp(lasts[0]), jnp.exp(lasts[1])]
    out["explast_b"] = jnp.where(lo1, out["explast"][0], out["explast"][1])
    return out


def _ssd_fwd(zx, xc, decay, d_lane, nw, d_inner, after, name):
    L = zx.shape[0]
    nc = L // SSD_CHUNK
    gw = d_inner // SSD_G
    heads = gw // SSD_P
    n_pair = heads // 2
    bc0 = d_inner // LANES

    def body(z_ref, xs_ref, b_ref, c_ref, dt_ref, cs_ref, cst_ref, last_ref, dl_ref, nw_ref, after_ref,
             y_ref, yn_ref, prev_ref, s_ref):
        @pl.when(pl.program_id(1) == 0)
        def _():
            s_ref[...] = jnp.zeros_like(s_ref)

        q = _ssd_common(dt_ref, cs_ref, last_ref, b_ref, c_ref)
        prev_ref[...] = s_ref[...]
        lo = q["lo"]
        for j in range(n_pair):
            sl = slice(j * LANES, (j + 1) * LANES)
            p = _pair_terms(q, cst_ref, pl.program_id(0) * heads + 2 * j)
            xs_p = xs_ref[:, sl]
            xp = xs_p * p["dt_b"]
            xb = xp.astype(BF16)
            m_a = (q["scores"] * p["lm"][0]).astype(BF16)
            m_b = (q["scores"] * p["lm"][1]).astype(BF16)
            yd = jnp.where(lo, _dot(m_a, xb), _dot(m_b, xb))
            s_p = s_ref[:, sl]
            yo = _dot(q["cb"], s_p.astype(BF16)) * p["e_b"]
            y_ref[:, sl] = yd + yo + dl_ref[:, sl] * xs_p
            st = _dot(q["bb"], (xp * p["dec_b"]).astype(BF16), "tn")
            s_ref[:, sl] = s_p * p["explast_b"] + st
        yv = y_ref[...]
        zv = z_ref[...]
        yg = yv * (zv * _sigmoid(zv))
        rstd = lax.rsqrt(jnp.mean(yg * yg, axis=-1, keepdims=True) + NORM_EPS)
        yn_ref[...] = (yg * rstd * nw_ref[...]).astype(BF16)

    grp = lambda width: pl.BlockSpec((None, 1, width), lambda g, c: (g, 0, 0))
    dt_, _, cs_, cst_, last_ = decay
    return _pcall(
        body, name=name, grid=(SSD_G, nc),
        out_shape=[jax.ShapeDtypeStruct((L, d_inner), F32), jax.ShapeDtypeStruct((L, d_inner), BF16),
                   jax.ShapeDtypeStruct((nc, SSD_G, SSD_N, gw), F32)],
        in_specs=[pl.BlockSpec((SSD_CHUNK, gw), lambda g, c: (c, g)),
                  pl.BlockSpec((SSD_CHUNK, gw), lambda g, c: (c, g)),
                  pl.BlockSpec((SSD_CHUNK, SSD_N), lambda g, c: (c, bc0 + g)),
                  pl.BlockSpec((SSD_CHUNK, SSD_N), lambda g, c: (c, bc0 + SSD_G + g)),
                  pl.BlockSpec((SSD_CHUNK, LANES), lambda g, c: (c, 0)),
                  pl.BlockSpec((SSD_CHUNK, LANES), lambda g, c: (c, 0)),
                  pl.BlockSpec((None, SSD_CHUNK, LANES), lambda g, c: (c, 0, 0)),
                  pl.BlockSpec((None, 1, LANES), lambda g, c: (c, 0, 0)),
                  grp(gw), grp(gw), pl.BlockSpec(memory_space=pl.ANY)],
        out_specs=[pl.BlockSpec((SSD_CHUNK, gw), lambda g, c: (c, g)),
                   pl.BlockSpec((SSD_CHUNK, gw), lambda g, c: (c, g)),
                   pl.BlockSpec((None, None, SSD_N, gw), lambda g, c: (c, g, 0, 0))],
        scratch_shapes=[pltpu.VMEM((SSD_N, gw), F32)],
        compiler_params=_cparams(("parallel", "arbitrary")))(
            zx, xc, xc, xc, dt_, cs_, cst_, last_, d_lane, nw, after)


def _ssd_bwd(dyn, y, zx, xc, prev, decay, alog_p, d_lane, nw, d_inner, name):
    L = zx.shape[0]
    nc = L // SSD_CHUNK
    gw = d_inner // SSD_G
    heads = gw // SSD_P
    n_pair = heads // 2
    bc0 = d_inner // LANES

    def body(dyn_ref, y_ref, z_ref, xs_ref, b_ref, c_ref, prev_ref, dt_ref, sg_ref, cs_ref, cst_ref, last_ref,
             alog_ref, dl_ref, nw_ref,
             dz_ref, dxs_ref, db_ref, dc_ref, ddt_ref, dbias_ref, dalog_ref, dd_ref, dnw_ref,
             ds_ref, racc_ref):
        @pl.when(pl.program_id(1) == 0)
        def _():
            ds_ref[...] = jnp.zeros_like(ds_ref)
            dbias_ref[...] = jnp.zeros_like(dbias_ref)
            dalog_ref[...] = jnp.zeros_like(dalog_ref)
            dd_ref[...] = jnp.zeros_like(dd_ref)
            dnw_ref[...] = jnp.zeros_like(dnw_ref)

        q = _ssd_common(dt_ref, cs_ref, last_ref, b_ref, c_ref)
        a_row = -jnp.exp(alog_ref[...])
        lane, row, lo = q["lane"], q["row"], q["lo"]
        lane1 = lax.broadcasted_iota(jnp.int32, (1, LANES), 1)
        head0 = pl.program_id(0) * heads
        mine = (lane >= head0) & (lane < head0 + heads)

        yv, zv, dynv, nwv = y_ref[...], z_ref[...], dyn_ref[...], nw_ref[...]
        sig = _sigmoid(zv)
        sz = zv * sig
        yg = yv * sz
        rstd = lax.rsqrt(jnp.mean(yg * yg, axis=-1, keepdims=True) + NORM_EPS)
        yhat = yg * rstd
        dnw_ref[...] += jnp.sum(dynv * yhat, axis=0, keepdims=True)
        dyh = dynv * nwv
        dyg = rstd * (dyh - yhat * jnp.mean(dyh * yhat, axis=-1, keepdims=True))
        dz_ref[...] = (dyg * yv * (sig * (1.0 + zv * (1.0 - sig)))).astype(BF16)
        dy_all = dyg * sz

        dg = jnp.zeros((SSD_CHUNK, SSD_CHUNK), F32)
        dc_acc = jnp.zeros((SSD_CHUNK, SSD_N), F32)
        db_acc = jnp.zeros((SSD_CHUNK, SSD_N), F32)
        dcs_mat = jnp.zeros((SSD_CHUNK, LANES), F32)
        ddt_mat = jnp.zeros((SSD_CHUNK, LANES), F32)
        dd_row = jnp.zeros((1, LANES), F32)
        racc_ref[...] = jnp.zeros_like(racc_ref)
        is_last = row == SSD_CHUNK - 1

        for j in range(n_pair):
            sl = slice(j * LANES, (j + 1) * LANES)
            ha, hb = head0 + 2 * j, head0 + 2 * j + 1
            p = _pair_terms(q, cst_ref, ha)
            xs_p = xs_ref[:, sl]
            dyp = dy_all[:, sl]
            xp = xs_p * p["dt_b"]
            xb = xp.astype(BF16)
            s_p = prev_ref[:, sl]
            s_pb = s_p.astype(BF16)
            dsn = ds_ref[:, sl]
            dsnb = dsn.astype(BF16)
            m_f = [q["scores"] * p["lm"][0], q["scores"] * p["lm"][1]]

            t0 = dyp * xs_p
            dd_row = dd_row + jnp.where(lane1 == ha, _sum_all(jnp.where(lo, t0, 0.0)), 0.0) \
                + jnp.where(lane1 == hb, _sum_all(jnp.where(lo, 0.0, t0)), 0.0)
            dxs_p = dl_ref[:, sl] * dyp

            yo = _dot(q["cb"], s_pb) * p["e_b"]
            dcs_b = (dyp * p["e_b"]).astype(BF16)
            dc_acc = dc_acc + _dot(dcs_b, s_pb, "nt")
            ds_yo = _dot(q["cb"], dcs_b, "tn")
            t1 = dyp * yo
            dcs_cols = [jnp.sum(jnp.where(lo, t1, 0.0), axis=1, keepdims=True),
                        jnp.sum(jnp.where(lo, 0.0, t1), axis=1, keepdims=True)]

            t2 = dsn * s_p
            dlast = [p["explast"][0] * _sum_all(jnp.where(lo, t2, 0.0)),
                     p["explast"][1] * _sum_all(jnp.where(lo, 0.0, t2))]
            ds_ref[:, sl] = dsn * p["explast_b"] + ds_yo
            w = _dot(q["bb"], dsnb)
            db_acc = db_acc + _dot((xp * p["dec_b"]).astype(BF16), dsnb, "nt")
            dxp = w * p["dec_b"]
            t3 = w * xp
            e = [jnp.sum(jnp.where(lo, t3, 0.0), axis=1, keepdims=True) * p["dec_cols"][0],
                 jnp.sum(jnp.where(lo, 0.0, t3), axis=1, keepdims=True) * p["dec_cols"][1]]
            for i in range(2):
                dlast[i] = dlast[i] + jnp.sum(e[i], axis=0, keepdims=True)
                dcs_cols[i] = dcs_cols[i] - e[i]

            dyb = dyp.astype(BF16)
            dy_h = [jnp.where(lo, dyp, 0.0).astype(BF16), jnp.where(lo, 0.0, dyp).astype(BF16)]
            dms = [_dot(dy_h[0], xb, "nt"), _dot(dy_h[1], xb, "nt")]
            dxp = dxp + jnp.where(lo, _dot(m_f[0].astype(BF16), dyb, "tn"), _dot(m_f[1].astype(BF16), dyb, "tn"))
            for i, h in enumerate((ha, hb)):
                dg = dg + dms[i] * p["lm"][i]
                qm = dms[i] * m_f[i]
                dcs_cols[i] = dcs_cols[i] + jnp.sum(qm, axis=1, keepdims=True)
                racc_ref[pl.ds(h, 1), :] = jnp.sum(qm, axis=0, keepdims=True)

            dxs_ref[:, sl] = dxs_p + dxp * p["dt_b"]
            t4 = dxp * xs_p
            ddt_cols = [jnp.sum(jnp.where(lo, t4, 0.0), axis=1, keepdims=True),
                        jnp.sum(jnp.where(lo, 0.0, t4), axis=1, keepdims=True)]
            for i, h in enumerate((ha, hb)):
                sel = lane == h
                dcs_mat = dcs_mat + jnp.where(sel, dcs_cols[i], 0.0) + jnp.where(sel & is_last, dlast[i], 0.0)
                ddt_mat = ddt_mat + jnp.where(sel, ddt_cols[i], 0.0)

        dcs_mat = dcs_mat - racc_ref[...].T
        tri_t = (row <= lane).astype(BF16)
        da = _dot_exact01(tri_t, dcs_mat)
        ddt = ddt_mat + da * a_row
        dalog_ref[...] += jnp.sum(jnp.where(mine, da * q["dt"], 0.0), axis=0, keepdims=True) * a_row
        draw = jnp.where(mine, ddt * sg_ref[...], 0.0)
        ddt_ref[...] = draw
        dbias_ref[...] += jnp.sum(draw, axis=0, keepdims=True)
        dd_ref[...] += dd_row
        dgb = dg.astype(BF16)
        dc_ref[...] = dc_acc + _dot(dgb, q["bb"])
        db_ref[...] = db_acc + _dot(dgb, q["cb"], "tn")

    rev = lambda c: nc - 1 - c
    grp = lambda width: pl.BlockSpec((None, 1, width), lambda g, c: (g, 0, 0))
    blk = lambda width, off: pl.BlockSpec((SSD_CHUNK, width), lambda g, c: (rev(c), off + g))
    head_vec = pl.BlockSpec((1, LANES), lambda g, c: (0, 0))
    chunk_rows = pl.BlockSpec((SSD_CHUNK, LANES), lambda g, c: (rev(c), 0))
    dt_, sg_, cs_, cst_, last_ = decay
    return _pcall(
        body, name=name, grid=(SSD_G, nc),
        out_shape=[jax.ShapeDtypeStruct(zx.shape, BF16), jax.ShapeDtypeStruct((L, d_inner), F32),
                   jax.ShapeDtypeStruct((L, SSD_G * SSD_N), F32), jax.ShapeDtypeStruct((L, SSD_G * SSD_N), F32),
                   jax.ShapeDtypeStruct((SSD_G, L, LANES), F32),
                   jax.ShapeDtypeStruct((SSD_G, 1, LANES), F32), jax.ShapeDtypeStruct((SSD_G, 1, LANES), F32),
                   jax.ShapeDtypeStruct((SSD_G, 1, LANES), F32), jax.ShapeDtypeStruct((SSD_G, 1, gw), F32)],
        in_specs=[blk(gw, 0), blk(gw, 0), blk(gw, 0), blk(gw, 0), blk(SSD_N, bc0), blk(SSD_N, bc0 + SSD_G),
                  pl.BlockSpec((None, None, SSD_N, gw), lambda g, c: (rev(c), g, 0, 0)),
                  chunk_rows, chunk_rows, chunk_rows,
                  pl.BlockSpec((None, SSD_CHUNK, LANES), lambda g, c: (rev(c), 0, 0)),
                  pl.BlockSpec((None, 1, LANES), lambda g, c: (rev(c), 0, 0)),
                  head_vec, grp(gw), grp(gw)],
        out_specs=[blk(gw, 0), blk(gw, 0), blk(SSD_N, 0), blk(SSD_N, 0),
                   pl.BlockSpec((None, SSD_CHUNK, LANES), lambda g, c: (g, rev(c), 0)),
                   grp(LANES), grp(LANES), grp(LANES), grp(gw)],
        scratch_shapes=[pltpu.VMEM((SSD_N, gw), F32), pltpu.VMEM((SSD_CHUNK, LANES), F32)],
        compiler_params=_cparams(("parallel", "arbitrary")))(
            dyn, y, zx, xc, xc, xc, prev, dt_, sg_, cs_, cst_, last_, alog_p, d_lane, nw)


def _cond_mod(c_pad, ada_w, ada_b_loc, after, name):
    depth, D, n = ada_w.shape
    rows = c_pad.shape[0]

    def body(c_ref, w_ref, b_ref, after_ref, mod_ref, cond_ref):
        cv = c_ref[...]
        cond = cv * _sigmoid(cv)
        cond_ref[...] = cond
        mod_ref[...] = _dot(cond.astype(BF16), w_ref[...].astype(BF16)) + b_ref[...]

    return _pcall(
        body, name=name, grid=(depth,),
        out_shape=[jax.ShapeDtypeStruct((depth, rows, n), F32), jax.ShapeDtypeStruct((rows, D), F32)],
        in_specs=[pl.BlockSpec((rows, D), lambda i: (0, 0)),
                  pl.BlockSpec((None, D, n), lambda i: (i, 0, 0)),
                  pl.BlockSpec((None, 1, n), lambda i: (i, 0, 0)),
                  pl.BlockSpec(memory_space=pl.ANY)],
        out_specs=[pl.BlockSpec((None, rows, n), lambda i: (i, 0, 0)),
                   pl.BlockSpec((rows, D), lambda i: (0, 0))],
        compiler_params=_cparams(("arbitrary",)))(c_pad, ada_w, ada_b_loc, after)


def _adamw_math(g, w, m, v):
    m_new = ADAM_B1 * m + (1.0 - ADAM_B1) * g
    v_new = ADAM_B2 * v + (1.0 - ADAM_B2) * (g * g)
    m_hat = m_new / (1.0 - ADAM_B1 ** ADAM_STEP)
    v_hat = v_new / (1.0 - ADAM_B2 ** ADAM_STEP)
    delta = -ADAM_LR * (m_hat / (jnp.sqrt(v_hat) + ADAM_EPS) + ADAM_WD * w)
    return delta, m_new, v_new


def _adamw_sum(parts, w, m, v, layer, name, prev=None, tr=None, window_off=None):
    depth, R, C = w.shape
    tr = _tile(R, tr if tr is not None else (512 if C <= 512 else 256))
    win = parts.shape[2]
    scratch = [] if window_off is None else [pltpu.VMEM((tr, win), F32)]

    def body(p_ref, w_ref, m_ref, v_ref, *rest):
        g_ref, d_ref, mo_ref, vo_ref = rest[-4 - len(scratch):len(rest) - len(scratch)]
        g = p_ref[0].astype(F32)
        for k in range(1, N_DEV):
            g = g + p_ref[k].astype(F32)
        if window_off is not None:
            me = _my_index()
            off = 0
            for k in range(N_DEV):
                off = jnp.where(me == k, window_off[k], off)
            src = lax.broadcasted_iota(jnp.int32, (win, win), 0)
            dst = lax.broadcasted_iota(jnp.int32, (win, win), 1)
            shift = ((src == dst + off) & (dst < C)).astype(BF16)
            hi, mid, lo = _split3(g)
            rest[-1][...] = _dot(hi, shift) + _dot(mid, shift) + _dot(lo, shift)
            g = rest[-1][:, 0:C]
        d, mn, vn = _adamw_math(g, w_ref[...], m_ref[...], v_ref[...])
        g_ref[...] = g
        d_ref[...] = d
        mo_ref[...] = mn
        vo_ref[...] = vn

    blk = pl.BlockSpec((None, tr, C), lambda i: (layer, i, 0))
    prev = list(prev) if prev is not None else []
    return _pcall(
        body, name=name, grid=(R // tr,),
        out_shape=[jax.ShapeDtypeStruct((depth, R, C), F32)] * 4,
        in_specs=[pl.BlockSpec((N_DEV, tr, win), lambda i: (0, i, 0)), blk, blk, blk]
        + [pl.BlockSpec(memory_space=pl.ANY)] * len(prev),
        out_specs=[blk] * 4, input_output_aliases={4 + k: k for k in range(len(prev))},
        scratch_shapes=scratch,
        compiler_params=_cparams(("parallel",)))(parts, w, m, v, *prev)


def _adamw_small(parts, wmv, head_parts, head_wmv, loss_parts, name):
    n, nh = len(parts), len(head_parts)
    n_heads = head_wmv[0][0].shape[1] if nh else 0
    groups = head_parts[0].shape[1] if nh else 0
    d_model = loss_parts.shape[2]

    def body(*refs):
        p_refs, refs = refs[:n], refs[n:]
        wmv_refs, refs = refs[:3 * n], refs[3 * n:]
        hp_refs, refs = refs[:nh], refs[nh:]
        hwmv_refs, refs = refs[:3 * nh], refs[3 * nh:]
        loss_ref, refs = refs[0], refs[1:]
        outs, loss_out, head_scr = refs[:4 * (n + nh)], refs[4 * (n + nh)], refs[4 * (n + nh) + 1]

        def update(i, g, w_ref, m_ref, v_ref):
            res = (g,) + _adamw_math(g, w_ref[...], m_ref[...], v_ref[...])
            for o_ref, r in zip(outs[4 * i:4 * i + 4], res):
                o_ref[...] = r

        for i in range(n):
            g = p_refs[i][0]
            for k in range(1, N_DEV):
                g = g + p_refs[i][k]
            update(i, g, *wmv_refs[3 * i:3 * i + 3])
        for i in range(nh):
            g = None
            for k in range(N_DEV):
                for grp in range(groups):
                    g = hp_refs[i][k, grp] if g is None else g + hp_refs[i][k, grp]
            head_scr[...] = g
            update(n + i, head_scr[:, 0:n_heads], *hwmv_refs[3 * i:3 * i + 3])
        tot = loss_ref[0]
        for k in range(1, N_DEV):
            tot = tot + loss_ref[k]
        loss_out[...] = jnp.broadcast_to(_sum_all(tot) * (0.5 / d_model), loss_out.shape)

    operands = list(parts) + [a for t in wmv for a in t] + list(head_parts) + [a for t in head_wmv for a in t]
    operands.append(loss_parts)
    out_shape = [jax.ShapeDtypeStruct(t[0].shape, F32) for t in list(wmv) + list(head_wmv) for _ in range(4)]
    out_shape.append(jax.ShapeDtypeStruct((1, LANES), F32))
    vmem = pl.BlockSpec(memory_space=pltpu.VMEM)
    outs = _pcall(body, name=name, out_shape=out_shape, in_specs=[vmem] * len(operands),
                  out_specs=[vmem] * len(out_shape), scratch_shapes=[pltpu.VMEM((1, LANES), F32)],
                  compiler_params=_cparams())(*operands)
    return [outs[4 * i:4 * i + 4] for i in range(n + nh)], outs[-1]


def _ada_adamw(cond_pad, dmod_pad, w, m, v, name, tr=512):
    depth, D, n = w.shape
    rows = cond_pad.shape[0]
    tr = _tile(D, tr)

    def body(c_ref, dm_ref, w_ref, m_ref, v_ref, g_ref, d_ref, mo_ref, vo_ref):
        g = _dot(c_ref[...].astype(BF16), dm_ref[...].astype(BF16), "tn")
        d, mn, vn = _adamw_math(g, w_ref[...], m_ref[...], v_ref[...])
        g_ref[...] = g
        d_ref[...] = d
        mo_ref[...] = mn
        vo_ref[...] = vn

    blk = pl.BlockSpec((None, tr, n), lambda i, r: (i, r, 0))
    return _pcall(
        body, name=name, grid=(depth, D // tr),
        out_shape=[jax.ShapeDtypeStruct((depth, D, n), F32)] * 4,
        in_specs=[pl.BlockSpec((rows, tr), lambda i, r: (0, r)),
                  pl.BlockSpec((None, rows, n), lambda i, r: (i, 0, 0)), blk, blk, blk],
        out_specs=[blk] * 4, compiler_params=_cparams(("parallel", "parallel")))(cond_pad, dmod_pad, w, m, v)


def kernel(x, c, ada_w, ada_b, mix_norm_w, mlp_norm_w, mlp_up, mlp_down, ssd_in_w, ssd_conv_w, ssd_conv_b, ssd_dt_bias, ssd_A_log, ssd_D, ssd_norm_w, ssd_out_w, sc_in_w, sc_conv_w, sc_out_w, final_norm_w, loss_target, m_ada_w, m_ada_b, m_mix_norm_w, m_mlp_norm_w, m_mlp_up, m_mlp_down, m_ssd_in_w, m_ssd_conv_w, m_ssd_conv_b, m_ssd_dt_bias, m_ssd_A_log, m_ssd_D, m_ssd_norm_w, m_ssd_out_w, m_sc_in_w, m_sc_conv_w, m_sc_out_w, m_final_norm_w, v_ada_w, v_ada_b, v_mix_norm_w, v_mlp_norm_w, v_mlp_up, v_mlp_down, v_ssd_in_w, v_ssd_conv_w, v_ssd_conv_b, v_ssd_dt_bias, v_ssd_A_log, v_ssd_D, v_ssd_norm_w, v_ssd_out_w, v_sc_in_w, v_sc_conv_w, v_sc_out_w, v_final_norm_w):
    weights = dict(ada_w=ada_w, ada_b=ada_b, mix_norm_w=mix_norm_w, mlp_norm_w=mlp_norm_w, mlp_up=mlp_up,
                   mlp_down=mlp_down, ssd_in_w=ssd_in_w, ssd_conv_w=ssd_conv_w, ssd_conv_b=ssd_conv_b,
                   ssd_dt_bias=ssd_dt_bias, ssd_A_log=ssd_A_log, ssd_D=ssd_D, ssd_norm_w=ssd_norm_w,
                   ssd_out_w=ssd_out_w, sc_in_w=sc_in_w, sc_conv_w=sc_conv_w, sc_out_w=sc_out_w,
                   final_norm_w=final_norm_w)
    moms = dict(ada_w=m_ada_w, ada_b=m_ada_b, mix_norm_w=m_mix_norm_w, mlp_norm_w=m_mlp_norm_w, mlp_up=m_mlp_up,
                mlp_down=m_mlp_down, ssd_in_w=m_ssd_in_w, ssd_conv_w=m_ssd_conv_w, ssd_conv_b=m_ssd_conv_b,
                ssd_dt_bias=m_ssd_dt_bias, ssd_A_log=m_ssd_A_log, ssd_D=m_ssd_D, ssd_norm_w=m_ssd_norm_w,
                ssd_out_w=m_ssd_out_w, sc_in_w=m_sc_in_w, sc_conv_w=m_sc_conv_w, sc_out_w=m_sc_out_w,
                final_norm_w=m_final_norm_w)
    vars_ = dict(ada_w=v_ada_w, ada_b=v_ada_b, mix_norm_w=v_mix_norm_w, mlp_norm_w=v_mlp_norm_w, mlp_up=v_mlp_up,
                 mlp_down=v_mlp_down, ssd_in_w=v_ssd_in_w, ssd_conv_w=v_ssd_conv_w, ssd_conv_b=v_ssd_conv_b,
                 ssd_dt_bias=v_ssd_dt_bias, ssd_A_log=v_ssd_A_log, ssd_D=v_ssd_D, ssd_norm_w=v_ssd_norm_w,
                 ssd_out_w=v_ssd_out_w, sc_in_w=v_sc_in_w, sc_conv_w=v_sc_conv_w, sc_out_w=v_sc_out_w,
                 final_norm_w=v_final_norm_w)
    names = list(weights)

    L, D = x.shape[1], x.shape[2]
    d_inner = 2 * D
    n_heads = d_inner // SSD_P
    hpg = n_heads // SSD_G
    gw = d_inner // SSD_G
    conv_dim = d_inner + 2 * SSD_G * SSD_N
    zx_dim = d_inner + conv_dim
    zx_pad = -(-(zx_dim + LANES) // 512) * 512
    in_ws = ssd_in_w.shape[2]
    in_base, in_off, in_win = _window_geometry(in_ws)
    me = _my_index()
    x0 = x[0]
    tgt = loss_target[0]

    n_mod = ada_w.shape[2]
    (c_all,) = _exchange([c], "gather_c", gather=True)
    gather_handle = {}
    (gather_handle["ssd_in_w"],), token_in = _xfer_start(
        [ssd_in_w[0].astype(BF16)], "gather_start_ssd_in_w", gather=True, via_sibling=(0,), after=(c_all,))
    c_pad = jnp.pad(c_all.reshape(N_DEV, D), ((0, 16 - N_DEV), (0, 0)))
    ada_b_loc = lax.dynamic_slice_in_dim(ada_b, me * n_mod, n_mod, axis=1).reshape(2, 1, n_mod)
    mod_blk, cond_pad = _cond_mod(c_pad, ada_w, ada_b_loc, token_in, "cond_mod")
    gather_order = ["mod", "ssd_conv_w", "sc_conv_w", "ssd_out_w", "up0", "down0", "sc_in_w", "sc_out_w", "up1",
                    "down1"]
    gather_src = dict(mod=mod_blk, ssd_conv_w=ssd_conv_w[0], sc_conv_w=sc_conv_w[0],
                      ssd_out_w=ssd_out_w[0].astype(BF16),
                      up0=mlp_up[0].astype(BF16), down0=mlp_down[0].astype(BF16),
                      sc_in_w=sc_in_w[0].astype(BF16), sc_out_w=sc_out_w[0].astype(BF16),
                      up1=mlp_up[1].astype(BF16), down1=mlp_down[1].astype(BF16))
    handles, gather_token = _xfer_start([gather_src[k] for k in gather_order], "gather_start", gather=True,
                                        via_sibling=tuple(range(3, len(gather_order))))
    gather_handle.update(zip(gather_order, handles))

    def gathered(keys, after, forward):
        tag = "_".join(keys)
        lands = _xfer_wait([gather_handle[k] for k in keys], after, f"gather_wait_{tag}", gather=True)
        return _sibling_forward(lands, f"gather_forward_{tag}") if forward else lands

    def forward_behind(keys, after):
        tag = "_".join(keys)
        lands = _xfer_wait([gather_handle[k] for k in keys], after, f"gather_wait_{tag}", gather=True)
        fwd_handles, token = _sibling_forward_start(lands, f"gather_forward_start_{tag}")
        return (lambda done: _sibling_forward_wait(fwd_handles, done, f"gather_forward_wait_{tag}")), token

    (ssd_in_g,) = gathered(["ssd_in_w"], (gather_token, m_ssd_in_w, v_ssd_in_w), True)
    w_in_all = _shards_to_columns(ssd_in_g, in_base, in_off, in_win, zx_pad, "ssd_in_w_columns")
    (mod_all,) = gathered(["mod"], w_in_all, False)
    mod_mine = lax.dynamic_index_in_dim(mod_all, me, axis=2, keepdims=False)
    mod_mine = jnp.transpose(mod_mine, (1, 0, 2)).reshape(2, 6, 1, D)
    sh_m, sc_m, g_m, sh_f, sc_f, g_f = [[mod_mine[i, k] for i in range(2)] for k in range(6)]

    vec = lambda a: a.reshape(1, -1)
    small = {}

    _, h0 = _norm_mod_fwd(x0, None, None, vec(mix_norm_w[0]), sc_m[0], sh_m[0], "l0_mix_norm")
    cw_all, scw_all = gathered(["ssd_conv_w", "sc_conv_w"], h0, False)
    (zx,) = _mm_nn(h0, w_in_all, F32, "ssd_in_proj", tm=2048, tn=512)
    conv_b0 = vec(ssd_conv_b[0])
    conv_w_full = jnp.transpose(cw_all, (1, 0, 2)).reshape(SSD_K, conv_dim)
    sc_conv_full = jnp.transpose(scw_all, (1, 0, 2)).reshape(SC_K, D)
    xc = _ssd_conv_fwd(zx, conv_w_full, conv_b0, d_inner, conv_dim, "ssd_conv")
    bias_p = jnp.pad(ssd_dt_bias[0], (0, LANES - n_heads)).reshape(1, LANES)
    alog_p = jnp.pad(ssd_A_log[0], (0, LANES - n_heads)).reshape(1, LANES)
    d_lane = jnp.repeat(ssd_D[0], SSD_P).reshape(SSD_G, 1, gw)
    nw_g = ssd_norm_w[0].reshape(SSD_G, 1, gw)
    finish, token = forward_behind(["ssd_out_w"], xc)
    decay = _ssd_decay(zx, bias_p, alog_p, n_heads, zx_dim // LANES, "ssd_decay")
    y_ssd, yn, prev = _ssd_fwd(zx, xc, decay, d_lane, nw_g, d_inner, token, "ssd_scan")
    ups, downs = [None, None], [None, None]
    (ssd_out_g,) = finish(yn)
    w_ssd_out = ssd_out_g.reshape(-1, D)
    finish, token = forward_behind(["up0", "down0"], ssd_out_g)
    (mix0,) = _mm_nn(yn, w_ssd_out, F32, "ssd_out_proj", after=(token,))
    x1, h1 = _norm_mod_fwd(x0, mix0, g_m[0], vec(mlp_norm_w[0]), sc_f[0], sh_f[0], "l0_mlp_norm")
    ups[0], down0_g = finish(h1)
    downs[0] = down0_g.reshape(-1, D)
    u0, s0 = _mm_nn_blocked(h1, ups[0], "l0_mlp_up", _ep_relu2, [BF16, BF16])
    finish, token = forward_behind(["sc_in_w", "sc_out_w", "up1", "down1"], s0)
    (d0,) = _mm_nn(s0, downs[0], F32, "l0_mlp_down", after=(token,))
    x2, h2 = _norm_mod_fwd(x1, d0, g_f[0], vec(mix_norm_w[1]), sc_m[1], sh_m[1], "l1_mix_norm")
    sc_in_g, sc_out_g, ups[1], down1_g = finish(h2)
    w_sc_out, downs[1] = sc_out_g.reshape(-1, D), down1_g.reshape(-1, D)
    (proj,) = _mm_nn_blocked(h2, sc_in_g, "sc_in_proj", _ep_store(F32), [F32])
    yc = _sc_conv_fwd(proj, sc_conv_full, "sc_conv")
    (mix1,) = _mm_nn(yc, w_sc_out, F32, "sc_out_proj")
    x3, h3 = _norm_mod_fwd(x2, mix1, g_m[1], vec(mlp_norm_w[1]), sc_f[1], sh_f[1], "l1_mlp_norm")
    u1, s1 = _mm_nn_blocked(h3, ups[1], "l1_mlp_up", _ep_relu2, [BF16, BF16])
    (d1,) = _mm_nn(s1, downs[1], F32, "l1_mlp_down")

    dx, loss_lane, dfw, dd1, dg = _final_loss(x3, d1, g_f[1], vec(final_norm_w), tgt, "final_loss")
    small["final_norm_w"] = dfw

    dmod = [[None] * 6 for _ in range(2)]
    dmod[1][5] = dg

    def mlp_backward(i, dx_out, dd, x_mid, h_in, u, s, mix, gate):
        du = _mm_nt(dd, downs[i], BF16, f"l{i}_mlp_down_bwd", epilogue=_ep_relu2_bwd, extra=(u,))
        gdown = _mm_tn(s, dd, BF16, f"l{i}_mlp_down_wgrad").reshape(N_DEV, -1, D)
        gup = _mm_tn_blocked(h_in, du, BF16, f"l{i}_mlp_up_wgrad")
        (h_down, h_up), token = _xfer_start([gdown, gup], f"l{i}_mlp_grads_start", gather=False)
        grad_handle[f"mlp_down{i}"], grad_handle[f"mlp_up{i}"] = h_down, h_up
        dh = _mm_nt_blocked(du, ups[i], F32, f"l{i}_mlp_up_bwd", after=(token,))
        dxm, dsh, dsc, dnw, dmix, dgate = _norm_mod_bwd(dh, x_mid, vec(mlp_norm_w[i]), sc_f[i], dx_out,
                                                        f"l{i}_mlp_norm_bwd", branch=(mix, gate))
        dmod[i][3], dmod[i][4], dmod[i][2] = dsh, dsc, dgate
        return dxm, dmix, dnw

    grad_handle = {}
    dx3, dyc, dnw_mlp1 = mlp_backward(1, dx, dd1, x3, h3, u1, s1, mix1, g_m[1])
    g_sc_out = _mm_tn(yc, dyc, BF16, "sc_out_wgrad").reshape(N_DEV, -1, D)
    dconv_out = _mm_nt(dyc, w_sc_out, F32, "sc_out_bwd")
    dbg, dcg, dxv, dscw = _sc_conv_bwd(proj, sc_conv_full, dconv_out, "sc_conv_bwd")
    dproj = jnp.concatenate([dbg, dcg, dxv], axis=1)
    g_sc_in = _mm_tn_blocked(h2, dproj, BF16, "sc_in_wgrad")
    (grad_handle["sc_out_w0"], grad_handle["sc_in_w0"]), token = _xfer_start(
        [g_sc_out, g_sc_in], "sc_grads_start", gather=False)
    dh2 = _mm_nt_blocked(dproj, sc_in_g, F32, "sc_in_bwd", after=(token,))
    dx2, dsh, dsc, dnw_mix1, dd0, dg = _norm_mod_bwd(dh2, x2, vec(mix_norm_w[1]), sc_m[1], dx3, "l1_mix_norm_bwd",
                                                     branch=(d0, g_f[0]))
    dmod[1][0], dmod[1][1], dmod[0][5] = dsh, dsc, dg
    dx1, dyo, dnw_mlp0 = mlp_backward(0, dx2, dd0, x1, h1, u0, s0, mix0, g_m[0])
    g_ssd_out = _mm_tn(yn, dyo, BF16, "ssd_out_wgrad").reshape(N_DEV, -1, D)
    (grad_handle["ssd_out_w0"],), token = _xfer_start([g_ssd_out], "ssd_out_grad_start", gather=False)
    dyn = _mm_nt(dyo, w_ssd_out, F32, "ssd_out_bwd", after=(token,))
    dz, dxs, db_, dc_, ddt, dbias, dalog, dd_, dnw_ssd = _ssd_bwd(
        dyn, y_ssd, zx, xc, prev, decay, alog_p, d_lane, nw_g, d_inner, "ssd_scan_bwd")
    dzx, dcw, dcb = _ssd_conv_bwd(zx, conv_w_full, conv_b0, [dxs, db_, dc_], dz, d_inner, "ssd_conv_bwd")
    dzx = _dzx_finish(dzx, ddt, zx_dim, "ssd_dzx_finish")
    g_in_all = _mm_tn(h0, dzx, BF16, "ssd_in_wgrad", tn=512, tk=2048)
    g_ssd_in = jnp.stack([g_in_all[:, b:b + in_win] for b in in_base], axis=0)
    (grad_handle["ssd_in_w0"],), token = _xfer_start([g_ssd_in], "ssd_in_grad_start", gather=False)
    dh0 = _mm_nt(dzx, w_in_all, F32, "ssd_in_bwd", tm=1024, tk=dzx.shape[1] // 2, after=(token,))
    grad_x, dsh, dsc, dnw_mix0 = _norm_mod_bwd(dh0, x0, vec(mix_norm_w[0]), sc_m[0], dx1, "l0_mix_norm_bwd")
    dmod[0][0], dmod[0][1] = dsh, dsc

    small["ada_b"] = jnp.concatenate([jnp.concatenate(dmod[i], axis=1) for i in range(2)], axis=0)
    small["mix_norm_w"] = jnp.concatenate([dnw_mix0, dnw_mix1], axis=0)
    small["mlp_norm_w"] = jnp.concatenate([dnw_mlp0, dnw_mlp1], axis=0)
    small["ssd_conv_w"] = dcw
    small["ssd_conv_b"] = dcb
    small["ssd_norm_w"] = dnw_ssd.reshape(1, d_inner)
    small["sc_conv_w"] = dscw
    small["loss"] = loss_lane
    small_names = list(small)
    head_names = ["ssd_dt_bias", "ssd_A_log", "ssd_D"]
    handles, small_token = _xfer_start([small[k] for k in small_names] + [dbias, dalog, dd_],
                                       "small_grads_start", gather=True)

    out_g, out_d, out_m, out_v = {}, {}, {}, {}

    layer_res = {}

    def big_update(name, i, after):
        (parts,) = _xfer_wait([grad_handle[f"{name}{i}"]], after, f"grads_wait_{name}_{i}", gather=False)
        res = _adamw_sum(parts, weights[name], moms[name], vars_[name], i, f"adamw_{name}_{i}",
                         prev=layer_res.get(name), window_off=in_off if name == "ssd_in_w" else None)
        layer_res[name] = res
        return res[1]

    chain = small_token
    for name, i in [("mlp_down", 1), ("mlp_up", 1), ("sc_out_w", 0), ("sc_in_w", 0), ("mlp_down", 0),
                    ("mlp_up", 0), ("ssd_out_w", 0), ("ssd_in_w", 0)]:
        chain = big_update(name, i, chain)
    gathered_small = _xfer_wait(handles, chain, "small_grads_wait", gather=True)
    small_all = dict(zip(small_names + head_names, gathered_small))

    dmod_loc = lax.dynamic_slice_in_dim(small_all["ada_b"], me * n_mod, n_mod, axis=2)
    dmod_pad = jnp.pad(jnp.transpose(dmod_loc, (1, 0, 2)), ((0, 0), (0, 16 - N_DEV), (0, 0)))
    out_g["ada_w"], out_d["ada_w"], out_m["ada_w"], out_v["ada_w"] = _ada_adamw(
        cond_pad, dmod_pad, ada_w, m_ada_w, v_ada_w, "adamw_ada_w")

    for k in ("ssd_conv_w", "sc_conv_w"):
        n_loc = weights[k].shape[2]
        small_all[k] = lax.dynamic_slice_in_dim(small_all[k], me * n_loc, n_loc, axis=2)
    plain = [k for k in small_names if k != "loss"]
    as2d = lambda a: a.reshape(-1, a.shape[-1])
    res, loss_row = _adamw_small(
        [small_all[k] for k in plain], [tuple(as2d(d[k]) for d in (weights, moms, vars_)) for k in plain],
        [small_all[k] for k in head_names], [tuple(as2d(d[k]) for d in (weights, moms, vars_)) for k in head_names],
        small_all["loss"], "adamw_small")
    loss = loss_row[0, 0]
    for k, res4 in zip(plain + head_names, res):
        for r, dst in zip(res4, (out_g, out_d, out_m, out_v)):
            dst[k] = r.reshape(weights[k].shape)
    for name, res4 in layer_res.items():
        for r, dst in zip(res4, (out_g, out_d, out_m, out_v)):
            dst[name] = r

    return (loss, grad_x[None], *[out_g[k] for k in names], *[out_d[k] for k in names],
            *[out_m[k] for k in names], *[out_v[k] for k in names])
```

```python
import jax
import jax.numpy as jnp
from jax import lax
from jax.experimental import pallas as pl
from jax.experimental.pallas import tpu as pltpu

F32 = jnp.float32
BF16 = jnp.bfloat16
N_DEV = 8
MESH = pl.DeviceIdType.MESH

NORM_EPS = 1e-5
SSD_G = 4
SSD_P = 64
SSD_N = 128
SSD_CHUNK = 128
SSD_K = 4
SC_K = 3
LANES = 128

ADAM_LR = 0.001
ADAM_B1 = 0.9
ADAM_B2 = 0.999
ADAM_EPS = 1e-08
ADAM_WD = 0.01
ADAM_STEP = 10

VMEM_LIMIT = 56 * 1024 * 1024


def _pcall(body, **kw):
    return pl.pallas_call(body, **kw)


def _cparams(sem=None):
    if sem is None:
        return pltpu.CompilerParams(vmem_limit_bytes=VMEM_LIMIT)
    return pltpu.CompilerParams(dimension_semantics=sem, vmem_limit_bytes=VMEM_LIMIT)


def _my_index():
    return 4 * lax.axis_index("x") + 2 * lax.axis_index("y") + lax.axis_index("c")


_PEER_MASKS = [(0, 0, 1), (0, 1, 0), (0, 1, 1), (1, 0, 0), (1, 0, 1), (1, 1, 0), (1, 1, 1)]


def _peers():
    x, y, c = lax.axis_index("x"), lax.axis_index("y"), lax.axis_index("c")
    out = []
    for mx, my, mc in _PEER_MASKS:
        px = (1 - x) if mx else x
        py = (1 - y) if my else y
        pc = (1 - c) if mc else c
        out.append(((px, py, pc), 4 * px + 2 * py + pc))
    return out


def _exchange(arrs, name, gather):
    n = len(arrs)
    n_peer = N_DEV - 1

    def body(*refs):
        ins, outs = refs[:n], refs[n:2 * n]
        send_sems, recv_sems, local_sems = refs[2 * n:]
        me = _my_index()
        peers = _peers()
        started = []
        for a in range(n):
            src_own = ins[a] if gather else ins[a].at[me]
            own = pltpu.make_async_copy(src_own, outs[a].at[me], local_sems.at[a])
            own.start()
            started.append(own)
        sends = []
        for a in range(n):
            for k, (peer, pidx) in enumerate(peers):
                src = ins[a] if gather else ins[a].at[pidx]
                cp = pltpu.make_async_remote_copy(
                    src_ref=src, dst_ref=outs[a].at[me],
                    send_sem=send_sems.at[a * n_peer + k], recv_sem=recv_sems.at[a * n_peer + k],
                    device_id=peer, device_id_type=MESH)
                cp.start()
                sends.append(cp)
        for a in range(n):
            for k, (peer, pidx) in enumerate(peers):
                src = ins[a] if gather else ins[a].at[pidx]
                pltpu.make_async_remote_copy(
                    src_ref=src, dst_ref=outs[a].at[pidx],
                    send_sem=send_sems.at[a * n_peer + k], recv_sem=recv_sems.at[a * n_peer + k],
                    device_id=peer, device_id_type=MESH).wait_recv()
        for cp in sends:
            cp.wait_send()
        for own in started:
            own.wait()

    if gather:
        out_shape = [jax.ShapeDtypeStruct((N_DEV,) + a.shape, a.dtype) for a in arrs]
    else:
        out_shape = [jax.ShapeDtypeStruct(a.shape, a.dtype) for a in arrs]
    any_spec = pl.BlockSpec(memory_space=pl.ANY)
    outs = _pcall(
        body, name=name, out_shape=out_shape,
        in_specs=[any_spec] * n, out_specs=[any_spec] * n,
        scratch_shapes=[pltpu.SemaphoreType.DMA((n * n_peer,)), pltpu.SemaphoreType.DMA((n * n_peer,)),
                        pltpu.SemaphoreType.DMA((n,))],
        compiler_params=pltpu.CompilerParams(has_side_effects=True),
    )(*arrs)
    return list(outs)


def _sibling_forward_start(lands, name):
    n = len(lands)
    n_fwd = len(_OTHER_CHIPS)

    def body(*refs):
        ins, bufs = refs[:n], refs[3 * n:4 * n]
        token = refs[-1]
        sibling = (lax.axis_index("x"), lax.axis_index("y"), 1 - lax.axis_index("c"))
        peers = _peers()
        for a in range(n):
            send_sems, recv_sems = refs[n + 2 * a], refs[n + 2 * a + 1]
            for j, k in enumerate(_OTHER_CHIPS):
                slot = peers[k][1]
                pltpu.make_async_remote_copy(
                    src_ref=ins[a].at[slot], dst_ref=bufs[a].at[slot], send_sem=send_sems.at[j],
                    recv_sem=recv_sems.at[j], device_id=sibling, device_id_type=MESH).start()
        token[...] = jnp.zeros_like(token)

    out_shape, out_specs = [], []
    for _ in range(n):
        out_shape += [pltpu.SemaphoreType.DMA((n_fwd,)), pltpu.SemaphoreType.DMA((n_fwd,))]
        out_specs += [_SEM, _SEM]
    out_shape += [pltpu.HBM(a.shape, a.dtype) for a in lands] + [jax.ShapeDtypeStruct((8, LANES), F32)]
    out_specs += [_HBM] * n + [pl.BlockSpec(memory_space=pltpu.VMEM)]
    outs = _pcall(
        body, name=name, out_shape=tuple(out_shape), in_specs=[_HBM] * n, out_specs=tuple(out_specs),
        input_output_aliases={a: 2 * n + a for a in range(n)},
        compiler_params=pltpu.CompilerParams(has_side_effects=_DATAFLOW),
    )(*[pltpu.with_memory_space_constraint(a, pltpu.HBM) for a in lands])
    return [(outs[2 * n + a], outs[2 * a], outs[2 * a + 1]) for a in range(n)], outs[-1]


def _sibling_forward_wait(handles, after, name):
    n = len(handles)

    def body(*refs):
        sibling = (lax.axis_index("x"), lax.axis_index("y"), 1 - lax.axis_index("c"))
        peers = _peers()
        for a in range(n):
            buf, send_sems, recv_sems = refs[3 * a:3 * a + 3]
            for j, k in enumerate(_OTHER_CHIPS):
                (px, py, pc), slot = peers[k]
                theirs = 4 * px + 2 * py + (1 - pc)
                cp = pltpu.make_async_remote_copy(
                    src_ref=buf.at[slot], dst_ref=buf.at[theirs], send_sem=send_sems.at[j],
                    recv_sem=recv_sems.at[j], device_id=sibling, device_id_type=MESH)
                cp.wait_send()
                cp.wait_recv()

    operands, in_specs = [], []
    for h in handles:
        operands += list(h)
        in_specs += [_HBM, _SEM, _SEM]
    outs = _pcall(
        body, name=name, out_shape=tuple(pltpu.HBM(h[0].shape, h[0].dtype) for h in handles),
        in_specs=in_specs + [pl.BlockSpec(memory_space=pl.ANY)], out_specs=tuple([_HBM] * n),
        input_output_aliases={3 * a: a for a in range(n)},
        compiler_params=pltpu.CompilerParams(has_side_effects=_DATAFLOW),
    )(*operands, after)
    return list(outs)


_HBM = pl.BlockSpec(memory_space=pltpu.HBM)
_SEM = pl.BlockSpec(memory_space=pltpu.SEMAPHORE)
_DATAFLOW = pltpu.SideEffectType.DATAFLOW_SIDE_EFFECTING


_ALL_PEERS = tuple(range(N_DEV - 1))
_SAME_CORE_PEERS = (0, 1, 3, 5)
_OTHER_CHIPS = (1, 3, 5)


def _xfer_start(arrs, name, gather, via_sibling=(), after=()):
    n = len(arrs)
    n_peer = N_DEV - 1
    n_after = len(after)
    peer_ks = [_SAME_CORE_PEERS if a in via_sibling else _ALL_PEERS for a in range(n)]

    def body(*refs):
        ins, lands = refs[:n], refs[n:2 * n]
        sems = refs[2 * n + n_after:5 * n + n_after]
        token = refs[-1]
        me = _my_index()
        peers = _peers()
        for a in range(n):
            send_sems, recv_sems, loc_sem = sems[3 * a:3 * a + 3]
            src_own = ins[a] if gather else ins[a].at[me]
            pltpu.make_async_copy(src_own, lands[a].at[me], loc_sem).start()
            for k in peer_ks[a]:
                peer, pidx = peers[k]
                src = ins[a] if gather else ins[a].at[pidx]
                pltpu.make_async_remote_copy(
                    src_ref=src, dst_ref=lands[a].at[me], send_sem=send_sems.at[k], recv_sem=recv_sems.at[k],
                    device_id=peer, device_id_type=MESH).start()
        token[...] = jnp.zeros_like(token)

    land_shapes = [((N_DEV,) + a.shape) if gather else a.shape for a in arrs]
    out_shape, out_specs = [], []
    for _ in range(n):
        out_shape += [pltpu.SemaphoreType.DMA((n_peer,)), pltpu.SemaphoreType.DMA((n_peer,)),
                      pltpu.SemaphoreType.DMA(())]
        out_specs += [_SEM, _SEM, _SEM]
    out_shape += [pltpu.HBM(a.shape, a.dtype) for a in arrs]
    out_shape += [pltpu.HBM(s, a.dtype) for s, a in zip(land_shapes, arrs)]
    out_shape += [jax.ShapeDtypeStruct((8, LANES), F32)]
    out_specs += [_HBM] * (2 * n) + [pl.BlockSpec(memory_space=pltpu.VMEM)]
    aliases = {}
    for a in range(n):
        aliases[a] = 3 * n + a
        aliases[n + a] = 4 * n + a
    operands = [pltpu.with_memory_space_constraint(a, pltpu.HBM) for a in arrs]
    operands += [pltpu.with_memory_space_constraint(lax.empty(s, a.dtype), pltpu.HBM)
                 for s, a in zip(land_shapes, arrs)]
    outs = _pcall(
        body, name=name, out_shape=tuple(out_shape),
        in_specs=[_HBM] * (2 * n) + [pl.BlockSpec(memory_space=pl.ANY)] * n_after, out_specs=tuple(out_specs),
        input_output_aliases=aliases,
        compiler_params=pltpu.CompilerParams(has_side_effects=_DATAFLOW),
    )(*operands, *after)
    handles = []
    for a in range(n):
        handles.append((outs[3 * n + a], outs[4 * n + a], outs[3 * a], outs[3 * a + 1], outs[3 * a + 2],
                        peer_ks[a]))
    return handles, outs[-1]


def _xfer_wait(handles, after, name, gather):
    n = len(handles)
    after = tuple(after) if isinstance(after, (tuple, list)) else (after,)
    peer_ks = [h[5] for h in handles]

    def body(*refs):
        me = _my_index()
        peers = _peers()
        for a in range(n):
            src_ref, land_ref, send_ref, recv_ref, loc_ref = refs[5 * a:5 * a + 5]
            src_own = src_ref if gather else src_ref.at[me]
            pltpu.make_async_copy(src_own, land_ref.at[me], loc_ref).wait()
            for k in peer_ks[a]:
                peer, pidx = peers[k]
                src = src_ref if gather else src_ref.at[pidx]
                cp = pltpu.make_async_remote_copy(
                    src_ref=src, dst_ref=land_ref.at[pidx], send_sem=send_ref.at[k], recv_sem=recv_ref.at[k],
                    device_id=peer, device_id_type=MESH)
                cp.wait_send()
                cp.wait_recv()

    operands, in_specs, out_shape, aliases = [], [], [], {}
    for a, h in enumerate(handles):
        operands += list(h[:5])
        in_specs += [_HBM, _HBM, _SEM, _SEM, _SEM]
        out_shape += [pltpu.HBM(h[0].shape, h[0].dtype), pltpu.HBM(h[1].shape, h[1].dtype)]
        aliases[5 * a] = 2 * a
        aliases[5 * a + 1] = 2 * a + 1
    outs = _pcall(
        body, name=name, out_shape=tuple(out_shape),
        in_specs=in_specs + [pl.BlockSpec(memory_space=pl.ANY)] * len(after),
        out_specs=tuple([_HBM] * (2 * n)), input_output_aliases=aliases,
        compiler_params=pltpu.CompilerParams(has_side_effects=_DATAFLOW),
    )(*operands, *after)
    return [outs[2 * a + 1] for a in range(n)]


def _sibling_forward(lands, name):
    n = len(lands)
    n_fwd = len(_OTHER_CHIPS)

    def body(*refs):
        ins, bufs = refs[:n], refs[n:2 * n]
        send_sems, recv_sems = refs[2 * n:]
        x, y, c = lax.axis_index("x"), lax.axis_index("y"), lax.axis_index("c")
        sibling = (x, y, 1 - c)
        peers = _peers()
        sends = []
        for a in range(n):
            for j, k in enumerate(_OTHER_CHIPS):
                slot = peers[k][1]
                cp = pltpu.make_async_remote_copy(
                    src_ref=ins[a].at[slot], dst_ref=bufs[a].at[slot],
                    send_sem=send_sems.at[a * n_fwd + j], recv_sem=recv_sems.at[a * n_fwd + j],
                    device_id=sibling, device_id_type=MESH)
                cp.start()
                sends.append(cp)
        for a in range(n):
            for j, k in enumerate(_OTHER_CHIPS):
                (px, py, pc), slot = peers[k]
                theirs = 4 * px + 2 * py + (1 - pc)
                pltpu.make_async_remote_copy(
                    src_ref=ins[a].at[slot], dst_ref=bufs[a].at[theirs],
                    send_sem=send_sems.at[a * n_fwd + j], recv_sem=recv_sems.at[a * n_fwd + j],
                    device_id=sibling, device_id_type=MESH).wait_recv()
        for cp in sends:
            cp.wait_send()

    any_spec = pl.BlockSpec(memory_space=pl.ANY)
    outs = _pcall(
        body, name=name, out_shape=[jax.ShapeDtypeStruct(a.shape, a.dtype) for a in lands],
        in_specs=[any_spec] * n, out_specs=[any_spec] * n,
        input_output_aliases={a: a for a in range(n)},
        scratch_shapes=[pltpu.SemaphoreType.DMA((n * n_fwd,)), pltpu.SemaphoreType.DMA((n * n_fwd,))],
        compiler_params=pltpu.CompilerParams(has_side_effects=True),
    )(*lands)
    return list(outs)


_DIMS = {"nn": (((1,), (0,)), ((), ())), "nt": (((1,), (1,)), ((), ())), "tn": (((0,), (0,)), ((), ()))}


def _dot(a, b, mode="nn"):
    return lax.dot_general(a, b, _DIMS[mode], preferred_element_type=F32)


def _mm(a, b, *, mode, grid, a_spec, b_spec, out_shape, out_specs, acc_shape, epilogue, name,
        extra=(), extra_specs=(), after=(), semantics=("parallel", "parallel", "arbitrary")):
    nk = grid[2]
    n_extra = len(extra)
    n_in = 2 + n_extra + len(after)

    def body_single(*refs):
        a_ref, b_ref = refs[0], refs[1]
        epilogue(_dot(a_ref[...], b_ref[...], mode), refs[2:2 + n_extra], refs[n_in:])

    def body_acc(*refs):
        a_ref, b_ref = refs[0], refs[1]
        ex = refs[2:2 + n_extra]
        outs = refs[n_in:-1]
        acc = refs[-1]
        k = pl.program_id(2)

        @pl.when(k == 0)
        def _():
            acc[...] = jnp.zeros_like(acc)

        acc[...] += _dot(a_ref[...], b_ref[...], mode)

        @pl.when(k == nk - 1)
        def _():
            epilogue(acc[...], ex, outs)

    return _pcall(
        body_single if nk == 1 else body_acc, name=name, grid=grid, out_shape=out_shape,
        in_specs=[a_spec, b_spec] + list(extra_specs) + [pl.BlockSpec(memory_space=pl.ANY)] * len(after),
        out_specs=out_specs,
        scratch_shapes=[] if nk == 1 else [pltpu.VMEM(acc_shape, F32)],
        compiler_params=_cparams(semantics),
    )(a, b, *extra, *after)


def _ep_store(dtype):
    def ep(acc, ex, outs):
        outs[0][...] = acc.astype(dtype)
    return ep


def _ep_relu2(acc, ex, outs):
    outs[0][...] = acc.astype(BF16)
    r = jnp.maximum(acc, 0.0)
    outs[1][...] = (r * r).astype(BF16)


def _ep_relu2_bwd(acc, ex, outs):
    u = ex[0][...].astype(F32)
    outs[0][...] = (acc * (2.0 * jnp.maximum(u, 0.0))).astype(BF16)


def _tile(n, want):
    t = min(n, want)
    while n % t:
        t //= 2
    return t


def _mm_nn(a, w, out_dtype, name, tm=2048, tn=1024, tk=1024, epilogue=None, out_dtypes=None, after=()):
    M, K = a.shape
    N = w.shape[1]
    tm, tn, tk = _tile(M, tm), _tile(N, tn), _tile(K, tk)
    out_dtypes = out_dtypes or [out_dtype]
    return _mm(a, w, mode="nn", grid=(M // tm, N // tn, K // tk),
               a_spec=pl.BlockSpec((tm, tk), lambda i, j, k: (i, k)),
               b_spec=pl.BlockSpec((tk, tn), lambda i, j, k: (k, j)),
               out_shape=[jax.ShapeDtypeStruct((M, N), d) for d in out_dtypes],
               out_specs=[pl.BlockSpec((tm, tn), lambda i, j, k: (i, j)) for _ in out_dtypes],
               acc_shape=(tm, tn), epilogue=epilogue or _ep_store(out_dtype), name=name, after=after)


def _mm_nn_blocked(a, wg, name, epilogue, out_dtypes, tm=2048):
    M, K = a.shape
    n = wg.shape[2]
    tm = _tile(M, tm)
    return _mm(a, wg, mode="nn", grid=(M // tm, N_DEV, 1),
               a_spec=pl.BlockSpec((tm, K), lambda i, j, k: (i, 0)),
               b_spec=pl.BlockSpec((None, K, n), lambda i, j, k: (j, 0, 0)),
               out_shape=[jax.ShapeDtypeStruct((M, N_DEV * n), d) for d in out_dtypes],
               out_specs=[pl.BlockSpec((tm, n), lambda i, j, k: (i, j)) for _ in out_dtypes],
               acc_shape=(tm, n), epilogue=epilogue, name=name)


def _mm_nt(a, w, out_dtype, name, tm=2048, tn=1024, tk=1024, epilogue=None, extra=(), extra_specs=(),
           after=()):
    M, K = a.shape
    N = w.shape[0]
    tm, tn, tk = _tile(M, tm), _tile(N, tn), _tile(K, tk)
    if extra and not extra_specs:
        extra_specs = [pl.BlockSpec((tm, tn), lambda i, j, k: (i, j)) for _ in extra]
    return _mm(a, w, mode="nt", grid=(M // tm, N // tn, K // tk),
               a_spec=pl.BlockSpec((tm, tk), lambda i, j, k: (i, k)),
               b_spec=pl.BlockSpec((tn, tk), lambda i, j, k: (j, k)),
               out_shape=[jax.ShapeDtypeStruct((M, N), out_dtype)],
               out_specs=[pl.BlockSpec((tm, tn), lambda i, j, k: (i, j))],
               acc_shape=(tm, tn), epilogue=epilogue or _ep_store(out_dtype), name=name,
               extra=extra, extra_specs=extra_specs, after=after)[0]


def _mm_nt_blocked(a, wg, out_dtype, name, tm=1024, after=()):
    M = a.shape[0]
    kout, n = wg.shape[1], wg.shape[2]
    tm = _tile(M, tm)
    return _mm(a, wg, mode="nt", grid=(M // tm, 1, N_DEV),
               a_spec=pl.BlockSpec((tm, n), lambda i, j, k: (i, k)),
               b_spec=pl.BlockSpec((None, kout, n), lambda i, j, k: (k, 0, 0)),
               out_shape=[jax.ShapeDtypeStruct((M, kout), out_dtype)],
               out_specs=[pl.BlockSpec((tm, kout), lambda i, j, k: (i, 0))],
               acc_shape=(tm, kout), epilogue=_ep_store(out_dtype), name=name, after=after)[0]


def _mm_tn(a, b, out_dtype, name, tm=1024, tn=1024, tk=2048):
    K, M = a.shape
    N = b.shape[1]
    tm, tn, tk = _tile(M, tm), _tile(N, tn), _tile(K, tk)
    return _mm(a, b, mode="tn", grid=(M // tm, N // tn, K // tk),
               a_spec=pl.BlockSpec((tk, tm), lambda i, j, k: (k, i)),
               b_spec=pl.BlockSpec((tk, tn), lambda i, j, k: (k, j)),
               out_shape=[jax.ShapeDtypeStruct((M, N), out_dtype)],
               out_specs=[pl.BlockSpec((tm, tn), lambda i, j, k: (i, j))],
               acc_shape=(tm, tn), epilogue=_ep_store(out_dtype), name=name)[0]


def _mm_tn_blocked(a, b, out_dtype, name, tm=1024, tk=2048):
    K, M = a.shape
    n = b.shape[1] // N_DEV
    tm, tk = _tile(M, tm), _tile(K, tk)
    return _mm(a, b, mode="tn", grid=(M // tm, N_DEV, K // tk),
               a_spec=pl.BlockSpec((tk, tm), lambda i, j, k: (k, i)),
               b_spec=pl.BlockSpec((tk, n), lambda i, j, k: (k, j)),
               out_shape=[jax.ShapeDtypeStruct((N_DEV, M, n), out_dtype)],
               out_specs=[pl.BlockSpec((None, tm, n), lambda i, j, k: (j, i, 0))],
               acc_shape=(tm, n), epilogue=_ep_store(out_dtype), name=name)[0]


def _window_geometry(ws):
    base = [(ws * k // LANES) * LANES for k in range(N_DEV)]
    off = [ws * k - base[k] for k in range(N_DEV)]
    win = -(-(max(off) + ws) // LANES) * LANES
    return base, off, win


def _shards_to_columns(xg, base, off, win, n_out, name, tr=256):
    R, ws = xg.shape[1], xg.shape[2]
    tr = _tile(R, tr)
    nb_win = win // LANES

    def body(x_ref, o_ref, frame_ref):
        written = set()
        frame_ref[...] = jnp.zeros_like(frame_ref)
        for k in range(N_DEV):
            frame_ref[:, 0:ws] = x_ref[k].astype(F32)
            window = frame_ref[...]
            if off[k]:
                window = pltpu.roll(window, off[k], 1)
            for i in range(nb_win):
                b = base[k] // LANES + i
                if b * LANES >= n_out:
                    continue
                cols = slice(b * LANES, (b + 1) * LANES)
                blk = window[:, i * LANES:(i + 1) * LANES]
                if b in written:
                    blk = blk + o_ref[:, cols].astype(F32)
                o_ref[:, cols] = blk.astype(o_ref.dtype)
                written.add(b)
        for b in range(n_out // LANES):
            if b not in written:
                o_ref[:, b * LANES:(b + 1) * LANES] = jnp.zeros((tr, LANES), o_ref.dtype)

    return _pcall(
        body, name=name, grid=(R // tr,), out_shape=jax.ShapeDtypeStruct((R, n_out), xg.dtype),
        in_specs=[pl.BlockSpec((N_DEV, tr, ws), lambda i: (0, i, 0))],
        out_specs=pl.BlockSpec((tr, n_out), lambda i: (i, 0)),
        scratch_shapes=[pltpu.VMEM((tr, win), F32)],
        compiler_params=_cparams(("parallel",)))(xg)


def _columns_to_shards(x, ws, base, off, win, name, tr=256):
    R = x.shape[0]
    tr = _tile(R, tr)

    def body(x_ref, o_ref, frame_ref):
        for k in range(N_DEV):
            window = x_ref[:, base[k]:base[k] + win].astype(F32)
            if off[k]:
                window = pltpu.roll(window, win - off[k], 1)
            frame_ref[...] = window
            o_ref[k] = frame_ref[:, 0:ws].astype(o_ref.dtype)

    return _pcall(
        body, name=name, grid=(R // tr,), out_shape=jax.ShapeDtypeStruct((N_DEV, R, ws), x.dtype),
        in_specs=[pl.BlockSpec((tr, x.shape[1]), lambda i: (i, 0))],
        out_specs=pl.BlockSpec((N_DEV, tr, ws), lambda i: (0, i, 0)),
        scratch_shapes=[pltpu.VMEM((tr, win), F32)],
        compiler_params=_cparams(("parallel",)))(x)


def _sigmoid(x):
    return 1.0 / (1.0 + jnp.exp(-x))


def _row_spec(tm, d):
    return pl.BlockSpec((tm, d), lambda i: (i, 0))


def _vec_spec(d):
    return pl.BlockSpec((1, d), lambda i: (0, 0))


def _norm_mod_fwd(x, y, gate, nw, scale, shift, name, tm=512):
    L, D = x.shape
    tm = _tile(L, tm)
    has_res = y is not None

    def body(*refs):
        if has_res:
            x_ref, y_ref, g_ref, nw_ref, sc_ref, sh_ref, xo_ref, h_ref = refs
            xn = x_ref[...] + g_ref[...] * y_ref[...]
            xo_ref[...] = xn
        else:
            x_ref, nw_ref, sc_ref, sh_ref, h_ref = refs
            xn = x_ref[...]
        rstd = lax.rsqrt(jnp.mean(xn * xn, axis=-1, keepdims=True) + NORM_EPS)
        h = xn * rstd * nw_ref[...] * (1.0 + sc_ref[...]) + sh_ref[...]
        h_ref[...] = h.astype(BF16)

    row, vec = _row_spec(tm, D), _vec_spec(D)
    if has_res:
        ins, in_specs = (x, y, gate, nw, scale, shift), [row, row, vec, vec, vec, vec]
        out_shape = [jax.ShapeDtypeStruct((L, D), F32), jax.ShapeDtypeStruct((L, D), BF16)]
        out_specs = [row, row]
    else:
        ins, in_specs = (x, nw, scale, shift), [row, vec, vec, vec]
        out_shape = [jax.ShapeDtypeStruct((L, D), BF16)]
        out_specs = [row]
    outs = _pcall(body, name=name, grid=(L // tm,), out_shape=out_shape, in_specs=in_specs,
                  out_specs=out_specs, compiler_params=_cparams(("parallel",)))(*ins)
    return outs if has_res else (x, outs[0])


def _gated_branch_bwd(dx, branch, y_ref, g_ref, dy_ref, dg_ref):
    if branch is None:
        return
    dy_ref[...] = (g_ref[...] * dx).astype(BF16)
    dg_ref[...] += jnp.sum(dx * y_ref[...], axis=0, keepdims=True)


def _norm_mod_bwd(dh, x, nw, scale, dres, name, branch=None, tm=512):
    L, D = x.shape
    tm = _tile(L, tm)
    nb = 0 if branch is None else 2

    def body(dh_ref, x_ref, nw_ref, sc_ref, dres_ref, *rest):
        y_ref, g_ref = rest[:nb] if nb else (None, None)
        dx_ref, dsh_ref, dsc_ref, dnw_ref = rest[nb:nb + 4]
        dy_ref, dg_ref = rest[nb + 4:] if nb else (None, None)

        @pl.when(pl.program_id(0) == 0)
        def _():
            dsh_ref[...] = jnp.zeros_like(dsh_ref)
            dsc_ref[...] = jnp.zeros_like(dsc_ref)
            dnw_ref[...] = jnp.zeros_like(dnw_ref)
            if nb:
                dg_ref[...] = jnp.zeros_like(dg_ref)

        xv = x_ref[...]
        dh_v = dh_ref[...]
        nw_v = nw_ref[...]
        rstd = lax.rsqrt(jnp.mean(xv * xv, axis=-1, keepdims=True) + NORM_EPS)
        xhat = xv * rstd
        dsh_ref[...] += jnp.sum(dh_v, axis=0, keepdims=True)
        dsc_ref[...] += jnp.sum(dh_v * (xhat * nw_v), axis=0, keepdims=True)
        dr = dh_v * (1.0 + sc_ref[...])
        dnw_ref[...] += jnp.sum(dr * xhat, axis=0, keepdims=True)
        dxh = dr * nw_v
        dx = rstd * (dxh - xhat * jnp.mean(dxh * xhat, axis=-1, keepdims=True)) + dres_ref[...]
        dx_ref[...] = dx
        _gated_branch_bwd(dx, branch, y_ref, g_ref, dy_ref, dg_ref)

    row, vec = _row_spec(tm, D), _vec_spec(D)
    extra_in = [] if branch is None else list(branch)
    return _pcall(
        body, name=name, grid=(L // tm,),
        out_shape=[jax.ShapeDtypeStruct((L, D), F32)] + [jax.ShapeDtypeStruct((1, D), F32)] * 3
        + ([jax.ShapeDtypeStruct((L, D), BF16), jax.ShapeDtypeStruct((1, D), F32)] if nb else []),
        in_specs=[row, row, vec, vec, row] + ([row, vec] if nb else []),
        out_specs=[row, vec, vec, vec] + ([row, vec] if nb else []),
        compiler_params=_cparams(("arbitrary",)))(dh, x, nw, scale, dres, *extra_in)


def _final_loss(x, y, gate, fw, target, name, tm=512):
    L, D = x.shape
    tm = _tile(L, tm)

    def body(x_ref, y_ref, g_ref, fw_ref, t_ref, dx_ref, loss_ref, dfw_ref, dy_ref, dg_ref):
        @pl.when(pl.program_id(0) == 0)
        def _():
            loss_ref[...] = jnp.zeros_like(loss_ref)
            dfw_ref[...] = jnp.zeros_like(dfw_ref)
            dg_ref[...] = jnp.zeros_like(dg_ref)

        xn = x_ref[...] + g_ref[...] * y_ref[...]
        fw_v = fw_ref[...]
        rstd = lax.rsqrt(jnp.mean(xn * xn, axis=-1, keepdims=True) + NORM_EPS)
        xhat = xn * rstd
        diff = xhat * fw_v - t_ref[...]
        loss_ref[...] += jnp.sum(diff * diff, axis=0, keepdims=True)
        dyf = diff * (1.0 / D)
        dfw_ref[...] += jnp.sum(dyf * xhat, axis=0, keepdims=True)
        dxh = dyf * fw_v
        dx = rstd * (dxh - xhat * jnp.mean(dxh * xhat, axis=-1, keepdims=True))
        dx_ref[...] = dx
        _gated_branch_bwd(dx, True, y_ref, g_ref, dy_ref, dg_ref)

    row, vec = _row_spec(tm, D), _vec_spec(D)
    return _pcall(
        body, name=name, grid=(L // tm,),
        out_shape=[jax.ShapeDtypeStruct((L, D), F32), jax.ShapeDtypeStruct((1, D), F32),
                   jax.ShapeDtypeStruct((1, D), F32), jax.ShapeDtypeStruct((L, D), BF16),
                   jax.ShapeDtypeStruct((1, D), F32)],
        in_specs=[row, row, vec, vec, row], out_specs=[row, vec, vec, row, vec],
        compiler_params=_cparams(("arbitrary",)))(x, y, gate, fw, target)


def _shift_down(v, s, row):
    if s == 0:
        return v
    return jnp.where(row >= s, pltpu.roll(v, s, 0), 0.0)


CONV_ROWS = 32


def _shifted_rows(x_ref, r0, n, lanes=slice(None)):
    cur = x_ref[r0:r0 + CONV_ROWS, lanes]
    if r0 >= n - 1:
        return [cur] + [x_ref[r0 - s:r0 - s + CONV_ROWS, lanes] for s in range(1, n)]
    row = lax.broadcasted_iota(jnp.int32, cur.shape, 0)
    return [_shift_down(cur, s, row) for s in range(n)]


def _ssd_conv_fwd(zx, w, b, col0, width, name, cb=512):
    L = zx.shape[0]
    nb = width // cb
    off = col0 // cb

    def body(x_ref, w_ref, b_ref, o_ref):
        for l0 in range(0, cb, LANES):
            lanes = slice(l0, l0 + LANES)
            taps = [w_ref[k:k + 1, lanes] for k in range(SSD_K)]
            bias = b_ref[:, lanes]
            for r0 in range(0, L, CONV_ROWS):
                taps_in = _shifted_rows(x_ref, r0, SSD_K, lanes)
                acc = bias + taps[SSD_K - 1] * taps_in[0]
                for s in range(1, SSD_K):
                    acc = acc + taps[SSD_K - 1 - s] * taps_in[s]
                o_ref[r0:r0 + CONV_ROWS, lanes] = acc * _sigmoid(acc)

    return _pcall(
        body, name=name, grid=(nb,), out_shape=jax.ShapeDtypeStruct((L, width), F32),
        in_specs=[pl.BlockSpec((L, cb), lambda j: (0, off + j)),
                  pl.BlockSpec((SSD_K, cb), lambda j: (0, j)),
                  pl.BlockSpec((1, cb), lambda j: (0, j))],
        out_specs=pl.BlockSpec((L, cb), lambda j: (0, j)),
        compiler_params=_cparams(("parallel",)))(zx, w, b)


def _ssd_conv_bwd(zx, w, b, d_parts, dzx, col0, name, cb=128):
    L = zx.shape[0]
    widths = [p.shape[1] for p in d_parts]
    width = sum(widths)
    nb = width // cb
    off = col0 // cb
    starts = [sum(widths[:i]) // cb for i in range(len(d_parts))]
    counts = [wd // cb for wd in widths]

    def body(x_ref, w_ref, b_ref, *rest):
        d_refs = rest[:len(d_parts)]
        dx_ref, dw_ref, db_ref, dpre_ref = rest[len(d_parts) + 1:]
        j = pl.program_id(0)
        taps = [w_ref[k:k + 1, :] for k in range(SSD_K)]
        bias = b_ref[...]
        fold = lambda v: sum(v[r:r + 8, :] for r in range(0, CONV_ROWS, 8))
        db8 = jnp.zeros((8, cb), F32)
        dw8 = [jnp.zeros((8, cb), F32) for _ in range(SSD_K)]
        for r0 in range(0, L, CONV_ROWS):
            rows = slice(r0, r0 + CONV_ROWS)
            d_val = d_refs[-1][rows, :]
            for i in range(len(d_parts) - 2, -1, -1):
                d_val = jnp.where(j < starts[i + 1], d_refs[i][rows, :], d_val)
            taps_in = _shifted_rows(x_ref, r0, SSD_K)
            acc = bias + taps[SSD_K - 1] * taps_in[0]
            for s in range(1, SSD_K):
                acc = acc + taps[SSD_K - 1 - s] * taps_in[s]
            sig = _sigmoid(acc)
            dpre = d_val * (sig * (1.0 + acc * (1.0 - sig)))
            dpre_ref[rows, :] = dpre
            db8 = db8 + fold(dpre)
            for s in range(SSD_K):
                dw8[s] = dw8[s] + fold(dpre * taps_in[s])
        dpre_ref[L:L + 8, :] = jnp.zeros((8, cb), F32)
        db_ref[...] = jnp.sum(db8, axis=0, keepdims=True)
        for s in range(SSD_K):
            dw_ref[SSD_K - 1 - s:SSD_K - s, :] = jnp.sum(dw8[s], axis=0, keepdims=True)
        for r0 in range(0, L, CONV_ROWS):
            dx = taps[SSD_K - 1] * dpre_ref[r0:r0 + CONV_ROWS, :]
            for s in range(1, SSD_K):
                dx = dx + taps[SSD_K - 1 - s] * dpre_ref[r0 + s:r0 + s + CONV_ROWS, :]
            dx_ref[r0:r0 + CONV_ROWS, :] = dx.astype(BF16)

    def part_spec(i):
        return pl.BlockSpec((L, cb), lambda j: (0, jnp.clip(j - starts[i], 0, counts[i] - 1)))

    return _pcall(
        body, name=name, grid=(nb,),
        out_shape=[jax.ShapeDtypeStruct(dzx.shape, BF16), jax.ShapeDtypeStruct((SSD_K, width), F32),
                   jax.ShapeDtypeStruct((1, width), F32)],
        in_specs=[pl.BlockSpec((L, cb), lambda j: (0, off + j)),
                  pl.BlockSpec((SSD_K, cb), lambda j: (0, j)),
                  pl.BlockSpec((1, cb), lambda j: (0, j))]
        + [part_spec(i) for i in range(len(d_parts))] + [pl.BlockSpec(memory_space=pl.ANY)],
        out_specs=[pl.BlockSpec((L, cb), lambda j: (0, off + j)),
                   pl.BlockSpec((SSD_K, cb), lambda j: (0, j)),
                   pl.BlockSpec((1, cb), lambda j: (0, j))],
        input_output_aliases={3 + len(d_parts): 0},
        scratch_shapes=[pltpu.VMEM((L + 8, cb), F32)],
        compiler_params=_cparams(("parallel",)))(zx, w, b, *d_parts, dzx)


def _dzx_finish(dzx, ddt, col0, name, tl=512):
    G, L, _ = ddt.shape
    tail = dzx.shape[1] - col0
    tl = _tile(L, tl)

    def body(ddt_ref, dzx_ref, o_ref):
        s = ddt_ref[0]
        for g in range(1, G):
            s = s + ddt_ref[g]
        o_ref[:, 0:LANES] = s.astype(o_ref.dtype)
        if tail > LANES:
            o_ref[:, LANES:] = jnp.zeros((tl, tail - LANES), o_ref.dtype)

    return _pcall(
        body, name=name, grid=(L // tl,), out_shape=jax.ShapeDtypeStruct(dzx.shape, dzx.dtype),
        in_specs=[pl.BlockSpec((G, tl, LANES), lambda i: (0, i, 0)), pl.BlockSpec(memory_space=pl.ANY)],
        out_specs=pl.BlockSpec((tl, tail), lambda i: (i, col0 // tail)),
        input_output_aliases={1: 0},
        compiler_params=_cparams(("parallel",)))(ddt, dzx)


def _sc_conv_fwd(proj, w, name, cb=512):
    L = proj.shape[0]
    width = proj.shape[1] // 3
    nb = width // cb

    def body(b_ref, c_ref, x_ref, w_ref, o_ref):
        for l0 in range(0, cb, LANES):
            lanes = slice(l0, l0 + LANES)
            taps = [w_ref[k:k + 1, lanes] for k in range(SC_K)]
            for r0 in range(0, L, CONV_ROWS):
                rows = slice(r0, r0 + CONV_ROWS)
                q = [c * x for c, x in zip(_shifted_rows(c_ref, r0, SC_K, lanes),
                                           _shifted_rows(x_ref, r0, SC_K, lanes))]
                acc = taps[SC_K - 1] * q[0]
                for s in range(1, SC_K):
                    acc = acc + taps[SC_K - 1 - s] * q[s]
                o_ref[rows, lanes] = (b_ref[rows, lanes] * acc).astype(BF16)

    return _pcall(
        body, name=name, grid=(nb,), out_shape=jax.ShapeDtypeStruct((L, width), BF16),
        in_specs=[pl.BlockSpec((L, cb), lambda j: (0, j)),
                  pl.BlockSpec((L, cb), lambda j: (0, nb + j)),
                  pl.BlockSpec((L, cb), lambda j: (0, 2 * nb + j)),
                  pl.BlockSpec((SC_K, cb), lambda j: (0, j))],
        out_specs=pl.BlockSpec((L, cb), lambda j: (0, j)),
        compiler_params=_cparams(("parallel",)))(proj, proj, proj, w)


def _sc_conv_bwd(proj, w, dy, name, cb=128):
    L = proj.shape[0]
    width = proj.shape[1] // 3
    nb = width // cb

    def body(b_ref, c_ref, x_ref, w_ref, dy_ref, db_ref, dc_ref, dxv_ref, dw_ref, dconv_ref):
        taps = [w_ref[k:k + 1, :] for k in range(SC_K)]
        fold = lambda v: sum(v[r:r + 8, :] for r in range(0, CONV_ROWS, 8))
        dw8 = [jnp.zeros((8, cb), F32) for _ in range(SC_K)]
        for r0 in range(0, L, CONV_ROWS):
            rows = slice(r0, r0 + CONV_ROWS)
            q = [c * x for c, x in zip(_shifted_rows(c_ref, r0, SC_K), _shifted_rows(x_ref, r0, SC_K))]
            conv = taps[SC_K - 1] * q[0]
            for s in range(1, SC_K):
                conv = conv + taps[SC_K - 1 - s] * q[s]
            dyv = dy_ref[rows, :]
            db_ref[rows, :] = (dyv * conv).astype(BF16)
            dconv = dyv * b_ref[rows, :]
            dconv_ref[rows, :] = dconv
            for s in range(SC_K):
                dw8[s] = dw8[s] + fold(dconv * q[s])
        dconv_ref[L:L + 8, :] = jnp.zeros((8, cb), F32)
        for s in range(SC_K):
            dw_ref[SC_K - 1 - s:SC_K - s, :] = jnp.sum(dw8[s], axis=0, keepdims=True)
        for r0 in range(0, L, CONV_ROWS):
            rows = slice(r0, r0 + CONV_ROWS)
            dq = taps[SC_K - 1] * dconv_ref[rows, :]
            for s in range(1, SC_K):
                dq = dq + taps[SC_K - 1 - s] * dconv_ref[r0 + s:r0 + s + CONV_ROWS, :]
            dc_ref[rows, :] = (dq * x_ref[rows, :]).astype(BF16)
            dxv_ref[rows, :] = (dq * c_ref[rows, :]).astype(BF16)

    blk = pl.BlockSpec((L, cb), lambda j: (0, j))
    wblk = pl.BlockSpec((SC_K, cb), lambda j: (0, j))
    return _pcall(
        body, name=name, grid=(nb,),
        out_shape=[jax.ShapeDtypeStruct((L, width), BF16)] * 3 + [jax.ShapeDtypeStruct((SC_K, width), F32)],
        in_specs=[blk, pl.BlockSpec((L, cb), lambda j: (0, nb + j)),
                  pl.BlockSpec((L, cb), lambda j: (0, 2 * nb + j)), wblk, blk],
        out_specs=[blk, blk, blk, wblk], scratch_shapes=[pltpu.VMEM((L + 8, cb), F32)],
        compiler_params=_cparams(("parallel",)))(proj, proj, proj, w, dy)


def _split3(v):
    hi = v.astype(BF16)
    r1 = v - hi.astype(F32)
    mid = r1.astype(BF16)
    lo = (r1 - mid.astype(F32)).astype(BF16)
    return hi, mid, lo


def _dot_exact01(t01, v):
    hi, mid, lo = _split3(v)
    return _dot(t01, hi) + _dot(t01, mid) + _dot(t01, lo)


def _lane_col(v, lane, h):
    return jnp.sum(jnp.where(lane == h, v, 0.0), axis=1, keepdims=True)


def _sum_all(v):
    return jnp.sum(jnp.sum(v, axis=1, keepdims=True), axis=0, keepdims=True)


def _softplus(x):
    return jnp.maximum(x, 0.0) + jnp.log1p(jnp.exp(-jnp.abs(x)))


def _ssd_decay(zx, bias_p, alog_p, n_heads, dt_block, name):
    L = zx.shape[0]
    nc = L // SSD_CHUNK
    per_step = 4 if nc % 4 == 0 else 1
    rows_step = per_step * SSD_CHUNK

    def body(raw_ref, bias_ref, alog_ref, dt_ref, sg_ref, cs_ref, cst_ref, last_ref):
        lane = lax.broadcasted_iota(jnp.int32, (SSD_CHUNK, LANES), 1)
        row = lax.broadcasted_iota(jnp.int32, (SSD_CHUNK, LANES), 0)
        valid = lane < n_heads
        tri = (row >= lane).astype(BF16)
        a_row = -jnp.exp(alog_ref[...])
        for i in range(per_step):
            rows = slice(i * SSD_CHUNK, (i + 1) * SSD_CHUNK)
            raw = raw_ref[rows, :] + bias_ref[...]
            dt = jnp.where(valid, _softplus(raw), 0.0)
            a = dt * a_row
            cs = _dot_exact01(tri, a)
            dt_ref[rows, :] = dt
            sg_ref[rows, :] = _sigmoid(raw)
            cs_ref[rows, :] = cs
            cst_ref[i] = cs.T
            last_ref[i] = jnp.sum(a, axis=0, keepdims=True)

    blk = pl.BlockSpec((rows_step, LANES), lambda c: (c, 0))
    head_vec = pl.BlockSpec((1, LANES), lambda c: (0, 0))
    return _pcall(
        body, name=name, grid=(nc // per_step,),
        out_shape=[jax.ShapeDtypeStruct((L, LANES), F32)] * 3
        + [jax.ShapeDtypeStruct((nc, SSD_CHUNK, LANES), F32), jax.ShapeDtypeStruct((nc, 1, LANES), F32)],
        in_specs=[pl.BlockSpec((rows_step, LANES), lambda c: (c, dt_block)), head_vec, head_vec],
        out_specs=[blk, blk, blk, pl.BlockSpec((per_step, SSD_CHUNK, LANES), lambda c: (c, 0, 0)),
                   pl.BlockSpec((per_step, 1, LANES), lambda c: (c, 0, 0))],
        compiler_params=_cparams(("parallel",)))(zx, bias_p, alog_p)


def _ssd_common(dt_ref, cs_ref, last_ref, b_ref, c_ref):
    c_sz = SSD_CHUNK
    lane = lax.broadcasted_iota(jnp.int32, (c_sz, LANES), 1)
    row = lax.broadcasted_iota(jnp.int32, (c_sz, LANES), 0)
    bb = b_ref[...].astype(BF16)
    cb = c_ref[...].astype(BF16)
    scores = _dot(cb, bb, "nt")
    return dict(lane=lane, row=row, dt=dt_ref[...], cs=cs_ref[...], last_row=last_ref[...], bb=bb, cb=cb,
                scores=scores, causal=row >= lane, lo=lane < SSD_P)


def _pair_terms(q, cst_ref, h0):
    lane, lo = q["lane"], q["lo"]
    out = {}
    cols, dts, lasts, lms = [], [], [], []
    lane1 = lax.broadcasted_iota(jnp.int32, (1, LANES), 1)
    for h in (h0, h0 + 1):
        col = _lane_col(q["cs"], lane, h)
        rowv = cst_ref[pl.ds(h, 1), :]
        lms.append(jnp.exp(jnp.where(q["causal"], col - rowv, -1e30)))
        cols.append(col)
        dts.append(_lane_col(q["dt"], lane, h))
        lasts.append(jnp.sum(jnp.where(lane1 == h, q["last_row"], 0.0), axis=1, keepdims=True))
    out["lm"] = lms
    out["cols"] = cols
    out["lasts"] = lasts
    out["dt_b"] = jnp.where(lo, dts[0], dts[1])
    out["e_b"] = jnp.where(lo, jnp.exp(cols[0]), jnp.exp(cols[1]))
    out["dec_cols"] = [jnp.exp(lasts[0] - cols[0]), jnp.exp(lasts[1] - cols[1])]
    out["dec_b"] = jnp.where(lo, out["dec_cols"][0], out["dec_cols"][1])
    lo1 = lane1 < SSD_P
    out["explast"] = [jnp.exp(lasts[0]), jnp.exp(lasts[1])]
    out["explast_b"] = jnp.where(lo1, out["explast"][0], out["explast"][1])
    return out


def _ssd_fwd(zx, xc, decay, d_lane, nw, d_inner, after, name):
    L = zx.shape[0]
    nc = L // SSD_CHUNK
    gw = d_inner // SSD_G
    heads = gw // SSD_P
    n_pair = heads // 2
    bc0 = d_inner // LANES

    def body(z_ref, xs_ref, b_ref, c_ref, dt_ref, cs_ref, cst_ref, last_ref, dl_ref, nw_ref, after_ref,
             y_ref, yn_ref, prev_ref, s_ref):
        @pl.when(pl.program_id(1) == 0)
        def _():
            s_ref[...] = jnp.zeros_like(s_ref)

        q = _ssd_common(dt_ref, cs_ref, last_ref, b_ref, c_ref)
        prev_ref[...] = s_ref[...]
        lo = q["lo"]
        for j in range(n_pair):
            sl = slice(j * LANES, (j + 1) * LANES)
            p = _pair_terms(q, cst_ref, pl.program_id(0) * heads + 2 * j)
            xs_p = xs_ref[:, sl]
            xp = xs_p * p["dt_b"]
            xb = xp.astype(BF16)
            m_a = (q["scores"] * p["lm"][0]).astype(BF16)
            m_b = (q["scores"] * p["lm"][1]).astype(BF16)
            yd = jnp.where(lo, _dot(m_a, xb), _dot(m_b, xb))
            s_p = s_ref[:, sl]
            yo = _dot(q["cb"], s_p.astype(BF16)) * p["e_b"]
            y_ref[:, sl] = yd + yo + dl_ref[:, sl] * xs_p
            st = _dot(q["bb"], (xp * p["dec_b"]).astype(BF16), "tn")
            s_ref[:, sl] = s_p * p["explast_b"] + st
        yv = y_ref[...]
        zv = z_ref[...]
        yg = yv * (zv * _sigmoid(zv))
        rstd = lax.rsqrt(jnp.mean(yg * yg, axis=-1, keepdims=True) + NORM_EPS)
        yn_ref[...] = (yg * rstd * nw_ref[...]).astype(BF16)

    grp = lambda width: pl.BlockSpec((None, 1, width), lambda g, c: (g, 0, 0))
    dt_, _, cs_, cst_, last_ = decay
    return _pcall(
        body, name=name, grid=(SSD_G, nc),
        out_shape=[jax.ShapeDtypeStruct((L, d_inner), F32), jax.ShapeDtypeStruct((L, d_inner), BF16),
                   jax.ShapeDtypeStruct((nc, SSD_G, SSD_N, gw), F32)],
        in_specs=[pl.BlockSpec((SSD_CHUNK, gw), lambda g, c: (c, g)),
                  pl.BlockSpec((SSD_CHUNK, gw), lambda g, c: (c, g)),
                  pl.BlockSpec((SSD_CHUNK, SSD_N), lambda g, c: (c, bc0 + g)),
                  pl.BlockSpec((SSD_CHUNK, SSD_N), lambda g, c: (c, bc0 + SSD_G + g)),
                  pl.BlockSpec((SSD_CHUNK, LANES), lambda g, c: (c, 0)),
                  pl.BlockSpec((SSD_CHUNK, LANES), lambda g, c: (c, 0)),
                  pl.BlockSpec((None, SSD_CHUNK, LANES), lambda g, c: (c, 0, 0)),
                  pl.BlockSpec((None, 1, LANES), lambda g, c: (c, 0, 0)),
                  grp(gw), grp(gw), pl.BlockSpec(memory_space=pl.ANY)],
        out_specs=[pl.BlockSpec((SSD_CHUNK, gw), lambda g, c: (c, g)),
                   pl.BlockSpec((SSD_CHUNK, gw), lambda g, c: (c, g)),
                   pl.BlockSpec((None, None, SSD_N, gw), lambda g, c: (c, g, 0, 0))],
        scratch_shapes=[pltpu.VMEM((SSD_N, gw), F32)],
        compiler_params=_cparams(("parallel", "arbitrary")))(
            zx, xc, xc, xc, dt_, cs_, cst_, last_, d_lane, nw, after)


def _ssd_bwd(dyn, y, zx, xc, prev, decay, alog_p, d_lane, nw, d_inner, name):
    L = zx.shape[0]
    nc = L // SSD_CHUNK
    gw = d_inner // SSD_G
    heads = gw // SSD_P
    n_pair = heads // 2
    bc0 = d_inner // LANES

    def body(dyn_ref, y_ref, z_ref, xs_ref, b_ref, c_ref, prev_ref, dt_ref, sg_ref, cs_ref, cst_ref, last_ref,
             alog_ref, dl_ref, nw_ref,
             dz_ref, dxs_ref, db_ref, dc_ref, ddt_ref, dbias_ref, dalog_ref, dd_ref, dnw_ref,
             ds_ref, racc_ref):
        @pl.when(pl.program_id(1) == 0)
        def _():
            ds_ref[...] = jnp.zeros_like(ds_ref)
            dbias_ref[...] = jnp.zeros_like(dbias_ref)
            dalog_ref[...] = jnp.zeros_like(dalog_ref)
            dd_ref[...] = jnp.zeros_like(dd_ref)
            dnw_ref[...] = jnp.zeros_like(dnw_ref)

        q = _ssd_common(dt_ref, cs_ref, last_ref, b_ref, c_ref)
        a_row = -jnp.exp(alog_ref[...])
        lane, row, lo = q["lane"], q["row"], q["lo"]
        lane1 = lax.broadcasted_iota(jnp.int32, (1, LANES), 1)
        head0 = pl.program_id(0) * heads
        mine = (lane >= head0) & (lane < head0 + heads)

        yv, zv, dynv, nwv = y_ref[...], z_ref[...], dyn_ref[...], nw_ref[...]
        sig = _sigmoid(zv)
        sz = zv * sig
        yg = yv * sz
        rstd = lax.rsqrt(jnp.mean(yg * yg, axis=-1, keepdims=True) + NORM_EPS)
        yhat = yg * rstd
        dnw_ref[...] += jnp.sum(dynv * yhat, axis=0, keepdims=True)
        dyh = dynv * nwv
        dyg = rstd * (dyh - yhat * jnp.mean(dyh * yhat, axis=-1, keepdims=True))
        dz_ref[...] = (dyg * yv * (sig * (1.0 + zv * (1.0 - sig)))).astype(BF16)
        dy_all = dyg * sz

        dg = jnp.zeros((SSD_CHUNK, SSD_CHUNK), F32)
        dc_acc = jnp.zeros((SSD_CHUNK, SSD_N), F32)
        db_acc = jnp.zeros((SSD_CHUNK, SSD_N), F32)
        dcs_mat = jnp.zeros((SSD_CHUNK, LANES), F32)
        ddt_mat = jnp.zeros((SSD_CHUNK, LANES), F32)
        dd_row = jnp.zeros((1, LANES), F32)
        racc_ref[...] = jnp.zeros_like(racc_ref)
        is_last = row == SSD_CHUNK - 1

        for j in range(n_pair):
            sl = slice(j * LANES, (j + 1) * LANES)
            ha, hb = head0 + 2 * j, head0 + 2 * j + 1
            p = _pair_terms(q, cst_ref, ha)
            xs_p = xs_ref[:, sl]
            dyp = dy_all[:, sl]
            xp = xs_p * p["dt_b"]
            xb = xp.astype(BF16)
            s_p = prev_ref[:, sl]
            s_pb = s_p.astype(BF16)
            dsn = ds_ref[:, sl]
            dsnb = dsn.astype(BF16)
            m_f = [q["scores"] * p["lm"][0], q["scores"] * p["lm"][1]]

            t0 = dyp * xs_p
            dd_row = dd_row + jnp.where(lane1 == ha, _sum_all(jnp.where(lo, t0, 0.0)), 0.0) \
                + jnp.where(lane1 == hb, _sum_all(jnp.where(lo, 0.0, t0)), 0.0)
            dxs_p = dl_ref[:, sl] * dyp

            yo = _dot(q["cb"], s_pb) * p["e_b"]
            dcs_b = (dyp * p["e_b"]).astype(BF16)
            dc_acc = dc_acc + _dot(dcs_b, s_pb, "nt")
            ds_yo = _dot(q["cb"], dcs_b, "tn")
            t1 = dyp * yo
            dcs_cols = [jnp.sum(jnp.where(lo, t1, 0.0), axis=1, keepdims=True),
                        jnp.sum(jnp.where(lo, 0.0, t1), axis=1, keepdims=True)]

            t2 = dsn * s_p
            dlast = [p["explast"][0] * _sum_all(jnp.where(lo, t2, 0.0)),
                     p["explast"][1] * _sum_all(jnp.where(lo, 0.0, t2))]
            ds_ref[:, sl] = dsn * p["explast_b"] + ds_yo
            w = _dot(q["bb"], dsnb)
            db_acc = db_acc + _dot((xp * p["dec_b"]).astype(BF16), dsnb, "nt")
            dxp = w * p["dec_b"]
            t3 = w * xp
            e = [jnp.sum(jnp.where(lo, t3, 0.0), axis=1, keepdims=True) * p["dec_cols"][0],
                 jnp.sum(jnp.where(lo, 0.0, t3), axis=1, keepdims=True) * p["dec_cols"][1]]
            for i in range(2):
                dlast[i] = dlast[i] + jnp.sum(e[i], axis=0, keepdims=True)
                dcs_cols[i] = dcs_cols[i] - e[i]

            dyb = dyp.astype(BF16)
            dy_h = [jnp.where(lo, dyp, 0.0).astype(BF16), jnp.where(lo, 0.0, dyp).astype(BF16)]
            dms = [_dot(dy_h[0], xb, "nt"), _dot(dy_h[1], xb, "nt")]
            dxp = dxp + jnp.where(lo, _dot(m_f[0].astype(BF16), dyb, "tn"), _dot(m_f[1].astype(BF16), dyb, "tn"))
            for i, h in enumerate((ha, hb)):
                dg = dg + dms[i] * p["lm"][i]
                qm = dms[i] * m_f[i]
                dcs_cols[i] = dcs_cols[i] + jnp.sum(qm, axis=1, keepdims=True)
                racc_ref[pl.ds(h, 1), :] = jnp.sum(qm, axis=0, keepdims=True)

            dxs_ref[:, sl] = dxs_p + dxp * p["dt_b"]
            t4 = dxp * xs_p
            ddt_cols = [jnp.sum(jnp.where(lo, t4, 0.0), axis=1, keepdims=True),
                        jnp.sum(jnp.where(lo, 0.0, t4), axis=1, keepdims=True)]
            for i, h in enumerate((ha, hb)):
                sel = lane == h
                dcs_mat = dcs_mat + jnp.where(sel, dcs_cols[i], 0.0) + jnp.where(sel & is_last, dlast[i], 0.0)
                ddt_mat = ddt_mat + jnp.where(sel, ddt_cols[i], 0.0)

        dcs_mat = dcs_mat - racc_ref[...].T
        tri_t = (row <= lane).astype(BF16)
        da = _dot_exact01(tri_t, dcs_mat)
        ddt = ddt_mat + da * a_row
        dalog_ref[...] += jnp.sum(jnp.where(mine, da * q["dt"], 0.0), axis=0, keepdims=True) * a_row
        draw = jnp.where(mine, ddt * sg_ref[...], 0.0)
        ddt_ref[...] = draw
        dbias_ref[...] += jnp.sum(draw, axis=0, keepdims=True)
        dd_ref[...] += dd_row
        dgb = dg.astype(BF16)
        dc_ref[...] = dc_acc + _dot(dgb, q["bb"])
        db_ref[...] = db_acc + _dot(dgb, q["cb"], "tn")

    rev = lambda c: nc - 1 - c
    grp = lambda width: pl.BlockSpec((None, 1, width), lambda g, c: (g, 0, 0))
    blk = lambda width, off: pl.BlockSpec((SSD_CHUNK, width), lambda g, c: (rev(c), off + g))
    head_vec = pl.BlockSpec((1, LANES), lambda g, c: (0, 0))
    chunk_rows = pl.BlockSpec((SSD_CHUNK, LANES), lambda g, c: (rev(c), 0))
    dt_, sg_, cs_, cst_, last_ = decay
    return _pcall(
        body, name=name, grid=(SSD_G, nc),
        out_shape=[jax.ShapeDtypeStruct(zx.shape, BF16), jax.ShapeDtypeStruct((L, d_inner), F32),
                   jax.ShapeDtypeStruct((L, SSD_G * SSD_N), F32), jax.ShapeDtypeStruct((L, SSD_G * SSD_N), F32),
                   jax.ShapeDtypeStruct((SSD_G, L, LANES), F32),
                   jax.ShapeDtypeStruct((SSD_G, 1, LANES), F32), jax.ShapeDtypeStruct((SSD_G, 1, LANES), F32),
                   jax.ShapeDtypeStruct((SSD_G, 1, LANES), F32), jax.ShapeDtypeStruct((SSD_G, 1, gw), F32)],
        in_specs=[blk(gw, 0), blk(gw, 0), blk(gw, 0), blk(gw, 0), blk(SSD_N, bc0), blk(SSD_N, bc0 + SSD_G),
                  pl.BlockSpec((None, None, SSD_N, gw), lambda g, c: (rev(c), g, 0, 0)),
                  chunk_rows, chunk_rows, chunk_rows,
                  pl.BlockSpec((None, SSD_CHUNK, LANES), lambda g, c: (rev(c), 0, 0)),
                  pl.BlockSpec((None, 1, LANES), lambda g, c: (rev(c), 0, 0)),
                  head_vec, grp(gw), grp(gw)],
        out_specs=[blk(gw, 0), blk(gw, 0), blk(SSD_N, 0), blk(SSD_N, 0),
                   pl.BlockSpec((None, SSD_CHUNK, LANES), lambda g, c: (g, rev(c), 0)),
                   grp(LANES), grp(LANES), grp(LANES), grp(gw)],
        scratch_shapes=[pltpu.VMEM((SSD_N, gw), F32), pltpu.VMEM((SSD_CHUNK, LANES), F32)],
        compiler_params=_cparams(("parallel", "arbitrary")))(
            dyn, y, zx, xc, xc, xc, prev, dt_, sg_, cs_, cst_, last_, alog_p, d_lane, nw)


def _cond_mod(c_pad, ada_w, ada_b_loc, after, name):
    depth, D, n = ada_w.shape
    rows = c_pad.shape[0]

    def body(c_ref, w_ref, b_ref, after_ref, mod_ref, cond_ref):
        cv = c_ref[...]
        cond = cv * _sigmoid(cv)
        cond_ref[...] = cond
        mod_ref[...] = _dot(cond.astype(BF16), w_ref[...].astype(BF16)) + b_ref[...]

    return _pcall(
        body, name=name, grid=(depth,),
        out_shape=[jax.ShapeDtypeStruct((depth, rows, n), F32), jax.ShapeDtypeStruct((rows, D), F32)],
        in_specs=[pl.BlockSpec((rows, D), lambda i: (0, 0)),
                  pl.BlockSpec((None, D, n), lambda i: (i, 0, 0)),
                  pl.BlockSpec((None, 1, n), lambda i: (i, 0, 0)),
                  pl.BlockSpec(memory_space=pl.ANY)],
        out_specs=[pl.BlockSpec((None, rows, n), lambda i: (i, 0, 0)),
                   pl.BlockSpec((rows, D), lambda i: (0, 0))],
        compiler_params=_cparams(("arbitrary",)))(c_pad, ada_w, ada_b_loc, after)


def _adamw_math(g, w, m, v):
    m_new = ADAM_B1 * m + (1.0 - ADAM_B1) * g
    v_new = ADAM_B2 * v + (1.0 - ADAM_B2) * (g * g)
    m_hat = m_new / (1.0 - ADAM_B1 ** ADAM_STEP)
    v_hat = v_new / (1.0 - ADAM_B2 ** ADAM_STEP)
    delta = -ADAM_LR * (m_hat / (jnp.sqrt(v_hat) + ADAM_EPS) + ADAM_WD * w)
    return delta, m_new, v_new


def _adamw_sum(parts, w, m, v, layer, name, prev=None, tr=None):
    depth, R, C = w.shape
    tr = _tile(R, tr if tr is not None else (512 if C <= 512 else 256))

    def body(p_ref, w_ref, m_ref, v_ref, *rest):
        g_ref, d_ref, mo_ref, vo_ref = rest[-4:]
        g = p_ref[0].astype(F32)
        for k in range(1, N_DEV):
            g = g + p_ref[k].astype(F32)
        d, mn, vn = _adamw_math(g, w_ref[...], m_ref[...], v_ref[...])
        g_ref[...] = g
        d_ref[...] = d
        mo_ref[...] = mn
        vo_ref[...] = vn

    blk = pl.BlockSpec((None, tr, C), lambda i: (layer, i, 0))
    prev = list(prev) if prev is not None else []
    return _pcall(
        body, name=name, grid=(R // tr,),
        out_shape=[jax.ShapeDtypeStruct((depth, R, C), F32)] * 4,
        in_specs=[pl.BlockSpec((N_DEV, tr, C), lambda i: (0, i, 0)), blk, blk, blk]
        + [pl.BlockSpec(memory_space=pl.ANY)] * len(prev),
        out_specs=[blk] * 4, input_output_aliases={4 + k: k for k in range(len(prev))},
        compiler_params=_cparams(("parallel",)))(parts, w, m, v, *prev)


def _adamw_small(parts, wmv, head_parts, head_wmv, loss_parts, name):
    n, nh = len(parts), len(head_parts)
    n_heads = head_wmv[0][0].shape[1] if nh else 0
    groups = head_parts[0].shape[1] if nh else 0
    d_model = loss_parts.shape[2]

    def body(*refs):
        p_refs, refs = refs[:n], refs[n:]
        wmv_refs, refs = refs[:3 * n], refs[3 * n:]
        hp_refs, refs = refs[:nh], refs[nh:]
        hwmv_refs, refs = refs[:3 * nh], refs[3 * nh:]
        loss_ref, refs = refs[0], refs[1:]
        outs, loss_out, head_scr = refs[:4 * (n + nh)], refs[4 * (n + nh)], refs[4 * (n + nh) + 1]

        def update(i, g, w_ref, m_ref, v_ref):
            res = (g,) + _adamw_math(g, w_ref[...], m_ref[...], v_ref[...])
            for o_ref, r in zip(outs[4 * i:4 * i + 4], res):
                o_ref[...] = r

        for i in range(n):
            g = p_refs[i][0]
            for k in range(1, N_DEV):
                g = g + p_refs[i][k]
            update(i, g, *wmv_refs[3 * i:3 * i + 3])
        for i in range(nh):
            g = None
            for k in range(N_DEV):
                for grp in range(groups):
                    g = hp_refs[i][k, grp] if g is None else g + hp_refs[i][k, grp]
            head_scr[...] = g
            update(n + i, head_scr[:, 0:n_heads], *hwmv_refs[3 * i:3 * i + 3])
        tot = loss_ref[0]
        for k in range(1, N_DEV):
            tot = tot + loss_ref[k]
        loss_out[...] = jnp.broadcast_to(_sum_all(tot) * (0.5 / d_model), loss_out.shape)

    operands = list(parts) + [a for t in wmv for a in t] + list(head_parts) + [a for t in head_wmv for a in t]
    operands.append(loss_parts)
    out_shape = [jax.ShapeDtypeStruct(t[0].shape, F32) for t in list(wmv) + list(head_wmv) for _ in range(4)]
    out_shape.append(jax.ShapeDtypeStruct((1, LANES), F32))
    vmem = pl.BlockSpec(memory_space=pltpu.VMEM)
    outs = _pcall(body, name=name, out_shape=out_shape, in_specs=[vmem] * len(operands),
                  out_specs=[vmem] * len(out_shape), scratch_shapes=[pltpu.VMEM((1, LANES), F32)],
                  compiler_params=_cparams())(*operands)
    return [outs[4 * i:4 * i + 4] for i in range(n + nh)], outs[-1]


def _ada_adamw(cond_pad, dmod_pad, w, m, v, name, tr=512):
    depth, D, n = w.shape
    rows = cond_pad.shape[0]
    tr = _tile(D, tr)

    def body(c_ref, dm_ref, w_ref, m_ref, v_ref, g_ref, d_ref, mo_ref, vo_ref):
        g = _dot(c_ref[...].astype(BF16), dm_ref[...].astype(BF16), "tn")
        d, mn, vn = _adamw_math(g, w_ref[...], m_ref[...], v_ref[...])
        g_ref[...] = g
        d_ref[...] = d
        mo_ref[...] = mn
        vo_ref[...] = vn

    blk = pl.BlockSpec((None, tr, n), lambda i, r: (i, r, 0))
    return _pcall(
        body, name=name, grid=(depth, D // tr),
        out_shape=[jax.ShapeDtypeStruct((depth, D, n), F32)] * 4,
        in_specs=[pl.BlockSpec((rows, tr), lambda i, r: (0, r)),
                  pl.BlockSpec((None, rows, n), lambda i, r: (i, 0, 0)), blk, blk, blk],
        out_specs=[blk] * 4, compiler_params=_cparams(("parallel", "parallel")))(cond_pad, dmod_pad, w, m, v)


def kernel(x, c, ada_w, ada_b, mix_norm_w, mlp_norm_w, mlp_up, mlp_down, ssd_in_w, ssd_conv_w, ssd_conv_b, ssd_dt_bias, ssd_A_log, ssd_D, ssd_norm_w, ssd_out_w, sc_in_w, sc_conv_w, sc_out_w, final_norm_w, loss_target, m_ada_w, m_ada_b, m_mix_norm_w, m_mlp_norm_w, m_mlp_up, m_mlp_down, m_ssd_in_w, m_ssd_conv_w, m_ssd_conv_b, m_ssd_dt_bias, m_ssd_A_log, m_ssd_D, m_ssd_norm_w, m_ssd_out_w, m_sc_in_w, m_sc_conv_w, m_sc_out_w, m_final_norm_w, v_ada_w, v_ada_b, v_mix_norm_w, v_mlp_norm_w, v_mlp_up, v_mlp_down, v_ssd_in_w, v_ssd_conv_w, v_ssd_conv_b, v_ssd_dt_bias, v_ssd_A_log, v_ssd_D, v_ssd_norm_w, v_ssd_out_w, v_sc_in_w, v_sc_conv_w, v_sc_out_w, v_final_norm_w):
    weights = dict(ada_w=ada_w, ada_b=ada_b, mix_norm_w=mix_norm_w, mlp_norm_w=mlp_norm_w, mlp_up=mlp_up,
                   mlp_down=mlp_down, ssd_in_w=ssd_in_w, ssd_conv_w=ssd_conv_w, ssd_conv_b=ssd_conv_b,
                   ssd_dt_bias=ssd_dt_bias, ssd_A_log=ssd_A_log, ssd_D=ssd_D, ssd_norm_w=ssd_norm_w,
                   ssd_out_w=ssd_out_w, sc_in_w=sc_in_w, sc_conv_w=sc_conv_w, sc_out_w=sc_out_w,
                   final_norm_w=final_norm_w)
    moms = dict(ada_w=m_ada_w, ada_b=m_ada_b, mix_norm_w=m_mix_norm_w, mlp_norm_w=m_mlp_norm_w, mlp_up=m_mlp_up,
                mlp_down=m_mlp_down, ssd_in_w=m_ssd_in_w, ssd_conv_w=m_ssd_conv_w, ssd_conv_b=m_ssd_conv_b,
                ssd_dt_bias=m_ssd_dt_bias, ssd_A_log=m_ssd_A_log, ssd_D=m_ssd_D, ssd_norm_w=m_ssd_norm_w,
                ssd_out_w=m_ssd_out_w, sc_in_w=m_sc_in_w, sc_conv_w=m_sc_conv_w, sc_out_w=m_sc_out_w,
                final_norm_w=m_final_norm_w)
    vars_ = dict(ada_w=v_ada_w, ada_b=v_ada_b, mix_norm_w=v_mix_norm_w, mlp_norm_w=v_mlp_norm_w, mlp_up=v_mlp_up,
                 mlp_down=v_mlp_down, ssd_in_w=v_ssd_in_w, ssd_conv_w=v_ssd_conv_w, ssd_conv_b=v_ssd_conv_b,
                 ssd_dt_bias=v_ssd_dt_bias, ssd_A_log=v_ssd_A_log, ssd_D=v_ssd_D, ssd_norm_w=v_ssd_norm_w,
                 ssd_out_w=v_ssd_out_w, sc_in_w=v_sc_in_w, sc_conv_w=v_sc_conv_w, sc_out_w=v_sc_out_w,
                 final_norm_w=v_final_norm_w)
    names = list(weights)

    L, D = x.shape[1], x.shape[2]
    d_inner = 2 * D
    n_heads = d_inner // SSD_P
    hpg = n_heads // SSD_G
    gw = d_inner // SSD_G
    conv_dim = d_inner + 2 * SSD_G * SSD_N
    zx_dim = d_inner + conv_dim
    zx_pad = -(-(zx_dim + LANES) // 512) * 512
    in_ws = ssd_in_w.shape[2]
    in_base, in_off, in_win = _window_geometry(in_ws)
    me = _my_index()
    x0 = x[0]
    tgt = loss_target[0]

    n_mod = ada_w.shape[2]
    (c_all,) = _exchange([c], "gather_c", gather=True)
    gather_handle = {}
    (gather_handle["ssd_in_w"],), token_in = _xfer_start(
        [ssd_in_w[0].astype(BF16)], "gather_start_ssd_in_w", gather=True, via_sibling=(0,), after=(c_all,))
    c_pad = jnp.pad(c_all.reshape(N_DEV, D), ((0, 16 - N_DEV), (0, 0)))
    ada_b_loc = lax.dynamic_slice_in_dim(ada_b, me * n_mod, n_mod, axis=1).reshape(2, 1, n_mod)
    mod_blk, cond_pad = _cond_mod(c_pad, ada_w, ada_b_loc, token_in, "cond_mod")
    gather_order = ["mod", "ssd_conv_w", "sc_conv_w", "ssd_out_w", "up0", "down0", "sc_in_w", "sc_out_w", "up1",
                    "down1"]
    gather_src = dict(mod=mod_blk, ssd_conv_w=ssd_conv_w[0], sc_conv_w=sc_conv_w[0],
                      ssd_out_w=ssd_out_w[0].astype(BF16),
                      up0=mlp_up[0].astype(BF16), down0=mlp_down[0].astype(BF16),
                      sc_in_w=sc_in_w[0].astype(BF16), sc_out_w=sc_out_w[0].astype(BF16),
                      up1=mlp_up[1].astype(BF16), down1=mlp_down[1].astype(BF16))
    handles, gather_token = _xfer_start([gather_src[k] for k in gather_order], "gather_start", gather=True,
                                        via_sibling=tuple(range(3, len(gather_order))))
    gather_handle.update(zip(gather_order, handles))

    def gathered(keys, after, forward):
        tag = "_".join(keys)
        lands = _xfer_wait([gather_handle[k] for k in keys], after, f"gather_wait_{tag}", gather=True)
        return _sibling_forward(lands, f"gather_forward_{tag}") if forward else lands

    def forward_behind(keys, after):
        tag = "_".join(keys)
        lands = _xfer_wait([gather_handle[k] for k in keys], after, f"gather_wait_{tag}", gather=True)
        fwd_handles, token = _sibling_forward_start(lands, f"gather_forward_start_{tag}")
        return (lambda done: _sibling_forward_wait(fwd_handles, done, f"gather_forward_wait_{tag}")), token

    (ssd_in_g,) = gathered(["ssd_in_w"], (gather_token, m_ssd_in_w, v_ssd_in_w), True)
    w_in_all = _shards_to_columns(ssd_in_g, in_base, in_off, in_win, zx_pad, "ssd_in_w_columns")
    (mod_all,) = gathered(["mod"], w_in_all, False)
    mod_mine = lax.dynamic_index_in_dim(mod_all, me, axis=2, keepdims=False)
    mod_mine = jnp.transpose(mod_mine, (1, 0, 2)).reshape(2, 6, 1, D)
    sh_m, sc_m, g_m, sh_f, sc_f, g_f = [[mod_mine[i, k] for i in range(2)] for k in range(6)]

    vec = lambda a: a.reshape(1, -1)
    small = {}

    _, h0 = _norm_mod_fwd(x0, None, None, vec(mix_norm_w[0]), sc_m[0], sh_m[0], "l0_mix_norm")
    cw_all, scw_all = gathered(["ssd_conv_w", "sc_conv_w"], h0, False)
    (zx,) = _mm_nn(h0, w_in_all, F32, "ssd_in_proj", tm=2048, tn=512)
    conv_b0 = vec(ssd_conv_b[0])
    conv_w_full = jnp.transpose(cw_all, (1, 0, 2)).reshape(SSD_K, conv_dim)
    sc_conv_full = jnp.transpose(scw_all, (1, 0, 2)).reshape(SC_K, D)
    xc = _ssd_conv_fwd(zx, conv_w_full, conv_b0, d_inner, conv_dim, "ssd_conv")
    bias_p = jnp.pad(ssd_dt_bias[0], (0, LANES - n_heads)).reshape(1, LANES)
    alog_p = jnp.pad(ssd_A_log[0], (0, LANES - n_heads)).reshape(1, LANES)
    d_lane = jnp.repeat(ssd_D[0], SSD_P).reshape(SSD_G, 1, gw)
    nw_g = ssd_norm_w[0].reshape(SSD_G, 1, gw)
    finish, token = forward_behind(["ssd_out_w"], xc)
    decay = _ssd_decay(zx, bias_p, alog_p, n_heads, zx_dim // LANES, "ssd_decay")
    y_ssd, yn, prev = _ssd_fwd(zx, xc, decay, d_lane, nw_g, d_inner, token, "ssd_scan")
    ups, downs = [None, None], [None, None]
    (ssd_out_g,) = finish(yn)
    w_ssd_out = ssd_out_g.reshape(-1, D)
    finish, token = forward_behind(["up0", "down0"], ssd_out_g)
    (mix0,) = _mm_nn(yn, w_ssd_out, F32, "ssd_out_proj", after=(token,))
    x1, h1 = _norm_mod_fwd(x0, mix0, g_m[0], vec(mlp_norm_w[0]), sc_f[0], sh_f[0], "l0_mlp_norm")
    ups[0], down0_g = finish(h1)
    downs[0] = down0_g.reshape(-1, D)
    u0, s0 = _mm_nn_blocked(h1, ups[0], "l0_mlp_up", _ep_relu2, [BF16, BF16])
    finish, token = forward_behind(["sc_in_w", "sc_out_w", "up1", "down1"], s0)
    (d0,) = _mm_nn(s0, downs[0], F32, "l0_mlp_down", after=(token,))
    x2, h2 = _norm_mod_fwd(x1, d0, g_f[0], vec(mix_norm_w[1]), sc_m[1], sh_m[1], "l1_mix_norm")
    sc_in_g, sc_out_g, ups[1], down1_g = finish(h2)
    w_sc_out, downs[1] = sc_out_g.reshape(-1, D), down1_g.reshape(-1, D)
    (proj,) = _mm_nn_blocked(h2, sc_in_g, "sc_in_proj", _ep_store(F32), [F32])
    yc = _sc_conv_fwd(proj, sc_conv_full, "sc_conv")
    (mix1,) = _mm_nn(yc, w_sc_out, F32, "sc_out_proj")
    x3, h3 = _norm_mod_fwd(x2, mix1, g_m[1], vec(mlp_norm_w[1]), sc_f[1], sh_f[1], "l1_mlp_norm")
    u1, s1 = _mm_nn_blocked(h3, ups[1], "l1_mlp_up", _ep_relu2, [BF16, BF16])
    (d1,) = _mm_nn(s1, downs[1], F32, "l1_mlp_down")

    dx, loss_lane, dfw, dd1, dg = _final_loss(x3, d1, g_f[1], vec(final_norm_w), tgt, "final_loss")
    small["final_norm_w"] = dfw

    dmod = [[None] * 6 for _ in range(2)]
    dmod[1][5] = dg

    def mlp_backward(i, dx_out, dd, x_mid, h_in, u, s, mix, gate):
        du = _mm_nt(dd, downs[i], BF16, f"l{i}_mlp_down_bwd", epilogue=_ep_relu2_bwd, extra=(u,))
        gdown = _mm_tn(s, dd, BF16, f"l{i}_mlp_down_wgrad").reshape(N_DEV, -1, D)
        gup = _mm_tn_blocked(h_in, du, BF16, f"l{i}_mlp_up_wgrad")
        (h_down, h_up), token = _xfer_start([gdown, gup], f"l{i}_mlp_grads_start", gather=False)
        grad_handle[f"mlp_down{i}"], grad_handle[f"mlp_up{i}"] = h_down, h_up
        dh = _mm_nt_blocked(du, ups[i], F32, f"l{i}_mlp_up_bwd", after=(token,))
        dxm, dsh, dsc, dnw, dmix, dgate = _norm_mod_bwd(dh, x_mid, vec(mlp_norm_w[i]), sc_f[i], dx_out,
                                                        f"l{i}_mlp_norm_bwd", branch=(mix, gate))
        dmod[i][3], dmod[i][4], dmod[i][2] = dsh, dsc, dgate
        return dxm, dmix, dnw

    grad_handle = {}
    dx3, dyc, dnw_mlp1 = mlp_backward(1, dx, dd1, x3, h3, u1, s1, mix1, g_m[1])
    g_sc_out = _mm_tn(yc, dyc, BF16, "sc_out_wgrad").reshape(N_DEV, -1, D)
    dconv_out = _mm_nt(dyc, w_sc_out, F32, "sc_out_bwd")
    dbg, dcg, dxv, dscw = _sc_conv_bwd(proj, sc_conv_full, dconv_out, "sc_conv_bwd")
    dproj = jnp.concatenate([dbg, dcg, dxv], axis=1)
    g_sc_in = _mm_tn_blocked(h2, dproj, BF16, "sc_in_wgrad")
    (grad_handle["sc_out_w0"], grad_handle["sc_in_w0"]), token = _xfer_start(
        [g_sc_out, g_sc_in], "sc_grads_start", gather=False)
    dh2 = _mm_nt_blocked(dproj, sc_in_g, F32, "sc_in_bwd", after=(token,))
    dx2, dsh, dsc, dnw_mix1, dd0, dg = _norm_mod_bwd(dh2, x2, vec(mix_norm_w[1]), sc_m[1], dx3, "l1_mix_norm_bwd",
                                                     branch=(d0, g_f[0]))
    dmod[1][0], dmod[1][1], dmod[0][5] = dsh, dsc, dg
    dx1, dyo, dnw_mlp0 = mlp_backward(0, dx2, dd0, x1, h1, u0, s0, mix0, g_m[0])
    g_ssd_out = _mm_tn(yn, dyo, BF16, "ssd_out_wgrad").reshape(N_DEV, -1, D)
    (grad_handle["ssd_out_w0"],), token = _xfer_start([g_ssd_out], "ssd_out_grad_start", gather=False)
    dyn = _mm_nt(dyo, w_ssd_out, F32, "ssd_out_bwd", after=(token,))
    dz, dxs, db_, dc_, ddt, dbias, dalog, dd_, dnw_ssd = _ssd_bwd(
        dyn, y_ssd, zx, xc, prev, decay, alog_p, d_lane, nw_g, d_inner, "ssd_scan_bwd")
    dzx, dcw, dcb = _ssd_conv_bwd(zx, conv_w_full, conv_b0, [dxs, db_, dc_], dz, d_inner, "ssd_conv_bwd")
    dzx = _dzx_finish(dzx, ddt, zx_dim, "ssd_dzx_finish")
    g_in_all = _mm_tn(h0, dzx, BF16, "ssd_in_wgrad", tn=512, tk=2048)
    g_ssd_in = _columns_to_shards(g_in_all, in_ws, in_base, in_off, in_win, "ssd_in_wgrad_shards")
    (grad_handle["ssd_in_w0"],), token = _xfer_start([g_ssd_in], "ssd_in_grad_start", gather=False)
    dh0 = _mm_nt(dzx, w_in_all, F32, "ssd_in_bwd", tm=1024, tk=dzx.shape[1] // 2, after=(token,))
    grad_x, dsh, dsc, dnw_mix0 = _norm_mod_bwd(dh0, x0, vec(mix_norm_w[0]), sc_m[0], dx1, "l0_mix_norm_bwd")
    dmod[0][0], dmod[0][1] = dsh, dsc

    small["ada_b"] = jnp.concatenate([jnp.concatenate(dmod[i], axis=1) for i in range(2)], axis=0)
    small["mix_norm_w"] = jnp.concatenate([dnw_mix0, dnw_mix1], axis=0)
    small["mlp_norm_w"] = jnp.concatenate([dnw_mlp0, dnw_mlp1], axis=0)
    small["ssd_conv_w"] = dcw
    small["ssd_conv_b"] = dcb
    small["ssd_norm_w"] = dnw_ssd.reshape(1, d_inner)
    small["sc_conv_w"] = dscw
    small["loss"] = loss_lane
    small_names = list(small)
    head_names = ["ssd_dt_bias", "ssd_A_log", "ssd_D"]
    handles, small_token = _xfer_start([small[k] for k in small_names] + [dbias, dalog, dd_],
                                       "small_grads_start", gather=True)

    out_g, out_d, out_m, out_v = {}, {}, {}, {}

    layer_res = {}

    def big_update(name, i, after):
        (parts,) = _xfer_wait([grad_handle[f"{name}{i}"]], after, f"grads_wait_{name}_{i}", gather=False)
        res = _adamw_sum(parts, weights[name], moms[name], vars_[name], i, f"adamw_{name}_{i}",
                         prev=layer_res.get(name))
        layer_res[name] = res
        return res[1]

    chain = small_token
    for name, i in [("mlp_down", 1), ("mlp_up", 1), ("sc_out_w", 0), ("sc_in_w", 0), ("mlp_down", 0),
                    ("mlp_up", 0), ("ssd_out_w", 0), ("ssd_in_w", 0)]:
        chain = big_update(name, i, chain)
    gathered_small = _xfer_wait(handles, chain, "small_grads_wait", gather=True)
    small_all = dict(zip(small_names + head_names, gathered_small))

    dmod_loc = lax.dynamic_slice_in_dim(small_all["ada_b"], me * n_mod, n_mod, axis=2)
    dmod_pad = jnp.pad(jnp.transpose(dmod_loc, (1, 0, 2)), ((0, 0), (0, 16 - N_DEV), (0, 0)))
    out_g["ada_w"], out_d["ada_w"], out_m["ada_w"], out_v["ada_w"] = _ada_adamw(
        cond_pad, dmod_pad, ada_w, m_ada_w, v_ada_w, "adamw_ada_w")

    for k in ("ssd_conv_w", "sc_conv_w"):
        n_loc = weights[k].shape[2]
        small_all[k] = lax.dynamic_slice_in_dim(small_all[k], me * n_loc, n_loc, axis=2)
    plain = [k for k in small_names if k != "loss"]
    as2d = lambda a: a.reshape(-1, a.shape[-1])
    res, loss_row = _adamw_small(
        [small_all[k] for k in plain], [tuple(as2d(d[k]) for d in (weights, moms, vars_)) for k in plain],
        [small_all[k] for k in head_names], [tuple(as2d(d[k]) for d in (weights, moms, vars_)) for k in head_names],
        small_all["loss"], "adamw_small")
    loss = loss_row[0, 0]
    for k, res4 in zip(plain + head_names, res):
        for r, dst in zip(res4, (out_g, out_d, out_m, out_v)):
            dst[k] = r.reshape(weights[k].shape)
    for name, res4 in layer_res.items():
        for r, dst in zip(res4, (out_g, out_d, out_m, out_v)):
            dst[name] = r

    return (loss, grad_x[None], *[out_g[k] for k in names], *[out_d[k] for k in names],
            *[out_m[k] for k in names], *[out_v[k] for k in names])
```

```python
import jax
import jax.numpy as jnp
from jax import lax
from jax.experimental import pallas as pl
from jax.experimental.pallas import tpu as pltpu

F32 = jnp.float32
BF16 = jnp.bfloat16
N_DEV = 8
MESH = pl.DeviceIdType.MESH

NORM_EPS = 1e-5
SSD_G = 4
SSD_P = 64
SSD_N = 128
SSD_CHUNK = 128
SSD_K = 4
SC_K = 3
LANES = 128

ADAM_LR = 0.001
ADAM_B1 = 0.9
ADAM_B2 = 0.999
ADAM_EPS = 1e-08
ADAM_WD = 0.01
ADAM_STEP = 10

VMEM_LIMIT = 56 * 1024 * 1024


def _pcall(body, **kw):
    return pl.pallas_call(body, **kw)


def _cparams(sem=None):
    if sem is None:
        return pltpu.CompilerParams(vmem_limit_bytes=VMEM_LIMIT)
    return pltpu.CompilerParams(dimension_semantics=sem, vmem_limit_bytes=VMEM_LIMIT)


def _my_index():
    return 4 * lax.axis_index("x") + 2 * lax.axis_index("y") + lax.axis_index("c")


_PEER_MASKS = [(0, 0, 1), (0, 1, 0), (0, 1, 1), (1, 0, 0), (1, 0, 1), (1, 1, 0), (1, 1, 1)]


def _peers():
    x, y, c = lax.axis_index("x"), lax.axis_index("y"), lax.axis_index("c")
    out = []
    for mx, my, mc in _PEER_MASKS:
        px = (1 - x) if mx else x
        py = (1 - y) if my else y
        pc = (1 - c) if mc else c
        out.append(((px, py, pc), 4 * px + 2 * py + pc))
    return out


def _exchange(arrs, name, gather):
    n = len(arrs)
    n_peer = N_DEV - 1

    def body(*refs):
        ins, outs = refs[:n], refs[n:2 * n]
        send_sems, recv_sems, local_sems = refs[2 * n:]
        me = _my_index()
        peers = _peers()
        started = []
        for a in range(n):
            src_own = ins[a] if gather else ins[a].at[me]
            own = pltpu.make_async_copy(src_own, outs[a].at[me], local_sems.at[a])
            own.start()
            started.append(own)
        sends = []
        for a in range(n):
            for k, (peer, pidx) in enumerate(peers):
                src = ins[a] if gather else ins[a].at[pidx]
                cp = pltpu.make_async_remote_copy(
                    src_ref=src, dst_ref=outs[a].at[me],
                    send_sem=send_sems.at[a * n_peer + k], recv_sem=recv_sems.at[a * n_peer + k],
                    device_id=peer, device_id_type=MESH)
                cp.start()
                sends.append(cp)
        for a in range(n):
            for k, (peer, pidx) in enumerate(peers):
                src = ins[a] if gather else ins[a].at[pidx]
                pltpu.make_async_remote_copy(
                    src_ref=src, dst_ref=outs[a].at[pidx],
                    send_sem=send_sems.at[a * n_peer + k], recv_sem=recv_sems.at[a * n_peer + k],
                    device_id=peer, device_id_type=MESH).wait_recv()
        for cp in sends:
            cp.wait_send()
        for own in started:
            own.wait()

    if gather:
        out_shape = [jax.ShapeDtypeStruct((N_DEV,) + a.shape, a.dtype) for a in arrs]
    else:
        out_shape = [jax.ShapeDtypeStruct(a.shape, a.dtype) for a in arrs]
    any_spec = pl.BlockSpec(memory_space=pl.ANY)
    outs = _pcall(
        body, name=name, out_shape=out_shape,
        in_specs=[any_spec] * n, out_specs=[any_spec] * n,
        scratch_shapes=[pltpu.SemaphoreType.DMA((n * n_peer,)), pltpu.SemaphoreType.DMA((n * n_peer,)),
                        pltpu.SemaphoreType.DMA((n,))],
        compiler_params=pltpu.CompilerParams(has_side_effects=True),
    )(*arrs)
    return list(outs)


def _sibling_forward_start(lands, name):
    n = len(lands)
    n_fwd = len(_OTHER_CHIPS)

    def body(*refs):
        ins, bufs = refs[:n], refs[3 * n:4 * n]
        token = refs[-1]
        sibling = (lax.axis_index("x"), lax.axis_index("y"), 1 - lax.axis_index("c"))
        peers = _peers()
        for a in range(n):
            send_sems, recv_sems = refs[n + 2 * a], refs[n + 2 * a + 1]
            for j, k in enumerate(_OTHER_CHIPS):
                slot = peers[k][1]
                pltpu.make_async_remote_copy(
                    src_ref=ins[a].at[slot], dst_ref=bufs[a].at[slot], send_sem=send_sems.at[j],
                    recv_sem=recv_sems.at[j], device_id=sibling, device_id_type=MESH).start()
        token[...] = jnp.zeros_like(token)

    out_shape, out_specs = [], []
    for _ in range(n):
        out_shape += [pltpu.SemaphoreType.DMA((n_fwd,)), pltpu.SemaphoreType.DMA((n_fwd,))]
        out_specs += [_SEM, _SEM]
    out_shape += [pltpu.HBM(a.shape, a.dtype) for a in lands] + [jax.ShapeDtypeStruct((8, LANES), F32)]
    out_specs += [_HBM] * n + [pl.BlockSpec(memory_space=pltpu.VMEM)]
    outs = _pcall(
        body, name=name, out_shape=tuple(out_shape), in_specs=[_HBM] * n, out_specs=tuple(out_specs),
        input_output_aliases={a: 2 * n + a for a in range(n)},
        compiler_params=pltpu.CompilerParams(has_side_effects=_DATAFLOW),
    )(*[pltpu.with_memory_space_constraint(a, pltpu.HBM) for a in lands])
    return [(outs[2 * n + a], outs[2 * a], outs[2 * a + 1]) for a in range(n)], outs[-1]


def _sibling_forward_wait(handles, after, name):
    n = len(handles)

    def body(*refs):
        sibling = (lax.axis_index("x"), lax.axis_index("y"), 1 - lax.axis_index("c"))
        peers = _peers()
        for a in range(n):
            buf, send_sems, recv_sems = refs[3 * a:3 * a + 3]
            for j, k in enumerate(_OTHER_CHIPS):
                (px, py, pc), slot = peers[k]
                theirs = 4 * px + 2 * py + (1 - pc)
                cp = pltpu.make_async_remote_copy(
                    src_ref=buf.at[slot], dst_ref=buf.at[theirs], send_sem=send_sems.at[j],
                    recv_sem=recv_sems.at[j], device_id=sibling, device_id_type=MESH)
                cp.wait_send()
                cp.wait_recv()

    operands, in_specs = [], []
    for h in handles:
        operands += list(h)
        in_specs += [_HBM, _SEM, _SEM]
    outs = _pcall(
        body, name=name, out_shape=tuple(pltpu.HBM(h[0].shape, h[0].dtype) for h in handles),
        in_specs=in_specs + [pl.BlockSpec(memory_space=pl.ANY)], out_specs=tuple([_HBM] * n),
        input_output_aliases={3 * a: a for a in range(n)},
        compiler_params=pltpu.CompilerParams(has_side_effects=_DATAFLOW),
    )(*operands, after)
    return list(outs)


_HBM = pl.BlockSpec(memory_space=pltpu.HBM)
_SEM = pl.BlockSpec(memory_space=pltpu.SEMAPHORE)
_DATAFLOW = pltpu.SideEffectType.DATAFLOW_SIDE_EFFECTING


_ALL_PEERS = tuple(range(N_DEV - 1))
_SAME_CORE_PEERS = (0, 1, 3, 5)
_OTHER_CHIPS = (1, 3, 5)


def _xfer_start(arrs, name, gather, via_sibling=(), after=()):
    n = len(arrs)
    n_peer = N_DEV - 1
    n_after = len(after)
    peer_ks = [_SAME_CORE_PEERS if a in via_sibling else _ALL_PEERS for a in range(n)]

    def body(*refs):
        ins, lands = refs[:n], refs[n:2 * n]
        sems = refs[2 * n + n_after:5 * n + n_after]
        token = refs[-1]
        me = _my_index()
        peers = _peers()
        for a in range(n):
            send_sems, recv_sems, loc_sem = sems[3 * a:3 * a + 3]
            src_own = ins[a] if gather else ins[a].at[me]
            pltpu.make_async_copy(src_own, lands[a].at[me], loc_sem).start()
            for k in peer_ks[a]:
                peer, pidx = peers[k]
                src = ins[a] if gather else ins[a].at[pidx]
                pltpu.make_async_remote_copy(
                    src_ref=src, dst_ref=lands[a].at[me], send_sem=send_sems.at[k], recv_sem=recv_sems.at[k],
                    device_id=peer, device_id_type=MESH).start()
        token[...] = jnp.zeros_like(token)

    land_shapes = [((N_DEV,) + a.shape) if gather else a.shape for a in arrs]
    out_shape, out_specs = [], []
    for _ in range(n):
        out_shape += [pltpu.SemaphoreType.DMA((n_peer,)), pltpu.SemaphoreType.DMA((n_peer,)),
                      pltpu.SemaphoreType.DMA(())]
        out_specs += [_SEM, _SEM, _SEM]
    out_shape += [pltpu.HBM(a.shape, a.dtype) for a in arrs]
    out_shape += [pltpu.HBM(s, a.dtype) for s, a in zip(land_shapes, arrs)]
    out_shape += [jax.ShapeDtypeStruct((8, LANES), F32)]
    out_specs += [_HBM] * (2 * n) + [pl.BlockSpec(memory_space=pltpu.VMEM)]
    aliases = {}
    for a in range(n):
        aliases[a] = 3 * n + a
        aliases[n + a] = 4 * n + a
    operands = [pltpu.with_memory_space_constraint(a, pltpu.HBM) for a in arrs]
    operands += [pltpu.with_memory_space_constraint(lax.empty(s, a.dtype), pltpu.HBM)
                 for s, a in zip(land_shapes, arrs)]
    outs = _pcall(
        body, name=name, out_shape=tuple(out_shape),
        in_specs=[_HBM] * (2 * n) + [pl.BlockSpec(memory_space=pl.ANY)] * n_after, out_specs=tuple(out_specs),
        input_output_aliases=aliases,
        compiler_params=pltpu.CompilerParams(has_side_effects=_DATAFLOW),
    )(*operands, *after)
    handles = []
    for a in range(n):
        handles.append((outs[3 * n + a], outs[4 * n + a], outs[3 * a], outs[3 * a + 1], outs[3 * a + 2],
                        peer_ks[a]))
    return handles, outs[-1]


def _xfer_wait(handles, after, name, gather):
    n = len(handles)
    after = tuple(after) if isinstance(after, (tuple, list)) else (after,)
    peer_ks = [h[5] for h in handles]

    def body(*refs):
        me = _my_index()
        peers = _peers()
        for a in range(n):
            src_ref, land_ref, send_ref, recv_ref, loc_ref = refs[5 * a:5 * a + 5]
            src_own = src_ref if gather else src_ref.at[me]
            pltpu.make_async_copy(src_own, land_ref.at[me], loc_ref).wait()
            for k in peer_ks[a]:
                peer, pidx = peers[k]
                src = src_ref if gather else src_ref.at[pidx]
                cp = pltpu.make_async_remote_copy(
                    src_ref=src, dst_ref=land_ref.at[pidx], send_sem=send_ref.at[k], recv_sem=recv_ref.at[k],
                    device_id=peer, device_id_type=MESH)
                cp.wait_send()
                cp.wait_recv()

    operands, in_specs, out_shape, aliases = [], [], [], {}
    for a, h in enumerate(handles):
        operands += list(h[:5])
        in_specs += [_HBM, _HBM, _SEM, _SEM, _SEM]
        out_shape += [pltpu.HBM(h[0].shape, h[0].dtype), pltpu.HBM(h[1].shape, h[1].dtype)]
        aliases[5 * a] = 2 * a
        aliases[5 * a + 1] = 2 * a + 1
    outs = _pcall(
        body, name=name, out_shape=tuple(out_shape),
        in_specs=in_specs + [pl.BlockSpec(memory_space=pl.ANY)] * len(after),
        out_specs=tuple([_HBM] * (2 * n)), input_output_aliases=aliases,
        compiler_params=pltpu.CompilerParams(has_side_effects=_DATAFLOW),
    )(*operands, *after)
    return [outs[2 * a + 1] for a in range(n)]


def _sibling_forward(lands, name):
    n = len(lands)
    n_fwd = len(_OTHER_CHIPS)

    def body(*refs):
        ins, bufs = refs[:n], refs[n:2 * n]
        send_sems, recv_sems = refs[2 * n:]
        x, y, c = lax.axis_index("x"), lax.axis_index("y"), lax.axis_index("c")
        sibling = (x, y, 1 - c)
        peers = _peers()
        sends = []
        for a in range(n):
            for j, k in enumerate(_OTHER_CHIPS):
                slot = peers[k][1]
                cp = pltpu.make_async_remote_copy(
                    src_ref=ins[a].at[slot], dst_ref=bufs[a].at[slot],
                    send_sem=send_sems.at[a * n_fwd + j], recv_sem=recv_sems.at[a * n_fwd + j],
                    device_id=sibling, device_id_type=MESH)
                cp.start()
                sends.append(cp)
        for a in range(n):
            for j, k in enumerate(_OTHER_CHIPS):
                (px, py, pc), slot = peers[k]
                theirs = 4 * px + 2 * py + (1 - pc)
                pltpu.make_async_remote_copy(
                    src_ref=ins[a].at[slot], dst_ref=bufs[a].at[theirs],
                    send_sem=send_sems.at[a * n_fwd + j], recv_sem=recv_sems.at[a * n_fwd + j],
                    device_id=sibling, device_id_type=MESH).wait_recv()
        for cp in sends:
            cp.wait_send()

    any_spec = pl.BlockSpec(memory_space=pl.ANY)
    outs = _pcall(
        body, name=name, out_shape=[jax.ShapeDtypeStruct(a.shape, a.dtype) for a in lands],
        in_specs=[any_spec] * n, out_specs=[any_spec] * n,
        input_output_aliases={a: a for a in range(n)},
        scratch_shapes=[pltpu.SemaphoreType.DMA((n * n_fwd,)), pltpu.SemaphoreType.DMA((n * n_fwd,))],
        compiler_params=pltpu.CompilerParams(has_side_effects=True),
    )(*lands)
    return list(outs)


_DIMS = {"nn": (((1,), (0,)), ((), ())), "nt": (((1,), (1,)), ((), ())), "tn": (((0,), (0,)), ((), ()))}


def _dot(a, b, mode="nn"):
    return lax.dot_general(a, b, _DIMS[mode], preferred_element_type=F32)


def _mm(a, b, *, mode, grid, a_spec, b_spec, out_shape, out_specs, acc_shape, epilogue, name,
        extra=(), extra_specs=(), after=(), semantics=("parallel", "parallel", "arbitrary")):
    nk = grid[2]
    n_extra = len(extra)
    n_in = 2 + n_extra + len(after)

    def body_single(*refs):
        a_ref, b_ref = refs[0], refs[1]
        epilogue(_dot(a_ref[...], b_ref[...], mode), refs[2:2 + n_extra], refs[n_in:])

    def body_acc(*refs):
        a_ref, b_ref = refs[0], refs[1]
        ex = refs[2:2 + n_extra]
        outs = refs[n_in:-1]
        acc = refs[-1]
        k = pl.program_id(2)

        @pl.when(k == 0)
        def _():
            acc[...] = jnp.zeros_like(acc)

        acc[...] += _dot(a_ref[...], b_ref[...], mode)

        @pl.when(k == nk - 1)
        def _():
            epilogue(acc[...], ex, outs)

    return _pcall(
        body_single if nk == 1 else body_acc, name=name, grid=grid, out_shape=out_shape,
        in_specs=[a_spec, b_spec] + list(extra_specs) + [pl.BlockSpec(memory_space=pl.ANY)] * len(after),
        out_specs=out_specs,
        scratch_shapes=[] if nk == 1 else [pltpu.VMEM(acc_shape, F32)],
        compiler_params=_cparams(semantics),
    )(a, b, *extra, *after)


def _ep_store(dtype):
    def ep(acc, ex, outs):
        outs[0][...] = acc.astype(dtype)
    return ep


def _ep_relu2(acc, ex, outs):
    outs[0][...] = acc.astype(BF16)
    r = jnp.maximum(acc, 0.0)
    outs[1][...] = (r * r).astype(BF16)


def _ep_relu2_bwd(acc, ex, outs):
    u = ex[0][...].astype(F32)
    outs[0][...] = (acc * (2.0 * jnp.maximum(u, 0.0))).astype(BF16)


def _tile(n, want):
    t = min(n, want)
    while n % t:
        t //= 2
    return t


def _mm_nn(a, w, out_dtype, name, tm=2048, tn=1024, tk=1024, epilogue=None, out_dtypes=None, after=()):
    M, K = a.shape
    N = w.shape[1]
    tm, tn, tk = _tile(M, tm), _tile(N, tn), _tile(K, tk)
    out_dtypes = out_dtypes or [out_dtype]
    return _mm(a, w, mode="nn", grid=(M // tm, N // tn, K // tk),
               a_spec=pl.BlockSpec((tm, tk), lambda i, j, k: (i, k)),
               b_spec=pl.BlockSpec((tk, tn), lambda i, j, k: (k, j)),
               out_shape=[jax.ShapeDtypeStruct((M, N), d) for d in out_dtypes],
               out_specs=[pl.BlockSpec((tm, tn), lambda i, j, k: (i, j)) for _ in out_dtypes],
               acc_shape=(tm, tn), epilogue=epilogue or _ep_store(out_dtype), name=name, after=after)


def _mm_nn_blocked(a, wg, name, epilogue, out_dtypes, tm=2048):
    M, K = a.shape
    n = wg.shape[2]
    tm = _tile(M, tm)
    return _mm(a, wg, mode="nn", grid=(M // tm, N_DEV, 1),
               a_spec=pl.BlockSpec((tm, K), lambda i, j, k: (i, 0)),
               b_spec=pl.BlockSpec((None, K, n), lambda i, j, k: (j, 0, 0)),
               out_shape=[jax.ShapeDtypeStruct((M, N_DEV * n), d) for d in out_dtypes],
               out_specs=[pl.BlockSpec((tm, n), lambda i, j, k: (i, j)) for _ in out_dtypes],
               acc_shape=(tm, n), epilogue=epilogue, name=name)


def _mm_nt(a, w, out_dtype, name, tm=2048, tn=1024, tk=1024, epilogue=None, extra=(), extra_specs=(),
           after=()):
    M, K = a.shape
    N = w.shape[0]
    tm, tn, tk = _tile(M, tm), _tile(N, tn), _tile(K, tk)
    if extra and not extra_specs:
        extra_specs = [pl.BlockSpec((tm, tn), lambda i, j, k: (i, j)) for _ in extra]
    return _mm(a, w, mode="nt", grid=(M // tm, N // tn, K // tk),
               a_spec=pl.BlockSpec((tm, tk), lambda i, j, k: (i, k)),
               b_spec=pl.BlockSpec((tn, tk), lambda i, j, k: (j, k)),
               out_shape=[jax.ShapeDtypeStruct((M, N), out_dtype)],
               out_specs=[pl.BlockSpec((tm, tn), lambda i, j, k: (i, j))],
               acc_shape=(tm, tn), epilogue=epilogue or _ep_store(out_dtype), name=name,
               extra=extra, extra_specs=extra_specs, after=after)[0]


def _mm_nt_blocked(a, wg, out_dtype, name, tm=1024, after=()):
    M = a.shape[0]
    kout, n = wg.shape[1], wg.shape[2]
    tm = _tile(M, tm)
    return _mm(a, wg, mode="nt", grid=(M // tm, 1, N_DEV),
               a_spec=pl.BlockSpec((tm, n), lambda i, j, k: (i, k)),
               b_spec=pl.BlockSpec((None, kout, n), lambda i, j, k: (k, 0, 0)),
               out_shape=[jax.ShapeDtypeStruct((M, kout), out_dtype)],
               out_specs=[pl.BlockSpec((tm, kout), lambda i, j, k: (i, 0))],
               acc_shape=(tm, kout), epilogue=_ep_store(out_dtype), name=name, after=after)[0]


def _mm_tn(a, b, out_dtype, name, tm=1024, tn=1024, tk=2048):
    K, M = a.shape
    N = b.shape[1]
    tm, tn, tk = _tile(M, tm), _tile(N, tn), _tile(K, tk)
    return _mm(a, b, mode="tn", grid=(M // tm, N // tn, K // tk),
               a_spec=pl.BlockSpec((tk, tm), lambda i, j, k: (k, i)),
               b_spec=pl.BlockSpec((tk, tn), lambda i, j, k: (k, j)),
               out_shape=[jax.ShapeDtypeStruct((M, N), out_dtype)],
               out_specs=[pl.BlockSpec((tm, tn), lambda i, j, k: (i, j))],
               acc_shape=(tm, tn), epilogue=_ep_store(out_dtype), name=name)[0]


def _mm_tn_blocked(a, b, out_dtype, name, tm=1024, tk=2048):
    K, M = a.shape
    n = b.shape[1] // N_DEV
    tm, tk = _tile(M, tm), _tile(K, tk)
    return _mm(a, b, mode="tn", grid=(M // tm, N_DEV, K // tk),
               a_spec=pl.BlockSpec((tk, tm), lambda i, j, k: (k, i)),
               b_spec=pl.BlockSpec((tk, n), lambda i, j, k: (k, j)),
               out_shape=[jax.ShapeDtypeStruct((N_DEV, M, n), out_dtype)],
               out_specs=[pl.BlockSpec((None, tm, n), lambda i, j, k: (j, i, 0))],
               acc_shape=(tm, n), epilogue=_ep_store(out_dtype), name=name)[0]


def _window_geometry(ws):
    base = [(ws * k // LANES) * LANES for k in range(N_DEV)]
    off = [ws * k - base[k] for k in range(N_DEV)]
    win = -(-(max(off) + ws) // LANES) * LANES
    return base, off, win


def _shards_to_columns(xg, base, off, win, n_out, name, tr=256):
    R, ws = xg.shape[1], xg.shape[2]
    tr = _tile(R, tr)
    nb_win = win // LANES

    def body(x_ref, o_ref, frame_ref):
        written = set()
        frame_ref[...] = jnp.zeros_like(frame_ref)
        for k in range(N_DEV):
            frame_ref[:, 0:ws] = x_ref[k].astype(F32)
            window = frame_ref[...]
            if off[k]:
                window = pltpu.roll(window, off[k], 1)
            for i in range(nb_win):
                b = base[k] // LANES + i
                if b * LANES >= n_out:
                    continue
                cols = slice(b * LANES, (b + 1) * LANES)
                blk = window[:, i * LANES:(i + 1) * LANES]
                if b in written:
                    blk = blk + o_ref[:, cols].astype(F32)
                o_ref[:, cols] = blk.astype(o_ref.dtype)
                written.add(b)
        for b in range(n_out // LANES):
            if b not in written:
                o_ref[:, b * LANES:(b + 1) * LANES] = jnp.zeros((tr, LANES), o_ref.dtype)

    return _pcall(
        body, name=name, grid=(R // tr,), out_shape=jax.ShapeDtypeStruct((R, n_out), xg.dtype),
        in_specs=[pl.BlockSpec((N_DEV, tr, ws), lambda i: (0, i, 0))],
        out_specs=pl.BlockSpec((tr, n_out), lambda i: (i, 0)),
        scratch_shapes=[pltpu.VMEM((tr, win), F32)],
        compiler_params=_cparams(("parallel",)))(xg)


def _columns_to_shards(x, ws, base, off, win, name, tr=256):
    R = x.shape[0]
    tr = _tile(R, tr)

    def body(x_ref, o_ref, frame_ref):
        for k in range(N_DEV):
            window = x_ref[:, base[k]:base[k] + win].astype(F32)
            if off[k]:
                window = pltpu.roll(window, win - off[k], 1)
            frame_ref[...] = window
            o_ref[k] = frame_ref[:, 0:ws].astype(o_ref.dtype)

    return _pcall(
        body, name=name, grid=(R // tr,), out_shape=jax.ShapeDtypeStruct((N_DEV, R, ws), x.dtype),
        in_specs=[pl.BlockSpec((tr, x.shape[1]), lambda i: (i, 0))],
        out_specs=pl.BlockSpec((N_DEV, tr, ws), lambda i: (0, i, 0)),
        scratch_shapes=[pltpu.VMEM((tr, win), F32)],
        compiler_params=_cparams(("parallel",)))(x)


def _sigmoid(x):
    return 1.0 / (1.0 + jnp.exp(-x))


def _row_spec(tm, d):
    return pl.BlockSpec((tm, d), lambda i: (i, 0))


def _vec_spec(d):
    return pl.BlockSpec((1, d), lambda i: (0, 0))


def _norm_mod_fwd(x, y, gate, nw, scale, shift, name, tm=512):
    L, D = x.shape
    tm = _tile(L, tm)
    has_res = y is not None

    def body(*refs):
        if has_res:
            x_ref, y_ref, g_ref, nw_ref, sc_ref, sh_ref, xo_ref, h_ref = refs
            xn = x_ref[...] + g_ref[...] * y_ref[...]
            xo_ref[...] = xn
        else:
            x_ref, nw_ref, sc_ref, sh_ref, h_ref = refs
            xn = x_ref[...]
        rstd = lax.rsqrt(jnp.mean(xn * xn, axis=-1, keepdims=True) + NORM_EPS)
        h = xn * rstd * nw_ref[...] * (1.0 + sc_ref[...]) + sh_ref[...]
        h_ref[...] = h.astype(BF16)

    row, vec = _row_spec(tm, D), _vec_spec(D)
    if has_res:
        ins, in_specs = (x, y, gate, nw, scale, shift), [row, row, vec, vec, vec, vec]
        out_shape = [jax.ShapeDtypeStruct((L, D), F32), jax.ShapeDtypeStruct((L, D), BF16)]
        out_specs = [row, row]
    else:
        ins, in_specs = (x, nw, scale, shift), [row, vec, vec, vec]
        out_shape = [jax.ShapeDtypeStruct((L, D), BF16)]
        out_specs = [row]
    outs = _pcall(body, name=name, grid=(L // tm,), out_shape=out_shape, in_specs=in_specs,
                  out_specs=out_specs, compiler_params=_cparams(("parallel",)))(*ins)
    return outs if has_res else (x, outs[0])


def _gated_branch_bwd(dx, branch, y_ref, g_ref, dy_ref, dg_ref):
    if branch is None:
        return
    dy_ref[...] = (g_ref[...] * dx).astype(BF16)
    dg_ref[...] += jnp.sum(dx * y_ref[...], axis=0, keepdims=True)


def _norm_mod_bwd(dh, x, nw, scale, dres, name, branch=None, tm=512):
    L, D = x.shape
    tm = _tile(L, tm)
    nb = 0 if branch is None else 2

    def body(dh_ref, x_ref, nw_ref, sc_ref, dres_ref, *rest):
        y_ref, g_ref = rest[:nb] if nb else (None, None)
        dx_ref, dsh_ref, dsc_ref, dnw_ref = rest[nb:nb + 4]
        dy_ref, dg_ref = rest[nb + 4:] if nb else (None, None)

        @pl.when(pl.program_id(0) == 0)
        def _():
            dsh_ref[...] = jnp.zeros_like(dsh_ref)
            dsc_ref[...] = jnp.zeros_like(dsc_ref)
            dnw_ref[...] = jnp.zeros_like(dnw_ref)
            if nb:
                dg_ref[...] = jnp.zeros_like(dg_ref)

        xv = x_ref[...]
        dh_v = dh_ref[...]
        nw_v = nw_ref[...]
        rstd = lax.rsqrt(jnp.mean(xv * xv, axis=-1, keepdims=True) + NORM_EPS)
        xhat = xv * rstd
        dsh_ref[...] += jnp.sum(dh_v, axis=0, keepdims=True)
        dsc_ref[...] += jnp.sum(dh_v * (xhat * nw_v), axis=0, keepdims=True)
        dr = dh_v * (1.0 + sc_ref[...])
        dnw_ref[...] += jnp.sum(dr * xhat, axis=0, keepdims=True)
        dxh = dr * nw_v
        dx = rstd * (dxh - xhat * jnp.mean(dxh * xhat, axis=-1, keepdims=True)) + dres_ref[...]
        dx_ref[...] = dx
        _gated_branch_bwd(dx, branch, y_ref, g_ref, dy_ref, dg_ref)

    row, vec = _row_spec(tm, D), _vec_spec(D)
    extra_in = [] if branch is None else list(branch)
    return _pcall(
        body, name=name, grid=(L // tm,),
        out_shape=[jax.ShapeDtypeStruct((L, D), F32)] + [jax.ShapeDtypeStruct((1, D), F32)] * 3
        + ([jax.ShapeDtypeStruct((L, D), BF16), jax.ShapeDtypeStruct((1, D), F32)] if nb else []),
        in_specs=[row, row, vec, vec, row] + ([row, vec] if nb else []),
        out_specs=[row, vec, vec, vec] + ([row, vec] if nb else []),
        compiler_params=_cparams(("arbitrary",)))(dh, x, nw, scale, dres, *extra_in)


def _final_loss(x, y, gate, fw, target, name, tm=512):
    L, D = x.shape
    tm = _tile(L, tm)

    def body(x_ref, y_ref, g_ref, fw_ref, t_ref, dx_ref, loss_ref, dfw_ref, dy_ref, dg_ref):
        @pl.when(pl.program_id(0) == 0)
        def _():
            loss_ref[...] = jnp.zeros_like(loss_ref)
            dfw_ref[...] = jnp.zeros_like(dfw_ref)
            dg_ref[...] = jnp.zeros_like(dg_ref)

        xn = x_ref[...] + g_ref[...] * y_ref[...]
        fw_v = fw_ref[...]
        rstd = lax.rsqrt(jnp.mean(xn * xn, axis=-1, keepdims=True) + NORM_EPS)
        xhat = xn * rstd
        diff = xhat * fw_v - t_ref[...]
        loss_ref[...] += jnp.sum(diff * diff, axis=0, keepdims=True)
        dyf = diff * (1.0 / D)
        dfw_ref[...] += jnp.sum(dyf * xhat, axis=0, keepdims=True)
        dxh = dyf * fw_v
        dx = rstd * (dxh - xhat * jnp.mean(dxh * xhat, axis=-1, keepdims=True))
        dx_ref[...] = dx
        _gated_branch_bwd(dx, True, y_ref, g_ref, dy_ref, dg_ref)

    row, vec = _row_spec(tm, D), _vec_spec(D)
    return _pcall(
        body, name=name, grid=(L // tm,),
        out_shape=[jax.ShapeDtypeStruct((L, D), F32), jax.ShapeDtypeStruct((1, D), F32),
                   jax.ShapeDtypeStruct((1, D), F32), jax.ShapeDtypeStruct((L, D), BF16),
                   jax.ShapeDtypeStruct((1, D), F32)],
        in_specs=[row, row, vec, vec, row], out_specs=[row, vec, vec, row, vec],
        compiler_params=_cparams(("arbitrary",)))(x, y, gate, fw, target)


def _shift_down(v, s, row):
    if s == 0:
        return v
    return jnp.where(row >= s, pltpu.roll(v, s, 0), 0.0)


CONV_ROWS = 32


def _shifted_rows(x_ref, r0, n, lanes=slice(None)):
    cur = x_ref[r0:r0 + CONV_ROWS, lanes]
    if r0 >= n - 1:
        return [cur] + [x_ref[r0 - s:r0 - s + CONV_ROWS, lanes] for s in range(1, n)]
    row = lax.broadcasted_iota(jnp.int32, cur.shape, 0)
    return [_shift_down(cur, s, row) for s in range(n)]


def _ssd_conv_fwd(zx, w, b, col0, width, name, cb=512):
    L = zx.shape[0]
    nb = width // cb
    off = col0 // cb

    def body(x_ref, w_ref, b_ref, o_ref):
        for l0 in range(0, cb, LANES):
            lanes = slice(l0, l0 + LANES)
            taps = [w_ref[k:k + 1, lanes] for k in range(SSD_K)]
            bias = b_ref[:, lanes]
            for r0 in range(0, L, CONV_ROWS):
                taps_in = _shifted_rows(x_ref, r0, SSD_K, lanes)
                acc = bias + taps[SSD_K - 1] * taps_in[0]
                for s in range(1, SSD_K):
                    acc = acc + taps[SSD_K - 1 - s] * taps_in[s]
                o_ref[r0:r0 + CONV_ROWS, lanes] = acc * _sigmoid(acc)

    return _pcall(
        body, name=name, grid=(nb,), out_shape=jax.ShapeDtypeStruct((L, width), F32),
        in_specs=[pl.BlockSpec((L, cb), lambda j: (0, off + j)),
                  pl.BlockSpec((SSD_K, cb), lambda j: (0, j)),
                  pl.BlockSpec((1, cb), lambda j: (0, j))],
        out_specs=pl.BlockSpec((L, cb), lambda j: (0, j)),
        compiler_params=_cparams(("parallel",)))(zx, w, b)


def _ssd_conv_bwd(zx, w, b, d_parts, dzx, col0, name, cb=128):
    L = zx.shape[0]
    widths = [p.shape[1] for p in d_parts]
    width = sum(widths)
    nb = width // cb
    off = col0 // cb
    starts = [sum(widths[:i]) // cb for i in range(len(d_parts))]
    counts = [wd // cb for wd in widths]

    def body(x_ref, w_ref, b_ref, *rest):
        d_refs = rest[:len(d_parts)]
        dx_ref, dw_ref, db_ref, dpre_ref = rest[len(d_parts) + 1:]
        j = pl.program_id(0)
        taps = [w_ref[k:k + 1, :] for k in range(SSD_K)]
        bias = b_ref[...]
        fold = lambda v: sum(v[r:r + 8, :] for r in range(0, CONV_ROWS, 8))
        db8 = jnp.zeros((8, cb), F32)
        dw8 = [jnp.zeros((8, cb), F32) for _ in range(SSD_K)]
        for r0 in range(0, L, CONV_ROWS):
            rows = slice(r0, r0 + CONV_ROWS)
            d_val = d_refs[-1][rows, :]
            for i in range(len(d_parts) - 2, -1, -1):
                d_val = jnp.where(j < starts[i + 1], d_refs[i][rows, :], d_val)
            taps_in = _shifted_rows(x_ref, r0, SSD_K)
            acc = bias + taps[SSD_K - 1] * taps_in[0]
            for s in range(1, SSD_K):
                acc = acc + taps[SSD_K - 1 - s] * taps_in[s]
            sig = _sigmoid(acc)
            dpre = d_val * (sig * (1.0 + acc * (1.0 - sig)))
            dpre_ref[rows, :] = dpre
            db8 = db8 + fold(dpre)
            for s in range(SSD_K):
                dw8[s] = dw8[s] + fold(dpre * taps_in[s])
        dpre_ref[L:L + 8, :] = jnp.zeros((8, cb), F32)
        db_ref[...] = jnp.sum(db8, axis=0, keepdims=True)
        for s in range(SSD_K):
            dw_ref[SSD_K - 1 - s:SSD_K - s, :] = jnp.sum(dw8[s], axis=0, keepdims=True)
        for r0 in range(0, L, CONV_ROWS):
            dx = taps[SSD_K - 1] * dpre_ref[r0:r0 + CONV_ROWS, :]
            for s in range(1, SSD_K):
                dx = dx + taps[SSD_K - 1 - s] * dpre_ref[r0 + s:r0 + s + CONV_ROWS, :]
            dx_ref[r0:r0 + CONV_ROWS, :] = dx.astype(BF16)

    def part_spec(i):
        return pl.BlockSpec((L, cb), lambda j: (0, jnp.clip(j - starts[i], 0, counts[i] - 1)))

    return _pcall(
        body, name=name, grid=(nb,),
        out_shape=[jax.ShapeDtypeStruct(dzx.shape, BF16), jax.ShapeDtypeStruct((SSD_K, width), F32),
                   jax.ShapeDtypeStruct((1, width), F32)],
        in_specs=[pl.BlockSpec((L, cb), lambda j: (0, off + j)),
                  pl.BlockSpec((SSD_K, cb), lambda j: (0, j)),
                  pl.BlockSpec((1, cb), lambda j: (0, j))]
        + [part_spec(i) for i in range(len(d_parts))] + [pl.BlockSpec(memory_space=pl.ANY)],
        out_specs=[pl.BlockSpec((L, cb), lambda j: (0, off + j)),
                   pl.BlockSpec((SSD_K, cb), lambda j: (0, j)),
                   pl.BlockSpec((1, cb), lambda j: (0, j))],
        input_output_aliases={3 + len(d_parts): 0},
        scratch_shapes=[pltpu.VMEM((L + 8, cb), F32)],
        compiler_params=_cparams(("parallel",)))(zx, w, b, *d_parts, dzx)


def _dzx_finish(dzx, ddt, col0, name, tl=512):
    G, L, _ = ddt.shape
    tail = dzx.shape[1] - col0
    tl = _tile(L, tl)

    def body(ddt_ref, dzx_ref, o_ref):
        s = ddt_ref[0]
        for g in range(1, G):
            s = s + ddt_ref[g]
        o_ref[:, 0:LANES] = s.astype(o_ref.dtype)
        if tail > LANES:
            o_ref[:, LANES:] = jnp.zeros((tl, tail - LANES), o_ref.dtype)

    return _pcall(
        body, name=name, grid=(L // tl,), out_shape=jax.ShapeDtypeStruct(dzx.shape, dzx.dtype),
        in_specs=[pl.BlockSpec((G, tl, LANES), lambda i: (0, i, 0)), pl.BlockSpec(memory_space=pl.ANY)],
        out_specs=pl.BlockSpec((tl, tail), lambda i: (i, col0 // tail)),
        input_output_aliases={1: 0},
        compiler_params=_cparams(("parallel",)))(ddt, dzx)


def _sc_conv_fwd(proj, w, name, cb=512):
    L = proj.shape[0]
    width = proj.shape[1] // 3
    nb = width // cb

    def body(b_ref, c_ref, x_ref, w_ref, o_ref):
        for l0 in range(0, cb, LANES):
            lanes = slice(l0, l0 + LANES)
            taps = [w_ref[k:k + 1, lanes] for k in range(SC_K)]
            for r0 in range(0, L, CONV_ROWS):
                rows = slice(r0, r0 + CONV_ROWS)
                q = [c * x for c, x in zip(_shifted_rows(c_ref, r0, SC_K, lanes),
                                           _shifted_rows(x_ref, r0, SC_K, lanes))]
                acc = taps[SC_K - 1] * q[0]
                for s in range(1, SC_K):
                    acc = acc + taps[SC_K - 1 - s] * q[s]
                o_ref[rows, lanes] = (b_ref[rows, lanes] * acc).astype(BF16)

    return _pcall(
        body, name=name, grid=(nb,), out_shape=jax.ShapeDtypeStruct((L, width), BF16),
        in_specs=[pl.BlockSpec((L, cb), lambda j: (0, j)),
                  pl.BlockSpec((L, cb), lambda j: (0, nb + j)),
                  pl.BlockSpec((L, cb), lambda j: (0, 2 * nb + j)),
                  pl.BlockSpec((SC_K, cb), lambda j: (0, j))],
        out_specs=pl.BlockSpec((L, cb), lambda j: (0, j)),
        compiler_params=_cparams(("parallel",)))(proj, proj, proj, w)


def _sc_conv_bwd(proj, w, dy, name, cb=128):
    L = proj.shape[0]
    width = proj.shape[1] // 3
    nb = width // cb

    def body(b_ref, c_ref, x_ref, w_ref, dy_ref, db_ref, dc_ref, dxv_ref, dw_ref, dconv_ref):
        taps = [w_ref[k:k + 1, :] for k in range(SC_K)]
        fold = lambda v: sum(v[r:r + 8, :] for r in range(0, CONV_ROWS, 8))
        dw8 = [jnp.zeros((8, cb), F32) for _ in range(SC_K)]
        for r0 in range(0, L, CONV_ROWS):
            rows = slice(r0, r0 + CONV_ROWS)
            q = [c * x for c, x in zip(_shifted_rows(c_ref, r0, SC_K), _shifted_rows(x_ref, r0, SC_K))]
            conv = taps[SC_K - 1] * q[0]
            for s in range(1, SC_K):
                conv = conv + taps[SC_K - 1 - s] * q[s]
            dyv = dy_ref[rows, :]
            db_ref[rows, :] = (dyv * conv).astype(BF16)
            dconv = dyv * b_ref[rows, :]
            dconv_ref[rows, :] = dconv
            for s in range(SC_K):
                dw8[s] = dw8[s] + fold(dconv * q[s])
        dconv_ref[L:L + 8, :] = jnp.zeros((8, cb), F32)
        for s in range(SC_K):
            dw_ref[SC_K - 1 - s:SC_K - s, :] = jnp.sum(dw8[s], axis=0, keepdims=True)
        for r0 in range(0, L, CONV_ROWS):
            rows = slice(r0, r0 + CONV_ROWS)
            dq = taps[SC_K - 1] * dconv_ref[rows, :]
            for s in range(1, SC_K):
                dq = dq + taps[SC_K - 1 - s] * dconv_ref[r0 + s:r0 + s + CONV_ROWS, :]
            dc_ref[rows, :] = (dq * x_ref[rows, :]).astype(BF16)
            dxv_ref[rows, :] = (dq * c_ref[rows, :]).astype(BF16)

    blk = pl.BlockSpec((L, cb), lambda j: (0, j))
    wblk = pl.BlockSpec((SC_K, cb), lambda j: (0, j))
    return _pcall(
        body, name=name, grid=(nb,),
        out_shape=[jax.ShapeDtypeStruct((L, width), BF16)] * 3 + [jax.ShapeDtypeStruct((SC_K, width), F32)],
        in_specs=[blk, pl.BlockSpec((L, cb), lambda j: (0, nb + j)),
                  pl.BlockSpec((L, cb), lambda j: (0, 2 * nb + j)), wblk, blk],
        out_specs=[blk, blk, blk, wblk], scratch_shapes=[pltpu.VMEM((L + 8, cb), F32)],
        compiler_params=_cparams(("parallel",)))(proj, proj, proj, w, dy)


def _split3(v):
    hi = v.astype(BF16)
    r1 = v - hi.astype(F32)
    mid = r1.astype(BF16)
    lo = (r1 - mid.astype(F32)).astype(BF16)
    return hi, mid, lo


def _dot_exact01(t01, v):
    hi, mid, lo = _split3(v)
    return _dot(t01, hi) + _dot(t01, mid) + _dot(t01, lo)


def _lane_col(v, lane, h):
    return jnp.sum(jnp.where(lane == h, v, 0.0), axis=1, keepdims=True)


def _sum_all(v):
    return jnp.sum(jnp.sum(v, axis=1, keepdims=True), axis=0, keepdims=True)


def _softplus(x):
    return jnp.maximum(x, 0.0) + jnp.log1p(jnp.exp(-jnp.abs(x)))


def _ssd_decay(zx, bias_p, alog_p, n_heads, dt_block, name):
    L = zx.shape[0]
    nc = L // SSD_CHUNK
    per_step = 4 if nc % 4 == 0 else 1
    rows_step = per_step * SSD_CHUNK

    def body(raw_ref, bias_ref, alog_ref, dt_ref, sg_ref, cs_ref, cst_ref, last_ref):
        lane = lax.broadcasted_iota(jnp.int32, (SSD_CHUNK, LANES), 1)
        row = lax.broadcasted_iota(jnp.int32, (SSD_CHUNK, LANES), 0)
        valid = lane < n_heads
        tri = (row >= lane).astype(BF16)
        a_row = -jnp.exp(alog_ref[...])
        for i in range(per_step):
            rows = slice(i * SSD_CHUNK, (i + 1) * SSD_CHUNK)
            raw = raw_ref[rows, :] + bias_ref[...]
            dt = jnp.where(valid, _softplus(raw), 0.0)
            a = dt * a_row
            cs = _dot_exact01(tri, a)
            dt_ref[rows, :] = dt
            sg_ref[rows, :] = _sigmoid(raw)
            cs_ref[rows, :] = cs
            cst_ref[i] = cs.T
            last_ref[i] = jnp.sum(a, axis=0, keepdims=True)

    blk = pl.BlockSpec((rows_step, LANES), lambda c: (c, 0))
    head_vec = pl.BlockSpec((1, LANES), lambda c: (0, 0))
    return _pcall(
        body, name=name, grid=(nc // per_step,),
        out_shape=[jax.ShapeDtypeStruct((L, LANES), F32)] * 3
        + [jax.ShapeDtypeStruct((nc, SSD_CHUNK, LANES), F32), jax.ShapeDtypeStruct((nc, 1, LANES), F32)],
        in_specs=[pl.BlockSpec((rows_step, LANES), lambda c: (c, dt_block)), head_vec, head_vec],
        out_specs=[blk, blk, blk, pl.BlockSpec((per_step, SSD_CHUNK, LANES), lambda c: (c, 0, 0)),
                   pl.BlockSpec((per_step, 1, LANES), lambda c: (c, 0, 0))],
        compiler_params=_cparams(("parallel",)))(zx, bias_p, alog_p)


def _ssd_common(dt_ref, cs_ref, last_ref, b_ref, c_ref):
    c_sz = SSD_CHUNK
    lane = lax.broadcasted_iota(jnp.int32, (c_sz, LANES), 1)
    row = lax.broadcasted_iota(jnp.int32, (c_sz, LANES), 0)
    bb = b_ref[...].astype(BF16)
    cb = c_ref[...].astype(BF16)
    scores = _dot(cb, bb, "nt")
    return dict(lane=lane, row=row, dt=dt_ref[...], cs=cs_ref[...], last_row=last_ref[...], bb=bb, cb=cb,
                scores=scores, causal=row >= lane, lo=lane < SSD_P)


def _pair_terms(q, cst_ref, h0):
    lane, lo = q["lane"], q["lo"]
    out = {}
    cols, dts, lasts, lms = [], [], [], []
    lane1 = lax.broadcasted_iota(jnp.int32, (1, LANES), 1)
    for h in (h0, h0 + 1):
        col = _lane_col(q["cs"], lane, h)
        rowv = cst_ref[pl.ds(h, 1), :]
        lms.append(jnp.exp(jnp.where(q["causal"], col - rowv, -1e30)))
        cols.append(col)
        dts.append(_lane_col(q["dt"], lane, h))
        lasts.append(jnp.sum(jnp.where(lane1 == h, q["last_row"], 0.0), axis=1, keepdims=True))
    out["lm"] = lms
    out["cols"] = cols
    out["lasts"] = lasts
    out["dt_b"] = jnp.where(lo, dts[0], dts[1])
    out["e_b"] = jnp.where(lo, jnp.exp(cols[0]), jnp.exp(cols[1]))
    out["dec_cols"] = [jnp.exp(lasts[0] - cols[0]), jnp.exp(lasts[1] - cols[1])]
    out["dec_b"] = jnp.where(lo, out["dec_cols"][0], out["dec_cols"][1])
    lo1 = lane1 < SSD_P
    out["explast"] = [jnp.exp(lasts[0]), jnp.exp(lasts[1])]
    out["explast_b"] = jnp.where(lo1, out["explast"][0], out["explast"][1])
    return out


def _ssd_fwd(zx, xc, decay, d_lane, nw, d_inner, after, name):
    L = zx.shape[0]
    nc = L // SSD_CHUNK
    gw = d_inner // SSD_G
    heads = gw // SSD_P
    n_pair = heads // 2
    bc0 = d_inner // LANES

    def body(z_ref, xs_ref, b_ref, c_ref, dt_ref, cs_ref, cst_ref, last_ref, dl_ref, nw_ref, after_ref,
             y_ref, yn_ref, prev_ref, s_ref):
        @pl.when(pl.program_id(1) == 0)
        def _():
            s_ref[...] = jnp.zeros_like(s_ref)

        q = _ssd_common(dt_ref, cs_ref, last_ref, b_ref, c_ref)
        prev_ref[...] = s_ref[...]
        lo = q["lo"]
        for j in range(n_pair):
            sl = slice(j * LANES, (j + 1) * LANES)
            p = _pair_terms(q, cst_ref, pl.program_id(0) * heads + 2 * j)
            xs_p = xs_ref[:, sl]
            xp = xs_p * p["dt_b"]
            xb = xp.astype(BF16)
            m_a = (q["scores"] * p["lm"][0]).astype(BF16)
            m_b = (q["scores"] * p["lm"][1]).astype(BF16)
            yd = jnp.where(lo, _dot(m_a, xb), _dot(m_b, xb))
            s_p = s_ref[:, sl]
            yo = _dot(q["cb"], s_p.astype(BF16)) * p["e_b"]
            y_ref[:, sl] = yd + yo + dl_ref[:, sl] * xs_p
            st = _dot(q["bb"], (xp * p["dec_b"]).astype(BF16), "tn")
            s_ref[:, sl] = s_p * p["explast_b"] + st
        yv = y_ref[...]
        zv = z_ref[...]
        yg = yv * (zv * _sigmoid(zv))
        rstd = lax.rsqrt(jnp.mean(yg * yg, axis=-1, keepdims=True) + NORM_EPS)
        yn_ref[...] = (yg * rstd * nw_ref[...]).astype(BF16)

    grp = lambda width: pl.BlockSpec((None, 1, width), lambda g, c: (g, 0, 0))
    dt_, _, cs_, cst_, last_ = decay
    return _pcall(
        body, name=name, grid=(SSD_G, nc),
        out_shape=[jax.ShapeDtypeStruct((L, d_inner), F32), jax.ShapeDtypeStruct((L, d_inner), BF16),
                   jax.ShapeDtypeStruct((nc, SSD_G, SSD_N, gw), F32)],
        in_specs=[pl.BlockSpec((SSD_CHUNK, gw), lambda g, c: (c, g)),
                  pl.BlockSpec((SSD_CHUNK, gw), lambda g, c: (c, g)),
                  pl.BlockSpec((SSD_CHUNK, SSD_N), lambda g, c: (c, bc0 + g)),
                  pl.BlockSpec((SSD_CHUNK, SSD_N), lambda g, c: (c, bc0 + SSD_G + g)),
                  pl.BlockSpec((SSD_CHUNK, LANES), lambda g, c: (c, 0)),
                  pl.BlockSpec((SSD_CHUNK, LANES), lambda g, c: (c, 0)),
                  pl.BlockSpec((None, SSD_CHUNK, LANES), lambda g, c: (c, 0, 0)),
                  pl.BlockSpec((None, 1, LANES), lambda g, c: (c, 0, 0)),
                  grp(gw), grp(gw), pl.BlockSpec(memory_space=pl.ANY)],
        out_specs=[pl.BlockSpec((SSD_CHUNK, gw), lambda g, c: (c, g)),
                   pl.BlockSpec((SSD_CHUNK, gw), lambda g, c: (c, g)),
                   pl.BlockSpec((None, None, SSD_N, gw), lambda g, c: (c, g, 0, 0))],
        scratch_shapes=[pltpu.VMEM((SSD_N, gw), F32)],
        compiler_params=_cparams(("parallel", "arbitrary")))(
            zx, xc, xc, xc, dt_, cs_, cst_, last_, d_lane, nw, after)


def _ssd_bwd(dyn, y, zx, xc, prev, decay, alog_p, d_lane, nw, d_inner, name):
    L = zx.shape[0]
    nc = L // SSD_CHUNK
    gw = d_inner // SSD_G
    heads = gw // SSD_P
    n_pair = heads // 2
    bc0 = d_inner // LANES

    def body(dyn_ref, y_ref, z_ref, xs_ref, b_ref, c_ref, prev_ref, dt_ref, sg_ref, cs_ref, cst_ref, last_ref,
             alog_ref, dl_ref, nw_ref,
             dz_ref, dxs_ref, db_ref, dc_ref, ddt_ref, dbias_ref, dalog_ref, dd_ref, dnw_ref,
             ds_ref, racc_ref):
        @pl.when(pl.program_id(1) == 0)
        def _():
            ds_ref[...] = jnp.zeros_like(ds_ref)
            dbias_ref[...] = jnp.zeros_like(dbias_ref)
            dalog_ref[...] = jnp.zeros_like(dalog_ref)
            dd_ref[...] = jnp.zeros_like(dd_ref)
            dnw_ref[...] = jnp.zeros_like(dnw_ref)

        q = _ssd_common(dt_ref, cs_ref, last_ref, b_ref, c_ref)
        a_row = -jnp.exp(alog_ref[...])
        lane, row, lo = q["lane"], q["row"], q["lo"]
        lane1 = lax.broadcasted_iota(jnp.int32, (1, LANES), 1)
        head0 = pl.program_id(0) * heads
        mine = (lane >= head0) & (lane < head0 + heads)

        yv, zv, dynv, nwv = y_ref[...], z_ref[...], dyn_ref[...], nw_ref[...]
        sig = _sigmoid(zv)
        sz = zv * sig
        yg = yv * sz
        rstd = lax.rsqrt(jnp.mean(yg * yg, axis=-1, keepdims=True) + NORM_EPS)
        yhat = yg * rstd
        dnw_ref[...] += jnp.sum(dynv * yhat, axis=0, keepdims=True)
        dyh = dynv * nwv
        dyg = rstd * (dyh - yhat * jnp.mean(dyh * yhat, axis=-1, keepdims=True))
        dz_ref[...] = (dyg * yv * (sig * (1.0 + zv * (1.0 - sig)))).astype(BF16)
        dy_all = dyg * sz

        dg = jnp.zeros((SSD_CHUNK, SSD_CHUNK), F32)
        dc_acc = jnp.zeros((SSD_CHUNK, SSD_N), F32)
        db_acc = jnp.zeros((SSD_CHUNK, SSD_N), F32)
        dcs_mat = jnp.zeros((SSD_CHUNK, LANES), F32)
        ddt_mat = jnp.zeros((SSD_CHUNK, LANES), F32)
        dd_row = jnp.zeros((1, LANES), F32)
        racc_ref[...] = jnp.zeros_like(racc_ref)
        is_last = row == SSD_CHUNK - 1

        for j in range(n_pair):
            sl = slice(j * LANES, (j + 1) * LANES)
            ha, hb = head0 + 2 * j, head0 + 2 * j + 1
            p = _pair_terms(q, cst_ref, ha)
            xs_p = xs_ref[:, sl]
            dyp = dy_all[:, sl]
            xp = xs_p * p["dt_b"]
            xb = xp.astype(BF16)
            s_p = prev_ref[:, sl]
            s_pb = s_p.astype(BF16)
            dsn = ds_ref[:, sl]
            dsnb = dsn.astype(BF16)
            m_f = [q["scores"] * p["lm"][0], q["scores"] * p["lm"][1]]

            t0 = dyp * xs_p
            dd_row = dd_row + jnp.where(lane1 == ha, _sum_all(jnp.where(lo, t0, 0.0)), 0.0) \
                + jnp.where(lane1 == hb, _sum_all(jnp.where(lo, 0.0, t0)), 0.0)
            dxs_p = dl_ref[:, sl] * dyp

            yo = _dot(q["cb"], s_pb) * p["e_b"]
            dcs_b = (dyp * p["e_b"]).astype(BF16)
            dc_acc = dc_acc + _dot(dcs_b, s_pb, "nt")
            ds_yo = _dot(q["cb"], dcs_b, "tn")
            t1 = dyp * yo
            dcs_cols = [jnp.sum(jnp.where(lo, t1, 0.0), axis=1, keepdims=True),
                        jnp.sum(jnp.where(lo, 0.0, t1), axis=1, keepdims=True)]

            t2 = dsn * s_p
            dlast = [p["explast"][0] * _sum_all(jnp.where(lo, t2, 0.0)),
                     p["explast"][1] * _sum_all(jnp.where(lo, 0.0, t2))]
            ds_ref[:, sl] = dsn * p["explast_b"] + ds_yo
            w = _dot(q["bb"], dsnb)
            db_acc = db_acc + _dot((xp * p["dec_b"]).astype(BF16), dsnb, "nt")
            dxp = w * p["dec_b"]
            t3 = w * xp
            e = [jnp.sum(jnp.where(lo, t3, 0.0), axis=1, keepdims=True) * p["dec_cols"][0],
                 jnp.sum(jnp.where(lo, 0.0, t3), axis=1, keepdims=True) * p["dec_cols"][1]]
            for i in range(2):
                dlast[i] = dlast[i] + jnp.sum(e[i], axis=0, keepdims=True)
                dcs_cols[i] = dcs_cols[i] - e[i]

            dyb = dyp.astype(BF16)
            dy_h = [jnp.where(lo, dyp, 0.0).astype(BF16), jnp.where(lo, 0.0, dyp).astype(BF16)]
            dms = [_dot(dy_h[0], xb, "nt"), _dot(dy_h[1], xb, "nt")]
            dxp = dxp + jnp.where(lo, _dot(m_f[0].astype(BF16), dyb, "tn"), _dot(m_f[1].astype(BF16), dyb, "tn"))
            for i, h in enumerate((ha, hb)):
                dg = dg + dms[i] * p["lm"][i]
                qm = dms[i] * m_f[i]
                dcs_cols[i] = dcs_cols[i] + jnp.sum(qm, axis=1, keepdims=True)
                racc_ref[pl.ds(h, 1), :] = jnp.sum(qm, axis=0, keepdims=True)

            dxs_ref[:, sl] = dxs_p + dxp * p["dt_b"]
            t4 = dxp * xs_p
            ddt_cols = [jnp.sum(jnp.where(lo, t4, 0.0), axis=1, keepdims=True),
                        jnp.sum(jnp.where(lo, 0.0, t4), axis=1, keepdims=True)]
            for i, h in enumerate((ha, hb)):
                sel = lane == h
                dcs_mat = dcs_mat + jnp.where(sel, dcs_cols[i], 0.0) + jnp.where(sel & is_last, dlast[i], 0.0)
                ddt_mat = ddt_mat + jnp.where(sel, ddt_cols[i], 0.0)

        dcs_mat = dcs_mat - racc_ref[...].T
        tri_t = (row <= lane).astype(BF16)
        da = _dot_exact01(tri_t, dcs_mat)
        ddt = ddt_mat + da * a_row
        dalog_ref[...] += jnp.sum(jnp.where(mine, da * q["dt"], 0.0), axis=0, keepdims=True) * a_row
        draw = jnp.where(mine, ddt * sg_ref[...], 0.0)
        ddt_ref[...] = draw
        dbias_ref[...] += jnp.sum(draw, axis=0, keepdims=True)
        dd_ref[...] += dd_row
        dgb = dg.astype(BF16)
        dc_ref[...] = dc_acc + _dot(dgb, q["bb"])
        db_ref[...] = db_acc + _dot(dgb, q["cb"], "tn")

    rev = lambda c: nc - 1 - c
    grp = lambda width: pl.BlockSpec((None, 1, width), lambda g, c: (g, 0, 0))
    blk = lambda width, off: pl.BlockSpec((SSD_CHUNK, width), lambda g, c: (rev(c), off + g))
    head_vec = pl.BlockSpec((1, LANES), lambda g, c: (0, 0))
    chunk_rows = pl.BlockSpec((SSD_CHUNK, LANES), lambda g, c: (rev(c), 0))
    dt_, sg_, cs_, cst_, last_ = decay
    return _pcall(
        body, name=name, grid=(SSD_G, nc),
        out_shape=[jax.ShapeDtypeStruct(zx.shape, BF16), jax.ShapeDtypeStruct((L, d_inner), F32),
                   jax.ShapeDtypeStruct((L, SSD_G * SSD_N), F32), jax.ShapeDtypeStruct((L, SSD_G * SSD_N), F32),
                   jax.ShapeDtypeStruct((SSD_G, L, LANES), F32),
                   jax.ShapeDtypeStruct((SSD_G, 1, LANES), F32), jax.ShapeDtypeStruct((SSD_G, 1, LANES), F32),
                   jax.ShapeDtypeStruct((SSD_G, 1, LANES), F32), jax.ShapeDtypeStruct((SSD_G, 1, gw), F32)],
        in_specs=[blk(gw, 0), blk(gw, 0), blk(gw, 0), blk(gw, 0), blk(SSD_N, bc0), blk(SSD_N, bc0 + SSD_G),
                  pl.BlockSpec((None, None, SSD_N, gw), lambda g, c: (rev(c), g, 0, 0)),
                  chunk_rows, chunk_rows, chunk_rows,
                  pl.BlockSpec((None, SSD_CHUNK, LANES), lambda g, c: (rev(c), 0, 0)),
                  pl.BlockSpec((None, 1, LANES), lambda g, c: (rev(c), 0, 0)),
                  head_vec, grp(gw), grp(gw)],
        out_specs=[blk(gw, 0), blk(gw, 0), blk(SSD_N, 0), blk(SSD_N, 0),
                   pl.BlockSpec((None, SSD_CHUNK, LANES), lambda g, c: (g, rev(c), 0)),
                   grp(LANES), grp(LANES), grp(LANES), grp(gw)],
        scratch_shapes=[pltpu.VMEM((SSD_N, gw), F32), pltpu.VMEM((SSD_CHUNK, LANES), F32)],
        compiler_params=_cparams(("parallel", "arbitrary")))(
            dyn, y, zx, xc, xc, xc, prev, dt_, sg_, cs_, cst_, last_, alog_p, d_lane, nw)


def _cond_mod(c_pad, ada_w, ada_b_loc, after, name):
    depth, D, n = ada_w.shape
    rows = c_pad.shape[0]

    def body(c_ref, w_ref, b_ref, after_ref, mod_ref, cond_ref):
        cv = c_ref[...]
        cond = cv * _sigmoid(cv)
        cond_ref[...] = cond
        mod_ref[...] = _dot(cond.astype(BF16), w_ref[...].astype(BF16)) + b_ref[...]

    return _pcall(
        body, name=name, grid=(depth,),
        out_shape=[jax.ShapeDtypeStruct((depth, rows, n), F32), jax.ShapeDtypeStruct((rows, D), F32)],
        in_specs=[pl.BlockSpec((rows, D), lambda i: (0, 0)),
                  pl.BlockSpec((None, D, n), lambda i: (i, 0, 0)),
                  pl.BlockSpec((None, 1, n), lambda i: (i, 0, 0)),
                  pl.BlockSpec(memory_space=pl.ANY)],
        out_specs=[pl.BlockSpec((None, rows, n), lambda i: (i, 0, 0)),
                   pl.BlockSpec((rows, D), lambda i: (0, 0))],
        compiler_params=_cparams(("arbitrary",)))(c_pad, ada_w, ada_b_loc, after)


def _adamw_math(g, w, m, v):
    m_new = ADAM_B1 * m + (1.0 - ADAM_B1) * g
    v_new = ADAM_B2 * v + (1.0 - ADAM_B2) * (g * g)
    m_hat = m_new / (1.0 - ADAM_B1 ** ADAM_STEP)
    v_hat = v_new / (1.0 - ADAM_B2 ** ADAM_STEP)
    delta = -ADAM_LR * (m_hat / (jnp.sqrt(v_hat) + ADAM_EPS) + ADAM_WD * w)
    return delta, m_new, v_new


def _adamw_sum(parts, w, m, v, layer, name, prev=None, tr=None):
    depth, R, C = w.shape
    tr = _tile(R, tr if tr is not None else (512 if C <= 512 else 256))

    def body(p_ref, w_ref, m_ref, v_ref, *rest):
        g_ref, d_ref, mo_ref, vo_ref = rest[-4:]
        g = p_ref[0].astype(F32)
        for k in range(1, N_DEV):
            g = g + p_ref[k].astype(F32)
        d, mn, vn = _adamw_math(g, w_ref[...], m_ref[...], v_ref[...])
        g_ref[...] = g
        d_ref[...] = d
        mo_ref[...] = mn
        vo_ref[...] = vn

    blk = pl.BlockSpec((None, tr, C), lambda i: (layer, i, 0))
    prev = list(prev) if prev is not None else []
    return _pcall(
        body, name=name, grid=(R // tr,),
        out_shape=[jax.ShapeDtypeStruct((depth, R, C), F32)] * 4,
        in_specs=[pl.BlockSpec((N_DEV, tr, C), lambda i: (0, i, 0)), blk, blk, blk]
        + [pl.BlockSpec(memory_space=pl.ANY)] * len(prev),
        out_specs=[blk] * 4, input_output_aliases={4 + k: k for k in range(len(prev))},
        compiler_params=_cparams(("parallel",)))(parts, w, m, v, *prev)


def _adamw_small(parts, wmv, head_parts, head_wmv, loss_parts, name):
    n, nh = len(parts), len(head_parts)
    n_heads = head_wmv[0][0].shape[1] if nh else 0
    groups = head_parts[0].shape[1] if nh else 0
    d_model = loss_parts.shape[2]

    def body(*refs):
        p_refs, refs = refs[:n], refs[n:]
        wmv_refs, refs = refs[:3 * n], refs[3 * n:]
        hp_refs, refs = refs[:nh], refs[nh:]
        hwmv_refs, refs = refs[:3 * nh], refs[3 * nh:]
        loss_ref, refs = refs[0], refs[1:]
        outs, loss_out, head_scr = refs[:4 * (n + nh)], refs[4 * (n + nh)], refs[4 * (n + nh) + 1]

        def update(i, g, w_ref, m_ref, v_ref):
            res = (g,) + _adamw_math(g, w_ref[...], m_ref[...], v_ref[...])
            for o_ref, r in zip(outs[4 * i:4 * i + 4], res):
                o_ref[...] = r

        for i in range(n):
            g = p_refs[i][0]
            for k in range(1, N_DEV):
                g = g + p_refs[i][k]
            update(i, g, *wmv_refs[3 * i:3 * i + 3])
        for i in range(nh):
            g = None
            for k in range(N_DEV):
                for grp in range(groups):
                    g = hp_refs[i][k, grp] if g is None else g + hp_refs[i][k, grp]
            head_scr[...] = g
            update(n + i, head_scr[:, 0:n_heads], *hwmv_refs[3 * i:3 * i + 3])
        tot = loss_ref[0]
        for k in range(1, N_DEV):
            tot = tot + loss_ref[k]
        loss_out[...] = jnp.broadcast_to(_sum_all(tot) * (0.5 / d_model), loss_out.shape)

    operands = list(parts) + [a for t in wmv for a in t] + list(head_parts) + [a for t in head_wmv for a in t]
    operands.append(loss_parts)
    out_shape = [jax.ShapeDtypeStruct(t[0].shape, F32) for t in list(wmv) + list(head_wmv) for _ in range(4)]
    out_shape.append(jax.ShapeDtypeStruct((1, LANES), F32))
    vmem = pl.BlockSpec(memory_space=pltpu.VMEM)
    outs = _pcall(body, name=name, out_shape=out_shape, in_specs=[vmem] * len(operands),
                  out_specs=[vmem] * len(out_shape), scratch_shapes=[pltpu.VMEM((1, LANES), F32)],
                  compiler_params=_cparams())(*operands)
    return [outs[4 * i:4 * i + 4] for i in range(n + nh)], outs[-1]


def _ada_adamw(cond_pad, dmod_pad, w, m, v, name, tr=512):
    depth, D, n = w.shape
    rows = cond_pad.shape[0]
    tr = _tile(D, tr)

    def body(c_ref, dm_ref, w_ref, m_ref, v_ref, g_ref, d_ref, mo_ref, vo_ref):
        g = _dot(c_ref[...].astype(BF16), dm_ref[...].astype(BF16), "tn")
        d, mn, vn = _adamw_math(g, w_ref[...], m_ref[...], v_ref[...])
        g_ref[...] = g
        d_ref[...] = d
        mo_ref[...] = mn
        vo_ref[...] = vn

    blk = pl.BlockSpec((None, tr, n), lambda i, r: (i, r, 0))
    return _pcall(
        body, name=name, grid=(depth, D // tr),
        out_shape=[jax.ShapeDtypeStruct((depth, D, n), F32)] * 4,
        in_specs=[pl.BlockSpec((rows, tr), lambda i, r: (0, r)),
                  pl.BlockSpec((None, rows, n), lambda i, r: (i, 0, 0)), blk, blk, blk],
        out_specs=[blk] * 4, compiler_params=_cparams(("parallel", "parallel")))(cond_pad, dmod_pad, w, m, v)


def kernel(x, c, ada_w, ada_b, mix_norm_w, mlp_norm_w, mlp_up, mlp_down, ssd_in_w, ssd_conv_w, ssd_conv_b, ssd_dt_bias, ssd_A_log, ssd_D, ssd_norm_w, ssd_out_w, sc_in_w, sc_conv_w, sc_out_w, final_norm_w, loss_target, m_ada_w, m_ada_b, m_mix_norm_w, m_mlp_norm_w, m_mlp_up, m_mlp_down, m_ssd_in_w, m_ssd_conv_w, m_ssd_conv_b, m_ssd_dt_bias, m_ssd_A_log, m_ssd_D, m_ssd_norm_w, m_ssd_out_w, m_sc_in_w, m_sc_conv_w, m_sc_out_w, m_final_norm_w, v_ada_w, v_ada_b, v_mix_norm_w, v_mlp_norm_w, v_mlp_up, v_mlp_down, v_ssd_in_w, v_ssd_conv_w, v_ssd_conv_b, v_ssd_dt_bias, v_ssd_A_log, v_ssd_D, v_ssd_norm_w, v_ssd_out_w, v_sc_in_w, v_sc_conv_w, v_sc_out_w, v_final_norm_w):
    weights = dict(ada_w=ada_w, ada_b=ada_b, mix_norm_w=mix_norm_w, mlp_norm_w=mlp_norm_w, mlp_up=mlp_up,
                   mlp_down=mlp_down, ssd_in_w=ssd_in_w, ssd_conv_w=ssd_conv_w, ssd_conv_b=ssd_conv_b,
                   ssd_dt_bias=ssd_dt_bias, ssd_A_log=ssd_A_log, ssd_D=ssd_D, ssd_norm_w=ssd_norm_w,
                   ssd_out_w=ssd_out_w, sc_in_w=sc_in_w, sc_conv_w=sc_conv_w, sc_out_w=sc_out_w,
                   final_norm_w=final_norm_w)
    moms = dict(ada_w=m_ada_w, ada_b=m_ada_b, mix_norm_w=m_mix_norm_w, mlp_norm_w=m_mlp_norm_w, mlp_up=m_mlp_up,
                mlp_down=m_mlp_down, ssd_in_w=m_ssd_in_w, ssd_conv_w=m_ssd_conv_w, ssd_conv_b=m_ssd_conv_b,
                ssd_dt_bias=m_ssd_dt_bias, ssd_A_log=m_ssd_A_log, ssd_D=m_ssd_D, ssd_norm_w=m_ssd_norm_w,
                ssd_out_w=m_ssd_out_w, sc_in_w=m_sc_in_w, sc_conv_w=m_sc_conv_w, sc_out_w=m_sc_out_w,
                final_norm_w=m_final_norm_w)
    vars_ = dict(ada_w=v_ada_w, ada_b=v_ada_b, mix_norm_w=v_mix_norm_w, mlp_norm_w=v_mlp_norm_w, mlp_up=v_mlp_up,
                 mlp_down=v_mlp_down, ssd_in_w=v_ssd_in_w, ssd_conv_w=v_ssd_conv_w, ssd_conv_b=v_ssd_conv_b,
                 ssd_dt_bias=v_ssd_dt_bias, ssd_A_log=v_ssd_A_log, ssd_D=v_ssd_D, ssd_norm_w=v_ssd_norm_w,
                 ssd_out_w=v_ssd_out_w, sc_in_w=v_sc_in_w, sc_conv_w=v_sc_conv_w, sc_out_w=v_sc_out_w,
                 final_norm_w=v_final_norm_w)
    names = list(weights)

    L, D = x.shape[1], x.shape[2]
    d_inner = 2 * D
    n_heads = d_inner // SSD_P
    hpg = n_heads // SSD_G
    gw = d_inner // SSD_G
    conv_dim = d_inner + 2 * SSD_G * SSD_N
    zx_dim = d_inner + conv_dim
    zx_pad = -(-(zx_dim + LANES) // 512) * 512
    in_ws = ssd_in_w.shape[2]
    in_base, in_off, in_win = _window_geometry(in_ws)
    me = _my_index()
    x0 = x[0]
    tgt = loss_target[0]

    n_mod = ada_w.shape[2]
    (c_all,) = _exchange([c], "gather_c", gather=True)
    gather_handle = {}
    (gather_handle["ssd_in_w"],), token_in = _xfer_start(
        [ssd_in_w[0].astype(BF16)], "gather_start_ssd_in_w", gather=True, via_sibling=(0,), after=(c_all,))
    c_pad = jnp.pad(c_all.reshape(N_DEV, D), ((0, 16 - N_DEV), (0, 0)))
    ada_b_loc = lax.dynamic_slice_in_dim(ada_b, me * n_mod, n_mod, axis=1).reshape(2, 1, n_mod)
    mod_blk, cond_pad = _cond_mod(c_pad, ada_w, ada_b_loc, token_in, "cond_mod")
    gather_order = ["mod", "ssd_conv_w", "sc_conv_w", "ssd_out_w", "up0", "down0", "sc_in_w", "sc_out_w", "up1",
                    "down1"]
    gather_src = dict(mod=mod_blk, ssd_conv_w=ssd_conv_w[0], sc_conv_w=sc_conv_w[0],
                      ssd_out_w=ssd_out_w[0].astype(BF16),
                      up0=mlp_up[0].astype(BF16), down0=mlp_down[0].astype(BF16),
                      sc_in_w=sc_in_w[0].astype(BF16), sc_out_w=sc_out_w[0].astype(BF16),
                      up1=mlp_up[1].astype(BF16), down1=mlp_down[1].astype(BF16))
    handles, gather_token = _xfer_start([gather_src[k] for k in gather_order], "gather_start", gather=True,
                                        via_sibling=tuple(range(3, len(gather_order))))
    gather_handle.update(zip(gather_order, handles))

    def gathered(keys, after, forward):
        tag = "_".join(keys)
        lands = _xfer_wait([gather_handle[k] for k in keys], after, f"gather_wait_{tag}", gather=True)
        return _sibling_forward(lands, f"gather_forward_{tag}") if forward else lands

    def forward_behind(keys, after):
        tag = "_".join(keys)
        lands = _xfer_wait([gather_handle[k] for k in keys], after, f"gather_wait_{tag}", gather=True)
        fwd_handles, token = _sibling_forward_start(lands, f"gather_forward_start_{tag}")
        return (lambda done: _sibling_forward_wait(fwd_handles, done, f"gather_forward_wait_{tag}")), token

    (ssd_in_g,) = gathered(["ssd_in_w"], (gather_token, m_ssd_in_w, v_ssd_in_w), True)
    w_in_all = _shards_to_columns(ssd_in_g, in_base, in_off, in_win, zx_pad, "ssd_in_w_columns")
    (mod_all,) = gathered(["mod"], w_in_all, False)
    mod_mine = lax.dynamic_index_in_dim(mod_all, me, axis=2, keepdims=False)
    mod_mine = jnp.transpose(mod_mine, (1, 0, 2)).reshape(2, 6, 1, D)
    sh_m, sc_m, g_m, sh_f, sc_f, g_f = [[mod_mine[i, k] for i in range(2)] for k in range(6)]

    vec = lambda a: a.reshape(1, -1)
    small = {}

    _, h0 = _norm_mod_fwd(x0, None, None, vec(mix_norm_w[0]), sc_m[0], sh_m[0], "l0_mix_norm")
    cw_all, scw_all = gathered(["ssd_conv_w", "sc_conv_w"], h0, False)
    (zx,) = _mm_nn(h0, w_in_all, F32, "ssd_in_proj", tm=2048, tn=512)
    conv_b0 = vec(ssd_conv_b[0])
    conv_w_full = jnp.transpose(cw_all, (1, 0, 2)).reshape(SSD_K, conv_dim)
    sc_conv_full = jnp.transpose(scw_all, (1, 0, 2)).reshape(SC_K, D)
    xc = _ssd_conv_fwd(zx, conv_w_full, conv_b0, d_inner, conv_dim, "ssd_conv")
    bias_p = jnp.pad(ssd_dt_bias[0], (0, LANES - n_heads)).reshape(1, LANES)
    alog_p = jnp.pad(ssd_A_log[0], (0, LANES - n_heads)).reshape(1, LANES)
    d_lane = jnp.repeat(ssd_D[0], SSD_P).reshape(SSD_G, 1, gw)
    nw_g = ssd_norm_w[0].reshape(SSD_G, 1, gw)
    finish, token = forward_behind(["ssd_out_w"], xc)
    decay = _ssd_decay(zx, bias_p, alog_p, n_heads, zx_dim // LANES, "ssd_decay")
    y_ssd, yn, prev = _ssd_fwd(zx, xc, decay, d_lane, nw_g, d_inner, token, "ssd_scan")
    ups, downs = [None, None], [None, None]
    (ssd_out_g,) = finish(yn)
    w_ssd_out = ssd_out_g.reshape(-1, D)
    finish, token = forward_behind(["up0", "down0"], ssd_out_g)
    (mix0,) = _mm_nn(yn, w_ssd_out, F32, "ssd_out_proj", after=(token,))
    x1, h1 = _norm_mod_fwd(x0, mix0, g_m[0], vec(mlp_norm_w[0]), sc_f[0], sh_f[0], "l0_mlp_norm")
    ups[0], down0_g = finish(h1)
    downs[0] = down0_g.reshape(-1, D)
    u0, s0 = _mm_nn_blocked(h1, ups[0], "l0_mlp_up", _ep_relu2, [BF16, BF16])
    finish, token = forward_behind(["sc_in_w", "sc_out_w", "up1", "down1"], s0)
    (d0,) = _mm_nn(s0, downs[0], F32, "l0_mlp_down", after=(token,))
    x2, h2 = _norm_mod_fwd(x1, d0, g_f[0], vec(mix_norm_w[1]), sc_m[1], sh_m[1], "l1_mix_norm")
    sc_in_g, sc_out_g, ups[1], down1_g = finish(h2)
    w_sc_out, downs[1] = sc_out_g.reshape(-1, D), down1_g.reshape(-1, D)
    (proj,) = _mm_nn_blocked(h2, sc_in_g, "sc_in_proj", _ep_store(F32), [F32])
    yc = _sc_conv_fwd(proj, sc_conv_full, "sc_conv")
    (mix1,) = _mm_nn(yc, w_sc_out, F32, "sc_out_proj")
    x3, h3 = _norm_mod_fwd(x2, mix1, g_m[1], vec(mlp_norm_w[1]), sc_f[1], sh_f[1], "l1_mlp_norm")
    u1, s1 = _mm_nn_blocked(h3, ups[1], "l1_mlp_up", _ep_relu2, [BF16, BF16])
    (d1,) = _mm_nn(s1, downs[1], F32, "l1_mlp_down")

    dx, loss_lane, dfw, dd1, dg = _final_loss(x3, d1, g_f[1], vec(final_norm_w), tgt, "final_loss")
    small["final_norm_w"] = dfw

    dmod = [[None] * 6 for _ in range(2)]
    dmod[1][5] = dg

    def mlp_backward(i, dx_out, dd, x_mid, h_in, u, s, mix, gate):
        du = _mm_nt(dd, downs[i], BF16, f"l{i}_mlp_down_bwd", epilogue=_ep_relu2_bwd, extra=(u,))
        gdown = _mm_tn(s, dd, BF16, f"l{i}_mlp_down_wgrad").reshape(N_DEV, -1, D)
        gup = _mm_tn_blocked(h_in, du, BF16, f"l{i}_mlp_up_wgrad")
        dh = _mm_nt_blocked(du, ups[i], F32, f"l{i}_mlp_up_bwd")
        dxm, dsh, dsc, dnw, dmix, dgate = _norm_mod_bwd(dh, x_mid, vec(mlp_norm_w[i]), sc_f[i], dx_out,
                                                        f"l{i}_mlp_norm_bwd", branch=(mix, gate))
        dmod[i][3], dmod[i][4], dmod[i][2] = dsh, dsc, dgate
        return dxm, dmix, dnw, gdown, gup

    grad_handle = {}
    dx3, dyc, dnw_mlp1, gdown1, gup1 = mlp_backward(1, dx, dd1, x3, h3, u1, s1, mix1, g_m[1])
    g_sc_out = _mm_tn(yc, dyc, BF16, "sc_out_wgrad").reshape(N_DEV, -1, D)
    dconv_out = _mm_nt(dyc, w_sc_out, F32, "sc_out_bwd")
    dbg, dcg, dxv, dscw = _sc_conv_bwd(proj, sc_conv_full, dconv_out, "sc_conv_bwd")
    dproj = jnp.concatenate([dbg, dcg, dxv], axis=1)
    g_sc_in = _mm_tn_blocked(h2, dproj, BF16, "sc_in_wgrad")
    early = ["mlp_down1", "mlp_up1", "sc_out_w0", "sc_in_w0"]
    handles, token = _xfer_start([gdown1, gup1, g_sc_out, g_sc_in], "l1_grads_start", gather=False)
    grad_handle.update(zip(early, handles))
    dh2 = _mm_nt_blocked(dproj, sc_in_g, F32, "sc_in_bwd", after=(token,))
    dx2, dsh, dsc, dnw_mix1, dd0, dg = _norm_mod_bwd(dh2, x2, vec(mix_norm_w[1]), sc_m[1], dx3, "l1_mix_norm_bwd",
                                                     branch=(d0, g_f[0]))
    dmod[1][0], dmod[1][1], dmod[0][5] = dsh, dsc, dg
    dx1, dyo, dnw_mlp0, gdown0, gup0 = mlp_backward(0, dx2, dd0, x1, h1, u0, s0, mix0, g_m[0])
    g_ssd_out = _mm_tn(yn, dyo, BF16, "ssd_out_wgrad").reshape(N_DEV, -1, D)
    handles, token = _xfer_start([gdown0, gup0, g_ssd_out], "l0_grads_start", gather=False)
    grad_handle.update(zip(["mlp_down0", "mlp_up0", "ssd_out_w0"], handles))
    early += ["mlp_down0", "mlp_up0", "ssd_out_w0"]
    dyn = _mm_nt(dyo, w_ssd_out, F32, "ssd_out_bwd", after=(token,))
    dz, dxs, db_, dc_, ddt, dbias, dalog, dd_, dnw_ssd = _ssd_bwd(
        dyn, y_ssd, zx, xc, prev, decay, alog_p, d_lane, nw_g, d_inner, "ssd_scan_bwd")
    dzx, dcw, dcb = _ssd_conv_bwd(zx, conv_w_full, conv_b0, [dxs, db_, dc_], dz, d_inner, "ssd_conv_bwd")
    dzx = _dzx_finish(dzx, ddt, zx_dim, "ssd_dzx_finish")
    g_in_all = _mm_tn(h0, dzx, BF16, "ssd_in_wgrad", tn=512, tk=2048)
    g_ssd_in = _columns_to_shards(g_in_all, in_ws, in_base, in_off, in_win, "ssd_in_wgrad_shards")
    (grad_handle["ssd_in_w0"],), token = _xfer_start([g_ssd_in], "ssd_in_grad_start", gather=False)
    dh0 = _mm_nt(dzx, w_in_all, F32, "ssd_in_bwd", tm=1024, tk=dzx.shape[1] // 2, after=(token,))
    grad_x, dsh, dsc, dnw_mix0 = _norm_mod_bwd(dh0, x0, vec(mix_norm_w[0]), sc_m[0], dx1, "l0_mix_norm_bwd")
    dmod[0][0], dmod[0][1] = dsh, dsc

    small["ada_b"] = jnp.concatenate([jnp.concatenate(dmod[i], axis=1) for i in range(2)], axis=0)
    small["mix_norm_w"] = jnp.concatenate([dnw_mix0, dnw_mix1], axis=0)
    small["mlp_norm_w"] = jnp.concatenate([dnw_mlp0, dnw_mlp1], axis=0)
    small["ssd_conv_w"] = dcw
    small["ssd_conv_b"] = dcb
    small["ssd_norm_w"] = dnw_ssd.reshape(1, d_inner)
    small["sc_conv_w"] = dscw
    small["loss"] = loss_lane
    small_names = list(small)
    head_names = ["ssd_dt_bias", "ssd_A_log", "ssd_D"]
    small_handles, small_token = _xfer_start([small[k] for k in small_names] + [dbias, dalog, dd_],
                                             "small_grads_start", gather=True)

    out_g, out_d, out_m, out_v = {}, {}, {}, {}

    layer_res = {}

    def big_update(key, parts):
        name, i = key[:-1], int(key[-1])
        res = _adamw_sum(parts, weights[name], moms[name], vars_[name], i, f"adamw_{name}_{i}",
                         prev=layer_res.get(name))
        layer_res[name] = res
        return res[1]

    early_parts = _xfer_wait([grad_handle[k] for k in early], small_token, "grads_wait_early", gather=False)
    done = tuple(big_update(k, p) for k, p in zip(early, early_parts))
    (late_parts,) = _xfer_wait([grad_handle["ssd_in_w0"]], done, "grads_wait_ssd_in_w", gather=False)
    chain = big_update("ssd_in_w0", late_parts)
    gathered_small = _xfer_wait(small_handles, chain, "small_grads_wait", gather=True)
    small_all = dict(zip(small_names + head_names, gathered_small))

    dmod_loc = lax.dynamic_slice_in_dim(small_all["ada_b"], me * n_mod, n_mod, axis=2)
    dmod_pad = jnp.pad(jnp.transpose(dmod_loc, (1, 0, 2)), ((0, 0), (0, 16 - N_DEV), (0, 0)))
    out_g["ada_w"], out_d["ada_w"], out_m["ada_w"], out_v["ada_w"] = _ada_adamw(
        cond_pad, dmod_pad, ada_w, m_ada_w, v_ada_w, "adamw_ada_w")

    for k in ("ssd_conv_w", "sc_conv_w"):
        n_loc = weights[k].shape[2]
        small_all[k] = lax.dynamic_slice_in_dim(small_all[k], me * n_loc, n_loc, axis=2)
    plain = [k for k in small_names if k != "loss"]
    as2d = lambda a: a.reshape(-1, a.shape[-1])
    res, loss_row = _adamw_small(
        [small_all[k] for k in plain], [tuple(as2d(d[k]) for d in (weights, moms, vars_)) for k in plain],
        [small_all[k] for k in head_names], [tuple(as2d(d[k]) for d in (weights, moms, vars_)) for k in head_names],
        small_all["loss"], "adamw_small")
    loss = loss_row[0, 0]
    for k, res4 in zip(plain + head_names, res):
        for r, dst in zip(res4, (out_g, out_d, out_m, out_v)):
            dst[k] = r.reshape(weights[k].shape)
    for name, res4 in layer_res.items():
        for r, dst in zip(res4, (out_g, out_d, out_m, out_v)):
            dst[name] = r

    return (loss, grad_x[None], *[out_g[k] for k in names], *[out_d[k] for k in names],
            *[out_m[k] for k in names], *[out_v[k] for k in names])
```

```python
import jax
import jax.numpy as jnp
from jax import lax
from jax.experimental import pallas as pl
from jax.experimental.pallas import tpu as pltpu

F32 = jnp.float32
BF16 = jnp.bfloat16
N_DEV = 8
MESH = pl.DeviceIdType.MESH

NORM_EPS = 1e-5
SSD_G = 4
SSD_P = 64
SSD_N = 128
SSD_CHUNK = 128
SSD_K = 4
SC_K = 3
LANES = 128

ADAM_LR = 0.001
ADAM_B1 = 0.9
ADAM_B2 = 0.999
ADAM_EPS = 1e-08
ADAM_WD = 0.01
ADAM_STEP = 10

VMEM_LIMIT = 56 * 1024 * 1024


def _pcall(body, **kw):
    return pl.pallas_call(body, **kw)


def _cparams(sem=None):
    if sem is None:
        return pltpu.CompilerParams(vmem_limit_bytes=VMEM_LIMIT)
    return pltpu.CompilerParams(dimension_semantics=sem, vmem_limit_bytes=VMEM_LIMIT)


def _my_index():
    return 4 * lax.axis_index("x") + 2 * lax.axis_index("y") + lax.axis_index("c")


_PEER_MASKS = [(0, 0, 1), (0, 1, 0), (0, 1, 1), (1, 0, 0), (1, 0, 1), (1, 1, 0), (1, 1, 1)]


def _peers():
    x, y, c = lax.axis_index("x"), lax.axis_index("y"), lax.axis_index("c")
    out = []
    for mx, my, mc in _PEER_MASKS:
        px = (1 - x) if mx else x
        py = (1 - y) if my else y
        pc = (1 - c) if mc else c
        out.append(((px, py, pc), 4 * px + 2 * py + pc))
    return out


def _exchange(arrs, name, gather):
    n = len(arrs)
    n_peer = N_DEV - 1

    def body(*refs):
        ins, outs = refs[:n], refs[n:2 * n]
        send_sems, recv_sems, local_sems = refs[2 * n:]
        me = _my_index()
        peers = _peers()
        started = []
        for a in range(n):
            src_own = ins[a] if gather else ins[a].at[me]
            own = pltpu.make_async_copy(src_own, outs[a].at[me], local_sems.at[a])
            own.start()
            started.append(own)
        sends = []
        for a in range(n):
            for k, (peer, pidx) in enumerate(peers):
                src = ins[a] if gather else ins[a].at[pidx]
                cp = pltpu.make_async_remote_copy(
                    src_ref=src, dst_ref=outs[a].at[me],
                    send_sem=send_sems.at[a * n_peer + k], recv_sem=recv_sems.at[a * n_peer + k],
                    device_id=peer, device_id_type=MESH)
                cp.start()
                sends.append(cp)
        for a in range(n):
            for k, (peer, pidx) in enumerate(peers):
                src = ins[a] if gather else ins[a].at[pidx]
                pltpu.make_async_remote_copy(
                    src_ref=src, dst_ref=outs[a].at[pidx],
                    send_sem=send_sems.at[a * n_peer + k], recv_sem=recv_sems.at[a * n_peer + k],
                    device_id=peer, device_id_type=MESH).wait_recv()
        for cp in sends:
            cp.wait_send()
        for own in started:
            own.wait()

    if gather:
        out_shape = [jax.ShapeDtypeStruct((N_DEV,) + a.shape, a.dtype) for a in arrs]
    else:
        out_shape = [jax.ShapeDtypeStruct(a.shape, a.dtype) for a in arrs]
    any_spec = pl.BlockSpec(memory_space=pl.ANY)
    outs = _pcall(
        body, name=name, out_shape=out_shape,
        in_specs=[any_spec] * n, out_specs=[any_spec] * n,
        scratch_shapes=[pltpu.SemaphoreType.DMA((n * n_peer,)), pltpu.SemaphoreType.DMA((n * n_peer,)),
                        pltpu.SemaphoreType.DMA((n,))],
        compiler_params=pltpu.CompilerParams(has_side_effects=True),
    )(*arrs)
    return list(outs)


def _sibling_forward_start(lands, name):
    n = len(lands)
    n_fwd = len(_OTHER_CHIPS)

    def body(*refs):
        ins, bufs = refs[:n], refs[3 * n:4 * n]
        token = refs[-1]
        sibling = (lax.axis_index("x"), lax.axis_index("y"), 1 - lax.axis_index("c"))
        peers = _peers()
        for a in range(n):
            send_sems, recv_sems = refs[n + 2 * a], refs[n + 2 * a + 1]
            for j, k in enumerate(_OTHER_CHIPS):
                slot = peers[k][1]
                pltpu.make_async_remote_copy(
                    src_ref=ins[a].at[slot], dst_ref=bufs[a].at[slot], send_sem=send_sems.at[j],
                    recv_sem=recv_sems.at[j], device_id=sibling, device_id_type=MESH).start()
        token[...] = jnp.zeros_like(token)

    out_shape, out_specs = [], []
    for _ in range(n):
        out_shape += [pltpu.SemaphoreType.DMA((n_fwd,)), pltpu.SemaphoreType.DMA((n_fwd,))]
        out_specs += [_SEM, _SEM]
    out_shape += [pltpu.HBM(a.shape, a.dtype) for a in lands] + [jax.ShapeDtypeStruct((8, LANES), F32)]
    out_specs += [_HBM] * n + [pl.BlockSpec(memory_space=pltpu.VMEM)]
    outs = _pcall(
        body, name=name, out_shape=tuple(out_shape), in_specs=[_HBM] * n, out_specs=tuple(out_specs),
        input_output_aliases={a: 2 * n + a for a in range(n)},
        compiler_params=pltpu.CompilerParams(has_side_effects=_DATAFLOW),
    )(*[pltpu.with_memory_space_constraint(a, pltpu.HBM) for a in lands])
    return [(outs[2 * n + a], outs[2 * a], outs[2 * a + 1]) for a in range(n)], outs[-1]


def _sibling_forward_wait(handles, after, name):
    n = len(handles)

    def body(*refs):
        sibling = (lax.axis_index("x"), lax.axis_index("y"), 1 - lax.axis_index("c"))
        peers = _peers()
        for a in range(n):
            buf, send_sems, recv_sems = refs[3 * a:3 * a + 3]
            for j, k in enumerate(_OTHER_CHIPS):
                (px, py, pc), slot = peers[k]
                theirs = 4 * px + 2 * py + (1 - pc)
                cp = pltpu.make_async_remote_copy(
                    src_ref=buf.at[slot], dst_ref=buf.at[theirs], send_sem=send_sems.at[j],
                    recv_sem=recv_sems.at[j], device_id=sibling, device_id_type=MESH)
                cp.wait_send()
                cp.wait_recv()

    operands, in_specs = [], []
    for h in handles:
        operands += list(h)
        in_specs += [_HBM, _SEM, _SEM]
    outs = _pcall(
        body, name=name, out_shape=tuple(pltpu.HBM(h[0].shape, h[0].dtype) for h in handles),
        in_specs=in_specs + [pl.BlockSpec(memory_space=pl.ANY)], out_specs=tuple([_HBM] * n),
        input_output_aliases={3 * a: a for a in range(n)},
        compiler_params=pltpu.CompilerParams(has_side_effects=_DATAFLOW),
    )(*operands, after)
    return list(outs)


_HBM = pl.BlockSpec(memory_space=pltpu.HBM)
_SEM = pl.BlockSpec(memory_space=pltpu.SEMAPHORE)
_DATAFLOW = pltpu.SideEffectType.DATAFLOW_SIDE_EFFECTING


_ALL_PEERS = tuple(range(N_DEV - 1))
_SAME_CORE_PEERS = (0, 1, 3, 5)
_OTHER_CHIPS = (1, 3, 5)


def _xfer_start(arrs, name, gather, via_sibling=(), after=()):
    n = len(arrs)
    n_peer = N_DEV - 1
    n_after = len(after)
    peer_ks = [_SAME_CORE_PEERS if a in via_sibling else _ALL_PEERS for a in range(n)]

    def body(*refs):
        ins, lands = refs[:n], refs[n:2 * n]
        sems = refs[2 * n + n_after:5 * n + n_after]
        token = refs[-1]
        me = _my_index()
        peers = _peers()
        for a in range(n):
            send_sems, recv_sems, loc_sem = sems[3 * a:3 * a + 3]
            src_own = ins[a] if gather else ins[a].at[me]
            pltpu.make_async_copy(src_own, lands[a].at[me], loc_sem).start()
            for k in peer_ks[a]:
                peer, pidx = peers[k]
                src = ins[a] if gather else ins[a].at[pidx]
                pltpu.make_async_remote_copy(
                    src_ref=src, dst_ref=lands[a].at[me], send_sem=send_sems.at[k], recv_sem=recv_sems.at[k],
                    device_id=peer, device_id_type=MESH).start()
        token[...] = jnp.zeros_like(token)

    land_shapes = [((N_DEV,) + a.shape) if gather else a.shape for a in arrs]
    out_shape, out_specs = [], []
    for _ in range(n):
        out_shape += [pltpu.SemaphoreType.DMA((n_peer,)), pltpu.SemaphoreType.DMA((n_peer,)),
                      pltpu.SemaphoreType.DMA(())]
        out_specs += [_SEM, _SEM, _SEM]
    out_shape += [pltpu.HBM(a.shape, a.dtype) for a in arrs]
    out_shape += [pltpu.HBM(s, a.dtype) for s, a in zip(land_shapes, arrs)]
    out_shape += [jax.ShapeDtypeStruct((8, LANES), F32)]
    out_specs += [_HBM] * (2 * n) + [pl.BlockSpec(memory_space=pltpu.VMEM)]
    aliases = {}
    for a in range(n):
        aliases[a] = 3 * n + a
        aliases[n + a] = 4 * n + a
    operands = [pltpu.with_memory_space_constraint(a, pltpu.HBM) for a in arrs]
    operands += [pltpu.with_memory_space_constraint(lax.empty(s, a.dtype), pltpu.HBM)
                 for s, a in zip(land_shapes, arrs)]
    outs = _pcall(
        body, name=name, out_shape=tuple(out_shape),
        in_specs=[_HBM] * (2 * n) + [pl.BlockSpec(memory_space=pl.ANY)] * n_after, out_specs=tuple(out_specs),
        input_output_aliases=aliases,
        compiler_params=pltpu.CompilerParams(has_side_effects=_DATAFLOW),
    )(*operands, *after)
    handles = []
    for a in range(n):
        handles.append((outs[3 * n + a], outs[4 * n + a], outs[3 * a], outs[3 * a + 1], outs[3 * a + 2],
                        peer_ks[a]))
    return handles, outs[-1]


def _xfer_wait(handles, after, name, gather):
    n = len(handles)
    after = tuple(after) if isinstance(after, (tuple, list)) else (after,)
    peer_ks = [h[5] for h in handles]

    def body(*refs):
        me = _my_index()
        peers = _peers()
        for a in range(n):
            src_ref, land_ref, send_ref, recv_ref, loc_ref = refs[5 * a:5 * a + 5]
            src_own = src_ref if gather else src_ref.at[me]
            pltpu.make_async_copy(src_own, land_ref.at[me], loc_ref).wait()
            for k in peer_ks[a]:
                peer, pidx = peers[k]
                src = src_ref if gather else src_ref.at[pidx]
                cp = pltpu.make_async_remote_copy(
                    src_ref=src, dst_ref=land_ref.at[pidx], send_sem=send_ref.at[k], recv_sem=recv_ref.at[k],
                    device_id=peer, device_id_type=MESH)
                cp.wait_send()
                cp.wait_recv()

    operands, in_specs, out_shape, aliases = [], [], [], {}
    for a, h in enumerate(handles):
        operands += list(h[:5])
        in_specs += [_HBM, _HBM, _SEM, _SEM, _SEM]
        out_shape += [pltpu.HBM(h[0].shape, h[0].dtype), pltpu.HBM(h[1].shape, h[1].dtype)]
        aliases[5 * a] = 2 * a
        aliases[5 * a + 1] = 2 * a + 1
    outs = _pcall(
        body, name=name, out_shape=tuple(out_shape),
        in_specs=in_specs + [pl.BlockSpec(memory_space=pl.ANY)] * len(after),
        out_specs=tuple([_HBM] * (2 * n)), input_output_aliases=aliases,
        compiler_params=pltpu.CompilerParams(has_side_effects=_DATAFLOW),
    )(*operands, *after)
    return [outs[2 * a + 1] for a in range(n)]


def _sibling_forward(lands, name):
    n = len(lands)
    n_fwd = len(_OTHER_CHIPS)

    def body(*refs):
        ins, bufs = refs[:n], refs[n:2 * n]
        send_sems, recv_sems = refs[2 * n:]
        x, y, c = lax.axis_index("x"), lax.axis_index("y"), lax.axis_index("c")
        sibling = (x, y, 1 - c)
        peers = _peers()
        sends = []
        for a in range(n):
            for j, k in enumerate(_OTHER_CHIPS):
                slot = peers[k][1]
                cp = pltpu.make_async_remote_copy(
                    src_ref=ins[a].at[slot], dst_ref=bufs[a].at[slot],
                    send_sem=send_sems.at[a * n_fwd + j], recv_sem=recv_sems.at[a * n_fwd + j],
                    device_id=sibling, device_id_type=MESH)
                cp.start()
                sends.append(cp)
        for a in range(n):
            for j, k in enumerate(_OTHER_CHIPS):
                (px, py, pc), slot = peers[k]
                theirs = 4 * px + 2 * py + (1 - pc)
                pltpu.make_async_remote_copy(
                    src_ref=ins[a].at[slot], dst_ref=bufs[a].at[theirs],
                    send_sem=send_sems.at[a * n_fwd + j], recv_sem=recv_sems.at[a * n_fwd + j],
                    device_id=sibling, device_id_type=MESH).wait_recv()
        for cp in sends:
            cp.wait_send()

    any_spec = pl.BlockSpec(memory_space=pl.ANY)
    outs = _pcall(
        body, name=name, out_shape=[jax.ShapeDtypeStruct(a.shape, a.dtype) for a in lands],
        in_specs=[any_spec] * n, out_specs=[any_spec] * n,
        input_output_aliases={a: a for a in range(n)},
        scratch_shapes=[pltpu.SemaphoreType.DMA((n * n_fwd,)), pltpu.SemaphoreType.DMA((n * n_fwd,))],
        compiler_params=pltpu.CompilerParams(has_side_effects=True),
    )(*lands)
    return list(outs)


_DIMS = {"nn": (((1,), (0,)), ((), ())), "nt": (((1,), (1,)), ((), ())), "tn": (((0,), (0,)), ((), ()))}


def _dot(a, b, mode="nn"):
    return lax.dot_general(a, b, _DIMS[mode], preferred_element_type=F32)


def _mm(a, b, *, mode, grid, a_spec, b_spec, out_shape, out_specs, acc_shape, epilogue, name,
        extra=(), extra_specs=(), after=(), semantics=("parallel", "parallel", "arbitrary")):
    nk = grid[2]
    n_extra = len(extra)
    n_in = 2 + n_extra + len(after)

    def body_single(*refs):
        a_ref, b_ref = refs[0], refs[1]
        epilogue(_dot(a_ref[...], b_ref[...], mode), refs[2:2 + n_extra], refs[n_in:])

    def body_acc(*refs):
        a_ref, b_ref = refs[0], refs[1]
        ex = refs[2:2 + n_extra]
        outs = refs[n_in:-1]
        acc = refs[-1]
        k = pl.program_id(2)

        @pl.when(k == 0)
        def _():
            acc[...] = jnp.zeros_like(acc)

        acc[...] += _dot(a_ref[...], b_ref[...], mode)

        @pl.when(k == nk - 1)
        def _():
            epilogue(acc[...], ex, outs)

    return _pcall(
        body_single if nk == 1 else body_acc, name=name, grid=grid, out_shape=out_shape,
        in_specs=[a_spec, b_spec] + list(extra_specs) + [pl.BlockSpec(memory_space=pl.ANY)] * len(after),
        out_specs=out_specs,
        scratch_shapes=[] if nk == 1 else [pltpu.VMEM(acc_shape, F32)],
        compiler_params=_cparams(semantics),
    )(a, b, *extra, *after)


def _ep_store(dtype):
    def ep(acc, ex, outs):
        outs[0][...] = acc.astype(dtype)
    return ep


def _ep_relu2(acc, ex, outs):
    outs[0][...] = acc.astype(BF16)
    r = jnp.maximum(acc, 0.0)
    outs[1][...] = (r * r).astype(BF16)


def _ep_relu2_bwd(acc, ex, outs):
    u = ex[0][...].astype(F32)
    outs[0][...] = (acc * (2.0 * jnp.maximum(u, 0.0))).astype(BF16)


def _tile(n, want):
    t = min(n, want)
    while n % t:
        t //= 2
    return t


def _mm_nn(a, w, out_dtype, name, tm=2048, tn=1024, tk=1024, epilogue=None, out_dtypes=None, after=()):
    M, K = a.shape
    N = w.shape[1]
    tm, tn, tk = _tile(M, tm), _tile(N, tn), _tile(K, tk)
    out_dtypes = out_dtypes or [out_dtype]
    return _mm(a, w, mode="nn", grid=(M // tm, N // tn, K // tk),
               a_spec=pl.BlockSpec((tm, tk), lambda i, j, k: (i, k)),
               b_spec=pl.BlockSpec((tk, tn), lambda i, j, k: (k, j)),
               out_shape=[jax.ShapeDtypeStruct((M, N), d) for d in out_dtypes],
               out_specs=[pl.BlockSpec((tm, tn), lambda i, j, k: (i, j)) for _ in out_dtypes],
               acc_shape=(tm, tn), epilogue=epilogue or _ep_store(out_dtype), name=name, after=after)


def _mm_nn_blocked(a, wg, name, epilogue, out_dtypes, tm=2048):
    M, K = a.shape
    n = wg.shape[2]
    tm = _tile(M, tm)
    return _mm(a, wg, mode="nn", grid=(M // tm, N_DEV, 1),
               a_spec=pl.BlockSpec((tm, K), lambda i, j, k: (i, 0)),
               b_spec=pl.BlockSpec((None, K, n), lambda i, j, k: (j, 0, 0)),
               out_shape=[jax.ShapeDtypeStruct((M, N_DEV * n), d) for d in out_dtypes],
               out_specs=[pl.BlockSpec((tm, n), lambda i, j, k: (i, j)) for _ in out_dtypes],
               acc_shape=(tm, n), epilogue=epilogue, name=name)


def _mm_nt(a, w, out_dtype, name, tm=2048, tn=1024, tk=1024, epilogue=None, extra=(), extra_specs=(),
           after=()):
    M, K = a.shape
    N = w.shape[0]
    tm, tn, tk = _tile(M, tm), _tile(N, tn), _tile(K, tk)
    if extra and not extra_specs:
        extra_specs = [pl.BlockSpec((tm, tn), lambda i, j, k: (i, j)) for _ in extra]
    return _mm(a, w, mode="nt", grid=(M // tm, N // tn, K // tk),
               a_spec=pl.BlockSpec((tm, tk), lambda i, j, k: (i, k)),
               b_spec=pl.BlockSpec((tn, tk), lambda i, j, k: (j, k)),
               out_shape=[jax.ShapeDtypeStruct((M, N), out_dtype)],
               out_specs=[pl.BlockSpec((tm, tn), lambda i, j, k: (i, j))],
               acc_shape=(tm, tn), epilogue=epilogue or _ep_store(out_dtype), name=name,
               extra=extra, extra_specs=extra_specs, after=after)[0]


def _mm_nt_blocked(a, wg, out_dtype, name, tm=1024, after=()):
    M = a.shape[0]
    kout, n = wg.shape[1], wg.shape[2]
    tm = _tile(M, tm)
    return _mm(a, wg, mode="nt", grid=(M // tm, 1, N_DEV),
               a_spec=pl.BlockSpec((tm, n), lambda i, j, k: (i, k)),
               b_spec=pl.BlockSpec((None, kout, n), lambda i, j, k: (k, 0, 0)),
               out_shape=[jax.ShapeDtypeStruct((M, kout), out_dtype)],
               out_specs=[pl.BlockSpec((tm, kout), lambda i, j, k: (i, 0))],
               acc_shape=(tm, kout), epilogue=_ep_store(out_dtype), name=name, after=after)[0]


def _mm_tn(a, b, out_dtype, name, tm=1024, tn=1024, tk=2048):
    K, M = a.shape
    N = b.shape[1]
    tm, tn, tk = _tile(M, tm), _tile(N, tn), _tile(K, tk)
    return _mm(a, b, mode="tn", grid=(M // tm, N // tn, K // tk),
               a_spec=pl.BlockSpec((tk, tm), lambda i, j, k: (k, i)),
               b_spec=pl.BlockSpec((tk, tn), lambda i, j, k: (k, j)),
               out_shape=[jax.ShapeDtypeStruct((M, N), out_dtype)],
               out_specs=[pl.BlockSpec((tm, tn), lambda i, j, k: (i, j))],
               acc_shape=(tm, tn), epilogue=_ep_store(out_dtype), name=name)[0]


def _mm_tn_blocked(a, b, out_dtype, name, tm=1024, tk=2048):
    K, M = a.shape
    n = b.shape[1] // N_DEV
    tm, tk = _tile(M, tm), _tile(K, tk)
    return _mm(a, b, mode="tn", grid=(M // tm, N_DEV, K // tk),
               a_spec=pl.BlockSpec((tk, tm), lambda i, j, k: (k, i)),
               b_spec=pl.BlockSpec((tk, n), lambda i, j, k: (k, j)),
               out_shape=[jax.ShapeDtypeStruct((N_DEV, M, n), out_dtype)],
               out_specs=[pl.BlockSpec((None, tm, n), lambda i, j, k: (j, i, 0))],
               acc_shape=(tm, n), epilogue=_ep_store(out_dtype), name=name)[0]


def _window_geometry(ws):
    base = [(ws * k // LANES) * LANES for k in range(N_DEV)]
    off = [ws * k - base[k] for k in range(N_DEV)]
    win = -(-(max(off) + ws) // LANES) * LANES
    return base, off, win


def _shards_to_columns(xg, base, off, win, n_out, name, tr=256):
    R, ws = xg.shape[1], xg.shape[2]
    tr = _tile(R, tr)
    nb_win = win // LANES

    def body(x_ref, o_ref, frame_ref):
        written = set()
        frame_ref[...] = jnp.zeros_like(frame_ref)
        for k in range(N_DEV):
            frame_ref[:, 0:ws] = x_ref[k].astype(F32)
            window = frame_ref[...]
            if off[k]:
                window = pltpu.roll(window, off[k], 1)
            for i in range(nb_win):
                b = base[k] // LANES + i
                if b * LANES >= n_out:
                    continue
                cols = slice(b * LANES, (b + 1) * LANES)
                blk = window[:, i * LANES:(i + 1) * LANES]
                if b in written:
                    blk = blk + o_ref[:, cols].astype(F32)
                o_ref[:, cols] = blk.astype(o_ref.dtype)
                written.add(b)
        for b in range(n_out // LANES):
            if b not in written:
                o_ref[:, b * LANES:(b + 1) * LANES] = jnp.zeros((tr, LANES), o_ref.dtype)

    return _pcall(
        body, name=name, grid=(R // tr,), out_shape=jax.ShapeDtypeStruct((R, n_out), xg.dtype),
        in_specs=[pl.BlockSpec((N_DEV, tr, ws), lambda i: (0, i, 0))],
        out_specs=pl.BlockSpec((tr, n_out), lambda i: (i, 0)),
        scratch_shapes=[pltpu.VMEM((tr, win), F32)],
        compiler_params=_cparams(("parallel",)))(xg)


def _columns_to_shards(x, ws, base, off, win, name, tr=256):
    R = x.shape[0]
    tr = _tile(R, tr)

    def body(x_ref, o_ref, frame_ref):
        for k in range(N_DEV):
            window = x_ref[:, base[k]:base[k] + win].astype(F32)
            if off[k]:
                window = pltpu.roll(window, win - off[k], 1)
            frame_ref[...] = window
            o_ref[k] = frame_ref[:, 0:ws].astype(o_ref.dtype)

    return _pcall(
        body, name=name, grid=(R // tr,), out_shape=jax.ShapeDtypeStruct((N_DEV, R, ws), x.dtype),
        in_specs=[pl.BlockSpec((tr, x.shape[1]), lambda i: (i, 0))],
        out_specs=pl.BlockSpec((N_DEV, tr, ws), lambda i: (0, i, 0)),
        scratch_shapes=[pltpu.VMEM((tr, win), F32)],
        compiler_params=_cparams(("parallel",)))(x)


def _sigmoid(x):
    return 1.0 / (1.0 + jnp.exp(-x))


def _row_spec(tm, d):
    return pl.BlockSpec((tm, d), lambda i: (i, 0))


def _vec_spec(d):
    return pl.BlockSpec((1, d), lambda i: (0, 0))


def _norm_mod_fwd(x, y, gate, nw, scale, shift, name, tm=512):
    L, D = x.shape
    tm = _tile(L, tm)
    has_res = y is not None

    def body(*refs):
        if has_res:
            x_ref, y_ref, g_ref, nw_ref, sc_ref, sh_ref, xo_ref, h_ref = refs
            xn = x_ref[...] + g_ref[...] * y_ref[...]
            xo_ref[...] = xn
        else:
            x_ref, nw_ref, sc_ref, sh_ref, h_ref = refs
            xn = x_ref[...]
        rstd = lax.rsqrt(jnp.mean(xn * xn, axis=-1, keepdims=True) + NORM_EPS)
        h = xn * rstd * nw_ref[...] * (1.0 + sc_ref[...]) + sh_ref[...]
        h_ref[...] = h.astype(BF16)

    row, vec = _row_spec(tm, D), _vec_spec(D)
    if has_res:
        ins, in_specs = (x, y, gate, nw, scale, shift), [row, row, vec, vec, vec, vec]
        out_shape = [jax.ShapeDtypeStruct((L, D), F32), jax.ShapeDtypeStruct((L, D), BF16)]
        out_specs = [row, row]
    else:
        ins, in_specs = (x, nw, scale, shift), [row, vec, vec, vec]
        out_shape = [jax.ShapeDtypeStruct((L, D), BF16)]
        out_specs = [row]
    outs = _pcall(body, name=name, grid=(L // tm,), out_shape=out_shape, in_specs=in_specs,
                  out_specs=out_specs, compiler_params=_cparams(("parallel",)))(*ins)
    return outs if has_res else (x, outs[0])


def _gated_branch_bwd(dx, branch, y_ref, g_ref, dy_ref, dg_ref):
    if branch is None:
        return
    dy_ref[...] = (g_ref[...] * dx).astype(BF16)
    dg_ref[...] += jnp.sum(dx * y_ref[...], axis=0, keepdims=True)


def _norm_mod_bwd(dh, x, nw, scale, dres, name, branch=None, tm=512):
    L, D = x.shape
    tm = _tile(L, tm)
    nb = 0 if branch is None else 2

    def body(dh_ref, x_ref, nw_ref, sc_ref, dres_ref, *rest):
        y_ref, g_ref = rest[:nb] if nb else (None, None)
        dx_ref, dsh_ref, dsc_ref, dnw_ref = rest[nb:nb + 4]
        dy_ref, dg_ref = rest[nb + 4:] if nb else (None, None)

        @pl.when(pl.program_id(0) == 0)
        def _():
            dsh_ref[...] = jnp.zeros_like(dsh_ref)
            dsc_ref[...] = jnp.zeros_like(dsc_ref)
            dnw_ref[...] = jnp.zeros_like(dnw_ref)
            if nb:
                dg_ref[...] = jnp.zeros_like(dg_ref)

        xv = x_ref[...]
        dh_v = dh_ref[...]
        nw_v = nw_ref[...]
        rstd = lax.rsqrt(jnp.mean(xv * xv, axis=-1, keepdims=True) + NORM_EPS)
        xhat = xv * rstd
        dsh_ref[...] += jnp.sum(dh_v, axis=0, keepdims=True)
        dsc_ref[...] += jnp.sum(dh_v * (xhat * nw_v), axis=0, keepdims=True)
        dr = dh_v * (1.0 + sc_ref[...])
        dnw_ref[...] += jnp.sum(dr * xhat, axis=0, keepdims=True)
        dxh = dr * nw_v
        dx = rstd * (dxh - xhat * jnp.mean(dxh * xhat, axis=-1, keepdims=True)) + dres_ref[...]
        dx_ref[...] = dx
        _gated_branch_bwd(dx, branch, y_ref, g_ref, dy_ref, dg_ref)

    row, vec = _row_spec(tm, D), _vec_spec(D)
    extra_in = [] if branch is None else list(branch)
    return _pcall(
        body, name=name, grid=(L // tm,),
        out_shape=[jax.ShapeDtypeStruct((L, D), F32)] + [jax.ShapeDtypeStruct((1, D), F32)] * 3
        + ([jax.ShapeDtypeStruct((L, D), BF16), jax.ShapeDtypeStruct((1, D), F32)] if nb else []),
        in_specs=[row, row, vec, vec, row] + ([row, vec] if nb else []),
        out_specs=[row, vec, vec, vec] + ([row, vec] if nb else []),
        compiler_params=_cparams(("arbitrary",)))(dh, x, nw, scale, dres, *extra_in)


def _final_loss(x, y, gate, fw, target, name, tm=512):
    L, D = x.shape
    tm = _tile(L, tm)

    def body(x_ref, y_ref, g_ref, fw_ref, t_ref, dx_ref, loss_ref, dfw_ref, dy_ref, dg_ref):
        @pl.when(pl.program_id(0) == 0)
        def _():
            loss_ref[...] = jnp.zeros_like(loss_ref)
            dfw_ref[...] = jnp.zeros_like(dfw_ref)
            dg_ref[...] = jnp.zeros_like(dg_ref)

        xn = x_ref[...] + g_ref[...] * y_ref[...]
        fw_v = fw_ref[...]
        rstd = lax.rsqrt(jnp.mean(xn * xn, axis=-1, keepdims=True) + NORM_EPS)
        xhat = xn * rstd
        diff = xhat * fw_v - t_ref[...]
        loss_ref[...] += jnp.sum(diff * diff, axis=0, keepdims=True)
        dyf = diff * (1.0 / D)
        dfw_ref[...] += jnp.sum(dyf * xhat, axis=0, keepdims=True)
        dxh = dyf * fw_v
        dx = rstd * (dxh - xhat * jnp.mean(dxh * xhat, axis=-1, keepdims=True))
        dx_ref[...] = dx
        _gated_branch_bwd(dx, True, y_ref, g_ref, dy_ref, dg_ref)

    row, vec = _row_spec(tm, D), _vec_spec(D)
    return _pcall(
        body, name=name, grid=(L // tm,),
        out_shape=[jax.ShapeDtypeStruct((L, D), F32), jax.ShapeDtypeStruct((1, D), F32),
                   jax.ShapeDtypeStruct((1, D), F32), jax.ShapeDtypeStruct((L, D), BF16),
                   jax.ShapeDtypeStruct((1, D), F32)],
        in_specs=[row, row, vec, vec, row], out_specs=[row, vec, vec, row, vec],
        compiler_params=_cparams(("arbitrary",)))(x, y, gate, fw, target)


def _shift_down(v, s, row):
    if s == 0:
        return v
    return jnp.where(row >= s, pltpu.roll(v, s, 0), 0.0)


CONV_ROWS = 32


def _shifted_rows(x_ref, r0, n, lanes=slice(None)):
    cur = x_ref[r0:r0 + CONV_ROWS, lanes]
    if r0 >= n - 1:
        return [cur] + [x_ref[r0 - s:r0 - s + CONV_ROWS, lanes] for s in range(1, n)]
    row = lax.broadcasted_iota(jnp.int32, cur.shape, 0)
    return [_shift_down(cur, s, row) for s in range(n)]


def _ssd_conv_fwd(zx, w, b, col0, width, name, cb=512):
    L = zx.shape[0]
    nb = width // cb
    off = col0 // cb

    def body(x_ref, w_ref, b_ref, o_ref):
        for l0 in range(0, cb, LANES):
            lanes = slice(l0, l0 + LANES)
            taps = [w_ref[k:k + 1, lanes] for k in range(SSD_K)]
            bias = b_ref[:, lanes]
            for r0 in range(0, L, CONV_ROWS):
                taps_in = _shifted_rows(x_ref, r0, SSD_K, lanes)
                acc = bias + taps[SSD_K - 1] * taps_in[0]
                for s in range(1, SSD_K):
                    acc = acc + taps[SSD_K - 1 - s] * taps_in[s]
                o_ref[r0:r0 + CONV_ROWS, lanes] = acc * _sigmoid(acc)

    return _pcall(
        body, name=name, grid=(nb,), out_shape=jax.ShapeDtypeStruct((L, width), F32),
        in_specs=[pl.BlockSpec((L, cb), lambda j: (0, off + j)),
                  pl.BlockSpec((SSD_K, cb), lambda j: (0, j)),
                  pl.BlockSpec((1, cb), lambda j: (0, j))],
        out_specs=pl.BlockSpec((L, cb), lambda j: (0, j)),
        compiler_params=_cparams(("parallel",)))(zx, w, b)


def _ssd_conv_bwd(zx, w, b, d_parts, dzx, col0, name, cb=128):
    L = zx.shape[0]
    widths = [p.shape[1] for p in d_parts]
    width = sum(widths)
    nb = width // cb
    off = col0 // cb
    starts = [sum(widths[:i]) // cb for i in range(len(d_parts))]
    counts = [wd // cb for wd in widths]

    def body(x_ref, w_ref, b_ref, *rest):
        d_refs = rest[:len(d_parts)]
        dx_ref, dw_ref, db_ref, dpre_ref = rest[len(d_parts) + 1:]
        j = pl.program_id(0)
        taps = [w_ref[k:k + 1, :] for k in range(SSD_K)]
        bias = b_ref[...]
        fold = lambda v: sum(v[r:r + 8, :] for r in range(0, CONV_ROWS, 8))
        db8 = jnp.zeros((8, cb), F32)
        dw8 = [jnp.zeros((8, cb), F32) for _ in range(SSD_K)]
        for r0 in range(0, L, CONV_ROWS):
            rows = slice(r0, r0 + CONV_ROWS)
            d_val = d_refs[-1][rows, :]
            for i in range(len(d_parts) - 2, -1, -1):
                d_val = jnp.where(j < starts[i + 1], d_refs[i][rows, :], d_val)
            taps_in = _shifted_rows(x_ref, r0, SSD_K)
            acc = bias + taps[SSD_K - 1] * taps_in[0]
            for s in range(1, SSD_K):
                acc = acc + taps[SSD_K - 1 - s] * taps_in[s]
            sig = _sigmoid(acc)
            dpre = d_val * (sig * (1.0 + acc * (1.0 - sig)))
            dpre_ref[rows, :] = dpre
            db8 = db8 + fold(dpre)
            for s in range(SSD_K):
                dw8[s] = dw8[s] + fold(dpre * taps_in[s])
        dpre_ref[L:L + 8, :] = jnp.zeros((8, cb), F32)
        db_ref[...] = jnp.sum(db8, axis=0, keepdims=True)
        for s in range(SSD_K):
            dw_ref[SSD_K - 1 - s:SSD_K - s, :] = jnp.sum(dw8[s], axis=0, keepdims=True)
        for r0 in range(0, L, CONV_ROWS):
            dx = taps[SSD_K - 1] * dpre_ref[r0:r0 + CONV_ROWS, :]
            for s in range(1, SSD_K):
                dx = dx + taps[SSD_K - 1 - s] * dpre_ref[r0 + s:r0 + s + CONV_ROWS, :]
            dx_ref[r0:r0 + CONV_ROWS, :] = dx.astype(BF16)

    def part_spec(i):
        return pl.BlockSpec((L, cb), lambda j: (0, jnp.clip(j - starts[i], 0, counts[i] - 1)))

    return _pcall(
        body, name=name, grid=(nb,),
        out_shape=[jax.ShapeDtypeStruct(dzx.shape, BF16), jax.ShapeDtypeStruct((SSD_K, width), F32),
                   jax.ShapeDtypeStruct((1, width), F32)],
        in_specs=[pl.BlockSpec((L, cb), lambda j: (0, off + j)),
                  pl.BlockSpec((SSD_K, cb), lambda j: (0, j)),
                  pl.BlockSpec((1, cb), lambda j: (0, j))]
        + [part_spec(i) for i in range(len(d_parts))] + [pl.BlockSpec(memory_space=pl.ANY)],
        out_specs=[pl.BlockSpec((L, cb), lambda j: (0, off + j)),
                   pl.BlockSpec((SSD_K, cb), lambda j: (0, j)),
                   pl.BlockSpec((1, cb), lambda j: (0, j))],
        input_output_aliases={3 + len(d_parts): 0},
        scratch_shapes=[pltpu.VMEM((L + 8, cb), F32)],
        compiler_params=_cparams(("parallel",)))(zx, w, b, *d_parts, dzx)


def _dzx_finish(dzx, ddt, col0, name, tl=512):
    G, L, _ = ddt.shape
    tail = dzx.shape[1] - col0
    tl = _tile(L, tl)

    def body(ddt_ref, dzx_ref, o_ref):
        s = ddt_ref[0]
        for g in range(1, G):
            s = s + ddt_ref[g]
        o_ref[:, 0:LANES] = s.astype(o_ref.dtype)
        if tail > LANES:
            o_ref[:, LANES:] = jnp.zeros((tl, tail - LANES), o_ref.dtype)

    return _pcall(
        body, name=name, grid=(L // tl,), out_shape=jax.ShapeDtypeStruct(dzx.shape, dzx.dtype),
        in_specs=[pl.BlockSpec((G, tl, LANES), lambda i: (0, i, 0)), pl.BlockSpec(memory_space=pl.ANY)],
        out_specs=pl.BlockSpec((tl, tail), lambda i: (i, col0 // tail)),
        input_output_aliases={1: 0},
        compiler_params=_cparams(("parallel",)))(ddt, dzx)


def _sc_conv_fwd(proj, w, name, cb=512):
    L = proj.shape[0]
    width = proj.shape[1] // 3
    nb = width // cb

    def body(b_ref, c_ref, x_ref, w_ref, o_ref):
        for l0 in range(0, cb, LANES):
            lanes = slice(l0, l0 + LANES)
            taps = [w_ref[k:k + 1, lanes] for k in range(SC_K)]
            for r0 in range(0, L, CONV_ROWS):
                rows = slice(r0, r0 + CONV_ROWS)
                q = [c * x for c, x in zip(_shifted_rows(c_ref, r0, SC_K, lanes),
                                           _shifted_rows(x_ref, r0, SC_K, lanes))]
                acc = taps[SC_K - 1] * q[0]
                for s in range(1, SC_K):
                    acc = acc + taps[SC_K - 1 - s] * q[s]
                o_ref[rows, lanes] = (b_ref[rows, lanes] * acc).astype(BF16)

    return _pcall(
        body, name=name, grid=(nb,), out_shape=jax.ShapeDtypeStruct((L, width), BF16),
        in_specs=[pl.BlockSpec((L, cb), lambda j: (0, j)),
                  pl.BlockSpec((L, cb), lambda j: (0, nb + j)),
                  pl.BlockSpec((L, cb), lambda j: (0, 2 * nb + j)),
                  pl.BlockSpec((SC_K, cb), lambda j: (0, j))],
        out_specs=pl.BlockSpec((L, cb), lambda j: (0, j)),
        compiler_params=_cparams(("parallel",)))(proj, proj, proj, w)


def _sc_conv_bwd(proj, w, dy, name, cb=128):
    L = proj.shape[0]
    width = proj.shape[1] // 3
    nb = width // cb

    def body(b_ref, c_ref, x_ref, w_ref, dy_ref, db_ref, dc_ref, dxv_ref, dw_ref, dconv_ref):
        taps = [w_ref[k:k + 1, :] for k in range(SC_K)]
        fold = lambda v: sum(v[r:r + 8, :] for r in range(0, CONV_ROWS, 8))
        dw8 = [jnp.zeros((8, cb), F32) for _ in range(SC_K)]
        for r0 in range(0, L, CONV_ROWS):
            rows = slice(r0, r0 + CONV_ROWS)
            q = [c * x for c, x in zip(_shifted_rows(c_ref, r0, SC_K), _shifted_rows(x_ref, r0, SC_K))]
            conv = taps[SC_K - 1] * q[0]
            for s in range(1, SC_K):
                conv = conv + taps[SC_K - 1 - s] * q[s]
            dyv = dy_ref[rows, :]
            db_ref[rows, :] = (dyv * conv).astype(BF16)
            dconv = dyv * b_ref[rows, :]
            dconv_ref[rows, :] = dconv
            for s in range(SC_K):
                dw8[s] = dw8[s] + fold(dconv * q[s])
        dconv_ref[L:L + 8, :] = jnp.zeros((8, cb), F32)
        for s in range(SC_K):
            dw_ref[SC_K - 1 - s:SC_K - s, :] = jnp.sum(dw8[s], axis=0, keepdims=True)
        for r0 in range(0, L, CONV_ROWS):
            rows = slice(r0, r0 + CONV_ROWS)
            dq = taps[SC_K - 1] * dconv_ref[rows, :]
            for s in range(1, SC_K):
                dq = dq + taps[SC_K - 1 - s] * dconv_ref[r0 + s:r0 + s + CONV_ROWS, :]
            dc_ref[rows, :] = (dq * x_ref[rows, :]).astype(BF16)
            dxv_ref[rows, :] = (dq * c_ref[rows, :]).astype(BF16)

    blk = pl.BlockSpec((L, cb), lambda j: (0, j))
    wblk = pl.BlockSpec((SC_K, cb), lambda j: (0, j))
    return _pcall(
        body, name=name, grid=(nb,),
        out_shape=[jax.ShapeDtypeStruct((L, width), BF16)] * 3 + [jax.ShapeDtypeStruct((SC_K, width), F32)],
        in_specs=[blk, pl.BlockSpec((L, cb), lambda j: (0, nb + j)),
                  pl.BlockSpec((L, cb), lambda j: (0, 2 * nb + j)), wblk, blk],
        out_specs=[blk, blk, blk, wblk], scratch_shapes=[pltpu.VMEM((L + 8, cb), F32)],
        compiler_params=_cparams(("parallel",)))(proj, proj, proj, w, dy)


def _split3(v):
    hi = v.astype(BF16)
    r1 = v - hi.astype(F32)
    mid = r1.astype(BF16)
    lo = (r1 - mid.astype(F32)).astype(BF16)
    return hi, mid, lo


def _dot_exact01(t01, v):
    hi, mid, lo = _split3(v)
    return _dot(t01, hi) + _dot(t01, mid) + _dot(t01, lo)


def _lane_col(v, lane, h):
    return jnp.sum(jnp.where(lane == h, v, 0.0), axis=1, keepdims=True)


def _sum_all(v):
    return jnp.sum(jnp.sum(v, axis=1, keepdims=True), axis=0, keepdims=True)


def _softplus(x):
    return jnp.maximum(x, 0.0) + jnp.log1p(jnp.exp(-jnp.abs(x)))


def _ssd_decay(zx, bias_p, alog_p, n_heads, dt_block, name):
    L = zx.shape[0]
    nc = L // SSD_CHUNK
    per_step = 4 if nc % 4 == 0 else 1
    rows_step = per_step * SSD_CHUNK

    def body(raw_ref, bias_ref, alog_ref, dt_ref, sg_ref, cs_ref, cst_ref, last_ref):
        lane = lax.broadcasted_iota(jnp.int32, (SSD_CHUNK, LANES), 1)
        row = lax.broadcasted_iota(jnp.int32, (SSD_CHUNK, LANES), 0)
        valid = lane < n_heads
        tri = (row >= lane).astype(BF16)
        a_row = -jnp.exp(alog_ref[...])
        for i in range(per_step):
            rows = slice(i * SSD_CHUNK, (i + 1) * SSD_CHUNK)
            raw = raw_ref[rows, :] + bias_ref[...]
            dt = jnp.where(valid, _softplus(raw), 0.0)
            a = dt * a_row
            cs = _dot_exact01(tri, a)
            dt_ref[rows, :] = dt
            sg_ref[rows, :] = _sigmoid(raw)
            cs_ref[rows, :] = cs
            cst_ref[i] = cs.T
            last_ref[i] = jnp.sum(a, axis=0, keepdims=True)

    blk = pl.BlockSpec((rows_step, LANES), lambda c: (c, 0))
    head_vec = pl.BlockSpec((1, LANES), lambda c: (0, 0))
    return _pcall(
        body, name=name, grid=(nc // per_step,),
        out_shape=[jax.ShapeDtypeStruct((L, LANES), F32)] * 3
        + [jax.ShapeDtypeStruct((nc, SSD_CHUNK, LANES), F32), jax.ShapeDtypeStruct((nc, 1, LANES), F32)],
        in_specs=[pl.BlockSpec((rows_step, LANES), lambda c: (c, dt_block)), head_vec, head_vec],
        out_specs=[blk, blk, blk, pl.BlockSpec((per_step, SSD_CHUNK, LANES), lambda c: (c, 0, 0)),
                   pl.BlockSpec((per_step, 1, LANES), lambda c: (c, 0, 0))],
        compiler_params=_cparams(("parallel",)))(zx, bias_p, alog_p)


def _ssd_common(dt_ref, cs_ref, last_ref, b_ref, c_ref):
    c_sz = SSD_CHUNK
    lane = lax.broadcasted_iota(jnp.int32, (c_sz, LANES), 1)
    row = lax.broadcasted_iota(jnp.int32, (c_sz, LANES), 0)
    bb = b_ref[...].astype(BF16)
    cb = c_ref[...].astype(BF16)
    scores = _dot(cb, bb, "nt")
    return dict(lane=lane, row=row, dt=dt_ref[...], cs=cs_ref[...], last_row=last_ref[...], bb=bb, cb=cb,
                scores=scores, causal=row >= lane, lo=lane < SSD_P)


def _pair_terms(q, cst_ref, h0):
    lane, lo = q["lane"], q["lo"]
    out = {}
    cols, dts, lasts, lms = [], [], [], []
    lane1 = lax.broadcasted_iota(jnp.int32, (1, LANES), 1)
    for h in (h0, h0 + 1):
        col = _lane_col(q["cs"], lane, h)
        rowv = cst_ref[pl.ds(h, 1), :]
        lms.append(jnp.exp(jnp.where(q["causal"], col - rowv, -1e30)))
        cols.append(col)
        dts.append(_lane_col(q["dt"], lane, h))
        lasts.append(jnp.sum(jnp.where(lane1 == h, q["last_row"], 0.0), axis=1, keepdims=True))
    out["lm"] = lms
    out["cols"] = cols
    out["lasts"] = lasts
    out["dt_b"] = jnp.where(lo, dts[0], dts[1])
    out["e_b"] = jnp.where(lo, jnp.exp(cols[0]), jnp.exp(cols[1]))
    out["dec_cols"] = [jnp.exp(lasts[0] - cols[0]), jnp.exp(lasts[1] - cols[1])]
    out["dec_b"] = jnp.where(lo, out["dec_cols"][0], out["dec_cols"][1])
    lo1 = lane1 < SSD_P
    out["explast"] = [jnp.exp(lasts[0]), jnp.exp(lasts[1])]
    out["explast_b"] = jnp.where(lo1, out["explast"][0], out["explast"][1])
    return out


def _ssd_fwd(zx, xc, decay, d_lane, nw, d_inner, after, name):
    L = zx.shape[0]
    nc = L // SSD_CHUNK
    gw = d_inner // SSD_G
    heads = gw // SSD_P
    n_pair = heads // 2
    bc0 = d_inner // LANES

    def body(z_ref, xs_ref, b_ref, c_ref, dt_ref, cs_ref, cst_ref, last_ref, dl_ref, nw_ref, after_ref,
             y_ref, yn_ref, prev_ref, s_ref):
        @pl.when(pl.program_id(1) == 0)
        def _():
            s_ref[...] = jnp.zeros_like(s_ref)

        q = _ssd_common(dt_ref, cs_ref, last_ref, b_ref, c_ref)
        prev_ref[...] = s_ref[...]
        lo = q["lo"]
        for j in range(n_pair):
            sl = slice(j * LANES, (j + 1) * LANES)
            p = _pair_terms(q, cst_ref, pl.program_id(0) * heads + 2 * j)
            xs_p = xs_ref[:, sl]
            xp = xs_p * p["dt_b"]
            xb = xp.astype(BF16)
            m_a = (q["scores"] * p["lm"][0]).astype(BF16)
            m_b = (q["scores"] * p["lm"][1]).astype(BF16)
            yd = jnp.where(lo, _dot(m_a, xb), _dot(m_b, xb))
            s_p = s_ref[:, sl]
            yo = _dot(q["cb"], s_p.astype(BF16)) * p["e_b"]
            y_ref[:, sl] = yd + yo + dl_ref[:, sl] * xs_p
            st = _dot(q["bb"], (xp * p["dec_b"]).astype(BF16), "tn")
            s_ref[:, sl] = s_p * p["explast_b"] + st
        yv = y_ref[...]
        zv = z_ref[...]
        yg = yv * (zv * _sigmoid(zv))
        rstd = lax.rsqrt(jnp.mean(yg * yg, axis=-1, keepdims=True) + NORM_EPS)
        yn_ref[...] = (yg * rstd * nw_ref[...]).astype(BF16)

    grp = lambda width: pl.BlockSpec((None, 1, width), lambda g, c: (g, 0, 0))
    dt_, _, cs_, cst_, last_ = decay
    return _pcall(
        body, name=name, grid=(SSD_G, nc),
        out_shape=[jax.ShapeDtypeStruct((L, d_inner), F32), jax.ShapeDtypeStruct((L, d_inner), BF16),
                   jax.ShapeDtypeStruct((nc, SSD_G, SSD_N, gw), F32)],
        in_specs=[pl.BlockSpec((SSD_CHUNK, gw), lambda g, c: (c, g)),
                  pl.BlockSpec((SSD_CHUNK, gw), lambda g, c: (c, g)),
                  pl.BlockSpec((SSD_CHUNK, SSD_N), lambda g, c: (c, bc0 + g)),
                  pl.BlockSpec((SSD_CHUNK, SSD_N), lambda g, c: (c, bc0 + SSD_G + g)),
                  pl.BlockSpec((SSD_CHUNK, LANES), lambda g, c: (c, 0)),
                  pl.BlockSpec((SSD_CHUNK, LANES), lambda g, c: (c, 0)),
                  pl.BlockSpec((None, SSD_CHUNK, LANES), lambda g, c: (c, 0, 0)),
                  pl.BlockSpec((None, 1, LANES), lambda g, c: (c, 0, 0)),
                  grp(gw), grp(gw), pl.BlockSpec(memory_space=pl.ANY)],
        out_specs=[pl.BlockSpec((SSD_CHUNK, gw), lambda g, c: (c, g)),
                   pl.BlockSpec((SSD_CHUNK, gw), lambda g, c: (c, g)),
                   pl.BlockSpec((None, None, SSD_N, gw), lambda g, c: (c, g, 0, 0))],
        scratch_shapes=[pltpu.VMEM((SSD_N, gw), F32)],
        compiler_params=_cparams(("parallel", "arbitrary")))(
            zx, xc, xc, xc, dt_, cs_, cst_, last_, d_lane, nw, after)


def _ssd_bwd(dyn, y, zx, xc, prev, decay, alog_p, d_lane, nw, d_inner, name):
    L = zx.shape[0]
    nc = L // SSD_CHUNK
    gw = d_inner // SSD_G
    heads = gw // SSD_P
    n_pair = heads // 2
    bc0 = d_inner // LANES

    def body(dyn_ref, y_ref, z_ref, xs_ref, b_ref, c_ref, prev_ref, dt_ref, sg_ref, cs_ref, cst_ref, last_ref,
             alog_ref, dl_ref, nw_ref,
             dz_ref, dxs_ref, db_ref, dc_ref, ddt_ref, dbias_ref, dalog_ref, dd_ref, dnw_ref,
             ds_ref, racc_ref):
        @pl.when(pl.program_id(1) == 0)
        def _():
            ds_ref[...] = jnp.zeros_like(ds_ref)
            dbias_ref[...] = jnp.zeros_like(dbias_ref)
            dalog_ref[...] = jnp.zeros_like(dalog_ref)
            dd_ref[...] = jnp.zeros_like(dd_ref)
            dnw_ref[...] = jnp.zeros_like(dnw_ref)

        q = _ssd_common(dt_ref, cs_ref, last_ref, b_ref, c_ref)
        a_row = -jnp.exp(alog_ref[...])
        lane, row, lo = q["lane"], q["row"], q["lo"]
        lane1 = lax.broadcasted_iota(jnp.int32, (1, LANES), 1)
        head0 = pl.program_id(0) * heads
        mine = (lane >= head0) & (lane < head0 + heads)

        yv, zv, dynv, nwv = y_ref[...], z_ref[...], dyn_ref[...], nw_ref[...]
        sig = _sigmoid(zv)
        sz = zv * sig
        yg = yv * sz
        rstd = lax.rsqrt(jnp.mean(yg * yg, axis=-1, keepdims=True) + NORM_EPS)
        yhat = yg * rstd
        dnw_ref[...] += jnp.sum(dynv * yhat, axis=0, keepdims=True)
        dyh = dynv * nwv
        dyg = rstd * (dyh - yhat * jnp.mean(dyh * yhat, axis=-1, keepdims=True))
        dz_ref[...] = (dyg * yv * (sig * (1.0 + zv * (1.0 - sig)))).astype(BF16)
        dy_all = dyg * sz

        dg = jnp.zeros((SSD_CHUNK, SSD_CHUNK), F32)
        dc_acc = jnp.zeros((SSD_CHUNK, SSD_N), F32)
        db_acc = jnp.zeros((SSD_CHUNK, SSD_N), F32)
        dcs_mat = jnp.zeros((SSD_CHUNK, LANES), F32)
        ddt_mat = jnp.zeros((SSD_CHUNK, LANES), F32)
        dd_row = jnp.zeros((1, LANES), F32)
        racc_ref[...] = jnp.zeros_like(racc_ref)
        is_last = row == SSD_CHUNK - 1

        for j in range(n_pair):
            sl = slice(j * LANES, (j + 1) * LANES)
            ha, hb = head0 + 2 * j, head0 + 2 * j + 1
            p = _pair_terms(q, cst_ref, ha)
            xs_p = xs_ref[:, sl]
            dyp = dy_all[:, sl]
            xp = xs_p * p["dt_b"]
            xb = xp.astype(BF16)
            s_p = prev_ref[:, sl]
            s_pb = s_p.astype(BF16)
            dsn = ds_ref[:, sl]
            dsnb = dsn.astype(BF16)
            m_f = [q["scores"] * p["lm"][0], q["scores"] * p["lm"][1]]

            t0 = dyp * xs_p
            dd_row = dd_row + jnp.where(lane1 == ha, _sum_all(jnp.where(lo, t0, 0.0)), 0.0) \
                + jnp.where(lane1 == hb, _sum_all(jnp.where(lo, 0.0, t0)), 0.0)
            dxs_p = dl_ref[:, sl] * dyp

            yo = _dot(q["cb"], s_pb) * p["e_b"]
            dcs_b = (dyp * p["e_b"]).astype(BF16)
            dc_acc = dc_acc + _dot(dcs_b, s_pb, "nt")
            ds_yo = _dot(q["cb"], dcs_b, "tn")
            t1 = dyp * yo
            dcs_cols = [jnp.sum(jnp.where(lo, t1, 0.0), axis=1, keepdims=True),
                        jnp.sum(jnp.where(lo, 0.0, t1), axis=1, keepdims=True)]

            t2 = dsn * s_p
            dlast = [p["explast"][0] * _sum_all(jnp.where(lo, t2, 0.0)),
                     p["explast"][1] * _sum_all(jnp.where(lo, 0.0, t2))]
            ds_ref[:, sl] = dsn * p["explast_b"] + ds_yo
            w = _dot(q["bb"], dsnb)
            db_acc = db_acc + _dot((xp * p["dec_b"]).astype(BF16), dsnb, "nt")
            dxp = w * p["dec_b"]
            t3 = w * xp
            e = [jnp.sum(jnp.where(lo, t3, 0.0), axis=1, keepdims=True) * p["dec_cols"][0],
                 jnp.sum(jnp.where(lo, 0.0, t3), axis=1, keepdims=True) * p["dec_cols"][1]]
            for i in range(2):
                dlast[i] = dlast[i] + jnp.sum(e[i], axis=0, keepdims=True)
                dcs_cols[i] = dcs_cols[i] - e[i]

            dyb = dyp.astype(BF16)
            dy_h = [jnp.where(lo, dyp, 0.0).astype(BF16), jnp.where(lo, 0.0, dyp).astype(BF16)]
            dms = [_dot(dy_h[0], xb, "nt"), _dot(dy_h[1], xb, "nt")]
            dxp = dxp + jnp.where(lo, _dot(m_f[0].astype(BF16), dyb, "tn"), _dot(m_f[1].astype(BF16), dyb, "tn"))
            for i, h in enumerate((ha, hb)):
                dg = dg + dms[i] * p["lm"][i]
                qm = dms[i] * m_f[i]
                dcs_cols[i] = dcs_cols[i] + jnp.sum(qm, axis=1, keepdims=True)
                racc_ref[pl.ds(h, 1), :] = jnp.sum(qm, axis=0, keepdims=True)

            dxs_ref[:, sl] = dxs_p + dxp * p["dt_b"]
            t4 = dxp * xs_p
            ddt_cols = [jnp.sum(jnp.where(lo, t4, 0.0), axis=1, keepdims=True),
                        jnp.sum(jnp.where(lo, 0.0, t4), axis=1, keepdims=True)]
            for i, h in enumerate((ha, hb)):
                sel = lane == h
                dcs_mat = dcs_mat + jnp.where(sel, dcs_cols[i], 0.0) + jnp.where(sel & is_last, dlast[i], 0.0)
                ddt_mat = ddt_mat + jnp.where(sel, ddt_cols[i], 0.0)

        dcs_mat = dcs_mat - racc_ref[...].T
        tri_t = (row <= lane).astype(BF16)
        da = _dot_exact01(tri_t, dcs_mat)
        ddt = ddt_mat + da * a_row
        dalog_ref[...] += jnp.sum(jnp.where(mine, da * q["dt"], 0.0), axis=0, keepdims=True) * a_row
        draw = jnp.where(mine, ddt * sg_ref[...], 0.0)
        ddt_ref[...] = draw
        dbias_ref[...] += jnp.sum(draw, axis=0, keepdims=True)
        dd_ref[...] += dd_row
        dgb = dg.astype(BF16)
        dc_ref[...] = dc_acc + _dot(dgb, q["bb"])
        db_ref[...] = db_acc + _dot(dgb, q["cb"], "tn")

    rev = lambda c: nc - 1 - c
    grp = lambda width: pl.BlockSpec((None, 1, width), lambda g, c: (g, 0, 0))
    blk = lambda width, off: pl.BlockSpec((SSD_CHUNK, width), lambda g, c: (rev(c), off + g))
    head_vec = pl.BlockSpec((1, LANES), lambda g, c: (0, 0))
    chunk_rows = pl.BlockSpec((SSD_CHUNK, LANES), lambda g, c: (rev(c), 0))
    dt_, sg_, cs_, cst_, last_ = decay
    return _pcall(
        body, name=name, grid=(SSD_G, nc),
        out_shape=[jax.ShapeDtypeStruct(zx.shape, BF16), jax.ShapeDtypeStruct((L, d_inner), F32),
                   jax.ShapeDtypeStruct((L, SSD_G * SSD_N), F32), jax.ShapeDtypeStruct((L, SSD_G * SSD_N), F32),
                   jax.ShapeDtypeStruct((SSD_G, L, LANES), F32),
                   jax.ShapeDtypeStruct((SSD_G, 1, LANES), F32), jax.ShapeDtypeStruct((SSD_G, 1, LANES), F32),
                   jax.ShapeDtypeStruct((SSD_G, 1, LANES), F32), jax.ShapeDtypeStruct((SSD_G, 1, gw), F32)],
        in_specs=[blk(gw, 0), blk(gw, 0), blk(gw, 0), blk(gw, 0), blk(SSD_N, bc0), blk(SSD_N, bc0 + SSD_G),
                  pl.BlockSpec((None, None, SSD_N, gw), lambda g, c: (rev(c), g, 0, 0)),
                  chunk_rows, chunk_rows, chunk_rows,
                  pl.BlockSpec((None, SSD_CHUNK, LANES), lambda g, c: (rev(c), 0, 0)),
                  pl.BlockSpec((None, 1, LANES), lambda g, c: (rev(c), 0, 0)),
                  head_vec, grp(gw), grp(gw)],
        out_specs=[blk(gw, 0), blk(gw, 0), blk(SSD_N, 0), blk(SSD_N, 0),
                   pl.BlockSpec((None, SSD_CHUNK, LANES), lambda g, c: (g, rev(c), 0)),
                   grp(LANES), grp(LANES), grp(LANES), grp(gw)],
        scratch_shapes=[pltpu.VMEM((SSD_N, gw), F32), pltpu.VMEM((SSD_CHUNK, LANES), F32)],
        compiler_params=_cparams(("parallel", "arbitrary")))(
            dyn, y, zx, xc, xc, xc, prev, dt_, sg_, cs_, cst_, last_, alog_p, d_lane, nw)


def _cond_mod(c_pad, ada_w, ada_b_loc, after, name):
    depth, D, n = ada_w.shape
    rows = c_pad.shape[0]

    def body(c_ref, w_ref, b_ref, after_ref, mod_ref, cond_ref):
        cv = c_ref[...]
        cond = cv * _sigmoid(cv)
        cond_ref[...] = cond
        mod_ref[...] = _dot(cond.astype(BF16), w_ref[...].astype(BF16)) + b_ref[...]

    return _pcall(
        body, name=name, grid=(depth,),
        out_shape=[jax.ShapeDtypeStruct((depth, rows, n), F32), jax.ShapeDtypeStruct((rows, D), F32)],
        in_specs=[pl.BlockSpec((rows, D), lambda i: (0, 0)),
                  pl.BlockSpec((None, D, n), lambda i: (i, 0, 0)),
                  pl.BlockSpec((None, 1, n), lambda i: (i, 0, 0)),
                  pl.BlockSpec(memory_space=pl.ANY)],
        out_specs=[pl.BlockSpec((None, rows, n), lambda i: (i, 0, 0)),
                   pl.BlockSpec((rows, D), lambda i: (0, 0))],
        compiler_params=_cparams(("arbitrary",)))(c_pad, ada_w, ada_b_loc, after)


def _adamw_math(g, w, m, v):
    m_new = ADAM_B1 * m + (1.0 - ADAM_B1) * g
    v_new = ADAM_B2 * v + (1.0 - ADAM_B2) * (g * g)
    m_hat = m_new / (1.0 - ADAM_B1 ** ADAM_STEP)
    v_hat = v_new / (1.0 - ADAM_B2 ** ADAM_STEP)
    delta = -ADAM_LR * (m_hat / (jnp.sqrt(v_hat) + ADAM_EPS) + ADAM_WD * w)
    return delta, m_new, v_new


def _adamw_sum(parts, w, m, v, layer, name, prev=None, tr=None):
    depth, R, C = w.shape
    tr = _tile(R, tr if tr is not None else (512 if C <= 512 else 256))

    def body(p_ref, w_ref, m_ref, v_ref, *rest):
        g_ref, d_ref, mo_ref, vo_ref = rest[-4:]
        g = p_ref[0].astype(F32)
        for k in range(1, N_DEV):
            g = g + p_ref[k].astype(F32)
        d, mn, vn = _adamw_math(g, w_ref[...], m_ref[...], v_ref[...])
        g_ref[...] = g
        d_ref[...] = d
        mo_ref[...] = mn
        vo_ref[...] = vn

    blk = pl.BlockSpec((None, tr, C), lambda i: (layer, i, 0))
    prev = list(prev) if prev is not None else []
    return _pcall(
        body, name=name, grid=(R // tr,),
        out_shape=[jax.ShapeDtypeStruct((depth, R, C), F32)] * 4,
        in_specs=[pl.BlockSpec((N_DEV, tr, C), lambda i: (0, i, 0)), blk, blk, blk]
        + [pl.BlockSpec(memory_space=pl.ANY)] * len(prev),
        out_specs=[blk] * 4, input_output_aliases={4 + k: k for k in range(len(prev))},
        compiler_params=_cparams(("parallel",)))(parts, w, m, v, *prev)


def _adamw_small(parts, wmv, head_parts, head_wmv, loss_parts, name):
    n, nh = len(parts), len(head_parts)
    n_heads = head_wmv[0][0].shape[1] if nh else 0
    groups = head_parts[0].shape[1] if nh else 0
    d_model = loss_parts.shape[2]

    def body(*refs):
        p_refs, refs = refs[:n], refs[n:]
        wmv_refs, refs = refs[:3 * n], refs[3 * n:]
        hp_refs, refs = refs[:nh], refs[nh:]
        hwmv_refs, refs = refs[:3 * nh], refs[3 * nh:]
        loss_ref, refs = refs[0], refs[1:]
        outs, loss_out, head_scr = refs[:4 * (n + nh)], refs[4 * (n + nh)], refs[4 * (n + nh) + 1]

        def update(i, g, w_ref, m_ref, v_ref):
            res = (g,) + _adamw_math(g, w_ref[...], m_ref[...], v_ref[...])
            for o_ref, r in zip(outs[4 * i:4 * i + 4], res):
                o_ref[...] = r

        for i in range(n):
            g = p_refs[i][0]
            for k in range(1, N_DEV):
                g = g + p_refs[i][k]
            update(i, g, *wmv_refs[3 * i:3 * i + 3])
        for i in range(nh):
            g = None
            for k in range(N_DEV):
                for grp in range(groups):
                    g = hp_refs[i][k, grp] if g is None else g + hp_refs[i][k, grp]
            head_scr[...] = g
            update(n + i, head_scr[:, 0:n_heads], *hwmv_refs[3 * i:3 * i + 3])
        tot = loss_ref[0]
        for k in range(1, N_DEV):
            tot = tot + loss_ref[k]
        loss_out[...] = jnp.broadcast_to(_sum_all(tot) * (0.5 / d_model), loss_out.shape)

    operands = list(parts) + [a for t in wmv for a in t] + list(head_parts) + [a for t in head_wmv for a in t]
    operands.append(loss_parts)
    out_shape = [jax.ShapeDtypeStruct(t[0].shape, F32) for t in list(wmv) + list(head_wmv) for _ in range(4)]
    out_shape.append(jax.ShapeDtypeStruct((1, LANES), F32))
    vmem = pl.BlockSpec(memory_space=pltpu.VMEM)
    outs = _pcall(body, name=name, out_shape=out_shape, in_specs=[vmem] * len(operands),
                  out_specs=[vmem] * len(out_shape), scratch_shapes=[pltpu.VMEM((1, LANES), F32)],
                  compiler_params=_cparams())(*operands)
    return [outs[4 * i:4 * i + 4] for i in range(n + nh)], outs[-1]


def _ada_adamw(cond_pad, dmod_pad, w, m, v, name, tr=512):
    depth, D, n = w.shape
    rows = cond_pad.shape[0]
    tr = _tile(D, tr)

    def body(c_ref, dm_ref, w_ref, m_ref, v_ref, g_ref, d_ref, mo_ref, vo_ref):
        g = _dot(c_ref[...].astype(BF16), dm_ref[...].astype(BF16), "tn")
        d, mn, vn = _adamw_math(g, w_ref[...], m_ref[...], v_ref[...])
        g_ref[...] = g
        d_ref[...] = d
        mo_ref[...] = mn
        vo_ref[...] = vn

    blk = pl.BlockSpec((None, tr, n), lambda i, r: (i, r, 0))
    return _pcall(
        body, name=name, grid=(depth, D // tr),
        out_shape=[jax.ShapeDtypeStruct((depth, D, n), F32)] * 4,
        in_specs=[pl.BlockSpec((rows, tr), lambda i, r: (0, r)),
                  pl.BlockSpec((None, rows, n), lambda i, r: (i, 0, 0)), blk, blk, blk],
        out_specs=[blk] * 4, compiler_params=_cparams(("parallel", "parallel")))(cond_pad, dmod_pad, w, m, v)


def kernel(x, c, ada_w, ada_b, mix_norm_w, mlp_norm_w, mlp_up, mlp_down, ssd_in_w, ssd_conv_w, ssd_conv_b, ssd_dt_bias, ssd_A_log, ssd_D, ssd_norm_w, ssd_out_w, sc_in_w, sc_conv_w, sc_out_w, final_norm_w, loss_target, m_ada_w, m_ada_b, m_mix_norm_w, m_mlp_norm_w, m_mlp_up, m_mlp_down, m_ssd_in_w, m_ssd_conv_w, m_ssd_conv_b, m_ssd_dt_bias, m_ssd_A_log, m_ssd_D, m_ssd_norm_w, m_ssd_out_w, m_sc_in_w, m_sc_conv_w, m_sc_out_w, m_final_norm_w, v_ada_w, v_ada_b, v_mix_norm_w, v_mlp_norm_w, v_mlp_up, v_mlp_down, v_ssd_in_w, v_ssd_conv_w, v_ssd_conv_b, v_ssd_dt_bias, v_ssd_A_log, v_ssd_D, v_ssd_norm_w, v_ssd_out_w, v_sc_in_w, v_sc_conv_w, v_sc_out_w, v_final_norm_w):
    weights = dict(ada_w=ada_w, ada_b=ada_b, mix_norm_w=mix_norm_w, mlp_norm_w=mlp_norm_w, mlp_up=mlp_up,
                   mlp_down=mlp_down, ssd_in_w=ssd_in_w, ssd_conv_w=ssd_conv_w, ssd_conv_b=ssd_conv_b,
                   ssd_dt_bias=ssd_dt_bias, ssd_A_log=ssd_A_log, ssd_D=ssd_D, ssd_norm_w=ssd_norm_w,
                   ssd_out_w=ssd_out_w, sc_in_w=sc_in_w, sc_conv_w=sc_conv_w, sc_out_w=sc_out_w,
                   final_norm_w=final_norm_w)
    moms = dict(ada_w=m_ada_w, ada_b=m_ada_b, mix_norm_w=m_mix_norm_w, mlp_norm_w=m_mlp_norm_w, mlp_up=m_mlp_up,
                mlp_down=m_mlp_down, ssd_in_w=m_ssd_in_w, ssd_conv_w=m_ssd_conv_w, ssd_conv_b=m_ssd_conv_b,
                ssd_dt_bias=m_ssd_dt_bias, ssd_A_log=m_ssd_A_log, ssd_D=m_ssd_D, ssd_norm_w=m_ssd_norm_w,
                ssd_out_w=m_ssd_out_w, sc_in_w=m_sc_in_w, sc_conv_w=m_sc_conv_w, sc_out_w=m_sc_out_w,
                final_norm_w=m_final_norm_w)
    vars_ = dict(ada_w=v_ada_w, ada_b=v_ada_b, mix_norm_w=v_mix_norm_w, mlp_norm_w=v_mlp_norm_w, mlp_up=v_mlp_up,
                 mlp_down=v_mlp_down, ssd_in_w=v_ssd_in_w, ssd_conv_w=v_ssd_conv_w, ssd_conv_b=v_ssd_conv_b,
                 ssd_dt_bias=v_ssd_dt_bias, ssd_A_log=v_ssd_A_log, ssd_D=v_ssd_D, ssd_norm_w=v_ssd_norm_w,
                 ssd_out_w=v_ssd_out_w, sc_in_w=v_sc_in_w, sc_conv_w=v_sc_conv_w, sc_out_w=v_sc_out_w,
                 final_norm_w=v_final_norm_w)
    names = list(weights)

    L, D = x.shape[1], x.shape[2]
    d_inner = 2 * D
    n_heads = d_inner // SSD_P
    hpg = n_heads // SSD_G
    gw = d_inner // SSD_G
    conv_dim = d_inner + 2 * SSD_G * SSD_N
    zx_dim = d_inner + conv_dim
    zx_pad = -(-(zx_dim + LANES) // 512) * 512
    in_ws = ssd_in_w.shape[2]
    in_base, in_off, in_win = _window_geometry(in_ws)
    me = _my_index()
    x0 = x[0]
    tgt = loss_target[0]

    n_mod = ada_w.shape[2]
    (c_all,) = _exchange([c], "gather_c", gather=True)
    gather_handle = {}
    (gather_handle["ssd_in_w"],), token_in = _xfer_start(
        [ssd_in_w[0].astype(BF16)], "gather_start_ssd_in_w", gather=True, via_sibling=(0,), after=(c_all,))
    c_pad = jnp.pad(c_all.reshape(N_DEV, D), ((0, 16 - N_DEV), (0, 0)))
    ada_b_loc = lax.dynamic_slice_in_dim(ada_b, me * n_mod, n_mod, axis=1).reshape(2, 1, n_mod)
    mod_blk, cond_pad = _cond_mod(c_pad, ada_w, ada_b_loc, token_in, "cond_mod")
    gather_order = ["mod", "ssd_conv_w", "sc_conv_w", "ssd_out_w", "up0", "down0", "sc_in_w", "sc_out_w", "up1",
                    "down1"]
    gather_src = dict(mod=mod_blk, ssd_conv_w=ssd_conv_w[0], sc_conv_w=sc_conv_w[0],
                      ssd_out_w=ssd_out_w[0].astype(BF16),
                      up0=mlp_up[0].astype(BF16), down0=mlp_down[0].astype(BF16),
                      sc_in_w=sc_in_w[0].astype(BF16), sc_out_w=sc_out_w[0].astype(BF16),
                      up1=mlp_up[1].astype(BF16), down1=mlp_down[1].astype(BF16))
    handles, gather_token = _xfer_start([gather_src[k] for k in gather_order], "gather_start", gather=True,
                                        via_sibling=tuple(range(3, len(gather_order))))
    gather_handle.update(zip(gather_order, handles))

    def gathered(keys, after, forward):
        tag = "_".join(keys)
        lands = _xfer_wait([gather_handle[k] for k in keys], after, f"gather_wait_{tag}", gather=True)
        return _sibling_forward(lands, f"gather_forward_{tag}") if forward else lands

    def forward_behind(keys, after):
        tag = "_".join(keys)
        lands = _xfer_wait([gather_handle[k] for k in keys], after, f"gather_wait_{tag}", gather=True)
        fwd_handles, token = _sibling_forward_start(lands, f"gather_forward_start_{tag}")
        return (lambda done: _sibling_forward_wait(fwd_handles, done, f"gather_forward_wait_{tag}")), token

    (ssd_in_g,) = gathered(["ssd_in_w"], (gather_token, m_ssd_in_w, v_ssd_in_w), True)
    w_in_all = _shards_to_columns(ssd_in_g, in_base, in_off, in_win, zx_pad, "ssd_in_w_columns")
    (mod_all,) = gathered(["mod"], w_in_all, False)
    mod_mine = lax.dynamic_index_in_dim(mod_all, me, axis=2, keepdims=False)
    mod_mine = jnp.transpose(mod_mine, (1, 0, 2)).reshape(2, 6, 1, D)
    sh_m, sc_m, g_m, sh_f, sc_f, g_f = [[mod_mine[i, k] for i in range(2)] for k in range(6)]

    vec = lambda a: a.reshape(1, -1)
    small = {}

    _, h0 = _norm_mod_fwd(x0, None, None, vec(mix_norm_w[0]), sc_m[0], sh_m[0], "l0_mix_norm")
    cw_all, scw_all = gathered(["ssd_conv_w", "sc_conv_w"], h0, False)
    (zx,) = _mm_nn(h0, w_in_all, F32, "ssd_in_proj", tm=2048, tn=512)
    conv_b0 = vec(ssd_conv_b[0])
    conv_w_full = jnp.transpose(cw_all, (1, 0, 2)).reshape(SSD_K, conv_dim)
    sc_conv_full = jnp.transpose(scw_all, (1, 0, 2)).reshape(SC_K, D)
    xc = _ssd_conv_fwd(zx, conv_w_full, conv_b0, d_inner, conv_dim, "ssd_conv")
    bias_p = jnp.pad(ssd_dt_bias[0], (0, LANES - n_heads)).reshape(1, LANES)
    alog_p = jnp.pad(ssd_A_log[0], (0, LANES - n_heads)).reshape(1, LANES)
    d_lane = jnp.repeat(ssd_D[0], SSD_P).reshape(SSD_G, 1, gw)
    nw_g = ssd_norm_w[0].reshape(SSD_G, 1, gw)
    finish, token = forward_behind(["ssd_out_w"], xc)
    decay = _ssd_decay(zx, bias_p, alog_p, n_heads, zx_dim // LANES, "ssd_decay")
    y_ssd, yn, prev = _ssd_fwd(zx, xc, decay, d_lane, nw_g, d_inner, token, "ssd_scan")
    ups, downs = [None, None], [None, None]
    (ssd_out_g,) = finish(yn)
    w_ssd_out = ssd_out_g.reshape(-1, D)
    finish, token = forward_behind(["up0", "down0"], ssd_out_g)
    (mix0,) = _mm_nn(yn, w_ssd_out, F32, "ssd_out_proj", after=(token,))
    x1, h1 = _norm_mod_fwd(x0, mix0, g_m[0], vec(mlp_norm_w[0]), sc_f[0], sh_f[0], "l0_mlp_norm")
    ups[0], down0_g = finish(h1)
    downs[0] = down0_g.reshape(-1, D)
    u0, s0 = _mm_nn_blocked(h1, ups[0], "l0_mlp_up", _ep_relu2, [BF16, BF16])
    finish, token = forward_behind(["sc_in_w", "sc_out_w", "up1", "down1"], s0)
    (d0,) = _mm_nn(s0, downs[0], F32, "l0_mlp_down", after=(token,))
    x2, h2 = _norm_mod_fwd(x1, d0, g_f[0], vec(mix_norm_w[1]), sc_m[1], sh_m[1], "l1_mix_norm")
    sc_in_g, sc_out_g, ups[1], down1_g = finish(h2)
    w_sc_out, downs[1] = sc_out_g.reshape(-1, D), down1_g.reshape(-1, D)
    (proj,) = _mm_nn_blocked(h2, sc_in_g, "sc_in_proj", _ep_store(F32), [F32])
    yc = _sc_conv_fwd(proj, sc_conv_full, "sc_conv")
    (mix1,) = _mm_nn(yc, w_sc_out, F32, "sc_out_proj")
    x3, h3 = _norm_mod_fwd(x2, mix1, g_m[1], vec(mlp_norm_w[1]), sc_f[1], sh_f[1], "l1_mlp_norm")
    u1, s1 = _mm_nn_blocked(h3, ups[1], "l1_mlp_up", _ep_relu2, [BF16, BF16])
    (d1,) = _mm_nn(s1, downs[1], F32, "l1_mlp_down")

    dx, loss_lane, dfw, dd1, dg = _final_loss(x3, d1, g_f[1], vec(final_norm_w), tgt, "final_loss")
    small["final_norm_w"] = dfw

    dmod = [[None] * 6 for _ in range(2)]
    dmod[1][5] = dg

    def mlp_backward(i, dx_out, dd, x_mid, h_in, u, s, mix, gate):
        du = _mm_nt(dd, downs[i], BF16, f"l{i}_mlp_down_bwd", epilogue=_ep_relu2_bwd, extra=(u,))
        gdown = _mm_tn(s, dd, BF16, f"l{i}_mlp_down_wgrad").reshape(N_DEV, -1, D)
        gup = _mm_tn_blocked(h_in, du, BF16, f"l{i}_mlp_up_wgrad")
        dh = _mm_nt_blocked(du, ups[i], F32, f"l{i}_mlp_up_bwd")
        dxm, dsh, dsc, dnw, dmix, dgate = _norm_mod_bwd(dh, x_mid, vec(mlp_norm_w[i]), sc_f[i], dx_out,
                                                        f"l{i}_mlp_norm_bwd", branch=(mix, gate))
        dmod[i][3], dmod[i][4], dmod[i][2] = dsh, dsc, dgate
        return dxm, dmix, dnw, gdown, gup

    grad_handle = {}
    dx3, dyc, dnw_mlp1, gdown1, gup1 = mlp_backward(1, dx, dd1, x3, h3, u1, s1, mix1, g_m[1])
    g_sc_out = _mm_tn(yc, dyc, BF16, "sc_out_wgrad").reshape(N_DEV, -1, D)
    dconv_out = _mm_nt(dyc, w_sc_out, F32, "sc_out_bwd")
    dbg, dcg, dxv, dscw = _sc_conv_bwd(proj, sc_conv_full, dconv_out, "sc_conv_bwd")
    dproj = jnp.concatenate([dbg, dcg, dxv], axis=1)
    g_sc_in = _mm_tn_blocked(h2, dproj, BF16, "sc_in_wgrad")
    early = ["mlp_down1", "mlp_up1", "sc_out_w0", "sc_in_w0"]
    handles, token = _xfer_start([gdown1, gup1, g_sc_out, g_sc_in], "l1_grads_start", gather=False)
    grad_handle.update(zip(early, handles))
    dh2 = _mm_nt_blocked(dproj, sc_in_g, F32, "sc_in_bwd", after=(token,))
    dx2, dsh, dsc, dnw_mix1, dd0, dg = _norm_mod_bwd(dh2, x2, vec(mix_norm_w[1]), sc_m[1], dx3, "l1_mix_norm_bwd",
                                                     branch=(d0, g_f[0]))
    dmod[1][0], dmod[1][1], dmod[0][5] = dsh, dsc, dg
    dx1, dyo, dnw_mlp0, gdown0, gup0 = mlp_backward(0, dx2, dd0, x1, h1, u0, s0, mix0, g_m[0])
    g_ssd_out = _mm_tn(yn, dyo, BF16, "ssd_out_wgrad").reshape(N_DEV, -1, D)
    handles, token = _xfer_start([gdown0, gup0, g_ssd_out], "l0_grads_start", gather=False)
    grad_handle.update(zip(["mlp_down0", "mlp_up0", "ssd_out_w0"], handles))
    early += ["mlp_down0", "mlp_up0", "ssd_out_w0"]
    dyn = _mm_nt(dyo, w_ssd_out, F32, "ssd_out_bwd", after=(token,))
    dz, dxs, db_, dc_, ddt, dbias, dalog, dd_, dnw_ssd = _ssd_bwd(
        dyn, y_ssd, zx, xc, prev, decay, alog_p, d_lane, nw_g, d_inner, "ssd_scan_bwd")
    dzx, dcw, dcb = _ssd_conv_bwd(zx, conv_w_full, conv_b0, [dxs, db_, dc_], dz, d_inner, "ssd_conv_bwd")
    dzx = _dzx_finish(dzx, ddt, zx_dim, "ssd_dzx_finish")
    g_in_all = _mm_tn(h0, dzx, BF16, "ssd_in_wgrad", tn=512, tk=2048)
    g_ssd_in = _columns_to_shards(g_in_all, in_ws, in_base, in_off, in_win, "ssd_in_wgrad_shards")
    (grad_handle["ssd_in_w0"],), token = _xfer_start([g_ssd_in], "ssd_in_grad_start", gather=False)
    dh0 = _mm_nt(dzx, w_in_all, F32, "ssd_in_bwd", tm=1024, tk=dzx.shape[1] // 2, after=(token,))
    grad_x, dsh, dsc, dnw_mix0 = _norm_mod_bwd(dh0, x0, vec(mix_norm_w[0]), sc_m[0], dx1, "l0_mix_norm_bwd")
    dmod[0][0], dmod[0][1] = dsh, dsc

    small["ada_b"] = jnp.concatenate([jnp.concatenate(dmod[i], axis=1) for i in range(2)], axis=0)
    small["mix_norm_w"] = jnp.concatenate([dnw_mix0, dnw_mix1], axis=0)
    small["mlp_norm_w"] = jnp.concatenate([dnw_mlp0, dnw_mlp1], axis=0)
    small["ssd_conv_w"] = dcw
    small["ssd_conv_b"] = dcb
    small["ssd_norm_w"] = dnw_ssd.reshape(1, d_inner)
    small["sc_conv_w"] = dscw
    small["loss"] = loss_lane
    small_names = list(small)
    head_names = ["ssd_dt_bias", "ssd_A_log", "ssd_D"]
    small_handles, small_token = _xfer_start([small[k] for k in small_names] + [dbias, dalog, dd_],
                                             "small_grads_start", gather=True)

    out_g, out_d, out_m, out_v = {}, {}, {}, {}

    layer_res = {}

    def big_update(key, after):
        name, i = key[:-1], int(key[-1])
        (parts,) = _xfer_wait([grad_handle[key]], after, f"grads_wait_{key}", gather=False)
        res = _adamw_sum(parts, weights[name], moms[name], vars_[name], i, f"adamw_{name}_{i}",
                         prev=layer_res.get(name))
        layer_res[name] = res
        return res[1]

    chain = small_token
    for key in early + ["ssd_in_w0"]:
        chain = big_update(key, chain)
    gathered_small = _xfer_wait(small_handles, chain, "small_grads_wait", gather=True)
    small_all = dict(zip(small_names + head_names, gathered_small))

    dmod_loc = lax.dynamic_slice_in_dim(small_all["ada_b"], me * n_mod, n_mod, axis=2)
    dmod_pad = jnp.pad(jnp.transpose(dmod_loc, (1, 0, 2)), ((0, 0), (0, 16 - N_DEV), (0, 0)))
    out_g["ada_w"], out_d["ada_w"], out_m["ada_w"], out_v["ada_w"] = _ada_adamw(
        cond_pad, dmod_pad, ada_w, m_ada_w, v_ada_w, "adamw_ada_w")

    for k in ("ssd_conv_w", "sc_conv_w"):
        n_loc = weights[k].shape[2]
        small_all[k] = lax.dynamic_slice_in_dim(small_all[k], me * n_loc, n_loc, axis=2)
    plain = [k for k in small_names if k != "loss"]
    as2d = lambda a: a.reshape(-1, a.shape[-1])
    res, loss_row = _adamw_small(
        [small_all[k] for k in plain], [tuple(as2d(d[k]) for d in (weights, moms, vars_)) for k in plain],
        [small_all[k] for k in head_names], [tuple(as2d(d[k]) for d in (weights, moms, vars_)) for k in head_names],
        small_all["loss"], "adamw_small")
    loss = loss_row[0, 0]
    for k, res4 in zip(plain + head_names, res):
        for r, dst in zip(res4, (out_g, out_d, out_m, out_v)):
            dst[k] = r.reshape(weights[k].shape)
    for name, res4 in layer_res.items():
        for r, dst in zip(res4, (out_g, out_d, out_m, out_v)):
            dst[name] = r

    return (loss, grad_x[None], *[out_g[k] for k in names], *[out_d[k] for k in names],
            *[out_m[k] for k in names], *[out_v[k] for k in names])
```

```python
import jax
import jax.numpy as jnp
from jax import lax
from jax.experimental import pallas as pl
from jax.experimental.pallas import tpu as pltpu

F32 = jnp.float32
BF16 = jnp.bfloat16
N_DEV = 8
MESH = pl.DeviceIdType.MESH

NORM_EPS = 1e-5
SSD_G = 4
SSD_P = 64
SSD_N = 128
SSD_CHUNK = 128
SSD_K = 4
SC_K = 3
LANES = 128

ADAM_LR = 0.001
ADAM_B1 = 0.9
ADAM_B2 = 0.999
ADAM_EPS = 1e-08
ADAM_WD = 0.01
ADAM_STEP = 10

VMEM_LIMIT = 56 * 1024 * 1024


def _pcall(body, **kw):
    return pl.pallas_call(body, **kw)


def _cparams(sem=None):
    if sem is None:
        return pltpu.CompilerParams(vmem_limit_bytes=VMEM_LIMIT)
    return pltpu.CompilerParams(dimension_semantics=sem, vmem_limit_bytes=VMEM_LIMIT)


def _my_index():
    return 4 * lax.axis_index("x") + 2 * lax.axis_index("y") + lax.axis_index("c")


_PEER_MASKS = [(0, 0, 1), (0, 1, 0), (0, 1, 1), (1, 0, 0), (1, 0, 1), (1, 1, 0), (1, 1, 1)]


def _peers():
    x, y, c = lax.axis_index("x"), lax.axis_index("y"), lax.axis_index("c")
    out = []
    for mx, my, mc in _PEER_MASKS:
        px = (1 - x) if mx else x
        py = (1 - y) if my else y
        pc = (1 - c) if mc else c
        out.append(((px, py, pc), 4 * px + 2 * py + pc))
    return out


def _exchange(arrs, name, gather):
    n = len(arrs)
    n_peer = N_DEV - 1

    def body(*refs):
        ins, outs = refs[:n], refs[n:2 * n]
        send_sems, recv_sems, local_sems = refs[2 * n:]
        me = _my_index()
        peers = _peers()
        started = []
        for a in range(n):
            src_own = ins[a] if gather else ins[a].at[me]
            own = pltpu.make_async_copy(src_own, outs[a].at[me], local_sems.at[a])
            own.start()
            started.append(own)
        sends = []
        for a in range(n):
            for k, (peer, pidx) in enumerate(peers):
                src = ins[a] if gather else ins[a].at[pidx]
                cp = pltpu.make_async_remote_copy(
                    src_ref=src, dst_ref=outs[a].at[me],
                    send_sem=send_sems.at[a * n_peer + k], recv_sem=recv_sems.at[a * n_peer + k],
                    device_id=peer, device_id_type=MESH)
                cp.start()
                sends.append(cp)
        for a in range(n):
            for k, (peer, pidx) in enumerate(peers):
                src = ins[a] if gather else ins[a].at[pidx]
                pltpu.make_async_remote_copy(
                    src_ref=src, dst_ref=outs[a].at[pidx],
                    send_sem=send_sems.at[a * n_peer + k], recv_sem=recv_sems.at[a * n_peer + k],
                    device_id=peer, device_id_type=MESH).wait_recv()
        for cp in sends:
            cp.wait_send()
        for own in started:
            own.wait()

    if gather:
        out_shape = [jax.ShapeDtypeStruct((N_DEV,) + a.shape, a.dtype) for a in arrs]
    else:
        out_shape = [jax.ShapeDtypeStruct(a.shape, a.dtype) for a in arrs]
    any_spec = pl.BlockSpec(memory_space=pl.ANY)
    outs = _pcall(
        body, name=name, out_shape=out_shape,
        in_specs=[any_spec] * n, out_specs=[any_spec] * n,
        scratch_shapes=[pltpu.SemaphoreType.DMA((n * n_peer,)), pltpu.SemaphoreType.DMA((n * n_peer,)),
                        pltpu.SemaphoreType.DMA((n,))],
        compiler_params=pltpu.CompilerParams(has_side_effects=True),
    )(*arrs)
    return list(outs)


def _sibling_forward_start(lands, name):
    n = len(lands)
    n_fwd = len(_OTHER_CHIPS)

    def body(*refs):
        ins, bufs = refs[:n], refs[3 * n:4 * n]
        token = refs[-1]
        sibling = (lax.axis_index("x"), lax.axis_index("y"), 1 - lax.axis_index("c"))
        peers = _peers()
        for a in range(n):
            send_sems, recv_sems = refs[n + 2 * a], refs[n + 2 * a + 1]
            for j, k in enumerate(_OTHER_CHIPS):
                slot = peers[k][1]
                pltpu.make_async_remote_copy(
                    src_ref=ins[a].at[slot], dst_ref=bufs[a].at[slot], send_sem=send_sems.at[j],
                    recv_sem=recv_sems.at[j], device_id=sibling, device_id_type=MESH).start()
        token[...] = jnp.zeros_like(token)

    out_shape, out_specs = [], []
    for _ in range(n):
        out_shape += [pltpu.SemaphoreType.DMA((n_fwd,)), pltpu.SemaphoreType.DMA((n_fwd,))]
        out_specs += [_SEM, _SEM]
    out_shape += [pltpu.HBM(a.shape, a.dtype) for a in lands] + [jax.ShapeDtypeStruct((8, LANES), F32)]
    out_specs += [_HBM] * n + [pl.BlockSpec(memory_space=pltpu.VMEM)]
    outs = _pcall(
        body, name=name, out_shape=tuple(out_shape), in_specs=[_HBM] * n, out_specs=tuple(out_specs),
        input_output_aliases={a: 2 * n + a for a in range(n)},
        compiler_params=pltpu.CompilerParams(has_side_effects=_DATAFLOW),
    )(*[pltpu.with_memory_space_constraint(a, pltpu.HBM) for a in lands])
    return [(outs[2 * n + a], outs[2 * a], outs[2 * a + 1]) for a in range(n)], outs[-1]


def _sibling_forward_wait(handles, after, name):
    n = len(handles)

    def body(*refs):
        sibling = (lax.axis_index("x"), lax.axis_index("y"), 1 - lax.axis_index("c"))
        peers = _peers()
        for a in range(n):
            buf, send_sems, recv_sems = refs[3 * a:3 * a + 3]
            for j, k in enumerate(_OTHER_CHIPS):
                (px, py, pc), slot = peers[k]
                theirs = 4 * px + 2 * py + (1 - pc)
                cp = pltpu.make_async_remote_copy(
                    src_ref=buf.at[slot], dst_ref=buf.at[theirs], send_sem=send_sems.at[j],
                    recv_sem=recv_sems.at[j], device_id=sibling, device_id_type=MESH)
                cp.wait_send()
                cp.wait_recv()

    operands, in_specs = [], []
    for h in handles:
        operands += list(h)
        in_specs += [_HBM, _SEM, _SEM]
    outs = _pcall(
        body, name=name, out_shape=tuple(pltpu.HBM(h[0].shape, h[0].dtype) for h in handles),
        in_specs=in_specs + [pl.BlockSpec(memory_space=pl.ANY)], out_specs=tuple([_HBM] * n),
        input_output_aliases={3 * a: a for a in range(n)},
        compiler_params=pltpu.CompilerParams(has_side_effects=_DATAFLOW),
    )(*operands, after)
    return list(outs)


_HBM = pl.BlockSpec(memory_space=pltpu.HBM)
_SEM = pl.BlockSpec(memory_space=pltpu.SEMAPHORE)
_DATAFLOW = pltpu.SideEffectType.DATAFLOW_SIDE_EFFECTING


_ALL_PEERS = tuple(range(N_DEV - 1))
_SAME_CORE_PEERS = (0, 1, 3, 5)
_OTHER_CHIPS = (1, 3, 5)


def _xfer_start(arrs, name, gather, via_sibling=(), after=(), by_chip=False):
    n = len(arrs)
    n_peer = N_DEV - 1
    n_after = len(after)
    peer_ks = [_OTHER_CHIPS if by_chip else _SAME_CORE_PEERS if a in via_sibling else _ALL_PEERS
               for a in range(n)]
    n_slot = N_DEV // 2 if by_chip else N_DEV

    def body(*refs):
        ins, lands = refs[:n], refs[n:2 * n]
        sems = refs[2 * n + n_after:5 * n + n_after]
        token = refs[-1]
        me = _my_index() // (N_DEV // n_slot)
        peers = [(peer, pidx // (N_DEV // n_slot)) for peer, pidx in _peers()]
        for a in range(n):
            send_sems, recv_sems, loc_sem = sems[3 * a:3 * a + 3]
            src_own = ins[a] if gather else ins[a].at[me]
            pltpu.make_async_copy(src_own, lands[a].at[me], loc_sem).start()
            for k in peer_ks[a]:
                peer, pidx = peers[k]
                src = ins[a] if gather else ins[a].at[pidx]
                pltpu.make_async_remote_copy(
                    src_ref=src, dst_ref=lands[a].at[me], send_sem=send_sems.at[k], recv_sem=recv_sems.at[k],
                    device_id=peer, device_id_type=MESH).start()
        token[...] = jnp.zeros_like(token)

    land_shapes = [((N_DEV,) + a.shape) if gather else a.shape for a in arrs]
    out_shape, out_specs = [], []
    for _ in range(n):
        out_shape += [pltpu.SemaphoreType.DMA((n_peer,)), pltpu.SemaphoreType.DMA((n_peer,)),
                      pltpu.SemaphoreType.DMA(())]
        out_specs += [_SEM, _SEM, _SEM]
    out_shape += [pltpu.HBM(a.shape, a.dtype) for a in arrs]
    out_shape += [pltpu.HBM(s, a.dtype) for s, a in zip(land_shapes, arrs)]
    out_shape += [jax.ShapeDtypeStruct((8, LANES), F32)]
    out_specs += [_HBM] * (2 * n) + [pl.BlockSpec(memory_space=pltpu.VMEM)]
    aliases = {}
    for a in range(n):
        aliases[a] = 3 * n + a
        aliases[n + a] = 4 * n + a
    operands = [pltpu.with_memory_space_constraint(a, pltpu.HBM) for a in arrs]
    operands += [pltpu.with_memory_space_constraint(lax.empty(s, a.dtype), pltpu.HBM)
                 for s, a in zip(land_shapes, arrs)]
    outs = _pcall(
        body, name=name, out_shape=tuple(out_shape),
        in_specs=[_HBM] * (2 * n) + [pl.BlockSpec(memory_space=pl.ANY)] * n_after, out_specs=tuple(out_specs),
        input_output_aliases=aliases,
        compiler_params=pltpu.CompilerParams(has_side_effects=_DATAFLOW),
    )(*operands, *after)
    handles = []
    for a in range(n):
        handles.append((outs[3 * n + a], outs[4 * n + a], outs[3 * a], outs[3 * a + 1], outs[3 * a + 2],
                        peer_ks[a], n_slot))
    return handles, outs[-1]


def _xfer_wait(handles, after, name, gather):
    n = len(handles)
    after = tuple(after) if isinstance(after, (tuple, list)) else (after,)
    peer_ks = [h[5] for h in handles]

    def body(*refs):
        for a in range(n):
            per_slot = N_DEV // handles[a][6]
            me = _my_index() // per_slot
            peers = [(peer, pidx // per_slot) for peer, pidx in _peers()]
            src_ref, land_ref, send_ref, recv_ref, loc_ref = refs[5 * a:5 * a + 5]
            src_own = src_ref if gather else src_ref.at[me]
            pltpu.make_async_copy(src_own, land_ref.at[me], loc_ref).wait()
            for k in peer_ks[a]:
                peer, pidx = peers[k]
                src = src_ref if gather else src_ref.at[pidx]
                cp = pltpu.make_async_remote_copy(
                    src_ref=src, dst_ref=land_ref.at[pidx], send_sem=send_ref.at[k], recv_sem=recv_ref.at[k],
                    device_id=peer, device_id_type=MESH)
                cp.wait_send()
                cp.wait_recv()

    operands, in_specs, out_shape, aliases = [], [], [], {}
    for a, h in enumerate(handles):
        operands += list(h[:5])
        in_specs += [_HBM, _HBM, _SEM, _SEM, _SEM]
        out_shape += [pltpu.HBM(h[0].shape, h[0].dtype), pltpu.HBM(h[1].shape, h[1].dtype)]
        aliases[5 * a] = 2 * a
        aliases[5 * a + 1] = 2 * a + 1
    outs = _pcall(
        body, name=name, out_shape=tuple(out_shape),
        in_specs=in_specs + [pl.BlockSpec(memory_space=pl.ANY)] * len(after),
        out_specs=tuple([_HBM] * (2 * n)), input_output_aliases=aliases,
        compiler_params=pltpu.CompilerParams(has_side_effects=_DATAFLOW),
    )(*operands, *after)
    return [outs[2 * a + 1] for a in range(n)]


def _sibling_reduce(g, name, tr=256):
    n_chip = N_DEV // 2
    _, R, C = g.shape

    def swap(g_ref, own_ref, got_ref, send_sems, recv_sems, loc_sems):
        c = lax.axis_index("c")
        sibling = (lax.axis_index("x"), lax.axis_index("y"), 1 - c)
        copies = []
        for j in range(n_chip):
            own = pltpu.make_async_copy(g_ref.at[2 * j + c], own_ref.at[j], loc_sems.at[j])
            cp = pltpu.make_async_remote_copy(
                src_ref=g_ref.at[2 * j + 1 - c], dst_ref=got_ref.at[j], send_sem=send_sems.at[j],
                recv_sem=recv_sems.at[j], device_id=sibling, device_id_type=MESH)
            own.start()
            cp.start()
            copies.append((own, cp))
        for own, cp in copies:
            cp.wait_recv()
            cp.wait_send()
            own.wait()

    any_spec = pl.BlockSpec(memory_space=pl.ANY)
    part = jax.ShapeDtypeStruct((n_chip, R, C), g.dtype)
    own4, got4 = _pcall(
        swap, name=name, out_shape=[part, part], in_specs=[any_spec], out_specs=[any_spec, any_spec],
        scratch_shapes=[pltpu.SemaphoreType.DMA((n_chip,))] * 3,
        compiler_params=pltpu.CompilerParams(has_side_effects=True))(g)

    tr = _tile(R, tr)

    def add(a_ref, b_ref, o_ref):
        o_ref[...] = (a_ref[...].astype(F32) + b_ref[...].astype(F32)).astype(o_ref.dtype)

    blk = pl.BlockSpec((n_chip, tr, C), lambda i: (0, i, 0))
    return _pcall(add, name=name + "_add", grid=(R // tr,), out_shape=part, in_specs=[blk, blk], out_specs=blk,
                  compiler_params=_cparams(("parallel",)))(own4, got4)


def _sibling_forward(lands, name):
    n = len(lands)
    n_fwd = len(_OTHER_CHIPS)

    def body(*refs):
        ins, bufs = refs[:n], refs[n:2 * n]
        send_sems, recv_sems = refs[2 * n:]
        x, y, c = lax.axis_index("x"), lax.axis_index("y"), lax.axis_index("c")
        sibling = (x, y, 1 - c)
        peers = _peers()
        sends = []
        for a in range(n):
            for j, k in enumerate(_OTHER_CHIPS):
                slot = peers[k][1]
                cp = pltpu.make_async_remote_copy(
                    src_ref=ins[a].at[slot], dst_ref=bufs[a].at[slot],
                    send_sem=send_sems.at[a * n_fwd + j], recv_sem=recv_sems.at[a * n_fwd + j],
                    device_id=sibling, device_id_type=MESH)
                cp.start()
                sends.append(cp)
        for a in range(n):
            for j, k in enumerate(_OTHER_CHIPS):
                (px, py, pc), slot = peers[k]
                theirs = 4 * px + 2 * py + (1 - pc)
                pltpu.make_async_remote_copy(
                    src_ref=ins[a].at[slot], dst_ref=bufs[a].at[theirs],
                    send_sem=send_sems.at[a * n_fwd + j], recv_sem=recv_sems.at[a * n_fwd + j],
                    device_id=sibling, device_id_type=MESH).wait_recv()
        for cp in sends:
            cp.wait_send()

    any_spec = pl.BlockSpec(memory_space=pl.ANY)
    outs = _pcall(
        body, name=name, out_shape=[jax.ShapeDtypeStruct(a.shape, a.dtype) for a in lands],
        in_specs=[any_spec] * n, out_specs=[any_spec] * n,
        input_output_aliases={a: a for a in range(n)},
        scratch_shapes=[pltpu.SemaphoreType.DMA((n * n_fwd,)), pltpu.SemaphoreType.DMA((n * n_fwd,))],
        compiler_params=pltpu.CompilerParams(has_side_effects=True),
    )(*lands)
    return list(outs)


_DIMS = {"nn": (((1,), (0,)), ((), ())), "nt": (((1,), (1,)), ((), ())), "tn": (((0,), (0,)), ((), ()))}


def _dot(a, b, mode="nn"):
    return lax.dot_general(a, b, _DIMS[mode], preferred_element_type=F32)


def _mm(a, b, *, mode, grid, a_spec, b_spec, out_shape, out_specs, acc_shape, epilogue, name,
        extra=(), extra_specs=(), after=(), semantics=("parallel", "parallel", "arbitrary")):
    nk = grid[2]
    n_extra = len(extra)
    n_in = 2 + n_extra + len(after)

    def body_single(*refs):
        a_ref, b_ref = refs[0], refs[1]
        epilogue(_dot(a_ref[...], b_ref[...], mode), refs[2:2 + n_extra], refs[n_in:])

    def body_acc(*refs):
        a_ref, b_ref = refs[0], refs[1]
        ex = refs[2:2 + n_extra]
        outs = refs[n_in:-1]
        acc = refs[-1]
        k = pl.program_id(2)

        @pl.when(k == 0)
        def _():
            acc[...] = jnp.zeros_like(acc)

        acc[...] += _dot(a_ref[...], b_ref[...], mode)

        @pl.when(k == nk - 1)
        def _():
            epilogue(acc[...], ex, outs)

    return _pcall(
        body_single if nk == 1 else body_acc, name=name, grid=grid, out_shape=out_shape,
        in_specs=[a_spec, b_spec] + list(extra_specs) + [pl.BlockSpec(memory_space=pl.ANY)] * len(after),
        out_specs=out_specs,
        scratch_shapes=[] if nk == 1 else [pltpu.VMEM(acc_shape, F32)],
        compiler_params=_cparams(semantics),
    )(a, b, *extra, *after)


def _ep_store(dtype):
    def ep(acc, ex, outs):
        outs[0][...] = acc.astype(dtype)
    return ep


def _ep_relu2(acc, ex, outs):
    outs[0][...] = acc.astype(BF16)
    r = jnp.maximum(acc, 0.0)
    outs[1][...] = (r * r).astype(BF16)


def _ep_relu2_bwd(acc, ex, outs):
    u = ex[0][...].astype(F32)
    outs[0][...] = (acc * (2.0 * jnp.maximum(u, 0.0))).astype(BF16)


def _tile(n, want):
    t = min(n, want)
    while n % t:
        t //= 2
    return t


def _mm_nn(a, w, out_dtype, name, tm=2048, tn=1024, tk=1024, epilogue=None, out_dtypes=None, after=()):
    M, K = a.shape
    N = w.shape[1]
    tm, tn, tk = _tile(M, tm), _tile(N, tn), _tile(K, tk)
    out_dtypes = out_dtypes or [out_dtype]
    return _mm(a, w, mode="nn", grid=(M // tm, N // tn, K // tk),
               a_spec=pl.BlockSpec((tm, tk), lambda i, j, k: (i, k)),
               b_spec=pl.BlockSpec((tk, tn), lambda i, j, k: (k, j)),
               out_shape=[jax.ShapeDtypeStruct((M, N), d) for d in out_dtypes],
               out_specs=[pl.BlockSpec((tm, tn), lambda i, j, k: (i, j)) for _ in out_dtypes],
               acc_shape=(tm, tn), epilogue=epilogue or _ep_store(out_dtype), name=name, after=after)


def _mm_nn_blocked(a, wg, name, epilogue, out_dtypes, tm=2048):
    M, K = a.shape
    n = wg.shape[2]
    tm = _tile(M, tm)
    return _mm(a, wg, mode="nn", grid=(M // tm, N_DEV, 1),
               a_spec=pl.BlockSpec((tm, K), lambda i, j, k: (i, 0)),
               b_spec=pl.BlockSpec((None, K, n), lambda i, j, k: (j, 0, 0)),
               out_shape=[jax.ShapeDtypeStruct((M, N_DEV * n), d) for d in out_dtypes],
               out_specs=[pl.BlockSpec((tm, n), lambda i, j, k: (i, j)) for _ in out_dtypes],
               acc_shape=(tm, n), epilogue=epilogue, name=name)


def _mm_nt(a, w, out_dtype, name, tm=2048, tn=1024, tk=1024, epilogue=None, extra=(), extra_specs=(),
           after=()):
    M, K = a.shape
    N = w.shape[0]
    tm, tn, tk = _tile(M, tm), _tile(N, tn), _tile(K, tk)
    if extra and not extra_specs:
        extra_specs = [pl.BlockSpec((tm, tn), lambda i, j, k: (i, j)) for _ in extra]
    return _mm(a, w, mode="nt", grid=(M // tm, N // tn, K // tk),
               a_spec=pl.BlockSpec((tm, tk), lambda i, j, k: (i, k)),
               b_spec=pl.BlockSpec((tn, tk), lambda i, j, k: (j, k)),
               out_shape=[jax.ShapeDtypeStruct((M, N), out_dtype)],
               out_specs=[pl.BlockSpec((tm, tn), lambda i, j, k: (i, j))],
               acc_shape=(tm, tn), epilogue=epilogue or _ep_store(out_dtype), name=name,
               extra=extra, extra_specs=extra_specs, after=after)[0]


def _mm_nt_blocked(a, wg, out_dtype, name, tm=1024, after=()):
    M = a.shape[0]
    kout, n = wg.shape[1], wg.shape[2]
    tm = _tile(M, tm)
    return _mm(a, wg, mode="nt", grid=(M // tm, 1, N_DEV),
               a_spec=pl.BlockSpec((tm, n), lambda i, j, k: (i, k)),
               b_spec=pl.BlockSpec((None, kout, n), lambda i, j, k: (k, 0, 0)),
               out_shape=[jax.ShapeDtypeStruct((M, kout), out_dtype)],
               out_specs=[pl.BlockSpec((tm, kout), lambda i, j, k: (i, 0))],
               acc_shape=(tm, kout), epilogue=_ep_store(out_dtype), name=name, after=after)[0]


def _mm_tn(a, b, out_dtype, name, tm=1024, tn=1024, tk=2048):
    K, M = a.shape
    N = b.shape[1]
    tm, tn, tk = _tile(M, tm), _tile(N, tn), _tile(K, tk)
    return _mm(a, b, mode="tn", grid=(M // tm, N // tn, K // tk),
               a_spec=pl.BlockSpec((tk, tm), lambda i, j, k: (k, i)),
               b_spec=pl.BlockSpec((tk, tn), lambda i, j, k: (k, j)),
               out_shape=[jax.ShapeDtypeStruct((M, N), out_dtype)],
               out_specs=[pl.BlockSpec((tm, tn), lambda i, j, k: (i, j))],
               acc_shape=(tm, tn), epilogue=_ep_store(out_dtype), name=name)[0]


def _mm_tn_blocked(a, b, out_dtype, name, tm=1024, tk=2048):
    K, M = a.shape
    n = b.shape[1] // N_DEV
    tm, tk = _tile(M, tm), _tile(K, tk)
    return _mm(a, b, mode="tn", grid=(M // tm, N_DEV, K // tk),
               a_spec=pl.BlockSpec((tk, tm), lambda i, j, k: (k, i)),
               b_spec=pl.BlockSpec((tk, n), lambda i, j, k: (k, j)),
               out_shape=[jax.ShapeDtypeStruct((N_DEV, M, n), out_dtype)],
               out_specs=[pl.BlockSpec((None, tm, n), lambda i, j, k: (j, i, 0))],
               acc_shape=(tm, n), epilogue=_ep_store(out_dtype), name=name)[0]


def _window_geometry(ws):
    base = [(ws * k // LANES) * LANES for k in range(N_DEV)]
    off = [ws * k - base[k] for k in range(N_DEV)]
    win = -(-(max(off) + ws) // LANES) * LANES
    return base, off, win


def _shards_to_columns(xg, base, off, win, n_out, name, tr=256):
    R, ws = xg.shape[1], xg.shape[2]
    tr = _tile(R, tr)
    nb_win = win // LANES

    def body(x_ref, o_ref, frame_ref):
        written = set()
        frame_ref[...] = jnp.zeros_like(frame_ref)
        for k in range(N_DEV):
            frame_ref[:, 0:ws] = x_ref[k].astype(F32)
            window = frame_ref[...]
            if off[k]:
                window = pltpu.roll(window, off[k], 1)
            for i in range(nb_win):
                b = base[k] // LANES + i
                if b * LANES >= n_out:
                    continue
                cols = slice(b * LANES, (b + 1) * LANES)
                blk = window[:, i * LANES:(i + 1) * LANES]
                if b in written:
                    blk = blk + o_ref[:, cols].astype(F32)
                o_ref[:, cols] = blk.astype(o_ref.dtype)
                written.add(b)
        for b in range(n_out // LANES):
            if b not in written:
                o_ref[:, b * LANES:(b + 1) * LANES] = jnp.zeros((tr, LANES), o_ref.dtype)

    return _pcall(
        body, name=name, grid=(R // tr,), out_shape=jax.ShapeDtypeStruct((R, n_out), xg.dtype),
        in_specs=[pl.BlockSpec((N_DEV, tr, ws), lambda i: (0, i, 0))],
        out_specs=pl.BlockSpec((tr, n_out), lambda i: (i, 0)),
        scratch_shapes=[pltpu.VMEM((tr, win), F32)],
        compiler_params=_cparams(("parallel",)))(xg)


def _columns_to_shards(x, ws, base, off, win, name, tr=256):
    R = x.shape[0]
    tr = _tile(R, tr)

    def body(x_ref, o_ref, frame_ref):
        for k in range(N_DEV):
            window = x_ref[:, base[k]:base[k] + win].astype(F32)
            if off[k]:
                window = pltpu.roll(window, win - off[k], 1)
            frame_ref[...] = window
            o_ref[k] = frame_ref[:, 0:ws].astype(o_ref.dtype)

    return _pcall(
        body, name=name, grid=(R // tr,), out_shape=jax.ShapeDtypeStruct((N_DEV, R, ws), x.dtype),
        in_specs=[pl.BlockSpec((tr, x.shape[1]), lambda i: (i, 0))],
        out_specs=pl.BlockSpec((N_DEV, tr, ws), lambda i: (0, i, 0)),
        scratch_shapes=[pltpu.VMEM((tr, win), F32)],
        compiler_params=_cparams(("parallel",)))(x)


def _sigmoid(x):
    return 1.0 / (1.0 + jnp.exp(-x))


def _row_spec(tm, d):
    return pl.BlockSpec((tm, d), lambda i: (i, 0))


def _vec_spec(d):
    return pl.BlockSpec((1, d), lambda i: (0, 0))


def _norm_mod_fwd(x, y, gate, nw, scale, shift, name, tm=512):
    L, D = x.shape
    tm = _tile(L, tm)
    has_res = y is not None

    def body(*refs):
        if has_res:
            x_ref, y_ref, g_ref, nw_ref, sc_ref, sh_ref, xo_ref, h_ref = refs
            xn = x_ref[...] + g_ref[...] * y_ref[...]
            xo_ref[...] = xn
        else:
            x_ref, nw_ref, sc_ref, sh_ref, h_ref = refs
            xn = x_ref[...]
        rstd = lax.rsqrt(jnp.mean(xn * xn, axis=-1, keepdims=True) + NORM_EPS)
        h = xn * rstd * nw_ref[...] * (1.0 + sc_ref[...]) + sh_ref[...]
        h_ref[...] = h.astype(BF16)

    row, vec = _row_spec(tm, D), _vec_spec(D)
    if has_res:
        ins, in_specs = (x, y, gate, nw, scale, shift), [row, row, vec, vec, vec, vec]
        out_shape = [jax.ShapeDtypeStruct((L, D), F32), jax.ShapeDtypeStruct((L, D), BF16)]
        out_specs = [row, row]
    else:
        ins, in_specs = (x, nw, scale, shift), [row, vec, vec, vec]
        out_shape = [jax.ShapeDtypeStruct((L, D), BF16)]
        out_specs = [row]
    outs = _pcall(body, name=name, grid=(L // tm,), out_shape=out_shape, in_specs=in_specs,
                  out_specs=out_specs, compiler_params=_cparams(("parallel",)))(*ins)
    return outs if has_res else (x, outs[0])


def _gated_branch_bwd(dx, branch, y_ref, g_ref, dy_ref, dg_ref):
    if branch is None:
        return
    dy_ref[...] = (g_ref[...] * dx).astype(BF16)
    dg_ref[...] += jnp.sum(dx * y_ref[...], axis=0, keepdims=True)


def _norm_mod_bwd(dh, x, nw, scale, dres, name, branch=None, tm=512):
    L, D = x.shape
    tm = _tile(L, tm)
    nb = 0 if branch is None else 2

    def body(dh_ref, x_ref, nw_ref, sc_ref, dres_ref, *rest):
        y_ref, g_ref = rest[:nb] if nb else (None, None)
        dx_ref, dsh_ref, dsc_ref, dnw_ref = rest[nb:nb + 4]
        dy_ref, dg_ref = rest[nb + 4:] if nb else (None, None)

        @pl.when(pl.program_id(0) == 0)
        def _():
            dsh_ref[...] = jnp.zeros_like(dsh_ref)
            dsc_ref[...] = jnp.zeros_like(dsc_ref)
            dnw_ref[...] = jnp.zeros_like(dnw_ref)
            if nb:
                dg_ref[...] = jnp.zeros_like(dg_ref)

        xv = x_ref[...]
        dh_v = dh_ref[...]
        nw_v = nw_ref[...]
        rstd = lax.rsqrt(jnp.mean(xv * xv, axis=-1, keepdims=True) + NORM_EPS)
        xhat = xv * rstd
        dsh_ref[...] += jnp.sum(dh_v, axis=0, keepdims=True)
        dsc_ref[...] += jnp.sum(dh_v * (xhat * nw_v), axis=0, keepdims=True)
        dr = dh_v * (1.0 + sc_ref[...])
        dnw_ref[...] += jnp.sum(dr * xhat, axis=0, keepdims=True)
        dxh = dr * nw_v
        dx = rstd * (dxh - xhat * jnp.mean(dxh * xhat, axis=-1, keepdims=True)) + dres_ref[...]
        dx_ref[...] = dx
        _gated_branch_bwd(dx, branch, y_ref, g_ref, dy_ref, dg_ref)

    row, vec = _row_spec(tm, D), _vec_spec(D)
    extra_in = [] if branch is None else list(branch)
    return _pcall(
        body, name=name, grid=(L // tm,),
        out_shape=[jax.ShapeDtypeStruct((L, D), F32)] + [jax.ShapeDtypeStruct((1, D), F32)] * 3
        + ([jax.ShapeDtypeStruct((L, D), BF16), jax.ShapeDtypeStruct((1, D), F32)] if nb else []),
        in_specs=[row, row, vec, vec, row] + ([row, vec] if nb else []),
        out_specs=[row, vec, vec, vec] + ([row, vec] if nb else []),
        compiler_params=_cparams(("arbitrary",)))(dh, x, nw, scale, dres, *extra_in)


def _final_loss(x, y, gate, fw, target, name, tm=512):
    L, D = x.shape
    tm = _tile(L, tm)

    def body(x_ref, y_ref, g_ref, fw_ref, t_ref, dx_ref, loss_ref, dfw_ref, dy_ref, dg_ref):
        @pl.when(pl.program_id(0) == 0)
        def _():
            loss_ref[...] = jnp.zeros_like(loss_ref)
            dfw_ref[...] = jnp.zeros_like(dfw_ref)
            dg_ref[...] = jnp.zeros_like(dg_ref)

        xn = x_ref[...] + g_ref[...] * y_ref[...]
        fw_v = fw_ref[...]
        rstd = lax.rsqrt(jnp.mean(xn * xn, axis=-1, keepdims=True) + NORM_EPS)
        xhat = xn * rstd
        diff = xhat * fw_v - t_ref[...]
        loss_ref[...] += jnp.sum(diff * diff, axis=0, keepdims=True)
        dyf = diff * (1.0 / D)
        dfw_ref[...] += jnp.sum(dyf * xhat, axis=0, keepdims=True)
        dxh = dyf * fw_v
        dx = rstd * (dxh - xhat * jnp.mean(dxh * xhat, axis=-1, keepdims=True))
        dx_ref[...] = dx
        _gated_branch_bwd(dx, True, y_ref, g_ref, dy_ref, dg_ref)

    row, vec = _row_spec(tm, D), _vec_spec(D)
    return _pcall(
        body, name=name, grid=(L // tm,),
        out_shape=[jax.ShapeDtypeStruct((L, D), F32), jax.ShapeDtypeStruct((1, D), F32),
                   jax.ShapeDtypeStruct((1, D), F32), jax.ShapeDtypeStruct((L, D), BF16),
                   jax.ShapeDtypeStruct((1, D), F32)],
        in_specs=[row, row, vec, vec, row], out_specs=[row, vec, vec, row, vec],
        compiler_params=_cparams(("arbitrary",)))(x, y, gate, fw, target)


def _shift_down(v, s, row):
    if s == 0:
        return v
    return jnp.where(row >= s, pltpu.roll(v, s, 0), 0.0)


CONV_ROWS = 32


def _shifted_rows(x_ref, r0, n, lanes=slice(None)):
    cur = x_ref[r0:r0 + CONV_ROWS, lanes]
    if r0 >= n - 1:
        return [cur] + [x_ref[r0 - s:r0 - s + CONV_ROWS, lanes] for s in range(1, n)]
    row = lax.broadcasted_iota(jnp.int32, cur.shape, 0)
    return [_shift_down(cur, s, row) for s in range(n)]


def _ssd_conv_fwd(zx, w, b, col0, width, name, cb=512):
    L = zx.shape[0]
    nb = width // cb
    off = col0 // cb

    def body(x_ref, w_ref, b_ref, o_ref):
        for l0 in range(0, cb, LANES):
            lanes = slice(l0, l0 + LANES)
            taps = [w_ref[k:k + 1, lanes] for k in range(SSD_K)]
            bias = b_ref[:, lanes]
            for r0 in range(0, L, CONV_ROWS):
                taps_in = _shifted_rows(x_ref, r0, SSD_K, lanes)
                acc = bias + taps[SSD_K - 1] * taps_in[0]
                for s in range(1, SSD_K):
                    acc = acc + taps[SSD_K - 1 - s] * taps_in[s]
                o_ref[r0:r0 + CONV_ROWS, lanes] = acc * _sigmoid(acc)

    return _pcall(
        body, name=name, grid=(nb,), out_shape=jax.ShapeDtypeStruct((L, width), F32),
        in_specs=[pl.BlockSpec((L, cb), lambda j: (0, off + j)),
                  pl.BlockSpec((SSD_K, cb), lambda j: (0, j)),
                  pl.BlockSpec((1, cb), lambda j: (0, j))],
        out_specs=pl.BlockSpec((L, cb), lambda j: (0, j)),
        compiler_params=_cparams(("parallel",)))(zx, w, b)


def _ssd_conv_bwd(zx, w, b, d_parts, dzx, col0, name, cb=128):
    L = zx.shape[0]
    widths = [p.shape[1] for p in d_parts]
    width = sum(widths)
    nb = width // cb
    off = col0 // cb
    starts = [sum(widths[:i]) // cb for i in range(len(d_parts))]
    counts = [wd // cb for wd in widths]

    def body(x_ref, w_ref, b_ref, *rest):
        d_refs = rest[:len(d_parts)]
        dx_ref, dw_ref, db_ref, dpre_ref = rest[len(d_parts) + 1:]
        j = pl.program_id(0)
        taps = [w_ref[k:k + 1, :] for k in range(SSD_K)]
        bias = b_ref[...]
        fold = lambda v: sum(v[r:r + 8, :] for r in range(0, CONV_ROWS, 8))
        db8 = jnp.zeros((8, cb), F32)
        dw8 = [jnp.zeros((8, cb), F32) for _ in range(SSD_K)]
        for r0 in range(0, L, CONV_ROWS):
            rows = slice(r0, r0 + CONV_ROWS)
            d_val = d_refs[-1][rows, :]
            for i in range(len(d_parts) - 2, -1, -1):
                d_val = jnp.where(j < starts[i + 1], d_refs[i][rows, :], d_val)
            taps_in = _shifted_rows(x_ref, r0, SSD_K)
            acc = bias + taps[SSD_K - 1] * taps_in[0]
            for s in range(1, SSD_K):
                acc = acc + taps[SSD_K - 1 - s] * taps_in[s]
            sig = _sigmoid(acc)
            dpre = d_val * (sig * (1.0 + acc * (1.0 - sig)))
            dpre_ref[rows, :] = dpre
            db8 = db8 + fold(dpre)
            for s in range(SSD_K):
                dw8[s] = dw8[s] + fold(dpre * taps_in[s])
        dpre_ref[L:L + 8, :] = jnp.zeros((8, cb), F32)
        db_ref[...] = jnp.sum(db8, axis=0, keepdims=True)
        for s in range(SSD_K):
            dw_ref[SSD_K - 1 - s:SSD_K - s, :] = jnp.sum(dw8[s], axis=0, keepdims=True)
        for r0 in range(0, L, CONV_ROWS):
            dx = taps[SSD_K - 1] * dpre_ref[r0:r0 + CONV_ROWS, :]
            for s in range(1, SSD_K):
                dx = dx + taps[SSD_K - 1 - s] * dpre_ref[r0 + s:r0 + s + CONV_ROWS, :]
            dx_ref[r0:r0 + CONV_ROWS, :] = dx.astype(BF16)

    def part_spec(i):
        return pl.BlockSpec((L, cb), lambda j: (0, jnp.clip(j - starts[i], 0, counts[i] - 1)))

    return _pcall(
        body, name=name, grid=(nb,),
        out_shape=[jax.ShapeDtypeStruct(dzx.shape, BF16), jax.ShapeDtypeStruct((SSD_K, width), F32),
                   jax.ShapeDtypeStruct((1, width), F32)],
        in_specs=[pl.BlockSpec((L, cb), lambda j: (0, off + j)),
                  pl.BlockSpec((SSD_K, cb), lambda j: (0, j)),
                  pl.BlockSpec((1, cb), lambda j: (0, j))]
        + [part_spec(i) for i in range(len(d_parts))] + [pl.BlockSpec(memory_space=pl.ANY)],
        out_specs=[pl.BlockSpec((L, cb), lambda j: (0, off + j)),
                   pl.BlockSpec((SSD_K, cb), lambda j: (0, j)),
                   pl.BlockSpec((1, cb), lambda j: (0, j))],
        input_output_aliases={3 + len(d_parts): 0},
        scratch_shapes=[pltpu.VMEM((L + 8, cb), F32)],
        compiler_params=_cparams(("parallel",)))(zx, w, b, *d_parts, dzx)


def _dzx_finish(dzx, ddt, col0, name, tl=512):
    G, L, _ = ddt.shape
    tail = dzx.shape[1] - col0
    tl = _tile(L, tl)

    def body(ddt_ref, dzx_ref, o_ref):
        s = ddt_ref[0]
        for g in range(1, G):
            s = s + ddt_ref[g]
        o_ref[:, 0:LANES] = s.astype(o_ref.dtype)
        if tail > LANES:
            o_ref[:, LANES:] = jnp.zeros((tl, tail - LANES), o_ref.dtype)

    return _pcall(
        body, name=name, grid=(L // tl,), out_shape=jax.ShapeDtypeStruct(dzx.shape, dzx.dtype),
        in_specs=[pl.BlockSpec((G, tl, LANES), lambda i: (0, i, 0)), pl.BlockSpec(memory_space=pl.ANY)],
        out_specs=pl.BlockSpec((tl, tail), lambda i: (i, col0 // tail)),
        input_output_aliases={1: 0},
        compiler_params=_cparams(("parallel",)))(ddt, dzx)


def _sc_conv_fwd(proj, w, name, cb=512):
    L = proj.shape[0]
    width = proj.shape[1] // 3
    nb = width // cb

    def body(b_ref, c_ref, x_ref, w_ref, o_ref):
        for l0 in range(0, cb, LANES):
            lanes = slice(l0, l0 + LANES)
            taps = [w_ref[k:k + 1, lanes] for k in range(SC_K)]
            for r0 in range(0, L, CONV_ROWS):
                rows = slice(r0, r0 + CONV_ROWS)
                q = [c * x for c, x in zip(_shifted_rows(c_ref, r0, SC_K, lanes),
                                           _shifted_rows(x_ref, r0, SC_K, lanes))]
                acc = taps[SC_K - 1] * q[0]
                for s in range(1, SC_K):
                    acc = acc + taps[SC_K - 1 - s] * q[s]
                o_ref[rows, lanes] = (b_ref[rows, lanes] * acc).astype(BF16)

    return _pcall(
        body, name=name, grid=(nb,), out_shape=jax.ShapeDtypeStruct((L, width), BF16),
        in_specs=[pl.BlockSpec((L, cb), lambda j: (0, j)),
                  pl.BlockSpec((L, cb), lambda j: (0, nb + j)),
                  pl.BlockSpec((L, cb), lambda j: (0, 2 * nb + j)),
                  pl.BlockSpec((SC_K, cb), lambda j: (0, j))],
        out_specs=pl.BlockSpec((L, cb), lambda j: (0, j)),
        compiler_params=_cparams(("parallel",)))(proj, proj, proj, w)


def _sc_conv_bwd(proj, w, dy, name, cb=128):
    L = proj.shape[0]
    width = proj.shape[1] // 3
    nb = width // cb

    def body(b_ref, c_ref, x_ref, w_ref, dy_ref, db_ref, dc_ref, dxv_ref, dw_ref, dconv_ref):
        taps = [w_ref[k:k + 1, :] for k in range(SC_K)]
        fold = lambda v: sum(v[r:r + 8, :] for r in range(0, CONV_ROWS, 8))
        dw8 = [jnp.zeros((8, cb), F32) for _ in range(SC_K)]
        for r0 in range(0, L, CONV_ROWS):
            rows = slice(r0, r0 + CONV_ROWS)
            q = [c * x for c, x in zip(_shifted_rows(c_ref, r0, SC_K), _shifted_rows(x_ref, r0, SC_K))]
            conv = taps[SC_K - 1] * q[0]
            for s in range(1, SC_K):
                conv = conv + taps[SC_K - 1 - s] * q[s]
            dyv = dy_ref[rows, :]
            db_ref[rows, :] = (dyv * conv).astype(BF16)
            dconv = dyv * b_ref[rows, :]
            dconv_ref[rows, :] = dconv
            for s in range(SC_K):
                dw8[s] = dw8[s] + fold(dconv * q[s])
        dconv_ref[L:L + 8, :] = jnp.zeros((8, cb), F32)
        for s in range(SC_K):
            dw_ref[SC_K - 1 - s:SC_K - s, :] = jnp.sum(dw8[s], axis=0, keepdims=True)
        for r0 in range(0, L, CONV_ROWS):
            rows = slice(r0, r0 + CONV_ROWS)
            dq = taps[SC_K - 1] * dconv_ref[rows, :]
            for s in range(1, SC_K):
                dq = dq + taps[SC_K - 1 - s] * dconv_ref[r0 + s:r0 + s + CONV_ROWS, :]
            dc_ref[rows, :] = (dq * x_ref[rows, :]).astype(BF16)
            dxv_ref[rows, :] = (dq * c_ref[rows, :]).astype(BF16)

    blk = pl.BlockSpec((L, cb), lambda j: (0, j))
    wblk = pl.BlockSpec((SC_K, cb), lambda j: (0, j))
    return _pcall(
        body, name=name, grid=(nb,),
        out_shape=[jax.ShapeDtypeStruct((L, width), BF16)] * 3 + [jax.ShapeDtypeStruct((SC_K, width), F32)],
        in_specs=[blk, pl.BlockSpec((L, cb), lambda j: (0, nb + j)),
                  pl.BlockSpec((L, cb), lambda j: (0, 2 * nb + j)), wblk, blk],
        out_specs=[blk, blk, blk, wblk], scratch_shapes=[pltpu.VMEM((L + 8, cb), F32)],
        compiler_params=_cparams(("parallel",)))(proj, proj, proj, w, dy)


def _split3(v):
    hi = v.astype(BF16)
    r1 = v - hi.astype(F32)
    mid = r1.astype(BF16)
    lo = (r1 - mid.astype(F32)).astype(BF16)
    return hi, mid, lo


def _dot_exact01(t01, v):
    hi, mid, lo = _split3(v)
    return _dot(t01, hi) + _dot(t01, mid) + _dot(t01, lo)


def _lane_col(v, lane, h):
    return jnp.sum(jnp.where(lane == h, v, 0.0), axis=1, keepdims=True)


def _sum_all(v):
    return jnp.sum(jnp.sum(v, axis=1, keepdims=True), axis=0, keepdims=True)


def _softplus(x):
    return jnp.maximum(x, 0.0) + jnp.log1p(jnp.exp(-jnp.abs(x)))


def _ssd_decay(zx, bias_p, alog_p, n_heads, dt_block, name):
    L = zx.shape[0]
    nc = L // SSD_CHUNK
    per_step = 4 if nc % 4 == 0 else 1
    rows_step = per_step * SSD_CHUNK

    def body(raw_ref, bias_ref, alog_ref, dt_ref, sg_ref, cs_ref, cst_ref, last_ref):
        lane = lax.broadcasted_iota(jnp.int32, (SSD_CHUNK, LANES), 1)
        row = lax.broadcasted_iota(jnp.int32, (SSD_CHUNK, LANES), 0)
        valid = lane < n_heads
        tri = (row >= lane).astype(BF16)
        a_row = -jnp.exp(alog_ref[...])
        for i in range(per_step):
            rows = slice(i * SSD_CHUNK, (i + 1) * SSD_CHUNK)
            raw = raw_ref[rows, :] + bias_ref[...]
            dt = jnp.where(valid, _softplus(raw), 0.0)
            a = dt * a_row
            cs = _dot_exact01(tri, a)
            dt_ref[rows, :] = dt
            sg_ref[rows, :] = _sigmoid(raw)
            cs_ref[rows, :] = cs
            cst_ref[i] = cs.T
            last_ref[i] = jnp.sum(a, axis=0, keepdims=True)

    blk = pl.BlockSpec((rows_step, LANES), lambda c: (c, 0))
    head_vec = pl.BlockSpec((1, LANES), lambda c: (0, 0))
    return _pcall(
        body, name=name, grid=(nc // per_step,),
        out_shape=[jax.ShapeDtypeStruct((L, LANES), F32)] * 3
        + [jax.ShapeDtypeStruct((nc, SSD_CHUNK, LANES), F32), jax.ShapeDtypeStruct((nc, 1, LANES), F32)],
        in_specs=[pl.BlockSpec((rows_step, LANES), lambda c: (c, dt_block)), head_vec, head_vec],
        out_specs=[blk, blk, blk, pl.BlockSpec((per_step, SSD_CHUNK, LANES), lambda c: (c, 0, 0)),
                   pl.BlockSpec((per_step, 1, LANES), lambda c: (c, 0, 0))],
        compiler_params=_cparams(("parallel",)))(zx, bias_p, alog_p)


def _ssd_common(dt_ref, cs_ref, last_ref, b_ref, c_ref):
    c_sz = SSD_CHUNK
    lane = lax.broadcasted_iota(jnp.int32, (c_sz, LANES), 1)
    row = lax.broadcasted_iota(jnp.int32, (c_sz, LANES), 0)
    bb = b_ref[...].astype(BF16)
    cb = c_ref[...].astype(BF16)
    scores = _dot(cb, bb, "nt")
    return dict(lane=lane, row=row, dt=dt_ref[...], cs=cs_ref[...], last_row=last_ref[...], bb=bb, cb=cb,
                scores=scores, causal=row >= lane, lo=lane < SSD_P)


def _pair_terms(q, cst_ref, h0):
    lane, lo = q["lane"], q["lo"]
    out = {}
    cols, dts, lasts, lms = [], [], [], []
    lane1 = lax.broadcasted_iota(jnp.int32, (1, LANES), 1)
    for h in (h0, h0 + 1):
        col = _lane_col(q["cs"], lane, h)
        rowv = cst_ref[pl.ds(h, 1), :]
        lms.append(jnp.exp(jnp.where(q["causal"], col - rowv, -1e30)))
        cols.append(col)
        dts.append(_lane_col(q["dt"], lane, h))
        lasts.append(jnp.sum(jnp.where(lane1 == h, q["last_row"], 0.0), axis=1, keepdims=True))
    out["lm"] = lms
    out["cols"] = cols
    out["lasts"] = lasts
    out["dt_b"] = jnp.where(lo, dts[0], dts[1])
    out["e_b"] = jnp.where(lo, jnp.exp(cols[0]), jnp.exp(cols[1]))
    out["dec_cols"] = [jnp.exp(lasts[0] - cols[0]), jnp.exp(lasts[1] - cols[1])]
    out["dec_b"] = jnp.where(lo, out["dec_cols"][0], out["dec_cols"][1])
    lo1 = lane1 < SSD_P
    out["explast"] = [jnp.exp(lasts[0]), jnp.exp(lasts[1])]
    out["explast_b"] = jnp.where(lo1, out["explast"][0], out["explast"][1])
    return out


def _ssd_fwd(zx, xc, decay, d_lane, nw, d_inner, after, name):
    L = zx.shape[0]
    nc = L // SSD_CHUNK
    gw = d_inner // SSD_G
    heads = gw // SSD_P
    n_pair = heads // 2
    bc0 = d_inner // LANES

    def body(z_ref, xs_ref, b_ref, c_ref, dt_ref, cs_ref, cst_ref, last_ref, dl_ref, nw_ref, after_ref,
             y_ref, yn_ref, prev_ref, s_ref):
        @pl.when(pl.program_id(1) == 0)
        def _():
            s_ref[...] = jnp.zeros_like(s_ref)

        q = _ssd_common(dt_ref, cs_ref, last_ref, b_ref, c_ref)
        prev_ref[...] = s_ref[...]
        lo = q["lo"]
        for j in range(n_pair):
            sl = slice(j * LANES, (j + 1) * LANES)
            p = _pair_terms(q, cst_ref, pl.program_id(0) * heads + 2 * j)
            xs_p = xs_ref[:, sl]
            xp = xs_p * p["dt_b"]
            xb = xp.astype(BF16)
            m_a = (q["scores"] * p["lm"][0]).astype(BF16)
            m_b = (q["scores"] * p["lm"][1]).astype(BF16)
            yd = jnp.where(lo, _dot(m_a, xb), _dot(m_b, xb))
            s_p = s_ref[:, sl]
            yo = _dot(q["cb"], s_p.astype(BF16)) * p["e_b"]
            y_ref[:, sl] = yd + yo + dl_ref[:, sl] * xs_p
            st = _dot(q["bb"], (xp * p["dec_b"]).astype(BF16), "tn")
            s_ref[:, sl] = s_p * p["explast_b"] + st
        yv = y_ref[...]
        zv = z_ref[...]
        yg = yv * (zv * _sigmoid(zv))
        rstd = lax.rsqrt(jnp.mean(yg * yg, axis=-1, keepdims=True) + NORM_EPS)
        yn_ref[...] = (yg * rstd * nw_ref[...]).astype(BF16)

    grp = lambda width: pl.BlockSpec((None, 1, width), lambda g, c: (g, 0, 0))
    dt_, _, cs_, cst_, last_ = decay
    return _pcall(
        body, name=name, grid=(SSD_G, nc),
        out_shape=[jax.ShapeDtypeStruct((L, d_inner), F32), jax.ShapeDtypeStruct((L, d_inner), BF16),
                   jax.ShapeDtypeStruct((nc, SSD_G, SSD_N, gw), F32)],
        in_specs=[pl.BlockSpec((SSD_CHUNK, gw), lambda g, c: (c, g)),
                  pl.BlockSpec((SSD_CHUNK, gw), lambda g, c: (c, g)),
                  pl.BlockSpec((SSD_CHUNK, SSD_N), lambda g, c: (c, bc0 + g)),
                  pl.BlockSpec((SSD_CHUNK, SSD_N), lambda g, c: (c, bc0 + SSD_G + g)),
                  pl.BlockSpec((SSD_CHUNK, LANES), lambda g, c: (c, 0)),
                  pl.BlockSpec((SSD_CHUNK, LANES), lambda g, c: (c, 0)),
                  pl.BlockSpec((None, SSD_CHUNK, LANES), lambda g, c: (c, 0, 0)),
                  pl.BlockSpec((None, 1, LANES), lambda g, c: (c, 0, 0)),
                  grp(gw), grp(gw), pl.BlockSpec(memory_space=pl.ANY)],
        out_specs=[pl.BlockSpec((SSD_CHUNK, gw), lambda g, c: (c, g)),
                   pl.BlockSpec((SSD_CHUNK, gw), lambda g, c: (c, g)),
                   pl.BlockSpec((None, None, SSD_N, gw), lambda g, c: (c, g, 0, 0))],
        scratch_shapes=[pltpu.VMEM((SSD_N, gw), F32)],
        compiler_params=_cparams(("parallel", "arbitrary")))(
            zx, xc, xc, xc, dt_, cs_, cst_, last_, d_lane, nw, after)


def _ssd_bwd(dyn, y, zx, xc, prev, decay, alog_p, d_lane, nw, d_inner, name):
    L = zx.shape[0]
    nc = L // SSD_CHUNK
    gw = d_inner // SSD_G
    heads = gw // SSD_P
    n_pair = heads // 2
    bc0 = d_inner // LANES

    def body(dyn_ref, y_ref, z_ref, xs_ref, b_ref, c_ref, prev_ref, dt_ref, sg_ref, cs_ref, cst_ref, last_ref,
             alog_ref, dl_ref, nw_ref,
             dz_ref, dxs_ref, db_ref, dc_ref, ddt_ref, dbias_ref, dalog_ref, dd_ref, dnw_ref,
             ds_ref, racc_ref):
        @pl.when(pl.program_id(1) == 0)
        def _():
            ds_ref[...] = jnp.zeros_like(ds_ref)
            dbias_ref[...] = jnp.zeros_like(dbias_ref)
            dalog_ref[...] = jnp.zeros_like(dalog_ref)
            dd_ref[...] = jnp.zeros_like(dd_ref)
            dnw_ref[...] = jnp.zeros_like(dnw_ref)

        q = _ssd_common(dt_ref, cs_ref, last_ref, b_ref, c_ref)
        a_row = -jnp.exp(alog_ref[...])
        lane, row, lo = q["lane"], q["row"], q["lo"]
        lane1 = lax.broadcasted_iota(jnp.int32, (1, LANES), 1)
        head0 = pl.program_id(0) * heads
        mine = (lane >= head0) & (lane < head0 + heads)

        yv, zv, dynv, nwv = y_ref[...], z_ref[...], dyn_ref[...], nw_ref[...]
        sig = _sigmoid(zv)
        sz = zv * sig
        yg = yv * sz
        rstd = lax.rsqrt(jnp.mean(yg * yg, axis=-1, keepdims=True) + NORM_EPS)
        yhat = yg * rstd
        dnw_ref[...] += jnp.sum(dynv * yhat, axis=0, keepdims=True)
        dyh = dynv * nwv
        dyg = rstd * (dyh - yhat * jnp.mean(dyh * yhat, axis=-1, keepdims=True))
        dz_ref[...] = (dyg * yv * (sig * (1.0 + zv * (1.0 - sig)))).astype(BF16)
        dy_all = dyg * sz

        dg = jnp.zeros((SSD_CHUNK, SSD_CHUNK), F32)
        dc_acc = jnp.zeros((SSD_CHUNK, SSD_N), F32)
        db_acc = jnp.zeros((SSD_CHUNK, SSD_N), F32)
        dcs_mat = jnp.zeros((SSD_CHUNK, LANES), F32)
        ddt_mat = jnp.zeros((SSD_CHUNK, LANES), F32)
        dd_row = jnp.zeros((1, LANES), F32)
        racc_ref[...] = jnp.zeros_like(racc_ref)
        is_last = row == SSD_CHUNK - 1

        for j in range(n_pair):
            sl = slice(j * LANES, (j + 1) * LANES)
            ha, hb = head0 + 2 * j, head0 + 2 * j + 1
            p = _pair_terms(q, cst_ref, ha)
            xs_p = xs_ref[:, sl]
            dyp = dy_all[:, sl]
            xp = xs_p * p["dt_b"]
            xb = xp.astype(BF16)
            s_p = prev_ref[:, sl]
            s_pb = s_p.astype(BF16)
            dsn = ds_ref[:, sl]
            dsnb = dsn.astype(BF16)
            m_f = [q["scores"] * p["lm"][0], q["scores"] * p["lm"][1]]

            t0 = dyp * xs_p
            dd_row = dd_row + jnp.where(lane1 == ha, _sum_all(jnp.where(lo, t0, 0.0)), 0.0) \
                + jnp.where(lane1 == hb, _sum_all(jnp.where(lo, 0.0, t0)), 0.0)
            dxs_p = dl_ref[:, sl] * dyp

            yo = _dot(q["cb"], s_pb) * p["e_b"]
            dcs_b = (dyp * p["e_b"]).astype(BF16)
            dc_acc = dc_acc + _dot(dcs_b, s_pb, "nt")
            ds_yo = _dot(q["cb"], dcs_b, "tn")
            t1 = dyp * yo
            dcs_cols = [jnp.sum(jnp.where(lo, t1, 0.0), axis=1, keepdims=True),
                        jnp.sum(jnp.where(lo, 0.0, t1), axis=1, keepdims=True)]

            t2 = dsn * s_p
            dlast = [p["explast"][0] * _sum_all(jnp.where(lo, t2, 0.0)),
                     p["explast"][1] * _sum_all(jnp.where(lo, 0.0, t2))]
            ds_ref[:, sl] = dsn * p["explast_b"] + ds_yo
            w = _dot(q["bb"], dsnb)
            db_acc = db_acc + _dot((xp * p["dec_b"]).astype(BF16), dsnb, "nt")
            dxp = w * p["dec_b"]
            t3 = w * xp
            e = [jnp.sum(jnp.where(lo, t3, 0.0), axis=1, keepdims=True) * p["dec_cols"][0],
                 jnp.sum(jnp.where(lo, 0.0, t3), axis=1, keepdims=True) * p["dec_cols"][1]]
            for i in range(2):
                dlast[i] = dlast[i] + jnp.sum(e[i], axis=0, keepdims=True)
                dcs_cols[i] = dcs_cols[i] - e[i]

            dyb = dyp.astype(BF16)
            dy_h = [jnp.where(lo, dyp, 0.0).astype(BF16), jnp.where(lo, 0.0, dyp).astype(BF16)]
            dms = [_dot(dy_h[0], xb, "nt"), _dot(dy_h[1], xb, "nt")]
            dxp = dxp + jnp.where(lo, _dot(m_f[0].astype(BF16), dyb, "tn"), _dot(m_f[1].astype(BF16), dyb, "tn"))
            for i, h in enumerate((ha, hb)):
                dg = dg + dms[i] * p["lm"][i]
                qm = dms[i] * m_f[i]
                dcs_cols[i] = dcs_cols[i] + jnp.sum(qm, axis=1, keepdims=True)
                racc_ref[pl.ds(h, 1), :] = jnp.sum(qm, axis=0, keepdims=True)

            dxs_ref[:, sl] = dxs_p + dxp * p["dt_b"]
            t4 = dxp * xs_p
            ddt_cols = [jnp.sum(jnp.where(lo, t4, 0.0), axis=1, keepdims=True),
                        jnp.sum(jnp.where(lo, 0.0, t4), axis=1, keepdims=True)]
            for i, h in enumerate((ha, hb)):
                sel = lane == h
                dcs_mat = dcs_mat + jnp.where(sel, dcs_cols[i], 0.0) + jnp.where(sel & is_last, dlast[i], 0.0)
                ddt_mat = ddt_mat + jnp.where(sel, ddt_cols[i], 0.0)

        dcs_mat = dcs_mat - racc_ref[...].T
        tri_t = (row <= lane).astype(BF16)
        da = _dot_exact01(tri_t, dcs_mat)
        ddt = ddt_mat + da * a_row
        dalog_ref[...] += jnp.sum(jnp.where(mine, da * q["dt"], 0.0), axis=0, keepdims=True) * a_row
        draw = jnp.where(mine, ddt * sg_ref[...], 0.0)
        ddt_ref[...] = draw
        dbias_ref[...] += jnp.sum(draw, axis=0, keepdims=True)
        dd_ref[...] += dd_row
        dgb = dg.astype(BF16)
        dc_ref[...] = dc_acc + _dot(dgb, q["bb"])
        db_ref[...] = db_acc + _dot(dgb, q["cb"], "tn")

    rev = lambda c: nc - 1 - c
    grp = lambda width: pl.BlockSpec((None, 1, width), lambda g, c: (g, 0, 0))
    blk = lambda width, off: pl.BlockSpec((SSD_CHUNK, width), lambda g, c: (rev(c), off + g))
    head_vec = pl.BlockSpec((1, LANES), lambda g, c: (0, 0))
    chunk_rows = pl.BlockSpec((SSD_CHUNK, LANES), lambda g, c: (rev(c), 0))
    dt_, sg_, cs_, cst_, last_ = decay
    return _pcall(
        body, name=name, grid=(SSD_G, nc),
        out_shape=[jax.ShapeDtypeStruct(zx.shape, BF16), jax.ShapeDtypeStruct((L, d_inner), F32),
                   jax.ShapeDtypeStruct((L, SSD_G * SSD_N), F32), jax.ShapeDtypeStruct((L, SSD_G * SSD_N), F32),
                   jax.ShapeDtypeStruct((SSD_G, L, LANES), F32),
                   jax.ShapeDtypeStruct((SSD_G, 1, LANES), F32), jax.ShapeDtypeStruct((SSD_G, 1, LANES), F32),
                   jax.ShapeDtypeStruct((SSD_G, 1, LANES), F32), jax.ShapeDtypeStruct((SSD_G, 1, gw), F32)],
        in_specs=[blk(gw, 0), blk(gw, 0), blk(gw, 0), blk(gw, 0), blk(SSD_N, bc0), blk(SSD_N, bc0 + SSD_G),
                  pl.BlockSpec((None, None, SSD_N, gw), lambda g, c: (rev(c), g, 0, 0)),
                  chunk_rows, chunk_rows, chunk_rows,
                  pl.BlockSpec((None, SSD_CHUNK, LANES), lambda g, c: (rev(c), 0, 0)),
                  pl.BlockSpec((None, 1, LANES), lambda g, c: (rev(c), 0, 0)),
                  head_vec, grp(gw), grp(gw)],
        out_specs=[blk(gw, 0), blk(gw, 0), blk(SSD_N, 0), blk(SSD_N, 0),
                   pl.BlockSpec((None, SSD_CHUNK, LANES), lambda g, c: (g, rev(c), 0)),
                   grp(LANES), grp(LANES), grp(LANES), grp(gw)],
        scratch_shapes=[pltpu.VMEM((SSD_N, gw), F32), pltpu.VMEM((SSD_CHUNK, LANES), F32)],
        compiler_params=_cparams(("parallel", "arbitrary")))(
            dyn, y, zx, xc, xc, xc, prev, dt_, sg_, cs_, cst_, last_, alog_p, d_lane, nw)


def _cond_mod(c_pad, ada_w, ada_b_loc, after, name):
    depth, D, n = ada_w.shape
    rows = c_pad.shape[0]

    def body(c_ref, w_ref, b_ref, after_ref, mod_ref, cond_ref):
        cv = c_ref[...]
        cond = cv * _sigmoid(cv)
        cond_ref[...] = cond
        mod_ref[...] = _dot(cond.astype(BF16), w_ref[...].astype(BF16)) + b_ref[...]

    return _pcall(
        body, name=name, grid=(depth,),
        out_shape=[jax.ShapeDtypeStruct((depth, rows, n), F32), jax.ShapeDtypeStruct((rows, D), F32)],
        in_specs=[pl.BlockSpec((rows, D), lambda i: (0, 0)),
                  pl.BlockSpec((None, D, n), lambda i: (i, 0, 0)),
                  pl.BlockSpec((None, 1, n), lambda i: (i, 0, 0)),
                  pl.BlockSpec(memory_space=pl.ANY)],
        out_specs=[pl.BlockSpec((None, rows, n), lambda i: (i, 0, 0)),
                   pl.BlockSpec((rows, D), lambda i: (0, 0))],
        compiler_params=_cparams(("arbitrary",)))(c_pad, ada_w, ada_b_loc, after)


def _adamw_math(g, w, m, v):
    m_new = ADAM_B1 * m + (1.0 - ADAM_B1) * g
    v_new = ADAM_B2 * v + (1.0 - ADAM_B2) * (g * g)
    m_hat = m_new / (1.0 - ADAM_B1 ** ADAM_STEP)
    v_hat = v_new / (1.0 - ADAM_B2 ** ADAM_STEP)
    delta = -ADAM_LR * (m_hat / (jnp.sqrt(v_hat) + ADAM_EPS) + ADAM_WD * w)
    return delta, m_new, v_new


def _adamw_sum(parts, w, m, v, layer, name, prev=None, tr=None):
    depth, R, C = w.shape
    n_parts = parts.shape[0]
    tr = _tile(R, tr if tr is not None else (512 if C <= 512 else 256))

    def body(p_ref, w_ref, m_ref, v_ref, *rest):
        g_ref, d_ref, mo_ref, vo_ref = rest[-4:]
        g = p_ref[0].astype(F32)
        for k in range(1, n_parts):
            g = g + p_ref[k].astype(F32)
        d, mn, vn = _adamw_math(g, w_ref[...], m_ref[...], v_ref[...])
        g_ref[...] = g
        d_ref[...] = d
        mo_ref[...] = mn
        vo_ref[...] = vn

    blk = pl.BlockSpec((None, tr, C), lambda i: (layer, i, 0))
    prev = list(prev) if prev is not None else []
    return _pcall(
        body, name=name, grid=(R // tr,),
        out_shape=[jax.ShapeDtypeStruct((depth, R, C), F32)] * 4,
        in_specs=[pl.BlockSpec((n_parts, tr, C), lambda i: (0, i, 0)), blk, blk, blk]
        + [pl.BlockSpec(memory_space=pl.ANY)] * len(prev),
        out_specs=[blk] * 4, input_output_aliases={4 + k: k for k in range(len(prev))},
        compiler_params=_cparams(("parallel",)))(parts, w, m, v, *prev)


def _adamw_small(parts, wmv, head_parts, head_wmv, loss_parts, name):
    n, nh = len(parts), len(head_parts)
    n_heads = head_wmv[0][0].shape[1] if nh else 0
    groups = head_parts[0].shape[1] if nh else 0
    d_model = loss_parts.shape[2]

    def body(*refs):
        p_refs, refs = refs[:n], refs[n:]
        wmv_refs, refs = refs[:3 * n], refs[3 * n:]
        hp_refs, refs = refs[:nh], refs[nh:]
        hwmv_refs, refs = refs[:3 * nh], refs[3 * nh:]
        loss_ref, refs = refs[0], refs[1:]
        outs, loss_out, head_scr = refs[:4 * (n + nh)], refs[4 * (n + nh)], refs[4 * (n + nh) + 1]

        def update(i, g, w_ref, m_ref, v_ref):
            res = (g,) + _adamw_math(g, w_ref[...], m_ref[...], v_ref[...])
            for o_ref, r in zip(outs[4 * i:4 * i + 4], res):
                o_ref[...] = r

        for i in range(n):
            g = p_refs[i][0]
            for k in range(1, N_DEV):
                g = g + p_refs[i][k]
            update(i, g, *wmv_refs[3 * i:3 * i + 3])
        for i in range(nh):
            g = None
            for k in range(N_DEV):
                for grp in range(groups):
                    g = hp_refs[i][k, grp] if g is None else g + hp_refs[i][k, grp]
            head_scr[...] = g
            update(n + i, head_scr[:, 0:n_heads], *hwmv_refs[3 * i:3 * i + 3])
        tot = loss_ref[0]
        for k in range(1, N_DEV):
            tot = tot + loss_ref[k]
        loss_out[...] = jnp.broadcast_to(_sum_all(tot) * (0.5 / d_model), loss_out.shape)

    operands = list(parts) + [a for t in wmv for a in t] + list(head_parts) + [a for t in head_wmv for a in t]
    operands.append(loss_parts)
    out_shape = [jax.ShapeDtypeStruct(t[0].shape, F32) for t in list(wmv) + list(head_wmv) for _ in range(4)]
    out_shape.append(jax.ShapeDtypeStruct((1, LANES), F32))
    vmem = pl.BlockSpec(memory_space=pltpu.VMEM)
    outs = _pcall(body, name=name, out_shape=out_shape, in_specs=[vmem] * len(operands),
                  out_specs=[vmem] * len(out_shape), scratch_shapes=[pltpu.VMEM((1, LANES), F32)],
                  compiler_params=_cparams())(*operands)
    return [outs[4 * i:4 * i + 4] for i in range(n + nh)], outs[-1]


def _ada_adamw(cond_pad, dmod_pad, w, m, v, name, tr=512):
    depth, D, n = w.shape
    rows = cond_pad.shape[0]
    tr = _tile(D, tr)

    def body(c_ref, dm_ref, w_ref, m_ref, v_ref, g_ref, d_ref, mo_ref, vo_ref):
        g = _dot(c_ref[...].astype(BF16), dm_ref[...].astype(BF16), "tn")
        d, mn, vn = _adamw_math(g, w_ref[...], m_ref[...], v_ref[...])
        g_ref[...] = g
        d_ref[...] = d
        mo_ref[...] = mn
        vo_ref[...] = vn

    blk = pl.BlockSpec((None, tr, n), lambda i, r: (i, r, 0))
    return _pcall(
        body, name=name, grid=(depth, D // tr),
        out_shape=[jax.ShapeDtypeStruct((depth, D, n), F32)] * 4,
        in_specs=[pl.BlockSpec((rows, tr), lambda i, r: (0, r)),
                  pl.BlockSpec((None, rows, n), lambda i, r: (i, 0, 0)), blk, blk, blk],
        out_specs=[blk] * 4, compiler_params=_cparams(("parallel", "parallel")))(cond_pad, dmod_pad, w, m, v)


def kernel(x, c, ada_w, ada_b, mix_norm_w, mlp_norm_w, mlp_up, mlp_down, ssd_in_w, ssd_conv_w, ssd_conv_b, ssd_dt_bias, ssd_A_log, ssd_D, ssd_norm_w, ssd_out_w, sc_in_w, sc_conv_w, sc_out_w, final_norm_w, loss_target, m_ada_w, m_ada_b, m_mix_norm_w, m_mlp_norm_w, m_mlp_up, m_mlp_down, m_ssd_in_w, m_ssd_conv_w, m_ssd_conv_b, m_ssd_dt_bias, m_ssd_A_log, m_ssd_D, m_ssd_norm_w, m_ssd_out_w, m_sc_in_w, m_sc_conv_w, m_sc_out_w, m_final_norm_w, v_ada_w, v_ada_b, v_mix_norm_w, v_mlp_norm_w, v_mlp_up, v_mlp_down, v_ssd_in_w, v_ssd_conv_w, v_ssd_conv_b, v_ssd_dt_bias, v_ssd_A_log, v_ssd_D, v_ssd_norm_w, v_ssd_out_w, v_sc_in_w, v_sc_conv_w, v_sc_out_w, v_final_norm_w):
    weights = dict(ada_w=ada_w, ada_b=ada_b, mix_norm_w=mix_norm_w, mlp_norm_w=mlp_norm_w, mlp_up=mlp_up,
                   mlp_down=mlp_down, ssd_in_w=ssd_in_w, ssd_conv_w=ssd_conv_w, ssd_conv_b=ssd_conv_b,
                   ssd_dt_bias=ssd_dt_bias, ssd_A_log=ssd_A_log, ssd_D=ssd_D, ssd_norm_w=ssd_norm_w,
                   ssd_out_w=ssd_out_w, sc_in_w=sc_in_w, sc_conv_w=sc_conv_w, sc_out_w=sc_out_w,
                   final_norm_w=final_norm_w)
    moms = dict(ada_w=m_ada_w, ada_b=m_ada_b, mix_norm_w=m_mix_norm_w, mlp_norm_w=m_mlp_norm_w, mlp_up=m_mlp_up,
                mlp_down=m_mlp_down, ssd_in_w=m_ssd_in_w, ssd_conv_w=m_ssd_conv_w, ssd_conv_b=m_ssd_conv_b,
                ssd_dt_bias=m_ssd_dt_bias, ssd_A_log=m_ssd_A_log, ssd_D=m_ssd_D, ssd_norm_w=m_ssd_norm_w,
                ssd_out_w=m_ssd_out_w, sc_in_w=m_sc_in_w, sc_conv_w=m_sc_conv_w, sc_out_w=m_sc_out_w,
                final_norm_w=m_final_norm_w)
    vars_ = dict(ada_w=v_ada_w, ada_b=v_ada_b, mix_norm_w=v_mix_norm_w, mlp_norm_w=v_mlp_norm_w, mlp_up=v_mlp_up,
                 mlp_down=v_mlp_down, ssd_in_w=v_ssd_in_w, ssd_conv_w=v_ssd_conv_w, ssd_conv_b=v_ssd_conv_b,
                 ssd_dt_bias=v_ssd_dt_bias, ssd_A_log=v_ssd_A_log, ssd_D=v_ssd_D, ssd_norm_w=v_ssd_norm_w,
                 ssd_out_w=v_ssd_out_w, sc_in_w=v_sc_in_w, sc_conv_w=v_sc_conv_w, sc_out_w=v_sc_out_w,
                 final_norm_w=v_final_norm_w)
    names = list(weights)

    L, D = x.shape[1], x.shape[2]
    d_inner = 2 * D
    n_heads = d_inner // SSD_P
    hpg = n_heads // SSD_G
    gw = d_inner // SSD_G
    conv_dim = d_inner + 2 * SSD_G * SSD_N
    zx_dim = d_inner + conv_dim
    zx_pad = -(-(zx_dim + LANES) // 512) * 512
    in_ws = ssd_in_w.shape[2]
    in_base, in_off, in_win = _window_geometry(in_ws)
    me = _my_index()
    x0 = x[0]
    tgt = loss_target[0]

    n_mod = ada_w.shape[2]
    (c_all,) = _exchange([c], "gather_c", gather=True)
    gather_handle = {}
    (gather_handle["ssd_in_w"],), token_in = _xfer_start(
        [ssd_in_w[0].astype(BF16)], "gather_start_ssd_in_w", gather=True, via_sibling=(0,), after=(c_all,))
    c_pad = jnp.pad(c_all.reshape(N_DEV, D), ((0, 16 - N_DEV), (0, 0)))
    ada_b_loc = lax.dynamic_slice_in_dim(ada_b, me * n_mod, n_mod, axis=1).reshape(2, 1, n_mod)
    mod_blk, cond_pad = _cond_mod(c_pad, ada_w, ada_b_loc, token_in, "cond_mod")
    gather_order = ["mod", "ssd_conv_w", "sc_conv_w", "ssd_out_w", "up0", "down0", "sc_in_w", "sc_out_w", "up1",
                    "down1"]
    gather_src = dict(mod=mod_blk, ssd_conv_w=ssd_conv_w[0], sc_conv_w=sc_conv_w[0],
                      ssd_out_w=ssd_out_w[0].astype(BF16),
                      up0=mlp_up[0].astype(BF16), down0=mlp_down[0].astype(BF16),
                      sc_in_w=sc_in_w[0].astype(BF16), sc_out_w=sc_out_w[0].astype(BF16),
                      up1=mlp_up[1].astype(BF16), down1=mlp_down[1].astype(BF16))
    handles, gather_token = _xfer_start([gather_src[k] for k in gather_order], "gather_start", gather=True,
                                        via_sibling=tuple(range(3, len(gather_order))))
    gather_handle.update(zip(gather_order, handles))

    def gathered(keys, after, forward):
        tag = "_".join(keys)
        lands = _xfer_wait([gather_handle[k] for k in keys], after, f"gather_wait_{tag}", gather=True)
        return _sibling_forward(lands, f"gather_forward_{tag}") if forward else lands

    def forward_behind(keys, after):
        tag = "_".join(keys)
        lands = _xfer_wait([gather_handle[k] for k in keys], after, f"gather_wait_{tag}", gather=True)
        fwd_handles, token = _sibling_forward_start(lands, f"gather_forward_start_{tag}")
        return (lambda done: _sibling_forward_wait(fwd_handles, done, f"gather_forward_wait_{tag}")), token

    (ssd_in_g,) = gathered(["ssd_in_w"], (gather_token, m_ssd_in_w, v_ssd_in_w), True)
    w_in_all = _shards_to_columns(ssd_in_g, in_base, in_off, in_win, zx_pad, "ssd_in_w_columns")
    (mod_all,) = gathered(["mod"], w_in_all, False)
    mod_mine = lax.dynamic_index_in_dim(mod_all, me, axis=2, keepdims=False)
    mod_mine = jnp.transpose(mod_mine, (1, 0, 2)).reshape(2, 6, 1, D)
    sh_m, sc_m, g_m, sh_f, sc_f, g_f = [[mod_mine[i, k] for i in range(2)] for k in range(6)]

    vec = lambda a: a.reshape(1, -1)
    small = {}

    _, h0 = _norm_mod_fwd(x0, None, None, vec(mix_norm_w[0]), sc_m[0], sh_m[0], "l0_mix_norm")
    cw_all, scw_all = gathered(["ssd_conv_w", "sc_conv_w"], h0, False)
    (zx,) = _mm_nn(h0, w_in_all, F32, "ssd_in_proj", tm=2048, tn=512)
    conv_b0 = vec(ssd_conv_b[0])
    conv_w_full = jnp.transpose(cw_all, (1, 0, 2)).reshape(SSD_K, conv_dim)
    sc_conv_full = jnp.transpose(scw_all, (1, 0, 2)).reshape(SC_K, D)
    xc = _ssd_conv_fwd(zx, conv_w_full, conv_b0, d_inner, conv_dim, "ssd_conv")
    bias_p = jnp.pad(ssd_dt_bias[0], (0, LANES - n_heads)).reshape(1, LANES)
    alog_p = jnp.pad(ssd_A_log[0], (0, LANES - n_heads)).reshape(1, LANES)
    d_lane = jnp.repeat(ssd_D[0], SSD_P).reshape(SSD_G, 1, gw)
    nw_g = ssd_norm_w[0].reshape(SSD_G, 1, gw)
    finish, token = forward_behind(["ssd_out_w"], xc)
    decay = _ssd_decay(zx, bias_p, alog_p, n_heads, zx_dim // LANES, "ssd_decay")
    y_ssd, yn, prev = _ssd_fwd(zx, xc, decay, d_lane, nw_g, d_inner, token, "ssd_scan")
    ups, downs = [None, None], [None, None]
    (ssd_out_g,) = finish(yn)
    w_ssd_out = ssd_out_g.reshape(-1, D)
    finish, token = forward_behind(["up0", "down0"], ssd_out_g)
    (mix0,) = _mm_nn(yn, w_ssd_out, F32, "ssd_out_proj", after=(token,))
    x1, h1 = _norm_mod_fwd(x0, mix0, g_m[0], vec(mlp_norm_w[0]), sc_f[0], sh_f[0], "l0_mlp_norm")
    ups[0], down0_g = finish(h1)
    downs[0] = down0_g.reshape(-1, D)
    u0, s0 = _mm_nn_blocked(h1, ups[0], "l0_mlp_up", _ep_relu2, [BF16, BF16])
    finish, token = forward_behind(["sc_in_w", "sc_out_w", "up1", "down1"], s0)
    (d0,) = _mm_nn(s0, downs[0], F32, "l0_mlp_down", after=(token,))
    x2, h2 = _norm_mod_fwd(x1, d0, g_f[0], vec(mix_norm_w[1]), sc_m[1], sh_m[1], "l1_mix_norm")
    sc_in_g, sc_out_g, ups[1], down1_g = finish(h2)
    w_sc_out, downs[1] = sc_out_g.reshape(-1, D), down1_g.reshape(-1, D)
    (proj,) = _mm_nn_blocked(h2, sc_in_g, "sc_in_proj", _ep_store(F32), [F32])
    yc = _sc_conv_fwd(proj, sc_conv_full, "sc_conv")
    (mix1,) = _mm_nn(yc, w_sc_out, F32, "sc_out_proj")
    x3, h3 = _norm_mod_fwd(x2, mix1, g_m[1], vec(mlp_norm_w[1]), sc_f[1], sh_f[1], "l1_mlp_norm")
    u1, s1 = _mm_nn_blocked(h3, ups[1], "l1_mlp_up", _ep_relu2, [BF16, BF16])
    (d1,) = _mm_nn(s1, downs[1], F32, "l1_mlp_down")

    dx, loss_lane, dfw, dd1, dg = _final_loss(x3, d1, g_f[1], vec(final_norm_w), tgt, "final_loss")
    small["final_norm_w"] = dfw

    dmod = [[None] * 6 for _ in range(2)]
    dmod[1][5] = dg

    def mlp_backward(i, dx_out, dd, x_mid, h_in, u, s, mix, gate):
        du = _mm_nt(dd, downs[i], BF16, f"l{i}_mlp_down_bwd", epilogue=_ep_relu2_bwd, extra=(u,))
        gdown = _mm_tn(s, dd, BF16, f"l{i}_mlp_down_wgrad").reshape(N_DEV, -1, D)
        gup = _mm_tn_blocked(h_in, du, BF16, f"l{i}_mlp_up_wgrad")
        (h_down, h_up), token = _xfer_start([gdown, gup], f"l{i}_mlp_grads_start", gather=False)
        grad_handle[f"mlp_down{i}"], grad_handle[f"mlp_up{i}"] = h_down, h_up
        dh = _mm_nt_blocked(du, ups[i], F32, f"l{i}_mlp_up_bwd", after=(token,))
        dxm, dsh, dsc, dnw, dmix, dgate = _norm_mod_bwd(dh, x_mid, vec(mlp_norm_w[i]), sc_f[i], dx_out,
                                                        f"l{i}_mlp_norm_bwd", branch=(mix, gate))
        dmod[i][3], dmod[i][4], dmod[i][2] = dsh, dsc, dgate
        return dxm, dmix, dnw

    grad_handle = {}
    dx3, dyc, dnw_mlp1 = mlp_backward(1, dx, dd1, x3, h3, u1, s1, mix1, g_m[1])
    g_sc_out = _mm_tn(yc, dyc, BF16, "sc_out_wgrad").reshape(N_DEV, -1, D)
    dconv_out = _mm_nt(dyc, w_sc_out, F32, "sc_out_bwd")
    dbg, dcg, dxv, dscw = _sc_conv_bwd(proj, sc_conv_full, dconv_out, "sc_conv_bwd")
    dproj = jnp.concatenate([dbg, dcg, dxv], axis=1)
    g_sc_in = _mm_tn_blocked(h2, dproj, BF16, "sc_in_wgrad")
    (grad_handle["sc_out_w0"], grad_handle["sc_in_w0"]), token = _xfer_start(
        [g_sc_out, g_sc_in], "sc_grads_start", gather=False)
    dh2 = _mm_nt_blocked(dproj, sc_in_g, F32, "sc_in_bwd", after=(token,))
    dx2, dsh, dsc, dnw_mix1, dd0, dg = _norm_mod_bwd(dh2, x2, vec(mix_norm_w[1]), sc_m[1], dx3, "l1_mix_norm_bwd",
                                                     branch=(d0, g_f[0]))
    dmod[1][0], dmod[1][1], dmod[0][5] = dsh, dsc, dg
    dx1, dyo, dnw_mlp0 = mlp_backward(0, dx2, dd0, x1, h1, u0, s0, mix0, g_m[0])
    g_ssd_out = _mm_tn(yn, dyo, BF16, "ssd_out_wgrad").reshape(N_DEV, -1, D)
    (grad_handle["ssd_out_w0"],), token = _xfer_start([g_ssd_out], "ssd_out_grad_start", gather=False)
    dyn = _mm_nt(dyo, w_ssd_out, F32, "ssd_out_bwd", after=(token,))
    dz, dxs, db_, dc_, ddt, dbias, dalog, dd_, dnw_ssd = _ssd_bwd(
        dyn, y_ssd, zx, xc, prev, decay, alog_p, d_lane, nw_g, d_inner, "ssd_scan_bwd")
    dzx, dcw, dcb = _ssd_conv_bwd(zx, conv_w_full, conv_b0, [dxs, db_, dc_], dz, d_inner, "ssd_conv_bwd")
    dzx = _dzx_finish(dzx, ddt, zx_dim, "ssd_dzx_finish")
    g_in_all = _mm_tn(h0, dzx, BF16, "ssd_in_wgrad", tn=512, tk=2048)
    g_ssd_in = _columns_to_shards(g_in_all, in_ws, in_base, in_off, in_win, "ssd_in_wgrad_shards")
    g_ssd_in = _sibling_reduce(g_ssd_in, "ssd_in_grad_pair")
    (grad_handle["ssd_in_w0"],), token = _xfer_start([g_ssd_in], "ssd_in_grad_start", gather=False, by_chip=True)
    dh0 = _mm_nt(dzx, w_in_all, F32, "ssd_in_bwd", tm=1024, tk=dzx.shape[1] // 2, after=(token,))
    grad_x, dsh, dsc, dnw_mix0 = _norm_mod_bwd(dh0, x0, vec(mix_norm_w[0]), sc_m[0], dx1, "l0_mix_norm_bwd")
    dmod[0][0], dmod[0][1] = dsh, dsc

    small["ada_b"] = jnp.concatenate([jnp.concatenate(dmod[i], axis=1) for i in range(2)], axis=0)
    small["mix_norm_w"] = jnp.concatenate([dnw_mix0, dnw_mix1], axis=0)
    small["mlp_norm_w"] = jnp.concatenate([dnw_mlp0, dnw_mlp1], axis=0)
    small["ssd_conv_w"] = dcw
    small["ssd_conv_b"] = dcb
    small["ssd_norm_w"] = dnw_ssd.reshape(1, d_inner)
    small["sc_conv_w"] = dscw
    small["loss"] = loss_lane
    small_names = list(small)
    head_names = ["ssd_dt_bias", "ssd_A_log", "ssd_D"]
    handles, small_token = _xfer_start([small[k] for k in small_names] + [dbias, dalog, dd_],
                                       "small_grads_start", gather=True)

    out_g, out_d, out_m, out_v = {}, {}, {}, {}

    layer_res = {}

    def big_update(name, i, after):
        (parts,) = _xfer_wait([grad_handle[f"{name}{i}"]], after, f"grads_wait_{name}_{i}", gather=False)
        res = _adamw_sum(parts, weights[name], moms[name], vars_[name], i, f"adamw_{name}_{i}",
                         prev=layer_res.get(name))
        layer_res[name] = res
        return res[1]

    chain = small_token
    for name, i in [("mlp_down", 1), ("mlp_up", 1), ("sc_out_w", 0), ("sc_in_w", 0), ("mlp_down", 0),
                    ("mlp_up", 0), ("ssd_out_w", 0), ("ssd_in_w", 0)]:
        chain = big_update(name, i, chain)
    gathered_small = _xfer_wait(handles, chain, "small_grads_wait", gather=True)
    small_all = dict(zip(small_names + head_names, gathered_small))

    dmod_loc = lax.dynamic_slice_in_dim(small_all["ada_b"], me * n_mod, n_mod, axis=2)
    dmod_pad = jnp.pad(jnp.transpose(dmod_loc, (1, 0, 2)), ((0, 0), (0, 16 - N_DEV), (0, 0)))
    out_g["ada_w"], out_d["ada_w"], out_m["ada_w"], out_v["ada_w"] = _ada_adamw(
        cond_pad, dmod_pad, ada_w, m_ada_w, v_ada_w, "adamw_ada_w")

    for k in ("ssd_conv_w", "sc_conv_w"):
        n_loc = weights[k].shape[2]
        small_all[k] = lax.dynamic_slice_in_dim(small_all[k], me * n_loc, n_loc, axis=2)
    plain = [k for k in small_names if k != "loss"]
    as2d = lambda a: a.reshape(-1, a.shape[-1])
    res, loss_row = _adamw_small(
        [small_all[k] for k in plain], [tuple(as2d(d[k]) for d in (weights, moms, vars_)) for k in plain],
        [small_all[k] for k in head_names], [tuple(as2d(d[k]) for d in (weights, moms, vars_)) for k in head_names],
        small_all["loss"], "adamw_small")
    loss = loss_row[0, 0]
    for k, res4 in zip(plain + head_names, res):
        for r, dst in zip(res4, (out_g, out_d, out_m, out_v)):
            dst[k] = r.reshape(weights[k].shape)
    for name, res4 in layer_res.items():
        for r, dst in zip(res4, (out_g, out_d, out_m, out_v)):
            dst[name] = r

    return (loss, grad_x[None], *[out_g[k] for k in names], *[out_d[k] for k in names],
            *[out_m[k] for k in names], *[out_v[k] for k in names])
```

```python
import jax
import jax.numpy as jnp
from jax import lax
from jax.experimental import pallas as pl
from jax.experimental.pallas import tpu as pltpu

F32 = jnp.float32
BF16 = jnp.bfloat16
N_DEV = 8
MESH = pl.DeviceIdType.MESH

NORM_EPS = 1e-5
SSD_G = 4
SSD_P = 64
SSD_N = 128
SSD_CHUNK = 128
SSD_K = 4
SC_K = 3
LANES = 128

ADAM_LR = 0.001
ADAM_B1 = 0.9
ADAM_B2 = 0.999
ADAM_EPS = 1e-08
ADAM_WD = 0.01
ADAM_STEP = 10

VMEM_LIMIT = 56 * 1024 * 1024


def _pcall(body, **kw):
    return pl.pallas_call(body, **kw)


def _cparams(sem=None):
    if sem is None:
        return pltpu.CompilerParams(vmem_limit_bytes=VMEM_LIMIT)
    return pltpu.CompilerParams(dimension_semantics=sem, vmem_limit_bytes=VMEM_LIMIT)


def _my_index():
    return 4 * lax.axis_index("x") + 2 * lax.axis_index("y") + lax.axis_index("c")


_PEER_MASKS = [(0, 0, 1), (0, 1, 0), (0, 1, 1), (1, 0, 0), (1, 0, 1), (1, 1, 0), (1, 1, 1)]


def _peers():
    x, y, c = lax.axis_index("x"), lax.axis_index("y"), lax.axis_index("c")
    out = []
    for mx, my, mc in _PEER_MASKS:
        px = (1 - x) if mx else x
        py = (1 - y) if my else y
        pc = (1 - c) if mc else c
        out.append(((px, py, pc), 4 * px + 2 * py + pc))
    return out


def _exchange(arrs, name, gather):
    n = len(arrs)
    n_peer = N_DEV - 1

    def body(*refs):
        ins, outs = refs[:n], refs[n:2 * n]
        send_sems, recv_sems, local_sems = refs[2 * n:]
        me = _my_index()
        peers = _peers()
        started = []
        for a in range(n):
            src_own = ins[a] if gather else ins[a].at[me]
            own = pltpu.make_async_copy(src_own, outs[a].at[me], local_sems.at[a])
            own.start()
            started.append(own)
        sends = []
        for a in range(n):
            for k, (peer, pidx) in enumerate(peers):
                src = ins[a] if gather else ins[a].at[pidx]
                cp = pltpu.make_async_remote_copy(
                    src_ref=src, dst_ref=outs[a].at[me],
                    send_sem=send_sems.at[a * n_peer + k], recv_sem=recv_sems.at[a * n_peer + k],
                    device_id=peer, device_id_type=MESH)
                cp.start()
                sends.append(cp)
        for a in range(n):
            for k, (peer, pidx) in enumerate(peers):
                src = ins[a] if gather else ins[a].at[pidx]
                pltpu.make_async_remote_copy(
                    src_ref=src, dst_ref=outs[a].at[pidx],
                    send_sem=send_sems.at[a * n_peer + k], recv_sem=recv_sems.at[a * n_peer + k],
                    device_id=peer, device_id_type=MESH).wait_recv()
        for cp in sends:
            cp.wait_send()
        for own in started:
            own.wait()

    if gather:
        out_shape = [jax.ShapeDtypeStruct((N_DEV,) + a.shape, a.dtype) for a in arrs]
    else:
        out_shape = [jax.ShapeDtypeStruct(a.shape, a.dtype) for a in arrs]
    any_spec = pl.BlockSpec(memory_space=pl.ANY)
    outs = _pcall(
        body, name=name, out_shape=out_shape,
        in_specs=[any_spec] * n, out_specs=[any_spec] * n,
        scratch_shapes=[pltpu.SemaphoreType.DMA((n * n_peer,)), pltpu.SemaphoreType.DMA((n * n_peer,)),
                        pltpu.SemaphoreType.DMA((n,))],
        compiler_params=pltpu.CompilerParams(has_side_effects=True),
    )(*arrs)
    return list(outs)


def _sibling_forward_start(lands, name):
    n = len(lands)
    n_fwd = len(_OTHER_CHIPS)

    def body(*refs):
        ins, bufs = refs[:n], refs[3 * n:4 * n]
        token = refs[-1]
        sibling = (lax.axis_index("x"), lax.axis_index("y"), 1 - lax.axis_index("c"))
        peers = _peers()
        for a in range(n):
            send_sems, recv_sems = refs[n + 2 * a], refs[n + 2 * a + 1]
            for j, k in enumerate(_OTHER_CHIPS):
                slot = peers[k][1]
                pltpu.make_async_remote_copy(
                    src_ref=ins[a].at[slot], dst_ref=bufs[a].at[slot], send_sem=send_sems.at[j],
                    recv_sem=recv_sems.at[j], device_id=sibling, device_id_type=MESH).start()
        token[...] = jnp.zeros_like(token)

    out_shape, out_specs = [], []
    for _ in range(n):
        out_shape += [pltpu.SemaphoreType.DMA((n_fwd,)), pltpu.SemaphoreType.DMA((n_fwd,))]
        out_specs += [_SEM, _SEM]
    out_shape += [pltpu.HBM(a.shape, a.dtype) for a in lands] + [jax.ShapeDtypeStruct((8, LANES), F32)]
    out_specs += [_HBM] * n + [pl.BlockSpec(memory_space=pltpu.VMEM)]
    outs = _pcall(
        body, name=name, out_shape=tuple(out_shape), in_specs=[_HBM] * n, out_specs=tuple(out_specs),
        input_output_aliases={a: 2 * n + a for a in range(n)},
        compiler_params=pltpu.CompilerParams(has_side_effects=_DATAFLOW),
    )(*[pltpu.with_memory_space_constraint(a, pltpu.HBM) for a in lands])
    return [(outs[2 * n + a], outs[2 * a], outs[2 * a + 1]) for a in range(n)], outs[-1]


def _sibling_forward_wait(handles, after, name):
    n = len(handles)

    def body(*refs):
        sibling = (lax.axis_index("x"), lax.axis_index("y"), 1 - lax.axis_index("c"))
        peers = _peers()
        for a in range(n):
            buf, send_sems, recv_sems = refs[3 * a:3 * a + 3]
            for j, k in enumerate(_OTHER_CHIPS):
                (px, py, pc), slot = peers[k]
                theirs = 4 * px + 2 * py + (1 - pc)
                cp = pltpu.make_async_remote_copy(
                    src_ref=buf.at[slot], dst_ref=buf.at[theirs], send_sem=send_sems.at[j],
                    recv_sem=recv_sems.at[j], device_id=sibling, device_id_type=MESH)
                cp.wait_send()
                cp.wait_recv()

    operands, in_specs = [], []
    for h in handles:
        operands += list(h)
        in_specs += [_HBM, _SEM, _SEM]
    outs = _pcall(
        body, name=name, out_shape=tuple(pltpu.HBM(h[0].shape, h[0].dtype) for h in handles),
        in_specs=in_specs + [pl.BlockSpec(memory_space=pl.ANY)], out_specs=tuple([_HBM] * n),
        input_output_aliases={3 * a: a for a in range(n)},
        compiler_params=pltpu.CompilerParams(has_side_effects=_DATAFLOW),
    )(*operands, after)
    return list(outs)


_HBM = pl.BlockSpec(memory_space=pltpu.HBM)
_SEM = pl.BlockSpec(memory_space=pltpu.SEMAPHORE)
_DATAFLOW = pltpu.SideEffectType.DATAFLOW_SIDE_EFFECTING


_ALL_PEERS = tuple(range(N_DEV - 1))
_SAME_CORE_PEERS = (0, 1, 3, 5)
_OTHER_CHIPS = (1, 3, 5)


def _xfer_start(arrs, name, gather, via_sibling=(), after=()):
    n = len(arrs)
    n_peer = N_DEV - 1
    n_after = len(after)
    peer_ks = [_SAME_CORE_PEERS if a in via_sibling else _ALL_PEERS for a in range(n)]

    def body(*refs):
        ins, lands = refs[:n], refs[n:2 * n]
        sems = refs[2 * n + n_after:5 * n + n_after]
        token = refs[-1]
        me = _my_index()
        peers = _peers()
        for a in range(n):
            send_sems, recv_sems, loc_sem = sems[3 * a:3 * a + 3]
            src_own = ins[a] if gather else ins[a].at[me]
            pltpu.make_async_copy(src_own, lands[a].at[me], loc_sem).start()
            for k in peer_ks[a]:
                peer, pidx = peers[k]
                src = ins[a] if gather else ins[a].at[pidx]
                pltpu.make_async_remote_copy(
                    src_ref=src, dst_ref=lands[a].at[me], send_sem=send_sems.at[k], recv_sem=recv_sems.at[k],
                    device_id=peer, device_id_type=MESH).start()
        token[...] = jnp.zeros_like(token)

    land_shapes = [((N_DEV,) + a.shape) if gather else a.shape for a in arrs]
    out_shape, out_specs = [], []
    for _ in range(n):
        out_shape += [pltpu.SemaphoreType.DMA((n_peer,)), pltpu.SemaphoreType.DMA((n_peer,)),
                      pltpu.SemaphoreType.DMA(())]
        out_specs += [_SEM, _SEM, _SEM]
    out_shape += [pltpu.HBM(a.shape, a.dtype) for a in arrs]
    out_shape += [pltpu.HBM(s, a.dtype) for s, a in zip(land_shapes, arrs)]
    out_shape += [jax.ShapeDtypeStruct((8, LANES), F32)]
    out_specs += [_HBM] * (2 * n) + [pl.BlockSpec(memory_space=pltpu.VMEM)]
    aliases = {}
    for a in range(n):
        aliases[a] = 3 * n + a
        aliases[n + a] = 4 * n + a
    operands = [pltpu.with_memory_space_constraint(a, pltpu.HBM) for a in arrs]
    operands += [pltpu.with_memory_space_constraint(lax.empty(s, a.dtype), pltpu.HBM)
                 for s, a in zip(land_shapes, arrs)]
    outs = _pcall(
        body, name=name, out_shape=tuple(out_shape),
        in_specs=[_HBM] * (2 * n) + [pl.BlockSpec(memory_space=pl.ANY)] * n_after, out_specs=tuple(out_specs),
        input_output_aliases=aliases,
        compiler_params=pltpu.CompilerParams(has_side_effects=_DATAFLOW),
    )(*operands, *after)
    handles = []
    for a in range(n):
        handles.append((outs[3 * n + a], outs[4 * n + a], outs[3 * a], outs[3 * a + 1], outs[3 * a + 2],
                        peer_ks[a]))
    return handles, outs[-1]


def _xfer_wait(handles, after, name, gather):
    n = len(handles)
    after = tuple(after) if isinstance(after, (tuple, list)) else (after,)
    peer_ks = [h[5] for h in handles]

    def body(*refs):
        me = _my_index()
        peers = _peers()
        for a in range(n):
            src_ref, land_ref, send_ref, recv_ref, loc_ref = refs[5 * a:5 * a + 5]
            src_own = src_ref if gather else src_ref.at[me]
            pltpu.make_async_copy(src_own, land_ref.at[me], loc_ref).wait()
            for k in peer_ks[a]:
                peer, pidx = peers[k]
                src = src_ref if gather else src_ref.at[pidx]
                cp = pltpu.make_async_remote_copy(
                    src_ref=src, dst_ref=land_ref.at[pidx], send_sem=send_ref.at[k], recv_sem=recv_ref.at[k],
                    device_id=peer, device_id_type=MESH)
                cp.wait_send()
                cp.wait_recv()

    operands, in_specs, out_shape, aliases = [], [], [], {}
    for a, h in enumerate(handles):
        operands += list(h[:5])
        in_specs += [_HBM, _HBM, _SEM, _SEM, _SEM]
        out_shape += [pltpu.HBM(h[0].shape, h[0].dtype), pltpu.HBM(h[1].shape, h[1].dtype)]
        aliases[5 * a] = 2 * a
        aliases[5 * a + 1] = 2 * a + 1
    outs = _pcall(
        body, name=name, out_shape=tuple(out_shape),
        in_specs=in_specs + [pl.BlockSpec(memory_space=pl.ANY)] * len(after),
        out_specs=tuple([_HBM] * (2 * n)), input_output_aliases=aliases,
        compiler_params=pltpu.CompilerParams(has_side_effects=_DATAFLOW),
    )(*operands, *after)
    return [outs[2 * a + 1] for a in range(n)]


def _sibling_forward(lands, name):
    n = len(lands)
    n_fwd = len(_OTHER_CHIPS)

    def body(*refs):
        ins, bufs = refs[:n], refs[n:2 * n]
        send_sems, recv_sems = refs[2 * n:]
        x, y, c = lax.axis_index("x"), lax.axis_index("y"), lax.axis_index("c")
        sibling = (x, y, 1 - c)
        peers = _peers()
        sends = []
        for a in range(n):
            for j, k in enumerate(_OTHER_CHIPS):
                slot = peers[k][1]
                cp = pltpu.make_async_remote_copy(
                    src_ref=ins[a].at[slot], dst_ref=bufs[a].at[slot],
                    send_sem=send_sems.at[a * n_fwd + j], recv_sem=recv_sems.at[a * n_fwd + j],
                    device_id=sibling, device_id_type=MESH)
                cp.start()
                sends.append(cp)
        for a in range(n):
            for j, k in enumerate(_OTHER_CHIPS):
                (px, py, pc), slot = peers[k]
                theirs = 4 * px + 2 * py + (1 - pc)
                pltpu.make_async_remote_copy(
                    src_ref=ins[a].at[slot], dst_ref=bufs[a].at[theirs],
                    send_sem=send_sems.at[a * n_fwd + j], recv_sem=recv_sems.at[a * n_fwd + j],
                    device_id=sibling, device_id_type=MESH).wait_recv()
        for cp in sends:
            cp.wait_send()

    any_spec = pl.BlockSpec(memory_space=pl.ANY)
    outs = _pcall(
        body, name=name, out_shape=[jax.ShapeDtypeStruct(a.shape, a.dtype) for a in lands],
        in_specs=[any_spec] * n, out_specs=[any_spec] * n,
        input_output_aliases={a: a for a in range(n)},
        scratch_shapes=[pltpu.SemaphoreType.DMA((n * n_fwd,)), pltpu.SemaphoreType.DMA((n * n_fwd,))],
        compiler_params=pltpu.CompilerParams(has_side_effects=True),
    )(*lands)
    return list(outs)


_DIMS = {"nn": (((1,), (0,)), ((), ())), "nt": (((1,), (1,)), ((), ())), "tn": (((0,), (0,)), ((), ()))}


def _dot(a, b, mode="nn"):
    return lax.dot_general(a, b, _DIMS[mode], preferred_element_type=F32)


def _mm(a, b, *, mode, grid, a_spec, b_spec, out_shape, out_specs, acc_shape, epilogue, name,
        extra=(), extra_specs=(), after=(), semantics=("parallel", "parallel", "arbitrary")):
    nk = grid[2]
    n_extra = len(extra)
    n_in = 2 + n_extra + len(after)

    def body_single(*refs):
        a_ref, b_ref = refs[0], refs[1]
        epilogue(_dot(a_ref[...], b_ref[...], mode), refs[2:2 + n_extra], refs[n_in:])

    def body_acc(*refs):
        a_ref, b_ref = refs[0], refs[1]
        ex = refs[2:2 + n_extra]
        outs = refs[n_in:-1]
        acc = refs[-1]
        k = pl.program_id(2)

        @pl.when(k == 0)
        def _():
            acc[...] = jnp.zeros_like(acc)

        acc[...] += _dot(a_ref[...], b_ref[...], mode)

        @pl.when(k == nk - 1)
        def _():
            epilogue(acc[...], ex, outs)

    return _pcall(
        body_single if nk == 1 else body_acc, name=name, grid=grid, out_shape=out_shape,
        in_specs=[a_spec, b_spec] + list(extra_specs) + [pl.BlockSpec(memory_space=pl.ANY)] * len(after),
        out_specs=out_specs,
        scratch_shapes=[] if nk == 1 else [pltpu.VMEM(acc_shape, F32)],
        compiler_params=_cparams(semantics),
    )(a, b, *extra, *after)


def _ep_store(dtype):
    def ep(acc, ex, outs):
        outs[0][...] = acc.astype(dtype)
    return ep


def _ep_relu2(acc, ex, outs):
    outs[0][...] = acc.astype(BF16)
    r = jnp.maximum(acc, 0.0)
    outs[1][...] = (r * r).astype(BF16)


def _ep_relu2_bwd(acc, ex, outs):
    u = ex[0][...].astype(F32)
    outs[0][...] = (acc * (2.0 * jnp.maximum(u, 0.0))).astype(BF16)


def _tile(n, want):
    t = min(n, want)
    while n % t:
        t //= 2
    return t


def _mm_nn(a, w, out_dtype, name, tm=2048, tn=1024, tk=1024, epilogue=None, out_dtypes=None, after=()):
    M, K = a.shape
    N = w.shape[1]
    tm, tn, tk = _tile(M, tm), _tile(N, tn), _tile(K, tk)
    out_dtypes = out_dtypes or [out_dtype]
    return _mm(a, w, mode="nn", grid=(M // tm, N // tn, K // tk),
               a_spec=pl.BlockSpec((tm, tk), lambda i, j, k: (i, k)),
               b_spec=pl.BlockSpec((tk, tn), lambda i, j, k: (k, j)),
               out_shape=[jax.ShapeDtypeStruct((M, N), d) for d in out_dtypes],
               out_specs=[pl.BlockSpec((tm, tn), lambda i, j, k: (i, j)) for _ in out_dtypes],
               acc_shape=(tm, tn), epilogue=epilogue or _ep_store(out_dtype), name=name, after=after)


def _mm_nn_blocked(a, wg, name, epilogue, out_dtypes, tm=2048):
    M, K = a.shape
    n = wg.shape[2]
    tm = _tile(M, tm)
    return _mm(a, wg, mode="nn", grid=(M // tm, N_DEV, 1),
               a_spec=pl.BlockSpec((tm, K), lambda i, j, k: (i, 0)),
               b_spec=pl.BlockSpec((None, K, n), lambda i, j, k: (j, 0, 0)),
               out_shape=[jax.ShapeDtypeStruct((M, N_DEV * n), d) for d in out_dtypes],
               out_specs=[pl.BlockSpec((tm, n), lambda i, j, k: (i, j)) for _ in out_dtypes],
               acc_shape=(tm, n), epilogue=epilogue, name=name)


def _mm_nt(a, w, out_dtype, name, tm=2048, tn=1024, tk=1024, epilogue=None, extra=(), extra_specs=(),
           after=()):
    M, K = a.shape
    N = w.shape[0]
    tm, tn, tk = _tile(M, tm), _tile(N, tn), _tile(K, tk)
    if extra and not extra_specs:
        extra_specs = [pl.BlockSpec((tm, tn), lambda i, j, k: (i, j)) for _ in extra]
    return _mm(a, w, mode="nt", grid=(M // tm, N // tn, K // tk),
               a_spec=pl.BlockSpec((tm, tk), lambda i, j, k: (i, k)),
               b_spec=pl.BlockSpec((tn, tk), lambda i, j, k: (j, k)),
               out_shape=[jax.ShapeDtypeStruct((M, N), out_dtype)],
               out_specs=[pl.BlockSpec((tm, tn), lambda i, j, k: (i, j))],
               acc_shape=(tm, tn), epilogue=epilogue or _ep_store(out_dtype), name=name,
               extra=extra, extra_specs=extra_specs, after=after)[0]


def _mm_nt_blocked(a, wg, out_dtype, name, tm=1024, after=()):
    M = a.shape[0]
    kout, n = wg.shape[1], wg.shape[2]
    tm = _tile(M, tm)
    return _mm(a, wg, mode="nt", grid=(M // tm, 1, N_DEV),
               a_spec=pl.BlockSpec((tm, n), lambda i, j, k: (i, k)),
               b_spec=pl.BlockSpec((None, kout, n), lambda i, j, k: (k, 0, 0)),
               out_shape=[jax.ShapeDtypeStruct((M, kout), out_dtype)],
               out_specs=[pl.BlockSpec((tm, kout), lambda i, j, k: (i, 0))],
               acc_shape=(tm, kout), epilogue=_ep_store(out_dtype), name=name, after=after)[0]


def _mm_tn(a, b, out_dtype, name, tm=1024, tn=1024, tk=2048):
    K, M = a.shape
    N = b.shape[1]
    tm, tn, tk = _tile(M, tm), _tile(N, tn), _tile(K, tk)
    return _mm(a, b, mode="tn", grid=(M // tm, N // tn, K // tk),
               a_spec=pl.BlockSpec((tk, tm), lambda i, j, k: (k, i)),
               b_spec=pl.BlockSpec((tk, tn), lambda i, j, k: (k, j)),
               out_shape=[jax.ShapeDtypeStruct((M, N), out_dtype)],
               out_specs=[pl.BlockSpec((tm, tn), lambda i, j, k: (i, j))],
               acc_shape=(tm, tn), epilogue=_ep_store(out_dtype), name=name)[0]


def _mm_tn_blocked(a, b, out_dtype, name, tm=1024, tk=2048):
    K, M = a.shape
    n = b.shape[1] // N_DEV
    tm, tk = _tile(M, tm), _tile(K, tk)
    return _mm(a, b, mode="tn", grid=(M // tm, N_DEV, K // tk),
               a_spec=pl.BlockSpec((tk, tm), lambda i, j, k: (k, i)),
               b_spec=pl.BlockSpec((tk, n), lambda i, j, k: (k, j)),
               out_shape=[jax.ShapeDtypeStruct((N_DEV, M, n), out_dtype)],
               out_specs=[pl.BlockSpec((None, tm, n), lambda i, j, k: (j, i, 0))],
               acc_shape=(tm, n), epilogue=_ep_store(out_dtype), name=name)[0]


def _window_geometry(ws):
    base = [(ws * k // LANES) * LANES for k in range(N_DEV)]
    off = [ws * k - base[k] for k in range(N_DEV)]
    win = -(-(max(off) + ws) // LANES) * LANES
    return base, off, win


def _shards_to_columns(xg, base, off, win, n_out, name, tr=256):
    R, ws = xg.shape[1], xg.shape[2]
    tr = _tile(R, tr)
    nb_win = win // LANES

    def body(x_ref, o_ref, frame_ref):
        written = set()
        frame_ref[...] = jnp.zeros_like(frame_ref)
        for k in range(N_DEV):
            frame_ref[:, 0:ws] = x_ref[k].astype(F32)
            window = frame_ref[...]
            if off[k]:
                window = pltpu.roll(window, off[k], 1)
            for i in range(nb_win):
                b = base[k] // LANES + i
                if b * LANES >= n_out:
                    continue
                cols = slice(b * LANES, (b + 1) * LANES)
                blk = window[:, i * LANES:(i + 1) * LANES]
                if b in written:
                    blk = blk + o_ref[:, cols].astype(F32)
                o_ref[:, cols] = blk.astype(o_ref.dtype)
                written.add(b)
        for b in range(n_out // LANES):
            if b not in written:
                o_ref[:, b * LANES:(b + 1) * LANES] = jnp.zeros((tr, LANES), o_ref.dtype)

    return _pcall(
        body, name=name, grid=(R // tr,), out_shape=jax.ShapeDtypeStruct((R, n_out), xg.dtype),
        in_specs=[pl.BlockSpec((N_DEV, tr, ws), lambda i: (0, i, 0))],
        out_specs=pl.BlockSpec((tr, n_out), lambda i: (i, 0)),
        scratch_shapes=[pltpu.VMEM((tr, win), F32)],
        compiler_params=_cparams(("parallel",)))(xg)


def _columns_to_shards(x, ws, base, off, win, name, tr=256):
    R = x.shape[0]
    tr = _tile(R, tr)

    def body(x_ref, o_ref, frame_ref):
        for k in range(N_DEV):
            window = x_ref[:, base[k]:base[k] + win].astype(F32)
            if off[k]:
                window = pltpu.roll(window, win - off[k], 1)
            frame_ref[...] = window
            o_ref[k] = frame_ref[:, 0:ws].astype(o_ref.dtype)

    return _pcall(
        body, name=name, grid=(R // tr,), out_shape=jax.ShapeDtypeStruct((N_DEV, R, ws), x.dtype),
        in_specs=[pl.BlockSpec((tr, x.shape[1]), lambda i: (i, 0))],
        out_specs=pl.BlockSpec((N_DEV, tr, ws), lambda i: (0, i, 0)),
        scratch_shapes=[pltpu.VMEM((tr, win), F32)],
        compiler_params=_cparams(("parallel",)))(x)


def _sigmoid(x):
    return 1.0 / (1.0 + jnp.exp(-x))


def _row_spec(tm, d):
    return pl.BlockSpec((tm, d), lambda i: (i, 0))


def _vec_spec(d):
    return pl.BlockSpec((1, d), lambda i: (0, 0))


def _norm_mod_fwd(x, y, gate, nw, scale, shift, name, tm=512):
    L, D = x.shape
    tm = _tile(L, tm)
    has_res = y is not None

    def body(*refs):
        if has_res:
            x_ref, y_ref, g_ref, nw_ref, sc_ref, sh_ref, xo_ref, h_ref = refs
            xn = x_ref[...] + g_ref[...] * y_ref[...]
            xo_ref[...] = xn
        else:
            x_ref, nw_ref, sc_ref, sh_ref, h_ref = refs
            xn = x_ref[...]
        rstd = lax.rsqrt(jnp.mean(xn * xn, axis=-1, keepdims=True) + NORM_EPS)
        h = xn * rstd * nw_ref[...] * (1.0 + sc_ref[...]) + sh_ref[...]
        h_ref[...] = h.astype(BF16)

    row, vec = _row_spec(tm, D), _vec_spec(D)
    if has_res:
        ins, in_specs = (x, y, gate, nw, scale, shift), [row, row, vec, vec, vec, vec]
        out_shape = [jax.ShapeDtypeStruct((L, D), F32), jax.ShapeDtypeStruct((L, D), BF16)]
        out_specs = [row, row]
    else:
        ins, in_specs = (x, nw, scale, shift), [row, vec, vec, vec]
        out_shape = [jax.ShapeDtypeStruct((L, D), BF16)]
        out_specs = [row]
    outs = _pcall(body, name=name, grid=(L // tm,), out_shape=out_shape, in_specs=in_specs,
                  out_specs=out_specs, compiler_params=_cparams(("parallel",)))(*ins)
    return outs if has_res else (x, outs[0])


def _gated_branch_bwd(dx, branch, y_ref, g_ref, dy_ref, dg_ref):
    if branch is None:
        return
    dy_ref[...] = (g_ref[...] * dx).astype(BF16)
    dg_ref[...] += jnp.sum(dx * y_ref[...], axis=0, keepdims=True)


def _norm_mod_bwd(dh, x, nw, scale, dres, name, branch=None, tm=512):
    L, D = x.shape
    tm = _tile(L, tm)
    nb = 0 if branch is None else 2

    def body(dh_ref, x_ref, nw_ref, sc_ref, dres_ref, *rest):
        y_ref, g_ref = rest[:nb] if nb else (None, None)
        dx_ref, dsh_ref, dsc_ref, dnw_ref = rest[nb:nb + 4]
        dy_ref, dg_ref = rest[nb + 4:] if nb else (None, None)

        @pl.when(pl.program_id(0) == 0)
        def _():
            dsh_ref[...] = jnp.zeros_like(dsh_ref)
            dsc_ref[...] = jnp.zeros_like(dsc_ref)
            dnw_ref[...] = jnp.zeros_like(dnw_ref)
            if nb:
                dg_ref[...] = jnp.zeros_like(dg_ref)

        xv = x_ref[...]
        dh_v = dh_ref[...]
        nw_v = nw_ref[...]
        rstd = lax.rsqrt(jnp.mean(xv * xv, axis=-1, keepdims=True) + NORM_EPS)
        xhat = xv * rstd
        dsh_ref[...] += jnp.sum(dh_v, axis=0, keepdims=True)
        dsc_ref[...] += jnp.sum(dh_v * (xhat * nw_v), axis=0, keepdims=True)
        dr = dh_v * (1.0 + sc_ref[...])
        dnw_ref[...] += jnp.sum(dr * xhat, axis=0, keepdims=True)
        dxh = dr * nw_v
        dx = rstd * (dxh - xhat * jnp.mean(dxh * xhat, axis=-1, keepdims=True)) + dres_ref[...]
        dx_ref[...] = dx
        _gated_branch_bwd(dx, branch, y_ref, g_ref, dy_ref, dg_ref)

    row, vec = _row_spec(tm, D), _vec_spec(D)
    extra_in = [] if branch is None else list(branch)
    return _pcall(
        body, name=name, grid=(L // tm,),
        out_shape=[jax.ShapeDtypeStruct((L, D), F32)] + [jax.ShapeDtypeStruct((1, D), F32)] * 3
        + ([jax.ShapeDtypeStruct((L, D), BF16), jax.ShapeDtypeStruct((1, D), F32)] if nb else []),
        in_specs=[row, row, vec, vec, row] + ([row, vec] if nb else []),
        out_specs=[row, vec, vec, vec] + ([row, vec] if nb else []),
        compiler_params=_cparams(("arbitrary",)))(dh, x, nw, scale, dres, *extra_in)


def _final_loss(x, y, gate, fw, target, name, tm=512):
    L, D = x.shape
    tm = _tile(L, tm)

    def body(x_ref, y_ref, g_ref, fw_ref, t_ref, dx_ref, loss_ref, dfw_ref, dy_ref, dg_ref):
        @pl.when(pl.program_id(0) == 0)
        def _():
            loss_ref[...] = jnp.zeros_like(loss_ref)
            dfw_ref[...] = jnp.zeros_like(dfw_ref)
            dg_ref[...] = jnp.zeros_like(dg_ref)

        xn = x_ref[...] + g_ref[...] * y_ref[...]
        fw_v = fw_ref[...]
        rstd = lax.rsqrt(jnp.mean(xn * xn, axis=-1, keepdims=True) + NORM_EPS)
        xhat = xn * rstd
        diff = xhat * fw_v - t_ref[...]
        loss_ref[...] += jnp.sum(diff * diff, axis=0, keepdims=True)
        dyf = diff * (1.0 / D)
        dfw_ref[...] += jnp.sum(dyf * xhat, axis=0, keepdims=True)
        dxh = dyf * fw_v
        dx = rstd * (dxh - xhat * jnp.mean(dxh * xhat, axis=-1, keepdims=True))
        dx_ref[...] = dx
        _gated_branch_bwd(dx, True, y_ref, g_ref, dy_ref, dg_ref)

    row, vec = _row_spec(tm, D), _vec_spec(D)
    return _pcall(
        body, name=name, grid=(L // tm,),
        out_shape=[jax.ShapeDtypeStruct((L, D), F32), jax.ShapeDtypeStruct((1, D), F32),
                   jax.ShapeDtypeStruct((1, D), F32), jax.ShapeDtypeStruct((L, D), BF16),
                   jax.ShapeDtypeStruct((1, D), F32)],
        in_specs=[row, row, vec, vec, row], out_specs=[row, vec, vec, row, vec],
        compiler_params=_cparams(("arbitrary",)))(x, y, gate, fw, target)


def _shift_down(v, s, row):
    if s == 0:
        return v
    return jnp.where(row >= s, pltpu.roll(v, s, 0), 0.0)


CONV_ROWS = 32


def _shifted_rows(x_ref, r0, n, lanes=slice(None)):
    cur = x_ref[r0:r0 + CONV_ROWS, lanes]
    if r0 >= n - 1:
        return [cur] + [x_ref[r0 - s:r0 - s + CONV_ROWS, lanes] for s in range(1, n)]
    row = lax.broadcasted_iota(jnp.int32, cur.shape, 0)
    return [_shift_down(cur, s, row) for s in range(n)]


def _ssd_conv_fwd(zx, w, b, col0, width, name, cb=512):
    L = zx.shape[0]
    nb = width // cb
    off = col0 // cb

    def body(x_ref, w_ref, b_ref, o_ref):
        for l0 in range(0, cb, LANES):
            lanes = slice(l0, l0 + LANES)
            taps = [w_ref[k:k + 1, lanes] for k in range(SSD_K)]
            bias = b_ref[:, lanes]
            for r0 in range(0, L, CONV_ROWS):
                taps_in = _shifted_rows(x_ref, r0, SSD_K, lanes)
                acc = bias + taps[SSD_K - 1] * taps_in[0]
                for s in range(1, SSD_K):
                    acc = acc + taps[SSD_K - 1 - s] * taps_in[s]
                o_ref[r0:r0 + CONV_ROWS, lanes] = acc * _sigmoid(acc)

    return _pcall(
        body, name=name, grid=(nb,), out_shape=jax.ShapeDtypeStruct((L, width), F32),
        in_specs=[pl.BlockSpec((L, cb), lambda j: (0, off + j)),
                  pl.BlockSpec((SSD_K, cb), lambda j: (0, j)),
                  pl.BlockSpec((1, cb), lambda j: (0, j))],
        out_specs=pl.BlockSpec((L, cb), lambda j: (0, j)),
        compiler_params=_cparams(("parallel",)))(zx, w, b)


def _ssd_conv_bwd(zx, w, b, d_parts, dzx, col0, name, cb=128):
    L = zx.shape[0]
    widths = [p.shape[1] for p in d_parts]
    width = sum(widths)
    nb = width // cb
    off = col0 // cb
    starts = [sum(widths[:i]) // cb for i in range(len(d_parts))]
    counts = [wd // cb for wd in widths]

    def body(x_ref, w_ref, b_ref, *rest):
        d_refs = rest[:len(d_parts)]
        dx_ref, dw_ref, db_ref, dpre_ref = rest[len(d_parts) + 1:]
        j = pl.program_id(0)
        taps = [w_ref[k:k + 1, :] for k in range(SSD_K)]
        bias = b_ref[...]
        fold = lambda v: sum(v[r:r + 8, :] for r in range(0, CONV_ROWS, 8))
        db8 = jnp.zeros((8, cb), F32)
        dw8 = [jnp.zeros((8, cb), F32) for _ in range(SSD_K)]
        for r0 in range(0, L, CONV_ROWS):
            rows = slice(r0, r0 + CONV_ROWS)
            d_val = d_refs[-1][rows, :]
            for i in range(len(d_parts) - 2, -1, -1):
                d_val = jnp.where(j < starts[i + 1], d_refs[i][rows, :], d_val)
            taps_in = _shifted_rows(x_ref, r0, SSD_K)
            acc = bias + taps[SSD_K - 1] * taps_in[0]
            for s in range(1, SSD_K):
                acc = acc + taps[SSD_K - 1 - s] * taps_in[s]
            sig = _sigmoid(acc)
            dpre = d_val * (sig * (1.0 + acc * (1.0 - sig)))
            dpre_ref[rows, :] = dpre
            db8 = db8 + fold(dpre)
            for s in range(SSD_K):
                dw8[s] = dw8[s] + fold(dpre * taps_in[s])
        dpre_ref[L:L + 8, :] = jnp.zeros((8, cb), F32)
        db_ref[...] = jnp.sum(db8, axis=0, keepdims=True)
        for s in range(SSD_K):
            dw_ref[SSD_K - 1 - s:SSD_K - s, :] = jnp.sum(dw8[s], axis=0, keepdims=True)
        for r0 in range(0, L, CONV_ROWS):
            dx = taps[SSD_K - 1] * dpre_ref[r0:r0 + CONV_ROWS, :]
            for s in range(1, SSD_K):
                dx = dx + taps[SSD_K - 1 - s] * dpre_ref[r0 + s:r0 + s + CONV_ROWS, :]
            dx_ref[r0:r0 + CONV_ROWS, :] = dx.astype(BF16)

    def part_spec(i):
        return pl.BlockSpec((L, cb), lambda j: (0, jnp.clip(j - starts[i], 0, counts[i] - 1)))

    return _pcall(
        body, name=name, grid=(nb,),
        out_shape=[jax.ShapeDtypeStruct(dzx.shape, BF16), jax.ShapeDtypeStruct((SSD_K, width), F32),
                   jax.ShapeDtypeStruct((1, width), F32)],
        in_specs=[pl.BlockSpec((L, cb), lambda j: (0, off + j)),
                  pl.BlockSpec((SSD_K, cb), lambda j: (0, j)),
                  pl.BlockSpec((1, cb), lambda j: (0, j))]
        + [part_spec(i) for i in range(len(d_parts))] + [pl.BlockSpec(memory_space=pl.ANY)],
        out_specs=[pl.BlockSpec((L, cb), lambda j: (0, off + j)),
                   pl.BlockSpec((SSD_K, cb), lambda j: (0, j)),
                   pl.BlockSpec((1, cb), lambda j: (0, j))],
        input_output_aliases={3 + len(d_parts): 0},
        scratch_shapes=[pltpu.VMEM((L + 8, cb), F32)],
        compiler_params=_cparams(("parallel",)))(zx, w, b, *d_parts, dzx)


def _dzx_finish(dzx, ddt, col0, name, tl=512):
    G, L, _ = ddt.shape
    tail = dzx.shape[1] - col0
    tl = _tile(L, tl)

    def body(ddt_ref, dzx_ref, o_ref):
        s = ddt_ref[0]
        for g in range(1, G):
            s = s + ddt_ref[g]
        o_ref[:, 0:LANES] = s.astype(o_ref.dtype)
        if tail > LANES:
            o_ref[:, LANES:] = jnp.zeros((tl, tail - LANES), o_ref.dtype)

    return _pcall(
        body, name=name, grid=(L // tl,), out_shape=jax.ShapeDtypeStruct(dzx.shape, dzx.dtype),
        in_specs=[pl.BlockSpec((G, tl, LANES), lambda i: (0, i, 0)), pl.BlockSpec(memory_space=pl.ANY)],
        out_specs=pl.BlockSpec((tl, tail), lambda i: (i, col0 // tail)),
        input_output_aliases={1: 0},
        compiler_params=_cparams(("parallel",)))(ddt, dzx)


def _sc_conv_fwd(proj, w, name, cb=512):
    L = proj.shape[0]
    width = proj.shape[1] // 3
    nb = width // cb

    def body(b_ref, c_ref, x_ref, w_ref, o_ref):
        for l0 in range(0, cb, LANES):
            lanes = slice(l0, l0 + LANES)
            taps = [w_ref[k:k + 1, lanes] for k in range(SC_K)]
            for r0 in range(0, L, CONV_ROWS):
                rows = slice(r0, r0 + CONV_ROWS)
                q = [c * x for c, x in zip(_shifted_rows(c_ref, r0, SC_K, lanes),
                                           _shifted_rows(x_ref, r0, SC_K, lanes))]
                acc = taps[SC_K - 1] * q[0]
                for s in range(1, SC_K):
                    acc = acc + taps[SC_K - 1 - s] * q[s]
                o_ref[rows, lanes] = (b_ref[rows, lanes] * acc).astype(BF16)

    return _pcall(
        body, name=name, grid=(nb,), out_shape=jax.ShapeDtypeStruct((L, width), BF16),
        in_specs=[pl.BlockSpec((L, cb), lambda j: (0, j)),
                  pl.BlockSpec((L, cb), lambda j: (0, nb + j)),
                  pl.BlockSpec((L, cb), lambda j: (0, 2 * nb + j)),
                  pl.BlockSpec((SC_K, cb), lambda j: (0, j))],
        out_specs=pl.BlockSpec((L, cb), lambda j: (0, j)),
        compiler_params=_cparams(("parallel",)))(proj, proj, proj, w)


def _sc_conv_bwd(proj, w, dy, name, cb=128):
    L = proj.shape[0]
    width = proj.shape[1] // 3
    nb = width // cb

    def body(b_ref, c_ref, x_ref, w_ref, dy_ref, db_ref, dc_ref, dxv_ref, dw_ref, dconv_ref):
        taps = [w_ref[k:k + 1, :] for k in range(SC_K)]
        fold = lambda v: sum(v[r:r + 8, :] for r in range(0, CONV_ROWS, 8))
        dw8 = [jnp.zeros((8, cb), F32) for _ in range(SC_K)]
        for r0 in range(0, L, CONV_ROWS):
            rows = slice(r0, r0 + CONV_ROWS)
            q = [c * x for c, x in zip(_shifted_rows(c_ref, r0, SC_K), _shifted_rows(x_ref, r0, SC_K))]
            conv = taps[SC_K - 1] * q[0]
            for s in range(1, SC_K):
                conv = conv + taps[SC_K - 1 - s] * q[s]
            dyv = dy_ref[rows, :]
            db_ref[rows, :] = (dyv * conv).astype(BF16)
            dconv = dyv * b_ref[rows, :]
            dconv_ref[rows, :] = dconv
            for s in range(SC_K):
                dw8[s] = dw8[s] + fold(dconv * q[s])
        dconv_ref[L:L + 8, :] = jnp.zeros((8, cb), F32)
        for s in range(SC_K):
            dw_ref[SC_K - 1 - s:SC_K - s, :] = jnp.sum(dw8[s], axis=0, keepdims=True)
        for r0 in range(0, L, CONV_ROWS):
            rows = slice(r0, r0 + CONV_ROWS)
            dq = taps[SC_K - 1] * dconv_ref[rows, :]
            for s in range(1, SC_K):
                dq = dq + taps[SC_K - 1 - s] * dconv_ref[r0 + s:r0 + s + CONV_ROWS, :]
            dc_ref[rows, :] = (dq * x_ref[rows, :]).astype(BF16)
            dxv_ref[rows, :] = (dq * c_ref[rows, :]).astype(BF16)

    blk = pl.BlockSpec((L, cb), lambda j: (0, j))
    wblk = pl.BlockSpec((SC_K, cb), lambda j: (0, j))
    return _pcall(
        body, name=name, grid=(nb,),
        out_shape=[jax.ShapeDtypeStruct((L, width), BF16)] * 3 + [jax.ShapeDtypeStruct((SC_K, width), F32)],
        in_specs=[blk, pl.BlockSpec((L, cb), lambda j: (0, nb + j)),
                  pl.BlockSpec((L, cb), lambda j: (0, 2 * nb + j)), wblk, blk],
        out_specs=[blk, blk, blk, wblk], scratch_shapes=[pltpu.VMEM((L + 8, cb), F32)],
        compiler_params=_cparams(("parallel",)))(proj, proj, proj, w, dy)


def _split3(v):
    hi = v.astype(BF16)
    r1 = v - hi.astype(F32)
    mid = r1.astype(BF16)
    lo = (r1 - mid.astype(F32)).astype(BF16)
    return hi, mid, lo


def _dot_exact01(t01, v):
    hi, mid, lo = _split3(v)
    return _dot(t01, hi) + _dot(t01, mid) + _dot(t01, lo)


def _lane_col(v, lane, h):
    return jnp.sum(jnp.where(lane == h, v, 0.0), axis=1, keepdims=True)


def _sum_all(v):
    return jnp.sum(jnp.sum(v, axis=1, keepdims=True), axis=0, keepdims=True)


def _softplus(x):
    return jnp.maximum(x, 0.0) + jnp.log1p(jnp.exp(-jnp.abs(x)))


def _ssd_decay(zx, bias_p, alog_p, n_heads, dt_block, name):
    L = zx.shape[0]
    nc = L // SSD_CHUNK
    per_step = 4 if nc % 4 == 0 else 1
    rows_step = per_step * SSD_CHUNK

    def body(raw_ref, bias_ref, alog_ref, dt_ref, sg_ref, cs_ref, cst_ref, last_ref):
        lane = lax.broadcasted_iota(jnp.int32, (SSD_CHUNK, LANES), 1)
        row = lax.broadcasted_iota(jnp.int32, (SSD_CHUNK, LANES), 0)
        valid = lane < n_heads
        tri = (row >= lane).astype(BF16)
        a_row = -jnp.exp(alog_ref[...])
        for i in range(per_step):
            rows = slice(i * SSD_CHUNK, (i + 1) * SSD_CHUNK)
            raw = raw_ref[rows, :] + bias_ref[...]
            dt = jnp.where(valid, _softplus(raw), 0.0)
            a = dt * a_row
            cs = _dot_exact01(tri, a)
            dt_ref[rows, :] = dt
            sg_ref[rows, :] = _sigmoid(raw)
            cs_ref[rows, :] = cs
            cst_ref[i] = cs.T
            last_ref[i] = jnp.sum(a, axis=0, keepdims=True)

    blk = pl.BlockSpec((rows_step, LANES), lambda c: (c, 0))
    head_vec = pl.BlockSpec((1, LANES), lambda c: (0, 0))
    return _pcall(
        body, name=name, grid=(nc // per_step,),
        out_shape=[jax.ShapeDtypeStruct((L, LANES), F32)] * 3
        + [jax.ShapeDtypeStruct((nc, SSD_CHUNK, LANES), F32), jax.ShapeDtypeStruct((nc, 1, LANES), F32)],
        in_specs=[pl.BlockSpec((rows_step, LANES), lambda c: (c, dt_block)), head_vec, head_vec],
        out_specs=[blk, blk, blk, pl.BlockSpec((per_step, SSD_CHUNK, LANES), lambda c: (c, 0, 0)),
                   pl.BlockSpec((per_step, 1, LANES), lambda c: (c, 0, 0))],
        compiler_params=_cparams(("parallel",)))(zx, bias_p, alog_p)


def _ssd_common(dt_ref, cs_ref, last_ref, b_ref, c_ref):
    c_sz = SSD_CHUNK
    lane = lax.broadcasted_iota(jnp.int32, (c_sz, LANES), 1)
    row = lax.broadcasted_iota(jnp.int32, (c_sz, LANES), 0)
    bb = b_ref[...].astype(BF16)
    cb = c_ref[...].astype(BF16)
    scores = _dot(cb, bb, "nt")
    return dict(lane=lane, row=row, dt=dt_ref[...], cs=cs_ref[...], last_row=last_ref[...], bb=bb, cb=cb,
                scores=scores, causal=row >= lane, lo=lane < SSD_P)


def _pair_terms(q, cst_ref, h0):
    lane, lo = q["lane"], q["lo"]
    out = {}
    cols, dts, lasts, lms = [], [], [], []
    lane1 = lax.broadcasted_iota(jnp.int32, (1, LANES), 1)
    for h in (h0, h0 + 1):
        col = _lane_col(q["cs"], lane, h)
        rowv = cst_ref[pl.ds(h, 1), :]
        lms.append(jnp.exp(jnp.where(q["causal"], col - rowv, -1e30)))
        cols.append(col)
        dts.append(_lane_col(q["dt"], lane, h))
        lasts.append(jnp.sum(jnp.where(lane1 == h, q["last_row"], 0.0), axis=1, keepdims=True))
    out["lm"] = lms
    out["cols"] = cols
    out["lasts"] = lasts
    out["dt_b"] = jnp.where(lo, dts[0], dts[1])
    out["e_b"] = jnp.where(lo, jnp.exp(cols[0]), jnp.exp(cols[1]))
    out["dec_cols"] = [jnp.exp(lasts[0] - cols[0]), jnp.exp(lasts[1] - cols[1])]
    out["dec_b"] = jnp.where(lo, out["dec_cols"][0], out["dec_cols"][1])
    lo1 = lane1 < SSD_P
    out["explast"] = [jnp.exp(lasts[0]), jnp.exp(lasts[1])]
    out["explast_b"] = jnp.where(lo1, out["explast"][0], out["explast"][1])
    return out


def _ssd_fwd(zx, xc, decay, d_lane, nw, d_inner, after, name):
    L = zx.shape[0]
    nc = L // SSD_CHUNK
    gw = d_inner // SSD_G
    heads = gw // SSD_P
    n_pair = heads // 2
    bc0 = d_inner // LANES

    def body(z_ref, xs_ref, b_ref, c_ref, dt_ref, cs_ref, cst_ref, last_ref, dl_ref, nw_ref, after_ref,
             y_ref, yn_ref, prev_ref, s_ref):
        @pl.when(pl.program_id(1) == 0)
        def _():
            s_ref[...] = jnp.zeros_like(s_ref)

        q = _ssd_common(dt_ref, cs_ref, last_ref, b_ref, c_ref)
        prev_ref[...] = s_ref[...]
        lo = q["lo"]
        for j in range(n_pair):
            sl = slice(j * LANES, (j + 1) * LANES)
            p = _pair_terms(q, cst_ref, pl.program_id(0) * heads + 2 * j)
            xs_p = xs_ref[:, sl]
            xp = xs_p * p["dt_b"]
            xb = xp.astype(BF16)
            m_a = (q["scores"] * p["lm"][0]).astype(BF16)
            m_b = (q["scores"] * p["lm"][1]).astype(BF16)
            yd = jnp.where(lo, _dot(m_a, xb), _dot(m_b, xb))
            s_p = s_ref[:, sl]
            yo = _dot(q["cb"], s_p.astype(BF16)) * p["e_b"]
            y_ref[:, sl] = yd + yo + dl_ref[:, sl] * xs_p
            st = _dot(q["bb"], (xp * p["dec_b"]).astype(BF16), "tn")
            s_ref[:, sl] = s_p * p["explast_b"] + st
        yv = y_ref[...]
        zv = z_ref[...]
        yg = yv * (zv * _sigmoid(zv))
        rstd = lax.rsqrt(jnp.mean(yg * yg, axis=-1, keepdims=True) + NORM_EPS)
        yn_ref[...] = (yg * rstd * nw_ref[...]).astype(BF16)

    grp = lambda width: pl.BlockSpec((None, 1, width), lambda g, c: (g, 0, 0))
    dt_, _, cs_, cst_, last_ = decay
    return _pcall(
        body, name=name, grid=(SSD_G, nc),
        out_shape=[jax.ShapeDtypeStruct((L, d_inner), F32), jax.ShapeDtypeStruct((L, d_inner), BF16),
                   jax.ShapeDtypeStruct((nc, SSD_G, SSD_N, gw), F32)],
        in_specs=[pl.BlockSpec((SSD_CHUNK, gw), lambda g, c: (c, g)),
                  pl.BlockSpec((SSD_CHUNK, gw), lambda g, c: (c, g)),
                  pl.BlockSpec((SSD_CHUNK, SSD_N), lambda g, c: (c, bc0 + g)),
                  pl.BlockSpec((SSD_CHUNK, SSD_N), lambda g, c: (c, bc0 + SSD_G + g)),
                  pl.BlockSpec((SSD_CHUNK, LANES), lambda g, c: (c, 0)),
                  pl.BlockSpec((SSD_CHUNK, LANES), lambda g, c: (c, 0)),
                  pl.BlockSpec((None, SSD_CHUNK, LANES), lambda g, c: (c, 0, 0)),
                  pl.BlockSpec((None, 1, LANES), lambda g, c: (c, 0, 0)),
                  grp(gw), grp(gw), pl.BlockSpec(memory_space=pl.ANY)],
        out_specs=[pl.BlockSpec((SSD_CHUNK, gw), lambda g, c: (c, g)),
                   pl.BlockSpec((SSD_CHUNK, gw), lambda g, c: (c, g)),
                   pl.BlockSpec((None, None, SSD_N, gw), lambda g, c: (c, g, 0, 0))],
        scratch_shapes=[pltpu.VMEM((SSD_N, gw), F32)],
        compiler_params=_cparams(("parallel", "arbitrary")))(
            zx, xc, xc, xc, dt_, cs_, cst_, last_, d_lane, nw, after)


def _ssd_bwd(dyn, y, zx, xc, prev, decay, alog_p, d_lane, nw, d_inner, name):
    L = zx.shape[0]
    nc = L // SSD_CHUNK
    gw = d_inner // SSD_G
    heads = gw // SSD_P
    n_pair = heads // 2
    bc0 = d_inner // LANES

    def body(dyn_ref, y_ref, z_ref, xs_ref, b_ref, c_ref, prev_ref, dt_ref, sg_ref, cs_ref, cst_ref, last_ref,
             alog_ref, dl_ref, nw_ref,
             dz_ref, dxs_ref, db_ref, dc_ref, ddt_ref, dbias_ref, dalog_ref, dd_ref, dnw_ref,
             ds_ref, racc_ref):
        @pl.when(pl.program_id(1) == 0)
        def _():
            ds_ref[...] = jnp.zeros_like(ds_ref)
            dbias_ref[...] = jnp.zeros_like(dbias_ref)
            dalog_ref[...] = jnp.zeros_like(dalog_ref)
            dd_ref[...] = jnp.zeros_like(dd_ref)
            dnw_ref[...] = jnp.zeros_like(dnw_ref)

        q = _ssd_common(dt_ref, cs_ref, last_ref, b_ref, c_ref)
        a_row = -jnp.exp(alog_ref[...])
        lane, row, lo = q["lane"], q["row"], q["lo"]
        lane1 = lax.broadcasted_iota(jnp.int32, (1, LANES), 1)
        head0 = pl.program_id(0) * heads
        mine = (lane >= head0) & (lane < head0 + heads)

        yv, zv, dynv, nwv = y_ref[...], z_ref[...], dyn_ref[...], nw_ref[...]
        sig = _sigmoid(zv)
        sz = zv * sig
        yg = yv * sz
        rstd = lax.rsqrt(jnp.mean(yg * yg, axis=-1, keepdims=True) + NORM_EPS)
        yhat = yg * rstd
        dnw_ref[...] += jnp.sum(dynv * yhat, axis=0, keepdims=True)
        dyh = dynv * nwv
        dyg = rstd * (dyh - yhat * jnp.mean(dyh * yhat, axis=-1, keepdims=True))
        dz_ref[...] = (dyg * yv * (sig * (1.0 + zv * (1.0 - sig)))).astype(BF16)
        dy_all = dyg * sz

        dg = jnp.zeros((SSD_CHUNK, SSD_CHUNK), F32)
        dc_acc = jnp.zeros((SSD_CHUNK, SSD_N), F32)
        db_acc = jnp.zeros((SSD_CHUNK, SSD_N), F32)
        dcs_mat = jnp.zeros((SSD_CHUNK, LANES), F32)
        ddt_mat = jnp.zeros((SSD_CHUNK, LANES), F32)
        dd_row = jnp.zeros((1, LANES), F32)
        racc_ref[...] = jnp.zeros_like(racc_ref)
        is_last = row == SSD_CHUNK - 1

        for j in range(n_pair):
            sl = slice(j * LANES, (j + 1) * LANES)
            ha, hb = head0 + 2 * j, head0 + 2 * j + 1
            p = _pair_terms(q, cst_ref, ha)
            xs_p = xs_ref[:, sl]
            dyp = dy_all[:, sl]
            xp = xs_p * p["dt_b"]
            xb = xp.astype(BF16)
            s_p = prev_ref[:, sl]
            s_pb = s_p.astype(BF16)
            dsn = ds_ref[:, sl]
            dsnb = dsn.astype(BF16)
            m_f = [q["scores"] * p["lm"][0], q["scores"] * p["lm"][1]]

            t0 = dyp * xs_p
            dd_row = dd_row + jnp.where(lane1 == ha, _sum_all(jnp.where(lo, t0, 0.0)), 0.0) \
                + jnp.where(lane1 == hb, _sum_all(jnp.where(lo, 0.0, t0)), 0.0)
            dxs_p = dl_ref[:, sl] * dyp

            yo = _dot(q["cb"], s_pb) * p["e_b"]
            dcs_b = (dyp * p["e_b"]).astype(BF16)
            dc_acc = dc_acc + _dot(dcs_b, s_pb, "nt")
            ds_yo = _dot(q["cb"], dcs_b, "tn")
            t1 = dyp * yo
            dcs_cols = [jnp.sum(jnp.where(lo, t1, 0.0), axis=1, keepdims=True),
                        jnp.sum(jnp.where(lo, 0.0, t1), axis=1, keepdims=True)]

            t2 = dsn * s_p
            dlast = [p["explast"][0] * _sum_all(jnp.where(lo, t2, 0.0)),
                     p["explast"][1] * _sum_all(jnp.where(lo, 0.0, t2))]
            ds_ref[:, sl] = dsn * p["explast_b"] + ds_yo
            w = _dot(q["bb"], dsnb)
            db_acc = db_acc + _dot((xp * p["dec_b"]).astype(BF16), dsnb, "nt")
            dxp = w * p["dec_b"]
            t3 = w * xp
            e = [jnp.sum(jnp.where(lo, t3, 0.0), axis=1, keepdims=True) * p["dec_cols"][0],
                 jnp.sum(jnp.where(lo, 0.0, t3), axis=1, keepdims=True) * p["dec_cols"][1]]
            for i in range(2):
                dlast[i] = dlast[i] + jnp.sum(e[i], axis=0, keepdims=True)
                dcs_cols[i] = dcs_cols[i] - e[i]

            dyb = dyp.astype(BF16)
            dy_h = [jnp.where(lo, dyp, 0.0).astype(BF16), jnp.where(lo, 0.0, dyp).astype(BF16)]
            dms = [_dot(dy_h[0], xb, "nt"), _dot(dy_h[1], xb, "nt")]
            dxp = dxp + jnp.where(lo, _dot(m_f[0].astype(BF16), dyb, "tn"), _dot(m_f[1].astype(BF16), dyb, "tn"))
            for i, h in enumerate((ha, hb)):
                dg = dg + dms[i] * p["lm"][i]
                qm = dms[i] * m_f[i]
                dcs_cols[i] = dcs_cols[i] + jnp.sum(qm, axis=1, keepdims=True)
                racc_ref[pl.ds(h, 1), :] = jnp.sum(qm, axis=0, keepdims=True)

            dxs_ref[:, sl] = dxs_p + dxp * p["dt_b"]
            t4 = dxp * xs_p
            ddt_cols = [jnp.sum(jnp.where(lo, t4, 0.0), axis=1, keepdims=True),
                        jnp.sum(jnp.where(lo, 0.0, t4), axis=1, keepdims=True)]
            for i, h in enumerate((ha, hb)):
                sel = lane == h
                dcs_mat = dcs_mat + jnp.where(sel, dcs_cols[i], 0.0) + jnp.where(sel & is_last, dlast[i], 0.0)
                ddt_mat = ddt_mat + jnp.where(sel, ddt_cols[i], 0.0)

        dcs_mat = dcs_mat - racc_ref[...].T
        tri_t = (row <= lane).astype(BF16)
        da = _dot_exact01(tri_t, dcs_mat)
        ddt = ddt_mat + da * a_row
        dalog_ref[...] += jnp.sum(jnp.where(mine, da * q["dt"], 0.0), axis=0, keepdims=True) * a_row
        draw = jnp.where(mine, ddt * sg_ref[...], 0.0)
        ddt_ref[...] = draw
        dbias_ref[...] += jnp.sum(draw, axis=0, keepdims=True)
        dd_ref[...] += dd_row
        dgb = dg.astype(BF16)
        dc_ref[...] = dc_acc + _dot(dgb, q["bb"])
        db_ref[...] = db_acc + _dot(dgb, q["cb"], "tn")

    rev = lambda c: nc - 1 - c
    grp = lambda width: pl.BlockSpec((None, 1, width), lambda g, c: (g, 0, 0))
    blk = lambda width, off: pl.BlockSpec((SSD_CHUNK, width), lambda g, c: (rev(c), off + g))
    head_vec = pl.BlockSpec((1, LANES), lambda g, c: (0, 0))
    chunk_rows = pl.BlockSpec((SSD_CHUNK, LANES), lambda g, c: (rev(c), 0))
    dt_, sg_, cs_, cst_, last_ = decay
    return _pcall(
        body, name=name, grid=(SSD_G, nc),
        out_shape=[jax.ShapeDtypeStruct(zx.shape, BF16), jax.ShapeDtypeStruct((L, d_inner), F32),
                   jax.ShapeDtypeStruct((L, SSD_G * SSD_N), F32), jax.ShapeDtypeStruct((L, SSD_G * SSD_N), F32),
                   jax.ShapeDtypeStruct((SSD_G, L, LANES), F32),
                   jax.ShapeDtypeStruct((SSD_G, 1, LANES), F32), jax.ShapeDtypeStruct((SSD_G, 1, LANES), F32),
                   jax.ShapeDtypeStruct((SSD_G, 1, LANES), F32), jax.ShapeDtypeStruct((SSD_G, 1, gw), F32)],
        in_specs=[blk(gw, 0), blk(gw, 0), blk(gw, 0), blk(gw, 0), blk(SSD_N, bc0), blk(SSD_N, bc0 + SSD_G),
                  pl.BlockSpec((None, None, SSD_N, gw), lambda g, c: (rev(c), g, 0, 0)),
                  chunk_rows, chunk_rows, chunk_rows,
                  pl.BlockSpec((None, SSD_CHUNK, LANES), lambda g, c: (rev(c), 0, 0)),
                  pl.BlockSpec((None, 1, LANES), lambda g, c: (rev(c), 0, 0)),
                  head_vec, grp(gw), grp(gw)],
        out_specs=[blk(gw, 0), blk(gw, 0), blk(SSD_N, 0), blk(SSD_N, 0),
                   pl.BlockSpec((None, SSD_CHUNK, LANES), lambda g, c: (g, rev(c), 0)),
                   grp(LANES), grp(LANES), grp(LANES), grp(gw)],
        scratch_shapes=[pltpu.VMEM((SSD_N, gw), F32), pltpu.VMEM((SSD_CHUNK, LANES), F32)],
        compiler_params=_cparams(("parallel", "arbitrary")))(
            dyn, y, zx, xc, xc, xc, prev, dt_, sg_, cs_, cst_, last_, alog_p, d_lane, nw)


def _cond_mod(c_pad, ada_w, ada_b_loc, after, name):
    depth, D, n = ada_w.shape
    rows = c_pad.shape[0]

    def body(c_ref, w_ref, b_ref, after_ref, mod_ref, cond_ref):
        cv = c_ref[...]
        cond = cv * _sigmoid(cv)
        cond_ref[...] = cond
        mod_ref[...] = _dot(cond.astype(BF16), w_ref[...].astype(BF16)) + b_ref[...]

    return _pcall(
        body, name=name, grid=(depth,),
        out_shape=[jax.ShapeDtypeStruct((depth, rows, n), F32), jax.ShapeDtypeStruct((rows, D), F32)],
        in_specs=[pl.BlockSpec((rows, D), lambda i: (0, 0)),
                  pl.BlockSpec((None, D, n), lambda i: (i, 0, 0)),
                  pl.BlockSpec((None, 1, n), lambda i: (i, 0, 0)),
                  pl.BlockSpec(memory_space=pl.ANY)],
        out_specs=[pl.BlockSpec((None, rows, n), lambda i: (i, 0, 0)),
                   pl.BlockSpec((rows, D), lambda i: (0, 0))],
        compiler_params=_cparams(("arbitrary",)))(c_pad, ada_w, ada_b_loc, after)


def _adamw_math(g, w, m, v):
    m_new = ADAM_B1 * m + (1.0 - ADAM_B1) * g
    v_new = ADAM_B2 * v + (1.0 - ADAM_B2) * (g * g)
    m_hat = m_new / (1.0 - ADAM_B1 ** ADAM_STEP)
    v_hat = v_new / (1.0 - ADAM_B2 ** ADAM_STEP)
    delta = -ADAM_LR * (m_hat / (jnp.sqrt(v_hat) + ADAM_EPS) + ADAM_WD * w)
    return delta, m_new, v_new


def _adamw_sum(parts, w, m, v, layer, name, prev=None, tr=None):
    depth, R, C = w.shape
    tr = _tile(R, tr if tr is not None else (512 if C <= 512 else 256))

    def body(p_ref, w_ref, m_ref, v_ref, *rest):
        g_ref, d_ref, mo_ref, vo_ref = rest[-4:]
        g = p_ref[0].astype(F32)
        for k in range(1, N_DEV):
            g = g + p_ref[k].astype(F32)
        d, mn, vn = _adamw_math(g, w_ref[...], m_ref[...], v_ref[...])
        g_ref[...] = g
        d_ref[...] = d
        mo_ref[...] = mn
        vo_ref[...] = vn

    blk = pl.BlockSpec((None, tr, C), lambda i: (layer, i, 0))
    prev = list(prev) if prev is not None else []
    return _pcall(
        body, name=name, grid=(R // tr,),
        out_shape=[jax.ShapeDtypeStruct((depth, R, C), F32)] * 4,
        in_specs=[pl.BlockSpec((N_DEV, tr, C), lambda i: (0, i, 0)), blk, blk, blk]
        + [pl.BlockSpec(memory_space=pl.ANY)] * len(prev),
        out_specs=[blk] * 4, input_output_aliases={4 + k: k for k in range(len(prev))},
        compiler_params=_cparams(("parallel",)))(parts, w, m, v, *prev)


def _adamw_small(parts, wmv, head_parts, head_wmv, loss_parts, name):
    n, nh = len(parts), len(head_parts)
    n_heads = head_wmv[0][0].shape[1] if nh else 0
    groups = head_parts[0].shape[1] if nh else 0
    d_model = loss_parts.shape[2]

    def body(*refs):
        p_refs, refs = refs[:n], refs[n:]
        wmv_refs, refs = refs[:3 * n], refs[3 * n:]
        hp_refs, refs = refs[:nh], refs[nh:]
        hwmv_refs, refs = refs[:3 * nh], refs[3 * nh:]
        loss_ref, refs = refs[0], refs[1:]
        outs, loss_out, head_scr = refs[:4 * (n + nh)], refs[4 * (n + nh)], refs[4 * (n + nh) + 1]

        def update(i, g, w_ref, m_ref, v_ref):
            res = (g,) + _adamw_math(g, w_ref[...], m_ref[...], v_ref[...])
            for o_ref, r in zip(outs[4 * i:4 * i + 4], res):
                o_ref[...] = r

        for i in range(n):
            g = p_refs[i][0]
            for k in range(1, N_DEV):
                g = g + p_refs[i][k]
            update(i, g, *wmv_refs[3 * i:3 * i + 3])
        for i in range(nh):
            g = None
            for k in range(N_DEV):
                for grp in range(groups):
                    g = hp_refs[i][k, grp] if g is None else g + hp_refs[i][k, grp]
            head_scr[...] = g
            update(n + i, head_scr[:, 0:n_heads], *hwmv_refs[3 * i:3 * i + 3])
        tot = loss_ref[0]
        for k in range(1, N_DEV):
            tot = tot + loss_ref[k]
        loss_out[...] = jnp.broadcast_to(_sum_all(tot) * (0.5 / d_model), loss_out.shape)

    operands = list(parts) + [a for t in wmv for a in t] + list(head_parts) + [a for t in head_wmv for a in t]
    operands.append(loss_parts)
    out_shape = [jax.ShapeDtypeStruct(t[0].shape, F32) for t in list(wmv) + list(head_wmv) for _ in range(4)]
    out_shape.append(jax.ShapeDtypeStruct((1, LANES), F32))
    vmem = pl.BlockSpec(memory_space=pltpu.VMEM)
    outs = _pcall(body, name=name, out_shape=out_shape, in_specs=[vmem] * len(operands),
                  out_specs=[vmem] * len(out_shape), scratch_shapes=[pltpu.VMEM((1, LANES), F32)],
                  compiler_params=_cparams())(*operands)
    return [outs[4 * i:4 * i + 4] for i in range(n + nh)], outs[-1]


def _ada_adamw(cond_pad, dmod_pad, w, m, v, name, tr=512):
    depth, D, n = w.shape
    rows = cond_pad.shape[0]
    tr = _tile(D, tr)

    def body(c_ref, dm_ref, w_ref, m_ref, v_ref, g_ref, d_ref, mo_ref, vo_ref):
        g = _dot(c_ref[...].astype(BF16), dm_ref[...].astype(BF16), "tn")
        d, mn, vn = _adamw_math(g, w_ref[...], m_ref[...], v_ref[...])
        g_ref[...] = g
        d_ref[...] = d
        mo_ref[...] = mn
        vo_ref[...] = vn

    blk = pl.BlockSpec((None, tr, n), lambda i, r: (i, r, 0))
    return _pcall(
        body, name=name, grid=(depth, D // tr),
        out_shape=[jax.ShapeDtypeStruct((depth, D, n), F32)] * 4,
        in_specs=[pl.BlockSpec((rows, tr), lambda i, r: (0, r)),
                  pl.BlockSpec((None, rows, n), lambda i, r: (i, 0, 0)), blk, blk, blk],
        out_specs=[blk] * 4, compiler_params=_cparams(("parallel", "parallel")))(cond_pad, dmod_pad, w, m, v)


def kernel(x, c, ada_w, ada_b, mix_norm_w, mlp_norm_w, mlp_up, mlp_down, ssd_in_w, ssd_conv_w, ssd_conv_b, ssd_dt_bias, ssd_A_log, ssd_D, ssd_norm_w, ssd_out_w, sc_in_w, sc_conv_w, sc_out_w, final_norm_w, loss_target, m_ada_w, m_ada_b, m_mix_norm_w, m_mlp_norm_w, m_mlp_up, m_mlp_down, m_ssd_in_w, m_ssd_conv_w, m_ssd_conv_b, m_ssd_dt_bias, m_ssd_A_log, m_ssd_D, m_ssd_norm_w, m_ssd_out_w, m_sc_in_w, m_sc_conv_w, m_sc_out_w, m_final_norm_w, v_ada_w, v_ada_b, v_mix_norm_w, v_mlp_norm_w, v_mlp_up, v_mlp_down, v_ssd_in_w, v_ssd_conv_w, v_ssd_conv_b, v_ssd_dt_bias, v_ssd_A_log, v_ssd_D, v_ssd_norm_w, v_ssd_out_w, v_sc_in_w, v_sc_conv_w, v_sc_out_w, v_final_norm_w):
    weights = dict(ada_w=ada_w, ada_b=ada_b, mix_norm_w=mix_norm_w, mlp_norm_w=mlp_norm_w, mlp_up=mlp_up,
                   mlp_down=mlp_down, ssd_in_w=ssd_in_w, ssd_conv_w=ssd_conv_w, ssd_conv_b=ssd_conv_b,
                   ssd_dt_bias=ssd_dt_bias, ssd_A_log=ssd_A_log, ssd_D=ssd_D, ssd_norm_w=ssd_norm_w,
                   ssd_out_w=ssd_out_w, sc_in_w=sc_in_w, sc_conv_w=sc_conv_w, sc_out_w=sc_out_w,
                   final_norm_w=final_norm_w)
    moms = dict(ada_w=m_ada_w, ada_b=m_ada_b, mix_norm_w=m_mix_norm_w, mlp_norm_w=m_mlp_norm_w, mlp_up=m_mlp_up,
                mlp_down=m_mlp_down, ssd_in_w=m_ssd_in_w, ssd_conv_w=m_ssd_conv_w, ssd_conv_b=m_ssd_conv_b,
                ssd_dt_bias=m_ssd_dt_bias, ssd_A_log=m_ssd_A_log, ssd_D=m_ssd_D, ssd_norm_w=m_ssd_norm_w,
                ssd_out_w=m_ssd_out_w, sc_in_w=m_sc_in_w, sc_conv_w=m_sc_conv_w, sc_out_w=m_sc_out_w,
                final_norm_w=m_final_norm_w)
    vars_ = dict(ada_w=v_ada_w, ada_b=v_ada_b, mix_norm_w=v_mix_norm_w, mlp_norm_w=v_mlp_norm_w, mlp_up=v_mlp_up,
                 mlp_down=v_mlp_down, ssd_in_w=v_ssd_in_w, ssd_conv_w=v_ssd_conv_w, ssd_conv_b=v_ssd_conv_b,
                 ssd_dt_bias=v_ssd_dt_bias, ssd_A_log=v_ssd_A_log, ssd_D=v_ssd_D, ssd_norm_w=v_ssd_norm_w,
                 ssd_out_w=v_ssd_out_w, sc_in_w=v_sc_in_w, sc_conv_w=v_sc_conv_w, sc_out_w=v_sc_out_w,
                 final_norm_w=v_final_norm_w)
    names = list(weights)

    L, D = x.shape[1], x.shape[2]
    d_inner = 2 * D
    n_heads = d_inner // SSD_P
    hpg = n_heads // SSD_G
    gw = d_inner // SSD_G
    conv_dim = d_inner + 2 * SSD_G * SSD_N
    zx_dim = d_inner + conv_dim
    zx_pad = -(-(zx_dim + LANES) // 512) * 512
    in_ws = ssd_in_w.shape[2]
    in_base, in_off, in_win = _window_geometry(in_ws)
    me = _my_index()
    x0 = x[0]
    tgt = loss_target[0]

    n_mod = ada_w.shape[2]
    (c_all,) = _exchange([c], "gather_c", gather=True)
    gather_handle = {}
    (gather_handle["ssd_in_w"],), token_in = _xfer_start(
        [ssd_in_w[0].astype(BF16)], "gather_start_ssd_in_w", gather=True, via_sibling=(0,), after=(c_all,))
    c_pad = jnp.pad(c_all.reshape(N_DEV, D), ((0, 16 - N_DEV), (0, 0)))
    ada_b_loc = lax.dynamic_slice_in_dim(ada_b, me * n_mod, n_mod, axis=1).reshape(2, 1, n_mod)
    mod_blk, cond_pad = _cond_mod(c_pad, ada_w, ada_b_loc, token_in, "cond_mod")
    gather_order = ["mod", "ssd_conv_w", "sc_conv_w", "ssd_out_w", "up0", "down0", "sc_in_w", "sc_out_w", "up1",
                    "down1"]
    gather_src = dict(mod=mod_blk, ssd_conv_w=ssd_conv_w[0], sc_conv_w=sc_conv_w[0],
                      ssd_out_w=ssd_out_w[0].astype(BF16),
                      up0=mlp_up[0].astype(BF16), down0=mlp_down[0].astype(BF16),
                      sc_in_w=sc_in_w[0].astype(BF16), sc_out_w=sc_out_w[0].astype(BF16),
                      up1=mlp_up[1].astype(BF16), down1=mlp_down[1].astype(BF16))
    handles, gather_token = _xfer_start([gather_src[k] for k in gather_order], "gather_start", gather=True,
                                        via_sibling=tuple(range(3, len(gather_order))))
    gather_handle.update(zip(gather_order, handles))

    def gathered(keys, after, forward):
        tag = "_".join(keys)
        lands = _xfer_wait([gather_handle[k] for k in keys], after, f"gather_wait_{tag}", gather=True)
        return _sibling_forward(lands, f"gather_forward_{tag}") if forward else lands

    def forward_behind(keys, after):
        tag = "_".join(keys)
        lands = _xfer_wait([gather_handle[k] for k in keys], after, f"gather_wait_{tag}", gather=True)
        fwd_handles, token = _sibling_forward_start(lands, f"gather_forward_start_{tag}")
        return (lambda done: _sibling_forward_wait(fwd_handles, done, f"gather_forward_wait_{tag}")), token

    (ssd_in_g,) = gathered(["ssd_in_w"], (gather_token, m_ssd_in_w, v_ssd_in_w), True)
    w_in_all = _shards_to_columns(ssd_in_g, in_base, in_off, in_win, zx_pad, "ssd_in_w_columns")
    (mod_all,) = gathered(["mod"], w_in_all, False)
    mod_mine = lax.dynamic_index_in_dim(mod_all, me, axis=2, keepdims=False)
    mod_mine = jnp.transpose(mod_mine, (1, 0, 2)).reshape(2, 6, 1, D)
    sh_m, sc_m, g_m, sh_f, sc_f, g_f = [[mod_mine[i, k] for i in range(2)] for k in range(6)]

    vec = lambda a: a.reshape(1, -1)
    small = {}

    _, h0 = _norm_mod_fwd(x0, None, None, vec(mix_norm_w[0]), sc_m[0], sh_m[0], "l0_mix_norm")
    cw_all, scw_all = gathered(["ssd_conv_w", "sc_conv_w"], h0, False)
    (zx,) = _mm_nn(h0, w_in_all, F32, "ssd_in_proj", tm=2048, tn=512)
    conv_b0 = vec(ssd_conv_b[0])
    conv_w_full = jnp.transpose(cw_all, (1, 0, 2)).reshape(SSD_K, conv_dim)
    sc_conv_full = jnp.transpose(scw_all, (1, 0, 2)).reshape(SC_K, D)
    xc = _ssd_conv_fwd(zx, conv_w_full, conv_b0, d_inner, conv_dim, "ssd_conv")
    bias_p = jnp.pad(ssd_dt_bias[0], (0, LANES - n_heads)).reshape(1, LANES)
    alog_p = jnp.pad(ssd_A_log[0], (0, LANES - n_heads)).reshape(1, LANES)
    d_lane = jnp.repeat(ssd_D[0], SSD_P).reshape(SSD_G, 1, gw)
    nw_g = ssd_norm_w[0].reshape(SSD_G, 1, gw)
    finish, token = forward_behind(["ssd_out_w"], xc)
    decay = _ssd_decay(zx, bias_p, alog_p, n_heads, zx_dim // LANES, "ssd_decay")
    y_ssd, yn, prev = _ssd_fwd(zx, xc, decay, d_lane, nw_g, d_inner, token, "ssd_scan")
    ups, downs = [None, None], [None, None]
    (ssd_out_g,) = finish(yn)
    w_ssd_out = ssd_out_g.reshape(-1, D)
    finish, token = forward_behind(["up0", "down0"], ssd_out_g)
    (mix0,) = _mm_nn(yn, w_ssd_out, F32, "ssd_out_proj", after=(token,))
    x1, h1 = _norm_mod_fwd(x0, mix0, g_m[0], vec(mlp_norm_w[0]), sc_f[0], sh_f[0], "l0_mlp_norm")
    ups[0], down0_g = finish(h1)
    downs[0] = down0_g.reshape(-1, D)
    u0, s0 = _mm_nn_blocked(h1, ups[0], "l0_mlp_up", _ep_relu2, [BF16, BF16])
    finish, token = forward_behind(["sc_in_w", "sc_out_w", "up1", "down1"], s0)
    (d0,) = _mm_nn(s0, downs[0], F32, "l0_mlp_down", after=(token,))
    x2, h2 = _norm_mod_fwd(x1, d0, g_f[0], vec(mix_norm_w[1]), sc_m[1], sh_m[1], "l1_mix_norm")
    sc_in_g, sc_out_g, ups[1], down1_g = finish(h2)
    w_sc_out, downs[1] = sc_out_g.reshape(-1, D), down1_g.reshape(-1, D)
    (proj,) = _mm_nn_blocked(h2, sc_in_g, "sc_in_proj", _ep_store(F32), [F32])
    yc = _sc_conv_fwd(proj, sc_conv_full, "sc_conv")
    (mix1,) = _mm_nn(yc, w_sc_out, F32, "sc_out_proj")
    x3, h3 = _norm_mod_fwd(x2, mix1, g_m[1], vec(mlp_norm_w[1]), sc_f[1], sh_f[1], "l1_mlp_norm")
    u1, s1 = _mm_nn_blocked(h3, ups[1], "l1_mlp_up", _ep_relu2, [BF16, BF16])
    (d1,) = _mm_nn(s1, downs[1], F32, "l1_mlp_down")

    dx, loss_lane, dfw, dd1, dg = _final_loss(x3, d1, g_f[1], vec(final_norm_w), tgt, "final_loss")
    small["final_norm_w"] = dfw

    dmod = [[None] * 6 for _ in range(2)]
    dmod[1][5] = dg

    def mlp_backward(i, dx_out, dd, x_mid, h_in, u, s, mix, gate):
        gdown = _mm_tn(s, dd, BF16, f"l{i}_mlp_down_wgrad").reshape(N_DEV, -1, D)
        if i == len(ups) - 1:
            (grad_handle[f"mlp_down{i}"],), token = _xfer_start([gdown], f"l{i}_mlp_down_grad_start", gather=False)
            du = _mm_nt(dd, downs[i], BF16, f"l{i}_mlp_down_bwd", epilogue=_ep_relu2_bwd, extra=(u,),
                        after=(token,))
            gup = _mm_tn_blocked(h_in, du, BF16, f"l{i}_mlp_up_wgrad")
            (grad_handle[f"mlp_up{i}"],), token = _xfer_start([gup], f"l{i}_mlp_up_grad_start", gather=False)
        else:
            du = _mm_nt(dd, downs[i], BF16, f"l{i}_mlp_down_bwd", epilogue=_ep_relu2_bwd, extra=(u,))
            gup = _mm_tn_blocked(h_in, du, BF16, f"l{i}_mlp_up_wgrad")
            (h_down, h_up), token = _xfer_start([gdown, gup], f"l{i}_mlp_grads_start", gather=False)
            grad_handle[f"mlp_down{i}"], grad_handle[f"mlp_up{i}"] = h_down, h_up
        dh = _mm_nt_blocked(du, ups[i], F32, f"l{i}_mlp_up_bwd", after=(token,))
        dxm, dsh, dsc, dnw, dmix, dgate = _norm_mod_bwd(dh, x_mid, vec(mlp_norm_w[i]), sc_f[i], dx_out,
                                                        f"l{i}_mlp_norm_bwd", branch=(mix, gate))
        dmod[i][3], dmod[i][4], dmod[i][2] = dsh, dsc, dgate
        return dxm, dmix, dnw

    grad_handle = {}
    dx3, dyc, dnw_mlp1 = mlp_backward(1, dx, dd1, x3, h3, u1, s1, mix1, g_m[1])
    g_sc_out = _mm_tn(yc, dyc, BF16, "sc_out_wgrad").reshape(N_DEV, -1, D)
    dconv_out = _mm_nt(dyc, w_sc_out, F32, "sc_out_bwd")
    dbg, dcg, dxv, dscw = _sc_conv_bwd(proj, sc_conv_full, dconv_out, "sc_conv_bwd")
    dproj = jnp.concatenate([dbg, dcg, dxv], axis=1)
    g_sc_in = _mm_tn_blocked(h2, dproj, BF16, "sc_in_wgrad")
    (grad_handle["sc_out_w0"], grad_handle["sc_in_w0"]), token = _xfer_start(
        [g_sc_out, g_sc_in], "sc_grads_start", gather=False)
    dh2 = _mm_nt_blocked(dproj, sc_in_g, F32, "sc_in_bwd", after=(token,))
    dx2, dsh, dsc, dnw_mix1, dd0, dg = _norm_mod_bwd(dh2, x2, vec(mix_norm_w[1]), sc_m[1], dx3, "l1_mix_norm_bwd",
                                                     branch=(d0, g_f[0]))
    dmod[1][0], dmod[1][1], dmod[0][5] = dsh, dsc, dg
    dx1, dyo, dnw_mlp0 = mlp_backward(0, dx2, dd0, x1, h1, u0, s0, mix0, g_m[0])
    g_ssd_out = _mm_tn(yn, dyo, BF16, "ssd_out_wgrad").reshape(N_DEV, -1, D)
    (grad_handle["ssd_out_w0"],), token = _xfer_start([g_ssd_out], "ssd_out_grad_start", gather=False)
    dyn = _mm_nt(dyo, w_ssd_out, F32, "ssd_out_bwd", after=(token,))
    dz, dxs, db_, dc_, ddt, dbias, dalog, dd_, dnw_ssd = _ssd_bwd(
        dyn, y_ssd, zx, xc, prev, decay, alog_p, d_lane, nw_g, d_inner, "ssd_scan_bwd")
    dzx, dcw, dcb = _ssd_conv_bwd(zx, conv_w_full, conv_b0, [dxs, db_, dc_], dz, d_inner, "ssd_conv_bwd")
    dzx = _dzx_finish(dzx, ddt, zx_dim, "ssd_dzx_finish")
    g_in_all = _mm_tn(h0, dzx, BF16, "ssd_in_wgrad", tn=512, tk=2048)
    g_ssd_in = _columns_to_shards(g_in_all, in_ws, in_base, in_off, in_win, "ssd_in_wgrad_shards")
    (grad_handle["ssd_in_w0"],), token = _xfer_start([g_ssd_in], "ssd_in_grad_start", gather=False)
    dh0 = _mm_nt(dzx, w_in_all, F32, "ssd_in_bwd", tm=1024, tk=dzx.shape[1] // 2, after=(token,))
    grad_x, dsh, dsc, dnw_mix0 = _norm_mod_bwd(dh0, x0, vec(mix_norm_w[0]), sc_m[0], dx1, "l0_mix_norm_bwd")
    dmod[0][0], dmod[0][1] = dsh, dsc

    small["ada_b"] = jnp.concatenate([jnp.concatenate(dmod[i], axis=1) for i in range(2)], axis=0)
    small["mix_norm_w"] = jnp.concatenate([dnw_mix0, dnw_mix1], axis=0)
    small["mlp_norm_w"] = jnp.concatenate([dnw_mlp0, dnw_mlp1], axis=0)
    small["ssd_conv_w"] = dcw
    small["ssd_conv_b"] = dcb
    small["ssd_norm_w"] = dnw_ssd.reshape(1, d_inner)
    small["sc_conv_w"] = dscw
    small["loss"] = loss_lane
    small_names = list(small)
    head_names = ["ssd_dt_bias", "ssd_A_log", "ssd_D"]
    handles, small_token = _xfer_start([small[k] for k in small_names] + [dbias, dalog, dd_],
                                       "small_grads_start", gather=True)

    out_g, out_d, out_m, out_v = {}, {}, {}, {}

    layer_res = {}

    def big_update(name, i, after):
        (parts,) = _xfer_wait([grad_handle[f"{name}{i}"]], after, f"grads_wait_{name}_{i}", gather=False)
        res = _adamw_sum(parts, weights[name], moms[name], vars_[name], i, f"adamw_{name}_{i}",
                         prev=layer_res.get(name))
        layer_res[name] = res
        return res[1]

    chain = small_token
    for name, i in [("mlp_down", 1), ("mlp_up", 1), ("sc_out_w", 0), ("sc_in_w", 0), ("mlp_down", 0),
                    ("mlp_up", 0), ("ssd_out_w", 0), ("ssd_in_w", 0)]:
        chain = big_update(name, i, chain)
    gathered_small = _xfer_wait(handles, chain, "small_grads_wait", gather=True)
    small_all = dict(zip(small_names + head_names, gathered_small))

    dmod_loc = lax.dynamic_slice_in_dim(small_all["ada_b"], me * n_mod, n_mod, axis=2)
    dmod_pad = jnp.pad(jnp.transpose(dmod_loc, (1, 0, 2)), ((0, 0), (0, 16 - N_DEV), (0, 0)))
    out_g["ada_w"], out_d["ada_w"], out_m["ada_w"], out_v["ada_w"] = _ada_adamw(
        cond_pad, dmod_pad, ada_w, m_ada_w, v_ada_w, "adamw_ada_w")

    for k in ("ssd_conv_w", "sc_conv_w"):
        n_loc = weights[k].shape[2]
        small_all[k] = lax.dynamic_slice_in_dim(small_all[k], me * n_loc, n_loc, axis=2)
    plain = [k for k in small_names if k != "loss"]
    as2d = lambda a: a.reshape(-1, a.shape[-1])
    res, loss_row = _adamw_small(
        [small_all[k] for k in plain], [tuple(as2d(d[k]) for d in (weights, moms, vars_)) for k in plain],
        [small_all[k] for k in head_names], [tuple(as2d(d[k]) for d in (weights, moms, vars_)) for k in head_names],
        small_all["loss"], "adamw_small")
    loss = loss_row[0, 0]
    for k, res4 in zip(plain + head_names, res):
        for r, dst in zip(res4, (out_g, out_d, out_m, out_v)):
            dst[k] = r.reshape(weights[k].shape)
    for name, res4 in layer_res.items():
        for r, dst in zip(res4, (out_g, out_d, out_m, out_v)):
            dst[name] = r

    return (loss, grad_x[None], *[out_g[k] for k in names], *[out_d[k] for k in names],
            *[out_m[k] for k in names], *[out_v[k] for k in names])
```

```python
import jax
import jax.numpy as jnp
from jax import lax
from jax.experimental import pallas as pl
from jax.experimental.pallas import tpu as pltpu

F32 = jnp.float32
BF16 = jnp.bfloat16
N_DEV = 8
MESH = pl.DeviceIdType.MESH

NORM_EPS = 1e-5
SSD_G = 4
SSD_P = 64
SSD_N = 128
SSD_CHUNK = 128
SSD_K = 4
SC_K = 3
LANES = 128

ADAM_LR = 0.001
ADAM_B1 = 0.9
ADAM_B2 = 0.999
ADAM_EPS = 1e-08
ADAM_WD = 0.01
ADAM_STEP = 10

VMEM_LIMIT = 56 * 1024 * 1024


def _pcall(body, **kw):
    return pl.pallas_call(body, **kw)


def _cparams(sem=None):
    if sem is None:
        return pltpu.CompilerParams(vmem_limit_bytes=VMEM_LIMIT)
    return pltpu.CompilerParams(dimension_semantics=sem, vmem_limit_bytes=VMEM_LIMIT)


def _my_index():
    return 4 * lax.axis_index("x") + 2 * lax.axis_index("y") + lax.axis_index("c")


_PEER_MASKS = [(0, 0, 1), (0, 1, 0), (0, 1, 1), (1, 0, 0), (1, 0, 1), (1, 1, 0), (1, 1, 1)]


def _peers():
    x, y, c = lax.axis_index("x"), lax.axis_index("y"), lax.axis_index("c")
    out = []
    for mx, my, mc in _PEER_MASKS:
        px = (1 - x) if mx else x
        py = (1 - y) if my else y
        pc = (1 - c) if mc else c
        out.append(((px, py, pc), 4 * px + 2 * py + pc))
    return out


def _exchange(arrs, name, gather):
    n = len(arrs)
    n_peer = N_DEV - 1

    def body(*refs):
        ins, outs = refs[:n], refs[n:2 * n]
        send_sems, recv_sems, local_sems = refs[2 * n:]
        me = _my_index()
        peers = _peers()
        started = []
        for a in range(n):
            src_own = ins[a] if gather else ins[a].at[me]
            own = pltpu.make_async_copy(src_own, outs[a].at[me], local_sems.at[a])
            own.start()
            started.append(own)
        sends = []
        for a in range(n):
            for k, (peer, pidx) in enumerate(peers):
                src = ins[a] if gather else ins[a].at[pidx]
                cp = pltpu.make_async_remote_copy(
                    src_ref=src, dst_ref=outs[a].at[me],
                    send_sem=send_sems.at[a * n_peer + k], recv_sem=recv_sems.at[a * n_peer + k],
                    device_id=peer, device_id_type=MESH)
                cp.start()
                sends.append(cp)
        for a in range(n):
            for k, (peer, pidx) in enumerate(peers):
                src = ins[a] if gather else ins[a].at[pidx]
                pltpu.make_async_remote_copy(
                    src_ref=src, dst_ref=outs[a].at[pidx],
                    send_sem=send_sems.at[a * n_peer + k], recv_sem=recv_sems.at[a * n_peer + k],
                    device_id=peer, device_id_type=MESH).wait_recv()
        for cp in sends:
            cp.wait_send()
        for own in started:
            own.wait()

    if gather:
        out_shape = [jax.ShapeDtypeStruct((N_DEV,) + a.shape, a.dtype) for a in arrs]
    else:
        out_shape = [jax.ShapeDtypeStruct(a.shape, a.dtype) for a in arrs]
    any_spec = pl.BlockSpec(memory_space=pl.ANY)
    outs = _pcall(
        body, name=name, out_shape=out_shape,
        in_specs=[any_spec] * n, out_specs=[any_spec] * n,
        scratch_shapes=[pltpu.SemaphoreType.DMA((n * n_peer,)), pltpu.SemaphoreType.DMA((n * n_peer,)),
                        pltpu.SemaphoreType.DMA((n,))],
        compiler_params=pltpu.CompilerParams(has_side_effects=True),
    )(*arrs)
    return list(outs)


def _sibling_forward_start(lands, name):
    n = len(lands)
    n_fwd = len(_OTHER_CHIPS)

    def body(*refs):
        ins, bufs = refs[:n], refs[3 * n:4 * n]
        token = refs[-1]
        sibling = (lax.axis_index("x"), lax.axis_index("y"), 1 - lax.axis_index("c"))
        peers = _peers()
        for a in range(n):
            send_sems, recv_sems = refs[n + 2 * a], refs[n + 2 * a + 1]
            for j, k in enumerate(_OTHER_CHIPS):
                slot = peers[k][1]
                pltpu.make_async_remote_copy(
                    src_ref=ins[a].at[slot], dst_ref=bufs[a].at[slot], send_sem=send_sems.at[j],
                    recv_sem=recv_sems.at[j], device_id=sibling, device_id_type=MESH).start()
        token[...] = jnp.zeros_like(token)

    out_shape, out_specs = [], []
    for _ in range(n):
        out_shape += [pltpu.SemaphoreType.DMA((n_fwd,)), pltpu.SemaphoreType.DMA((n_fwd,))]
        out_specs += [_SEM, _SEM]
    out_shape += [pltpu.HBM(a.shape, a.dtype) for a in lands] + [jax.ShapeDtypeStruct((8, LANES), F32)]
    out_specs += [_HBM] * n + [pl.BlockSpec(memory_space=pltpu.VMEM)]
    outs = _pcall(
        body, name=name, out_shape=tuple(out_shape), in_specs=[_HBM] * n, out_specs=tuple(out_specs),
        input_output_aliases={a: 2 * n + a for a in range(n)},
        compiler_params=pltpu.CompilerParams(has_side_effects=_DATAFLOW),
    )(*[pltpu.with_memory_space_constraint(a, pltpu.HBM) for a in lands])
    return [(outs[2 * n + a], outs[2 * a], outs[2 * a + 1]) for a in range(n)], outs[-1]


def _sibling_forward_wait(handles, after, name):
    n = len(handles)

    def body(*refs):
        sibling = (lax.axis_index("x"), lax.axis_index("y"), 1 - lax.axis_index("c"))
        peers = _peers()
        for a in range(n):
            buf, send_sems, recv_sems = refs[3 * a:3 * a + 3]
            for j, k in enumerate(_OTHER_CHIPS):
                (px, py, pc), slot = peers[k]
                theirs = 4 * px + 2 * py + (1 - pc)
                cp = pltpu.make_async_remote_copy(
                    src_ref=buf.at[slot], dst_ref=buf.at[theirs], send_sem=send_sems.at[j],
                    recv_sem=recv_sems.at[j], device_id=sibling, device_id_type=MESH)
                cp.wait_send()
                cp.wait_recv()

    operands, in_specs = [], []
    for h in handles:
        operands += list(h)
        in_specs += [_HBM, _SEM, _SEM]
    outs = _pcall(
        body, name=name, out_shape=tuple(pltpu.HBM(h[0].shape, h[0].dtype) for h in handles),
        in_specs=in_specs + [pl.BlockSpec(memory_space=pl.ANY)], out_specs=tuple([_HBM] * n),
        input_output_aliases={3 * a: a for a in range(n)},
        compiler_params=pltpu.CompilerParams(has_side_effects=_DATAFLOW),
    )(*operands, after)
    return list(outs)


_HBM = pl.BlockSpec(memory_space=pltpu.HBM)
_SEM = pl.BlockSpec(memory_space=pltpu.SEMAPHORE)
_DATAFLOW = pltpu.SideEffectType.DATAFLOW_SIDE_EFFECTING


_ALL_PEERS = tuple(range(N_DEV - 1))
_SAME_CORE_PEERS = (0, 1, 3, 5)
_OTHER_CHIPS = (1, 3, 5)


def _xfer_start(arrs, name, gather, via_sibling=(), after=()):
    n = len(arrs)
    n_peer = N_DEV - 1
    n_after = len(after)
    peer_ks = [_SAME_CORE_PEERS if a in via_sibling else _ALL_PEERS for a in range(n)]

    def body(*refs):
        ins, lands = refs[:n], refs[n:2 * n]
        sems = refs[2 * n + n_after:5 * n + n_after]
        token = refs[-1]
        me = _my_index()
        peers = _peers()
        for a in range(n):
            send_sems, recv_sems, loc_sem = sems[3 * a:3 * a + 3]
            src_own = ins[a] if gather else ins[a].at[me]
            pltpu.make_async_copy(src_own, lands[a].at[me], loc_sem).start()
            for k in peer_ks[a]:
                peer, pidx = peers[k]
                src = ins[a] if gather else ins[a].at[pidx]
                pltpu.make_async_remote_copy(
                    src_ref=src, dst_ref=lands[a].at[me], send_sem=send_sems.at[k], recv_sem=recv_sems.at[k],
                    device_id=peer, device_id_type=MESH).start()
        token[...] = jnp.zeros_like(token)

    land_shapes = [((N_DEV,) + a.shape) if gather else a.shape for a in arrs]
    out_shape, out_specs = [], []
    for _ in range(n):
        out_shape += [pltpu.SemaphoreType.DMA((n_peer,)), pltpu.SemaphoreType.DMA((n_peer,)),
                      pltpu.SemaphoreType.DMA(())]
        out_specs += [_SEM, _SEM, _SEM]
    out_shape += [pltpu.HBM(a.shape, a.dtype) for a in arrs]
    out_shape += [pltpu.HBM(s, a.dtype) for s, a in zip(land_shapes, arrs)]
    out_shape += [jax.ShapeDtypeStruct((8, LANES), F32)]
    out_specs += [_HBM] * (2 * n) + [pl.BlockSpec(memory_space=pltpu.VMEM)]
    aliases = {}
    for a in range(n):
        aliases[a] = 3 * n + a
        aliases[n + a] = 4 * n + a
    operands = [pltpu.with_memory_space_constraint(a, pltpu.HBM) for a in arrs]
    operands += [pltpu.with_memory_space_constraint(lax.empty(s, a.dtype), pltpu.HBM)
                 for s, a in zip(land_shapes, arrs)]
    outs = _pcall(
        body, name=name, out_shape=tuple(out_shape),
        in_specs=[_HBM] * (2 * n) + [pl.BlockSpec(memory_space=pl.ANY)] * n_after, out_specs=tuple(out_specs),
        input_output_aliases=aliases,
        compiler_params=pltpu.CompilerParams(has_side_effects=_DATAFLOW),
    )(*operands, *after)
    handles = []
    for a in range(n):
        handles.append((outs[3 * n + a], outs[4 * n + a], outs[3 * a], outs[3 * a + 1], outs[3 * a + 2],
                        peer_ks[a]))
    return handles, outs[-1]


def _xfer_wait(handles, after, name, gather):
    n = len(handles)
    after = tuple(after) if isinstance(after, (tuple, list)) else (after,)
    peer_ks = [h[5] for h in handles]

    def body(*refs):
        me = _my_index()
        peers = _peers()
        for a in range(n):
            src_ref, land_ref, send_ref, recv_ref, loc_ref = refs[5 * a:5 * a + 5]
            src_own = src_ref if gather else src_ref.at[me]
            pltpu.make_async_copy(src_own, land_ref.at[me], loc_ref).wait()
            for k in peer_ks[a]:
                peer, pidx = peers[k]
                src = src_ref if gather else src_ref.at[pidx]
                cp = pltpu.make_async_remote_copy(
                    src_ref=src, dst_ref=land_ref.at[pidx], send_sem=send_ref.at[k], recv_sem=recv_ref.at[k],
                    device_id=peer, device_id_type=MESH)
                cp.wait_send()
                cp.wait_recv()

    operands, in_specs, out_shape, aliases = [], [], [], {}
    for a, h in enumerate(handles):
        operands += list(h[:5])
        in_specs += [_HBM, _HBM, _SEM, _SEM, _SEM]
        out_shape += [pltpu.HBM(h[0].shape, h[0].dtype), pltpu.HBM(h[1].shape, h[1].dtype)]
        aliases[5 * a] = 2 * a
        aliases[5 * a + 1] = 2 * a + 1
    outs = _pcall(
        body, name=name, out_shape=tuple(out_shape),
        in_specs=in_specs + [pl.BlockSpec(memory_space=pl.ANY)] * len(after),
        out_specs=tuple([_HBM] * (2 * n)), input_output_aliases=aliases,
        compiler_params=pltpu.CompilerParams(has_side_effects=_DATAFLOW),
    )(*operands, *after)
    return [outs[2 * a + 1] for a in range(n)]


def _sibling_forward(lands, name):
    n = len(lands)
    n_fwd = len(_OTHER_CHIPS)

    def body(*refs):
        ins, bufs = refs[:n], refs[n:2 * n]
        send_sems, recv_sems = refs[2 * n:]
        x, y, c = lax.axis_index("x"), lax.axis_index("y"), lax.axis_index("c")
        sibling = (x, y, 1 - c)
        peers = _peers()
        sends = []
        for a in range(n):
            for j, k in enumerate(_OTHER_CHIPS):
                slot = peers[k][1]
                cp = pltpu.make_async_remote_copy(
                    src_ref=ins[a].at[slot], dst_ref=bufs[a].at[slot],
                    send_sem=send_sems.at[a * n_fwd + j], recv_sem=recv_sems.at[a * n_fwd + j],
                    device_id=sibling, device_id_type=MESH)
                cp.start()
                sends.append(cp)
        for a in range(n):
            for j, k in enumerate(_OTHER_CHIPS):
                (px, py, pc), slot = peers[k]
                theirs = 4 * px + 2 * py + (1 - pc)
                pltpu.make_async_remote_copy(
                    src_ref=ins[a].at[slot], dst_ref=bufs[a].at[theirs],
                    send_sem=send_sems.at[a * n_fwd + j], recv_sem=recv_sems.at[a * n_fwd + j],
                    device_id=sibling, device_id_type=MESH).wait_recv()
        for cp in sends:
            cp.wait_send()

    any_spec = pl.BlockSpec(memory_space=pl.ANY)
    outs = _pcall(
        body, name=name, out_shape=[jax.ShapeDtypeStruct(a.shape, a.dtype) for a in lands],
        in_specs=[any_spec] * n, out_specs=[any_spec] * n,
        input_output_aliases={a: a for a in range(n)},
        scratch_shapes=[pltpu.SemaphoreType.DMA((n * n_fwd,)), pltpu.SemaphoreType.DMA((n * n_fwd,))],
        compiler_params=pltpu.CompilerParams(has_side_effects=True),
    )(*lands)
    return list(outs)


_DIMS = {"nn": (((1,), (0,)), ((), ())), "nt": (((1,), (1,)), ((), ())), "tn": (((0,), (0,)), ((), ()))}


def _dot(a, b, mode="nn"):
    if mode == "nt_pair":
        n = b.shape[2]
        return _dot(a[:, :n], b[0], "nt") + _dot(a[:, n:], b[1], "nt")
    return lax.dot_general(a, b, _DIMS[mode], preferred_element_type=F32)


def _mm(a, b, *, mode, grid, a_spec, b_spec, out_shape, out_specs, acc_shape, epilogue, name,
        extra=(), extra_specs=(), after=(), semantics=("parallel", "parallel", "arbitrary")):
    nk = grid[2]
    n_extra = len(extra)
    n_in = 2 + n_extra + len(after)

    def body_single(*refs):
        a_ref, b_ref = refs[0], refs[1]
        epilogue(_dot(a_ref[...], b_ref[...], mode), refs[2:2 + n_extra], refs[n_in:])

    def body_acc(*refs):
        a_ref, b_ref = refs[0], refs[1]
        ex = refs[2:2 + n_extra]
        outs = refs[n_in:-1]
        acc = refs[-1]
        k = pl.program_id(2)

        @pl.when(k == 0)
        def _():
            acc[...] = jnp.zeros_like(acc)

        acc[...] += _dot(a_ref[...], b_ref[...], mode)

        @pl.when(k == nk - 1)
        def _():
            epilogue(acc[...], ex, outs)

    return _pcall(
        body_single if nk == 1 else body_acc, name=name, grid=grid, out_shape=out_shape,
        in_specs=[a_spec, b_spec] + list(extra_specs) + [pl.BlockSpec(memory_space=pl.ANY)] * len(after),
        out_specs=out_specs,
        scratch_shapes=[] if nk == 1 else [pltpu.VMEM(acc_shape, F32)],
        compiler_params=_cparams(semantics),
    )(a, b, *extra, *after)


def _ep_store(dtype):
    def ep(acc, ex, outs):
        outs[0][...] = acc.astype(dtype)
    return ep


def _ep_relu2(acc, ex, outs):
    outs[0][...] = acc.astype(BF16)
    r = jnp.maximum(acc, 0.0)
    outs[1][...] = (r * r).astype(BF16)


def _ep_relu2_bwd(acc, ex, outs):
    u = ex[0][...].astype(F32)
    outs[0][...] = (acc * (2.0 * jnp.maximum(u, 0.0))).astype(BF16)


def _tile(n, want):
    t = min(n, want)
    while n % t:
        t //= 2
    return t


def _mm_nn(a, w, out_dtype, name, tm=2048, tn=1024, tk=1024, epilogue=None, out_dtypes=None, after=()):
    M, K = a.shape
    N = w.shape[1]
    tm, tn, tk = _tile(M, tm), _tile(N, tn), _tile(K, tk)
    out_dtypes = out_dtypes or [out_dtype]
    return _mm(a, w, mode="nn", grid=(M // tm, N // tn, K // tk),
               a_spec=pl.BlockSpec((tm, tk), lambda i, j, k: (i, k)),
               b_spec=pl.BlockSpec((tk, tn), lambda i, j, k: (k, j)),
               out_shape=[jax.ShapeDtypeStruct((M, N), d) for d in out_dtypes],
               out_specs=[pl.BlockSpec((tm, tn), lambda i, j, k: (i, j)) for _ in out_dtypes],
               acc_shape=(tm, tn), epilogue=epilogue or _ep_store(out_dtype), name=name, after=after)


def _mm_nn_blocked(a, wg, name, epilogue, out_dtypes, tm=2048):
    M, K = a.shape
    n = wg.shape[2]
    tm = _tile(M, tm)
    return _mm(a, wg, mode="nn", grid=(M // tm, N_DEV, 1),
               a_spec=pl.BlockSpec((tm, K), lambda i, j, k: (i, 0)),
               b_spec=pl.BlockSpec((None, K, n), lambda i, j, k: (j, 0, 0)),
               out_shape=[jax.ShapeDtypeStruct((M, N_DEV * n), d) for d in out_dtypes],
               out_specs=[pl.BlockSpec((tm, n), lambda i, j, k: (i, j)) for _ in out_dtypes],
               acc_shape=(tm, n), epilogue=epilogue, name=name)


def _mm_nt(a, w, out_dtype, name, tm=2048, tn=1024, tk=1024, epilogue=None, extra=(), extra_specs=(),
           after=()):
    M, K = a.shape
    N = w.shape[0]
    tm, tn, tk = _tile(M, tm), _tile(N, tn), _tile(K, tk)
    if extra and not extra_specs:
        extra_specs = [pl.BlockSpec((tm, tn), lambda i, j, k: (i, j)) for _ in extra]
    return _mm(a, w, mode="nt", grid=(M // tm, N // tn, K // tk),
               a_spec=pl.BlockSpec((tm, tk), lambda i, j, k: (i, k)),
               b_spec=pl.BlockSpec((tn, tk), lambda i, j, k: (j, k)),
               out_shape=[jax.ShapeDtypeStruct((M, N), out_dtype)],
               out_specs=[pl.BlockSpec((tm, tn), lambda i, j, k: (i, j))],
               acc_shape=(tm, tn), epilogue=epilogue or _ep_store(out_dtype), name=name,
               extra=extra, extra_specs=extra_specs, after=after)[0]


def _mm_nt_blocked(a, wg, out_dtype, name, tm=1024, after=()):
    M = a.shape[0]
    kout, n = wg.shape[1], wg.shape[2]
    tm = _tile(M, tm)
    return _mm(a, wg, mode="nt_pair", grid=(M // tm, 1, N_DEV // 2),
               a_spec=pl.BlockSpec((tm, 2 * n), lambda i, j, k: (i, k)),
               b_spec=pl.BlockSpec((2, kout, n), lambda i, j, k: (k, 0, 0)),
               out_shape=[jax.ShapeDtypeStruct((M, kout), out_dtype)],
               out_specs=[pl.BlockSpec((tm, kout), lambda i, j, k: (i, 0))],
               acc_shape=(tm, kout), epilogue=_ep_store(out_dtype), name=name, after=after)[0]


def _mm_tn(a, b, out_dtype, name, tm=1024, tn=1024, tk=2048):
    K, M = a.shape
    N = b.shape[1]
    tm, tn, tk = _tile(M, tm), _tile(N, tn), _tile(K, tk)
    return _mm(a, b, mode="tn", grid=(M // tm, N // tn, K // tk),
               a_spec=pl.BlockSpec((tk, tm), lambda i, j, k: (k, i)),
               b_spec=pl.BlockSpec((tk, tn), lambda i, j, k: (k, j)),
               out_shape=[jax.ShapeDtypeStruct((M, N), out_dtype)],
               out_specs=[pl.BlockSpec((tm, tn), lambda i, j, k: (i, j))],
               acc_shape=(tm, tn), epilogue=_ep_store(out_dtype), name=name)[0]


def _mm_tn_blocked(a, b, out_dtype, name, tm=1024, tk=2048):
    K, M = a.shape
    n = b.shape[1] // N_DEV
    tm, tk = _tile(M, tm), _tile(K, tk)
    return _mm(a, b, mode="tn", grid=(M // tm, N_DEV, K // tk),
               a_spec=pl.BlockSpec((tk, tm), lambda i, j, k: (k, i)),
               b_spec=pl.BlockSpec((tk, n), lambda i, j, k: (k, j)),
               out_shape=[jax.ShapeDtypeStruct((N_DEV, M, n), out_dtype)],
               out_specs=[pl.BlockSpec((None, tm, n), lambda i, j, k: (j, i, 0))],
               acc_shape=(tm, n), epilogue=_ep_store(out_dtype), name=name)[0]


def _window_geometry(ws):
    base = [(ws * k // LANES) * LANES for k in range(N_DEV)]
    off = [ws * k - base[k] for k in range(N_DEV)]
    win = -(-(max(off) + ws) // LANES) * LANES
    return base, off, win


def _shards_to_columns(xg, base, off, win, n_out, name, tr=256):
    R, ws = xg.shape[1], xg.shape[2]
    tr = _tile(R, tr)
    nb_win = win // LANES

    def body(x_ref, o_ref, frame_ref):
        written = set()
        frame_ref[...] = jnp.zeros_like(frame_ref)
        for k in range(N_DEV):
            frame_ref[:, 0:ws] = x_ref[k].astype(F32)
            window = frame_ref[...]
            if off[k]:
                window = pltpu.roll(window, off[k], 1)
            for i in range(nb_win):
                b = base[k] // LANES + i
                if b * LANES >= n_out:
                    continue
                cols = slice(b * LANES, (b + 1) * LANES)
                blk = window[:, i * LANES:(i + 1) * LANES]
                if b in written:
                    blk = blk + o_ref[:, cols].astype(F32)
                o_ref[:, cols] = blk.astype(o_ref.dtype)
                written.add(b)
        for b in range(n_out // LANES):
            if b not in written:
                o_ref[:, b * LANES:(b + 1) * LANES] = jnp.zeros((tr, LANES), o_ref.dtype)

    return _pcall(
        body, name=name, grid=(R // tr,), out_shape=jax.ShapeDtypeStruct((R, n_out), xg.dtype),
        in_specs=[pl.BlockSpec((N_DEV, tr, ws), lambda i: (0, i, 0))],
        out_specs=pl.BlockSpec((tr, n_out), lambda i: (i, 0)),
        scratch_shapes=[pltpu.VMEM((tr, win), F32)],
        compiler_params=_cparams(("parallel",)))(xg)


def _columns_to_shards(x, ws, base, off, win, name, tr=256):
    R = x.shape[0]
    tr = _tile(R, tr)

    def body(x_ref, o_ref, frame_ref):
        for k in range(N_DEV):
            window = x_ref[:, base[k]:base[k] + win].astype(F32)
            if off[k]:
                window = pltpu.roll(window, win - off[k], 1)
            frame_ref[...] = window
            o_ref[k] = frame_ref[:, 0:ws].astype(o_ref.dtype)

    return _pcall(
        body, name=name, grid=(R // tr,), out_shape=jax.ShapeDtypeStruct((N_DEV, R, ws), x.dtype),
        in_specs=[pl.BlockSpec((tr, x.shape[1]), lambda i: (i, 0))],
        out_specs=pl.BlockSpec((N_DEV, tr, ws), lambda i: (0, i, 0)),
        scratch_shapes=[pltpu.VMEM((tr, win), F32)],
        compiler_params=_cparams(("parallel",)))(x)


def _sigmoid(x):
    return 1.0 / (1.0 + jnp.exp(-x))


def _row_spec(tm, d):
    return pl.BlockSpec((tm, d), lambda i: (i, 0))


def _vec_spec(d):
    return pl.BlockSpec((1, d), lambda i: (0, 0))


def _norm_mod_fwd(x, y, gate, nw, scale, shift, name, tm=512):
    L, D = x.shape
    tm = _tile(L, tm)
    has_res = y is not None

    def body(*refs):
        if has_res:
            x_ref, y_ref, g_ref, nw_ref, sc_ref, sh_ref, xo_ref, h_ref = refs
            xn = x_ref[...] + g_ref[...] * y_ref[...]
            xo_ref[...] = xn
        else:
            x_ref, nw_ref, sc_ref, sh_ref, h_ref = refs
            xn = x_ref[...]
        rstd = lax.rsqrt(jnp.mean(xn * xn, axis=-1, keepdims=True) + NORM_EPS)
        h = xn * rstd * nw_ref[...] * (1.0 + sc_ref[...]) + sh_ref[...]
        h_ref[...] = h.astype(BF16)

    row, vec = _row_spec(tm, D), _vec_spec(D)
    if has_res:
        ins, in_specs = (x, y, gate, nw, scale, shift), [row, row, vec, vec, vec, vec]
        out_shape = [jax.ShapeDtypeStruct((L, D), F32), jax.ShapeDtypeStruct((L, D), BF16)]
        out_specs = [row, row]
    else:
        ins, in_specs = (x, nw, scale, shift), [row, vec, vec, vec]
        out_shape = [jax.ShapeDtypeStruct((L, D), BF16)]
        out_specs = [row]
    outs = _pcall(body, name=name, grid=(L // tm,), out_shape=out_shape, in_specs=in_specs,
                  out_specs=out_specs, compiler_params=_cparams(("parallel",)))(*ins)
    return outs if has_res else (x, outs[0])


def _gated_branch_bwd(dx, branch, y_ref, g_ref, dy_ref, dg_ref):
    if branch is None:
        return
    dy_ref[...] = (g_ref[...] * dx).astype(BF16)
    dg_ref[...] += jnp.sum(dx * y_ref[...], axis=0, keepdims=True)


def _norm_mod_bwd(dh, x, nw, scale, dres, name, branch=None, tm=512):
    L, D = x.shape
    tm = _tile(L, tm)
    nb = 0 if branch is None else 2

    def body(dh_ref, x_ref, nw_ref, sc_ref, dres_ref, *rest):
        y_ref, g_ref = rest[:nb] if nb else (None, None)
        dx_ref, dsh_ref, dsc_ref, dnw_ref = rest[nb:nb + 4]
        dy_ref, dg_ref = rest[nb + 4:] if nb else (None, None)

        @pl.when(pl.program_id(0) == 0)
        def _():
            dsh_ref[...] = jnp.zeros_like(dsh_ref)
            dsc_ref[...] = jnp.zeros_like(dsc_ref)
            dnw_ref[...] = jnp.zeros_like(dnw_ref)
            if nb:
                dg_ref[...] = jnp.zeros_like(dg_ref)

        xv = x_ref[...]
        dh_v = dh_ref[...]
        nw_v = nw_ref[...]
        rstd = lax.rsqrt(jnp.mean(xv * xv, axis=-1, keepdims=True) + NORM_EPS)
        xhat = xv * rstd
        dsh_ref[...] += jnp.sum(dh_v, axis=0, keepdims=True)
        dsc_ref[...] += jnp.sum(dh_v * (xhat * nw_v), axis=0, keepdims=True)
        dr = dh_v * (1.0 + sc_ref[...])
        dnw_ref[...] += jnp.sum(dr * xhat, axis=0, keepdims=True)
        dxh = dr * nw_v
        dx = rstd * (dxh - xhat * jnp.mean(dxh * xhat, axis=-1, keepdims=True)) + dres_ref[...]
        dx_ref[...] = dx
        _gated_branch_bwd(dx, branch, y_ref, g_ref, dy_ref, dg_ref)

    row, vec = _row_spec(tm, D), _vec_spec(D)
    extra_in = [] if branch is None else list(branch)
    return _pcall(
        body, name=name, grid=(L // tm,),
        out_shape=[jax.ShapeDtypeStruct((L, D), F32)] + [jax.ShapeDtypeStruct((1, D), F32)] * 3
        + ([jax.ShapeDtypeStruct((L, D), BF16), jax.ShapeDtypeStruct((1, D), F32)] if nb else []),
        in_specs=[row, row, vec, vec, row] + ([row, vec] if nb else []),
        out_specs=[row, vec, vec, vec] + ([row, vec] if nb else []),
        compiler_params=_cparams(("arbitrary",)))(dh, x, nw, scale, dres, *extra_in)


def _final_loss(x, y, gate, fw, target, name, tm=512):
    L, D = x.shape
    tm = _tile(L, tm)

    def body(x_ref, y_ref, g_ref, fw_ref, t_ref, dx_ref, loss_ref, dfw_ref, dy_ref, dg_ref):
        @pl.when(pl.program_id(0) == 0)
        def _():
            loss_ref[...] = jnp.zeros_like(loss_ref)
            dfw_ref[...] = jnp.zeros_like(dfw_ref)
            dg_ref[...] = jnp.zeros_like(dg_ref)

        xn = x_ref[...] + g_ref[...] * y_ref[...]
        fw_v = fw_ref[...]
        rstd = lax.rsqrt(jnp.mean(xn * xn, axis=-1, keepdims=True) + NORM_EPS)
        xhat = xn * rstd
        diff = xhat * fw_v - t_ref[...]
        loss_ref[...] += jnp.sum(diff * diff, axis=0, keepdims=True)
        dyf = diff * (1.0 / D)
        dfw_ref[...] += jnp.sum(dyf * xhat, axis=0, keepdims=True)
        dxh = dyf * fw_v
        dx = rstd * (dxh - xhat * jnp.mean(dxh * xhat, axis=-1, keepdims=True))
        dx_ref[...] = dx
        _gated_branch_bwd(dx, True, y_ref, g_ref, dy_ref, dg_ref)

    row, vec = _row_spec(tm, D), _vec_spec(D)
    return _pcall(
        body, name=name, grid=(L // tm,),
        out_shape=[jax.ShapeDtypeStruct((L, D), F32), jax.ShapeDtypeStruct((1, D), F32),
                   jax.ShapeDtypeStruct((1, D), F32), jax.ShapeDtypeStruct((L, D), BF16),
                   jax.ShapeDtypeStruct((1, D), F32)],
        in_specs=[row, row, vec, vec, row], out_specs=[row, vec, vec, row, vec],
        compiler_params=_cparams(("arbitrary",)))(x, y, gate, fw, target)


def _shift_down(v, s, row):
    if s == 0:
        return v
    return jnp.where(row >= s, pltpu.roll(v, s, 0), 0.0)


CONV_ROWS = 32


def _shifted_rows(x_ref, r0, n, lanes=slice(None)):
    cur = x_ref[r0:r0 + CONV_ROWS, lanes]
    if r0 >= n - 1:
        return [cur] + [x_ref[r0 - s:r0 - s + CONV_ROWS, lanes] for s in range(1, n)]
    row = lax.broadcasted_iota(jnp.int32, cur.shape, 0)
    return [_shift_down(cur, s, row) for s in range(n)]


def _ssd_conv_fwd(zx, w, b, col0, width, name, cb=512):
    L = zx.shape[0]
    nb = width // cb
    off = col0 // cb

    def body(x_ref, w_ref, b_ref, o_ref):
        for l0 in range(0, cb, LANES):
            lanes = slice(l0, l0 + LANES)
            taps = [w_ref[k:k + 1, lanes] for k in range(SSD_K)]
            bias = b_ref[:, lanes]
            for r0 in range(0, L, CONV_ROWS):
                taps_in = _shifted_rows(x_ref, r0, SSD_K, lanes)
                acc = bias + taps[SSD_K - 1] * taps_in[0]
                for s in range(1, SSD_K):
                    acc = acc + taps[SSD_K - 1 - s] * taps_in[s]
                o_ref[r0:r0 + CONV_ROWS, lanes] = acc * _sigmoid(acc)

    return _pcall(
        body, name=name, grid=(nb,), out_shape=jax.ShapeDtypeStruct((L, width), F32),
        in_specs=[pl.BlockSpec((L, cb), lambda j: (0, off + j)),
                  pl.BlockSpec((SSD_K, cb), lambda j: (0, j)),
                  pl.BlockSpec((1, cb), lambda j: (0, j))],
        out_specs=pl.BlockSpec((L, cb), lambda j: (0, j)),
        compiler_params=_cparams(("parallel",)))(zx, w, b)


def _ssd_conv_bwd(zx, w, b, d_parts, dzx, col0, name, cb=128):
    L = zx.shape[0]
    widths = [p.shape[1] for p in d_parts]
    width = sum(widths)
    nb = width // cb
    off = col0 // cb
    starts = [sum(widths[:i]) // cb for i in range(len(d_parts))]
    counts = [wd // cb for wd in widths]

    def body(x_ref, w_ref, b_ref, *rest):
        d_refs = rest[:len(d_parts)]
        dx_ref, dw_ref, db_ref, dpre_ref = rest[len(d_parts) + 1:]
        j = pl.program_id(0)
        taps = [w_ref[k:k + 1, :] for k in range(SSD_K)]
        bias = b_ref[...]
        fold = lambda v: sum(v[r:r + 8, :] for r in range(0, CONV_ROWS, 8))
        db8 = jnp.zeros((8, cb), F32)
        dw8 = [jnp.zeros((8, cb), F32) for _ in range(SSD_K)]
        for r0 in range(0, L, CONV_ROWS):
            rows = slice(r0, r0 + CONV_ROWS)
            d_val = d_refs[-1][rows, :]
            for i in range(len(d_parts) - 2, -1, -1):
                d_val = jnp.where(j < starts[i + 1], d_refs[i][rows, :], d_val)
            taps_in = _shifted_rows(x_ref, r0, SSD_K)
            acc = bias + taps[SSD_K - 1] * taps_in[0]
            for s in range(1, SSD_K):
                acc = acc + taps[SSD_K - 1 - s] * taps_in[s]
            sig = _sigmoid(acc)
            dpre = d_val * (sig * (1.0 + acc * (1.0 - sig)))
            dpre_ref[rows, :] = dpre
            db8 = db8 + fold(dpre)
            for s in range(SSD_K):
                dw8[s] = dw8[s] + fold(dpre * taps_in[s])
        dpre_ref[L:L + 8, :] = jnp.zeros((8, cb), F32)
        db_ref[...] = jnp.sum(db8, axis=0, keepdims=True)
        for s in range(SSD_K):
            dw_ref[SSD_K - 1 - s:SSD_K - s, :] = jnp.sum(dw8[s], axis=0, keepdims=True)
        for r0 in range(0, L, CONV_ROWS):
            dx = taps[SSD_K - 1] * dpre_ref[r0:r0 + CONV_ROWS, :]
            for s in range(1, SSD_K):
                dx = dx + taps[SSD_K - 1 - s] * dpre_ref[r0 + s:r0 + s + CONV_ROWS, :]
            dx_ref[r0:r0 + CONV_ROWS, :] = dx.astype(BF16)

    def part_spec(i):
        return pl.BlockSpec((L, cb), lambda j: (0, jnp.clip(j - starts[i], 0, counts[i] - 1)))

    return _pcall(
        body, name=name, grid=(nb,),
        out_shape=[jax.ShapeDtypeStruct(dzx.shape, BF16), jax.ShapeDtypeStruct((SSD_K, width), F32),
                   jax.ShapeDtypeStruct((1, width), F32)],
        in_specs=[pl.BlockSpec((L, cb), lambda j: (0, off + j)),
                  pl.BlockSpec((SSD_K, cb), lambda j: (0, j)),
                  pl.BlockSpec((1, cb), lambda j: (0, j))]
        + [part_spec(i) for i in range(len(d_parts))] + [pl.BlockSpec(memory_space=pl.ANY)],
        out_specs=[pl.BlockSpec((L, cb), lambda j: (0, off + j)),
                   pl.BlockSpec((SSD_K, cb), lambda j: (0, j)),
                   pl.BlockSpec((1, cb), lambda j: (0, j))],
        input_output_aliases={3 + len(d_parts): 0},
        scratch_shapes=[pltpu.VMEM((L + 8, cb), F32)],
        compiler_params=_cparams(("parallel",)))(zx, w, b, *d_parts, dzx)


def _dzx_finish(dzx, ddt, col0, name, tl=512):
    G, L, _ = ddt.shape
    tail = dzx.shape[1] - col0
    tl = _tile(L, tl)

    def body(ddt_ref, dzx_ref, o_ref):
        s = ddt_ref[0]
        for g in range(1, G):
            s = s + ddt_ref[g]
        o_ref[:, 0:LANES] = s.astype(o_ref.dtype)
        if tail > LANES:
            o_ref[:, LANES:] = jnp.zeros((tl, tail - LANES), o_ref.dtype)

    return _pcall(
        body, name=name, grid=(L // tl,), out_shape=jax.ShapeDtypeStruct(dzx.shape, dzx.dtype),
        in_specs=[pl.BlockSpec((G, tl, LANES), lambda i: (0, i, 0)), pl.BlockSpec(memory_space=pl.ANY)],
        out_specs=pl.BlockSpec((tl, tail), lambda i: (i, col0 // tail)),
        input_output_aliases={1: 0},
        compiler_params=_cparams(("parallel",)))(ddt, dzx)


def _sc_conv_fwd(proj, w, name, cb=512):
    L = proj.shape[0]
    width = proj.shape[1] // 3
    nb = width // cb

    def body(b_ref, c_ref, x_ref, w_ref, o_ref):
        for l0 in range(0, cb, LANES):
            lanes = slice(l0, l0 + LANES)
            taps = [w_ref[k:k + 1, lanes] for k in range(SC_K)]
            for r0 in range(0, L, CONV_ROWS):
                rows = slice(r0, r0 + CONV_ROWS)
                q = [c * x for c, x in zip(_shifted_rows(c_ref, r0, SC_K, lanes),
                                           _shifted_rows(x_ref, r0, SC_K, lanes))]
                acc = taps[SC_K - 1] * q[0]
                for s in range(1, SC_K):
                    acc = acc + taps[SC_K - 1 - s] * q[s]
                o_ref[rows, lanes] = (b_ref[rows, lanes] * acc).astype(BF16)

    return _pcall(
        body, name=name, grid=(nb,), out_shape=jax.ShapeDtypeStruct((L, width), BF16),
        in_specs=[pl.BlockSpec((L, cb), lambda j: (0, j)),
                  pl.BlockSpec((L, cb), lambda j: (0, nb + j)),
                  pl.BlockSpec((L, cb), lambda j: (0, 2 * nb + j)),
                  pl.BlockSpec((SC_K, cb), lambda j: (0, j))],
        out_specs=pl.BlockSpec((L, cb), lambda j: (0, j)),
        compiler_params=_cparams(("parallel",)))(proj, proj, proj, w)


def _sc_conv_bwd(proj, w, dy, name, cb=128):
    L = proj.shape[0]
    width = proj.shape[1] // 3
    nb = width // cb

    def body(b_ref, c_ref, x_ref, w_ref, dy_ref, db_ref, dc_ref, dxv_ref, dw_ref, dconv_ref):
        taps = [w_ref[k:k + 1, :] for k in range(SC_K)]
        fold = lambda v: sum(v[r:r + 8, :] for r in range(0, CONV_ROWS, 8))
        dw8 = [jnp.zeros((8, cb), F32) for _ in range(SC_K)]
        for r0 in range(0, L, CONV_ROWS):
            rows = slice(r0, r0 + CONV_ROWS)
            q = [c * x for c, x in zip(_shifted_rows(c_ref, r0, SC_K), _shifted_rows(x_ref, r0, SC_K))]
            conv = taps[SC_K - 1] * q[0]
            for s in range(1, SC_K):
                conv = conv + taps[SC_K - 1 - s] * q[s]
            dyv = dy_ref[rows, :]
            db_ref[rows, :] = (dyv * conv).astype(BF16)
            dconv = dyv * b_ref[rows, :]
            dconv_ref[rows, :] = dconv
            for s in range(SC_K):
                dw8[s] = dw8[s] + fold(dconv * q[s])
        dconv_ref[L:L + 8, :] = jnp.zeros((8, cb), F32)
        for s in range(SC_K):
            dw_ref[SC_K - 1 - s:SC_K - s, :] = jnp.sum(dw8[s], axis=0, keepdims=True)
        for r0 in range(0, L, CONV_ROWS):
            rows = slice(r0, r0 + CONV_ROWS)
            dq = taps[SC_K - 1] * dconv_ref[rows, :]
            for s in range(1, SC_K):
                dq = dq + taps[SC_K - 1 - s] * dconv_ref[r0 + s:r0 + s + CONV_ROWS, :]
            dc_ref[rows, :] = (dq * x_ref[rows, :]).astype(BF16)
            dxv_ref[rows, :] = (dq * c_ref[rows, :]).astype(BF16)

    blk = pl.BlockSpec((L, cb), lambda j: (0, j))
    wblk = pl.BlockSpec((SC_K, cb), lambda j: (0, j))
    return _pcall(
        body, name=name, grid=(nb,),
        out_shape=[jax.ShapeDtypeStruct((L, width), BF16)] * 3 + [jax.ShapeDtypeStruct((SC_K, width), F32)],
        in_specs=[blk, pl.BlockSpec((L, cb), lambda j: (0, nb + j)),
                  pl.BlockSpec((L, cb), lambda j: (0, 2 * nb + j)), wblk, blk],
        out_specs=[blk, blk, blk, wblk], scratch_shapes=[pltpu.VMEM((L + 8, cb), F32)],
        compiler_params=_cparams(("parallel",)))(proj, proj, proj, w, dy)


def _split3(v):
    hi = v.astype(BF16)
    r1 = v - hi.astype(F32)
    mid = r1.astype(BF16)
    lo = (r1 - mid.astype(F32)).astype(BF16)
    return hi, mid, lo


def _dot_exact01(t01, v):
    hi, mid, lo = _split3(v)
    return _dot(t01, hi) + _dot(t01, mid) + _dot(t01, lo)


def _lane_col(v, lane, h):
    return jnp.sum(jnp.where(lane == h, v, 0.0), axis=1, keepdims=True)


def _sum_all(v):
    return jnp.sum(jnp.sum(v, axis=1, keepdims=True), axis=0, keepdims=True)


def _softplus(x):
    return jnp.maximum(x, 0.0) + jnp.log1p(jnp.exp(-jnp.abs(x)))


def _ssd_decay(zx, bias_p, alog_p, n_heads, dt_block, name):
    L = zx.shape[0]
    nc = L // SSD_CHUNK
    per_step = 4 if nc % 4 == 0 else 1
    rows_step = per_step * SSD_CHUNK

    def body(raw_ref, bias_ref, alog_ref, dt_ref, sg_ref, cs_ref, cst_ref, last_ref):
        lane = lax.broadcasted_iota(jnp.int32, (SSD_CHUNK, LANES), 1)
        row = lax.broadcasted_iota(jnp.int32, (SSD_CHUNK, LANES), 0)
        valid = lane < n_heads
        tri = (row >= lane).astype(BF16)
        a_row = -jnp.exp(alog_ref[...])
        for i in range(per_step):
            rows = slice(i * SSD_CHUNK, (i + 1) * SSD_CHUNK)
            raw = raw_ref[rows, :] + bias_ref[...]
            dt = jnp.where(valid, _softplus(raw), 0.0)
            a = dt * a_row
            cs = _dot_exact01(tri, a)
            dt_ref[rows, :] = dt
            sg_ref[rows, :] = _sigmoid(raw)
            cs_ref[rows, :] = cs
            cst_ref[i] = cs.T
            last_ref[i] = jnp.sum(a, axis=0, keepdims=True)

    blk = pl.BlockSpec((rows_step, LANES), lambda c: (c, 0))
    head_vec = pl.BlockSpec((1, LANES), lambda c: (0, 0))
    return _pcall(
        body, name=name, grid=(nc // per_step,),
        out_shape=[jax.ShapeDtypeStruct((L, LANES), F32)] * 3
        + [jax.ShapeDtypeStruct((nc, SSD_CHUNK, LANES), F32), jax.ShapeDtypeStruct((nc, 1, LANES), F32)],
        in_specs=[pl.BlockSpec((rows_step, LANES), lambda c: (c, dt_block)), head_vec, head_vec],
        out_specs=[blk, blk, blk, pl.BlockSpec((per_step, SSD_CHUNK, LANES), lambda c: (c, 0, 0)),
                   pl.BlockSpec((per_step, 1, LANES), lambda c: (c, 0, 0))],
        compiler_params=_cparams(("parallel",)))(zx, bias_p, alog_p)


def _ssd_common(dt_ref, cs_ref, last_ref, b_ref, c_ref):
    c_sz = SSD_CHUNK
    lane = lax.broadcasted_iota(jnp.int32, (c_sz, LANES), 1)
    row = lax.broadcasted_iota(jnp.int32, (c_sz, LANES), 0)
    bb = b_ref[...].astype(BF16)
    cb = c_ref[...].astype(BF16)
    scores = _dot(cb, bb, "nt")
    return dict(lane=lane, row=row, dt=dt_ref[...], cs=cs_ref[...], last_row=last_ref[...], bb=bb, cb=cb,
                scores=scores, causal=row >= lane, lo=lane < SSD_P)


def _pair_terms(q, cst_ref, h0):
    lane, lo = q["lane"], q["lo"]
    out = {}
    cols, dts, lasts, lms = [], [], [], []
    lane1 = lax.broadcasted_iota(jnp.int32, (1, LANES), 1)
    for h in (h0, h0 + 1):
        col = _lane_col(q["cs"], lane, h)
        rowv = cst_ref[pl.ds(h, 1), :]
        lms.append(jnp.exp(jnp.where(q["causal"], col - rowv, -1e30)))
        cols.append(col)
        dts.append(_lane_col(q["dt"], lane, h))
        lasts.append(jnp.sum(jnp.where(lane1 == h, q["last_row"], 0.0), axis=1, keepdims=True))
    out["lm"] = lms
    out["cols"] = cols
    out["lasts"] = lasts
    out["dt_b"] = jnp.where(lo, dts[0], dts[1])
    out["e_b"] = jnp.where(lo, jnp.exp(cols[0]), jnp.exp(cols[1]))
    out["dec_cols"] = [jnp.exp(lasts[0] - cols[0]), jnp.exp(lasts[1] - cols[1])]
    out["dec_b"] = jnp.where(lo, out["dec_cols"][0], out["dec_cols"][1])
    lo1 = lane1 < SSD_P
    out["explast"] = [jnp.exp(lasts[0]), jnp.exp(lasts[1])]
    out["explast_b"] = jnp.where(lo1, out["explast"][0], out["explast"][1])
    return out


def _ssd_fwd(zx, xc, decay, d_lane, nw, d_inner, after, name):
    L = zx.shape[0]
    nc = L // SSD_CHUNK
    gw = d_inner // SSD_G
    heads = gw // SSD_P
    n_pair = heads // 2
    bc0 = d_inner // LANES

    def body(z_ref, xs_ref, b_ref, c_ref, dt_ref, cs_ref, cst_ref, last_ref, dl_ref, nw_ref, after_ref,
             y_ref, yn_ref, prev_ref, s_ref):
        @pl.when(pl.program_id(1) == 0)
        def _():
            s_ref[...] = jnp.zeros_like(s_ref)

        q = _ssd_common(dt_ref, cs_ref, last_ref, b_ref, c_ref)
        prev_ref[...] = s_ref[...]
        lo = q["lo"]
        for j in range(n_pair):
            sl = slice(j * LANES, (j + 1) * LANES)
            p = _pair_terms(q, cst_ref, pl.program_id(0) * heads + 2 * j)
            xs_p = xs_ref[:, sl]
            xp = xs_p * p["dt_b"]
            xb = xp.astype(BF16)
            m_a = (q["scores"] * p["lm"][0]).astype(BF16)
            m_b = (q["scores"] * p["lm"][1]).astype(BF16)
            yd = jnp.where(lo, _dot(m_a, xb), _dot(m_b, xb))
            s_p = s_ref[:, sl]
            yo = _dot(q["cb"], s_p.astype(BF16)) * p["e_b"]
            y_ref[:, sl] = yd + yo + dl_ref[:, sl] * xs_p
            st = _dot(q["bb"], (xp * p["dec_b"]).astype(BF16), "tn")
            s_ref[:, sl] = s_p * p["explast_b"] + st
        yv = y_ref[...]
        zv = z_ref[...]
        yg = yv * (zv * _sigmoid(zv))
        rstd = lax.rsqrt(jnp.mean(yg * yg, axis=-1, keepdims=True) + NORM_EPS)
        yn_ref[...] = (yg * rstd * nw_ref[...]).astype(BF16)

    grp = lambda width: pl.BlockSpec((None, 1, width), lambda g, c: (g, 0, 0))
    dt_, _, cs_, cst_, last_ = decay
    return _pcall(
        body, name=name, grid=(SSD_G, nc),
        out_shape=[jax.ShapeDtypeStruct((L, d_inner), F32), jax.ShapeDtypeStruct((L, d_inner), BF16),
                   jax.ShapeDtypeStruct((nc, SSD_G, SSD_N, gw), F32)],
        in_specs=[pl.BlockSpec((SSD_CHUNK, gw), lambda g, c: (c, g)),
                  pl.BlockSpec((SSD_CHUNK, gw), lambda g, c: (c, g)),
                  pl.BlockSpec((SSD_CHUNK, SSD_N), lambda g, c: (c, bc0 + g)),
                  pl.BlockSpec((SSD_CHUNK, SSD_N), lambda g, c: (c, bc0 + SSD_G + g)),
                  pl.BlockSpec((SSD_CHUNK, LANES), lambda g, c: (c, 0)),
                  pl.BlockSpec((SSD_CHUNK, LANES), lambda g, c: (c, 0)),
                  pl.BlockSpec((None, SSD_CHUNK, LANES), lambda g, c: (c, 0, 0)),
                  pl.BlockSpec((None, 1, LANES), lambda g, c: (c, 0, 0)),
                  grp(gw), grp(gw), pl.BlockSpec(memory_space=pl.ANY)],
        out_specs=[pl.BlockSpec((SSD_CHUNK, gw), lambda g, c: (c, g)),
                   pl.BlockSpec((SSD_CHUNK, gw), lambda g, c: (c, g)),
                   pl.BlockSpec((None, None, SSD_N, gw), lambda g, c: (c, g, 0, 0))],
        scratch_shapes=[pltpu.VMEM((SSD_N, gw), F32)],
        compiler_params=_cparams(("parallel", "arbitrary")))(
            zx, xc, xc, xc, dt_, cs_, cst_, last_, d_lane, nw, after)


def _ssd_bwd(dyn, y, zx, xc, prev, decay, alog_p, d_lane, nw, d_inner, name):
    L = zx.shape[0]
    nc = L // SSD_CHUNK
    gw = d_inner // SSD_G
    heads = gw // SSD_P
    n_pair = heads // 2
    bc0 = d_inner // LANES

    def body(dyn_ref, y_ref, z_ref, xs_ref, b_ref, c_ref, prev_ref, dt_ref, sg_ref, cs_ref, cst_ref, last_ref,
             alog_ref, dl_ref, nw_ref,
             dz_ref, dxs_ref, db_ref, dc_ref, ddt_ref, dbias_ref, dalog_ref, dd_ref, dnw_ref,
             ds_ref, racc_ref):
        @pl.when(pl.program_id(1) == 0)
        def _():
            ds_ref[...] = jnp.zeros_like(ds_ref)
            dbias_ref[...] = jnp.zeros_like(dbias_ref)
            dalog_ref[...] = jnp.zeros_like(dalog_ref)
            dd_ref[...] = jnp.zeros_like(dd_ref)
            dnw_ref[...] = jnp.zeros_like(dnw_ref)

        q = _ssd_common(dt_ref, cs_ref, last_ref, b_ref, c_ref)
        a_row = -jnp.exp(alog_ref[...])
        lane, row, lo = q["lane"], q["row"], q["lo"]
        lane1 = lax.broadcasted_iota(jnp.int32, (1, LANES), 1)
        head0 = pl.program_id(0) * heads
        mine = (lane >= head0) & (lane < head0 + heads)

        yv, zv, dynv, nwv = y_ref[...], z_ref[...], dyn_ref[...], nw_ref[...]
        sig = _sigmoid(zv)
        sz = zv * sig
        yg = yv * sz
        rstd = lax.rsqrt(jnp.mean(yg * yg, axis=-1, keepdims=True) + NORM_EPS)
        yhat = yg * rstd
        dnw_ref[...] += jnp.sum(dynv * yhat, axis=0, keepdims=True)
        dyh = dynv * nwv
        dyg = rstd * (dyh - yhat * jnp.mean(dyh * yhat, axis=-1, keepdims=True))
        dz_ref[...] = (dyg * yv * (sig * (1.0 + zv * (1.0 - sig)))).astype(BF16)
        dy_all = dyg * sz

        dg = jnp.zeros((SSD_CHUNK, SSD_CHUNK), F32)
        dc_acc = jnp.zeros((SSD_CHUNK, SSD_N), F32)
        db_acc = jnp.zeros((SSD_CHUNK, SSD_N), F32)
        dcs_mat = jnp.zeros((SSD_CHUNK, LANES), F32)
        ddt_mat = jnp.zeros((SSD_CHUNK, LANES), F32)
        dd_row = jnp.zeros((1, LANES), F32)
        racc_ref[...] = jnp.zeros_like(racc_ref)
        is_last = row == SSD_CHUNK - 1

        for j in range(n_pair):
            sl = slice(j * LANES, (j + 1) * LANES)
            ha, hb = head0 + 2 * j, head0 + 2 * j + 1
            p = _pair_terms(q, cst_ref, ha)
            xs_p = xs_ref[:, sl]
            dyp = dy_all[:, sl]
            xp = xs_p * p["dt_b"]
            xb = xp.astype(BF16)
            s_p = prev_ref[:, sl]
            s_pb = s_p.astype(BF16)
            dsn = ds_ref[:, sl]
            dsnb = dsn.astype(BF16)
            m_f = [q["scores"] * p["lm"][0], q["scores"] * p["lm"][1]]

            t0 = dyp * xs_p
            dd_row = dd_row + jnp.where(lane1 == ha, _sum_all(jnp.where(lo, t0, 0.0)), 0.0) \
                + jnp.where(lane1 == hb, _sum_all(jnp.where(lo, 0.0, t0)), 0.0)
            dxs_p = dl_ref[:, sl] * dyp

            yo = _dot(q["cb"], s_pb) * p["e_b"]
            dcs_b = (dyp * p["e_b"]).astype(BF16)
            dc_acc = dc_acc + _dot(dcs_b, s_pb, "nt")
            ds_yo = _dot(q["cb"], dcs_b, "tn")
            t1 = dyp * yo
            dcs_cols = [jnp.sum(jnp.where(lo, t1, 0.0), axis=1, keepdims=True),
                        jnp.sum(jnp.where(lo, 0.0, t1), axis=1, keepdims=True)]

            t2 = dsn * s_p
            dlast = [p["explast"][0] * _sum_all(jnp.where(lo, t2, 0.0)),
                     p["explast"][1] * _sum_all(jnp.where(lo, 0.0, t2))]
            ds_ref[:, sl] = dsn * p["explast_b"] + ds_yo
            w = _dot(q["bb"], dsnb)
            db_acc = db_acc + _dot((xp * p["dec_b"]).astype(BF16), dsnb, "nt")
            dxp = w * p["dec_b"]
            t3 = w * xp
            e = [jnp.sum(jnp.where(lo, t3, 0.0), axis=1, keepdims=True) * p["dec_cols"][0],
                 jnp.sum(jnp.where(lo, 0.0, t3), axis=1, keepdims=True) * p["dec_cols"][1]]
            for i in range(2):
                dlast[i] = dlast[i] + jnp.sum(e[i], axis=0, keepdims=True)
                dcs_cols[i] = dcs_cols[i] - e[i]

            dyb = dyp.astype(BF16)
            dy_h = [jnp.where(lo, dyp, 0.0).astype(BF16), jnp.where(lo, 0.0, dyp).astype(BF16)]
            dms = [_dot(dy_h[0], xb, "nt"), _dot(dy_h[1], xb, "nt")]
            dxp = dxp + jnp.where(lo, _dot(m_f[0].astype(BF16), dyb, "tn"), _dot(m_f[1].astype(BF16), dyb, "tn"))
            for i, h in enumerate((ha, hb)):
                dg = dg + dms[i] * p["lm"][i]
                qm = dms[i] * m_f[i]
                dcs_cols[i] = dcs_cols[i] + jnp.sum(qm, axis=1, keepdims=True)
                racc_ref[pl.ds(h, 1), :] = jnp.sum(qm, axis=0, keepdims=True)

            dxs_ref[:, sl] = dxs_p + dxp * p["dt_b"]
            t4 = dxp * xs_p
            ddt_cols = [jnp.sum(jnp.where(lo, t4, 0.0), axis=1, keepdims=True),
                        jnp.sum(jnp.where(lo, 0.0, t4), axis=1, keepdims=True)]
            for i, h in enumerate((ha, hb)):
                sel = lane == h
                dcs_mat = dcs_mat + jnp.where(sel, dcs_cols[i], 0.0) + jnp.where(sel & is_last, dlast[i], 0.0)
                ddt_mat = ddt_mat + jnp.where(sel, ddt_cols[i], 0.0)

        dcs_mat = dcs_mat - racc_ref[...].T
        tri_t = (row <= lane).astype(BF16)
        da = _dot_exact01(tri_t, dcs_mat)
        ddt = ddt_mat + da * a_row
        dalog_ref[...] += jnp.sum(jnp.where(mine, da * q["dt"], 0.0), axis=0, keepdims=True) * a_row
        draw = jnp.where(mine, ddt * sg_ref[...], 0.0)
        ddt_ref[...] = draw
        dbias_ref[...] += jnp.sum(draw, axis=0, keepdims=True)
        dd_ref[...] += dd_row
        dgb = dg.astype(BF16)
        dc_ref[...] = dc_acc + _dot(dgb, q["bb"])
        db_ref[...] = db_acc + _dot(dgb, q["cb"], "tn")

    rev = lambda c: nc - 1 - c
    grp = lambda width: pl.BlockSpec((None, 1, width), lambda g, c: (g, 0, 0))
    blk = lambda width, off: pl.BlockSpec((SSD_CHUNK, width), lambda g, c: (rev(c), off + g))
    head_vec = pl.BlockSpec((1, LANES), lambda g, c: (0, 0))
    chunk_rows = pl.BlockSpec((SSD_CHUNK, LANES), lambda g, c: (rev(c), 0))
    dt_, sg_, cs_, cst_, last_ = decay
    return _pcall(
        body, name=name, grid=(SSD_G, nc),
        out_shape=[jax.ShapeDtypeStruct(zx.shape, BF16), jax.ShapeDtypeStruct((L, d_inner), F32),
                   jax.ShapeDtypeStruct((L, SSD_G * SSD_N), F32), jax.ShapeDtypeStruct((L, SSD_G * SSD_N), F32),
                   jax.ShapeDtypeStruct((SSD_G, L, LANES), F32),
                   jax.ShapeDtypeStruct((SSD_G, 1, LANES), F32), jax.ShapeDtypeStruct((SSD_G, 1, LANES), F32),
                   jax.ShapeDtypeStruct((SSD_G, 1, LANES), F32), jax.ShapeDtypeStruct((SSD_G, 1, gw), F32)],
        in_specs=[blk(gw, 0), blk(gw, 0), blk(gw, 0), blk(gw, 0), blk(SSD_N, bc0), blk(SSD_N, bc0 + SSD_G),
                  pl.BlockSpec((None, None, SSD_N, gw), lambda g, c: (rev(c), g, 0, 0)),
                  chunk_rows, chunk_rows, chunk_rows,
                  pl.BlockSpec((None, SSD_CHUNK, LANES), lambda g, c: (rev(c), 0, 0)),
                  pl.BlockSpec((None, 1, LANES), lambda g, c: (rev(c), 0, 0)),
                  head_vec, grp(gw), grp(gw)],
        out_specs=[blk(gw, 0), blk(gw, 0), blk(SSD_N, 0), blk(SSD_N, 0),
                   pl.BlockSpec((None, SSD_CHUNK, LANES), lambda g, c: (g, rev(c), 0)),
                   grp(LANES), grp(LANES), grp(LANES), grp(gw)],
        scratch_shapes=[pltpu.VMEM((SSD_N, gw), F32), pltpu.VMEM((SSD_CHUNK, LANES), F32)],
        compiler_params=_cparams(("parallel", "arbitrary")))(
            dyn, y, zx, xc, xc, xc, prev, dt_, sg_, cs_, cst_, last_, alog_p, d_lane, nw)


def _cond_mod(c_pad, ada_w, ada_b_loc, after, name):
    depth, D, n = ada_w.shape
    rows = c_pad.shape[0]

    def body(c_ref, w_ref, b_ref, after_ref, mod_ref, cond_ref):
        cv = c_ref[...]
        cond = cv * _sigmoid(cv)
        cond_ref[...] = cond
        mod_ref[...] = _dot(cond.astype(BF16), w_ref[...].astype(BF16)) + b_ref[...]

    return _pcall(
        body, name=name, grid=(depth,),
        out_shape=[jax.ShapeDtypeStruct((depth, rows, n), F32), jax.ShapeDtypeStruct((rows, D), F32)],
        in_specs=[pl.BlockSpec((rows, D), lambda i: (0, 0)),
                  pl.BlockSpec((None, D, n), lambda i: (i, 0, 0)),
                  pl.BlockSpec((None, 1, n), lambda i: (i, 0, 0)),
                  pl.BlockSpec(memory_space=pl.ANY)],
        out_specs=[pl.BlockSpec((None, rows, n), lambda i: (i, 0, 0)),
                   pl.BlockSpec((rows, D), lambda i: (0, 0))],
        compiler_params=_cparams(("arbitrary",)))(c_pad, ada_w, ada_b_loc, after)


def _adamw_math(g, w, m, v):
    m_new = ADAM_B1 * m + (1.0 - ADAM_B1) * g
    v_new = ADAM_B2 * v + (1.0 - ADAM_B2) * (g * g)
    m_hat = m_new / (1.0 - ADAM_B1 ** ADAM_STEP)
    v_hat = v_new / (1.0 - ADAM_B2 ** ADAM_STEP)
    delta = -ADAM_LR * (m_hat / (jnp.sqrt(v_hat) + ADAM_EPS) + ADAM_WD * w)
    return delta, m_new, v_new


def _adamw_sum(parts, w, m, v, layer, name, prev=None, tr=None):
    depth, R, C = w.shape
    tr = _tile(R, tr if tr is not None else (512 if C <= 512 else 256))

    def body(p_ref, w_ref, m_ref, v_ref, *rest):
        g_ref, d_ref, mo_ref, vo_ref = rest[-4:]
        g = p_ref[0].astype(F32)
        for k in range(1, N_DEV):
            g = g + p_ref[k].astype(F32)
        d, mn, vn = _adamw_math(g, w_ref[...], m_ref[...], v_ref[...])
        g_ref[...] = g
        d_ref[...] = d
        mo_ref[...] = mn
        vo_ref[...] = vn

    blk = pl.BlockSpec((None, tr, C), lambda i: (layer, i, 0))
    prev = list(prev) if prev is not None else []
    return _pcall(
        body, name=name, grid=(R // tr,),
        out_shape=[jax.ShapeDtypeStruct((depth, R, C), F32)] * 4,
        in_specs=[pl.BlockSpec((N_DEV, tr, C), lambda i: (0, i, 0)), blk, blk, blk]
        + [pl.BlockSpec(memory_space=pl.ANY)] * len(prev),
        out_specs=[blk] * 4, input_output_aliases={4 + k: k for k in range(len(prev))},
        compiler_params=_cparams(("parallel",)))(parts, w, m, v, *prev)


def _adamw_small(parts, wmv, head_parts, head_wmv, loss_parts, name):
    n, nh = len(parts), len(head_parts)
    n_heads = head_wmv[0][0].shape[1] if nh else 0
    groups = head_parts[0].shape[1] if nh else 0
    d_model = loss_parts.shape[2]

    def body(*refs):
        p_refs, refs = refs[:n], refs[n:]
        wmv_refs, refs = refs[:3 * n], refs[3 * n:]
        hp_refs, refs = refs[:nh], refs[nh:]
        hwmv_refs, refs = refs[:3 * nh], refs[3 * nh:]
        loss_ref, refs = refs[0], refs[1:]
        outs, loss_out, head_scr = refs[:4 * (n + nh)], refs[4 * (n + nh)], refs[4 * (n + nh) + 1]

        def update(i, g, w_ref, m_ref, v_ref):
            res = (g,) + _adamw_math(g, w_ref[...], m_ref[...], v_ref[...])
            for o_ref, r in zip(outs[4 * i:4 * i + 4], res):
                o_ref[...] = r

        for i in range(n):
            g = p_refs[i][0]
            for k in range(1, N_DEV):
                g = g + p_refs[i][k]
            update(i, g, *wmv_refs[3 * i:3 * i + 3])
        for i in range(nh):
            g = None
            for k in range(N_DEV):
                for grp in range(groups):
                    g = hp_refs[i][k, grp] if g is None else g + hp_refs[i][k, grp]
            head_scr[...] = g
            update(n + i, head_scr[:, 0:n_heads], *hwmv_refs[3 * i:3 * i + 3])
        tot = loss_ref[0]
        for k in range(1, N_DEV):
            tot = tot + loss_ref[k]
        loss_out[...] = jnp.broadcast_to(_sum_all(tot) * (0.5 / d_model), loss_out.shape)

    operands = list(parts) + [a for t in wmv for a in t] + list(head_parts) + [a for t in head_wmv for a in t]
    operands.append(loss_parts)
    out_shape = [jax.ShapeDtypeStruct(t[0].shape, F32) for t in list(wmv) + list(head_wmv) for _ in range(4)]
    out_shape.append(jax.ShapeDtypeStruct((1, LANES), F32))
    vmem = pl.BlockSpec(memory_space=pltpu.VMEM)
    outs = _pcall(body, name=name, out_shape=out_shape, in_specs=[vmem] * len(operands),
                  out_specs=[vmem] * len(out_shape), scratch_shapes=[pltpu.VMEM((1, LANES), F32)],
                  compiler_params=_cparams())(*operands)
    return [outs[4 * i:4 * i + 4] for i in range(n + nh)], outs[-1]


def _ada_adamw(cond_pad, dmod_pad, w, m, v, name, tr=512):
    depth, D, n = w.shape
    rows = cond_pad.shape[0]
    tr = _tile(D, tr)

    def body(c_ref, dm_ref, w_ref, m_ref, v_ref, g_ref, d_ref, mo_ref, vo_ref):
        g = _dot(c_ref[...].astype(BF16), dm_ref[...].astype(BF16), "tn")
        d, mn, vn = _adamw_math(g, w_ref[...], m_ref[...], v_ref[...])
        g_ref[...] = g
        d_ref[...] = d
        mo_ref[...] = mn
        vo_ref[...] = vn

    blk = pl.BlockSpec((None, tr, n), lambda i, r: (i, r, 0))
    return _pcall(
        body, name=name, grid=(depth, D // tr),
        out_shape=[jax.ShapeDtypeStruct((depth, D, n), F32)] * 4,
        in_specs=[pl.BlockSpec((rows, tr), lambda i, r: (0, r)),
                  pl.BlockSpec((None, rows, n), lambda i, r: (i, 0, 0)), blk, blk, blk],
        out_specs=[blk] * 4, compiler_params=_cparams(("parallel", "parallel")))(cond_pad, dmod_pad, w, m, v)


def kernel(x, c, ada_w, ada_b, mix_norm_w, mlp_norm_w, mlp_up, mlp_down, ssd_in_w, ssd_conv_w, ssd_conv_b, ssd_dt_bias, ssd_A_log, ssd_D, ssd_norm_w, ssd_out_w, sc_in_w, sc_conv_w, sc_out_w, final_norm_w, loss_target, m_ada_w, m_ada_b, m_mix_norm_w, m_mlp_norm_w, m_mlp_up, m_mlp_down, m_ssd_in_w, m_ssd_conv_w, m_ssd_conv_b, m_ssd_dt_bias, m_ssd_A_log, m_ssd_D, m_ssd_norm_w, m_ssd_out_w, m_sc_in_w, m_sc_conv_w, m_sc_out_w, m_final_norm_w, v_ada_w, v_ada_b, v_mix_norm_w, v_mlp_norm_w, v_mlp_up, v_mlp_down, v_ssd_in_w, v_ssd_conv_w, v_ssd_conv_b, v_ssd_dt_bias, v_ssd_A_log, v_ssd_D, v_ssd_norm_w, v_ssd_out_w, v_sc_in_w, v_sc_conv_w, v_sc_out_w, v_final_norm_w):
    weights = dict(ada_w=ada_w, ada_b=ada_b, mix_norm_w=mix_norm_w, mlp_norm_w=mlp_norm_w, mlp_up=mlp_up,
                   mlp_down=mlp_down, ssd_in_w=ssd_in_w, ssd_conv_w=ssd_conv_w, ssd_conv_b=ssd_conv_b,
                   ssd_dt_bias=ssd_dt_bias, ssd_A_log=ssd_A_log, ssd_D=ssd_D, ssd_norm_w=ssd_norm_w,
                   ssd_out_w=ssd_out_w, sc_in_w=sc_in_w, sc_conv_w=sc_conv_w, sc_out_w=sc_out_w,
                   final_norm_w=final_norm_w)
    moms = dict(ada_w=m_ada_w, ada_b=m_ada_b, mix_norm_w=m_mix_norm_w, mlp_norm_w=m_mlp_norm_w, mlp_up=m_mlp_up,
                mlp_down=m_mlp_down, ssd_in_w=m_ssd_in_w, ssd_conv_w=m_ssd_conv_w, ssd_conv_b=m_ssd_conv_b,
                ssd_dt_bias=m_ssd_dt_bias, ssd_A_log=m_ssd_A_log, ssd_D=m_ssd_D, ssd_norm_w=m_ssd_norm_w,
                ssd_out_w=m_ssd_out_w, sc_in_w=m_sc_in_w, sc_conv_w=m_sc_conv_w, sc_out_w=m_sc_out_w,
                final_norm_w=m_final_norm_w)
    vars_ = dict(ada_w=v_ada_w, ada_b=v_ada_b, mix_norm_w=v_mix_norm_w, mlp_norm_w=v_mlp_norm_w, mlp_up=v_mlp_up,
                 mlp_down=v_mlp_down, ssd_in_w=v_ssd_in_w, ssd_conv_w=v_ssd_conv_w, ssd_conv_b=v_ssd_conv_b,
                 ssd_dt_bias=v_ssd_dt_bias, ssd_A_log=v_ssd_A_log, ssd_D=v_ssd_D, ssd_norm_w=v_ssd_norm_w,
                 ssd_out_w=v_ssd_out_w, sc_in_w=v_sc_in_w, sc_conv_w=v_sc_conv_w, sc_out_w=v_sc_out_w,
                 final_norm_w=v_final_norm_w)
    names = list(weights)

    L, D = x.shape[1], x.shape[2]
    d_inner = 2 * D
    n_heads = d_inner // SSD_P
    hpg = n_heads // SSD_G
    gw = d_inner // SSD_G
    conv_dim = d_inner + 2 * SSD_G * SSD_N
    zx_dim = d_inner + conv_dim
    zx_pad = -(-(zx_dim + LANES) // 512) * 512
    in_ws = ssd_in_w.shape[2]
    in_base, in_off, in_win = _window_geometry(in_ws)
    me = _my_index()
    x0 = x[0]
    tgt = loss_target[0]

    n_mod = ada_w.shape[2]
    (c_all,) = _exchange([c], "gather_c", gather=True)
    gather_handle = {}
    (gather_handle["ssd_in_w"],), token_in = _xfer_start(
        [ssd_in_w[0].astype(BF16)], "gather_start_ssd_in_w", gather=True, via_sibling=(0,), after=(c_all,))
    c_pad = jnp.pad(c_all.reshape(N_DEV, D), ((0, 16 - N_DEV), (0, 0)))
    ada_b_loc = lax.dynamic_slice_in_dim(ada_b, me * n_mod, n_mod, axis=1).reshape(2, 1, n_mod)
    mod_blk, cond_pad = _cond_mod(c_pad, ada_w, ada_b_loc, token_in, "cond_mod")
    gather_order = ["mod", "ssd_conv_w", "sc_conv_w", "ssd_out_w", "up0", "down0", "sc_in_w", "sc_out_w", "up1",
                    "down1"]
    gather_src = dict(mod=mod_blk, ssd_conv_w=ssd_conv_w[0], sc_conv_w=sc_conv_w[0],
                      ssd_out_w=ssd_out_w[0].astype(BF16),
                      up0=mlp_up[0].astype(BF16), down0=mlp_down[0].astype(BF16),
                      sc_in_w=sc_in_w[0].astype(BF16), sc_out_w=sc_out_w[0].astype(BF16),
                      up1=mlp_up[1].astype(BF16), down1=mlp_down[1].astype(BF16))
    handles, gather_token = _xfer_start([gather_src[k] for k in gather_order], "gather_start", gather=True,
                                        via_sibling=tuple(range(3, len(gather_order))))
    gather_handle.update(zip(gather_order, handles))

    def gathered(keys, after, forward):
        tag = "_".join(keys)
        lands = _xfer_wait([gather_handle[k] for k in keys], after, f"gather_wait_{tag}", gather=True)
        return _sibling_forward(lands, f"gather_forward_{tag}") if forward else lands

    def forward_behind(keys, after):
        tag = "_".join(keys)
        lands = _xfer_wait([gather_handle[k] for k in keys], after, f"gather_wait_{tag}", gather=True)
        fwd_handles, token = _sibling_forward_start(lands, f"gather_forward_start_{tag}")
        return (lambda done: _sibling_forward_wait(fwd_handles, done, f"gather_forward_wait_{tag}")), token

    (ssd_in_g,) = gathered(["ssd_in_w"], (gather_token, m_ssd_in_w, v_ssd_in_w), True)
    w_in_all = _shards_to_columns(ssd_in_g, in_base, in_off, in_win, zx_pad, "ssd_in_w_columns")
    (mod_all,) = gathered(["mod"], w_in_all, False)
    mod_mine = lax.dynamic_index_in_dim(mod_all, me, axis=2, keepdims=False)
    mod_mine = jnp.transpose(mod_mine, (1, 0, 2)).reshape(2, 6, 1, D)
    sh_m, sc_m, g_m, sh_f, sc_f, g_f = [[mod_mine[i, k] for i in range(2)] for k in range(6)]

    vec = lambda a: a.reshape(1, -1)
    small = {}

    _, h0 = _norm_mod_fwd(x0, None, None, vec(mix_norm_w[0]), sc_m[0], sh_m[0], "l0_mix_norm")
    cw_all, scw_all = gathered(["ssd_conv_w", "sc_conv_w"], h0, False)
    (zx,) = _mm_nn(h0, w_in_all, F32, "ssd_in_proj", tm=2048, tn=512)
    conv_b0 = vec(ssd_conv_b[0])
    conv_w_full = jnp.transpose(cw_all, (1, 0, 2)).reshape(SSD_K, conv_dim)
    sc_conv_full = jnp.transpose(scw_all, (1, 0, 2)).reshape(SC_K, D)
    xc = _ssd_conv_fwd(zx, conv_w_full, conv_b0, d_inner, conv_dim, "ssd_conv")
    bias_p = jnp.pad(ssd_dt_bias[0], (0, LANES - n_heads)).reshape(1, LANES)
    alog_p = jnp.pad(ssd_A_log[0], (0, LANES - n_heads)).reshape(1, LANES)
    d_lane = jnp.repeat(ssd_D[0], SSD_P).reshape(SSD_G, 1, gw)
    nw_g = ssd_norm_w[0].reshape(SSD_G, 1, gw)
    finish, token = forward_behind(["ssd_out_w"], xc)
    decay = _ssd_decay(zx, bias_p, alog_p, n_heads, zx_dim // LANES, "ssd_decay")
    y_ssd, yn, prev = _ssd_fwd(zx, xc, decay, d_lane, nw_g, d_inner, token, "ssd_scan")
    ups, downs = [None, None], [None, None]
    (ssd_out_g,) = finish(yn)
    w_ssd_out = ssd_out_g.reshape(-1, D)
    finish, token = forward_behind(["up0", "down0"], ssd_out_g)
    (mix0,) = _mm_nn(yn, w_ssd_out, F32, "ssd_out_proj", after=(token,))
    x1, h1 = _norm_mod_fwd(x0, mix0, g_m[0], vec(mlp_norm_w[0]), sc_f[0], sh_f[0], "l0_mlp_norm")
    ups[0], down0_g = finish(h1)
    downs[0] = down0_g.reshape(-1, D)
    u0, s0 = _mm_nn_blocked(h1, ups[0], "l0_mlp_up", _ep_relu2, [BF16, BF16])
    finish, token = forward_behind(["sc_in_w", "sc_out_w", "up1", "down1"], s0)
    (d0,) = _mm_nn(s0, downs[0], F32, "l0_mlp_down", after=(token,))
    x2, h2 = _norm_mod_fwd(x1, d0, g_f[0], vec(mix_norm_w[1]), sc_m[1], sh_m[1], "l1_mix_norm")
    sc_in_g, sc_out_g, ups[1], down1_g = finish(h2)
    w_sc_out, downs[1] = sc_out_g.reshape(-1, D), down1_g.reshape(-1, D)
    (proj,) = _mm_nn_blocked(h2, sc_in_g, "sc_in_proj", _ep_store(F32), [F32])
    yc = _sc_conv_fwd(proj, sc_conv_full, "sc_conv")
    (mix1,) = _mm_nn(yc, w_sc_out, F32, "sc_out_proj")
    x3, h3 = _norm_mod_fwd(x2, mix1, g_m[1], vec(mlp_norm_w[1]), sc_f[1], sh_f[1], "l1_mlp_norm")
    u1, s1 = _mm_nn_blocked(h3, ups[1], "l1_mlp_up", _ep_relu2, [BF16, BF16])
    (d1,) = _mm_nn(s1, downs[1], F32, "l1_mlp_down")

    dx, loss_lane, dfw, dd1, dg = _final_loss(x3, d1, g_f[1], vec(final_norm_w), tgt, "final_loss")
    small["final_norm_w"] = dfw

    dmod = [[None] * 6 for _ in range(2)]
    dmod[1][5] = dg

    def mlp_backward(i, dx_out, dd, x_mid, h_in, u, s, mix, gate):
        du = _mm_nt(dd, downs[i], BF16, f"l{i}_mlp_down_bwd", epilogue=_ep_relu2_bwd, extra=(u,))
        gdown = _mm_tn(s, dd, BF16, f"l{i}_mlp_down_wgrad").reshape(N_DEV, -1, D)
        gup = _mm_tn_blocked(h_in, du, BF16, f"l{i}_mlp_up_wgrad")
        (h_down, h_up), token = _xfer_start([gdown, gup], f"l{i}_mlp_grads_start", gather=False)
        grad_handle[f"mlp_down{i}"], grad_handle[f"mlp_up{i}"] = h_down, h_up
        dh = _mm_nt_blocked(du, ups[i], F32, f"l{i}_mlp_up_bwd", after=(token,))
        dxm, dsh, dsc, dnw, dmix, dgate = _norm_mod_bwd(dh, x_mid, vec(mlp_norm_w[i]), sc_f[i], dx_out,
                                                        f"l{i}_mlp_norm_bwd", branch=(mix, gate))
        dmod[i][3], dmod[i][4], dmod[i][2] = dsh, dsc, dgate
        return dxm, dmix, dnw

    grad_handle = {}
    dx3, dyc, dnw_mlp1 = mlp_backward(1, dx, dd1, x3, h3, u1, s1, mix1, g_m[1])
    g_sc_out = _mm_tn(yc, dyc, BF16, "sc_out_wgrad").reshape(N_DEV, -1, D)
    dconv_out = _mm_nt(dyc, w_sc_out, F32, "sc_out_bwd")
    dbg, dcg, dxv, dscw = _sc_conv_bwd(proj, sc_conv_full, dconv_out, "sc_conv_bwd")
    dproj = jnp.concatenate([dbg, dcg, dxv], axis=1)
    g_sc_in = _mm_tn_blocked(h2, dproj, BF16, "sc_in_wgrad")
    (grad_handle["sc_out_w0"], grad_handle["sc_in_w0"]), token = _xfer_start(
        [g_sc_out, g_sc_in], "sc_grads_start", gather=False)
    dh2 = _mm_nt_blocked(dproj, sc_in_g, F32, "sc_in_bwd", after=(token,))
    dx2, dsh, dsc, dnw_mix1, dd0, dg = _norm_mod_bwd(dh2, x2, vec(mix_norm_w[1]), sc_m[1], dx3, "l1_mix_norm_bwd",
                                                     branch=(d0, g_f[0]))
    dmod[1][0], dmod[1][1], dmod[0][5] = dsh, dsc, dg
    dx1, dyo, dnw_mlp0 = mlp_backward(0, dx2, dd0, x1, h1, u0, s0, mix0, g_m[0])
    g_ssd_out = _mm_tn(yn, dyo, BF16, "ssd_out_wgrad").reshape(N_DEV, -1, D)
    (grad_handle["ssd_out_w0"],), token = _xfer_start([g_ssd_out], "ssd_out_grad_start", gather=False)
    dyn = _mm_nt(dyo, w_ssd_out, F32, "ssd_out_bwd", after=(token,))
    dz, dxs, db_, dc_, ddt, dbias, dalog, dd_, dnw_ssd = _ssd_bwd(
        dyn, y_ssd, zx, xc, prev, decay, alog_p, d_lane, nw_g, d_inner, "ssd_scan_bwd")
    dzx, dcw, dcb = _ssd_conv_bwd(zx, conv_w_full, conv_b0, [dxs, db_, dc_], dz, d_inner, "ssd_conv_bwd")
    dzx = _dzx_finish(dzx, ddt, zx_dim, "ssd_dzx_finish")
    g_in_all = _mm_tn(h0, dzx, BF16, "ssd_in_wgrad", tn=512, tk=2048)
    g_ssd_in = _columns_to_shards(g_in_all, in_ws, in_base, in_off, in_win, "ssd_in_wgrad_shards")
    (grad_handle["ssd_in_w0"],), token = _xfer_start([g_ssd_in], "ssd_in_grad_start", gather=False)
    dh0 = _mm_nt(dzx, w_in_all, F32, "ssd_in_bwd", tm=1024, tk=dzx.shape[1] // 2, after=(token,))
    grad_x, dsh, dsc, dnw_mix0 = _norm_mod_bwd(dh0, x0, vec(mix_norm_w[0]), sc_m[0], dx1, "l0_mix_norm_bwd")
    dmod[0][0], dmod[0][1] = dsh, dsc

    small["ada_b"] = jnp.concatenate([jnp.concatenate(dmod[i], axis=1) for i in range(2)], axis=0)
    small["mix_norm_w"] = jnp.concatenate([dnw_mix0, dnw_mix1], axis=0)
    small["mlp_norm_w"] = jnp.concatenate([dnw_mlp0, dnw_mlp1], axis=0)
    small["ssd_conv_w"] = dcw
    small["ssd_conv_b"] = dcb
    small["ssd_norm_w"] = dnw_ssd.reshape(1, d_inner)
    small["sc_conv_w"] = dscw
    small["loss"] = loss_lane
    small_names = list(small)
    head_names = ["ssd_dt_bias", "ssd_A_log", "ssd_D"]
    handles, small_token = _xfer_start([small[k] for k in small_names] + [dbias, dalog, dd_],
                                       "small_grads_start", gather=True)

    out_g, out_d, out_m, out_v = {}, {}, {}, {}

    layer_res = {}

    def big_update(name, i, after):
        (parts,) = _xfer_wait([grad_handle[f"{name}{i}"]], after, f"grads_wait_{name}_{i}", gather=False)
        res = _adamw_sum(parts, weights[name], moms[name], vars_[name], i, f"adamw_{name}_{i}",
                         prev=layer_res.get(name))
        layer_res[name] = res
        return res[1]

    chain = small_token
    for name, i in [("mlp_down", 1), ("mlp_up", 1), ("sc_out_w", 0), ("sc_in_w", 0), ("mlp_down", 0),
                    ("mlp_up", 0), ("ssd_out_w", 0), ("ssd_in_w", 0)]:
        chain = big_update(name, i, chain)
    gathered_small = _xfer_wait(handles, chain, "small_grads_wait", gather=True)
    small_all = dict(zip(small_names + head_names, gathered_small))

    dmod_loc = lax.dynamic_slice_in_dim(small_all["ada_b"], me * n_mod, n_mod, axis=2)
    dmod_pad = jnp.pad(jnp.transpose(dmod_loc, (1, 0, 2)), ((0, 0), (0, 16 - N_DEV), (0, 0)))
    out_g["ada_w"], out_d["ada_w"], out_m["ada_w"], out_v["ada_w"] = _ada_adamw(
        cond_pad, dmod_pad, ada_w, m_ada_w, v_ada_w, "adamw_ada_w")

    for k in ("ssd_conv_w", "sc_conv_w"):
        n_loc = weights[k].shape[2]
        small_all[k] = lax.dynamic_slice_in_dim(small_all[k], me * n_loc, n_loc, axis=2)
    plain = [k for k in small_names if k != "loss"]
    as2d = lambda a: a.reshape(-1, a.shape[-1])
    res, loss_row = _adamw_small(
        [small_all[k] for k in plain], [tuple(as2d(d[k]) for d in (weights, moms, vars_)) for k in plain],
        [small_all[k] for k in head_names], [tuple(as2d(d[k]) for d in (weights, moms, vars_)) for k in head_names],
        small_all["loss"], "adamw_small")
    loss = loss_row[0, 0]
    for k, res4 in zip(plain + head_names, res):
        for r, dst in zip(res4, (out_g, out_d, out_m, out_v)):
            dst[k] = r.reshape(weights[k].shape)
    for name, res4 in layer_res.items():
        for r, dst in zip(res4, (out_g, out_d, out_m, out_v)):
            dst[name] = r

    return (loss, grad_x[None], *[out_g[k] for k in names], *[out_d[k] for k in names],
            *[out_m[k] for k in names], *[out_v[k] for k in names])
```

```python
import jax
import jax.numpy as jnp
from jax import lax
from jax.experimental import pallas as pl
from jax.experimental.pallas import tpu as pltpu

F32 = jnp.float32
BF16 = jnp.bfloat16
N_DEV = 8
MESH = pl.DeviceIdType.MESH

NORM_EPS = 1e-5
SSD_G = 4
SSD_P = 64
SSD_N = 128
SSD_CHUNK = 128
SSD_K = 4
SC_K = 3
LANES = 128

ADAM_LR = 0.001
ADAM_B1 = 0.9
ADAM_B2 = 0.999
ADAM_EPS = 1e-08
ADAM_WD = 0.01
ADAM_STEP = 10

VMEM_LIMIT = 56 * 1024 * 1024


def _pcall(body, **kw):
    return pl.pallas_call(body, **kw)


def _cparams(sem=None):
    if sem is None:
        return pltpu.CompilerParams(vmem_limit_bytes=VMEM_LIMIT)
    return pltpu.CompilerParams(dimension_semantics=sem, vmem_limit_bytes=VMEM_LIMIT)


def _my_index():
    return 4 * lax.axis_index("x") + 2 * lax.axis_index("y") + lax.axis_index("c")


_PEER_MASKS = [(0, 0, 1), (0, 1, 0), (0, 1, 1), (1, 0, 0), (1, 0, 1), (1, 1, 0), (1, 1, 1)]


def _peers():
    x, y, c = lax.axis_index("x"), lax.axis_index("y"), lax.axis_index("c")
    out = []
    for mx, my, mc in _PEER_MASKS:
        px = (1 - x) if mx else x
        py = (1 - y) if my else y
        pc = (1 - c) if mc else c
        out.append(((px, py, pc), 4 * px + 2 * py + pc))
    return out


def _exchange(arrs, name, gather):
    n = len(arrs)
    n_peer = N_DEV - 1

    def body(*refs):
        ins, outs = refs[:n], refs[n:2 * n]
        send_sems, recv_sems, local_sems = refs[2 * n:]
        me = _my_index()
        peers = _peers()
        started = []
        for a in range(n):
            src_own = ins[a] if gather else ins[a].at[me]
            own = pltpu.make_async_copy(src_own, outs[a].at[me], local_sems.at[a])
            own.start()
            started.append(own)
        sends = []
        for a in range(n):
            for k, (peer, pidx) in enumerate(peers):
                src = ins[a] if gather else ins[a].at[pidx]
                cp = pltpu.make_async_remote_copy(
                    src_ref=src, dst_ref=outs[a].at[me],
                    send_sem=send_sems.at[a * n_peer + k], recv_sem=recv_sems.at[a * n_peer + k],
                    device_id=peer, device_id_type=MESH)
                cp.start()
                sends.append(cp)
        for a in range(n):
            for k, (peer, pidx) in enumerate(peers):
                src = ins[a] if gather else ins[a].at[pidx]
                pltpu.make_async_remote_copy(
                    src_ref=src, dst_ref=outs[a].at[pidx],
                    send_sem=send_sems.at[a * n_peer + k], recv_sem=recv_sems.at[a * n_peer + k],
                    device_id=peer, device_id_type=MESH).wait_recv()
        for cp in sends:
            cp.wait_send()
        for own in started:
            own.wait()

    if gather:
        out_shape = [jax.ShapeDtypeStruct((N_DEV,) + a.shape, a.dtype) for a in arrs]
    else:
        out_shape = [jax.ShapeDtypeStruct(a.shape, a.dtype) for a in arrs]
    any_spec = pl.BlockSpec(memory_space=pl.ANY)
    outs = _pcall(
        body, name=name, out_shape=out_shape,
        in_specs=[any_spec] * n, out_specs=[any_spec] * n,
        scratch_shapes=[pltpu.SemaphoreType.DMA((n * n_peer,)), pltpu.SemaphoreType.DMA((n * n_peer,)),
                        pltpu.SemaphoreType.DMA((n,))],
        compiler_params=pltpu.CompilerParams(has_side_effects=True),
    )(*arrs)
    return list(outs)


def _sibling_forward_start(lands, name):
    n = len(lands)
    n_fwd = len(_OTHER_CHIPS)

    def body(*refs):
        ins, bufs = refs[:n], refs[3 * n:4 * n]
        token = refs[-1]
        sibling = (lax.axis_index("x"), lax.axis_index("y"), 1 - lax.axis_index("c"))
        peers = _peers()
        for a in range(n):
            send_sems, recv_sems = refs[n + 2 * a], refs[n + 2 * a + 1]
            for j, k in enumerate(_OTHER_CHIPS):
                slot = peers[k][1]
                pltpu.make_async_remote_copy(
                    src_ref=ins[a].at[slot], dst_ref=bufs[a].at[slot], send_sem=send_sems.at[j],
                    recv_sem=recv_sems.at[j], device_id=sibling, device_id_type=MESH).start()
        token[...] = jnp.zeros_like(token)

    out_shape, out_specs = [], []
    for _ in range(n):
        out_shape += [pltpu.SemaphoreType.DMA((n_fwd,)), pltpu.SemaphoreType.DMA((n_fwd,))]
        out_specs += [_SEM, _SEM]
    out_shape += [pltpu.HBM(a.shape, a.dtype) for a in lands] + [jax.ShapeDtypeStruct((8, LANES), F32)]
    out_specs += [_HBM] * n + [pl.BlockSpec(memory_space=pltpu.VMEM)]
    outs = _pcall(
        body, name=name, out_shape=tuple(out_shape), in_specs=[_HBM] * n, out_specs=tuple(out_specs),
        input_output_aliases={a: 2 * n + a for a in range(n)},
        compiler_params=pltpu.CompilerParams(has_side_effects=_DATAFLOW),
    )(*[pltpu.with_memory_space_constraint(a, pltpu.HBM) for a in lands])
    return [(outs[2 * n + a], outs[2 * a], outs[2 * a + 1]) for a in range(n)], outs[-1]


def _sibling_forward_wait(handles, after, name):
    n = len(handles)

    def body(*refs):
        sibling = (lax.axis_index("x"), lax.axis_index("y"), 1 - lax.axis_index("c"))
        peers = _peers()
        for a in range(n):
            buf, send_sems, recv_sems = refs[3 * a:3 * a + 3]
            for j, k in enumerate(_OTHER_CHIPS):
                (px, py, pc), slot = peers[k]
                theirs = 4 * px + 2 * py + (1 - pc)
                cp = pltpu.make_async_remote_copy(
                    src_ref=buf.at[slot], dst_ref=buf.at[theirs], send_sem=send_sems.at[j],
                    recv_sem=recv_sems.at[j], device_id=sibling, device_id_type=MESH)
                cp.wait_send()
                cp.wait_recv()

    operands, in_specs = [], []
    for h in handles:
        operands += list(h)
        in_specs += [_HBM, _SEM, _SEM]
    outs = _pcall(
        body, name=name, out_shape=tuple(pltpu.HBM(h[0].shape, h[0].dtype) for h in handles),
        in_specs=in_specs + [pl.BlockSpec(memory_space=pl.ANY)], out_specs=tuple([_HBM] * n),
        input_output_aliases={3 * a: a for a in range(n)},
        compiler_params=pltpu.CompilerParams(has_side_effects=_DATAFLOW),
    )(*operands, after)
    return list(outs)


_HBM = pl.BlockSpec(memory_space=pltpu.HBM)
_SEM = pl.BlockSpec(memory_space=pltpu.SEMAPHORE)
_DATAFLOW = pltpu.SideEffectType.DATAFLOW_SIDE_EFFECTING


_ALL_PEERS = tuple(range(N_DEV - 1))
_SAME_CORE_PEERS = (0, 1, 3, 5)
_OTHER_CHIPS = (1, 3, 5)


def _xfer_start(arrs, name, gather, via_sibling=(), after=()):
    n = len(arrs)
    n_peer = N_DEV - 1
    n_after = len(after)
    peer_ks = [_SAME_CORE_PEERS if a in via_sibling else _ALL_PEERS for a in range(n)]

    def body(*refs):
        ins, lands = refs[:n], refs[n:2 * n]
        sems = refs[2 * n + n_after:5 * n + n_after]
        token = refs[-1]
        me = _my_index()
        peers = _peers()
        for a in range(n):
            send_sems, recv_sems, loc_sem = sems[3 * a:3 * a + 3]
            src_own = ins[a] if gather else ins[a].at[me]
            pltpu.make_async_copy(src_own, lands[a].at[me], loc_sem).start()
            for k in peer_ks[a]:
                peer, pidx = peers[k]
                src = ins[a] if gather else ins[a].at[pidx]
                pltpu.make_async_remote_copy(
                    src_ref=src, dst_ref=lands[a].at[me], send_sem=send_sems.at[k], recv_sem=recv_sems.at[k],
                    device_id=peer, device_id_type=MESH).start()
        token[...] = jnp.zeros_like(token)

    land_shapes = [((N_DEV,) + a.shape) if gather else a.shape for a in arrs]
    out_shape, out_specs = [], []
    for _ in range(n):
        out_shape += [pltpu.SemaphoreType.DMA((n_peer,)), pltpu.SemaphoreType.DMA((n_peer,)),
                      pltpu.SemaphoreType.DMA(())]
        out_specs += [_SEM, _SEM, _SEM]
    out_shape += [pltpu.HBM(a.shape, a.dtype) for a in arrs]
    out_shape += [pltpu.HBM(s, a.dtype) for s, a in zip(land_shapes, arrs)]
    out_shape += [jax.ShapeDtypeStruct((8, LANES), F32)]
    out_specs += [_HBM] * (2 * n) + [pl.BlockSpec(memory_space=pltpu.VMEM)]
    aliases = {}
    for a in range(n):
        aliases[a] = 3 * n + a
        aliases[n + a] = 4 * n + a
    operands = [pltpu.with_memory_space_constraint(a, pltpu.HBM) for a in arrs]
    operands += [pltpu.with_memory_space_constraint(lax.empty(s, a.dtype), pltpu.HBM)
                 for s, a in zip(land_shapes, arrs)]
    outs = _pcall(
        body, name=name, out_shape=tuple(out_shape),
        in_specs=[_HBM] * (2 * n) + [pl.BlockSpec(memory_space=pl.ANY)] * n_after, out_specs=tuple(out_specs),
        input_output_aliases=aliases,
        compiler_params=pltpu.CompilerParams(has_side_effects=_DATAFLOW),
    )(*operands, *after)
    handles = []
    for a in range(n):
        handles.append((outs[3 * n + a], outs[4 * n + a], outs[3 * a], outs[3 * a + 1], outs[3 * a + 2],
                        peer_ks[a]))
    return handles, outs[-1]


def _xfer_wait(handles, after, name, gather):
    n = len(handles)
    after = tuple(after) if isinstance(after, (tuple, list)) else (after,)
    peer_ks = [h[5] for h in handles]

    def body(*refs):
        me = _my_index()
        peers = _peers()
        for a in range(n):
            src_ref, land_ref, send_ref, recv_ref, loc_ref = refs[5 * a:5 * a + 5]
            src_own = src_ref if gather else src_ref.at[me]
            pltpu.make_async_copy(src_own, land_ref.at[me], loc_ref).wait()
            for k in peer_ks[a]:
                peer, pidx = peers[k]
                src = src_ref if gather else src_ref.at[pidx]
                cp = pltpu.make_async_remote_copy(
                    src_ref=src, dst_ref=land_ref.at[pidx], send_sem=send_ref.at[k], recv_sem=recv_ref.at[k],
                    device_id=peer, device_id_type=MESH)
                cp.wait_send()
                cp.wait_recv()

    operands, in_specs, out_shape, aliases = [], [], [], {}
    for a, h in enumerate(handles):
        operands += list(h[:5])
        in_specs += [_HBM, _HBM, _SEM, _SEM, _SEM]
        out_shape += [pltpu.HBM(h[0].shape, h[0].dtype), pltpu.HBM(h[1].shape, h[1].dtype)]
        aliases[5 * a] = 2 * a
        aliases[5 * a + 1] = 2 * a + 1
    outs = _pcall(
        body, name=name, out_shape=tuple(out_shape),
        in_specs=in_specs + [pl.BlockSpec(memory_space=pl.ANY)] * len(after),
        out_specs=tuple([_HBM] * (2 * n)), input_output_aliases=aliases,
        compiler_params=pltpu.CompilerParams(has_side_effects=_DATAFLOW),
    )(*operands, *after)
    return [outs[2 * a + 1] for a in range(n)]


def _sibling_forward(lands, name):
    n = len(lands)
    n_fwd = len(_OTHER_CHIPS)

    def body(*refs):
        ins, bufs = refs[:n], refs[n:2 * n]
        send_sems, recv_sems = refs[2 * n:]
        x, y, c = lax.axis_index("x"), lax.axis_index("y"), lax.axis_index("c")
        sibling = (x, y, 1 - c)
        peers = _peers()
        sends = []
        for a in range(n):
            for j, k in enumerate(_OTHER_CHIPS):
                slot = peers[k][1]
                cp = pltpu.make_async_remote_copy(
                    src_ref=ins[a].at[slot], dst_ref=bufs[a].at[slot],
                    send_sem=send_sems.at[a * n_fwd + j], recv_sem=recv_sems.at[a * n_fwd + j],
                    device_id=sibling, device_id_type=MESH)
                cp.start()
                sends.append(cp)
        for a in range(n):
            for j, k in enumerate(_OTHER_CHIPS):
                (px, py, pc), slot = peers[k]
                theirs = 4 * px + 2 * py + (1 - pc)
                pltpu.make_async_remote_copy(
                    src_ref=ins[a].at[slot], dst_ref=bufs[a].at[theirs],
                    send_sem=send_sems.at[a * n_fwd + j], recv_sem=recv_sems.at[a * n_fwd + j],
                    device_id=sibling, device_id_type=MESH).wait_recv()
        for cp in sends:
            cp.wait_send()

    any_spec = pl.BlockSpec(memory_space=pl.ANY)
    outs = _pcall(
        body, name=name, out_shape=[jax.ShapeDtypeStruct(a.shape, a.dtype) for a in lands],
        in_specs=[any_spec] * n, out_specs=[any_spec] * n,
        input_output_aliases={a: a for a in range(n)},
        scratch_shapes=[pltpu.SemaphoreType.DMA((n * n_fwd,)), pltpu.SemaphoreType.DMA((n * n_fwd,))],
        compiler_params=pltpu.CompilerParams(has_side_effects=True),
    )(*lands)
    return list(outs)


_DIMS = {"nn": (((1,), (0,)), ((), ())), "nt": (((1,), (1,)), ((), ())), "tn": (((0,), (0,)), ((), ()))}


def _dot(a, b, mode="nn"):
    if mode == "nt_blocks":
        n = b.shape[2]
        return sum(_dot(a[:, i * n:(i + 1) * n], b[i], "nt") for i in range(b.shape[0]))
    return lax.dot_general(a, b, _DIMS[mode], preferred_element_type=F32)


def _mm(a, b, *, mode, grid, a_spec, b_spec, out_shape, out_specs, acc_shape, epilogue, name,
        extra=(), extra_specs=(), after=(), semantics=("parallel", "parallel", "arbitrary")):
    nk = grid[2]
    n_extra = len(extra)
    n_in = 2 + n_extra + len(after)

    def body_single(*refs):
        a_ref, b_ref = refs[0], refs[1]
        epilogue(_dot(a_ref[...], b_ref[...], mode), refs[2:2 + n_extra], refs[n_in:])

    def body_acc(*refs):
        a_ref, b_ref = refs[0], refs[1]
        ex = refs[2:2 + n_extra]
        outs = refs[n_in:-1]
        acc = refs[-1]
        k = pl.program_id(2)

        @pl.when(k == 0)
        def _():
            acc[...] = jnp.zeros_like(acc)

        acc[...] += _dot(a_ref[...], b_ref[...], mode)

        @pl.when(k == nk - 1)
        def _():
            epilogue(acc[...], ex, outs)

    return _pcall(
        body_single if nk == 1 else body_acc, name=name, grid=grid, out_shape=out_shape,
        in_specs=[a_spec, b_spec] + list(extra_specs) + [pl.BlockSpec(memory_space=pl.ANY)] * len(after),
        out_specs=out_specs,
        scratch_shapes=[] if nk == 1 else [pltpu.VMEM(acc_shape, F32)],
        compiler_params=_cparams(semantics),
    )(a, b, *extra, *after)


def _ep_store(dtype):
    def ep(acc, ex, outs):
        outs[0][...] = acc.astype(dtype)
    return ep


def _ep_relu2(acc, ex, outs):
    outs[0][...] = acc.astype(BF16)
    r = jnp.maximum(acc, 0.0)
    outs[1][...] = (r * r).astype(BF16)


def _ep_relu2_bwd(acc, ex, outs):
    u = ex[0][...].astype(F32)
    outs[0][...] = (acc * (2.0 * jnp.maximum(u, 0.0))).astype(BF16)


def _tile(n, want):
    t = min(n, want)
    while n % t:
        t //= 2
    return t


def _mm_nn(a, w, out_dtype, name, tm=2048, tn=1024, tk=1024, epilogue=None, out_dtypes=None, after=()):
    M, K = a.shape
    N = w.shape[1]
    tm, tn, tk = _tile(M, tm), _tile(N, tn), _tile(K, tk)
    out_dtypes = out_dtypes or [out_dtype]
    return _mm(a, w, mode="nn", grid=(M // tm, N // tn, K // tk),
               a_spec=pl.BlockSpec((tm, tk), lambda i, j, k: (i, k)),
               b_spec=pl.BlockSpec((tk, tn), lambda i, j, k: (k, j)),
               out_shape=[jax.ShapeDtypeStruct((M, N), d) for d in out_dtypes],
               out_specs=[pl.BlockSpec((tm, tn), lambda i, j, k: (i, j)) for _ in out_dtypes],
               acc_shape=(tm, tn), epilogue=epilogue or _ep_store(out_dtype), name=name, after=after)


def _mm_nn_blocked(a, wg, name, epilogue, out_dtypes, tm=2048):
    M, K = a.shape
    n = wg.shape[2]
    tm = _tile(M, tm)
    return _mm(a, wg, mode="nn", grid=(M // tm, N_DEV, 1),
               a_spec=pl.BlockSpec((tm, K), lambda i, j, k: (i, 0)),
               b_spec=pl.BlockSpec((None, K, n), lambda i, j, k: (j, 0, 0)),
               out_shape=[jax.ShapeDtypeStruct((M, N_DEV * n), d) for d in out_dtypes],
               out_specs=[pl.BlockSpec((tm, n), lambda i, j, k: (i, j)) for _ in out_dtypes],
               acc_shape=(tm, n), epilogue=epilogue, name=name)


def _mm_nt(a, w, out_dtype, name, tm=2048, tn=1024, tk=1024, epilogue=None, extra=(), extra_specs=(),
           after=()):
    M, K = a.shape
    N = w.shape[0]
    tm, tn, tk = _tile(M, tm), _tile(N, tn), _tile(K, tk)
    if extra and not extra_specs:
        extra_specs = [pl.BlockSpec((tm, tn), lambda i, j, k: (i, j)) for _ in extra]
    return _mm(a, w, mode="nt", grid=(M // tm, N // tn, K // tk),
               a_spec=pl.BlockSpec((tm, tk), lambda i, j, k: (i, k)),
               b_spec=pl.BlockSpec((tn, tk), lambda i, j, k: (j, k)),
               out_shape=[jax.ShapeDtypeStruct((M, N), out_dtype)],
               out_specs=[pl.BlockSpec((tm, tn), lambda i, j, k: (i, j))],
               acc_shape=(tm, tn), epilogue=epilogue or _ep_store(out_dtype), name=name,
               extra=extra, extra_specs=extra_specs, after=after)[0]


def _mm_nt_blocked(a, wg, out_dtype, name, tm=1024, after=()):
    M = a.shape[0]
    kout, n = wg.shape[1], wg.shape[2]
    tm = _tile(M, tm)
    per = 4
    return _mm(a, wg, mode="nt_blocks", grid=(M // tm, 1, N_DEV // per),
               a_spec=pl.BlockSpec((tm, per * n), lambda i, j, k: (i, k)),
               b_spec=pl.BlockSpec((per, kout, n), lambda i, j, k: (k, 0, 0)),
               out_shape=[jax.ShapeDtypeStruct((M, kout), out_dtype)],
               out_specs=[pl.BlockSpec((tm, kout), lambda i, j, k: (i, 0))],
               acc_shape=(tm, kout), epilogue=_ep_store(out_dtype), name=name, after=after)[0]


def _mm_tn(a, b, out_dtype, name, tm=1024, tn=1024, tk=2048):
    K, M = a.shape
    N = b.shape[1]
    tm, tn, tk = _tile(M, tm), _tile(N, tn), _tile(K, tk)
    return _mm(a, b, mode="tn", grid=(M // tm, N // tn, K // tk),
               a_spec=pl.BlockSpec((tk, tm), lambda i, j, k: (k, i)),
               b_spec=pl.BlockSpec((tk, tn), lambda i, j, k: (k, j)),
               out_shape=[jax.ShapeDtypeStruct((M, N), out_dtype)],
               out_specs=[pl.BlockSpec((tm, tn), lambda i, j, k: (i, j))],
               acc_shape=(tm, tn), epilogue=_ep_store(out_dtype), name=name)[0]


def _mm_tn_blocked(a, b, out_dtype, name, tm=1024, tk=2048):
    K, M = a.shape
    n = b.shape[1] // N_DEV
    tm, tk = _tile(M, tm), _tile(K, tk)
    return _mm(a, b, mode="tn", grid=(M // tm, N_DEV, K // tk),
               a_spec=pl.BlockSpec((tk, tm), lambda i, j, k: (k, i)),
               b_spec=pl.BlockSpec((tk, n), lambda i, j, k: (k, j)),
               out_shape=[jax.ShapeDtypeStruct((N_DEV, M, n), out_dtype)],
               out_specs=[pl.BlockSpec((None, tm, n), lambda i, j, k: (j, i, 0))],
               acc_shape=(tm, n), epilogue=_ep_store(out_dtype), name=name)[0]


def _window_geometry(ws):
    base = [(ws * k // LANES) * LANES for k in range(N_DEV)]
    off = [ws * k - base[k] for k in range(N_DEV)]
    win = -(-(max(off) + ws) // LANES) * LANES
    return base, off, win


def _shards_to_columns(xg, base, off, win, n_out, name, tr=256):
    R, ws = xg.shape[1], xg.shape[2]
    tr = _tile(R, tr)
    nb_win = win // LANES

    def body(x_ref, o_ref, frame_ref):
        written = set()
        frame_ref[...] = jnp.zeros_like(frame_ref)
        for k in range(N_DEV):
            frame_ref[:, 0:ws] = x_ref[k].astype(F32)
            window = frame_ref[...]
            if off[k]:
                window = pltpu.roll(window, off[k], 1)
            for i in range(nb_win):
                b = base[k] // LANES + i
                if b * LANES >= n_out:
                    continue
                cols = slice(b * LANES, (b + 1) * LANES)
                blk = window[:, i * LANES:(i + 1) * LANES]
                if b in written:
                    blk = blk + o_ref[:, cols].astype(F32)
                o_ref[:, cols] = blk.astype(o_ref.dtype)
                written.add(b)
        for b in range(n_out // LANES):
            if b not in written:
                o_ref[:, b * LANES:(b + 1) * LANES] = jnp.zeros((tr, LANES), o_ref.dtype)

    return _pcall(
        body, name=name, grid=(R // tr,), out_shape=jax.ShapeDtypeStruct((R, n_out), xg.dtype),
        in_specs=[pl.BlockSpec((N_DEV, tr, ws), lambda i: (0, i, 0))],
        out_specs=pl.BlockSpec((tr, n_out), lambda i: (i, 0)),
        scratch_shapes=[pltpu.VMEM((tr, win), F32)],
        compiler_params=_cparams(("parallel",)))(xg)


def _columns_to_shards(x, ws, base, off, win, name, tr=256):
    R = x.shape[0]
    tr = _tile(R, tr)

    def body(x_ref, o_ref, frame_ref):
        for k in range(N_DEV):
            window = x_ref[:, base[k]:base[k] + win].astype(F32)
            if off[k]:
                window = pltpu.roll(window, win - off[k], 1)
            frame_ref[...] = window
            o_ref[k] = frame_ref[:, 0:ws].astype(o_ref.dtype)

    return _pcall(
        body, name=name, grid=(R // tr,), out_shape=jax.ShapeDtypeStruct((N_DEV, R, ws), x.dtype),
        in_specs=[pl.BlockSpec((tr, x.shape[1]), lambda i: (i, 0))],
        out_specs=pl.BlockSpec((N_DEV, tr, ws), lambda i: (0, i, 0)),
        scratch_shapes=[pltpu.VMEM((tr, win), F32)],
        compiler_params=_cparams(("parallel",)))(x)


def _sigmoid(x):
    return 1.0 / (1.0 + jnp.exp(-x))


def _row_spec(tm, d):
    return pl.BlockSpec((tm, d), lambda i: (i, 0))


def _vec_spec(d):
    return pl.BlockSpec((1, d), lambda i: (0, 0))


def _norm_mod_fwd(x, y, gate, nw, scale, shift, name, tm=512):
    L, D = x.shape
    tm = _tile(L, tm)
    has_res = y is not None

    def body(*refs):
        if has_res:
            x_ref, y_ref, g_ref, nw_ref, sc_ref, sh_ref, xo_ref, h_ref = refs
            xn = x_ref[...] + g_ref[...] * y_ref[...]
            xo_ref[...] = xn
        else:
            x_ref, nw_ref, sc_ref, sh_ref, h_ref = refs
            xn = x_ref[...]
        rstd = lax.rsqrt(jnp.mean(xn * xn, axis=-1, keepdims=True) + NORM_EPS)
        h = xn * rstd * nw_ref[...] * (1.0 + sc_ref[...]) + sh_ref[...]
        h_ref[...] = h.astype(BF16)

    row, vec = _row_spec(tm, D), _vec_spec(D)
    if has_res:
        ins, in_specs = (x, y, gate, nw, scale, shift), [row, row, vec, vec, vec, vec]
        out_shape = [jax.ShapeDtypeStruct((L, D), F32), jax.ShapeDtypeStruct((L, D), BF16)]
        out_specs = [row, row]
    else:
        ins, in_specs = (x, nw, scale, shift), [row, vec, vec, vec]
        out_shape = [jax.ShapeDtypeStruct((L, D), BF16)]
        out_specs = [row]
    outs = _pcall(body, name=name, grid=(L // tm,), out_shape=out_shape, in_specs=in_specs,
                  out_specs=out_specs, compiler_params=_cparams(("parallel",)))(*ins)
    return outs if has_res else (x, outs[0])


def _gated_branch_bwd(dx, branch, y_ref, g_ref, dy_ref, dg_ref):
    if branch is None:
        return
    dy_ref[...] = (g_ref[...] * dx).astype(BF16)
    dg_ref[...] += jnp.sum(dx * y_ref[...], axis=0, keepdims=True)


def _norm_mod_bwd(dh, x, nw, scale, dres, name, branch=None, tm=512):
    L, D = x.shape
    tm = _tile(L, tm)
    nb = 0 if branch is None else 2

    def body(dh_ref, x_ref, nw_ref, sc_ref, dres_ref, *rest):
        y_ref, g_ref = rest[:nb] if nb else (None, None)
        dx_ref, dsh_ref, dsc_ref, dnw_ref = rest[nb:nb + 4]
        dy_ref, dg_ref = rest[nb + 4:] if nb else (None, None)

        @pl.when(pl.program_id(0) == 0)
        def _():
            dsh_ref[...] = jnp.zeros_like(dsh_ref)
            dsc_ref[...] = jnp.zeros_like(dsc_ref)
            dnw_ref[...] = jnp.zeros_like(dnw_ref)
            if nb:
                dg_ref[...] = jnp.zeros_like(dg_ref)

        xv = x_ref[...]
        dh_v = dh_ref[...]
        nw_v = nw_ref[...]
        rstd = lax.rsqrt(jnp.mean(xv * xv, axis=-1, keepdims=True) + NORM_EPS)
        xhat = xv * rstd
        dsh_ref[...] += jnp.sum(dh_v, axis=0, keepdims=True)
        dsc_ref[...] += jnp.sum(dh_v * (xhat * nw_v), axis=0, keepdims=True)
        dr = dh_v * (1.0 + sc_ref[...])
        dnw_ref[...] += jnp.sum(dr * xhat, axis=0, keepdims=True)
        dxh = dr * nw_v
        dx = rstd * (dxh - xhat * jnp.mean(dxh * xhat, axis=-1, keepdims=True)) + dres_ref[...]
        dx_ref[...] = dx
        _gated_branch_bwd(dx, branch, y_ref, g_ref, dy_ref, dg_ref)

    row, vec = _row_spec(tm, D), _vec_spec(D)
    extra_in = [] if branch is None else list(branch)
    return _pcall(
        body, name=name, grid=(L // tm,),
        out_shape=[jax.ShapeDtypeStruct((L, D), F32)] + [jax.ShapeDtypeStruct((1, D), F32)] * 3
        + ([jax.ShapeDtypeStruct((L, D), BF16), jax.ShapeDtypeStruct((1, D), F32)] if nb else []),
        in_specs=[row, row, vec, vec, row] + ([row, vec] if nb else []),
        out_specs=[row, vec, vec, vec] + ([row, vec] if nb else []),
        compiler_params=_cparams(("arbitrary",)))(dh, x, nw, scale, dres, *extra_in)


def _final_loss(x, y, gate, fw, target, name, tm=512):
    L, D = x.shape
    tm = _tile(L, tm)

    def body(x_ref, y_ref, g_ref, fw_ref, t_ref, dx_ref, loss_ref, dfw_ref, dy_ref, dg_ref):
        @pl.when(pl.program_id(0) == 0)
        def _():
            loss_ref[...] = jnp.zeros_like(loss_ref)
            dfw_ref[...] = jnp.zeros_like(dfw_ref)
            dg_ref[...] = jnp.zeros_like(dg_ref)

        xn = x_ref[...] + g_ref[...] * y_ref[...]
        fw_v = fw_ref[...]
        rstd = lax.rsqrt(jnp.mean(xn * xn, axis=-1, keepdims=True) + NORM_EPS)
        xhat = xn * rstd
        diff = xhat * fw_v - t_ref[...]
        loss_ref[...] += jnp.sum(diff * diff, axis=0, keepdims=True)
        dyf = diff * (1.0 / D)
        dfw_ref[...] += jnp.sum(dyf * xhat, axis=0, keepdims=True)
        dxh = dyf * fw_v
        dx = rstd * (dxh - xhat * jnp.mean(dxh * xhat, axis=-1, keepdims=True))
        dx_ref[...] = dx
        _gated_branch_bwd(dx, True, y_ref, g_ref, dy_ref, dg_ref)

    row, vec = _row_spec(tm, D), _vec_spec(D)
    return _pcall(
        body, name=name, grid=(L // tm,),
        out_shape=[jax.ShapeDtypeStruct((L, D), F32), jax.ShapeDtypeStruct((1, D), F32),
                   jax.ShapeDtypeStruct((1, D), F32), jax.ShapeDtypeStruct((L, D), BF16),
                   jax.ShapeDtypeStruct((1, D), F32)],
        in_specs=[row, row, vec, vec, row], out_specs=[row, vec, vec, row, vec],
        compiler_params=_cparams(("arbitrary",)))(x, y, gate, fw, target)


def _shift_down(v, s, row):
    if s == 0:
        return v
    return jnp.where(row >= s, pltpu.roll(v, s, 0), 0.0)


CONV_ROWS = 32


def _shifted_rows(x_ref, r0, n, lanes=slice(None)):
    cur = x_ref[r0:r0 + CONV_ROWS, lanes]
    if r0 >= n - 1:
        return [cur] + [x_ref[r0 - s:r0 - s + CONV_ROWS, lanes] for s in range(1, n)]
    row = lax.broadcasted_iota(jnp.int32, cur.shape, 0)
    return [_shift_down(cur, s, row) for s in range(n)]


def _ssd_conv_fwd(zx, w, b, col0, width, name, cb=512):
    L = zx.shape[0]
    nb = width // cb
    off = col0 // cb

    def body(x_ref, w_ref, b_ref, o_ref):
        for l0 in range(0, cb, LANES):
            lanes = slice(l0, l0 + LANES)
            taps = [w_ref[k:k + 1, lanes] for k in range(SSD_K)]
            bias = b_ref[:, lanes]
            for r0 in range(0, L, CONV_ROWS):
                taps_in = _shifted_rows(x_ref, r0, SSD_K, lanes)
                acc = bias + taps[SSD_K - 1] * taps_in[0]
                for s in range(1, SSD_K):
                    acc = acc + taps[SSD_K - 1 - s] * taps_in[s]
                o_ref[r0:r0 + CONV_ROWS, lanes] = acc * _sigmoid(acc)

    return _pcall(
        body, name=name, grid=(nb,), out_shape=jax.ShapeDtypeStruct((L, width), F32),
        in_specs=[pl.BlockSpec((L, cb), lambda j: (0, off + j)),
                  pl.BlockSpec((SSD_K, cb), lambda j: (0, j)),
                  pl.BlockSpec((1, cb), lambda j: (0, j))],
        out_specs=pl.BlockSpec((L, cb), lambda j: (0, j)),
        compiler_params=_cparams(("parallel",)))(zx, w, b)


def _ssd_conv_bwd(zx, w, b, d_parts, dzx, col0, name, cb=128):
    L = zx.shape[0]
    widths = [p.shape[1] for p in d_parts]
    width = sum(widths)
    nb = width // cb
    off = col0 // cb
    starts = [sum(widths[:i]) // cb for i in range(len(d_parts))]
    counts = [wd // cb for wd in widths]

    def body(x_ref, w_ref, b_ref, *rest):
        d_refs = rest[:len(d_parts)]
        dx_ref, dw_ref, db_ref, dpre_ref = rest[len(d_parts) + 1:]
        j = pl.program_id(0)
        taps = [w_ref[k:k + 1, :] for k in range(SSD_K)]
        bias = b_ref[...]
        fold = lambda v: sum(v[r:r + 8, :] for r in range(0, CONV_ROWS, 8))
        db8 = jnp.zeros((8, cb), F32)
        dw8 = [jnp.zeros((8, cb), F32) for _ in range(SSD_K)]
        for r0 in range(0, L, CONV_ROWS):
            rows = slice(r0, r0 + CONV_ROWS)
            d_val = d_refs[-1][rows, :]
            for i in range(len(d_parts) - 2, -1, -1):
                d_val = jnp.where(j < starts[i + 1], d_refs[i][rows, :], d_val)
            taps_in = _shifted_rows(x_ref, r0, SSD_K)
            acc = bias + taps[SSD_K - 1] * taps_in[0]
            for s in range(1, SSD_K):
                acc = acc + taps[SSD_K - 1 - s] * taps_in[s]
            sig = _sigmoid(acc)
            dpre = d_val * (sig * (1.0 + acc * (1.0 - sig)))
            dpre_ref[rows, :] = dpre
            db8 = db8 + fold(dpre)
            for s in range(SSD_K):
                dw8[s] = dw8[s] + fold(dpre * taps_in[s])
        dpre_ref[L:L + 8, :] = jnp.zeros((8, cb), F32)
        db_ref[...] = jnp.sum(db8, axis=0, keepdims=True)
        for s in range(SSD_K):
            dw_ref[SSD_K - 1 - s:SSD_K - s, :] = jnp.sum(dw8[s], axis=0, keepdims=True)
        for r0 in range(0, L, CONV_ROWS):
            dx = taps[SSD_K - 1] * dpre_ref[r0:r0 + CONV_ROWS, :]
            for s in range(1, SSD_K):
                dx = dx + taps[SSD_K - 1 - s] * dpre_ref[r0 + s:r0 + s + CONV_ROWS, :]
            dx_ref[r0:r0 + CONV_ROWS, :] = dx.astype(BF16)

    def part_spec(i):
        return pl.BlockSpec((L, cb), lambda j: (0, jnp.clip(j - starts[i], 0, counts[i] - 1)))

    return _pcall(
        body, name=name, grid=(nb,),
        out_shape=[jax.ShapeDtypeStruct(dzx.shape, BF16), jax.ShapeDtypeStruct((SSD_K, width), F32),
                   jax.ShapeDtypeStruct((1, width), F32)],
        in_specs=[pl.BlockSpec((L, cb), lambda j: (0, off + j)),
                  pl.BlockSpec((SSD_K, cb), lambda j: (0, j)),
                  pl.BlockSpec((1, cb), lambda j: (0, j))]
        + [part_spec(i) for i in range(len(d_parts))] + [pl.BlockSpec(memory_space=pl.ANY)],
        out_specs=[pl.BlockSpec((L, cb), lambda j: (0, off + j)),
                   pl.BlockSpec((SSD_K, cb), lambda j: (0, j)),
                   pl.BlockSpec((1, cb), lambda j: (0, j))],
        input_output_aliases={3 + len(d_parts): 0},
        scratch_shapes=[pltpu.VMEM((L + 8, cb), F32)],
        compiler_params=_cparams(("parallel",)))(zx, w, b, *d_parts, dzx)


def _dzx_finish(dzx, ddt, col0, name, tl=512):
    G, L, _ = ddt.shape
    tail = dzx.shape[1] - col0
    tl = _tile(L, tl)

    def body(ddt_ref, dzx_ref, o_ref):
        s = ddt_ref[0]
        for g in range(1, G):
            s = s + ddt_ref[g]
        o_ref[:, 0:LANES] = s.astype(o_ref.dtype)
        if tail > LANES:
            o_ref[:, LANES:] = jnp.zeros((tl, tail - LANES), o_ref.dtype)

    return _pcall(
        body, name=name, grid=(L // tl,), out_shape=jax.ShapeDtypeStruct(dzx.shape, dzx.dtype),
        in_specs=[pl.BlockSpec((G, tl, LANES), lambda i: (0, i, 0)), pl.BlockSpec(memory_space=pl.ANY)],
        out_specs=pl.BlockSpec((tl, tail), lambda i: (i, col0 // tail)),
        input_output_aliases={1: 0},
        compiler_params=_cparams(("parallel",)))(ddt, dzx)


def _sc_conv_fwd(proj, w, name, cb=512):
    L = proj.shape[0]
    width = proj.shape[1] // 3
    nb = width // cb

    def body(b_ref, c_ref, x_ref, w_ref, o_ref):
        for l0 in range(0, cb, LANES):
            lanes = slice(l0, l0 + LANES)
            taps = [w_ref[k:k + 1, lanes] for k in range(SC_K)]
            for r0 in range(0, L, CONV_ROWS):
                rows = slice(r0, r0 + CONV_ROWS)
                q = [c * x for c, x in zip(_shifted_rows(c_ref, r0, SC_K, lanes),
                                           _shifted_rows(x_ref, r0, SC_K, lanes))]
                acc = taps[SC_K - 1] * q[0]
                for s in range(1, SC_K):
                    acc = acc + taps[SC_K - 1 - s] * q[s]
                o_ref[rows, lanes] = (b_ref[rows, lanes] * acc).astype(BF16)

    return _pcall(
        body, name=name, grid=(nb,), out_shape=jax.ShapeDtypeStruct((L, width), BF16),
        in_specs=[pl.BlockSpec((L, cb), lambda j: (0, j)),
                  pl.BlockSpec((L, cb), lambda j: (0, nb + j)),
                  pl.BlockSpec((L, cb), lambda j: (0, 2 * nb + j)),
                  pl.BlockSpec((SC_K, cb), lambda j: (0, j))],
        out_specs=pl.BlockSpec((L, cb), lambda j: (0, j)),
        compiler_params=_cparams(("parallel",)))(proj, proj, proj, w)


def _sc_conv_bwd(proj, w, dy, name, cb=128):
    L = proj.shape[0]
    width = proj.shape[1] // 3
    nb = width // cb

    def body(b_ref, c_ref, x_ref, w_ref, dy_ref, db_ref, dc_ref, dxv_ref, dw_ref, dconv_ref):
        taps = [w_ref[k:k + 1, :] for k in range(SC_K)]
        fold = lambda v: sum(v[r:r + 8, :] for r in range(0, CONV_ROWS, 8))
        dw8 = [jnp.zeros((8, cb), F32) for _ in range(SC_K)]
        for r0 in range(0, L, CONV_ROWS):
            rows = slice(r0, r0 + CONV_ROWS)
            q = [c * x for c, x in zip(_shifted_rows(c_ref, r0, SC_K), _shifted_rows(x_ref, r0, SC_K))]
            conv = taps[SC_K - 1] * q[0]
            for s in range(1, SC_K):
                conv = conv + taps[SC_K - 1 - s] * q[s]
            dyv = dy_ref[rows, :]
            db_ref[rows, :] = (dyv * conv).astype(BF16)
            dconv = dyv * b_ref[rows, :]
            dconv_ref[rows, :] = dconv
            for s in range(SC_K):
                dw8[s] = dw8[s] + fold(dconv * q[s])
        dconv_ref[L:L + 8, :] = jnp.zeros((8, cb), F32)
        for s in range(SC_K):
            dw_ref[SC_K - 1 - s:SC_K - s, :] = jnp.sum(dw8[s], axis=0, keepdims=True)
        for r0 in range(0, L, CONV_ROWS):
            rows = slice(r0, r0 + CONV_ROWS)
            dq = taps[SC_K - 1] * dconv_ref[rows, :]
            for s in range(1, SC_K):
                dq = dq + taps[SC_K - 1 - s] * dconv_ref[r0 + s:r0 + s + CONV_ROWS, :]
            dc_ref[rows, :] = (dq * x_ref[rows, :]).astype(BF16)
            dxv_ref[rows, :] = (dq * c_ref[rows, :]).astype(BF16)

    blk = pl.BlockSpec((L, cb), lambda j: (0, j))
    wblk = pl.BlockSpec((SC_K, cb), lambda j: (0, j))
    return _pcall(
        body, name=name, grid=(nb,),
        out_shape=[jax.ShapeDtypeStruct((L, width), BF16)] * 3 + [jax.ShapeDtypeStruct((SC_K, width), F32)],
        in_specs=[blk, pl.BlockSpec((L, cb), lambda j: (0, nb + j)),
                  pl.BlockSpec((L, cb), lambda j: (0, 2 * nb + j)), wblk, blk],
        out_specs=[blk, blk, blk, wblk], scratch_shapes=[pltpu.VMEM((L + 8, cb), F32)],
        compiler_params=_cparams(("parallel",)))(proj, proj, proj, w, dy)


def _split3(v):
    hi = v.astype(BF16)
    r1 = v - hi.astype(F32)
    mid = r1.astype(BF16)
    lo = (r1 - mid.astype(F32)).astype(BF16)
    return hi, mid, lo


def _dot_exact01(t01, v):
    hi, mid, lo = _split3(v)
    return _dot(t01, hi) + _dot(t01, mid) + _dot(t01, lo)


def _lane_col(v, lane, h):
    return jnp.sum(jnp.where(lane == h, v, 0.0), axis=1, keepdims=True)


def _sum_all(v):
    return jnp.sum(jnp.sum(v, axis=1, keepdims=True), axis=0, keepdims=True)


def _softplus(x):
    return jnp.maximum(x, 0.0) + jnp.log1p(jnp.exp(-jnp.abs(x)))


def _ssd_decay(zx, bias_p, alog_p, n_heads, dt_block, name):
    L = zx.shape[0]
    nc = L // SSD_CHUNK
    per_step = 4 if nc % 4 == 0 else 1
    rows_step = per_step * SSD_CHUNK

    def body(raw_ref, bias_ref, alog_ref, dt_ref, sg_ref, cs_ref, cst_ref, last_ref):
        lane = lax.broadcasted_iota(jnp.int32, (SSD_CHUNK, LANES), 1)
        row = lax.broadcasted_iota(jnp.int32, (SSD_CHUNK, LANES), 0)
        valid = lane < n_heads
        tri = (row >= lane).astype(BF16)
        a_row = -jnp.exp(alog_ref[...])
        for i in range(per_step):
            rows = slice(i * SSD_CHUNK, (i + 1) * SSD_CHUNK)
            raw = raw_ref[rows, :] + bias_ref[...]
            dt = jnp.where(valid, _softplus(raw), 0.0)
            a = dt * a_row
            cs = _dot_exact01(tri, a)
            dt_ref[rows, :] = dt
            sg_ref[rows, :] = _sigmoid(raw)
            cs_ref[rows, :] = cs
            cst_ref[i] = cs.T
            last_ref[i] = jnp.sum(a, axis=0, keepdims=True)

    blk = pl.BlockSpec((rows_step, LANES), lambda c: (c, 0))
    head_vec = pl.BlockSpec((1, LANES), lambda c: (0, 0))
    return _pcall(
        body, name=name, grid=(nc // per_step,),
        out_shape=[jax.ShapeDtypeStruct((L, LANES), F32)] * 3
        + [jax.ShapeDtypeStruct((nc, SSD_CHUNK, LANES), F32), jax.ShapeDtypeStruct((nc, 1, LANES), F32)],
        in_specs=[pl.BlockSpec((rows_step, LANES), lambda c: (c, dt_block)), head_vec, head_vec],
        out_specs=[blk, blk, blk, pl.BlockSpec((per_step, SSD_CHUNK, LANES), lambda c: (c, 0, 0)),
                   pl.BlockSpec((per_step, 1, LANES), lambda c: (c, 0, 0))],
        compiler_params=_cparams(("parallel",)))(zx, bias_p, alog_p)


def _ssd_common(dt_ref, cs_ref, last_ref, b_ref, c_ref):
    c_sz = SSD_CHUNK
    lane = lax.broadcasted_iota(jnp.int32, (c_sz, LANES), 1)
    row = lax.broadcasted_iota(jnp.int32, (c_sz, LANES), 0)
    bb = b_ref[...].astype(BF16)
    cb = c_ref[...].astype(BF16)
    scores = _dot(cb, bb, "nt")
    return dict(lane=lane, row=row, dt=dt_ref[...], cs=cs_ref[...], last_row=last_ref[...], bb=bb, cb=cb,
                scores=scores, causal=row >= lane, lo=lane < SSD_P)


def _pair_terms(q, cst_ref, h0):
    lane, lo = q["lane"], q["lo"]
    out = {}
    cols, dts, lasts, lms = [], [], [], []
    lane1 = lax.broadcasted_iota(jnp.int32, (1, LANES), 1)
    for h in (h0, h0 + 1):
        col = _lane_col(q["cs"], lane, h)
        rowv = cst_ref[pl.ds(h, 1), :]
        lms.append(jnp.exp(jnp.where(q["causal"], col - rowv, -1e30)))
        cols.append(col)
        dts.append(_lane_col(q["dt"], lane, h))
        lasts.append(jnp.sum(jnp.where(lane1 == h, q["last_row"], 0.0), axis=1, keepdims=True))
    out["lm"] = lms
    out["cols"] = cols
    out["lasts"] = lasts
    out["dt_b"] = jnp.where(lo, dts[0], dts[1])
    out["e_b"] = jnp.where(lo, jnp.exp(cols[0]), jnp.exp(cols[1]))
    out["dec_cols"] = [jnp.exp(lasts[0] - cols[0]), jnp.exp(lasts[1] - cols[1])]
    out["dec_b"] = jnp.where(lo, out["dec_cols"][0], out["dec_cols"][1])
    lo1 = lane1 < SSD_P
    out["explast"] = [jnp.exp(lasts[0]), jnp.exp(lasts[1])]
    out["explast_b"] = jnp.where(lo1, out["explast"][0], out["explast"][1])
    return out


def _ssd_fwd(zx, xc, decay, d_lane, nw, d_inner, after, name):
    L = zx.shape[0]
    nc = L // SSD_CHUNK
    gw = d_inner // SSD_G
    heads = gw // SSD_P
    n_pair = heads // 2
    bc0 = d_inner // LANES

    def body(z_ref, xs_ref, b_ref, c_ref, dt_ref, cs_ref, cst_ref, last_ref, dl_ref, nw_ref, after_ref,
             y_ref, yn_ref, prev_ref, s_ref):
        @pl.when(pl.program_id(1) == 0)
        def _():
            s_ref[...] = jnp.zeros_like(s_ref)

        q = _ssd_common(dt_ref, cs_ref, last_ref, b_ref, c_ref)
        prev_ref[...] = s_ref[...]
        lo = q["lo"]
        for j in range(n_pair):
            sl = slice(j * LANES, (j + 1) * LANES)
            p = _pair_terms(q, cst_ref, pl.program_id(0) * heads + 2 * j)
            xs_p = xs_ref[:, sl]
            xp = xs_p * p["dt_b"]
            xb = xp.astype(BF16)
            m_a = (q["scores"] * p["lm"][0]).astype(BF16)
            m_b = (q["scores"] * p["lm"][1]).astype(BF16)
            yd = jnp.where(lo, _dot(m_a, xb), _dot(m_b, xb))
            s_p = s_ref[:, sl]
            yo = _dot(q["cb"], s_p.astype(BF16)) * p["e_b"]
            y_ref[:, sl] = yd + yo + dl_ref[:, sl] * xs_p
            st = _dot(q["bb"], (xp * p["dec_b"]).astype(BF16), "tn")
            s_ref[:, sl] = s_p * p["explast_b"] + st
        yv = y_ref[...]
        zv = z_ref[...]
        yg = yv * (zv * _sigmoid(zv))
        rstd = lax.rsqrt(jnp.mean(yg * yg, axis=-1, keepdims=True) + NORM_EPS)
        yn_ref[...] = (yg * rstd * nw_ref[...]).astype(BF16)

    grp = lambda width: pl.BlockSpec((None, 1, width), lambda g, c: (g, 0, 0))
    dt_, _, cs_, cst_, last_ = decay
    return _pcall(
        body, name=name, grid=(SSD_G, nc),
        out_shape=[jax.ShapeDtypeStruct((L, d_inner), F32), jax.ShapeDtypeStruct((L, d_inner), BF16),
                   jax.ShapeDtypeStruct((nc, SSD_G, SSD_N, gw), F32)],
        in_specs=[pl.BlockSpec((SSD_CHUNK, gw), lambda g, c: (c, g)),
                  pl.BlockSpec((SSD_CHUNK, gw), lambda g, c: (c, g)),
                  pl.BlockSpec((SSD_CHUNK, SSD_N), lambda g, c: (c, bc0 + g)),
                  pl.BlockSpec((SSD_CHUNK, SSD_N), lambda g, c: (c, bc0 + SSD_G + g)),
                  pl.BlockSpec((SSD_CHUNK, LANES), lambda g, c: (c, 0)),
                  pl.BlockSpec((SSD_CHUNK, LANES), lambda g, c: (c, 0)),
                  pl.BlockSpec((None, SSD_CHUNK, LANES), lambda g, c: (c, 0, 0)),
                  pl.BlockSpec((None, 1, LANES), lambda g, c: (c, 0, 0)),
                  grp(gw), grp(gw), pl.BlockSpec(memory_space=pl.ANY)],
        out_specs=[pl.BlockSpec((SSD_CHUNK, gw), lambda g, c: (c, g)),
                   pl.BlockSpec((SSD_CHUNK, gw), lambda g, c: (c, g)),
                   pl.BlockSpec((None, None, SSD_N, gw), lambda g, c: (c, g, 0, 0))],
        scratch_shapes=[pltpu.VMEM((SSD_N, gw), F32)],
        compiler_params=_cparams(("parallel", "arbitrary")))(
            zx, xc, xc, xc, dt_, cs_, cst_, last_, d_lane, nw, after)


def _ssd_bwd(dyn, y, zx, xc, prev, decay, alog_p, d_lane, nw, d_inner, name):
    L = zx.shape[0]
    nc = L // SSD_CHUNK
    gw = d_inner // SSD_G
    heads = gw // SSD_P
    n_pair = heads // 2
    bc0 = d_inner // LANES

    def body(dyn_ref, y_ref, z_ref, xs_ref, b_ref, c_ref, prev_ref, dt_ref, sg_ref, cs_ref, cst_ref, last_ref,
             alog_ref, dl_ref, nw_ref,
             dz_ref, dxs_ref, db_ref, dc_ref, ddt_ref, dbias_ref, dalog_ref, dd_ref, dnw_ref,
             ds_ref, racc_ref):
        @pl.when(pl.program_id(1) == 0)
        def _():
            ds_ref[...] = jnp.zeros_like(ds_ref)
            dbias_ref[...] = jnp.zeros_like(dbias_ref)
            dalog_ref[...] = jnp.zeros_like(dalog_ref)
            dd_ref[...] = jnp.zeros_like(dd_ref)
            dnw_ref[...] = jnp.zeros_like(dnw_ref)

        q = _ssd_common(dt_ref, cs_ref, last_ref, b_ref, c_ref)
        a_row = -jnp.exp(alog_ref[...])
        lane, row, lo = q["lane"], q["row"], q["lo"]
        lane1 = lax.broadcasted_iota(jnp.int32, (1, LANES), 1)
        head0 = pl.program_id(0) * heads
        mine = (lane >= head0) & (lane < head0 + heads)

        yv, zv, dynv, nwv = y_ref[...], z_ref[...], dyn_ref[...], nw_ref[...]
        sig = _sigmoid(zv)
        sz = zv * sig
        yg = yv * sz
        rstd = lax.rsqrt(jnp.mean(yg * yg, axis=-1, keepdims=True) + NORM_EPS)
        yhat = yg * rstd
        dnw_ref[...] += jnp.sum(dynv * yhat, axis=0, keepdims=True)
        dyh = dynv * nwv
        dyg = rstd * (dyh - yhat * jnp.mean(dyh * yhat, axis=-1, keepdims=True))
        dz_ref[...] = (dyg * yv * (sig * (1.0 + zv * (1.0 - sig)))).astype(BF16)
        dy_all = dyg * sz

        dg = jnp.zeros((SSD_CHUNK, SSD_CHUNK), F32)
        dc_acc = jnp.zeros((SSD_CHUNK, SSD_N), F32)
        db_acc = jnp.zeros((SSD_CHUNK, SSD_N), F32)
        dcs_mat = jnp.zeros((SSD_CHUNK, LANES), F32)
        ddt_mat = jnp.zeros((SSD_CHUNK, LANES), F32)
        dd_row = jnp.zeros((1, LANES), F32)
        racc_ref[...] = jnp.zeros_like(racc_ref)
        is_last = row == SSD_CHUNK - 1

        for j in range(n_pair):
            sl = slice(j * LANES, (j + 1) * LANES)
            ha, hb = head0 + 2 * j, head0 + 2 * j + 1
            p = _pair_terms(q, cst_ref, ha)
            xs_p = xs_ref[:, sl]
            dyp = dy_all[:, sl]
            xp = xs_p * p["dt_b"]
            xb = xp.astype(BF16)
            s_p = prev_ref[:, sl]
            s_pb = s_p.astype(BF16)
            dsn = ds_ref[:, sl]
            dsnb = dsn.astype(BF16)
            m_f = [q["scores"] * p["lm"][0], q["scores"] * p["lm"][1]]

            t0 = dyp * xs_p
            dd_row = dd_row + jnp.where(lane1 == ha, _sum_all(jnp.where(lo, t0, 0.0)), 0.0) \
                + jnp.where(lane1 == hb, _sum_all(jnp.where(lo, 0.0, t0)), 0.0)
            dxs_p = dl_ref[:, sl] * dyp

            yo = _dot(q["cb"], s_pb) * p["e_b"]
            dcs_b = (dyp * p["e_b"]).astype(BF16)
            dc_acc = dc_acc + _dot(dcs_b, s_pb, "nt")
            ds_yo = _dot(q["cb"], dcs_b, "tn")
            t1 = dyp * yo
            dcs_cols = [jnp.sum(jnp.where(lo, t1, 0.0), axis=1, keepdims=True),
                        jnp.sum(jnp.where(lo, 0.0, t1), axis=1, keepdims=True)]

            t2 = dsn * s_p
            dlast = [p["explast"][0] * _sum_all(jnp.where(lo, t2, 0.0)),
                     p["explast"][1] * _sum_all(jnp.where(lo, 0.0, t2))]
            ds_ref[:, sl] = dsn * p["explast_b"] + ds_yo
            w = _dot(q["bb"], dsnb)
            db_acc = db_acc + _dot((xp * p["dec_b"]).astype(BF16), dsnb, "nt")
            dxp = w * p["dec_b"]
            t3 = w * xp
            e = [jnp.sum(jnp.where(lo, t3, 0.0), axis=1, keepdims=True) * p["dec_cols"][0],
                 jnp.sum(jnp.where(lo, 0.0, t3), axis=1, keepdims=True) * p["dec_cols"][1]]
            for i in range(2):
                dlast[i] = dlast[i] + jnp.sum(e[i], axis=0, keepdims=True)
                dcs_cols[i] = dcs_cols[i] - e[i]

            dyb = dyp.astype(BF16)
            dy_h = [jnp.where(lo, dyp, 0.0).astype(BF16), jnp.where(lo, 0.0, dyp).astype(BF16)]
            dms = [_dot(dy_h[0], xb, "nt"), _dot(dy_h[1], xb, "nt")]
            dxp = dxp + jnp.where(lo, _dot(m_f[0].astype(BF16), dyb, "tn"), _dot(m_f[1].astype(BF16), dyb, "tn"))
            for i, h in enumerate((ha, hb)):
                dg = dg + dms[i] * p["lm"][i]
                qm = dms[i] * m_f[i]
                dcs_cols[i] = dcs_cols[i] + jnp.sum(qm, axis=1, keepdims=True)
                racc_ref[pl.ds(h, 1), :] = jnp.sum(qm, axis=0, keepdims=True)

            dxs_ref[:, sl] = dxs_p + dxp * p["dt_b"]
            t4 = dxp * xs_p
            ddt_cols = [jnp.sum(jnp.where(lo, t4, 0.0), axis=1, keepdims=True),
                        jnp.sum(jnp.where(lo, 0.0, t4), axis=1, keepdims=True)]
            for i, h in enumerate((ha, hb)):
                sel = lane == h
                dcs_mat = dcs_mat + jnp.where(sel, dcs_cols[i], 0.0) + jnp.where(sel & is_last, dlast[i], 0.0)
                ddt_mat = ddt_mat + jnp.where(sel, ddt_cols[i], 0.0)

        dcs_mat = dcs_mat - racc_ref[...].T
        tri_t = (row <= lane).astype(BF16)
        da = _dot_exact01(tri_t, dcs_mat)
        ddt = ddt_mat + da * a_row
        dalog_ref[...] += jnp.sum(jnp.where(mine, da * q["dt"], 0.0), axis=0, keepdims=True) * a_row
        draw = jnp.where(mine, ddt * sg_ref[...], 0.0)
        ddt_ref[...] = draw
        dbias_ref[...] += jnp.sum(draw, axis=0, keepdims=True)
        dd_ref[...] += dd_row
        dgb = dg.astype(BF16)
        dc_ref[...] = dc_acc + _dot(dgb, q["bb"])
        db_ref[...] = db_acc + _dot(dgb, q["cb"], "tn")

    rev = lambda c: nc - 1 - c
    grp = lambda width: pl.BlockSpec((None, 1, width), lambda g, c: (g, 0, 0))
    blk = lambda width, off: pl.BlockSpec((SSD_CHUNK, width), lambda g, c: (rev(c), off + g))
    head_vec = pl.BlockSpec((1, LANES), lambda g, c: (0, 0))
    chunk_rows = pl.BlockSpec((SSD_CHUNK, LANES), lambda g, c: (rev(c), 0))
    dt_, sg_, cs_, cst_, last_ = decay
    return _pcall(
        body, name=name, grid=(SSD_G, nc),
        out_shape=[jax.ShapeDtypeStruct(zx.shape, BF16), jax.ShapeDtypeStruct((L, d_inner), F32),
                   jax.ShapeDtypeStruct((L, SSD_G * SSD_N), F32), jax.ShapeDtypeStruct((L, SSD_G * SSD_N), F32),
                   jax.ShapeDtypeStruct((SSD_G, L, LANES), F32),
                   jax.ShapeDtypeStruct((SSD_G, 1, LANES), F32), jax.ShapeDtypeStruct((SSD_G, 1, LANES), F32),
                   jax.ShapeDtypeStruct((SSD_G, 1, LANES), F32), jax.ShapeDtypeStruct((SSD_G, 1, gw), F32)],
        in_specs=[blk(gw, 0), blk(gw, 0), blk(gw, 0), blk(gw, 0), blk(SSD_N, bc0), blk(SSD_N, bc0 + SSD_G),
                  pl.BlockSpec((None, None, SSD_N, gw), lambda g, c: (rev(c), g, 0, 0)),
                  chunk_rows, chunk_rows, chunk_rows,
                  pl.BlockSpec((None, SSD_CHUNK, LANES), lambda g, c: (rev(c), 0, 0)),
                  pl.BlockSpec((None, 1, LANES), lambda g, c: (rev(c), 0, 0)),
                  head_vec, grp(gw), grp(gw)],
        out_specs=[blk(gw, 0), blk(gw, 0), blk(SSD_N, 0), blk(SSD_N, 0),
                   pl.BlockSpec((None, SSD_CHUNK, LANES), lambda g, c: (g, rev(c), 0)),
                   grp(LANES), grp(LANES), grp(LANES), grp(gw)],
        scratch_shapes=[pltpu.VMEM((SSD_N, gw), F32), pltpu.VMEM((SSD_CHUNK, LANES), F32)],
        compiler_params=_cparams(("parallel", "arbitrary")))(
            dyn, y, zx, xc, xc, xc, prev, dt_, sg_, cs_, cst_, last_, alog_p, d_lane, nw)


def _cond_mod(c_pad, ada_w, ada_b_loc, after, name):
    depth, D, n = ada_w.shape
    rows = c_pad.shape[0]

    def body(c_ref, w_ref, b_ref, after_ref, mod_ref, cond_ref):
        cv = c_ref[...]
        cond = cv * _sigmoid(cv)
        cond_ref[...] = cond
        mod_ref[...] = _dot(cond.astype(BF16), w_ref[...].astype(BF16)) + b_ref[...]

    return _pcall(
        body, name=name, grid=(depth,),
        out_shape=[jax.ShapeDtypeStruct((depth, rows, n), F32), jax.ShapeDtypeStruct((rows, D), F32)],
        in_specs=[pl.BlockSpec((rows, D), lambda i: (0, 0)),
                  pl.BlockSpec((None, D, n), lambda i: (i, 0, 0)),
                  pl.BlockSpec((None, 1, n), lambda i: (i, 0, 0)),
                  pl.BlockSpec(memory_space=pl.ANY)],
        out_specs=[pl.BlockSpec((None, rows, n), lambda i: (i, 0, 0)),
                   pl.BlockSpec((rows, D), lambda i: (0, 0))],
        compiler_params=_cparams(("arbitrary",)))(c_pad, ada_w, ada_b_loc, after)


def _adamw_math(g, w, m, v):
    m_new = ADAM_B1 * m + (1.0 - ADAM_B1) * g
    v_new = ADAM_B2 * v + (1.0 - ADAM_B2) * (g * g)
    m_hat = m_new / (1.0 - ADAM_B1 ** ADAM_STEP)
    v_hat = v_new / (1.0 - ADAM_B2 ** ADAM_STEP)
    delta = -ADAM_LR * (m_hat / (jnp.sqrt(v_hat) + ADAM_EPS) + ADAM_WD * w)
    return delta, m_new, v_new


def _adamw_sum(parts, w, m, v, layer, name, prev=None, tr=None):
    depth, R, C = w.shape
    tr = _tile(R, tr if tr is not None else (512 if C <= 512 else 256))

    def body(p_ref, w_ref, m_ref, v_ref, *rest):
        g_ref, d_ref, mo_ref, vo_ref = rest[-4:]
        g = p_ref[0].astype(F32)
        for k in range(1, N_DEV):
            g = g + p_ref[k].astype(F32)
        d, mn, vn = _adamw_math(g, w_ref[...], m_ref[...], v_ref[...])
        g_ref[...] = g
        d_ref[...] = d
        mo_ref[...] = mn
        vo_ref[...] = vn

    blk = pl.BlockSpec((None, tr, C), lambda i: (layer, i, 0))
    prev = list(prev) if prev is not None else []
    return _pcall(
        body, name=name, grid=(R // tr,),
        out_shape=[jax.ShapeDtypeStruct((depth, R, C), F32)] * 4,
        in_specs=[pl.BlockSpec((N_DEV, tr, C), lambda i: (0, i, 0)), blk, blk, blk]
        + [pl.BlockSpec(memory_space=pl.ANY)] * len(prev),
        out_specs=[blk] * 4, input_output_aliases={4 + k: k for k in range(len(prev))},
        compiler_params=_cparams(("parallel",)))(parts, w, m, v, *prev)


def _adamw_small(parts, wmv, head_parts, head_wmv, loss_parts, name):
    n, nh = len(parts), len(head_parts)
    n_heads = head_wmv[0][0].shape[1] if nh else 0
    groups = head_parts[0].shape[1] if nh else 0
    d_model = loss_parts.shape[2]

    def body(*refs):
        p_refs, refs = refs[:n], refs[n:]
        wmv_refs, refs = refs[:3 * n], refs[3 * n:]
        hp_refs, refs = refs[:nh], refs[nh:]
        hwmv_refs, refs = refs[:3 * nh], refs[3 * nh:]
        loss_ref, refs = refs[0], refs[1:]
        outs, loss_out, head_scr = refs[:4 * (n + nh)], refs[4 * (n + nh)], refs[4 * (n + nh) + 1]

        def update(i, g, w_ref, m_ref, v_ref):
            res = (g,) + _adamw_math(g, w_ref[...], m_ref[...], v_ref[...])
            for o_ref, r in zip(outs[4 * i:4 * i + 4], res):
                o_ref[...] = r

        for i in range(n):
            g = p_refs[i][0]
            for k in range(1, N_DEV):
                g = g + p_refs[i][k]
            update(i, g, *wmv_refs[3 * i:3 * i + 3])
        for i in range(nh):
            g = None
            for k in range(N_DEV):
                for grp in range(groups):
                    g = hp_refs[i][k, grp] if g is None else g + hp_refs[i][k, grp]
            head_scr[...] = g
            update(n + i, head_scr[:, 0:n_heads], *hwmv_refs[3 * i:3 * i + 3])
        tot = loss_ref[0]
        for k in range(1, N_DEV):
            tot = tot + loss_ref[k]
        loss_out[...] = jnp.broadcast_to(_sum_all(tot) * (0.5 / d_model), loss_out.shape)

    operands = list(parts) + [a for t in wmv for a in t] + list(head_parts) + [a for t in head_wmv for a in t]
    operands.append(loss_parts)
    out_shape = [jax.ShapeDtypeStruct(t[0].shape, F32) for t in list(wmv) + list(head_wmv) for _ in range(4)]
    out_shape.append(jax.ShapeDtypeStruct((1, LANES), F32))
    vmem = pl.BlockSpec(memory_space=pltpu.VMEM)
    outs = _pcall(body, name=name, out_shape=out_shape, in_specs=[vmem] * len(operands),
                  out_specs=[vmem] * len(out_shape), scratch_shapes=[pltpu.VMEM((1, LANES), F32)],
                  compiler_params=_cparams())(*operands)
    return [outs[4 * i:4 * i + 4] for i in range(n + nh)], outs[-1]


def _ada_adamw(cond_pad, dmod_pad, w, m, v, name, tr=512):
    depth, D, n = w.shape
    rows = cond_pad.shape[0]
    tr = _tile(D, tr)

    def body(c_ref, dm_ref, w_ref, m_ref, v_ref, g_ref, d_ref, mo_ref, vo_ref):
        g = _dot(c_ref[...].astype(BF16), dm_ref[...].astype(BF16), "tn")
        d, mn, vn = _adamw_math(g, w_ref[...], m_ref[...], v_ref[...])
        g_ref[...] = g
        d_ref[...] = d
        mo_ref[...] = mn
        vo_ref[...] = vn

    blk = pl.BlockSpec((None, tr, n), lambda i, r: (i, r, 0))
    return _pcall(
        body, name=name, grid=(depth, D // tr),
        out_shape=[jax.ShapeDtypeStruct((depth, D, n), F32)] * 4,
        in_specs=[pl.BlockSpec((rows, tr), lambda i, r: (0, r)),
                  pl.BlockSpec((None, rows, n), lambda i, r: (i, 0, 0)), blk, blk, blk],
        out_specs=[blk] * 4, compiler_params=_cparams(("parallel", "parallel")))(cond_pad, dmod_pad, w, m, v)


def kernel(x, c, ada_w, ada_b, mix_norm_w, mlp_norm_w, mlp_up, mlp_down, ssd_in_w, ssd_conv_w, ssd_conv_b, ssd_dt_bias, ssd_A_log, ssd_D, ssd_norm_w, ssd_out_w, sc_in_w, sc_conv_w, sc_out_w, final_norm_w, loss_target, m_ada_w, m_ada_b, m_mix_norm_w, m_mlp_norm_w, m_mlp_up, m_mlp_down, m_ssd_in_w, m_ssd_conv_w, m_ssd_conv_b, m_ssd_dt_bias, m_ssd_A_log, m_ssd_D, m_ssd_norm_w, m_ssd_out_w, m_sc_in_w, m_sc_conv_w, m_sc_out_w, m_final_norm_w, v_ada_w, v_ada_b, v_mix_norm_w, v_mlp_norm_w, v_mlp_up, v_mlp_down, v_ssd_in_w, v_ssd_conv_w, v_ssd_conv_b, v_ssd_dt_bias, v_ssd_A_log, v_ssd_D, v_ssd_norm_w, v_ssd_out_w, v_sc_in_w, v_sc_conv_w, v_sc_out_w, v_final_norm_w):
    weights = dict(ada_w=ada_w, ada_b=ada_b, mix_norm_w=mix_norm_w, mlp_norm_w=mlp_norm_w, mlp_up=mlp_up,
                   mlp_down=mlp_down, ssd_in_w=ssd_in_w, ssd_conv_w=ssd_conv_w, ssd_conv_b=ssd_conv_b,
                   ssd_dt_bias=ssd_dt_bias, ssd_A_log=ssd_A_log, ssd_D=ssd_D, ssd_norm_w=ssd_norm_w,
                   ssd_out_w=ssd_out_w, sc_in_w=sc_in_w, sc_conv_w=sc_conv_w, sc_out_w=sc_out_w,
                   final_norm_w=final_norm_w)
    moms = dict(ada_w=m_ada_w, ada_b=m_ada_b, mix_norm_w=m_mix_norm_w, mlp_norm_w=m_mlp_norm_w, mlp_up=m_mlp_up,
                mlp_down=m_mlp_down, ssd_in_w=m_ssd_in_w, ssd_conv_w=m_ssd_conv_w, ssd_conv_b=m_ssd_conv_b,
                ssd_dt_bias=m_ssd_dt_bias, ssd_A_log=m_ssd_A_log, ssd_D=m_ssd_D, ssd_norm_w=m_ssd_norm_w,
                ssd_out_w=m_ssd_out_w, sc_in_w=m_sc_in_w, sc_conv_w=m_sc_conv_w, sc_out_w=m_sc_out_w,
                final_norm_w=m_final_norm_w)
    vars_ = dict(ada_w=v_ada_w, ada_b=v_ada_b, mix_norm_w=v_mix_norm_w, mlp_norm_w=v_mlp_norm_w, mlp_up=v_mlp_up,
                 mlp_down=v_mlp_down, ssd_in_w=v_ssd_in_w, ssd_conv_w=v_ssd_conv_w, ssd_conv_b=v_ssd_conv_b,
                 ssd_dt_bias=v_ssd_dt_bias, ssd_A_log=v_ssd_A_log, ssd_D=v_ssd_D, ssd_norm_w=v_ssd_norm_w,
                 ssd_out_w=v_ssd_out_w, sc_in_w=v_sc_in_w, sc_conv_w=v_sc_conv_w, sc_out_w=v_sc_out_w,
                 final_norm_w=v_final_norm_w)
    names = list(weights)

    L, D = x.shape[1], x.shape[2]
    d_inner = 2 * D
    n_heads = d_inner // SSD_P
    hpg = n_heads // SSD_G
    gw = d_inner // SSD_G
    conv_dim = d_inner + 2 * SSD_G * SSD_N
    zx_dim = d_inner + conv_dim
    zx_pad = -(-(zx_dim + LANES) // 512) * 512
    in_ws = ssd_in_w.shape[2]
    in_base, in_off, in_win = _window_geometry(in_ws)
    me = _my_index()
    x0 = x[0]
    tgt = loss_target[0]

    n_mod = ada_w.shape[2]
    (c_all,) = _exchange([c], "gather_c", gather=True)
    gather_handle = {}
    (gather_handle["ssd_in_w"],), token_in = _xfer_start(
        [ssd_in_w[0].astype(BF16)], "gather_start_ssd_in_w", gather=True, via_sibling=(0,), after=(c_all,))
    c_pad = jnp.pad(c_all.reshape(N_DEV, D), ((0, 16 - N_DEV), (0, 0)))
    ada_b_loc = lax.dynamic_slice_in_dim(ada_b, me * n_mod, n_mod, axis=1).reshape(2, 1, n_mod)
    mod_blk, cond_pad = _cond_mod(c_pad, ada_w, ada_b_loc, token_in, "cond_mod")
    gather_order = ["mod", "ssd_conv_w", "sc_conv_w", "ssd_out_w", "up0", "down0", "sc_in_w", "sc_out_w", "up1",
                    "down1"]
    gather_src = dict(mod=mod_blk, ssd_conv_w=ssd_conv_w[0], sc_conv_w=sc_conv_w[0],
                      ssd_out_w=ssd_out_w[0].astype(BF16),
                      up0=mlp_up[0].astype(BF16), down0=mlp_down[0].astype(BF16),
                      sc_in_w=sc_in_w[0].astype(BF16), sc_out_w=sc_out_w[0].astype(BF16),
                      up1=mlp_up[1].astype(BF16), down1=mlp_down[1].astype(BF16))
    handles, gather_token = _xfer_start([gather_src[k] for k in gather_order], "gather_start", gather=True,
                                        via_sibling=tuple(range(3, len(gather_order))))
    gather_handle.update(zip(gather_order, handles))

    def gathered(keys, after, forward):
        tag = "_".join(keys)
        lands = _xfer_wait([gather_handle[k] for k in keys], after, f"gather_wait_{tag}", gather=True)
        return _sibling_forward(lands, f"gather_forward_{tag}") if forward else lands

    def forward_behind(keys, after):
        tag = "_".join(keys)
        lands = _xfer_wait([gather_handle[k] for k in keys], after, f"gather_wait_{tag}", gather=True)
        fwd_handles, token = _sibling_forward_start(lands, f"gather_forward_start_{tag}")
        return (lambda done: _sibling_forward_wait(fwd_handles, done, f"gather_forward_wait_{tag}")), token

    (ssd_in_g,) = gathered(["ssd_in_w"], (gather_token, m_ssd_in_w, v_ssd_in_w), True)
    w_in_all = _shards_to_columns(ssd_in_g, in_base, in_off, in_win, zx_pad, "ssd_in_w_columns")
    (mod_all,) = gathered(["mod"], w_in_all, False)
    mod_mine = lax.dynamic_index_in_dim(mod_all, me, axis=2, keepdims=False)
    mod_mine = jnp.transpose(mod_mine, (1, 0, 2)).reshape(2, 6, 1, D)
    sh_m, sc_m, g_m, sh_f, sc_f, g_f = [[mod_mine[i, k] for i in range(2)] for k in range(6)]

    vec = lambda a: a.reshape(1, -1)
    small = {}

    _, h0 = _norm_mod_fwd(x0, None, None, vec(mix_norm_w[0]), sc_m[0], sh_m[0], "l0_mix_norm")
    cw_all, scw_all = gathered(["ssd_conv_w", "sc_conv_w"], h0, False)
    (zx,) = _mm_nn(h0, w_in_all, F32, "ssd_in_proj", tm=2048, tn=512)
    conv_b0 = vec(ssd_conv_b[0])
    conv_w_full = jnp.transpose(cw_all, (1, 0, 2)).reshape(SSD_K, conv_dim)
    sc_conv_full = jnp.transpose(scw_all, (1, 0, 2)).reshape(SC_K, D)
    xc = _ssd_conv_fwd(zx, conv_w_full, conv_b0, d_inner, conv_dim, "ssd_conv")
    bias_p = jnp.pad(ssd_dt_bias[0], (0, LANES - n_heads)).reshape(1, LANES)
    alog_p = jnp.pad(ssd_A_log[0], (0, LANES - n_heads)).reshape(1, LANES)
    d_lane = jnp.repeat(ssd_D[0], SSD_P).reshape(SSD_G, 1, gw)
    nw_g = ssd_norm_w[0].reshape(SSD_G, 1, gw)
    finish, token = forward_behind(["ssd_out_w"], xc)
    decay = _ssd_decay(zx, bias_p, alog_p, n_heads, zx_dim // LANES, "ssd_decay")
    y_ssd, yn, prev = _ssd_fwd(zx, xc, decay, d_lane, nw_g, d_inner, token, "ssd_scan")
    ups, downs = [None, None], [None, None]
    (ssd_out_g,) = finish(yn)
    w_ssd_out = ssd_out_g.reshape(-1, D)
    finish, token = forward_behind(["up0", "down0"], ssd_out_g)
    (mix0,) = _mm_nn(yn, w_ssd_out, F32, "ssd_out_proj", tm=1024, tk=2048, after=(token,))
    x1, h1 = _norm_mod_fwd(x0, mix0, g_m[0], vec(mlp_norm_w[0]), sc_f[0], sh_f[0], "l0_mlp_norm")
    ups[0], down0_g = finish(h1)
    downs[0] = down0_g.reshape(-1, D)
    u0, s0 = _mm_nn_blocked(h1, ups[0], "l0_mlp_up", _ep_relu2, [BF16, BF16])
    finish, token = forward_behind(["sc_in_w", "sc_out_w", "up1", "down1"], s0)
    (d0,) = _mm_nn(s0, downs[0], F32, "l0_mlp_down", tm=1024, tk=2048, after=(token,))
    x2, h2 = _norm_mod_fwd(x1, d0, g_f[0], vec(mix_norm_w[1]), sc_m[1], sh_m[1], "l1_mix_norm")
    sc_in_g, sc_out_g, ups[1], down1_g = finish(h2)
    w_sc_out, downs[1] = sc_out_g.reshape(-1, D), down1_g.reshape(-1, D)
    (proj,) = _mm_nn_blocked(h2, sc_in_g, "sc_in_proj", _ep_store(F32), [F32])
    yc = _sc_conv_fwd(proj, sc_conv_full, "sc_conv")
    (mix1,) = _mm_nn(yc, w_sc_out, F32, "sc_out_proj")
    x3, h3 = _norm_mod_fwd(x2, mix1, g_m[1], vec(mlp_norm_w[1]), sc_f[1], sh_f[1], "l1_mlp_norm")
    u1, s1 = _mm_nn_blocked(h3, ups[1], "l1_mlp_up", _ep_relu2, [BF16, BF16])
    (d1,) = _mm_nn(s1, downs[1], F32, "l1_mlp_down", tm=1024, tk=2048)

    dx, loss_lane, dfw, dd1, dg = _final_loss(x3, d1, g_f[1], vec(final_norm_w), tgt, "final_loss")
    small["final_norm_w"] = dfw

    dmod = [[None] * 6 for _ in range(2)]
    dmod[1][5] = dg

    def mlp_backward(i, dx_out, dd, x_mid, h_in, u, s, mix, gate):
        du = _mm_nt(dd, downs[i], BF16, f"l{i}_mlp_down_bwd", epilogue=_ep_relu2_bwd, extra=(u,))
        gdown = _mm_tn(s, dd, BF16, f"l{i}_mlp_down_wgrad").reshape(N_DEV, -1, D)
        gup = _mm_tn_blocked(h_in, du, BF16, f"l{i}_mlp_up_wgrad")
        (h_down, h_up), token = _xfer_start([gdown, gup], f"l{i}_mlp_grads_start", gather=False)
        grad_handle[f"mlp_down{i}"], grad_handle[f"mlp_up{i}"] = h_down, h_up
        dh = _mm_nt_blocked(du, ups[i], F32, f"l{i}_mlp_up_bwd", after=(token,))
        dxm, dsh, dsc, dnw, dmix, dgate = _norm_mod_bwd(dh, x_mid, vec(mlp_norm_w[i]), sc_f[i], dx_out,
                                                        f"l{i}_mlp_norm_bwd", branch=(mix, gate))
        dmod[i][3], dmod[i][4], dmod[i][2] = dsh, dsc, dgate
        return dxm, dmix, dnw

    grad_handle = {}
    dx3, dyc, dnw_mlp1 = mlp_backward(1, dx, dd1, x3, h3, u1, s1, mix1, g_m[1])
    g_sc_out = _mm_tn(yc, dyc, BF16, "sc_out_wgrad").reshape(N_DEV, -1, D)
    dconv_out = _mm_nt(dyc, w_sc_out, F32, "sc_out_bwd")
    dbg, dcg, dxv, dscw = _sc_conv_bwd(proj, sc_conv_full, dconv_out, "sc_conv_bwd")
    dproj = jnp.concatenate([dbg, dcg, dxv], axis=1)
    g_sc_in = _mm_tn_blocked(h2, dproj, BF16, "sc_in_wgrad")
    (grad_handle["sc_out_w0"], grad_handle["sc_in_w0"]), token = _xfer_start(
        [g_sc_out, g_sc_in], "sc_grads_start", gather=False)
    dh2 = _mm_nt_blocked(dproj, sc_in_g, F32, "sc_in_bwd", after=(token,))
    dx2, dsh, dsc, dnw_mix1, dd0, dg = _norm_mod_bwd(dh2, x2, vec(mix_norm_w[1]), sc_m[1], dx3, "l1_mix_norm_bwd",
                                                     branch=(d0, g_f[0]))
    dmod[1][0], dmod[1][1], dmod[0][5] = dsh, dsc, dg
    dx1, dyo, dnw_mlp0 = mlp_backward(0, dx2, dd0, x1, h1, u0, s0, mix0, g_m[0])
    g_ssd_out = _mm_tn(yn, dyo, BF16, "ssd_out_wgrad").reshape(N_DEV, -1, D)
    (grad_handle["ssd_out_w0"],), token = _xfer_start([g_ssd_out], "ssd_out_grad_start", gather=False)
    dyn = _mm_nt(dyo, w_ssd_out, F32, "ssd_out_bwd", after=(token,))
    dz, dxs, db_, dc_, ddt, dbias, dalog, dd_, dnw_ssd = _ssd_bwd(
        dyn, y_ssd, zx, xc, prev, decay, alog_p, d_lane, nw_g, d_inner, "ssd_scan_bwd")
    dzx, dcw, dcb = _ssd_conv_bwd(zx, conv_w_full, conv_b0, [dxs, db_, dc_], dz, d_inner, "ssd_conv_bwd")
    dzx = _dzx_finish(dzx, ddt, zx_dim, "ssd_dzx_finish")
    g_in_all = _mm_tn(h0, dzx, BF16, "ssd_in_wgrad", tn=512, tk=2048)
    g_ssd_in = _columns_to_shards(g_in_all, in_ws, in_base, in_off, in_win, "ssd_in_wgrad_shards")
    (grad_handle["ssd_in_w0"],), token = _xfer_start([g_ssd_in], "ssd_in_grad_start", gather=False)
    dh0 = _mm_nt(dzx, w_in_all, F32, "ssd_in_bwd", tm=1024, tk=dzx.shape[1] // 2, after=(token,))
    grad_x, dsh, dsc, dnw_mix0 = _norm_mod_bwd(dh0, x0, vec(mix_norm_w[0]), sc_m[0], dx1, "l0_mix_norm_bwd")
    dmod[0][0], dmod[0][1] = dsh, dsc

    small["ada_b"] = jnp.concatenate([jnp.concatenate(dmod[i], axis=1) for i in range(2)], axis=0)
    small["mix_norm_w"] = jnp.concatenate([dnw_mix0, dnw_mix1], axis=0)
    small["mlp_norm_w"] = jnp.concatenate([dnw_mlp0, dnw_mlp1], axis=0)
    small["ssd_conv_w"] = dcw
    small["ssd_conv_b"] = dcb
    small["ssd_norm_w"] = dnw_ssd.reshape(1, d_inner)
    small["sc_conv_w"] = dscw
    small["loss"] = loss_lane
    small_names = list(small)
    head_names = ["ssd_dt_bias", "ssd_A_log", "ssd_D"]
    handles, small_token = _xfer_start([small[k] for k in small_names] + [dbias, dalog, dd_],
                                       "small_grads_start", gather=True)

    out_g, out_d, out_m, out_v = {}, {}, {}, {}

    layer_res = {}

    def big_update(name, i, after):
        (parts,) = _xfer_wait([grad_handle[f"{name}{i}"]], after, f"grads_wait_{name}_{i}", gather=False)
        res = _adamw_sum(parts, weights[name], moms[name], vars_[name], i, f"adamw_{name}_{i}",
                         prev=layer_res.get(name))
        layer_res[name] = res
        return res[1]

    chain = small_token
    for name, i in [("mlp_down", 1), ("mlp_up", 1), ("sc_out_w", 0), ("sc_in_w", 0), ("mlp_down", 0),
                    ("mlp_up", 0), ("ssd_out_w", 0), ("ssd_in_w", 0)]:
        chain = big_update(name, i, chain)
    gathered_small = _xfer_wait(handles, chain, "small_grads_wait", gather=True)
    small_all = dict(zip(small_names + head_names, gathered_small))

    dmod_loc = lax.dynamic_slice_in_dim(small_all["ada_b"], me * n_mod, n_mod, axis=2)
    dmod_pad = jnp.pad(jnp.transpose(dmod_loc, (1, 0, 2)), ((0, 0), (0, 16 - N_DEV), (0, 0)))
    out_g["ada_w"], out_d["ada_w"], out_m["ada_w"], out_v["ada_w"] = _ada_adamw(
        cond_pad, dmod_pad, ada_w, m_ada_w, v_ada_w, "adamw_ada_w")

    for k in ("ssd_conv_w", "sc_conv_w"):
        n_loc = weights[k].shape[2]
        small_all[k] = lax.dynamic_slice_in_dim(small_all[k], me * n_loc, n_loc, axis=2)
    plain = [k for k in small_names if k != "loss"]
    as2d = lambda a: a.reshape(-1, a.shape[-1])
    res, loss_row = _adamw_small(
        [small_all[k] for k in plain], [tuple(as2d(d[k]) for d in (weights, moms, vars_)) for k in plain],
        [small_all[k] for k in head_names], [tuple(as2d(d[k]) for d in (weights, moms, vars_)) for k in head_names],
        small_all["loss"], "adamw_small")
    loss = loss_row[0, 0]
    for k, res4 in zip(plain + head_names, res):
        for r, dst in zip(res4, (out_g, out_d, out_m, out_v)):
            dst[k] = r.reshape(weights[k].shape)
    for name, res4 in layer_res.items():
        for r, dst in zip(res4, (out_g, out_d, out_m, out_v)):
            dst[name] = r

    return (loss, grad_x[None], *[out_g[k] for k in names], *[out_d[k] for k in names],
            *[out_m[k] for k in names], *[out_v[k] for k in names])
```

```python
import jax
import jax.numpy as jnp
from jax import lax
from jax.experimental import pallas as pl
from jax.experimental.pallas import tpu as pltpu

F32 = jnp.float32
BF16 = jnp.bfloat16
N_DEV = 8
MESH = pl.DeviceIdType.MESH

NORM_EPS = 1e-5
SSD_G = 4
SSD_P = 64
SSD_N = 128
SSD_CHUNK = 128
SSD_K = 4
SC_K = 3
LANES = 128

ADAM_LR = 0.001
ADAM_B1 = 0.9
ADAM_B2 = 0.999
ADAM_EPS = 1e-08
ADAM_WD = 0.01
ADAM_STEP = 10

VMEM_LIMIT = 56 * 1024 * 1024


def _pcall(body, **kw):
    return pl.pallas_call(body, **kw)


def _cparams(sem=None):
    if sem is None:
        return pltpu.CompilerParams(vmem_limit_bytes=VMEM_LIMIT)
    return pltpu.CompilerParams(dimension_semantics=sem, vmem_limit_bytes=VMEM_LIMIT)


def _my_index():
    return 4 * lax.axis_index("x") + 2 * lax.axis_index("y") + lax.axis_index("c")


_PEER_MASKS = [(0, 0, 1), (0, 1, 0), (0, 1, 1), (1, 0, 0), (1, 0, 1), (1, 1, 0), (1, 1, 1)]


def _peers():
    x, y, c = lax.axis_index("x"), lax.axis_index("y"), lax.axis_index("c")
    out = []
    for mx, my, mc in _PEER_MASKS:
        px = (1 - x) if mx else x
        py = (1 - y) if my else y
        pc = (1 - c) if mc else c
        out.append(((px, py, pc), 4 * px + 2 * py + pc))
    return out


def _exchange(arrs, name, gather):
    n = len(arrs)
    n_peer = N_DEV - 1

    def body(*refs):
        ins, outs = refs[:n], refs[n:2 * n]
        send_sems, recv_sems, local_sems = refs[2 * n:]
        me = _my_index()
        peers = _peers()
        started = []
        for a in range(n):
            src_own = ins[a] if gather else ins[a].at[me]
            own = pltpu.make_async_copy(src_own, outs[a].at[me], local_sems.at[a])
            own.start()
            started.append(own)
        sends = []
        for a in range(n):
            for k, (peer, pidx) in enumerate(peers):
                src = ins[a] if gather else ins[a].at[pidx]
                cp = pltpu.make_async_remote_copy(
                    src_ref=src, dst_ref=outs[a].at[me],
                    send_sem=send_sems.at[a * n_peer + k], recv_sem=recv_sems.at[a * n_peer + k],
                    device_id=peer, device_id_type=MESH)
                cp.start()
                sends.append(cp)
        for a in range(n):
            for k, (peer, pidx) in enumerate(peers):
                src = ins[a] if gather else ins[a].at[pidx]
                pltpu.make_async_remote_copy(
                    src_ref=src, dst_ref=outs[a].at[pidx],
                    send_sem=send_sems.at[a * n_peer + k], recv_sem=recv_sems.at[a * n_peer + k],
                    device_id=peer, device_id_type=MESH).wait_recv()
        for cp in sends:
            cp.wait_send()
        for own in started:
            own.wait()

    if gather:
        out_shape = [jax.ShapeDtypeStruct((N_DEV,) + a.shape, a.dtype) for a in arrs]
    else:
        out_shape = [jax.ShapeDtypeStruct(a.shape, a.dtype) for a in arrs]
    any_spec = pl.BlockSpec(memory_space=pl.ANY)
    outs = _pcall(
        body, name=name, out_shape=out_shape,
        in_specs=[any_spec] * n, out_specs=[any_spec] * n,
        scratch_shapes=[pltpu.SemaphoreType.DMA((n * n_peer,)), pltpu.SemaphoreType.DMA((n * n_peer,)),
                        pltpu.SemaphoreType.DMA((n,))],
        compiler_params=pltpu.CompilerParams(has_side_effects=True),
    )(*arrs)
    return list(outs)


def _sibling_forward_start(lands, name):
    n = len(lands)
    n_fwd = len(_OTHER_CHIPS)

    def body(*refs):
        ins, bufs = refs[:n], refs[3 * n:4 * n]
        token = refs[-1]
        sibling = (lax.axis_index("x"), lax.axis_index("y"), 1 - lax.axis_index("c"))
        peers = _peers()
        for a in range(n):
            send_sems, recv_sems = refs[n + 2 * a], refs[n + 2 * a + 1]
            for j, k in enumerate(_OTHER_CHIPS):
                slot = peers[k][1]
                pltpu.make_async_remote_copy(
                    src_ref=ins[a].at[slot], dst_ref=bufs[a].at[slot], send_sem=send_sems.at[j],
                    recv_sem=recv_sems.at[j], device_id=sibling, device_id_type=MESH).start()
        token[...] = jnp.zeros_like(token)

    out_shape, out_specs = [], []
    for _ in range(n):
        out_shape += [pltpu.SemaphoreType.DMA((n_fwd,)), pltpu.SemaphoreType.DMA((n_fwd,))]
        out_specs += [_SEM, _SEM]
    out_shape += [pltpu.HBM(a.shape, a.dtype) for a in lands] + [jax.ShapeDtypeStruct((8, LANES), F32)]
    out_specs += [_HBM] * n + [pl.BlockSpec(memory_space=pltpu.VMEM)]
    outs = _pcall(
        body, name=name, out_shape=tuple(out_shape), in_specs=[_HBM] * n, out_specs=tuple(out_specs),
        input_output_aliases={a: 2 * n + a for a in range(n)},
        compiler_params=pltpu.CompilerParams(has_side_effects=_DATAFLOW),
    )(*[pltpu.with_memory_space_constraint(a, pltpu.HBM) for a in lands])
    return [(outs[2 * n + a], outs[2 * a], outs[2 * a + 1]) for a in range(n)], outs[-1]


def _sibling_forward_wait(handles, after, name):
    n = len(handles)

    def body(*refs):
        sibling = (lax.axis_index("x"), lax.axis_index("y"), 1 - lax.axis_index("c"))
        peers = _peers()
        for a in range(n):
            buf, send_sems, recv_sems = refs[3 * a:3 * a + 3]
            for j, k in enumerate(_OTHER_CHIPS):
                (px, py, pc), slot = peers[k]
                theirs = 4 * px + 2 * py + (1 - pc)
                cp = pltpu.make_async_remote_copy(
                    src_ref=buf.at[slot], dst_ref=buf.at[theirs], send_sem=send_sems.at[j],
                    recv_sem=recv_sems.at[j], device_id=sibling, device_id_type=MESH)
                cp.wait_send()
                cp.wait_recv()

    operands, in_specs = [], []
    for h in handles:
        operands += list(h)
        in_specs += [_HBM, _SEM, _SEM]
    outs = _pcall(
        body, name=name, out_shape=tuple(pltpu.HBM(h[0].shape, h[0].dtype) for h in handles),
        in_specs=in_specs + [pl.BlockSpec(memory_space=pl.ANY)], out_specs=tuple([_HBM] * n),
        input_output_aliases={3 * a: a for a in range(n)},
        compiler_params=pltpu.CompilerParams(has_side_effects=_DATAFLOW),
    )(*operands, after)
    return list(outs)


_HBM = pl.BlockSpec(memory_space=pltpu.HBM)
_SEM = pl.BlockSpec(memory_space=pltpu.SEMAPHORE)
_DATAFLOW = pltpu.SideEffectType.DATAFLOW_SIDE_EFFECTING


_ALL_PEERS = tuple(range(N_DEV - 1))
_SAME_CORE_PEERS = (0, 1, 3, 5)
_OTHER_CHIPS = (1, 3, 5)


def _xfer_start(arrs, name, gather, via_sibling=(), after=()):
    n = len(arrs)
    n_peer = N_DEV - 1
    n_after = len(after)
    peer_ks = [_SAME_CORE_PEERS if a in via_sibling else _ALL_PEERS for a in range(n)]

    def body(*refs):
        ins, lands = refs[:n], refs[n:2 * n]
        sems = refs[2 * n + n_after:5 * n + n_after]
        token = refs[-1]
        me = _my_index()
        peers = _peers()
        for a in range(n):
            send_sems, recv_sems, loc_sem = sems[3 * a:3 * a + 3]
            src_own = ins[a] if gather else ins[a].at[me]
            pltpu.make_async_copy(src_own, lands[a].at[me], loc_sem).start()
            for k in peer_ks[a]:
                peer, pidx = peers[k]
                src = ins[a] if gather else ins[a].at[pidx]
                pltpu.make_async_remote_copy(
                    src_ref=src, dst_ref=lands[a].at[me], send_sem=send_sems.at[k], recv_sem=recv_sems.at[k],
                    device_id=peer, device_id_type=MESH).start()
        token[...] = jnp.zeros_like(token)

    land_shapes = [((N_DEV,) + a.shape) if gather else a.shape for a in arrs]
    out_shape, out_specs = [], []
    for _ in range(n):
        out_shape += [pltpu.SemaphoreType.DMA((n_peer,)), pltpu.SemaphoreType.DMA((n_peer,)),
                      pltpu.SemaphoreType.DMA(())]
        out_specs += [_SEM, _SEM, _SEM]
    out_shape += [pltpu.HBM(a.shape, a.dtype) for a in arrs]
    out_shape += [pltpu.HBM(s, a.dtype) for s, a in zip(land_shapes, arrs)]
    out_shape += [jax.ShapeDtypeStruct((8, LANES), F32)]
    out_specs += [_HBM] * (2 * n) + [pl.BlockSpec(memory_space=pltpu.VMEM)]
    aliases = {}
    for a in range(n):
        aliases[a] = 3 * n + a
        aliases[n + a] = 4 * n + a
    operands = [pltpu.with_memory_space_constraint(a, pltpu.HBM) for a in arrs]
    operands += [pltpu.with_memory_space_constraint(lax.empty(s, a.dtype), pltpu.HBM)
                 for s, a in zip(land_shapes, arrs)]
    outs = _pcall(
        body, name=name, out_shape=tuple(out_shape),
        in_specs=[_HBM] * (2 * n) + [pl.BlockSpec(memory_space=pl.ANY)] * n_after, out_specs=tuple(out_specs),
        input_output_aliases=aliases,
        compiler_params=pltpu.CompilerParams(has_side_effects=_DATAFLOW),
    )(*operands, *after)
    handles = []
    for a in range(n):
        handles.append((outs[3 * n + a], outs[4 * n + a], outs[3 * a], outs[3 * a + 1], outs[3 * a + 2],
                        peer_ks[a]))
    return handles, outs[-1]


def _xfer_wait(handles, after, name, gather):
    n = len(handles)
    after = tuple(after) if isinstance(after, (tuple, list)) else (after,)
    peer_ks = [h[5] for h in handles]

    def body(*refs):
        me = _my_index()
        peers = _peers()
        for a in range(n):
            src_ref, land_ref, send_ref, recv_ref, loc_ref = refs[5 * a:5 * a + 5]
            src_own = src_ref if gather else src_ref.at[me]
            pltpu.make_async_copy(src_own, land_ref.at[me], loc_ref).wait()
            for k in peer_ks[a]:
                peer, pidx = peers[k]
                src = src_ref if gather else src_ref.at[pidx]
                cp = pltpu.make_async_remote_copy(
                    src_ref=src, dst_ref=land_ref.at[pidx], send_sem=send_ref.at[k], recv_sem=recv_ref.at[k],
                    device_id=peer, device_id_type=MESH)
                cp.wait_send()
                cp.wait_recv()

    operands, in_specs, out_shape, aliases = [], [], [], {}
    for a, h in enumerate(handles):
        operands += list(h[:5])
        in_specs += [_HBM, _HBM, _SEM, _SEM, _SEM]
        out_shape += [pltpu.HBM(h[0].shape, h[0].dtype), pltpu.HBM(h[1].shape, h[1].dtype)]
        aliases[5 * a] = 2 * a
        aliases[5 * a + 1] = 2 * a + 1
    outs = _pcall(
        body, name=name, out_shape=tuple(out_shape),
        in_specs=in_specs + [pl.BlockSpec(memory_space=pl.ANY)] * len(after),
        out_specs=tuple([_HBM] * (2 * n)), input_output_aliases=aliases,
        compiler_params=pltpu.CompilerParams(has_side_effects=_DATAFLOW),
    )(*operands, *after)
    return [outs[2 * a + 1] for a in range(n)]


def _sibling_forward(lands, name):
    n = len(lands)
    n_fwd = len(_OTHER_CHIPS)

    def body(*refs):
        ins, bufs = refs[:n], refs[n:2 * n]
        send_sems, recv_sems = refs[2 * n:]
        x, y, c = lax.axis_index("x"), lax.axis_index("y"), lax.axis_index("c")
        sibling = (x, y, 1 - c)
        peers = _peers()
        sends = []
        for a in range(n):
            for j, k in enumerate(_OTHER_CHIPS):
                slot = peers[k][1]
                cp = pltpu.make_async_remote_copy(
                    src_ref=ins[a].at[slot], dst_ref=bufs[a].at[slot],
                    send_sem=send_sems.at[a * n_fwd + j], recv_sem=recv_sems.at[a * n_fwd + j],
                    device_id=sibling, device_id_type=MESH)
                cp.start()
                sends.append(cp)
        for a in range(n):
            for j, k in enumerate(_OTHER_CHIPS):
                (px, py, pc), slot = peers[k]
                theirs = 4 * px + 2 * py + (1 - pc)
                pltpu.make_async_remote_copy(
                    src_ref=ins[a].at[slot], dst_ref=bufs[a].at[theirs],
                    send_sem=send_sems.at[a * n_fwd + j], recv_sem=recv_sems.at[a * n_fwd + j],
                    device_id=sibling, device_id_type=MESH).wait_recv()
        for cp in sends:
            cp.wait_send()

    any_spec = pl.BlockSpec(memory_space=pl.ANY)
    outs = _pcall(
        body, name=name, out_shape=[jax.ShapeDtypeStruct(a.shape, a.dtype) for a in lands],
        in_specs=[any_spec] * n, out_specs=[any_spec] * n,
        input_output_aliases={a: a for a in range(n)},
        scratch_shapes=[pltpu.SemaphoreType.DMA((n * n_fwd,)), pltpu.SemaphoreType.DMA((n * n_fwd,))],
        compiler_params=pltpu.CompilerParams(has_side_effects=True),
    )(*lands)
    return list(outs)


_DIMS = {"nn": (((1,), (0,)), ((), ())), "nt": (((1,), (1,)), ((), ())), "tn": (((0,), (0,)), ((), ()))}


def _dot(a, b, mode="nn"):
    if mode == "nt_blocks":
        n = b.shape[2]
        return sum(_dot(a[:, i * n:(i + 1) * n], b[i], "nt") for i in range(b.shape[0]))
    return lax.dot_general(a, b, _DIMS[mode], preferred_element_type=F32)


def _mm(a, b, *, mode, grid, a_spec, b_spec, out_shape, out_specs, acc_shape, epilogue, name,
        extra=(), extra_specs=(), after=(), semantics=("parallel", "parallel", "arbitrary")):
    nk = grid[2]
    n_extra = len(extra)
    n_in = 2 + n_extra + len(after)

    def body_single(*refs):
        a_ref, b_ref = refs[0], refs[1]
        epilogue(_dot(a_ref[...], b_ref[...], mode), refs[2:2 + n_extra], refs[n_in:])

    def body_acc(*refs):
        a_ref, b_ref = refs[0], refs[1]
        ex = refs[2:2 + n_extra]
        outs = refs[n_in:-1]
        acc = refs[-1]
        k = pl.program_id(2)

        @pl.when(k == 0)
        def _():
            acc[...] = jnp.zeros_like(acc)

        acc[...] += _dot(a_ref[...], b_ref[...], mode)

        @pl.when(k == nk - 1)
        def _():
            epilogue(acc[...], ex, outs)

    return _pcall(
        body_single if nk == 1 else body_acc, name=name, grid=grid, out_shape=out_shape,
        in_specs=[a_spec, b_spec] + list(extra_specs) + [pl.BlockSpec(memory_space=pl.ANY)] * len(after),
        out_specs=out_specs,
        scratch_shapes=[] if nk == 1 else [pltpu.VMEM(acc_shape, F32)],
        compiler_params=_cparams(semantics),
    )(a, b, *extra, *after)


def _ep_store(dtype):
    def ep(acc, ex, outs):
        outs[0][...] = acc.astype(dtype)
    return ep


def _ep_relu2(acc, ex, outs):
    outs[0][...] = acc.astype(BF16)
    r = jnp.maximum(acc, 0.0)
    outs[1][...] = (r * r).astype(BF16)


def _ep_relu2_bwd(acc, ex, outs):
    u = ex[0][...].astype(F32)
    outs[0][...] = (acc * (2.0 * jnp.maximum(u, 0.0))).astype(BF16)


def _tile(n, want):
    t = min(n, want)
    while n % t:
        t //= 2
    return t


def _mm_nn(a, w, out_dtype, name, tm=2048, tn=1024, tk=1024, epilogue=None, out_dtypes=None, after=()):
    M, K = a.shape
    N = w.shape[1]
    tm, tn, tk = _tile(M, tm), _tile(N, tn), _tile(K, tk)
    out_dtypes = out_dtypes or [out_dtype]
    return _mm(a, w, mode="nn", grid=(M // tm, N // tn, K // tk),
               a_spec=pl.BlockSpec((tm, tk), lambda i, j, k: (i, k)),
               b_spec=pl.BlockSpec((tk, tn), lambda i, j, k: (k, j)),
               out_shape=[jax.ShapeDtypeStruct((M, N), d) for d in out_dtypes],
               out_specs=[pl.BlockSpec((tm, tn), lambda i, j, k: (i, j)) for _ in out_dtypes],
               acc_shape=(tm, tn), epilogue=epilogue or _ep_store(out_dtype), name=name, after=after)


def _mm_nn_blocked(a, wg, name, epilogue, out_dtypes, tm=2048):
    M, K = a.shape
    n = wg.shape[2]
    tm = _tile(M, tm)
    return _mm(a, wg, mode="nn", grid=(M // tm, N_DEV, 1),
               a_spec=pl.BlockSpec((tm, K), lambda i, j, k: (i, 0)),
               b_spec=pl.BlockSpec((None, K, n), lambda i, j, k: (j, 0, 0)),
               out_shape=[jax.ShapeDtypeStruct((M, N_DEV * n), d) for d in out_dtypes],
               out_specs=[pl.BlockSpec((tm, n), lambda i, j, k: (i, j)) for _ in out_dtypes],
               acc_shape=(tm, n), epilogue=epilogue, name=name)


def _mm_nt(a, w, out_dtype, name, tm=2048, tn=1024, tk=1024, epilogue=None, extra=(), extra_specs=(),
           after=()):
    M, K = a.shape
    N = w.shape[0]
    tm, tn, tk = _tile(M, tm), _tile(N, tn), _tile(K, tk)
    if extra and not extra_specs:
        extra_specs = [pl.BlockSpec((tm, tn), lambda i, j, k: (i, j)) for _ in extra]
    return _mm(a, w, mode="nt", grid=(M // tm, N // tn, K // tk),
               a_spec=pl.BlockSpec((tm, tk), lambda i, j, k: (i, k)),
               b_spec=pl.BlockSpec((tn, tk), lambda i, j, k: (j, k)),
               out_shape=[jax.ShapeDtypeStruct((M, N), out_dtype)],
               out_specs=[pl.BlockSpec((tm, tn), lambda i, j, k: (i, j))],
               acc_shape=(tm, tn), epilogue=epilogue or _ep_store(out_dtype), name=name,
               extra=extra, extra_specs=extra_specs, after=after)[0]


def _mm_nt_blocked(a, wg, out_dtype, name, tm=1024, after=()):
    M = a.shape[0]
    kout, n = wg.shape[1], wg.shape[2]
    tm = _tile(M, tm)
    per = 4
    return _mm(a, wg, mode="nt_blocks", grid=(M // tm, 1, N_DEV // per),
               a_spec=pl.BlockSpec((tm, per * n), lambda i, j, k: (i, k)),
               b_spec=pl.BlockSpec((per, kout, n), lambda i, j, k: (k, 0, 0)),
               out_shape=[jax.ShapeDtypeStruct((M, kout), out_dtype)],
               out_specs=[pl.BlockSpec((tm, kout), lambda i, j, k: (i, 0))],
               acc_shape=(tm, kout), epilogue=_ep_store(out_dtype), name=name, after=after)[0]


def _mm_tn(a, b, out_dtype, name, tm=1024, tn=1024, tk=2048):
    K, M = a.shape
    N = b.shape[1]
    tm, tn, tk = _tile(M, tm), _tile(N, tn), _tile(K, tk)
    return _mm(a, b, mode="tn", grid=(M // tm, N // tn, K // tk),
               a_spec=pl.BlockSpec((tk, tm), lambda i, j, k: (k, i)),
               b_spec=pl.BlockSpec((tk, tn), lambda i, j, k: (k, j)),
               out_shape=[jax.ShapeDtypeStruct((M, N), out_dtype)],
               out_specs=[pl.BlockSpec((tm, tn), lambda i, j, k: (i, j))],
               acc_shape=(tm, tn), epilogue=_ep_store(out_dtype), name=name)[0]


def _mm_tn_blocked(a, b, out_dtype, name, tm=1024, tk=2048):
    K, M = a.shape
    n = b.shape[1] // N_DEV
    tm, tk = _tile(M, tm), _tile(K, tk)
    return _mm(a, b, mode="tn", grid=(M // tm, N_DEV, K // tk),
               a_spec=pl.BlockSpec((tk, tm), lambda i, j, k: (k, i)),
               b_spec=pl.BlockSpec((tk, n), lambda i, j, k: (k, j)),
               out_shape=[jax.ShapeDtypeStruct((N_DEV, M, n), out_dtype)],
               out_specs=[pl.BlockSpec((None, tm, n), lambda i, j, k: (j, i, 0))],
               acc_shape=(tm, n), epilogue=_ep_store(out_dtype), name=name)[0]


def _window_geometry(ws):
    base = [(ws * k // LANES) * LANES for k in range(N_DEV)]
    off = [ws * k - base[k] for k in range(N_DEV)]
    win = -(-(max(off) + ws) // LANES) * LANES
    return base, off, win


def _shards_to_columns(xg, base, off, win, n_out, name, tr=256):
    R, ws = xg.shape[1], xg.shape[2]
    tr = _tile(R, tr)
    nb_win = win // LANES

    def body(x_ref, o_ref, frame_ref):
        written = set()
        frame_ref[...] = jnp.zeros_like(frame_ref)
        for k in range(N_DEV):
            frame_ref[:, 0:ws] = x_ref[k].astype(F32)
            window = frame_ref[...]
            if off[k]:
                window = pltpu.roll(window, off[k], 1)
            for i in range(nb_win):
                b = base[k] // LANES + i
                if b * LANES >= n_out:
                    continue
                cols = slice(b * LANES, (b + 1) * LANES)
                blk = window[:, i * LANES:(i + 1) * LANES]
                if b in written:
                    blk = blk + o_ref[:, cols].astype(F32)
                o_ref[:, cols] = blk.astype(o_ref.dtype)
                written.add(b)
        for b in range(n_out // LANES):
            if b not in written:
                o_ref[:, b * LANES:(b + 1) * LANES] = jnp.zeros((tr, LANES), o_ref.dtype)

    return _pcall(
        body, name=name, grid=(R // tr,), out_shape=jax.ShapeDtypeStruct((R, n_out), xg.dtype),
        in_specs=[pl.BlockSpec((N_DEV, tr, ws), lambda i: (0, i, 0))],
        out_specs=pl.BlockSpec((tr, n_out), lambda i: (i, 0)),
        scratch_shapes=[pltpu.VMEM((tr, win), F32)],
        compiler_params=_cparams(("parallel",)))(xg)


def _columns_to_shards(x, ws, base, off, win, name, tr=256):
    R = x.shape[0]
    tr = _tile(R, tr)

    def body(x_ref, o_ref, frame_ref):
        for k in range(N_DEV):
            window = x_ref[:, base[k]:base[k] + win].astype(F32)
            if off[k]:
                window = pltpu.roll(window, win - off[k], 1)
            frame_ref[...] = window
            o_ref[k] = frame_ref[:, 0:ws].astype(o_ref.dtype)

    return _pcall(
        body, name=name, grid=(R // tr,), out_shape=jax.ShapeDtypeStruct((N_DEV, R, ws), x.dtype),
        in_specs=[pl.BlockSpec((tr, x.shape[1]), lambda i: (i, 0))],
        out_specs=pl.BlockSpec((N_DEV, tr, ws), lambda i: (0, i, 0)),
        scratch_shapes=[pltpu.VMEM((tr, win), F32)],
        compiler_params=_cparams(("parallel",)))(x)


def _sigmoid(x):
    return 1.0 / (1.0 + jnp.exp(-x))


def _row_spec(tm, d):
    return pl.BlockSpec((tm, d), lambda i: (i, 0))


def _vec_spec(d):
    return pl.BlockSpec((1, d), lambda i: (0, 0))


def _norm_mod_fwd(x, y, gate, nw, scale, shift, name, tm=512):
    L, D = x.shape
    tm = _tile(L, tm)
    has_res = y is not None

    def body(*refs):
        if has_res:
            x_ref, y_ref, g_ref, nw_ref, sc_ref, sh_ref, xo_ref, h_ref = refs
            xn = x_ref[...] + g_ref[...] * y_ref[...]
            xo_ref[...] = xn
        else:
            x_ref, nw_ref, sc_ref, sh_ref, h_ref = refs
            xn = x_ref[...]
        rstd = lax.rsqrt(jnp.mean(xn * xn, axis=-1, keepdims=True) + NORM_EPS)
        h = xn * rstd * nw_ref[...] * (1.0 + sc_ref[...]) + sh_ref[...]
        h_ref[...] = h.astype(BF16)

    row, vec = _row_spec(tm, D), _vec_spec(D)
    if has_res:
        ins, in_specs = (x, y, gate, nw, scale, shift), [row, row, vec, vec, vec, vec]
        out_shape = [jax.ShapeDtypeStruct((L, D), F32), jax.ShapeDtypeStruct((L, D), BF16)]
        out_specs = [row, row]
    else:
        ins, in_specs = (x, nw, scale, shift), [row, vec, vec, vec]
        out_shape = [jax.ShapeDtypeStruct((L, D), BF16)]
        out_specs = [row]
    outs = _pcall(body, name=name, grid=(L // tm,), out_shape=out_shape, in_specs=in_specs,
                  out_specs=out_specs, compiler_params=_cparams(("parallel",)))(*ins)
    return outs if has_res else (x, outs[0])


def _gated_branch_bwd(dx, branch, y_ref, g_ref, dy_ref, dg_ref):
    if branch is None:
        return
    dy_ref[...] = (g_ref[...] * dx).astype(BF16)
    dg_ref[...] += jnp.sum(dx * y_ref[...], axis=0, keepdims=True)


def _norm_mod_bwd(dh, x, nw, scale, dres, name, branch=None, tm=512):
    L, D = x.shape
    tm = _tile(L, tm)
    nb = 0 if branch is None else 2

    def body(dh_ref, x_ref, nw_ref, sc_ref, dres_ref, *rest):
        y_ref, g_ref = rest[:nb] if nb else (None, None)
        dx_ref, dsh_ref, dsc_ref, dnw_ref = rest[nb:nb + 4]
        dy_ref, dg_ref = rest[nb + 4:] if nb else (None, None)

        @pl.when(pl.program_id(0) == 0)
        def _():
            dsh_ref[...] = jnp.zeros_like(dsh_ref)
            dsc_ref[...] = jnp.zeros_like(dsc_ref)
            dnw_ref[...] = jnp.zeros_like(dnw_ref)
            if nb:
                dg_ref[...] = jnp.zeros_like(dg_ref)

        xv = x_ref[...]
        dh_v = dh_ref[...]
        nw_v = nw_ref[...]
        rstd = lax.rsqrt(jnp.mean(xv * xv, axis=-1, keepdims=True) + NORM_EPS)
        xhat = xv * rstd
        dsh_ref[...] += jnp.sum(dh_v, axis=0, keepdims=True)
        dsc_ref[...] += jnp.sum(dh_v * (xhat * nw_v), axis=0, keepdims=True)
        dr = dh_v * (1.0 + sc_ref[...])
        dnw_ref[...] += jnp.sum(dr * xhat, axis=0, keepdims=True)
        dxh = dr * nw_v
        dx = rstd * (dxh - xhat * jnp.mean(dxh * xhat, axis=-1, keepdims=True)) + dres_ref[...]
        dx_ref[...] = dx
        _gated_branch_bwd(dx, branch, y_ref, g_ref, dy_ref, dg_ref)

    row, vec = _row_spec(tm, D), _vec_spec(D)
    extra_in = [] if branch is None else list(branch)
    return _pcall(
        body, name=name, grid=(L // tm,),
        out_shape=[jax.ShapeDtypeStruct((L, D), F32)] + [jax.ShapeDtypeStruct((1, D), F32)] * 3
        + ([jax.ShapeDtypeStruct((L, D), BF16), jax.ShapeDtypeStruct((1, D), F32)] if nb else []),
        in_specs=[row, row, vec, vec, row] + ([row, vec] if nb else []),
        out_specs=[row, vec, vec, vec] + ([row, vec] if nb else []),
        compiler_params=_cparams(("arbitrary",)))(dh, x, nw, scale, dres, *extra_in)


def _final_loss(x, y, gate, fw, target, name, tm=512):
    L, D = x.shape
    tm = _tile(L, tm)

    def body(x_ref, y_ref, g_ref, fw_ref, t_ref, dx_ref, loss_ref, dfw_ref, dy_ref, dg_ref):
        @pl.when(pl.program_id(0) == 0)
        def _():
            loss_ref[...] = jnp.zeros_like(loss_ref)
            dfw_ref[...] = jnp.zeros_like(dfw_ref)
            dg_ref[...] = jnp.zeros_like(dg_ref)

        xn = x_ref[...] + g_ref[...] * y_ref[...]
        fw_v = fw_ref[...]
        rstd = lax.rsqrt(jnp.mean(xn * xn, axis=-1, keepdims=True) + NORM_EPS)
        xhat = xn * rstd
        diff = xhat * fw_v - t_ref[...]
        loss_ref[...] += jnp.sum(diff * diff, axis=0, keepdims=True)
        dyf = diff * (1.0 / D)
        dfw_ref[...] += jnp.sum(dyf * xhat, axis=0, keepdims=True)
        dxh = dyf * fw_v
        dx = rstd * (dxh - xhat * jnp.mean(dxh * xhat, axis=-1, keepdims=True))
        dx_ref[...] = dx
        _gated_branch_bwd(dx, True, y_ref, g_ref, dy_ref, dg_ref)

    row, vec = _row_spec(tm, D), _vec_spec(D)
    return _pcall(
        body, name=name, grid=(L // tm,),
        out_shape=[jax.ShapeDtypeStruct((L, D), F32), jax.ShapeDtypeStruct((1, D), F32),
                   jax.ShapeDtypeStruct((1, D), F32), jax.ShapeDtypeStruct((L, D), BF16),
                   jax.ShapeDtypeStruct((1, D), F32)],
        in_specs=[row, row, vec, vec, row], out_specs=[row, vec, vec, row, vec],
        compiler_params=_cparams(("arbitrary",)))(x, y, gate, fw, target)


def _shift_down(v, s, row):
    if s == 0:
        return v
    return jnp.where(row >= s, pltpu.roll(v, s, 0), 0.0)


CONV_ROWS = 32


def _shifted_rows(x_ref, r0, n, lanes=slice(None)):
    cur = x_ref[r0:r0 + CONV_ROWS, lanes]
    if r0 >= n - 1:
        return [cur] + [x_ref[r0 - s:r0 - s + CONV_ROWS, lanes] for s in range(1, n)]
    row = lax.broadcasted_iota(jnp.int32, cur.shape, 0)
    return [_shift_down(cur, s, row) for s in range(n)]


def _ssd_conv_fwd(zx, w, b, col0, width, name, cb=512):
    L = zx.shape[0]
    nb = width // cb
    off = col0 // cb

    def body(x_ref, w_ref, b_ref, o_ref):
        for l0 in range(0, cb, LANES):
            lanes = slice(l0, l0 + LANES)
            taps = [w_ref[k:k + 1, lanes] for k in range(SSD_K)]
            bias = b_ref[:, lanes]
            for r0 in range(0, L, CONV_ROWS):
                taps_in = _shifted_rows(x_ref, r0, SSD_K, lanes)
                acc = bias + taps[SSD_K - 1] * taps_in[0]
                for s in range(1, SSD_K):
                    acc = acc + taps[SSD_K - 1 - s] * taps_in[s]
                o_ref[r0:r0 + CONV_ROWS, lanes] = acc * _sigmoid(acc)

    return _pcall(
        body, name=name, grid=(nb,), out_shape=jax.ShapeDtypeStruct((L, width), F32),
        in_specs=[pl.BlockSpec((L, cb), lambda j: (0, off + j)),
                  pl.BlockSpec((SSD_K, cb), lambda j: (0, j)),
                  pl.BlockSpec((1, cb), lambda j: (0, j))],
        out_specs=pl.BlockSpec((L, cb), lambda j: (0, j)),
        compiler_params=_cparams(("parallel",)))(zx, w, b)


def _ssd_conv_bwd(zx, w, b, d_parts, dzx, col0, name, cb=128):
    L = zx.shape[0]
    widths = [p.shape[1] for p in d_parts]
    width = sum(widths)
    nb = width // cb
    off = col0 // cb
    starts = [sum(widths[:i]) // cb for i in range(len(d_parts))]
    counts = [wd // cb for wd in widths]

    def body(x_ref, w_ref, b_ref, *rest):
        d_refs = rest[:len(d_parts)]
        dx_ref, dw_ref, db_ref, dpre_ref = rest[len(d_parts) + 1:]
        j = pl.program_id(0)
        taps = [w_ref[k:k + 1, :] for k in range(SSD_K)]
        bias = b_ref[...]
        fold = lambda v: sum(v[r:r + 8, :] for r in range(0, CONV_ROWS, 8))
        db8 = jnp.zeros((8, cb), F32)
        dw8 = [jnp.zeros((8, cb), F32) for _ in range(SSD_K)]
        for r0 in range(0, L, CONV_ROWS):
            rows = slice(r0, r0 + CONV_ROWS)
            d_val = d_refs[-1][rows, :]
            for i in range(len(d_parts) - 2, -1, -1):
                d_val = jnp.where(j < starts[i + 1], d_refs[i][rows, :], d_val)
            taps_in = _shifted_rows(x_ref, r0, SSD_K)
            acc = bias + taps[SSD_K - 1] * taps_in[0]
            for s in range(1, SSD_K):
                acc = acc + taps[SSD_K - 1 - s] * taps_in[s]
            sig = _sigmoid(acc)
            dpre = d_val * (sig * (1.0 + acc * (1.0 - sig)))
            dpre_ref[rows, :] = dpre
            db8 = db8 + fold(dpre)
            for s in range(SSD_K):
                dw8[s] = dw8[s] + fold(dpre * taps_in[s])
        dpre_ref[L:L + 8, :] = jnp.zeros((8, cb), F32)
        db_ref[...] = jnp.sum(db8, axis=0, keepdims=True)
        for s in range(SSD_K):
            dw_ref[SSD_K - 1 - s:SSD_K - s, :] = jnp.sum(dw8[s], axis=0, keepdims=True)
        for r0 in range(0, L, CONV_ROWS):
            dx = taps[SSD_K - 1] * dpre_ref[r0:r0 + CONV_ROWS, :]
            for s in range(1, SSD_K):
                dx = dx + taps[SSD_K - 1 - s] * dpre_ref[r0 + s:r0 + s + CONV_ROWS, :]
            dx_ref[r0:r0 + CONV_ROWS, :] = dx.astype(BF16)

    def part_spec(i):
        return pl.BlockSpec((L, cb), lambda j: (0, jnp.clip(j - starts[i], 0, counts[i] - 1)))

    return _pcall(
        body, name=name, grid=(nb,),
        out_shape=[jax.ShapeDtypeStruct(dzx.shape, BF16), jax.ShapeDtypeStruct((SSD_K, width), F32),
                   jax.ShapeDtypeStruct((1, width), F32)],
        in_specs=[pl.BlockSpec((L, cb), lambda j: (0, off + j)),
                  pl.BlockSpec((SSD_K, cb), lambda j: (0, j)),
                  pl.BlockSpec((1, cb), lambda j: (0, j))]
        + [part_spec(i) for i in range(len(d_parts))] + [pl.BlockSpec(memory_space=pl.ANY)],
        out_specs=[pl.BlockSpec((L, cb), lambda j: (0, off + j)),
                   pl.BlockSpec((SSD_K, cb), lambda j: (0, j)),
                   pl.BlockSpec((1, cb), lambda j: (0, j))],
        input_output_aliases={3 + len(d_parts): 0},
        scratch_shapes=[pltpu.VMEM((L + 8, cb), F32)],
        compiler_params=_cparams(("parallel",)))(zx, w, b, *d_parts, dzx)


def _dzx_finish(dzx, ddt, col0, name, tl=512):
    G, L, _ = ddt.shape
    tail = dzx.shape[1] - col0
    tl = _tile(L, tl)

    def body(ddt_ref, dzx_ref, o_ref):
        s = ddt_ref[0]
        for g in range(1, G):
            s = s + ddt_ref[g]
        o_ref[:, 0:LANES] = s.astype(o_ref.dtype)
        if tail > LANES:
            o_ref[:, LANES:] = jnp.zeros((tl, tail - LANES), o_ref.dtype)

    return _pcall(
        body, name=name, grid=(L // tl,), out_shape=jax.ShapeDtypeStruct(dzx.shape, dzx.dtype),
        in_specs=[pl.BlockSpec((G, tl, LANES), lambda i: (0, i, 0)), pl.BlockSpec(memory_space=pl.ANY)],
        out_specs=pl.BlockSpec((tl, tail), lambda i: (i, col0 // tail)),
        input_output_aliases={1: 0},
        compiler_params=_cparams(("parallel",)))(ddt, dzx)


def _sc_conv_fwd(proj, w, name, cb=512):
    L = proj.shape[0]
    width = proj.shape[1] // 3
    nb = width // cb

    def body(b_ref, c_ref, x_ref, w_ref, o_ref):
        for l0 in range(0, cb, LANES):
            lanes = slice(l0, l0 + LANES)
            taps = [w_ref[k:k + 1, lanes] for k in range(SC_K)]
            for r0 in range(0, L, CONV_ROWS):
                rows = slice(r0, r0 + CONV_ROWS)
                q = [c * x for c, x in zip(_shifted_rows(c_ref, r0, SC_K, lanes),
                                           _shifted_rows(x_ref, r0, SC_K, lanes))]
                acc = taps[SC_K - 1] * q[0]
                for s in range(1, SC_K):
                    acc = acc + taps[SC_K - 1 - s] * q[s]
                o_ref[rows, lanes] = (b_ref[rows, lanes] * acc).astype(BF16)

    return _pcall(
        body, name=name, grid=(nb,), out_shape=jax.ShapeDtypeStruct((L, width), BF16),
        in_specs=[pl.BlockSpec((L, cb), lambda j: (0, j)),
                  pl.BlockSpec((L, cb), lambda j: (0, nb + j)),
                  pl.BlockSpec((L, cb), lambda j: (0, 2 * nb + j)),
                  pl.BlockSpec((SC_K, cb), lambda j: (0, j))],
        out_specs=pl.BlockSpec((L, cb), lambda j: (0, j)),
        compiler_params=_cparams(("parallel",)))(proj, proj, proj, w)


def _sc_conv_bwd(proj, w, dy, name, cb=128):
    L = proj.shape[0]
    width = proj.shape[1] // 3
    nb = width // cb

    def body(b_ref, c_ref, x_ref, w_ref, dy_ref, db_ref, dc_ref, dxv_ref, dw_ref, dconv_ref):
        taps = [w_ref[k:k + 1, :] for k in range(SC_K)]
        fold = lambda v: sum(v[r:r + 8, :] for r in range(0, CONV_ROWS, 8))
        dw8 = [jnp.zeros((8, cb), F32) for _ in range(SC_K)]
        for r0 in range(0, L, CONV_ROWS):
            rows = slice(r0, r0 + CONV_ROWS)
            q = [c * x for c, x in zip(_shifted_rows(c_ref, r0, SC_K), _shifted_rows(x_ref, r0, SC_K))]
            conv = taps[SC_K - 1] * q[0]
            for s in range(1, SC_K):
                conv = conv + taps[SC_K - 1 - s] * q[s]
            dyv = dy_ref[rows, :]
            db_ref[rows, :] = (dyv * conv).astype(BF16)
            dconv = dyv * b_ref[rows, :]
            dconv_ref[rows, :] = dconv
            for s in range(SC_K):
                dw8[s] = dw8[s] + fold(dconv * q[s])
        dconv_ref[L:L + 8, :] = jnp.zeros((8, cb), F32)
        for s in range(SC_K):
            dw_ref[SC_K - 1 - s:SC_K - s, :] = jnp.sum(dw8[s], axis=0, keepdims=True)
        for r0 in range(0, L, CONV_ROWS):
            rows = slice(r0, r0 + CONV_ROWS)
            dq = taps[SC_K - 1] * dconv_ref[rows, :]
            for s in range(1, SC_K):
                dq = dq + taps[SC_K - 1 - s] * dconv_ref[r0 + s:r0 + s + CONV_ROWS, :]
            dc_ref[rows, :] = (dq * x_ref[rows, :]).astype(BF16)
            dxv_ref[rows, :] = (dq * c_ref[rows, :]).astype(BF16)

    blk = pl.BlockSpec((L, cb), lambda j: (0, j))
    wblk = pl.BlockSpec((SC_K, cb), lambda j: (0, j))
    return _pcall(
        body, name=name, grid=(nb,),
        out_shape=[jax.ShapeDtypeStruct((L, width), BF16)] * 3 + [jax.ShapeDtypeStruct((SC_K, width), F32)],
        in_specs=[blk, pl.BlockSpec((L, cb), lambda j: (0, nb + j)),
                  pl.BlockSpec((L, cb), lambda j: (0, 2 * nb + j)), wblk, blk],
        out_specs=[blk, blk, blk, wblk], scratch_shapes=[pltpu.VMEM((L + 8, cb), F32)],
        compiler_params=_cparams(("parallel",)))(proj, proj, proj, w, dy)


def _split3(v):
    hi = v.astype(BF16)
    r1 = v - hi.astype(F32)
    mid = r1.astype(BF16)
    lo = (r1 - mid.astype(F32)).astype(BF16)
    return hi, mid, lo


def _dot_exact01(t01, v):
    hi, mid, lo = _split3(v)
    return _dot(t01, hi) + _dot(t01, mid) + _dot(t01, lo)


def _lane_col(v, lane, h):
    return jnp.sum(jnp.where(lane == h, v, 0.0), axis=1, keepdims=True)


def _sum_all(v):
    return jnp.sum(jnp.sum(v, axis=1, keepdims=True), axis=0, keepdims=True)


def _softplus(x):
    return jnp.maximum(x, 0.0) + jnp.log1p(jnp.exp(-jnp.abs(x)))


def _ssd_decay(zx, bias_p, alog_p, n_heads, dt_block, name):
    L = zx.shape[0]
    nc = L // SSD_CHUNK
    per_step = 4 if nc % 4 == 0 else 1
    rows_step = per_step * SSD_CHUNK

    def body(raw_ref, bias_ref, alog_ref, dt_ref, sg_ref, cs_ref, cst_ref, last_ref):
        lane = lax.broadcasted_iota(jnp.int32, (SSD_CHUNK, LANES), 1)
        row = lax.broadcasted_iota(jnp.int32, (SSD_CHUNK, LANES), 0)
        valid = lane < n_heads
        tri = (row >= lane).astype(BF16)
        a_row = -jnp.exp(alog_ref[...])
        for i in range(per_step):
            rows = slice(i * SSD_CHUNK, (i + 1) * SSD_CHUNK)
            raw = raw_ref[rows, :] + bias_ref[...]
            dt = jnp.where(valid, _softplus(raw), 0.0)
            a = dt * a_row
            cs = _dot_exact01(tri, a)
            dt_ref[rows, :] = dt
            sg_ref[rows, :] = _sigmoid(raw)
            cs_ref[rows, :] = cs
            cst_ref[i] = cs.T
            last_ref[i] = jnp.sum(a, axis=0, keepdims=True)

    blk = pl.BlockSpec((rows_step, LANES), lambda c: (c, 0))
    head_vec = pl.BlockSpec((1, LANES), lambda c: (0, 0))
    return _pcall(
        body, name=name, grid=(nc // per_step,),
        out_shape=[jax.ShapeDtypeStruct((L, LANES), F32)] * 3
        + [jax.ShapeDtypeStruct((nc, SSD_CHUNK, LANES), F32), jax.ShapeDtypeStruct((nc, 1, LANES), F32)],
        in_specs=[pl.BlockSpec((rows_step, LANES), lambda c: (c, dt_block)), head_vec, head_vec],
        out_specs=[blk, blk, blk, pl.BlockSpec((per_step, SSD_CHUNK, LANES), lambda c: (c, 0, 0)),
                   pl.BlockSpec((per_step, 1, LANES), lambda c: (c, 0, 0))],
        compiler_params=_cparams(("parallel",)))(zx, bias_p, alog_p)


def _ssd_common(dt_ref, cs_ref, last_ref, b_ref, c_ref):
    c_sz = SSD_CHUNK
    lane = lax.broadcasted_iota(jnp.int32, (c_sz, LANES), 1)
    row = lax.broadcasted_iota(jnp.int32, (c_sz, LANES), 0)
    bb = b_ref[...].astype(BF16)
    cb = c_ref[...].astype(BF16)
    scores = _dot(cb, bb, "nt")
    return dict(lane=lane, row=row, dt=dt_ref[...], cs=cs_ref[...], last_row=last_ref[...], bb=bb, cb=cb,
                scores=scores, causal=row >= lane, lo=lane < SSD_P)


def _pair_terms(q, cst_ref, h0):
    lane, lo = q["lane"], q["lo"]
    out = {}
    cols, dts, lasts, lms = [], [], [], []
    lane1 = lax.broadcasted_iota(jnp.int32, (1, LANES), 1)
    for h in (h0, h0 + 1):
        col = _lane_col(q["cs"], lane, h)
        rowv = cst_ref[pl.ds(h, 1), :]
        lms.append(jnp.exp(jnp.where(q["causal"], col - rowv, -1e30)))
        cols.append(col)
        dts.append(_lane_col(q["dt"], lane, h))
        lasts.append(jnp.sum(jnp.where(lane1 == h, q["last_row"], 0.0), axis=1, keepdims=True))
    out["lm"] = lms
    out["cols"] = cols
    out["lasts"] = lasts
    out["dt_b"] = jnp.where(lo, dts[0], dts[1])
    out["e_b"] = jnp.where(lo, jnp.exp(cols[0]), jnp.exp(cols[1]))
    out["dec_cols"] = [jnp.exp(lasts[0] - cols[0]), jnp.exp(lasts[1] - cols[1])]
    out["dec_b"] = jnp.where(lo, out["dec_cols"][0], out["dec_cols"][1])
    lo1 = lane1 < SSD_P
    out["explast"] = [jnp.exp(lasts[0]), jnp.exp(lasts[1])]
    out["explast_b"] = jnp.where(lo1, out["explast"][0], out["explast"][1])
    return out


def _ssd_fwd(zx, xc, decay, d_lane, nw, d_inner, after, name):
    L = zx.shape[0]
    nc = L // SSD_CHUNK
    gw = d_inner // SSD_G
    heads = gw // SSD_P
    n_pair = heads // 2
    bc0 = d_inner // LANES

    def body(z_ref, xs_ref, b_ref, c_ref, dt_ref, cs_ref, cst_ref, last_ref, dl_ref, nw_ref, after_ref,
             y_ref, yn_ref, prev_ref, s_ref):
        @pl.when(pl.program_id(1) == 0)
        def _():
            s_ref[...] = jnp.zeros_like(s_ref)

        q = _ssd_common(dt_ref, cs_ref, last_ref, b_ref, c_ref)
        prev_ref[...] = s_ref[...]
        lo = q["lo"]
        for j in range(n_pair):
            sl = slice(j * LANES, (j + 1) * LANES)
            p = _pair_terms(q, cst_ref, pl.program_id(0) * heads + 2 * j)
            xs_p = xs_ref[:, sl]
            xp = xs_p * p["dt_b"]
            xb = xp.astype(BF16)
            m_a = (q["scores"] * p["lm"][0]).astype(BF16)
            m_b = (q["scores"] * p["lm"][1]).astype(BF16)
            yd = jnp.where(lo, _dot(m_a, xb), _dot(m_b, xb))
            s_p = s_ref[:, sl]
            yo = _dot(q["cb"], s_p.astype(BF16)) * p["e_b"]
            y_ref[:, sl] = yd + yo + dl_ref[:, sl] * xs_p
            st = _dot(q["bb"], (xp * p["dec_b"]).astype(BF16), "tn")
            s_ref[:, sl] = s_p * p["explast_b"] + st
        yv = y_ref[...]
        zv = z_ref[...]
        yg = yv * (zv * _sigmoid(zv))
        rstd = lax.rsqrt(jnp.mean(yg * yg, axis=-1, keepdims=True) + NORM_EPS)
        yn_ref[...] = (yg * rstd * nw_ref[...]).astype(BF16)

    grp = lambda width: pl.BlockSpec((None, 1, width), lambda g, c: (g, 0, 0))
    dt_, _, cs_, cst_, last_ = decay
    return _pcall(
        body, name=name, grid=(SSD_G, nc),
        out_shape=[jax.ShapeDtypeStruct((L, d_inner), F32), jax.ShapeDtypeStruct((L, d_inner), BF16),
                   jax.ShapeDtypeStruct((nc, SSD_G, SSD_N, gw), F32)],
        in_specs=[pl.BlockSpec((SSD_CHUNK, gw), lambda g, c: (c, g)),
                  pl.BlockSpec((SSD_CHUNK, gw), lambda g, c: (c, g)),
                  pl.BlockSpec((SSD_CHUNK, SSD_N), lambda g, c: (c, bc0 + g)),
                  pl.BlockSpec((SSD_CHUNK, SSD_N), lambda g, c: (c, bc0 + SSD_G + g)),
                  pl.BlockSpec((SSD_CHUNK, LANES), lambda g, c: (c, 0)),
                  pl.BlockSpec((SSD_CHUNK, LANES), lambda g, c: (c, 0)),
                  pl.BlockSpec((None, SSD_CHUNK, LANES), lambda g, c: (c, 0, 0)),
                  pl.BlockSpec((None, 1, LANES), lambda g, c: (c, 0, 0)),
                  grp(gw), grp(gw), pl.BlockSpec(memory_space=pl.ANY)],
        out_specs=[pl.BlockSpec((SSD_CHUNK, gw), lambda g, c: (c, g)),
                   pl.BlockSpec((SSD_CHUNK, gw), lambda g, c: (c, g)),
                   pl.BlockSpec((None, None, SSD_N, gw), lambda g, c: (c, g, 0, 0))],
        scratch_shapes=[pltpu.VMEM((SSD_N, gw), F32)],
        compiler_params=_cparams(("parallel", "arbitrary")))(
            zx, xc, xc, xc, dt_, cs_, cst_, last_, d_lane, nw, after)


def _ssd_bwd(dyn, y, zx, xc, prev, decay, alog_p, d_lane, nw, d_inner, name):
    L = zx.shape[0]
    nc = L // SSD_CHUNK
    gw = d_inner // SSD_G
    heads = gw // SSD_P
    n_pair = heads // 2
    bc0 = d_inner // LANES

    def body(dyn_ref, y_ref, z_ref, xs_ref, b_ref, c_ref, prev_ref, dt_ref, sg_ref, cs_ref, cst_ref, last_ref,
             alog_ref, dl_ref, nw_ref,
             dz_ref, dxs_ref, db_ref, dc_ref, ddt_ref, dbias_ref, dalog_ref, dd_ref, dnw_ref,
             ds_ref, racc_ref):
        @pl.when(pl.program_id(1) == 0)
        def _():
            ds_ref[...] = jnp.zeros_like(ds_ref)
            dbias_ref[...] = jnp.zeros_like(dbias_ref)
            dalog_ref[...] = jnp.zeros_like(dalog_ref)
            dd_ref[...] = jnp.zeros_like(dd_ref)
            dnw_ref[...] = jnp.zeros_like(dnw_ref)

        q = _ssd_common(dt_ref, cs_ref, last_ref, b_ref, c_ref)
        a_row = -jnp.exp(alog_ref[...])
        lane, row, lo = q["lane"], q["row"], q["lo"]
        lane1 = lax.broadcasted_iota(jnp.int32, (1, LANES), 1)
        head0 = pl.program_id(0) * heads
        mine = (lane >= head0) & (lane < head0 + heads)

        yv, zv, dynv, nwv = y_ref[...], z_ref[...], dyn_ref[...], nw_ref[...]
        sig = _sigmoid(zv)
        sz = zv * sig
        yg = yv * sz
        rstd = lax.rsqrt(jnp.mean(yg * yg, axis=-1, keepdims=True) + NORM_EPS)
        yhat = yg * rstd
        dnw_ref[...] += jnp.sum(dynv * yhat, axis=0, keepdims=True)
        dyh = dynv * nwv
        dyg = rstd * (dyh - yhat * jnp.mean(dyh * yhat, axis=-1, keepdims=True))
        dz_ref[...] = (dyg * yv * (sig * (1.0 + zv * (1.0 - sig)))).astype(BF16)
        dy_all = dyg * sz

        dg = jnp.zeros((SSD_CHUNK, SSD_CHUNK), F32)
        dc_acc = jnp.zeros((SSD_CHUNK, SSD_N), F32)
        db_acc = jnp.zeros((SSD_CHUNK, SSD_N), F32)
        dcs_mat = jnp.zeros((SSD_CHUNK, LANES), F32)
        ddt_mat = jnp.zeros((SSD_CHUNK, LANES), F32)
        dd_row = jnp.zeros((1, LANES), F32)
        racc_ref[...] = jnp.zeros_like(racc_ref)
        is_last = row == SSD_CHUNK - 1

        for j in range(n_pair):
            sl = slice(j * LANES, (j + 1) * LANES)
            ha, hb = head0 + 2 * j, head0 + 2 * j + 1
            p = _pair_terms(q, cst_ref, ha)
            xs_p = xs_ref[:, sl]
            dyp = dy_all[:, sl]
            xp = xs_p * p["dt_b"]
            xb = xp.astype(BF16)
            s_p = prev_ref[:, sl]
            s_pb = s_p.astype(BF16)
            dsn = ds_ref[:, sl]
            dsnb = dsn.astype(BF16)
            m_f = [q["scores"] * p["lm"][0], q["scores"] * p["lm"][1]]

            t0 = dyp * xs_p
            dd_row = dd_row + jnp.where(lane1 == ha, _sum_all(jnp.where(lo, t0, 0.0)), 0.0) \
                + jnp.where(lane1 == hb, _sum_all(jnp.where(lo, 0.0, t0)), 0.0)
            dxs_p = dl_ref[:, sl] * dyp

            yo = _dot(q["cb"], s_pb) * p["e_b"]
            dcs_b = (dyp * p["e_b"]).astype(BF16)
            dc_acc = dc_acc + _dot(dcs_b, s_pb, "nt")
            ds_yo = _dot(q["cb"], dcs_b, "tn")
            t1 = dyp * yo
            dcs_cols = [jnp.sum(jnp.where(lo, t1, 0.0), axis=1, keepdims=True),
                        jnp.sum(jnp.where(lo, 0.0, t1), axis=1, keepdims=True)]

            t2 = dsn * s_p
            dlast = [p["explast"][0] * _sum_all(jnp.where(lo, t2, 0.0)),
                     p["explast"][1] * _sum_all(jnp.where(lo, 0.0, t2))]
            ds_ref[:, sl] = dsn * p["explast_b"] + ds_yo
            w = _dot(q["bb"], dsnb)
            db_acc = db_acc + _dot((xp * p["dec_b"]).astype(BF16), dsnb, "nt")
            dxp = w * p["dec_b"]
            t3 = w * xp
            e = [jnp.sum(jnp.where(lo, t3, 0.0), axis=1, keepdims=True) * p["dec_cols"][0],
                 jnp.sum(jnp.where(lo, 0.0, t3), axis=1, keepdims=True) * p["dec_cols"][1]]
            for i in range(2):
                dlast[i] = dlast[i] + jnp.sum(e[i], axis=0, keepdims=True)
                dcs_cols[i] = dcs_cols[i] - e[i]

            dyb = dyp.astype(BF16)
            dy_h = [jnp.where(lo, dyp, 0.0).astype(BF16), jnp.where(lo, 0.0, dyp).astype(BF16)]
            dms = [_dot(dy_h[0], xb, "nt"), _dot(dy_h[1], xb, "nt")]
            dxp = dxp + jnp.where(lo, _dot(m_f[0].astype(BF16), dyb, "tn"), _dot(m_f[1].astype(BF16), dyb, "tn"))
            for i, h in enumerate((ha, hb)):
                dg = dg + dms[i] * p["lm"][i]
                qm = dms[i] * m_f[i]
                dcs_cols[i] = dcs_cols[i] + jnp.sum(qm, axis=1, keepdims=True)
                racc_ref[pl.ds(h, 1), :] = jnp.sum(qm, axis=0, keepdims=True)

            dxs_ref[:, sl] = dxs_p + dxp * p["dt_b"]
            t4 = dxp * xs_p
            ddt_cols = [jnp.sum(jnp.where(lo, t4, 0.0), axis=1, keepdims=True),
                        jnp.sum(jnp.where(lo, 0.0, t4), axis=1, keepdims=True)]
            for i, h in enumerate((ha, hb)):
                sel = lane == h
                dcs_mat = dcs_mat + jnp.where(sel, dcs_cols[i], 0.0) + jnp.where(sel & is_last, dlast[i], 0.0)
                ddt_mat = ddt_mat + jnp.where(sel, ddt_cols[i], 0.0)

        dcs_mat = dcs_mat - racc_ref[...].T
        tri_t = (row <= lane).astype(BF16)
        da = _dot_exact01(tri_t, dcs_mat)
        ddt = ddt_mat + da * a_row
        dalog_ref[...] += jnp.sum(jnp.where(mine, da * q["dt"], 0.0), axis=0, keepdims=True) * a_row
        draw = jnp.where(mine, ddt * sg_ref[...], 0.0)
        ddt_ref[...] = draw
        dbias_ref[...] += jnp.sum(draw, axis=0, keepdims=True)
        dd_ref[...] += dd_row
        dgb = dg.astype(BF16)
        dc_ref[...] = dc_acc + _dot(dgb, q["bb"])
        db_ref[...] = db_acc + _dot(dgb, q["cb"], "tn")

    rev = lambda c: nc - 1 - c
    grp = lambda width: pl.BlockSpec((None, 1, width), lambda g, c: (g, 0, 0))
    blk = lambda width, off: pl.BlockSpec((SSD_CHUNK, width), lambda g, c: (rev(c), off + g))
    head_vec = pl.BlockSpec((1, LANES), lambda g, c: (0, 0))
    chunk_rows = pl.BlockSpec((SSD_CHUNK, LANES), lambda g, c: (rev(c), 0))
    dt_, sg_, cs_, cst_, last_ = decay
    return _pcall(
        body, name=name, grid=(SSD_G, nc),
        out_shape=[jax.ShapeDtypeStruct(zx.shape, BF16), jax.ShapeDtypeStruct((L, d_inner), F32),
                   jax.ShapeDtypeStruct((L, SSD_G * SSD_N), F32), jax.ShapeDtypeStruct((L, SSD_G * SSD_N), F32),
                   jax.ShapeDtypeStruct((SSD_G, L, LANES), F32),
                   jax.ShapeDtypeStruct((SSD_G, 1, LANES), F32), jax.ShapeDtypeStruct((SSD_G, 1, LANES), F32),
                   jax.ShapeDtypeStruct((SSD_G, 1, LANES), F32), jax.ShapeDtypeStruct((SSD_G, 1, gw), F32)],
        in_specs=[blk(gw, 0), blk(gw, 0), blk(gw, 0), blk(gw, 0), blk(SSD_N, bc0), blk(SSD_N, bc0 + SSD_G),
                  pl.BlockSpec((None, None, SSD_N, gw), lambda g, c: (rev(c), g, 0, 0)),
                  chunk_rows, chunk_rows, chunk_rows,
                  pl.BlockSpec((None, SSD_CHUNK, LANES), lambda g, c: (rev(c), 0, 0)),
                  pl.BlockSpec((None, 1, LANES), lambda g, c: (rev(c), 0, 0)),
                  head_vec, grp(gw), grp(gw)],
        out_specs=[blk(gw, 0), blk(gw, 0), blk(SSD_N, 0), blk(SSD_N, 0),
                   pl.BlockSpec((None, SSD_CHUNK, LANES), lambda g, c: (g, rev(c), 0)),
                   grp(LANES), grp(LANES), grp(LANES), grp(gw)],
        scratch_shapes=[pltpu.VMEM((SSD_N, gw), F32), pltpu.VMEM((SSD_CHUNK, LANES), F32)],
        compiler_params=_cparams(("parallel", "arbitrary")))(
            dyn, y, zx, xc, xc, xc, prev, dt_, sg_, cs_, cst_, last_, alog_p, d_lane, nw)


def _cond_mod(c_pad, ada_w, ada_b_loc, after, name):
    depth, D, n = ada_w.shape
    rows = c_pad.shape[0]

    def body(c_ref, w_ref, b_ref, after_ref, mod_ref, cond_ref):
        cv = c_ref[...]
        cond = cv * _sigmoid(cv)
        cond_ref[...] = cond
        mod_ref[...] = _dot(cond.astype(BF16), w_ref[...].astype(BF16)) + b_ref[...]

    return _pcall(
        body, name=name, grid=(depth,),
        out_shape=[jax.ShapeDtypeStruct((depth, rows, n), F32), jax.ShapeDtypeStruct((rows, D), F32)],
        in_specs=[pl.BlockSpec((rows, D), lambda i: (0, 0)),
                  pl.BlockSpec((None, D, n), lambda i: (i, 0, 0)),
                  pl.BlockSpec((None, 1, n), lambda i: (i, 0, 0)),
                  pl.BlockSpec(memory_space=pl.ANY)],
        out_specs=[pl.BlockSpec((None, rows, n), lambda i: (i, 0, 0)),
                   pl.BlockSpec((rows, D), lambda i: (0, 0))],
        compiler_params=_cparams(("arbitrary",)))(c_pad, ada_w, ada_b_loc, after)


def _adamw_math(g, w, m, v):
    m_new = ADAM_B1 * m + (1.0 - ADAM_B1) * g
    v_new = ADAM_B2 * v + (1.0 - ADAM_B2) * (g * g)
    m_hat = m_new / (1.0 - ADAM_B1 ** ADAM_STEP)
    v_hat = v_new / (1.0 - ADAM_B2 ** ADAM_STEP)
    delta = -ADAM_LR * (m_hat / (jnp.sqrt(v_hat) + ADAM_EPS) + ADAM_WD * w)
    return delta, m_new, v_new


def _adamw_sum(parts, w, m, v, layer, name, prev=None, tr=None):
    depth, R, C = w.shape
    tr = _tile(R, tr if tr is not None else (512 if C <= 512 else 256))

    def body(p_ref, w_ref, m_ref, v_ref, *rest):
        g_ref, d_ref, mo_ref, vo_ref = rest[-4:]
        g = p_ref[0].astype(F32)
        for k in range(1, N_DEV):
            g = g + p_ref[k].astype(F32)
        d, mn, vn = _adamw_math(g, w_ref[...], m_ref[...], v_ref[...])
        g_ref[...] = g
        d_ref[...] = d
        mo_ref[...] = mn
        vo_ref[...] = vn

    blk = pl.BlockSpec((None, tr, C), lambda i: (layer, i, 0))
    prev = list(prev) if prev is not None else []
    return _pcall(
        body, name=name, grid=(R // tr,),
        out_shape=[jax.ShapeDtypeStruct((depth, R, C), F32)] * 4,
        in_specs=[pl.BlockSpec((N_DEV, tr, C), lambda i: (0, i, 0)), blk, blk, blk]
        + [pl.BlockSpec(memory_space=pl.ANY)] * len(prev),
        out_specs=[blk] * 4, input_output_aliases={4 + k: k for k in range(len(prev))},
        compiler_params=_cparams(("parallel",)))(parts, w, m, v, *prev)


def _adamw_small(parts, wmv, head_parts, head_wmv, loss_parts, name):
    n, nh = len(parts), len(head_parts)
    n_heads = head_wmv[0][0].shape[1] if nh else 0
    groups = head_parts[0].shape[1] if nh else 0
    d_model = loss_parts.shape[2]

    def body(*refs):
        p_refs, refs = refs[:n], refs[n:]
        wmv_refs, refs = refs[:3 * n], refs[3 * n:]
        hp_refs, refs = refs[:nh], refs[nh:]
        hwmv_refs, refs = refs[:3 * nh], refs[3 * nh:]
        loss_ref, refs = refs[0], refs[1:]
        outs, loss_out, head_scr = refs[:4 * (n + nh)], refs[4 * (n + nh)], refs[4 * (n + nh) + 1]

        def update(i, g, w_ref, m_ref, v_ref):
            res = (g,) + _adamw_math(g, w_ref[...], m_ref[...], v_ref[...])
            for o_ref, r in zip(outs[4 * i:4 * i + 4], res):
                o_ref[...] = r

        for i in range(n):
            g = p_refs[i][0]
            for k in range(1, N_DEV):
                g = g + p_refs[i][k]
            update(i, g, *wmv_refs[3 * i:3 * i + 3])
        for i in range(nh):
            g = None
            for k in range(N_DEV):
                for grp in range(groups):
                    g = hp_refs[i][k, grp] if g is None else g + hp_refs[i][k, grp]
            head_scr[...] = g
            update(n + i, head_scr[:, 0:n_heads], *hwmv_refs[3 * i:3 * i + 3])
        tot = loss_ref[0]
        for k in range(1, N_DEV):
            tot = tot + loss_ref[k]
        loss_out[...] = jnp.broadcast_to(_sum_all(tot) * (0.5 / d_model), loss_out.shape)

    operands = list(parts) + [a for t in wmv for a in t] + list(head_parts) + [a for t in head_wmv for a in t]
    operands.append(loss_parts)
    out_shape = [jax.ShapeDtypeStruct(t[0].shape, F32) for t in list(wmv) + list(head_wmv) for _ in range(4)]
    out_shape.append(jax.ShapeDtypeStruct((1, LANES), F32))
    vmem = pl.BlockSpec(memory_space=pltpu.VMEM)
    outs = _pcall(body, name=name, out_shape=out_shape, in_specs=[vmem] * len(operands),
                  out_specs=[vmem] * len(out_shape), scratch_shapes=[pltpu.VMEM((1, LANES), F32)],
                  compiler_params=_cparams())(*operands)
    return [outs[4 * i:4 * i + 4] for i in range(n + nh)], outs[-1]


def _ada_adamw(cond_pad, dmod_pad, w, m, v, name, tr=512):
    depth, D, n = w.shape
    rows = cond_pad.shape[0]
    tr = _tile(D, tr)

    def body(c_ref, dm_ref, w_ref, m_ref, v_ref, g_ref, d_ref, mo_ref, vo_ref):
        g = _dot(c_ref[...].astype(BF16), dm_ref[...].astype(BF16), "tn")
        d, mn, vn = _adamw_math(g, w_ref[...], m_ref[...], v_ref[...])
        g_ref[...] = g
        d_ref[...] = d
        mo_ref[...] = mn
        vo_ref[...] = vn

    blk = pl.BlockSpec((None, tr, n), lambda i, r: (i, r, 0))
    return _pcall(
        body, name=name, grid=(depth, D // tr),
        out_shape=[jax.ShapeDtypeStruct((depth, D, n), F32)] * 4,
        in_specs=[pl.BlockSpec((rows, tr), lambda i, r: (0, r)),
                  pl.BlockSpec((None, rows, n), lambda i, r: (i, 0, 0)), blk, blk, blk],
        out_specs=[blk] * 4, compiler_params=_cparams(("parallel", "parallel")))(cond_pad, dmod_pad, w, m, v)


def kernel(x, c, ada_w, ada_b, mix_norm_w, mlp_norm_w, mlp_up, mlp_down, ssd_in_w, ssd_conv_w, ssd_conv_b, ssd_dt_bias, ssd_A_log, ssd_D, ssd_norm_w, ssd_out_w, sc_in_w, sc_conv_w, sc_out_w, final_norm_w, loss_target, m_ada_w, m_ada_b, m_mix_norm_w, m_mlp_norm_w, m_mlp_up, m_mlp_down, m_ssd_in_w, m_ssd_conv_w, m_ssd_conv_b, m_ssd_dt_bias, m_ssd_A_log, m_ssd_D, m_ssd_norm_w, m_ssd_out_w, m_sc_in_w, m_sc_conv_w, m_sc_out_w, m_final_norm_w, v_ada_w, v_ada_b, v_mix_norm_w, v_mlp_norm_w, v_mlp_up, v_mlp_down, v_ssd_in_w, v_ssd_conv_w, v_ssd_conv_b, v_ssd_dt_bias, v_ssd_A_log, v_ssd_D, v_ssd_norm_w, v_ssd_out_w, v_sc_in_w, v_sc_conv_w, v_sc_out_w, v_final_norm_w):
    weights = dict(ada_w=ada_w, ada_b=ada_b, mix_norm_w=mix_norm_w, mlp_norm_w=mlp_norm_w, mlp_up=mlp_up,
                   mlp_down=mlp_down, ssd_in_w=ssd_in_w, ssd_conv_w=ssd_conv_w, ssd_conv_b=ssd_conv_b,
                   ssd_dt_bias=ssd_dt_bias, ssd_A_log=ssd_A_log, ssd_D=ssd_D, ssd_norm_w=ssd_norm_w,
                   ssd_out_w=ssd_out_w, sc_in_w=sc_in_w, sc_conv_w=sc_conv_w, sc_out_w=sc_out_w,
                   final_norm_w=final_norm_w)
    moms = dict(ada_w=m_ada_w, ada_b=m_ada_b, mix_norm_w=m_mix_norm_w, mlp_norm_w=m_mlp_norm_w, mlp_up=m_mlp_up,
                mlp_down=m_mlp_down, ssd_in_w=m_ssd_in_w, ssd_conv_w=m_ssd_conv_w, ssd_conv_b=m_ssd_conv_b,
                ssd_dt_bias=m_ssd_dt_bias, ssd_A_log=m_ssd_A_log, ssd_D=m_ssd_D, ssd_norm_w=m_ssd_norm_w,
                ssd_out_w=m_ssd_out_w, sc_in_w=m_sc_in_w, sc_conv_w=m_sc_conv_w, sc_out_w=m_sc_out_w,
                final_norm_w=m_final_norm_w)
    vars_ = dict(ada_w=v_ada_w, ada_b=v_ada_b, mix_norm_w=v_mix_norm_w, mlp_norm_w=v_mlp_norm_w, mlp_up=v_mlp_up,
                 mlp_down=v_mlp_down, ssd_in_w=v_ssd_in_w, ssd_conv_w=v_ssd_conv_w, ssd_conv_b=v_ssd_conv_b,
                 ssd_dt_bias=v_ssd_dt_bias, ssd_A_log=v_ssd_A_log, ssd_D=v_ssd_D, ssd_norm_w=v_ssd_norm_w,
                 ssd_out_w=v_ssd_out_w, sc_in_w=v_sc_in_w, sc_conv_w=v_sc_conv_w, sc_out_w=v_sc_out_w,
                 final_norm_w=v_final_norm_w)
    names = list(weights)

    L, D = x.shape[1], x.shape[2]
    d_inner = 2 * D
    n_heads = d_inner // SSD_P
    hpg = n_heads // SSD_G
    gw = d_inner // SSD_G
    conv_dim = d_inner + 2 * SSD_G * SSD_N
    zx_dim = d_inner + conv_dim
    zx_pad = -(-(zx_dim + LANES) // 512) * 512
    in_ws = ssd_in_w.shape[2]
    in_base, in_off, in_win = _window_geometry(in_ws)
    me = _my_index()
    x0 = x[0]
    tgt = loss_target[0]

    n_mod = ada_w.shape[2]
    (c_all,) = _exchange([c], "gather_c", gather=True)
    gather_handle = {}
    (gather_handle["ssd_in_w"],), token_in = _xfer_start(
        [ssd_in_w[0].astype(BF16)], "gather_start_ssd_in_w", gather=True, via_sibling=(0,), after=(c_all,))
    c_pad = jnp.pad(c_all.reshape(N_DEV, D), ((0, 16 - N_DEV), (0, 0)))
    ada_b_loc = lax.dynamic_slice_in_dim(ada_b, me * n_mod, n_mod, axis=1).reshape(2, 1, n_mod)
    mod_blk, cond_pad = _cond_mod(c_pad, ada_w, ada_b_loc, token_in, "cond_mod")
    gather_order = ["mod", "ssd_conv_w", "sc_conv_w", "ssd_out_w", "up0", "down0", "sc_in_w", "sc_out_w", "up1",
                    "down1"]
    gather_src = dict(mod=mod_blk, ssd_conv_w=ssd_conv_w[0], sc_conv_w=sc_conv_w[0],
                      ssd_out_w=ssd_out_w[0].astype(BF16),
                      up0=mlp_up[0].astype(BF16), down0=mlp_down[0].astype(BF16),
                      sc_in_w=sc_in_w[0].astype(BF16), sc_out_w=sc_out_w[0].astype(BF16),
                      up1=mlp_up[1].astype(BF16), down1=mlp_down[1].astype(BF16))
    handles, gather_token = _xfer_start([gather_src[k] for k in gather_order], "gather_start", gather=True,
                                        via_sibling=tuple(range(3, len(gather_order))))
    gather_handle.update(zip(gather_order, handles))

    def gathered(keys, after, forward):
        tag = "_".join(keys)
        lands = _xfer_wait([gather_handle[k] for k in keys], after, f"gather_wait_{tag}", gather=True)
        return _sibling_forward(lands, f"gather_forward_{tag}") if forward else lands

    def forward_behind(keys, after):
        tag = "_".join(keys)
        lands = _xfer_wait([gather_handle[k] for k in keys], after, f"gather_wait_{tag}", gather=True)
        fwd_handles, token = _sibling_forward_start(lands, f"gather_forward_start_{tag}")
        return (lambda done: _sibling_forward_wait(fwd_handles, done, f"gather_forward_wait_{tag}")), token

    (ssd_in_g,) = gathered(["ssd_in_w"], (gather_token, m_ssd_in_w, v_ssd_in_w), True)
    w_in_all = _shards_to_columns(ssd_in_g, in_base, in_off, in_win, zx_pad, "ssd_in_w_columns")
    (mod_all,) = gathered(["mod"], w_in_all, False)
    mod_mine = lax.dynamic_index_in_dim(mod_all, me, axis=2, keepdims=False)
    mod_mine = jnp.transpose(mod_mine, (1, 0, 2)).reshape(2, 6, 1, D)
    sh_m, sc_m, g_m, sh_f, sc_f, g_f = [[mod_mine[i, k] for i in range(2)] for k in range(6)]

    vec = lambda a: a.reshape(1, -1)
    small = {}

    _, h0 = _norm_mod_fwd(x0, None, None, vec(mix_norm_w[0]), sc_m[0], sh_m[0], "l0_mix_norm")
    cw_all, scw_all = gathered(["ssd_conv_w", "sc_conv_w"], h0, False)
    (zx,) = _mm_nn(h0, w_in_all, F32, "ssd_in_proj", tm=2048, tn=512)
    conv_b0 = vec(ssd_conv_b[0])
    conv_w_full = jnp.transpose(cw_all, (1, 0, 2)).reshape(SSD_K, conv_dim)
    sc_conv_full = jnp.transpose(scw_all, (1, 0, 2)).reshape(SC_K, D)
    xc = _ssd_conv_fwd(zx, conv_w_full, conv_b0, d_inner, conv_dim, "ssd_conv")
    bias_p = jnp.pad(ssd_dt_bias[0], (0, LANES - n_heads)).reshape(1, LANES)
    alog_p = jnp.pad(ssd_A_log[0], (0, LANES - n_heads)).reshape(1, LANES)
    d_lane = jnp.repeat(ssd_D[0], SSD_P).reshape(SSD_G, 1, gw)
    nw_g = ssd_norm_w[0].reshape(SSD_G, 1, gw)
    finish, token = forward_behind(["ssd_out_w"], xc)
    decay = _ssd_decay(zx, bias_p, alog_p, n_heads, zx_dim // LANES, "ssd_decay")
    y_ssd, yn, prev = _ssd_fwd(zx, xc, decay, d_lane, nw_g, d_inner, token, "ssd_scan")
    ups, downs = [None, None], [None, None]
    (ssd_out_g,) = finish(yn)
    w_ssd_out = ssd_out_g.reshape(-1, D)
    finish, token = forward_behind(["up0", "down0"], ssd_out_g)
    (mix0,) = _mm_nn(yn, w_ssd_out, F32, "ssd_out_proj", tm=1024, tk=2048, after=(token,))
    x1, h1 = _norm_mod_fwd(x0, mix0, g_m[0], vec(mlp_norm_w[0]), sc_f[0], sh_f[0], "l0_mlp_norm")
    ups[0], down0_g = finish(h1)
    downs[0] = down0_g.reshape(-1, D)
    u0, s0 = _mm_nn_blocked(h1, ups[0], "l0_mlp_up", _ep_relu2, [BF16, BF16])
    finish, token = forward_behind(["sc_in_w", "sc_out_w", "up1", "down1"], s0)
    (d0,) = _mm_nn(s0, downs[0], F32, "l0_mlp_down", tm=1024, tk=2048, after=(token,))
    x2, h2 = _norm_mod_fwd(x1, d0, g_f[0], vec(mix_norm_w[1]), sc_m[1], sh_m[1], "l1_mix_norm")
    sc_in_g, sc_out_g, ups[1], down1_g = finish(h2)
    w_sc_out, downs[1] = sc_out_g.reshape(-1, D), down1_g.reshape(-1, D)
    (proj,) = _mm_nn_blocked(h2, sc_in_g, "sc_in_proj", _ep_store(F32), [F32])
    yc = _sc_conv_fwd(proj, sc_conv_full, "sc_conv")
    (mix1,) = _mm_nn(yc, w_sc_out, F32, "sc_out_proj")
    x3, h3 = _norm_mod_fwd(x2, mix1, g_m[1], vec(mlp_norm_w[1]), sc_f[1], sh_f[1], "l1_mlp_norm")
    u1, s1 = _mm_nn_blocked(h3, ups[1], "l1_mlp_up", _ep_relu2, [BF16, BF16])
    (d1,) = _mm_nn(s1, downs[1], F32, "l1_mlp_down", tm=1024, tk=2048)

    dx, loss_lane, dfw, dd1, dg = _final_loss(x3, d1, g_f[1], vec(final_norm_w), tgt, "final_loss")
    small["final_norm_w"] = dfw

    dmod = [[None] * 6 for _ in range(2)]
    dmod[1][5] = dg

    def mlp_backward(i, dx_out, dd, x_mid, h_in, u, s, mix, gate):
        du = _mm_nt(dd, downs[i], BF16, f"l{i}_mlp_down_bwd", epilogue=_ep_relu2_bwd, extra=(u,))
        gdown = _mm_tn(s, dd, BF16, f"l{i}_mlp_down_wgrad").reshape(N_DEV, -1, D)
        gup = _mm_tn_blocked(h_in, du, BF16, f"l{i}_mlp_up_wgrad")
        (h_down, h_up), token = _xfer_start([gdown, gup], f"l{i}_mlp_grads_start", gather=False)
        grad_handle[f"mlp_down{i}"], grad_handle[f"mlp_up{i}"] = h_down, h_up
        dh = _mm_nt_blocked(du, ups[i], F32, f"l{i}_mlp_up_bwd", after=(token,))
        dxm, dsh, dsc, dnw, dmix, dgate = _norm_mod_bwd(dh, x_mid, vec(mlp_norm_w[i]), sc_f[i], dx_out,
                                                        f"l{i}_mlp_norm_bwd", branch=(mix, gate))
        dmod[i][3], dmod[i][4], dmod[i][2] = dsh, dsc, dgate
        return dxm, dmix, dnw

    grad_handle = {}
    dx3, dyc, dnw_mlp1 = mlp_backward(1, dx, dd1, x3, h3, u1, s1, mix1, g_m[1])
    g_sc_out = _mm_tn(yc, dyc, BF16, "sc_out_wgrad").reshape(N_DEV, -1, D)
    dconv_out = _mm_nt(dyc, w_sc_out, F32, "sc_out_bwd")
    dbg, dcg, dxv, dscw = _sc_conv_bwd(proj, sc_conv_full, dconv_out, "sc_conv_bwd")
    dproj = jnp.concatenate([dbg, dcg, dxv], axis=1)
    g_sc_in = _mm_tn_blocked(h2, dproj, BF16, "sc_in_wgrad")
    (grad_handle["sc_out_w0"], grad_handle["sc_in_w0"]), token = _xfer_start(
        [g_sc_out, g_sc_in], "sc_grads_start", gather=False)
    dh2 = _mm_nt_blocked(dproj, sc_in_g, F32, "sc_in_bwd", after=(token,))
    dx2, dsh, dsc, dnw_mix1, dd0, dg = _norm_mod_bwd(dh2, x2, vec(mix_norm_w[1]), sc_m[1], dx3, "l1_mix_norm_bwd",
                                                     branch=(d0, g_f[0]))
    dmod[1][0], dmod[1][1], dmod[0][5] = dsh, dsc, dg
    dx1, dyo, dnw_mlp0 = mlp_backward(0, dx2, dd0, x1, h1, u0, s0, mix0, g_m[0])
    g_ssd_out = _mm_tn(yn, dyo, BF16, "ssd_out_wgrad").reshape(N_DEV, -1, D)
    (grad_handle["ssd_out_w0"],), token = _xfer_start([g_ssd_out], "ssd_out_grad_start", gather=False)
    dyn = _mm_nt(dyo, w_ssd_out, F32, "ssd_out_bwd", after=(token,))
    dz, dxs, db_, dc_, ddt, dbias, dalog, dd_, dnw_ssd = _ssd_bwd(
        dyn, y_ssd, zx, xc, prev, decay, alog_p, d_lane, nw_g, d_inner, "ssd_scan_bwd")
    dzx, dcw, dcb = _ssd_conv_bwd(zx, conv_w_full, conv_b0, [dxs, db_, dc_], dz, d_inner, "ssd_conv_bwd")
    dzx = _dzx_finish(dzx, ddt, zx_dim, "ssd_dzx_finish")
    g_in_all = _mm_tn(h0, dzx, BF16, "ssd_in_wgrad", tn=512, tk=2048)
    g_ssd_in = _columns_to_shards(g_in_all, in_ws, in_base, in_off, in_win, "ssd_in_wgrad_shards")
    (grad_handle["ssd_in_w0"],), token = _xfer_start([g_ssd_in], "ssd_in_grad_start", gather=False)
    dh0 = _mm_nt(dzx, w_in_all, F32, "ssd_in_bwd", tm=512, tk=dzx.shape[1], after=(token,))
    grad_x, dsh, dsc, dnw_mix0 = _norm_mod_bwd(dh0, x0, vec(mix_norm_w[0]), sc_m[0], dx1, "l0_mix_norm_bwd")
    dmod[0][0], dmod[0][1] = dsh, dsc

    small["ada_b"] = jnp.concatenate([jnp.concatenate(dmod[i], axis=1) for i in range(2)], axis=0)
    small["mix_norm_w"] = jnp.concatenate([dnw_mix0, dnw_mix1], axis=0)
    small["mlp_norm_w"] = jnp.concatenate([dnw_mlp0, dnw_mlp1], axis=0)
    small["ssd_conv_w"] = dcw
    small["ssd_conv_b"] = dcb
    small["ssd_norm_w"] = dnw_ssd.reshape(1, d_inner)
    small["sc_conv_w"] = dscw
    small["loss"] = loss_lane
    small_names = list(small)
    head_names = ["ssd_dt_bias", "ssd_A_log", "ssd_D"]
    handles, small_token = _xfer_start([small[k] for k in small_names] + [dbias, dalog, dd_],
                                       "small_grads_start", gather=True)

    out_g, out_d, out_m, out_v = {}, {}, {}, {}

    layer_res = {}

    def big_update(name, i, after):
        (parts,) = _xfer_wait([grad_handle[f"{name}{i}"]], after, f"grads_wait_{name}_{i}", gather=False)
        res = _adamw_sum(parts, weights[name], moms[name], vars_[name], i, f"adamw_{name}_{i}",
                         prev=layer_res.get(name))
        layer_res[name] = res
        return res[1]

    chain = small_token
    for name, i in [("mlp_down", 1), ("mlp_up", 1), ("sc_out_w", 0), ("sc_in_w", 0), ("mlp_down", 0),
                    ("mlp_up", 0), ("ssd_out_w", 0), ("ssd_in_w", 0)]:
        chain = big_update(name, i, chain)
    gathered_small = _xfer_wait(handles, chain, "small_grads_wait", gather=True)
    small_all = dict(zip(small_names + head_names, gathered_small))

    dmod_loc = lax.dynamic_slice_in_dim(small_all["ada_b"], me * n_mod, n_mod, axis=2)
    dmod_pad = jnp.pad(jnp.transpose(dmod_loc, (1, 0, 2)), ((0, 0), (0, 16 - N_DEV), (0, 0)))
    out_g["ada_w"], out_d["ada_w"], out_m["ada_w"], out_v["ada_w"] = _ada_adamw(
        cond_pad, dmod_pad, ada_w, m_ada_w, v_ada_w, "adamw_ada_w")

    for k in ("ssd_conv_w", "sc_conv_w"):
        n_loc = weights[k].shape[2]
        small_all[k] = lax.dynamic_slice_in_dim(small_all[k], me * n_loc, n_loc, axis=2)
    plain = [k for k in small_names if k != "loss"]
    as2d = lambda a: a.reshape(-1, a.shape[-1])
    res, loss_row = _adamw_small(
        [small_all[k] for k in plain], [tuple(as2d(d[k]) for d in (weights, moms, vars_)) for k in plain],
        [small_all[k] for k in head_names], [tuple(as2d(d[k]) for d in (weights, moms, vars_)) for k in head_names],
        small_all["loss"], "adamw_small")
    loss = loss_row[0, 0]
    for k, res4 in zip(plain + head_names, res):
        for r, dst in zip(res4, (out_g, out_d, out_m, out_v)):
            dst[k] = r.reshape(weights[k].shape)
    for name, res4 in layer_res.items():
        for r, dst in zip(res4, (out_g, out_d, out_m, out_v)):
            dst[name] = r

    return (loss, grad_x[None], *[out_g[k] for k in names], *[out_d[k] for k in names],
            *[out_m[k] for k in names], *[out_v[k] for k in names])
```

```python
import jax
import jax.numpy as jnp
from jax import lax
from jax.experimental import pallas as pl
from jax.experimental.pallas import tpu as pltpu

F32 = jnp.float32
BF16 = jnp.bfloat16
N_DEV = 8
MESH = pl.DeviceIdType.MESH

NORM_EPS = 1e-5
SSD_G = 4
SSD_P = 64
SSD_N = 128
SSD_CHUNK = 128
SSD_K = 4
SC_K = 3
LANES = 128

ADAM_LR = 0.001
ADAM_B1 = 0.9
ADAM_B2 = 0.999
ADAM_EPS = 1e-08
ADAM_WD = 0.01
ADAM_STEP = 10

VMEM_LIMIT = 56 * 1024 * 1024


def _pcall(body, **kw):
    return pl.pallas_call(body, **kw)


def _cparams(sem=None):
    if sem is None:
        return pltpu.CompilerParams(vmem_limit_bytes=VMEM_LIMIT)
    return pltpu.CompilerParams(dimension_semantics=sem, vmem_limit_bytes=VMEM_LIMIT)


def _my_index():
    return 4 * lax.axis_index("x") + 2 * lax.axis_index("y") + lax.axis_index("c")


_PEER_MASKS = [(0, 0, 1), (0, 1, 0), (0, 1, 1), (1, 0, 0), (1, 0, 1), (1, 1, 0), (1, 1, 1)]


def _peers():
    x, y, c = lax.axis_index("x"), lax.axis_index("y"), lax.axis_index("c")
    out = []
    for mx, my, mc in _PEER_MASKS:
        px = (1 - x) if mx else x
        py = (1 - y) if my else y
        pc = (1 - c) if mc else c
        out.append(((px, py, pc), 4 * px + 2 * py + pc))
    return out


def _exchange(arrs, name, gather):
    n = len(arrs)
    n_peer = N_DEV - 1

    def body(*refs):
        ins, outs = refs[:n], refs[n:2 * n]
        send_sems, recv_sems, local_sems = refs[2 * n:]
        me = _my_index()
        peers = _peers()
        started = []
        for a in range(n):
            src_own = ins[a] if gather else ins[a].at[me]
            own = pltpu.make_async_copy(src_own, outs[a].at[me], local_sems.at[a])
            own.start()
            started.append(own)
        sends = []
        for a in range(n):
            for k, (peer, pidx) in enumerate(peers):
                src = ins[a] if gather else ins[a].at[pidx]
                cp = pltpu.make_async_remote_copy(
                    src_ref=src, dst_ref=outs[a].at[me],
                    send_sem=send_sems.at[a * n_peer + k], recv_sem=recv_sems.at[a * n_peer + k],
                    device_id=peer, device_id_type=MESH)
                cp.start()
                sends.append(cp)
        for a in range(n):
            for k, (peer, pidx) in enumerate(peers):
                src = ins[a] if gather else ins[a].at[pidx]
                pltpu.make_async_remote_copy(
                    src_ref=src, dst_ref=outs[a].at[pidx],
                    send_sem=send_sems.at[a * n_peer + k], recv_sem=recv_sems.at[a * n_peer + k],
                    device_id=peer, device_id_type=MESH).wait_recv()
        for cp in sends:
            cp.wait_send()
        for own in started:
            own.wait()

    if gather:
        out_shape = [jax.ShapeDtypeStruct((N_DEV,) + a.shape, a.dtype) for a in arrs]
    else:
        out_shape = [jax.ShapeDtypeStruct(a.shape, a.dtype) for a in arrs]
    any_spec = pl.BlockSpec(memory_space=pl.ANY)
    outs = _pcall(
        body, name=name, out_shape=out_shape,
        in_specs=[any_spec] * n, out_specs=[any_spec] * n,
        scratch_shapes=[pltpu.SemaphoreType.DMA((n * n_peer,)), pltpu.SemaphoreType.DMA((n * n_peer,)),
                        pltpu.SemaphoreType.DMA((n,))],
        compiler_params=pltpu.CompilerParams(has_side_effects=True),
    )(*arrs)
    return list(outs)


def _sibling_forward_start(lands, name):
    n = len(lands)
    n_fwd = len(_OTHER_CHIPS)

    def body(*refs):
        ins, bufs = refs[:n], refs[3 * n:4 * n]
        token = refs[-1]
        sibling = (lax.axis_index("x"), lax.axis_index("y"), 1 - lax.axis_index("c"))
        peers = _peers()
        for a in range(n):
            send_sems, recv_sems = refs[n + 2 * a], refs[n + 2 * a + 1]
            for j, k in enumerate(_OTHER_CHIPS):
                slot = peers[k][1]
                pltpu.make_async_remote_copy(
                    src_ref=ins[a].at[slot], dst_ref=bufs[a].at[slot], send_sem=send_sems.at[j],
                    recv_sem=recv_sems.at[j], device_id=sibling, device_id_type=MESH).start()
        token[...] = jnp.zeros_like(token)

    out_shape, out_specs = [], []
    for _ in range(n):
        out_shape += [pltpu.SemaphoreType.DMA((n_fwd,)), pltpu.SemaphoreType.DMA((n_fwd,))]
        out_specs += [_SEM, _SEM]
    out_shape += [pltpu.HBM(a.shape, a.dtype) for a in lands] + [jax.ShapeDtypeStruct((8, LANES), F32)]
    out_specs += [_HBM] * n + [pl.BlockSpec(memory_space=pltpu.VMEM)]
    outs = _pcall(
        body, name=name, out_shape=tuple(out_shape), in_specs=[_HBM] * n, out_specs=tuple(out_specs),
        input_output_aliases={a: 2 * n + a for a in range(n)},
        compiler_params=pltpu.CompilerParams(has_side_effects=_DATAFLOW),
    )(*[pltpu.with_memory_space_constraint(a, pltpu.HBM) for a in lands])
    return [(outs[2 * n + a], outs[2 * a], outs[2 * a + 1]) for a in range(n)], outs[-1]


def _sibling_forward_wait(handles, after, name):
    n = len(handles)

    def body(*refs):
        sibling = (lax.axis_index("x"), lax.axis_index("y"), 1 - lax.axis_index("c"))
        peers = _peers()
        for a in range(n):
            buf, send_sems, recv_sems = refs[3 * a:3 * a + 3]
            for j, k in enumerate(_OTHER_CHIPS):
                (px, py, pc), slot = peers[k]
                theirs = 4 * px + 2 * py + (1 - pc)
                cp = pltpu.make_async_remote_copy(
                    src_ref=buf.at[slot], dst_ref=buf.at[theirs], send_sem=send_sems.at[j],
                    recv_sem=recv_sems.at[j], device_id=sibling, device_id_type=MESH)
                cp.wait_send()
                cp.wait_recv()

    operands, in_specs = [], []
    for h in handles:
        operands += list(h)
        in_specs += [_HBM, _SEM, _SEM]
    outs = _pcall(
        body, name=name, out_shape=tuple(pltpu.HBM(h[0].shape, h[0].dtype) for h in handles),
        in_specs=in_specs + [pl.BlockSpec(memory_space=pl.ANY)], out_specs=tuple([_HBM] * n),
        input_output_aliases={3 * a: a for a in range(n)},
        compiler_params=pltpu.CompilerParams(has_side_effects=_DATAFLOW),
    )(*operands, after)
    return list(outs)


_HBM = pl.BlockSpec(memory_space=pltpu.HBM)
_SEM = pl.BlockSpec(memory_space=pltpu.SEMAPHORE)
_DATAFLOW = pltpu.SideEffectType.DATAFLOW_SIDE_EFFECTING


_ALL_PEERS = tuple(range(N_DEV - 1))
_SAME_CORE_PEERS = (0, 1, 3, 5)
_OTHER_CHIPS = (1, 3, 5)


def _xfer_start(arrs, name, gather, via_sibling=(), after=()):
    n = len(arrs)
    n_peer = N_DEV - 1
    n_after = len(after)
    peer_ks = [_SAME_CORE_PEERS if a in via_sibling else _ALL_PEERS for a in range(n)]

    def body(*refs):
        ins, lands = refs[:n], refs[n:2 * n]
        sems = refs[2 * n + n_after:5 * n + n_after]
        token = refs[-1]
        me = _my_index()
        peers = _peers()
        for a in range(n):
            send_sems, recv_sems, loc_sem = sems[3 * a:3 * a + 3]
            src_own = ins[a] if gather else ins[a].at[me]
            pltpu.make_async_copy(src_own, lands[a].at[me], loc_sem).start()
            for k in peer_ks[a]:
                peer, pidx = peers[k]
                src = ins[a] if gather else ins[a].at[pidx]
                pltpu.make_async_remote_copy(
                    src_ref=src, dst_ref=lands[a].at[me], send_sem=send_sems.at[k], recv_sem=recv_sems.at[k],
                    device_id=peer, device_id_type=MESH).start()
        token[...] = jnp.zeros_like(token)

    land_shapes = [((N_DEV,) + a.shape) if gather else a.shape for a in arrs]
    out_shape, out_specs = [], []
    for _ in range(n):
        out_shape += [pltpu.SemaphoreType.DMA((n_peer,)), pltpu.SemaphoreType.DMA((n_peer,)),
                      pltpu.SemaphoreType.DMA(())]
        out_specs += [_SEM, _SEM, _SEM]
    out_shape += [pltpu.HBM(a.shape, a.dtype) for a in arrs]
    out_shape += [pltpu.HBM(s, a.dtype) for s, a in zip(land_shapes, arrs)]
    out_shape += [jax.ShapeDtypeStruct((8, LANES), F32)]
    out_specs += [_HBM] * (2 * n) + [pl.BlockSpec(memory_space=pltpu.VMEM)]
    aliases = {}
    for a in range(n):
        aliases[a] = 3 * n + a
        aliases[n + a] = 4 * n + a
    operands = [pltpu.with_memory_space_constraint(a, pltpu.HBM) for a in arrs]
    operands += [pltpu.with_memory_space_constraint(lax.empty(s, a.dtype), pltpu.HBM)
                 for s, a in zip(land_shapes, arrs)]
    outs = _pcall(
        body, name=name, out_shape=tuple(out_shape),
        in_specs=[_HBM] * (2 * n) + [pl.BlockSpec(memory_space=pl.ANY)] * n_after, out_specs=tuple(out_specs),
        input_output_aliases=aliases,
        compiler_params=pltpu.CompilerParams(has_side_effects=_DATAFLOW),
    )(*operands, *after)
    handles = []
    for a in range(n):
        handles.append((outs[3 * n + a], outs[4 * n + a], outs[3 * a], outs[3 * a + 1], outs[3 * a + 2],
                        peer_ks[a]))
    return handles, outs[-1]


def _xfer_wait(handles, after, name, gather):
    n = len(handles)
    after = tuple(after) if isinstance(after, (tuple, list)) else (after,)
    peer_ks = [h[5] for h in handles]

    def body(*refs):
        me = _my_index()
        peers = _peers()
        for a in range(n):
            src_ref, land_ref, send_ref, recv_ref, loc_ref = refs[5 * a:5 * a + 5]
            src_own = src_ref if gather else src_ref.at[me]
            pltpu.make_async_copy(src_own, land_ref.at[me], loc_ref).wait()
            for k in peer_ks[a]:
                peer, pidx = peers[k]
                src = src_ref if gather else src_ref.at[pidx]
                cp = pltpu.make_async_remote_copy(
                    src_ref=src, dst_ref=land_ref.at[pidx], send_sem=send_ref.at[k], recv_sem=recv_ref.at[k],
                    device_id=peer, device_id_type=MESH)
                cp.wait_send()
                cp.wait_recv()

    operands, in_specs, out_shape, aliases = [], [], [], {}
    for a, h in enumerate(handles):
        operands += list(h[:5])
        in_specs += [_HBM, _HBM, _SEM, _SEM, _SEM]
        out_shape += [pltpu.HBM(h[0].shape, h[0].dtype), pltpu.HBM(h[1].shape, h[1].dtype)]
        aliases[5 * a] = 2 * a
        aliases[5 * a + 1] = 2 * a + 1
    outs = _pcall(
        body, name=name, out_shape=tuple(out_shape),
        in_specs=in_specs + [pl.BlockSpec(memory_space=pl.ANY)] * len(after),
        out_specs=tuple([_HBM] * (2 * n)), input_output_aliases=aliases,
        compiler_params=pltpu.CompilerParams(has_side_effects=_DATAFLOW),
    )(*operands, *after)
    return [outs[2 * a + 1] for a in range(n)]


def _sibling_forward(lands, name):
    n = len(lands)
    n_fwd = len(_OTHER_CHIPS)

    def body(*refs):
        ins, bufs = refs[:n], refs[n:2 * n]
        send_sems, recv_sems = refs[2 * n:]
        x, y, c = lax.axis_index("x"), lax.axis_index("y"), lax.axis_index("c")
        sibling = (x, y, 1 - c)
        peers = _peers()
        sends = []
        for a in range(n):
            for j, k in enumerate(_OTHER_CHIPS):
                slot = peers[k][1]
                cp = pltpu.make_async_remote_copy(
                    src_ref=ins[a].at[slot], dst_ref=bufs[a].at[slot],
                    send_sem=send_sems.at[a * n_fwd + j], recv_sem=recv_sems.at[a * n_fwd + j],
                    device_id=sibling, device_id_type=MESH)
                cp.start()
                sends.append(cp)
        for a in range(n):
            for j, k in enumerate(_OTHER_CHIPS):
                (px, py, pc), slot = peers[k]
                theirs = 4 * px + 2 * py + (1 - pc)
                pltpu.make_async_remote_copy(
                    src_ref=ins[a].at[slot], dst_ref=bufs[a].at[theirs],
                    send_sem=send_sems.at[a * n_fwd + j], recv_sem=recv_sems.at[a * n_fwd + j],
                    device_id=sibling, device_id_type=MESH).wait_recv()
        for cp in sends:
            cp.wait_send()

    any_spec = pl.BlockSpec(memory_space=pl.ANY)
    outs = _pcall(
        body, name=name, out_shape=[jax.ShapeDtypeStruct(a.shape, a.dtype) for a in lands],
        in_specs=[any_spec] * n, out_specs=[any_spec] * n,
        input_output_aliases={a: a for a in range(n)},
        scratch_shapes=[pltpu.SemaphoreType.DMA((n * n_fwd,)), pltpu.SemaphoreType.DMA((n * n_fwd,))],
        compiler_params=pltpu.CompilerParams(has_side_effects=True),
    )(*lands)
    return list(outs)


_DIMS = {"nn": (((1,), (0,)), ((), ())), "nt": (((1,), (1,)), ((), ())), "tn": (((0,), (0,)), ((), ()))}


def _dot(a, b, mode="nn"):
    if mode == "nt_blocks":
        n = b.shape[2]
        return sum(_dot(a[:, i * n:(i + 1) * n], b[i], "nt") for i in range(b.shape[0]))
    return lax.dot_general(a, b, _DIMS[mode], preferred_element_type=F32)


def _mm(a, b, *, mode, grid, a_spec, b_spec, out_shape, out_specs, acc_shape, epilogue, name,
        extra=(), extra_specs=(), after=(), semantics=("parallel", "parallel", "arbitrary")):
    nk = grid[2]
    n_extra = len(extra)
    n_in = 2 + n_extra + len(after)

    def body_single(*refs):
        a_ref, b_ref = refs[0], refs[1]
        epilogue(_dot(a_ref[...], b_ref[...], mode), refs[2:2 + n_extra], refs[n_in:])

    def body_acc(*refs):
        a_ref, b_ref = refs[0], refs[1]
        ex = refs[2:2 + n_extra]
        outs = refs[n_in:-1]
        acc = refs[-1]
        k = pl.program_id(2)

        @pl.when(k == 0)
        def _():
            acc[...] = jnp.zeros_like(acc)

        acc[...] += _dot(a_ref[...], b_ref[...], mode)

        @pl.when(k == nk - 1)
        def _():
            epilogue(acc[...], ex, outs)

    return _pcall(
        body_single if nk == 1 else body_acc, name=name, grid=grid, out_shape=out_shape,
        in_specs=[a_spec, b_spec] + list(extra_specs) + [pl.BlockSpec(memory_space=pl.ANY)] * len(after),
        out_specs=out_specs,
        scratch_shapes=[] if nk == 1 else [pltpu.VMEM(acc_shape, F32)],
        compiler_params=_cparams(semantics),
    )(a, b, *extra, *after)


def _ep_store(dtype):
    def ep(acc, ex, outs):
        outs[0][...] = acc.astype(dtype)
    return ep


def _ep_relu2(acc, ex, outs):
    outs[0][...] = acc.astype(BF16)
    r = jnp.maximum(acc, 0.0)
    outs[1][...] = (r * r).astype(BF16)


def _ep_relu2_bwd(acc, ex, outs):
    u = ex[0][...].astype(F32)
    outs[0][...] = (acc * (2.0 * jnp.maximum(u, 0.0))).astype(BF16)


def _tile(n, want):
    t = min(n, want)
    while n % t:
        t //= 2
    return t


def _mm_nn(a, w, out_dtype, name, tm=2048, tn=1024, tk=1024, epilogue=None, out_dtypes=None, after=()):
    M, K = a.shape
    N = w.shape[1]
    tm, tn, tk = _tile(M, tm), _tile(N, tn), _tile(K, tk)
    out_dtypes = out_dtypes or [out_dtype]
    return _mm(a, w, mode="nn", grid=(M // tm, N // tn, K // tk),
               a_spec=pl.BlockSpec((tm, tk), lambda i, j, k: (i, k)),
               b_spec=pl.BlockSpec((tk, tn), lambda i, j, k: (k, j)),
               out_shape=[jax.ShapeDtypeStruct((M, N), d) for d in out_dtypes],
               out_specs=[pl.BlockSpec((tm, tn), lambda i, j, k: (i, j)) for _ in out_dtypes],
               acc_shape=(tm, tn), epilogue=epilogue or _ep_store(out_dtype), name=name, after=after)


def _mm_nn_blocked(a, wg, name, epilogue, out_dtypes, tm=2048):
    M, K = a.shape
    n = wg.shape[2]
    tm = _tile(M, tm)
    return _mm(a, wg, mode="nn", grid=(M // tm, N_DEV, 1),
               a_spec=pl.BlockSpec((tm, K), lambda i, j, k: (i, 0)),
               b_spec=pl.BlockSpec((None, K, n), lambda i, j, k: (j, 0, 0)),
               out_shape=[jax.ShapeDtypeStruct((M, N_DEV * n), d) for d in out_dtypes],
               out_specs=[pl.BlockSpec((tm, n), lambda i, j, k: (i, j)) for _ in out_dtypes],
               acc_shape=(tm, n), epilogue=epilogue, name=name)


def _mm_nt(a, w, out_dtype, name, tm=2048, tn=1024, tk=1024, epilogue=None, extra=(), extra_specs=(),
           after=()):
    M, K = a.shape
    N = w.shape[0]
    tm, tn, tk = _tile(M, tm), _tile(N, tn), _tile(K, tk)
    if extra and not extra_specs:
        extra_specs = [pl.BlockSpec((tm, tn), lambda i, j, k: (i, j)) for _ in extra]
    return _mm(a, w, mode="nt", grid=(M // tm, N // tn, K // tk),
               a_spec=pl.BlockSpec((tm, tk), lambda i, j, k: (i, k)),
               b_spec=pl.BlockSpec((tn, tk), lambda i, j, k: (j, k)),
               out_shape=[jax.ShapeDtypeStruct((M, N), out_dtype)],
               out_specs=[pl.BlockSpec((tm, tn), lambda i, j, k: (i, j))],
               acc_shape=(tm, tn), epilogue=epilogue or _ep_store(out_dtype), name=name,
               extra=extra, extra_specs=extra_specs, after=after)[0]


def _mm_nt_blocked(a, wg, out_dtype, name, tm=1024, after=()):
    M = a.shape[0]
    kout, n = wg.shape[1], wg.shape[2]
    tm = _tile(M, tm)
    per = 4
    return _mm(a, wg, mode="nt_blocks", grid=(M // tm, 1, N_DEV // per),
               a_spec=pl.BlockSpec((tm, per * n), lambda i, j, k: (i, k)),
               b_spec=pl.BlockSpec((per, kout, n), lambda i, j, k: (k, 0, 0)),
               out_shape=[jax.ShapeDtypeStruct((M, kout), out_dtype)],
               out_specs=[pl.BlockSpec((tm, kout), lambda i, j, k: (i, 0))],
               acc_shape=(tm, kout), epilogue=_ep_store(out_dtype), name=name, after=after)[0]


def _mm_tn(a, b, out_dtype, name, tm=1024, tn=1024, tk=2048):
    K, M = a.shape
    N = b.shape[1]
    tm, tn, tk = _tile(M, tm), _tile(N, tn), _tile(K, tk)
    return _mm(a, b, mode="tn", grid=(M // tm, N // tn, K // tk),
               a_spec=pl.BlockSpec((tk, tm), lambda i, j, k: (k, i)),
               b_spec=pl.BlockSpec((tk, tn), lambda i, j, k: (k, j)),
               out_shape=[jax.ShapeDtypeStruct((M, N), out_dtype)],
               out_specs=[pl.BlockSpec((tm, tn), lambda i, j, k: (i, j))],
               acc_shape=(tm, tn), epilogue=_ep_store(out_dtype), name=name)[0]


def _mm_tn_blocked(a, b, out_dtype, name, tm=1024, tk=2048):
    K, M = a.shape
    n = b.shape[1] // N_DEV
    tm, tk = _tile(M, tm), _tile(K, tk)
    return _mm(a, b, mode="tn", grid=(M // tm, N_DEV, K // tk),
               a_spec=pl.BlockSpec((tk, tm), lambda i, j, k: (k, i)),
               b_spec=pl.BlockSpec((tk, n), lambda i, j, k: (k, j)),
               out_shape=[jax.ShapeDtypeStruct((N_DEV, M, n), out_dtype)],
               out_specs=[pl.BlockSpec((None, tm, n), lambda i, j, k: (j, i, 0))],
               acc_shape=(tm, n), epilogue=_ep_store(out_dtype), name=name)[0]


def _window_geometry(ws):
    base = [(ws * k // LANES) * LANES for k in range(N_DEV)]
    off = [ws * k - base[k] for k in range(N_DEV)]
    win = -(-(max(off) + ws) // LANES) * LANES
    return base, off, win


def _shards_to_columns(xg, base, off, win, n_out, name, tr=256):
    R, ws = xg.shape[1], xg.shape[2]
    tr = _tile(R, tr)
    nb_win = win // LANES

    def body(x_ref, o_ref, frame_ref):
        written = set()
        frame_ref[...] = jnp.zeros_like(frame_ref)
        for k in range(N_DEV):
            frame_ref[:, 0:ws] = x_ref[k].astype(F32)
            window = frame_ref[...]
            if off[k]:
                window = pltpu.roll(window, off[k], 1)
            for i in range(nb_win):
                b = base[k] // LANES + i
                if b * LANES >= n_out:
                    continue
                cols = slice(b * LANES, (b + 1) * LANES)
                blk = window[:, i * LANES:(i + 1) * LANES]
                if b in written:
                    blk = blk + o_ref[:, cols].astype(F32)
                o_ref[:, cols] = blk.astype(o_ref.dtype)
                written.add(b)
        for b in range(n_out // LANES):
            if b not in written:
                o_ref[:, b * LANES:(b + 1) * LANES] = jnp.zeros((tr, LANES), o_ref.dtype)

    return _pcall(
        body, name=name, grid=(R // tr,), out_shape=jax.ShapeDtypeStruct((R, n_out), xg.dtype),
        in_specs=[pl.BlockSpec((N_DEV, tr, ws), lambda i: (0, i, 0))],
        out_specs=pl.BlockSpec((tr, n_out), lambda i: (i, 0)),
        scratch_shapes=[pltpu.VMEM((tr, win), F32)],
        compiler_params=_cparams(("parallel",)))(xg)


def _columns_to_shards(x, ws, base, off, win, name, tr=256):
    R = x.shape[0]
    tr = _tile(R, tr)

    def body(x_ref, o_ref, frame_ref):
        for k in range(N_DEV):
            window = x_ref[:, base[k]:base[k] + win].astype(F32)
            if off[k]:
                window = pltpu.roll(window, win - off[k], 1)
            frame_ref[...] = window
            o_ref[k] = frame_ref[:, 0:ws].astype(o_ref.dtype)

    return _pcall(
        body, name=name, grid=(R // tr,), out_shape=jax.ShapeDtypeStruct((N_DEV, R, ws), x.dtype),
        in_specs=[pl.BlockSpec((tr, x.shape[1]), lambda i: (i, 0))],
        out_specs=pl.BlockSpec((N_DEV, tr, ws), lambda i: (0, i, 0)),
        scratch_shapes=[pltpu.VMEM((tr, win), F32)],
        compiler_params=_cparams(("parallel",)))(x)


def _sigmoid(x):
    return 1.0 / (1.0 + jnp.exp(-x))


def _row_spec(tm, d):
    return pl.BlockSpec((tm, d), lambda i: (i, 0))


def _vec_spec(d):
    return pl.BlockSpec((1, d), lambda i: (0, 0))


def _norm_mod_fwd(x, y, gate, nw, scale, shift, name, tm=512):
    L, D = x.shape
    tm = _tile(L, tm)
    has_res = y is not None

    def body(*refs):
        if has_res:
            x_ref, y_ref, g_ref, nw_ref, sc_ref, sh_ref, xo_ref, h_ref = refs
            xn = x_ref[...] + g_ref[...] * y_ref[...]
            xo_ref[...] = xn
        else:
            x_ref, nw_ref, sc_ref, sh_ref, h_ref = refs
            xn = x_ref[...]
        rstd = lax.rsqrt(jnp.mean(xn * xn, axis=-1, keepdims=True) + NORM_EPS)
        h = xn * rstd * nw_ref[...] * (1.0 + sc_ref[...]) + sh_ref[...]
        h_ref[...] = h.astype(BF16)

    row, vec = _row_spec(tm, D), _vec_spec(D)
    if has_res:
        ins, in_specs = (x, y, gate, nw, scale, shift), [row, row, vec, vec, vec, vec]
        out_shape = [jax.ShapeDtypeStruct((L, D), F32), jax.ShapeDtypeStruct((L, D), BF16)]
        out_specs = [row, row]
    else:
        ins, in_specs = (x, nw, scale, shift), [row, vec, vec, vec]
        out_shape = [jax.ShapeDtypeStruct((L, D), BF16)]
        out_specs = [row]
    outs = _pcall(body, name=name, grid=(L // tm,), out_shape=out_shape, in_specs=in_specs,
                  out_specs=out_specs, compiler_params=_cparams(("parallel",)))(*ins)
    return outs if has_res else (x, outs[0])


def _gated_branch_bwd(dx, branch, y_ref, g_ref, dy_ref, dg_ref):
    if branch is None:
        return
    dy_ref[...] = (g_ref[...] * dx).astype(BF16)
    dg_ref[...] += jnp.sum(dx * y_ref[...], axis=0, keepdims=True)


def _norm_mod_bwd(dh, x, nw, scale, dres, name, branch=None, tm=512):
    L, D = x.shape
    tm = _tile(L, tm)
    nb = 0 if branch is None else 2

    def body(dh_ref, x_ref, nw_ref, sc_ref, dres_ref, *rest):
        y_ref, g_ref = rest[:nb] if nb else (None, None)
        dx_ref, dsh_ref, dsc_ref, dnw_ref = rest[nb:nb + 4]
        dy_ref, dg_ref = rest[nb + 4:] if nb else (None, None)

        @pl.when(pl.program_id(0) == 0)
        def _():
            dsh_ref[...] = jnp.zeros_like(dsh_ref)
            dsc_ref[...] = jnp.zeros_like(dsc_ref)
            dnw_ref[...] = jnp.zeros_like(dnw_ref)
            if nb:
                dg_ref[...] = jnp.zeros_like(dg_ref)

        xv = x_ref[...]
        dh_v = dh_ref[...]
        nw_v = nw_ref[...]
        rstd = lax.rsqrt(jnp.mean(xv * xv, axis=-1, keepdims=True) + NORM_EPS)
        xhat = xv * rstd
        dsh_ref[...] += jnp.sum(dh_v, axis=0, keepdims=True)
        dsc_ref[...] += jnp.sum(dh_v * (xhat * nw_v), axis=0, keepdims=True)
        dr = dh_v * (1.0 + sc_ref[...])
        dnw_ref[...] += jnp.sum(dr * xhat, axis=0, keepdims=True)
        dxh = dr * nw_v
        dx = rstd * (dxh - xhat * jnp.mean(dxh * xhat, axis=-1, keepdims=True)) + dres_ref[...]
        dx_ref[...] = dx
        _gated_branch_bwd(dx, branch, y_ref, g_ref, dy_ref, dg_ref)

    row, vec = _row_spec(tm, D), _vec_spec(D)
    extra_in = [] if branch is None else list(branch)
    return _pcall(
        body, name=name, grid=(L // tm,),
        out_shape=[jax.ShapeDtypeStruct((L, D), F32)] + [jax.ShapeDtypeStruct((1, D), F32)] * 3
        + ([jax.ShapeDtypeStruct((L, D), BF16), jax.ShapeDtypeStruct((1, D), F32)] if nb else []),
        in_specs=[row, row, vec, vec, row] + ([row, vec] if nb else []),
        out_specs=[row, vec, vec, vec] + ([row, vec] if nb else []),
        compiler_params=_cparams(("arbitrary",)))(dh, x, nw, scale, dres, *extra_in)


def _final_loss(x, y, gate, fw, target, name, tm=512):
    L, D = x.shape
    tm = _tile(L, tm)

    def body(x_ref, y_ref, g_ref, fw_ref, t_ref, dx_ref, loss_ref, dfw_ref, dy_ref, dg_ref):
        @pl.when(pl.program_id(0) == 0)
        def _():
            loss_ref[...] = jnp.zeros_like(loss_ref)
            dfw_ref[...] = jnp.zeros_like(dfw_ref)
            dg_ref[...] = jnp.zeros_like(dg_ref)

        xn = x_ref[...] + g_ref[...] * y_ref[...]
        fw_v = fw_ref[...]
        rstd = lax.rsqrt(jnp.mean(xn * xn, axis=-1, keepdims=True) + NORM_EPS)
        xhat = xn * rstd
        diff = xhat * fw_v - t_ref[...]
        loss_ref[...] += jnp.sum(diff * diff, axis=0, keepdims=True)
        dyf = diff * (1.0 / D)
        dfw_ref[...] += jnp.sum(dyf * xhat, axis=0, keepdims=True)
        dxh = dyf * fw_v
        dx = rstd * (dxh - xhat * jnp.mean(dxh * xhat, axis=-1, keepdims=True))
        dx_ref[...] = dx
        _gated_branch_bwd(dx, True, y_ref, g_ref, dy_ref, dg_ref)

    row, vec = _row_spec(tm, D), _vec_spec(D)
    return _pcall(
        body, name=name, grid=(L // tm,),
        out_shape=[jax.ShapeDtypeStruct((L, D), F32), jax.ShapeDtypeStruct((1, D), F32),
                   jax.ShapeDtypeStruct((1, D), F32), jax.ShapeDtypeStruct((L, D), BF16),
                   jax.ShapeDtypeStruct((1, D), F32)],
        in_specs=[row, row, vec, vec, row], out_specs=[row, vec, vec, row, vec],
        compiler_params=_cparams(("arbitrary",)))(x, y, gate, fw, target)


def _shift_down(v, s, row):
    if s == 0:
        return v
    return jnp.where(row >= s, pltpu.roll(v, s, 0), 0.0)


CONV_ROWS = 32


def _shifted_rows(x_ref, r0, n, lanes=slice(None)):
    cur = x_ref[r0:r0 + CONV_ROWS, lanes]
    if r0 >= n - 1:
        return [cur] + [x_ref[r0 - s:r0 - s + CONV_ROWS, lanes] for s in range(1, n)]
    row = lax.broadcasted_iota(jnp.int32, cur.shape, 0)
    return [_shift_down(cur, s, row) for s in range(n)]


def _ssd_conv_fwd(zx, w, b, col0, width, name, cb=512):
    L = zx.shape[0]
    nb = width // cb
    off = col0 // cb

    def body(x_ref, w_ref, b_ref, o_ref):
        for l0 in range(0, cb, LANES):
            lanes = slice(l0, l0 + LANES)
            taps = [w_ref[k:k + 1, lanes] for k in range(SSD_K)]
            bias = b_ref[:, lanes]
            for r0 in range(0, L, CONV_ROWS):
                taps_in = _shifted_rows(x_ref, r0, SSD_K, lanes)
                acc = bias + taps[SSD_K - 1] * taps_in[0]
                for s in range(1, SSD_K):
                    acc = acc + taps[SSD_K - 1 - s] * taps_in[s]
                o_ref[r0:r0 + CONV_ROWS, lanes] = acc * _sigmoid(acc)

    return _pcall(
        body, name=name, grid=(nb,), out_shape=jax.ShapeDtypeStruct((L, width), F32),
        in_specs=[pl.BlockSpec((L, cb), lambda j: (0, off + j)),
                  pl.BlockSpec((SSD_K, cb), lambda j: (0, j)),
                  pl.BlockSpec((1, cb), lambda j: (0, j))],
        out_specs=pl.BlockSpec((L, cb), lambda j: (0, j)),
        compiler_params=_cparams(("parallel",)))(zx, w, b)


def _ssd_conv_bwd(zx, w, b, d_parts, dzx, col0, name, cb=128):
    L = zx.shape[0]
    widths = [p.shape[1] for p in d_parts]
    width = sum(widths)
    nb = width // cb
    off = col0 // cb
    starts = [sum(widths[:i]) // cb for i in range(len(d_parts))]
    counts = [wd // cb for wd in widths]

    def body(x_ref, w_ref, b_ref, *rest):
        d_refs = rest[:len(d_parts)]
        dx_ref, dw_ref, db_ref, dpre_ref = rest[len(d_parts) + 1:]
        j = pl.program_id(0)
        taps = [w_ref[k:k + 1, :] for k in range(SSD_K)]
        bias = b_ref[...]
        fold = lambda v: sum(v[r:r + 8, :] for r in range(0, CONV_ROWS, 8))
        db8 = jnp.zeros((8, cb), F32)
        dw8 = [jnp.zeros((8, cb), F32) for _ in range(SSD_K)]
        for r0 in range(0, L, CONV_ROWS):
            rows = slice(r0, r0 + CONV_ROWS)
            d_val = d_refs[-1][rows, :]
            for i in range(len(d_parts) - 2, -1, -1):
                d_val = jnp.where(j < starts[i + 1], d_refs[i][rows, :], d_val)
            taps_in = _shifted_rows(x_ref, r0, SSD_K)
            acc = bias + taps[SSD_K - 1] * taps_in[0]
            for s in range(1, SSD_K):
                acc = acc + taps[SSD_K - 1 - s] * taps_in[s]
            sig = _sigmoid(acc)
            dpre = d_val * (sig * (1.0 + acc * (1.0 - sig)))
            dpre_ref[rows, :] = dpre
            db8 = db8 + fold(dpre)
            for s in range(SSD_K):
                dw8[s] = dw8[s] + fold(dpre * taps_in[s])
        dpre_ref[L:L + 8, :] = jnp.zeros((8, cb), F32)
        db_ref[...] = jnp.sum(db8, axis=0, keepdims=True)
        for s in range(SSD_K):
            dw_ref[SSD_K - 1 - s:SSD_K - s, :] = jnp.sum(dw8[s], axis=0, keepdims=True)
        for r0 in range(0, L, CONV_ROWS):
            dx = taps[SSD_K - 1] * dpre_ref[r0:r0 + CONV_ROWS, :]
            for s in range(1, SSD_K):
                dx = dx + taps[SSD_K - 1 - s] * dpre_ref[r0 + s:r0 + s + CONV_ROWS, :]
            dx_ref[r0:r0 + CONV_ROWS, :] = dx.astype(BF16)

    def part_spec(i):
        return pl.BlockSpec((L, cb), lambda j: (0, jnp.clip(j - starts[i], 0, counts[i] - 1)))

    return _pcall(
        body, name=name, grid=(nb,),
        out_shape=[jax.ShapeDtypeStruct(dzx.shape, BF16), jax.ShapeDtypeStruct((SSD_K, width), F32),
                   jax.ShapeDtypeStruct((1, width), F32)],
        in_specs=[pl.BlockSpec((L, cb), lambda j: (0, off + j)),
                  pl.BlockSpec((SSD_K, cb), lambda j: (0, j)),
                  pl.BlockSpec((1, cb), lambda j: (0, j))]
        + [part_spec(i) for i in range(len(d_parts))] + [pl.BlockSpec(memory_space=pl.ANY)],
        out_specs=[pl.BlockSpec((L, cb), lambda j: (0, off + j)),
                   pl.BlockSpec((SSD_K, cb), lambda j: (0, j)),
                   pl.BlockSpec((1, cb), lambda j: (0, j))],
        input_output_aliases={3 + len(d_parts): 0},
        scratch_shapes=[pltpu.VMEM((L + 8, cb), F32)],
        compiler_params=_cparams(("parallel",)))(zx, w, b, *d_parts, dzx)


def _dzx_finish(dzx, ddt, col0, name, tl=512):
    G, L, _ = ddt.shape
    tail = dzx.shape[1] - col0
    tl = _tile(L, tl)

    def body(ddt_ref, dzx_ref, o_ref):
        s = ddt_ref[0]
        for g in range(1, G):
            s = s + ddt_ref[g]
        o_ref[:, 0:LANES] = s.astype(o_ref.dtype)
        if tail > LANES:
            o_ref[:, LANES:] = jnp.zeros((tl, tail - LANES), o_ref.dtype)

    return _pcall(
        body, name=name, grid=(L // tl,), out_shape=jax.ShapeDtypeStruct(dzx.shape, dzx.dtype),
        in_specs=[pl.BlockSpec((G, tl, LANES), lambda i: (0, i, 0)), pl.BlockSpec(memory_space=pl.ANY)],
        out_specs=pl.BlockSpec((tl, tail), lambda i: (i, col0 // tail)),
        input_output_aliases={1: 0},
        compiler_params=_cparams(("parallel",)))(ddt, dzx)


def _sc_conv_fwd(proj, w, name, cb=512):
    L = proj.shape[0]
    width = proj.shape[1] // 3
    nb = width // cb

    def body(b_ref, c_ref, x_ref, w_ref, o_ref):
        for l0 in range(0, cb, LANES):
            lanes = slice(l0, l0 + LANES)
            taps = [w_ref[k:k + 1, lanes] for k in range(SC_K)]
            for r0 in range(0, L, CONV_ROWS):
                rows = slice(r0, r0 + CONV_ROWS)
                q = [c * x for c, x in zip(_shifted_rows(c_ref, r0, SC_K, lanes),
                                           _shifted_rows(x_ref, r0, SC_K, lanes))]
                acc = taps[SC_K - 1] * q[0]
                for s in range(1, SC_K):
                    acc = acc + taps[SC_K - 1 - s] * q[s]
                o_ref[rows, lanes] = (b_ref[rows, lanes] * acc).astype(BF16)

    return _pcall(
        body, name=name, grid=(nb,), out_shape=jax.ShapeDtypeStruct((L, width), BF16),
        in_specs=[pl.BlockSpec((L, cb), lambda j: (0, j)),
                  pl.BlockSpec((L, cb), lambda j: (0, nb + j)),
                  pl.BlockSpec((L, cb), lambda j: (0, 2 * nb + j)),
                  pl.BlockSpec((SC_K, cb), lambda j: (0, j))],
        out_specs=pl.BlockSpec((L, cb), lambda j: (0, j)),
        compiler_params=_cparams(("parallel",)))(proj, proj, proj, w)


def _sc_conv_bwd(proj, w, dy, name, cb=128):
    L = proj.shape[0]
    width = proj.shape[1] // 3
    nb = width // cb

    def body(b_ref, c_ref, x_ref, w_ref, dy_ref, db_ref, dc_ref, dxv_ref, dw_ref, dconv_ref):
        taps = [w_ref[k:k + 1, :] for k in range(SC_K)]
        fold = lambda v: sum(v[r:r + 8, :] for r in range(0, CONV_ROWS, 8))
        dw8 = [jnp.zeros((8, cb), F32) for _ in range(SC_K)]
        for r0 in range(0, L, CONV_ROWS):
            rows = slice(r0, r0 + CONV_ROWS)
            q = [c * x for c, x in zip(_shifted_rows(c_ref, r0, SC_K), _shifted_rows(x_ref, r0, SC_K))]
            conv = taps[SC_K - 1] * q[0]
            for s in range(1, SC_K):
                conv = conv + taps[SC_K - 1 - s] * q[s]
            dyv = dy_ref[rows, :]
            db_ref[rows, :] = (dyv * conv).astype(BF16)
            dconv = dyv * b_ref[rows, :]
            dconv_ref[rows, :] = dconv
            for s in range(SC_K):
                dw8[s] = dw8[s] + fold(dconv * q[s])
        dconv_ref[L:L + 8, :] = jnp.zeros((8, cb), F32)
        for s in range(SC_K):
            dw_ref[SC_K - 1 - s:SC_K - s, :] = jnp.sum(dw8[s], axis=0, keepdims=True)
        for r0 in range(0, L, CONV_ROWS):
            rows = slice(r0, r0 + CONV_ROWS)
            dq = taps[SC_K - 1] * dconv_ref[rows, :]
            for s in range(1, SC_K):
                dq = dq + taps[SC_K - 1 - s] * dconv_ref[r0 + s:r0 + s + CONV_ROWS, :]
            dc_ref[rows, :] = (dq * x_ref[rows, :]).astype(BF16)
            dxv_ref[rows, :] = (dq * c_ref[rows, :]).astype(BF16)

    blk = pl.BlockSpec((L, cb), lambda j: (0, j))
    wblk = pl.BlockSpec((SC_K, cb), lambda j: (0, j))
    return _pcall(
        body, name=name, grid=(nb,),
        out_shape=[jax.ShapeDtypeStruct((L, width), BF16)] * 3 + [jax.ShapeDtypeStruct((SC_K, width), F32)],
        in_specs=[blk, pl.BlockSpec((L, cb), lambda j: (0, nb + j)),
                  pl.BlockSpec((L, cb), lambda j: (0, 2 * nb + j)), wblk, blk],
        out_specs=[blk, blk, blk, wblk], scratch_shapes=[pltpu.VMEM((L + 8, cb), F32)],
        compiler_params=_cparams(("parallel",)))(proj, proj, proj, w, dy)


def _split3(v):
    hi = v.astype(BF16)
    r1 = v - hi.astype(F32)
    mid = r1.astype(BF16)
    lo = (r1 - mid.astype(F32)).astype(BF16)
    return hi, mid, lo


def _dot_exact01(t01, v):
    hi, mid, lo = _split3(v)
    return _dot(t01, hi) + _dot(t01, mid) + _dot(t01, lo)


def _lane_col(v, lane, h):
    return jnp.sum(jnp.where(lane == h, v, 0.0), axis=1, keepdims=True)


def _sum_all(v):
    return jnp.sum(jnp.sum(v, axis=1, keepdims=True), axis=0, keepdims=True)


def _softplus(x):
    return jnp.maximum(x, 0.0) + jnp.log1p(jnp.exp(-jnp.abs(x)))


def _ssd_decay(zx, bias_p, alog_p, n_heads, dt_block, name):
    L = zx.shape[0]
    nc = L // SSD_CHUNK
    per_step = 4 if nc % 4 == 0 else 1
    rows_step = per_step * SSD_CHUNK

    def body(raw_ref, bias_ref, alog_ref, dt_ref, sg_ref, cs_ref, cst_ref, last_ref):
        lane = lax.broadcasted_iota(jnp.int32, (SSD_CHUNK, LANES), 1)
        row = lax.broadcasted_iota(jnp.int32, (SSD_CHUNK, LANES), 0)
        valid = lane < n_heads
        tri = (row >= lane).astype(BF16)
        a_row = -jnp.exp(alog_ref[...])
        for i in range(per_step):
            rows = slice(i * SSD_CHUNK, (i + 1) * SSD_CHUNK)
            raw = raw_ref[rows, :] + bias_ref[...]
            dt = jnp.where(valid, _softplus(raw), 0.0)
            a = dt * a_row
            cs = _dot_exact01(tri, a)
            dt_ref[rows, :] = dt
            sg_ref[rows, :] = _sigmoid(raw)
            cs_ref[rows, :] = cs
            cst_ref[i] = cs.T
            last_ref[i] = jnp.sum(a, axis=0, keepdims=True)

    blk = pl.BlockSpec((rows_step, LANES), lambda c: (c, 0))
    head_vec = pl.BlockSpec((1, LANES), lambda c: (0, 0))
    return _pcall(
        body, name=name, grid=(nc // per_step,),
        out_shape=[jax.ShapeDtypeStruct((L, LANES), F32)] * 3
        + [jax.ShapeDtypeStruct((nc, SSD_CHUNK, LANES), F32), jax.ShapeDtypeStruct((nc, 1, LANES), F32)],
        in_specs=[pl.BlockSpec((rows_step, LANES), lambda c: (c, dt_block)), head_vec, head_vec],
        out_specs=[blk, blk, blk, pl.BlockSpec((per_step, SSD_CHUNK, LANES), lambda c: (c, 0, 0)),
                   pl.BlockSpec((per_step, 1, LANES), lambda c: (c, 0, 0))],
        compiler_params=_cparams(("parallel",)))(zx, bias_p, alog_p)


def _ssd_common(dt_ref, cs_ref, last_ref, b_ref, c_ref):
    c_sz = SSD_CHUNK
    lane = lax.broadcasted_iota(jnp.int32, (c_sz, LANES), 1)
    row = lax.broadcasted_iota(jnp.int32, (c_sz, LANES), 0)
    bb = b_ref[...].astype(BF16)
    cb = c_ref[...].astype(BF16)
    scores = _dot(cb, bb, "nt")
    return dict(lane=lane, row=row, dt=dt_ref[...], cs=cs_ref[...], last_row=last_ref[...], bb=bb, cb=cb,
                scores=scores, causal=row >= lane, lo=lane < SSD_P)


def _pair_terms(q, cst_ref, h0):
    lane, lo = q["lane"], q["lo"]
    out = {}
    cols, dts, lasts, lms = [], [], [], []
    lane1 = lax.broadcasted_iota(jnp.int32, (1, LANES), 1)
    for h in (h0, h0 + 1):
        col = _lane_col(q["cs"], lane, h)
        rowv = cst_ref[pl.ds(h, 1), :]
        lms.append(jnp.exp(jnp.where(q["causal"], col - rowv, -1e30)))
        cols.append(col)
        dts.append(_lane_col(q["dt"], lane, h))
        lasts.append(jnp.sum(jnp.where(lane1 == h, q["last_row"], 0.0), axis=1, keepdims=True))
    out["lm"] = lms
    out["cols"] = cols
    out["lasts"] = lasts
    out["dt_b"] = jnp.where(lo, dts[0], dts[1])
    out["e_b"] = jnp.where(lo, jnp.exp(cols[0]), jnp.exp(cols[1]))
    out["dec_cols"] = [jnp.exp(lasts[0] - cols[0]), jnp.exp(lasts[1] - cols[1])]
    out["dec_b"] = jnp.where(lo, out["dec_cols"][0], out["dec_cols"][1])
    lo1 = lane1 < SSD_P
    out["explast"] = [jnp.exp(lasts[0]), jnp.exp(lasts[1])]
    out["explast_b"] = jnp.where(lo1, out["explast"][0], out["explast"][1])
    return out


def _ssd_fwd(zx, xc, decay, d_lane, nw, d_inner, after, name):
    L = zx.shape[0]
    nc = L // SSD_CHUNK
    gw = d_inner // SSD_G
    heads = gw // SSD_P
    n_pair = heads // 2
    bc0 = d_inner // LANES

    def body(z_ref, xs_ref, b_ref, c_ref, dt_ref, cs_ref, cst_ref, last_ref, dl_ref, nw_ref, after_ref,
             y_ref, yn_ref, prev_ref, s_ref):
        @pl.when(pl.program_id(1) == 0)
        def _():
            s_ref[...] = jnp.zeros_like(s_ref)

        q = _ssd_common(dt_ref, cs_ref, last_ref, b_ref, c_ref)
        prev_ref[...] = s_ref[...]
        lo = q["lo"]
        for j in range(n_pair):
            sl = slice(j * LANES, (j + 1) * LANES)
            p = _pair_terms(q, cst_ref, pl.program_id(0) * heads + 2 * j)
            xs_p = xs_ref[:, sl]
            xp = xs_p * p["dt_b"]
            xb = xp.astype(BF16)
            m_a = (q["scores"] * p["lm"][0]).astype(BF16)
            m_b = (q["scores"] * p["lm"][1]).astype(BF16)
            yd = jnp.where(lo, _dot(m_a, xb), _dot(m_b, xb))
            s_p = s_ref[:, sl]
            yo = _dot(q["cb"], s_p.astype(BF16)) * p["e_b"]
            y_ref[:, sl] = yd + yo + dl_ref[:, sl] * xs_p
            st = _dot(q["bb"], (xp * p["dec_b"]).astype(BF16), "tn")
            s_ref[:, sl] = s_p * p["explast_b"] + st
        yv = y_ref[...]
        zv = z_ref[...]
        yg = yv * (zv * _sigmoid(zv))
        rstd = lax.rsqrt(jnp.mean(yg * yg, axis=-1, keepdims=True) + NORM_EPS)
        yn_ref[...] = (yg * rstd * nw_ref[...]).astype(BF16)

    grp = lambda width: pl.BlockSpec((None, 1, width), lambda g, c: (g, 0, 0))
    dt_, _, cs_, cst_, last_ = decay
    return _pcall(
        body, name=name, grid=(SSD_G, nc),
        out_shape=[jax.ShapeDtypeStruct((L, d_inner), F32), jax.ShapeDtypeStruct((L, d_inner), BF16),
                   jax.ShapeDtypeStruct((nc, SSD_G, SSD_N, gw), F32)],
        in_specs=[pl.BlockSpec((SSD_CHUNK, gw), lambda g, c: (c, g)),
                  pl.BlockSpec((SSD_CHUNK, gw), lambda g, c: (c, g)),
                  pl.BlockSpec((SSD_CHUNK, SSD_N), lambda g, c: (c, bc0 + g)),
                  pl.BlockSpec((SSD_CHUNK, SSD_N), lambda g, c: (c, bc0 + SSD_G + g)),
                  pl.BlockSpec((SSD_CHUNK, LANES), lambda g, c: (c, 0)),
                  pl.BlockSpec((SSD_CHUNK, LANES), lambda g, c: (c, 0)),
                  pl.BlockSpec((None, SSD_CHUNK, LANES), lambda g, c: (c, 0, 0)),
                  pl.BlockSpec((None, 1, LANES), lambda g, c: (c, 0, 0)),
                  grp(gw), grp(gw), pl.BlockSpec(memory_space=pl.ANY)],
        out_specs=[pl.BlockSpec((SSD_CHUNK, gw), lambda g, c: (c, g)),
                   pl.BlockSpec((SSD_CHUNK, gw), lambda g, c: (c, g)),
                   pl.BlockSpec((None, None, SSD_N, gw), lambda g, c: (c, g, 0, 0))],
        scratch_shapes=[pltpu.VMEM((SSD_N, gw), F32)],
        compiler_params=_cparams(("parallel", "arbitrary")))(
            zx, xc, xc, xc, dt_, cs_, cst_, last_, d_lane, nw, after)


def _ssd_bwd(dyn, y, zx, xc, prev, decay, alog_p, d_lane, nw, d_inner, name):
    L = zx.shape[0]
    nc = L // SSD_CHUNK
    gw = d_inner // SSD_G
    heads = gw // SSD_P
    n_pair = heads // 2
    bc0 = d_inner // LANES

    def body(dyn_ref, y_ref, z_ref, xs_ref, b_ref, c_ref, prev_ref, dt_ref, sg_ref, cs_ref, cst_ref, last_ref,
             alog_ref, dl_ref, nw_ref,
             dz_ref, dxs_ref, db_ref, dc_ref, ddt_ref, dbias_ref, dalog_ref, dd_ref, dnw_ref,
             ds_ref, racc_ref):
        @pl.when(pl.program_id(1) == 0)
        def _():
            ds_ref[...] = jnp.zeros_like(ds_ref)
            dbias_ref[...] = jnp.zeros_like(dbias_ref)
            dalog_ref[...] = jnp.zeros_like(dalog_ref)
            dd_ref[...] = jnp.zeros_like(dd_ref)
            dnw_ref[...] = jnp.zeros_like(dnw_ref)

        q = _ssd_common(dt_ref, cs_ref, last_ref, b_ref, c_ref)
        a_row = -jnp.exp(alog_ref[...])
        lane, row, lo = q["lane"], q["row"], q["lo"]
        lane1 = lax.broadcasted_iota(jnp.int32, (1, LANES), 1)
        head0 = pl.program_id(0) * heads
        mine = (lane >= head0) & (lane < head0 + heads)

        yv, zv, dynv, nwv = y_ref[...], z_ref[...], dyn_ref[...], nw_ref[...]
        sig = _sigmoid(zv)
        sz = zv * sig
        yg = yv * sz
        rstd = lax.rsqrt(jnp.mean(yg * yg, axis=-1, keepdims=True) + NORM_EPS)
        yhat = yg * rstd
        dnw_ref[...] += jnp.sum(dynv * yhat, axis=0, keepdims=True)
        dyh = dynv * nwv
        dyg = rstd * (dyh - yhat * jnp.mean(dyh * yhat, axis=-1, keepdims=True))
        dz_ref[...] = (dyg * yv * (sig * (1.0 + zv * (1.0 - sig)))).astype(BF16)
        dy_all = dyg * sz

        dg = jnp.zeros((SSD_CHUNK, SSD_CHUNK), F32)
        dc_acc = jnp.zeros((SSD_CHUNK, SSD_N), F32)
        db_acc = jnp.zeros((SSD_CHUNK, SSD_N), F32)
        dcs_mat = jnp.zeros((SSD_CHUNK, LANES), F32)
        ddt_mat = jnp.zeros((SSD_CHUNK, LANES), F32)
        dd_row = jnp.zeros((1, LANES), F32)
        racc_ref[...] = jnp.zeros_like(racc_ref)
        is_last = row == SSD_CHUNK - 1

        for j in range(n_pair):
            sl = slice(j * LANES, (j + 1) * LANES)
            ha, hb = head0 + 2 * j, head0 + 2 * j + 1
            p = _pair_terms(q, cst_ref, ha)
            xs_p = xs_ref[:, sl]
            dyp = dy_all[:, sl]
            xp = xs_p * p["dt_b"]
            xb = xp.astype(BF16)
            s_p = prev_ref[:, sl]
            s_pb = s_p.astype(BF16)
            dsn = ds_ref[:, sl]
            dsnb = dsn.astype(BF16)
            m_f = [q["scores"] * p["lm"][0], q["scores"] * p["lm"][1]]

            t0 = dyp * xs_p
            dd_row = dd_row + jnp.where(lane1 == ha, _sum_all(jnp.where(lo, t0, 0.0)), 0.0) \
                + jnp.where(lane1 == hb, _sum_all(jnp.where(lo, 0.0, t0)), 0.0)
            dxs_p = dl_ref[:, sl] * dyp

            yo = _dot(q["cb"], s_pb) * p["e_b"]
            dcs_b = (dyp * p["e_b"]).astype(BF16)
            dc_acc = dc_acc + _dot(dcs_b, s_pb, "nt")
            ds_yo = _dot(q["cb"], dcs_b, "tn")
            t1 = dyp * yo
            dcs_cols = [jnp.sum(jnp.where(lo, t1, 0.0), axis=1, keepdims=True),
                        jnp.sum(jnp.where(lo, 0.0, t1), axis=1, keepdims=True)]

            t2 = dsn * s_p
            dlast = [p["explast"][0] * _sum_all(jnp.where(lo, t2, 0.0)),
                     p["explast"][1] * _sum_all(jnp.where(lo, 0.0, t2))]
            ds_ref[:, sl] = dsn * p["explast_b"] + ds_yo
            w = _dot(q["bb"], dsnb)
            db_acc = db_acc + _dot((xp * p["dec_b"]).astype(BF16), dsnb, "nt")
            dxp = w * p["dec_b"]
            t3 = w * xp
            e = [jnp.sum(jnp.where(lo, t3, 0.0), axis=1, keepdims=True) * p["dec_cols"][0],
                 jnp.sum(jnp.where(lo, 0.0, t3), axis=1, keepdims=True) * p["dec_cols"][1]]
            for i in range(2):
                dlast[i] = dlast[i] + jnp.sum(e[i], axis=0, keepdims=True)
                dcs_cols[i] = dcs_cols[i] - e[i]

            dyb = dyp.astype(BF16)
            dy_h = [jnp.where(lo, dyp, 0.0).astype(BF16), jnp.where(lo, 0.0, dyp).astype(BF16)]
            dms = [_dot(dy_h[0], xb, "nt"), _dot(dy_h[1], xb, "nt")]
            dxp = dxp + jnp.where(lo, _dot(m_f[0].astype(BF16), dyb, "tn"), _dot(m_f[1].astype(BF16), dyb, "tn"))
            for i, h in enumerate((ha, hb)):
                dg = dg + dms[i] * p["lm"][i]
                qm = dms[i] * m_f[i]
                dcs_cols[i] = dcs_cols[i] + jnp.sum(qm, axis=1, keepdims=True)
                racc_ref[pl.ds(h, 1), :] = jnp.sum(qm, axis=0, keepdims=True)

            dxs_ref[:, sl] = dxs_p + dxp * p["dt_b"]
            t4 = dxp * xs_p
            ddt_cols = [jnp.sum(jnp.where(lo, t4, 0.0), axis=1, keepdims=True),
                        jnp.sum(jnp.where(lo, 0.0, t4), axis=1, keepdims=True)]
            for i, h in enumerate((ha, hb)):
                sel = lane == h
                dcs_mat = dcs_mat + jnp.where(sel, dcs_cols[i], 0.0) + jnp.where(sel & is_last, dlast[i], 0.0)
                ddt_mat = ddt_mat + jnp.where(sel, ddt_cols[i], 0.0)

        dcs_mat = dcs_mat - racc_ref[...].T
        tri_t = (row <= lane).astype(BF16)
        da = _dot_exact01(tri_t, dcs_mat)
        ddt = ddt_mat + da * a_row
        dalog_ref[...] += jnp.sum(jnp.where(mine, da * q["dt"], 0.0), axis=0, keepdims=True) * a_row
        draw = jnp.where(mine, ddt * sg_ref[...], 0.0)
        ddt_ref[...] = draw
        dbias_ref[...] += jnp.sum(draw, axis=0, keepdims=True)
        dd_ref[...] += dd_row
        dgb = dg.astype(BF16)
        dc_ref[...] = dc_acc + _dot(dgb, q["bb"])
        db_ref[...] = db_acc + _dot(dgb, q["cb"], "tn")

    rev = lambda c: nc - 1 - c
    grp = lambda width: pl.BlockSpec((None, 1, width), lambda g, c: (g, 0, 0))
    blk = lambda width, off: pl.BlockSpec((SSD_CHUNK, width), lambda g, c: (rev(c), off + g))
    head_vec = pl.BlockSpec((1, LANES), lambda g, c: (0, 0))
    chunk_rows = pl.BlockSpec((SSD_CHUNK, LANES), lambda g, c: (rev(c), 0))
    dt_, sg_, cs_, cst_, last_ = decay
    return _pcall(
        body, name=name, grid=(SSD_G, nc),
        out_shape=[jax.ShapeDtypeStruct(zx.shape, BF16), jax.ShapeDtypeStruct((L, d_inner), F32),
                   jax.ShapeDtypeStruct((L, SSD_G * SSD_N), F32), jax.ShapeDtypeStruct((L, SSD_G * SSD_N), F32),
                   jax.ShapeDtypeStruct((SSD_G, L, LANES), F32),
                   jax.ShapeDtypeStruct((SSD_G, 1, LANES), F32), jax.ShapeDtypeStruct((SSD_G, 1, LANES), F32),
                   jax.ShapeDtypeStruct((SSD_G, 1, LANES), F32), jax.ShapeDtypeStruct((SSD_G, 1, gw), F32)],
        in_specs=[blk(gw, 0), blk(gw, 0), blk(gw, 0), blk(gw, 0), blk(SSD_N, bc0), blk(SSD_N, bc0 + SSD_G),
                  pl.BlockSpec((None, None, SSD_N, gw), lambda g, c: (rev(c), g, 0, 0)),
                  chunk_rows, chunk_rows, chunk_rows,
                  pl.BlockSpec((None, SSD_CHUNK, LANES), lambda g, c: (rev(c), 0, 0)),
                  pl.BlockSpec((None, 1, LANES), lambda g, c: (rev(c), 0, 0)),
                  head_vec, grp(gw), grp(gw)],
        out_specs=[blk(gw, 0), blk(gw, 0), blk(SSD_N, 0), blk(SSD_N, 0),
                   pl.BlockSpec((None, SSD_CHUNK, LANES), lambda g, c: (g, rev(c), 0)),
                   grp(LANES), grp(LANES), grp(LANES), grp(gw)],
        scratch_shapes=[pltpu.VMEM((SSD_N, gw), F32), pltpu.VMEM((SSD_CHUNK, LANES), F32)],
        compiler_params=_cparams(("parallel", "arbitrary")))(
            dyn, y, zx, xc, xc, xc, prev, dt_, sg_, cs_, cst_, last_, alog_p, d_lane, nw)


def _cond_mod(c_pad, ada_w, ada_b_loc, after, name):
    depth, D, n = ada_w.shape
    rows = c_pad.shape[0]

    def body(c_ref, w_ref, b_ref, after_ref, mod_ref, cond_ref):
        cv = c_ref[...]
        cond = cv * _sigmoid(cv)
        cond_ref[...] = cond
        mod_ref[...] = _dot(cond.astype(BF16), w_ref[...].astype(BF16)) + b_ref[...]

    return _pcall(
        body, name=name, grid=(depth,),
        out_shape=[jax.ShapeDtypeStruct((depth, rows, n), F32), jax.ShapeDtypeStruct((rows, D), F32)],
        in_specs=[pl.BlockSpec((rows, D), lambda i: (0, 0)),
                  pl.BlockSpec((None, D, n), lambda i: (i, 0, 0)),
                  pl.BlockSpec((None, 1, n), lambda i: (i, 0, 0)),
                  pl.BlockSpec(memory_space=pl.ANY)],
        out_specs=[pl.BlockSpec((None, rows, n), lambda i: (i, 0, 0)),
                   pl.BlockSpec((rows, D), lambda i: (0, 0))],
        compiler_params=_cparams(("arbitrary",)))(c_pad, ada_w, ada_b_loc, after)


def _adamw_math(g, w, m, v):
    m_new = ADAM_B1 * m + (1.0 - ADAM_B1) * g
    v_new = ADAM_B2 * v + (1.0 - ADAM_B2) * (g * g)
    m_hat = m_new / (1.0 - ADAM_B1 ** ADAM_STEP)
    v_hat = v_new / (1.0 - ADAM_B2 ** ADAM_STEP)
    delta = -ADAM_LR * (m_hat / (jnp.sqrt(v_hat) + ADAM_EPS) + ADAM_WD * w)
    return delta, m_new, v_new


def _adamw_sum(parts, w, m, v, layer, name, prev=None, tr=None):
    depth, R, C = w.shape
    tr = _tile(R, tr if tr is not None else (512 if C <= 512 else 256))

    def body(p_ref, w_ref, m_ref, v_ref, *rest):
        g_ref, d_ref, mo_ref, vo_ref = rest[-4:]
        g = p_ref[0].astype(F32)
        for k in range(1, N_DEV):
            g = g + p_ref[k].astype(F32)
        d, mn, vn = _adamw_math(g, w_ref[...], m_ref[...], v_ref[...])
        g_ref[...] = g
        d_ref[...] = d
        mo_ref[...] = mn
        vo_ref[...] = vn

    blk = pl.BlockSpec((None, tr, C), lambda i: (layer, i, 0))
    prev = list(prev) if prev is not None else []
    return _pcall(
        body, name=name, grid=(R // tr,),
        out_shape=[jax.ShapeDtypeStruct((depth, R, C), F32)] * 4,
        in_specs=[pl.BlockSpec((N_DEV, tr, C), lambda i: (0, i, 0)), blk, blk, blk]
        + [pl.BlockSpec(memory_space=pl.ANY)] * len(prev),
        out_specs=[blk] * 4, input_output_aliases={4 + k: k for k in range(len(prev))},
        compiler_params=_cparams(("parallel",)))(parts, w, m, v, *prev)


def _adamw_small(parts, wmv, head_parts, head_wmv, loss_parts, name):
    n, nh = len(parts), len(head_parts)
    n_heads = head_wmv[0][0].shape[1] if nh else 0
    groups = head_parts[0].shape[1] if nh else 0
    d_model = loss_parts.shape[2]

    def body(*refs):
        p_refs, refs = refs[:n], refs[n:]
        wmv_refs, refs = refs[:3 * n], refs[3 * n:]
        hp_refs, refs = refs[:nh], refs[nh:]
        hwmv_refs, refs = refs[:3 * nh], refs[3 * nh:]
        loss_ref, refs = refs[0], refs[1:]
        outs, loss_out, head_scr = refs[:4 * (n + nh)], refs[4 * (n + nh)], refs[4 * (n + nh) + 1]

        def update(i, g, w_ref, m_ref, v_ref):
            res = (g,) + _adamw_math(g, w_ref[...], m_ref[...], v_ref[...])
            for o_ref, r in zip(outs[4 * i:4 * i + 4], res):
                o_ref[...] = r

        for i in range(n):
            g = p_refs[i][0]
            for k in range(1, N_DEV):
                g = g + p_refs[i][k]
            update(i, g, *wmv_refs[3 * i:3 * i + 3])
        for i in range(nh):
            g = None
            for k in range(N_DEV):
                for grp in range(groups):
                    g = hp_refs[i][k, grp] if g is None else g + hp_refs[i][k, grp]
            head_scr[...] = g
            update(n + i, head_scr[:, 0:n_heads], *hwmv_refs[3 * i:3 * i + 3])
        tot = loss_ref[0]
        for k in range(1, N_DEV):
            tot = tot + loss_ref[k]
        loss_out[...] = jnp.broadcast_to(_sum_all(tot) * (0.5 / d_model), loss_out.shape)

    operands = list(parts) + [a for t in wmv for a in t] + list(head_parts) + [a for t in head_wmv for a in t]
    operands.append(loss_parts)
    out_shape = [jax.ShapeDtypeStruct(t[0].shape, F32) for t in list(wmv) + list(head_wmv) for _ in range(4)]
    out_shape.append(jax.ShapeDtypeStruct((1, LANES), F32))
    vmem = pl.BlockSpec(memory_space=pltpu.VMEM)
    outs = _pcall(body, name=name, out_shape=out_shape, in_specs=[vmem] * len(operands),
                  out_specs=[vmem] * len(out_shape), scratch_shapes=[pltpu.VMEM((1, LANES), F32)],
                  compiler_params=_cparams())(*operands)
    return [outs[4 * i:4 * i + 4] for i in range(n + nh)], outs[-1]


def _ada_adamw(cond_pad, dmod_pad, w, m, v, name, tr=512):
    depth, D, n = w.shape
    rows = cond_pad.shape[0]
    tr = _tile(D, tr)

    def body(c_ref, dm_ref, w_ref, m_ref, v_ref, g_ref, d_ref, mo_ref, vo_ref):
        g = _dot(c_ref[...].astype(BF16), dm_ref[...].astype(BF16), "tn")
        d, mn, vn = _adamw_math(g, w_ref[...], m_ref[...], v_ref[...])
        g_ref[...] = g
        d_ref[...] = d
        mo_ref[...] = mn
        vo_ref[...] = vn

    blk = pl.BlockSpec((None, tr, n), lambda i, r: (i, r, 0))
    return _pcall(
        body, name=name, grid=(depth, D // tr),
        out_shape=[jax.ShapeDtypeStruct((depth, D, n), F32)] * 4,
        in_specs=[pl.BlockSpec((rows, tr), lambda i, r: (0, r)),
                  pl.BlockSpec((None, rows, n), lambda i, r: (i, 0, 0)), blk, blk, blk],
        out_specs=[blk] * 4, compiler_params=_cparams(("parallel", "parallel")))(cond_pad, dmod_pad, w, m, v)


def kernel(x, c, ada_w, ada_b, mix_norm_w, mlp_norm_w, mlp_up, mlp_down, ssd_in_w, ssd_conv_w, ssd_conv_b, ssd_dt_bias, ssd_A_log, ssd_D, ssd_norm_w, ssd_out_w, sc_in_w, sc_conv_w, sc_out_w, final_norm_w, loss_target, m_ada_w, m_ada_b, m_mix_norm_w, m_mlp_norm_w, m_mlp_up, m_mlp_down, m_ssd_in_w, m_ssd_conv_w, m_ssd_conv_b, m_ssd_dt_bias, m_ssd_A_log, m_ssd_D, m_ssd_norm_w, m_ssd_out_w, m_sc_in_w, m_sc_conv_w, m_sc_out_w, m_final_norm_w, v_ada_w, v_ada_b, v_mix_norm_w, v_mlp_norm_w, v_mlp_up, v_mlp_down, v_ssd_in_w, v_ssd_conv_w, v_ssd_conv_b, v_ssd_dt_bias, v_ssd_A_log, v_ssd_D, v_ssd_norm_w, v_ssd_out_w, v_sc_in_w, v_sc_conv_w, v_sc_out_w, v_final_norm_w):
    weights = dict(ada_w=ada_w, ada_b=ada_b, mix_norm_w=mix_norm_w, mlp_norm_w=mlp_norm_w, mlp_up=mlp_up,
                   mlp_down=mlp_down, ssd_in_w=ssd_in_w, ssd_conv_w=ssd_conv_w, ssd_conv_b=ssd_conv_b,
                   ssd_dt_bias=ssd_dt_bias, ssd_A_log=ssd_A_log, ssd_D=ssd_D, ssd_norm_w=ssd_norm_w,
                   ssd_out_w=ssd_out_w, sc_in_w=sc_in_w, sc_conv_w=sc_conv_w, sc_out_w=sc_out_w,
                   final_norm_w=final_norm_w)
    moms = dict(ada_w=m_ada_w, ada_b=m_ada_b, mix_norm_w=m_mix_norm_w, mlp_norm_w=m_mlp_norm_w, mlp_up=m_mlp_up,
                mlp_down=m_mlp_down, ssd_in_w=m_ssd_in_w, ssd_conv_w=m_ssd_conv_w, ssd_conv_b=m_ssd_conv_b,
                ssd_dt_bias=m_ssd_dt_bias, ssd_A_log=m_ssd_A_log, ssd_D=m_ssd_D, ssd_norm_w=m_ssd_norm_w,
                ssd_out_w=m_ssd_out_w, sc_in_w=m_sc_in_w, sc_conv_w=m_sc_conv_w, sc_out_w=m_sc_out_w,
                final_norm_w=m_final_norm_w)
    vars_ = dict(ada_w=v_ada_w, ada_b=v_ada_b, mix_norm_w=v_mix_norm_w, mlp_norm_w=v_mlp_norm_w, mlp_up=v_mlp_up,
                 mlp_down=v_mlp_down, ssd_in_w=v_ssd_in_w, ssd_conv_w=v_ssd_conv_w, ssd_conv_b=v_ssd_conv_b,
                 ssd_dt_bias=v_ssd_dt_bias, ssd_A_log=v_ssd_A_log, ssd_D=v_ssd_D, ssd_norm_w=v_ssd_norm_w,
                 ssd_out_w=v_ssd_out_w, sc_in_w=v_sc_in_w, sc_conv_w=v_sc_conv_w, sc_out_w=v_sc_out_w,
                 final_norm_w=v_final_norm_w)
    names = list(weights)

    L, D = x.shape[1], x.shape[2]
    d_inner = 2 * D
    n_heads = d_inner // SSD_P
    hpg = n_heads // SSD_G
    gw = d_inner // SSD_G
    conv_dim = d_inner + 2 * SSD_G * SSD_N
    zx_dim = d_inner + conv_dim
    zx_pad = -(-(zx_dim + LANES) // 512) * 512
    in_ws = ssd_in_w.shape[2]
    in_base, in_off, in_win = _window_geometry(in_ws)
    me = _my_index()
    x0 = x[0]
    tgt = loss_target[0]

    n_mod = ada_w.shape[2]
    (c_all,) = _exchange([c], "gather_c", gather=True)
    gather_handle = {}
    (gather_handle["ssd_in_w"],), token_in = _xfer_start(
        [ssd_in_w[0].astype(BF16)], "gather_start_ssd_in_w", gather=True, via_sibling=(0,), after=(c_all,))
    c_pad = jnp.pad(c_all.reshape(N_DEV, D), ((0, 16 - N_DEV), (0, 0)))
    ada_b_loc = lax.dynamic_slice_in_dim(ada_b, me * n_mod, n_mod, axis=1).reshape(2, 1, n_mod)
    mod_blk, cond_pad = _cond_mod(c_pad, ada_w, ada_b_loc, token_in, "cond_mod")
    gather_order = ["mod", "ssd_conv_w", "sc_conv_w", "ssd_out_w", "up0", "down0", "sc_in_w", "sc_out_w", "up1",
                    "down1"]
    gather_src = dict(mod=mod_blk, ssd_conv_w=ssd_conv_w[0], sc_conv_w=sc_conv_w[0],
                      ssd_out_w=ssd_out_w[0].astype(BF16),
                      up0=mlp_up[0].astype(BF16), down0=mlp_down[0].astype(BF16),
                      sc_in_w=sc_in_w[0].astype(BF16), sc_out_w=sc_out_w[0].astype(BF16),
                      up1=mlp_up[1].astype(BF16), down1=mlp_down[1].astype(BF16))
    handles, gather_token = _xfer_start([gather_src[k] for k in gather_order], "gather_start", gather=True,
                                        via_sibling=tuple(range(3, len(gather_order))))
    gather_handle.update(zip(gather_order, handles))

    def gathered(keys, after, forward):
        tag = "_".join(keys)
        lands = _xfer_wait([gather_handle[k] for k in keys], after, f"gather_wait_{tag}", gather=True)
        return _sibling_forward(lands, f"gather_forward_{tag}") if forward else lands

    def forward_behind(keys, after):
        tag = "_".join(keys)
        lands = _xfer_wait([gather_handle[k] for k in keys], after, f"gather_wait_{tag}", gather=True)
        fwd_handles, token = _sibling_forward_start(lands, f"gather_forward_start_{tag}")
        return (lambda done: _sibling_forward_wait(fwd_handles, done, f"gather_forward_wait_{tag}")), token

    (ssd_in_g,) = gathered(["ssd_in_w"], (gather_token, m_ssd_in_w, v_ssd_in_w), True)
    w_in_all = _shards_to_columns(ssd_in_g, in_base, in_off, in_win, zx_pad, "ssd_in_w_columns")
    (mod_all,) = gathered(["mod"], w_in_all, False)
    mod_mine = lax.dynamic_index_in_dim(mod_all, me, axis=2, keepdims=False)
    mod_mine = jnp.transpose(mod_mine, (1, 0, 2)).reshape(2, 6, 1, D)
    sh_m, sc_m, g_m, sh_f, sc_f, g_f = [[mod_mine[i, k] for i in range(2)] for k in range(6)]

    vec = lambda a: a.reshape(1, -1)
    small = {}

    _, h0 = _norm_mod_fwd(x0, None, None, vec(mix_norm_w[0]), sc_m[0], sh_m[0], "l0_mix_norm")
    cw_all, scw_all = gathered(["ssd_conv_w", "sc_conv_w"], h0, False)
    (zx,) = _mm_nn(h0, w_in_all, F32, "ssd_in_proj", tm=2048, tn=512)
    conv_b0 = vec(ssd_conv_b[0])
    conv_w_full = jnp.transpose(cw_all, (1, 0, 2)).reshape(SSD_K, conv_dim)
    sc_conv_full = jnp.transpose(scw_all, (1, 0, 2)).reshape(SC_K, D)
    xc = _ssd_conv_fwd(zx, conv_w_full, conv_b0, d_inner, conv_dim, "ssd_conv")
    bias_p = jnp.pad(ssd_dt_bias[0], (0, LANES - n_heads)).reshape(1, LANES)
    alog_p = jnp.pad(ssd_A_log[0], (0, LANES - n_heads)).reshape(1, LANES)
    d_lane = jnp.repeat(ssd_D[0], SSD_P).reshape(SSD_G, 1, gw)
    nw_g = ssd_norm_w[0].reshape(SSD_G, 1, gw)
    finish, token = forward_behind(["ssd_out_w"], xc)
    decay = _ssd_decay(zx, bias_p, alog_p, n_heads, zx_dim // LANES, "ssd_decay")
    y_ssd, yn, prev = _ssd_fwd(zx, xc, decay, d_lane, nw_g, d_inner, token, "ssd_scan")
    ups, downs = [None, None], [None, None]
    (ssd_out_g,) = finish(yn)
    w_ssd_out = ssd_out_g.reshape(-1, D)
    finish, token = forward_behind(["up0", "down0"], ssd_out_g)
    (mix0,) = _mm_nn(yn, w_ssd_out, F32, "ssd_out_proj", tm=1024, tk=2048, after=(token,))
    x1, h1 = _norm_mod_fwd(x0, mix0, g_m[0], vec(mlp_norm_w[0]), sc_f[0], sh_f[0], "l0_mlp_norm")
    ups[0], down0_g = finish(h1)
    downs[0] = down0_g.reshape(-1, D)
    u0, s0 = _mm_nn_blocked(h1, ups[0], "l0_mlp_up", _ep_relu2, [BF16, BF16])
    finish, token = forward_behind(["sc_in_w", "sc_out_w", "up1", "down1"], s0)
    (d0,) = _mm_nn(s0, downs[0], F32, "l0_mlp_down", tm=1024, tk=2048, after=(token,))
    x2, h2 = _norm_mod_fwd(x1, d0, g_f[0], vec(mix_norm_w[1]), sc_m[1], sh_m[1], "l1_mix_norm")
    sc_in_g, sc_out_g, ups[1], down1_g = finish(h2)
    w_sc_out, downs[1] = sc_out_g.reshape(-1, D), down1_g.reshape(-1, D)
    (proj,) = _mm_nn_blocked(h2, sc_in_g, "sc_in_proj", _ep_store(F32), [F32])
    yc = _sc_conv_fwd(proj, sc_conv_full, "sc_conv")
    (mix1,) = _mm_nn(yc, w_sc_out, F32, "sc_out_proj", tm=1024)
    x3, h3 = _norm_mod_fwd(x2, mix1, g_m[1], vec(mlp_norm_w[1]), sc_f[1], sh_f[1], "l1_mlp_norm")
    u1, s1 = _mm_nn_blocked(h3, ups[1], "l1_mlp_up", _ep_relu2, [BF16, BF16])
    (d1,) = _mm_nn(s1, downs[1], F32, "l1_mlp_down", tm=1024, tk=2048)

    dx, loss_lane, dfw, dd1, dg = _final_loss(x3, d1, g_f[1], vec(final_norm_w), tgt, "final_loss")
    small["final_norm_w"] = dfw

    dmod = [[None] * 6 for _ in range(2)]
    dmod[1][5] = dg

    def mlp_backward(i, dx_out, dd, x_mid, h_in, u, s, mix, gate):
        du = _mm_nt(dd, downs[i], BF16, f"l{i}_mlp_down_bwd", tm=1024, epilogue=_ep_relu2_bwd, extra=(u,))
        gdown = _mm_tn(s, dd, BF16, f"l{i}_mlp_down_wgrad").reshape(N_DEV, -1, D)
        gup = _mm_tn_blocked(h_in, du, BF16, f"l{i}_mlp_up_wgrad")
        (h_down, h_up), token = _xfer_start([gdown, gup], f"l{i}_mlp_grads_start", gather=False)
        grad_handle[f"mlp_down{i}"], grad_handle[f"mlp_up{i}"] = h_down, h_up
        dh = _mm_nt_blocked(du, ups[i], F32, f"l{i}_mlp_up_bwd", after=(token,))
        dxm, dsh, dsc, dnw, dmix, dgate = _norm_mod_bwd(dh, x_mid, vec(mlp_norm_w[i]), sc_f[i], dx_out,
                                                        f"l{i}_mlp_norm_bwd", branch=(mix, gate))
        dmod[i][3], dmod[i][4], dmod[i][2] = dsh, dsc, dgate
        return dxm, dmix, dnw

    grad_handle = {}
    dx3, dyc, dnw_mlp1 = mlp_backward(1, dx, dd1, x3, h3, u1, s1, mix1, g_m[1])
    g_sc_out = _mm_tn(yc, dyc, BF16, "sc_out_wgrad").reshape(N_DEV, -1, D)
    dconv_out = _mm_nt(dyc, w_sc_out, F32, "sc_out_bwd", tm=1024)
    dbg, dcg, dxv, dscw = _sc_conv_bwd(proj, sc_conv_full, dconv_out, "sc_conv_bwd")
    dproj = jnp.concatenate([dbg, dcg, dxv], axis=1)
    g_sc_in = _mm_tn_blocked(h2, dproj, BF16, "sc_in_wgrad")
    (grad_handle["sc_out_w0"], grad_handle["sc_in_w0"]), token = _xfer_start(
        [g_sc_out, g_sc_in], "sc_grads_start", gather=False)
    dh2 = _mm_nt_blocked(dproj, sc_in_g, F32, "sc_in_bwd", after=(token,))
    dx2, dsh, dsc, dnw_mix1, dd0, dg = _norm_mod_bwd(dh2, x2, vec(mix_norm_w[1]), sc_m[1], dx3, "l1_mix_norm_bwd",
                                                     branch=(d0, g_f[0]))
    dmod[1][0], dmod[1][1], dmod[0][5] = dsh, dsc, dg
    dx1, dyo, dnw_mlp0 = mlp_backward(0, dx2, dd0, x1, h1, u0, s0, mix0, g_m[0])
    g_ssd_out = _mm_tn(yn, dyo, BF16, "ssd_out_wgrad").reshape(N_DEV, -1, D)
    (grad_handle["ssd_out_w0"],), token = _xfer_start([g_ssd_out], "ssd_out_grad_start", gather=False)
    dyn = _mm_nt(dyo, w_ssd_out, F32, "ssd_out_bwd", after=(token,))
    dz, dxs, db_, dc_, ddt, dbias, dalog, dd_, dnw_ssd = _ssd_bwd(
        dyn, y_ssd, zx, xc, prev, decay, alog_p, d_lane, nw_g, d_inner, "ssd_scan_bwd")
    dzx, dcw, dcb = _ssd_conv_bwd(zx, conv_w_full, conv_b0, [dxs, db_, dc_], dz, d_inner, "ssd_conv_bwd")
    dzx = _dzx_finish(dzx, ddt, zx_dim, "ssd_dzx_finish")
    g_in_all = _mm_tn(h0, dzx, BF16, "ssd_in_wgrad", tn=512, tk=2048)
    g_ssd_in = _columns_to_shards(g_in_all, in_ws, in_base, in_off, in_win, "ssd_in_wgrad_shards")
    (grad_handle["ssd_in_w0"],), token = _xfer_start([g_ssd_in], "ssd_in_grad_start", gather=False)
    dh0 = _mm_nt(dzx, w_in_all, F32, "ssd_in_bwd", tm=1024, tk=dzx.shape[1] // 2, after=(token,))
    grad_x, dsh, dsc, dnw_mix0 = _norm_mod_bwd(dh0, x0, vec(mix_norm_w[0]), sc_m[0], dx1, "l0_mix_norm_bwd")
    dmod[0][0], dmod[0][1] = dsh, dsc

    small["ada_b"] = jnp.concatenate([jnp.concatenate(dmod[i], axis=1) for i in range(2)], axis=0)
    small["mix_norm_w"] = jnp.concatenate([dnw_mix0, dnw_mix1], axis=0)
    small["mlp_norm_w"] = jnp.concatenate([dnw_mlp0, dnw_mlp1], axis=0)
    small["ssd_conv_w"] = dcw
    small["ssd_conv_b"] = dcb
    small["ssd_norm_w"] = dnw_ssd.reshape(1, d_inner)
    small["sc_conv_w"] = dscw
    small["loss"] = loss_lane
    small_names = list(small)
    head_names = ["ssd_dt_bias", "ssd_A_log", "ssd_D"]
    handles, small_token = _xfer_start([small[k] for k in small_names] + [dbias, dalog, dd_],
                                       "small_grads_start", gather=True)

    out_g, out_d, out_m, out_v = {}, {}, {}, {}

    layer_res = {}

    def big_update(name, i, after):
        (parts,) = _xfer_wait([grad_handle[f"{name}{i}"]], after, f"grads_wait_{name}_{i}", gather=False)
        res = _adamw_sum(parts, weights[name], moms[name], vars_[name], i, f"adamw_{name}_{i}",
                         prev=layer_res.get(name))
        layer_res[name] = res
        return res[1]

    chain = small_token
    for name, i in [("mlp_down", 1), ("mlp_up", 1), ("sc_out_w", 0), ("sc_in_w", 0), ("mlp_down", 0),
                    ("mlp_up", 0), ("ssd_out_w", 0), ("ssd_in_w", 0)]:
        chain = big_update(name, i, chain)
    gathered_small = _xfer_wait(handles, chain, "small_grads_wait", gather=True)
    small_all = dict(zip(small_names + head_names, gathered_small))

    dmod_loc = lax.dynamic_slice_in_dim(small_all["ada_b"], me * n_mod, n_mod, axis=2)
    dmod_pad = jnp.pad(jnp.transpose(dmod_loc, (1, 0, 2)), ((0, 0), (0, 16 - N_DEV), (0, 0)))
    out_g["ada_w"], out_d["ada_w"], out_m["ada_w"], out_v["ada_w"] = _ada_adamw(
        cond_pad, dmod_pad, ada_w, m_ada_w, v_ada_w, "adamw_ada_w")

    for k in ("ssd_conv_w", "sc_conv_w"):
        n_loc = weights[k].shape[2]
        small_all[k] = lax.dynamic_slice_in_dim(small_all[k], me * n_loc, n_loc, axis=2)
    plain = [k for k in small_names if k != "loss"]
    as2d = lambda a: a.reshape(-1, a.shape[-1])
    res, loss_row = _adamw_small(
        [small_all[k] for k in plain], [tuple(as2d(d[k]) for d in (weights, moms, vars_)) for k in plain],
        [small_all[k] for k in head_names], [tuple(as2d(d[k]) for d in (weights, moms, vars_)) for k in head_names],
        small_all["loss"], "adamw_small")
    loss = loss_row[0, 0]
    for k, res4 in zip(plain + head_names, res):
        for r, dst in zip(res4, (out_g, out_d, out_m, out_v)):
            dst[k] = r.reshape(weights[k].shape)
    for name, res4 in layer_res.items():
        for r, dst in zip(res4, (out_g, out_d, out_m, out_v)):
            dst[name] = r

    return (loss, grad_x[None], *[out_g[k] for k in names], *[out_d[k] for k in names],
            *[out_m[k] for k in names], *[out_v[k] for k in names])
```

```python
import jax
import jax.numpy as jnp
from jax import lax
from jax.experimental import pallas as pl
from jax.experimental.pallas import tpu as pltpu

F32 = jnp.float32
BF16 = jnp.bfloat16
N_DEV = 8
MESH = pl.DeviceIdType.MESH

NORM_EPS = 1e-5
SSD_G = 4
SSD_P = 64
SSD_N = 128
SSD_CHUNK = 128
SSD_K = 4
SC_K = 3
LANES = 128

ADAM_LR = 0.001
ADAM_B1 = 0.9
ADAM_B2 = 0.999
ADAM_EPS = 1e-08
ADAM_WD = 0.01
ADAM_STEP = 10

VMEM_LIMIT = 56 * 1024 * 1024


def _pcall(body, **kw):
    return pl.pallas_call(body, **kw)


def _cparams(sem=None):
    if sem is None:
        return pltpu.CompilerParams(vmem_limit_bytes=VMEM_LIMIT)
    return pltpu.CompilerParams(dimension_semantics=sem, vmem_limit_bytes=VMEM_LIMIT)


def _my_index():
    return 4 * lax.axis_index("x") + 2 * lax.axis_index("y") + lax.axis_index("c")


_PEER_MASKS = [(0, 0, 1), (0, 1, 0), (0, 1, 1), (1, 0, 0), (1, 0, 1), (1, 1, 0), (1, 1, 1)]


def _peers():
    x, y, c = lax.axis_index("x"), lax.axis_index("y"), lax.axis_index("c")
    out = []
    for mx, my, mc in _PEER_MASKS:
        px = (1 - x) if mx else x
        py = (1 - y) if my else y
        pc = (1 - c) if mc else c
        out.append(((px, py, pc), 4 * px + 2 * py + pc))
    return out


def _exchange(arrs, name, gather):
    n = len(arrs)
    n_peer = N_DEV - 1

    def body(*refs):
        ins, outs = refs[:n], refs[n:2 * n]
        send_sems, recv_sems, local_sems = refs[2 * n:]
        me = _my_index()
        peers = _peers()
        started = []
        for a in range(n):
            src_own = ins[a] if gather else ins[a].at[me]
            own = pltpu.make_async_copy(src_own, outs[a].at[me], local_sems.at[a])
            own.start()
            started.append(own)
        sends = []
        for a in range(n):
            for k, (peer, pidx) in enumerate(peers):
                src = ins[a] if gather else ins[a].at[pidx]
                cp = pltpu.make_async_remote_copy(
                    src_ref=src, dst_ref=outs[a].at[me],
                    send_sem=send_sems.at[a * n_peer + k], recv_sem=recv_sems.at[a * n_peer + k],
                    device_id=peer, device_id_type=MESH)
                cp.start()
                sends.append(cp)
        for a in range(n):
            for k, (peer, pidx) in enumerate(peers):
                src = ins[a] if gather else ins[a].at[pidx]
                pltpu.make_async_remote_copy(
                    src_ref=src, dst_ref=outs[a].at[pidx],
                    send_sem=send_sems.at[a * n_peer + k], recv_sem=recv_sems.at[a * n_peer + k],
                    device_id=peer, device_id_type=MESH).wait_recv()
        for cp in sends:
            cp.wait_send()
        for own in started:
            own.wait()

    if gather:
        out_shape = [jax.ShapeDtypeStruct((N_DEV,) + a.shape, a.dtype) for a in arrs]
    else:
        out_shape = [jax.ShapeDtypeStruct(a.shape, a.dtype) for a in arrs]
    any_spec = pl.BlockSpec(memory_space=pl.ANY)
    outs = _pcall(
        body, name=name, out_shape=out_shape,
        in_specs=[any_spec] * n, out_specs=[any_spec] * n,
        scratch_shapes=[pltpu.SemaphoreType.DMA((n * n_peer,)), pltpu.SemaphoreType.DMA((n * n_peer,)),
                        pltpu.SemaphoreType.DMA((n,))],
        compiler_params=pltpu.CompilerParams(has_side_effects=True),
    )(*arrs)
    return list(outs)


def _sibling_forward_start(lands, name):
    n = len(lands)
    n_fwd = len(_OTHER_CHIPS)

    def body(*refs):
        ins, bufs = refs[:n], refs[3 * n:4 * n]
        token = refs[-1]
        sibling = (lax.axis_index("x"), lax.axis_index("y"), 1 - lax.axis_index("c"))
        peers = _peers()
        for a in range(n):
            send_sems, recv_sems = refs[n + 2 * a], refs[n + 2 * a + 1]
            for j, k in enumerate(_OTHER_CHIPS):
                slot = peers[k][1]
                pltpu.make_async_remote_copy(
                    src_ref=ins[a].at[slot], dst_ref=bufs[a].at[slot], send_sem=send_sems.at[j],
                    recv_sem=recv_sems.at[j], device_id=sibling, device_id_type=MESH).start()
        token[...] = jnp.zeros_like(token)

    out_shape, out_specs = [], []
    for _ in range(n):
        out_shape += [pltpu.SemaphoreType.DMA((n_fwd,)), pltpu.SemaphoreType.DMA((n_fwd,))]
        out_specs += [_SEM, _SEM]
    out_shape += [pltpu.HBM(a.shape, a.dtype) for a in lands] + [jax.ShapeDtypeStruct((8, LANES), F32)]
    out_specs += [_HBM] * n + [pl.BlockSpec(memory_space=pltpu.VMEM)]
    outs = _pcall(
        body, name=name, out_shape=tuple(out_shape), in_specs=[_HBM] * n, out_specs=tuple(out_specs),
        input_output_aliases={a: 2 * n + a for a in range(n)},
        compiler_params=pltpu.CompilerParams(has_side_effects=_DATAFLOW),
    )(*[pltpu.with_memory_space_constraint(a, pltpu.HBM) for a in lands])
    return [(outs[2 * n + a], outs[2 * a], outs[2 * a + 1]) for a in range(n)], outs[-1]


def _sibling_forward_wait(handles, after, name):
    n = len(handles)

    def body(*refs):
        sibling = (lax.axis_index("x"), lax.axis_index("y"), 1 - lax.axis_index("c"))
        peers = _peers()
        for a in range(n):
            buf, send_sems, recv_sems = refs[3 * a:3 * a + 3]
            for j, k in enumerate(_OTHER_CHIPS):
                (px, py, pc), slot = peers[k]
                theirs = 4 * px + 2 * py + (1 - pc)
                cp = pltpu.make_async_remote_copy(
                    src_ref=buf.at[slot], dst_ref=buf.at[theirs], send_sem=send_sems.at[j],
                    recv_sem=recv_sems.at[j], device_id=sibling, device_id_type=MESH)
                cp.wait_send()
                cp.wait_recv()

    operands, in_specs = [], []
    for h in handles:
        operands += list(h)
        in_specs += [_HBM, _SEM, _SEM]
    outs = _pcall(
        body, name=name, out_shape=tuple(pltpu.HBM(h[0].shape, h[0].dtype) for h in handles),
        in_specs=in_specs + [pl.BlockSpec(memory_space=pl.ANY)], out_specs=tuple([_HBM] * n),
        input_output_aliases={3 * a: a for a in range(n)},
        compiler_params=pltpu.CompilerParams(has_side_effects=_DATAFLOW),
    )(*operands, after)
    return list(outs)


_HBM = pl.BlockSpec(memory_space=pltpu.HBM)
_SEM = pl.BlockSpec(memory_space=pltpu.SEMAPHORE)
_DATAFLOW = pltpu.SideEffectType.DATAFLOW_SIDE_EFFECTING


_ALL_PEERS = tuple(range(N_DEV - 1))
_SAME_CORE_PEERS = (0, 1, 3, 5)
_OTHER_CHIPS = (1, 3, 5)


def _xfer_start(arrs, name, gather, via_sibling=(), after=()):
    n = len(arrs)
    n_peer = N_DEV - 1
    n_after = len(after)
    peer_ks = [_SAME_CORE_PEERS if a in via_sibling else _ALL_PEERS for a in range(n)]

    def body(*refs):
        ins, lands = refs[:n], refs[n:2 * n]
        sems = refs[2 * n + n_after:5 * n + n_after]
        token = refs[-1]
        me = _my_index()
        peers = _peers()
        for a in range(n):
            send_sems, recv_sems, loc_sem = sems[3 * a:3 * a + 3]
            src_own = ins[a] if gather else ins[a].at[me]
            pltpu.make_async_copy(src_own, lands[a].at[me], loc_sem).start()
            for k in peer_ks[a]:
                peer, pidx = peers[k]
                src = ins[a] if gather else ins[a].at[pidx]
                pltpu.make_async_remote_copy(
                    src_ref=src, dst_ref=lands[a].at[me], send_sem=send_sems.at[k], recv_sem=recv_sems.at[k],
                    device_id=peer, device_id_type=MESH).start()
        token[...] = jnp.zeros_like(token)

    land_shapes = [((N_DEV,) + a.shape) if gather else a.shape for a in arrs]
    out_shape, out_specs = [], []
    for _ in range(n):
        out_shape += [pltpu.SemaphoreType.DMA((n_peer,)), pltpu.SemaphoreType.DMA((n_peer,)),
                      pltpu.SemaphoreType.DMA(())]
        out_specs += [_SEM, _SEM, _SEM]
    out_shape += [pltpu.HBM(a.shape, a.dtype) for a in arrs]
    out_shape += [pltpu.HBM(s, a.dtype) for s, a in zip(land_shapes, arrs)]
    out_shape += [jax.ShapeDtypeStruct((8, LANES), F32)]
    out_specs += [_HBM] * (2 * n) + [pl.BlockSpec(memory_space=pltpu.VMEM)]
    aliases = {}
    for a in range(n):
        aliases[a] = 3 * n + a
        aliases[n + a] = 4 * n + a
    operands = [pltpu.with_memory_space_constraint(a, pltpu.HBM) for a in arrs]
    operands += [pltpu.with_memory_space_constraint(lax.empty(s, a.dtype), pltpu.HBM)
                 for s, a in zip(land_shapes, arrs)]
    outs = _pcall(
        body, name=name, out_shape=tuple(out_shape),
        in_specs=[_HBM] * (2 * n) + [pl.BlockSpec(memory_space=pl.ANY)] * n_after, out_specs=tuple(out_specs),
        input_output_aliases=aliases,
        compiler_params=pltpu.CompilerParams(has_side_effects=_DATAFLOW),
    )(*operands, *after)
    handles = []
    for a in range(n):
        handles.append((outs[3 * n + a], outs[4 * n + a], outs[3 * a], outs[3 * a + 1], outs[3 * a + 2],
                        peer_ks[a]))
    return handles, outs[-1]


def _xfer_wait(handles, after, name, gather):
    n = len(handles)
    after = tuple(after) if isinstance(after, (tuple, list)) else (after,)
    peer_ks = [h[5] for h in handles]

    def body(*refs):
        me = _my_index()
        peers = _peers()
        for a in range(n):
            src_ref, land_ref, send_ref, recv_ref, loc_ref = refs[5 * a:5 * a + 5]
            src_own = src_ref if gather else src_ref.at[me]
            pltpu.make_async_copy(src_own, land_ref.at[me], loc_ref).wait()
            for k in peer_ks[a]:
                peer, pidx = peers[k]
                src = src_ref if gather else src_ref.at[pidx]
                cp = pltpu.make_async_remote_copy(
                    src_ref=src, dst_ref=land_ref.at[pidx], send_sem=send_ref.at[k], recv_sem=recv_ref.at[k],
                    device_id=peer, device_id_type=MESH)
                cp.wait_send()
                cp.wait_recv()

    operands, in_specs, out_shape, aliases = [], [], [], {}
    for a, h in enumerate(handles):
        operands += list(h[:5])
        in_specs += [_HBM, _HBM, _SEM, _SEM, _SEM]
        out_shape += [pltpu.HBM(h[0].shape, h[0].dtype), pltpu.HBM(h[1].shape, h[1].dtype)]
        aliases[5 * a] = 2 * a
        aliases[5 * a + 1] = 2 * a + 1
    outs = _pcall(
        body, name=name, out_shape=tuple(out_shape),
        in_specs=in_specs + [pl.BlockSpec(memory_space=pl.ANY)] * len(after),
        out_specs=tuple([_HBM] * (2 * n)), input_output_aliases=aliases,
        compiler_params=pltpu.CompilerParams(has_side_effects=_DATAFLOW),
    )(*operands, *after)
    return [outs[2 * a + 1] for a in range(n)]


def _sibling_forward(lands, name):
    n = len(lands)
    n_fwd = len(_OTHER_CHIPS)

    def body(*refs):
        ins, bufs = refs[:n], refs[n:2 * n]
        send_sems, recv_sems = refs[2 * n:]
        x, y, c = lax.axis_index("x"), lax.axis_index("y"), lax.axis_index("c")
        sibling = (x, y, 1 - c)
        peers = _peers()
        sends = []
        for a in range(n):
            for j, k in enumerate(_OTHER_CHIPS):
                slot = peers[k][1]
                cp = pltpu.make_async_remote_copy(
                    src_ref=ins[a].at[slot], dst_ref=bufs[a].at[slot],
                    send_sem=send_sems.at[a * n_fwd + j], recv_sem=recv_sems.at[a * n_fwd + j],
                    device_id=sibling, device_id_type=MESH)
                cp.start()
                sends.append(cp)
        for a in range(n):
            for j, k in enumerate(_OTHER_CHIPS):
                (px, py, pc), slot = peers[k]
                theirs = 4 * px + 2 * py + (1 - pc)
                pltpu.make_async_remote_copy(
                    src_ref=ins[a].at[slot], dst_ref=bufs[a].at[theirs],
                    send_sem=send_sems.at[a * n_fwd + j], recv_sem=recv_sems.at[a * n_fwd + j],
                    device_id=sibling, device_id_type=MESH).wait_recv()
        for cp in sends:
            cp.wait_send()

    any_spec = pl.BlockSpec(memory_space=pl.ANY)
    outs = _pcall(
        body, name=name, out_shape=[jax.ShapeDtypeStruct(a.shape, a.dtype) for a in lands],
        in_specs=[any_spec] * n, out_specs=[any_spec] * n,
        input_output_aliases={a: a for a in range(n)},
        scratch_shapes=[pltpu.SemaphoreType.DMA((n * n_fwd,)), pltpu.SemaphoreType.DMA((n * n_fwd,))],
        compiler_params=pltpu.CompilerParams(has_side_effects=True),
    )(*lands)
    return list(outs)


_DIMS = {"nn": (((1,), (0,)), ((), ())), "nt": (((1,), (1,)), ((), ())), "tn": (((0,), (0,)), ((), ()))}


def _dot(a, b, mode="nn"):
    if mode == "nt_blocks":
        n = b.shape[2]
        return sum(_dot(a[:, i * n:(i + 1) * n], b[i], "nt") for i in range(b.shape[0]))
    return lax.dot_general(a, b, _DIMS[mode], preferred_element_type=F32)


def _mm(a, b, *, mode, grid, a_spec, b_spec, out_shape, out_specs, acc_shape, epilogue, name,
        extra=(), extra_specs=(), after=(), semantics=("parallel", "parallel", "arbitrary")):
    nk = grid[2]
    n_extra = len(extra)
    n_in = 2 + n_extra + len(after)

    def body_single(*refs):
        a_ref, b_ref = refs[0], refs[1]
        epilogue(_dot(a_ref[...], b_ref[...], mode), refs[2:2 + n_extra], refs[n_in:])

    def body_acc(*refs):
        a_ref, b_ref = refs[0], refs[1]
        ex = refs[2:2 + n_extra]
        outs = refs[n_in:-1]
        acc = refs[-1]
        k = pl.program_id(2)

        @pl.when(k == 0)
        def _():
            acc[...] = jnp.zeros_like(acc)

        acc[...] += _dot(a_ref[...], b_ref[...], mode)

        @pl.when(k == nk - 1)
        def _():
            epilogue(acc[...], ex, outs)

    return _pcall(
        body_single if nk == 1 else body_acc, name=name, grid=grid, out_shape=out_shape,
        in_specs=[a_spec, b_spec] + list(extra_specs) + [pl.BlockSpec(memory_space=pl.ANY)] * len(after),
        out_specs=out_specs,
        scratch_shapes=[] if nk == 1 else [pltpu.VMEM(acc_shape, F32)],
        compiler_params=_cparams(semantics),
    )(a, b, *extra, *after)


def _ep_store(dtype):
    def ep(acc, ex, outs):
        outs[0][...] = acc.astype(dtype)
    return ep


def _ep_relu2(acc, ex, outs):
    outs[0][...] = acc.astype(BF16)
    r = jnp.maximum(acc, 0.0)
    outs[1][...] = (r * r).astype(BF16)


def _ep_relu2_bwd(acc, ex, outs):
    u = ex[0][...].astype(F32)
    outs[0][...] = (acc * (2.0 * jnp.maximum(u, 0.0))).astype(BF16)


def _tile(n, want):
    t = min(n, want)
    while n % t:
        t //= 2
    return t


def _mm_nn(a, w, out_dtype, name, tm=2048, tn=1024, tk=1024, epilogue=None, out_dtypes=None, after=()):
    M, K = a.shape
    N = w.shape[1]
    tm, tn, tk = _tile(M, tm), _tile(N, tn), _tile(K, tk)
    out_dtypes = out_dtypes or [out_dtype]
    return _mm(a, w, mode="nn", grid=(M // tm, N // tn, K // tk),
               a_spec=pl.BlockSpec((tm, tk), lambda i, j, k: (i, k)),
               b_spec=pl.BlockSpec((tk, tn), lambda i, j, k: (k, j)),
               out_shape=[jax.ShapeDtypeStruct((M, N), d) for d in out_dtypes],
               out_specs=[pl.BlockSpec((tm, tn), lambda i, j, k: (i, j)) for _ in out_dtypes],
               acc_shape=(tm, tn), epilogue=epilogue or _ep_store(out_dtype), name=name, after=after)


def _mm_nn_blocked(a, wg, name, epilogue, out_dtypes, tm=2048):
    M, K = a.shape
    n = wg.shape[2]
    tm = _tile(M, tm)
    return _mm(a, wg, mode="nn", grid=(M // tm, N_DEV, 1),
               a_spec=pl.BlockSpec((tm, K), lambda i, j, k: (i, 0)),
               b_spec=pl.BlockSpec((None, K, n), lambda i, j, k: (j, 0, 0)),
               out_shape=[jax.ShapeDtypeStruct((M, N_DEV * n), d) for d in out_dtypes],
               out_specs=[pl.BlockSpec((tm, n), lambda i, j, k: (i, j)) for _ in out_dtypes],
               acc_shape=(tm, n), epilogue=epilogue, name=name)


def _mm_nt(a, w, out_dtype, name, tm=2048, tn=1024, tk=1024, epilogue=None, extra=(), extra_specs=(),
           after=()):
    M, K = a.shape
    N = w.shape[0]
    tm, tn, tk = _tile(M, tm), _tile(N, tn), _tile(K, tk)
    if extra and not extra_specs:
        extra_specs = [pl.BlockSpec((tm, tn), lambda i, j, k: (i, j)) for _ in extra]
    return _mm(a, w, mode="nt", grid=(M // tm, N // tn, K // tk),
               a_spec=pl.BlockSpec((tm, tk), lambda i, j, k: (i, k)),
               b_spec=pl.BlockSpec((tn, tk), lambda i, j, k: (j, k)),
               out_shape=[jax.ShapeDtypeStruct((M, N), out_dtype)],
               out_specs=[pl.BlockSpec((tm, tn), lambda i, j, k: (i, j))],
               acc_shape=(tm, tn), epilogue=epilogue or _ep_store(out_dtype), name=name,
               extra=extra, extra_specs=extra_specs, after=after)[0]


def _mm_nt_blocked(a, wg, out_dtype, name, tm=1024, after=()):
    M = a.shape[0]
    kout, n = wg.shape[1], wg.shape[2]
    tm = _tile(M, tm)
    per = 4
    return _mm(a, wg, mode="nt_blocks", grid=(M // tm, 1, N_DEV // per),
               a_spec=pl.BlockSpec((tm, per * n), lambda i, j, k: (i, k)),
               b_spec=pl.BlockSpec((per, kout, n), lambda i, j, k: (k, 0, 0)),
               out_shape=[jax.ShapeDtypeStruct((M, kout), out_dtype)],
               out_specs=[pl.BlockSpec((tm, kout), lambda i, j, k: (i, 0))],
               acc_shape=(tm, kout), epilogue=_ep_store(out_dtype), name=name, after=after)[0]


def _mm_tn(a, b, out_dtype, name, tm=1024, tn=1024, tk=2048):
    K, M = a.shape
    N = b.shape[1]
    tm, tn, tk = _tile(M, tm), _tile(N, tn), _tile(K, tk)
    return _mm(a, b, mode="tn", grid=(M // tm, N // tn, K // tk),
               a_spec=pl.BlockSpec((tk, tm), lambda i, j, k: (k, i)),
               b_spec=pl.BlockSpec((tk, tn), lambda i, j, k: (k, j)),
               out_shape=[jax.ShapeDtypeStruct((M, N), out_dtype)],
               out_specs=[pl.BlockSpec((tm, tn), lambda i, j, k: (i, j))],
               acc_shape=(tm, tn), epilogue=_ep_store(out_dtype), name=name)[0]


def _mm_tn_blocked(a, b, out_dtype, name, tm=1024, tk=2048):
    K, M = a.shape
    n = b.shape[1] // N_DEV
    tm, tk = _tile(M, tm), _tile(K, tk)
    return _mm(a, b, mode="tn", grid=(M // tm, N_DEV, K // tk),
               a_spec=pl.BlockSpec((tk, tm), lambda i, j, k: (k, i)),
               b_spec=pl.BlockSpec((tk, n), lambda i, j, k: (k, j)),
               out_shape=[jax.ShapeDtypeStruct((N_DEV, M, n), out_dtype)],
               out_specs=[pl.BlockSpec((None, tm, n), lambda i, j, k: (j, i, 0))],
               acc_shape=(tm, n), epilogue=_ep_store(out_dtype), name=name)[0]


def _window_geometry(ws):
    base = [(ws * k // LANES) * LANES for k in range(N_DEV)]
    off = [ws * k - base[k] for k in range(N_DEV)]
    win = -(-(max(off) + ws) // LANES) * LANES
    return base, off, win


def _shards_to_columns(xg, base, off, win, n_out, name, tr=256):
    R, ws = xg.shape[1], xg.shape[2]
    tr = _tile(R, tr)
    nb_win = win // LANES

    def body(x_ref, o_ref, frame_ref):
        written = set()
        frame_ref[...] = jnp.zeros_like(frame_ref)
        for k in range(N_DEV):
            frame_ref[:, 0:ws] = x_ref[k].astype(F32)
            window = frame_ref[...]
            if off[k]:
                window = pltpu.roll(window, off[k], 1)
            for i in range(nb_win):
                b = base[k] // LANES + i
                if b * LANES >= n_out:
                    continue
                cols = slice(b * LANES, (b + 1) * LANES)
                blk = window[:, i * LANES:(i + 1) * LANES]
                if b in written:
                    blk = blk + o_ref[:, cols].astype(F32)
                o_ref[:, cols] = blk.astype(o_ref.dtype)
                written.add(b)
        for b in range(n_out // LANES):
            if b not in written:
                o_ref[:, b * LANES:(b + 1) * LANES] = jnp.zeros((tr, LANES), o_ref.dtype)

    return _pcall(
        body, name=name, grid=(R // tr,), out_shape=jax.ShapeDtypeStruct((R, n_out), xg.dtype),
        in_specs=[pl.BlockSpec((N_DEV, tr, ws), lambda i: (0, i, 0))],
        out_specs=pl.BlockSpec((tr, n_out), lambda i: (i, 0)),
        scratch_shapes=[pltpu.VMEM((tr, win), F32)],
        compiler_params=_cparams(("parallel",)))(xg)


def _columns_to_shards(x, ws, base, off, win, name, tr=256):
    R = x.shape[0]
    tr = _tile(R, tr)

    def body(x_ref, o_ref, frame_ref):
        for k in range(N_DEV):
            window = x_ref[:, base[k]:base[k] + win].astype(F32)
            if off[k]:
                window = pltpu.roll(window, win - off[k], 1)
            frame_ref[...] = window
            o_ref[k] = frame_ref[:, 0:ws].astype(o_ref.dtype)

    return _pcall(
        body, name=name, grid=(R // tr,), out_shape=jax.ShapeDtypeStruct((N_DEV, R, ws), x.dtype),
        in_specs=[pl.BlockSpec((tr, x.shape[1]), lambda i: (i, 0))],
        out_specs=pl.BlockSpec((N_DEV, tr, ws), lambda i: (0, i, 0)),
        scratch_shapes=[pltpu.VMEM((tr, win), F32)],
        compiler_params=_cparams(("parallel",)))(x)


def _sigmoid(x):
    return 1.0 / (1.0 + jnp.exp(-x))


def _row_spec(tm, d):
    return pl.BlockSpec((tm, d), lambda i: (i, 0))


def _vec_spec(d):
    return pl.BlockSpec((1, d), lambda i: (0, 0))


def _norm_mod_fwd(x, y, gate, nw, scale, shift, name, tm=512):
    L, D = x.shape
    tm = _tile(L, tm)
    has_res = y is not None

    def body(*refs):
        if has_res:
            x_ref, y_ref, g_ref, nw_ref, sc_ref, sh_ref, xo_ref, h_ref = refs
            xn = x_ref[...] + g_ref[...] * y_ref[...]
            xo_ref[...] = xn
        else:
            x_ref, nw_ref, sc_ref, sh_ref, h_ref = refs
            xn = x_ref[...]
        rstd = lax.rsqrt(jnp.mean(xn * xn, axis=-1, keepdims=True) + NORM_EPS)
        h = xn * rstd * nw_ref[...] * (1.0 + sc_ref[...]) + sh_ref[...]
        h_ref[...] = h.astype(BF16)

    row, vec = _row_spec(tm, D), _vec_spec(D)
    if has_res:
        ins, in_specs = (x, y, gate, nw, scale, shift), [row, row, vec, vec, vec, vec]
        out_shape = [jax.ShapeDtypeStruct((L, D), F32), jax.ShapeDtypeStruct((L, D), BF16)]
        out_specs = [row, row]
    else:
        ins, in_specs = (x, nw, scale, shift), [row, vec, vec, vec]
        out_shape = [jax.ShapeDtypeStruct((L, D), BF16)]
        out_specs = [row]
    outs = _pcall(body, name=name, grid=(L // tm,), out_shape=out_shape, in_specs=in_specs,
                  out_specs=out_specs, compiler_params=_cparams(("parallel",)))(*ins)
    return outs if has_res else (x, outs[0])


def _gated_branch_bwd(dx, branch, y_ref, g_ref, dy_ref, dg_ref):
    if branch is None:
        return
    dy_ref[...] = (g_ref[...] * dx).astype(BF16)
    dg_ref[...] += jnp.sum(dx * y_ref[...], axis=0, keepdims=True)


def _norm_mod_bwd(dh, x, nw, scale, dres, name, branch=None, tm=512):
    L, D = x.shape
    tm = _tile(L, tm)
    nb = 0 if branch is None else 2

    def body(dh_ref, x_ref, nw_ref, sc_ref, dres_ref, *rest):
        y_ref, g_ref = rest[:nb] if nb else (None, None)
        dx_ref, dsh_ref, dsc_ref, dnw_ref = rest[nb:nb + 4]
        dy_ref, dg_ref = rest[nb + 4:] if nb else (None, None)

        @pl.when(pl.program_id(0) == 0)
        def _():
            dsh_ref[...] = jnp.zeros_like(dsh_ref)
            dsc_ref[...] = jnp.zeros_like(dsc_ref)
            dnw_ref[...] = jnp.zeros_like(dnw_ref)
            if nb:
                dg_ref[...] = jnp.zeros_like(dg_ref)

        xv = x_ref[...]
        dh_v = dh_ref[...]
        nw_v = nw_ref[...]
        rstd = lax.rsqrt(jnp.mean(xv * xv, axis=-1, keepdims=True) + NORM_EPS)
        xhat = xv * rstd
        dsh_ref[...] += jnp.sum(dh_v, axis=0, keepdims=True)
        dsc_ref[...] += jnp.sum(dh_v * (xhat * nw_v), axis=0, keepdims=True)
        dr = dh_v * (1.0 + sc_ref[...])
        dnw_ref[...] += jnp.sum(dr * xhat, axis=0, keepdims=True)
        dxh = dr * nw_v
        dx = rstd * (dxh - xhat * jnp.mean(dxh * xhat, axis=-1, keepdims=True)) + dres_ref[...]
        dx_ref[...] = dx
        _gated_branch_bwd(dx, branch, y_ref, g_ref, dy_ref, dg_ref)

    row, vec = _row_spec(tm, D), _vec_spec(D)
    extra_in = [] if branch is None else list(branch)
    return _pcall(
        body, name=name, grid=(L // tm,),
        out_shape=[jax.ShapeDtypeStruct((L, D), F32)] + [jax.ShapeDtypeStruct((1, D), F32)] * 3
        + ([jax.ShapeDtypeStruct((L, D), BF16), jax.ShapeDtypeStruct((1, D), F32)] if nb else []),
        in_specs=[row, row, vec, vec, row] + ([row, vec] if nb else []),
        out_specs=[row, vec, vec, vec] + ([row, vec] if nb else []),
        compiler_params=_cparams(("arbitrary",)))(dh, x, nw, scale, dres, *extra_in)


def _final_loss(x, y, gate, fw, target, name, tm=512):
    L, D = x.shape
    tm = _tile(L, tm)

    def body(x_ref, y_ref, g_ref, fw_ref, t_ref, dx_ref, loss_ref, dfw_ref, dy_ref, dg_ref):
        @pl.when(pl.program_id(0) == 0)
        def _():
            loss_ref[...] = jnp.zeros_like(loss_ref)
            dfw_ref[...] = jnp.zeros_like(dfw_ref)
            dg_ref[...] = jnp.zeros_like(dg_ref)

        xn = x_ref[...] + g_ref[...] * y_ref[...]
        fw_v = fw_ref[...]
        rstd = lax.rsqrt(jnp.mean(xn * xn, axis=-1, keepdims=True) + NORM_EPS)
        xhat = xn * rstd
        diff = xhat * fw_v - t_ref[...]
        loss_ref[...] += jnp.sum(diff * diff, axis=0, keepdims=True)
        dyf = diff * (1.0 / D)
        dfw_ref[...] += jnp.sum(dyf * xhat, axis=0, keepdims=True)
        dxh = dyf * fw_v
        dx = rstd * (dxh - xhat * jnp.mean(dxh * xhat, axis=-1, keepdims=True))
        dx_ref[...] = dx
        _gated_branch_bwd(dx, True, y_ref, g_ref, dy_ref, dg_ref)

    row, vec = _row_spec(tm, D), _vec_spec(D)
    return _pcall(
        body, name=name, grid=(L // tm,),
        out_shape=[jax.ShapeDtypeStruct((L, D), F32), jax.ShapeDtypeStruct((1, D), F32),
                   jax.ShapeDtypeStruct((1, D), F32), jax.ShapeDtypeStruct((L, D), BF16),
                   jax.ShapeDtypeStruct((1, D), F32)],
        in_specs=[row, row, vec, vec, row], out_specs=[row, vec, vec, row, vec],
        compiler_params=_cparams(("arbitrary",)))(x, y, gate, fw, target)


def _shift_down(v, s, row):
    if s == 0:
        return v
    return jnp.where(row >= s, pltpu.roll(v, s, 0), 0.0)


CONV_ROWS = 32
NORM_ROWS = 16


def _shifted_rows(x_ref, r0, n, lanes=slice(None)):
    cur = x_ref[r0:r0 + CONV_ROWS, lanes]
    if r0 >= n - 1:
        return [cur] + [x_ref[r0 - s:r0 - s + CONV_ROWS, lanes] for s in range(1, n)]
    row = lax.broadcasted_iota(jnp.int32, cur.shape, 0)
    return [_shift_down(cur, s, row) for s in range(n)]


def _ssd_conv_fwd(zx, w, b, col0, width, name, cb=512):
    L = zx.shape[0]
    nb = width // cb
    off = col0 // cb

    def body(x_ref, w_ref, b_ref, o_ref):
        for l0 in range(0, cb, LANES):
            lanes = slice(l0, l0 + LANES)
            taps = [w_ref[k:k + 1, lanes] for k in range(SSD_K)]
            bias = b_ref[:, lanes]
            for r0 in range(0, L, CONV_ROWS):
                taps_in = _shifted_rows(x_ref, r0, SSD_K, lanes)
                acc = bias + taps[SSD_K - 1] * taps_in[0]
                for s in range(1, SSD_K):
                    acc = acc + taps[SSD_K - 1 - s] * taps_in[s]
                o_ref[r0:r0 + CONV_ROWS, lanes] = acc * _sigmoid(acc)

    return _pcall(
        body, name=name, grid=(nb,), out_shape=jax.ShapeDtypeStruct((L, width), F32),
        in_specs=[pl.BlockSpec((L, cb), lambda j: (0, off + j)),
                  pl.BlockSpec((SSD_K, cb), lambda j: (0, j)),
                  pl.BlockSpec((1, cb), lambda j: (0, j))],
        out_specs=pl.BlockSpec((L, cb), lambda j: (0, j)),
        compiler_params=_cparams(("parallel",)))(zx, w, b)


def _ssd_conv_bwd(zx, w, b, d_parts, dzx, col0, name, cb=128):
    L = zx.shape[0]
    widths = [p.shape[1] for p in d_parts]
    width = sum(widths)
    nb = width // cb
    off = col0 // cb
    starts = [sum(widths[:i]) // cb for i in range(len(d_parts))]
    counts = [wd // cb for wd in widths]

    def body(x_ref, w_ref, b_ref, *rest):
        d_refs = rest[:len(d_parts)]
        dx_ref, dw_ref, db_ref, dpre_ref = rest[len(d_parts) + 1:]
        j = pl.program_id(0)
        taps = [w_ref[k:k + 1, :] for k in range(SSD_K)]
        bias = b_ref[...]
        fold = lambda v: sum(v[r:r + 8, :] for r in range(0, CONV_ROWS, 8))
        db8 = jnp.zeros((8, cb), F32)
        dw8 = [jnp.zeros((8, cb), F32) for _ in range(SSD_K)]
        for r0 in range(0, L, CONV_ROWS):
            rows = slice(r0, r0 + CONV_ROWS)
            d_val = d_refs[-1][rows, :]
            for i in range(len(d_parts) - 2, -1, -1):
                d_val = jnp.where(j < starts[i + 1], d_refs[i][rows, :], d_val)
            taps_in = _shifted_rows(x_ref, r0, SSD_K)
            acc = bias + taps[SSD_K - 1] * taps_in[0]
            for s in range(1, SSD_K):
                acc = acc + taps[SSD_K - 1 - s] * taps_in[s]
            sig = _sigmoid(acc)
            dpre = d_val * (sig * (1.0 + acc * (1.0 - sig)))
            dpre_ref[rows, :] = dpre
            db8 = db8 + fold(dpre)
            for s in range(SSD_K):
                dw8[s] = dw8[s] + fold(dpre * taps_in[s])
        dpre_ref[L:L + 8, :] = jnp.zeros((8, cb), F32)
        db_ref[...] = jnp.sum(db8, axis=0, keepdims=True)
        for s in range(SSD_K):
            dw_ref[SSD_K - 1 - s:SSD_K - s, :] = jnp.sum(dw8[s], axis=0, keepdims=True)
        for r0 in range(0, L, CONV_ROWS):
            dx = taps[SSD_K - 1] * dpre_ref[r0:r0 + CONV_ROWS, :]
            for s in range(1, SSD_K):
                dx = dx + taps[SSD_K - 1 - s] * dpre_ref[r0 + s:r0 + s + CONV_ROWS, :]
            dx_ref[r0:r0 + CONV_ROWS, :] = dx.astype(BF16)

    def part_spec(i):
        return pl.BlockSpec((L, cb), lambda j: (0, jnp.clip(j - starts[i], 0, counts[i] - 1)))

    return _pcall(
        body, name=name, grid=(nb,),
        out_shape=[jax.ShapeDtypeStruct(dzx.shape, BF16), jax.ShapeDtypeStruct((SSD_K, width), F32),
                   jax.ShapeDtypeStruct((1, width), F32)],
        in_specs=[pl.BlockSpec((L, cb), lambda j: (0, off + j)),
                  pl.BlockSpec((SSD_K, cb), lambda j: (0, j)),
                  pl.BlockSpec((1, cb), lambda j: (0, j))]
        + [part_spec(i) for i in range(len(d_parts))] + [pl.BlockSpec(memory_space=pl.ANY)],
        out_specs=[pl.BlockSpec((L, cb), lambda j: (0, off + j)),
                   pl.BlockSpec((SSD_K, cb), lambda j: (0, j)),
                   pl.BlockSpec((1, cb), lambda j: (0, j))],
        input_output_aliases={3 + len(d_parts): 0},
        scratch_shapes=[pltpu.VMEM((L + 8, cb), F32)],
        compiler_params=_cparams(("parallel",)))(zx, w, b, *d_parts, dzx)


def _dzx_finish(dzx, ddt, col0, name, tl=512):
    G, L, _ = ddt.shape
    tail = dzx.shape[1] - col0
    tl = _tile(L, tl)

    def body(ddt_ref, dzx_ref, o_ref):
        s = ddt_ref[0]
        for g in range(1, G):
            s = s + ddt_ref[g]
        o_ref[:, 0:LANES] = s.astype(o_ref.dtype)
        if tail > LANES:
            o_ref[:, LANES:] = jnp.zeros((tl, tail - LANES), o_ref.dtype)

    return _pcall(
        body, name=name, grid=(L // tl,), out_shape=jax.ShapeDtypeStruct(dzx.shape, dzx.dtype),
        in_specs=[pl.BlockSpec((G, tl, LANES), lambda i: (0, i, 0)), pl.BlockSpec(memory_space=pl.ANY)],
        out_specs=pl.BlockSpec((tl, tail), lambda i: (i, col0 // tail)),
        input_output_aliases={1: 0},
        compiler_params=_cparams(("parallel",)))(ddt, dzx)


def _sc_conv_fwd(proj, w, name, cb=512):
    L = proj.shape[0]
    width = proj.shape[1] // 3
    nb = width // cb

    def body(b_ref, c_ref, x_ref, w_ref, o_ref):
        for l0 in range(0, cb, LANES):
            lanes = slice(l0, l0 + LANES)
            taps = [w_ref[k:k + 1, lanes] for k in range(SC_K)]
            for r0 in range(0, L, CONV_ROWS):
                rows = slice(r0, r0 + CONV_ROWS)
                q = [c * x for c, x in zip(_shifted_rows(c_ref, r0, SC_K, lanes),
                                           _shifted_rows(x_ref, r0, SC_K, lanes))]
                acc = taps[SC_K - 1] * q[0]
                for s in range(1, SC_K):
                    acc = acc + taps[SC_K - 1 - s] * q[s]
                o_ref[rows, lanes] = (b_ref[rows, lanes] * acc).astype(BF16)

    return _pcall(
        body, name=name, grid=(nb,), out_shape=jax.ShapeDtypeStruct((L, width), BF16),
        in_specs=[pl.BlockSpec((L, cb), lambda j: (0, j)),
                  pl.BlockSpec((L, cb), lambda j: (0, nb + j)),
                  pl.BlockSpec((L, cb), lambda j: (0, 2 * nb + j)),
                  pl.BlockSpec((SC_K, cb), lambda j: (0, j))],
        out_specs=pl.BlockSpec((L, cb), lambda j: (0, j)),
        compiler_params=_cparams(("parallel",)))(proj, proj, proj, w)


def _sc_conv_bwd(proj, w, dy, name, cb=128):
    L = proj.shape[0]
    width = proj.shape[1] // 3
    nb = width // cb

    def body(b_ref, c_ref, x_ref, w_ref, dy_ref, db_ref, dc_ref, dxv_ref, dw_ref, dconv_ref):
        taps = [w_ref[k:k + 1, :] for k in range(SC_K)]
        fold = lambda v: sum(v[r:r + 8, :] for r in range(0, CONV_ROWS, 8))
        dw8 = [jnp.zeros((8, cb), F32) for _ in range(SC_K)]
        for r0 in range(0, L, CONV_ROWS):
            rows = slice(r0, r0 + CONV_ROWS)
            q = [c * x for c, x in zip(_shifted_rows(c_ref, r0, SC_K), _shifted_rows(x_ref, r0, SC_K))]
            conv = taps[SC_K - 1] * q[0]
            for s in range(1, SC_K):
                conv = conv + taps[SC_K - 1 - s] * q[s]
            dyv = dy_ref[rows, :]
            db_ref[rows, :] = (dyv * conv).astype(BF16)
            dconv = dyv * b_ref[rows, :]
            dconv_ref[rows, :] = dconv
            for s in range(SC_K):
                dw8[s] = dw8[s] + fold(dconv * q[s])
        dconv_ref[L:L + 8, :] = jnp.zeros((8, cb), F32)
        for s in range(SC_K):
            dw_ref[SC_K - 1 - s:SC_K - s, :] = jnp.sum(dw8[s], axis=0, keepdims=True)
        for r0 in range(0, L, CONV_ROWS):
            rows = slice(r0, r0 + CONV_ROWS)
            dq = taps[SC_K - 1] * dconv_ref[rows, :]
            for s in range(1, SC_K):
                dq = dq + taps[SC_K - 1 - s] * dconv_ref[r0 + s:r0 + s + CONV_ROWS, :]
            dc_ref[rows, :] = (dq * x_ref[rows, :]).astype(BF16)
            dxv_ref[rows, :] = (dq * c_ref[rows, :]).astype(BF16)

    blk = pl.BlockSpec((L, cb), lambda j: (0, j))
    wblk = pl.BlockSpec((SC_K, cb), lambda j: (0, j))
    return _pcall(
        body, name=name, grid=(nb,),
        out_shape=[jax.ShapeDtypeStruct((L, width), BF16)] * 3 + [jax.ShapeDtypeStruct((SC_K, width), F32)],
        in_specs=[blk, pl.BlockSpec((L, cb), lambda j: (0, nb + j)),
                  pl.BlockSpec((L, cb), lambda j: (0, 2 * nb + j)), wblk, blk],
        out_specs=[blk, blk, blk, wblk], scratch_shapes=[pltpu.VMEM((L + 8, cb), F32)],
        compiler_params=_cparams(("parallel",)))(proj, proj, proj, w, dy)


def _split3(v):
    hi = v.astype(BF16)
    r1 = v - hi.astype(F32)
    mid = r1.astype(BF16)
    lo = (r1 - mid.astype(F32)).astype(BF16)
    return hi, mid, lo


def _dot_exact01(t01, v):
    hi, mid, lo = _split3(v)
    return _dot(t01, hi) + _dot(t01, mid) + _dot(t01, lo)


def _lane_col(v, lane, h):
    return jnp.sum(jnp.where(lane == h, v, 0.0), axis=1, keepdims=True)


def _sum_all(v):
    return jnp.sum(jnp.sum(v, axis=1, keepdims=True), axis=0, keepdims=True)


def _softplus(x):
    return jnp.maximum(x, 0.0) + jnp.log1p(jnp.exp(-jnp.abs(x)))


def _ssd_decay(zx, bias_p, alog_p, n_heads, dt_block, name):
    L = zx.shape[0]
    nc = L // SSD_CHUNK
    per_step = 4 if nc % 4 == 0 else 1
    rows_step = per_step * SSD_CHUNK

    def body(raw_ref, bias_ref, alog_ref, dt_ref, sg_ref, cs_ref, cst_ref, last_ref):
        lane = lax.broadcasted_iota(jnp.int32, (SSD_CHUNK, LANES), 1)
        row = lax.broadcasted_iota(jnp.int32, (SSD_CHUNK, LANES), 0)
        valid = lane < n_heads
        tri = (row >= lane).astype(BF16)
        a_row = -jnp.exp(alog_ref[...])
        for i in range(per_step):
            rows = slice(i * SSD_CHUNK, (i + 1) * SSD_CHUNK)
            raw = raw_ref[rows, :] + bias_ref[...]
            dt = jnp.where(valid, _softplus(raw), 0.0)
            a = dt * a_row
            cs = _dot_exact01(tri, a)
            dt_ref[rows, :] = dt
            sg_ref[rows, :] = _sigmoid(raw)
            cs_ref[rows, :] = cs
            cst_ref[i] = cs.T
            last_ref[i] = jnp.sum(a, axis=0, keepdims=True)

    blk = pl.BlockSpec((rows_step, LANES), lambda c: (c, 0))
    head_vec = pl.BlockSpec((1, LANES), lambda c: (0, 0))
    return _pcall(
        body, name=name, grid=(nc // per_step,),
        out_shape=[jax.ShapeDtypeStruct((L, LANES), F32)] * 3
        + [jax.ShapeDtypeStruct((nc, SSD_CHUNK, LANES), F32), jax.ShapeDtypeStruct((nc, 1, LANES), F32)],
        in_specs=[pl.BlockSpec((rows_step, LANES), lambda c: (c, dt_block)), head_vec, head_vec],
        out_specs=[blk, blk, blk, pl.BlockSpec((per_step, SSD_CHUNK, LANES), lambda c: (c, 0, 0)),
                   pl.BlockSpec((per_step, 1, LANES), lambda c: (c, 0, 0))],
        compiler_params=_cparams(("parallel",)))(zx, bias_p, alog_p)


def _ssd_common(dt_ref, cs_ref, last_ref, b_ref, c_ref):
    c_sz = SSD_CHUNK
    lane = lax.broadcasted_iota(jnp.int32, (c_sz, LANES), 1)
    row = lax.broadcasted_iota(jnp.int32, (c_sz, LANES), 0)
    bb = b_ref[...].astype(BF16)
    cb = c_ref[...].astype(BF16)
    scores = _dot(cb, bb, "nt")
    return dict(lane=lane, row=row, dt=dt_ref[...], cs=cs_ref[...], last_row=last_ref[...], bb=bb, cb=cb,
                scores=scores, causal=row >= lane, lo=lane < SSD_P)


def _pair_terms(q, cst_ref, h0):
    lane, lo = q["lane"], q["lo"]
    out = {}
    cols, dts, lasts, lms = [], [], [], []
    lane1 = lax.broadcasted_iota(jnp.int32, (1, LANES), 1)
    for h in (h0, h0 + 1):
        col = _lane_col(q["cs"], lane, h)
        rowv = cst_ref[pl.ds(h, 1), :]
        lms.append(jnp.exp(jnp.where(q["causal"], col - rowv, -1e30)))
        cols.append(col)
        dts.append(_lane_col(q["dt"], lane, h))
        lasts.append(jnp.sum(jnp.where(lane1 == h, q["last_row"], 0.0), axis=1, keepdims=True))
    out["lm"] = lms
    out["cols"] = cols
    out["lasts"] = lasts
    out["dt_b"] = jnp.where(lo, dts[0], dts[1])
    out["e_b"] = jnp.where(lo, jnp.exp(cols[0]), jnp.exp(cols[1]))
    out["dec_cols"] = [jnp.exp(lasts[0] - cols[0]), jnp.exp(lasts[1] - cols[1])]
    out["dec_b"] = jnp.where(lo, out["dec_cols"][0], out["dec_cols"][1])
    lo1 = lane1 < SSD_P
    out["explast"] = [jnp.exp(lasts[0]), jnp.exp(lasts[1])]
    out["explast_b"] = jnp.where(lo1, out["explast"][0], out["explast"][1])
    return out


def _ssd_fwd(zx, xc, decay, d_lane, nw, d_inner, after, name):
    L = zx.shape[0]
    nc = L // SSD_CHUNK
    gw = d_inner // SSD_G
    heads = gw // SSD_P
    n_pair = heads // 2
    bc0 = d_inner // LANES

    def body(z_ref, xs_ref, b_ref, c_ref, dt_ref, cs_ref, cst_ref, last_ref, dl_ref, nw_ref, after_ref,
             y_ref, yn_ref, prev_ref, s_ref):
        @pl.when(pl.program_id(1) == 0)
        def _():
            s_ref[...] = jnp.zeros_like(s_ref)

        q = _ssd_common(dt_ref, cs_ref, last_ref, b_ref, c_ref)
        prev_ref[...] = s_ref[...]
        lo = q["lo"]
        for j in range(n_pair):
            sl = slice(j * LANES, (j + 1) * LANES)
            p = _pair_terms(q, cst_ref, pl.program_id(0) * heads + 2 * j)
            xs_p = xs_ref[:, sl]
            xp = xs_p * p["dt_b"]
            xb = xp.astype(BF16)
            m_a = (q["scores"] * p["lm"][0]).astype(BF16)
            m_b = (q["scores"] * p["lm"][1]).astype(BF16)
            yd = jnp.where(lo, _dot(m_a, xb), _dot(m_b, xb))
            s_p = s_ref[:, sl]
            yo = _dot(q["cb"], s_p.astype(BF16)) * p["e_b"]
            y_ref[:, sl] = yd + yo + dl_ref[:, sl] * xs_p
            st = _dot(q["bb"], (xp * p["dec_b"]).astype(BF16), "tn")
            s_ref[:, sl] = s_p * p["explast_b"] + st
        for r0 in range(0, SSD_CHUNK, NORM_ROWS):
            rows = slice(r0, r0 + NORM_ROWS)
            zv = z_ref[rows, :]
            yg = y_ref[rows, :] * (zv * _sigmoid(zv))
            rstd = lax.rsqrt(jnp.mean(yg * yg, axis=-1, keepdims=True) + NORM_EPS)
            yn_ref[rows, :] = (yg * rstd * nw_ref[...]).astype(BF16)

    grp = lambda width: pl.BlockSpec((None, 1, width), lambda g, c: (g, 0, 0))
    dt_, _, cs_, cst_, last_ = decay
    return _pcall(
        body, name=name, grid=(SSD_G, nc),
        out_shape=[jax.ShapeDtypeStruct((L, d_inner), F32), jax.ShapeDtypeStruct((L, d_inner), BF16),
                   jax.ShapeDtypeStruct((nc, SSD_G, SSD_N, gw), F32)],
        in_specs=[pl.BlockSpec((SSD_CHUNK, gw), lambda g, c: (c, g)),
                  pl.BlockSpec((SSD_CHUNK, gw), lambda g, c: (c, g)),
                  pl.BlockSpec((SSD_CHUNK, SSD_N), lambda g, c: (c, bc0 + g)),
                  pl.BlockSpec((SSD_CHUNK, SSD_N), lambda g, c: (c, bc0 + SSD_G + g)),
                  pl.BlockSpec((SSD_CHUNK, LANES), lambda g, c: (c, 0)),
                  pl.BlockSpec((SSD_CHUNK, LANES), lambda g, c: (c, 0)),
                  pl.BlockSpec((None, SSD_CHUNK, LANES), lambda g, c: (c, 0, 0)),
                  pl.BlockSpec((None, 1, LANES), lambda g, c: (c, 0, 0)),
                  grp(gw), grp(gw), pl.BlockSpec(memory_space=pl.ANY)],
        out_specs=[pl.BlockSpec((SSD_CHUNK, gw), lambda g, c: (c, g)),
                   pl.BlockSpec((SSD_CHUNK, gw), lambda g, c: (c, g)),
                   pl.BlockSpec((None, None, SSD_N, gw), lambda g, c: (c, g, 0, 0))],
        scratch_shapes=[pltpu.VMEM((SSD_N, gw), F32)],
        compiler_params=_cparams(("parallel", "arbitrary")))(
            zx, xc, xc, xc, dt_, cs_, cst_, last_, d_lane, nw, after)


def _ssd_bwd(dyn, y, zx, xc, prev, decay, alog_p, d_lane, nw, d_inner, name):
    L = zx.shape[0]
    nc = L // SSD_CHUNK
    gw = d_inner // SSD_G
    heads = gw // SSD_P
    n_pair = heads // 2
    bc0 = d_inner // LANES

    def body(dyn_ref, y_ref, z_ref, xs_ref, b_ref, c_ref, prev_ref, dt_ref, sg_ref, cs_ref, cst_ref, last_ref,
             alog_ref, dl_ref, nw_ref,
             dz_ref, dxs_ref, db_ref, dc_ref, ddt_ref, dbias_ref, dalog_ref, dd_ref, dnw_ref,
             ds_ref, racc_ref, dy_ref):
        @pl.when(pl.program_id(1) == 0)
        def _():
            ds_ref[...] = jnp.zeros_like(ds_ref)
            dbias_ref[...] = jnp.zeros_like(dbias_ref)
            dalog_ref[...] = jnp.zeros_like(dalog_ref)
            dd_ref[...] = jnp.zeros_like(dd_ref)
            dnw_ref[...] = jnp.zeros_like(dnw_ref)

        q = _ssd_common(dt_ref, cs_ref, last_ref, b_ref, c_ref)
        a_row = -jnp.exp(alog_ref[...])
        lane, row, lo = q["lane"], q["row"], q["lo"]
        lane1 = lax.broadcasted_iota(jnp.int32, (1, LANES), 1)
        head0 = pl.program_id(0) * heads
        mine = (lane >= head0) & (lane < head0 + heads)

        nwv = nw_ref[...]
        dnw_sum = jnp.zeros_like(nwv)
        for r0 in range(0, SSD_CHUNK, NORM_ROWS):
            rows = slice(r0, r0 + NORM_ROWS)
            yv, zv, dynv = y_ref[rows, :], z_ref[rows, :], dyn_ref[rows, :]
            sig = _sigmoid(zv)
            sz = zv * sig
            yg = yv * sz
            rstd = lax.rsqrt(jnp.mean(yg * yg, axis=-1, keepdims=True) + NORM_EPS)
            yhat = yg * rstd
            dnw_sum = dnw_sum + jnp.sum(dynv * yhat, axis=0, keepdims=True)
            dyh = dynv * nwv
            dyg = rstd * (dyh - yhat * jnp.mean(dyh * yhat, axis=-1, keepdims=True))
            dz_ref[rows, :] = (dyg * yv * (sig * (1.0 + zv * (1.0 - sig)))).astype(BF16)
            dy_ref[rows, :] = dyg * sz
        dnw_ref[...] += dnw_sum

        dg = jnp.zeros((SSD_CHUNK, SSD_CHUNK), F32)
        dc_acc = jnp.zeros((SSD_CHUNK, SSD_N), F32)
        db_acc = jnp.zeros((SSD_CHUNK, SSD_N), F32)
        dcs_mat = jnp.zeros((SSD_CHUNK, LANES), F32)
        ddt_mat = jnp.zeros((SSD_CHUNK, LANES), F32)
        dd_row = jnp.zeros((1, LANES), F32)
        racc_ref[...] = jnp.zeros_like(racc_ref)
        is_last = row == SSD_CHUNK - 1

        for j in range(n_pair):
            sl = slice(j * LANES, (j + 1) * LANES)
            ha, hb = head0 + 2 * j, head0 + 2 * j + 1
            p = _pair_terms(q, cst_ref, ha)
            xs_p = xs_ref[:, sl]
            dyp = dy_ref[:, sl]
            xp = xs_p * p["dt_b"]
            xb = xp.astype(BF16)
            s_p = prev_ref[:, sl]
            s_pb = s_p.astype(BF16)
            dsn = ds_ref[:, sl]
            dsnb = dsn.astype(BF16)
            m_f = [q["scores"] * p["lm"][0], q["scores"] * p["lm"][1]]

            t0 = dyp * xs_p
            dd_row = dd_row + jnp.where(lane1 == ha, _sum_all(jnp.where(lo, t0, 0.0)), 0.0) \
                + jnp.where(lane1 == hb, _sum_all(jnp.where(lo, 0.0, t0)), 0.0)
            dxs_p = dl_ref[:, sl] * dyp

            yo = _dot(q["cb"], s_pb) * p["e_b"]
            dcs_b = (dyp * p["e_b"]).astype(BF16)
            dc_acc = dc_acc + _dot(dcs_b, s_pb, "nt")
            ds_yo = _dot(q["cb"], dcs_b, "tn")
            t1 = dyp * yo
            dcs_cols = [jnp.sum(jnp.where(lo, t1, 0.0), axis=1, keepdims=True),
                        jnp.sum(jnp.where(lo, 0.0, t1), axis=1, keepdims=True)]

            t2 = dsn * s_p
            dlast = [p["explast"][0] * _sum_all(jnp.where(lo, t2, 0.0)),
                     p["explast"][1] * _sum_all(jnp.where(lo, 0.0, t2))]
            ds_ref[:, sl] = dsn * p["explast_b"] + ds_yo
            w = _dot(q["bb"], dsnb)
            db_acc = db_acc + _dot((xp * p["dec_b"]).astype(BF16), dsnb, "nt")
            dxp = w * p["dec_b"]
            t3 = w * xp
            e = [jnp.sum(jnp.where(lo, t3, 0.0), axis=1, keepdims=True) * p["dec_cols"][0],
                 jnp.sum(jnp.where(lo, 0.0, t3), axis=1, keepdims=True) * p["dec_cols"][1]]
            for i in range(2):
                dlast[i] = dlast[i] + jnp.sum(e[i], axis=0, keepdims=True)
                dcs_cols[i] = dcs_cols[i] - e[i]

            dyb = dyp.astype(BF16)
            dy_h = [jnp.where(lo, dyp, 0.0).astype(BF16), jnp.where(lo, 0.0, dyp).astype(BF16)]
            dms = [_dot(dy_h[0], xb, "nt"), _dot(dy_h[1], xb, "nt")]
            dxp = dxp + jnp.where(lo, _dot(m_f[0].astype(BF16), dyb, "tn"), _dot(m_f[1].astype(BF16), dyb, "tn"))
            for i, h in enumerate((ha, hb)):
                dg = dg + dms[i] * p["lm"][i]
                qm = dms[i] * m_f[i]
                dcs_cols[i] = dcs_cols[i] + jnp.sum(qm, axis=1, keepdims=True)
                racc_ref[pl.ds(h, 1), :] = jnp.sum(qm, axis=0, keepdims=True)

            dxs_ref[:, sl] = dxs_p + dxp * p["dt_b"]
            t4 = dxp * xs_p
            ddt_cols = [jnp.sum(jnp.where(lo, t4, 0.0), axis=1, keepdims=True),
                        jnp.sum(jnp.where(lo, 0.0, t4), axis=1, keepdims=True)]
            for i, h in enumerate((ha, hb)):
                sel = lane == h
                dcs_mat = dcs_mat + jnp.where(sel, dcs_cols[i], 0.0) + jnp.where(sel & is_last, dlast[i], 0.0)
                ddt_mat = ddt_mat + jnp.where(sel, ddt_cols[i], 0.0)

        dcs_mat = dcs_mat - racc_ref[...].T
        tri_t = (row <= lane).astype(BF16)
        da = _dot_exact01(tri_t, dcs_mat)
        ddt = ddt_mat + da * a_row
        dalog_ref[...] += jnp.sum(jnp.where(mine, da * q["dt"], 0.0), axis=0, keepdims=True) * a_row
        draw = jnp.where(mine, ddt * sg_ref[...], 0.0)
        ddt_ref[...] = draw
        dbias_ref[...] += jnp.sum(draw, axis=0, keepdims=True)
        dd_ref[...] += dd_row
        dgb = dg.astype(BF16)
        dc_ref[...] = dc_acc + _dot(dgb, q["bb"])
        db_ref[...] = db_acc + _dot(dgb, q["cb"], "tn")

    rev = lambda c: nc - 1 - c
    grp = lambda width: pl.BlockSpec((None, 1, width), lambda g, c: (g, 0, 0))
    blk = lambda width, off: pl.BlockSpec((SSD_CHUNK, width), lambda g, c: (rev(c), off + g))
    head_vec = pl.BlockSpec((1, LANES), lambda g, c: (0, 0))
    chunk_rows = pl.BlockSpec((SSD_CHUNK, LANES), lambda g, c: (rev(c), 0))
    dt_, sg_, cs_, cst_, last_ = decay
    return _pcall(
        body, name=name, grid=(SSD_G, nc),
        out_shape=[jax.ShapeDtypeStruct(zx.shape, BF16), jax.ShapeDtypeStruct((L, d_inner), F32),
                   jax.ShapeDtypeStruct((L, SSD_G * SSD_N), F32), jax.ShapeDtypeStruct((L, SSD_G * SSD_N), F32),
                   jax.ShapeDtypeStruct((SSD_G, L, LANES), F32),
                   jax.ShapeDtypeStruct((SSD_G, 1, LANES), F32), jax.ShapeDtypeStruct((SSD_G, 1, LANES), F32),
                   jax.ShapeDtypeStruct((SSD_G, 1, LANES), F32), jax.ShapeDtypeStruct((SSD_G, 1, gw), F32)],
        in_specs=[blk(gw, 0), blk(gw, 0), blk(gw, 0), blk(gw, 0), blk(SSD_N, bc0), blk(SSD_N, bc0 + SSD_G),
                  pl.BlockSpec((None, None, SSD_N, gw), lambda g, c: (rev(c), g, 0, 0)),
                  chunk_rows, chunk_rows, chunk_rows,
                  pl.BlockSpec((None, SSD_CHUNK, LANES), lambda g, c: (rev(c), 0, 0)),
                  pl.BlockSpec((None, 1, LANES), lambda g, c: (rev(c), 0, 0)),
                  head_vec, grp(gw), grp(gw)],
        out_specs=[blk(gw, 0), blk(gw, 0), blk(SSD_N, 0), blk(SSD_N, 0),
                   pl.BlockSpec((None, SSD_CHUNK, LANES), lambda g, c: (g, rev(c), 0)),
                   grp(LANES), grp(LANES), grp(LANES), grp(gw)],
        scratch_shapes=[pltpu.VMEM((SSD_N, gw), F32), pltpu.VMEM((SSD_CHUNK, LANES), F32),
                        pltpu.VMEM((SSD_CHUNK, gw), F32)],
        compiler_params=_cparams(("parallel", "arbitrary")))(
            dyn, y, zx, xc, xc, xc, prev, dt_, sg_, cs_, cst_, last_, alog_p, d_lane, nw)


def _cond_mod(c_pad, ada_w, ada_b_loc, after, name):
    depth, D, n = ada_w.shape
    rows = c_pad.shape[0]

    def body(c_ref, w_ref, b_ref, after_ref, mod_ref, cond_ref):
        cv = c_ref[...]
        cond = cv * _sigmoid(cv)
        cond_ref[...] = cond
        mod_ref[...] = _dot(cond.astype(BF16), w_ref[...].astype(BF16)) + b_ref[...]

    return _pcall(
        body, name=name, grid=(depth,),
        out_shape=[jax.ShapeDtypeStruct((depth, rows, n), F32), jax.ShapeDtypeStruct((rows, D), F32)],
        in_specs=[pl.BlockSpec((rows, D), lambda i: (0, 0)),
                  pl.BlockSpec((None, D, n), lambda i: (i, 0, 0)),
                  pl.BlockSpec((None, 1, n), lambda i: (i, 0, 0)),
                  pl.BlockSpec(memory_space=pl.ANY)],
        out_specs=[pl.BlockSpec((None, rows, n), lambda i: (i, 0, 0)),
                   pl.BlockSpec((rows, D), lambda i: (0, 0))],
        compiler_params=_cparams(("arbitrary",)))(c_pad, ada_w, ada_b_loc, after)


def _adamw_math(g, w, m, v):
    m_new = ADAM_B1 * m + (1.0 - ADAM_B1) * g
    v_new = ADAM_B2 * v + (1.0 - ADAM_B2) * (g * g)
    m_hat = m_new / (1.0 - ADAM_B1 ** ADAM_STEP)
    v_hat = v_new / (1.0 - ADAM_B2 ** ADAM_STEP)
    delta = -ADAM_LR * (m_hat / (jnp.sqrt(v_hat) + ADAM_EPS) + ADAM_WD * w)
    return delta, m_new, v_new


def _adamw_sum(parts, w, m, v, layer, name, prev=None, tr=None):
    depth, R, C = w.shape
    tr = _tile(R, tr if tr is not None else (512 if C <= 512 else 256))

    def body(p_ref, w_ref, m_ref, v_ref, *rest):
        g_ref, d_ref, mo_ref, vo_ref = rest[-4:]
        g = p_ref[0].astype(F32)
        for k in range(1, N_DEV):
            g = g + p_ref[k].astype(F32)
        d, mn, vn = _adamw_math(g, w_ref[...], m_ref[...], v_ref[...])
        g_ref[...] = g
        d_ref[...] = d
        mo_ref[...] = mn
        vo_ref[...] = vn

    blk = pl.BlockSpec((None, tr, C), lambda i: (layer, i, 0))
    prev = list(prev) if prev is not None else []
    return _pcall(
        body, name=name, grid=(R // tr,),
        out_shape=[jax.ShapeDtypeStruct((depth, R, C), F32)] * 4,
        in_specs=[pl.BlockSpec((N_DEV, tr, C), lambda i: (0, i, 0)), blk, blk, blk]
        + [pl.BlockSpec(memory_space=pl.ANY)] * len(prev),
        out_specs=[blk] * 4, input_output_aliases={4 + k: k for k in range(len(prev))},
        compiler_params=_cparams(("parallel",)))(parts, w, m, v, *prev)


def _adamw_small(parts, wmv, head_parts, head_wmv, loss_parts, name):
    n, nh = len(parts), len(head_parts)
    n_heads = head_wmv[0][0].shape[1] if nh else 0
    groups = head_parts[0].shape[1] if nh else 0
    d_model = loss_parts.shape[2]

    def body(*refs):
        p_refs, refs = refs[:n], refs[n:]
        wmv_refs, refs = refs[:3 * n], refs[3 * n:]
        hp_refs, refs = refs[:nh], refs[nh:]
        hwmv_refs, refs = refs[:3 * nh], refs[3 * nh:]
        loss_ref, refs = refs[0], refs[1:]
        outs, loss_out, head_scr = refs[:4 * (n + nh)], refs[4 * (n + nh)], refs[4 * (n + nh) + 1]

        def update(i, g, w_ref, m_ref, v_ref):
            res = (g,) + _adamw_math(g, w_ref[...], m_ref[...], v_ref[...])
            for o_ref, r in zip(outs[4 * i:4 * i + 4], res):
                o_ref[...] = r

        for i in range(n):
            g = p_refs[i][0]
            for k in range(1, N_DEV):
                g = g + p_refs[i][k]
            update(i, g, *wmv_refs[3 * i:3 * i + 3])
        for i in range(nh):
            g = None
            for k in range(N_DEV):
                for grp in range(groups):
                    g = hp_refs[i][k, grp] if g is None else g + hp_refs[i][k, grp]
            head_scr[...] = g
            update(n + i, head_scr[:, 0:n_heads], *hwmv_refs[3 * i:3 * i + 3])
        tot = loss_ref[0]
        for k in range(1, N_DEV):
            tot = tot + loss_ref[k]
        loss_out[...] = jnp.broadcast_to(_sum_all(tot) * (0.5 / d_model), loss_out.shape)

    operands = list(parts) + [a for t in wmv for a in t] + list(head_parts) + [a for t in head_wmv for a in t]
    operands.append(loss_parts)
    out_shape = [jax.ShapeDtypeStruct(t[0].shape, F32) for t in list(wmv) + list(head_wmv) for _ in range(4)]
    out_shape.append(jax.ShapeDtypeStruct((1, LANES), F32))
    vmem = pl.BlockSpec(memory_space=pltpu.VMEM)
    outs = _pcall(body, name=name, out_shape=out_shape, in_specs=[vmem] * len(operands),
                  out_specs=[vmem] * len(out_shape), scratch_shapes=[pltpu.VMEM((1, LANES), F32)],
                  compiler_params=_cparams())(*operands)
    return [outs[4 * i:4 * i + 4] for i in range(n + nh)], outs[-1]


def _ada_adamw(cond_pad, dmod_pad, w, m, v, name, tr=512):
    depth, D, n = w.shape
    rows = cond_pad.shape[0]
    tr = _tile(D, tr)

    def body(c_ref, dm_ref, w_ref, m_ref, v_ref, g_ref, d_ref, mo_ref, vo_ref):
        g = _dot(c_ref[...].astype(BF16), dm_ref[...].astype(BF16), "tn")
        d, mn, vn = _adamw_math(g, w_ref[...], m_ref[...], v_ref[...])
        g_ref[...] = g
        d_ref[...] = d
        mo_ref[...] = mn
        vo_ref[...] = vn

    blk = pl.BlockSpec((None, tr, n), lambda i, r: (i, r, 0))
    return _pcall(
        body, name=name, grid=(depth, D // tr),
        out_shape=[jax.ShapeDtypeStruct((depth, D, n), F32)] * 4,
        in_specs=[pl.BlockSpec((rows, tr), lambda i, r: (0, r)),
                  pl.BlockSpec((None, rows, n), lambda i, r: (i, 0, 0)), blk, blk, blk],
        out_specs=[blk] * 4, compiler_params=_cparams(("parallel", "parallel")))(cond_pad, dmod_pad, w, m, v)


def kernel(x, c, ada_w, ada_b, mix_norm_w, mlp_norm_w, mlp_up, mlp_down, ssd_in_w, ssd_conv_w, ssd_conv_b, ssd_dt_bias, ssd_A_log, ssd_D, ssd_norm_w, ssd_out_w, sc_in_w, sc_conv_w, sc_out_w, final_norm_w, loss_target, m_ada_w, m_ada_b, m_mix_norm_w, m_mlp_norm_w, m_mlp_up, m_mlp_down, m_ssd_in_w, m_ssd_conv_w, m_ssd_conv_b, m_ssd_dt_bias, m_ssd_A_log, m_ssd_D, m_ssd_norm_w, m_ssd_out_w, m_sc_in_w, m_sc_conv_w, m_sc_out_w, m_final_norm_w, v_ada_w, v_ada_b, v_mix_norm_w, v_mlp_norm_w, v_mlp_up, v_mlp_down, v_ssd_in_w, v_ssd_conv_w, v_ssd_conv_b, v_ssd_dt_bias, v_ssd_A_log, v_ssd_D, v_ssd_norm_w, v_ssd_out_w, v_sc_in_w, v_sc_conv_w, v_sc_out_w, v_final_norm_w):
    weights = dict(ada_w=ada_w, ada_b=ada_b, mix_norm_w=mix_norm_w, mlp_norm_w=mlp_norm_w, mlp_up=mlp_up,
                   mlp_down=mlp_down, ssd_in_w=ssd_in_w, ssd_conv_w=ssd_conv_w, ssd_conv_b=ssd_conv_b,
                   ssd_dt_bias=ssd_dt_bias, ssd_A_log=ssd_A_log, ssd_D=ssd_D, ssd_norm_w=ssd_norm_w,
                   ssd_out_w=ssd_out_w, sc_in_w=sc_in_w, sc_conv_w=sc_conv_w, sc_out_w=sc_out_w,
                   final_norm_w=final_norm_w)
    moms = dict(ada_w=m_ada_w, ada_b=m_ada_b, mix_norm_w=m_mix_norm_w, mlp_norm_w=m_mlp_norm_w, mlp_up=m_mlp_up,
                mlp_down=m_mlp_down, ssd_in_w=m_ssd_in_w, ssd_conv_w=m_ssd_conv_w, ssd_conv_b=m_ssd_conv_b,
                ssd_dt_bias=m_ssd_dt_bias, ssd_A_log=m_ssd_A_log, ssd_D=m_ssd_D, ssd_norm_w=m_ssd_norm_w,
                ssd_out_w=m_ssd_out_w, sc_in_w=m_sc_in_w, sc_conv_w=m_sc_conv_w, sc_out_w=m_sc_out_w,
                final_norm_w=m_final_norm_w)
    vars_ = dict(ada_w=v_ada_w, ada_b=v_ada_b, mix_norm_w=v_mix_norm_w, mlp_norm_w=v_mlp_norm_w, mlp_up=v_mlp_up,
                 mlp_down=v_mlp_down, ssd_in_w=v_ssd_in_w, ssd_conv_w=v_ssd_conv_w, ssd_conv_b=v_ssd_conv_b,
                 ssd_dt_bias=v_ssd_dt_bias, ssd_A_log=v_ssd_A_log, ssd_D=v_ssd_D, ssd_norm_w=v_ssd_norm_w,
                 ssd_out_w=v_ssd_out_w, sc_in_w=v_sc_in_w, sc_conv_w=v_sc_conv_w, sc_out_w=v_sc_out_w,
                 final_norm_w=v_final_norm_w)
    names = list(weights)

    L, D = x.shape[1], x.shape[2]
    d_inner = 2 * D
    n_heads = d_inner // SSD_P
    hpg = n_heads // SSD_G
    gw = d_inner // SSD_G
    conv_dim = d_inner + 2 * SSD_G * SSD_N
    zx_dim = d_inner + conv_dim
    zx_pad = -(-(zx_dim + LANES) // 512) * 512
    in_ws = ssd_in_w.shape[2]
    in_base, in_off, in_win = _window_geometry(in_ws)
    me = _my_index()
    x0 = x[0]
    tgt = loss_target[0]

    n_mod = ada_w.shape[2]
    (c_all,) = _exchange([c], "gather_c", gather=True)
    gather_handle = {}
    (gather_handle["ssd_in_w"],), token_in = _xfer_start(
        [ssd_in_w[0].astype(BF16)], "gather_start_ssd_in_w", gather=True, via_sibling=(0,), after=(c_all,))
    c_pad = jnp.pad(c_all.reshape(N_DEV, D), ((0, 16 - N_DEV), (0, 0)))
    ada_b_loc = lax.dynamic_slice_in_dim(ada_b, me * n_mod, n_mod, axis=1).reshape(2, 1, n_mod)
    mod_blk, cond_pad = _cond_mod(c_pad, ada_w, ada_b_loc, token_in, "cond_mod")
    gather_order = ["mod", "ssd_conv_w", "sc_conv_w", "ssd_out_w", "up0", "down0", "sc_in_w", "sc_out_w", "up1",
                    "down1"]
    gather_src = dict(mod=mod_blk, ssd_conv_w=ssd_conv_w[0], sc_conv_w=sc_conv_w[0],
                      ssd_out_w=ssd_out_w[0].astype(BF16),
                      up0=mlp_up[0].astype(BF16), down0=mlp_down[0].astype(BF16),
                      sc_in_w=sc_in_w[0].astype(BF16), sc_out_w=sc_out_w[0].astype(BF16),
                      up1=mlp_up[1].astype(BF16), down1=mlp_down[1].astype(BF16))
    handles, gather_token = _xfer_start([gather_src[k] for k in gather_order], "gather_start", gather=True,
                                        via_sibling=tuple(range(3, len(gather_order))))
    gather_handle.update(zip(gather_order, handles))

    def gathered(keys, after, forward):
        tag = "_".join(keys)
        lands = _xfer_wait([gather_handle[k] for k in keys], after, f"gather_wait_{tag}", gather=True)
        return _sibling_forward(lands, f"gather_forward_{tag}") if forward else lands

    def forward_behind(keys, after):
        tag = "_".join(keys)
        lands = _xfer_wait([gather_handle[k] for k in keys], after, f"gather_wait_{tag}", gather=True)
        fwd_handles, token = _sibling_forward_start(lands, f"gather_forward_start_{tag}")
        return (lambda done: _sibling_forward_wait(fwd_handles, done, f"gather_forward_wait_{tag}")), token

    (ssd_in_g,) = gathered(["ssd_in_w"], (gather_token, m_ssd_in_w, v_ssd_in_w), True)
    w_in_all = _shards_to_columns(ssd_in_g, in_base, in_off, in_win, zx_pad, "ssd_in_w_columns")
    (mod_all,) = gathered(["mod"], w_in_all, False)
    mod_mine = lax.dynamic_index_in_dim(mod_all, me, axis=2, keepdims=False)
    mod_mine = jnp.transpose(mod_mine, (1, 0, 2)).reshape(2, 6, 1, D)
    sh_m, sc_m, g_m, sh_f, sc_f, g_f = [[mod_mine[i, k] for i in range(2)] for k in range(6)]

    vec = lambda a: a.reshape(1, -1)
    small = {}

    _, h0 = _norm_mod_fwd(x0, None, None, vec(mix_norm_w[0]), sc_m[0], sh_m[0], "l0_mix_norm")
    cw_all, scw_all = gathered(["ssd_conv_w", "sc_conv_w"], h0, False)
    (zx,) = _mm_nn(h0, w_in_all, F32, "ssd_in_proj", tm=2048, tn=512)
    conv_b0 = vec(ssd_conv_b[0])
    conv_w_full = jnp.transpose(cw_all, (1, 0, 2)).reshape(SSD_K, conv_dim)
    sc_conv_full = jnp.transpose(scw_all, (1, 0, 2)).reshape(SC_K, D)
    xc = _ssd_conv_fwd(zx, conv_w_full, conv_b0, d_inner, conv_dim, "ssd_conv")
    bias_p = jnp.pad(ssd_dt_bias[0], (0, LANES - n_heads)).reshape(1, LANES)
    alog_p = jnp.pad(ssd_A_log[0], (0, LANES - n_heads)).reshape(1, LANES)
    d_lane = jnp.repeat(ssd_D[0], SSD_P).reshape(SSD_G, 1, gw)
    nw_g = ssd_norm_w[0].reshape(SSD_G, 1, gw)
    finish, token = forward_behind(["ssd_out_w"], xc)
    decay = _ssd_decay(zx, bias_p, alog_p, n_heads, zx_dim // LANES, "ssd_decay")
    y_ssd, yn, prev = _ssd_fwd(zx, xc, decay, d_lane, nw_g, d_inner, token, "ssd_scan")
    ups, downs = [None, None], [None, None]
    (ssd_out_g,) = finish(yn)
    w_ssd_out = ssd_out_g.reshape(-1, D)
    finish, token = forward_behind(["up0", "down0"], ssd_out_g)
    (mix0,) = _mm_nn(yn, w_ssd_out, F32, "ssd_out_proj", tm=1024, tk=2048, after=(token,))
    x1, h1 = _norm_mod_fwd(x0, mix0, g_m[0], vec(mlp_norm_w[0]), sc_f[0], sh_f[0], "l0_mlp_norm")
    ups[0], down0_g = finish(h1)
    downs[0] = down0_g.reshape(-1, D)
    u0, s0 = _mm_nn_blocked(h1, ups[0], "l0_mlp_up", _ep_relu2, [BF16, BF16])
    finish, token = forward_behind(["sc_in_w", "sc_out_w", "up1", "down1"], s0)
    (d0,) = _mm_nn(s0, downs[0], F32, "l0_mlp_down", tm=1024, tk=2048, after=(token,))
    x2, h2 = _norm_mod_fwd(x1, d0, g_f[0], vec(mix_norm_w[1]), sc_m[1], sh_m[1], "l1_mix_norm")
    sc_in_g, sc_out_g, ups[1], down1_g = finish(h2)
    w_sc_out, downs[1] = sc_out_g.reshape(-1, D), down1_g.reshape(-1, D)
    (proj,) = _mm_nn_blocked(h2, sc_in_g, "sc_in_proj", _ep_store(F32), [F32])
    yc = _sc_conv_fwd(proj, sc_conv_full, "sc_conv")
    (mix1,) = _mm_nn(yc, w_sc_out, F32, "sc_out_proj", tm=1024)
    x3, h3 = _norm_mod_fwd(x2, mix1, g_m[1], vec(mlp_norm_w[1]), sc_f[1], sh_f[1], "l1_mlp_norm")
    u1, s1 = _mm_nn_blocked(h3, ups[1], "l1_mlp_up", _ep_relu2, [BF16, BF16])
    (d1,) = _mm_nn(s1, downs[1], F32, "l1_mlp_down", tm=1024, tk=2048)

    dx, loss_lane, dfw, dd1, dg = _final_loss(x3, d1, g_f[1], vec(final_norm_w), tgt, "final_loss")
    small["final_norm_w"] = dfw

    dmod = [[None] * 6 for _ in range(2)]
    dmod[1][5] = dg

    def mlp_backward(i, dx_out, dd, x_mid, h_in, u, s, mix, gate):
        du = _mm_nt(dd, downs[i], BF16, f"l{i}_mlp_down_bwd", tm=1024, epilogue=_ep_relu2_bwd, extra=(u,))
        gdown = _mm_tn(s, dd, BF16, f"l{i}_mlp_down_wgrad").reshape(N_DEV, -1, D)
        gup = _mm_tn_blocked(h_in, du, BF16, f"l{i}_mlp_up_wgrad")
        (h_down, h_up), token = _xfer_start([gdown, gup], f"l{i}_mlp_grads_start", gather=False)
        grad_handle[f"mlp_down{i}"], grad_handle[f"mlp_up{i}"] = h_down, h_up
        dh = _mm_nt_blocked(du, ups[i], F32, f"l{i}_mlp_up_bwd", after=(token,))
        dxm, dsh, dsc, dnw, dmix, dgate = _norm_mod_bwd(dh, x_mid, vec(mlp_norm_w[i]), sc_f[i], dx_out,
                                                        f"l{i}_mlp_norm_bwd", branch=(mix, gate))
        dmod[i][3], dmod[i][4], dmod[i][2] = dsh, dsc, dgate
        return dxm, dmix, dnw

    grad_handle = {}
    dx3, dyc, dnw_mlp1 = mlp_backward(1, dx, dd1, x3, h3, u1, s1, mix1, g_m[1])
    g_sc_out = _mm_tn(yc, dyc, BF16, "sc_out_wgrad").reshape(N_DEV, -1, D)
    dconv_out = _mm_nt(dyc, w_sc_out, F32, "sc_out_bwd", tm=1024)
    dbg, dcg, dxv, dscw = _sc_conv_bwd(proj, sc_conv_full, dconv_out, "sc_conv_bwd")
    dproj = jnp.concatenate([dbg, dcg, dxv], axis=1)
    g_sc_in = _mm_tn_blocked(h2, dproj, BF16, "sc_in_wgrad")
    (grad_handle["sc_out_w0"], grad_handle["sc_in_w0"]), token = _xfer_start(
        [g_sc_out, g_sc_in], "sc_grads_start", gather=False)
    dh2 = _mm_nt_blocked(dproj, sc_in_g, F32, "sc_in_bwd", after=(token,))
    dx2, dsh, dsc, dnw_mix1, dd0, dg = _norm_mod_bwd(dh2, x2, vec(mix_norm_w[1]), sc_m[1], dx3, "l1_mix_norm_bwd",
                                                     branch=(d0, g_f[0]))
    dmod[1][0], dmod[1][1], dmod[0][5] = dsh, dsc, dg
    dx1, dyo, dnw_mlp0 = mlp_backward(0, dx2, dd0, x1, h1, u0, s0, mix0, g_m[0])
    g_ssd_out = _mm_tn(yn, dyo, BF16, "ssd_out_wgrad").reshape(N_DEV, -1, D)
    (grad_handle["ssd_out_w0"],), token = _xfer_start([g_ssd_out], "ssd_out_grad_start", gather=False)
    dyn = _mm_nt(dyo, w_ssd_out, F32, "ssd_out_bwd", after=(token,))
    dz, dxs, db_, dc_, ddt, dbias, dalog, dd_, dnw_ssd = _ssd_bwd(
        dyn, y_ssd, zx, xc, prev, decay, alog_p, d_lane, nw_g, d_inner, "ssd_scan_bwd")
    dzx, dcw, dcb = _ssd_conv_bwd(zx, conv_w_full, conv_b0, [dxs, db_, dc_], dz, d_inner, "ssd_conv_bwd")
    dzx = _dzx_finish(dzx, ddt, zx_dim, "ssd_dzx_finish")
    g_in_all = _mm_tn(h0, dzx, BF16, "ssd_in_wgrad", tn=512, tk=2048)
    g_ssd_in = _columns_to_shards(g_in_all, in_ws, in_base, in_off, in_win, "ssd_in_wgrad_shards")
    (grad_handle["ssd_in_w0"],), token = _xfer_start([g_ssd_in], "ssd_in_grad_start", gather=False)
    dh0 = _mm_nt(dzx, w_in_all, F32, "ssd_in_bwd", tm=1024, tk=dzx.shape[1] // 2, after=(token,))
    grad_x, dsh, dsc, dnw_mix0 = _norm_mod_bwd(dh0, x0, vec(mix_norm_w[0]), sc_m[0], dx1, "l0_mix_norm_bwd")
    dmod[0][0], dmod[0][1] = dsh, dsc

    small["ada_b"] = jnp.concatenate([jnp.concatenate(dmod[i], axis=1) for i in range(2)], axis=0)
    small["mix_norm_w"] = jnp.concatenate([dnw_mix0, dnw_mix1], axis=0)
    small["mlp_norm_w"] = jnp.concatenate([dnw_mlp0, dnw_mlp1], axis=0)
    small["ssd_conv_w"] = dcw
    small["ssd_conv_b"] = dcb
    small["ssd_norm_w"] = dnw_ssd.reshape(1, d_inner)
    small["sc_conv_w"] = dscw
    small["loss"] = loss_lane
    small_names = list(small)
    head_names = ["ssd_dt_bias", "ssd_A_log", "ssd_D"]
    handles, small_token = _xfer_start([small[k] for k in small_names] + [dbias, dalog, dd_],
                                       "small_grads_start", gather=True)

    out_g, out_d, out_m, out_v = {}, {}, {}, {}

    layer_res = {}

    def big_update(name, i, after):
        (parts,) = _xfer_wait([grad_handle[f"{name}{i}"]], after, f"grads_wait_{name}_{i}", gather=False)
        res = _adamw_sum(parts, weights[name], moms[name], vars_[name], i, f"adamw_{name}_{i}",
                         prev=layer_res.get(name))
        layer_res[name] = res
        return res[1]

    chain = small_token
    for name, i in [("mlp_down", 1), ("mlp_up", 1), ("sc_out_w", 0), ("sc_in_w", 0), ("mlp_down", 0),
                    ("mlp_up", 0), ("ssd_out_w", 0), ("ssd_in_w", 0)]:
        chain = big_update(name, i, chain)
    gathered_small = _xfer_wait(handles, chain, "small_grads_wait", gather=True)
    small_all = dict(zip(small_names + head_names, gathered_small))

    dmod_loc = lax.dynamic_slice_in_dim(small_all["ada_b"], me * n_mod, n_mod, axis=2)
    dmod_pad = jnp.pad(jnp.transpose(dmod_loc, (1, 0, 2)), ((0, 0), (0, 16 - N_DEV), (0, 0)))
    out_g["ada_w"], out_d["ada_w"], out_m["ada_w"], out_v["ada_w"] = _ada_adamw(
        cond_pad, dmod_pad, ada_w, m_ada_w, v_ada_w, "adamw_ada_w")

    for k in ("ssd_conv_w", "sc_conv_w"):
        n_loc = weights[k].shape[2]
        small_all[k] = lax.dynamic_slice_in_dim(small_all[k], me * n_loc, n_loc, axis=2)
    plain = [k for k in small_names if k != "loss"]
    as2d = lambda a: a.reshape(-1, a.shape[-1])
    res, loss_row = _adamw_small(
        [small_all[k] for k in plain], [tuple(as2d(d[k]) for d in (weights, moms, vars_)) for k in plain],
        [small_all[k] for k in head_names], [tuple(as2d(d[k]) for d in (weights, moms, vars_)) for k in head_names],
        small_all["loss"], "adamw_small")
    loss = loss_row[0, 0]
    for k, res4 in zip(plain + head_names, res):
        for r, dst in zip(res4, (out_g, out_d, out_m, out_v)):
            dst[k] = r.reshape(weights[k].shape)
    for name, res4 in layer_res.items():
        for r, dst in zip(res4, (out_g, out_d, out_m, out_v)):
            dst[name] = r

    return (loss, grad_x[None], *[out_g[k] for k in names], *[out_d[k] for k in names],
            *[out_m[k] for k in names], *[out_v[k] for k in names])
```
